```python
import jax, jax.numpy as jnp
from jax import lax
import numpy as np

D_MODEL = 1024
BATCH = 8
SEQ = 8192
DEPTH = 4

N_MIXERS = 2
RMS_EPS = 1e-6
L2_EPS = 1e-6

GDN_HEADS = 8
GDN_DK = 128
GDN_DV = 128
GDN_CONV = 4
GDN_CHUNK = 64
GDN_QK_WIDTH = GDN_HEADS * GDN_DK
GDN_V_WIDTH = GDN_HEADS * GDN_DV
GDN_IN_WIDTH = 2 * GDN_QK_WIDTH + 2 * GDN_V_WIDTH + 2 * GDN_HEADS

DIL_GROUPS = ((128, 1), (512, 4), (2048, 16))
N_DIL_GROUPS = 3
DIL_HEADS_PER_GROUP = 8
DIL_HEAD_DIM = 64
DIL_TOTAL_HEADS = N_DIL_GROUPS * DIL_HEADS_PER_GROUP
DIL_IN_WIDTH = 3 * DIL_TOTAL_HEADS * DIL_HEAD_DIM
DIL_OUT_WIDTH = DIL_HEADS_PER_GROUP * DIL_HEAD_DIM
ALIBI_MAX_BIAS = 8.0

FFN_HIDDEN = -(-8 * D_MODEL // (3 * 256)) * 256

kernel_name = "hybrid_gdn_dilated_swa_swiglu"


def _rmsnorm(x, w):
    xf = x.astype(jnp.float32)
    y = xf * lax.rsqrt(jnp.mean(xf * xf, axis=-1, keepdims=True) + RMS_EPS)
    return (y * w.astype(jnp.float32)).astype(x.dtype)


def _l2norm(x):
    xf = x.astype(jnp.float32)
    return xf * lax.rsqrt(jnp.sum(xf * xf, axis=-1, keepdims=True) + L2_EPS)


def _causal_depthwise_conv(x, w):
    c = x.shape[-1]
    return lax.conv_general_dilated(
        x, w[:, None, :].astype(x.dtype), window_strides=(1,),
        padding=[(w.shape[0] - 1, 0)], dimension_numbers=("NWC", "WIO", "NWC"),
        feature_group_count=c)


def _chunk_gated_delta_rule(q, k, v, g, beta):
    b, s, h, dk = q.shape
    dv = v.shape[-1]
    c = GDN_CHUNK
    nc = s // c

    def chunks(t):
        return t.reshape(b, nc, c, h, -1).transpose(0, 3, 1, 2, 4)

    q, k, v = chunks(q), chunks(k), chunks(v)
    g = g.reshape(b, nc, c, h).transpose(0, 3, 1, 2)
    beta = beta.reshape(b, nc, c, h).transpose(0, 3, 1, 2)
    gc = jnp.cumsum(g, axis=-1)
    idx = jnp.arange(c)
    causal = idx[:, None] >= idx[None, :]
    strict = idx[:, None] > idx[None, :]
    decay = jnp.exp(jnp.where(causal, gc[..., :, None] - gc[..., None, :], -jnp.inf))
    kk = jnp.einsum("bhncd,bhnmd->bhncm", k, k)
    a_mat = jnp.where(strict, kk * beta[..., :, None] * decay, 0.0) + jnp.eye(c, dtype=jnp.float32)
    rhs = jnp.concatenate([v * beta[..., None], k * (beta * jnp.exp(gc))[..., None]], axis=-1)
    sol = lax.linalg.triangular_solve(a_mat, rhs, left_side=True, lower=True, unit_diagonal=True)
    u, w = sol[..., :dv], sol[..., dv:]
    qk = jnp.einsum("bhncd,bhnmd->bhncm", q, k) * decay
    q_dec = q * jnp.exp(gc)[..., None]
    k_tail = k * jnp.exp(gc[..., -1:] - gc)[..., None]
    c_dec = jnp.exp(gc[..., -1])
    xs = (jnp.moveaxis(qk, 2, 0), jnp.moveaxis(q_dec, 2, 0), jnp.moveaxis(k_tail, 2, 0),
          jnp.moveaxis(u, 2, 0), jnp.moveaxis(w, 2, 0), jnp.moveaxis(c_dec, 2, 0))

    def step(state, inp):
        qk_c, qd_c, kt_c, u_c, w_c, cd_c = inp
        v_new = u_c - jnp.einsum("bhck,bhkv->bhcv", w_c, state)
        o_c = (jnp.einsum("bhck,bhkv->bhcv", qd_c, state)
               + jnp.einsum("bhcm,bhmv->bhcv", qk_c, v_new))
        state = state * cd_c[..., None, None] + jnp.einsum("bhck,bhcv->bhkv", kt_c, v_new)
        return state, o_c

    s0 = jnp.zeros((b, h, dk, dv), jnp.float32)
    _, o = lax.scan(step, s0, xs)
    return o.transpose(1, 0, 3, 2, 4).reshape(b, s, h, dv)


def _gated_deltanet(h, w_in, conv_w, a_log, dt_bias, norm_w, w_out):
    b, s, _ = h.shape
    proj = h @ w_in
    n_qkv = 2 * GDN_QK_WIDTH + GDN_V_WIDTH
    qkv = jax.nn.silu(_causal_depthwise_conv(proj[..., :n_qkv], conv_w))
    z = proj[..., n_qkv:n_qkv + GDN_V_WIDTH].reshape(b, s, GDN_HEADS, GDN_DV)
    a = proj[..., n_qkv + GDN_V_WIDTH:n_qkv + GDN_V_WIDTH + GDN_HEADS].astype(jnp.float32)
    bb = proj[..., n_qkv + GDN_V_WIDTH + GDN_HEADS:].astype(jnp.float32)
    q = _l2norm(qkv[..., :GDN_QK_WIDTH].reshape(b, s, GDN_HEADS, GDN_DK)) * (GDN_DK ** -0.5)
    k = _l2norm(qkv[..., GDN_QK_WIDTH:2 * GDN_QK_WIDTH].reshape(b, s, GDN_HEADS, GDN_DK))
    v = qkv[..., 2 * GDN_QK_WIDTH:].reshape(b, s, GDN_HEADS, GDN_DV).astype(jnp.float32)
    beta = jax.nn.sigmoid(bb)
    g = -jnp.exp(a_log.astype(jnp.float32)) * jax.nn.softplus(a + dt_bias.astype(jnp.float32))
    o = _chunk_gated_delta_rule(q, k, v, g, beta)
    o = o * lax.rsqrt(jnp.mean(o * o, axis=-1, keepdims=True) + RMS_EPS)
    o = o * norm_w.astype(jnp.float32) * jax.nn.silu(z.astype(jnp.float32))
    return o.astype(h.dtype).reshape(b, s, GDN_V_WIDTH) @ w_out


def _dilated_group_attention(q, k, v, slopes, dilation, span):
    b, s, h, dh = q.shape
    n_sub = -(-s // dilation)
    nb = -(-n_sub // span)
    l_pad = nb * span
    t_pad = l_pad * dilation

    def to_blocks(t):
        t = jnp.pad(t, ((0, 0), (0, t_pad - s), (0, 0), (0, 0)))
        t = t.reshape(b, l_pad, dilation, h, dh).transpose(0, 2, 1, 3, 4)
        return t.reshape(b, dilation, nb, span, h, dh)

    def with_prev(t):
        prev = jnp.pad(t, ((0, 0), (0, 0), (1, 0), (0, 0), (0, 0), (0, 0)))[:, :, :-1]
        return jnp.concatenate([prev, t], axis=3)

    def from_blocks(t):
        e = t.shape[-1]
        t = t.reshape(b, dilation, l_pad, h, e).transpose(0, 2, 1, 3, 4)
        return t.reshape(b, t_pad, h, e)[:, :s]

    qb = to_blocks(q)
    kb = with_prev(to_blocks(k))
    vb = with_prev(to_blocks(v))
    scores = jnp.einsum("brnqhd,brnkhd->brnhqk", qb, kb,
                        preferred_element_type=jnp.float32) * (dh ** -0.5)
    qi = jnp.arange(span)
    kj = jnp.arange(2 * span)
    steps = qi[:, None] + span - kj[None, :]
    key_sub = jnp.arange(nb)[:, None] * span - span + kj[None, :]
    valid = ((steps >= 0) & (steps <= span))[None] & (key_sub >= 0)[:, None, :]
    bias = -slopes.astype(jnp.float32)[:, None, None] * (steps * dilation).astype(jnp.float32)
    logits = jnp.where(valid[:, None], scores + bias, -jnp.inf)
    m = jnp.max(logits, axis=-1, keepdims=True)
    p = jnp.exp(logits - m)
    l = jnp.sum(p, axis=-1)
    o = jnp.einsum("brnhqk,brnkhd->brnqhd", p, vb.astype(jnp.float32))
    o = o / l.transpose(0, 1, 2, 4, 3)[..., None]
    m_out = from_blocks(m[..., 0].transpose(0, 1, 2, 4, 3)[..., None])[..., 0]
    l_out = from_blocks(l.transpose(0, 1, 2, 4, 3)[..., None])[..., 0]
    return from_blocks(o), m_out, l_out


def _dilated_attention(h, w_in, q_norm, k_norm, w_out):
    b, s, _ = h.shape
    qkv = (h @ w_in).reshape(b, s, 3, N_DIL_GROUPS, DIL_HEADS_PER_GROUP, DIL_HEAD_DIM)
    q = _rmsnorm(qkv[:, :, 0], q_norm)
    k = _rmsnorm(qkv[:, :, 1], k_norm)
    v = qkv[:, :, 2]
    slopes = (2.0 ** (-ALIBI_MAX_BIAS * jnp.arange(1, DIL_TOTAL_HEADS + 1, dtype=jnp.float32)
                      / DIL_TOTAL_HEADS)).reshape(N_DIL_GROUPS, DIL_HEADS_PER_GROUP)
    outs, maxes, dens = [], [], []
    for gi, (window, dilation) in enumerate(DIL_GROUPS):
        o_g, m_g, l_g = _dilated_group_attention(q[:, :, gi], k[:, :, gi], v[:, :, gi],
                                                 slopes[gi], dilation, window // dilation)
        outs.append(o_g)
        maxes.append(m_g)
        dens.append(l_g)
    o_all = jnp.stack(outs)
    m_all = jnp.stack(maxes)
    l_all = jnp.stack(dens)
    wts = l_all * jnp.exp(m_all - jnp.max(m_all, axis=0, keepdims=True))
    o = jnp.sum(wts[..., None] * o_all, axis=0) / jnp.sum(wts, axis=0)[..., None]
    return o.astype(h.dtype).reshape(b, s, DIL_OUT_WIDTH) @ w_out


def _swiglu(h, w_in, w_out):
    gu = h @ w_in
    return (jax.nn.silu(gu[..., :FFN_HIDDEN]) * gu[..., FFN_HIDDEN:]) @ w_out


def _fwd_setup_inputs(seed: int = 0) -> dict:
    key = jax.random.key(seed)
    ks = jax.random.split(key, 16)
    f32 = jnp.float32
    n_a = (DEPTH + 1) // 2
    n_b = DEPTH // 2
    out_scale = (2.0 * DEPTH) ** -0.5

    def nrm(k, shape, scale):
        return jax.random.normal(k, shape, f32) * scale

    def gain(k, shape):
        return 1.0 + 0.02 * jax.random.normal(k, shape, f32)

    dt = jnp.exp(jax.random.uniform(ks[6], (n_a, GDN_HEADS), f32, np.log(1e-3), np.log(1e-1)))
    return {
        "x": nrm(ks[0], (BATCH, SEQ, D_MODEL), 1.0),
        "norm_mix": gain(ks[1], (DEPTH, D_MODEL)),
        "norm_ffn": gain(ks[2], (DEPTH, D_MODEL)),
        "gdn_w_in": nrm(ks[3], (n_a, D_MODEL, GDN_IN_WIDTH), D_MODEL ** -0.5),
        "gdn_conv_w": nrm(ks[4], (n_a, GDN_CONV, 2 * GDN_QK_WIDTH + GDN_V_WIDTH), GDN_CONV ** -0.5),
        "gdn_a_log": jnp.log(jax.random.uniform(ks[5], (n_a, GDN_HEADS), f32, 1.0, 16.0)),
        "gdn_dt_bias": dt + jnp.log(-jnp.expm1(-dt)),
        "gdn_norm_w": gain(ks[7], (n_a, GDN_DV)),
        "gdn_w_out": nrm(ks[8], (n_a, GDN_V_WIDTH, D_MODEL), GDN_V_WIDTH ** -0.5 * out_scale),
        "dil_w_in": nrm(ks[9], (n_b, D_MODEL, DIL_IN_WIDTH), D_MODEL ** -0.5),
        "dil_q_norm": gain(ks[10], (n_b, DIL_HEAD_DIM)),
        "dil_k_norm": gain(ks[11], (n_b, DIL_HEAD_DIM)),
        "dil_w_out": nrm(ks[12], (n_b, DIL_OUT_WIDTH, D_MODEL), DIL_OUT_WIDTH ** -0.5 * out_scale),
        "ffn_w_in": nrm(ks[13], (DEPTH, D_MODEL, 2 * FFN_HIDDEN), D_MODEL ** -0.5),
        "ffn_w_out": nrm(ks[14], (DEPTH, FFN_HIDDEN, D_MODEL), FFN_HIDDEN ** -0.5 * out_scale),
    }


def _fwd_reference(x, norm_mix, norm_ffn, gdn_w_in, gdn_conv_w, gdn_a_log, gdn_dt_bias, gdn_norm_w,
              gdn_w_out, dil_w_in, dil_q_norm, dil_k_norm, dil_w_out, ffn_w_in, ffn_w_out):
    for i in range(DEPTH):
        j = i // N_MIXERS
        hn = _rmsnorm(x, norm_mix[i])
        if i % N_MIXERS == 0:
            x = x + _gated_deltanet(hn, gdn_w_in[j], gdn_conv_w[j], gdn_a_log[j], gdn_dt_bias[j],
                                    gdn_norm_w[j], gdn_w_out[j])
        else:
            x = x + _dilated_attention(hn, dil_w_in[j], dil_q_norm[j], dil_k_norm[j], dil_w_out[j])
        x = x + _swiglu(_rmsnorm(x, norm_ffn[i]), ffn_w_in[i], ffn_w_out[i])
    return x


import jax as _jax
import jax.numpy as _jnp

TWIN_FORMAT = 'train_step'
FWD_PARAMS = ['x', 'norm_mix', 'norm_ffn', 'gdn_w_in', 'gdn_conv_w', 'gdn_a_log', 'gdn_dt_bias', 'gdn_norm_w', 'gdn_w_out', 'dil_w_in', 'dil_q_norm', 'dil_k_norm', 'dil_w_out', 'ffn_w_in', 'ffn_w_out']
TWIN_WEIGHTS = ['norm_mix', 'norm_ffn', 'gdn_w_in', 'gdn_conv_w', 'gdn_a_log', 'gdn_dt_bias', 'gdn_norm_w', 'gdn_w_out', 'dil_w_in', 'dil_q_norm', 'dil_k_norm', 'dil_w_out', 'ffn_w_in', 'ffn_w_out']
TWIN_DIFF_INPUT = 'x'
TWIN_INPUTS = ['x', 'norm_mix', 'norm_ffn', 'gdn_w_in', 'gdn_conv_w', 'gdn_a_log', 'gdn_dt_bias', 'gdn_norm_w', 'gdn_w_out', 'dil_w_in', 'dil_q_norm', 'dil_k_norm', 'dil_w_out', 'ffn_w_in', 'ffn_w_out', 'loss_target', 'm_norm_mix', 'm_norm_ffn', 'm_gdn_w_in', 'm_gdn_conv_w', 'm_gdn_a_log', 'm_gdn_dt_bias', 'm_gdn_norm_w', 'm_gdn_w_out', 'm_dil_w_in', 'm_dil_q_norm', 'm_dil_k_norm', 'm_dil_w_out', 'm_ffn_w_in', 'm_ffn_w_out', 'v_norm_mix', 'v_norm_ffn', 'v_gdn_w_in', 'v_gdn_conv_w', 'v_gdn_a_log', 'v_gdn_dt_bias', 'v_gdn_norm_w', 'v_gdn_w_out', 'v_dil_w_in', 'v_dil_q_norm', 'v_dil_k_norm', 'v_dil_w_out', 'v_ffn_w_in', 'v_ffn_w_out']
TWIN_OUTPUTS = ['loss', 'grad_x', 'grad_norm_mix', 'grad_norm_ffn', 'grad_gdn_w_in', 'grad_gdn_conv_w', 'grad_gdn_a_log', 'grad_gdn_dt_bias', 'grad_gdn_norm_w', 'grad_gdn_w_out', 'grad_dil_w_in', 'grad_dil_q_norm', 'grad_dil_k_norm', 'grad_dil_w_out', 'grad_ffn_w_in', 'grad_ffn_w_out', 'delta_norm_mix', 'delta_norm_ffn', 'delta_gdn_w_in', 'delta_gdn_conv_w', 'delta_gdn_a_log', 'delta_gdn_dt_bias', 'delta_gdn_norm_w', 'delta_gdn_w_out', 'delta_dil_w_in', 'delta_dil_q_norm', 'delta_dil_k_norm', 'delta_dil_w_out', 'delta_ffn_w_in', 'delta_ffn_w_out', 'new_m_norm_mix', 'new_m_norm_ffn', 'new_m_gdn_w_in', 'new_m_gdn_conv_w', 'new_m_gdn_a_log', 'new_m_gdn_dt_bias', 'new_m_gdn_norm_w', 'new_m_gdn_w_out', 'new_m_dil_w_in', 'new_m_dil_q_norm', 'new_m_dil_k_norm', 'new_m_dil_w_out', 'new_m_ffn_w_in', 'new_m_ffn_w_out', 'new_v_norm_mix', 'new_v_norm_ffn', 'new_v_gdn_w_in', 'new_v_gdn_conv_w', 'new_v_gdn_a_log', 'new_v_gdn_dt_bias', 'new_v_gdn_norm_w', 'new_v_gdn_w_out', 'new_v_dil_w_in', 'new_v_dil_q_norm', 'new_v_dil_k_norm', 'new_v_dil_w_out', 'new_v_ffn_w_in', 'new_v_ffn_w_out']
TWIN_LEAF_KINDS = {'loss': 'loss', 'grad_x': 'grad_x', 'grad_norm_mix': 'grad_w', 'grad_norm_ffn': 'grad_w', 'grad_gdn_w_in': 'grad_w', 'grad_gdn_conv_w': 'grad_w', 'grad_gdn_a_log': 'grad_w', 'grad_gdn_dt_bias': 'grad_w', 'grad_gdn_norm_w': 'grad_w', 'grad_gdn_w_out': 'grad_w', 'grad_dil_w_in': 'grad_w', 'grad_dil_q_norm': 'grad_w', 'grad_dil_k_norm': 'grad_w', 'grad_dil_w_out': 'grad_w', 'grad_ffn_w_in': 'grad_w', 'grad_ffn_w_out': 'grad_w', 'delta_norm_mix': 'delta_w', 'delta_norm_ffn': 'delta_w', 'delta_gdn_w_in': 'delta_w', 'delta_gdn_conv_w': 'delta_w', 'delta_gdn_a_log': 'delta_w', 'delta_gdn_dt_bias': 'delta_w', 'delta_gdn_norm_w': 'delta_w', 'delta_gdn_w_out': 'delta_w', 'delta_dil_w_in': 'delta_w', 'delta_dil_q_norm': 'delta_w', 'delta_dil_k_norm': 'delta_w', 'delta_dil_w_out': 'delta_w', 'delta_ffn_w_in': 'delta_w', 'delta_ffn_w_out': 'delta_w', 'new_m_norm_mix': 'new_m', 'new_m_norm_ffn': 'new_m', 'new_m_gdn_w_in': 'new_m', 'new_m_gdn_conv_w': 'new_m', 'new_m_gdn_a_log': 'new_m', 'new_m_gdn_dt_bias': 'new_m', 'new_m_gdn_norm_w': 'new_m', 'new_m_gdn_w_out': 'new_m', 'new_m_dil_w_in': 'new_m', 'new_m_dil_q_norm': 'new_m', 'new_m_dil_k_norm': 'new_m', 'new_m_dil_w_out': 'new_m', 'new_m_ffn_w_in': 'new_m', 'new_m_ffn_w_out': 'new_m', 'new_v_norm_mix': 'new_v', 'new_v_norm_ffn': 'new_v', 'new_v_gdn_w_in': 'new_v', 'new_v_gdn_conv_w': 'new_v', 'new_v_gdn_a_log': 'new_v', 'new_v_gdn_dt_bias': 'new_v', 'new_v_gdn_norm_w': 'new_v', 'new_v_gdn_w_out': 'new_v', 'new_v_dil_w_in': 'new_v', 'new_v_dil_q_norm': 'new_v', 'new_v_dil_k_norm': 'new_v', 'new_v_dil_w_out': 'new_v', 'new_v_ffn_w_in': 'new_v', 'new_v_ffn_w_out': 'new_v'}


def _forward(args):
    return _fwd_reference(*[args[k] for k in FWD_PARAMS])


def _output_shape():
    out = _jax.eval_shape(lambda: _forward(_fwd_setup_inputs(0)))
    return out.shape, out.dtype

N_MICROBATCH = 1
ADAM_LR = 0.001
ADAM_B1 = 0.9
ADAM_B2 = 0.999
ADAM_EPS = 1e-08
ADAM_WD = 0.01
ADAM_STEP = 10
PER_EXAMPLE_BATCH_AXIS = {'x': 0, 'loss_target': 0}
SHARED_INPUTS = []
_WEIGHT_DTYPES = {'norm_mix': _jnp.float32, 'norm_ffn': _jnp.float32, 'gdn_w_in': _jnp.float32, 'gdn_conv_w': _jnp.float32, 'gdn_a_log': _jnp.float32, 'gdn_dt_bias': _jnp.float32, 'gdn_norm_w': _jnp.float32, 'gdn_w_out': _jnp.float32, 'dil_w_in': _jnp.float32, 'dil_q_norm': _jnp.float32, 'dil_k_norm': _jnp.float32, 'dil_w_out': _jnp.float32, 'ffn_w_in': _jnp.float32, 'ffn_w_out': _jnp.float32}
MOMENT_SCALE = {'norm_mix': 2.139506e+00, 'norm_ffn': 5.986797e+00, 'gdn_w_in': 1.051539e-01, 'gdn_conv_w': 1.170485e-01, 'gdn_a_log': 1.004807e+01, 'gdn_dt_bias': 9.638327e+00, 'gdn_norm_w': 2.229391e+01, 'gdn_w_out': 5.319589e-01, 'dil_w_in': 3.951963e-02, 'dil_q_norm': 4.200429e+00, 'dil_k_norm': 4.192984e+00, 'dil_w_out': 1.603071e-01, 'ffn_w_in': 6.606052e-02, 'ffn_w_out': 3.277213e-01}


def _to_microbatches(a, axis):
    t = _jnp.moveaxis(a, axis, 0)
    t = t.reshape((N_MICROBATCH, t.shape[0] // N_MICROBATCH) + t.shape[1:])
    return _jnp.moveaxis(t, 1, axis + 1)


def setup_inputs(seed: int = 0) -> dict:
    inp = _fwd_setup_inputs(seed)
    key = _jax.random.fold_in(_jax.random.key(seed), 7919)
    shape, _ = _output_shape()
    out = dict(inp)
    out["loss_target"] = _jax.random.normal(_jax.random.fold_in(key, 0), shape, _jnp.float32)
    for i, name in enumerate(TWIN_WEIGHTS):
        w = inp[name].astype(_jnp.float32)
        if MOMENT_SCALE is None:
            s = _jnp.sqrt(_jnp.mean(_jnp.square(w)) + 1e-30)
        else:
            s = MOMENT_SCALE[name]
        km, kv = _jax.random.split(_jax.random.fold_in(key, i + 1))
        out[name] = w
        out["m_" + name] = s * _jax.random.normal(km, w.shape, _jnp.float32)
        out["v_" + name] = (s * s) * _jax.random.uniform(kv, w.shape, _jnp.float32, 0.5, 1.5)
    if N_MICROBATCH > 1:
        for name, axis in PER_EXAMPLE_BATCH_AXIS.items():
            out[name] = _to_microbatches(out[name], axis)
    return {'x': out['x'], 'norm_mix': out['norm_mix'], 'norm_ffn': out['norm_ffn'], 'gdn_w_in': out['gdn_w_in'], 'gdn_conv_w': out['gdn_conv_w'], 'gdn_a_log': out['gdn_a_log'], 'gdn_dt_bias': out['gdn_dt_bias'], 'gdn_norm_w': out['gdn_norm_w'], 'gdn_w_out': out['gdn_w_out'], 'dil_w_in': out['dil_w_in'], 'dil_q_norm': out['dil_q_norm'], 'dil_k_norm': out['dil_k_norm'], 'dil_w_out': out['dil_w_out'], 'ffn_w_in': out['ffn_w_in'], 'ffn_w_out': out['ffn_w_out'], 'loss_target': out['loss_target'], 'm_norm_mix': out['m_norm_mix'], 'm_norm_ffn': out['m_norm_ffn'], 'm_gdn_w_in': out['m_gdn_w_in'], 'm_gdn_conv_w': out['m_gdn_conv_w'], 'm_gdn_a_log': out['m_gdn_a_log'], 'm_gdn_dt_bias': out['m_gdn_dt_bias'], 'm_gdn_norm_w': out['m_gdn_norm_w'], 'm_gdn_w_out': out['m_gdn_w_out'], 'm_dil_w_in': out['m_dil_w_in'], 'm_dil_q_norm': out['m_dil_q_norm'], 'm_dil_k_norm': out['m_dil_k_norm'], 'm_dil_w_out': out['m_dil_w_out'], 'm_ffn_w_in': out['m_ffn_w_in'], 'm_ffn_w_out': out['m_ffn_w_out'], 'v_norm_mix': out['v_norm_mix'], 'v_norm_ffn': out['v_norm_ffn'], 'v_gdn_w_in': out['v_gdn_w_in'], 'v_gdn_conv_w': out['v_gdn_conv_w'], 'v_gdn_a_log': out['v_gdn_a_log'], 'v_gdn_dt_bias': out['v_gdn_dt_bias'], 'v_gdn_norm_w': out['v_gdn_norm_w'], 'v_gdn_w_out': out['v_gdn_w_out'], 'v_dil_w_in': out['v_dil_w_in'], 'v_dil_q_norm': out['v_dil_q_norm'], 'v_dil_k_norm': out['v_dil_k_norm'], 'v_dil_w_out': out['v_dil_w_out'], 'v_ffn_w_in': out['v_ffn_w_in'], 'v_ffn_w_out': out['v_ffn_w_out']}


def _loss(weights, diff, rest, loss_target):
    with _jax.named_scope("forward"):
        args = {**rest, TWIN_DIFF_INPUT: diff, **{k: w.astype(_WEIGHT_DTYPES[k]) for k, w in weights.items()}}
        y = _forward(args)
    with _jax.named_scope("loss_head"):
        err = _jnp.square(y.astype(_jnp.float32) - loss_target)
        return 0.5 * _jnp.sum(_jnp.mean(err, axis=-1)) if err.ndim else 0.5 * err


def _adamw(w, g, m, v):
    m = ADAM_B1 * m + (1.0 - ADAM_B1) * g
    v = ADAM_B2 * v + (1.0 - ADAM_B2) * _jnp.square(g)
    m_hat = m / (1.0 - ADAM_B1 ** ADAM_STEP)
    v_hat = v / (1.0 - ADAM_B2 ** ADAM_STEP)
    delta = -ADAM_LR * (m_hat / (_jnp.sqrt(v_hat) + ADAM_EPS) + ADAM_WD * w)
    return delta, m, v


def reference(x, norm_mix, norm_ffn, gdn_w_in, gdn_conv_w, gdn_a_log, gdn_dt_bias, gdn_norm_w, gdn_w_out, dil_w_in, dil_q_norm, dil_k_norm, dil_w_out, ffn_w_in, ffn_w_out, loss_target, m_norm_mix, m_norm_ffn, m_gdn_w_in, m_gdn_conv_w, m_gdn_a_log, m_gdn_dt_bias, m_gdn_norm_w, m_gdn_w_out, m_dil_w_in, m_dil_q_norm, m_dil_k_norm, m_dil_w_out, m_ffn_w_in, m_ffn_w_out, v_norm_mix, v_norm_ffn, v_gdn_w_in, v_gdn_conv_w, v_gdn_a_log, v_gdn_dt_bias, v_gdn_norm_w, v_gdn_w_out, v_dil_w_in, v_dil_q_norm, v_dil_k_norm, v_dil_w_out, v_ffn_w_in, v_ffn_w_out):
    given = dict(x=x, norm_mix=norm_mix, norm_ffn=norm_ffn, gdn_w_in=gdn_w_in, gdn_conv_w=gdn_conv_w, gdn_a_log=gdn_a_log, gdn_dt_bias=gdn_dt_bias, gdn_norm_w=gdn_norm_w, gdn_w_out=gdn_w_out, dil_w_in=dil_w_in, dil_q_norm=dil_q_norm, dil_k_norm=dil_k_norm, dil_w_out=dil_w_out, ffn_w_in=ffn_w_in, ffn_w_out=ffn_w_out, loss_target=loss_target, m_norm_mix=m_norm_mix, m_norm_ffn=m_norm_ffn, m_gdn_w_in=m_gdn_w_in, m_gdn_conv_w=m_gdn_conv_w, m_gdn_a_log=m_gdn_a_log, m_gdn_dt_bias=m_gdn_dt_bias, m_gdn_norm_w=m_gdn_norm_w, m_gdn_w_out=m_gdn_w_out, m_dil_w_in=m_dil_w_in, m_dil_q_norm=m_dil_q_norm, m_dil_k_norm=m_dil_k_norm, m_dil_w_out=m_dil_w_out, m_ffn_w_in=m_ffn_w_in, m_ffn_w_out=m_ffn_w_out, v_norm_mix=v_norm_mix, v_norm_ffn=v_norm_ffn, v_gdn_w_in=v_gdn_w_in, v_gdn_conv_w=v_gdn_conv_w, v_gdn_a_log=v_gdn_a_log, v_gdn_dt_bias=v_gdn_dt_bias, v_gdn_norm_w=v_gdn_norm_w, v_gdn_w_out=v_gdn_w_out, v_dil_w_in=v_dil_w_in, v_dil_q_norm=v_dil_q_norm, v_dil_k_norm=v_dil_k_norm, v_dil_w_out=v_dil_w_out, v_ffn_w_in=v_ffn_w_in, v_ffn_w_out=v_ffn_w_out)
    weights = {n: given[n] for n in TWIN_WEIGHTS}
    shared = {n: given[n] for n in SHARED_INPUTS}
    per_example = {n: given[n] for n in ['x']}
    grad_fn = _jax.value_and_grad(_loss, argnums=(0, 1))

    def one_microbatch(ex, loss_target):
        ex = dict(ex)
        diff = ex.pop(TWIN_DIFF_INPUT)
        return grad_fn(weights, diff, {**shared, **ex}, loss_target)

    if N_MICROBATCH == 1:
        loss, (grad_w, grad_x) = one_microbatch(per_example, given["loss_target"])
    else:
        def body(carry, xs):
            loss_sum, grad_sum = carry
            l_k, (gw_k, gx_k) = one_microbatch(xs[0], xs[1])
            with _jax.named_scope("update"):
                return (loss_sum + l_k, _jax.tree.map(_jnp.add, grad_sum, gw_k)), gx_k

        init = (_jnp.zeros((), _jnp.float32), _jax.tree.map(_jnp.zeros_like, weights))
        (loss, grad_w), grad_x = _jax.lax.scan(body, init, (per_example, given["loss_target"]))
    with _jax.named_scope("update"):
        delta_w, new_m, new_v = {}, {}, {}
        for n in TWIN_WEIGHTS:
            delta_w[n], new_m[n], new_v[n] = _adamw(weights[n], grad_w[n], given["m_" + n], given["v_" + n])
    return (loss, grad_x, *[grad_w[n] for n in TWIN_WEIGHTS], *[delta_w[n] for n in TWIN_WEIGHTS],
            *[new_m[n] for n in TWIN_WEIGHTS], *[new_v[n] for n in TWIN_WEIGHTS])
```

```python
import functools
import math

import jax
import jax.numpy as jnp
from jax import lax
from jax.experimental import pallas as pl
from jax.experimental.pallas import tpu as pltpu

F32 = jnp.float32
BF16 = jnp.bfloat16
MM_DTYPE = BF16

N_DEV = 8
D_MODEL = 1024
DEPTH = 4
RMS_EPS = 1e-6
L2_EPS = 1e-6

LANES = 128

GDN_HEADS = 8
GDN_DK = 128
GDN_DV = 128
GDN_CONV = 4
GDN_CHUNK = 128
GDN_QKV = 3 * GDN_HEADS * GDN_DK
GDN_MAIN = GDN_QKV + GDN_HEADS * GDN_DV
GDN_IN_WIDTH = GDN_MAIN + 2 * GDN_HEADS

DIL_GROUPS = ((128, 1), (512, 4), (2048, 16))
DIL_HEADS = 8
DIL_DH = 64
DIL_SPAN = 128
DIL_SLAB = 3 * DIL_HEADS * LANES
ALIBI_MAX_BIAS = 8.0

FFN_HIDDEN = 2816

ADAM_LR = 0.001
ADAM_B1 = 0.9
ADAM_B2 = 0.999
ADAM_EPS = 1e-08
ADAM_WD = 0.01
ADAM_STEP = 10

VMEM_LIMIT = 56 * 1024 * 1024
NEG = -1e30
HI = lax.Precision.HIGHEST


def _cparams(sem):
    return pltpu.CompilerParams(dimension_semantics=sem, vmem_limit_bytes=VMEM_LIMIT)


def _dot(a, b):
    return lax.dot_general(a, b, (((1,), (0,)), ((), ())), preferred_element_type=F32, precision=HI)


def _dot_nt(a, b):
    return lax.dot_general(a, b, (((1,), (1,)), ((), ())), preferred_element_type=F32, precision=HI)


def _dot_tn(a, b):
    return lax.dot_general(a, b, (((0,), (0,)), ((), ())), preferred_element_type=F32, precision=HI)


def _pick(n, candidates):
    for c in candidates:
        if n % c == 0:
            return c
    raise ValueError(f"no tile for {n}")


def _matmul(a, b, *, name, trans_a=False, add=None, out_dtype=F32):
    if trans_a:
        k_dim, m_dim = a.shape
    else:
        m_dim, k_dim = a.shape
    k2, n_dim = b.shape
    assert k_dim == k2, (a.shape, b.shape)
    tn = _pick(n_dim, (1024, 512, 256, 128))
    tm = min(m_dim, 2048, max(512, (1024 * 1024) // tn))
    tm = _pick(m_dim, (tm, 1024, 512, 256, 128))
    tk = _pick(k_dim, (1024, 1408, 512, 256, 128))
    nk = k_dim // tk
    has_add = add is not None
    dn = (((0,), (0,)), ((), ())) if trans_a else (((1,), (0,)), ((), ()))

    def body(*refs):
        if has_add:
            a_ref, b_ref, add_ref, o_ref, acc_ref = refs
        else:
            a_ref, b_ref, o_ref, acc_ref = refs
        part = lax.dot_general(a_ref[...], b_ref[...], dn, preferred_element_type=F32)

        def finish(total):
            if has_add:
                total = total + add_ref[...]
            o_ref[...] = total.astype(out_dtype)

        if nk == 1:
            finish(part)
        else:
            k = pl.program_id(2)

            @pl.when(k == 0)
            def _():
                acc_ref[...] = part

            @pl.when(k > 0)
            def _():
                acc_ref[...] += part

            @pl.when(k == nk - 1)
            def _():
                finish(acc_ref[...])

    if trans_a:
        a_spec = pl.BlockSpec((tk, tm), lambda i, j, k: (k, i))
    else:
        a_spec = pl.BlockSpec((tm, tk), lambda i, j, k: (i, k))
    in_specs = [a_spec, pl.BlockSpec((tk, tn), lambda i, j, k: (k, j))]
    args = [a, b]
    if has_add:
        in_specs.append(pl.BlockSpec((tm, tn), lambda i, j, k: (i, j)))
        args.append(add)
    return pl.pallas_call(
        body,
        grid=(m_dim // tm, n_dim // tn, nk),
        in_specs=in_specs,
        out_specs=pl.BlockSpec((tm, tn), lambda i, j, k: (i, j)),
        out_shape=jax.ShapeDtypeStruct((m_dim, n_dim), out_dtype),
        scratch_shapes=[pltpu.VMEM((tm, tn) if nk > 1 else (8, LANES), F32)],
        compiler_params=_cparams(("parallel", "parallel", "arbitrary")),
        name=name,
    )(*args)


def _rmsnorm_fwd(x, w_row, *, name):
    t, d = x.shape
    tb = min(t, 1024)

    def body(x_ref, w_ref, o_ref):
        xf = x_ref[...]
        r = lax.rsqrt(jnp.mean(xf * xf, axis=-1, keepdims=True) + RMS_EPS)
        o_ref[...] = (xf * r * w_ref[...]).astype(o_ref.dtype)

    return pl.pallas_call(
        body,
        grid=(t // tb,),
        in_specs=[pl.BlockSpec((tb, d), lambda i: (i, 0)), pl.BlockSpec((1, d), lambda i: (0, 0))],
        out_specs=pl.BlockSpec((tb, d), lambda i: (i, 0)),
        out_shape=jax.ShapeDtypeStruct((t, d), MM_DTYPE),
        compiler_params=_cparams(("parallel",)),
        name=name,
    )(x, w_row)


def _rmsnorm_bwd(x, w_row, dy, dskip, *, name):
    t, d = x.shape
    tb = min(t, 512)

    def body(x_ref, w_ref, dy_ref, ds_ref, dx_ref, dw_ref):
        xf = x_ref[...]
        g = dy_ref[...]
        r = lax.rsqrt(jnp.mean(xf * xf, axis=-1, keepdims=True) + RMS_EPS)
        gw = g * w_ref[...]
        proj = jnp.mean(gw * xf, axis=-1, keepdims=True)
        dx_ref[...] = r * gw - xf * (r * r * r * proj) + ds_ref[...]
        part = jnp.sum(g * xf * r, axis=0, keepdims=True)

        @pl.when(pl.program_id(0) == 0)
        def _():
            dw_ref[...] = part

        @pl.when(pl.program_id(0) > 0)
        def _():
            dw_ref[...] += part

    row = pl.BlockSpec((tb, d), lambda i: (i, 0))
    one = pl.BlockSpec((1, d), lambda i: (0, 0))
    return pl.pallas_call(
        body,
        grid=(t // tb,),
        in_specs=[row, one, row, row],
        out_specs=[row, one],
        out_shape=[jax.ShapeDtypeStruct((t, d), F32), jax.ShapeDtypeStruct((1, d), F32)],
        compiler_params=_cparams(("arbitrary",)),
        name=name,
    )(x, w_row, dy, dskip)


def _silu(z):
    return z / (1.0 + jnp.exp(-z))


def _swiglu_fwd(gu, *, name):
    t = gu.shape[0]
    h = FFN_HIDDEN
    tb, tc = min(t, 1024), 256
    nc = h // tc

    def body(g_ref, u_ref, o_ref):
        o_ref[...] = (_silu(g_ref[...]) * u_ref[...]).astype(o_ref.dtype)

    return pl.pallas_call(
        body,
        grid=(t // tb, nc),
        in_specs=[pl.BlockSpec((tb, tc), lambda i, j: (i, j)), pl.BlockSpec((tb, tc), lambda i, j: (i, j + nc))],
        out_specs=pl.BlockSpec((tb, tc), lambda i, j: (i, j)),
        out_shape=jax.ShapeDtypeStruct((t, h), MM_DTYPE),
        compiler_params=_cparams(("parallel", "parallel")),
        name=name,
    )(gu, gu)


def _swiglu_bwd(gu, dact, *, name):
    t = gu.shape[0]
    h = FFN_HIDDEN
    tb, tc = min(t, 1024), 256
    nc = h // tc

    def body(g_ref, u_ref, da_ref, dg_ref, du_ref):
        g = g_ref[...]
        da = da_ref[...]
        sig = 1.0 / (1.0 + jnp.exp(-g))
        sg = g * sig
        dg_ref[...] = (da * u_ref[...] * (sig + sg * (1.0 - sig))).astype(dg_ref.dtype)
        du_ref[...] = (da * sg).astype(du_ref.dtype)

    blk = pl.BlockSpec((tb, tc), lambda i, j: (i, j))
    return pl.pallas_call(
        body,
        grid=(t // tb, nc),
        in_specs=[blk, pl.BlockSpec((tb, tc), lambda i, j: (i, j + nc)), blk],
        out_specs=[blk, blk],
        out_shape=[jax.ShapeDtypeStruct((t, h), MM_DTYPE)] * 2,
        compiler_params=_cparams(("parallel", "parallel")),
        name=name,
    )(gu, gu, dact)


def _loss_head(y, target, *, name):
    t, d = y.shape
    tb = min(t, 1024)

    def body(y_ref, t_ref, dy_ref, l_ref):
        err = y_ref[...] - t_ref[...]
        dy_ref[...] = err * (1.0 / d)
        part = jnp.sum(jnp.sum(err * err, axis=0, keepdims=True), axis=1, keepdims=True) * (0.5 / d)
        part = jnp.broadcast_to(part, l_ref.shape)

        @pl.when(pl.program_id(0) == 0)
        def _():
            l_ref[...] = part

        @pl.when(pl.program_id(0) > 0)
        def _():
            l_ref[...] += part

    row = pl.BlockSpec((tb, d), lambda i: (i, 0))
    return pl.pallas_call(
        body,
        grid=(t // tb,),
        in_specs=[row, row],
        out_specs=[row, pl.BlockSpec((8, LANES), lambda i: (0, 0))],
        out_shape=[jax.ShapeDtypeStruct((t, d), F32), jax.ShapeDtypeStruct((8, LANES), F32)],
        compiler_params=_cparams(("arbitrary",)),
        name=name,
    )(y, target)


CONV_HALO = 8


def _conv_tile_scale(c):
    is_qk = c < 2 * GDN_HEADS
    scale = jnp.where(c < GDN_HEADS, GDN_DK ** -0.5, 1.0).astype(F32)
    return is_qk, scale


def _gdn_conv_fwd(pm, conv_w, *, name):
    t = pm.shape[0]
    tb = min(t, 1024)
    nt = t // tb
    hb = tb // CONV_HALO

    def body(x_ref, xp_ref, w_ref, o_ref):
        c = pl.program_id(0)
        ti = pl.program_id(1)
        prev = jnp.where(ti > 0, xp_ref[...], 0.0)
        xe = jnp.concatenate([prev, x_ref[...]], axis=0)
        w = w_ref[...]
        y = jnp.zeros((tb, LANES), F32)
        for j in range(GDN_CONV):
            off = CONV_HALO - (GDN_CONV - 1) + j
            y = y + w[j:j + 1, :] * xe[off:off + tb, :]
        s = _silu(y)
        is_qk, scale = _conv_tile_scale(c)
        r = lax.rsqrt(jnp.sum(s * s, axis=-1, keepdims=True) + L2_EPS) * scale
        o_ref[...] = s * jnp.where(is_qk, r, 1.0)

    return pl.pallas_call(
        body,
        grid=(GDN_QKV // LANES, nt),
        in_specs=[
            pl.BlockSpec((tb, LANES), lambda c, i: (i, c)),
            pl.BlockSpec((CONV_HALO, LANES), lambda c, i: (jnp.maximum(i * hb - 1, 0), c)),
            pl.BlockSpec((GDN_CONV, LANES), lambda c, i: (0, c)),
        ],
        out_specs=pl.BlockSpec((tb, LANES), lambda c, i: (i, c)),
        out_shape=jax.ShapeDtypeStruct((t, GDN_QKV), F32),
        compiler_params=_cparams(("parallel", "parallel")),
        name=name,
    )(pm, pm, conv_w)


def _gdn_conv_bwd(pm, conv_w, dout, *, name):
    t = pm.shape[0]
    tb = min(t, 1024)
    nt = t // tb
    hb = tb // CONV_HALO
    last_hb = t // CONV_HALO - 1
    ext = tb + CONV_HALO

    def body(x_ref, xp_ref, xn_ref, d_ref, dn_ref, w_ref, dx_ref, dw_ref):
        c = pl.program_id(0)
        ti = pl.program_id(1)
        prev = jnp.where(ti > 0, xp_ref[...], 0.0)
        has_next = ti < nt - 1
        nxt = jnp.where(has_next, xn_ref[...], 0.0)
        xe = jnp.concatenate([prev, x_ref[...], nxt], axis=0)
        de = jnp.concatenate([d_ref[...], jnp.where(has_next, dn_ref[...], 0.0)], axis=0)
        w = w_ref[...]
        y = jnp.zeros((ext, LANES), F32)
        for j in range(GDN_CONV):
            off = CONV_HALO - (GDN_CONV - 1) + j
            y = y + w[j:j + 1, :] * xe[off:off + ext, :]
        sig = 1.0 / (1.0 + jnp.exp(-y))
        s = y * sig
        is_qk, scale = _conv_tile_scale(c)
        r = lax.rsqrt(jnp.sum(s * s, axis=-1, keepdims=True) + L2_EPS)
        n = s * r
        dnrm = de * scale
        ds_qk = r * (dnrm - n * jnp.sum(dnrm * n, axis=-1, keepdims=True))
        ds = jnp.where(is_qk, ds_qk, de)
        dy = ds * (sig + s * (1.0 - sig))
        dx = jnp.zeros((tb, LANES), F32)
        dw_rows = []
        for j in range(GDN_CONV):
            sh = GDN_CONV - 1 - j
            dx = dx + w[j:j + 1, :] * dy[sh:sh + tb, :]
            off = CONV_HALO - (GDN_CONV - 1) + j
            dw_rows.append(jnp.sum(dy[:tb, :] * xe[off:off + tb, :], axis=0, keepdims=True))
        dx_ref[...] = dx.astype(dx_ref.dtype)
        part = jnp.concatenate(dw_rows, axis=0)

        @pl.when(ti == 0)
        def _():
            dw_ref[...] = part

        @pl.when(ti > 0)
        def _():
            dw_ref[...] += part

    main = pl.BlockSpec((tb, LANES), lambda c, i: (i, c))
    prev = pl.BlockSpec((CONV_HALO, LANES), lambda c, i: (jnp.maximum(i * hb - 1, 0), c))
    nxt = pl.BlockSpec((CONV_HALO, LANES), lambda c, i: (jnp.minimum((i + 1) * hb, last_hb), c))
    return pl.pallas_call(
        body,
        grid=(GDN_QKV // LANES, nt),
        in_specs=[main, prev, nxt, main, nxt, pl.BlockSpec((GDN_CONV, LANES), lambda c, i: (0, c))],
        out_specs=[main, pl.BlockSpec((GDN_CONV, LANES), lambda c, i: (0, c))],
        out_shape=[jax.ShapeDtypeStruct((t, GDN_QKV), MM_DTYPE), jax.ShapeDtypeStruct((GDN_CONV, GDN_QKV), F32)],
        compiler_params=_cparams(("parallel", "arbitrary")),
        name=name,
    )(pm, pm, pm, dout, dout, conv_w)


def _head_selector(first_col):
    row = lax.broadcasted_iota(jnp.int32, (LANES, GDN_HEADS * LANES), 0)
    col = lax.broadcasted_iota(jnp.int32, (LANES, GDN_HEADS * LANES), 1)
    return (col // LANES + first_col == row).astype(F32)


def _softplus(x):
    return jnp.maximum(x, 0.0) + jnp.log(1.0 + jnp.exp(-jnp.abs(x)))


def _gdn_gates_fwd(ab, alog_row, dt_row, *, name):
    t = ab.shape[0]
    tb = min(t, 1024)
    wide = GDN_HEADS * LANES

    def body(ab_ref, al_ref, dt_ref, g_ref, b_ref):
        x = ab_ref[...]
        g_cols = -jnp.exp(al_ref[...]) * _softplus(x + dt_ref[...])
        b_cols = 1.0 / (1.0 + jnp.exp(-x))
        g_ref[...] = _dot(g_cols, _head_selector(0))
        b_ref[...] = _dot(b_cols, _head_selector(GDN_HEADS))

    row = pl.BlockSpec((tb, LANES), lambda i: (i, 0))
    one = pl.BlockSpec((1, LANES), lambda i: (0, 0))
    out = pl.BlockSpec((tb, wide), lambda i: (i, 0))
    return pl.pallas_call(
        body,
        grid=(t // tb,),
        in_specs=[row, one, one],
        out_specs=[out, out],
        out_shape=[jax.ShapeDtypeStruct((t, wide), F32)] * 2,
        compiler_params=_cparams(("parallel",)),
        name=name,
    )(ab, alog_row, dt_row)


def _gdn_gates_bwd(ab, alog_row, dt_row, dgb, dbb, *, name):
    t = ab.shape[0]
    tb = min(t, 1024)
    wide = GDN_HEADS * LANES

    def body(ab_ref, al_ref, dt_ref, dg_ref, db_ref, dab_ref, dal_ref, ddt_ref):
        x = ab_ref[...]
        lane = lax.broadcasted_iota(jnp.int32, (tb, LANES), 1)
        dg_cols = _dot_nt(dg_ref[...], _head_selector(0))
        db_cols = _dot_nt(db_ref[...], _head_selector(GDN_HEADS))
        ea = jnp.exp(al_ref[...])
        z = x + dt_ref[...]
        sp = _softplus(z)
        sg = 1.0 / (1.0 + jnp.exp(-z))
        beta = 1.0 / (1.0 + jnp.exp(-x))
        da = jnp.where(lane < GDN_HEADS, dg_cols * (-ea) * sg, 0.0)
        db = jnp.where((lane >= GDN_HEADS) & (lane < 2 * GDN_HEADS), db_cols * beta * (1.0 - beta), 0.0)
        dab_ref[...] = (da + db).astype(dab_ref.dtype)
        p_al = jnp.sum(jnp.where(lane < GDN_HEADS, dg_cols * (-ea) * sp, 0.0), axis=0, keepdims=True)
        p_dt = jnp.sum(da, axis=0, keepdims=True)

        @pl.when(pl.program_id(0) == 0)
        def _():
            dal_ref[...] = p_al
            ddt_ref[...] = p_dt

        @pl.when(pl.program_id(0) > 0)
        def _():
            dal_ref[...] += p_al
            ddt_ref[...] += p_dt

    row = pl.BlockSpec((tb, LANES), lambda i: (i, 0))
    one = pl.BlockSpec((1, LANES), lambda i: (0, 0))
    big = pl.BlockSpec((tb, wide), lambda i: (i, 0))
    return pl.pallas_call(
        body,
        grid=(t // tb,),
        in_specs=[row, one, one, big, big],
        out_specs=[row, one, one],
        out_shape=[jax.ShapeDtypeStruct((t, LANES), MM_DTYPE), jax.ShapeDtypeStruct((1, LANES), F32),
                   jax.ShapeDtypeStruct((1, LANES), F32)],
        compiler_params=_cparams(("arbitrary",)),
        name=name,
    )(ab, alog_row, dt_row, dgb, dbb)


@jax.custom_vjp
def _unit_lower_inverse(n):
    eye = (lax.broadcasted_iota(jnp.int32, n.shape, 0) == lax.broadcasted_iota(jnp.int32, n.shape, 1)).astype(F32)
    inv = eye - n
    power = n
    for _ in range(6):
        power = _dot(power, power)
        inv = _dot(inv, eye + power)
    return inv


def _unit_lower_inverse_fwd(n):
    inv = _unit_lower_inverse(n)
    return inv, inv


def _unit_lower_inverse_bwd(inv, ct):
    return (-_dot_tn(inv, _dot_nt(ct, inv)),)


_unit_lower_inverse.defvjp(_unit_lower_inverse_fwd, _unit_lower_inverse_bwd)


def _gdn_prep_math(q, k, v, gb, bb):
    c = GDN_CHUNK
    ri = lax.broadcasted_iota(jnp.int32, (c, c), 0)
    ci = lax.broadcasted_iota(jnp.int32, (c, c), 1)
    causal = ri >= ci
    gc = _dot(causal.astype(F32), gb)
    decay = jnp.exp(jnp.where(causal, gc - gc.T, NEG))
    n = jnp.where(ri > ci, _dot_nt(k, k) * bb * decay, 0.0)
    inv = _unit_lower_inverse(n)
    eg = jnp.exp(gc)
    u = _dot(inv, v * bb)
    w = _dot(inv, k * bb * eg)
    qk = _dot_nt(q, k) * decay
    qd = q * eg
    gl = _dot((ci == c - 1).astype(F32), gc)
    kt = k * jnp.exp(gl - gc)
    cd = jnp.exp(gl)
    return u, w, qk, qd, kt, cd


def _head_tiles(ref, h):
    return ref[:, h * LANES:(h + 1) * LANES]


def _gdn_prep_fwd(qkv, gb, bb, *, name):
    t = qkv.shape[0]
    c = GDN_CHUNK
    wide = GDN_HEADS * LANES

    def body(q_ref, k_ref, v_ref, g_ref, b_ref, *outs):
        for h in range(GDN_HEADS):
            res = _gdn_prep_math(_head_tiles(q_ref, h), _head_tiles(k_ref, h), _head_tiles(v_ref, h),
                                 _head_tiles(g_ref, h), _head_tiles(b_ref, h))
            for o_ref, val in zip(outs, res):
                o_ref[:, h * LANES:(h + 1) * LANES] = val

    blk = lambda off: pl.BlockSpec((c, wide), lambda i: (i, off))
    return pl.pallas_call(
        body,
        grid=(t // c,),
        in_specs=[blk(0), blk(1), blk(2), blk(0), blk(0)],
        out_specs=[blk(0)] * 6,
        out_shape=[jax.ShapeDtypeStruct((t, wide), F32)] * 6,
        compiler_params=_cparams(("parallel",)),
        name=name,
    )(qkv, qkv, qkv, gb, bb)


def _gdn_prep_bwd(qkv, gb, bb, cts, *, name):
    t = qkv.shape[0]
    c = GDN_CHUNK
    wide = GDN_HEADS * LANES

    def body(q_ref, k_ref, v_ref, g_ref, b_ref, c0, c1, c2, c3, c4, c5, dqkv_ref, dg_ref, db_ref):
        for h in range(GDN_HEADS):
            prim = (_head_tiles(q_ref, h), _head_tiles(k_ref, h), _head_tiles(v_ref, h),
                    _head_tiles(g_ref, h), _head_tiles(b_ref, h))
            _, pull = jax.vjp(_gdn_prep_math, *prim)
            dq, dk, dv, dg, db = pull(tuple(_head_tiles(r, h) for r in (c0, c1, c2, c3, c4, c5)))
            dqkv_ref[:, h * LANES:(h + 1) * LANES] = dq
            dqkv_ref[:, wide + h * LANES:wide + (h + 1) * LANES] = dk
            dqkv_ref[:, 2 * wide + h * LANES:2 * wide + (h + 1) * LANES] = dv
            dg_ref[:, h * LANES:(h + 1) * LANES] = dg
            db_ref[:, h * LANES:(h + 1) * LANES] = db

    blk = lambda off: pl.BlockSpec((c, wide), lambda i: (i, off))
    return pl.pallas_call(
        body,
        grid=(t // c,),
        in_specs=[blk(0), blk(1), blk(2), blk(0), blk(0)] + [blk(0)] * 6,
        out_specs=[pl.BlockSpec((c, 3 * wide), lambda i: (i, 0)), blk(0), blk(0)],
        out_shape=[jax.ShapeDtypeStruct((t, 3 * wide), F32), jax.ShapeDtypeStruct((t, wide), F32),
                   jax.ShapeDtypeStruct((t, wide), F32)],
        compiler_params=_cparams(("parallel",)),
        name=name,
    )(qkv, qkv, qkv, gb, bb, *cts)


def _gdn_scan_math(s, u, w, qk, qd, kt, cd):
    v_new = u - _dot(w, s)
    o = _dot(qd, s) + _dot(qk, v_new)
    s_new = s * cd + _dot_tn(kt, v_new)
    return o, s_new


def _gdn_scan_fwd(prep, *, name):
    t = prep[0].shape[0]
    c = GDN_CHUNK
    wide = GDN_HEADS * LANES

    def body(u_ref, w_ref, qk_ref, qd_ref, kt_ref, cd_ref, o_ref, st_ref, s_ref):
        @pl.when(pl.program_id(0) == 0)
        def _():
            s_ref[...] = jnp.zeros_like(s_ref)

        for h in range(GDN_HEADS):
            s = _head_tiles(s_ref, h)
            st_ref[:, h * LANES:(h + 1) * LANES] = s
            o, s_new = _gdn_scan_math(s, *(_head_tiles(r, h) for r in (u_ref, w_ref, qk_ref, qd_ref, kt_ref, cd_ref)))
            o_ref[:, h * LANES:(h + 1) * LANES] = o
            s_ref[:, h * LANES:(h + 1) * LANES] = s_new

    blk = pl.BlockSpec((c, wide), lambda i: (i, 0))
    return pl.pallas_call(
        body,
        grid=(t // c,),
        in_specs=[blk] * 6,
        out_specs=[blk, blk],
        out_shape=[jax.ShapeDtypeStruct((t, wide), F32)] * 2,
        scratch_shapes=[pltpu.VMEM((GDN_DK, wide), F32)],
        compiler_params=_cparams(("arbitrary",)),
        name=name,
    )(*prep)


def _gdn_scan_bwd(prep, states, do, *, name):
    t = do.shape[0]
    c = GDN_CHUNK
    wide = GDN_HEADS * LANES
    nc = t // c

    def body(u_ref, w_ref, qk_ref, qd_ref, kt_ref, cd_ref, st_ref, do_ref, *rest):
        outs, ds_ref = rest[:6], rest[6]

        @pl.when(pl.program_id(0) == 0)
        def _():
            ds_ref[...] = jnp.zeros_like(ds_ref)

        for h in range(GDN_HEADS):
            prim = (_head_tiles(st_ref, h),) + tuple(
                _head_tiles(r, h) for r in (u_ref, w_ref, qk_ref, qd_ref, kt_ref, cd_ref))
            _, pull = jax.vjp(_gdn_scan_math, *prim)
            grads = pull((_head_tiles(do_ref, h), _head_tiles(ds_ref, h)))
            ds_ref[:, h * LANES:(h + 1) * LANES] = grads[0]
            for o_ref, val in zip(outs, grads[1:]):
                o_ref[:, h * LANES:(h + 1) * LANES] = val

    blk = pl.BlockSpec((c, wide), lambda i: (nc - 1 - i, 0))
    return pl.pallas_call(
        body,
        grid=(nc,),
        in_specs=[blk] * 8,
        out_specs=[blk] * 6,
        out_shape=[jax.ShapeDtypeStruct((t, wide), F32)] * 6,
        scratch_shapes=[pltpu.VMEM((GDN_DK, wide), F32)],
        compiler_params=_cparams(("arbitrary",)),
        name=name,
    )(*prep, states, do)


def _gdn_outgate_math(o, z, nw):
    r = lax.rsqrt(jnp.mean(o * o, axis=-1, keepdims=True) + RMS_EPS)
    return o * r * nw * _silu(z)


def _gdn_outgate_fwd(o, pm, nw_row, *, name):
    t = o.shape[0]
    tb = min(t, 1024)
    z_off = GDN_QKV // LANES

    def body(o_ref, z_ref, nw_ref, y_ref):
        y_ref[...] = _gdn_outgate_math(o_ref[...], z_ref[...], nw_ref[...]).astype(y_ref.dtype)

    return pl.pallas_call(
        body,
        grid=(t // tb, GDN_HEADS),
        in_specs=[pl.BlockSpec((tb, LANES), lambda i, h: (i, h)), pl.BlockSpec((tb, LANES), lambda i, h: (i, h + z_off)),
                  pl.BlockSpec((1, LANES), lambda i, h: (0, 0))],
        out_specs=pl.BlockSpec((tb, LANES), lambda i, h: (i, h)),
        out_shape=jax.ShapeDtypeStruct((t, GDN_HEADS * LANES), MM_DTYPE),
        compiler_params=_cparams(("parallel", "parallel")),
        name=name,
    )(o, pm, nw_row)


def _gdn_outgate_bwd(o, pm, nw_row, dy, *, name):
    t = o.shape[0]
    tb = min(t, 1024)
    z_off = GDN_QKV // LANES

    def body(o_ref, z_ref, nw_ref, dy_ref, do_ref, dz_ref, dnw_ref):
        _, pull = jax.vjp(_gdn_outgate_math, o_ref[...], z_ref[...], nw_ref[...])
        d_o, d_z, d_nw = pull(dy_ref[...])
        do_ref[...] = d_o
        dz_ref[...] = d_z.astype(dz_ref.dtype)
        first = (pl.program_id(0) == 0) & (pl.program_id(1) == 0)

        @pl.when(first)
        def _():
            dnw_ref[...] = d_nw

        @pl.when(jnp.logical_not(first))
        def _():
            dnw_ref[...] += d_nw

    blk = pl.BlockSpec((tb, LANES), lambda i, h: (i, h))
    one = pl.BlockSpec((1, LANES), lambda i, h: (0, 0))
    return pl.pallas_call(
        body,
        grid=(t // tb, GDN_HEADS),
        in_specs=[blk, pl.BlockSpec((tb, LANES), lambda i, h: (i, h + z_off)), one, blk],
        out_specs=[blk, blk, one],
        out_shape=[jax.ShapeDtypeStruct((t, GDN_HEADS * LANES), F32),
                   jax.ShapeDtypeStruct((t, GDN_HEADS * LANES), MM_DTYPE), jax.ShapeDtypeStruct((1, LANES), F32)],
        compiler_params=_cparams(("arbitrary", "arbitrary")),
        name=name,
    )(o, pm, nw_row, dy)


def _rms64(x, w_row):
    return x * lax.rsqrt(jnp.sum(x * x, axis=-1, keepdims=True) * (1.0 / DIL_DH) + RMS_EPS) * w_row


def _alibi_slope(group, head):
    return 2.0 ** (-ALIBI_MAX_BIAS * (group * DIL_HEADS + head + 1) / (len(DIL_GROUPS) * DIL_HEADS))


def _band_logits(qn, kp, kc, slope_d, has_prev):
    qi = lax.broadcasted_iota(jnp.int32, (DIL_SPAN, DIL_SPAN), 0)
    kj = lax.broadcasted_iota(jnp.int32, (DIL_SPAN, DIL_SPAN), 1)
    steps_c = (qi - kj).astype(F32)
    scale = DIL_DH ** -0.5
    sp = _dot_nt(qn, kp) * scale - slope_d * (steps_c + float(DIL_SPAN))
    sc = _dot_nt(qn, kc) * scale - slope_d * steps_c
    sp = jnp.where((kj >= qi) & has_prev, sp, NEG)
    sc = jnp.where(kj <= qi, sc, NEG)
    return sp, sc


def _dil_attn_fwd(slab, wq_row, wk_row, *, group, name):
    dilation = DIL_GROUPS[group][1]
    t = slab.shape[0]
    rows = t // dilation
    nlb = rows // DIL_SPAN
    wide = DIL_HEADS * LANES
    view = slab.reshape(rows, dilation * DIL_SLAB)

    def body(q_ref, kc_ref, vc_ref, kp_ref, vp_ref, wq_ref, wk_ref, o_ref):
        has_prev = pl.program_id(1) > 0
        lane = lax.broadcasted_iota(jnp.int32, (DIL_SPAN, LANES), 1)
        for h in range(DIL_HEADS):
            qn = _rms64(_head_tiles(q_ref, h), wq_ref[...])
            kc = _rms64(_head_tiles(kc_ref, h), wk_ref[...])
            kp = _rms64(_head_tiles(kp_ref, h), wk_ref[...])
            sp, sc = _band_logits(qn, kp, kc, _alibi_slope(group, h) * dilation, has_prev)
            m = jnp.maximum(jnp.max(sp, axis=-1, keepdims=True), jnp.max(sc, axis=-1, keepdims=True))
            pp = jnp.exp(sp - m)
            pc = jnp.exp(sc - m)
            l = jnp.sum(pp, axis=-1, keepdims=True) + jnp.sum(pc, axis=-1, keepdims=True)
            o = (_dot(pp, _head_tiles(vp_ref, h)) + _dot(pc, _head_tiles(vc_ref, h))) / l
            o_ref[:, h * LANES:(h + 1) * LANES] = jnp.where(lane < DIL_DH, o, m + jnp.log(l))

    cur = lambda part: pl.BlockSpec((DIL_SPAN, wide), lambda r, i: (i, 3 * r + part))
    prv = lambda part: pl.BlockSpec((DIL_SPAN, wide), lambda r, i: (jnp.maximum(i - 1, 0), 3 * r + part))
    one = pl.BlockSpec((1, LANES), lambda r, i: (0, 0))
    out = pl.pallas_call(
        body,
        grid=(dilation, nlb),
        in_specs=[cur(0), cur(1), cur(2), prv(1), prv(2), one, one],
        out_specs=pl.BlockSpec((DIL_SPAN, wide), lambda r, i: (i, r)),
        out_shape=jax.ShapeDtypeStruct((rows, dilation * wide), F32),
        compiler_params=_cparams(("parallel", "parallel")),
        name=name,
    )(view, view, view, view, view, wq_row, wk_row)
    return out.reshape(t, wide)


def _dil_merge_fwd(oe, *, name):
    t = oe[0].shape[0]
    tb = min(t, 1024)

    def body(e0, e1, e2, y_ref, om_ref):
        lane = lax.broadcasted_iota(jnp.int32, (tb, LANES), 1)
        es = [e0[...], e1[...], e2[...]]
        lse = [jnp.sum(jnp.where(lane == DIL_DH, e, 0.0), axis=-1, keepdims=True) for e in es]
        top = jnp.maximum(jnp.maximum(lse[0], lse[1]), lse[2])
        joint = top + jnp.log(jnp.exp(lse[0] - top) + jnp.exp(lse[1] - top) + jnp.exp(lse[2] - top))
        o = sum(jnp.exp(l - joint) * e for l, e in zip(lse, es))
        y_ref[...] = jnp.where(lane < DIL_DH, o, 0.0).astype(y_ref.dtype)
        om_ref[...] = jnp.where(lane < DIL_DH, o, joint)

    blk = pl.BlockSpec((tb, LANES), lambda i, h: (i, h))
    return pl.pallas_call(
        body,
        grid=(t // tb, DIL_HEADS),
        in_specs=[blk] * 3,
        out_specs=[blk, blk],
        out_shape=[jax.ShapeDtypeStruct((t, DIL_HEADS * LANES), MM_DTYPE),
                   jax.ShapeDtypeStruct((t, DIL_HEADS * LANES), F32)],
        compiler_params=_cparams(("parallel", "parallel")),
        name=name,
    )(*oe)


def _dil_merge_bwd(dy, om, *, name):
    t = dy.shape[0]
    tb = min(t, 1024)

    def body(dy_ref, om_ref, st_ref):
        lane = lax.broadcasted_iota(jnp.int32, (tb, LANES), 1)
        d_o = jnp.where(lane < DIL_DH, dy_ref[...], 0.0)
        om_t = om_ref[...]
        delta = jnp.sum(d_o * om_t, axis=-1, keepdims=True)
        st_ref[...] = jnp.where(lane < DIL_DH, d_o, jnp.where(lane == DIL_DH, om_t, jnp.where(lane == DIL_DH + 1, delta, 0.0)))

    blk = pl.BlockSpec((tb, LANES), lambda i, h: (i, h))
    return pl.pallas_call(
        body,
        grid=(t // tb, DIL_HEADS),
        in_specs=[blk, blk],
        out_specs=blk,
        out_shape=jax.ShapeDtypeStruct((t, DIL_HEADS * LANES), F32),
        compiler_params=_cparams(("parallel", "parallel")),
        name=name,
    )(dy, om)


def _rms64_bwd(x, w_row, dy):
    r = lax.rsqrt(jnp.sum(x * x, axis=-1, keepdims=True) * (1.0 / DIL_DH) + RMS_EPS)
    gw = dy * w_row
    dx = r * gw - x * (r * r * r * jnp.sum(gw * x, axis=-1, keepdims=True) * (1.0 / DIL_DH))
    return dx, dy * x * r


def _dil_attn_bwd(slab, stat, wq_row, wk_row, dwq_in, dwk_in, *, group, name):
    dilation = DIL_GROUPS[group][1]
    t = slab.shape[0]
    rows = t // dilation
    nlb = rows // DIL_SPAN
    wide = DIL_HEADS * LANES
    view = slab.reshape(rows, dilation * DIL_SLAB)
    stat_view = stat.reshape(rows, dilation * wide)

    def body(cur_ref, kp_ref, vp_ref, st_ref, wq_ref, wk_ref, dwq_in_ref, dwk_in_ref, d_ref, dwq_ref, dwk_ref,
             dk_carry, dv_carry):
        step = pl.program_id(1)
        has_prev = step < nlb - 1
        first = (pl.program_id(0) == 0) & (step == 0)

        @pl.when(step == 0)
        def _():
            dk_carry[...] = jnp.zeros_like(dk_carry)
            dv_carry[...] = jnp.zeros_like(dv_carry)

        @pl.when(first)
        def _():
            dwq_ref[...] = dwq_in_ref[...]
            dwk_ref[...] = dwk_in_ref[...]

        lane = lax.broadcasted_iota(jnp.int32, (DIL_SPAN, LANES), 1)
        scale = DIL_DH ** -0.5
        dwq = jnp.zeros((1, LANES), F32)
        dwk = jnp.zeros((1, LANES), F32)
        for h in range(DIL_HEADS):
            q_raw = cur_ref[:, h * LANES:(h + 1) * LANES]
            kc_raw = cur_ref[:, wide + h * LANES:wide + (h + 1) * LANES]
            vc = cur_ref[:, 2 * wide + h * LANES:2 * wide + (h + 1) * LANES]
            kp_raw = _head_tiles(kp_ref, h)
            vp = _head_tiles(vp_ref, h)
            st = _head_tiles(st_ref, h)
            d_o = jnp.where(lane < DIL_DH, st, 0.0)
            lse = jnp.sum(jnp.where(lane == DIL_DH, st, 0.0), axis=-1, keepdims=True)
            delta = jnp.sum(jnp.where(lane == DIL_DH + 1, st, 0.0), axis=-1, keepdims=True)
            qn = _rms64(q_raw, wq_ref[...])
            kc = _rms64(kc_raw, wk_ref[...])
            kp = _rms64(kp_raw, wk_ref[...])
            sp, sc = _band_logits(qn, kp, kc, _alibi_slope(group, h) * dilation, has_prev)
            pp = jnp.exp(sp - lse)
            pc = jnp.exp(sc - lse)
            dsp = pp * (_dot_nt(d_o, vp) - delta) * scale
            dsc = pc * (_dot_nt(d_o, vc) - delta) * scale
            dqn = _dot(dsp, kp) + _dot(dsc, kc)
            dkc_n = _dot_tn(dsc, qn) + _head_tiles(dk_carry, h)
            dvc = _dot_tn(pc, d_o) + _head_tiles(dv_carry, h)
            dk_carry[:, h * LANES:(h + 1) * LANES] = _dot_tn(dsp, qn)
            dv_carry[:, h * LANES:(h + 1) * LANES] = _dot_tn(pp, d_o)
            dq_raw, dwq_rows = _rms64_bwd(q_raw, wq_ref[...], dqn)
            dk_raw, dwk_rows = _rms64_bwd(kc_raw, wk_ref[...], dkc_n)
            dwq = dwq + jnp.sum(dwq_rows, axis=0, keepdims=True)
            dwk = dwk + jnp.sum(dwk_rows, axis=0, keepdims=True)
            d_ref[:, h * LANES:(h + 1) * LANES] = dq_raw.astype(d_ref.dtype)
            d_ref[:, wide + h * LANES:wide + (h + 1) * LANES] = dk_raw.astype(d_ref.dtype)
            d_ref[:, 2 * wide + h * LANES:2 * wide + (h + 1) * LANES] = dvc.astype(d_ref.dtype)
        dwq_ref[...] += dwq
        dwk_ref[...] += dwk

    blk_i = lambda i: nlb - 1 - i
    cur = pl.BlockSpec((DIL_SPAN, DIL_SLAB), lambda r, i: (blk_i(i), r))
    prv = lambda part: pl.BlockSpec((DIL_SPAN, wide), lambda r, i: (jnp.maximum(blk_i(i) - 1, 0), 3 * r + part))
    one = pl.BlockSpec((1, LANES), lambda r, i: (0, 0))
    dslab, dwq, dwk = pl.pallas_call(
        body,
        grid=(dilation, nlb),
        in_specs=[cur, prv(1), prv(2), pl.BlockSpec((DIL_SPAN, wide), lambda r, i: (blk_i(i), r)), one, one, one, one],
        out_specs=[cur, one, one],
        out_shape=[jax.ShapeDtypeStruct((rows, dilation * DIL_SLAB), MM_DTYPE), jax.ShapeDtypeStruct((1, LANES), F32),
                   jax.ShapeDtypeStruct((1, LANES), F32)],
        scratch_shapes=[pltpu.VMEM((DIL_SPAN, wide), F32), pltpu.VMEM((DIL_SPAN, wide), F32)],
        compiler_params=_cparams(("arbitrary", "arbitrary")),
        name=name,
    )(view, view, view, stat_view, wq_row, wk_row, dwq_in, dwk_in)
    return dslab.reshape(t, DIL_SLAB), dwq, dwk


def _row(v, width=LANES):
    v = v.astype(F32).reshape(-1)
    return jnp.pad(v, (0, width - v.shape[0])).reshape(1, width)


def _prepare_weights(w):
    d = D_MODEL
    gdn, dil, ffn = [], [], []
    for j in range(DEPTH // 2):
        win = w["gdn_w_in"][j]
        qkv, z = win[:, :GDN_QKV], win[:, GDN_QKV:GDN_MAIN]
        ab = jnp.pad(win[:, GDN_MAIN:], ((0, 0), (0, LANES - 2 * GDN_HEADS)))
        wout = w["gdn_w_out"][j]
        gdn.append(dict(main=win[:, :GDN_MAIN], ab=ab, qkv_t=qkv.T, z_t=z.T, ab_t=ab.T, out=wout, out_t=wout.T,
                        conv=w["gdn_conv_w"][j].astype(F32), alog=_row(w["gdn_a_log"][j]), dt=_row(w["gdn_dt_bias"][j]),
                        nw=_row(w["gdn_norm_w"][j])))
        win = w["dil_w_in"][j].reshape(d, 3, len(DIL_GROUPS), DIL_HEADS, DIL_DH)
        wg = [jnp.pad(win[:, :, g], ((0, 0), (0, 0), (0, 0), (0, LANES - DIL_DH))).reshape(d, DIL_SLAB)
              for g in range(len(DIL_GROUPS))]
        wout = jnp.pad(w["dil_w_out"][j].reshape(DIL_HEADS, DIL_DH, d), ((0, 0), (0, LANES - DIL_DH), (0, 0)))
        wout = wout.reshape(DIL_HEADS * LANES, d)
        dil.append(dict(wg=wg, wg_t=[m.T for m in wg], out=wout, out_t=wout.T, wq=_row(w["dil_q_norm"][j]),
                        wk=_row(w["dil_k_norm"][j])))
    for i in range(DEPTH):
        win, wout = w["ffn_w_in"][i], w["ffn_w_out"][i]
        ffn.append(dict(win=win, g_t=win[:, :FFN_HIDDEN].T, u_t=win[:, FFN_HIDDEN:].T, out=wout, out_t=wout.T))
    return dict(gdn=gdn, dil=dil, ffn=ffn)


def _gdn_layer_fwd(x, nrow, p):
    hn = _rmsnorm_fwd(x, nrow, name="rmsnorm_fwd")
    pm = _matmul(hn, p["main"], name="gdn_proj_main")
    ab = _matmul(hn, p["ab"], name="gdn_proj_gates")
    qkv = _gdn_conv_fwd(pm, p["conv"], name="gdn_conv_fwd")
    gb, bb = _gdn_gates_fwd(ab, p["alog"], p["dt"], name="gdn_gates_fwd")
    prep = _gdn_prep_fwd(qkv, gb, bb, name="gdn_prep_fwd")
    o, states = _gdn_scan_fwd(prep, name="gdn_scan_fwd")
    og = _gdn_outgate_fwd(o, pm, p["nw"], name="gdn_outgate_fwd")
    y = _matmul(og, p["out"], add=x, name="gdn_proj_out")
    return y, (x, hn, pm, ab, qkv, gb, bb, prep, states, o, og)


def _gdn_layer_bwd(dx, nrow, p, saved):
    x, hn, pm, ab, qkv, gb, bb, prep, states, o, og = saved
    dxb = dx.astype(MM_DTYPE)
    d_og = _matmul(dxb, p["out_t"], name="gdn_dgate")
    g_out = _matmul(og, dxb, trans_a=True, name="gdn_gw_out")
    d_o, d_z, d_nw = _gdn_outgate_bwd(o, pm, p["nw"], d_og, name="gdn_outgate_bwd")
    cts = _gdn_scan_bwd(prep, states, d_o, name="gdn_scan_bwd")
    dqkv, dgb, dbb = _gdn_prep_bwd(qkv, gb, bb, cts, name="gdn_prep_bwd")
    d_ab, d_alog, d_dt = _gdn_gates_bwd(ab, p["alog"], p["dt"], dgb, dbb, name="gdn_gates_bwd")
    d_conv, g_conv = _gdn_conv_bwd(pm, p["conv"], dqkv, name="gdn_conv_bwd")
    d_hn = _matmul(d_conv, p["qkv_t"], name="gdn_dhn_qkv")
    d_hn = _matmul(d_z, p["z_t"], add=d_hn, name="gdn_dhn_z")
    d_hn = _matmul(d_ab, p["ab_t"], add=d_hn, name="gdn_dhn_gates")
    g_in = jnp.concatenate([
        _matmul(hn, d_conv, trans_a=True, name="gdn_gw_qkv"),
        _matmul(hn, d_z, trans_a=True, name="gdn_gw_z"),
        _matmul(hn, d_ab, trans_a=True, name="gdn_gw_gates")[:, :2 * GDN_HEADS],
    ], axis=1)
    dx_new, g_norm = _rmsnorm_bwd(x, nrow, d_hn, dx, name="rmsnorm_bwd")
    grads = dict(w_in=g_in, conv=g_conv, a_log=d_alog[0, :GDN_HEADS], dt_bias=d_dt[0, :GDN_HEADS], norm_w=d_nw[0],
                 w_out=g_out, norm=g_norm[0])
    return dx_new, grads


def _dil_layer_fwd(x, nrow, p):
    hn = _rmsnorm_fwd(x, nrow, name="rmsnorm_fwd")
    slabs = [_matmul(hn, p["wg"][g], name="dil_proj_in") for g in range(len(DIL_GROUPS))]
    oe = [_dil_attn_fwd(slabs[g], p["wq"], p["wk"], group=g, name=f"dil_attn_fwd_g{g}") for g in range(len(DIL_GROUPS))]
    y, om = _dil_merge_fwd(oe, name="dil_merge_fwd")
    out = _matmul(y, p["out"], add=x, name="dil_proj_out")
    return out, (x, hn, slabs, y, om)


def _dil_layer_bwd(dx, nrow, p, saved):
    x, hn, slabs, y, om = saved
    dxb = dx.astype(MM_DTYPE)
    d_y = _matmul(dxb, p["out_t"], name="dil_dmerged")
    g_out = _matmul(y, dxb, trans_a=True, name="dil_gw_out")
    g_out = g_out.reshape(DIL_HEADS, LANES, D_MODEL)[:, :DIL_DH].reshape(DIL_HEADS * DIL_DH, D_MODEL)
    stat = _dil_merge_bwd(d_y, om, name="dil_merge_bwd")
    d_hn = None
    dwq = jnp.zeros((1, LANES), F32)
    dwk = jnp.zeros((1, LANES), F32)
    g_groups = []
    for g in range(len(DIL_GROUPS)):
        dslab, dwq, dwk = _dil_attn_bwd(slabs[g], stat, p["wq"], p["wk"], dwq, dwk, group=g, name=f"dil_attn_bwd_g{g}")
        d_hn = _matmul(dslab, p["wg_t"][g], add=d_hn, name="dil_dhn")
        g_w = _matmul(hn, dslab, trans_a=True, name="dil_gw_in")
        g_groups.append(g_w.reshape(D_MODEL, 3, DIL_HEADS, LANES)[..., :DIL_DH])
    g_in = jnp.stack(g_groups, axis=2).reshape(D_MODEL, 3 * len(DIL_GROUPS) * DIL_HEADS * DIL_DH)
    dx_new, g_norm = _rmsnorm_bwd(x, nrow, d_hn, dx, name="rmsnorm_bwd")
    grads = dict(w_in=g_in, q_norm=dwq[0, :DIL_DH], k_norm=dwk[0, :DIL_DH], w_out=g_out, norm=g_norm[0])
    return dx_new, grads


def _ffn_layer_fwd(x, nrow, p):
    hn = _rmsnorm_fwd(x, nrow, name="rmsnorm_fwd")
    gu = _matmul(hn, p["win"], name="ffn_proj_in")
    act = _swiglu_fwd(gu, name="swiglu_fwd")
    y = _matmul(act, p["out"], add=x, name="ffn_proj_out")
    return y, (x, hn, gu, act)


def _ffn_layer_bwd(dx, nrow, p, saved):
    x, hn, gu, act = saved
    dxb = dx.astype(MM_DTYPE)
    d_act = _matmul(dxb, p["out_t"], name="ffn_dact")
    g_out = _matmul(act, dxb, trans_a=True, name="ffn_gw_out")
    d_g, d_u = _swiglu_bwd(gu, d_act, name="swiglu_bwd")
    d_hn = _matmul(d_g, p["g_t"], name="ffn_dhn_gate")
    d_hn = _matmul(d_u, p["u_t"], add=d_hn, name="ffn_dhn_up")
    g_in = jnp.concatenate([_matmul(hn, d_g, trans_a=True, name="ffn_gw_gate"),
                            _matmul(hn, d_u, trans_a=True, name="ffn_gw_up")], axis=1)
    dx_new, g_norm = _rmsnorm_bwd(x, nrow, d_hn, dx, name="rmsnorm_bwd")
    return dx_new, dict(w_in=g_in, w_out=g_out, norm=g_norm[0])


def _local_step(x, target, prepared, norm_mix, norm_ffn):
    saved = []
    for i in range(DEPTH):
        j = i // 2
        mix_row = norm_mix[i].reshape(1, D_MODEL)
        if i % 2 == 0:
            x, s_mix = _gdn_layer_fwd(x, mix_row, prepared["gdn"][j])
        else:
            x, s_mix = _dil_layer_fwd(x, mix_row, prepared["dil"][j])
        x, s_ffn = _ffn_layer_fwd(x, norm_ffn[i].reshape(1, D_MODEL), prepared["ffn"][i])
        saved.append((s_mix, s_ffn))
    dx, loss = _loss_head(x, target, name="loss_head")
    g_mix, g_ffn = [None] * DEPTH, [None] * DEPTH
    for i in reversed(range(DEPTH)):
        j = i // 2
        s_mix, s_ffn = saved[i]
        dx, g_ffn[i] = _ffn_layer_bwd(dx, norm_ffn[i].reshape(1, D_MODEL), prepared["ffn"][i], s_ffn)
        mix_row = norm_mix[i].reshape(1, D_MODEL)
        if i % 2 == 0:
            dx, g_mix[i] = _gdn_layer_bwd(dx, mix_row, prepared["gdn"][j], s_mix)
        else:
            dx, g_mix[i] = _dil_layer_bwd(dx, mix_row, prepared["dil"][j], s_mix)
    gdn = [g_mix[i] for i in range(0, DEPTH, 2)]
    dil = [g_mix[i] for i in range(1, DEPTH, 2)]
    grads = dict(
        norm_mix=jnp.stack([g["norm"] for g in g_mix]),
        norm_ffn=jnp.stack([g["norm"] for g in g_ffn]),
        gdn_w_in=jnp.stack([g["w_in"] for g in gdn]),
        gdn_conv_w=jnp.stack([g["conv"] for g in gdn]),
        gdn_a_log=jnp.stack([g["a_log"] for g in gdn]),
        gdn_dt_bias=jnp.stack([g["dt_bias"] for g in gdn]),
        gdn_norm_w=jnp.stack([g["norm_w"] for g in gdn]),
        gdn_w_out=jnp.stack([g["w_out"] for g in gdn]),
        dil_w_in=jnp.stack([g["w_in"] for g in dil]),
        dil_q_norm=jnp.stack([g["q_norm"] for g in dil]),
        dil_k_norm=jnp.stack([g["k_norm"] for g in dil]),
        dil_w_out=jnp.stack([g["w_out"] for g in dil]),
        ffn_w_in=jnp.stack([g["w_in"] for g in g_ffn]),
        ffn_w_out=jnp.stack([g["w_out"] for g in g_ffn]),
    )
    return loss[0, 0], dx, grads


MESH_ID = pl.DeviceIdType.MESH
ANY_SPACE = pl.BlockSpec(memory_space=pl.ANY)


def _mesh_position():
    return lax.axis_index("x"), lax.axis_index("y"), lax.axis_index("c")


def _flip(pos, k):
    x, y, c = pos
    return (1 - x if k & 4 else x, 1 - y if k & 2 else y, 1 - c if k & 1 else c)


def _linear(pos):
    return 4 * pos[0] + 2 * pos[1] + pos[2]


def _comm_scratch():
    return [pltpu.SemaphoreType.DMA((N_DEV - 1,)), pltpu.SemaphoreType.DMA((N_DEV - 1,)), pltpu.SemaphoreType.DMA(())]


def _all_gather(shard, *, name):
    def body(x_ref, out_ref, send_sems, recv_sems, local_sem):
        me = _mesh_position()
        mine = out_ref.at[_linear(me)]
        local = pltpu.make_async_copy(x_ref, mine, local_sem)
        local.start()
        copies = []
        for k in range(1, N_DEV):
            cp = pltpu.make_async_remote_copy(src_ref=x_ref, dst_ref=mine, send_sem=send_sems.at[k - 1],
                                              recv_sem=recv_sems.at[k - 1], device_id=_flip(me, k), device_id_type=MESH_ID)
            cp.start()
            copies.append(cp)
        for cp in copies:
            cp.wait()
        local.wait()

    return pl.pallas_call(
        body,
        out_shape=jax.ShapeDtypeStruct((N_DEV,) + shard.shape, shard.dtype),
        in_specs=[ANY_SPACE],
        out_specs=ANY_SPACE,
        scratch_shapes=_comm_scratch(),
        name=name,
    )(shard)


def _exchange(parts, *, name):
    def body(p_ref, out_ref, send_sems, recv_sems, local_sem):
        me = _mesh_position()
        mine = out_ref.at[_linear(me)]
        local = pltpu.make_async_copy(p_ref.at[_linear(me)], mine, local_sem)
        local.start()
        copies = []
        for k in range(1, N_DEV):
            peer = _flip(me, k)
            cp = pltpu.make_async_remote_copy(src_ref=p_ref.at[_linear(peer)], dst_ref=mine, send_sem=send_sems.at[k - 1],
                                              recv_sem=recv_sems.at[k - 1], device_id=peer, device_id_type=MESH_ID)
            cp.start()
            copies.append(cp)
        for cp in copies:
            cp.wait()
        local.wait()

    return pl.pallas_call(
        body,
        out_shape=jax.ShapeDtypeStruct(parts.shape, parts.dtype),
        in_specs=[ANY_SPACE],
        out_specs=ANY_SPACE,
        scratch_shapes=_comm_scratch(),
        name=name,
    )(parts)


def _adamw(parts, w, m, v, *, name):
    rows, n = w.shape
    tb = _pick(rows, (400, 16))
    c1 = 1.0 - ADAM_B1 ** ADAM_STEP
    c2 = 1.0 - ADAM_B2 ** ADAM_STEP

    def body(p_ref, w_ref, m_ref, v_ref, g_ref, d_ref, nm_ref, nv_ref):
        g = p_ref[0].astype(F32)
        for s in range(1, N_DEV):
            g = g + p_ref[s].astype(F32)
        m_new = ADAM_B1 * m_ref[...] + (1.0 - ADAM_B1) * g
        v_new = ADAM_B2 * v_ref[...] + (1.0 - ADAM_B2) * (g * g)
        m_hat = m_new / c1
        v_hat = v_new / c2
        g_ref[...] = g
        nm_ref[...] = m_new
        nv_ref[...] = v_new
        d_ref[...] = -ADAM_LR * (m_hat / (jnp.sqrt(v_hat) + ADAM_EPS) + ADAM_WD * w_ref[...])

    blk = pl.BlockSpec((tb, n), lambda i: (i, 0))
    return pl.pallas_call(
        body,
        grid=(rows // tb,),
        in_specs=[pl.BlockSpec((N_DEV, tb, n), lambda i: (0, i, 0)), blk, blk, blk],
        out_specs=[blk] * 4,
        out_shape=[jax.ShapeDtypeStruct((rows, n), F32)] * 4,
        compiler_params=_cparams(("parallel",)),
        name=name,
    )(parts, w, m, v)


PACK_WIDTH = 1024
SHARDED = {
    "gdn_w_in": ((2, D_MODEL, GDN_IN_WIDTH), 2),
    "gdn_conv_w": ((2, GDN_CONV, GDN_QKV), 2),
    "gdn_w_out": ((2, GDN_HEADS * GDN_DV, D_MODEL), 1),
    "dil_w_in": ((2, D_MODEL, 3 * len(DIL_GROUPS) * DIL_HEADS * DIL_DH), 2),
    "dil_w_out": ((2, DIL_HEADS * DIL_DH, D_MODEL), 2),
    "ffn_w_in": ((DEPTH, D_MODEL, 2 * FFN_HIDDEN), 2),
    "ffn_w_out": ((DEPTH, FFN_HIDDEN, D_MODEL), 1),
}
REPLICATED = {"norm_mix": (DEPTH, D_MODEL), "norm_ffn": (DEPTH, D_MODEL), "gdn_a_log": (2, GDN_HEADS),
              "gdn_dt_bias": (2, GDN_HEADS), "gdn_norm_w": (2, GDN_DV), "dil_q_norm": (2, DIL_DH), "dil_k_norm": (2, DIL_DH)}
WEIGHT_ORDER = ("norm_mix", "norm_ffn", "gdn_w_in", "gdn_conv_w", "gdn_a_log", "gdn_dt_bias", "gdn_norm_w", "gdn_w_out",
                "dil_w_in", "dil_q_norm", "dil_k_norm", "dil_w_out", "ffn_w_in", "ffn_w_out")
PACK_ROW_ALIGN = 400
SMALL_ROWS = 16


def _shard_shape(name):
    shape, axis = SHARDED[name]
    return tuple(s // N_DEV if i == axis else s for i, s in enumerate(shape))


def _shard_rows(name):
    return math.prod(_shard_shape(name)) // PACK_WIDTH


def _padded_rows(rows):
    return -(-rows // PACK_ROW_ALIGN) * PACK_ROW_ALIGN


def _split_shards(full, name):
    shape, axis = SHARDED[name]
    split = full.reshape(shape[:axis] + (N_DEV, shape[axis] // N_DEV) + shape[axis + 1:])
    return jnp.moveaxis(split, axis, 0)


def _join_shards(stacked, name):
    shape, axis = SHARDED[name]
    return jnp.moveaxis(stacked, 0, axis).reshape(shape)


def _pack_rows(pieces, lead=()):
    flat = [p.reshape(lead + (-1, PACK_WIDTH)) for p in pieces]
    buf = jnp.concatenate(flat, axis=len(lead))
    rows = buf.shape[len(lead)]
    pad = [(0, 0)] * len(lead) + [(0, _padded_rows(rows) - rows), (0, 0)]
    return jnp.pad(buf, pad)


def _unpack_rows(buf, names, lead=()):
    out, at = {}, 0
    for n in names:
        rows = _shard_rows(n)
        out[n] = buf[..., at:at + rows, :].reshape(lead + _shard_shape(n))
        at += rows
    return out


def _pack_small(vals):
    tail = jnp.concatenate([vals[n].astype(F32).reshape(-1) for n in REPLICATED if n not in ("norm_mix", "norm_ffn")])
    tail = jnp.pad(tail, (0, PACK_WIDTH - tail.shape[0])).reshape(1, PACK_WIDTH)
    buf = jnp.concatenate([vals["norm_mix"].astype(F32), vals["norm_ffn"].astype(F32), tail], axis=0)
    return jnp.pad(buf, ((0, SMALL_ROWS - buf.shape[0]), (0, 0)))


def _unpack_small(buf):
    out = {"norm_mix": buf[0:DEPTH], "norm_ffn": buf[DEPTH:2 * DEPTH]}
    at = 0
    for n, shape in REPLICATED.items():
        if n in out:
            continue
        size = math.prod(shape)
        out[n] = buf[2 * DEPTH, at:at + size].reshape(shape)
        at += size
    return out


def kernel(x, norm_mix, norm_ffn, gdn_w_in, gdn_conv_w, gdn_a_log, gdn_dt_bias, gdn_norm_w, gdn_w_out, dil_w_in, dil_q_norm, dil_k_norm, dil_w_out, ffn_w_in, ffn_w_out, loss_target, m_norm_mix, m_norm_ffn, m_gdn_w_in, m_gdn_conv_w, m_gdn_a_log, m_gdn_dt_bias, m_gdn_norm_w, m_gdn_w_out, m_dil_w_in, m_dil_q_norm, m_dil_k_norm, m_dil_w_out, m_ffn_w_in, m_ffn_w_out, v_norm_mix, v_norm_ffn, v_gdn_w_in, v_gdn_conv_w, v_gdn_a_log, v_gdn_dt_bias, v_gdn_norm_w, v_gdn_w_out, v_dil_w_in, v_dil_q_norm, v_dil_k_norm, v_dil_w_out, v_ffn_w_in, v_ffn_w_out):
    w = dict(norm_mix=norm_mix, norm_ffn=norm_ffn, gdn_w_in=gdn_w_in, gdn_conv_w=gdn_conv_w, gdn_a_log=gdn_a_log,
             gdn_dt_bias=gdn_dt_bias, gdn_norm_w=gdn_norm_w, gdn_w_out=gdn_w_out, dil_w_in=dil_w_in, dil_q_norm=dil_q_norm,
             dil_k_norm=dil_k_norm, dil_w_out=dil_w_out, ffn_w_in=ffn_w_in, ffn_w_out=ffn_w_out)
    m = dict(norm_mix=m_norm_mix, norm_ffn=m_norm_ffn, gdn_w_in=m_gdn_w_in, gdn_conv_w=m_gdn_conv_w, gdn_a_log=m_gdn_a_log,
             gdn_dt_bias=m_gdn_dt_bias, gdn_norm_w=m_gdn_norm_w, gdn_w_out=m_gdn_w_out, dil_w_in=m_dil_w_in,
             dil_q_norm=m_dil_q_norm, dil_k_norm=m_dil_k_norm, dil_w_out=m_dil_w_out, ffn_w_in=m_ffn_w_in, ffn_w_out=m_ffn_w_out)
    v = dict(norm_mix=v_norm_mix, norm_ffn=v_norm_ffn, gdn_w_in=v_gdn_w_in, gdn_conv_w=v_gdn_conv_w, gdn_a_log=v_gdn_a_log,
             gdn_dt_bias=v_gdn_dt_bias, gdn_norm_w=v_gdn_norm_w, gdn_w_out=v_gdn_w_out, dil_w_in=v_dil_w_in,
             dil_q_norm=v_dil_q_norm, dil_k_norm=v_dil_k_norm, dil_w_out=v_dil_w_out, ffn_w_in=v_ffn_w_in, ffn_w_out=v_ffn_w_out)
    big = tuple(SHARDED)

    def as_operand(name, a):
        if name == "gdn_conv_w":
            return lax.bitcast_convert_type(a, BF16)
        return a.astype(BF16)

    gathered = _all_gather(_pack_rows([as_operand(n, w[n]) for n in big]), name="weight_all_gather")
    full, at = {}, 0
    for n in big:
        rows = _shard_rows(n) * (2 if n == "gdn_conv_w" else 1)
        piece = gathered[:, at:at + rows, :]
        at += rows
        if n == "gdn_conv_w":
            piece = lax.bitcast_convert_type(piece.reshape((N_DEV,) + _shard_shape(n) + (2,)), F32)
        else:
            piece = piece.reshape((N_DEV,) + _shard_shape(n))
        full[n] = _join_shards(piece, n)
    for n in REPLICATED:
        full[n] = w[n]
    prepared = _prepare_weights(full)

    loss, grad_x, grads = _local_step(x[0], loss_target[0], prepared, norm_mix, norm_ffn)

    parts = _pack_rows([_split_shards(grads[n], n).astype(BF16) for n in big], lead=(N_DEV,))
    received = _exchange(parts, name="grad_exchange")
    packed = [_pack_rows([src[n].astype(F32) for n in big]) for src in (w, m, v)]
    outs_big = [_unpack_rows(o, big) for o in _adamw(received, *packed, name="adamw_sharded")]

    small_parts = _all_gather(_pack_small(grads), name="small_grad_all_gather")
    outs_small = [_unpack_small(o) for o in
                  _adamw(small_parts, _pack_small(w), _pack_small(m), _pack_small(v), name="adamw_replicated")]

    total_loss = lax.psum(loss, ("x", "y", "c"))
    result = [total_loss, grad_x[None]]
    for k in range(4):
        for n in WEIGHT_ORDER:
            result.append(outs_big[k][n] if n in SHARDED else outs_small[k][n])
    return tuple(result)
```

```python
import functools
import math

import jax
import jax.numpy as jnp
from jax import lax
from jax.experimental import pallas as pl
from jax.experimental.pallas import tpu as pltpu

F32 = jnp.float32
BF16 = jnp.bfloat16
MM_DTYPE = BF16

N_DEV = 8
D_MODEL = 1024
DEPTH = 4
RMS_EPS = 1e-6
L2_EPS = 1e-6

LANES = 128

GDN_HEADS = 8
GDN_DK = 128
GDN_DV = 128
GDN_CONV = 4
GDN_CHUNK = 128
GDN_QKV = 3 * GDN_HEADS * GDN_DK
GDN_MAIN = GDN_QKV + GDN_HEADS * GDN_DV
GDN_IN_WIDTH = GDN_MAIN + 2 * GDN_HEADS

DIL_GROUPS = ((128, 1), (512, 4), (2048, 16))
DIL_HEADS = 8
DIL_DH = 64
DIL_SPAN = 128
DIL_SLAB = 3 * DIL_HEADS * LANES
ALIBI_MAX_BIAS = 8.0

FFN_HIDDEN = 2816

ADAM_LR = 0.001
ADAM_B1 = 0.9
ADAM_B2 = 0.999
ADAM_EPS = 1e-08
ADAM_WD = 0.01
ADAM_STEP = 10

VMEM_LIMIT = 56 * 1024 * 1024
NEG = -1e30
HI = lax.Precision.HIGHEST


def _cparams(sem):
    return pltpu.CompilerParams(dimension_semantics=sem, vmem_limit_bytes=VMEM_LIMIT)


def _dot(a, b):
    return lax.dot_general(a, b, (((1,), (0,)), ((), ())), preferred_element_type=F32, precision=HI)


def _dot_nt(a, b):
    return lax.dot_general(a, b, (((1,), (1,)), ((), ())), preferred_element_type=F32, precision=HI)


def _dot_tn(a, b):
    return lax.dot_general(a, b, (((0,), (0,)), ((), ())), preferred_element_type=F32, precision=HI)


def _bdot(a, b):
    return lax.dot_general(a.astype(BF16), b.astype(BF16), (((1,), (0,)), ((), ())), preferred_element_type=F32)


def _bdot_nt(a, b):
    return lax.dot_general(a.astype(BF16), b.astype(BF16), (((1,), (1,)), ((), ())), preferred_element_type=F32)


def _bdot_tn(a, b):
    return lax.dot_general(a.astype(BF16), b.astype(BF16), (((0,), (0,)), ((), ())), preferred_element_type=F32)


def _pick(n, candidates):
    for c in candidates:
        if n % c == 0:
            return c
    raise ValueError(f"no tile for {n}")


def _matmul(a, b, *, name, trans_a=False, add=None, out_dtype=F32):
    if trans_a:
        k_dim, m_dim = a.shape
    else:
        m_dim, k_dim = a.shape
    k2, n_dim = b.shape
    assert k_dim == k2, (a.shape, b.shape)
    tn = _pick(n_dim, (1024, 512, 256, 128))
    tm = min(m_dim, 2048, max(512, (1024 * 1024) // tn))
    tm = _pick(m_dim, (tm, 1024, 512, 256, 128))
    tk = _pick(k_dim, (1024, 1408, 512, 256, 128))
    nk = k_dim // tk
    has_add = add is not None
    dn = (((0,), (0,)), ((), ())) if trans_a else (((1,), (0,)), ((), ()))

    def body(*refs):
        if has_add:
            a_ref, b_ref, add_ref, o_ref, acc_ref = refs
        else:
            a_ref, b_ref, o_ref, acc_ref = refs
        part = lax.dot_general(a_ref[...], b_ref[...], dn, preferred_element_type=F32)

        def finish(total):
            if has_add:
                total = total + add_ref[...]
            o_ref[...] = total.astype(out_dtype)

        if nk == 1:
            finish(part)
        else:
            k = pl.program_id(2)

            @pl.when(k == 0)
            def _():
                acc_ref[...] = part

            @pl.when(k > 0)
            def _():
                acc_ref[...] += part

            @pl.when(k == nk - 1)
            def _():
                finish(acc_ref[...])

    if trans_a:
        a_spec = pl.BlockSpec((tk, tm), lambda i, j, k: (k, i))
    else:
        a_spec = pl.BlockSpec((tm, tk), lambda i, j, k: (i, k))
    in_specs = [a_spec, pl.BlockSpec((tk, tn), lambda i, j, k: (k, j))]
    args = [a, b]
    if has_add:
        in_specs.append(pl.BlockSpec((tm, tn), lambda i, j, k: (i, j)))
        args.append(add)
    return pl.pallas_call(
        body,
        grid=(m_dim // tm, n_dim // tn, nk),
        in_specs=in_specs,
        out_specs=pl.BlockSpec((tm, tn), lambda i, j, k: (i, j)),
        out_shape=jax.ShapeDtypeStruct((m_dim, n_dim), out_dtype),
        scratch_shapes=[pltpu.VMEM((tm, tn) if nk > 1 else (8, LANES), F32)],
        compiler_params=_cparams(("parallel", "parallel", "arbitrary")),
        name=name,
    )(*args)


def _rmsnorm_fwd(x, w_row, *, name):
    t, d = x.shape
    tb = min(t, 1024)

    def body(x_ref, w_ref, o_ref):
        xf = x_ref[...]
        r = lax.rsqrt(jnp.mean(xf * xf, axis=-1, keepdims=True) + RMS_EPS)
        o_ref[...] = (xf * r * w_ref[...]).astype(o_ref.dtype)

    return pl.pallas_call(
        body,
        grid=(t // tb,),
        in_specs=[pl.BlockSpec((tb, d), lambda i: (i, 0)), pl.BlockSpec((1, d), lambda i: (0, 0))],
        out_specs=pl.BlockSpec((tb, d), lambda i: (i, 0)),
        out_shape=jax.ShapeDtypeStruct((t, d), MM_DTYPE),
        compiler_params=_cparams(("parallel",)),
        name=name,
    )(x, w_row)


def _rmsnorm_bwd(x, w_row, dy, dskip, *, name):
    t, d = x.shape
    tb = min(t, 512)

    def body(x_ref, w_ref, dy_ref, ds_ref, dx_ref, dw_ref):
        xf = x_ref[...]
        g = dy_ref[...]
        r = lax.rsqrt(jnp.mean(xf * xf, axis=-1, keepdims=True) + RMS_EPS)
        gw = g * w_ref[...]
        proj = jnp.mean(gw * xf, axis=-1, keepdims=True)
        dx_ref[...] = r * gw - xf * (r * r * r * proj) + ds_ref[...]
        part = jnp.sum(g * xf * r, axis=0, keepdims=True)

        @pl.when(pl.program_id(0) == 0)
        def _():
            dw_ref[...] = part

        @pl.when(pl.program_id(0) > 0)
        def _():
            dw_ref[...] += part

    row = pl.BlockSpec((tb, d), lambda i: (i, 0))
    one = pl.BlockSpec((1, d), lambda i: (0, 0))
    return pl.pallas_call(
        body,
        grid=(t // tb,),
        in_specs=[row, one, row, row],
        out_specs=[row, one],
        out_shape=[jax.ShapeDtypeStruct((t, d), F32), jax.ShapeDtypeStruct((1, d), F32)],
        compiler_params=_cparams(("arbitrary",)),
        name=name,
    )(x, w_row, dy, dskip)


def _silu(z):
    return z / (1.0 + jnp.exp(-z))


def _swiglu_fwd(gu, *, name):
    t = gu.shape[0]
    h = FFN_HIDDEN
    tb, tc = min(t, 1024), 256
    nc = h // tc

    def body(g_ref, u_ref, o_ref):
        o_ref[...] = (_silu(g_ref[...]) * u_ref[...]).astype(o_ref.dtype)

    return pl.pallas_call(
        body,
        grid=(t // tb, nc),
        in_specs=[pl.BlockSpec((tb, tc), lambda i, j: (i, j)), pl.BlockSpec((tb, tc), lambda i, j: (i, j + nc))],
        out_specs=pl.BlockSpec((tb, tc), lambda i, j: (i, j)),
        out_shape=jax.ShapeDtypeStruct((t, h), MM_DTYPE),
        compiler_params=_cparams(("parallel", "parallel")),
        name=name,
    )(gu, gu)


def _swiglu_bwd(gu, dact, *, name):
    t = gu.shape[0]
    h = FFN_HIDDEN
    tb, tc = min(t, 1024), 256
    nc = h // tc

    def body(g_ref, u_ref, da_ref, dg_ref, du_ref):
        g = g_ref[...]
        da = da_ref[...]
        sig = 1.0 / (1.0 + jnp.exp(-g))
        sg = g * sig
        dg_ref[...] = (da * u_ref[...] * (sig + sg * (1.0 - sig))).astype(dg_ref.dtype)
        du_ref[...] = (da * sg).astype(du_ref.dtype)

    blk = pl.BlockSpec((tb, tc), lambda i, j: (i, j))
    return pl.pallas_call(
        body,
        grid=(t // tb, nc),
        in_specs=[blk, pl.BlockSpec((tb, tc), lambda i, j: (i, j + nc)), blk],
        out_specs=[blk, blk],
        out_shape=[jax.ShapeDtypeStruct((t, h), MM_DTYPE)] * 2,
        compiler_params=_cparams(("parallel", "parallel")),
        name=name,
    )(gu, gu, dact)


def _loss_head(y, target, *, name):
    t, d = y.shape
    tb = min(t, 1024)

    def body(y_ref, t_ref, dy_ref, l_ref):
        err = y_ref[...] - t_ref[...]
        dy_ref[...] = err * (1.0 / d)
        part = jnp.sum(jnp.sum(err * err, axis=0, keepdims=True), axis=1, keepdims=True) * (0.5 / d)
        part = jnp.broadcast_to(part, l_ref.shape)

        @pl.when(pl.program_id(0) == 0)
        def _():
            l_ref[...] = part

        @pl.when(pl.program_id(0) > 0)
        def _():
            l_ref[...] += part

    row = pl.BlockSpec((tb, d), lambda i: (i, 0))
    return pl.pallas_call(
        body,
        grid=(t // tb,),
        in_specs=[row, row],
        out_specs=[row, pl.BlockSpec((8, LANES), lambda i: (0, 0))],
        out_shape=[jax.ShapeDtypeStruct((t, d), F32), jax.ShapeDtypeStruct((8, LANES), F32)],
        compiler_params=_cparams(("arbitrary",)),
        name=name,
    )(y, target)


CONV_HALO = 8


def _conv_tile_scale(c):
    is_qk = c < 2 * GDN_HEADS
    scale = jnp.where(c < GDN_HEADS, GDN_DK ** -0.5, 1.0).astype(F32)
    return is_qk, scale


def _gdn_conv_fwd(pm, conv_w, *, name):
    t = pm.shape[0]
    tb = min(t, 1024)
    nt = t // tb
    hb = tb // CONV_HALO

    def body(x_ref, xp_ref, w_ref, o_ref):
        c = pl.program_id(0)
        ti = pl.program_id(1)
        prev = jnp.where(ti > 0, xp_ref[...], 0.0)
        xe = jnp.concatenate([prev, x_ref[...]], axis=0)
        w = w_ref[...]
        y = jnp.zeros((tb, LANES), F32)
        for j in range(GDN_CONV):
            off = CONV_HALO - (GDN_CONV - 1) + j
            y = y + w[j:j + 1, :] * xe[off:off + tb, :]
        s = _silu(y)
        is_qk, scale = _conv_tile_scale(c)
        r = lax.rsqrt(jnp.sum(s * s, axis=-1, keepdims=True) + L2_EPS) * scale
        o_ref[...] = s * jnp.where(is_qk, r, 1.0)

    return pl.pallas_call(
        body,
        grid=(GDN_QKV // LANES, nt),
        in_specs=[
            pl.BlockSpec((tb, LANES), lambda c, i: (i, c)),
            pl.BlockSpec((CONV_HALO, LANES), lambda c, i: (jnp.maximum(i * hb - 1, 0), c)),
            pl.BlockSpec((GDN_CONV, LANES), lambda c, i: (0, c)),
        ],
        out_specs=pl.BlockSpec((tb, LANES), lambda c, i: (i, c)),
        out_shape=jax.ShapeDtypeStruct((t, GDN_QKV), F32),
        compiler_params=_cparams(("parallel", "parallel")),
        name=name,
    )(pm, pm, conv_w)


def _gdn_conv_bwd(pm, conv_w, dout, *, name):
    t = pm.shape[0]
    tb = min(t, 1024)
    nt = t // tb
    hb = tb // CONV_HALO
    last_hb = t // CONV_HALO - 1
    ext = tb + CONV_HALO

    def body(x_ref, xp_ref, xn_ref, d_ref, dn_ref, w_ref, dx_ref, dw_ref):
        c = pl.program_id(0)
        ti = pl.program_id(1)
        prev = jnp.where(ti > 0, xp_ref[...], 0.0)
        has_next = ti < nt - 1
        nxt = jnp.where(has_next, xn_ref[...], 0.0)
        xe = jnp.concatenate([prev, x_ref[...], nxt], axis=0)
        de = jnp.concatenate([d_ref[...], jnp.where(has_next, dn_ref[...], 0.0)], axis=0)
        w = w_ref[...]
        y = jnp.zeros((ext, LANES), F32)
        for j in range(GDN_CONV):
            off = CONV_HALO - (GDN_CONV - 1) + j
            y = y + w[j:j + 1, :] * xe[off:off + ext, :]
        sig = 1.0 / (1.0 + jnp.exp(-y))
        s = y * sig
        is_qk, scale = _conv_tile_scale(c)
        r = lax.rsqrt(jnp.sum(s * s, axis=-1, keepdims=True) + L2_EPS)
        n = s * r
        dnrm = de * scale
        ds_qk = r * (dnrm - n * jnp.sum(dnrm * n, axis=-1, keepdims=True))
        ds = jnp.where(is_qk, ds_qk, de)
        dy = ds * (sig + s * (1.0 - sig))
        dx = jnp.zeros((tb, LANES), F32)
        dw_rows = []
        for j in range(GDN_CONV):
            sh = GDN_CONV - 1 - j
            dx = dx + w[j:j + 1, :] * dy[sh:sh + tb, :]
            off = CONV_HALO - (GDN_CONV - 1) + j
            dw_rows.append(jnp.sum(dy[:tb, :] * xe[off:off + tb, :], axis=0, keepdims=True))
        dx_ref[...] = dx.astype(dx_ref.dtype)
        part = jnp.concatenate(dw_rows, axis=0)

        @pl.when(ti == 0)
        def _():
            dw_ref[...] = part

        @pl.when(ti > 0)
        def _():
            dw_ref[...] += part

    main = pl.BlockSpec((tb, LANES), lambda c, i: (i, c))
    prev = pl.BlockSpec((CONV_HALO, LANES), lambda c, i: (jnp.maximum(i * hb - 1, 0), c))
    nxt = pl.BlockSpec((CONV_HALO, LANES), lambda c, i: (jnp.minimum((i + 1) * hb, last_hb), c))
    return pl.pallas_call(
        body,
        grid=(GDN_QKV // LANES, nt),
        in_specs=[main, prev, nxt, main, nxt, pl.BlockSpec((GDN_CONV, LANES), lambda c, i: (0, c))],
        out_specs=[main, pl.BlockSpec((GDN_CONV, LANES), lambda c, i: (0, c))],
        out_shape=[jax.ShapeDtypeStruct((t, GDN_QKV), MM_DTYPE), jax.ShapeDtypeStruct((GDN_CONV, GDN_QKV), F32)],
        compiler_params=_cparams(("parallel", "arbitrary")),
        name=name,
    )(pm, pm, pm, dout, dout, conv_w)


def _head_selector(first_col):
    row = lax.broadcasted_iota(jnp.int32, (LANES, GDN_HEADS * LANES), 0)
    col = lax.broadcasted_iota(jnp.int32, (LANES, GDN_HEADS * LANES), 1)
    return (col // LANES + first_col == row).astype(F32)


def _softplus(x):
    return jnp.maximum(x, 0.0) + jnp.log(1.0 + jnp.exp(-jnp.abs(x)))


def _gdn_gates_fwd(ab, alog_row, dt_row, *, name):
    t = ab.shape[0]
    tb = min(t, 1024)
    wide = GDN_HEADS * LANES

    def body(ab_ref, al_ref, dt_ref, g_ref, b_ref):
        x = ab_ref[...]
        g_cols = -jnp.exp(al_ref[...]) * _softplus(x + dt_ref[...])
        b_cols = 1.0 / (1.0 + jnp.exp(-x))
        g_ref[...] = _dot(g_cols, _head_selector(0))
        b_ref[...] = _dot(b_cols, _head_selector(GDN_HEADS))

    row = pl.BlockSpec((tb, LANES), lambda i: (i, 0))
    one = pl.BlockSpec((1, LANES), lambda i: (0, 0))
    out = pl.BlockSpec((tb, wide), lambda i: (i, 0))
    return pl.pallas_call(
        body,
        grid=(t // tb,),
        in_specs=[row, one, one],
        out_specs=[out, out],
        out_shape=[jax.ShapeDtypeStruct((t, wide), F32)] * 2,
        compiler_params=_cparams(("parallel",)),
        name=name,
    )(ab, alog_row, dt_row)


def _gdn_gates_bwd(ab, alog_row, dt_row, dgb, dbb, *, name):
    t = ab.shape[0]
    tb = min(t, 1024)
    wide = GDN_HEADS * LANES

    def body(ab_ref, al_ref, dt_ref, dg_ref, db_ref, dab_ref, dal_ref, ddt_ref):
        x = ab_ref[...]
        lane = lax.broadcasted_iota(jnp.int32, (tb, LANES), 1)
        dg_cols = _dot_nt(dg_ref[...], _head_selector(0))
        db_cols = _dot_nt(db_ref[...], _head_selector(GDN_HEADS))
        ea = jnp.exp(al_ref[...])
        z = x + dt_ref[...]
        sp = _softplus(z)
        sg = 1.0 / (1.0 + jnp.exp(-z))
        beta = 1.0 / (1.0 + jnp.exp(-x))
        da = jnp.where(lane < GDN_HEADS, dg_cols * (-ea) * sg, 0.0)
        db = jnp.where((lane >= GDN_HEADS) & (lane < 2 * GDN_HEADS), db_cols * beta * (1.0 - beta), 0.0)
        dab_ref[...] = (da + db).astype(dab_ref.dtype)
        p_al = jnp.sum(jnp.where(lane < GDN_HEADS, dg_cols * (-ea) * sp, 0.0), axis=0, keepdims=True)
        p_dt = jnp.sum(da, axis=0, keepdims=True)

        @pl.when(pl.program_id(0) == 0)
        def _():
            dal_ref[...] = p_al
            ddt_ref[...] = p_dt

        @pl.when(pl.program_id(0) > 0)
        def _():
            dal_ref[...] += p_al
            ddt_ref[...] += p_dt

    row = pl.BlockSpec((tb, LANES), lambda i: (i, 0))
    one = pl.BlockSpec((1, LANES), lambda i: (0, 0))
    big = pl.BlockSpec((tb, wide), lambda i: (i, 0))
    return pl.pallas_call(
        body,
        grid=(t // tb,),
        in_specs=[row, one, one, big, big],
        out_specs=[row, one, one],
        out_shape=[jax.ShapeDtypeStruct((t, LANES), MM_DTYPE), jax.ShapeDtypeStruct((1, LANES), F32),
                   jax.ShapeDtypeStruct((1, LANES), F32)],
        compiler_params=_cparams(("arbitrary",)),
        name=name,
    )(ab, alog_row, dt_row, dgb, dbb)


@jax.custom_vjp
def _unit_lower_inverse_rest(n):
    rest = -n
    power = n
    for _ in range(6):
        power = _bdot(power, power)
        rest = rest + power + _bdot(rest, power)
    return rest


def _unit_lower_inverse_rest_fwd(n):
    rest = _unit_lower_inverse_rest(n)
    return rest, rest


def _unit_lower_inverse_rest_bwd(rest, ct):
    left = ct + _bdot_tn(rest, ct)
    return (-(left + _bdot_nt(left, rest)),)


_unit_lower_inverse_rest.defvjp(_unit_lower_inverse_rest_fwd, _unit_lower_inverse_rest_bwd)


def _gdn_prep_math(q, k, v, gb, bb):
    c = GDN_CHUNK
    ri = lax.broadcasted_iota(jnp.int32, (c, c), 0)
    ci = lax.broadcasted_iota(jnp.int32, (c, c), 1)
    causal = ri >= ci
    gc = _dot(causal.astype(F32), gb)
    decay = jnp.exp(jnp.where(causal, gc - gc.T, NEG))
    n = jnp.where(ri > ci, _bdot_nt(k, k) * bb * decay, 0.0)
    rest = _unit_lower_inverse_rest(n)
    eg = jnp.exp(gc)
    rhs_v = v * bb
    rhs_k = k * bb * eg
    u = rhs_v + _bdot(rest, rhs_v)
    w = rhs_k + _bdot(rest, rhs_k)
    qk = _bdot_nt(q, k) * decay
    qd = q * eg
    gl = _dot((ci == c - 1).astype(F32), gc)
    kt = k * jnp.exp(gl - gc)
    cd = jnp.exp(gl)
    return u, w, qk, qd, kt, cd


def _head_tiles(ref, h):
    return ref[:, h * LANES:(h + 1) * LANES]


def _gdn_prep_fwd(qkv, gb, bb, *, name):
    t = qkv.shape[0]
    c = GDN_CHUNK
    wide = GDN_HEADS * LANES

    def body(q_ref, k_ref, v_ref, g_ref, b_ref, *outs):
        for h in range(GDN_HEADS):
            res = _gdn_prep_math(_head_tiles(q_ref, h), _head_tiles(k_ref, h), _head_tiles(v_ref, h),
                                 _head_tiles(g_ref, h), _head_tiles(b_ref, h))
            for o_ref, val in zip(outs, res):
                o_ref[:, h * LANES:(h + 1) * LANES] = val

    blk = lambda off: pl.BlockSpec((c, wide), lambda i: (i, off))
    return pl.pallas_call(
        body,
        grid=(t // c,),
        in_specs=[blk(0), blk(1), blk(2), blk(0), blk(0)],
        out_specs=[blk(0)] * 6,
        out_shape=[jax.ShapeDtypeStruct((t, wide), F32)] * 6,
        compiler_params=_cparams(("parallel",)),
        name=name,
    )(qkv, qkv, qkv, gb, bb)


def _gdn_prep_bwd(qkv, gb, bb, cts, *, name):
    t = qkv.shape[0]
    c = GDN_CHUNK
    wide = GDN_HEADS * LANES

    def body(q_ref, k_ref, v_ref, g_ref, b_ref, c0, c1, c2, c3, c4, c5, dqkv_ref, dg_ref, db_ref):
        for h in range(GDN_HEADS):
            prim = (_head_tiles(q_ref, h), _head_tiles(k_ref, h), _head_tiles(v_ref, h),
                    _head_tiles(g_ref, h), _head_tiles(b_ref, h))
            _, pull = jax.vjp(_gdn_prep_math, *prim)
            dq, dk, dv, dg, db = pull(tuple(_head_tiles(r, h) for r in (c0, c1, c2, c3, c4, c5)))
            dqkv_ref[:, h * LANES:(h + 1) * LANES] = dq
            dqkv_ref[:, wide + h * LANES:wide + (h + 1) * LANES] = dk
            dqkv_ref[:, 2 * wide + h * LANES:2 * wide + (h + 1) * LANES] = dv
            dg_ref[:, h * LANES:(h + 1) * LANES] = dg
            db_ref[:, h * LANES:(h + 1) * LANES] = db

    blk = lambda off: pl.BlockSpec((c, wide), lambda i: (i, off))
    return pl.pallas_call(
        body,
        grid=(t // c,),
        in_specs=[blk(0), blk(1), blk(2), blk(0), blk(0)] + [blk(0)] * 6,
        out_specs=[pl.BlockSpec((c, 3 * wide), lambda i: (i, 0)), blk(0), blk(0)],
        out_shape=[jax.ShapeDtypeStruct((t, 3 * wide), F32), jax.ShapeDtypeStruct((t, wide), F32),
                   jax.ShapeDtypeStruct((t, wide), F32)],
        compiler_params=_cparams(("parallel",)),
        name=name,
    )(qkv, qkv, qkv, gb, bb, *cts)


def _gdn_scan_math(s, u, w, qk, qd, kt, cd):
    v_new = u - _bdot(w, s)
    o = _bdot(qd, s) + _bdot(qk, v_new)
    s_new = s * cd + _bdot_tn(kt, v_new)
    return o, s_new


def _gdn_scan_fwd(prep, *, name):
    t = prep[0].shape[0]
    c = GDN_CHUNK
    wide = GDN_HEADS * LANES

    def body(u_ref, w_ref, qk_ref, qd_ref, kt_ref, cd_ref, o_ref, st_ref, s_ref):
        @pl.when(pl.program_id(0) == 0)
        def _():
            s_ref[...] = jnp.zeros_like(s_ref)

        for h in range(GDN_HEADS):
            s = _head_tiles(s_ref, h)
            st_ref[:, h * LANES:(h + 1) * LANES] = s
            o, s_new = _gdn_scan_math(s, *(_head_tiles(r, h) for r in (u_ref, w_ref, qk_ref, qd_ref, kt_ref, cd_ref)))
            o_ref[:, h * LANES:(h + 1) * LANES] = o
            s_ref[:, h * LANES:(h + 1) * LANES] = s_new

    blk = pl.BlockSpec((c, wide), lambda i: (i, 0))
    return pl.pallas_call(
        body,
        grid=(t // c,),
        in_specs=[blk] * 6,
        out_specs=[blk, blk],
        out_shape=[jax.ShapeDtypeStruct((t, wide), F32)] * 2,
        scratch_shapes=[pltpu.VMEM((GDN_DK, wide), F32)],
        compiler_params=_cparams(("arbitrary",)),
        name=name,
    )(*prep)


def _gdn_scan_bwd(prep, states, do, *, name):
    t = do.shape[0]
    c = GDN_CHUNK
    wide = GDN_HEADS * LANES
    nc = t // c

    def body(u_ref, w_ref, qk_ref, qd_ref, kt_ref, cd_ref, st_ref, do_ref, *rest):
        outs, ds_ref = rest[:6], rest[6]

        @pl.when(pl.program_id(0) == 0)
        def _():
            ds_ref[...] = jnp.zeros_like(ds_ref)

        for h in range(GDN_HEADS):
            prim = (_head_tiles(st_ref, h),) + tuple(
                _head_tiles(r, h) for r in (u_ref, w_ref, qk_ref, qd_ref, kt_ref, cd_ref))
            _, pull = jax.vjp(_gdn_scan_math, *prim)
            grads = pull((_head_tiles(do_ref, h), _head_tiles(ds_ref, h)))
            ds_ref[:, h * LANES:(h + 1) * LANES] = grads[0]
            for o_ref, val in zip(outs, grads[1:]):
                o_ref[:, h * LANES:(h + 1) * LANES] = val

    blk = pl.BlockSpec((c, wide), lambda i: (nc - 1 - i, 0))
    return pl.pallas_call(
        body,
        grid=(nc,),
        in_specs=[blk] * 8,
        out_specs=[blk] * 6,
        out_shape=[jax.ShapeDtypeStruct((t, wide), F32)] * 6,
        scratch_shapes=[pltpu.VMEM((GDN_DK, wide), F32)],
        compiler_params=_cparams(("arbitrary",)),
        name=name,
    )(*prep, states, do)


def _gdn_outgate_math(o, z, nw):
    r = lax.rsqrt(jnp.mean(o * o, axis=-1, keepdims=True) + RMS_EPS)
    return o * r * nw * _silu(z)


def _gdn_outgate_fwd(o, pm, nw_row, *, name):
    t = o.shape[0]
    tb = min(t, 1024)
    z_off = GDN_QKV // LANES

    def body(o_ref, z_ref, nw_ref, y_ref):
        y_ref[...] = _gdn_outgate_math(o_ref[...], z_ref[...], nw_ref[...]).astype(y_ref.dtype)

    return pl.pallas_call(
        body,
        grid=(t // tb, GDN_HEADS),
        in_specs=[pl.BlockSpec((tb, LANES), lambda i, h: (i, h)), pl.BlockSpec((tb, LANES), lambda i, h: (i, h + z_off)),
                  pl.BlockSpec((1, LANES), lambda i, h: (0, 0))],
        out_specs=pl.BlockSpec((tb, LANES), lambda i, h: (i, h)),
        out_shape=jax.ShapeDtypeStruct((t, GDN_HEADS * LANES), MM_DTYPE),
        compiler_params=_cparams(("parallel", "parallel")),
        name=name,
    )(o, pm, nw_row)


def _gdn_outgate_bwd(o, pm, nw_row, dy, *, name):
    t = o.shape[0]
    tb = min(t, 1024)
    z_off = GDN_QKV // LANES

    def body(o_ref, z_ref, nw_ref, dy_ref, do_ref, dz_ref, dnw_ref):
        _, pull = jax.vjp(_gdn_outgate_math, o_ref[...], z_ref[...], nw_ref[...])
        d_o, d_z, d_nw = pull(dy_ref[...])
        do_ref[...] = d_o
        dz_ref[...] = d_z.astype(dz_ref.dtype)
        first = (pl.program_id(0) == 0) & (pl.program_id(1) == 0)

        @pl.when(first)
        def _():
            dnw_ref[...] = d_nw

        @pl.when(jnp.logical_not(first))
        def _():
            dnw_ref[...] += d_nw

    blk = pl.BlockSpec((tb, LANES), lambda i, h: (i, h))
    one = pl.BlockSpec((1, LANES), lambda i, h: (0, 0))
    return pl.pallas_call(
        body,
        grid=(t // tb, GDN_HEADS),
        in_specs=[blk, pl.BlockSpec((tb, LANES), lambda i, h: (i, h + z_off)), one, blk],
        out_specs=[blk, blk, one],
        out_shape=[jax.ShapeDtypeStruct((t, GDN_HEADS * LANES), F32),
                   jax.ShapeDtypeStruct((t, GDN_HEADS * LANES), MM_DTYPE), jax.ShapeDtypeStruct((1, LANES), F32)],
        compiler_params=_cparams(("arbitrary", "arbitrary")),
        name=name,
    )(o, pm, nw_row, dy)


def _rms64(x, w_row):
    return x * lax.rsqrt(jnp.sum(x * x, axis=-1, keepdims=True) * (1.0 / DIL_DH) + RMS_EPS) * w_row


def _alibi_slope(group, head):
    return 2.0 ** (-ALIBI_MAX_BIAS * (group * DIL_HEADS + head + 1) / (len(DIL_GROUPS) * DIL_HEADS))


def _band_logits(qn, kp, kc, slope_d, has_prev):
    qi = lax.broadcasted_iota(jnp.int32, (DIL_SPAN, DIL_SPAN), 0)
    kj = lax.broadcasted_iota(jnp.int32, (DIL_SPAN, DIL_SPAN), 1)
    steps_c = (qi - kj).astype(F32)
    scale = DIL_DH ** -0.5
    sp = _bdot_nt(qn, kp) * scale - slope_d * (steps_c + float(DIL_SPAN))
    sc = _bdot_nt(qn, kc) * scale - slope_d * steps_c
    sp = jnp.where((kj >= qi) & has_prev, sp, NEG)
    sc = jnp.where(kj <= qi, sc, NEG)
    return sp, sc


def _dil_attn_fwd(slab, wq_row, wk_row, *, group, name):
    dilation = DIL_GROUPS[group][1]
    t = slab.shape[0]
    rows = t // dilation
    nlb = rows // DIL_SPAN
    wide = DIL_HEADS * LANES
    view = slab.reshape(rows, dilation * DIL_SLAB)

    def body(q_ref, kc_ref, vc_ref, kp_ref, vp_ref, wq_ref, wk_ref, o_ref):
        has_prev = pl.program_id(1) > 0
        lane = lax.broadcasted_iota(jnp.int32, (DIL_SPAN, LANES), 1)
        for h in range(DIL_HEADS):
            qn = _rms64(_head_tiles(q_ref, h), wq_ref[...])
            kc = _rms64(_head_tiles(kc_ref, h), wk_ref[...])
            kp = _rms64(_head_tiles(kp_ref, h), wk_ref[...])
            sp, sc = _band_logits(qn, kp, kc, _alibi_slope(group, h) * dilation, has_prev)
            m = jnp.maximum(jnp.max(sp, axis=-1, keepdims=True), jnp.max(sc, axis=-1, keepdims=True))
            pp = jnp.exp(sp - m)
            pc = jnp.exp(sc - m)
            l = jnp.sum(pp, axis=-1, keepdims=True) + jnp.sum(pc, axis=-1, keepdims=True)
            o = (_bdot(pp, _head_tiles(vp_ref, h)) + _bdot(pc, _head_tiles(vc_ref, h))) / l
            o_ref[:, h * LANES:(h + 1) * LANES] = jnp.where(lane < DIL_DH, o, m + jnp.log(l))

    cur = lambda part: pl.BlockSpec((DIL_SPAN, wide), lambda r, i: (i, 3 * r + part))
    prv = lambda part: pl.BlockSpec((DIL_SPAN, wide), lambda r, i: (jnp.maximum(i - 1, 0), 3 * r + part))
    one = pl.BlockSpec((1, LANES), lambda r, i: (0, 0))
    out = pl.pallas_call(
        body,
        grid=(dilation, nlb),
        in_specs=[cur(0), cur(1), cur(2), prv(1), prv(2), one, one],
        out_specs=pl.BlockSpec((DIL_SPAN, wide), lambda r, i: (i, r)),
        out_shape=jax.ShapeDtypeStruct((rows, dilation * wide), F32),
        compiler_params=_cparams(("parallel", "parallel")),
        name=name,
    )(view, view, view, view, view, wq_row, wk_row)
    return out.reshape(t, wide)


def _dil_merge_fwd(oe, *, name):
    t = oe[0].shape[0]
    tb = min(t, 1024)

    def body(e0, e1, e2, y_ref, om_ref):
        lane = lax.broadcasted_iota(jnp.int32, (tb, LANES), 1)
        es = [e0[...], e1[...], e2[...]]
        lse = [jnp.sum(jnp.where(lane == DIL_DH, e, 0.0), axis=-1, keepdims=True) for e in es]
        top = jnp.maximum(jnp.maximum(lse[0], lse[1]), lse[2])
        joint = top + jnp.log(jnp.exp(lse[0] - top) + jnp.exp(lse[1] - top) + jnp.exp(lse[2] - top))
        o = sum(jnp.exp(l - joint) * e for l, e in zip(lse, es))
        y_ref[...] = jnp.where(lane < DIL_DH, o, 0.0).astype(y_ref.dtype)
        om_ref[...] = jnp.where(lane < DIL_DH, o, joint)

    blk = pl.BlockSpec((tb, LANES), lambda i, h: (i, h))
    return pl.pallas_call(
        body,
        grid=(t // tb, DIL_HEADS),
        in_specs=[blk] * 3,
        out_specs=[blk, blk],
        out_shape=[jax.ShapeDtypeStruct((t, DIL_HEADS * LANES), MM_DTYPE),
                   jax.ShapeDtypeStruct((t, DIL_HEADS * LANES), F32)],
        compiler_params=_cparams(("parallel", "parallel")),
        name=name,
    )(*oe)


def _dil_merge_bwd(dy, om, *, name):
    t = dy.shape[0]
    tb = min(t, 1024)

    def body(dy_ref, om_ref, st_ref):
        lane = lax.broadcasted_iota(jnp.int32, (tb, LANES), 1)
        d_o = jnp.where(lane < DIL_DH, dy_ref[...], 0.0)
        om_t = om_ref[...]
        delta = jnp.sum(d_o * om_t, axis=-1, keepdims=True)
        st_ref[...] = jnp.where(lane < DIL_DH, d_o, jnp.where(lane == DIL_DH, om_t, jnp.where(lane == DIL_DH + 1, delta, 0.0)))

    blk = pl.BlockSpec((tb, LANES), lambda i, h: (i, h))
    return pl.pallas_call(
        body,
        grid=(t // tb, DIL_HEADS),
        in_specs=[blk, blk],
        out_specs=blk,
        out_shape=jax.ShapeDtypeStruct((t, DIL_HEADS * LANES), F32),
        compiler_params=_cparams(("parallel", "parallel")),
        name=name,
    )(dy, om)


def _rms64_bwd(x, w_row, dy):
    r = lax.rsqrt(jnp.sum(x * x, axis=-1, keepdims=True) * (1.0 / DIL_DH) + RMS_EPS)
    gw = dy * w_row
    dx = r * gw - x * (r * r * r * jnp.sum(gw * x, axis=-1, keepdims=True) * (1.0 / DIL_DH))
    return dx, dy * x * r


def _dil_attn_bwd(slab, stat, wq_row, wk_row, dwq_in, dwk_in, *, group, name):
    dilation = DIL_GROUPS[group][1]
    t = slab.shape[0]
    rows = t // dilation
    nlb = rows // DIL_SPAN
    wide = DIL_HEADS * LANES
    view = slab.reshape(rows, dilation * DIL_SLAB)
    stat_view = stat.reshape(rows, dilation * wide)

    def body(cur_ref, kp_ref, vp_ref, st_ref, wq_ref, wk_ref, dwq_in_ref, dwk_in_ref, d_ref, dwq_ref, dwk_ref,
             dk_carry, dv_carry):
        step = pl.program_id(1)
        has_prev = step < nlb - 1
        first = (pl.program_id(0) == 0) & (step == 0)

        @pl.when(step == 0)
        def _():
            dk_carry[...] = jnp.zeros_like(dk_carry)
            dv_carry[...] = jnp.zeros_like(dv_carry)

        @pl.when(first)
        def _():
            dwq_ref[...] = dwq_in_ref[...]
            dwk_ref[...] = dwk_in_ref[...]

        lane = lax.broadcasted_iota(jnp.int32, (DIL_SPAN, LANES), 1)
        scale = DIL_DH ** -0.5
        dwq = jnp.zeros((1, LANES), F32)
        dwk = jnp.zeros((1, LANES), F32)
        for h in range(DIL_HEADS):
            q_raw = cur_ref[:, h * LANES:(h + 1) * LANES]
            kc_raw = cur_ref[:, wide + h * LANES:wide + (h + 1) * LANES]
            vc = cur_ref[:, 2 * wide + h * LANES:2 * wide + (h + 1) * LANES]
            kp_raw = _head_tiles(kp_ref, h)
            vp = _head_tiles(vp_ref, h)
            st = _head_tiles(st_ref, h)
            d_o = jnp.where(lane < DIL_DH, st, 0.0)
            lse = jnp.sum(jnp.where(lane == DIL_DH, st, 0.0), axis=-1, keepdims=True)
            delta = jnp.sum(jnp.where(lane == DIL_DH + 1, st, 0.0), axis=-1, keepdims=True)
            qn = _rms64(q_raw, wq_ref[...])
            kc = _rms64(kc_raw, wk_ref[...])
            kp = _rms64(kp_raw, wk_ref[...])
            sp, sc = _band_logits(qn, kp, kc, _alibi_slope(group, h) * dilation, has_prev)
            pp = jnp.exp(sp - lse)
            pc = jnp.exp(sc - lse)
            dsp = pp * (_bdot_nt(d_o, vp) - delta) * scale
            dsc = pc * (_bdot_nt(d_o, vc) - delta) * scale
            dqn = _bdot(dsp, kp) + _bdot(dsc, kc)
            dkc_n = _bdot_tn(dsc, qn) + _head_tiles(dk_carry, h)
            dvc = _bdot_tn(pc, d_o) + _head_tiles(dv_carry, h)
            dk_carry[:, h * LANES:(h + 1) * LANES] = _bdot_tn(dsp, qn)
            dv_carry[:, h * LANES:(h + 1) * LANES] = _bdot_tn(pp, d_o)
            dq_raw, dwq_rows = _rms64_bwd(q_raw, wq_ref[...], dqn)
            dk_raw, dwk_rows = _rms64_bwd(kc_raw, wk_ref[...], dkc_n)
            dwq = dwq + jnp.sum(dwq_rows, axis=0, keepdims=True)
            dwk = dwk + jnp.sum(dwk_rows, axis=0, keepdims=True)
            d_ref[:, h * LANES:(h + 1) * LANES] = dq_raw.astype(d_ref.dtype)
            d_ref[:, wide + h * LANES:wide + (h + 1) * LANES] = dk_raw.astype(d_ref.dtype)
            d_ref[:, 2 * wide + h * LANES:2 * wide + (h + 1) * LANES] = dvc.astype(d_ref.dtype)
        dwq_ref[...] += dwq
        dwk_ref[...] += dwk

    blk_i = lambda i: nlb - 1 - i
    cur = pl.BlockSpec((DIL_SPAN, DIL_SLAB), lambda r, i: (blk_i(i), r))
    prv = lambda part: pl.BlockSpec((DIL_SPAN, wide), lambda r, i: (jnp.maximum(blk_i(i) - 1, 0), 3 * r + part))
    one = pl.BlockSpec((1, LANES), lambda r, i: (0, 0))
    dslab, dwq, dwk = pl.pallas_call(
        body,
        grid=(dilation, nlb),
        in_specs=[cur, prv(1), prv(2), pl.BlockSpec((DIL_SPAN, wide), lambda r, i: (blk_i(i), r)), one, one, one, one],
        out_specs=[cur, one, one],
        out_shape=[jax.ShapeDtypeStruct((rows, dilation * DIL_SLAB), MM_DTYPE), jax.ShapeDtypeStruct((1, LANES), F32),
                   jax.ShapeDtypeStruct((1, LANES), F32)],
        scratch_shapes=[pltpu.VMEM((DIL_SPAN, wide), F32), pltpu.VMEM((DIL_SPAN, wide), F32)],
        compiler_params=_cparams(("arbitrary", "arbitrary")),
        name=name,
    )(view, view, view, stat_view, wq_row, wk_row, dwq_in, dwk_in)
    return dslab.reshape(t, DIL_SLAB), dwq, dwk


def _row(v, width=LANES):
    v = v.astype(F32).reshape(-1)
    return jnp.pad(v, (0, width - v.shape[0])).reshape(1, width)


def _prepare_weights(w):
    d = D_MODEL
    gdn, dil, ffn = [], [], []
    for j in range(DEPTH // 2):
        win = w["gdn_w_in"][j]
        qkv, z = win[:, :GDN_QKV], win[:, GDN_QKV:GDN_MAIN]
        ab = jnp.pad(win[:, GDN_MAIN:], ((0, 0), (0, LANES - 2 * GDN_HEADS)))
        wout = w["gdn_w_out"][j]
        gdn.append(dict(main=win[:, :GDN_MAIN], ab=ab, qkv_t=qkv.T, z_t=z.T, ab_t=ab.T, out=wout, out_t=wout.T,
                        conv=w["gdn_conv_w"][j].astype(F32), alog=_row(w["gdn_a_log"][j]), dt=_row(w["gdn_dt_bias"][j]),
                        nw=_row(w["gdn_norm_w"][j])))
        win = w["dil_w_in"][j].reshape(d, 3, len(DIL_GROUPS), DIL_HEADS, DIL_DH)
        wg = [jnp.pad(win[:, :, g], ((0, 0), (0, 0), (0, 0), (0, LANES - DIL_DH))).reshape(d, DIL_SLAB)
              for g in range(len(DIL_GROUPS))]
        wout = jnp.pad(w["dil_w_out"][j].reshape(DIL_HEADS, DIL_DH, d), ((0, 0), (0, LANES - DIL_DH), (0, 0)))
        wout = wout.reshape(DIL_HEADS * LANES, d)
        dil.append(dict(wg=wg, wg_t=[m.T for m in wg], out=wout, out_t=wout.T, wq=_row(w["dil_q_norm"][j]),
                        wk=_row(w["dil_k_norm"][j])))
    for i in range(DEPTH):
        win, wout = w["ffn_w_in"][i], w["ffn_w_out"][i]
        ffn.append(dict(win=win, g_t=win[:, :FFN_HIDDEN].T, u_t=win[:, FFN_HIDDEN:].T, out=wout, out_t=wout.T))
    return dict(gdn=gdn, dil=dil, ffn=ffn)


def _gdn_layer_fwd(x, nrow, p):
    hn = _rmsnorm_fwd(x, nrow, name="rmsnorm_fwd")
    pm = _matmul(hn, p["main"], name="gdn_proj_main")
    ab = _matmul(hn, p["ab"], name="gdn_proj_gates")
    qkv = _gdn_conv_fwd(pm, p["conv"], name="gdn_conv_fwd")
    gb, bb = _gdn_gates_fwd(ab, p["alog"], p["dt"], name="gdn_gates_fwd")
    prep = _gdn_prep_fwd(qkv, gb, bb, name="gdn_prep_fwd")
    o, states = _gdn_scan_fwd(prep, name="gdn_scan_fwd")
    og = _gdn_outgate_fwd(o, pm, p["nw"], name="gdn_outgate_fwd")
    y = _matmul(og, p["out"], add=x, name="gdn_proj_out")
    return y, (x, hn, pm, ab, qkv, gb, bb, prep, states, o, og)


def _gdn_layer_bwd(dx, nrow, p, saved):
    x, hn, pm, ab, qkv, gb, bb, prep, states, o, og = saved
    dxb = dx.astype(MM_DTYPE)
    d_og = _matmul(dxb, p["out_t"], name="gdn_dgate")
    g_out = _matmul(og, dxb, trans_a=True, name="gdn_gw_out")
    d_o, d_z, d_nw = _gdn_outgate_bwd(o, pm, p["nw"], d_og, name="gdn_outgate_bwd")
    cts = _gdn_scan_bwd(prep, states, d_o, name="gdn_scan_bwd")
    dqkv, dgb, dbb = _gdn_prep_bwd(qkv, gb, bb, cts, name="gdn_prep_bwd")
    d_ab, d_alog, d_dt = _gdn_gates_bwd(ab, p["alog"], p["dt"], dgb, dbb, name="gdn_gates_bwd")
    d_conv, g_conv = _gdn_conv_bwd(pm, p["conv"], dqkv, name="gdn_conv_bwd")
    d_hn = _matmul(d_conv, p["qkv_t"], name="gdn_dhn_qkv")
    d_hn = _matmul(d_z, p["z_t"], add=d_hn, name="gdn_dhn_z")
    d_hn = _matmul(d_ab, p["ab_t"], add=d_hn, name="gdn_dhn_gates")
    g_in = jnp.concatenate([
        _matmul(hn, d_conv, trans_a=True, name="gdn_gw_qkv"),
        _matmul(hn, d_z, trans_a=True, name="gdn_gw_z"),
        _matmul(hn, d_ab, trans_a=True, name="gdn_gw_gates")[:, :2 * GDN_HEADS],
    ], axis=1)
    dx_new, g_norm = _rmsnorm_bwd(x, nrow, d_hn, dx, name="rmsnorm_bwd")
    grads = dict(w_in=g_in, conv=g_conv, a_log=d_alog[0, :GDN_HEADS], dt_bias=d_dt[0, :GDN_HEADS], norm_w=d_nw[0],
                 w_out=g_out, norm=g_norm[0])
    return dx_new, grads


def _dil_layer_fwd(x, nrow, p):
    hn = _rmsnorm_fwd(x, nrow, name="rmsnorm_fwd")
    slabs = [_matmul(hn, p["wg"][g], name="dil_proj_in") for g in range(len(DIL_GROUPS))]
    oe = [_dil_attn_fwd(slabs[g], p["wq"], p["wk"], group=g, name=f"dil_attn_fwd_g{g}") for g in range(len(DIL_GROUPS))]
    y, om = _dil_merge_fwd(oe, name="dil_merge_fwd")
    out = _matmul(y, p["out"], add=x, name="dil_proj_out")
    return out, (x, hn, slabs, y, om)


def _dil_layer_bwd(dx, nrow, p, saved):
    x, hn, slabs, y, om = saved
    dxb = dx.astype(MM_DTYPE)
    d_y = _matmul(dxb, p["out_t"], name="dil_dmerged")
    g_out = _matmul(y, dxb, trans_a=True, name="dil_gw_out")
    g_out = g_out.reshape(DIL_HEADS, LANES, D_MODEL)[:, :DIL_DH].reshape(DIL_HEADS * DIL_DH, D_MODEL)
    stat = _dil_merge_bwd(d_y, om, name="dil_merge_bwd")
    d_hn = None
    dwq = jnp.zeros((1, LANES), F32)
    dwk = jnp.zeros((1, LANES), F32)
    g_groups = []
    for g in range(len(DIL_GROUPS)):
        dslab, dwq, dwk = _dil_attn_bwd(slabs[g], stat, p["wq"], p["wk"], dwq, dwk, group=g, name=f"dil_attn_bwd_g{g}")
        d_hn = _matmul(dslab, p["wg_t"][g], add=d_hn, name="dil_dhn")
        g_w = _matmul(hn, dslab, trans_a=True, name="dil_gw_in")
        g_groups.append(g_w.reshape(D_MODEL, 3, DIL_HEADS, LANES)[..., :DIL_DH])
    g_in = jnp.stack(g_groups, axis=2).reshape(D_MODEL, 3 * len(DIL_GROUPS) * DIL_HEADS * DIL_DH)
    dx_new, g_norm = _rmsnorm_bwd(x, nrow, d_hn, dx, name="rmsnorm_bwd")
    grads = dict(w_in=g_in, q_norm=dwq[0, :DIL_DH], k_norm=dwk[0, :DIL_DH], w_out=g_out, norm=g_norm[0])
    return dx_new, grads


def _ffn_layer_fwd(x, nrow, p):
    hn = _rmsnorm_fwd(x, nrow, name="rmsnorm_fwd")
    gu = _matmul(hn, p["win"], name="ffn_proj_in")
    act = _swiglu_fwd(gu, name="swiglu_fwd")
    y = _matmul(act, p["out"], add=x, name="ffn_proj_out")
    return y, (x, hn, gu, act)


def _ffn_layer_bwd(dx, nrow, p, saved):
    x, hn, gu, act = saved
    dxb = dx.astype(MM_DTYPE)
    d_act = _matmul(dxb, p["out_t"], name="ffn_dact")
    g_out = _matmul(act, dxb, trans_a=True, name="ffn_gw_out")
    d_g, d_u = _swiglu_bwd(gu, d_act, name="swiglu_bwd")
    d_hn = _matmul(d_g, p["g_t"], name="ffn_dhn_gate")
    d_hn = _matmul(d_u, p["u_t"], add=d_hn, name="ffn_dhn_up")
    g_in = jnp.concatenate([_matmul(hn, d_g, trans_a=True, name="ffn_gw_gate"),
                            _matmul(hn, d_u, trans_a=True, name="ffn_gw_up")], axis=1)
    dx_new, g_norm = _rmsnorm_bwd(x, nrow, d_hn, dx, name="rmsnorm_bwd")
    return dx_new, dict(w_in=g_in, w_out=g_out, norm=g_norm[0])


def _local_step(x, target, prepared, norm_mix, norm_ffn):
    saved = []
    for i in range(DEPTH):
        j = i // 2
        mix_row = norm_mix[i].reshape(1, D_MODEL)
        if i % 2 == 0:
            x, s_mix = _gdn_layer_fwd(x, mix_row, prepared["gdn"][j])
        else:
            x, s_mix = _dil_layer_fwd(x, mix_row, prepared["dil"][j])
        x, s_ffn = _ffn_layer_fwd(x, norm_ffn[i].reshape(1, D_MODEL), prepared["ffn"][i])
        saved.append((s_mix, s_ffn))
    dx, loss = _loss_head(x, target, name="loss_head")
    g_mix, g_ffn = [None] * DEPTH, [None] * DEPTH
    for i in reversed(range(DEPTH)):
        j = i // 2
        s_mix, s_ffn = saved[i]
        dx, g_ffn[i] = _ffn_layer_bwd(dx, norm_ffn[i].reshape(1, D_MODEL), prepared["ffn"][i], s_ffn)
        mix_row = norm_mix[i].reshape(1, D_MODEL)
        if i % 2 == 0:
            dx, g_mix[i] = _gdn_layer_bwd(dx, mix_row, prepared["gdn"][j], s_mix)
        else:
            dx, g_mix[i] = _dil_layer_bwd(dx, mix_row, prepared["dil"][j], s_mix)
    gdn = [g_mix[i] for i in range(0, DEPTH, 2)]
    dil = [g_mix[i] for i in range(1, DEPTH, 2)]
    grads = dict(
        norm_mix=jnp.stack([g["norm"] for g in g_mix]),
        norm_ffn=jnp.stack([g["norm"] for g in g_ffn]),
        gdn_w_in=jnp.stack([g["w_in"] for g in gdn]),
        gdn_conv_w=jnp.stack([g["conv"] for g in gdn]),
        gdn_a_log=jnp.stack([g["a_log"] for g in gdn]),
        gdn_dt_bias=jnp.stack([g["dt_bias"] for g in gdn]),
        gdn_norm_w=jnp.stack([g["norm_w"] for g in gdn]),
        gdn_w_out=jnp.stack([g["w_out"] for g in gdn]),
        dil_w_in=jnp.stack([g["w_in"] for g in dil]),
        dil_q_norm=jnp.stack([g["q_norm"] for g in dil]),
        dil_k_norm=jnp.stack([g["k_norm"] for g in dil]),
        dil_w_out=jnp.stack([g["w_out"] for g in dil]),
        ffn_w_in=jnp.stack([g["w_in"] for g in g_ffn]),
        ffn_w_out=jnp.stack([g["w_out"] for g in g_ffn]),
    )
    return loss[0, 0], dx, grads


MESH_ID = pl.DeviceIdType.MESH
ANY_SPACE = pl.BlockSpec(memory_space=pl.ANY)


def _mesh_position():
    return lax.axis_index("x"), lax.axis_index("y"), lax.axis_index("c")


def _flip(pos, k):
    x, y, c = pos
    return (1 - x if k & 4 else x, 1 - y if k & 2 else y, 1 - c if k & 1 else c)


def _linear(pos):
    return 4 * pos[0] + 2 * pos[1] + pos[2]


def _comm_scratch():
    return [pltpu.SemaphoreType.DMA((N_DEV - 1,)), pltpu.SemaphoreType.DMA((N_DEV - 1,)), pltpu.SemaphoreType.DMA(())]


def _all_gather(shard, *, name):
    def body(x_ref, out_ref, send_sems, recv_sems, local_sem):
        me = _mesh_position()
        mine = out_ref.at[_linear(me)]
        local = pltpu.make_async_copy(x_ref, mine, local_sem)
        local.start()
        copies = []
        for k in range(1, N_DEV):
            cp = pltpu.make_async_remote_copy(src_ref=x_ref, dst_ref=mine, send_sem=send_sems.at[k - 1],
                                              recv_sem=recv_sems.at[k - 1], device_id=_flip(me, k), device_id_type=MESH_ID)
            cp.start()
            copies.append(cp)
        for cp in copies:
            cp.wait()
        local.wait()

    return pl.pallas_call(
        body,
        out_shape=jax.ShapeDtypeStruct((N_DEV,) + shard.shape, shard.dtype),
        in_specs=[ANY_SPACE],
        out_specs=ANY_SPACE,
        scratch_shapes=_comm_scratch(),
        name=name,
    )(shard)


def _exchange(parts, *, name):
    def body(p_ref, out_ref, send_sems, recv_sems, local_sem):
        me = _mesh_position()
        mine = out_ref.at[_linear(me)]
        local = pltpu.make_async_copy(p_ref.at[_linear(me)], mine, local_sem)
        local.start()
        copies = []
        for k in range(1, N_DEV):
            peer = _flip(me, k)
            cp = pltpu.make_async_remote_copy(src_ref=p_ref.at[_linear(peer)], dst_ref=mine, send_sem=send_sems.at[k - 1],
                                              recv_sem=recv_sems.at[k - 1], device_id=peer, device_id_type=MESH_ID)
            cp.start()
            copies.append(cp)
        for cp in copies:
            cp.wait()
        local.wait()

    return pl.pallas_call(
        body,
        out_shape=jax.ShapeDtypeStruct(parts.shape, parts.dtype),
        in_specs=[ANY_SPACE],
        out_specs=ANY_SPACE,
        scratch_shapes=_comm_scratch(),
        name=name,
    )(parts)


def _adamw(parts, w, m, v, *, name):
    rows, n = w.shape
    tb = _pick(rows, (400, 16))
    c1 = 1.0 - ADAM_B1 ** ADAM_STEP
    c2 = 1.0 - ADAM_B2 ** ADAM_STEP

    def body(p_ref, w_ref, m_ref, v_ref, g_ref, d_ref, nm_ref, nv_ref):
        g = p_ref[0].astype(F32)
        for s in range(1, N_DEV):
            g = g + p_ref[s].astype(F32)
        m_new = ADAM_B1 * m_ref[...] + (1.0 - ADAM_B1) * g
        v_new = ADAM_B2 * v_ref[...] + (1.0 - ADAM_B2) * (g * g)
        m_hat = m_new / c1
        v_hat = v_new / c2
        g_ref[...] = g
        nm_ref[...] = m_new
        nv_ref[...] = v_new
        d_ref[...] = -ADAM_LR * (m_hat / (jnp.sqrt(v_hat) + ADAM_EPS) + ADAM_WD * w_ref[...])

    blk = pl.BlockSpec((tb, n), lambda i: (i, 0))
    return pl.pallas_call(
        body,
        grid=(rows // tb,),
        in_specs=[pl.BlockSpec((N_DEV, tb, n), lambda i: (0, i, 0)), blk, blk, blk],
        out_specs=[blk] * 4,
        out_shape=[jax.ShapeDtypeStruct((rows, n), F32)] * 4,
        compiler_params=_cparams(("parallel",)),
        name=name,
    )(parts, w, m, v)


PACK_WIDTH = 1024
SHARDED = {
    "gdn_w_in": ((2, D_MODEL, GDN_IN_WIDTH), 2),
    "gdn_conv_w": ((2, GDN_CONV, GDN_QKV), 2),
    "gdn_w_out": ((2, GDN_HEADS * GDN_DV, D_MODEL), 1),
    "dil_w_in": ((2, D_MODEL, 3 * len(DIL_GROUPS) * DIL_HEADS * DIL_DH), 2),
    "dil_w_out": ((2, DIL_HEADS * DIL_DH, D_MODEL), 2),
    "ffn_w_in": ((DEPTH, D_MODEL, 2 * FFN_HIDDEN), 2),
    "ffn_w_out": ((DEPTH, FFN_HIDDEN, D_MODEL), 1),
}
REPLICATED = {"norm_mix": (DEPTH, D_MODEL), "norm_ffn": (DEPTH, D_MODEL), "gdn_a_log": (2, GDN_HEADS),
              "gdn_dt_bias": (2, GDN_HEADS), "gdn_norm_w": (2, GDN_DV), "dil_q_norm": (2, DIL_DH), "dil_k_norm": (2, DIL_DH)}
WEIGHT_ORDER = ("norm_mix", "norm_ffn", "gdn_w_in", "gdn_conv_w", "gdn_a_log", "gdn_dt_bias", "gdn_norm_w", "gdn_w_out",
                "dil_w_in", "dil_q_norm", "dil_k_norm", "dil_w_out", "ffn_w_in", "ffn_w_out")
PACK_ROW_ALIGN = 400
SMALL_ROWS = 16


def _shard_shape(name):
    shape, axis = SHARDED[name]
    return tuple(s // N_DEV if i == axis else s for i, s in enumerate(shape))


def _shard_rows(name):
    return math.prod(_shard_shape(name)) // PACK_WIDTH


def _padded_rows(rows):
    return -(-rows // PACK_ROW_ALIGN) * PACK_ROW_ALIGN


def _split_shards(full, name):
    shape, axis = SHARDED[name]
    split = full.reshape(shape[:axis] + (N_DEV, shape[axis] // N_DEV) + shape[axis + 1:])
    return jnp.moveaxis(split, axis, 0)


def _join_shards(stacked, name):
    shape, axis = SHARDED[name]
    return jnp.moveaxis(stacked, 0, axis).reshape(shape)


def _pack_rows(pieces, lead=()):
    flat = [p.reshape(lead + (-1, PACK_WIDTH)) for p in pieces]
    buf = jnp.concatenate(flat, axis=len(lead))
    rows = buf.shape[len(lead)]
    pad = [(0, 0)] * len(lead) + [(0, _padded_rows(rows) - rows), (0, 0)]
    return jnp.pad(buf, pad)


def _unpack_rows(buf, names, lead=()):
    out, at = {}, 0
    for n in names:
        rows = _shard_rows(n)
        out[n] = buf[..., at:at + rows, :].reshape(lead + _shard_shape(n))
        at += rows
    return out


def _pack_small(vals):
    tail = jnp.concatenate([vals[n].astype(F32).reshape(-1) for n in REPLICATED if n not in ("norm_mix", "norm_ffn")])
    tail = jnp.pad(tail, (0, PACK_WIDTH - tail.shape[0])).reshape(1, PACK_WIDTH)
    buf = jnp.concatenate([vals["norm_mix"].astype(F32), vals["norm_ffn"].astype(F32), tail], axis=0)
    return jnp.pad(buf, ((0, SMALL_ROWS - buf.shape[0]), (0, 0)))


def _unpack_small(buf):
    out = {"norm_mix": buf[0:DEPTH], "norm_ffn": buf[DEPTH:2 * DEPTH]}
    at = 0
    for n, shape in REPLICATED.items():
        if n in out:
            continue
        size = math.prod(shape)
        out[n] = buf[2 * DEPTH, at:at + size].reshape(shape)
        at += size
    return out


def kernel(x, norm_mix, norm_ffn, gdn_w_in, gdn_conv_w, gdn_a_log, gdn_dt_bias, gdn_norm_w, gdn_w_out, dil_w_in, dil_q_norm, dil_k_norm, dil_w_out, ffn_w_in, ffn_w_out, loss_target, m_norm_mix, m_norm_ffn, m_gdn_w_in, m_gdn_conv_w, m_gdn_a_log, m_gdn_dt_bias, m_gdn_norm_w, m_gdn_w_out, m_dil_w_in, m_dil_q_norm, m_dil_k_norm, m_dil_w_out, m_ffn_w_in, m_ffn_w_out, v_norm_mix, v_norm_ffn, v_gdn_w_in, v_gdn_conv_w, v_gdn_a_log, v_gdn_dt_bias, v_gdn_norm_w, v_gdn_w_out, v_dil_w_in, v_dil_q_norm, v_dil_k_norm, v_dil_w_out, v_ffn_w_in, v_ffn_w_out):
    w = dict(norm_mix=norm_mix, norm_ffn=norm_ffn, gdn_w_in=gdn_w_in, gdn_conv_w=gdn_conv_w, gdn_a_log=gdn_a_log,
             gdn_dt_bias=gdn_dt_bias, gdn_norm_w=gdn_norm_w, gdn_w_out=gdn_w_out, dil_w_in=dil_w_in, dil_q_norm=dil_q_norm,
             dil_k_norm=dil_k_norm, dil_w_out=dil_w_out, ffn_w_in=ffn_w_in, ffn_w_out=ffn_w_out)
    m = dict(norm_mix=m_norm_mix, norm_ffn=m_norm_ffn, gdn_w_in=m_gdn_w_in, gdn_conv_w=m_gdn_conv_w, gdn_a_log=m_gdn_a_log,
             gdn_dt_bias=m_gdn_dt_bias, gdn_norm_w=m_gdn_norm_w, gdn_w_out=m_gdn_w_out, dil_w_in=m_dil_w_in,
             dil_q_norm=m_dil_q_norm, dil_k_norm=m_dil_k_norm, dil_w_out=m_dil_w_out, ffn_w_in=m_ffn_w_in, ffn_w_out=m_ffn_w_out)
    v = dict(norm_mix=v_norm_mix, norm_ffn=v_norm_ffn, gdn_w_in=v_gdn_w_in, gdn_conv_w=v_gdn_conv_w, gdn_a_log=v_gdn_a_log,
             gdn_dt_bias=v_gdn_dt_bias, gdn_norm_w=v_gdn_norm_w, gdn_w_out=v_gdn_w_out, dil_w_in=v_dil_w_in,
             dil_q_norm=v_dil_q_norm, dil_k_norm=v_dil_k_norm, dil_w_out=v_dil_w_out, ffn_w_in=v_ffn_w_in, ffn_w_out=v_ffn_w_out)
    big = tuple(SHARDED)

    def as_operand(name, a):
        if name == "gdn_conv_w":
            return lax.bitcast_convert_type(a, BF16)
        return a.astype(BF16)

    gathered = _all_gather(_pack_rows([as_operand(n, w[n]) for n in big]), name="weight_all_gather")
    full, at = {}, 0
    for n in big:
        rows = _shard_rows(n) * (2 if n == "gdn_conv_w" else 1)
        piece = gathered[:, at:at + rows, :]
        at += rows
        if n == "gdn_conv_w":
            piece = lax.bitcast_convert_type(piece.reshape((N_DEV,) + _shard_shape(n) + (2,)), F32)
        else:
            piece = piece.reshape((N_DEV,) + _shard_shape(n))
        full[n] = _join_shards(piece, n)
    for n in REPLICATED:
        full[n] = w[n]
    prepared = _prepare_weights(full)

    loss, grad_x, grads = _local_step(x[0], loss_target[0], prepared, norm_mix, norm_ffn)

    parts = _pack_rows([_split_shards(grads[n], n).astype(BF16) for n in big], lead=(N_DEV,))
    received = _exchange(parts, name="grad_exchange")
    packed = [_pack_rows([src[n].astype(F32) for n in big]) for src in (w, m, v)]
    outs_big = [_unpack_rows(o, big) for o in _adamw(received, *packed, name="adamw_sharded")]

    small_parts = _all_gather(_pack_small(grads), name="small_grad_all_gather")
    outs_small = [_unpack_small(o) for o in
                  _adamw(small_parts, _pack_small(w), _pack_small(m), _pack_small(v), name="adamw_replicated")]

    total_loss = lax.psum(loss, ("x", "y", "c"))
    result = [total_loss, grad_x[None]]
    for k in range(4):
        for n in WEIGHT_ORDER:
            result.append(outs_big[k][n] if n in SHARDED else outs_small[k][n])
    return tuple(result)
```

```python
import functools
import math

import jax
import jax.numpy as jnp
from jax import lax
from jax.experimental import pallas as pl
from jax.experimental.pallas import tpu as pltpu

F32 = jnp.float32
BF16 = jnp.bfloat16
MM_DTYPE = BF16

N_DEV = 8
D_MODEL = 1024
DEPTH = 4
RMS_EPS = 1e-6
L2_EPS = 1e-6

LANES = 128

GDN_HEADS = 8
GDN_DK = 128
GDN_DV = 128
GDN_CONV = 4
GDN_CHUNK = 128
GDN_QKV = 3 * GDN_HEADS * GDN_DK
GDN_MAIN = GDN_QKV + GDN_HEADS * GDN_DV
GDN_IN_WIDTH = GDN_MAIN + 2 * GDN_HEADS

DIL_GROUPS = ((128, 1), (512, 4), (2048, 16))
DIL_HEADS = 8
DIL_DH = 64
DIL_SPAN = 128
DIL_SLAB = 3 * DIL_HEADS * LANES
ALIBI_MAX_BIAS = 8.0

FFN_HIDDEN = 2816

ADAM_LR = 0.001
ADAM_B1 = 0.9
ADAM_B2 = 0.999
ADAM_EPS = 1e-08
ADAM_WD = 0.01
ADAM_STEP = 10

VMEM_LIMIT = 56 * 1024 * 1024
NEG = -1e30
HI = lax.Precision.HIGHEST


def _cparams(sem):
    return pltpu.CompilerParams(dimension_semantics=sem, vmem_limit_bytes=VMEM_LIMIT)


def _dot(a, b):
    return lax.dot_general(a, b, (((1,), (0,)), ((), ())), preferred_element_type=F32, precision=HI)


def _dot_nt(a, b):
    return lax.dot_general(a, b, (((1,), (1,)), ((), ())), preferred_element_type=F32, precision=HI)


def _dot_tn(a, b):
    return lax.dot_general(a, b, (((0,), (0,)), ((), ())), preferred_element_type=F32, precision=HI)


def _bdot(a, b):
    return lax.dot_general(a.astype(BF16), b.astype(BF16), (((1,), (0,)), ((), ())), preferred_element_type=F32)


def _bdot_nt(a, b):
    return lax.dot_general(a.astype(BF16), b.astype(BF16), (((1,), (1,)), ((), ())), preferred_element_type=F32)


def _bdot_tn(a, b):
    return lax.dot_general(a.astype(BF16), b.astype(BF16), (((0,), (0,)), ((), ())), preferred_element_type=F32)


def _pick(n, candidates):
    for c in candidates:
        if n % c == 0:
            return c
    raise ValueError(f"no tile for {n}")


def _matmul(a, b, *, name, trans_a=False, trans_b=False, b_rows=None, add=None, out_dtype=F32):
    if trans_a:
        k_dim, m_dim = a.shape
    else:
        m_dim, k_dim = a.shape
    b_start, b_size = b_rows if b_rows is not None else (0, b.shape[0])
    if trans_b:
        n_dim, k2 = b_size, b.shape[1]
    else:
        k2, n_dim = b_size, b.shape[1]
    assert k_dim == k2, (a.shape, b.shape, b_rows)
    tn = _pick(n_dim, (1024, 512, 256, 128))
    tm = min(m_dim, 2048, max(512, (1024 * 1024) // tn))
    tm = _pick(m_dim, (tm, 1408, 1024, 512, 256, 128))
    tk = _pick(k_dim, (1024, 1408, 512, 256, 128))
    nk = k_dim // tk
    has_add = add is not None
    dn = (((0 if trans_a else 1,), (1 if trans_b else 0,)), ((), ()))
    b_tile = tn if trans_b else tk
    assert b_start % b_tile == 0, (b_rows, b_tile)
    b_off = b_start // b_tile

    def body(*refs):
        if has_add:
            a_ref, b_ref, add_ref, o_ref, acc_ref = refs
        else:
            a_ref, b_ref, o_ref, acc_ref = refs
        part = lax.dot_general(a_ref[...], b_ref[...], dn, preferred_element_type=F32)

        def finish(total):
            if has_add:
                total = total + add_ref[...]
            o_ref[...] = total.astype(out_dtype)

        if nk == 1:
            finish(part)
        else:
            k = pl.program_id(2)

            @pl.when(k == 0)
            def _():
                acc_ref[...] = part

            @pl.when(k > 0)
            def _():
                acc_ref[...] += part

            @pl.when(k == nk - 1)
            def _():
                finish(acc_ref[...])

    if trans_a:
        a_spec = pl.BlockSpec((tk, tm), lambda i, j, k: (k, i))
    else:
        a_spec = pl.BlockSpec((tm, tk), lambda i, j, k: (i, k))
    if trans_b:
        b_spec = pl.BlockSpec((tn, tk), lambda i, j, k: (j + b_off, k))
    else:
        b_spec = pl.BlockSpec((tk, tn), lambda i, j, k: (k + b_off, j))
    in_specs = [a_spec, b_spec]
    args = [a, b]
    if has_add:
        in_specs.append(pl.BlockSpec((tm, tn), lambda i, j, k: (i, j)))
        args.append(add)
    return pl.pallas_call(
        body,
        grid=(m_dim // tm, n_dim // tn, nk),
        in_specs=in_specs,
        out_specs=pl.BlockSpec((tm, tn), lambda i, j, k: (i, j)),
        out_shape=jax.ShapeDtypeStruct((m_dim, n_dim), out_dtype),
        scratch_shapes=[pltpu.VMEM((tm, tn) if nk > 1 else (8, LANES), F32)],
        compiler_params=_cparams(("parallel", "parallel", "arbitrary")),
        name=name,
    )(*args)


def _rmsnorm_fwd(x, w_row, *, name):
    t, d = x.shape
    tb = min(t, 1024)

    def body(x_ref, w_ref, o_ref):
        xf = x_ref[...]
        r = lax.rsqrt(jnp.mean(xf * xf, axis=-1, keepdims=True) + RMS_EPS)
        o_ref[...] = (xf * r * w_ref[...]).astype(o_ref.dtype)

    return pl.pallas_call(
        body,
        grid=(t // tb,),
        in_specs=[pl.BlockSpec((tb, d), lambda i: (i, 0)), pl.BlockSpec((1, d), lambda i: (0, 0))],
        out_specs=pl.BlockSpec((tb, d), lambda i: (i, 0)),
        out_shape=jax.ShapeDtypeStruct((t, d), MM_DTYPE),
        compiler_params=_cparams(("parallel",)),
        name=name,
    )(x, w_row)


def _rmsnorm_bwd(x, w_row, dy, dskip, *, name):
    t, d = x.shape
    tb = min(t, 512)

    def body(x_ref, w_ref, dy_ref, ds_ref, dx_ref, dxb_ref, dw_ref):
        xf = x_ref[...]
        g = dy_ref[...]
        r = lax.rsqrt(jnp.mean(xf * xf, axis=-1, keepdims=True) + RMS_EPS)
        gw = g * w_ref[...]
        proj = jnp.mean(gw * xf, axis=-1, keepdims=True)
        dx = r * gw - xf * (r * r * r * proj) + ds_ref[...]
        dx_ref[...] = dx
        dxb_ref[...] = dx.astype(dxb_ref.dtype)
        part = jnp.sum(g * xf * r, axis=0, keepdims=True)

        @pl.when(pl.program_id(0) == 0)
        def _():
            dw_ref[...] = part

        @pl.when(pl.program_id(0) > 0)
        def _():
            dw_ref[...] += part

    row = pl.BlockSpec((tb, d), lambda i: (i, 0))
    one = pl.BlockSpec((1, d), lambda i: (0, 0))
    return pl.pallas_call(
        body,
        grid=(t // tb,),
        in_specs=[row, one, row, row],
        out_specs=[row, row, one],
        out_shape=[jax.ShapeDtypeStruct((t, d), F32), jax.ShapeDtypeStruct((t, d), MM_DTYPE),
                   jax.ShapeDtypeStruct((1, d), F32)],
        compiler_params=_cparams(("arbitrary",)),
        name=name,
    )(x, w_row, dy, dskip)


def _silu(z):
    return z / (1.0 + jnp.exp(-z))


def _swiglu_fwd(gu, *, name):
    t = gu.shape[0]
    h = FFN_HIDDEN
    tb, tc = min(t, 1024), 256
    nc = h // tc

    def body(g_ref, u_ref, o_ref):
        o_ref[...] = (_silu(g_ref[...]) * u_ref[...]).astype(o_ref.dtype)

    return pl.pallas_call(
        body,
        grid=(t // tb, nc),
        in_specs=[pl.BlockSpec((tb, tc), lambda i, j: (i, j)), pl.BlockSpec((tb, tc), lambda i, j: (i, j + nc))],
        out_specs=pl.BlockSpec((tb, tc), lambda i, j: (i, j)),
        out_shape=jax.ShapeDtypeStruct((t, h), MM_DTYPE),
        compiler_params=_cparams(("parallel", "parallel")),
        name=name,
    )(gu, gu)


def _swiglu_bwd(gu, dact, *, name):
    t = gu.shape[0]
    h = FFN_HIDDEN
    tb, tc = min(t, 1024), 256
    nc = h // tc

    def body(g_ref, u_ref, da_ref, dg_ref, du_ref):
        g = g_ref[...]
        da = da_ref[...]
        sig = 1.0 / (1.0 + jnp.exp(-g))
        sg = g * sig
        dg_ref[...] = (da * u_ref[...] * (sig + sg * (1.0 - sig))).astype(dg_ref.dtype)
        du_ref[...] = (da * sg).astype(du_ref.dtype)

    blk = pl.BlockSpec((tb, tc), lambda i, j: (i, j))
    return pl.pallas_call(
        body,
        grid=(t // tb, nc),
        in_specs=[blk, pl.BlockSpec((tb, tc), lambda i, j: (i, j + nc)), blk],
        out_specs=[blk, blk],
        out_shape=[jax.ShapeDtypeStruct((t, h), MM_DTYPE)] * 2,
        compiler_params=_cparams(("parallel", "parallel")),
        name=name,
    )(gu, gu, dact)


def _loss_head(y, target, *, name):
    t, d = y.shape
    tb = min(t, 1024)

    def body(y_ref, t_ref, dy_ref, dyb_ref, l_ref):
        err = y_ref[...] - t_ref[...]
        dy_ref[...] = err * (1.0 / d)
        dyb_ref[...] = (err * (1.0 / d)).astype(dyb_ref.dtype)
        part = jnp.sum(jnp.sum(err * err, axis=0, keepdims=True), axis=1, keepdims=True) * (0.5 / d)
        part = jnp.broadcast_to(part, l_ref.shape)

        @pl.when(pl.program_id(0) == 0)
        def _():
            l_ref[...] = part

        @pl.when(pl.program_id(0) > 0)
        def _():
            l_ref[...] += part

    row = pl.BlockSpec((tb, d), lambda i: (i, 0))
    return pl.pallas_call(
        body,
        grid=(t // tb,),
        in_specs=[row, row],
        out_specs=[row, row, pl.BlockSpec((8, LANES), lambda i: (0, 0))],
        out_shape=[jax.ShapeDtypeStruct((t, d), F32), jax.ShapeDtypeStruct((t, d), MM_DTYPE),
                   jax.ShapeDtypeStruct((8, LANES), F32)],
        compiler_params=_cparams(("arbitrary",)),
        name=name,
    )(y, target)


CONV_HALO = 8


def _conv_tile_scale(c):
    is_qk = c < 2 * GDN_HEADS
    scale = jnp.where(c < GDN_HEADS, GDN_DK ** -0.5, 1.0).astype(F32)
    return is_qk, scale


def _gdn_conv_fwd(pm, conv_w, *, name):
    t = pm.shape[0]
    tb = min(t, 1024)
    nt = t // tb
    hb = tb // CONV_HALO

    def body(x_ref, xp_ref, w_ref, o_ref):
        c = pl.program_id(0)
        ti = pl.program_id(1)
        prev = jnp.where(ti > 0, xp_ref[...], 0.0)
        xe = jnp.concatenate([prev, x_ref[...]], axis=0)
        w = w_ref[...]
        y = jnp.zeros((tb, LANES), F32)
        for j in range(GDN_CONV):
            off = CONV_HALO - (GDN_CONV - 1) + j
            y = y + w[j:j + 1, :] * xe[off:off + tb, :]
        s = _silu(y)
        is_qk, scale = _conv_tile_scale(c)
        r = lax.rsqrt(jnp.sum(s * s, axis=-1, keepdims=True) + L2_EPS) * scale
        o_ref[...] = s * jnp.where(is_qk, r, 1.0)

    return pl.pallas_call(
        body,
        grid=(GDN_QKV // LANES, nt),
        in_specs=[
            pl.BlockSpec((tb, LANES), lambda c, i: (i, c)),
            pl.BlockSpec((CONV_HALO, LANES), lambda c, i: (jnp.maximum(i * hb - 1, 0), c)),
            pl.BlockSpec((GDN_CONV, LANES), lambda c, i: (0, c)),
        ],
        out_specs=pl.BlockSpec((tb, LANES), lambda c, i: (i, c)),
        out_shape=jax.ShapeDtypeStruct((t, GDN_QKV), F32),
        compiler_params=_cparams(("parallel", "parallel")),
        name=name,
    )(pm, pm, conv_w)


def _gdn_conv_bwd(pm, conv_w, dout, *, name):
    t = pm.shape[0]
    tb = min(t, 1024)
    nt = t // tb
    hb = tb // CONV_HALO
    last_hb = t // CONV_HALO - 1
    ext = tb + CONV_HALO

    def body(x_ref, xp_ref, xn_ref, d_ref, dn_ref, w_ref, dx_ref, dw_ref):
        c = pl.program_id(0)
        ti = pl.program_id(1)
        prev = jnp.where(ti > 0, xp_ref[...], 0.0)
        has_next = ti < nt - 1
        nxt = jnp.where(has_next, xn_ref[...], 0.0)
        xe = jnp.concatenate([prev, x_ref[...], nxt], axis=0)
        de = jnp.concatenate([d_ref[...], jnp.where(has_next, dn_ref[...], 0.0)], axis=0)
        w = w_ref[...]
        y = jnp.zeros((ext, LANES), F32)
        for j in range(GDN_CONV):
            off = CONV_HALO - (GDN_CONV - 1) + j
            y = y + w[j:j + 1, :] * xe[off:off + ext, :]
        sig = 1.0 / (1.0 + jnp.exp(-y))
        s = y * sig
        is_qk, scale = _conv_tile_scale(c)
        r = lax.rsqrt(jnp.sum(s * s, axis=-1, keepdims=True) + L2_EPS)
        n = s * r
        dnrm = de * scale
        ds_qk = r * (dnrm - n * jnp.sum(dnrm * n, axis=-1, keepdims=True))
        ds = jnp.where(is_qk, ds_qk, de)
        dy = ds * (sig + s * (1.0 - sig))
        dx = jnp.zeros((tb, LANES), F32)
        dw_rows = []
        for j in range(GDN_CONV):
            sh = GDN_CONV - 1 - j
            dx = dx + w[j:j + 1, :] * dy[sh:sh + tb, :]
            off = CONV_HALO - (GDN_CONV - 1) + j
            dw_rows.append(jnp.sum(dy[:tb, :] * xe[off:off + tb, :], axis=0, keepdims=True))
        dx_ref[...] = dx.astype(dx_ref.dtype)
        part = jnp.concatenate(dw_rows, axis=0)

        @pl.when(ti == 0)
        def _():
            dw_ref[...] = part

        @pl.when(ti > 0)
        def _():
            dw_ref[...] += part

    main = pl.BlockSpec((tb, LANES), lambda c, i: (i, c))
    prev = pl.BlockSpec((CONV_HALO, LANES), lambda c, i: (jnp.maximum(i * hb - 1, 0), c))
    nxt = pl.BlockSpec((CONV_HALO, LANES), lambda c, i: (jnp.minimum((i + 1) * hb, last_hb), c))
    return pl.pallas_call(
        body,
        grid=(GDN_QKV // LANES, nt),
        in_specs=[main, prev, nxt, main, nxt, pl.BlockSpec((GDN_CONV, LANES), lambda c, i: (0, c))],
        out_specs=[main, pl.BlockSpec((GDN_CONV, LANES), lambda c, i: (0, c))],
        out_shape=[jax.ShapeDtypeStruct((t, GDN_QKV), MM_DTYPE), jax.ShapeDtypeStruct((GDN_CONV, GDN_QKV), F32)],
        compiler_params=_cparams(("parallel", "arbitrary")),
        name=name,
    )(pm, pm, pm, dout, dout, conv_w)


def _head_selector(first_col):
    row = lax.broadcasted_iota(jnp.int32, (LANES, GDN_HEADS * LANES), 0)
    col = lax.broadcasted_iota(jnp.int32, (LANES, GDN_HEADS * LANES), 1)
    return (col // LANES + first_col == row).astype(F32)


def _softplus(x):
    return jnp.maximum(x, 0.0) + jnp.log(1.0 + jnp.exp(-jnp.abs(x)))


def _gdn_gates_fwd(ab, alog_row, dt_row, *, name):
    t = ab.shape[0]
    tb = min(t, 1024)
    wide = GDN_HEADS * LANES

    def body(ab_ref, al_ref, dt_ref, g_ref, b_ref):
        x = ab_ref[...]
        g_cols = -jnp.exp(al_ref[...]) * _softplus(x + dt_ref[...])
        b_cols = 1.0 / (1.0 + jnp.exp(-x))
        g_ref[...] = _dot(g_cols, _head_selector(0))
        b_ref[...] = _dot(b_cols, _head_selector(GDN_HEADS))

    row = pl.BlockSpec((tb, LANES), lambda i: (i, 0))
    one = pl.BlockSpec((1, LANES), lambda i: (0, 0))
    out = pl.BlockSpec((tb, wide), lambda i: (i, 0))
    return pl.pallas_call(
        body,
        grid=(t // tb,),
        in_specs=[row, one, one],
        out_specs=[out, out],
        out_shape=[jax.ShapeDtypeStruct((t, wide), F32)] * 2,
        compiler_params=_cparams(("parallel",)),
        name=name,
    )(ab, alog_row, dt_row)


def _gdn_gates_bwd(ab, alog_row, dt_row, dgb, dbb, *, name):
    t = ab.shape[0]
    tb = min(t, 1024)
    wide = GDN_HEADS * LANES

    def body(ab_ref, al_ref, dt_ref, dg_ref, db_ref, dab_ref, dal_ref, ddt_ref):
        x = ab_ref[...]
        lane = lax.broadcasted_iota(jnp.int32, (tb, LANES), 1)
        dg_cols = _dot_nt(dg_ref[...], _head_selector(0))
        db_cols = _dot_nt(db_ref[...], _head_selector(GDN_HEADS))
        ea = jnp.exp(al_ref[...])
        z = x + dt_ref[...]
        sp = _softplus(z)
        sg = 1.0 / (1.0 + jnp.exp(-z))
        beta = 1.0 / (1.0 + jnp.exp(-x))
        da = jnp.where(lane < GDN_HEADS, dg_cols * (-ea) * sg, 0.0)
        db = jnp.where((lane >= GDN_HEADS) & (lane < 2 * GDN_HEADS), db_cols * beta * (1.0 - beta), 0.0)
        dab_ref[...] = (da + db).astype(dab_ref.dtype)
        p_al = jnp.sum(jnp.where(lane < GDN_HEADS, dg_cols * (-ea) * sp, 0.0), axis=0, keepdims=True)
        p_dt = jnp.sum(da, axis=0, keepdims=True)

        @pl.when(pl.program_id(0) == 0)
        def _():
            dal_ref[...] = p_al
            ddt_ref[...] = p_dt

        @pl.when(pl.program_id(0) > 0)
        def _():
            dal_ref[...] += p_al
            ddt_ref[...] += p_dt

    row = pl.BlockSpec((tb, LANES), lambda i: (i, 0))
    one = pl.BlockSpec((1, LANES), lambda i: (0, 0))
    big = pl.BlockSpec((tb, wide), lambda i: (i, 0))
    return pl.pallas_call(
        body,
        grid=(t // tb,),
        in_specs=[row, one, one, big, big],
        out_specs=[row, one, one],
        out_shape=[jax.ShapeDtypeStruct((t, LANES), MM_DTYPE), jax.ShapeDtypeStruct((1, LANES), F32),
                   jax.ShapeDtypeStruct((1, LANES), F32)],
        compiler_params=_cparams(("arbitrary",)),
        name=name,
    )(ab, alog_row, dt_row, dgb, dbb)


@jax.custom_vjp
def _unit_lower_inverse_rest(n):
    c = n.shape[0]
    ri = lax.broadcasted_iota(jnp.int32, (c, c), 0)
    ci = lax.broadcasted_iota(jnp.int32, (c, c), 1)
    rest = None
    size = 1
    while size < c:
        joins = ((ri // (2 * size)) == (ci // (2 * size))) & ((ri // size) != (ci // size))
        low = jnp.where(joins, n, 0.0)
        if rest is None:
            rest = -low
        else:
            left = low + _bdot(rest, low)
            rest = rest - (left + _bdot(left, rest))
        size *= 2
    return rest


def _unit_lower_inverse_rest_fwd(n):
    rest = _unit_lower_inverse_rest(n)
    return rest, rest


def _unit_lower_inverse_rest_bwd(rest, ct):
    left = ct + _bdot_tn(rest, ct)
    return (-(left + _bdot_nt(left, rest)),)


_unit_lower_inverse_rest.defvjp(_unit_lower_inverse_rest_fwd, _unit_lower_inverse_rest_bwd)


def _bf16_pieces(x):
    hi = x.astype(BF16)
    r1 = x - hi.astype(F32)
    mid = r1.astype(BF16)
    lo = (r1 - mid.astype(F32)).astype(BF16)
    return hi, mid, lo


def _lower_ones(c):
    ri = lax.broadcasted_iota(jnp.int32, (c, c), 0)
    ci = lax.broadcasted_iota(jnp.int32, (c, c), 1)
    return (ri >= ci).astype(BF16)


@jax.custom_vjp
def _running_sum(x):
    tri = _lower_ones(x.shape[0])
    return sum(lax.dot_general(tri, p, (((1,), (0,)), ((), ())), preferred_element_type=F32) for p in _bf16_pieces(x))


def _running_sum_fwd(x):
    return _running_sum(x), None


def _running_sum_bwd(_, ct):
    tri = _lower_ones(ct.shape[0])
    return (sum(lax.dot_general(tri, p, (((0,), (0,)), ((), ())), preferred_element_type=F32) for p in _bf16_pieces(ct)),)


_running_sum.defvjp(_running_sum_fwd, _running_sum_bwd)


def _gdn_prep_math(q, k, v, gb, bb):
    c = GDN_CHUNK
    ri = lax.broadcasted_iota(jnp.int32, (c, c), 0)
    ci = lax.broadcasted_iota(jnp.int32, (c, c), 1)
    causal = ri >= ci
    gc = _running_sum(gb)
    decay = jnp.exp(jnp.where(causal, gc - gc.T, NEG))
    n = jnp.where(ri > ci, _bdot_nt(k, k) * bb * decay, 0.0)
    rest = _unit_lower_inverse_rest(n)
    eg = jnp.exp(gc)
    rhs_v = v * bb
    rhs_k = k * bb * eg
    u = rhs_v + _bdot(rest, rhs_v)
    w = rhs_k + _bdot(rest, rhs_k)
    qk = _bdot_nt(q, k) * decay
    qd = q * eg
    last = jnp.sum(jnp.where(ri == c - 1, gc, 0.0), axis=0, keepdims=True)
    gl = jnp.broadcast_to(last, (c, c))
    kt = k * jnp.exp(gl - gc)
    cd = jnp.exp(gl)
    return u, w, qk, qd, kt, cd


def _head_tiles(ref, h):
    return ref[:, h * LANES:(h + 1) * LANES]


def _gdn_prep_fwd(qkv, gb, bb, *, name):
    t = qkv.shape[0]
    c = GDN_CHUNK
    wide = GDN_HEADS * LANES

    def body(q_ref, k_ref, v_ref, g_ref, b_ref, *outs):
        for h in range(GDN_HEADS):
            res = _gdn_prep_math(_head_tiles(q_ref, h), _head_tiles(k_ref, h), _head_tiles(v_ref, h),
                                 _head_tiles(g_ref, h), _head_tiles(b_ref, h))
            for o_ref, val in zip(outs, res):
                o_ref[:, h * LANES:(h + 1) * LANES] = val

    blk = lambda off: pl.BlockSpec((c, wide), lambda i: (i, off))
    return pl.pallas_call(
        body,
        grid=(t // c,),
        in_specs=[blk(0), blk(1), blk(2), blk(0), blk(0)],
        out_specs=[blk(0)] * 6,
        out_shape=[jax.ShapeDtypeStruct((t, wide), F32)] * 6,
        compiler_params=_cparams(("parallel",)),
        name=name,
    )(qkv, qkv, qkv, gb, bb)


def _gdn_prep_bwd(qkv, gb, bb, cts, *, name):
    t = qkv.shape[0]
    c = GDN_CHUNK
    wide = GDN_HEADS * LANES

    def body(q_ref, k_ref, v_ref, g_ref, b_ref, c0, c1, c2, c3, c4, c5, dqkv_ref, dg_ref, db_ref):
        for h in range(GDN_HEADS):
            prim = (_head_tiles(q_ref, h), _head_tiles(k_ref, h), _head_tiles(v_ref, h),
                    _head_tiles(g_ref, h), _head_tiles(b_ref, h))
            _, pull = jax.vjp(_gdn_prep_math, *prim)
            dq, dk, dv, dg, db = pull(tuple(_head_tiles(r, h) for r in (c0, c1, c2, c3, c4, c5)))
            dqkv_ref[:, h * LANES:(h + 1) * LANES] = dq
            dqkv_ref[:, wide + h * LANES:wide + (h + 1) * LANES] = dk
            dqkv_ref[:, 2 * wide + h * LANES:2 * wide + (h + 1) * LANES] = dv
            dg_ref[:, h * LANES:(h + 1) * LANES] = dg
            db_ref[:, h * LANES:(h + 1) * LANES] = db

    blk = lambda off: pl.BlockSpec((c, wide), lambda i: (i, off))
    return pl.pallas_call(
        body,
        grid=(t // c,),
        in_specs=[blk(0), blk(1), blk(2), blk(0), blk(0)] + [blk(0)] * 6,
        out_specs=[pl.BlockSpec((c, 3 * wide), lambda i: (i, 0)), blk(0), blk(0)],
        out_shape=[jax.ShapeDtypeStruct((t, 3 * wide), F32), jax.ShapeDtypeStruct((t, wide), F32),
                   jax.ShapeDtypeStruct((t, wide), F32)],
        compiler_params=_cparams(("parallel",)),
        name=name,
    )(qkv, qkv, qkv, gb, bb, *cts)


def _gdn_scan_math(s, u, w, qk, qd, kt, cd):
    v_new = u - _bdot(w, s)
    o = _bdot(qd, s) + _bdot(qk, v_new)
    s_new = s * cd + _bdot_tn(kt, v_new)
    return o, s_new


def _gdn_scan_fwd(prep, *, name):
    t = prep[0].shape[0]
    c = GDN_CHUNK
    wide = GDN_HEADS * LANES

    def body(u_ref, w_ref, qk_ref, qd_ref, kt_ref, cd_ref, o_ref, st_ref, s_ref):
        @pl.when(pl.program_id(0) == 0)
        def _():
            s_ref[...] = jnp.zeros_like(s_ref)

        for h in range(GDN_HEADS):
            s = _head_tiles(s_ref, h)
            st_ref[:, h * LANES:(h + 1) * LANES] = s
            o, s_new = _gdn_scan_math(s, *(_head_tiles(r, h) for r in (u_ref, w_ref, qk_ref, qd_ref, kt_ref, cd_ref)))
            o_ref[:, h * LANES:(h + 1) * LANES] = o
            s_ref[:, h * LANES:(h + 1) * LANES] = s_new

    blk = pl.BlockSpec((c, wide), lambda i: (i, 0))
    return pl.pallas_call(
        body,
        grid=(t // c,),
        in_specs=[blk] * 6,
        out_specs=[blk, blk],
        out_shape=[jax.ShapeDtypeStruct((t, wide), F32)] * 2,
        scratch_shapes=[pltpu.VMEM((GDN_DK, wide), F32)],
        compiler_params=_cparams(("arbitrary",)),
        name=name,
    )(*prep)


def _gdn_scan_bwd(prep, states, do, *, name):
    t = do.shape[0]
    c = GDN_CHUNK
    wide = GDN_HEADS * LANES
    nc = t // c

    def body(u_ref, w_ref, qk_ref, qd_ref, kt_ref, cd_ref, st_ref, do_ref, *rest):
        outs, ds_ref = rest[:6], rest[6]

        @pl.when(pl.program_id(0) == 0)
        def _():
            ds_ref[...] = jnp.zeros_like(ds_ref)

        for h in range(GDN_HEADS):
            prim = (_head_tiles(st_ref, h),) + tuple(
                _head_tiles(r, h) for r in (u_ref, w_ref, qk_ref, qd_ref, kt_ref, cd_ref))
            _, pull = jax.vjp(_gdn_scan_math, *prim)
            grads = pull((_head_tiles(do_ref, h), _head_tiles(ds_ref, h)))
            ds_ref[:, h * LANES:(h + 1) * LANES] = grads[0]
            for o_ref, val in zip(outs, grads[1:]):
                o_ref[:, h * LANES:(h + 1) * LANES] = val

    blk = pl.BlockSpec((c, wide), lambda i: (nc - 1 - i, 0))
    return pl.pallas_call(
        body,
        grid=(nc,),
        in_specs=[blk] * 8,
        out_specs=[blk] * 6,
        out_shape=[jax.ShapeDtypeStruct((t, wide), F32)] * 6,
        scratch_shapes=[pltpu.VMEM((GDN_DK, wide), F32)],
        compiler_params=_cparams(("arbitrary",)),
        name=name,
    )(*prep, states, do)


def _gdn_outgate_math(o, z, nw):
    r = lax.rsqrt(jnp.mean(o * o, axis=-1, keepdims=True) + RMS_EPS)
    return o * r * nw * _silu(z)


def _gdn_outgate_fwd(o, pm, nw_row, *, name):
    t = o.shape[0]
    tb = min(t, 1024)
    z_off = GDN_QKV // LANES

    def body(o_ref, z_ref, nw_ref, y_ref):
        y_ref[...] = _gdn_outgate_math(o_ref[...], z_ref[...], nw_ref[...]).astype(y_ref.dtype)

    return pl.pallas_call(
        body,
        grid=(t // tb, GDN_HEADS),
        in_specs=[pl.BlockSpec((tb, LANES), lambda i, h: (i, h)), pl.BlockSpec((tb, LANES), lambda i, h: (i, h + z_off)),
                  pl.BlockSpec((1, LANES), lambda i, h: (0, 0))],
        out_specs=pl.BlockSpec((tb, LANES), lambda i, h: (i, h)),
        out_shape=jax.ShapeDtypeStruct((t, GDN_HEADS * LANES), MM_DTYPE),
        compiler_params=_cparams(("parallel", "parallel")),
        name=name,
    )(o, pm, nw_row)


def _gdn_outgate_bwd(o, pm, nw_row, dy, *, name):
    t = o.shape[0]
    tb = min(t, 1024)
    z_off = GDN_QKV // LANES

    def body(o_ref, z_ref, nw_ref, dy_ref, do_ref, dz_ref, dnw_ref):
        _, pull = jax.vjp(_gdn_outgate_math, o_ref[...], z_ref[...], nw_ref[...])
        d_o, d_z, d_nw = pull(dy_ref[...])
        do_ref[...] = d_o
        dz_ref[...] = d_z.astype(dz_ref.dtype)
        first = (pl.program_id(0) == 0) & (pl.program_id(1) == 0)

        @pl.when(first)
        def _():
            dnw_ref[...] = d_nw

        @pl.when(jnp.logical_not(first))
        def _():
            dnw_ref[...] += d_nw

    blk = pl.BlockSpec((tb, LANES), lambda i, h: (i, h))
    one = pl.BlockSpec((1, LANES), lambda i, h: (0, 0))
    return pl.pallas_call(
        body,
        grid=(t // tb, GDN_HEADS),
        in_specs=[blk, pl.BlockSpec((tb, LANES), lambda i, h: (i, h + z_off)), one, blk],
        out_specs=[blk, blk, one],
        out_shape=[jax.ShapeDtypeStruct((t, GDN_HEADS * LANES), F32),
                   jax.ShapeDtypeStruct((t, GDN_HEADS * LANES), MM_DTYPE), jax.ShapeDtypeStruct((1, LANES), F32)],
        compiler_params=_cparams(("arbitrary", "arbitrary")),
        name=name,
    )(o, pm, nw_row, dy)


def _rms64(x, w_row):
    return x * lax.rsqrt(jnp.sum(x * x, axis=-1, keepdims=True) * (1.0 / DIL_DH) + RMS_EPS) * w_row


def _alibi_slope(group, head):
    return 2.0 ** (-ALIBI_MAX_BIAS * (group * DIL_HEADS + head + 1) / (len(DIL_GROUPS) * DIL_HEADS))


def _band_logits(qn, kp, kc, slope_d, has_prev):
    qi = lax.broadcasted_iota(jnp.int32, (DIL_SPAN, DIL_SPAN), 0)
    kj = lax.broadcasted_iota(jnp.int32, (DIL_SPAN, DIL_SPAN), 1)
    steps_c = (qi - kj).astype(F32)
    scale = DIL_DH ** -0.5
    sp = _bdot_nt(qn, kp) * scale - slope_d * (steps_c + float(DIL_SPAN))
    sc = _bdot_nt(qn, kc) * scale - slope_d * steps_c
    sp = jnp.where((kj >= qi) & has_prev, sp, NEG)
    sc = jnp.where(kj <= qi, sc, NEG)
    return sp, sc


def _dil_attn_fwd(slab, wq_row, wk_row, *, group, name):
    dilation = DIL_GROUPS[group][1]
    t = slab.shape[0]
    rows = t // dilation
    nlb = rows // DIL_SPAN
    wide = DIL_HEADS * LANES
    view = slab.reshape(rows, dilation * DIL_SLAB)

    def body(q_ref, kc_ref, vc_ref, kp_ref, vp_ref, wq_ref, wk_ref, o_ref):
        has_prev = pl.program_id(1) > 0
        lane = lax.broadcasted_iota(jnp.int32, (DIL_SPAN, LANES), 1)
        for h in range(DIL_HEADS):
            qn = _rms64(_head_tiles(q_ref, h), wq_ref[...])
            kc = _rms64(_head_tiles(kc_ref, h), wk_ref[...])
            kp = _rms64(_head_tiles(kp_ref, h), wk_ref[...])
            sp, sc = _band_logits(qn, kp, kc, _alibi_slope(group, h) * dilation, has_prev)
            m = jnp.maximum(jnp.max(sp, axis=-1, keepdims=True), jnp.max(sc, axis=-1, keepdims=True))
            pp = jnp.exp(sp - m)
            pc = jnp.exp(sc - m)
            l = jnp.sum(pp, axis=-1, keepdims=True) + jnp.sum(pc, axis=-1, keepdims=True)
            o = (_bdot(pp, _head_tiles(vp_ref, h)) + _bdot(pc, _head_tiles(vc_ref, h))) / l
            o_ref[:, h * LANES:(h + 1) * LANES] = jnp.where(lane < DIL_DH, o, m + jnp.log(l))

    cur = lambda part: pl.BlockSpec((DIL_SPAN, wide), lambda r, i: (i, 3 * r + part))
    prv = lambda part: pl.BlockSpec((DIL_SPAN, wide), lambda r, i: (jnp.maximum(i - 1, 0), 3 * r + part))
    one = pl.BlockSpec((1, LANES), lambda r, i: (0, 0))
    out = pl.pallas_call(
        body,
        grid=(dilation, nlb),
        in_specs=[cur(0), cur(1), cur(2), prv(1), prv(2), one, one],
        out_specs=pl.BlockSpec((DIL_SPAN, wide), lambda r, i: (i, r)),
        out_shape=jax.ShapeDtypeStruct((rows, dilation * wide), F32),
        compiler_params=_cparams(("parallel", "parallel")),
        name=name,
    )(view, view, view, view, view, wq_row, wk_row)
    return out.reshape(t, wide)


def _dil_merge_fwd(oe, *, name):
    t = oe[0].shape[0]
    tb = min(t, 1024)

    def body(e0, e1, e2, y_ref, om_ref):
        lane = lax.broadcasted_iota(jnp.int32, (tb, LANES), 1)
        es = [e0[...], e1[...], e2[...]]
        lse = [jnp.sum(jnp.where(lane == DIL_DH, e, 0.0), axis=-1, keepdims=True) for e in es]
        top = jnp.maximum(jnp.maximum(lse[0], lse[1]), lse[2])
        joint = top + jnp.log(jnp.exp(lse[0] - top) + jnp.exp(lse[1] - top) + jnp.exp(lse[2] - top))
        o = sum(jnp.exp(l - joint) * e for l, e in zip(lse, es))
        y_ref[...] = jnp.where(lane < DIL_DH, o, 0.0).astype(y_ref.dtype)
        om_ref[...] = jnp.where(lane < DIL_DH, o, joint)

    blk = pl.BlockSpec((tb, LANES), lambda i, h: (i, h))
    return pl.pallas_call(
        body,
        grid=(t // tb, DIL_HEADS),
        in_specs=[blk] * 3,
        out_specs=[blk, blk],
        out_shape=[jax.ShapeDtypeStruct((t, DIL_HEADS * LANES), MM_DTYPE),
                   jax.ShapeDtypeStruct((t, DIL_HEADS * LANES), F32)],
        compiler_params=_cparams(("parallel", "parallel")),
        name=name,
    )(*oe)


def _dil_merge_bwd(dy, om, *, name):
    t = dy.shape[0]
    tb = min(t, 1024)

    def body(dy_ref, om_ref, st_ref):
        lane = lax.broadcasted_iota(jnp.int32, (tb, LANES), 1)
        d_o = jnp.where(lane < DIL_DH, dy_ref[...], 0.0)
        om_t = om_ref[...]
        delta = jnp.sum(d_o * om_t, axis=-1, keepdims=True)
        st_ref[...] = jnp.where(lane < DIL_DH, d_o, jnp.where(lane == DIL_DH, om_t, jnp.where(lane == DIL_DH + 1, delta, 0.0)))

    blk = pl.BlockSpec((tb, LANES), lambda i, h: (i, h))
    return pl.pallas_call(
        body,
        grid=(t // tb, DIL_HEADS),
        in_specs=[blk, blk],
        out_specs=blk,
        out_shape=jax.ShapeDtypeStruct((t, DIL_HEADS * LANES), F32),
        compiler_params=_cparams(("parallel", "parallel")),
        name=name,
    )(dy, om)


def _rms64_bwd(x, w_row, dy):
    r = lax.rsqrt(jnp.sum(x * x, axis=-1, keepdims=True) * (1.0 / DIL_DH) + RMS_EPS)
    gw = dy * w_row
    dx = r * gw - x * (r * r * r * jnp.sum(gw * x, axis=-1, keepdims=True) * (1.0 / DIL_DH))
    return dx, dy * x * r


def _dil_attn_bwd(slab, stat, wq_row, wk_row, dwq_in, dwk_in, *, group, name):
    dilation = DIL_GROUPS[group][1]
    t = slab.shape[0]
    rows = t // dilation
    nlb = rows // DIL_SPAN
    wide = DIL_HEADS * LANES
    view = slab.reshape(rows, dilation * DIL_SLAB)
    stat_view = stat.reshape(rows, dilation * wide)

    def body(cur_ref, kp_ref, vp_ref, st_ref, wq_ref, wk_ref, dwq_in_ref, dwk_in_ref, d_ref, dwq_ref, dwk_ref,
             dk_carry, dv_carry):
        step = pl.program_id(1)
        has_prev = step < nlb - 1
        first = (pl.program_id(0) == 0) & (step == 0)

        @pl.when(step == 0)
        def _():
            dk_carry[...] = jnp.zeros_like(dk_carry)
            dv_carry[...] = jnp.zeros_like(dv_carry)

        @pl.when(first)
        def _():
            dwq_ref[...] = dwq_in_ref[...]
            dwk_ref[...] = dwk_in_ref[...]

        lane = lax.broadcasted_iota(jnp.int32, (DIL_SPAN, LANES), 1)
        scale = DIL_DH ** -0.5
        dwq = jnp.zeros((1, LANES), F32)
        dwk = jnp.zeros((1, LANES), F32)
        for h in range(DIL_HEADS):
            q_raw = cur_ref[:, h * LANES:(h + 1) * LANES]
            kc_raw = cur_ref[:, wide + h * LANES:wide + (h + 1) * LANES]
            vc = cur_ref[:, 2 * wide + h * LANES:2 * wide + (h + 1) * LANES]
            kp_raw = _head_tiles(kp_ref, h)
            vp = _head_tiles(vp_ref, h)
            st = _head_tiles(st_ref, h)
            d_o = jnp.where(lane < DIL_DH, st, 0.0)
            lse = jnp.sum(jnp.where(lane == DIL_DH, st, 0.0), axis=-1, keepdims=True)
            delta = jnp.sum(jnp.where(lane == DIL_DH + 1, st, 0.0), axis=-1, keepdims=True)
            qn = _rms64(q_raw, wq_ref[...])
            kc = _rms64(kc_raw, wk_ref[...])
            kp = _rms64(kp_raw, wk_ref[...])
            sp, sc = _band_logits(qn, kp, kc, _alibi_slope(group, h) * dilation, has_prev)
            pp = jnp.exp(sp - lse)
            pc = jnp.exp(sc - lse)
            dsp = pp * (_bdot_nt(d_o, vp) - delta) * scale
            dsc = pc * (_bdot_nt(d_o, vc) - delta) * scale
            dqn = _bdot(dsp, kp) + _bdot(dsc, kc)
            dkc_n = _bdot_tn(dsc, qn) + _head_tiles(dk_carry, h)
            dvc = _bdot_tn(pc, d_o) + _head_tiles(dv_carry, h)
            dk_carry[:, h * LANES:(h + 1) * LANES] = _bdot_tn(dsp, qn)
            dv_carry[:, h * LANES:(h + 1) * LANES] = _bdot_tn(pp, d_o)
            dq_raw, dwq_rows = _rms64_bwd(q_raw, wq_ref[...], dqn)
            dk_raw, dwk_rows = _rms64_bwd(kc_raw, wk_ref[...], dkc_n)
            dwq = dwq + jnp.sum(dwq_rows, axis=0, keepdims=True)
            dwk = dwk + jnp.sum(dwk_rows, axis=0, keepdims=True)
            d_ref[:, h * LANES:(h + 1) * LANES] = dq_raw.astype(d_ref.dtype)
            d_ref[:, wide + h * LANES:wide + (h + 1) * LANES] = dk_raw.astype(d_ref.dtype)
            d_ref[:, 2 * wide + h * LANES:2 * wide + (h + 1) * LANES] = dvc.astype(d_ref.dtype)
        dwq_ref[...] += dwq
        dwk_ref[...] += dwk

    blk_i = lambda i: nlb - 1 - i
    cur = pl.BlockSpec((DIL_SPAN, DIL_SLAB), lambda r, i: (blk_i(i), r))
    prv = lambda part: pl.BlockSpec((DIL_SPAN, wide), lambda r, i: (jnp.maximum(blk_i(i) - 1, 0), 3 * r + part))
    one = pl.BlockSpec((1, LANES), lambda r, i: (0, 0))
    dslab, dwq, dwk = pl.pallas_call(
        body,
        grid=(dilation, nlb),
        in_specs=[cur, prv(1), prv(2), pl.BlockSpec((DIL_SPAN, wide), lambda r, i: (blk_i(i), r)), one, one, one, one],
        out_specs=[cur, one, one],
        out_shape=[jax.ShapeDtypeStruct((rows, dilation * DIL_SLAB), MM_DTYPE), jax.ShapeDtypeStruct((1, LANES), F32),
                   jax.ShapeDtypeStruct((1, LANES), F32)],
        scratch_shapes=[pltpu.VMEM((DIL_SPAN, wide), F32), pltpu.VMEM((DIL_SPAN, wide), F32)],
        compiler_params=_cparams(("arbitrary", "arbitrary")),
        name=name,
    )(view, view, view, stat_view, wq_row, wk_row, dwq_in, dwk_in)
    return dslab.reshape(t, DIL_SLAB), dwq, dwk


def _row(v, width=LANES):
    v = v.astype(F32).reshape(-1)
    return jnp.pad(v, (0, width - v.shape[0])).reshape(1, width)


def _prepare_weights(w):
    d = D_MODEL
    gdn, dil, ffn = [], [], []
    for j in range(DEPTH // 2):
        wt = w["gdn_w_in"][j]
        gates_t = jnp.pad(wt[GDN_MAIN:], ((0, LANES - 2 * GDN_HEADS), (0, 0)))
        gdn.append(dict(in_t=wt, gates_t=gates_t, out=w["gdn_w_out"][j], conv=w["gdn_conv_w"][j].astype(F32),
                        alog=_row(w["gdn_a_log"][j]), dt=_row(w["gdn_dt_bias"][j]), nw=_row(w["gdn_norm_w"][j])))
        wt = w["dil_w_in"][j].reshape(3, len(DIL_GROUPS), DIL_HEADS, DIL_DH, d)
        wg_t = [jnp.pad(wt[:, g], ((0, 0), (0, 0), (0, LANES - DIL_DH), (0, 0))).reshape(DIL_SLAB, d)
                for g in range(len(DIL_GROUPS))]
        out_t = jnp.pad(w["dil_w_out"][j].reshape(d, DIL_HEADS, DIL_DH), ((0, 0), (0, 0), (0, LANES - DIL_DH)))
        dil.append(dict(wg_t=wg_t, out_t=out_t.reshape(d, DIL_HEADS * LANES), wq=_row(w["dil_q_norm"][j]),
                        wk=_row(w["dil_k_norm"][j])))
    for i in range(DEPTH):
        ffn.append(dict(in_t=w["ffn_w_in"][i], out=w["ffn_w_out"][i]))
    return dict(gdn=gdn, dil=dil, ffn=ffn)


def _gdn_layer_fwd(x, nrow, p):
    hn = _rmsnorm_fwd(x, nrow, name="rmsnorm_fwd")
    pm = _matmul(hn, p["in_t"], trans_b=True, b_rows=(0, GDN_MAIN), name="gdn_proj_main")
    ab = _matmul(hn, p["gates_t"], trans_b=True, name="gdn_proj_gates")
    qkv = _gdn_conv_fwd(pm, p["conv"], name="gdn_conv_fwd")
    gb, bb = _gdn_gates_fwd(ab, p["alog"], p["dt"], name="gdn_gates_fwd")
    prep = _gdn_prep_fwd(qkv, gb, bb, name="gdn_prep_fwd")
    o, states = _gdn_scan_fwd(prep, name="gdn_scan_fwd")
    og = _gdn_outgate_fwd(o, pm, p["nw"], name="gdn_outgate_fwd")
    y = _matmul(og, p["out"], add=x, name="gdn_proj_out")
    return y, (x, hn, pm, ab, qkv, gb, bb, prep, states, o, og)


def _gdn_layer_bwd(dx, dxb, nrow, p, saved):
    x, hn, pm, ab, qkv, gb, bb, prep, states, o, og = saved
    d_og = _matmul(dxb, p["out"], trans_b=True, name="gdn_dgate")
    g_out = _matmul(og, dxb, trans_a=True, name="gdn_gw_out")
    d_o, d_z, d_nw = _gdn_outgate_bwd(o, pm, p["nw"], d_og, name="gdn_outgate_bwd")
    cts = _gdn_scan_bwd(prep, states, d_o, name="gdn_scan_bwd")
    dqkv, dgb, dbb = _gdn_prep_bwd(qkv, gb, bb, cts, name="gdn_prep_bwd")
    d_ab, d_alog, d_dt = _gdn_gates_bwd(ab, p["alog"], p["dt"], dgb, dbb, name="gdn_gates_bwd")
    d_conv, g_conv = _gdn_conv_bwd(pm, p["conv"], dqkv, name="gdn_conv_bwd")
    d_hn = _matmul(d_conv, p["in_t"], b_rows=(0, GDN_QKV), name="gdn_dhn_qkv")
    d_hn = _matmul(d_z, p["in_t"], b_rows=(GDN_QKV, GDN_MAIN - GDN_QKV), add=d_hn, name="gdn_dhn_z")
    d_hn = _matmul(d_ab, p["gates_t"], add=d_hn, name="gdn_dhn_gates")
    g_in_t = jnp.concatenate([
        _matmul(d_conv, hn, trans_a=True, name="gdn_gw_qkv"),
        _matmul(d_z, hn, trans_a=True, name="gdn_gw_z"),
        _matmul(d_ab, hn, trans_a=True, name="gdn_gw_gates")[:2 * GDN_HEADS],
    ], axis=0)
    dx_new, dxb_new, g_norm = _rmsnorm_bwd(x, nrow, d_hn, dx, name="rmsnorm_bwd")
    grads = dict(w_in=g_in_t, conv=g_conv, a_log=d_alog[0, :GDN_HEADS], dt_bias=d_dt[0, :GDN_HEADS], norm_w=d_nw[0],
                 w_out=g_out, norm=g_norm[0])
    return dx_new, dxb_new, grads


def _dil_layer_fwd(x, nrow, p):
    hn = _rmsnorm_fwd(x, nrow, name="rmsnorm_fwd")
    slabs = [_matmul(hn, p["wg_t"][g], trans_b=True, name="dil_proj_in") for g in range(len(DIL_GROUPS))]
    oe = [_dil_attn_fwd(slabs[g], p["wq"], p["wk"], group=g, name=f"dil_attn_fwd_g{g}") for g in range(len(DIL_GROUPS))]
    y, om = _dil_merge_fwd(oe, name="dil_merge_fwd")
    out = _matmul(y, p["out_t"], trans_b=True, add=x, name="dil_proj_out")
    return out, (x, hn, slabs, y, om)


def _dil_layer_bwd(dx, dxb, nrow, p, saved):
    x, hn, slabs, y, om = saved
    d_y = _matmul(dxb, p["out_t"], name="dil_dmerged")
    g_out_t = _matmul(dxb, y, trans_a=True, name="dil_gw_out")
    g_out_t = g_out_t.reshape(D_MODEL, DIL_HEADS, LANES)[..., :DIL_DH].reshape(D_MODEL, DIL_HEADS * DIL_DH)
    stat = _dil_merge_bwd(d_y, om, name="dil_merge_bwd")
    d_hn = None
    dwq = jnp.zeros((1, LANES), F32)
    dwk = jnp.zeros((1, LANES), F32)
    g_groups = []
    for g in range(len(DIL_GROUPS)):
        dslab, dwq, dwk = _dil_attn_bwd(slabs[g], stat, p["wq"], p["wk"], dwq, dwk, group=g, name=f"dil_attn_bwd_g{g}")
        d_hn = _matmul(dslab, p["wg_t"][g], add=d_hn, name="dil_dhn")
        g_w = _matmul(dslab, hn, trans_a=True, name="dil_gw_in")
        g_groups.append(g_w.reshape(3, DIL_HEADS, LANES, D_MODEL)[:, :, :DIL_DH])
    g_in_t = jnp.stack(g_groups, axis=1).reshape(3 * len(DIL_GROUPS) * DIL_HEADS * DIL_DH, D_MODEL)
    dx_new, dxb_new, g_norm = _rmsnorm_bwd(x, nrow, d_hn, dx, name="rmsnorm_bwd")
    grads = dict(w_in=g_in_t, q_norm=dwq[0, :DIL_DH], k_norm=dwk[0, :DIL_DH], w_out=g_out_t, norm=g_norm[0])
    return dx_new, dxb_new, grads


def _ffn_layer_fwd(x, nrow, p):
    hn = _rmsnorm_fwd(x, nrow, name="rmsnorm_fwd")
    gu = _matmul(hn, p["in_t"], trans_b=True, name="ffn_proj_in")
    act = _swiglu_fwd(gu, name="swiglu_fwd")
    y = _matmul(act, p["out"], add=x, name="ffn_proj_out")
    return y, (x, hn, gu, act)


def _ffn_layer_bwd(dx, dxb, nrow, p, saved):
    x, hn, gu, act = saved
    d_act = _matmul(dxb, p["out"], trans_b=True, name="ffn_dact")
    g_out = _matmul(act, dxb, trans_a=True, name="ffn_gw_out")
    d_g, d_u = _swiglu_bwd(gu, d_act, name="swiglu_bwd")
    d_hn = _matmul(d_g, p["in_t"], b_rows=(0, FFN_HIDDEN), name="ffn_dhn_gate")
    d_hn = _matmul(d_u, p["in_t"], b_rows=(FFN_HIDDEN, FFN_HIDDEN), add=d_hn, name="ffn_dhn_up")
    g_in_t = jnp.concatenate([_matmul(d_g, hn, trans_a=True, name="ffn_gw_gate"),
                              _matmul(d_u, hn, trans_a=True, name="ffn_gw_up")], axis=0)
    dx_new, dxb_new, g_norm = _rmsnorm_bwd(x, nrow, d_hn, dx, name="rmsnorm_bwd")
    return dx_new, dxb_new, dict(w_in=g_in_t, w_out=g_out, norm=g_norm[0])


def _local_step(x, target, prepared, norm_mix, norm_ffn):
    saved = []
    for i in range(DEPTH):
        j = i // 2
        mix_row = norm_mix[i].reshape(1, D_MODEL)
        if i % 2 == 0:
            x, s_mix = _gdn_layer_fwd(x, mix_row, prepared["gdn"][j])
        else:
            x, s_mix = _dil_layer_fwd(x, mix_row, prepared["dil"][j])
        x, s_ffn = _ffn_layer_fwd(x, norm_ffn[i].reshape(1, D_MODEL), prepared["ffn"][i])
        saved.append((s_mix, s_ffn))
    dx, dxb, loss = _loss_head(x, target, name="loss_head")
    g_mix, g_ffn = [None] * DEPTH, [None] * DEPTH
    for i in reversed(range(DEPTH)):
        j = i // 2
        s_mix, s_ffn = saved[i]
        dx, dxb, g_ffn[i] = _ffn_layer_bwd(dx, dxb, norm_ffn[i].reshape(1, D_MODEL), prepared["ffn"][i], s_ffn)
        mix_row = norm_mix[i].reshape(1, D_MODEL)
        if i % 2 == 0:
            dx, dxb, g_mix[i] = _gdn_layer_bwd(dx, dxb, mix_row, prepared["gdn"][j], s_mix)
        else:
            dx, dxb, g_mix[i] = _dil_layer_bwd(dx, dxb, mix_row, prepared["dil"][j], s_mix)
    gdn = [g_mix[i] for i in range(0, DEPTH, 2)]
    dil = [g_mix[i] for i in range(1, DEPTH, 2)]
    grads = dict(
        norm_mix=jnp.stack([g["norm"] for g in g_mix]),
        norm_ffn=jnp.stack([g["norm"] for g in g_ffn]),
        gdn_w_in=[g["w_in"] for g in gdn],
        gdn_conv_w=jnp.stack([g["conv"] for g in gdn]),
        gdn_a_log=jnp.stack([g["a_log"] for g in gdn]),
        gdn_dt_bias=jnp.stack([g["dt_bias"] for g in gdn]),
        gdn_norm_w=jnp.stack([g["norm_w"] for g in gdn]),
        gdn_w_out=[g["w_out"] for g in gdn],
        dil_w_in=[g["w_in"] for g in dil],
        dil_q_norm=jnp.stack([g["q_norm"] for g in dil]),
        dil_k_norm=jnp.stack([g["k_norm"] for g in dil]),
        dil_w_out=[g["w_out"] for g in dil],
        ffn_w_in=[g["w_in"] for g in g_ffn],
        ffn_w_out=[g["w_out"] for g in g_ffn],
    )
    return loss[0, 0], dx, grads


MESH_ID = pl.DeviceIdType.MESH
ANY_SPACE = pl.BlockSpec(memory_space=pl.ANY)


def _mesh_position():
    return lax.axis_index("x"), lax.axis_index("y"), lax.axis_index("c")


def _flip(pos, k):
    x, y, c = pos
    return (1 - x if k & 4 else x, 1 - y if k & 2 else y, 1 - c if k & 1 else c)


def _linear(pos):
    return 4 * pos[0] + 2 * pos[1] + pos[2]


def _comm_scratch():
    return [pltpu.SemaphoreType.DMA((N_DEV - 1,)), pltpu.SemaphoreType.DMA((N_DEV - 1,)), pltpu.SemaphoreType.DMA(())]


def _all_gather(shard, *, name):
    def body(x_ref, out_ref, send_sems, recv_sems, local_sem):
        me = _mesh_position()
        mine = out_ref.at[_linear(me)]
        local = pltpu.make_async_copy(x_ref, mine, local_sem)
        local.start()
        copies = []
        for k in range(1, N_DEV):
            cp = pltpu.make_async_remote_copy(src_ref=x_ref, dst_ref=mine, send_sem=send_sems.at[k - 1],
                                              recv_sem=recv_sems.at[k - 1], device_id=_flip(me, k), device_id_type=MESH_ID)
            cp.start()
            copies.append(cp)
        for cp in copies:
            cp.wait()
        local.wait()

    return pl.pallas_call(
        body,
        out_shape=jax.ShapeDtypeStruct((N_DEV,) + shard.shape, shard.dtype),
        in_specs=[ANY_SPACE],
        out_specs=ANY_SPACE,
        scratch_shapes=_comm_scratch(),
        name=name,
    )(shard)


def _exchange(parts, *, name):
    def body(p_ref, out_ref, send_sems, recv_sems, local_sem):
        me = _mesh_position()
        mine = out_ref.at[_linear(me)]
        local = pltpu.make_async_copy(p_ref.at[_linear(me)], mine, local_sem)
        local.start()
        copies = []
        for k in range(1, N_DEV):
            peer = _flip(me, k)
            cp = pltpu.make_async_remote_copy(src_ref=p_ref.at[_linear(peer)], dst_ref=mine, send_sem=send_sems.at[k - 1],
                                              recv_sem=recv_sems.at[k - 1], device_id=peer, device_id_type=MESH_ID)
            cp.start()
            copies.append(cp)
        for cp in copies:
            cp.wait()
        local.wait()

    return pl.pallas_call(
        body,
        out_shape=jax.ShapeDtypeStruct(parts.shape, parts.dtype),
        in_specs=[ANY_SPACE],
        out_specs=ANY_SPACE,
        scratch_shapes=_comm_scratch(),
        name=name,
    )(parts)


def _adamw(parts, w, m, v, *, name):
    rows, n = w.shape
    tb = _pick(rows, (400, 16))
    c1 = 1.0 - ADAM_B1 ** ADAM_STEP
    c2 = 1.0 - ADAM_B2 ** ADAM_STEP

    def body(p_ref, w_ref, m_ref, v_ref, g_ref, d_ref, nm_ref, nv_ref):
        g = p_ref[0].astype(F32)
        for s in range(1, N_DEV):
            g = g + p_ref[s].astype(F32)
        m_new = ADAM_B1 * m_ref[...] + (1.0 - ADAM_B1) * g
        v_new = ADAM_B2 * v_ref[...] + (1.0 - ADAM_B2) * (g * g)
        m_hat = m_new / c1
        v_hat = v_new / c2
        g_ref[...] = g
        nm_ref[...] = m_new
        nv_ref[...] = v_new
        d_ref[...] = -ADAM_LR * (m_hat / (jnp.sqrt(v_hat) + ADAM_EPS) + ADAM_WD * w_ref[...])

    blk = pl.BlockSpec((tb, n), lambda i: (i, 0))
    return pl.pallas_call(
        body,
        grid=(rows // tb,),
        in_specs=[pl.BlockSpec((N_DEV, tb, n), lambda i: (0, i, 0)), blk, blk, blk],
        out_specs=[blk] * 4,
        out_shape=[jax.ShapeDtypeStruct((rows, n), F32)] * 4,
        compiler_params=_cparams(("parallel",)),
        name=name,
    )(parts, w, m, v)


PACK_WIDTH = 1024
SHARDED = {
    "gdn_w_in": ((2, D_MODEL, GDN_IN_WIDTH), 2),
    "gdn_conv_w": ((2, GDN_CONV, GDN_QKV), 2),
    "gdn_w_out": ((2, GDN_HEADS * GDN_DV, D_MODEL), 1),
    "dil_w_in": ((2, D_MODEL, 3 * len(DIL_GROUPS) * DIL_HEADS * DIL_DH), 2),
    "dil_w_out": ((2, DIL_HEADS * DIL_DH, D_MODEL), 2),
    "ffn_w_in": ((DEPTH, D_MODEL, 2 * FFN_HIDDEN), 2),
    "ffn_w_out": ((DEPTH, FFN_HIDDEN, D_MODEL), 1),
}
REPLICATED = {"norm_mix": (DEPTH, D_MODEL), "norm_ffn": (DEPTH, D_MODEL), "gdn_a_log": (2, GDN_HEADS),
              "gdn_dt_bias": (2, GDN_HEADS), "gdn_norm_w": (2, GDN_DV), "dil_q_norm": (2, DIL_DH), "dil_k_norm": (2, DIL_DH)}
WEIGHT_ORDER = ("norm_mix", "norm_ffn", "gdn_w_in", "gdn_conv_w", "gdn_a_log", "gdn_dt_bias", "gdn_norm_w", "gdn_w_out",
                "dil_w_in", "dil_q_norm", "dil_k_norm", "dil_w_out", "ffn_w_in", "ffn_w_out")
PACK_ROW_ALIGN = 400
SMALL_ROWS = 16


def _shard_shape(name):
    shape, axis = SHARDED[name]
    return tuple(s // N_DEV if i == axis else s for i, s in enumerate(shape))


def _shard_rows(name):
    return math.prod(_shard_shape(name)) // PACK_WIDTH


def _padded_rows(rows):
    return -(-rows // PACK_ROW_ALIGN) * PACK_ROW_ALIGN


def _split_shards(full, name):
    shape, axis = SHARDED[name]
    split = full.reshape(shape[:axis] + (N_DEV, shape[axis] // N_DEV) + shape[axis + 1:])
    return jnp.moveaxis(split, axis, 0)


def _join_shards(stacked, name):
    shape, axis = SHARDED[name]
    return jnp.moveaxis(stacked, 0, axis).reshape(shape)


COLUMN_SHARDED = ("gdn_w_in", "dil_w_in", "dil_w_out", "ffn_w_in")


def _to_rows(shard, name):
    if name in COLUMN_SHARDED:
        shard = jnp.swapaxes(shard, 1, 2)
    return shard.reshape(-1, PACK_WIDTH)


def _from_rows(rows, name):
    layers, r, c = _shard_shape(name)
    if name in COLUMN_SHARDED:
        return jnp.swapaxes(rows.reshape(layers, c, r), 1, 2)
    return rows.reshape(layers, r, c)


def _layer_columns(name):
    _, r, c = _shard_shape(name)
    return r if name in COLUMN_SHARDED else c


def _pack_rows(pieces, lead=()):
    buf = jnp.concatenate(pieces, axis=len(lead))
    rows = buf.shape[len(lead)]
    pad = [(0, 0)] * len(lead) + [(0, _padded_rows(rows) - rows), (0, 0)]
    return jnp.pad(buf, pad)


def _unpack_rows(buf, names):
    out, at = {}, 0
    for n in names:
        rows = _shard_rows(n)
        out[n] = _from_rows(buf[at:at + rows], n)
        at += rows
    return out


def _pack_small(vals):
    tail = jnp.concatenate([vals[n].astype(F32).reshape(-1) for n in REPLICATED if n not in ("norm_mix", "norm_ffn")])
    tail = jnp.pad(tail, (0, PACK_WIDTH - tail.shape[0])).reshape(1, PACK_WIDTH)
    buf = jnp.concatenate([vals["norm_mix"].astype(F32), vals["norm_ffn"].astype(F32), tail], axis=0)
    return jnp.pad(buf, ((0, SMALL_ROWS - buf.shape[0]), (0, 0)))


def _unpack_small(buf):
    out = {"norm_mix": buf[0:DEPTH], "norm_ffn": buf[DEPTH:2 * DEPTH]}
    at = 0
    for n, shape in REPLICATED.items():
        if n in out:
            continue
        size = math.prod(shape)
        out[n] = buf[2 * DEPTH, at:at + size].reshape(shape)
        at += size
    return out


def kernel(x, norm_mix, norm_ffn, gdn_w_in, gdn_conv_w, gdn_a_log, gdn_dt_bias, gdn_norm_w, gdn_w_out, dil_w_in, dil_q_norm, dil_k_norm, dil_w_out, ffn_w_in, ffn_w_out, loss_target, m_norm_mix, m_norm_ffn, m_gdn_w_in, m_gdn_conv_w, m_gdn_a_log, m_gdn_dt_bias, m_gdn_norm_w, m_gdn_w_out, m_dil_w_in, m_dil_q_norm, m_dil_k_norm, m_dil_w_out, m_ffn_w_in, m_ffn_w_out, v_norm_mix, v_norm_ffn, v_gdn_w_in, v_gdn_conv_w, v_gdn_a_log, v_gdn_dt_bias, v_gdn_norm_w, v_gdn_w_out, v_dil_w_in, v_dil_q_norm, v_dil_k_norm, v_dil_w_out, v_ffn_w_in, v_ffn_w_out):
    w = dict(norm_mix=norm_mix, norm_ffn=norm_ffn, gdn_w_in=gdn_w_in, gdn_conv_w=gdn_conv_w, gdn_a_log=gdn_a_log,
             gdn_dt_bias=gdn_dt_bias, gdn_norm_w=gdn_norm_w, gdn_w_out=gdn_w_out, dil_w_in=dil_w_in, dil_q_norm=dil_q_norm,
             dil_k_norm=dil_k_norm, dil_w_out=dil_w_out, ffn_w_in=ffn_w_in, ffn_w_out=ffn_w_out)
    m = dict(norm_mix=m_norm_mix, norm_ffn=m_norm_ffn, gdn_w_in=m_gdn_w_in, gdn_conv_w=m_gdn_conv_w, gdn_a_log=m_gdn_a_log,
             gdn_dt_bias=m_gdn_dt_bias, gdn_norm_w=m_gdn_norm_w, gdn_w_out=m_gdn_w_out, dil_w_in=m_dil_w_in,
             dil_q_norm=m_dil_q_norm, dil_k_norm=m_dil_k_norm, dil_w_out=m_dil_w_out, ffn_w_in=m_ffn_w_in, ffn_w_out=m_ffn_w_out)
    v = dict(norm_mix=v_norm_mix, norm_ffn=v_norm_ffn, gdn_w_in=v_gdn_w_in, gdn_conv_w=v_gdn_conv_w, gdn_a_log=v_gdn_a_log,
             gdn_dt_bias=v_gdn_dt_bias, gdn_norm_w=v_gdn_norm_w, gdn_w_out=v_gdn_w_out, dil_w_in=v_dil_w_in,
             dil_q_norm=v_dil_q_norm, dil_k_norm=v_dil_k_norm, dil_w_out=v_dil_w_out, ffn_w_in=v_ffn_w_in, ffn_w_out=v_ffn_w_out)
    big = tuple(SHARDED)

    def as_operand(name, a):
        if name == "gdn_conv_w":
            return lax.bitcast_convert_type(a, BF16).reshape(-1, PACK_WIDTH)
        return _to_rows(a.astype(BF16), name)

    gathered = _all_gather(_pack_rows([as_operand(n, w[n]) for n in big]), name="weight_all_gather")
    full, at = {}, 0
    for n in big:
        if n == "gdn_conv_w":
            rows = 2 * _shard_rows(n)
            piece = gathered[:, at:at + rows].reshape((N_DEV,) + _shard_shape(n) + (2,))
            full[n] = _join_shards(lax.bitcast_convert_type(piece, F32), n)
        else:
            rows = _shard_rows(n)
            per_layer = rows // SHARDED[n][0][0]
            full[n] = [gathered[:, at + l * per_layer:at + (l + 1) * per_layer].reshape(-1, _layer_columns(n))
                       for l in range(SHARDED[n][0][0])]
        at += rows
    for n in REPLICATED:
        full[n] = w[n]
    prepared = _prepare_weights(full)

    loss, grad_x, grads = _local_step(x[0], loss_target[0], prepared, norm_mix, norm_ffn)

    pieces = []
    for n in big:
        if n == "gdn_conv_w":
            pieces.append(_split_shards(grads[n], n).astype(BF16).reshape(N_DEV, -1, PACK_WIDTH))
        else:
            pieces.extend(g.astype(BF16).reshape(N_DEV, -1, PACK_WIDTH) for g in grads[n])
    received = _exchange(_pack_rows(pieces, lead=(N_DEV,)), name="grad_exchange")
    packed = [_pack_rows([_to_rows(src[n].astype(F32), n) for n in big]) for src in (w, m, v)]
    outs_big = [_unpack_rows(o, big) for o in _adamw(received, *packed, name="adamw_sharded")]

    small_parts = _all_gather(_pack_small(grads), name="small_grad_all_gather")
    outs_small = [_unpack_small(o) for o in
                  _adamw(small_parts, _pack_small(w), _pack_small(m), _pack_small(v), name="adamw_replicated")]

    total_loss = lax.psum(loss, ("x", "y", "c"))
    result = [total_loss, grad_x[None]]
    for k in range(4):
        for n in WEIGHT_ORDER:
            result.append(outs_big[k][n] if n in SHARDED else outs_small[k][n])
    return tuple(result)
```

```python
import functools
import math

import jax
import jax.numpy as jnp
from jax import lax
from jax.experimental import pallas as pl
from jax.experimental.pallas import tpu as pltpu

F32 = jnp.float32
BF16 = jnp.bfloat16
MM_DTYPE = BF16

N_DEV = 8
D_MODEL = 1024
DEPTH = 4
RMS_EPS = 1e-6
L2_EPS = 1e-6

LANES = 128

GDN_HEADS = 8
GDN_DK = 128
GDN_DV = 128
GDN_CONV = 4
GDN_CHUNK = 128
GDN_QKV = 3 * GDN_HEADS * GDN_DK
GDN_MAIN = GDN_QKV + GDN_HEADS * GDN_DV
GDN_IN_WIDTH = GDN_MAIN + 2 * GDN_HEADS

DIL_GROUPS = ((128, 1), (512, 4), (2048, 16))
DIL_HEADS = 8
DIL_DH = 64
DIL_SPAN = 128
DIL_SLAB = 3 * DIL_HEADS * LANES
ALIBI_MAX_BIAS = 8.0

FFN_HIDDEN = 2816

ADAM_LR = 0.001
ADAM_B1 = 0.9
ADAM_B2 = 0.999
ADAM_EPS = 1e-08
ADAM_WD = 0.01
ADAM_STEP = 10

VMEM_LIMIT = 56 * 1024 * 1024
NEG = -1e30
HI = lax.Precision.HIGHEST


def _cparams(sem):
    return pltpu.CompilerParams(dimension_semantics=sem, vmem_limit_bytes=VMEM_LIMIT)


def _dot(a, b):
    return lax.dot_general(a, b, (((1,), (0,)), ((), ())), preferred_element_type=F32, precision=HI)


def _dot_nt(a, b):
    return lax.dot_general(a, b, (((1,), (1,)), ((), ())), preferred_element_type=F32, precision=HI)


def _dot_tn(a, b):
    return lax.dot_general(a, b, (((0,), (0,)), ((), ())), preferred_element_type=F32, precision=HI)


def _single_pass(a, b, a_dim, b_dim):
    lead = a.ndim - 2
    batch = ((0,), (0,)) if lead else ((), ())
    return lax.dot_general(a.astype(BF16), b.astype(BF16), (((lead + a_dim,), (lead + b_dim,)), batch),
                           preferred_element_type=F32)


def _bdot(a, b):
    return _single_pass(a, b, 1, 0)


def _bdot_nt(a, b):
    return _single_pass(a, b, 1, 1)


def _bdot_tn(a, b):
    return _single_pass(a, b, 0, 0)


def _pick(n, candidates):
    for c in candidates:
        if n % c == 0:
            return c
    raise ValueError(f"no tile for {n}")


def _matmul(a, b, *, name, trans_a=False, trans_b=False, b_rows=None, add=None, out_dtype=F32):
    if trans_a:
        k_dim, m_dim = a.shape
    else:
        m_dim, k_dim = a.shape
    b_start, b_size = b_rows if b_rows is not None else (0, b.shape[0])
    if trans_b:
        n_dim, k2 = b_size, b.shape[1]
    else:
        k2, n_dim = b_size, b.shape[1]
    assert k_dim == k2, (a.shape, b.shape, b_rows)
    tn = _pick(n_dim, (1024, 512, 256, 128))
    tm = min(m_dim, 2048, max(512, (1024 * 1024) // tn))
    tm = _pick(m_dim, (tm, 1408, 1024, 512, 256, 128))
    tk = _pick(k_dim, (1024, 1408, 512, 256, 128))
    nk = k_dim // tk
    has_add = add is not None
    dn = (((0 if trans_a else 1,), (1 if trans_b else 0,)), ((), ()))
    b_tile = tn if trans_b else tk
    assert b_start % b_tile == 0, (b_rows, b_tile)
    b_off = b_start // b_tile

    def body(*refs):
        if has_add:
            a_ref, b_ref, add_ref, o_ref, acc_ref = refs
        else:
            a_ref, b_ref, o_ref, acc_ref = refs
        part = lax.dot_general(a_ref[...], b_ref[...], dn, preferred_element_type=F32)

        def finish(total):
            if has_add:
                total = total + add_ref[...]
            o_ref[...] = total.astype(out_dtype)

        if nk == 1:
            finish(part)
        else:
            k = pl.program_id(2)

            @pl.when(k == 0)
            def _():
                acc_ref[...] = part

            @pl.when(k > 0)
            def _():
                acc_ref[...] += part

            @pl.when(k == nk - 1)
            def _():
                finish(acc_ref[...])

    if trans_a:
        a_spec = pl.BlockSpec((tk, tm), lambda i, j, k: (k, i))
    else:
        a_spec = pl.BlockSpec((tm, tk), lambda i, j, k: (i, k))
    if trans_b:
        b_spec = pl.BlockSpec((tn, tk), lambda i, j, k: (j + b_off, k))
    else:
        b_spec = pl.BlockSpec((tk, tn), lambda i, j, k: (k + b_off, j))
    in_specs = [a_spec, b_spec]
    args = [a, b]
    if has_add:
        in_specs.append(pl.BlockSpec((tm, tn), lambda i, j, k: (i, j)))
        args.append(add)
    return pl.pallas_call(
        body,
        grid=(m_dim // tm, n_dim // tn, nk),
        in_specs=in_specs,
        out_specs=pl.BlockSpec((tm, tn), lambda i, j, k: (i, j)),
        out_shape=jax.ShapeDtypeStruct((m_dim, n_dim), out_dtype),
        scratch_shapes=[pltpu.VMEM((tm, tn) if nk > 1 else (8, LANES), F32)],
        compiler_params=_cparams(("parallel", "parallel", "arbitrary")),
        name=name,
    )(*args)


def _rmsnorm_fwd(x, w_row, *, name):
    t, d = x.shape
    tb = min(t, 1024)

    def body(x_ref, w_ref, o_ref):
        xf = x_ref[...]
        r = lax.rsqrt(jnp.mean(xf * xf, axis=-1, keepdims=True) + RMS_EPS)
        o_ref[...] = (xf * r * w_ref[...]).astype(o_ref.dtype)

    return pl.pallas_call(
        body,
        grid=(t // tb,),
        in_specs=[pl.BlockSpec((tb, d), lambda i: (i, 0)), pl.BlockSpec((1, d), lambda i: (0, 0))],
        out_specs=pl.BlockSpec((tb, d), lambda i: (i, 0)),
        out_shape=jax.ShapeDtypeStruct((t, d), MM_DTYPE),
        compiler_params=_cparams(("parallel",)),
        name=name,
    )(x, w_row)


def _rmsnorm_bwd(x, w_row, dy, dskip, *, name):
    t, d = x.shape
    tb = min(t, 512)

    def body(x_ref, w_ref, dy_ref, ds_ref, dx_ref, dxb_ref, dw_ref):
        xf = x_ref[...]
        g = dy_ref[...]
        r = lax.rsqrt(jnp.mean(xf * xf, axis=-1, keepdims=True) + RMS_EPS)
        gw = g * w_ref[...]
        proj = jnp.mean(gw * xf, axis=-1, keepdims=True)
        dx = r * gw - xf * (r * r * r * proj) + ds_ref[...]
        dx_ref[...] = dx
        dxb_ref[...] = dx.astype(dxb_ref.dtype)
        part = jnp.sum(g * xf * r, axis=0, keepdims=True)

        @pl.when(pl.program_id(0) == 0)
        def _():
            dw_ref[...] = part

        @pl.when(pl.program_id(0) > 0)
        def _():
            dw_ref[...] += part

    row = pl.BlockSpec((tb, d), lambda i: (i, 0))
    one = pl.BlockSpec((1, d), lambda i: (0, 0))
    return pl.pallas_call(
        body,
        grid=(t // tb,),
        in_specs=[row, one, row, row],
        out_specs=[row, row, one],
        out_shape=[jax.ShapeDtypeStruct((t, d), F32), jax.ShapeDtypeStruct((t, d), MM_DTYPE),
                   jax.ShapeDtypeStruct((1, d), F32)],
        compiler_params=_cparams(("arbitrary",)),
        name=name,
    )(x, w_row, dy, dskip)


def _silu(z):
    return z / (1.0 + jnp.exp(-z))


def _swiglu_fwd(gu, *, name):
    t = gu.shape[0]
    h = FFN_HIDDEN
    tb, tc = min(t, 1024), 256
    nc = h // tc

    def body(g_ref, u_ref, o_ref):
        o_ref[...] = (_silu(g_ref[...]) * u_ref[...]).astype(o_ref.dtype)

    return pl.pallas_call(
        body,
        grid=(t // tb, nc),
        in_specs=[pl.BlockSpec((tb, tc), lambda i, j: (i, j)), pl.BlockSpec((tb, tc), lambda i, j: (i, j + nc))],
        out_specs=pl.BlockSpec((tb, tc), lambda i, j: (i, j)),
        out_shape=jax.ShapeDtypeStruct((t, h), MM_DTYPE),
        compiler_params=_cparams(("parallel", "parallel")),
        name=name,
    )(gu, gu)


def _swiglu_bwd(gu, dact, *, name):
    t = gu.shape[0]
    h = FFN_HIDDEN
    tb, tc = min(t, 1024), 256
    nc = h // tc

    def body(g_ref, u_ref, da_ref, dg_ref, du_ref):
        g = g_ref[...]
        da = da_ref[...]
        sig = 1.0 / (1.0 + jnp.exp(-g))
        sg = g * sig
        dg_ref[...] = (da * u_ref[...] * (sig + sg * (1.0 - sig))).astype(dg_ref.dtype)
        du_ref[...] = (da * sg).astype(du_ref.dtype)

    blk = pl.BlockSpec((tb, tc), lambda i, j: (i, j))
    return pl.pallas_call(
        body,
        grid=(t // tb, nc),
        in_specs=[blk, pl.BlockSpec((tb, tc), lambda i, j: (i, j + nc)), blk],
        out_specs=[blk, blk],
        out_shape=[jax.ShapeDtypeStruct((t, h), MM_DTYPE)] * 2,
        compiler_params=_cparams(("parallel", "parallel")),
        name=name,
    )(gu, gu, dact)


def _loss_head(y, target, *, name):
    t, d = y.shape
    tb = min(t, 1024)

    def body(y_ref, t_ref, dy_ref, dyb_ref, l_ref):
        err = y_ref[...] - t_ref[...]
        dy_ref[...] = err * (1.0 / d)
        dyb_ref[...] = (err * (1.0 / d)).astype(dyb_ref.dtype)
        part = jnp.sum(jnp.sum(err * err, axis=0, keepdims=True), axis=1, keepdims=True) * (0.5 / d)
        part = jnp.broadcast_to(part, l_ref.shape)

        @pl.when(pl.program_id(0) == 0)
        def _():
            l_ref[...] = part

        @pl.when(pl.program_id(0) > 0)
        def _():
            l_ref[...] += part

    row = pl.BlockSpec((tb, d), lambda i: (i, 0))
    return pl.pallas_call(
        body,
        grid=(t // tb,),
        in_specs=[row, row],
        out_specs=[row, row, pl.BlockSpec((8, LANES), lambda i: (0, 0))],
        out_shape=[jax.ShapeDtypeStruct((t, d), F32), jax.ShapeDtypeStruct((t, d), MM_DTYPE),
                   jax.ShapeDtypeStruct((8, LANES), F32)],
        compiler_params=_cparams(("arbitrary",)),
        name=name,
    )(y, target)


CONV_HALO = 8


def _conv_tile_scale(c):
    is_qk = c < 2 * GDN_HEADS
    scale = jnp.where(c < GDN_HEADS, GDN_DK ** -0.5, 1.0).astype(F32)
    return is_qk, scale


def _gdn_conv_fwd(pm, conv_w, *, name):
    t = pm.shape[0]
    tb = min(t, 1024)
    nt = t // tb
    hb = tb // CONV_HALO

    def body(x_ref, xp_ref, w_ref, o_ref):
        c = pl.program_id(0)
        ti = pl.program_id(1)
        prev = jnp.where(ti > 0, xp_ref[...], 0.0)
        xe = jnp.concatenate([prev, x_ref[...]], axis=0)
        w = w_ref[...]
        y = jnp.zeros((tb, LANES), F32)
        for j in range(GDN_CONV):
            off = CONV_HALO - (GDN_CONV - 1) + j
            y = y + w[j:j + 1, :] * xe[off:off + tb, :]
        s = _silu(y)
        is_qk, scale = _conv_tile_scale(c)
        r = lax.rsqrt(jnp.sum(s * s, axis=-1, keepdims=True) + L2_EPS) * scale
        o_ref[...] = s * jnp.where(is_qk, r, 1.0)

    return pl.pallas_call(
        body,
        grid=(GDN_QKV // LANES, nt),
        in_specs=[
            pl.BlockSpec((tb, LANES), lambda c, i: (i, c)),
            pl.BlockSpec((CONV_HALO, LANES), lambda c, i: (jnp.maximum(i * hb - 1, 0), c)),
            pl.BlockSpec((GDN_CONV, LANES), lambda c, i: (0, c)),
        ],
        out_specs=pl.BlockSpec((tb, LANES), lambda c, i: (i, c)),
        out_shape=jax.ShapeDtypeStruct((t, GDN_QKV), F32),
        compiler_params=_cparams(("parallel", "parallel")),
        name=name,
    )(pm, pm, conv_w)


def _gdn_conv_bwd(pm, conv_w, dout, *, name):
    t = pm.shape[0]
    tb = min(t, 1024)
    nt = t // tb
    hb = tb // CONV_HALO
    last_hb = t // CONV_HALO - 1
    ext = tb + CONV_HALO

    def body(x_ref, xp_ref, xn_ref, d_ref, dn_ref, w_ref, dx_ref, dw_ref):
        c = pl.program_id(0)
        ti = pl.program_id(1)
        prev = jnp.where(ti > 0, xp_ref[...], 0.0)
        has_next = ti < nt - 1
        nxt = jnp.where(has_next, xn_ref[...], 0.0)
        xe = jnp.concatenate([prev, x_ref[...], nxt], axis=0)
        de = jnp.concatenate([d_ref[...], jnp.where(has_next, dn_ref[...], 0.0)], axis=0)
        w = w_ref[...]
        y = jnp.zeros((ext, LANES), F32)
        for j in range(GDN_CONV):
            off = CONV_HALO - (GDN_CONV - 1) + j
            y = y + w[j:j + 1, :] * xe[off:off + ext, :]
        sig = 1.0 / (1.0 + jnp.exp(-y))
        s = y * sig
        is_qk, scale = _conv_tile_scale(c)
        r = lax.rsqrt(jnp.sum(s * s, axis=-1, keepdims=True) + L2_EPS)
        n = s * r
        dnrm = de * scale
        ds_qk = r * (dnrm - n * jnp.sum(dnrm * n, axis=-1, keepdims=True))
        ds = jnp.where(is_qk, ds_qk, de)
        dy = ds * (sig + s * (1.0 - sig))
        dx = jnp.zeros((tb, LANES), F32)
        dw_rows = []
        for j in range(GDN_CONV):
            sh = GDN_CONV - 1 - j
            dx = dx + w[j:j + 1, :] * dy[sh:sh + tb, :]
            off = CONV_HALO - (GDN_CONV - 1) + j
            dw_rows.append(jnp.sum(dy[:tb, :] * xe[off:off + tb, :], axis=0, keepdims=True))
        dx_ref[...] = dx.astype(dx_ref.dtype)
        part = jnp.concatenate(dw_rows, axis=0)

        @pl.when(ti == 0)
        def _():
            dw_ref[...] = part

        @pl.when(ti > 0)
        def _():
            dw_ref[...] += part

    main = pl.BlockSpec((tb, LANES), lambda c, i: (i, c))
    prev = pl.BlockSpec((CONV_HALO, LANES), lambda c, i: (jnp.maximum(i * hb - 1, 0), c))
    nxt = pl.BlockSpec((CONV_HALO, LANES), lambda c, i: (jnp.minimum((i + 1) * hb, last_hb), c))
    return pl.pallas_call(
        body,
        grid=(GDN_QKV // LANES, nt),
        in_specs=[main, prev, nxt, main, nxt, pl.BlockSpec((GDN_CONV, LANES), lambda c, i: (0, c))],
        out_specs=[main, pl.BlockSpec((GDN_CONV, LANES), lambda c, i: (0, c))],
        out_shape=[jax.ShapeDtypeStruct((t, GDN_QKV), MM_DTYPE), jax.ShapeDtypeStruct((GDN_CONV, GDN_QKV), F32)],
        compiler_params=_cparams(("parallel", "arbitrary")),
        name=name,
    )(pm, pm, pm, dout, dout, conv_w)


def _head_selector(first_col):
    row = lax.broadcasted_iota(jnp.int32, (LANES, GDN_HEADS * LANES), 0)
    col = lax.broadcasted_iota(jnp.int32, (LANES, GDN_HEADS * LANES), 1)
    return (col // LANES + first_col == row).astype(F32)


def _softplus(x):
    return jnp.maximum(x, 0.0) + jnp.log(1.0 + jnp.exp(-jnp.abs(x)))


def _gdn_gates_fwd(ab, alog_row, dt_row, *, name):
    t = ab.shape[0]
    tb = min(t, 1024)
    wide = GDN_HEADS * LANES

    def body(ab_ref, al_ref, dt_ref, g_ref, b_ref):
        x = ab_ref[...]
        g_cols = -jnp.exp(al_ref[...]) * _softplus(x + dt_ref[...])
        b_cols = 1.0 / (1.0 + jnp.exp(-x))
        g_ref[...] = _dot(g_cols, _head_selector(0))
        b_ref[...] = _dot(b_cols, _head_selector(GDN_HEADS))

    row = pl.BlockSpec((tb, LANES), lambda i: (i, 0))
    one = pl.BlockSpec((1, LANES), lambda i: (0, 0))
    out = pl.BlockSpec((tb, wide), lambda i: (i, 0))
    return pl.pallas_call(
        body,
        grid=(t // tb,),
        in_specs=[row, one, one],
        out_specs=[out, out],
        out_shape=[jax.ShapeDtypeStruct((t, wide), F32)] * 2,
        compiler_params=_cparams(("parallel",)),
        name=name,
    )(ab, alog_row, dt_row)


def _gdn_gates_bwd(ab, alog_row, dt_row, dgb, dbb, *, name):
    t = ab.shape[0]
    tb = min(t, 1024)
    wide = GDN_HEADS * LANES

    def body(ab_ref, al_ref, dt_ref, dg_ref, db_ref, dab_ref, dal_ref, ddt_ref):
        x = ab_ref[...]
        lane = lax.broadcasted_iota(jnp.int32, (tb, LANES), 1)
        dg_cols = _dot_nt(dg_ref[...], _head_selector(0))
        db_cols = _dot_nt(db_ref[...], _head_selector(GDN_HEADS))
        ea = jnp.exp(al_ref[...])
        z = x + dt_ref[...]
        sp = _softplus(z)
        sg = 1.0 / (1.0 + jnp.exp(-z))
        beta = 1.0 / (1.0 + jnp.exp(-x))
        da = jnp.where(lane < GDN_HEADS, dg_cols * (-ea) * sg, 0.0)
        db = jnp.where((lane >= GDN_HEADS) & (lane < 2 * GDN_HEADS), db_cols * beta * (1.0 - beta), 0.0)
        dab_ref[...] = (da + db).astype(dab_ref.dtype)
        p_al = jnp.sum(jnp.where(lane < GDN_HEADS, dg_cols * (-ea) * sp, 0.0), axis=0, keepdims=True)
        p_dt = jnp.sum(da, axis=0, keepdims=True)

        @pl.when(pl.program_id(0) == 0)
        def _():
            dal_ref[...] = p_al
            ddt_ref[...] = p_dt

        @pl.when(pl.program_id(0) > 0)
        def _():
            dal_ref[...] += p_al
            ddt_ref[...] += p_dt

    row = pl.BlockSpec((tb, LANES), lambda i: (i, 0))
    one = pl.BlockSpec((1, LANES), lambda i: (0, 0))
    big = pl.BlockSpec((tb, wide), lambda i: (i, 0))
    return pl.pallas_call(
        body,
        grid=(t // tb,),
        in_specs=[row, one, one, big, big],
        out_specs=[row, one, one],
        out_shape=[jax.ShapeDtypeStruct((t, LANES), MM_DTYPE), jax.ShapeDtypeStruct((1, LANES), F32),
                   jax.ShapeDtypeStruct((1, LANES), F32)],
        compiler_params=_cparams(("arbitrary",)),
        name=name,
    )(ab, alog_row, dt_row, dgb, dbb)


@jax.custom_vjp
def _unit_lower_inverse_rest(n):
    c = n.shape[-1]
    ri = lax.broadcasted_iota(jnp.int32, (c, c), 0)
    ci = lax.broadcasted_iota(jnp.int32, (c, c), 1)
    rest = None
    size = 1
    while size < c:
        joins = ((ri // (2 * size)) == (ci // (2 * size))) & ((ri // size) != (ci // size))
        low = jnp.where(joins, n, 0.0)
        if rest is None:
            rest = -low
        else:
            left = low + _bdot(rest, low)
            rest = rest - (left + _bdot(left, rest))
        size *= 2
    return rest


def _unit_lower_inverse_rest_fwd(n):
    rest = _unit_lower_inverse_rest(n)
    return rest, rest


def _unit_lower_inverse_rest_bwd(rest, ct):
    left = ct + _bdot_tn(rest, ct)
    return (-(left + _bdot_nt(left, rest)),)


_unit_lower_inverse_rest.defvjp(_unit_lower_inverse_rest_fwd, _unit_lower_inverse_rest_bwd)


def _bf16_pieces(x):
    hi = x.astype(BF16)
    r1 = x - hi.astype(F32)
    mid = r1.astype(BF16)
    lo = (r1 - mid.astype(F32)).astype(BF16)
    return hi, mid, lo


def _lower_ones(shape):
    c = shape[-1]
    ri = lax.broadcasted_iota(jnp.int32, (c, c), 0)
    ci = lax.broadcasted_iota(jnp.int32, (c, c), 1)
    return jnp.broadcast_to((ri >= ci).astype(BF16), shape)


@jax.custom_vjp
def _running_sum(x):
    tri = _lower_ones(x.shape)
    return sum(_bdot(tri, p) for p in _bf16_pieces(x))


def _running_sum_fwd(x):
    return _running_sum(x), None


def _running_sum_bwd(_, ct):
    tri = _lower_ones(ct.shape)
    return (sum(_bdot_tn(tri, p) for p in _bf16_pieces(ct)),)


_running_sum.defvjp(_running_sum_fwd, _running_sum_bwd)


def _gdn_prep_math(q, k, v, gb, bb):
    c = GDN_CHUNK
    ri = lax.broadcasted_iota(jnp.int32, (c, c), 0)
    ci = lax.broadcasted_iota(jnp.int32, (c, c), 1)
    causal = ri >= ci
    gc = _running_sum(gb)
    decay = jnp.exp(jnp.where(causal, gc - jnp.swapaxes(gc, -1, -2), NEG))
    n = jnp.where(ri > ci, _bdot_nt(k, k) * bb * decay, 0.0)
    rest = _unit_lower_inverse_rest(n)
    eg = jnp.exp(gc)
    rhs_v = v * bb
    rhs_k = k * bb * eg
    u = rhs_v + _bdot(rest, rhs_v)
    w = rhs_k + _bdot(rest, rhs_k)
    qk = _bdot_nt(q, k) * decay
    qd = q * eg
    last = jnp.sum(jnp.where(ri == c - 1, gc, 0.0), axis=-2, keepdims=True)
    gl = jnp.broadcast_to(last, gc.shape)
    kt = k * jnp.exp(gl - gc)
    cd = jnp.exp(gl)
    return u, w, qk, qd, kt, cd


def _head_tiles(ref, h):
    return ref[:, h * LANES:(h + 1) * LANES]


def _stack_heads(ref, first=0, heads=GDN_HEADS):
    return jnp.stack([_head_tiles(ref, first + h) for h in range(heads)])


def _store_heads(ref, val, first=0):
    for h in range(val.shape[0]):
        ref[:, (first + h) * LANES:(first + h + 1) * LANES] = val[h].astype(ref.dtype)


def _gdn_prep_fwd(qkv, gb, bb, *, name):
    t = qkv.shape[0]
    c = GDN_CHUNK
    wide = GDN_HEADS * LANES

    def body(q_ref, k_ref, v_ref, g_ref, b_ref, *outs):
        res = _gdn_prep_math(*(_stack_heads(r) for r in (q_ref, k_ref, v_ref, g_ref, b_ref)))
        for o_ref, val in zip(outs, res):
            _store_heads(o_ref, val)

    blk = lambda off: pl.BlockSpec((c, wide), lambda i: (i, off))
    return pl.pallas_call(
        body,
        grid=(t // c,),
        in_specs=[blk(0), blk(1), blk(2), blk(0), blk(0)],
        out_specs=[blk(0)] * 6,
        out_shape=[jax.ShapeDtypeStruct((t, wide), F32)] * 6,
        compiler_params=_cparams(("parallel",)),
        name=name,
    )(qkv, qkv, qkv, gb, bb)


def _gdn_prep_bwd(qkv, gb, bb, cts, *, name):
    t = qkv.shape[0]
    c = GDN_CHUNK
    wide = GDN_HEADS * LANES

    def body(q_ref, k_ref, v_ref, g_ref, b_ref, c0, c1, c2, c3, c4, c5, dqkv_ref, dg_ref, db_ref):
        prim = tuple(_stack_heads(r) for r in (q_ref, k_ref, v_ref, g_ref, b_ref))
        _, pull = jax.vjp(_gdn_prep_math, *prim)
        dq, dk, dv, dg, db = pull(tuple(_stack_heads(r) for r in (c0, c1, c2, c3, c4, c5)))
        _store_heads(dqkv_ref, dq)
        _store_heads(dqkv_ref, dk, first=GDN_HEADS)
        _store_heads(dqkv_ref, dv, first=2 * GDN_HEADS)
        _store_heads(dg_ref, dg)
        _store_heads(db_ref, db)

    blk = lambda off: pl.BlockSpec((c, wide), lambda i: (i, off))
    return pl.pallas_call(
        body,
        grid=(t // c,),
        in_specs=[blk(0), blk(1), blk(2), blk(0), blk(0)] + [blk(0)] * 6,
        out_specs=[pl.BlockSpec((c, 3 * wide), lambda i: (i, 0)), blk(0), blk(0)],
        out_shape=[jax.ShapeDtypeStruct((t, 3 * wide), F32), jax.ShapeDtypeStruct((t, wide), F32),
                   jax.ShapeDtypeStruct((t, wide), F32)],
        compiler_params=_cparams(("parallel",)),
        name=name,
    )(qkv, qkv, qkv, gb, bb, *cts)


def _gdn_scan_math(s, u, w, qk, qd, kt, cd):
    v_new = u - _bdot(w, s)
    o = _bdot(qd, s) + _bdot(qk, v_new)
    s_new = s * cd + _bdot_tn(kt, v_new)
    return o, s_new


def _gdn_scan_fwd(prep, *, name):
    t = prep[0].shape[0]
    c = GDN_CHUNK
    wide = GDN_HEADS * LANES

    def body(u_ref, w_ref, qk_ref, qd_ref, kt_ref, cd_ref, o_ref, st_ref, s_ref):
        @pl.when(pl.program_id(0) == 0)
        def _():
            s_ref[...] = jnp.zeros_like(s_ref)

        s = _stack_heads(s_ref)
        _store_heads(st_ref, s)
        o, s_new = _gdn_scan_math(s, *(_stack_heads(r) for r in (u_ref, w_ref, qk_ref, qd_ref, kt_ref, cd_ref)))
        _store_heads(o_ref, o)
        _store_heads(s_ref, s_new)

    blk = pl.BlockSpec((c, wide), lambda i: (i, 0))
    return pl.pallas_call(
        body,
        grid=(t // c,),
        in_specs=[blk] * 6,
        out_specs=[blk, blk],
        out_shape=[jax.ShapeDtypeStruct((t, wide), F32)] * 2,
        scratch_shapes=[pltpu.VMEM((GDN_DK, wide), F32)],
        compiler_params=_cparams(("arbitrary",)),
        name=name,
    )(*prep)


def _gdn_scan_bwd(prep, states, do, *, name):
    t = do.shape[0]
    c = GDN_CHUNK
    wide = GDN_HEADS * LANES
    nc = t // c

    def body(u_ref, w_ref, qk_ref, qd_ref, kt_ref, cd_ref, st_ref, do_ref, *rest):
        outs, ds_ref = rest[:6], rest[6]

        @pl.when(pl.program_id(0) == 0)
        def _():
            ds_ref[...] = jnp.zeros_like(ds_ref)

        prim = tuple(_stack_heads(r) for r in (st_ref, u_ref, w_ref, qk_ref, qd_ref, kt_ref, cd_ref))
        _, pull = jax.vjp(_gdn_scan_math, *prim)
        grads = pull((_stack_heads(do_ref), _stack_heads(ds_ref)))
        _store_heads(ds_ref, grads[0])
        for o_ref, val in zip(outs, grads[1:]):
            _store_heads(o_ref, val)

    blk = pl.BlockSpec((c, wide), lambda i: (nc - 1 - i, 0))
    return pl.pallas_call(
        body,
        grid=(nc,),
        in_specs=[blk] * 8,
        out_specs=[blk] * 6,
        out_shape=[jax.ShapeDtypeStruct((t, wide), F32)] * 6,
        scratch_shapes=[pltpu.VMEM((GDN_DK, wide), F32)],
        compiler_params=_cparams(("arbitrary",)),
        name=name,
    )(*prep, states, do)


def _gdn_outgate_math(o, z, nw):
    r = lax.rsqrt(jnp.mean(o * o, axis=-1, keepdims=True) + RMS_EPS)
    return o * r * nw * _silu(z)


def _gdn_outgate_fwd(o, pm, nw_row, *, name):
    t = o.shape[0]
    tb = min(t, 1024)
    z_off = GDN_QKV // LANES

    def body(o_ref, z_ref, nw_ref, y_ref):
        y_ref[...] = _gdn_outgate_math(o_ref[...], z_ref[...], nw_ref[...]).astype(y_ref.dtype)

    return pl.pallas_call(
        body,
        grid=(t // tb, GDN_HEADS),
        in_specs=[pl.BlockSpec((tb, LANES), lambda i, h: (i, h)), pl.BlockSpec((tb, LANES), lambda i, h: (i, h + z_off)),
                  pl.BlockSpec((1, LANES), lambda i, h: (0, 0))],
        out_specs=pl.BlockSpec((tb, LANES), lambda i, h: (i, h)),
        out_shape=jax.ShapeDtypeStruct((t, GDN_HEADS * LANES), MM_DTYPE),
        compiler_params=_cparams(("parallel", "parallel")),
        name=name,
    )(o, pm, nw_row)


def _gdn_outgate_bwd(o, pm, nw_row, dy, *, name):
    t = o.shape[0]
    tb = min(t, 1024)
    z_off = GDN_QKV // LANES

    def body(o_ref, z_ref, nw_ref, dy_ref, do_ref, dz_ref, dnw_ref):
        _, pull = jax.vjp(_gdn_outgate_math, o_ref[...], z_ref[...], nw_ref[...])
        d_o, d_z, d_nw = pull(dy_ref[...])
        do_ref[...] = d_o
        dz_ref[...] = d_z.astype(dz_ref.dtype)
        first = (pl.program_id(0) == 0) & (pl.program_id(1) == 0)

        @pl.when(first)
        def _():
            dnw_ref[...] = d_nw

        @pl.when(jnp.logical_not(first))
        def _():
            dnw_ref[...] += d_nw

    blk = pl.BlockSpec((tb, LANES), lambda i, h: (i, h))
    one = pl.BlockSpec((1, LANES), lambda i, h: (0, 0))
    return pl.pallas_call(
        body,
        grid=(t // tb, GDN_HEADS),
        in_specs=[blk, pl.BlockSpec((tb, LANES), lambda i, h: (i, h + z_off)), one, blk],
        out_specs=[blk, blk, one],
        out_shape=[jax.ShapeDtypeStruct((t, GDN_HEADS * LANES), F32),
                   jax.ShapeDtypeStruct((t, GDN_HEADS * LANES), MM_DTYPE), jax.ShapeDtypeStruct((1, LANES), F32)],
        compiler_params=_cparams(("arbitrary", "arbitrary")),
        name=name,
    )(o, pm, nw_row, dy)


def _rms64(x, w_row):
    return x * lax.rsqrt(jnp.sum(x * x, axis=-1, keepdims=True) * (1.0 / DIL_DH) + RMS_EPS) * w_row


def _alibi_slopes(group):
    head = lax.broadcasted_iota(jnp.int32, (DIL_HEADS, 8, LANES), 0).astype(F32)
    rate = -math.log(2.0) * ALIBI_MAX_BIAS / (len(DIL_GROUPS) * DIL_HEADS)
    slope = jnp.exp(rate * (head + float(group * DIL_HEADS + 1)))
    return jnp.broadcast_to(slope[:, 0:1, :], (DIL_HEADS, DIL_SPAN, LANES))


def _band_logits(qn, kp, kc, slope_d, has_prev):
    qi = lax.broadcasted_iota(jnp.int32, (DIL_SPAN, DIL_SPAN), 0)
    kj = lax.broadcasted_iota(jnp.int32, (DIL_SPAN, DIL_SPAN), 1)
    steps_c = (qi - kj).astype(F32)
    scale = DIL_DH ** -0.5
    sp = _bdot_nt(qn, kp) * scale - slope_d * (steps_c + float(DIL_SPAN))
    sc = _bdot_nt(qn, kc) * scale - slope_d * steps_c
    sp = jnp.where((kj >= qi) & has_prev, sp, NEG)
    sc = jnp.where(kj <= qi, sc, NEG)
    return sp, sc


def _dil_attn_fwd(slab, wq_row, wk_row, *, group, name):
    dilation = DIL_GROUPS[group][1]
    t = slab.shape[0]
    rows = t // dilation
    nlb = rows // DIL_SPAN
    wide = DIL_HEADS * LANES
    view = slab.reshape(rows, dilation * DIL_SLAB)

    def body(q_ref, kc_ref, vc_ref, kp_ref, vp_ref, wq_ref, wk_ref, o_ref):
        has_prev = pl.program_id(1) > 0
        lane = lax.broadcasted_iota(jnp.int32, (DIL_SPAN, LANES), 1)
        qn = _rms64(_stack_heads(q_ref), wq_ref[...])
        kc = _rms64(_stack_heads(kc_ref), wk_ref[...])
        kp = _rms64(_stack_heads(kp_ref), wk_ref[...])
        sp, sc = _band_logits(qn, kp, kc, _alibi_slopes(group) * float(dilation), has_prev)
        m = jnp.maximum(jnp.max(sp, axis=-1, keepdims=True), jnp.max(sc, axis=-1, keepdims=True))
        pp = jnp.exp(sp - m)
        pc = jnp.exp(sc - m)
        l = jnp.sum(pp, axis=-1, keepdims=True) + jnp.sum(pc, axis=-1, keepdims=True)
        o = (_bdot(pp, _stack_heads(vp_ref)) + _bdot(pc, _stack_heads(vc_ref))) / l
        _store_heads(o_ref, jnp.where(lane < DIL_DH, o, m + jnp.log(l)))

    cur = lambda part: pl.BlockSpec((DIL_SPAN, wide), lambda r, i: (i, 3 * r + part))
    prv = lambda part: pl.BlockSpec((DIL_SPAN, wide), lambda r, i: (jnp.maximum(i - 1, 0), 3 * r + part))
    one = pl.BlockSpec((1, LANES), lambda r, i: (0, 0))
    out = pl.pallas_call(
        body,
        grid=(dilation, nlb),
        in_specs=[cur(0), cur(1), cur(2), prv(1), prv(2), one, one],
        out_specs=pl.BlockSpec((DIL_SPAN, wide), lambda r, i: (i, r)),
        out_shape=jax.ShapeDtypeStruct((rows, dilation * wide), F32),
        compiler_params=_cparams(("parallel", "parallel")),
        name=name,
    )(view, view, view, view, view, wq_row, wk_row)
    return out.reshape(t, wide)


def _dil_merge_fwd(oe, *, name):
    t = oe[0].shape[0]
    tb = min(t, 1024)

    def body(e0, e1, e2, y_ref, om_ref):
        lane = lax.broadcasted_iota(jnp.int32, (tb, LANES), 1)
        es = [e0[...], e1[...], e2[...]]
        lse = [jnp.sum(jnp.where(lane == DIL_DH, e, 0.0), axis=-1, keepdims=True) for e in es]
        top = jnp.maximum(jnp.maximum(lse[0], lse[1]), lse[2])
        joint = top + jnp.log(jnp.exp(lse[0] - top) + jnp.exp(lse[1] - top) + jnp.exp(lse[2] - top))
        o = sum(jnp.exp(l - joint) * e for l, e in zip(lse, es))
        y_ref[...] = jnp.where(lane < DIL_DH, o, 0.0).astype(y_ref.dtype)
        om_ref[...] = jnp.where(lane < DIL_DH, o, joint)

    blk = pl.BlockSpec((tb, LANES), lambda i, h: (i, h))
    return pl.pallas_call(
        body,
        grid=(t // tb, DIL_HEADS),
        in_specs=[blk] * 3,
        out_specs=[blk, blk],
        out_shape=[jax.ShapeDtypeStruct((t, DIL_HEADS * LANES), MM_DTYPE),
                   jax.ShapeDtypeStruct((t, DIL_HEADS * LANES), F32)],
        compiler_params=_cparams(("parallel", "parallel")),
        name=name,
    )(*oe)


def _dil_merge_bwd(dy, om, *, name):
    t = dy.shape[0]
    tb = min(t, 1024)

    def body(dy_ref, om_ref, st_ref):
        lane = lax.broadcasted_iota(jnp.int32, (tb, LANES), 1)
        d_o = jnp.where(lane < DIL_DH, dy_ref[...], 0.0)
        om_t = om_ref[...]
        delta = jnp.sum(d_o * om_t, axis=-1, keepdims=True)
        st_ref[...] = jnp.where(lane < DIL_DH, d_o, jnp.where(lane == DIL_DH, om_t, jnp.where(lane == DIL_DH + 1, delta, 0.0)))

    blk = pl.BlockSpec((tb, LANES), lambda i, h: (i, h))
    return pl.pallas_call(
        body,
        grid=(t // tb, DIL_HEADS),
        in_specs=[blk, blk],
        out_specs=blk,
        out_shape=jax.ShapeDtypeStruct((t, DIL_HEADS * LANES), F32),
        compiler_params=_cparams(("parallel", "parallel")),
        name=name,
    )(dy, om)


def _rms64_bwd(x, w_row, dy):
    r = lax.rsqrt(jnp.sum(x * x, axis=-1, keepdims=True) * (1.0 / DIL_DH) + RMS_EPS)
    gw = dy * w_row
    dx = r * gw - x * (r * r * r * jnp.sum(gw * x, axis=-1, keepdims=True) * (1.0 / DIL_DH))
    return dx, dy * x * r


def _dil_attn_bwd(slab, stat, wq_row, wk_row, dwq_in, dwk_in, *, group, name):
    dilation = DIL_GROUPS[group][1]
    t = slab.shape[0]
    rows = t // dilation
    nlb = rows // DIL_SPAN
    wide = DIL_HEADS * LANES
    view = slab.reshape(rows, dilation * DIL_SLAB)
    stat_view = stat.reshape(rows, dilation * wide)

    def body(cur_ref, kp_ref, vp_ref, st_ref, wq_ref, wk_ref, dwq_in_ref, dwk_in_ref, d_ref, dwq_ref, dwk_ref,
             dk_carry, dv_carry):
        step = pl.program_id(1)
        has_prev = step < nlb - 1
        first = (pl.program_id(0) == 0) & (step == 0)

        @pl.when(step == 0)
        def _():
            dk_carry[...] = jnp.zeros_like(dk_carry)
            dv_carry[...] = jnp.zeros_like(dv_carry)

        @pl.when(first)
        def _():
            dwq_ref[...] = dwq_in_ref[...]
            dwk_ref[...] = dwk_in_ref[...]

        lane = lax.broadcasted_iota(jnp.int32, (DIL_SPAN, LANES), 1)
        scale = DIL_DH ** -0.5
        q_raw = _stack_heads(cur_ref)
        kc_raw = _stack_heads(cur_ref, first=DIL_HEADS)
        vc = _stack_heads(cur_ref, first=2 * DIL_HEADS)
        kp_raw = _stack_heads(kp_ref)
        vp = _stack_heads(vp_ref)
        st = _stack_heads(st_ref)
        d_o = jnp.where(lane < DIL_DH, st, 0.0)
        lse = jnp.sum(jnp.where(lane == DIL_DH, st, 0.0), axis=-1, keepdims=True)
        delta = jnp.sum(jnp.where(lane == DIL_DH + 1, st, 0.0), axis=-1, keepdims=True)
        qn = _rms64(q_raw, wq_ref[...])
        kc = _rms64(kc_raw, wk_ref[...])
        kp = _rms64(kp_raw, wk_ref[...])
        sp, sc = _band_logits(qn, kp, kc, _alibi_slopes(group) * float(dilation), has_prev)
        pp = jnp.exp(sp - lse)
        pc = jnp.exp(sc - lse)
        dsp = pp * (_bdot_nt(d_o, vp) - delta) * scale
        dsc = pc * (_bdot_nt(d_o, vc) - delta) * scale
        dqn = _bdot(dsp, kp) + _bdot(dsc, kc)
        dkc_n = _bdot_tn(dsc, qn) + _stack_heads(dk_carry)
        dvc = _bdot_tn(pc, d_o) + _stack_heads(dv_carry)
        _store_heads(dk_carry, _bdot_tn(dsp, qn))
        _store_heads(dv_carry, _bdot_tn(pp, d_o))
        dq_raw, dwq_rows = _rms64_bwd(q_raw, wq_ref[...], dqn)
        dk_raw, dwk_rows = _rms64_bwd(kc_raw, wk_ref[...], dkc_n)
        _store_heads(d_ref, dq_raw)
        _store_heads(d_ref, dk_raw, first=DIL_HEADS)
        _store_heads(d_ref, dvc, first=2 * DIL_HEADS)
        dwq_ref[...] += jnp.sum(jnp.sum(dwq_rows, axis=0), axis=0, keepdims=True)
        dwk_ref[...] += jnp.sum(jnp.sum(dwk_rows, axis=0), axis=0, keepdims=True)

    blk_i = lambda i: nlb - 1 - i
    cur = pl.BlockSpec((DIL_SPAN, DIL_SLAB), lambda r, i: (blk_i(i), r))
    prv = lambda part: pl.BlockSpec((DIL_SPAN, wide), lambda r, i: (jnp.maximum(blk_i(i) - 1, 0), 3 * r + part))
    one = pl.BlockSpec((1, LANES), lambda r, i: (0, 0))
    dslab, dwq, dwk = pl.pallas_call(
        body,
        grid=(dilation, nlb),
        in_specs=[cur, prv(1), prv(2), pl.BlockSpec((DIL_SPAN, wide), lambda r, i: (blk_i(i), r)), one, one, one, one],
        out_specs=[cur, one, one],
        out_shape=[jax.ShapeDtypeStruct((rows, dilation * DIL_SLAB), MM_DTYPE), jax.ShapeDtypeStruct((1, LANES), F32),
                   jax.ShapeDtypeStruct((1, LANES), F32)],
        scratch_shapes=[pltpu.VMEM((DIL_SPAN, wide), F32), pltpu.VMEM((DIL_SPAN, wide), F32)],
        compiler_params=_cparams(("arbitrary", "arbitrary")),
        name=name,
    )(view, view, view, stat_view, wq_row, wk_row, dwq_in, dwk_in)
    return dslab.reshape(t, DIL_SLAB), dwq, dwk


def _row(v, width=LANES):
    v = v.astype(F32).reshape(-1)
    return jnp.pad(v, (0, width - v.shape[0])).reshape(1, width)


def _prepare_weights(w):
    d = D_MODEL
    gdn, dil, ffn = [], [], []
    for j in range(DEPTH // 2):
        wt = w["gdn_w_in"][j]
        gates_t = jnp.pad(wt[GDN_MAIN:], ((0, LANES - 2 * GDN_HEADS), (0, 0)))
        gdn.append(dict(in_t=wt, gates_t=gates_t, out=w["gdn_w_out"][j], conv=w["gdn_conv_w"][j].astype(F32),
                        alog=_row(w["gdn_a_log"][j]), dt=_row(w["gdn_dt_bias"][j]), nw=_row(w["gdn_norm_w"][j])))
        wt = w["dil_w_in"][j].reshape(3, len(DIL_GROUPS), DIL_HEADS, DIL_DH, d)
        wg_t = [jnp.pad(wt[:, g], ((0, 0), (0, 0), (0, LANES - DIL_DH), (0, 0))).reshape(DIL_SLAB, d)
                for g in range(len(DIL_GROUPS))]
        out_t = jnp.pad(w["dil_w_out"][j].reshape(d, DIL_HEADS, DIL_DH), ((0, 0), (0, 0), (0, LANES - DIL_DH)))
        dil.append(dict(wg_t=wg_t, out_t=out_t.reshape(d, DIL_HEADS * LANES), wq=_row(w["dil_q_norm"][j]),
                        wk=_row(w["dil_k_norm"][j])))
    for i in range(DEPTH):
        ffn.append(dict(in_t=w["ffn_w_in"][i], out=w["ffn_w_out"][i]))
    return dict(gdn=gdn, dil=dil, ffn=ffn)


def _gdn_layer_fwd(x, nrow, p):
    hn = _rmsnorm_fwd(x, nrow, name="rmsnorm_fwd")
    pm = _matmul(hn, p["in_t"], trans_b=True, b_rows=(0, GDN_MAIN), name="gdn_proj_main")
    ab = _matmul(hn, p["gates_t"], trans_b=True, name="gdn_proj_gates")
    qkv = _gdn_conv_fwd(pm, p["conv"], name="gdn_conv_fwd")
    gb, bb = _gdn_gates_fwd(ab, p["alog"], p["dt"], name="gdn_gates_fwd")
    prep = _gdn_prep_fwd(qkv, gb, bb, name="gdn_prep_fwd")
    o, states = _gdn_scan_fwd(prep, name="gdn_scan_fwd")
    og = _gdn_outgate_fwd(o, pm, p["nw"], name="gdn_outgate_fwd")
    y = _matmul(og, p["out"], add=x, name="gdn_proj_out")
    return y, (x, hn, pm, ab, qkv, gb, bb, prep, states, o, og)


def _gdn_layer_bwd(dx, dxb, nrow, p, saved):
    x, hn, pm, ab, qkv, gb, bb, prep, states, o, og = saved
    d_og = _matmul(dxb, p["out"], trans_b=True, name="gdn_dgate")
    g_out = _matmul(og, dxb, trans_a=True, name="gdn_gw_out")
    d_o, d_z, d_nw = _gdn_outgate_bwd(o, pm, p["nw"], d_og, name="gdn_outgate_bwd")
    cts = _gdn_scan_bwd(prep, states, d_o, name="gdn_scan_bwd")
    dqkv, dgb, dbb = _gdn_prep_bwd(qkv, gb, bb, cts, name="gdn_prep_bwd")
    d_ab, d_alog, d_dt = _gdn_gates_bwd(ab, p["alog"], p["dt"], dgb, dbb, name="gdn_gates_bwd")
    d_conv, g_conv = _gdn_conv_bwd(pm, p["conv"], dqkv, name="gdn_conv_bwd")
    d_hn = _matmul(d_conv, p["in_t"], b_rows=(0, GDN_QKV), name="gdn_dhn_qkv")
    d_hn = _matmul(d_z, p["in_t"], b_rows=(GDN_QKV, GDN_MAIN - GDN_QKV), add=d_hn, name="gdn_dhn_z")
    d_hn = _matmul(d_ab, p["gates_t"], add=d_hn, name="gdn_dhn_gates")
    g_in_t = jnp.concatenate([
        _matmul(d_conv, hn, trans_a=True, name="gdn_gw_qkv"),
        _matmul(d_z, hn, trans_a=True, name="gdn_gw_z"),
        _matmul(d_ab, hn, trans_a=True, name="gdn_gw_gates")[:2 * GDN_HEADS],
    ], axis=0)
    dx_new, dxb_new, g_norm = _rmsnorm_bwd(x, nrow, d_hn, dx, name="rmsnorm_bwd")
    grads = dict(w_in=g_in_t, conv=g_conv, a_log=d_alog[0, :GDN_HEADS], dt_bias=d_dt[0, :GDN_HEADS], norm_w=d_nw[0],
                 w_out=g_out, norm=g_norm[0])
    return dx_new, dxb_new, grads


def _dil_layer_fwd(x, nrow, p):
    hn = _rmsnorm_fwd(x, nrow, name="rmsnorm_fwd")
    slabs = [_matmul(hn, p["wg_t"][g], trans_b=True, name="dil_proj_in") for g in range(len(DIL_GROUPS))]
    oe = [_dil_attn_fwd(slabs[g], p["wq"], p["wk"], group=g, name=f"dil_attn_fwd_g{g}") for g in range(len(DIL_GROUPS))]
    y, om = _dil_merge_fwd(oe, name="dil_merge_fwd")
    out = _matmul(y, p["out_t"], trans_b=True, add=x, name="dil_proj_out")
    return out, (x, hn, slabs, y, om)


def _dil_layer_bwd(dx, dxb, nrow, p, saved):
    x, hn, slabs, y, om = saved
    d_y = _matmul(dxb, p["out_t"], name="dil_dmerged")
    g_out_t = _matmul(dxb, y, trans_a=True, name="dil_gw_out")
    g_out_t = g_out_t.reshape(D_MODEL, DIL_HEADS, LANES)[..., :DIL_DH].reshape(D_MODEL, DIL_HEADS * DIL_DH)
    stat = _dil_merge_bwd(d_y, om, name="dil_merge_bwd")
    d_hn = None
    dwq = jnp.zeros((1, LANES), F32)
    dwk = jnp.zeros((1, LANES), F32)
    g_groups = []
    for g in range(len(DIL_GROUPS)):
        dslab, dwq, dwk = _dil_attn_bwd(slabs[g], stat, p["wq"], p["wk"], dwq, dwk, group=g, name=f"dil_attn_bwd_g{g}")
        d_hn = _matmul(dslab, p["wg_t"][g], add=d_hn, name="dil_dhn")
        g_w = _matmul(dslab, hn, trans_a=True, name="dil_gw_in")
        g_groups.append(g_w.reshape(3, DIL_HEADS, LANES, D_MODEL)[:, :, :DIL_DH])
    g_in_t = jnp.stack(g_groups, axis=1).reshape(3 * len(DIL_GROUPS) * DIL_HEADS * DIL_DH, D_MODEL)
    dx_new, dxb_new, g_norm = _rmsnorm_bwd(x, nrow, d_hn, dx, name="rmsnorm_bwd")
    grads = dict(w_in=g_in_t, q_norm=dwq[0, :DIL_DH], k_norm=dwk[0, :DIL_DH], w_out=g_out_t, norm=g_norm[0])
    return dx_new, dxb_new, grads


def _ffn_layer_fwd(x, nrow, p):
    hn = _rmsnorm_fwd(x, nrow, name="rmsnorm_fwd")
    gu = _matmul(hn, p["in_t"], trans_b=True, name="ffn_proj_in")
    act = _swiglu_fwd(gu, name="swiglu_fwd")
    y = _matmul(act, p["out"], add=x, name="ffn_proj_out")
    return y, (x, hn, gu, act)


def _ffn_layer_bwd(dx, dxb, nrow, p, saved):
    x, hn, gu, act = saved
    d_act = _matmul(dxb, p["out"], trans_b=True, name="ffn_dact")
    g_out = _matmul(act, dxb, trans_a=True, name="ffn_gw_out")
    d_g, d_u = _swiglu_bwd(gu, d_act, name="swiglu_bwd")
    d_hn = _matmul(d_g, p["in_t"], b_rows=(0, FFN_HIDDEN), name="ffn_dhn_gate")
    d_hn = _matmul(d_u, p["in_t"], b_rows=(FFN_HIDDEN, FFN_HIDDEN), add=d_hn, name="ffn_dhn_up")
    g_in_t = jnp.concatenate([_matmul(d_g, hn, trans_a=True, name="ffn_gw_gate"),
                              _matmul(d_u, hn, trans_a=True, name="ffn_gw_up")], axis=0)
    dx_new, dxb_new, g_norm = _rmsnorm_bwd(x, nrow, d_hn, dx, name="rmsnorm_bwd")
    return dx_new, dxb_new, dict(w_in=g_in_t, w_out=g_out, norm=g_norm[0])


def _local_step(x, target, prepared, norm_mix, norm_ffn):
    saved = []
    for i in range(DEPTH):
        j = i // 2
        mix_row = norm_mix[i].reshape(1, D_MODEL)
        if i % 2 == 0:
            x, s_mix = _gdn_layer_fwd(x, mix_row, prepared["gdn"][j])
        else:
            x, s_mix = _dil_layer_fwd(x, mix_row, prepared["dil"][j])
        x, s_ffn = _ffn_layer_fwd(x, norm_ffn[i].reshape(1, D_MODEL), prepared["ffn"][i])
        saved.append((s_mix, s_ffn))
    dx, dxb, loss = _loss_head(x, target, name="loss_head")
    g_mix, g_ffn = [None] * DEPTH, [None] * DEPTH
    for i in reversed(range(DEPTH)):
        j = i // 2
        s_mix, s_ffn = saved[i]
        dx, dxb, g_ffn[i] = _ffn_layer_bwd(dx, dxb, norm_ffn[i].reshape(1, D_MODEL), prepared["ffn"][i], s_ffn)
        mix_row = norm_mix[i].reshape(1, D_MODEL)
        if i % 2 == 0:
            dx, dxb, g_mix[i] = _gdn_layer_bwd(dx, dxb, mix_row, prepared["gdn"][j], s_mix)
        else:
            dx, dxb, g_mix[i] = _dil_layer_bwd(dx, dxb, mix_row, prepared["dil"][j], s_mix)
    gdn = [g_mix[i] for i in range(0, DEPTH, 2)]
    dil = [g_mix[i] for i in range(1, DEPTH, 2)]
    grads = dict(
        norm_mix=jnp.stack([g["norm"] for g in g_mix]),
        norm_ffn=jnp.stack([g["norm"] for g in g_ffn]),
        gdn_w_in=[g["w_in"] for g in gdn],
        gdn_conv_w=jnp.stack([g["conv"] for g in gdn]),
        gdn_a_log=jnp.stack([g["a_log"] for g in gdn]),
        gdn_dt_bias=jnp.stack([g["dt_bias"] for g in gdn]),
        gdn_norm_w=jnp.stack([g["norm_w"] for g in gdn]),
        gdn_w_out=[g["w_out"] for g in gdn],
        dil_w_in=[g["w_in"] for g in dil],
        dil_q_norm=jnp.stack([g["q_norm"] for g in dil]),
        dil_k_norm=jnp.stack([g["k_norm"] for g in dil]),
        dil_w_out=[g["w_out"] for g in dil],
        ffn_w_in=[g["w_in"] for g in g_ffn],
        ffn_w_out=[g["w_out"] for g in g_ffn],
    )
    return loss[0, 0], dx, grads


MESH_ID = pl.DeviceIdType.MESH
ANY_SPACE = pl.BlockSpec(memory_space=pl.ANY)


def _mesh_position():
    return lax.axis_index("x"), lax.axis_index("y"), lax.axis_index("c")


def _flip(pos, k):
    x, y, c = pos
    return (1 - x if k & 4 else x, 1 - y if k & 2 else y, 1 - c if k & 1 else c)


def _linear(pos):
    return 4 * pos[0] + 2 * pos[1] + pos[2]


def _comm_scratch():
    return [pltpu.SemaphoreType.DMA((N_DEV - 1,)), pltpu.SemaphoreType.DMA((N_DEV - 1,)), pltpu.SemaphoreType.DMA(())]


def _all_gather(shard, *, name):
    def body(x_ref, out_ref, send_sems, recv_sems, local_sem):
        me = _mesh_position()
        mine = out_ref.at[_linear(me)]
        local = pltpu.make_async_copy(x_ref, mine, local_sem)
        local.start()
        copies = []
        for k in range(1, N_DEV):
            cp = pltpu.make_async_remote_copy(src_ref=x_ref, dst_ref=mine, send_sem=send_sems.at[k - 1],
                                              recv_sem=recv_sems.at[k - 1], device_id=_flip(me, k), device_id_type=MESH_ID)
            cp.start()
            copies.append(cp)
        for cp in copies:
            cp.wait()
        local.wait()

    return pl.pallas_call(
        body,
        out_shape=jax.ShapeDtypeStruct((N_DEV,) + shard.shape, shard.dtype),
        in_specs=[ANY_SPACE],
        out_specs=ANY_SPACE,
        scratch_shapes=_comm_scratch(),
        name=name,
    )(shard)


def _exchange(parts, *, name):
    def body(p_ref, out_ref, send_sems, recv_sems, local_sem):
        me = _mesh_position()
        mine = out_ref.at[_linear(me)]
        local = pltpu.make_async_copy(p_ref.at[_linear(me)], mine, local_sem)
        local.start()
        copies = []
        for k in range(1, N_DEV):
            peer = _flip(me, k)
            cp = pltpu.make_async_remote_copy(src_ref=p_ref.at[_linear(peer)], dst_ref=mine, send_sem=send_sems.at[k - 1],
                                              recv_sem=recv_sems.at[k - 1], device_id=peer, device_id_type=MESH_ID)
            cp.start()
            copies.append(cp)
        for cp in copies:
            cp.wait()
        local.wait()

    return pl.pallas_call(
        body,
        out_shape=jax.ShapeDtypeStruct(parts.shape, parts.dtype),
        in_specs=[ANY_SPACE],
        out_specs=ANY_SPACE,
        scratch_shapes=_comm_scratch(),
        name=name,
    )(parts)


def _adamw(parts, w, m, v, *, name):
    rows, n = w.shape
    tb = _pick(rows, (400, 16))
    c1 = 1.0 - ADAM_B1 ** ADAM_STEP
    c2 = 1.0 - ADAM_B2 ** ADAM_STEP

    def body(p_ref, w_ref, m_ref, v_ref, g_ref, d_ref, nm_ref, nv_ref):
        g = p_ref[0].astype(F32)
        for s in range(1, N_DEV):
            g = g + p_ref[s].astype(F32)
        m_new = ADAM_B1 * m_ref[...] + (1.0 - ADAM_B1) * g
        v_new = ADAM_B2 * v_ref[...] + (1.0 - ADAM_B2) * (g * g)
        m_hat = m_new / c1
        v_hat = v_new / c2
        g_ref[...] = g
        nm_ref[...] = m_new
        nv_ref[...] = v_new
        d_ref[...] = -ADAM_LR * (m_hat / (jnp.sqrt(v_hat) + ADAM_EPS) + ADAM_WD * w_ref[...])

    blk = pl.BlockSpec((tb, n), lambda i: (i, 0))
    return pl.pallas_call(
        body,
        grid=(rows // tb,),
        in_specs=[pl.BlockSpec((N_DEV, tb, n), lambda i: (0, i, 0)), blk, blk, blk],
        out_specs=[blk] * 4,
        out_shape=[jax.ShapeDtypeStruct((rows, n), F32)] * 4,
        compiler_params=_cparams(("parallel",)),
        name=name,
    )(parts, w, m, v)


PACK_WIDTH = 1024
SHARDED = {
    "gdn_w_in": ((2, D_MODEL, GDN_IN_WIDTH), 2),
    "gdn_conv_w": ((2, GDN_CONV, GDN_QKV), 2),
    "gdn_w_out": ((2, GDN_HEADS * GDN_DV, D_MODEL), 1),
    "dil_w_in": ((2, D_MODEL, 3 * len(DIL_GROUPS) * DIL_HEADS * DIL_DH), 2),
    "dil_w_out": ((2, DIL_HEADS * DIL_DH, D_MODEL), 2),
    "ffn_w_in": ((DEPTH, D_MODEL, 2 * FFN_HIDDEN), 2),
    "ffn_w_out": ((DEPTH, FFN_HIDDEN, D_MODEL), 1),
}
REPLICATED = {"norm_mix": (DEPTH, D_MODEL), "norm_ffn": (DEPTH, D_MODEL), "gdn_a_log": (2, GDN_HEADS),
              "gdn_dt_bias": (2, GDN_HEADS), "gdn_norm_w": (2, GDN_DV), "dil_q_norm": (2, DIL_DH), "dil_k_norm": (2, DIL_DH)}
WEIGHT_ORDER = ("norm_mix", "norm_ffn", "gdn_w_in", "gdn_conv_w", "gdn_a_log", "gdn_dt_bias", "gdn_norm_w", "gdn_w_out",
                "dil_w_in", "dil_q_norm", "dil_k_norm", "dil_w_out", "ffn_w_in", "ffn_w_out")
PACK_ROW_ALIGN = 400
SMALL_ROWS = 16


def _shard_shape(name):
    shape, axis = SHARDED[name]
    return tuple(s // N_DEV if i == axis else s for i, s in enumerate(shape))


def _shard_rows(name):
    return math.prod(_shard_shape(name)) // PACK_WIDTH


def _padded_rows(rows):
    return -(-rows // PACK_ROW_ALIGN) * PACK_ROW_ALIGN


def _split_shards(full, name):
    shape, axis = SHARDED[name]
    split = full.reshape(shape[:axis] + (N_DEV, shape[axis] // N_DEV) + shape[axis + 1:])
    return jnp.moveaxis(split, axis, 0)


def _join_shards(stacked, name):
    shape, axis = SHARDED[name]
    return jnp.moveaxis(stacked, 0, axis).reshape(shape)


COLUMN_SHARDED = ("gdn_w_in", "dil_w_in", "dil_w_out", "ffn_w_in")


def _to_rows(shard, name):
    if name in COLUMN_SHARDED:
        shard = jnp.swapaxes(shard, 1, 2)
    return shard.reshape(-1, PACK_WIDTH)


def _from_rows(rows, name):
    layers, r, c = _shard_shape(name)
    if name in COLUMN_SHARDED:
        return jnp.swapaxes(rows.reshape(layers, c, r), 1, 2)
    return rows.reshape(layers, r, c)


def _layer_columns(name):
    _, r, c = _shard_shape(name)
    return r if name in COLUMN_SHARDED else c


def _pack_rows(pieces, lead=()):
    buf = jnp.concatenate(pieces, axis=len(lead))
    rows = buf.shape[len(lead)]
    pad = [(0, 0)] * len(lead) + [(0, _padded_rows(rows) - rows), (0, 0)]
    return jnp.pad(buf, pad)


def _unpack_rows(buf, names):
    out, at = {}, 0
    for n in names:
        rows = _shard_rows(n)
        out[n] = _from_rows(buf[at:at + rows], n)
        at += rows
    return out


def _pack_small(vals):
    tail = jnp.concatenate([vals[n].astype(F32).reshape(-1) for n in REPLICATED if n not in ("norm_mix", "norm_ffn")])
    tail = jnp.pad(tail, (0, PACK_WIDTH - tail.shape[0])).reshape(1, PACK_WIDTH)
    buf = jnp.concatenate([vals["norm_mix"].astype(F32), vals["norm_ffn"].astype(F32), tail], axis=0)
    return jnp.pad(buf, ((0, SMALL_ROWS - buf.shape[0]), (0, 0)))


def _unpack_small(buf):
    out = {"norm_mix": buf[0:DEPTH], "norm_ffn": buf[DEPTH:2 * DEPTH]}
    at = 0
    for n, shape in REPLICATED.items():
        if n in out:
            continue
        size = math.prod(shape)
        out[n] = buf[2 * DEPTH, at:at + size].reshape(shape)
        at += size
    return out


def kernel(x, norm_mix, norm_ffn, gdn_w_in, gdn_conv_w, gdn_a_log, gdn_dt_bias, gdn_norm_w, gdn_w_out, dil_w_in, dil_q_norm, dil_k_norm, dil_w_out, ffn_w_in, ffn_w_out, loss_target, m_norm_mix, m_norm_ffn, m_gdn_w_in, m_gdn_conv_w, m_gdn_a_log, m_gdn_dt_bias, m_gdn_norm_w, m_gdn_w_out, m_dil_w_in, m_dil_q_norm, m_dil_k_norm, m_dil_w_out, m_ffn_w_in, m_ffn_w_out, v_norm_mix, v_norm_ffn, v_gdn_w_in, v_gdn_conv_w, v_gdn_a_log, v_gdn_dt_bias, v_gdn_norm_w, v_gdn_w_out, v_dil_w_in, v_dil_q_norm, v_dil_k_norm, v_dil_w_out, v_ffn_w_in, v_ffn_w_out):
    w = dict(norm_mix=norm_mix, norm_ffn=norm_ffn, gdn_w_in=gdn_w_in, gdn_conv_w=gdn_conv_w, gdn_a_log=gdn_a_log,
             gdn_dt_bias=gdn_dt_bias, gdn_norm_w=gdn_norm_w, gdn_w_out=gdn_w_out, dil_w_in=dil_w_in, dil_q_norm=dil_q_norm,
             dil_k_norm=dil_k_norm, dil_w_out=dil_w_out, ffn_w_in=ffn_w_in, ffn_w_out=ffn_w_out)
    m = dict(norm_mix=m_norm_mix, norm_ffn=m_norm_ffn, gdn_w_in=m_gdn_w_in, gdn_conv_w=m_gdn_conv_w, gdn_a_log=m_gdn_a_log,
             gdn_dt_bias=m_gdn_dt_bias, gdn_norm_w=m_gdn_norm_w, gdn_w_out=m_gdn_w_out, dil_w_in=m_dil_w_in,
             dil_q_norm=m_dil_q_norm, dil_k_norm=m_dil_k_norm, dil_w_out=m_dil_w_out, ffn_w_in=m_ffn_w_in, ffn_w_out=m_ffn_w_out)
    v = dict(norm_mix=v_norm_mix, norm_ffn=v_norm_ffn, gdn_w_in=v_gdn_w_in, gdn_conv_w=v_gdn_conv_w, gdn_a_log=v_gdn_a_log,
             gdn_dt_bias=v_gdn_dt_bias, gdn_norm_w=v_gdn_norm_w, gdn_w_out=v_gdn_w_out, dil_w_in=v_dil_w_in,
             dil_q_norm=v_dil_q_norm, dil_k_norm=v_dil_k_norm, dil_w_out=v_dil_w_out, ffn_w_in=v_ffn_w_in, ffn_w_out=v_ffn_w_out)
    big = tuple(SHARDED)

    def as_operand(name, a):
        if name == "gdn_conv_w":
            return lax.bitcast_convert_type(a, BF16).reshape(-1, PACK_WIDTH)
        return _to_rows(a.astype(BF16), name)

    gathered = _all_gather(_pack_rows([as_operand(n, w[n]) for n in big]), name="weight_all_gather")
    full, at = {}, 0
    for n in big:
        if n == "gdn_conv_w":
            rows = 2 * _shard_rows(n)
            piece = gathered[:, at:at + rows].reshape((N_DEV,) + _shard_shape(n) + (2,))
            full[n] = _join_shards(lax.bitcast_convert_type(piece, F32), n)
        else:
            rows = _shard_rows(n)
            per_layer = rows // SHARDED[n][0][0]
            full[n] = [gathered[:, at + l * per_layer:at + (l + 1) * per_layer].reshape(-1, _layer_columns(n))
                       for l in range(SHARDED[n][0][0])]
        at += rows
    for n in REPLICATED:
        full[n] = w[n]
    prepared = _prepare_weights(full)

    loss, grad_x, grads = _local_step(x[0], loss_target[0], prepared, norm_mix, norm_ffn)

    pieces = []
    for n in big:
        if n == "gdn_conv_w":
            pieces.append(_split_shards(grads[n], n).astype(BF16).reshape(N_DEV, -1, PACK_WIDTH))
        else:
            pieces.extend(g.astype(BF16).reshape(N_DEV, -1, PACK_WIDTH) for g in grads[n])
    received = _exchange(_pack_rows(pieces, lead=(N_DEV,)), name="grad_exchange")
    packed = [_pack_rows([_to_rows(src[n].astype(F32), n) for n in big]) for src in (w, m, v)]
    outs_big = [_unpack_rows(o, big) for o in _adamw(received, *packed, name="adamw_sharded")]

    small_parts = _all_gather(_pack_small(grads), name="small_grad_all_gather")
    outs_small = [_unpack_small(o) for o in
                  _adamw(small_parts, _pack_small(w), _pack_small(m), _pack_small(v), name="adamw_replicated")]

    total_loss = lax.psum(loss, ("x", "y", "c"))
    result = [total_loss, grad_x[None]]
    for k in range(4):
        for n in WEIGHT_ORDER:
            result.append(outs_big[k][n] if n in SHARDED else outs_small[k][n])
    return tuple(result)
```

```python
import functools
import math

import jax
import jax.numpy as jnp
from jax import lax
from jax.experimental import pallas as pl
from jax.experimental.pallas import tpu as pltpu

F32 = jnp.float32
BF16 = jnp.bfloat16
MM_DTYPE = BF16

N_DEV = 8
D_MODEL = 1024
DEPTH = 4
RMS_EPS = 1e-6
L2_EPS = 1e-6

LANES = 128

GDN_HEADS = 8
GDN_DK = 128
GDN_DV = 128
GDN_CONV = 4
GDN_CHUNK = 128
GDN_QKV = 3 * GDN_HEADS * GDN_DK
GDN_MAIN = GDN_QKV + GDN_HEADS * GDN_DV
GDN_IN_WIDTH = GDN_MAIN + 2 * GDN_HEADS

DIL_GROUPS = ((128, 1), (512, 4), (2048, 16))
DIL_HEADS = 8
DIL_DH = 64
DIL_SPAN = 128
DIL_SLAB = 3 * DIL_HEADS * LANES
ALIBI_MAX_BIAS = 8.0

FFN_HIDDEN = 2816

ADAM_LR = 0.001
ADAM_B1 = 0.9
ADAM_B2 = 0.999
ADAM_EPS = 1e-08
ADAM_WD = 0.01
ADAM_STEP = 10

VMEM_LIMIT = 56 * 1024 * 1024
NEG = -1e30
HI = lax.Precision.HIGHEST


def _cparams(sem):
    return pltpu.CompilerParams(dimension_semantics=sem, vmem_limit_bytes=VMEM_LIMIT)


def _dot(a, b):
    return lax.dot_general(a, b, (((1,), (0,)), ((), ())), preferred_element_type=F32, precision=HI)


def _dot_nt(a, b):
    return lax.dot_general(a, b, (((1,), (1,)), ((), ())), preferred_element_type=F32, precision=HI)


def _dot_tn(a, b):
    return lax.dot_general(a, b, (((0,), (0,)), ((), ())), preferred_element_type=F32, precision=HI)


def _single_pass(a, b, a_dim, b_dim):
    lead = a.ndim - 2
    batch = ((0,), (0,)) if lead else ((), ())
    return lax.dot_general(a.astype(BF16), b.astype(BF16), (((lead + a_dim,), (lead + b_dim,)), batch),
                           preferred_element_type=F32)


def _bdot(a, b):
    return _single_pass(a, b, 1, 0)


def _bdot_nt(a, b):
    return _single_pass(a, b, 1, 1)


def _bdot_tn(a, b):
    return _single_pass(a, b, 0, 0)


def _pick(n, candidates):
    for c in candidates:
        if n % c == 0:
            return c
    raise ValueError(f"no tile for {n}")


def _matmul(a, b, *, name, trans_a=False, trans_b=False, b_rows=None, add=None, out_dtype=F32):
    if trans_a:
        k_dim, m_dim = a.shape
    else:
        m_dim, k_dim = a.shape
    b_start, b_size = b_rows if b_rows is not None else (0, b.shape[0])
    if trans_b:
        n_dim, k2 = b_size, b.shape[1]
    else:
        k2, n_dim = b_size, b.shape[1]
    assert k_dim == k2, (a.shape, b.shape, b_rows)
    tn = _pick(n_dim, (1024, 512, 256, 128))
    tm = min(m_dim, 2048, max(512, (1024 * 1024) // tn))
    tm = _pick(m_dim, (tm, 1408, 1024, 512, 256, 128))
    tk = _pick(k_dim, (1024, 1408, 512, 256, 128))
    nk = k_dim // tk
    has_add = add is not None
    dn = (((0 if trans_a else 1,), (1 if trans_b else 0,)), ((), ()))
    b_tile = tn if trans_b else tk
    assert b_start % b_tile == 0, (b_rows, b_tile)
    b_off = b_start // b_tile

    def body(*refs):
        if has_add:
            a_ref, b_ref, add_ref, o_ref, acc_ref = refs
        else:
            a_ref, b_ref, o_ref, acc_ref = refs
        part = lax.dot_general(a_ref[...], b_ref[...], dn, preferred_element_type=F32)

        def finish(total):
            if has_add:
                total = total + add_ref[...]
            o_ref[...] = total.astype(out_dtype)

        if nk == 1:
            finish(part)
        else:
            k = pl.program_id(2)

            @pl.when(k == 0)
            def _():
                acc_ref[...] = part

            @pl.when(k > 0)
            def _():
                acc_ref[...] += part

            @pl.when(k == nk - 1)
            def _():
                finish(acc_ref[...])

    if trans_a:
        a_spec = pl.BlockSpec((tk, tm), lambda i, j, k: (k, i))
    else:
        a_spec = pl.BlockSpec((tm, tk), lambda i, j, k: (i, k))
    if trans_b:
        b_spec = pl.BlockSpec((tn, tk), lambda i, j, k: (j + b_off, k))
    else:
        b_spec = pl.BlockSpec((tk, tn), lambda i, j, k: (k + b_off, j))
    in_specs = [a_spec, b_spec]
    args = [a, b]
    if has_add:
        in_specs.append(pl.BlockSpec((tm, tn), lambda i, j, k: (i, j)))
        args.append(add)
    return pl.pallas_call(
        body,
        grid=(m_dim // tm, n_dim // tn, nk),
        in_specs=in_specs,
        out_specs=pl.BlockSpec((tm, tn), lambda i, j, k: (i, j)),
        out_shape=jax.ShapeDtypeStruct((m_dim, n_dim), out_dtype),
        scratch_shapes=[pltpu.VMEM((tm, tn) if nk > 1 else (8, LANES), F32)],
        compiler_params=_cparams(("parallel", "parallel", "arbitrary")),
        name=name,
    )(*args)


def _rmsnorm_fwd(x, w_row, *, name):
    t, d = x.shape
    tb = min(t, 1024)

    def body(x_ref, w_ref, o_ref):
        xf = x_ref[...]
        r = lax.rsqrt(jnp.mean(xf * xf, axis=-1, keepdims=True) + RMS_EPS)
        o_ref[...] = (xf * r * w_ref[...]).astype(o_ref.dtype)

    return pl.pallas_call(
        body,
        grid=(t // tb,),
        in_specs=[pl.BlockSpec((tb, d), lambda i: (i, 0)), pl.BlockSpec((1, d), lambda i: (0, 0))],
        out_specs=pl.BlockSpec((tb, d), lambda i: (i, 0)),
        out_shape=jax.ShapeDtypeStruct((t, d), MM_DTYPE),
        compiler_params=_cparams(("parallel",)),
        name=name,
    )(x, w_row)


def _rmsnorm_bwd(x, w_row, dy, dskip, *, name):
    t, d = x.shape
    tb = min(t, 512)

    def body(x_ref, w_ref, dy_ref, ds_ref, dx_ref, dxb_ref, dw_ref):
        xf = x_ref[...]
        g = dy_ref[...]
        r = lax.rsqrt(jnp.mean(xf * xf, axis=-1, keepdims=True) + RMS_EPS)
        gw = g * w_ref[...]
        proj = jnp.mean(gw * xf, axis=-1, keepdims=True)
        dx = r * gw - xf * (r * r * r * proj) + ds_ref[...]
        dx_ref[...] = dx
        dxb_ref[...] = dx.astype(dxb_ref.dtype)
        part = jnp.sum(g * xf * r, axis=0, keepdims=True)

        @pl.when(pl.program_id(0) == 0)
        def _():
            dw_ref[...] = part

        @pl.when(pl.program_id(0) > 0)
        def _():
            dw_ref[...] += part

    row = pl.BlockSpec((tb, d), lambda i: (i, 0))
    one = pl.BlockSpec((1, d), lambda i: (0, 0))
    return pl.pallas_call(
        body,
        grid=(t // tb,),
        in_specs=[row, one, row, row],
        out_specs=[row, row, one],
        out_shape=[jax.ShapeDtypeStruct((t, d), F32), jax.ShapeDtypeStruct((t, d), MM_DTYPE),
                   jax.ShapeDtypeStruct((1, d), F32)],
        compiler_params=_cparams(("arbitrary",)),
        name=name,
    )(x, w_row, dy, dskip)


def _silu(z):
    return z / (1.0 + jnp.exp(-z))


def _swiglu_fwd(gu, *, name):
    t = gu.shape[0]
    h = FFN_HIDDEN
    tb, tc = min(t, 1024), 256
    nc = h // tc

    def body(g_ref, u_ref, o_ref):
        o_ref[...] = (_silu(g_ref[...]) * u_ref[...]).astype(o_ref.dtype)

    return pl.pallas_call(
        body,
        grid=(t // tb, nc),
        in_specs=[pl.BlockSpec((tb, tc), lambda i, j: (i, j)), pl.BlockSpec((tb, tc), lambda i, j: (i, j + nc))],
        out_specs=pl.BlockSpec((tb, tc), lambda i, j: (i, j)),
        out_shape=jax.ShapeDtypeStruct((t, h), MM_DTYPE),
        compiler_params=_cparams(("parallel", "parallel")),
        name=name,
    )(gu, gu)


def _swiglu_bwd(gu, dact, *, name):
    t = gu.shape[0]
    h = FFN_HIDDEN
    tb, tc = min(t, 1024), 256
    nc = h // tc

    def body(g_ref, u_ref, da_ref, dg_ref, du_ref):
        g = g_ref[...]
        da = da_ref[...]
        sig = 1.0 / (1.0 + jnp.exp(-g))
        sg = g * sig
        dg_ref[...] = (da * u_ref[...] * (sig + sg * (1.0 - sig))).astype(dg_ref.dtype)
        du_ref[...] = (da * sg).astype(du_ref.dtype)

    blk = pl.BlockSpec((tb, tc), lambda i, j: (i, j))
    return pl.pallas_call(
        body,
        grid=(t // tb, nc),
        in_specs=[blk, pl.BlockSpec((tb, tc), lambda i, j: (i, j + nc)), blk],
        out_specs=[blk, blk],
        out_shape=[jax.ShapeDtypeStruct((t, h), MM_DTYPE)] * 2,
        compiler_params=_cparams(("parallel", "parallel")),
        name=name,
    )(gu, gu, dact)


def _loss_head(y, target, *, name):
    t, d = y.shape
    tb = min(t, 1024)

    def body(y_ref, t_ref, dy_ref, dyb_ref, l_ref):
        err = y_ref[...] - t_ref[...]
        dy_ref[...] = err * (1.0 / d)
        dyb_ref[...] = (err * (1.0 / d)).astype(dyb_ref.dtype)
        part = jnp.sum(jnp.sum(err * err, axis=0, keepdims=True), axis=1, keepdims=True) * (0.5 / d)
        part = jnp.broadcast_to(part, l_ref.shape)

        @pl.when(pl.program_id(0) == 0)
        def _():
            l_ref[...] = part

        @pl.when(pl.program_id(0) > 0)
        def _():
            l_ref[...] += part

    row = pl.BlockSpec((tb, d), lambda i: (i, 0))
    return pl.pallas_call(
        body,
        grid=(t // tb,),
        in_specs=[row, row],
        out_specs=[row, row, pl.BlockSpec((8, LANES), lambda i: (0, 0))],
        out_shape=[jax.ShapeDtypeStruct((t, d), F32), jax.ShapeDtypeStruct((t, d), MM_DTYPE),
                   jax.ShapeDtypeStruct((8, LANES), F32)],
        compiler_params=_cparams(("arbitrary",)),
        name=name,
    )(y, target)


CONV_HALO = 8


def _conv_tile_scale(c):
    is_qk = c < 2 * GDN_HEADS
    scale = jnp.where(c < GDN_HEADS, GDN_DK ** -0.5, 1.0).astype(F32)
    return is_qk, scale


def _gdn_conv_fwd(pm, conv_w, *, name):
    t = pm.shape[0]
    tb = min(t, 1024)
    nt = t // tb
    hb = tb // CONV_HALO

    def body(x_ref, xp_ref, w_ref, o_ref):
        c = pl.program_id(0)
        ti = pl.program_id(1)
        prev = jnp.where(ti > 0, xp_ref[...], 0.0)
        xe = jnp.concatenate([prev, x_ref[...]], axis=0)
        w = w_ref[...]
        y = jnp.zeros((tb, LANES), F32)
        for j in range(GDN_CONV):
            off = CONV_HALO - (GDN_CONV - 1) + j
            y = y + w[j:j + 1, :] * xe[off:off + tb, :]
        s = _silu(y)
        is_qk, scale = _conv_tile_scale(c)
        r = lax.rsqrt(jnp.sum(s * s, axis=-1, keepdims=True) + L2_EPS) * scale
        o_ref[...] = s * jnp.where(is_qk, r, 1.0)

    return pl.pallas_call(
        body,
        grid=(GDN_QKV // LANES, nt),
        in_specs=[
            pl.BlockSpec((tb, LANES), lambda c, i: (i, c)),
            pl.BlockSpec((CONV_HALO, LANES), lambda c, i: (jnp.maximum(i * hb - 1, 0), c)),
            pl.BlockSpec((GDN_CONV, LANES), lambda c, i: (0, c)),
        ],
        out_specs=pl.BlockSpec((tb, LANES), lambda c, i: (i, c)),
        out_shape=jax.ShapeDtypeStruct((t, GDN_QKV), F32),
        compiler_params=_cparams(("parallel", "parallel")),
        name=name,
    )(pm, pm, conv_w)


def _gdn_conv_bwd(pm, conv_w, dout, *, name):
    t = pm.shape[0]
    tb = min(t, 1024)
    nt = t // tb
    hb = tb // CONV_HALO
    last_hb = t // CONV_HALO - 1
    ext = tb + CONV_HALO

    def body(x_ref, xp_ref, xn_ref, d_ref, dn_ref, w_ref, dx_ref, dw_ref):
        c = pl.program_id(0)
        ti = pl.program_id(1)
        prev = jnp.where(ti > 0, xp_ref[...], 0.0)
        has_next = ti < nt - 1
        nxt = jnp.where(has_next, xn_ref[...], 0.0)
        xe = jnp.concatenate([prev, x_ref[...], nxt], axis=0)
        de = jnp.concatenate([d_ref[...], jnp.where(has_next, dn_ref[...], 0.0)], axis=0)
        w = w_ref[...]
        y = jnp.zeros((ext, LANES), F32)
        for j in range(GDN_CONV):
            off = CONV_HALO - (GDN_CONV - 1) + j
            y = y + w[j:j + 1, :] * xe[off:off + ext, :]
        sig = 1.0 / (1.0 + jnp.exp(-y))
        s = y * sig
        is_qk, scale = _conv_tile_scale(c)
        r = lax.rsqrt(jnp.sum(s * s, axis=-1, keepdims=True) + L2_EPS)
        n = s * r
        dnrm = de * scale
        ds_qk = r * (dnrm - n * jnp.sum(dnrm * n, axis=-1, keepdims=True))
        ds = jnp.where(is_qk, ds_qk, de)
        dy = ds * (sig + s * (1.0 - sig))
        dx = jnp.zeros((tb, LANES), F32)
        dw_rows = []
        for j in range(GDN_CONV):
            sh = GDN_CONV - 1 - j
            dx = dx + w[j:j + 1, :] * dy[sh:sh + tb, :]
            off = CONV_HALO - (GDN_CONV - 1) + j
            dw_rows.append(jnp.sum(dy[:tb, :] * xe[off:off + tb, :], axis=0, keepdims=True))
        dx_ref[...] = dx.astype(dx_ref.dtype)
        part = jnp.concatenate(dw_rows, axis=0)

        @pl.when(ti == 0)
        def _():
            dw_ref[...] = part

        @pl.when(ti > 0)
        def _():
            dw_ref[...] += part

    main = pl.BlockSpec((tb, LANES), lambda c, i: (i, c))
    prev = pl.BlockSpec((CONV_HALO, LANES), lambda c, i: (jnp.maximum(i * hb - 1, 0), c))
    nxt = pl.BlockSpec((CONV_HALO, LANES), lambda c, i: (jnp.minimum((i + 1) * hb, last_hb), c))
    return pl.pallas_call(
        body,
        grid=(GDN_QKV // LANES, nt),
        in_specs=[main, prev, nxt, main, nxt, pl.BlockSpec((GDN_CONV, LANES), lambda c, i: (0, c))],
        out_specs=[main, pl.BlockSpec((GDN_CONV, LANES), lambda c, i: (0, c))],
        out_shape=[jax.ShapeDtypeStruct((t, GDN_QKV), MM_DTYPE), jax.ShapeDtypeStruct((GDN_CONV, GDN_QKV), F32)],
        compiler_params=_cparams(("parallel", "arbitrary")),
        name=name,
    )(pm, pm, pm, dout, dout, conv_w)


def _head_selector(first_col):
    row = lax.broadcasted_iota(jnp.int32, (LANES, GDN_HEADS * LANES), 0)
    col = lax.broadcasted_iota(jnp.int32, (LANES, GDN_HEADS * LANES), 1)
    return (col // LANES + first_col == row).astype(F32)


def _softplus(x):
    return jnp.maximum(x, 0.0) + jnp.log(1.0 + jnp.exp(-jnp.abs(x)))


def _gdn_gates_fwd(ab, alog_row, dt_row, *, name):
    t = ab.shape[0]
    tb = min(t, 1024)
    wide = GDN_HEADS * LANES

    def body(ab_ref, al_ref, dt_ref, g_ref, b_ref):
        x = ab_ref[...]
        g_cols = -jnp.exp(al_ref[...]) * _softplus(x + dt_ref[...])
        b_cols = 1.0 / (1.0 + jnp.exp(-x))
        g_ref[...] = _dot(g_cols, _head_selector(0))
        b_ref[...] = _dot(b_cols, _head_selector(GDN_HEADS))

    row = pl.BlockSpec((tb, LANES), lambda i: (i, 0))
    one = pl.BlockSpec((1, LANES), lambda i: (0, 0))
    out = pl.BlockSpec((tb, wide), lambda i: (i, 0))
    return pl.pallas_call(
        body,
        grid=(t // tb,),
        in_specs=[row, one, one],
        out_specs=[out, out],
        out_shape=[jax.ShapeDtypeStruct((t, wide), F32)] * 2,
        compiler_params=_cparams(("parallel",)),
        name=name,
    )(ab, alog_row, dt_row)


def _gdn_gates_bwd(ab, alog_row, dt_row, dgb, dbb, *, name):
    t = ab.shape[0]
    tb = min(t, 1024)
    wide = GDN_HEADS * LANES

    def body(ab_ref, al_ref, dt_ref, dg_ref, db_ref, dab_ref, dal_ref, ddt_ref):
        x = ab_ref[...]
        lane = lax.broadcasted_iota(jnp.int32, (tb, LANES), 1)
        dg_cols = _dot_nt(dg_ref[...], _head_selector(0))
        db_cols = _dot_nt(db_ref[...], _head_selector(GDN_HEADS))
        ea = jnp.exp(al_ref[...])
        z = x + dt_ref[...]
        sp = _softplus(z)
        sg = 1.0 / (1.0 + jnp.exp(-z))
        beta = 1.0 / (1.0 + jnp.exp(-x))
        da = jnp.where(lane < GDN_HEADS, dg_cols * (-ea) * sg, 0.0)
        db = jnp.where((lane >= GDN_HEADS) & (lane < 2 * GDN_HEADS), db_cols * beta * (1.0 - beta), 0.0)
        dab_ref[...] = (da + db).astype(dab_ref.dtype)
        p_al = jnp.sum(jnp.where(lane < GDN_HEADS, dg_cols * (-ea) * sp, 0.0), axis=0, keepdims=True)
        p_dt = jnp.sum(da, axis=0, keepdims=True)

        @pl.when(pl.program_id(0) == 0)
        def _():
            dal_ref[...] = p_al
            ddt_ref[...] = p_dt

        @pl.when(pl.program_id(0) > 0)
        def _():
            dal_ref[...] += p_al
            ddt_ref[...] += p_dt

    row = pl.BlockSpec((tb, LANES), lambda i: (i, 0))
    one = pl.BlockSpec((1, LANES), lambda i: (0, 0))
    big = pl.BlockSpec((tb, wide), lambda i: (i, 0))
    return pl.pallas_call(
        body,
        grid=(t // tb,),
        in_specs=[row, one, one, big, big],
        out_specs=[row, one, one],
        out_shape=[jax.ShapeDtypeStruct((t, LANES), MM_DTYPE), jax.ShapeDtypeStruct((1, LANES), F32),
                   jax.ShapeDtypeStruct((1, LANES), F32)],
        compiler_params=_cparams(("arbitrary",)),
        name=name,
    )(ab, alog_row, dt_row, dgb, dbb)


@jax.custom_vjp
def _unit_lower_inverse_rest(n):
    c = n.shape[-1]
    ri = lax.broadcasted_iota(jnp.int32, (c, c), 0)
    ci = lax.broadcasted_iota(jnp.int32, (c, c), 1)
    rest = None
    size = 1
    while size < c:
        joins = ((ri // (2 * size)) == (ci // (2 * size))) & ((ri // size) != (ci // size))
        low = jnp.where(joins, n, 0.0)
        if rest is None:
            rest = -low
        else:
            left = low + _bdot(rest, low)
            rest = rest - (left + _bdot(left, rest))
        size *= 2
    return rest


def _unit_lower_inverse_rest_fwd(n):
    rest = _unit_lower_inverse_rest(n)
    return rest, rest


def _unit_lower_inverse_rest_bwd(rest, ct):
    left = ct + _bdot_tn(rest, ct)
    return (-(left + _bdot_nt(left, rest)),)


_unit_lower_inverse_rest.defvjp(_unit_lower_inverse_rest_fwd, _unit_lower_inverse_rest_bwd)


def _bf16_pieces(x):
    hi = x.astype(BF16)
    r1 = x - hi.astype(F32)
    mid = r1.astype(BF16)
    lo = (r1 - mid.astype(F32)).astype(BF16)
    return hi, mid, lo


def _lower_ones(shape):
    c = shape[-1]
    ri = lax.broadcasted_iota(jnp.int32, (c, c), 0)
    ci = lax.broadcasted_iota(jnp.int32, (c, c), 1)
    return jnp.broadcast_to((ri >= ci).astype(BF16), shape)


@jax.custom_vjp
def _running_sum(x):
    tri = _lower_ones(x.shape)
    return sum(_bdot(tri, p) for p in _bf16_pieces(x))


def _running_sum_fwd(x):
    return _running_sum(x), None


def _running_sum_bwd(_, ct):
    tri = _lower_ones(ct.shape)
    return (sum(_bdot_tn(tri, p) for p in _bf16_pieces(ct)),)


_running_sum.defvjp(_running_sum_fwd, _running_sum_bwd)


def _gdn_prep_math(q, k, v, gb, bb):
    c = GDN_CHUNK
    ri = lax.broadcasted_iota(jnp.int32, (c, c), 0)
    ci = lax.broadcasted_iota(jnp.int32, (c, c), 1)
    causal = ri >= ci
    gc = _running_sum(gb)
    decay = jnp.exp(jnp.where(causal, gc - jnp.swapaxes(gc, -1, -2), NEG))
    n = jnp.where(ri > ci, _bdot_nt(k, k) * bb * decay, 0.0)
    rest = _unit_lower_inverse_rest(n)
    eg = jnp.exp(gc)
    rhs_v = v * bb
    rhs_k = k * bb * eg
    u = rhs_v + _bdot(rest, rhs_v)
    w = rhs_k + _bdot(rest, rhs_k)
    qk = _bdot_nt(q, k) * decay
    qd = q * eg
    last = jnp.sum(jnp.where(ri == c - 1, gc, 0.0), axis=-2, keepdims=True)
    gl = jnp.broadcast_to(last, gc.shape)
    kt = k * jnp.exp(gl - gc)
    cd = jnp.exp(gl)
    return u, w, qk, qd, kt, cd


def _head_tiles(ref, h):
    return ref[:, h * LANES:(h + 1) * LANES]


def _stack_heads(ref, first=0, heads=GDN_HEADS):
    return jnp.stack([_head_tiles(ref, first + h) for h in range(heads)])


def _store_heads(ref, val, first=0):
    for h in range(val.shape[0]):
        ref[:, (first + h) * LANES:(first + h + 1) * LANES] = val[h].astype(ref.dtype)


def _gdn_prep_fwd(qkv, gb, bb, *, name):
    t = qkv.shape[0]
    c = GDN_CHUNK
    wide = GDN_HEADS * LANES

    def body(q_ref, k_ref, v_ref, g_ref, b_ref, *outs):
        res = _gdn_prep_math(*(_stack_heads(r) for r in (q_ref, k_ref, v_ref, g_ref, b_ref)))
        for o_ref, val in zip(outs, res):
            _store_heads(o_ref, val)

    blk = lambda off: pl.BlockSpec((c, wide), lambda i: (i, off))
    return pl.pallas_call(
        body,
        grid=(t // c,),
        in_specs=[blk(0), blk(1), blk(2), blk(0), blk(0)],
        out_specs=[blk(0)] * 6,
        out_shape=[jax.ShapeDtypeStruct((t, wide), F32)] * 6,
        compiler_params=_cparams(("parallel",)),
        name=name,
    )(qkv, qkv, qkv, gb, bb)


def _gdn_prep_bwd(qkv, gb, bb, cts, *, name):
    t = qkv.shape[0]
    c = GDN_CHUNK
    wide = GDN_HEADS * LANES

    def body(q_ref, k_ref, v_ref, g_ref, b_ref, c0, c1, c2, c3, c4, c5, dqkv_ref, dg_ref, db_ref):
        prim = tuple(_stack_heads(r) for r in (q_ref, k_ref, v_ref, g_ref, b_ref))
        _, pull = jax.vjp(_gdn_prep_math, *prim)
        dq, dk, dv, dg, db = pull(tuple(_stack_heads(r) for r in (c0, c1, c2, c3, c4, c5)))
        _store_heads(dqkv_ref, dq)
        _store_heads(dqkv_ref, dk, first=GDN_HEADS)
        _store_heads(dqkv_ref, dv, first=2 * GDN_HEADS)
        _store_heads(dg_ref, dg)
        _store_heads(db_ref, db)

    blk = lambda off: pl.BlockSpec((c, wide), lambda i: (i, off))
    return pl.pallas_call(
        body,
        grid=(t // c,),
        in_specs=[blk(0), blk(1), blk(2), blk(0), blk(0)] + [blk(0)] * 6,
        out_specs=[pl.BlockSpec((c, 3 * wide), lambda i: (i, 0)), blk(0), blk(0)],
        out_shape=[jax.ShapeDtypeStruct((t, 3 * wide), F32), jax.ShapeDtypeStruct((t, wide), F32),
                   jax.ShapeDtypeStruct((t, wide), F32)],
        compiler_params=_cparams(("parallel",)),
        name=name,
    )(qkv, qkv, qkv, gb, bb, *cts)


def _gdn_scan_math(s, u, w, qk, qd, kt, cd):
    v_new = u - _bdot(w, s)
    o = _bdot(qd, s) + _bdot(qk, v_new)
    s_new = s * cd + _bdot_tn(kt, v_new)
    return o, s_new


def _gdn_scan_fwd(prep, *, name):
    t = prep[0].shape[0]
    c = GDN_CHUNK
    wide = GDN_HEADS * LANES

    def body(u_ref, w_ref, qk_ref, qd_ref, kt_ref, cd_ref, o_ref, st_ref, s_ref):
        @pl.when(pl.program_id(0) == 0)
        def _():
            s_ref[...] = jnp.zeros_like(s_ref)

        s = _stack_heads(s_ref)
        _store_heads(st_ref, s)
        o, s_new = _gdn_scan_math(s, *(_stack_heads(r) for r in (u_ref, w_ref, qk_ref, qd_ref, kt_ref, cd_ref)))
        _store_heads(o_ref, o)
        _store_heads(s_ref, s_new)

    blk = pl.BlockSpec((c, wide), lambda i: (i, 0))
    return pl.pallas_call(
        body,
        grid=(t // c,),
        in_specs=[blk] * 6,
        out_specs=[blk, blk],
        out_shape=[jax.ShapeDtypeStruct((t, wide), F32)] * 2,
        scratch_shapes=[pltpu.VMEM((GDN_DK, wide), F32)],
        compiler_params=_cparams(("arbitrary",)),
        name=name,
    )(*prep)


def _gdn_scan_bwd(prep, states, do, *, name):
    t = do.shape[0]
    c = GDN_CHUNK
    wide = GDN_HEADS * LANES
    nc = t // c

    def body(u_ref, w_ref, qk_ref, qd_ref, kt_ref, cd_ref, st_ref, do_ref, *rest):
        outs, ds_ref = rest[:6], rest[6]

        @pl.when(pl.program_id(0) == 0)
        def _():
            ds_ref[...] = jnp.zeros_like(ds_ref)

        prim = tuple(_stack_heads(r) for r in (st_ref, u_ref, w_ref, qk_ref, qd_ref, kt_ref, cd_ref))
        _, pull = jax.vjp(_gdn_scan_math, *prim)
        grads = pull((_stack_heads(do_ref), _stack_heads(ds_ref)))
        _store_heads(ds_ref, grads[0])
        for o_ref, val in zip(outs, grads[1:]):
            _store_heads(o_ref, val)

    blk = pl.BlockSpec((c, wide), lambda i: (nc - 1 - i, 0))
    return pl.pallas_call(
        body,
        grid=(nc,),
        in_specs=[blk] * 8,
        out_specs=[blk] * 6,
        out_shape=[jax.ShapeDtypeStruct((t, wide), F32)] * 6,
        scratch_shapes=[pltpu.VMEM((GDN_DK, wide), F32)],
        compiler_params=_cparams(("arbitrary",)),
        name=name,
    )(*prep, states, do)


def _gdn_outgate_math(o, z, nw):
    r = lax.rsqrt(jnp.mean(o * o, axis=-1, keepdims=True) + RMS_EPS)
    return o * r * nw * _silu(z)


def _gdn_outgate_fwd(o, pm, nw_row, *, name):
    t = o.shape[0]
    tb = min(t, 1024)
    z_off = GDN_QKV // LANES

    def body(o_ref, z_ref, nw_ref, y_ref):
        y_ref[...] = _gdn_outgate_math(o_ref[...], z_ref[...], nw_ref[...]).astype(y_ref.dtype)

    return pl.pallas_call(
        body,
        grid=(t // tb, GDN_HEADS),
        in_specs=[pl.BlockSpec((tb, LANES), lambda i, h: (i, h)), pl.BlockSpec((tb, LANES), lambda i, h: (i, h + z_off)),
                  pl.BlockSpec((1, LANES), lambda i, h: (0, 0))],
        out_specs=pl.BlockSpec((tb, LANES), lambda i, h: (i, h)),
        out_shape=jax.ShapeDtypeStruct((t, GDN_HEADS * LANES), MM_DTYPE),
        compiler_params=_cparams(("parallel", "parallel")),
        name=name,
    )(o, pm, nw_row)


def _gdn_outgate_bwd(o, pm, nw_row, dy, *, name):
    t = o.shape[0]
    tb = min(t, 1024)
    z_off = GDN_QKV // LANES

    def body(o_ref, z_ref, nw_ref, dy_ref, do_ref, dz_ref, dnw_ref):
        _, pull = jax.vjp(_gdn_outgate_math, o_ref[...], z_ref[...], nw_ref[...])
        d_o, d_z, d_nw = pull(dy_ref[...])
        do_ref[...] = d_o
        dz_ref[...] = d_z.astype(dz_ref.dtype)
        first = (pl.program_id(0) == 0) & (pl.program_id(1) == 0)

        @pl.when(first)
        def _():
            dnw_ref[...] = d_nw

        @pl.when(jnp.logical_not(first))
        def _():
            dnw_ref[...] += d_nw

    blk = pl.BlockSpec((tb, LANES), lambda i, h: (i, h))
    one = pl.BlockSpec((1, LANES), lambda i, h: (0, 0))
    return pl.pallas_call(
        body,
        grid=(t // tb, GDN_HEADS),
        in_specs=[blk, pl.BlockSpec((tb, LANES), lambda i, h: (i, h + z_off)), one, blk],
        out_specs=[blk, blk, one],
        out_shape=[jax.ShapeDtypeStruct((t, GDN_HEADS * LANES), F32),
                   jax.ShapeDtypeStruct((t, GDN_HEADS * LANES), MM_DTYPE), jax.ShapeDtypeStruct((1, LANES), F32)],
        compiler_params=_cparams(("arbitrary", "arbitrary")),
        name=name,
    )(o, pm, nw_row, dy)


def _rms64(x, w_row):
    return x * lax.rsqrt(jnp.sum(x * x, axis=-1, keepdims=True) * (1.0 / DIL_DH) + RMS_EPS) * w_row


def _alibi_slopes(group):
    head = lax.broadcasted_iota(jnp.int32, (DIL_HEADS, 8, LANES), 0).astype(F32)
    rate = -math.log(2.0) * ALIBI_MAX_BIAS / (len(DIL_GROUPS) * DIL_HEADS)
    slope = jnp.exp(rate * (head + float(group * DIL_HEADS + 1)))
    return jnp.broadcast_to(slope[:, 0:1, :], (DIL_HEADS, DIL_SPAN, LANES))


def _band_logits(qn, kp, kc, slope_d, has_prev):
    qi = lax.broadcasted_iota(jnp.int32, (DIL_SPAN, DIL_SPAN), 0)
    kj = lax.broadcasted_iota(jnp.int32, (DIL_SPAN, DIL_SPAN), 1)
    steps_c = (qi - kj).astype(F32)
    scale = DIL_DH ** -0.5
    sp = _bdot_nt(qn, kp) * scale - slope_d * (steps_c + float(DIL_SPAN))
    sc = _bdot_nt(qn, kc) * scale - slope_d * steps_c
    sp = jnp.where((kj >= qi) & has_prev, sp, NEG)
    sc = jnp.where(kj <= qi, sc, NEG)
    return sp, sc


def _dil_attn_fwd(slab, wq_row, wk_row, *, group, name):
    dilation = DIL_GROUPS[group][1]
    t = slab.shape[0]
    rows = t // dilation
    nlb = rows // DIL_SPAN
    wide = DIL_HEADS * LANES
    view = slab.reshape(rows, dilation * DIL_SLAB)

    def body(q_ref, kc_ref, vc_ref, kp_ref, vp_ref, wq_ref, wk_ref, o_ref):
        has_prev = pl.program_id(1) > 0
        lane = lax.broadcasted_iota(jnp.int32, (DIL_SPAN, LANES), 1)
        qn = _rms64(_stack_heads(q_ref), wq_ref[...])
        kc = _rms64(_stack_heads(kc_ref), wk_ref[...])
        kp = _rms64(_stack_heads(kp_ref), wk_ref[...])
        sp, sc = _band_logits(qn, kp, kc, _alibi_slopes(group) * float(dilation), has_prev)
        m = jnp.maximum(jnp.max(sp, axis=-1, keepdims=True), jnp.max(sc, axis=-1, keepdims=True))
        pp = jnp.exp(sp - m)
        pc = jnp.exp(sc - m)
        l = jnp.sum(pp, axis=-1, keepdims=True) + jnp.sum(pc, axis=-1, keepdims=True)
        o = (_bdot(pp, _stack_heads(vp_ref)) + _bdot(pc, _stack_heads(vc_ref))) / l
        _store_heads(o_ref, jnp.where(lane < DIL_DH, o, m + jnp.log(l)))

    cur = lambda part: pl.BlockSpec((DIL_SPAN, wide), lambda r, i: (i, 3 * r + part))
    prv = lambda part: pl.BlockSpec((DIL_SPAN, wide), lambda r, i: (jnp.maximum(i - 1, 0), 3 * r + part))
    one = pl.BlockSpec((1, LANES), lambda r, i: (0, 0))
    out = pl.pallas_call(
        body,
        grid=(dilation, nlb),
        in_specs=[cur(0), cur(1), cur(2), prv(1), prv(2), one, one],
        out_specs=pl.BlockSpec((DIL_SPAN, wide), lambda r, i: (i, r)),
        out_shape=jax.ShapeDtypeStruct((rows, dilation * wide), F32),
        compiler_params=_cparams(("parallel", "parallel")),
        name=name,
    )(view, view, view, view, view, wq_row, wk_row)
    return out.reshape(t, wide)


def _dil_merge_fwd(oe, *, name):
    t = oe[0].shape[0]
    tb = min(t, 1024)

    def body(e0, e1, e2, y_ref, om_ref):
        lane = lax.broadcasted_iota(jnp.int32, (tb, LANES), 1)
        es = [e0[...], e1[...], e2[...]]
        lse = [jnp.sum(jnp.where(lane == DIL_DH, e, 0.0), axis=-1, keepdims=True) for e in es]
        top = jnp.maximum(jnp.maximum(lse[0], lse[1]), lse[2])
        joint = top + jnp.log(jnp.exp(lse[0] - top) + jnp.exp(lse[1] - top) + jnp.exp(lse[2] - top))
        o = sum(jnp.exp(l - joint) * e for l, e in zip(lse, es))
        y_ref[...] = jnp.where(lane < DIL_DH, o, 0.0).astype(y_ref.dtype)
        om_ref[...] = jnp.where(lane < DIL_DH, o, joint)

    blk = pl.BlockSpec((tb, LANES), lambda i, h: (i, h))
    return pl.pallas_call(
        body,
        grid=(t // tb, DIL_HEADS),
        in_specs=[blk] * 3,
        out_specs=[blk, blk],
        out_shape=[jax.ShapeDtypeStruct((t, DIL_HEADS * LANES), MM_DTYPE),
                   jax.ShapeDtypeStruct((t, DIL_HEADS * LANES), F32)],
        compiler_params=_cparams(("parallel", "parallel")),
        name=name,
    )(*oe)


def _dil_merge_bwd(dy, om, *, name):
    t = dy.shape[0]
    tb = min(t, 1024)

    def body(dy_ref, om_ref, st_ref):
        lane = lax.broadcasted_iota(jnp.int32, (tb, LANES), 1)
        d_o = jnp.where(lane < DIL_DH, dy_ref[...], 0.0)
        om_t = om_ref[...]
        delta = jnp.sum(d_o * om_t, axis=-1, keepdims=True)
        st_ref[...] = jnp.where(lane < DIL_DH, d_o, jnp.where(lane == DIL_DH, om_t, jnp.where(lane == DIL_DH + 1, delta, 0.0)))

    blk = pl.BlockSpec((tb, LANES), lambda i, h: (i, h))
    return pl.pallas_call(
        body,
        grid=(t // tb, DIL_HEADS),
        in_specs=[blk, blk],
        out_specs=blk,
        out_shape=jax.ShapeDtypeStruct((t, DIL_HEADS * LANES), F32),
        compiler_params=_cparams(("parallel", "parallel")),
        name=name,
    )(dy, om)


def _rms64_bwd(x, w_row, dy):
    r = lax.rsqrt(jnp.sum(x * x, axis=-1, keepdims=True) * (1.0 / DIL_DH) + RMS_EPS)
    gw = dy * w_row
    dx = r * gw - x * (r * r * r * jnp.sum(gw * x, axis=-1, keepdims=True) * (1.0 / DIL_DH))
    return dx, dy * x * r


def _dil_attn_bwd(slab, stat, wq_row, wk_row, dwq_in, dwk_in, *, group, name):
    dilation = DIL_GROUPS[group][1]
    t = slab.shape[0]
    rows = t // dilation
    nlb = rows // DIL_SPAN
    wide = DIL_HEADS * LANES
    view = slab.reshape(rows, dilation * DIL_SLAB)
    stat_view = stat.reshape(rows, dilation * wide)

    def body(cur_ref, kp_ref, vp_ref, st_ref, wq_ref, wk_ref, dwq_in_ref, dwk_in_ref, d_ref, dwq_ref, dwk_ref,
             dk_carry, dv_carry):
        step = pl.program_id(1)
        has_prev = step < nlb - 1
        first = (pl.program_id(0) == 0) & (step == 0)

        @pl.when(step == 0)
        def _():
            dk_carry[...] = jnp.zeros_like(dk_carry)
            dv_carry[...] = jnp.zeros_like(dv_carry)

        @pl.when(first)
        def _():
            dwq_ref[...] = dwq_in_ref[...]
            dwk_ref[...] = dwk_in_ref[...]

        lane = lax.broadcasted_iota(jnp.int32, (DIL_SPAN, LANES), 1)
        scale = DIL_DH ** -0.5
        q_raw = _stack_heads(cur_ref)
        kc_raw = _stack_heads(cur_ref, first=DIL_HEADS)
        vc = _stack_heads(cur_ref, first=2 * DIL_HEADS)
        kp_raw = _stack_heads(kp_ref)
        vp = _stack_heads(vp_ref)
        st = _stack_heads(st_ref)
        d_o = jnp.where(lane < DIL_DH, st, 0.0)
        lse = jnp.sum(jnp.where(lane == DIL_DH, st, 0.0), axis=-1, keepdims=True)
        delta = jnp.sum(jnp.where(lane == DIL_DH + 1, st, 0.0), axis=-1, keepdims=True)
        qn = _rms64(q_raw, wq_ref[...])
        kc = _rms64(kc_raw, wk_ref[...])
        kp = _rms64(kp_raw, wk_ref[...])
        sp, sc = _band_logits(qn, kp, kc, _alibi_slopes(group) * float(dilation), has_prev)
        pp = jnp.exp(sp - lse)
        pc = jnp.exp(sc - lse)
        dsp = pp * (_bdot_nt(d_o, vp) - delta) * scale
        dsc = pc * (_bdot_nt(d_o, vc) - delta) * scale
        dqn = _bdot(dsp, kp) + _bdot(dsc, kc)
        dkc_n = _bdot_tn(dsc, qn) + _stack_heads(dk_carry)
        dvc = _bdot_tn(pc, d_o) + _stack_heads(dv_carry)
        _store_heads(dk_carry, _bdot_tn(dsp, qn))
        _store_heads(dv_carry, _bdot_tn(pp, d_o))
        dq_raw, dwq_rows = _rms64_bwd(q_raw, wq_ref[...], dqn)
        dk_raw, dwk_rows = _rms64_bwd(kc_raw, wk_ref[...], dkc_n)
        _store_heads(d_ref, dq_raw)
        _store_heads(d_ref, dk_raw, first=DIL_HEADS)
        _store_heads(d_ref, dvc, first=2 * DIL_HEADS)
        dwq_ref[...] += jnp.sum(jnp.sum(dwq_rows, axis=0), axis=0, keepdims=True)
        dwk_ref[...] += jnp.sum(jnp.sum(dwk_rows, axis=0), axis=0, keepdims=True)

    blk_i = lambda i: nlb - 1 - i
    cur = pl.BlockSpec((DIL_SPAN, DIL_SLAB), lambda r, i: (blk_i(i), r))
    prv = lambda part: pl.BlockSpec((DIL_SPAN, wide), lambda r, i: (jnp.maximum(blk_i(i) - 1, 0), 3 * r + part))
    one = pl.BlockSpec((1, LANES), lambda r, i: (0, 0))
    dslab, dwq, dwk = pl.pallas_call(
        body,
        grid=(dilation, nlb),
        in_specs=[cur, prv(1), prv(2), pl.BlockSpec((DIL_SPAN, wide), lambda r, i: (blk_i(i), r)), one, one, one, one],
        out_specs=[cur, one, one],
        out_shape=[jax.ShapeDtypeStruct((rows, dilation * DIL_SLAB), MM_DTYPE), jax.ShapeDtypeStruct((1, LANES), F32),
                   jax.ShapeDtypeStruct((1, LANES), F32)],
        scratch_shapes=[pltpu.VMEM((DIL_SPAN, wide), F32), pltpu.VMEM((DIL_SPAN, wide), F32)],
        compiler_params=_cparams(("arbitrary", "arbitrary")),
        name=name,
    )(view, view, view, stat_view, wq_row, wk_row, dwq_in, dwk_in)
    return dslab.reshape(t, DIL_SLAB), dwq, dwk


def _row(v, width=LANES):
    v = v.astype(F32).reshape(-1)
    return jnp.pad(v, (0, width - v.shape[0])).reshape(1, width)


def _prepare_weights(w):
    return dict(gdn=_prepare_gdn(w), dil=_prepare_dil(w), ffn=_prepare_ffn(w))


def _prepare_gdn(w):
    gdn = []
    for j in range(DEPTH // 2):
        wt = w["gdn_w_in"][j]
        gates_t = jnp.pad(wt[GDN_MAIN:], ((0, LANES - 2 * GDN_HEADS), (0, 0)))
        gdn.append(dict(in_t=wt, gates_t=gates_t, out=w["gdn_w_out"][j], conv=w["gdn_conv_w"][j].astype(F32),
                        alog=_row(w["gdn_a_log"][j]), dt=_row(w["gdn_dt_bias"][j]), nw=_row(w["gdn_norm_w"][j])))
    return gdn


def _prepare_dil(w):
    d = D_MODEL
    dil = []
    for j in range(DEPTH // 2):
        wt = w["dil_w_in"][j].reshape(3, len(DIL_GROUPS), DIL_HEADS, DIL_DH, d)
        wg_t = [jnp.pad(wt[:, g], ((0, 0), (0, 0), (0, LANES - DIL_DH), (0, 0))).reshape(DIL_SLAB, d)
                for g in range(len(DIL_GROUPS))]
        out_t = jnp.pad(w["dil_w_out"][j].reshape(d, DIL_HEADS, DIL_DH), ((0, 0), (0, 0), (0, LANES - DIL_DH)))
        dil.append(dict(wg_t=wg_t, out_t=out_t.reshape(d, DIL_HEADS * LANES), wq=_row(w["dil_q_norm"][j]),
                        wk=_row(w["dil_k_norm"][j])))
    return dil


def _prepare_ffn(w):
    return [dict(in_t=w["ffn_w_in"][i], out=w["ffn_w_out"][i]) for i in range(DEPTH)]


def _gdn_layer_fwd(x, nrow, p):
    hn = _rmsnorm_fwd(x, nrow, name="rmsnorm_fwd")
    pm = _matmul(hn, p["in_t"], trans_b=True, b_rows=(0, GDN_MAIN), name="gdn_proj_main")
    ab = _matmul(hn, p["gates_t"], trans_b=True, name="gdn_proj_gates")
    qkv = _gdn_conv_fwd(pm, p["conv"], name="gdn_conv_fwd")
    gb, bb = _gdn_gates_fwd(ab, p["alog"], p["dt"], name="gdn_gates_fwd")
    prep = _gdn_prep_fwd(qkv, gb, bb, name="gdn_prep_fwd")
    o, states = _gdn_scan_fwd(prep, name="gdn_scan_fwd")
    og = _gdn_outgate_fwd(o, pm, p["nw"], name="gdn_outgate_fwd")
    y = _matmul(og, p["out"], add=x, name="gdn_proj_out")
    return y, (x, hn, pm, ab, qkv, gb, bb, prep, states, o, og)


def _gdn_layer_bwd(dx, dxb, nrow, p, saved):
    x, hn, pm, ab, qkv, gb, bb, prep, states, o, og = saved
    d_og = _matmul(dxb, p["out"], trans_b=True, name="gdn_dgate")
    g_out = _matmul(og, dxb, trans_a=True, name="gdn_gw_out")
    d_o, d_z, d_nw = _gdn_outgate_bwd(o, pm, p["nw"], d_og, name="gdn_outgate_bwd")
    cts = _gdn_scan_bwd(prep, states, d_o, name="gdn_scan_bwd")
    dqkv, dgb, dbb = _gdn_prep_bwd(qkv, gb, bb, cts, name="gdn_prep_bwd")
    d_ab, d_alog, d_dt = _gdn_gates_bwd(ab, p["alog"], p["dt"], dgb, dbb, name="gdn_gates_bwd")
    d_conv, g_conv = _gdn_conv_bwd(pm, p["conv"], dqkv, name="gdn_conv_bwd")
    d_hn = _matmul(d_conv, p["in_t"], b_rows=(0, GDN_QKV), name="gdn_dhn_qkv")
    d_hn = _matmul(d_z, p["in_t"], b_rows=(GDN_QKV, GDN_MAIN - GDN_QKV), add=d_hn, name="gdn_dhn_z")
    d_hn = _matmul(d_ab, p["gates_t"], add=d_hn, name="gdn_dhn_gates")
    g_in_t = jnp.concatenate([
        _matmul(d_conv, hn, trans_a=True, name="gdn_gw_qkv"),
        _matmul(d_z, hn, trans_a=True, name="gdn_gw_z"),
        _matmul(d_ab, hn, trans_a=True, name="gdn_gw_gates")[:2 * GDN_HEADS],
    ], axis=0)
    dx_new, dxb_new, g_norm = _rmsnorm_bwd(x, nrow, d_hn, dx, name="rmsnorm_bwd")
    grads = dict(w_in=g_in_t, conv=g_conv, a_log=d_alog[0, :GDN_HEADS], dt_bias=d_dt[0, :GDN_HEADS], norm_w=d_nw[0],
                 w_out=g_out, norm=g_norm[0])
    return dx_new, dxb_new, grads


def _dil_layer_fwd(x, nrow, p):
    hn = _rmsnorm_fwd(x, nrow, name="rmsnorm_fwd")
    slabs = [_matmul(hn, p["wg_t"][g], trans_b=True, name="dil_proj_in") for g in range(len(DIL_GROUPS))]
    oe = [_dil_attn_fwd(slabs[g], p["wq"], p["wk"], group=g, name=f"dil_attn_fwd_g{g}") for g in range(len(DIL_GROUPS))]
    y, om = _dil_merge_fwd(oe, name="dil_merge_fwd")
    out = _matmul(y, p["out_t"], trans_b=True, add=x, name="dil_proj_out")
    return out, (x, hn, slabs, y, om)


def _dil_layer_bwd(dx, dxb, nrow, p, saved):
    x, hn, slabs, y, om = saved
    d_y = _matmul(dxb, p["out_t"], name="dil_dmerged")
    g_out_t = _matmul(dxb, y, trans_a=True, name="dil_gw_out")
    g_out_t = g_out_t.reshape(D_MODEL, DIL_HEADS, LANES)[..., :DIL_DH].reshape(D_MODEL, DIL_HEADS * DIL_DH)
    stat = _dil_merge_bwd(d_y, om, name="dil_merge_bwd")
    d_hn = None
    dwq = jnp.zeros((1, LANES), F32)
    dwk = jnp.zeros((1, LANES), F32)
    g_groups = []
    for g in range(len(DIL_GROUPS)):
        dslab, dwq, dwk = _dil_attn_bwd(slabs[g], stat, p["wq"], p["wk"], dwq, dwk, group=g, name=f"dil_attn_bwd_g{g}")
        d_hn = _matmul(dslab, p["wg_t"][g], add=d_hn, name="dil_dhn")
        g_w = _matmul(dslab, hn, trans_a=True, name="dil_gw_in")
        g_groups.append(g_w.reshape(3, DIL_HEADS, LANES, D_MODEL)[:, :, :DIL_DH])
    g_in_t = jnp.stack(g_groups, axis=1).reshape(3 * len(DIL_GROUPS) * DIL_HEADS * DIL_DH, D_MODEL)
    dx_new, dxb_new, g_norm = _rmsnorm_bwd(x, nrow, d_hn, dx, name="rmsnorm_bwd")
    grads = dict(w_in=g_in_t, q_norm=dwq[0, :DIL_DH], k_norm=dwk[0, :DIL_DH], w_out=g_out_t, norm=g_norm[0])
    return dx_new, dxb_new, grads


def _ffn_layer_fwd(x, nrow, p):
    hn = _rmsnorm_fwd(x, nrow, name="rmsnorm_fwd")
    gu = _matmul(hn, p["in_t"], trans_b=True, name="ffn_proj_in")
    act = _swiglu_fwd(gu, name="swiglu_fwd")
    y = _matmul(act, p["out"], add=x, name="ffn_proj_out")
    return y, (x, hn, gu, act)


def _ffn_layer_bwd(dx, dxb, nrow, p, saved):
    x, hn, gu, act = saved
    d_act = _matmul(dxb, p["out"], trans_b=True, name="ffn_dact")
    g_out = _matmul(act, dxb, trans_a=True, name="ffn_gw_out")
    d_g, d_u = _swiglu_bwd(gu, d_act, name="swiglu_bwd")
    d_hn = _matmul(d_g, p["in_t"], b_rows=(0, FFN_HIDDEN), name="ffn_dhn_gate")
    d_hn = _matmul(d_u, p["in_t"], b_rows=(FFN_HIDDEN, FFN_HIDDEN), add=d_hn, name="ffn_dhn_up")
    g_in_t = jnp.concatenate([_matmul(d_g, hn, trans_a=True, name="ffn_gw_gate"),
                              _matmul(d_u, hn, trans_a=True, name="ffn_gw_up")], axis=0)
    dx_new, dxb_new, g_norm = _rmsnorm_bwd(x, nrow, d_hn, dx, name="rmsnorm_bwd")
    return dx_new, dxb_new, dict(w_in=g_in_t, w_out=g_out, norm=g_norm[0])


def _mixer_fwd(i, x, mix_row, prepared):
    if i % 2 == 0:
        return _gdn_layer_fwd(x, mix_row, prepared["gdn"][i // 2])
    return _dil_layer_fwd(x, mix_row, prepared["dil"][i // 2])


def _mixer_bwd(i, dx, dxb, mix_row, prepared, saved):
    if i % 2 == 0:
        return _gdn_layer_bwd(dx, dxb, mix_row, prepared["gdn"][i // 2], saved)
    return _dil_layer_bwd(dx, dxb, mix_row, prepared["dil"][i // 2], saved)


def _local_step(x, target, prepared, norm_mix, norm_ffn):
    saved = []
    for i in range(DEPTH):
        x, s_mix = _mixer_fwd(i, x, norm_mix[i].reshape(1, D_MODEL), prepared)
        x, s_ffn = _ffn_layer_fwd(x, norm_ffn[i].reshape(1, D_MODEL), prepared["ffn"][i])
        saved.append((s_mix, s_ffn))
    dx, dxb, loss = _loss_head(x, target, name="loss_head")
    g_mix, g_ffn = [None] * DEPTH, [None] * DEPTH
    for i in reversed(range(DEPTH)):
        s_mix, s_ffn = saved[i]
        dx, dxb, g_ffn[i] = _ffn_layer_bwd(dx, dxb, norm_ffn[i].reshape(1, D_MODEL), prepared["ffn"][i], s_ffn)
        dx, dxb, g_mix[i] = _mixer_bwd(i, dx, dxb, norm_mix[i].reshape(1, D_MODEL), prepared, s_mix)
    return loss[0, 0], dx, _collect_grads(g_mix, g_ffn)


def _collect_grads(g_mix, g_ffn):
    gdn = [g_mix[i] for i in range(0, DEPTH, 2)]
    dil = [g_mix[i] for i in range(1, DEPTH, 2)]
    if any(g is None for g in g_mix + g_ffn):
        pick = lambda gs, key: [None if g is None else g[key] for g in gs]
        return dict(gdn_w_in=pick(gdn, "w_in"), gdn_w_out=pick(gdn, "w_out"), dil_w_in=pick(dil, "w_in"),
                    dil_w_out=pick(dil, "w_out"), ffn_w_in=pick(g_ffn, "w_in"), ffn_w_out=pick(g_ffn, "w_out"))
    grads = dict(
        norm_mix=jnp.stack([g["norm"] for g in g_mix]),
        norm_ffn=jnp.stack([g["norm"] for g in g_ffn]),
        gdn_w_in=[g["w_in"] for g in gdn],
        gdn_conv_w=jnp.stack([g["conv"] for g in gdn]),
        gdn_a_log=jnp.stack([g["a_log"] for g in gdn]),
        gdn_dt_bias=jnp.stack([g["dt_bias"] for g in gdn]),
        gdn_norm_w=jnp.stack([g["norm_w"] for g in gdn]),
        gdn_w_out=[g["w_out"] for g in gdn],
        dil_w_in=[g["w_in"] for g in dil],
        dil_q_norm=jnp.stack([g["q_norm"] for g in dil]),
        dil_k_norm=jnp.stack([g["k_norm"] for g in dil]),
        dil_w_out=[g["w_out"] for g in dil],
        ffn_w_in=[g["w_in"] for g in g_ffn],
        ffn_w_out=[g["w_out"] for g in g_ffn],
    )
    return grads


MESH_ID = pl.DeviceIdType.MESH
ANY_SPACE = pl.BlockSpec(memory_space=pl.ANY)


def _mesh_position():
    return lax.axis_index("x"), lax.axis_index("y"), lax.axis_index("c")


def _flip(pos, k):
    x, y, c = pos
    return (1 - x if k & 4 else x, 1 - y if k & 2 else y, 1 - c if k & 1 else c)


def _linear(pos):
    return 4 * pos[0] + 2 * pos[1] + pos[2]


def _comm_scratch():
    return [pltpu.SemaphoreType.DMA((N_DEV - 1,)), pltpu.SemaphoreType.DMA((N_DEV - 1,)), pltpu.SemaphoreType.DMA(())]


def _all_gather(shard, *, name):
    def body(x_ref, out_ref, send_sems, recv_sems, local_sem):
        me = _mesh_position()
        mine = out_ref.at[_linear(me)]
        local = pltpu.make_async_copy(x_ref, mine, local_sem)
        local.start()
        copies = []
        for k in range(1, N_DEV):
            cp = pltpu.make_async_remote_copy(src_ref=x_ref, dst_ref=mine, send_sem=send_sems.at[k - 1],
                                              recv_sem=recv_sems.at[k - 1], device_id=_flip(me, k), device_id_type=MESH_ID)
            cp.start()
            copies.append(cp)
        for cp in copies:
            cp.wait()
        local.wait()

    return pl.pallas_call(
        body,
        out_shape=jax.ShapeDtypeStruct((N_DEV,) + shard.shape, shard.dtype),
        in_specs=[ANY_SPACE],
        out_specs=ANY_SPACE,
        scratch_shapes=_comm_scratch(),
        name=name,
    )(shard)


def _exchange(parts, *, name):
    def body(p_ref, out_ref, send_sems, recv_sems, local_sem):
        me = _mesh_position()
        mine = out_ref.at[_linear(me)]
        local = pltpu.make_async_copy(p_ref.at[_linear(me)], mine, local_sem)
        local.start()
        copies = []
        for k in range(1, N_DEV):
            peer = _flip(me, k)
            cp = pltpu.make_async_remote_copy(src_ref=p_ref.at[_linear(peer)], dst_ref=mine, send_sem=send_sems.at[k - 1],
                                              recv_sem=recv_sems.at[k - 1], device_id=peer, device_id_type=MESH_ID)
            cp.start()
            copies.append(cp)
        for cp in copies:
            cp.wait()
        local.wait()

    return pl.pallas_call(
        body,
        out_shape=jax.ShapeDtypeStruct(parts.shape, parts.dtype),
        in_specs=[ANY_SPACE],
        out_specs=ANY_SPACE,
        scratch_shapes=_comm_scratch(),
        name=name,
    )(parts)


HBM_SPACE = pl.BlockSpec(memory_space=pltpu.HBM)
SEM_SPACE = pl.BlockSpec(memory_space=pltpu.SEMAPHORE)
DATAFLOW = pltpu.SideEffectType.DATAFLOW_SIDE_EFFECTING


def _split_copies(src_ref, land_ref, send_sems, recv_sems, per_peer):
    me = _mesh_position()
    mine = land_ref.at[_linear(me)]
    copies = []
    for k in range(1, N_DEV):
        peer = _flip(me, k)
        src = src_ref.at[_linear(peer)] if per_peer else src_ref
        copies.append(pltpu.make_async_remote_copy(src_ref=src, dst_ref=mine, send_sem=send_sems.at[k - 1],
                                                   recv_sem=recv_sems.at[k - 1], device_id=peer, device_id_type=MESH_ID))
    return copies


def _travel_start(src, *, per_peer, name):
    me = _linear(_mesh_position())
    own = src[me] if per_peer else src
    shape = own.shape
    landing = lax.dynamic_update_slice(lax.empty((N_DEV,) + shape, src.dtype), own[None], (me, 0, 0))

    def body(src_ref, land_ref, send_sems, recv_sems, src_thru, land_thru, token):
        for cp in _split_copies(src_ref, land_ref, send_sems, recv_sems, per_peer):
            cp.start()
        token[...] = jnp.zeros_like(token)

    return pl.pallas_call(
        body,
        name=name,
        out_shape=(pltpu.SemaphoreType.DMA((N_DEV - 1,)), pltpu.SemaphoreType.DMA((N_DEV - 1,)),
                   pltpu.HBM(src.shape, src.dtype), pltpu.HBM(landing.shape, landing.dtype),
                   jax.ShapeDtypeStruct((8, LANES), F32)),
        in_specs=(HBM_SPACE, HBM_SPACE),
        out_specs=(SEM_SPACE, SEM_SPACE, HBM_SPACE, HBM_SPACE, pl.BlockSpec(memory_space=pltpu.VMEM)),
        input_output_aliases={0: 2, 1: 3},
        compiler_params=pltpu.CompilerParams(has_side_effects=DATAFLOW),
    )(pltpu.with_memory_space_constraint(src, pltpu.HBM), pltpu.with_memory_space_constraint(landing, pltpu.HBM))


def _travel_wait(started, after, *, per_peer, name):
    send_sems, recv_sems, src_thru, land_thru, _ = started

    def body(src_ref, land_ref, send_sems, recv_sems, after_ref, src_dead, got_ref):
        for cp in _split_copies(src_ref, land_ref, send_sems, recv_sems, per_peer):
            cp.wait_send()
            cp.wait_recv()

    return pl.pallas_call(
        body,
        name=name,
        out_shape=(pltpu.HBM(src_thru.shape, src_thru.dtype), pltpu.HBM(land_thru.shape, land_thru.dtype)),
        in_specs=(HBM_SPACE, HBM_SPACE, SEM_SPACE, SEM_SPACE, ANY_SPACE),
        out_specs=(HBM_SPACE, HBM_SPACE),
        input_output_aliases={0: 0, 1: 1},
        compiler_params=pltpu.CompilerParams(has_side_effects=DATAFLOW),
    )(src_thru, land_thru, send_sems, recv_sems, after)[1]


def _adamw(parts, w, m, v, *, name):
    rows, n = w.shape
    tb = _pick(rows, (PACK_ROW_ALIGN, 16))
    c1 = 1.0 - ADAM_B1 ** ADAM_STEP
    c2 = 1.0 - ADAM_B2 ** ADAM_STEP

    def body(p_ref, w_ref, m_ref, v_ref, g_ref, d_ref, nm_ref, nv_ref):
        g = p_ref[0].astype(F32)
        for s in range(1, N_DEV):
            g = g + p_ref[s].astype(F32)
        m_new = ADAM_B1 * m_ref[...] + (1.0 - ADAM_B1) * g
        v_new = ADAM_B2 * v_ref[...] + (1.0 - ADAM_B2) * (g * g)
        m_hat = m_new / c1
        v_hat = v_new / c2
        g_ref[...] = g
        nm_ref[...] = m_new
        nv_ref[...] = v_new
        d_ref[...] = -ADAM_LR * (m_hat / (jnp.sqrt(v_hat) + ADAM_EPS) + ADAM_WD * w_ref[...])

    blk = pl.BlockSpec((tb, n), lambda i: (i, 0))
    return pl.pallas_call(
        body,
        grid=(rows // tb,),
        in_specs=[pl.BlockSpec((N_DEV, tb, n), lambda i: (0, i, 0)), blk, blk, blk],
        out_specs=[blk] * 4,
        out_shape=[jax.ShapeDtypeStruct((rows, n), F32)] * 4,
        compiler_params=_cparams(("parallel",)),
        name=name,
    )(parts, w, m, v)


PACK_WIDTH = 1024
SHARDED = {
    "gdn_w_in": ((2, D_MODEL, GDN_IN_WIDTH), 2),
    "gdn_conv_w": ((2, GDN_CONV, GDN_QKV), 2),
    "gdn_w_out": ((2, GDN_HEADS * GDN_DV, D_MODEL), 1),
    "dil_w_in": ((2, D_MODEL, 3 * len(DIL_GROUPS) * DIL_HEADS * DIL_DH), 2),
    "dil_w_out": ((2, DIL_HEADS * DIL_DH, D_MODEL), 2),
    "ffn_w_in": ((DEPTH, D_MODEL, 2 * FFN_HIDDEN), 2),
    "ffn_w_out": ((DEPTH, FFN_HIDDEN, D_MODEL), 1),
}
REPLICATED = {"norm_mix": (DEPTH, D_MODEL), "norm_ffn": (DEPTH, D_MODEL), "gdn_a_log": (2, GDN_HEADS),
              "gdn_dt_bias": (2, GDN_HEADS), "gdn_norm_w": (2, GDN_DV), "dil_q_norm": (2, DIL_DH), "dil_k_norm": (2, DIL_DH)}
WEIGHT_ORDER = ("norm_mix", "norm_ffn", "gdn_w_in", "gdn_conv_w", "gdn_a_log", "gdn_dt_bias", "gdn_norm_w", "gdn_w_out",
                "dil_w_in", "dil_q_norm", "dil_k_norm", "dil_w_out", "ffn_w_in", "ffn_w_out")
PACK_ROW_ALIGN = 128
PIECE_ALIGN = 16
SMALL_ROWS = 16


def _shard_shape(name):
    shape, axis = SHARDED[name]
    return tuple(s // N_DEV if i == axis else s for i, s in enumerate(shape))


def _shard_rows(name):
    return math.prod(_shard_shape(name)) // PACK_WIDTH


def _split_shards(full, name):
    shape, axis = SHARDED[name]
    split = full.reshape(shape[:axis] + (N_DEV, shape[axis] // N_DEV) + shape[axis + 1:])
    return jnp.moveaxis(split, axis, 0)


def _join_shards(stacked, name):
    shape, axis = SHARDED[name]
    return jnp.moveaxis(stacked, 0, axis).reshape(shape)


COLUMN_SHARDED = ("gdn_w_in", "dil_w_in", "dil_w_out", "ffn_w_in")


def _to_rows(shard, name):
    if name in COLUMN_SHARDED:
        shard = jnp.swapaxes(shard, 1, 2)
    return shard.reshape(-1, PACK_WIDTH)


def _layer_columns(name):
    _, r, c = _shard_shape(name)
    return r if name in COLUMN_SHARDED else c


def _piece_rows(piece, halves=1):
    name, layer = piece
    rows = _shard_rows(name) * halves
    return rows if layer is None else rows // SHARDED[name][0][0]


def _aligned(rows, to=PIECE_ALIGN):
    return -(-rows // to) * to


def _pack_pieces(arrays, total_align=PIECE_ALIGN):
    padded, total = [], 0
    for a in arrays:
        rows = a.shape[-2]
        extra = _aligned(rows) - rows
        if extra:
            a = jnp.pad(a, [(0, 0)] * (a.ndim - 2) + [(0, extra), (0, 0)])
        padded.append(a)
        total += rows + extra
    tail = _aligned(total, total_align) - total
    if tail:
        padded.append(jnp.zeros(padded[0].shape[:-2] + (tail, PACK_WIDTH), padded[0].dtype))
    return jnp.concatenate(padded, axis=-2)


def _piece_offsets(pieces, halves=None):
    out, at = [], 0
    for p in pieces:
        rows = _piece_rows(p, (halves or {}).get(p[0], 1))
        out.append((p, at, rows))
        at += _aligned(rows)
    return out


def _shard_piece_rows(src, piece):
    name, layer = piece
    part = src[name] if layer is None else src[name][layer:layer + 1]
    return _to_rows(part.astype(F32), name)


def _piece_from_rows(rows, piece):
    name, layer = piece
    layers, r, c = _shard_shape(name)
    n_l = layers if layer is None else 1
    if name in COLUMN_SHARDED:
        return jnp.swapaxes(rows.reshape(n_l, c, r), 1, 2)
    return rows.reshape(n_l, r, c)


SMALL_TAIL = tuple(n for n in REPLICATED if n not in ("norm_mix", "norm_ffn"))


def _pack_small(vals):
    tail, at = jnp.zeros((PACK_WIDTH,), F32), 0
    for n in SMALL_TAIL:
        vec = vals[n].astype(F32).reshape(-1)
        tail = tail + jnp.pad(vec, (at, PACK_WIDTH - at - vec.shape[0]))
        at += vec.shape[0]
    buf = jnp.pad(vals["norm_mix"].astype(F32), ((0, SMALL_ROWS - DEPTH), (0, 0)))
    buf = buf + jnp.pad(vals["norm_ffn"].astype(F32), ((8, SMALL_ROWS - 8 - DEPTH), (0, 0)))
    return buf + jnp.pad(tail.reshape(1, PACK_WIDTH), ((SMALL_ROWS - 1, 0), (0, 0)))


def _unpack_small(buf):
    out = {"norm_mix": buf[0:DEPTH], "norm_ffn": buf[8:8 + DEPTH]}
    at = 0
    for n in SMALL_TAIL:
        size = math.prod(REPLICATED[n])
        out[n] = buf[SMALL_ROWS - 1, at:at + size].reshape(REPLICATED[n])
        at += size
    return out


GATHER_FIRST = ("gdn_w_in", "gdn_conv_w", "gdn_w_out")
GATHER_REST = ("dil_w_in", "dil_w_out", "ffn_w_in", "ffn_w_out")
EXCHANGE_GROUPS = (
    (("ffn_w_in", 3), ("ffn_w_out", 3), ("dil_w_in", 1), ("dil_w_out", 1),
     ("ffn_w_in", 2), ("ffn_w_out", 2), ("gdn_w_in", 1), ("gdn_w_out", 1)),
    (("ffn_w_in", 1), ("ffn_w_out", 1), ("dil_w_in", 0), ("dil_w_out", 0)),
    (("ffn_w_in", 0), ("ffn_w_out", 0), ("gdn_w_in", 0), ("gdn_w_out", 0), ("gdn_conv_w", None)),
)


def _gather_operand(w, names):
    arrays = []
    for n in names:
        if n == "gdn_conv_w":
            arrays.append(lax.bitcast_convert_type(w[n], BF16).reshape(-1, PACK_WIDTH))
        else:
            layers = SHARDED[n][0][0]
            rows = _to_rows(w[n].astype(BF16), n)
            arrays.extend(rows[l * (rows.shape[0] // layers):(l + 1) * (rows.shape[0] // layers)] for l in range(layers))
    return _pack_pieces(arrays)


def _gathered_weights(gathered, names):
    pieces = []
    for n in names:
        pieces.extend([(n, None)] if n == "gdn_conv_w" else [(n, l) for l in range(SHARDED[n][0][0])])
    full = {}
    for (n, layer), at, rows in _piece_offsets(pieces, halves={"gdn_conv_w": 2}):
        block = gathered[:, at:at + rows]
        if n == "gdn_conv_w":
            block = lax.bitcast_convert_type(block.reshape((N_DEV,) + _shard_shape(n) + (2,)), F32)
            full[n] = _join_shards(block, n)
        else:
            full.setdefault(n, []).append(block.reshape(-1, _layer_columns(n)))
    return full


def _exchange_operand(grads, pieces):
    arrays = []
    for n, layer in pieces:
        if layer is None:
            arrays.append(_split_shards(grads[n], n).astype(BF16).reshape(N_DEV, -1, PACK_WIDTH))
        else:
            arrays.append(grads[n][layer].astype(BF16).reshape(N_DEV, -1, PACK_WIDTH))
    return _pack_pieces(arrays, total_align=PACK_ROW_ALIGN)


def _update_group(received, pieces, w, m, v, *, name):
    packed = [_pack_pieces([_shard_piece_rows(src, p) for p in pieces], total_align=PACK_ROW_ALIGN) for src in (w, m, v)]
    outs = _adamw(received, *packed, name=name)
    return {p: tuple(_piece_from_rows(o[at:at + rows], p) for o in outs) for p, at, rows in _piece_offsets(pieces)}


def kernel(x, norm_mix, norm_ffn, gdn_w_in, gdn_conv_w, gdn_a_log, gdn_dt_bias, gdn_norm_w, gdn_w_out, dil_w_in, dil_q_norm, dil_k_norm, dil_w_out, ffn_w_in, ffn_w_out, loss_target, m_norm_mix, m_norm_ffn, m_gdn_w_in, m_gdn_conv_w, m_gdn_a_log, m_gdn_dt_bias, m_gdn_norm_w, m_gdn_w_out, m_dil_w_in, m_dil_q_norm, m_dil_k_norm, m_dil_w_out, m_ffn_w_in, m_ffn_w_out, v_norm_mix, v_norm_ffn, v_gdn_w_in, v_gdn_conv_w, v_gdn_a_log, v_gdn_dt_bias, v_gdn_norm_w, v_gdn_w_out, v_dil_w_in, v_dil_q_norm, v_dil_k_norm, v_dil_w_out, v_ffn_w_in, v_ffn_w_out):
    w = dict(norm_mix=norm_mix, norm_ffn=norm_ffn, gdn_w_in=gdn_w_in, gdn_conv_w=gdn_conv_w, gdn_a_log=gdn_a_log,
             gdn_dt_bias=gdn_dt_bias, gdn_norm_w=gdn_norm_w, gdn_w_out=gdn_w_out, dil_w_in=dil_w_in, dil_q_norm=dil_q_norm,
             dil_k_norm=dil_k_norm, dil_w_out=dil_w_out, ffn_w_in=ffn_w_in, ffn_w_out=ffn_w_out)
    m = dict(norm_mix=m_norm_mix, norm_ffn=m_norm_ffn, gdn_w_in=m_gdn_w_in, gdn_conv_w=m_gdn_conv_w, gdn_a_log=m_gdn_a_log,
             gdn_dt_bias=m_gdn_dt_bias, gdn_norm_w=m_gdn_norm_w, gdn_w_out=m_gdn_w_out, dil_w_in=m_dil_w_in,
             dil_q_norm=m_dil_q_norm, dil_k_norm=m_dil_k_norm, dil_w_out=m_dil_w_out, ffn_w_in=m_ffn_w_in, ffn_w_out=m_ffn_w_out)
    v = dict(norm_mix=v_norm_mix, norm_ffn=v_norm_ffn, gdn_w_in=v_gdn_w_in, gdn_conv_w=v_gdn_conv_w, gdn_a_log=v_gdn_a_log,
             gdn_dt_bias=v_gdn_dt_bias, gdn_norm_w=v_gdn_norm_w, gdn_w_out=v_gdn_w_out, dil_w_in=v_dil_w_in,
             dil_q_norm=v_dil_q_norm, dil_k_norm=v_dil_k_norm, dil_w_out=v_dil_w_out, ffn_w_in=v_ffn_w_in, ffn_w_out=v_ffn_w_out)
    def row(src, i):
        return src[i].reshape(1, D_MODEL)

    first = _all_gather(_gather_operand(w, GATHER_FIRST), name="weight_all_gather_first")
    rest_started = _travel_start(_gather_operand(w, GATHER_REST), per_peer=False, name="weight_gather_start")
    full = dict(_gathered_weights(first, GATHER_FIRST), **{n: w[n] for n in REPLICATED})
    prepared = dict(gdn=_prepare_gdn(full))
    h = x[0]
    saved = [None] * DEPTH
    h, s_mix = _mixer_fwd(0, h, row(norm_mix, 0) + rest_started[4][0, 0], prepared)
    rest = _travel_wait(rest_started, h, per_peer=False, name="weight_gather_wait")
    full.update(_gathered_weights(rest, GATHER_REST))
    prepared.update(dil=_prepare_dil(full), ffn=_prepare_ffn(full))
    for i in range(DEPTH):
        if i > 0:
            h, s_mix = _mixer_fwd(i, h, row(norm_mix, i), prepared)
        h, s_ffn = _ffn_layer_fwd(h, row(norm_ffn, i), prepared["ffn"][i])
        saved[i] = (s_mix, s_ffn)
    dx, dxb, loss = _loss_head(h, loss_target[0], name="loss_head")

    g_mix, g_ffn = [None] * DEPTH, [None] * DEPTH
    started = {}
    for i in reversed(range(DEPTH)):
        s_mix, s_ffn = saved[i]
        dx, dxb, g_ffn[i] = _ffn_layer_bwd(dx, dxb, row(norm_ffn, i), prepared["ffn"][i], s_ffn)
        dx, dxb, g_mix[i] = _mixer_bwd(i, dx, dxb, row(norm_mix, i), prepared, s_mix)
        group = {2: 0, 1: 1}.get(i)
        if group is not None:
            operand = _exchange_operand(_collect_grads(g_mix, g_ffn), EXCHANGE_GROUPS[group])
            started[group] = _travel_start(operand, per_peer=True, name=f"grad_exchange_start_{group}")
            dxb = dxb + started[group][4][0, 0].astype(dxb.dtype)
    grads = _collect_grads(g_mix, g_ffn)
    received = [_travel_wait(started[g], dx, per_peer=True, name=f"grad_exchange_wait_{g}") for g in (0, 1)]
    received.append(_exchange(_exchange_operand(grads, EXCHANGE_GROUPS[2]), name="grad_exchange_last"))
    updated = {}
    for g, pieces in enumerate(EXCHANGE_GROUPS):
        updated.update(_update_group(received[g], pieces, w, m, v, name=f"adamw_sharded_{g}"))

    small_parts = _all_gather(_pack_small(grads), name="small_grad_all_gather")
    outs_small = [_unpack_small(o) for o in
                  _adamw(small_parts, _pack_small(w), _pack_small(m), _pack_small(v), name="adamw_replicated")]

    total_loss = lax.psum(loss[0, 0], ("x", "y", "c"))
    result = [total_loss, dx[None]]
    for k in range(4):
        for n in WEIGHT_ORDER:
            if n not in SHARDED:
                result.append(outs_small[k][n])
            elif (n, None) in updated:
                result.append(updated[(n, None)][k])
            else:
                result.append(jnp.concatenate([updated[(n, l)][k] for l in range(SHARDED[n][0][0])], axis=0))
    return tuple(result)
```

```python
import functools
import math

import jax
import jax.numpy as jnp
from jax import lax
from jax.experimental import pallas as pl
from jax.experimental.pallas import tpu as pltpu

F32 = jnp.float32
BF16 = jnp.bfloat16
MM_DTYPE = BF16

N_DEV = 8
D_MODEL = 1024
DEPTH = 4
RMS_EPS = 1e-6
L2_EPS = 1e-6

LANES = 128

GDN_HEADS = 8
GDN_DK = 128
GDN_DV = 128
GDN_CONV = 4
GDN_CHUNK = 128
GDN_QKV = 3 * GDN_HEADS * GDN_DK
GDN_MAIN = GDN_QKV + GDN_HEADS * GDN_DV
GDN_IN_WIDTH = GDN_MAIN + 2 * GDN_HEADS

DIL_GROUPS = ((128, 1), (512, 4), (2048, 16))
DIL_HEADS = 8
DIL_DH = 64
DIL_SPAN = 128
DIL_SLAB = 3 * DIL_HEADS * LANES
ALIBI_MAX_BIAS = 8.0

FFN_HIDDEN = 2816

ADAM_LR = 0.001
ADAM_B1 = 0.9
ADAM_B2 = 0.999
ADAM_EPS = 1e-08
ADAM_WD = 0.01
ADAM_STEP = 10

VMEM_LIMIT = 56 * 1024 * 1024
NEG = -1e30
HI = lax.Precision.HIGHEST


def _cparams(sem):
    return pltpu.CompilerParams(dimension_semantics=sem, vmem_limit_bytes=VMEM_LIMIT)


def _dot(a, b):
    return lax.dot_general(a, b, (((1,), (0,)), ((), ())), preferred_element_type=F32, precision=HI)


def _dot_nt(a, b):
    return lax.dot_general(a, b, (((1,), (1,)), ((), ())), preferred_element_type=F32, precision=HI)


def _dot_tn(a, b):
    return lax.dot_general(a, b, (((0,), (0,)), ((), ())), preferred_element_type=F32, precision=HI)


def _single_pass(a, b, a_dim, b_dim):
    lead = a.ndim - 2
    batch = ((0,), (0,)) if lead else ((), ())
    return lax.dot_general(a.astype(BF16), b.astype(BF16), (((lead + a_dim,), (lead + b_dim,)), batch),
                           preferred_element_type=F32)


def _bdot(a, b):
    return _single_pass(a, b, 1, 0)


def _bdot_nt(a, b):
    return _single_pass(a, b, 1, 1)


def _bdot_tn(a, b):
    return _single_pass(a, b, 0, 0)


def _pick(n, candidates):
    for c in candidates:
        if n % c == 0:
            return c
    raise ValueError(f"no tile for {n}")


def _matmul(a, b, *, name, trans_a=False, trans_b=False, b_rows=None, add=None, out_dtype=F32):
    if trans_a:
        k_dim, m_dim = a.shape
    else:
        m_dim, k_dim = a.shape
    b_start, b_size = b_rows if b_rows is not None else (0, b.shape[0])
    if trans_b:
        n_dim, k2 = b_size, b.shape[1]
    else:
        k2, n_dim = b_size, b.shape[1]
    assert k_dim == k2, (a.shape, b.shape, b_rows)
    tn = _pick(n_dim, (1024, 512, 256, 128))
    tm = min(m_dim, 2048, max(512, (1024 * 1024) // tn))
    tm = _pick(m_dim, (tm, 1408, 1024, 512, 256, 128))
    tk = _pick(k_dim, (1024, 1408, 512, 256, 128))
    nk = k_dim // tk
    has_add = add is not None
    dn = (((0 if trans_a else 1,), (1 if trans_b else 0,)), ((), ()))
    b_tile = tn if trans_b else tk
    assert b_start % b_tile == 0, (b_rows, b_tile)
    b_off = b_start // b_tile

    def body(*refs):
        if has_add:
            a_ref, b_ref, add_ref, o_ref, acc_ref = refs
        else:
            a_ref, b_ref, o_ref, acc_ref = refs
        part = lax.dot_general(a_ref[...], b_ref[...], dn, preferred_element_type=F32)

        def finish(total):
            if has_add:
                total = total + add_ref[...]
            o_ref[...] = total.astype(out_dtype)

        if nk == 1:
            finish(part)
        else:
            k = pl.program_id(2)

            @pl.when(k == 0)
            def _():
                acc_ref[...] = part

            @pl.when(k > 0)
            def _():
                acc_ref[...] += part

            @pl.when(k == nk - 1)
            def _():
                finish(acc_ref[...])

    if trans_a:
        a_spec = pl.BlockSpec((tk, tm), lambda i, j, k: (k, i))
    else:
        a_spec = pl.BlockSpec((tm, tk), lambda i, j, k: (i, k))
    if trans_b:
        b_spec = pl.BlockSpec((tn, tk), lambda i, j, k: (j + b_off, k))
    else:
        b_spec = pl.BlockSpec((tk, tn), lambda i, j, k: (k + b_off, j))
    in_specs = [a_spec, b_spec]
    args = [a, b]
    if has_add:
        in_specs.append(pl.BlockSpec((tm, tn), lambda i, j, k: (i, j)))
        args.append(add)
    return pl.pallas_call(
        body,
        grid=(m_dim // tm, n_dim // tn, nk),
        in_specs=in_specs,
        out_specs=pl.BlockSpec((tm, tn), lambda i, j, k: (i, j)),
        out_shape=jax.ShapeDtypeStruct((m_dim, n_dim), out_dtype),
        scratch_shapes=[pltpu.VMEM((tm, tn) if nk > 1 else (8, LANES), F32)],
        compiler_params=_cparams(("parallel", "parallel", "arbitrary")),
        name=name,
    )(*args)


def _rmsnorm_fwd(x, w_row, *, name):
    t, d = x.shape
    tb = min(t, 1024)

    def body(x_ref, w_ref, o_ref):
        xf = x_ref[...]
        r = lax.rsqrt(jnp.mean(xf * xf, axis=-1, keepdims=True) + RMS_EPS)
        o_ref[...] = (xf * r * w_ref[...]).astype(o_ref.dtype)

    return pl.pallas_call(
        body,
        grid=(t // tb,),
        in_specs=[pl.BlockSpec((tb, d), lambda i: (i, 0)), pl.BlockSpec((1, d), lambda i: (0, 0))],
        out_specs=pl.BlockSpec((tb, d), lambda i: (i, 0)),
        out_shape=jax.ShapeDtypeStruct((t, d), MM_DTYPE),
        compiler_params=_cparams(("parallel",)),
        name=name,
    )(x, w_row)


def _rmsnorm_bwd(x, w_row, dy, dskip, *, name):
    t, d = x.shape
    tb = min(t, 512)

    def body(x_ref, w_ref, dy_ref, ds_ref, dx_ref, dxb_ref, dw_ref):
        xf = x_ref[...]
        g = dy_ref[...]
        r = lax.rsqrt(jnp.mean(xf * xf, axis=-1, keepdims=True) + RMS_EPS)
        gw = g * w_ref[...]
        proj = jnp.mean(gw * xf, axis=-1, keepdims=True)
        dx = r * gw - xf * (r * r * r * proj) + ds_ref[...]
        dx_ref[...] = dx
        dxb_ref[...] = dx.astype(dxb_ref.dtype)
        part = jnp.sum(g * xf * r, axis=0, keepdims=True)

        @pl.when(pl.program_id(0) == 0)
        def _():
            dw_ref[...] = part

        @pl.when(pl.program_id(0) > 0)
        def _():
            dw_ref[...] += part

    row = pl.BlockSpec((tb, d), lambda i: (i, 0))
    one = pl.BlockSpec((1, d), lambda i: (0, 0))
    return pl.pallas_call(
        body,
        grid=(t // tb,),
        in_specs=[row, one, row, row],
        out_specs=[row, row, one],
        out_shape=[jax.ShapeDtypeStruct((t, d), F32), jax.ShapeDtypeStruct((t, d), MM_DTYPE),
                   jax.ShapeDtypeStruct((1, d), F32)],
        compiler_params=_cparams(("arbitrary",)),
        name=name,
    )(x, w_row, dy, dskip)


def _silu(z):
    return z / (1.0 + jnp.exp(-z))


FFN_TM, FFN_TN = 512, 1408


def _ffn_in(hn, in_t, *, name):
    t, d = hn.shape
    h = FFN_HIDDEN
    tm, tn = min(t, FFN_TM), FFN_TN
    nj = h // tn
    dn = (((1,), (1,)), ((), ()))

    def body(a_ref, bg_ref, bu_ref, g_ref, u_ref, act_ref):
        a = a_ref[...]
        g = lax.dot_general(a, bg_ref[...], dn, preferred_element_type=F32)
        u = lax.dot_general(a, bu_ref[...], dn, preferred_element_type=F32)
        g_ref[...] = g
        u_ref[...] = u
        act_ref[...] = (_silu(g) * u).astype(act_ref.dtype)

    out = pl.BlockSpec((tm, tn), lambda i, j: (i, j))
    return pl.pallas_call(
        body,
        grid=(t // tm, nj),
        in_specs=[pl.BlockSpec((tm, d), lambda i, j: (i, 0)), pl.BlockSpec((tn, d), lambda i, j: (j, 0)),
                  pl.BlockSpec((tn, d), lambda i, j: (j + nj, 0))],
        out_specs=[out, out, out],
        out_shape=[jax.ShapeDtypeStruct((t, h), F32), jax.ShapeDtypeStruct((t, h), F32),
                   jax.ShapeDtypeStruct((t, h), MM_DTYPE)],
        compiler_params=_cparams(("parallel", "parallel")),
        name=name,
    )(hn, in_t, in_t)


def _ffn_dact(dy, out_w, g, u, *, name):
    t, d = dy.shape
    h = FFN_HIDDEN
    tm, tn = min(t, FFN_TM), FFN_TN

    def body(a_ref, b_ref, g_ref, u_ref, dg_ref, du_ref):
        da = lax.dot_general(a_ref[...], b_ref[...], (((1,), (1,)), ((), ())), preferred_element_type=F32)
        gate = g_ref[...]
        sig = 1.0 / (1.0 + jnp.exp(-gate))
        sg = gate * sig
        dg_ref[...] = (da * u_ref[...] * (sig + sg * (1.0 - sig))).astype(dg_ref.dtype)
        du_ref[...] = (da * sg).astype(du_ref.dtype)

    blk = pl.BlockSpec((tm, tn), lambda i, j: (i, j))
    return pl.pallas_call(
        body,
        grid=(t // tm, h // tn),
        in_specs=[pl.BlockSpec((tm, d), lambda i, j: (i, 0)), pl.BlockSpec((tn, d), lambda i, j: (j, 0)), blk, blk],
        out_specs=[blk, blk],
        out_shape=[jax.ShapeDtypeStruct((t, h), MM_DTYPE)] * 2,
        compiler_params=_cparams(("parallel", "parallel")),
        name=name,
    )(dy, out_w, g, u)


def _loss_head(y, target, *, name):
    t, d = y.shape
    tb = min(t, 1024)

    def body(y_ref, t_ref, dy_ref, dyb_ref, l_ref):
        err = y_ref[...] - t_ref[...]
        dy_ref[...] = err * (1.0 / d)
        dyb_ref[...] = (err * (1.0 / d)).astype(dyb_ref.dtype)
        part = jnp.sum(jnp.sum(err * err, axis=0, keepdims=True), axis=1, keepdims=True) * (0.5 / d)
        part = jnp.broadcast_to(part, l_ref.shape)

        @pl.when(pl.program_id(0) == 0)
        def _():
            l_ref[...] = part

        @pl.when(pl.program_id(0) > 0)
        def _():
            l_ref[...] += part

    row = pl.BlockSpec((tb, d), lambda i: (i, 0))
    return pl.pallas_call(
        body,
        grid=(t // tb,),
        in_specs=[row, row],
        out_specs=[row, row, pl.BlockSpec((8, LANES), lambda i: (0, 0))],
        out_shape=[jax.ShapeDtypeStruct((t, d), F32), jax.ShapeDtypeStruct((t, d), MM_DTYPE),
                   jax.ShapeDtypeStruct((8, LANES), F32)],
        compiler_params=_cparams(("arbitrary",)),
        name=name,
    )(y, target)


CONV_HALO = 8


def _conv_tile_scale(c):
    is_qk = c < 2 * GDN_HEADS
    scale = jnp.where(c < GDN_HEADS, GDN_DK ** -0.5, 1.0).astype(F32)
    return is_qk, scale


def _gdn_conv_fwd(pm, conv_w, *, name):
    t = pm.shape[0]
    tb = min(t, 1024)
    nt = t // tb
    hb = tb // CONV_HALO

    def body(x_ref, xp_ref, w_ref, o_ref):
        c = pl.program_id(0)
        ti = pl.program_id(1)
        prev = jnp.where(ti > 0, xp_ref[...], 0.0)
        xe = jnp.concatenate([prev, x_ref[...]], axis=0)
        w = w_ref[...]
        y = jnp.zeros((tb, LANES), F32)
        for j in range(GDN_CONV):
            off = CONV_HALO - (GDN_CONV - 1) + j
            y = y + w[j:j + 1, :] * xe[off:off + tb, :]
        s = _silu(y)
        is_qk, scale = _conv_tile_scale(c)
        r = lax.rsqrt(jnp.sum(s * s, axis=-1, keepdims=True) + L2_EPS) * scale
        o_ref[...] = s * jnp.where(is_qk, r, 1.0)

    return pl.pallas_call(
        body,
        grid=(GDN_QKV // LANES, nt),
        in_specs=[
            pl.BlockSpec((tb, LANES), lambda c, i: (i, c)),
            pl.BlockSpec((CONV_HALO, LANES), lambda c, i: (jnp.maximum(i * hb - 1, 0), c)),
            pl.BlockSpec((GDN_CONV, LANES), lambda c, i: (0, c)),
        ],
        out_specs=pl.BlockSpec((tb, LANES), lambda c, i: (i, c)),
        out_shape=jax.ShapeDtypeStruct((t, GDN_QKV), F32),
        compiler_params=_cparams(("parallel", "parallel")),
        name=name,
    )(pm, pm, conv_w)


def _gdn_conv_bwd(pm, conv_w, dout, *, name):
    t = pm.shape[0]
    tb = min(t, 1024)
    nt = t // tb
    hb = tb // CONV_HALO
    last_hb = t // CONV_HALO - 1
    ext = tb + CONV_HALO

    def body(x_ref, xp_ref, xn_ref, d_ref, dn_ref, w_ref, dx_ref, dw_ref):
        c = pl.program_id(0)
        ti = pl.program_id(1)
        prev = jnp.where(ti > 0, xp_ref[...], 0.0)
        has_next = ti < nt - 1
        nxt = jnp.where(has_next, xn_ref[...], 0.0)
        xe = jnp.concatenate([prev, x_ref[...], nxt], axis=0)
        de = jnp.concatenate([d_ref[...], jnp.where(has_next, dn_ref[...], 0.0)], axis=0)
        w = w_ref[...]
        y = jnp.zeros((ext, LANES), F32)
        for j in range(GDN_CONV):
            off = CONV_HALO - (GDN_CONV - 1) + j
            y = y + w[j:j + 1, :] * xe[off:off + ext, :]
        sig = 1.0 / (1.0 + jnp.exp(-y))
        s = y * sig
        is_qk, scale = _conv_tile_scale(c)
        r = lax.rsqrt(jnp.sum(s * s, axis=-1, keepdims=True) + L2_EPS)
        n = s * r
        dnrm = de * scale
        ds_qk = r * (dnrm - n * jnp.sum(dnrm * n, axis=-1, keepdims=True))
        ds = jnp.where(is_qk, ds_qk, de)
        dy = ds * (sig + s * (1.0 - sig))
        dx = jnp.zeros((tb, LANES), F32)
        dw_rows = []
        for j in range(GDN_CONV):
            sh = GDN_CONV - 1 - j
            dx = dx + w[j:j + 1, :] * dy[sh:sh + tb, :]
            off = CONV_HALO - (GDN_CONV - 1) + j
            dw_rows.append(jnp.sum(dy[:tb, :] * xe[off:off + tb, :], axis=0, keepdims=True))
        dx_ref[...] = dx.astype(dx_ref.dtype)
        part = jnp.concatenate(dw_rows, axis=0)

        @pl.when(ti == 0)
        def _():
            dw_ref[...] = part

        @pl.when(ti > 0)
        def _():
            dw_ref[...] += part

    main = pl.BlockSpec((tb, LANES), lambda c, i: (i, c))
    prev = pl.BlockSpec((CONV_HALO, LANES), lambda c, i: (jnp.maximum(i * hb - 1, 0), c))
    nxt = pl.BlockSpec((CONV_HALO, LANES), lambda c, i: (jnp.minimum((i + 1) * hb, last_hb), c))
    return pl.pallas_call(
        body,
        grid=(GDN_QKV // LANES, nt),
        in_specs=[main, prev, nxt, main, nxt, pl.BlockSpec((GDN_CONV, LANES), lambda c, i: (0, c))],
        out_specs=[main, pl.BlockSpec((GDN_CONV, LANES), lambda c, i: (0, c))],
        out_shape=[jax.ShapeDtypeStruct((t, GDN_QKV), MM_DTYPE), jax.ShapeDtypeStruct((GDN_CONV, GDN_QKV), F32)],
        compiler_params=_cparams(("parallel", "arbitrary")),
        name=name,
    )(pm, pm, pm, dout, dout, conv_w)


def _head_selector(first_col):
    row = lax.broadcasted_iota(jnp.int32, (LANES, GDN_HEADS * LANES), 0)
    col = lax.broadcasted_iota(jnp.int32, (LANES, GDN_HEADS * LANES), 1)
    return (col // LANES + first_col == row).astype(F32)


def _softplus(x):
    return jnp.maximum(x, 0.0) + jnp.log(1.0 + jnp.exp(-jnp.abs(x)))


def _gdn_gates_fwd(ab, alog_row, dt_row, *, name):
    t = ab.shape[0]
    tb = min(t, 1024)
    wide = GDN_HEADS * LANES

    def body(ab_ref, al_ref, dt_ref, g_ref, b_ref):
        x = ab_ref[...]
        g_cols = -jnp.exp(al_ref[...]) * _softplus(x + dt_ref[...])
        b_cols = 1.0 / (1.0 + jnp.exp(-x))
        g_ref[...] = _dot(g_cols, _head_selector(0))
        b_ref[...] = _dot(b_cols, _head_selector(GDN_HEADS))

    row = pl.BlockSpec((tb, LANES), lambda i: (i, 0))
    one = pl.BlockSpec((1, LANES), lambda i: (0, 0))
    out = pl.BlockSpec((tb, wide), lambda i: (i, 0))
    return pl.pallas_call(
        body,
        grid=(t // tb,),
        in_specs=[row, one, one],
        out_specs=[out, out],
        out_shape=[jax.ShapeDtypeStruct((t, wide), F32)] * 2,
        compiler_params=_cparams(("parallel",)),
        name=name,
    )(ab, alog_row, dt_row)


def _gdn_gates_bwd(ab, alog_row, dt_row, dgb, dbb, *, name):
    t = ab.shape[0]
    tb = min(t, 1024)
    wide = GDN_HEADS * LANES

    def body(ab_ref, al_ref, dt_ref, dg_ref, db_ref, dab_ref, dal_ref, ddt_ref):
        x = ab_ref[...]
        lane = lax.broadcasted_iota(jnp.int32, (tb, LANES), 1)
        dg_cols = _dot_nt(dg_ref[...], _head_selector(0))
        db_cols = _dot_nt(db_ref[...], _head_selector(GDN_HEADS))
        ea = jnp.exp(al_ref[...])
        z = x + dt_ref[...]
        sp = _softplus(z)
        sg = 1.0 / (1.0 + jnp.exp(-z))
        beta = 1.0 / (1.0 + jnp.exp(-x))
        da = jnp.where(lane < GDN_HEADS, dg_cols * (-ea) * sg, 0.0)
        db = jnp.where((lane >= GDN_HEADS) & (lane < 2 * GDN_HEADS), db_cols * beta * (1.0 - beta), 0.0)
        dab_ref[...] = (da + db).astype(dab_ref.dtype)
        p_al = jnp.sum(jnp.where(lane < GDN_HEADS, dg_cols * (-ea) * sp, 0.0), axis=0, keepdims=True)
        p_dt = jnp.sum(da, axis=0, keepdims=True)

        @pl.when(pl.program_id(0) == 0)
        def _():
            dal_ref[...] = p_al
            ddt_ref[...] = p_dt

        @pl.when(pl.program_id(0) > 0)
        def _():
            dal_ref[...] += p_al
            ddt_ref[...] += p_dt

    row = pl.BlockSpec((tb, LANES), lambda i: (i, 0))
    one = pl.BlockSpec((1, LANES), lambda i: (0, 0))
    big = pl.BlockSpec((tb, wide), lambda i: (i, 0))
    return pl.pallas_call(
        body,
        grid=(t // tb,),
        in_specs=[row, one, one, big, big],
        out_specs=[row, one, one],
        out_shape=[jax.ShapeDtypeStruct((t, LANES), MM_DTYPE), jax.ShapeDtypeStruct((1, LANES), F32),
                   jax.ShapeDtypeStruct((1, LANES), F32)],
        compiler_params=_cparams(("arbitrary",)),
        name=name,
    )(ab, alog_row, dt_row, dgb, dbb)


@jax.custom_vjp
def _unit_lower_inverse_rest(n):
    c = n.shape[-1]
    ri = lax.broadcasted_iota(jnp.int32, (c, c), 0)
    ci = lax.broadcasted_iota(jnp.int32, (c, c), 1)
    rest = None
    size = 1
    while size < c:
        joins = ((ri // (2 * size)) == (ci // (2 * size))) & ((ri // size) != (ci // size))
        low = jnp.where(joins, n, 0.0)
        if rest is None:
            rest = -low
        else:
            left = low + _bdot(rest, low)
            rest = rest - (left + _bdot(left, rest))
        size *= 2
    return rest


def _unit_lower_inverse_rest_fwd(n):
    rest = _unit_lower_inverse_rest(n)
    return rest, rest


def _unit_lower_inverse_rest_bwd(rest, ct):
    left = ct + _bdot_tn(rest, ct)
    return (-(left + _bdot_nt(left, rest)),)


_unit_lower_inverse_rest.defvjp(_unit_lower_inverse_rest_fwd, _unit_lower_inverse_rest_bwd)


def _bf16_pieces(x):
    hi = x.astype(BF16)
    r1 = x - hi.astype(F32)
    mid = r1.astype(BF16)
    lo = (r1 - mid.astype(F32)).astype(BF16)
    return hi, mid, lo


def _lower_ones(shape):
    c = shape[-1]
    ri = lax.broadcasted_iota(jnp.int32, (c, c), 0)
    ci = lax.broadcasted_iota(jnp.int32, (c, c), 1)
    return jnp.broadcast_to((ri >= ci).astype(BF16), shape)


@jax.custom_vjp
def _running_sum(x):
    tri = _lower_ones(x.shape)
    return sum(_bdot(tri, p) for p in _bf16_pieces(x))


def _running_sum_fwd(x):
    return _running_sum(x), None


def _running_sum_bwd(_, ct):
    tri = _lower_ones(ct.shape)
    return (sum(_bdot_tn(tri, p) for p in _bf16_pieces(ct)),)


_running_sum.defvjp(_running_sum_fwd, _running_sum_bwd)


def _gdn_prep_math(q, k, v, gb, bb):
    c = GDN_CHUNK
    ri = lax.broadcasted_iota(jnp.int32, (c, c), 0)
    ci = lax.broadcasted_iota(jnp.int32, (c, c), 1)
    causal = ri >= ci
    gc = _running_sum(gb)
    decay = jnp.exp(jnp.where(causal, gc - jnp.swapaxes(gc, -1, -2), NEG))
    n = jnp.where(ri > ci, _bdot_nt(k, k) * bb * decay, 0.0)
    rest = _unit_lower_inverse_rest(n)
    eg = jnp.exp(gc)
    rhs_v = v * bb
    rhs_k = k * bb * eg
    u = rhs_v + _bdot(rest, rhs_v)
    w = rhs_k + _bdot(rest, rhs_k)
    qk = _bdot_nt(q, k) * decay
    qd = q * eg
    last = jnp.sum(jnp.where(ri == c - 1, gc, 0.0), axis=-2, keepdims=True)
    gl = jnp.broadcast_to(last, gc.shape)
    kt = k * jnp.exp(gl - gc)
    cd = jnp.exp(gl)
    return u, w, qk, qd, kt, cd


def _head_tiles(ref, h):
    return ref[:, h * LANES:(h + 1) * LANES]


def _stack_heads(ref, first=0, heads=GDN_HEADS):
    return jnp.stack([_head_tiles(ref, first + h) for h in range(heads)])


def _store_heads(ref, val, first=0):
    for h in range(val.shape[0]):
        ref[:, (first + h) * LANES:(first + h + 1) * LANES] = val[h].astype(ref.dtype)


def _gdn_prep_fwd(qkv, gb, bb, *, name):
    t = qkv.shape[0]
    c = GDN_CHUNK
    wide = GDN_HEADS * LANES

    def body(q_ref, k_ref, v_ref, g_ref, b_ref, *outs):
        res = _gdn_prep_math(*(_stack_heads(r) for r in (q_ref, k_ref, v_ref, g_ref, b_ref)))
        for o_ref, val in zip(outs, res):
            _store_heads(o_ref, val)

    blk = lambda off: pl.BlockSpec((c, wide), lambda i: (i, off))
    return pl.pallas_call(
        body,
        grid=(t // c,),
        in_specs=[blk(0), blk(1), blk(2), blk(0), blk(0)],
        out_specs=[blk(0)] * 6,
        out_shape=[jax.ShapeDtypeStruct((t, wide), F32)] * 6,
        compiler_params=_cparams(("parallel",)),
        name=name,
    )(qkv, qkv, qkv, gb, bb)


def _gdn_prep_bwd(qkv, gb, bb, cts, *, name):
    t = qkv.shape[0]
    c = GDN_CHUNK
    wide = GDN_HEADS * LANES

    def body(q_ref, k_ref, v_ref, g_ref, b_ref, c0, c1, c2, c3, c4, c5, dqkv_ref, dg_ref, db_ref):
        prim = tuple(_stack_heads(r) for r in (q_ref, k_ref, v_ref, g_ref, b_ref))
        _, pull = jax.vjp(_gdn_prep_math, *prim)
        dq, dk, dv, dg, db = pull(tuple(_stack_heads(r) for r in (c0, c1, c2, c3, c4, c5)))
        _store_heads(dqkv_ref, dq)
        _store_heads(dqkv_ref, dk, first=GDN_HEADS)
        _store_heads(dqkv_ref, dv, first=2 * GDN_HEADS)
        _store_heads(dg_ref, dg)
        _store_heads(db_ref, db)

    blk = lambda off: pl.BlockSpec((c, wide), lambda i: (i, off))
    return pl.pallas_call(
        body,
        grid=(t // c,),
        in_specs=[blk(0), blk(1), blk(2), blk(0), blk(0)] + [blk(0)] * 6,
        out_specs=[pl.BlockSpec((c, 3 * wide), lambda i: (i, 0)), blk(0), blk(0)],
        out_shape=[jax.ShapeDtypeStruct((t, 3 * wide), F32), jax.ShapeDtypeStruct((t, wide), F32),
                   jax.ShapeDtypeStruct((t, wide), F32)],
        compiler_params=_cparams(("parallel",)),
        name=name,
    )(qkv, qkv, qkv, gb, bb, *cts)


def _gdn_scan_math(s, u, w, qk, qd, kt, cd):
    v_new = u - _bdot(w, s)
    o = _bdot(qd, s) + _bdot(qk, v_new)
    s_new = s * cd + _bdot_tn(kt, v_new)
    return o, s_new


def _gdn_scan_fwd(prep, *, name):
    t = prep[0].shape[0]
    c = GDN_CHUNK
    wide = GDN_HEADS * LANES

    def body(u_ref, w_ref, qk_ref, qd_ref, kt_ref, cd_ref, o_ref, st_ref, s_ref):
        @pl.when(pl.program_id(0) == 0)
        def _():
            s_ref[...] = jnp.zeros_like(s_ref)

        s = _stack_heads(s_ref)
        _store_heads(st_ref, s)
        o, s_new = _gdn_scan_math(s, *(_stack_heads(r) for r in (u_ref, w_ref, qk_ref, qd_ref, kt_ref, cd_ref)))
        _store_heads(o_ref, o)
        _store_heads(s_ref, s_new)

    blk = pl.BlockSpec((c, wide), lambda i: (i, 0))
    return pl.pallas_call(
        body,
        grid=(t // c,),
        in_specs=[blk] * 6,
        out_specs=[blk, blk],
        out_shape=[jax.ShapeDtypeStruct((t, wide), F32)] * 2,
        scratch_shapes=[pltpu.VMEM((GDN_DK, wide), F32)],
        compiler_params=_cparams(("arbitrary",)),
        name=name,
    )(*prep)


def _gdn_scan_bwd(prep, states, do, *, name):
    t = do.shape[0]
    c = GDN_CHUNK
    wide = GDN_HEADS * LANES
    nc = t // c

    def body(u_ref, w_ref, qk_ref, qd_ref, kt_ref, cd_ref, st_ref, do_ref, *rest):
        outs, ds_ref = rest[:6], rest[6]

        @pl.when(pl.program_id(0) == 0)
        def _():
            ds_ref[...] = jnp.zeros_like(ds_ref)

        prim = tuple(_stack_heads(r) for r in (st_ref, u_ref, w_ref, qk_ref, qd_ref, kt_ref, cd_ref))
        _, pull = jax.vjp(_gdn_scan_math, *prim)
        grads = pull((_stack_heads(do_ref), _stack_heads(ds_ref)))
        _store_heads(ds_ref, grads[0])
        for o_ref, val in zip(outs, grads[1:]):
            _store_heads(o_ref, val)

    blk = pl.BlockSpec((c, wide), lambda i: (nc - 1 - i, 0))
    return pl.pallas_call(
        body,
        grid=(nc,),
        in_specs=[blk] * 8,
        out_specs=[blk] * 6,
        out_shape=[jax.ShapeDtypeStruct((t, wide), F32)] * 6,
        scratch_shapes=[pltpu.VMEM((GDN_DK, wide), F32)],
        compiler_params=_cparams(("arbitrary",)),
        name=name,
    )(*prep, states, do)


def _gdn_outgate_math(o, z, nw):
    r = lax.rsqrt(jnp.mean(o * o, axis=-1, keepdims=True) + RMS_EPS)
    return o * r * nw * _silu(z)


def _gdn_outgate_fwd(o, pm, nw_row, *, name):
    t = o.shape[0]
    tb = min(t, 1024)
    z_off = GDN_QKV // LANES

    def body(o_ref, z_ref, nw_ref, y_ref):
        y_ref[...] = _gdn_outgate_math(o_ref[...], z_ref[...], nw_ref[...]).astype(y_ref.dtype)

    return pl.pallas_call(
        body,
        grid=(t // tb, GDN_HEADS),
        in_specs=[pl.BlockSpec((tb, LANES), lambda i, h: (i, h)), pl.BlockSpec((tb, LANES), lambda i, h: (i, h + z_off)),
                  pl.BlockSpec((1, LANES), lambda i, h: (0, 0))],
        out_specs=pl.BlockSpec((tb, LANES), lambda i, h: (i, h)),
        out_shape=jax.ShapeDtypeStruct((t, GDN_HEADS * LANES), MM_DTYPE),
        compiler_params=_cparams(("parallel", "parallel")),
        name=name,
    )(o, pm, nw_row)


def _gdn_outgate_bwd(o, pm, nw_row, dy, *, name):
    t = o.shape[0]
    tb = min(t, 1024)
    z_off = GDN_QKV // LANES

    def body(o_ref, z_ref, nw_ref, dy_ref, do_ref, dz_ref, dnw_ref):
        _, pull = jax.vjp(_gdn_outgate_math, o_ref[...], z_ref[...], nw_ref[...])
        d_o, d_z, d_nw = pull(dy_ref[...])
        do_ref[...] = d_o
        dz_ref[...] = d_z.astype(dz_ref.dtype)
        first = (pl.program_id(0) == 0) & (pl.program_id(1) == 0)

        @pl.when(first)
        def _():
            dnw_ref[...] = d_nw

        @pl.when(jnp.logical_not(first))
        def _():
            dnw_ref[...] += d_nw

    blk = pl.BlockSpec((tb, LANES), lambda i, h: (i, h))
    one = pl.BlockSpec((1, LANES), lambda i, h: (0, 0))
    return pl.pallas_call(
        body,
        grid=(t // tb, GDN_HEADS),
        in_specs=[blk, pl.BlockSpec((tb, LANES), lambda i, h: (i, h + z_off)), one, blk],
        out_specs=[blk, blk, one],
        out_shape=[jax.ShapeDtypeStruct((t, GDN_HEADS * LANES), F32),
                   jax.ShapeDtypeStruct((t, GDN_HEADS * LANES), MM_DTYPE), jax.ShapeDtypeStruct((1, LANES), F32)],
        compiler_params=_cparams(("arbitrary", "arbitrary")),
        name=name,
    )(o, pm, nw_row, dy)


def _rms64(x, w_row):
    return x * lax.rsqrt(jnp.sum(x * x, axis=-1, keepdims=True) * (1.0 / DIL_DH) + RMS_EPS) * w_row


def _alibi_slopes(group):
    head = lax.broadcasted_iota(jnp.int32, (DIL_HEADS, 8, LANES), 0).astype(F32)
    rate = -math.log(2.0) * ALIBI_MAX_BIAS / (len(DIL_GROUPS) * DIL_HEADS)
    slope = jnp.exp(rate * (head + float(group * DIL_HEADS + 1)))
    return jnp.broadcast_to(slope[:, 0:1, :], (DIL_HEADS, DIL_SPAN, LANES))


def _band_logits(qn, kp, kc, slope_d, has_prev):
    qi = lax.broadcasted_iota(jnp.int32, (DIL_SPAN, DIL_SPAN), 0)
    kj = lax.broadcasted_iota(jnp.int32, (DIL_SPAN, DIL_SPAN), 1)
    steps_c = (qi - kj).astype(F32)
    scale = DIL_DH ** -0.5
    sp = _bdot_nt(qn, kp) * scale - slope_d * (steps_c + float(DIL_SPAN))
    sc = _bdot_nt(qn, kc) * scale - slope_d * steps_c
    sp = jnp.where((kj >= qi) & has_prev, sp, NEG)
    sc = jnp.where(kj <= qi, sc, NEG)
    return sp, sc


def _dil_attn_fwd(slab, wq_row, wk_row, *, group, name):
    dilation = DIL_GROUPS[group][1]
    t = slab.shape[0]
    rows = t // dilation
    nlb = rows // DIL_SPAN
    wide = DIL_HEADS * LANES
    view = slab.reshape(rows, dilation * DIL_SLAB)

    def body(q_ref, kc_ref, vc_ref, kp_ref, vp_ref, wq_ref, wk_ref, o_ref):
        has_prev = pl.program_id(1) > 0
        lane = lax.broadcasted_iota(jnp.int32, (DIL_SPAN, LANES), 1)
        qn = _rms64(_stack_heads(q_ref), wq_ref[...])
        kc = _rms64(_stack_heads(kc_ref), wk_ref[...])
        kp = _rms64(_stack_heads(kp_ref), wk_ref[...])
        sp, sc = _band_logits(qn, kp, kc, _alibi_slopes(group) * float(dilation), has_prev)
        m = jnp.maximum(jnp.max(sp, axis=-1, keepdims=True), jnp.max(sc, axis=-1, keepdims=True))
        pp = jnp.exp(sp - m)
        pc = jnp.exp(sc - m)
        l = jnp.sum(pp, axis=-1, keepdims=True) + jnp.sum(pc, axis=-1, keepdims=True)
        o = (_bdot(pp, _stack_heads(vp_ref)) + _bdot(pc, _stack_heads(vc_ref))) / l
        _store_heads(o_ref, jnp.where(lane < DIL_DH, o, m + jnp.log(l)))

    cur = lambda part: pl.BlockSpec((DIL_SPAN, wide), lambda r, i: (i, 3 * r + part))
    prv = lambda part: pl.BlockSpec((DIL_SPAN, wide), lambda r, i: (jnp.maximum(i - 1, 0), 3 * r + part))
    one = pl.BlockSpec((1, LANES), lambda r, i: (0, 0))
    out = pl.pallas_call(
        body,
        grid=(dilation, nlb),
        in_specs=[cur(0), cur(1), cur(2), prv(1), prv(2), one, one],
        out_specs=pl.BlockSpec((DIL_SPAN, wide), lambda r, i: (i, r)),
        out_shape=jax.ShapeDtypeStruct((rows, dilation * wide), F32),
        compiler_params=_cparams(("parallel", "parallel")),
        name=name,
    )(view, view, view, view, view, wq_row, wk_row)
    return out.reshape(t, wide)


def _dil_merge_fwd(oe, *, name):
    t = oe[0].shape[0]
    tb = min(t, 1024)

    def body(e0, e1, e2, y_ref, om_ref):
        lane = lax.broadcasted_iota(jnp.int32, (tb, LANES), 1)
        es = [e0[...], e1[...], e2[...]]
        lse = [jnp.sum(jnp.where(lane == DIL_DH, e, 0.0), axis=-1, keepdims=True) for e in es]
        top = jnp.maximum(jnp.maximum(lse[0], lse[1]), lse[2])
        joint = top + jnp.log(jnp.exp(lse[0] - top) + jnp.exp(lse[1] - top) + jnp.exp(lse[2] - top))
        o = sum(jnp.exp(l - joint) * e for l, e in zip(lse, es))
        y_ref[...] = jnp.where(lane < DIL_DH, o, 0.0).astype(y_ref.dtype)
        om_ref[...] = jnp.where(lane < DIL_DH, o, joint)

    blk = pl.BlockSpec((tb, LANES), lambda i, h: (i, h))
    return pl.pallas_call(
        body,
        grid=(t // tb, DIL_HEADS),
        in_specs=[blk] * 3,
        out_specs=[blk, blk],
        out_shape=[jax.ShapeDtypeStruct((t, DIL_HEADS * LANES), MM_DTYPE),
                   jax.ShapeDtypeStruct((t, DIL_HEADS * LANES), F32)],
        compiler_params=_cparams(("parallel", "parallel")),
        name=name,
    )(*oe)


def _dil_merge_bwd(dy, om, *, name):
    t = dy.shape[0]
    tb = min(t, 1024)

    def body(dy_ref, om_ref, st_ref):
        lane = lax.broadcasted_iota(jnp.int32, (tb, LANES), 1)
        d_o = jnp.where(lane < DIL_DH, dy_ref[...], 0.0)
        om_t = om_ref[...]
        delta = jnp.sum(d_o * om_t, axis=-1, keepdims=True)
        st_ref[...] = jnp.where(lane < DIL_DH, d_o, jnp.where(lane == DIL_DH, om_t, jnp.where(lane == DIL_DH + 1, delta, 0.0)))

    blk = pl.BlockSpec((tb, LANES), lambda i, h: (i, h))
    return pl.pallas_call(
        body,
        grid=(t // tb, DIL_HEADS),
        in_specs=[blk, blk],
        out_specs=blk,
        out_shape=jax.ShapeDtypeStruct((t, DIL_HEADS * LANES), F32),
        compiler_params=_cparams(("parallel", "parallel")),
        name=name,
    )(dy, om)


def _rms64_bwd(x, w_row, dy):
    r = lax.rsqrt(jnp.sum(x * x, axis=-1, keepdims=True) * (1.0 / DIL_DH) + RMS_EPS)
    gw = dy * w_row
    dx = r * gw - x * (r * r * r * jnp.sum(gw * x, axis=-1, keepdims=True) * (1.0 / DIL_DH))
    return dx, dy * x * r


def _dil_attn_bwd(slab, stat, wq_row, wk_row, dwq_in, dwk_in, *, group, name):
    dilation = DIL_GROUPS[group][1]
    t = slab.shape[0]
    rows = t // dilation
    nlb = rows // DIL_SPAN
    wide = DIL_HEADS * LANES
    view = slab.reshape(rows, dilation * DIL_SLAB)
    stat_view = stat.reshape(rows, dilation * wide)

    def body(cur_ref, kp_ref, vp_ref, st_ref, wq_ref, wk_ref, dwq_in_ref, dwk_in_ref, d_ref, dwq_ref, dwk_ref,
             dk_carry, dv_carry):
        step = pl.program_id(1)
        has_prev = step < nlb - 1
        first = (pl.program_id(0) == 0) & (step == 0)

        @pl.when(step == 0)
        def _():
            dk_carry[...] = jnp.zeros_like(dk_carry)
            dv_carry[...] = jnp.zeros_like(dv_carry)

        @pl.when(first)
        def _():
            dwq_ref[...] = dwq_in_ref[...]
            dwk_ref[...] = dwk_in_ref[...]

        lane = lax.broadcasted_iota(jnp.int32, (DIL_SPAN, LANES), 1)
        scale = DIL_DH ** -0.5
        q_raw = _stack_heads(cur_ref)
        kc_raw = _stack_heads(cur_ref, first=DIL_HEADS)
        vc = _stack_heads(cur_ref, first=2 * DIL_HEADS)
        kp_raw = _stack_heads(kp_ref)
        vp = _stack_heads(vp_ref)
        st = _stack_heads(st_ref)
        d_o = jnp.where(lane < DIL_DH, st, 0.0)
        lse = jnp.sum(jnp.where(lane == DIL_DH, st, 0.0), axis=-1, keepdims=True)
        delta = jnp.sum(jnp.where(lane == DIL_DH + 1, st, 0.0), axis=-1, keepdims=True)
        qn = _rms64(q_raw, wq_ref[...])
        kc = _rms64(kc_raw, wk_ref[...])
        kp = _rms64(kp_raw, wk_ref[...])
        sp, sc = _band_logits(qn, kp, kc, _alibi_slopes(group) * float(dilation), has_prev)
        pp = jnp.exp(sp - lse)
        pc = jnp.exp(sc - lse)
        dsp = pp * (_bdot_nt(d_o, vp) - delta) * scale
        dsc = pc * (_bdot_nt(d_o, vc) - delta) * scale
        dqn = _bdot(dsp, kp) + _bdot(dsc, kc)
        dkc_n = _bdot_tn(dsc, qn) + _stack_heads(dk_carry)
        dvc = _bdot_tn(pc, d_o) + _stack_heads(dv_carry)
        _store_heads(dk_carry, _bdot_tn(dsp, qn))
        _store_heads(dv_carry, _bdot_tn(pp, d_o))
        dq_raw, dwq_rows = _rms64_bwd(q_raw, wq_ref[...], dqn)
        dk_raw, dwk_rows = _rms64_bwd(kc_raw, wk_ref[...], dkc_n)
        _store_heads(d_ref, dq_raw)
        _store_heads(d_ref, dk_raw, first=DIL_HEADS)
        _store_heads(d_ref, dvc, first=2 * DIL_HEADS)
        dwq_ref[...] += jnp.sum(jnp.sum(dwq_rows, axis=0), axis=0, keepdims=True)
        dwk_ref[...] += jnp.sum(jnp.sum(dwk_rows, axis=0), axis=0, keepdims=True)

    blk_i = lambda i: nlb - 1 - i
    cur = pl.BlockSpec((DIL_SPAN, DIL_SLAB), lambda r, i: (blk_i(i), r))
    prv = lambda part: pl.BlockSpec((DIL_SPAN, wide), lambda r, i: (jnp.maximum(blk_i(i) - 1, 0), 3 * r + part))
    one = pl.BlockSpec((1, LANES), lambda r, i: (0, 0))
    dslab, dwq, dwk = pl.pallas_call(
        body,
        grid=(dilation, nlb),
        in_specs=[cur, prv(1), prv(2), pl.BlockSpec((DIL_SPAN, wide), lambda r, i: (blk_i(i), r)), one, one, one, one],
        out_specs=[cur, one, one],
        out_shape=[jax.ShapeDtypeStruct((rows, dilation * DIL_SLAB), MM_DTYPE), jax.ShapeDtypeStruct((1, LANES), F32),
                   jax.ShapeDtypeStruct((1, LANES), F32)],
        scratch_shapes=[pltpu.VMEM((DIL_SPAN, wide), F32), pltpu.VMEM((DIL_SPAN, wide), F32)],
        compiler_params=_cparams(("arbitrary", "arbitrary")),
        name=name,
    )(view, view, view, stat_view, wq_row, wk_row, dwq_in, dwk_in)
    return dslab.reshape(t, DIL_SLAB), dwq, dwk


def _row(v, width=LANES):
    v = v.astype(F32).reshape(-1)
    return jnp.pad(v, (0, width - v.shape[0])).reshape(1, width)


def _prepare_weights(w):
    return dict(gdn=_prepare_gdn(w), dil=_prepare_dil(w), ffn=_prepare_ffn(w))


def _prepare_gdn(w, layers=range(DEPTH // 2)):
    gdn = {}
    for j in layers:
        wt = w["gdn_w_in"][j]
        gates_t = jnp.pad(wt[GDN_MAIN:], ((0, LANES - 2 * GDN_HEADS), (0, 0)))
        gdn[j] = dict(in_t=wt, gates_t=gates_t, out=w["gdn_w_out"][j], conv=w["gdn_conv_w"][j].astype(F32),
                      alog=_row(w["gdn_a_log"][j]), dt=_row(w["gdn_dt_bias"][j]), nw=_row(w["gdn_norm_w"][j]))
    return gdn


def _prepare_dil(w):
    d = D_MODEL
    dil = []
    for j in range(DEPTH // 2):
        wt = w["dil_w_in"][j].reshape(3, len(DIL_GROUPS), DIL_HEADS, DIL_DH, d)
        wg_t = [jnp.pad(wt[:, g], ((0, 0), (0, 0), (0, LANES - DIL_DH), (0, 0))).reshape(DIL_SLAB, d)
                for g in range(len(DIL_GROUPS))]
        out_t = jnp.pad(w["dil_w_out"][j].reshape(d, DIL_HEADS, DIL_DH), ((0, 0), (0, 0), (0, LANES - DIL_DH)))
        dil.append(dict(wg_t=wg_t, out_t=out_t.reshape(d, DIL_HEADS * LANES), wq=_row(w["dil_q_norm"][j]),
                        wk=_row(w["dil_k_norm"][j])))
    return dil


def _prepare_ffn(w):
    return [dict(in_t=w["ffn_w_in"][i], out=w["ffn_w_out"][i]) for i in range(DEPTH)]


def _gdn_layer_fwd(x, nrow, p):
    hn = _rmsnorm_fwd(x, nrow, name="rmsnorm_fwd")
    pm = _matmul(hn, p["in_t"], trans_b=True, b_rows=(0, GDN_MAIN), name="gdn_proj_main")
    ab = _matmul(hn, p["gates_t"], trans_b=True, name="gdn_proj_gates")
    qkv = _gdn_conv_fwd(pm, p["conv"], name="gdn_conv_fwd")
    gb, bb = _gdn_gates_fwd(ab, p["alog"], p["dt"], name="gdn_gates_fwd")
    prep = _gdn_prep_fwd(qkv, gb, bb, name="gdn_prep_fwd")
    o, states = _gdn_scan_fwd(prep, name="gdn_scan_fwd")
    og = _gdn_outgate_fwd(o, pm, p["nw"], name="gdn_outgate_fwd")
    y = _matmul(og, p["out"], add=x, name="gdn_proj_out")
    return y, (x, hn, pm, ab, qkv, gb, bb, prep, states, o, og)


def _gdn_layer_bwd(dx, dxb, nrow, p, saved):
    x, hn, pm, ab, qkv, gb, bb, prep, states, o, og = saved
    d_og = _matmul(dxb, p["out"], trans_b=True, name="gdn_dgate")
    g_out = _matmul(og, dxb, trans_a=True, name="gdn_gw_out")
    d_o, d_z, d_nw = _gdn_outgate_bwd(o, pm, p["nw"], d_og, name="gdn_outgate_bwd")
    cts = _gdn_scan_bwd(prep, states, d_o, name="gdn_scan_bwd")
    dqkv, dgb, dbb = _gdn_prep_bwd(qkv, gb, bb, cts, name="gdn_prep_bwd")
    d_ab, d_alog, d_dt = _gdn_gates_bwd(ab, p["alog"], p["dt"], dgb, dbb, name="gdn_gates_bwd")
    d_conv, g_conv = _gdn_conv_bwd(pm, p["conv"], dqkv, name="gdn_conv_bwd")
    d_hn = _matmul(d_conv, p["in_t"], b_rows=(0, GDN_QKV), name="gdn_dhn_qkv")
    d_hn = _matmul(d_z, p["in_t"], b_rows=(GDN_QKV, GDN_MAIN - GDN_QKV), add=d_hn, name="gdn_dhn_z")
    d_hn = _matmul(d_ab, p["gates_t"], add=d_hn, name="gdn_dhn_gates")
    g_in_t = jnp.concatenate([
        _matmul(d_conv, hn, trans_a=True, name="gdn_gw_qkv"),
        _matmul(d_z, hn, trans_a=True, name="gdn_gw_z"),
        _matmul(d_ab, hn, trans_a=True, name="gdn_gw_gates")[:2 * GDN_HEADS],
    ], axis=0)
    dx_new, dxb_new, g_norm = _rmsnorm_bwd(x, nrow, d_hn, dx, name="rmsnorm_bwd")
    grads = dict(w_in=g_in_t, conv=g_conv, a_log=d_alog[0, :GDN_HEADS], dt_bias=d_dt[0, :GDN_HEADS], norm_w=d_nw[0],
                 w_out=g_out, norm=g_norm[0])
    return dx_new, dxb_new, grads


def _dil_layer_fwd(x, nrow, p):
    hn = _rmsnorm_fwd(x, nrow, name="rmsnorm_fwd")
    slabs = [_matmul(hn, p["wg_t"][g], trans_b=True, name="dil_proj_in") for g in range(len(DIL_GROUPS))]
    oe = [_dil_attn_fwd(slabs[g], p["wq"], p["wk"], group=g, name=f"dil_attn_fwd_g{g}") for g in range(len(DIL_GROUPS))]
    y, om = _dil_merge_fwd(oe, name="dil_merge_fwd")
    out = _matmul(y, p["out_t"], trans_b=True, add=x, name="dil_proj_out")
    return out, (x, hn, slabs, y, om)


def _dil_layer_bwd(dx, dxb, nrow, p, saved):
    x, hn, slabs, y, om = saved
    d_y = _matmul(dxb, p["out_t"], name="dil_dmerged")
    g_out_t = _matmul(dxb, y, trans_a=True, name="dil_gw_out")
    g_out_t = g_out_t.reshape(D_MODEL, DIL_HEADS, LANES)[..., :DIL_DH].reshape(D_MODEL, DIL_HEADS * DIL_DH)
    stat = _dil_merge_bwd(d_y, om, name="dil_merge_bwd")
    d_hn = None
    dwq = jnp.zeros((1, LANES), F32)
    dwk = jnp.zeros((1, LANES), F32)
    g_groups = []
    for g in range(len(DIL_GROUPS)):
        dslab, dwq, dwk = _dil_attn_bwd(slabs[g], stat, p["wq"], p["wk"], dwq, dwk, group=g, name=f"dil_attn_bwd_g{g}")
        d_hn = _matmul(dslab, p["wg_t"][g], add=d_hn, name="dil_dhn")
        g_w = _matmul(dslab, hn, trans_a=True, name="dil_gw_in")
        g_groups.append(g_w.reshape(3, DIL_HEADS, LANES, D_MODEL)[:, :, :DIL_DH])
    g_in_t = jnp.stack(g_groups, axis=1).reshape(3 * len(DIL_GROUPS) * DIL_HEADS * DIL_DH, D_MODEL)
    dx_new, dxb_new, g_norm = _rmsnorm_bwd(x, nrow, d_hn, dx, name="rmsnorm_bwd")
    grads = dict(w_in=g_in_t, q_norm=dwq[0, :DIL_DH], k_norm=dwk[0, :DIL_DH], w_out=g_out_t, norm=g_norm[0])
    return dx_new, dxb_new, grads


def _ffn_layer_fwd(x, nrow, p):
    hn = _rmsnorm_fwd(x, nrow, name="rmsnorm_fwd")
    gate, up, act = _ffn_in(hn, p["in_t"], name="ffn_proj_in")
    y = _matmul(act, p["out"], add=x, name="ffn_proj_out")
    return y, (x, hn, gate, up, act)


def _ffn_layer_bwd(dx, dxb, nrow, p, saved):
    x, hn, gate, up, act = saved
    g_out = _matmul(act, dxb, trans_a=True, name="ffn_gw_out")
    d_g, d_u = _ffn_dact(dxb, p["out"], gate, up, name="ffn_dact")
    d_hn = _matmul(d_g, p["in_t"], b_rows=(0, FFN_HIDDEN), name="ffn_dhn_gate")
    d_hn = _matmul(d_u, p["in_t"], b_rows=(FFN_HIDDEN, FFN_HIDDEN), add=d_hn, name="ffn_dhn_up")
    g_in_t = jnp.concatenate([_matmul(d_g, hn, trans_a=True, name="ffn_gw_gate"),
                              _matmul(d_u, hn, trans_a=True, name="ffn_gw_up")], axis=0)
    dx_new, dxb_new, g_norm = _rmsnorm_bwd(x, nrow, d_hn, dx, name="rmsnorm_bwd")
    return dx_new, dxb_new, dict(w_in=g_in_t, w_out=g_out, norm=g_norm[0])


def _mixer_fwd(i, x, mix_row, prepared):
    if i % 2 == 0:
        return _gdn_layer_fwd(x, mix_row, prepared["gdn"][i // 2])
    return _dil_layer_fwd(x, mix_row, prepared["dil"][i // 2])


def _mixer_bwd(i, dx, dxb, mix_row, prepared, saved):
    if i % 2 == 0:
        return _gdn_layer_bwd(dx, dxb, mix_row, prepared["gdn"][i // 2], saved)
    return _dil_layer_bwd(dx, dxb, mix_row, prepared["dil"][i // 2], saved)


def _local_step(x, target, prepared, norm_mix, norm_ffn):
    saved = []
    for i in range(DEPTH):
        x, s_mix = _mixer_fwd(i, x, norm_mix[i].reshape(1, D_MODEL), prepared)
        x, s_ffn = _ffn_layer_fwd(x, norm_ffn[i].reshape(1, D_MODEL), prepared["ffn"][i])
        saved.append((s_mix, s_ffn))
    dx, dxb, loss = _loss_head(x, target, name="loss_head")
    g_mix, g_ffn = [None] * DEPTH, [None] * DEPTH
    for i in reversed(range(DEPTH)):
        s_mix, s_ffn = saved[i]
        dx, dxb, g_ffn[i] = _ffn_layer_bwd(dx, dxb, norm_ffn[i].reshape(1, D_MODEL), prepared["ffn"][i], s_ffn)
        dx, dxb, g_mix[i] = _mixer_bwd(i, dx, dxb, norm_mix[i].reshape(1, D_MODEL), prepared, s_mix)
    return loss[0, 0], dx, _collect_grads(g_mix, g_ffn)


def _collect_grads(g_mix, g_ffn):
    gdn = [g_mix[i] for i in range(0, DEPTH, 2)]
    dil = [g_mix[i] for i in range(1, DEPTH, 2)]
    if any(g is None for g in g_mix + g_ffn):
        pick = lambda gs, key: [None if g is None else g[key] for g in gs]
        return dict(gdn_w_in=pick(gdn, "w_in"), gdn_w_out=pick(gdn, "w_out"), dil_w_in=pick(dil, "w_in"),
                    dil_w_out=pick(dil, "w_out"), ffn_w_in=pick(g_ffn, "w_in"), ffn_w_out=pick(g_ffn, "w_out"))
    grads = dict(
        norm_mix=jnp.stack([g["norm"] for g in g_mix]),
        norm_ffn=jnp.stack([g["norm"] for g in g_ffn]),
        gdn_w_in=[g["w_in"] for g in gdn],
        gdn_conv_w=jnp.stack([g["conv"] for g in gdn]),
        gdn_a_log=jnp.stack([g["a_log"] for g in gdn]),
        gdn_dt_bias=jnp.stack([g["dt_bias"] for g in gdn]),
        gdn_norm_w=jnp.stack([g["norm_w"] for g in gdn]),
        gdn_w_out=[g["w_out"] for g in gdn],
        dil_w_in=[g["w_in"] for g in dil],
        dil_q_norm=jnp.stack([g["q_norm"] for g in dil]),
        dil_k_norm=jnp.stack([g["k_norm"] for g in dil]),
        dil_w_out=[g["w_out"] for g in dil],
        ffn_w_in=[g["w_in"] for g in g_ffn],
        ffn_w_out=[g["w_out"] for g in g_ffn],
    )
    return grads


MESH_ID = pl.DeviceIdType.MESH
ANY_SPACE = pl.BlockSpec(memory_space=pl.ANY)


def _mesh_position():
    return lax.axis_index("x"), lax.axis_index("y"), lax.axis_index("c")


def _flip(pos, k):
    x, y, c = pos
    return (1 - x if k & 4 else x, 1 - y if k & 2 else y, 1 - c if k & 1 else c)


def _linear(pos):
    return 4 * pos[0] + 2 * pos[1] + pos[2]


def _comm_scratch():
    return [pltpu.SemaphoreType.DMA((N_DEV - 1,)), pltpu.SemaphoreType.DMA((N_DEV - 1,)), pltpu.SemaphoreType.DMA(())]


def _all_gather(shard, *, name):
    def body(x_ref, out_ref, send_sems, recv_sems, local_sem):
        me = _mesh_position()
        mine = out_ref.at[_linear(me)]
        local = pltpu.make_async_copy(x_ref, mine, local_sem)
        local.start()
        copies = []
        for k in range(1, N_DEV):
            cp = pltpu.make_async_remote_copy(src_ref=x_ref, dst_ref=mine, send_sem=send_sems.at[k - 1],
                                              recv_sem=recv_sems.at[k - 1], device_id=_flip(me, k), device_id_type=MESH_ID)
            cp.start()
            copies.append(cp)
        for cp in copies:
            cp.wait()
        local.wait()

    return pl.pallas_call(
        body,
        out_shape=jax.ShapeDtypeStruct((N_DEV,) + shard.shape, shard.dtype),
        in_specs=[ANY_SPACE],
        out_specs=ANY_SPACE,
        scratch_shapes=_comm_scratch(),
        name=name,
    )(shard)


def _exchange(parts, *, name):
    def body(p_ref, out_ref, send_sems, recv_sems, local_sem):
        me = _mesh_position()
        mine = out_ref.at[_linear(me)]
        local = pltpu.make_async_copy(p_ref.at[_linear(me)], mine, local_sem)
        local.start()
        copies = []
        for k in range(1, N_DEV):
            peer = _flip(me, k)
            cp = pltpu.make_async_remote_copy(src_ref=p_ref.at[_linear(peer)], dst_ref=mine, send_sem=send_sems.at[k - 1],
                                              recv_sem=recv_sems.at[k - 1], device_id=peer, device_id_type=MESH_ID)
            cp.start()
            copies.append(cp)
        for cp in copies:
            cp.wait()
        local.wait()

    return pl.pallas_call(
        body,
        out_shape=jax.ShapeDtypeStruct(parts.shape, parts.dtype),
        in_specs=[ANY_SPACE],
        out_specs=ANY_SPACE,
        scratch_shapes=_comm_scratch(),
        name=name,
    )(parts)


HBM_SPACE = pl.BlockSpec(memory_space=pltpu.HBM)
SEM_SPACE = pl.BlockSpec(memory_space=pltpu.SEMAPHORE)
DATAFLOW = pltpu.SideEffectType.DATAFLOW_SIDE_EFFECTING


def _split_copies(src_ref, land_ref, send_sems, recv_sems, per_peer):
    me = _mesh_position()
    mine = land_ref.at[_linear(me)]
    copies = []
    for k in range(1, N_DEV):
        peer = _flip(me, k)
        src = src_ref.at[_linear(peer)] if per_peer else src_ref
        copies.append(pltpu.make_async_remote_copy(src_ref=src, dst_ref=mine, send_sem=send_sems.at[k - 1],
                                                   recv_sem=recv_sems.at[k - 1], device_id=peer, device_id_type=MESH_ID))
    return copies


def _travel_start(src, after, *, per_peer, name):
    me = _linear(_mesh_position())
    own = src[me] if per_peer else src
    shape = own.shape
    landing = lax.dynamic_update_slice(lax.empty((N_DEV,) + shape, src.dtype), own[None], (me, 0, 0))

    def body(src_ref, land_ref, after_ref, send_sems, recv_sems, src_thru, land_thru, token):
        for cp in _split_copies(src_ref, land_ref, send_sems, recv_sems, per_peer):
            cp.start()
        token[...] = jnp.zeros_like(token)

    return pl.pallas_call(
        body,
        name=name,
        out_shape=(pltpu.SemaphoreType.DMA((N_DEV - 1,)), pltpu.SemaphoreType.DMA((N_DEV - 1,)),
                   pltpu.HBM(src.shape, src.dtype), pltpu.HBM(landing.shape, landing.dtype),
                   jax.ShapeDtypeStruct((8, LANES), F32)),
        in_specs=(HBM_SPACE, HBM_SPACE, ANY_SPACE),
        out_specs=(SEM_SPACE, SEM_SPACE, HBM_SPACE, HBM_SPACE, pl.BlockSpec(memory_space=pltpu.VMEM)),
        input_output_aliases={0: 2, 1: 3},
        compiler_params=pltpu.CompilerParams(has_side_effects=DATAFLOW),
    )(pltpu.with_memory_space_constraint(src, pltpu.HBM), pltpu.with_memory_space_constraint(landing, pltpu.HBM), after)


def _travel_wait(started, after, *, per_peer, name):
    send_sems, recv_sems, src_thru, land_thru, _ = started

    def body(src_ref, land_ref, send_sems, recv_sems, after_ref, src_dead, got_ref):
        for cp in _split_copies(src_ref, land_ref, send_sems, recv_sems, per_peer):
            cp.wait_send()
            cp.wait_recv()

    return pl.pallas_call(
        body,
        name=name,
        out_shape=(pltpu.HBM(src_thru.shape, src_thru.dtype), pltpu.HBM(land_thru.shape, land_thru.dtype)),
        in_specs=(HBM_SPACE, HBM_SPACE, SEM_SPACE, SEM_SPACE, ANY_SPACE),
        out_specs=(HBM_SPACE, HBM_SPACE),
        input_output_aliases={0: 0, 1: 1},
        compiler_params=pltpu.CompilerParams(has_side_effects=DATAFLOW),
    )(src_thru, land_thru, send_sems, recv_sems, after)[1]


def _adamw(parts, w, m, v, *, name):
    rows, n = w.shape
    tb = _pick(rows, (PACK_ROW_ALIGN, 16))
    c1 = 1.0 - ADAM_B1 ** ADAM_STEP
    c2 = 1.0 - ADAM_B2 ** ADAM_STEP

    def body(p_ref, w_ref, m_ref, v_ref, g_ref, d_ref, nm_ref, nv_ref):
        g = p_ref[0].astype(F32)
        for s in range(1, N_DEV):
            g = g + p_ref[s].astype(F32)
        m_new = ADAM_B1 * m_ref[...] + (1.0 - ADAM_B1) * g
        v_new = ADAM_B2 * v_ref[...] + (1.0 - ADAM_B2) * (g * g)
        m_hat = m_new / c1
        v_hat = v_new / c2
        g_ref[...] = g
        nm_ref[...] = m_new
        nv_ref[...] = v_new
        d_ref[...] = -ADAM_LR * (m_hat / (jnp.sqrt(v_hat) + ADAM_EPS) + ADAM_WD * w_ref[...])

    blk = pl.BlockSpec((tb, n), lambda i: (i, 0))
    return pl.pallas_call(
        body,
        grid=(rows // tb,),
        in_specs=[pl.BlockSpec((N_DEV, tb, n), lambda i: (0, i, 0)), blk, blk, blk],
        out_specs=[blk] * 4,
        out_shape=[jax.ShapeDtypeStruct((rows, n), F32)] * 4,
        compiler_params=_cparams(("parallel",)),
        name=name,
    )(parts, w, m, v)


PACK_WIDTH = 1024
SHARDED = {
    "gdn_w_in": ((2, D_MODEL, GDN_IN_WIDTH), 2),
    "gdn_conv_w": ((2, GDN_CONV, GDN_QKV), 2),
    "gdn_w_out": ((2, GDN_HEADS * GDN_DV, D_MODEL), 1),
    "dil_w_in": ((2, D_MODEL, 3 * len(DIL_GROUPS) * DIL_HEADS * DIL_DH), 2),
    "dil_w_out": ((2, DIL_HEADS * DIL_DH, D_MODEL), 2),
    "ffn_w_in": ((DEPTH, D_MODEL, 2 * FFN_HIDDEN), 2),
    "ffn_w_out": ((DEPTH, FFN_HIDDEN, D_MODEL), 1),
}
REPLICATED = {"norm_mix": (DEPTH, D_MODEL), "norm_ffn": (DEPTH, D_MODEL), "gdn_a_log": (2, GDN_HEADS),
              "gdn_dt_bias": (2, GDN_HEADS), "gdn_norm_w": (2, GDN_DV), "dil_q_norm": (2, DIL_DH), "dil_k_norm": (2, DIL_DH)}
WEIGHT_ORDER = ("norm_mix", "norm_ffn", "gdn_w_in", "gdn_conv_w", "gdn_a_log", "gdn_dt_bias", "gdn_norm_w", "gdn_w_out",
                "dil_w_in", "dil_q_norm", "dil_k_norm", "dil_w_out", "ffn_w_in", "ffn_w_out")
PACK_ROW_ALIGN = 128
PIECE_ALIGN = 16
SMALL_ROWS = 16


def _shard_shape(name):
    shape, axis = SHARDED[name]
    return tuple(s // N_DEV if i == axis else s for i, s in enumerate(shape))


def _shard_rows(name):
    return math.prod(_shard_shape(name)) // PACK_WIDTH


def _split_shards(full, name):
    shape, axis = SHARDED[name]
    split = full.reshape(shape[:axis] + (N_DEV, shape[axis] // N_DEV) + shape[axis + 1:])
    return jnp.moveaxis(split, axis, 0)


def _join_shards(stacked, name):
    shape, axis = SHARDED[name]
    return jnp.moveaxis(stacked, 0, axis).reshape(shape)


COLUMN_SHARDED = ("gdn_w_in", "dil_w_in", "dil_w_out", "ffn_w_in")


def _to_rows(shard, name):
    if name in COLUMN_SHARDED:
        shard = jnp.swapaxes(shard, 1, 2)
    return shard.reshape(-1, PACK_WIDTH)


def _layer_columns(name):
    _, r, c = _shard_shape(name)
    return r if name in COLUMN_SHARDED else c


def _piece_rows(piece, halves=1):
    name, layer = piece
    rows = _shard_rows(name) * halves
    return rows if layer is None else rows // SHARDED[name][0][0]


def _aligned(rows, to=PIECE_ALIGN):
    return -(-rows // to) * to


def _pack_pieces(arrays, total_align=PIECE_ALIGN):
    padded, total = [], 0
    for a in arrays:
        rows = a.shape[-2]
        extra = _aligned(rows) - rows
        if extra:
            a = jnp.pad(a, [(0, 0)] * (a.ndim - 2) + [(0, extra), (0, 0)])
        padded.append(a)
        total += rows + extra
    tail = _aligned(total, total_align) - total
    if tail:
        padded.append(jnp.zeros(padded[0].shape[:-2] + (tail, PACK_WIDTH), padded[0].dtype))
    return jnp.concatenate(padded, axis=-2)


def _piece_offsets(pieces, halves=None):
    out, at = [], 0
    for p in pieces:
        rows = _piece_rows(p, (halves or {}).get(p[0], 1))
        out.append((p, at, rows))
        at += _aligned(rows)
    return out


def _shard_piece_rows(src, piece):
    name, layer = piece
    part = src[name] if layer is None else src[name][layer:layer + 1]
    return _to_rows(part.astype(F32), name)


def _piece_from_rows(rows, piece):
    name, layer = piece
    layers, r, c = _shard_shape(name)
    n_l = layers if layer is None else 1
    if name in COLUMN_SHARDED:
        return jnp.swapaxes(rows.reshape(n_l, c, r), 1, 2)
    return rows.reshape(n_l, r, c)


SMALL_TAIL = tuple(n for n in REPLICATED if n not in ("norm_mix", "norm_ffn"))


def _pack_small(vals):
    tail, at = jnp.zeros((PACK_WIDTH,), F32), 0
    for n in SMALL_TAIL:
        vec = vals[n].astype(F32).reshape(-1)
        tail = tail + jnp.pad(vec, (at, PACK_WIDTH - at - vec.shape[0]))
        at += vec.shape[0]
    buf = jnp.pad(vals["norm_mix"].astype(F32), ((0, SMALL_ROWS - DEPTH), (0, 0)))
    buf = buf + jnp.pad(vals["norm_ffn"].astype(F32), ((8, SMALL_ROWS - 8 - DEPTH), (0, 0)))
    return buf + jnp.pad(tail.reshape(1, PACK_WIDTH), ((SMALL_ROWS - 1, 0), (0, 0)))


def _unpack_small(buf):
    out = {"norm_mix": buf[0:DEPTH], "norm_ffn": buf[8:8 + DEPTH]}
    at = 0
    for n in SMALL_TAIL:
        size = math.prod(REPLICATED[n])
        out[n] = buf[SMALL_ROWS - 1, at:at + size].reshape(REPLICATED[n])
        at += size
    return out


GATHER_FIRST = (("gdn_w_in", 0), ("gdn_conv_w", None), ("gdn_w_out", 0))
GATHER_REST = ((("gdn_w_in", 1), ("gdn_w_out", 1)) + tuple(("dil_w_in", j) for j in range(2))
               + tuple(("dil_w_out", j) for j in range(2)) + tuple(("ffn_w_in", i) for i in range(DEPTH))
               + tuple(("ffn_w_out", i) for i in range(DEPTH)))
EXCHANGE_GROUPS = (
    (("ffn_w_in", 3), ("ffn_w_out", 3), ("dil_w_in", 1), ("dil_w_out", 1),
     ("ffn_w_in", 2), ("ffn_w_out", 2), ("gdn_w_in", 1), ("gdn_w_out", 1)),
    (("ffn_w_in", 1), ("ffn_w_out", 1), ("dil_w_in", 0), ("dil_w_out", 0)),
    (("ffn_w_in", 0), ("ffn_w_out", 0), ("gdn_w_in", 0), ("gdn_w_out", 0), ("gdn_conv_w", None)),
)


def _gather_operand(w, pieces):
    arrays = []
    for n, layer in pieces:
        if layer is None:
            arrays.append(lax.bitcast_convert_type(w[n], BF16).reshape(-1, PACK_WIDTH))
        else:
            arrays.append(_to_rows(w[n][layer:layer + 1].astype(BF16), n))
    return _pack_pieces(arrays)


def _gathered_weights(gathered, pieces, full):
    for (n, layer), at, rows in _piece_offsets(pieces, halves={"gdn_conv_w": 2}):
        block = gathered[:, at:at + rows]
        if layer is None:
            block = lax.bitcast_convert_type(block.reshape((N_DEV,) + _shard_shape(n) + (2,)), F32)
            full[n] = _join_shards(block, n)
        else:
            full.setdefault(n, {})[layer] = block.reshape(-1, _layer_columns(n))
    return full


def _exchange_operand(grads, pieces):
    arrays = []
    for n, layer in pieces:
        if layer is None:
            arrays.append(_split_shards(grads[n], n).astype(BF16).reshape(N_DEV, -1, PACK_WIDTH))
        else:
            arrays.append(grads[n][layer].astype(BF16).reshape(N_DEV, -1, PACK_WIDTH))
    return _pack_pieces(arrays, total_align=PACK_ROW_ALIGN)


def _update_group(received, pieces, w, m, v, *, name):
    packed = [_pack_pieces([_shard_piece_rows(src, p) for p in pieces], total_align=PACK_ROW_ALIGN) for src in (w, m, v)]
    outs = _adamw(received, *packed, name=name)
    return {p: tuple(_piece_from_rows(o[at:at + rows], p) for o in outs) for p, at, rows in _piece_offsets(pieces)}


def kernel(x, norm_mix, norm_ffn, gdn_w_in, gdn_conv_w, gdn_a_log, gdn_dt_bias, gdn_norm_w, gdn_w_out, dil_w_in, dil_q_norm, dil_k_norm, dil_w_out, ffn_w_in, ffn_w_out, loss_target, m_norm_mix, m_norm_ffn, m_gdn_w_in, m_gdn_conv_w, m_gdn_a_log, m_gdn_dt_bias, m_gdn_norm_w, m_gdn_w_out, m_dil_w_in, m_dil_q_norm, m_dil_k_norm, m_dil_w_out, m_ffn_w_in, m_ffn_w_out, v_norm_mix, v_norm_ffn, v_gdn_w_in, v_gdn_conv_w, v_gdn_a_log, v_gdn_dt_bias, v_gdn_norm_w, v_gdn_w_out, v_dil_w_in, v_dil_q_norm, v_dil_k_norm, v_dil_w_out, v_ffn_w_in, v_ffn_w_out):
    w = dict(norm_mix=norm_mix, norm_ffn=norm_ffn, gdn_w_in=gdn_w_in, gdn_conv_w=gdn_conv_w, gdn_a_log=gdn_a_log,
             gdn_dt_bias=gdn_dt_bias, gdn_norm_w=gdn_norm_w, gdn_w_out=gdn_w_out, dil_w_in=dil_w_in, dil_q_norm=dil_q_norm,
             dil_k_norm=dil_k_norm, dil_w_out=dil_w_out, ffn_w_in=ffn_w_in, ffn_w_out=ffn_w_out)
    m = dict(norm_mix=m_norm_mix, norm_ffn=m_norm_ffn, gdn_w_in=m_gdn_w_in, gdn_conv_w=m_gdn_conv_w, gdn_a_log=m_gdn_a_log,
             gdn_dt_bias=m_gdn_dt_bias, gdn_norm_w=m_gdn_norm_w, gdn_w_out=m_gdn_w_out, dil_w_in=m_dil_w_in,
             dil_q_norm=m_dil_q_norm, dil_k_norm=m_dil_k_norm, dil_w_out=m_dil_w_out, ffn_w_in=m_ffn_w_in, ffn_w_out=m_ffn_w_out)
    v = dict(norm_mix=v_norm_mix, norm_ffn=v_norm_ffn, gdn_w_in=v_gdn_w_in, gdn_conv_w=v_gdn_conv_w, gdn_a_log=v_gdn_a_log,
             gdn_dt_bias=v_gdn_dt_bias, gdn_norm_w=v_gdn_norm_w, gdn_w_out=v_gdn_w_out, dil_w_in=v_dil_w_in,
             dil_q_norm=v_dil_q_norm, dil_k_norm=v_dil_k_norm, dil_w_out=v_dil_w_out, ffn_w_in=v_ffn_w_in, ffn_w_out=v_ffn_w_out)
    def row(src, i):
        return src[i].reshape(1, D_MODEL)

    first = _all_gather(_gather_operand(w, GATHER_FIRST), name="weight_all_gather_first")
    rest_started = _travel_start(_gather_operand(w, GATHER_REST), first, per_peer=False, name="weight_gather_start")
    full = _gathered_weights(first, GATHER_FIRST, {n: w[n] for n in REPLICATED})
    prepared = dict(gdn=_prepare_gdn(full, layers=(0,)))
    h = x[0]
    saved = [None] * DEPTH
    h, s_mix = _mixer_fwd(0, h, row(norm_mix, 0) + rest_started[4][0, 0], prepared)
    rest = _travel_wait(rest_started, h, per_peer=False, name="weight_gather_wait")
    full = _gathered_weights(rest, GATHER_REST, full)
    prepared["gdn"].update(_prepare_gdn(full, layers=(1,)))
    prepared.update(dil=_prepare_dil(full), ffn=_prepare_ffn(full))
    for i in range(DEPTH):
        if i > 0:
            h, s_mix = _mixer_fwd(i, h, row(norm_mix, i), prepared)
        h, s_ffn = _ffn_layer_fwd(h, row(norm_ffn, i), prepared["ffn"][i])
        saved[i] = (s_mix, s_ffn)
    dx, dxb, loss = _loss_head(h, loss_target[0], name="loss_head")

    g_mix, g_ffn = [None] * DEPTH, [None] * DEPTH
    started = {}
    for i in reversed(range(DEPTH)):
        s_mix, s_ffn = saved[i]
        dx, dxb, g_ffn[i] = _ffn_layer_bwd(dx, dxb, row(norm_ffn, i), prepared["ffn"][i], s_ffn)
        dx, dxb, g_mix[i] = _mixer_bwd(i, dx, dxb, row(norm_mix, i), prepared, s_mix)
        group = {2: 0, 1: 1}.get(i)
        if group is not None:
            operand = _exchange_operand(_collect_grads(g_mix, g_ffn), EXCHANGE_GROUPS[group])
            started[group] = _travel_start(operand, dx, per_peer=True, name=f"grad_exchange_start_{group}")
            dxb = dxb + started[group][4][0, 0].astype(dxb.dtype)
    grads = _collect_grads(g_mix, g_ffn)
    received = [_travel_wait(started[g], dx, per_peer=True, name=f"grad_exchange_wait_{g}") for g in (0, 1)]
    received.append(_exchange(_exchange_operand(grads, EXCHANGE_GROUPS[2]), name="grad_exchange_last"))
    updated = {}
    for g, pieces in enumerate(EXCHANGE_GROUPS):
        updated.update(_update_group(received[g], pieces, w, m, v, name=f"adamw_sharded_{g}"))

    small_parts = _all_gather(_pack_small(grads), name="small_grad_all_gather")
    outs_small = [_unpack_small(o) for o in
                  _adamw(small_parts, _pack_small(w), _pack_small(m), _pack_small(v), name="adamw_replicated")]

    total_loss = lax.psum(loss[0, 0], ("x", "y", "c"))
    result = [total_loss, dx[None]]
    for k in range(4):
        for n in WEIGHT_ORDER:
            if n not in SHARDED:
                result.append(outs_small[k][n])
            elif (n, None) in updated:
                result.append(updated[(n, None)][k])
            else:
                result.append(jnp.concatenate([updated[(n, l)][k] for l in range(SHARDED[n][0][0])], axis=0))
    return tuple(result)
```

```python
import functools
import math

import jax
import jax.numpy as jnp
from jax import lax
from jax.experimental import pallas as pl
from jax.experimental.pallas import tpu as pltpu

F32 = jnp.float32
BF16 = jnp.bfloat16
MM_DTYPE = BF16

N_DEV = 8
D_MODEL = 1024
DEPTH = 4
RMS_EPS = 1e-6
L2_EPS = 1e-6

LANES = 128

GDN_HEADS = 8
GDN_DK = 128
GDN_DV = 128
GDN_CONV = 4
GDN_CHUNK = 128
GDN_QKV = 3 * GDN_HEADS * GDN_DK
GDN_MAIN = GDN_QKV + GDN_HEADS * GDN_DV
GDN_IN_WIDTH = GDN_MAIN + 2 * GDN_HEADS

DIL_GROUPS = ((128, 1), (512, 4), (2048, 16))
DIL_HEADS = 8
DIL_DH = 64
DIL_SPAN = 128
DIL_SLAB = 3 * DIL_HEADS * LANES
ALIBI_MAX_BIAS = 8.0

FFN_HIDDEN = 2816

ADAM_LR = 0.001
ADAM_B1 = 0.9
ADAM_B2 = 0.999
ADAM_EPS = 1e-08
ADAM_WD = 0.01
ADAM_STEP = 10

VMEM_LIMIT = 56 * 1024 * 1024
NEG = -1e30
HI = lax.Precision.HIGHEST


def _cparams(sem):
    return pltpu.CompilerParams(dimension_semantics=sem, vmem_limit_bytes=VMEM_LIMIT)


def _dot(a, b):
    return lax.dot_general(a, b, (((1,), (0,)), ((), ())), preferred_element_type=F32, precision=HI)


def _dot_nt(a, b):
    return lax.dot_general(a, b, (((1,), (1,)), ((), ())), preferred_element_type=F32, precision=HI)


def _dot_tn(a, b):
    return lax.dot_general(a, b, (((0,), (0,)), ((), ())), preferred_element_type=F32, precision=HI)


def _single_pass(a, b, a_dim, b_dim):
    lead = a.ndim - 2
    batch = ((0,), (0,)) if lead else ((), ())
    return lax.dot_general(a.astype(BF16), b.astype(BF16), (((lead + a_dim,), (lead + b_dim,)), batch),
                           preferred_element_type=F32)


def _bdot(a, b):
    return _single_pass(a, b, 1, 0)


def _bdot_nt(a, b):
    return _single_pass(a, b, 1, 1)


def _bdot_tn(a, b):
    return _single_pass(a, b, 0, 0)


def _pick(n, candidates):
    for c in candidates:
        if n % c == 0:
            return c
    raise ValueError(f"no tile for {n}")


def _matmul(a, b, *, name, trans_a=False, trans_b=False, b_rows=None, add=None, out_dtype=F32):
    if trans_a:
        k_dim, m_dim = a.shape
    else:
        m_dim, k_dim = a.shape
    b_start, b_size = b_rows if b_rows is not None else (0, b.shape[0])
    if trans_b:
        n_dim, k2 = b_size, b.shape[1]
    else:
        k2, n_dim = b_size, b.shape[1]
    assert k_dim == k2, (a.shape, b.shape, b_rows)
    tn = _pick(n_dim, (1024, 512, 256, 128))
    tm = min(m_dim, 2048, max(512, (1024 * 1024) // tn))
    tm = _pick(m_dim, (tm, 1408, 1024, 512, 256, 128))
    tk = _pick(k_dim, (1024, 1408, 512, 256, 128))
    nk = k_dim // tk
    has_add = add is not None
    dn = (((0 if trans_a else 1,), (1 if trans_b else 0,)), ((), ()))
    b_tile = tn if trans_b else tk
    assert b_start % b_tile == 0, (b_rows, b_tile)
    b_off = b_start // b_tile

    def body(*refs):
        if has_add:
            a_ref, b_ref, add_ref, o_ref, acc_ref = refs
        else:
            a_ref, b_ref, o_ref, acc_ref = refs
        part = lax.dot_general(a_ref[...], b_ref[...], dn, preferred_element_type=F32)

        def finish(total):
            if has_add:
                total = total + add_ref[...]
            o_ref[...] = total.astype(out_dtype)

        if nk == 1:
            finish(part)
        else:
            k = pl.program_id(2)

            @pl.when(k == 0)
            def _():
                acc_ref[...] = part

            @pl.when(k > 0)
            def _():
                acc_ref[...] += part

            @pl.when(k == nk - 1)
            def _():
                finish(acc_ref[...])

    if trans_a:
        a_spec = pl.BlockSpec((tk, tm), lambda i, j, k: (k, i))
    else:
        a_spec = pl.BlockSpec((tm, tk), lambda i, j, k: (i, k))
    if trans_b:
        b_spec = pl.BlockSpec((tn, tk), lambda i, j, k: (j + b_off, k))
    else:
        b_spec = pl.BlockSpec((tk, tn), lambda i, j, k: (k + b_off, j))
    in_specs = [a_spec, b_spec]
    args = [a, b]
    if has_add:
        in_specs.append(pl.BlockSpec((tm, tn), lambda i, j, k: (i, j)))
        args.append(add)
    return pl.pallas_call(
        body,
        grid=(m_dim // tm, n_dim // tn, nk),
        in_specs=in_specs,
        out_specs=pl.BlockSpec((tm, tn), lambda i, j, k: (i, j)),
        out_shape=jax.ShapeDtypeStruct((m_dim, n_dim), out_dtype),
        scratch_shapes=[pltpu.VMEM((tm, tn) if nk > 1 else (8, LANES), F32)],
        compiler_params=_cparams(("parallel", "parallel", "arbitrary")),
        name=name,
    )(*args)


def _rmsnorm_fwd(x, w_row, *, name):
    t, d = x.shape
    tb = min(t, 1024)

    def body(x_ref, w_ref, o_ref):
        xf = x_ref[...]
        r = lax.rsqrt(jnp.mean(xf * xf, axis=-1, keepdims=True) + RMS_EPS)
        o_ref[...] = (xf * r * w_ref[...]).astype(o_ref.dtype)

    return pl.pallas_call(
        body,
        grid=(t // tb,),
        in_specs=[pl.BlockSpec((tb, d), lambda i: (i, 0)), pl.BlockSpec((1, d), lambda i: (0, 0))],
        out_specs=pl.BlockSpec((tb, d), lambda i: (i, 0)),
        out_shape=jax.ShapeDtypeStruct((t, d), MM_DTYPE),
        compiler_params=_cparams(("parallel",)),
        name=name,
    )(x, w_row)


def _rmsnorm_bwd(x, w_row, dy, dskip, *, name):
    t, d = x.shape
    tb = min(t, 512)

    def body(x_ref, w_ref, dy_ref, ds_ref, dx_ref, dxb_ref, dw_ref):
        xf = x_ref[...]
        g = dy_ref[...]
        r = lax.rsqrt(jnp.mean(xf * xf, axis=-1, keepdims=True) + RMS_EPS)
        gw = g * w_ref[...]
        proj = jnp.mean(gw * xf, axis=-1, keepdims=True)
        dx = r * gw - xf * (r * r * r * proj) + ds_ref[...]
        dx_ref[...] = dx
        dxb_ref[...] = dx.astype(dxb_ref.dtype)
        part = jnp.sum(g * xf * r, axis=0, keepdims=True)

        @pl.when(pl.program_id(0) == 0)
        def _():
            dw_ref[...] = part

        @pl.when(pl.program_id(0) > 0)
        def _():
            dw_ref[...] += part

    row = pl.BlockSpec((tb, d), lambda i: (i, 0))
    one = pl.BlockSpec((1, d), lambda i: (0, 0))
    return pl.pallas_call(
        body,
        grid=(t // tb,),
        in_specs=[row, one, row, row],
        out_specs=[row, row, one],
        out_shape=[jax.ShapeDtypeStruct((t, d), F32), jax.ShapeDtypeStruct((t, d), MM_DTYPE),
                   jax.ShapeDtypeStruct((1, d), F32)],
        compiler_params=_cparams(("arbitrary",)),
        name=name,
    )(x, w_row, dy, dskip)


def _silu(z):
    return z / (1.0 + jnp.exp(-z))


FFN_TM, FFN_TN = 512, 1408


def _ffn_in(hn, in_t, *, name):
    t, d = hn.shape
    h = FFN_HIDDEN
    tm, tn = min(t, FFN_TM), FFN_TN
    nj = h // tn
    dn = (((1,), (1,)), ((), ()))

    def body(a_ref, bg_ref, bu_ref, g_ref, u_ref, act_ref):
        a = a_ref[...]
        g = lax.dot_general(a, bg_ref[...], dn, preferred_element_type=F32)
        u = lax.dot_general(a, bu_ref[...], dn, preferred_element_type=F32)
        g_ref[...] = g.astype(g_ref.dtype)
        u_ref[...] = u.astype(u_ref.dtype)
        act_ref[...] = (_silu(g) * u).astype(act_ref.dtype)

    out = pl.BlockSpec((tm, tn), lambda i, j: (i, j))
    return pl.pallas_call(
        body,
        grid=(t // tm, nj),
        in_specs=[pl.BlockSpec((tm, d), lambda i, j: (i, 0)), pl.BlockSpec((tn, d), lambda i, j: (j, 0)),
                  pl.BlockSpec((tn, d), lambda i, j: (j + nj, 0))],
        out_specs=[out, out, out],
        out_shape=[jax.ShapeDtypeStruct((t, h), MM_DTYPE)] * 3,
        compiler_params=_cparams(("parallel", "parallel")),
        name=name,
    )(hn, in_t, in_t)


def _ffn_dact(dy, out_w, g, u, *, name):
    t, d = dy.shape
    h = FFN_HIDDEN
    tm, tn = min(t, FFN_TM), FFN_TN

    def body(a_ref, b_ref, g_ref, u_ref, dg_ref, du_ref):
        da = lax.dot_general(a_ref[...], b_ref[...], (((1,), (1,)), ((), ())), preferred_element_type=F32)
        gate = g_ref[...].astype(F32)
        sig = 1.0 / (1.0 + jnp.exp(-gate))
        sg = gate * sig
        dg_ref[...] = (da * u_ref[...].astype(F32) * (sig + sg * (1.0 - sig))).astype(dg_ref.dtype)
        du_ref[...] = (da * sg).astype(du_ref.dtype)

    blk = pl.BlockSpec((tm, tn), lambda i, j: (i, j))
    return pl.pallas_call(
        body,
        grid=(t // tm, h // tn),
        in_specs=[pl.BlockSpec((tm, d), lambda i, j: (i, 0)), pl.BlockSpec((tn, d), lambda i, j: (j, 0)), blk, blk],
        out_specs=[blk, blk],
        out_shape=[jax.ShapeDtypeStruct((t, h), MM_DTYPE)] * 2,
        compiler_params=_cparams(("parallel", "parallel")),
        name=name,
    )(dy, out_w, g, u)


def _loss_head(y, target, *, name):
    t, d = y.shape
    tb = min(t, 1024)

    def body(y_ref, t_ref, dy_ref, dyb_ref, l_ref):
        err = y_ref[...] - t_ref[...]
        dy_ref[...] = err * (1.0 / d)
        dyb_ref[...] = (err * (1.0 / d)).astype(dyb_ref.dtype)
        part = jnp.sum(jnp.sum(err * err, axis=0, keepdims=True), axis=1, keepdims=True) * (0.5 / d)
        part = jnp.broadcast_to(part, l_ref.shape)

        @pl.when(pl.program_id(0) == 0)
        def _():
            l_ref[...] = part

        @pl.when(pl.program_id(0) > 0)
        def _():
            l_ref[...] += part

    row = pl.BlockSpec((tb, d), lambda i: (i, 0))
    return pl.pallas_call(
        body,
        grid=(t // tb,),
        in_specs=[row, row],
        out_specs=[row, row, pl.BlockSpec((8, LANES), lambda i: (0, 0))],
        out_shape=[jax.ShapeDtypeStruct((t, d), F32), jax.ShapeDtypeStruct((t, d), MM_DTYPE),
                   jax.ShapeDtypeStruct((8, LANES), F32)],
        compiler_params=_cparams(("arbitrary",)),
        name=name,
    )(y, target)


CONV_HALO = 8


def _conv_tile_scale(c):
    is_qk = c < 2 * GDN_HEADS
    scale = jnp.where(c < GDN_HEADS, GDN_DK ** -0.5, 1.0).astype(F32)
    return is_qk, scale


def _gdn_conv_fwd(pm, conv_w, *, name):
    t = pm.shape[0]
    tb = min(t, 1024)
    nt = t // tb
    hb = tb // CONV_HALO

    def body(x_ref, xp_ref, w_ref, o_ref):
        c = pl.program_id(0)
        ti = pl.program_id(1)
        prev = jnp.where(ti > 0, xp_ref[...], 0.0)
        xe = jnp.concatenate([prev, x_ref[...]], axis=0)
        w = w_ref[...]
        y = jnp.zeros((tb, LANES), F32)
        for j in range(GDN_CONV):
            off = CONV_HALO - (GDN_CONV - 1) + j
            y = y + w[j:j + 1, :] * xe[off:off + tb, :]
        s = _silu(y)
        is_qk, scale = _conv_tile_scale(c)
        r = lax.rsqrt(jnp.sum(s * s, axis=-1, keepdims=True) + L2_EPS) * scale
        o_ref[...] = s * jnp.where(is_qk, r, 1.0)

    return pl.pallas_call(
        body,
        grid=(GDN_QKV // LANES, nt),
        in_specs=[
            pl.BlockSpec((tb, LANES), lambda c, i: (i, c)),
            pl.BlockSpec((CONV_HALO, LANES), lambda c, i: (jnp.maximum(i * hb - 1, 0), c)),
            pl.BlockSpec((GDN_CONV, LANES), lambda c, i: (0, c)),
        ],
        out_specs=pl.BlockSpec((tb, LANES), lambda c, i: (i, c)),
        out_shape=jax.ShapeDtypeStruct((t, GDN_QKV), F32),
        compiler_params=_cparams(("parallel", "parallel")),
        name=name,
    )(pm, pm, conv_w)


def _gdn_conv_bwd(pm, conv_w, dout, *, name):
    t = pm.shape[0]
    tb = min(t, 1024)
    nt = t // tb
    hb = tb // CONV_HALO
    last_hb = t // CONV_HALO - 1
    ext = tb + CONV_HALO

    def body(x_ref, xp_ref, xn_ref, d_ref, dn_ref, w_ref, dx_ref, dw_ref):
        c = pl.program_id(0)
        ti = pl.program_id(1)
        prev = jnp.where(ti > 0, xp_ref[...], 0.0)
        has_next = ti < nt - 1
        nxt = jnp.where(has_next, xn_ref[...], 0.0)
        xe = jnp.concatenate([prev, x_ref[...], nxt], axis=0)
        de = jnp.concatenate([d_ref[...], jnp.where(has_next, dn_ref[...], 0.0)], axis=0)
        w = w_ref[...]
        y = jnp.zeros((ext, LANES), F32)
        for j in range(GDN_CONV):
            off = CONV_HALO - (GDN_CONV - 1) + j
            y = y + w[j:j + 1, :] * xe[off:off + ext, :]
        sig = 1.0 / (1.0 + jnp.exp(-y))
        s = y * sig
        is_qk, scale = _conv_tile_scale(c)
        r = lax.rsqrt(jnp.sum(s * s, axis=-1, keepdims=True) + L2_EPS)
        n = s * r
        dnrm = de * scale
        ds_qk = r * (dnrm - n * jnp.sum(dnrm * n, axis=-1, keepdims=True))
        ds = jnp.where(is_qk, ds_qk, de)
        dy = ds * (sig + s * (1.0 - sig))
        dx = jnp.zeros((tb, LANES), F32)
        dw_rows = []
        for j in range(GDN_CONV):
            sh = GDN_CONV - 1 - j
            dx = dx + w[j:j + 1, :] * dy[sh:sh + tb, :]
            off = CONV_HALO - (GDN_CONV - 1) + j
            dw_rows.append(jnp.sum(dy[:tb, :] * xe[off:off + tb, :], axis=0, keepdims=True))
        dx_ref[...] = dx.astype(dx_ref.dtype)
        part = jnp.concatenate(dw_rows, axis=0)

        @pl.when(ti == 0)
        def _():
            dw_ref[...] = part

        @pl.when(ti > 0)
        def _():
            dw_ref[...] += part

    main = pl.BlockSpec((tb, LANES), lambda c, i: (i, c))
    prev = pl.BlockSpec((CONV_HALO, LANES), lambda c, i: (jnp.maximum(i * hb - 1, 0), c))
    nxt = pl.BlockSpec((CONV_HALO, LANES), lambda c, i: (jnp.minimum((i + 1) * hb, last_hb), c))
    return pl.pallas_call(
        body,
        grid=(GDN_QKV // LANES, nt),
        in_specs=[main, prev, nxt, main, nxt, pl.BlockSpec((GDN_CONV, LANES), lambda c, i: (0, c))],
        out_specs=[main, pl.BlockSpec((GDN_CONV, LANES), lambda c, i: (0, c))],
        out_shape=[jax.ShapeDtypeStruct((t, GDN_QKV), MM_DTYPE), jax.ShapeDtypeStruct((GDN_CONV, GDN_QKV), F32)],
        compiler_params=_cparams(("parallel", "arbitrary")),
        name=name,
    )(pm, pm, pm, dout, dout, conv_w)


def _head_selector(first_col):
    row = lax.broadcasted_iota(jnp.int32, (LANES, GDN_HEADS * LANES), 0)
    col = lax.broadcasted_iota(jnp.int32, (LANES, GDN_HEADS * LANES), 1)
    return (col // LANES + first_col == row).astype(F32)


def _softplus(x):
    return jnp.maximum(x, 0.0) + jnp.log(1.0 + jnp.exp(-jnp.abs(x)))


def _gdn_gates_fwd(ab, alog_row, dt_row, *, name):
    t = ab.shape[0]
    tb = min(t, 1024)
    wide = GDN_HEADS * LANES

    def body(ab_ref, al_ref, dt_ref, g_ref, b_ref):
        x = ab_ref[...]
        g_cols = -jnp.exp(al_ref[...]) * _softplus(x + dt_ref[...])
        b_cols = 1.0 / (1.0 + jnp.exp(-x))
        g_ref[...] = _dot(g_cols, _head_selector(0))
        b_ref[...] = _dot(b_cols, _head_selector(GDN_HEADS))

    row = pl.BlockSpec((tb, LANES), lambda i: (i, 0))
    one = pl.BlockSpec((1, LANES), lambda i: (0, 0))
    out = pl.BlockSpec((tb, wide), lambda i: (i, 0))
    return pl.pallas_call(
        body,
        grid=(t // tb,),
        in_specs=[row, one, one],
        out_specs=[out, out],
        out_shape=[jax.ShapeDtypeStruct((t, wide), F32)] * 2,
        compiler_params=_cparams(("parallel",)),
        name=name,
    )(ab, alog_row, dt_row)


def _gdn_gates_bwd(ab, alog_row, dt_row, dgb, dbb, *, name):
    t = ab.shape[0]
    tb = min(t, 1024)
    wide = GDN_HEADS * LANES

    def body(ab_ref, al_ref, dt_ref, dg_ref, db_ref, dab_ref, dal_ref, ddt_ref):
        x = ab_ref[...]
        lane = lax.broadcasted_iota(jnp.int32, (tb, LANES), 1)
        dg_cols = _dot_nt(dg_ref[...], _head_selector(0))
        db_cols = _dot_nt(db_ref[...], _head_selector(GDN_HEADS))
        ea = jnp.exp(al_ref[...])
        z = x + dt_ref[...]
        sp = _softplus(z)
        sg = 1.0 / (1.0 + jnp.exp(-z))
        beta = 1.0 / (1.0 + jnp.exp(-x))
        da = jnp.where(lane < GDN_HEADS, dg_cols * (-ea) * sg, 0.0)
        db = jnp.where((lane >= GDN_HEADS) & (lane < 2 * GDN_HEADS), db_cols * beta * (1.0 - beta), 0.0)
        dab_ref[...] = (da + db).astype(dab_ref.dtype)
        p_al = jnp.sum(jnp.where(lane < GDN_HEADS, dg_cols * (-ea) * sp, 0.0), axis=0, keepdims=True)
        p_dt = jnp.sum(da, axis=0, keepdims=True)

        @pl.when(pl.program_id(0) == 0)
        def _():
            dal_ref[...] = p_al
            ddt_ref[...] = p_dt

        @pl.when(pl.program_id(0) > 0)
        def _():
            dal_ref[...] += p_al
            ddt_ref[...] += p_dt

    row = pl.BlockSpec((tb, LANES), lambda i: (i, 0))
    one = pl.BlockSpec((1, LANES), lambda i: (0, 0))
    big = pl.BlockSpec((tb, wide), lambda i: (i, 0))
    return pl.pallas_call(
        body,
        grid=(t // tb,),
        in_specs=[row, one, one, big, big],
        out_specs=[row, one, one],
        out_shape=[jax.ShapeDtypeStruct((t, LANES), MM_DTYPE), jax.ShapeDtypeStruct((1, LANES), F32),
                   jax.ShapeDtypeStruct((1, LANES), F32)],
        compiler_params=_cparams(("arbitrary",)),
        name=name,
    )(ab, alog_row, dt_row, dgb, dbb)


@jax.custom_vjp
def _unit_lower_inverse_rest(n):
    c = n.shape[-1]
    ri = lax.broadcasted_iota(jnp.int32, (c, c), 0)
    ci = lax.broadcasted_iota(jnp.int32, (c, c), 1)
    rest = None
    size = 1
    while size < c:
        joins = ((ri // (2 * size)) == (ci // (2 * size))) & ((ri // size) != (ci // size))
        low = jnp.where(joins, n, 0.0)
        if rest is None:
            rest = -low
        else:
            left = low + _bdot(rest, low)
            rest = rest - (left + _bdot(left, rest))
        size *= 2
    return rest


def _unit_lower_inverse_rest_fwd(n):
    rest = _unit_lower_inverse_rest(n)
    return rest, rest


def _unit_lower_inverse_rest_bwd(rest, ct):
    left = ct + _bdot_tn(rest, ct)
    return (-(left + _bdot_nt(left, rest)),)


_unit_lower_inverse_rest.defvjp(_unit_lower_inverse_rest_fwd, _unit_lower_inverse_rest_bwd)


def _bf16_pieces(x):
    hi = x.astype(BF16)
    r1 = x - hi.astype(F32)
    mid = r1.astype(BF16)
    lo = (r1 - mid.astype(F32)).astype(BF16)
    return hi, mid, lo


def _lower_ones(shape):
    c = shape[-1]
    ri = lax.broadcasted_iota(jnp.int32, (c, c), 0)
    ci = lax.broadcasted_iota(jnp.int32, (c, c), 1)
    return jnp.broadcast_to((ri >= ci).astype(BF16), shape)


@jax.custom_vjp
def _running_sum(x):
    tri = _lower_ones(x.shape)
    return sum(_bdot(tri, p) for p in _bf16_pieces(x))


def _running_sum_fwd(x):
    return _running_sum(x), None


def _running_sum_bwd(_, ct):
    tri = _lower_ones(ct.shape)
    return (sum(_bdot_tn(tri, p) for p in _bf16_pieces(ct)),)


_running_sum.defvjp(_running_sum_fwd, _running_sum_bwd)


def _gdn_prep_math(q, k, v, gb, bb):
    c = GDN_CHUNK
    ri = lax.broadcasted_iota(jnp.int32, (c, c), 0)
    ci = lax.broadcasted_iota(jnp.int32, (c, c), 1)
    causal = ri >= ci
    gc = _running_sum(gb)
    decay = jnp.exp(jnp.where(causal, gc - jnp.swapaxes(gc, -1, -2), NEG))
    n = jnp.where(ri > ci, _bdot_nt(k, k) * bb * decay, 0.0)
    rest = _unit_lower_inverse_rest(n)
    eg = jnp.exp(gc)
    rhs_v = v * bb
    rhs_k = k * bb * eg
    u = rhs_v + _bdot(rest, rhs_v)
    w = rhs_k + _bdot(rest, rhs_k)
    qk = _bdot_nt(q, k) * decay
    qd = q * eg
    last = jnp.sum(jnp.where(ri == c - 1, gc, 0.0), axis=-2, keepdims=True)
    gl = jnp.broadcast_to(last, gc.shape)
    kt = k * jnp.exp(gl - gc)
    cd = jnp.exp(gl)
    return u, w, qk, qd, kt, cd


def _head_tiles(ref, h):
    return ref[:, h * LANES:(h + 1) * LANES]


def _stack_heads(ref, first=0, heads=GDN_HEADS):
    return jnp.stack([_head_tiles(ref, first + h) for h in range(heads)])


def _store_heads(ref, val, first=0):
    for h in range(val.shape[0]):
        ref[:, (first + h) * LANES:(first + h + 1) * LANES] = val[h].astype(ref.dtype)


def _gdn_prep_fwd(qkv, gb, bb, *, name):
    t = qkv.shape[0]
    c = GDN_CHUNK
    wide = GDN_HEADS * LANES

    def body(q_ref, k_ref, v_ref, g_ref, b_ref, *outs):
        res = _gdn_prep_math(*(_stack_heads(r) for r in (q_ref, k_ref, v_ref, g_ref, b_ref)))
        for o_ref, val in zip(outs, res):
            _store_heads(o_ref, val)

    blk = lambda off: pl.BlockSpec((c, wide), lambda i: (i, off))
    return pl.pallas_call(
        body,
        grid=(t // c,),
        in_specs=[blk(0), blk(1), blk(2), blk(0), blk(0)],
        out_specs=[blk(0)] * 6,
        out_shape=[jax.ShapeDtypeStruct((t, wide), F32)] * 6,
        compiler_params=_cparams(("parallel",)),
        name=name,
    )(qkv, qkv, qkv, gb, bb)


def _gdn_prep_bwd(qkv, gb, bb, cts, *, name):
    t = qkv.shape[0]
    c = GDN_CHUNK
    wide = GDN_HEADS * LANES

    def body(q_ref, k_ref, v_ref, g_ref, b_ref, c0, c1, c2, c3, c4, c5, dqkv_ref, dg_ref, db_ref):
        prim = tuple(_stack_heads(r) for r in (q_ref, k_ref, v_ref, g_ref, b_ref))
        _, pull = jax.vjp(_gdn_prep_math, *prim)
        dq, dk, dv, dg, db = pull(tuple(_stack_heads(r) for r in (c0, c1, c2, c3, c4, c5)))
        _store_heads(dqkv_ref, dq)
        _store_heads(dqkv_ref, dk, first=GDN_HEADS)
        _store_heads(dqkv_ref, dv, first=2 * GDN_HEADS)
        _store_heads(dg_ref, dg)
        _store_heads(db_ref, db)

    blk = lambda off: pl.BlockSpec((c, wide), lambda i: (i, off))
    return pl.pallas_call(
        body,
        grid=(t // c,),
        in_specs=[blk(0), blk(1), blk(2), blk(0), blk(0)] + [blk(0)] * 6,
        out_specs=[pl.BlockSpec((c, 3 * wide), lambda i: (i, 0)), blk(0), blk(0)],
        out_shape=[jax.ShapeDtypeStruct((t, 3 * wide), F32), jax.ShapeDtypeStruct((t, wide), F32),
                   jax.ShapeDtypeStruct((t, wide), F32)],
        compiler_params=_cparams(("parallel",)),
        name=name,
    )(qkv, qkv, qkv, gb, bb, *cts)


def _gdn_scan_math(s, u, w, qk, qd, kt, cd):
    v_new = u - _bdot(w, s)
    o = _bdot(qd, s) + _bdot(qk, v_new)
    s_new = s * cd + _bdot_tn(kt, v_new)
    return o, s_new


def _gdn_scan_fwd(prep, *, name):
    t = prep[0].shape[0]
    c = GDN_CHUNK
    wide = GDN_HEADS * LANES

    def body(u_ref, w_ref, qk_ref, qd_ref, kt_ref, cd_ref, o_ref, st_ref, s_ref):
        @pl.when(pl.program_id(0) == 0)
        def _():
            s_ref[...] = jnp.zeros_like(s_ref)

        s = _stack_heads(s_ref)
        _store_heads(st_ref, s)
        o, s_new = _gdn_scan_math(s, *(_stack_heads(r) for r in (u_ref, w_ref, qk_ref, qd_ref, kt_ref, cd_ref)))
        _store_heads(o_ref, o)
        _store_heads(s_ref, s_new)

    blk = pl.BlockSpec((c, wide), lambda i: (i, 0))
    return pl.pallas_call(
        body,
        grid=(t // c,),
        in_specs=[blk] * 6,
        out_specs=[blk, blk],
        out_shape=[jax.ShapeDtypeStruct((t, wide), F32)] * 2,
        scratch_shapes=[pltpu.VMEM((GDN_DK, wide), F32)],
        compiler_params=_cparams(("arbitrary",)),
        name=name,
    )(*prep)


def _gdn_scan_bwd(prep, states, do, *, name):
    t = do.shape[0]
    c = GDN_CHUNK
    wide = GDN_HEADS * LANES
    nc = t // c

    def body(u_ref, w_ref, qk_ref, qd_ref, kt_ref, cd_ref, st_ref, do_ref, *rest):
        outs, ds_ref = rest[:6], rest[6]

        @pl.when(pl.program_id(0) == 0)
        def _():
            ds_ref[...] = jnp.zeros_like(ds_ref)

        prim = tuple(_stack_heads(r) for r in (st_ref, u_ref, w_ref, qk_ref, qd_ref, kt_ref, cd_ref))
        _, pull = jax.vjp(_gdn_scan_math, *prim)
        grads = pull((_stack_heads(do_ref), _stack_heads(ds_ref)))
        _store_heads(ds_ref, grads[0])
        for o_ref, val in zip(outs, grads[1:]):
            _store_heads(o_ref, val)

    blk = pl.BlockSpec((c, wide), lambda i: (nc - 1 - i, 0))
    return pl.pallas_call(
        body,
        grid=(nc,),
        in_specs=[blk] * 8,
        out_specs=[blk] * 6,
        out_shape=[jax.ShapeDtypeStruct((t, wide), F32)] * 6,
        scratch_shapes=[pltpu.VMEM((GDN_DK, wide), F32)],
        compiler_params=_cparams(("arbitrary",)),
        name=name,
    )(*prep, states, do)


def _gdn_outgate_math(o, z, nw):
    r = lax.rsqrt(jnp.mean(o * o, axis=-1, keepdims=True) + RMS_EPS)
    return o * r * nw * _silu(z)


def _gdn_outgate_fwd(o, pm, nw_row, *, name):
    t = o.shape[0]
    tb = min(t, 1024)
    z_off = GDN_QKV // LANES

    def body(o_ref, z_ref, nw_ref, y_ref):
        y_ref[...] = _gdn_outgate_math(o_ref[...], z_ref[...], nw_ref[...]).astype(y_ref.dtype)

    return pl.pallas_call(
        body,
        grid=(t // tb, GDN_HEADS),
        in_specs=[pl.BlockSpec((tb, LANES), lambda i, h: (i, h)), pl.BlockSpec((tb, LANES), lambda i, h: (i, h + z_off)),
                  pl.BlockSpec((1, LANES), lambda i, h: (0, 0))],
        out_specs=pl.BlockSpec((tb, LANES), lambda i, h: (i, h)),
        out_shape=jax.ShapeDtypeStruct((t, GDN_HEADS * LANES), MM_DTYPE),
        compiler_params=_cparams(("parallel", "parallel")),
        name=name,
    )(o, pm, nw_row)


def _gdn_outgate_bwd(o, pm, nw_row, dy, *, name):
    t = o.shape[0]
    tb = min(t, 1024)
    z_off = GDN_QKV // LANES

    def body(o_ref, z_ref, nw_ref, dy_ref, do_ref, dz_ref, dnw_ref):
        _, pull = jax.vjp(_gdn_outgate_math, o_ref[...], z_ref[...], nw_ref[...])
        d_o, d_z, d_nw = pull(dy_ref[...])
        do_ref[...] = d_o
        dz_ref[...] = d_z.astype(dz_ref.dtype)
        first = (pl.program_id(0) == 0) & (pl.program_id(1) == 0)

        @pl.when(first)
        def _():
            dnw_ref[...] = d_nw

        @pl.when(jnp.logical_not(first))
        def _():
            dnw_ref[...] += d_nw

    blk = pl.BlockSpec((tb, LANES), lambda i, h: (i, h))
    one = pl.BlockSpec((1, LANES), lambda i, h: (0, 0))
    return pl.pallas_call(
        body,
        grid=(t // tb, GDN_HEADS),
        in_specs=[blk, pl.BlockSpec((tb, LANES), lambda i, h: (i, h + z_off)), one, blk],
        out_specs=[blk, blk, one],
        out_shape=[jax.ShapeDtypeStruct((t, GDN_HEADS * LANES), F32),
                   jax.ShapeDtypeStruct((t, GDN_HEADS * LANES), MM_DTYPE), jax.ShapeDtypeStruct((1, LANES), F32)],
        compiler_params=_cparams(("arbitrary", "arbitrary")),
        name=name,
    )(o, pm, nw_row, dy)


def _rms64(x, w_row):
    return x * lax.rsqrt(jnp.sum(x * x, axis=-1, keepdims=True) * (1.0 / DIL_DH) + RMS_EPS) * w_row


def _alibi_slopes(group):
    head = lax.broadcasted_iota(jnp.int32, (DIL_HEADS, 8, LANES), 0).astype(F32)
    rate = -math.log(2.0) * ALIBI_MAX_BIAS / (len(DIL_GROUPS) * DIL_HEADS)
    slope = jnp.exp(rate * (head + float(group * DIL_HEADS + 1)))
    return jnp.broadcast_to(slope[:, 0:1, :], (DIL_HEADS, DIL_SPAN, LANES))


def _band_logits(qn, kp, kc, slope_d, has_prev):
    qi = lax.broadcasted_iota(jnp.int32, (DIL_SPAN, DIL_SPAN), 0)
    kj = lax.broadcasted_iota(jnp.int32, (DIL_SPAN, DIL_SPAN), 1)
    steps_c = (qi - kj).astype(F32)
    scale = DIL_DH ** -0.5
    sp = _bdot_nt(qn, kp) * scale - slope_d * (steps_c + float(DIL_SPAN))
    sc = _bdot_nt(qn, kc) * scale - slope_d * steps_c
    sp = jnp.where((kj >= qi) & has_prev, sp, NEG)
    sc = jnp.where(kj <= qi, sc, NEG)
    return sp, sc


def _dil_attn_fwd(slab, wq_row, wk_row, *, group, name):
    dilation = DIL_GROUPS[group][1]
    t = slab.shape[0]
    rows = t // dilation
    nlb = rows // DIL_SPAN
    wide = DIL_HEADS * LANES
    view = slab.reshape(rows, dilation * DIL_SLAB)

    def body(q_ref, kc_ref, vc_ref, kp_ref, vp_ref, wq_ref, wk_ref, o_ref):
        has_prev = pl.program_id(1) > 0
        lane = lax.broadcasted_iota(jnp.int32, (DIL_SPAN, LANES), 1)
        qn = _rms64(_stack_heads(q_ref), wq_ref[...])
        kc = _rms64(_stack_heads(kc_ref), wk_ref[...])
        kp = _rms64(_stack_heads(kp_ref), wk_ref[...])
        sp, sc = _band_logits(qn, kp, kc, _alibi_slopes(group) * float(dilation), has_prev)
        m = jnp.maximum(jnp.max(sp, axis=-1, keepdims=True), jnp.max(sc, axis=-1, keepdims=True))
        pp = jnp.exp(sp - m)
        pc = jnp.exp(sc - m)
        l = jnp.sum(pp, axis=-1, keepdims=True) + jnp.sum(pc, axis=-1, keepdims=True)
        o = (_bdot(pp, _stack_heads(vp_ref)) + _bdot(pc, _stack_heads(vc_ref))) / l
        _store_heads(o_ref, jnp.where(lane < DIL_DH, o, m + jnp.log(l)))

    cur = lambda part: pl.BlockSpec((DIL_SPAN, wide), lambda r, i: (i, 3 * r + part))
    prv = lambda part: pl.BlockSpec((DIL_SPAN, wide), lambda r, i: (jnp.maximum(i - 1, 0), 3 * r + part))
    one = pl.BlockSpec((1, LANES), lambda r, i: (0, 0))
    out = pl.pallas_call(
        body,
        grid=(dilation, nlb),
        in_specs=[cur(0), cur(1), cur(2), prv(1), prv(2), one, one],
        out_specs=pl.BlockSpec((DIL_SPAN, wide), lambda r, i: (i, r)),
        out_shape=jax.ShapeDtypeStruct((rows, dilation * wide), F32),
        compiler_params=_cparams(("parallel", "parallel")),
        name=name,
    )(view, view, view, view, view, wq_row, wk_row)
    return out.reshape(t, wide)


def _dil_merge_fwd(oe, *, name):
    t = oe[0].shape[0]
    tb = min(t, 1024)

    def body(e0, e1, e2, y_ref, om_ref):
        lane = lax.broadcasted_iota(jnp.int32, (tb, LANES), 1)
        es = [e0[...], e1[...], e2[...]]
        lse = [jnp.sum(jnp.where(lane == DIL_DH, e, 0.0), axis=-1, keepdims=True) for e in es]
        top = jnp.maximum(jnp.maximum(lse[0], lse[1]), lse[2])
        joint = top + jnp.log(jnp.exp(lse[0] - top) + jnp.exp(lse[1] - top) + jnp.exp(lse[2] - top))
        o = sum(jnp.exp(l - joint) * e for l, e in zip(lse, es))
        y_ref[...] = jnp.where(lane < DIL_DH, o, 0.0).astype(y_ref.dtype)
        om_ref[...] = jnp.where(lane < DIL_DH, o, joint)

    blk = pl.BlockSpec((tb, LANES), lambda i, h: (i, h))
    return pl.pallas_call(
        body,
        grid=(t // tb, DIL_HEADS),
        in_specs=[blk] * 3,
        out_specs=[blk, blk],
        out_shape=[jax.ShapeDtypeStruct((t, DIL_HEADS * LANES), MM_DTYPE),
                   jax.ShapeDtypeStruct((t, DIL_HEADS * LANES), F32)],
        compiler_params=_cparams(("parallel", "parallel")),
        name=name,
    )(*oe)


def _dil_merge_bwd(dy, om, *, name):
    t = dy.shape[0]
    tb = min(t, 1024)

    def body(dy_ref, om_ref, st_ref):
        lane = lax.broadcasted_iota(jnp.int32, (tb, LANES), 1)
        d_o = jnp.where(lane < DIL_DH, dy_ref[...], 0.0)
        om_t = om_ref[...]
        delta = jnp.sum(d_o * om_t, axis=-1, keepdims=True)
        st_ref[...] = jnp.where(lane < DIL_DH, d_o, jnp.where(lane == DIL_DH, om_t, jnp.where(lane == DIL_DH + 1, delta, 0.0)))

    blk = pl.BlockSpec((tb, LANES), lambda i, h: (i, h))
    return pl.pallas_call(
        body,
        grid=(t // tb, DIL_HEADS),
        in_specs=[blk, blk],
        out_specs=blk,
        out_shape=jax.ShapeDtypeStruct((t, DIL_HEADS * LANES), F32),
        compiler_params=_cparams(("parallel", "parallel")),
        name=name,
    )(dy, om)


def _rms64_bwd(x, w_row, dy):
    r = lax.rsqrt(jnp.sum(x * x, axis=-1, keepdims=True) * (1.0 / DIL_DH) + RMS_EPS)
    gw = dy * w_row
    dx = r * gw - x * (r * r * r * jnp.sum(gw * x, axis=-1, keepdims=True) * (1.0 / DIL_DH))
    return dx, dy * x * r


def _dil_attn_bwd(slab, stat, wq_row, wk_row, dwq_in, dwk_in, *, group, name):
    dilation = DIL_GROUPS[group][1]
    t = slab.shape[0]
    rows = t // dilation
    nlb = rows // DIL_SPAN
    wide = DIL_HEADS * LANES
    view = slab.reshape(rows, dilation * DIL_SLAB)
    stat_view = stat.reshape(rows, dilation * wide)

    def body(cur_ref, kp_ref, vp_ref, st_ref, wq_ref, wk_ref, dwq_in_ref, dwk_in_ref, d_ref, dwq_ref, dwk_ref,
             dk_carry, dv_carry):
        step = pl.program_id(1)
        has_prev = step < nlb - 1
        first = (pl.program_id(0) == 0) & (step == 0)

        @pl.when(step == 0)
        def _():
            dk_carry[...] = jnp.zeros_like(dk_carry)
            dv_carry[...] = jnp.zeros_like(dv_carry)

        @pl.when(first)
        def _():
            dwq_ref[...] = dwq_in_ref[...]
            dwk_ref[...] = dwk_in_ref[...]

        lane = lax.broadcasted_iota(jnp.int32, (DIL_SPAN, LANES), 1)
        scale = DIL_DH ** -0.5
        q_raw = _stack_heads(cur_ref)
        kc_raw = _stack_heads(cur_ref, first=DIL_HEADS)
        vc = _stack_heads(cur_ref, first=2 * DIL_HEADS)
        kp_raw = _stack_heads(kp_ref)
        vp = _stack_heads(vp_ref)
        st = _stack_heads(st_ref)
        d_o = jnp.where(lane < DIL_DH, st, 0.0)
        lse = jnp.sum(jnp.where(lane == DIL_DH, st, 0.0), axis=-1, keepdims=True)
        delta = jnp.sum(jnp.where(lane == DIL_DH + 1, st, 0.0), axis=-1, keepdims=True)
        qn = _rms64(q_raw, wq_ref[...])
        kc = _rms64(kc_raw, wk_ref[...])
        kp = _rms64(kp_raw, wk_ref[...])
        sp, sc = _band_logits(qn, kp, kc, _alibi_slopes(group) * float(dilation), has_prev)
        pp = jnp.exp(sp - lse)
        pc = jnp.exp(sc - lse)
        dsp = pp * (_bdot_nt(d_o, vp) - delta) * scale
        dsc = pc * (_bdot_nt(d_o, vc) - delta) * scale
        dqn = _bdot(dsp, kp) + _bdot(dsc, kc)
        dkc_n = _bdot_tn(dsc, qn) + _stack_heads(dk_carry)
        dvc = _bdot_tn(pc, d_o) + _stack_heads(dv_carry)
        _store_heads(dk_carry, _bdot_tn(dsp, qn))
        _store_heads(dv_carry, _bdot_tn(pp, d_o))
        dq_raw, dwq_rows = _rms64_bwd(q_raw, wq_ref[...], dqn)
        dk_raw, dwk_rows = _rms64_bwd(kc_raw, wk_ref[...], dkc_n)
        _store_heads(d_ref, dq_raw)
        _store_heads(d_ref, dk_raw, first=DIL_HEADS)
        _store_heads(d_ref, dvc, first=2 * DIL_HEADS)
        dwq_ref[...] += jnp.sum(jnp.sum(dwq_rows, axis=0), axis=0, keepdims=True)
        dwk_ref[...] += jnp.sum(jnp.sum(dwk_rows, axis=0), axis=0, keepdims=True)

    blk_i = lambda i: nlb - 1 - i
    cur = pl.BlockSpec((DIL_SPAN, DIL_SLAB), lambda r, i: (blk_i(i), r))
    prv = lambda part: pl.BlockSpec((DIL_SPAN, wide), lambda r, i: (jnp.maximum(blk_i(i) - 1, 0), 3 * r + part))
    one = pl.BlockSpec((1, LANES), lambda r, i: (0, 0))
    dslab, dwq, dwk = pl.pallas_call(
        body,
        grid=(dilation, nlb),
        in_specs=[cur, prv(1), prv(2), pl.BlockSpec((DIL_SPAN, wide), lambda r, i: (blk_i(i), r)), one, one, one, one],
        out_specs=[cur, one, one],
        out_shape=[jax.ShapeDtypeStruct((rows, dilation * DIL_SLAB), MM_DTYPE), jax.ShapeDtypeStruct((1, LANES), F32),
                   jax.ShapeDtypeStruct((1, LANES), F32)],
        scratch_shapes=[pltpu.VMEM((DIL_SPAN, wide), F32), pltpu.VMEM((DIL_SPAN, wide), F32)],
        compiler_params=_cparams(("arbitrary", "arbitrary")),
        name=name,
    )(view, view, view, stat_view, wq_row, wk_row, dwq_in, dwk_in)
    return dslab.reshape(t, DIL_SLAB), dwq, dwk


def _row(v, width=LANES):
    v = v.astype(F32).reshape(-1)
    return jnp.pad(v, (0, width - v.shape[0])).reshape(1, width)


def _prepare_weights(w):
    return dict(gdn=_prepare_gdn(w), dil=_prepare_dil(w), ffn=_prepare_ffn(w))


def _prepare_gdn(w, layers=range(DEPTH // 2)):
    gdn = {}
    for j in layers:
        wt = w["gdn_w_in"][j]
        gates_t = jnp.pad(wt[GDN_MAIN:], ((0, LANES - 2 * GDN_HEADS), (0, 0)))
        gdn[j] = dict(in_t=wt, gates_t=gates_t, out=w["gdn_w_out"][j], conv=w["gdn_conv_w"][j].astype(F32),
                      alog=_row(w["gdn_a_log"][j]), dt=_row(w["gdn_dt_bias"][j]), nw=_row(w["gdn_norm_w"][j]))
    return gdn


def _prepare_dil(w, layers=range(DEPTH // 2)):
    d = D_MODEL
    dil = {}
    for j in layers:
        wt = w["dil_w_in"][j].reshape(3, len(DIL_GROUPS), DIL_HEADS, DIL_DH, d)
        wg_t = [jnp.pad(wt[:, g], ((0, 0), (0, 0), (0, LANES - DIL_DH), (0, 0))).reshape(DIL_SLAB, d)
                for g in range(len(DIL_GROUPS))]
        out_t = jnp.pad(w["dil_w_out"][j].reshape(d, DIL_HEADS, DIL_DH), ((0, 0), (0, 0), (0, LANES - DIL_DH)))
        dil[j] = dict(wg_t=wg_t, out_t=out_t.reshape(d, DIL_HEADS * LANES), wq=_row(w["dil_q_norm"][j]),
                      wk=_row(w["dil_k_norm"][j]))
    return dil


def _prepare_ffn(w, layers=range(DEPTH)):
    return {i: dict(in_t=w["ffn_w_in"][i], out=w["ffn_w_out"][i]) for i in layers}


def _gdn_layer_fwd(x, nrow, p):
    hn = _rmsnorm_fwd(x, nrow, name="rmsnorm_fwd")
    pm = _matmul(hn, p["in_t"], trans_b=True, b_rows=(0, GDN_MAIN), name="gdn_proj_main")
    ab = _matmul(hn, p["gates_t"], trans_b=True, name="gdn_proj_gates")
    qkv = _gdn_conv_fwd(pm, p["conv"], name="gdn_conv_fwd")
    gb, bb = _gdn_gates_fwd(ab, p["alog"], p["dt"], name="gdn_gates_fwd")
    prep = _gdn_prep_fwd(qkv, gb, bb, name="gdn_prep_fwd")
    o, states = _gdn_scan_fwd(prep, name="gdn_scan_fwd")
    og = _gdn_outgate_fwd(o, pm, p["nw"], name="gdn_outgate_fwd")
    y = _matmul(og, p["out"], add=x, name="gdn_proj_out")
    return y, (x, hn, pm, ab, qkv, gb, bb, prep, states, o, og)


def _gdn_layer_bwd(dx, dxb, nrow, p, saved):
    x, hn, pm, ab, qkv, gb, bb, prep, states, o, og = saved
    d_og = _matmul(dxb, p["out"], trans_b=True, name="gdn_dgate")
    g_out = _matmul(og, dxb, trans_a=True, name="gdn_gw_out")
    d_o, d_z, d_nw = _gdn_outgate_bwd(o, pm, p["nw"], d_og, name="gdn_outgate_bwd")
    cts = _gdn_scan_bwd(prep, states, d_o, name="gdn_scan_bwd")
    dqkv, dgb, dbb = _gdn_prep_bwd(qkv, gb, bb, cts, name="gdn_prep_bwd")
    d_ab, d_alog, d_dt = _gdn_gates_bwd(ab, p["alog"], p["dt"], dgb, dbb, name="gdn_gates_bwd")
    d_conv, g_conv = _gdn_conv_bwd(pm, p["conv"], dqkv, name="gdn_conv_bwd")
    d_hn = _matmul(d_conv, p["in_t"], b_rows=(0, GDN_QKV), name="gdn_dhn_qkv")
    d_hn = _matmul(d_z, p["in_t"], b_rows=(GDN_QKV, GDN_MAIN - GDN_QKV), add=d_hn, name="gdn_dhn_z")
    d_hn = _matmul(d_ab, p["gates_t"], add=d_hn, name="gdn_dhn_gates")
    g_in_t = jnp.concatenate([
        _matmul(d_conv, hn, trans_a=True, name="gdn_gw_qkv"),
        _matmul(d_z, hn, trans_a=True, name="gdn_gw_z"),
        _matmul(d_ab, hn, trans_a=True, name="gdn_gw_gates")[:2 * GDN_HEADS],
    ], axis=0)
    dx_new, dxb_new, g_norm = _rmsnorm_bwd(x, nrow, d_hn, dx, name="rmsnorm_bwd")
    grads = dict(w_in=g_in_t, conv=g_conv, a_log=d_alog[0, :GDN_HEADS], dt_bias=d_dt[0, :GDN_HEADS], norm_w=d_nw[0],
                 w_out=g_out, norm=g_norm[0])
    return dx_new, dxb_new, grads


def _dil_layer_fwd(x, nrow, p):
    hn = _rmsnorm_fwd(x, nrow, name="rmsnorm_fwd")
    slabs = [_matmul(hn, p["wg_t"][g], trans_b=True, name="dil_proj_in") for g in range(len(DIL_GROUPS))]
    oe = [_dil_attn_fwd(slabs[g], p["wq"], p["wk"], group=g, name=f"dil_attn_fwd_g{g}") for g in range(len(DIL_GROUPS))]
    y, om = _dil_merge_fwd(oe, name="dil_merge_fwd")
    out = _matmul(y, p["out_t"], trans_b=True, add=x, name="dil_proj_out")
    return out, (x, hn, slabs, y, om)


def _dil_layer_bwd(dx, dxb, nrow, p, saved):
    x, hn, slabs, y, om = saved
    d_y = _matmul(dxb, p["out_t"], name="dil_dmerged")
    g_out_t = _matmul(dxb, y, trans_a=True, name="dil_gw_out")
    g_out_t = g_out_t.reshape(D_MODEL, DIL_HEADS, LANES)[..., :DIL_DH].reshape(D_MODEL, DIL_HEADS * DIL_DH)
    stat = _dil_merge_bwd(d_y, om, name="dil_merge_bwd")
    d_hn = None
    dwq = jnp.zeros((1, LANES), F32)
    dwk = jnp.zeros((1, LANES), F32)
    g_groups = []
    for g in range(len(DIL_GROUPS)):
        dslab, dwq, dwk = _dil_attn_bwd(slabs[g], stat, p["wq"], p["wk"], dwq, dwk, group=g, name=f"dil_attn_bwd_g{g}")
        d_hn = _matmul(dslab, p["wg_t"][g], add=d_hn, name="dil_dhn")
        g_w = _matmul(dslab, hn, trans_a=True, name="dil_gw_in")
        g_groups.append(g_w.reshape(3, DIL_HEADS, LANES, D_MODEL)[:, :, :DIL_DH])
    g_in_t = jnp.stack(g_groups, axis=1).reshape(3 * len(DIL_GROUPS) * DIL_HEADS * DIL_DH, D_MODEL)
    dx_new, dxb_new, g_norm = _rmsnorm_bwd(x, nrow, d_hn, dx, name="rmsnorm_bwd")
    grads = dict(w_in=g_in_t, q_norm=dwq[0, :DIL_DH], k_norm=dwk[0, :DIL_DH], w_out=g_out_t, norm=g_norm[0])
    return dx_new, dxb_new, grads


def _ffn_layer_fwd(x, nrow, p):
    hn = _rmsnorm_fwd(x, nrow, name="rmsnorm_fwd")
    gate, up, act = _ffn_in(hn, p["in_t"], name="ffn_proj_in")
    y = _matmul(act, p["out"], add=x, name="ffn_proj_out")
    return y, (x, hn, gate, up, act)


def _ffn_layer_bwd(dx, dxb, nrow, p, saved):
    x, hn, gate, up, act = saved
    g_out = _matmul(act, dxb, trans_a=True, name="ffn_gw_out")
    d_g, d_u = _ffn_dact(dxb, p["out"], gate, up, name="ffn_dact")
    d_hn = _matmul(d_g, p["in_t"], b_rows=(0, FFN_HIDDEN), name="ffn_dhn_gate")
    d_hn = _matmul(d_u, p["in_t"], b_rows=(FFN_HIDDEN, FFN_HIDDEN), add=d_hn, name="ffn_dhn_up")
    g_in_t = jnp.concatenate([_matmul(d_g, hn, trans_a=True, name="ffn_gw_gate"),
                              _matmul(d_u, hn, trans_a=True, name="ffn_gw_up")], axis=0)
    dx_new, dxb_new, g_norm = _rmsnorm_bwd(x, nrow, d_hn, dx, name="rmsnorm_bwd")
    return dx_new, dxb_new, dict(w_in=g_in_t, w_out=g_out, norm=g_norm[0])


def _mixer_fwd(i, x, mix_row, prepared):
    if i % 2 == 0:
        return _gdn_layer_fwd(x, mix_row, prepared["gdn"][i // 2])
    return _dil_layer_fwd(x, mix_row, prepared["dil"][i // 2])


def _mixer_bwd(i, dx, dxb, mix_row, prepared, saved):
    if i % 2 == 0:
        return _gdn_layer_bwd(dx, dxb, mix_row, prepared["gdn"][i // 2], saved)
    return _dil_layer_bwd(dx, dxb, mix_row, prepared["dil"][i // 2], saved)


def _local_step(x, target, prepared, norm_mix, norm_ffn):
    saved = []
    for i in range(DEPTH):
        x, s_mix = _mixer_fwd(i, x, norm_mix[i].reshape(1, D_MODEL), prepared)
        x, s_ffn = _ffn_layer_fwd(x, norm_ffn[i].reshape(1, D_MODEL), prepared["ffn"][i])
        saved.append((s_mix, s_ffn))
    dx, dxb, loss = _loss_head(x, target, name="loss_head")
    g_mix, g_ffn = [None] * DEPTH, [None] * DEPTH
    for i in reversed(range(DEPTH)):
        s_mix, s_ffn = saved[i]
        dx, dxb, g_ffn[i] = _ffn_layer_bwd(dx, dxb, norm_ffn[i].reshape(1, D_MODEL), prepared["ffn"][i], s_ffn)
        dx, dxb, g_mix[i] = _mixer_bwd(i, dx, dxb, norm_mix[i].reshape(1, D_MODEL), prepared, s_mix)
    return loss[0, 0], dx, _collect_grads(g_mix, g_ffn)


def _collect_grads(g_mix, g_ffn):
    gdn = [g_mix[i] for i in range(0, DEPTH, 2)]
    dil = [g_mix[i] for i in range(1, DEPTH, 2)]
    if any(g is None for g in g_mix + g_ffn):
        pick = lambda gs, key: [None if g is None else g[key] for g in gs]
        return dict(gdn_w_in=pick(gdn, "w_in"), gdn_w_out=pick(gdn, "w_out"), dil_w_in=pick(dil, "w_in"),
                    dil_w_out=pick(dil, "w_out"), ffn_w_in=pick(g_ffn, "w_in"), ffn_w_out=pick(g_ffn, "w_out"))
    grads = dict(
        norm_mix=jnp.stack([g["norm"] for g in g_mix]),
        norm_ffn=jnp.stack([g["norm"] for g in g_ffn]),
        gdn_w_in=[g["w_in"] for g in gdn],
        gdn_conv_w=jnp.stack([g["conv"] for g in gdn]),
        gdn_a_log=jnp.stack([g["a_log"] for g in gdn]),
        gdn_dt_bias=jnp.stack([g["dt_bias"] for g in gdn]),
        gdn_norm_w=jnp.stack([g["norm_w"] for g in gdn]),
        gdn_w_out=[g["w_out"] for g in gdn],
        dil_w_in=[g["w_in"] for g in dil],
        dil_q_norm=jnp.stack([g["q_norm"] for g in dil]),
        dil_k_norm=jnp.stack([g["k_norm"] for g in dil]),
        dil_w_out=[g["w_out"] for g in dil],
        ffn_w_in=[g["w_in"] for g in g_ffn],
        ffn_w_out=[g["w_out"] for g in g_ffn],
    )
    return grads


MESH_ID = pl.DeviceIdType.MESH
ANY_SPACE = pl.BlockSpec(memory_space=pl.ANY)


def _mesh_position():
    return lax.axis_index("x"), lax.axis_index("y"), lax.axis_index("c")


def _flip(pos, k):
    x, y, c = pos
    return (1 - x if k & 4 else x, 1 - y if k & 2 else y, 1 - c if k & 1 else c)


def _linear(pos):
    return 4 * pos[0] + 2 * pos[1] + pos[2]


def _comm_scratch():
    return [pltpu.SemaphoreType.DMA((N_DEV - 1,)), pltpu.SemaphoreType.DMA((N_DEV - 1,)), pltpu.SemaphoreType.DMA(())]


def _all_gather(shard, *, name):
    def body(x_ref, out_ref, send_sems, recv_sems, local_sem):
        me = _mesh_position()
        mine = out_ref.at[_linear(me)]
        local = pltpu.make_async_copy(x_ref, mine, local_sem)
        local.start()
        copies = []
        for k in range(1, N_DEV):
            cp = pltpu.make_async_remote_copy(src_ref=x_ref, dst_ref=mine, send_sem=send_sems.at[k - 1],
                                              recv_sem=recv_sems.at[k - 1], device_id=_flip(me, k), device_id_type=MESH_ID)
            cp.start()
            copies.append(cp)
        for cp in copies:
            cp.wait()
        local.wait()

    return pl.pallas_call(
        body,
        out_shape=jax.ShapeDtypeStruct((N_DEV,) + shard.shape, shard.dtype),
        in_specs=[ANY_SPACE],
        out_specs=ANY_SPACE,
        scratch_shapes=_comm_scratch(),
        name=name,
    )(shard)


def _exchange(parts, *, name):
    def body(p_ref, out_ref, send_sems, recv_sems, local_sem):
        me = _mesh_position()
        mine = out_ref.at[_linear(me)]
        local = pltpu.make_async_copy(p_ref.at[_linear(me)], mine, local_sem)
        local.start()
        copies = []
        for k in range(1, N_DEV):
            peer = _flip(me, k)
            cp = pltpu.make_async_remote_copy(src_ref=p_ref.at[_linear(peer)], dst_ref=mine, send_sem=send_sems.at[k - 1],
                                              recv_sem=recv_sems.at[k - 1], device_id=peer, device_id_type=MESH_ID)
            cp.start()
            copies.append(cp)
        for cp in copies:
            cp.wait()
        local.wait()

    return pl.pallas_call(
        body,
        out_shape=jax.ShapeDtypeStruct(parts.shape, parts.dtype),
        in_specs=[ANY_SPACE],
        out_specs=ANY_SPACE,
        scratch_shapes=_comm_scratch(),
        name=name,
    )(parts)


HBM_SPACE = pl.BlockSpec(memory_space=pltpu.HBM)
SEM_SPACE = pl.BlockSpec(memory_space=pltpu.SEMAPHORE)
DATAFLOW = pltpu.SideEffectType.DATAFLOW_SIDE_EFFECTING


def _split_copies(src_ref, land_ref, send_sems, recv_sems, per_peer):
    me = _mesh_position()
    mine = land_ref.at[_linear(me)]
    copies = []
    for k in range(1, N_DEV):
        peer = _flip(me, k)
        src = src_ref.at[_linear(peer)] if per_peer else src_ref
        copies.append(pltpu.make_async_remote_copy(src_ref=src, dst_ref=mine, send_sem=send_sems.at[k - 1],
                                                   recv_sem=recv_sems.at[k - 1], device_id=peer, device_id_type=MESH_ID))
    return copies


def _travel_start(src, after, *, per_peer, name):
    me = _linear(_mesh_position())
    own = src[me] if per_peer else src
    shape = own.shape
    landing = lax.dynamic_update_slice(lax.empty((N_DEV,) + shape, src.dtype), own[None], (me, 0, 0))

    def body(src_ref, land_ref, after_ref, send_sems, recv_sems, src_thru, land_thru, token):
        for cp in _split_copies(src_ref, land_ref, send_sems, recv_sems, per_peer):
            cp.start()
        token[...] = jnp.zeros_like(token)

    return pl.pallas_call(
        body,
        name=name,
        out_shape=(pltpu.SemaphoreType.DMA((N_DEV - 1,)), pltpu.SemaphoreType.DMA((N_DEV - 1,)),
                   pltpu.HBM(src.shape, src.dtype), pltpu.HBM(landing.shape, landing.dtype),
                   jax.ShapeDtypeStruct((8, LANES), F32)),
        in_specs=(HBM_SPACE, HBM_SPACE, ANY_SPACE),
        out_specs=(SEM_SPACE, SEM_SPACE, HBM_SPACE, HBM_SPACE, pl.BlockSpec(memory_space=pltpu.VMEM)),
        input_output_aliases={0: 2, 1: 3},
        compiler_params=pltpu.CompilerParams(has_side_effects=DATAFLOW),
    )(pltpu.with_memory_space_constraint(src, pltpu.HBM), pltpu.with_memory_space_constraint(landing, pltpu.HBM), after)


def _travel_wait(started, after, *, per_peer, name):
    send_sems, recv_sems, src_thru, land_thru, _ = started

    def body(src_ref, land_ref, send_sems, recv_sems, after_ref, src_dead, got_ref):
        for cp in _split_copies(src_ref, land_ref, send_sems, recv_sems, per_peer):
            cp.wait_send()
            cp.wait_recv()

    return pl.pallas_call(
        body,
        name=name,
        out_shape=(pltpu.HBM(src_thru.shape, src_thru.dtype), pltpu.HBM(land_thru.shape, land_thru.dtype)),
        in_specs=(HBM_SPACE, HBM_SPACE, SEM_SPACE, SEM_SPACE, ANY_SPACE),
        out_specs=(HBM_SPACE, HBM_SPACE),
        input_output_aliases={0: 0, 1: 1},
        compiler_params=pltpu.CompilerParams(has_side_effects=DATAFLOW),
    )(src_thru, land_thru, send_sems, recv_sems, after)[1]


def _adamw(parts, w, m, v, *, name):
    rows, n = w.shape
    tb = _pick(rows, (PACK_ROW_ALIGN, 16))
    c1 = 1.0 - ADAM_B1 ** ADAM_STEP
    c2 = 1.0 - ADAM_B2 ** ADAM_STEP

    def body(p_ref, w_ref, m_ref, v_ref, g_ref, d_ref, nm_ref, nv_ref):
        g = p_ref[0].astype(F32)
        for s in range(1, N_DEV):
            g = g + p_ref[s].astype(F32)
        m_new = ADAM_B1 * m_ref[...] + (1.0 - ADAM_B1) * g
        v_new = ADAM_B2 * v_ref[...] + (1.0 - ADAM_B2) * (g * g)
        m_hat = m_new / c1
        v_hat = v_new / c2
        g_ref[...] = g
        nm_ref[...] = m_new
        nv_ref[...] = v_new
        d_ref[...] = -ADAM_LR * (m_hat / (jnp.sqrt(v_hat) + ADAM_EPS) + ADAM_WD * w_ref[...])

    blk = pl.BlockSpec((tb, n), lambda i: (i, 0))
    return pl.pallas_call(
        body,
        grid=(rows // tb,),
        in_specs=[pl.BlockSpec((N_DEV, tb, n), lambda i: (0, i, 0)), blk, blk, blk],
        out_specs=[blk] * 4,
        out_shape=[jax.ShapeDtypeStruct((rows, n), F32)] * 4,
        compiler_params=_cparams(("parallel",)),
        name=name,
    )(parts, w, m, v)


PACK_WIDTH = 1024
SHARDED = {
    "gdn_w_in": ((2, D_MODEL, GDN_IN_WIDTH), 2),
    "gdn_conv_w": ((2, GDN_CONV, GDN_QKV), 2),
    "gdn_w_out": ((2, GDN_HEADS * GDN_DV, D_MODEL), 1),
    "dil_w_in": ((2, D_MODEL, 3 * len(DIL_GROUPS) * DIL_HEADS * DIL_DH), 2),
    "dil_w_out": ((2, DIL_HEADS * DIL_DH, D_MODEL), 2),
    "ffn_w_in": ((DEPTH, D_MODEL, 2 * FFN_HIDDEN), 2),
    "ffn_w_out": ((DEPTH, FFN_HIDDEN, D_MODEL), 1),
}
REPLICATED = {"norm_mix": (DEPTH, D_MODEL), "norm_ffn": (DEPTH, D_MODEL), "gdn_a_log": (2, GDN_HEADS),
              "gdn_dt_bias": (2, GDN_HEADS), "gdn_norm_w": (2, GDN_DV), "dil_q_norm": (2, DIL_DH), "dil_k_norm": (2, DIL_DH)}
WEIGHT_ORDER = ("norm_mix", "norm_ffn", "gdn_w_in", "gdn_conv_w", "gdn_a_log", "gdn_dt_bias", "gdn_norm_w", "gdn_w_out",
                "dil_w_in", "dil_q_norm", "dil_k_norm", "dil_w_out", "ffn_w_in", "ffn_w_out")
PACK_ROW_ALIGN = 128
PIECE_ALIGN = 16
SMALL_ROWS = 16


def _shard_shape(name):
    shape, axis = SHARDED[name]
    return tuple(s // N_DEV if i == axis else s for i, s in enumerate(shape))


def _shard_rows(name):
    return math.prod(_shard_shape(name)) // PACK_WIDTH


def _split_shards(full, name):
    shape, axis = SHARDED[name]
    split = full.reshape(shape[:axis] + (N_DEV, shape[axis] // N_DEV) + shape[axis + 1:])
    return jnp.moveaxis(split, axis, 0)


def _join_shards(stacked, name):
    shape, axis = SHARDED[name]
    return jnp.moveaxis(stacked, 0, axis).reshape(shape)


COLUMN_SHARDED = ("gdn_w_in", "dil_w_in", "dil_w_out", "ffn_w_in")


def _to_rows(shard, name):
    if name in COLUMN_SHARDED:
        shard = jnp.swapaxes(shard, 1, 2)
    return shard.reshape(-1, PACK_WIDTH)


def _layer_columns(name):
    _, r, c = _shard_shape(name)
    return r if name in COLUMN_SHARDED else c


def _piece_rows(piece, halves=1):
    name, layer = piece
    rows = _shard_rows(name) * halves
    return rows if layer is None else rows // SHARDED[name][0][0]


def _aligned(rows, to=PIECE_ALIGN):
    return -(-rows // to) * to


def _pack_pieces(arrays, total_align=PIECE_ALIGN):
    padded, total = [], 0
    for a in arrays:
        rows = a.shape[-2]
        extra = _aligned(rows) - rows
        if extra:
            a = jnp.pad(a, [(0, 0)] * (a.ndim - 2) + [(0, extra), (0, 0)])
        padded.append(a)
        total += rows + extra
    tail = _aligned(total, total_align) - total
    if tail:
        padded.append(jnp.zeros(padded[0].shape[:-2] + (tail, PACK_WIDTH), padded[0].dtype))
    return jnp.concatenate(padded, axis=-2)


def _piece_offsets(pieces, halves=None):
    out, at = [], 0
    for p in pieces:
        rows = _piece_rows(p, (halves or {}).get(p[0], 1))
        out.append((p, at, rows))
        at += _aligned(rows)
    return out


def _shard_piece_rows(src, piece):
    name, layer = piece
    part = src[name] if layer is None else src[name][layer:layer + 1]
    return _to_rows(part.astype(F32), name)


def _piece_from_rows(rows, piece):
    name, layer = piece
    layers, r, c = _shard_shape(name)
    n_l = layers if layer is None else 1
    if name in COLUMN_SHARDED:
        return jnp.swapaxes(rows.reshape(n_l, c, r), 1, 2)
    return rows.reshape(n_l, r, c)


SMALL_TAIL = tuple(n for n in REPLICATED if n not in ("norm_mix", "norm_ffn"))


def _pack_small(vals):
    tail, at = jnp.zeros((PACK_WIDTH,), F32), 0
    for n in SMALL_TAIL:
        vec = vals[n].astype(F32).reshape(-1)
        tail = tail + jnp.pad(vec, (at, PACK_WIDTH - at - vec.shape[0]))
        at += vec.shape[0]
    buf = jnp.pad(vals["norm_mix"].astype(F32), ((0, SMALL_ROWS - DEPTH), (0, 0)))
    buf = buf + jnp.pad(vals["norm_ffn"].astype(F32), ((8, SMALL_ROWS - 8 - DEPTH), (0, 0)))
    return buf + jnp.pad(tail.reshape(1, PACK_WIDTH), ((SMALL_ROWS - 1, 0), (0, 0)))


def _unpack_small(buf):
    out = {"norm_mix": buf[0:DEPTH], "norm_ffn": buf[8:8 + DEPTH]}
    at = 0
    for n in SMALL_TAIL:
        size = math.prod(REPLICATED[n])
        out[n] = buf[SMALL_ROWS - 1, at:at + size].reshape(REPLICATED[n])
        at += size
    return out


GATHER_FIRST = (("gdn_w_in", 0), ("gdn_conv_w", None), ("gdn_w_out", 0))
GATHER_NEXT = (("ffn_w_in", 0), ("ffn_w_out", 0), ("dil_w_in", 0), ("dil_w_out", 0))
GATHER_LAST = (("ffn_w_in", 1), ("ffn_w_out", 1), ("gdn_w_in", 1), ("gdn_w_out", 1), ("ffn_w_in", 2), ("ffn_w_out", 2),
               ("dil_w_in", 1), ("dil_w_out", 1), ("ffn_w_in", 3), ("ffn_w_out", 3))
EXCHANGE_GROUPS = (
    (("ffn_w_in", 3), ("ffn_w_out", 3), ("dil_w_in", 1), ("dil_w_out", 1),
     ("ffn_w_in", 2), ("ffn_w_out", 2), ("gdn_w_in", 1), ("gdn_w_out", 1)),
    (("ffn_w_in", 1), ("ffn_w_out", 1), ("dil_w_in", 0), ("dil_w_out", 0)),
    (("ffn_w_in", 0), ("ffn_w_out", 0)),
    (("gdn_w_in", 0), ("gdn_w_out", 0), ("gdn_conv_w", None)),
)
EXCHANGE_AFTER = {("mix", 2): 0, ("mix", 1): 1, ("ffn", 0): 2}


def _gather_operand(w, pieces):
    arrays = []
    for n, layer in pieces:
        if layer is None:
            arrays.append(lax.bitcast_convert_type(w[n], BF16).reshape(-1, PACK_WIDTH))
        else:
            arrays.append(_to_rows(w[n][layer:layer + 1].astype(BF16), n))
    return _pack_pieces(arrays)


def _gathered_weights(gathered, pieces, full):
    for (n, layer), at, rows in _piece_offsets(pieces, halves={"gdn_conv_w": 2}):
        block = gathered[:, at:at + rows]
        if layer is None:
            block = lax.bitcast_convert_type(block.reshape((N_DEV,) + _shard_shape(n) + (2,)), F32)
            full[n] = _join_shards(block, n)
        else:
            full.setdefault(n, {})[layer] = block.reshape(-1, _layer_columns(n))
    return full


def _exchange_operand(grads, pieces):
    arrays = []
    for n, layer in pieces:
        if layer is None:
            arrays.append(_split_shards(grads[n], n).astype(BF16).reshape(N_DEV, -1, PACK_WIDTH))
        else:
            arrays.append(grads[n][layer].astype(BF16).reshape(N_DEV, -1, PACK_WIDTH))
    return _pack_pieces(arrays, total_align=PACK_ROW_ALIGN)


def _update_group(received, pieces, w, m, v, *, name):
    packed = [_pack_pieces([_shard_piece_rows(src, p) for p in pieces], total_align=PACK_ROW_ALIGN) for src in (w, m, v)]
    outs = _adamw(received, *packed, name=name)
    return {p: tuple(_piece_from_rows(o[at:at + rows], p) for o in outs) for p, at, rows in _piece_offsets(pieces)}


def kernel(x, norm_mix, norm_ffn, gdn_w_in, gdn_conv_w, gdn_a_log, gdn_dt_bias, gdn_norm_w, gdn_w_out, dil_w_in, dil_q_norm, dil_k_norm, dil_w_out, ffn_w_in, ffn_w_out, loss_target, m_norm_mix, m_norm_ffn, m_gdn_w_in, m_gdn_conv_w, m_gdn_a_log, m_gdn_dt_bias, m_gdn_norm_w, m_gdn_w_out, m_dil_w_in, m_dil_q_norm, m_dil_k_norm, m_dil_w_out, m_ffn_w_in, m_ffn_w_out, v_norm_mix, v_norm_ffn, v_gdn_w_in, v_gdn_conv_w, v_gdn_a_log, v_gdn_dt_bias, v_gdn_norm_w, v_gdn_w_out, v_dil_w_in, v_dil_q_norm, v_dil_k_norm, v_dil_w_out, v_ffn_w_in, v_ffn_w_out):
    w = dict(norm_mix=norm_mix, norm_ffn=norm_ffn, gdn_w_in=gdn_w_in, gdn_conv_w=gdn_conv_w, gdn_a_log=gdn_a_log,
             gdn_dt_bias=gdn_dt_bias, gdn_norm_w=gdn_norm_w, gdn_w_out=gdn_w_out, dil_w_in=dil_w_in, dil_q_norm=dil_q_norm,
             dil_k_norm=dil_k_norm, dil_w_out=dil_w_out, ffn_w_in=ffn_w_in, ffn_w_out=ffn_w_out)
    m = dict(norm_mix=m_norm_mix, norm_ffn=m_norm_ffn, gdn_w_in=m_gdn_w_in, gdn_conv_w=m_gdn_conv_w, gdn_a_log=m_gdn_a_log,
             gdn_dt_bias=m_gdn_dt_bias, gdn_norm_w=m_gdn_norm_w, gdn_w_out=m_gdn_w_out, dil_w_in=m_dil_w_in,
             dil_q_norm=m_dil_q_norm, dil_k_norm=m_dil_k_norm, dil_w_out=m_dil_w_out, ffn_w_in=m_ffn_w_in, ffn_w_out=m_ffn_w_out)
    v = dict(norm_mix=v_norm_mix, norm_ffn=v_norm_ffn, gdn_w_in=v_gdn_w_in, gdn_conv_w=v_gdn_conv_w, gdn_a_log=v_gdn_a_log,
             gdn_dt_bias=v_gdn_dt_bias, gdn_norm_w=v_gdn_norm_w, gdn_w_out=v_gdn_w_out, dil_w_in=v_dil_w_in,
             dil_q_norm=v_dil_q_norm, dil_k_norm=v_dil_k_norm, dil_w_out=v_dil_w_out, ffn_w_in=v_ffn_w_in, ffn_w_out=v_ffn_w_out)
    def row(src, i):
        return src[i].reshape(1, D_MODEL)

    first = _all_gather(_gather_operand(w, GATHER_FIRST), name="weight_all_gather_first")
    next_started = _travel_start(_gather_operand(w, GATHER_NEXT), first, per_peer=False, name="weight_gather_start_next")
    last_started = _travel_start(_gather_operand(w, GATHER_LAST), next_started[4], per_peer=False,
                                 name="weight_gather_start_last")
    full = _gathered_weights(first, GATHER_FIRST, {n: w[n] for n in REPLICATED})
    prepared = dict(gdn=_prepare_gdn(full, layers=(0,)))
    h = x[0]
    saved = [None] * DEPTH
    h, s_mix = _mixer_fwd(0, h, row(norm_mix, 0) + last_started[4][0, 0], prepared)
    got = _travel_wait(next_started, h, per_peer=False, name="weight_gather_wait_next")
    full = _gathered_weights(got, GATHER_NEXT, full)
    prepared.update(dil=_prepare_dil(full, layers=(0,)), ffn=_prepare_ffn(full, layers=(0,)))
    for i in range(DEPTH):
        if i > 0:
            h, s_mix = _mixer_fwd(i, h, row(norm_mix, i), prepared)
        if i == 1:
            got = _travel_wait(last_started, h, per_peer=False, name="weight_gather_wait_last")
            full = _gathered_weights(got, GATHER_LAST, full)
            prepared["gdn"].update(_prepare_gdn(full, layers=(1,)))
            prepared["dil"].update(_prepare_dil(full, layers=(1,)))
            prepared["ffn"].update(_prepare_ffn(full, layers=(1, 2, 3)))
        h, s_ffn = _ffn_layer_fwd(h, row(norm_ffn, i), prepared["ffn"][i])
        saved[i] = (s_mix, s_ffn)
    dx, dxb, loss = _loss_head(h, loss_target[0], name="loss_head")

    g_mix, g_ffn = [None] * DEPTH, [None] * DEPTH
    started = {}

    def travel(group, dxb):
        operand = _exchange_operand(_collect_grads(g_mix, g_ffn), EXCHANGE_GROUPS[group])
        started[group] = _travel_start(operand, dx, per_peer=True, name=f"grad_exchange_start_{group}")
        return dxb + started[group][4][0, 0].astype(dxb.dtype)

    for i in reversed(range(DEPTH)):
        s_mix, s_ffn = saved[i]
        dx, dxb, g_ffn[i] = _ffn_layer_bwd(dx, dxb, row(norm_ffn, i), prepared["ffn"][i], s_ffn)
        if ("ffn", i) in EXCHANGE_AFTER:
            dxb = travel(EXCHANGE_AFTER[("ffn", i)], dxb)
        dx, dxb, g_mix[i] = _mixer_bwd(i, dx, dxb, row(norm_mix, i), prepared, s_mix)
        if ("mix", i) in EXCHANGE_AFTER:
            dxb = travel(EXCHANGE_AFTER[("mix", i)], dxb)
    grads = _collect_grads(g_mix, g_ffn)
    received = [_travel_wait(started[g], dx, per_peer=True, name=f"grad_exchange_wait_{g}") for g in sorted(started)]
    received.append(_exchange(_exchange_operand(grads, EXCHANGE_GROUPS[-1]), name="grad_exchange_last"))
    updated = {}
    for g, pieces in enumerate(EXCHANGE_GROUPS):
        updated.update(_update_group(received[g], pieces, w, m, v, name=f"adamw_sharded_{g}"))

    small_parts = _all_gather(_pack_small(grads), name="small_grad_all_gather")
    outs_small = [_unpack_small(o) for o in
                  _adamw(small_parts, _pack_small(w), _pack_small(m), _pack_small(v), name="adamw_replicated")]

    total_loss = lax.psum(loss[0, 0], ("x", "y", "c"))
    result = [total_loss, dx[None]]
    for k in range(4):
        for n in WEIGHT_ORDER:
            if n not in SHARDED:
                result.append(outs_small[k][n])
            elif (n, None) in updated:
                result.append(updated[(n, None)][k])
            else:
                result.append(jnp.concatenate([updated[(n, l)][k] for l in range(SHARDED[n][0][0])], axis=0))
    return tuple(result)
```

```python
import functools
import math

import jax
import jax.numpy as jnp
from jax import lax
from jax.experimental import pallas as pl
from jax.experimental.pallas import tpu as pltpu

F32 = jnp.float32
BF16 = jnp.bfloat16
MM_DTYPE = BF16

N_DEV = 8
D_MODEL = 1024
DEPTH = 4
RMS_EPS = 1e-6
L2_EPS = 1e-6

LANES = 128

GDN_HEADS = 8
GDN_DK = 128
GDN_DV = 128
GDN_CONV = 4
GDN_CHUNK = 128
GDN_QKV = 3 * GDN_HEADS * GDN_DK
GDN_MAIN = GDN_QKV + GDN_HEADS * GDN_DV
GDN_IN_WIDTH = GDN_MAIN + 2 * GDN_HEADS

DIL_GROUPS = ((128, 1), (512, 4), (2048, 16))
DIL_HEADS = 8
DIL_DH = 64
DIL_SPAN = 128
DIL_SLAB = 3 * DIL_HEADS * LANES
ALIBI_MAX_BIAS = 8.0

FFN_HIDDEN = 2816

ADAM_LR = 0.001
ADAM_B1 = 0.9
ADAM_B2 = 0.999
ADAM_EPS = 1e-08
ADAM_WD = 0.01
ADAM_STEP = 10

VMEM_LIMIT = 56 * 1024 * 1024
NEG = -1e30
HI = lax.Precision.HIGHEST


def _cparams(sem):
    return pltpu.CompilerParams(dimension_semantics=sem, vmem_limit_bytes=VMEM_LIMIT)


def _dot(a, b):
    return lax.dot_general(a, b, (((1,), (0,)), ((), ())), preferred_element_type=F32, precision=HI)


def _dot_nt(a, b):
    return lax.dot_general(a, b, (((1,), (1,)), ((), ())), preferred_element_type=F32, precision=HI)


def _dot_tn(a, b):
    return lax.dot_general(a, b, (((0,), (0,)), ((), ())), preferred_element_type=F32, precision=HI)


def _single_pass(a, b, a_dim, b_dim):
    lead = a.ndim - 2
    batch = ((0,), (0,)) if lead else ((), ())
    return lax.dot_general(a.astype(BF16), b.astype(BF16), (((lead + a_dim,), (lead + b_dim,)), batch),
                           preferred_element_type=F32)


def _bdot(a, b):
    return _single_pass(a, b, 1, 0)


def _bdot_nt(a, b):
    return _single_pass(a, b, 1, 1)


def _bdot_tn(a, b):
    return _single_pass(a, b, 0, 0)


def _pick(n, candidates):
    for c in candidates:
        if n % c == 0:
            return c
    raise ValueError(f"no tile for {n}")


def _matmul(a, b, *, name, trans_a=False, trans_b=False, b_rows=None, a_lead=None, add=None, out_dtype=F32):
    if trans_a:
        k_dim, m_dim = a.shape[-2:]
    else:
        m_dim, k_dim = a.shape[-2:]
    b_start, b_size = b_rows if b_rows is not None else (0, b.shape[0])
    if trans_b:
        n_dim, k2 = b_size, b.shape[1]
    else:
        k2, n_dim = b_size, b.shape[1]
    assert k_dim == k2, (a.shape, b.shape, b_rows)
    tn = _pick(n_dim, (1024, 512, 256, 128))
    tm = min(m_dim, 2048, max(512, (1024 * 1024) // tn))
    tm = _pick(m_dim, (tm, 1408, 1024, 512, 256, 128))
    tk = _pick(k_dim, (1024, 1408, 512, 256, 128))
    nk = k_dim // tk
    has_add = add is not None
    dn = (((0 if trans_a else 1,), (1 if trans_b else 0,)), ((), ()))
    b_tile = tn if trans_b else tk
    assert b_start % b_tile == 0, (b_rows, b_tile)
    b_off = b_start // b_tile

    def body(*refs):
        if has_add:
            a_ref, b_ref, add_ref, o_ref, acc_ref = refs
        else:
            a_ref, b_ref, o_ref, acc_ref = refs
        part = lax.dot_general(a_ref[...], b_ref[...], dn, preferred_element_type=F32)

        def finish(total):
            if has_add:
                total = total + add_ref[...]
            o_ref[...] = total.astype(out_dtype)

        if nk == 1:
            finish(part)
        else:
            k = pl.program_id(2)

            @pl.when(k == 0)
            def _():
                acc_ref[...] = part

            @pl.when(k > 0)
            def _():
                acc_ref[...] += part

            @pl.when(k == nk - 1)
            def _():
                finish(acc_ref[...])

    a_tile = (tk, tm) if trans_a else (tm, tk)
    a_at = (lambda i, j, k: (k, i)) if trans_a else (lambda i, j, k: (i, k))
    if a_lead is None:
        a_spec = pl.BlockSpec(a_tile, a_at)
    else:
        a_spec = pl.BlockSpec((None,) + a_tile, lambda i, j, k: (a_lead,) + a_at(i, j, k))
    if trans_b:
        b_spec = pl.BlockSpec((tn, tk), lambda i, j, k: (j + b_off, k))
    else:
        b_spec = pl.BlockSpec((tk, tn), lambda i, j, k: (k + b_off, j))
    in_specs = [a_spec, b_spec]
    args = [a, b]
    if has_add:
        in_specs.append(pl.BlockSpec((tm, tn), lambda i, j, k: (i, j)))
        args.append(add)
    return pl.pallas_call(
        body,
        grid=(m_dim // tm, n_dim // tn, nk),
        in_specs=in_specs,
        out_specs=pl.BlockSpec((tm, tn), lambda i, j, k: (i, j)),
        out_shape=jax.ShapeDtypeStruct((m_dim, n_dim), out_dtype),
        scratch_shapes=[pltpu.VMEM((tm, tn) if nk > 1 else (8, LANES), F32)],
        compiler_params=_cparams(("parallel", "parallel", "arbitrary")),
        name=name,
    )(*args)


def _rmsnorm_fwd(x, w_row, *, name):
    t, d = x.shape
    tb = min(t, 1024)

    def body(x_ref, w_ref, o_ref):
        xf = x_ref[...]
        r = lax.rsqrt(jnp.mean(xf * xf, axis=-1, keepdims=True) + RMS_EPS)
        o_ref[...] = (xf * r * w_ref[...]).astype(o_ref.dtype)

    return pl.pallas_call(
        body,
        grid=(t // tb,),
        in_specs=[pl.BlockSpec((tb, d), lambda i: (i, 0)), pl.BlockSpec((1, d), lambda i: (0, 0))],
        out_specs=pl.BlockSpec((tb, d), lambda i: (i, 0)),
        out_shape=jax.ShapeDtypeStruct((t, d), MM_DTYPE),
        compiler_params=_cparams(("parallel",)),
        name=name,
    )(x, w_row)


def _rmsnorm_bwd(x, w_row, dy, dskip, *, name):
    t, d = x.shape
    tb = min(t, 512)

    def body(x_ref, w_ref, dy_ref, ds_ref, dx_ref, dxb_ref, dw_ref):
        xf = x_ref[...]
        g = dy_ref[...]
        r = lax.rsqrt(jnp.mean(xf * xf, axis=-1, keepdims=True) + RMS_EPS)
        gw = g * w_ref[...]
        proj = jnp.mean(gw * xf, axis=-1, keepdims=True)
        dx = r * gw - xf * (r * r * r * proj) + ds_ref[...]
        dx_ref[...] = dx
        dxb_ref[...] = dx.astype(dxb_ref.dtype)
        part = jnp.sum(g * xf * r, axis=0, keepdims=True)

        @pl.when(pl.program_id(0) == 0)
        def _():
            dw_ref[...] = part

        @pl.when(pl.program_id(0) > 0)
        def _():
            dw_ref[...] += part

    row = pl.BlockSpec((tb, d), lambda i: (i, 0))
    one = pl.BlockSpec((1, d), lambda i: (0, 0))
    return pl.pallas_call(
        body,
        grid=(t // tb,),
        in_specs=[row, one, row, row],
        out_specs=[row, row, one],
        out_shape=[jax.ShapeDtypeStruct((t, d), F32), jax.ShapeDtypeStruct((t, d), MM_DTYPE),
                   jax.ShapeDtypeStruct((1, d), F32)],
        compiler_params=_cparams(("arbitrary",)),
        name=name,
    )(x, w_row, dy, dskip)


def _silu(z):
    return z / (1.0 + jnp.exp(-z))


FFN_TM, FFN_TN = 512, 1408


def _ffn_in(hn, in_t, *, name):
    t, d = hn.shape
    h = FFN_HIDDEN
    tm, tn = min(t, FFN_TM), FFN_TN
    nj = h // tn
    dn = (((1,), (1,)), ((), ()))

    def body(a_ref, bg_ref, bu_ref, g_ref, u_ref, act_ref):
        a = a_ref[...]
        g = lax.dot_general(a, bg_ref[...], dn, preferred_element_type=F32)
        u = lax.dot_general(a, bu_ref[...], dn, preferred_element_type=F32)
        g_ref[...] = g.astype(g_ref.dtype)
        u_ref[...] = u.astype(u_ref.dtype)
        act_ref[...] = (_silu(g) * u).astype(act_ref.dtype)

    out = pl.BlockSpec((tm, tn), lambda i, j: (i, j))
    return pl.pallas_call(
        body,
        grid=(t // tm, nj),
        in_specs=[pl.BlockSpec((tm, d), lambda i, j: (i, 0)), pl.BlockSpec((tn, d), lambda i, j: (j, 0)),
                  pl.BlockSpec((tn, d), lambda i, j: (j + nj, 0))],
        out_specs=[out, out, out],
        out_shape=[jax.ShapeDtypeStruct((t, h), MM_DTYPE)] * 3,
        compiler_params=_cparams(("parallel", "parallel")),
        name=name,
    )(hn, in_t, in_t)


def _ffn_dact(dy, out_w, g, u, *, name):
    t, d = dy.shape
    h = FFN_HIDDEN
    tm, tn = min(t, FFN_TM), FFN_TN

    def body(a_ref, b_ref, g_ref, u_ref, dg_ref, du_ref):
        da = lax.dot_general(a_ref[...], b_ref[...], (((1,), (1,)), ((), ())), preferred_element_type=F32)
        gate = g_ref[...].astype(F32)
        sig = 1.0 / (1.0 + jnp.exp(-gate))
        sg = gate * sig
        dg_ref[...] = (da * u_ref[...].astype(F32) * (sig + sg * (1.0 - sig))).astype(dg_ref.dtype)
        du_ref[...] = (da * sg).astype(du_ref.dtype)

    blk = pl.BlockSpec((tm, tn), lambda i, j: (i, j))
    return pl.pallas_call(
        body,
        grid=(t // tm, h // tn),
        in_specs=[pl.BlockSpec((tm, d), lambda i, j: (i, 0)), pl.BlockSpec((tn, d), lambda i, j: (j, 0)), blk, blk],
        out_specs=[blk, blk],
        out_shape=[jax.ShapeDtypeStruct((t, h), MM_DTYPE)] * 2,
        compiler_params=_cparams(("parallel", "parallel")),
        name=name,
    )(dy, out_w, g, u)


def _loss_head(y, target, *, name):
    t, d = y.shape
    tb = min(t, 1024)

    def body(y_ref, t_ref, dy_ref, dyb_ref, l_ref):
        err = y_ref[...] - t_ref[...]
        dy_ref[...] = err * (1.0 / d)
        dyb_ref[...] = (err * (1.0 / d)).astype(dyb_ref.dtype)
        part = jnp.sum(jnp.sum(err * err, axis=0, keepdims=True), axis=1, keepdims=True) * (0.5 / d)
        part = jnp.broadcast_to(part, l_ref.shape)

        @pl.when(pl.program_id(0) == 0)
        def _():
            l_ref[...] = part

        @pl.when(pl.program_id(0) > 0)
        def _():
            l_ref[...] += part

    row = pl.BlockSpec((tb, d), lambda i: (i, 0))
    return pl.pallas_call(
        body,
        grid=(t // tb,),
        in_specs=[row, row],
        out_specs=[row, row, pl.BlockSpec((8, LANES), lambda i: (0, 0))],
        out_shape=[jax.ShapeDtypeStruct((t, d), F32), jax.ShapeDtypeStruct((t, d), MM_DTYPE),
                   jax.ShapeDtypeStruct((8, LANES), F32)],
        compiler_params=_cparams(("arbitrary",)),
        name=name,
    )(y, target)


CONV_HALO = 8


def _conv_tile_scale(c):
    is_qk = c < 2 * GDN_HEADS
    scale = jnp.where(c < GDN_HEADS, GDN_DK ** -0.5, 1.0).astype(F32)
    return is_qk, scale


def _gdn_conv_fwd(pm, conv_w, *, name):
    t = pm.shape[0]
    tb = min(t, 1024)
    nt = t // tb
    hb = tb // CONV_HALO

    def body(x_ref, xp_ref, w_ref, o_ref):
        c = pl.program_id(0)
        ti = pl.program_id(1)
        prev = jnp.where(ti > 0, xp_ref[...], 0.0)
        xe = jnp.concatenate([prev, x_ref[...]], axis=0)
        w = w_ref[...]
        y = jnp.zeros((tb, LANES), F32)
        for j in range(GDN_CONV):
            off = CONV_HALO - (GDN_CONV - 1) + j
            y = y + w[j:j + 1, :] * xe[off:off + tb, :]
        s = _silu(y)
        is_qk, scale = _conv_tile_scale(c)
        r = lax.rsqrt(jnp.sum(s * s, axis=-1, keepdims=True) + L2_EPS) * scale
        o_ref[...] = s * jnp.where(is_qk, r, 1.0)

    return pl.pallas_call(
        body,
        grid=(GDN_QKV // LANES, nt),
        in_specs=[
            pl.BlockSpec((tb, LANES), lambda c, i: (i, c)),
            pl.BlockSpec((CONV_HALO, LANES), lambda c, i: (jnp.maximum(i * hb - 1, 0), c)),
            pl.BlockSpec((GDN_CONV, LANES), lambda c, i: (0, c)),
        ],
        out_specs=pl.BlockSpec((tb, LANES), lambda c, i: (i, c)),
        out_shape=jax.ShapeDtypeStruct((t, GDN_QKV), F32),
        compiler_params=_cparams(("parallel", "parallel")),
        name=name,
    )(pm, pm, conv_w)


def _gdn_conv_bwd(pm, conv_w, dout, *, name):
    t = pm.shape[0]
    tb = min(t, 1024)
    nt = t // tb
    hb = tb // CONV_HALO
    last_hb = t // CONV_HALO - 1
    ext = tb + CONV_HALO

    def body(x_ref, xp_ref, xn_ref, d_ref, dn_ref, w_ref, dx_ref, dw_ref):
        c = pl.program_id(0)
        ti = pl.program_id(1)
        prev = jnp.where(ti > 0, xp_ref[...], 0.0)
        has_next = ti < nt - 1
        nxt = jnp.where(has_next, xn_ref[...], 0.0)
        xe = jnp.concatenate([prev, x_ref[...], nxt], axis=0)
        de = jnp.concatenate([d_ref[...], jnp.where(has_next, dn_ref[...], 0.0)], axis=0)
        w = w_ref[...]
        y = jnp.zeros((ext, LANES), F32)
        for j in range(GDN_CONV):
            off = CONV_HALO - (GDN_CONV - 1) + j
            y = y + w[j:j + 1, :] * xe[off:off + ext, :]
        sig = 1.0 / (1.0 + jnp.exp(-y))
        s = y * sig
        is_qk, scale = _conv_tile_scale(c)
        r = lax.rsqrt(jnp.sum(s * s, axis=-1, keepdims=True) + L2_EPS)
        n = s * r
        dnrm = de * scale
        ds_qk = r * (dnrm - n * jnp.sum(dnrm * n, axis=-1, keepdims=True))
        ds = jnp.where(is_qk, ds_qk, de)
        dy = ds * (sig + s * (1.0 - sig))
        dx = jnp.zeros((tb, LANES), F32)
        dw_rows = []
        for j in range(GDN_CONV):
            sh = GDN_CONV - 1 - j
            dx = dx + w[j:j + 1, :] * dy[sh:sh + tb, :]
            off = CONV_HALO - (GDN_CONV - 1) + j
            dw_rows.append(jnp.sum(dy[:tb, :] * xe[off:off + tb, :], axis=0, keepdims=True))
        dx_ref[...] = dx.astype(dx_ref.dtype)
        part = jnp.concatenate(dw_rows, axis=0)

        @pl.when(ti == 0)
        def _():
            dw_ref[...] = part

        @pl.when(ti > 0)
        def _():
            dw_ref[...] += part

    main = pl.BlockSpec((tb, LANES), lambda c, i: (i, c))
    prev = pl.BlockSpec((CONV_HALO, LANES), lambda c, i: (jnp.maximum(i * hb - 1, 0), c))
    nxt = pl.BlockSpec((CONV_HALO, LANES), lambda c, i: (jnp.minimum((i + 1) * hb, last_hb), c))
    return pl.pallas_call(
        body,
        grid=(GDN_QKV // LANES, nt),
        in_specs=[main, prev, nxt, main, nxt, pl.BlockSpec((GDN_CONV, LANES), lambda c, i: (0, c))],
        out_specs=[main, pl.BlockSpec((GDN_CONV, LANES), lambda c, i: (0, c))],
        out_shape=[jax.ShapeDtypeStruct((t, GDN_QKV), MM_DTYPE), jax.ShapeDtypeStruct((GDN_CONV, GDN_QKV), F32)],
        compiler_params=_cparams(("parallel", "arbitrary")),
        name=name,
    )(pm, pm, pm, dout, dout, conv_w)


def _head_selector(first_col):
    row = lax.broadcasted_iota(jnp.int32, (LANES, GDN_HEADS * LANES), 0)
    col = lax.broadcasted_iota(jnp.int32, (LANES, GDN_HEADS * LANES), 1)
    return (col // LANES + first_col == row).astype(F32)


def _softplus(x):
    return jnp.maximum(x, 0.0) + jnp.log(1.0 + jnp.exp(-jnp.abs(x)))


def _gdn_gates_fwd(ab, alog_row, dt_row, *, name):
    t = ab.shape[0]
    tb = min(t, 1024)
    wide = GDN_HEADS * LANES

    def body(ab_ref, al_ref, dt_ref, g_ref, b_ref):
        x = ab_ref[...]
        g_cols = -jnp.exp(al_ref[...]) * _softplus(x + dt_ref[...])
        b_cols = 1.0 / (1.0 + jnp.exp(-x))
        g_ref[...] = _dot(g_cols, _head_selector(0))
        b_ref[...] = _dot(b_cols, _head_selector(GDN_HEADS))

    row = pl.BlockSpec((tb, LANES), lambda i: (i, 0))
    one = pl.BlockSpec((1, LANES), lambda i: (0, 0))
    out = pl.BlockSpec((tb, wide), lambda i: (i, 0))
    return pl.pallas_call(
        body,
        grid=(t // tb,),
        in_specs=[row, one, one],
        out_specs=[out, out],
        out_shape=[jax.ShapeDtypeStruct((t, wide), F32)] * 2,
        compiler_params=_cparams(("parallel",)),
        name=name,
    )(ab, alog_row, dt_row)


def _gdn_gates_bwd(ab, alog_row, dt_row, dgb, dbb, *, name):
    t = ab.shape[0]
    tb = min(t, 1024)
    wide = GDN_HEADS * LANES

    def body(ab_ref, al_ref, dt_ref, dg_ref, db_ref, dab_ref, dal_ref, ddt_ref):
        x = ab_ref[...]
        lane = lax.broadcasted_iota(jnp.int32, (tb, LANES), 1)
        dg_cols = _dot_nt(dg_ref[...], _head_selector(0))
        db_cols = _dot_nt(db_ref[...], _head_selector(GDN_HEADS))
        ea = jnp.exp(al_ref[...])
        z = x + dt_ref[...]
        sp = _softplus(z)
        sg = 1.0 / (1.0 + jnp.exp(-z))
        beta = 1.0 / (1.0 + jnp.exp(-x))
        da = jnp.where(lane < GDN_HEADS, dg_cols * (-ea) * sg, 0.0)
        db = jnp.where((lane >= GDN_HEADS) & (lane < 2 * GDN_HEADS), db_cols * beta * (1.0 - beta), 0.0)
        dab_ref[...] = (da + db).astype(dab_ref.dtype)
        p_al = jnp.sum(jnp.where(lane < GDN_HEADS, dg_cols * (-ea) * sp, 0.0), axis=0, keepdims=True)
        p_dt = jnp.sum(da, axis=0, keepdims=True)

        @pl.when(pl.program_id(0) == 0)
        def _():
            dal_ref[...] = p_al
            ddt_ref[...] = p_dt

        @pl.when(pl.program_id(0) > 0)
        def _():
            dal_ref[...] += p_al
            ddt_ref[...] += p_dt

    row = pl.BlockSpec((tb, LANES), lambda i: (i, 0))
    one = pl.BlockSpec((1, LANES), lambda i: (0, 0))
    big = pl.BlockSpec((tb, wide), lambda i: (i, 0))
    return pl.pallas_call(
        body,
        grid=(t // tb,),
        in_specs=[row, one, one, big, big],
        out_specs=[row, one, one],
        out_shape=[jax.ShapeDtypeStruct((t, LANES), MM_DTYPE), jax.ShapeDtypeStruct((1, LANES), F32),
                   jax.ShapeDtypeStruct((1, LANES), F32)],
        compiler_params=_cparams(("arbitrary",)),
        name=name,
    )(ab, alog_row, dt_row, dgb, dbb)


@jax.custom_vjp
def _unit_lower_inverse_rest(n):
    c = n.shape[-1]
    ri = lax.broadcasted_iota(jnp.int32, (c, c), 0)
    ci = lax.broadcasted_iota(jnp.int32, (c, c), 1)
    rest = None
    size = 1
    while size < c:
        joins = ((ri // (2 * size)) == (ci // (2 * size))) & ((ri // size) != (ci // size))
        low = jnp.where(joins, n, 0.0)
        if rest is None:
            rest = -low
        else:
            left = low + _bdot(rest, low)
            rest = rest - (left + _bdot(left, rest))
        size *= 2
    return rest


def _unit_lower_inverse_rest_fwd(n):
    rest = _unit_lower_inverse_rest(n)
    return rest, rest


def _unit_lower_inverse_rest_bwd(rest, ct):
    left = ct + _bdot_tn(rest, ct)
    return (-(left + _bdot_nt(left, rest)),)


_unit_lower_inverse_rest.defvjp(_unit_lower_inverse_rest_fwd, _unit_lower_inverse_rest_bwd)


def _bf16_pieces(x):
    hi = x.astype(BF16)
    r1 = x - hi.astype(F32)
    mid = r1.astype(BF16)
    lo = (r1 - mid.astype(F32)).astype(BF16)
    return hi, mid, lo


def _lower_ones(shape):
    c = shape[-1]
    ri = lax.broadcasted_iota(jnp.int32, (c, c), 0)
    ci = lax.broadcasted_iota(jnp.int32, (c, c), 1)
    return jnp.broadcast_to((ri >= ci).astype(BF16), shape)


@jax.custom_vjp
def _running_sum(x):
    tri = _lower_ones(x.shape)
    return sum(_bdot(tri, p) for p in _bf16_pieces(x))


def _running_sum_fwd(x):
    return _running_sum(x), None


def _running_sum_bwd(_, ct):
    tri = _lower_ones(ct.shape)
    return (sum(_bdot_tn(tri, p) for p in _bf16_pieces(ct)),)


_running_sum.defvjp(_running_sum_fwd, _running_sum_bwd)


def _gdn_prep_math(q, k, v, gb, bb):
    c = GDN_CHUNK
    ri = lax.broadcasted_iota(jnp.int32, (c, c), 0)
    ci = lax.broadcasted_iota(jnp.int32, (c, c), 1)
    causal = ri >= ci
    gc = _running_sum(gb)
    decay = jnp.exp(jnp.where(causal, gc - jnp.swapaxes(gc, -1, -2), NEG))
    n = jnp.where(ri > ci, _bdot_nt(k, k) * bb * decay, 0.0)
    rest = _unit_lower_inverse_rest(n)
    eg = jnp.exp(gc)
    rhs_v = v * bb
    rhs_k = k * bb * eg
    u = rhs_v + _bdot(rest, rhs_v)
    w = rhs_k + _bdot(rest, rhs_k)
    qk = _bdot_nt(q, k) * decay
    qd = q * eg
    last = jnp.sum(jnp.where(ri == c - 1, gc, 0.0), axis=-2, keepdims=True)
    gl = jnp.broadcast_to(last, gc.shape)
    kt = k * jnp.exp(gl - gc)
    cd = jnp.exp(gl)
    return u, w, qk, qd, kt, cd


def _head_tiles(ref, h):
    return ref[:, h * LANES:(h + 1) * LANES]


def _stack_heads(ref, first=0, heads=GDN_HEADS):
    return jnp.stack([_head_tiles(ref, first + h) for h in range(heads)])


def _store_heads(ref, val, first=0):
    for h in range(val.shape[0]):
        ref[:, (first + h) * LANES:(first + h + 1) * LANES] = val[h].astype(ref.dtype)


def _gdn_prep_fwd(qkv, gb, bb, *, name):
    t = qkv.shape[0]
    c = GDN_CHUNK
    wide = GDN_HEADS * LANES

    def body(q_ref, k_ref, v_ref, g_ref, b_ref, *outs):
        res = _gdn_prep_math(*(_stack_heads(r) for r in (q_ref, k_ref, v_ref, g_ref, b_ref)))
        for o_ref, val in zip(outs, res):
            _store_heads(o_ref, val)

    blk = lambda off: pl.BlockSpec((c, wide), lambda i: (i, off))
    return pl.pallas_call(
        body,
        grid=(t // c,),
        in_specs=[blk(0), blk(1), blk(2), blk(0), blk(0)],
        out_specs=[blk(0)] * 6,
        out_shape=[jax.ShapeDtypeStruct((t, wide), F32)] * 6,
        compiler_params=_cparams(("parallel",)),
        name=name,
    )(qkv, qkv, qkv, gb, bb)


def _gdn_prep_bwd(qkv, gb, bb, cts, *, name):
    t = qkv.shape[0]
    c = GDN_CHUNK
    wide = GDN_HEADS * LANES

    def body(q_ref, k_ref, v_ref, g_ref, b_ref, c0, c1, c2, c3, c4, c5, dqkv_ref, dg_ref, db_ref):
        prim = tuple(_stack_heads(r) for r in (q_ref, k_ref, v_ref, g_ref, b_ref))
        _, pull = jax.vjp(_gdn_prep_math, *prim)
        dq, dk, dv, dg, db = pull(tuple(_stack_heads(r) for r in (c0, c1, c2, c3, c4, c5)))
        _store_heads(dqkv_ref, dq)
        _store_heads(dqkv_ref, dk, first=GDN_HEADS)
        _store_heads(dqkv_ref, dv, first=2 * GDN_HEADS)
        _store_heads(dg_ref, dg)
        _store_heads(db_ref, db)

    blk = lambda off: pl.BlockSpec((c, wide), lambda i: (i, off))
    return pl.pallas_call(
        body,
        grid=(t // c,),
        in_specs=[blk(0), blk(1), blk(2), blk(0), blk(0)] + [blk(0)] * 6,
        out_specs=[pl.BlockSpec((c, 3 * wide), lambda i: (i, 0)), blk(0), blk(0)],
        out_shape=[jax.ShapeDtypeStruct((t, 3 * wide), F32), jax.ShapeDtypeStruct((t, wide), F32),
                   jax.ShapeDtypeStruct((t, wide), F32)],
        compiler_params=_cparams(("parallel",)),
        name=name,
    )(qkv, qkv, qkv, gb, bb, *cts)


def _gdn_scan_math(s, u, w, qk, qd, kt, cd):
    v_new = u - _bdot(w, s)
    o = _bdot(qd, s) + _bdot(qk, v_new)
    s_new = s * cd + _bdot_tn(kt, v_new)
    return o, s_new


def _gdn_scan_fwd(prep, *, name):
    t = prep[0].shape[0]
    c = GDN_CHUNK
    wide = GDN_HEADS * LANES

    def body(u_ref, w_ref, qk_ref, qd_ref, kt_ref, cd_ref, o_ref, st_ref, s_ref):
        @pl.when(pl.program_id(0) == 0)
        def _():
            s_ref[...] = jnp.zeros_like(s_ref)

        s = _stack_heads(s_ref)
        _store_heads(st_ref, s)
        o, s_new = _gdn_scan_math(s, *(_stack_heads(r) for r in (u_ref, w_ref, qk_ref, qd_ref, kt_ref, cd_ref)))
        _store_heads(o_ref, o)
        _store_heads(s_ref, s_new)

    blk = pl.BlockSpec((c, wide), lambda i: (i, 0))
    return pl.pallas_call(
        body,
        grid=(t // c,),
        in_specs=[blk] * 6,
        out_specs=[blk, blk],
        out_shape=[jax.ShapeDtypeStruct((t, wide), F32)] * 2,
        scratch_shapes=[pltpu.VMEM((GDN_DK, wide), F32)],
        compiler_params=_cparams(("arbitrary",)),
        name=name,
    )(*prep)


def _gdn_scan_bwd(prep, states, do, *, name):
    t = do.shape[0]
    c = GDN_CHUNK
    wide = GDN_HEADS * LANES
    nc = t // c

    def body(u_ref, w_ref, qk_ref, qd_ref, kt_ref, cd_ref, st_ref, do_ref, *rest):
        outs, ds_ref = rest[:6], rest[6]

        @pl.when(pl.program_id(0) == 0)
        def _():
            ds_ref[...] = jnp.zeros_like(ds_ref)

        prim = tuple(_stack_heads(r) for r in (st_ref, u_ref, w_ref, qk_ref, qd_ref, kt_ref, cd_ref))
        _, pull = jax.vjp(_gdn_scan_math, *prim)
        grads = pull((_stack_heads(do_ref), _stack_heads(ds_ref)))
        _store_heads(ds_ref, grads[0])
        for o_ref, val in zip(outs, grads[1:]):
            _store_heads(o_ref, val)

    blk = pl.BlockSpec((c, wide), lambda i: (nc - 1 - i, 0))
    return pl.pallas_call(
        body,
        grid=(nc,),
        in_specs=[blk] * 8,
        out_specs=[blk] * 6,
        out_shape=[jax.ShapeDtypeStruct((t, wide), F32)] * 6,
        scratch_shapes=[pltpu.VMEM((GDN_DK, wide), F32)],
        compiler_params=_cparams(("arbitrary",)),
        name=name,
    )(*prep, states, do)


def _gdn_outgate_math(o, z, nw):
    r = lax.rsqrt(jnp.mean(o * o, axis=-1, keepdims=True) + RMS_EPS)
    return o * r * nw * _silu(z)


def _gdn_outgate_fwd(o, pm, nw_row, *, name):
    t = o.shape[0]
    tb = min(t, 1024)
    z_off = GDN_QKV // LANES

    def body(o_ref, z_ref, nw_ref, y_ref):
        y_ref[...] = _gdn_outgate_math(o_ref[...], z_ref[...], nw_ref[...]).astype(y_ref.dtype)

    return pl.pallas_call(
        body,
        grid=(t // tb, GDN_HEADS),
        in_specs=[pl.BlockSpec((tb, LANES), lambda i, h: (i, h)), pl.BlockSpec((tb, LANES), lambda i, h: (i, h + z_off)),
                  pl.BlockSpec((1, LANES), lambda i, h: (0, 0))],
        out_specs=pl.BlockSpec((tb, LANES), lambda i, h: (i, h)),
        out_shape=jax.ShapeDtypeStruct((t, GDN_HEADS * LANES), MM_DTYPE),
        compiler_params=_cparams(("parallel", "parallel")),
        name=name,
    )(o, pm, nw_row)


def _gdn_outgate_bwd(o, pm, nw_row, dy, *, name):
    t = o.shape[0]
    tb = min(t, 1024)
    z_off = GDN_QKV // LANES

    def body(o_ref, z_ref, nw_ref, dy_ref, do_ref, dz_ref, dnw_ref):
        _, pull = jax.vjp(_gdn_outgate_math, o_ref[...], z_ref[...], nw_ref[...])
        d_o, d_z, d_nw = pull(dy_ref[...])
        do_ref[...] = d_o
        dz_ref[...] = d_z.astype(dz_ref.dtype)
        first = (pl.program_id(0) == 0) & (pl.program_id(1) == 0)

        @pl.when(first)
        def _():
            dnw_ref[...] = d_nw

        @pl.when(jnp.logical_not(first))
        def _():
            dnw_ref[...] += d_nw

    blk = pl.BlockSpec((tb, LANES), lambda i, h: (i, h))
    one = pl.BlockSpec((1, LANES), lambda i, h: (0, 0))
    return pl.pallas_call(
        body,
        grid=(t // tb, GDN_HEADS),
        in_specs=[blk, pl.BlockSpec((tb, LANES), lambda i, h: (i, h + z_off)), one, blk],
        out_specs=[blk, blk, one],
        out_shape=[jax.ShapeDtypeStruct((t, GDN_HEADS * LANES), F32),
                   jax.ShapeDtypeStruct((t, GDN_HEADS * LANES), MM_DTYPE), jax.ShapeDtypeStruct((1, LANES), F32)],
        compiler_params=_cparams(("arbitrary", "arbitrary")),
        name=name,
    )(o, pm, nw_row, dy)


def _rms64(x, w_row):
    return x * lax.rsqrt(jnp.sum(x * x, axis=-1, keepdims=True) * (1.0 / DIL_DH) + RMS_EPS) * w_row


def _alibi_slopes(group):
    head = lax.broadcasted_iota(jnp.int32, (DIL_HEADS, 8, LANES), 0).astype(F32)
    rate = -math.log(2.0) * ALIBI_MAX_BIAS / (len(DIL_GROUPS) * DIL_HEADS)
    slope = jnp.exp(rate * (head + float(group * DIL_HEADS + 1)))
    return jnp.broadcast_to(slope[:, 0:1, :], (DIL_HEADS, DIL_SPAN, LANES))


def _band_logits(qn, kp, kc, slope_d, has_prev):
    qi = lax.broadcasted_iota(jnp.int32, (DIL_SPAN, DIL_SPAN), 0)
    kj = lax.broadcasted_iota(jnp.int32, (DIL_SPAN, DIL_SPAN), 1)
    steps_c = (qi - kj).astype(F32)
    scale = DIL_DH ** -0.5
    sp = _bdot_nt(qn, kp) * scale - slope_d * (steps_c + float(DIL_SPAN))
    sc = _bdot_nt(qn, kc) * scale - slope_d * steps_c
    sp = jnp.where((kj >= qi) & has_prev, sp, NEG)
    sc = jnp.where(kj <= qi, sc, NEG)
    return sp, sc


def _dil_attn_fwd(slab, wq_row, wk_row, *, group, name):
    dilation = DIL_GROUPS[group][1]
    t = slab.shape[0]
    rows = t // dilation
    nlb = rows // DIL_SPAN
    wide = DIL_HEADS * LANES
    view = slab.reshape(rows, dilation * DIL_SLAB)

    def body(q_ref, kc_ref, vc_ref, kp_ref, vp_ref, wq_ref, wk_ref, o_ref):
        has_prev = pl.program_id(1) > 0
        lane = lax.broadcasted_iota(jnp.int32, (DIL_SPAN, LANES), 1)
        qn = _rms64(_stack_heads(q_ref), wq_ref[...])
        kc = _rms64(_stack_heads(kc_ref), wk_ref[...])
        kp = _rms64(_stack_heads(kp_ref), wk_ref[...])
        sp, sc = _band_logits(qn, kp, kc, _alibi_slopes(group) * float(dilation), has_prev)
        m = jnp.maximum(jnp.max(sp, axis=-1, keepdims=True), jnp.max(sc, axis=-1, keepdims=True))
        pp = jnp.exp(sp - m)
        pc = jnp.exp(sc - m)
        l = jnp.sum(pp, axis=-1, keepdims=True) + jnp.sum(pc, axis=-1, keepdims=True)
        o = (_bdot(pp, _stack_heads(vp_ref)) + _bdot(pc, _stack_heads(vc_ref))) / l
        _store_heads(o_ref, jnp.where(lane < DIL_DH, o, m + jnp.log(l)))

    cur = lambda part: pl.BlockSpec((DIL_SPAN, wide), lambda r, i: (i, 3 * r + part))
    prv = lambda part: pl.BlockSpec((DIL_SPAN, wide), lambda r, i: (jnp.maximum(i - 1, 0), 3 * r + part))
    one = pl.BlockSpec((1, LANES), lambda r, i: (0, 0))
    out = pl.pallas_call(
        body,
        grid=(dilation, nlb),
        in_specs=[cur(0), cur(1), cur(2), prv(1), prv(2), one, one],
        out_specs=pl.BlockSpec((DIL_SPAN, wide), lambda r, i: (i, r)),
        out_shape=jax.ShapeDtypeStruct((rows, dilation * wide), F32),
        compiler_params=_cparams(("parallel", "parallel")),
        name=name,
    )(view, view, view, view, view, wq_row, wk_row)
    return out.reshape(t, wide)


def _head_slope(group, head):
    idx = jnp.zeros((8, LANES), F32) + head.astype(F32)
    rate = -math.log(2.0) * ALIBI_MAX_BIAS / (len(DIL_GROUPS) * DIL_HEADS)
    slope = jnp.exp(rate * (idx + float(group * DIL_HEADS + 1)))
    return jnp.broadcast_to(slope[0:1, :], (DIL_SPAN, LANES))


def _take_residues(ref, d):
    return jnp.stack([ref[pl.ds(r, DIL_SPAN, stride=d), :] for r in range(d)])


def _put_residues(ref, val, d):
    for r in range(d):
        ref[pl.ds(r, DIL_SPAN, stride=d), :] = val[r]


def _dil_attn_fwd_strided(slab, wq_row, wk_row, *, group, name):
    d = DIL_GROUPS[group][1]
    t = slab.shape[0]
    span = DIL_SPAN * d
    nsb = t // span

    def body(q_ref, kc_ref, vc_ref, kp_ref, vp_ref, wq_ref, wk_ref, o_ref):
        has_prev = pl.program_id(0) > 0
        lane = lax.broadcasted_iota(jnp.int32, (DIL_SPAN, LANES), 1)
        qn = _rms64(_take_residues(q_ref, d), wq_ref[...])
        kc = _rms64(_take_residues(kc_ref, d), wk_ref[...])
        kp = _rms64(_take_residues(kp_ref, d), wk_ref[...])
        sp, sc = _band_logits(qn, kp, kc, _head_slope(group, pl.program_id(1)) * float(d), has_prev)
        m = jnp.maximum(jnp.max(sp, axis=-1, keepdims=True), jnp.max(sc, axis=-1, keepdims=True))
        pp = jnp.exp(sp - m)
        pc = jnp.exp(sc - m)
        l = jnp.sum(pp, axis=-1, keepdims=True) + jnp.sum(pc, axis=-1, keepdims=True)
        o = (_bdot(pp, _take_residues(vp_ref, d)) + _bdot(pc, _take_residues(vc_ref, d))) / l
        _put_residues(o_ref, jnp.where(lane < DIL_DH, o, m + jnp.log(l)), d)

    cur = lambda part: pl.BlockSpec((span, LANES), lambda i, h: (i, part * DIL_HEADS + h))
    prv = lambda part: pl.BlockSpec((span, LANES), lambda i, h: (jnp.maximum(i - 1, 0), part * DIL_HEADS + h))
    one = pl.BlockSpec((1, LANES), lambda i, h: (0, 0))
    return pl.pallas_call(
        body,
        grid=(nsb, DIL_HEADS),
        in_specs=[cur(0), cur(1), cur(2), prv(1), prv(2), one, one],
        out_specs=pl.BlockSpec((span, LANES), lambda i, h: (i, h)),
        out_shape=jax.ShapeDtypeStruct((t, DIL_HEADS * LANES), F32),
        compiler_params=_cparams(("parallel", "parallel")),
        name=name,
    )(slab, slab, slab, slab, slab, wq_row, wk_row)


def _dil_attn_bwd_strided(slab, stat, wq_row, wk_row, dwq_in, dwk_in, *, group, name):
    d = DIL_GROUPS[group][1]
    t = slab.shape[0]
    span = DIL_SPAN * d
    nsb = t // span

    def body(q_ref, kc_ref, vc_ref, kp_ref, vp_ref, st_ref, wq_ref, wk_ref, dwq_in_ref, dwk_in_ref,
             d_ref, dwq_ref, dwk_ref, dk_carry, dv_carry, spread):
        step = pl.program_id(1)
        has_prev = step < nsb - 1
        first = (pl.program_id(0) == 0) & (step == 0)

        @pl.when(step == 0)
        def _():
            dk_carry[...] = jnp.zeros_like(dk_carry)
            dv_carry[...] = jnp.zeros_like(dv_carry)

        @pl.when(first)
        def _():
            dwq_ref[...] = dwq_in_ref[...]
            dwk_ref[...] = dwk_in_ref[...]

        lane = lax.broadcasted_iota(jnp.int32, (DIL_SPAN, LANES), 1)
        scale = DIL_DH ** -0.5
        q_raw = _take_residues(q_ref, d)
        kc_raw = _take_residues(kc_ref, d)
        vc = _take_residues(vc_ref, d)
        kp_raw = _take_residues(kp_ref, d)
        vp = _take_residues(vp_ref, d)
        st = _take_residues(st_ref, d)
        d_o = jnp.where(lane < DIL_DH, st, 0.0)
        lse = jnp.sum(jnp.where(lane == DIL_DH, st, 0.0), axis=-1, keepdims=True)
        delta = jnp.sum(jnp.where(lane == DIL_DH + 1, st, 0.0), axis=-1, keepdims=True)
        qn = _rms64(q_raw, wq_ref[...])
        kc = _rms64(kc_raw, wk_ref[...])
        kp = _rms64(kp_raw, wk_ref[...])
        sp, sc = _band_logits(qn, kp, kc, _head_slope(group, pl.program_id(0)) * float(d), has_prev)
        pp = jnp.exp(sp - lse)
        pc = jnp.exp(sc - lse)
        dsp = pp * (_bdot_nt(d_o, vp) - delta) * scale
        dsc = pc * (_bdot_nt(d_o, vc) - delta) * scale
        dqn = _bdot(dsp, kp) + _bdot(dsc, kc)
        dkc_n = _bdot_tn(dsc, qn) + dk_carry[...]
        dvc = _bdot_tn(pc, d_o) + dv_carry[...]
        dk_carry[...] = _bdot_tn(dsp, qn)
        dv_carry[...] = _bdot_tn(pp, d_o)
        dq_raw, dwq_rows = _rms64_bwd(q_raw, wq_ref[...], dqn)
        dk_raw, dwk_rows = _rms64_bwd(kc_raw, wk_ref[...], dkc_n)
        for part, val in enumerate((dq_raw, dk_raw, dvc)):
            _put_residues(spread, val, d)
            d_ref[part] = spread[...].astype(d_ref.dtype)
        dwq_ref[...] += jnp.sum(jnp.sum(dwq_rows, axis=0), axis=0, keepdims=True)
        dwk_ref[...] += jnp.sum(jnp.sum(dwk_rows, axis=0), axis=0, keepdims=True)

    at = lambda i: nsb - 1 - i
    cur = lambda part: pl.BlockSpec((span, LANES), lambda h, i: (at(i), part * DIL_HEADS + h))
    prv = lambda part: pl.BlockSpec((span, LANES), lambda h, i: (jnp.maximum(at(i) - 1, 0), part * DIL_HEADS + h))
    one = pl.BlockSpec((1, LANES), lambda h, i: (0, 0))
    return pl.pallas_call(
        body,
        grid=(DIL_HEADS, nsb),
        in_specs=[cur(0), cur(1), cur(2), prv(1), prv(2), pl.BlockSpec((span, LANES), lambda h, i: (at(i), h)),
                  one, one, one, one],
        out_specs=[pl.BlockSpec((3, span, LANES), lambda h, i: (0, at(i), h)), one, one],
        out_shape=[jax.ShapeDtypeStruct((3, t, DIL_HEADS * LANES), MM_DTYPE), jax.ShapeDtypeStruct((1, LANES), F32),
                   jax.ShapeDtypeStruct((1, LANES), F32)],
        scratch_shapes=[pltpu.VMEM((d, DIL_SPAN, LANES), F32), pltpu.VMEM((d, DIL_SPAN, LANES), F32),
                        pltpu.VMEM((span, LANES), F32)],
        compiler_params=_cparams(("arbitrary", "arbitrary")),
        name=name,
    )(slab, slab, slab, slab, slab, stat, wq_row, wk_row, dwq_in, dwk_in)


def _dil_merge_fwd(oe, *, name):
    t = oe[0].shape[0]
    tb = min(t, 1024)

    def body(e0, e1, e2, y_ref, om_ref):
        lane = lax.broadcasted_iota(jnp.int32, (tb, LANES), 1)
        es = [e0[...], e1[...], e2[...]]
        lse = [jnp.sum(jnp.where(lane == DIL_DH, e, 0.0), axis=-1, keepdims=True) for e in es]
        top = jnp.maximum(jnp.maximum(lse[0], lse[1]), lse[2])
        joint = top + jnp.log(jnp.exp(lse[0] - top) + jnp.exp(lse[1] - top) + jnp.exp(lse[2] - top))
        o = sum(jnp.exp(l - joint) * e for l, e in zip(lse, es))
        y_ref[...] = jnp.where(lane < DIL_DH, o, 0.0).astype(y_ref.dtype)
        om_ref[...] = jnp.where(lane < DIL_DH, o, joint)

    blk = pl.BlockSpec((tb, LANES), lambda i, h: (i, h))
    return pl.pallas_call(
        body,
        grid=(t // tb, DIL_HEADS),
        in_specs=[blk] * 3,
        out_specs=[blk, blk],
        out_shape=[jax.ShapeDtypeStruct((t, DIL_HEADS * LANES), MM_DTYPE),
                   jax.ShapeDtypeStruct((t, DIL_HEADS * LANES), F32)],
        compiler_params=_cparams(("parallel", "parallel")),
        name=name,
    )(*oe)


def _dil_merge_bwd(dy, om, *, name):
    t = dy.shape[0]
    tb = min(t, 1024)

    def body(dy_ref, om_ref, st_ref):
        lane = lax.broadcasted_iota(jnp.int32, (tb, LANES), 1)
        d_o = jnp.where(lane < DIL_DH, dy_ref[...], 0.0)
        om_t = om_ref[...]
        delta = jnp.sum(d_o * om_t, axis=-1, keepdims=True)
        st_ref[...] = jnp.where(lane < DIL_DH, d_o, jnp.where(lane == DIL_DH, om_t, jnp.where(lane == DIL_DH + 1, delta, 0.0)))

    blk = pl.BlockSpec((tb, LANES), lambda i, h: (i, h))
    return pl.pallas_call(
        body,
        grid=(t // tb, DIL_HEADS),
        in_specs=[blk, blk],
        out_specs=blk,
        out_shape=jax.ShapeDtypeStruct((t, DIL_HEADS * LANES), F32),
        compiler_params=_cparams(("parallel", "parallel")),
        name=name,
    )(dy, om)


def _rms64_bwd(x, w_row, dy):
    r = lax.rsqrt(jnp.sum(x * x, axis=-1, keepdims=True) * (1.0 / DIL_DH) + RMS_EPS)
    gw = dy * w_row
    dx = r * gw - x * (r * r * r * jnp.sum(gw * x, axis=-1, keepdims=True) * (1.0 / DIL_DH))
    return dx, dy * x * r


def _dil_attn_bwd(slab, stat, wq_row, wk_row, dwq_in, dwk_in, *, group, name):
    dilation = DIL_GROUPS[group][1]
    t = slab.shape[0]
    rows = t // dilation
    nlb = rows // DIL_SPAN
    wide = DIL_HEADS * LANES
    view = slab.reshape(rows, dilation * DIL_SLAB)
    stat_view = stat.reshape(rows, dilation * wide)

    def body(cur_ref, kp_ref, vp_ref, st_ref, wq_ref, wk_ref, dwq_in_ref, dwk_in_ref, d_ref, dwq_ref, dwk_ref,
             dk_carry, dv_carry):
        step = pl.program_id(1)
        has_prev = step < nlb - 1
        first = (pl.program_id(0) == 0) & (step == 0)

        @pl.when(step == 0)
        def _():
            dk_carry[...] = jnp.zeros_like(dk_carry)
            dv_carry[...] = jnp.zeros_like(dv_carry)

        @pl.when(first)
        def _():
            dwq_ref[...] = dwq_in_ref[...]
            dwk_ref[...] = dwk_in_ref[...]

        lane = lax.broadcasted_iota(jnp.int32, (DIL_SPAN, LANES), 1)
        scale = DIL_DH ** -0.5
        q_raw = _stack_heads(cur_ref)
        kc_raw = _stack_heads(cur_ref, first=DIL_HEADS)
        vc = _stack_heads(cur_ref, first=2 * DIL_HEADS)
        kp_raw = _stack_heads(kp_ref)
        vp = _stack_heads(vp_ref)
        st = _stack_heads(st_ref)
        d_o = jnp.where(lane < DIL_DH, st, 0.0)
        lse = jnp.sum(jnp.where(lane == DIL_DH, st, 0.0), axis=-1, keepdims=True)
        delta = jnp.sum(jnp.where(lane == DIL_DH + 1, st, 0.0), axis=-1, keepdims=True)
        qn = _rms64(q_raw, wq_ref[...])
        kc = _rms64(kc_raw, wk_ref[...])
        kp = _rms64(kp_raw, wk_ref[...])
        sp, sc = _band_logits(qn, kp, kc, _alibi_slopes(group) * float(dilation), has_prev)
        pp = jnp.exp(sp - lse)
        pc = jnp.exp(sc - lse)
        dsp = pp * (_bdot_nt(d_o, vp) - delta) * scale
        dsc = pc * (_bdot_nt(d_o, vc) - delta) * scale
        dqn = _bdot(dsp, kp) + _bdot(dsc, kc)
        dkc_n = _bdot_tn(dsc, qn) + _stack_heads(dk_carry)
        dvc = _bdot_tn(pc, d_o) + _stack_heads(dv_carry)
        _store_heads(dk_carry, _bdot_tn(dsp, qn))
        _store_heads(dv_carry, _bdot_tn(pp, d_o))
        dq_raw, dwq_rows = _rms64_bwd(q_raw, wq_ref[...], dqn)
        dk_raw, dwk_rows = _rms64_bwd(kc_raw, wk_ref[...], dkc_n)
        _store_heads(d_ref, dq_raw)
        _store_heads(d_ref, dk_raw, first=DIL_HEADS)
        _store_heads(d_ref, dvc, first=2 * DIL_HEADS)
        dwq_ref[...] += jnp.sum(jnp.sum(dwq_rows, axis=0), axis=0, keepdims=True)
        dwk_ref[...] += jnp.sum(jnp.sum(dwk_rows, axis=0), axis=0, keepdims=True)

    blk_i = lambda i: nlb - 1 - i
    cur = pl.BlockSpec((DIL_SPAN, DIL_SLAB), lambda r, i: (blk_i(i), r))
    prv = lambda part: pl.BlockSpec((DIL_SPAN, wide), lambda r, i: (jnp.maximum(blk_i(i) - 1, 0), 3 * r + part))
    one = pl.BlockSpec((1, LANES), lambda r, i: (0, 0))
    dslab, dwq, dwk = pl.pallas_call(
        body,
        grid=(dilation, nlb),
        in_specs=[cur, prv(1), prv(2), pl.BlockSpec((DIL_SPAN, wide), lambda r, i: (blk_i(i), r)), one, one, one, one],
        out_specs=[cur, one, one],
        out_shape=[jax.ShapeDtypeStruct((rows, dilation * DIL_SLAB), MM_DTYPE), jax.ShapeDtypeStruct((1, LANES), F32),
                   jax.ShapeDtypeStruct((1, LANES), F32)],
        scratch_shapes=[pltpu.VMEM((DIL_SPAN, wide), F32), pltpu.VMEM((DIL_SPAN, wide), F32)],
        compiler_params=_cparams(("arbitrary", "arbitrary")),
        name=name,
    )(view, view, view, stat_view, wq_row, wk_row, dwq_in, dwk_in)
    return dslab.reshape(t, DIL_SLAB), dwq, dwk


def _row(v, width=LANES):
    v = v.astype(F32).reshape(-1)
    return jnp.pad(v, (0, width - v.shape[0])).reshape(1, width)


def _prepare_weights(w):
    return dict(gdn=_prepare_gdn(w), dil=_prepare_dil(w), ffn=_prepare_ffn(w))


def _prepare_gdn(w, layers=range(DEPTH // 2)):
    gdn = {}
    for j in layers:
        wt = w["gdn_w_in"][j]
        gates_t = jnp.pad(wt[GDN_MAIN:], ((0, LANES - 2 * GDN_HEADS), (0, 0)))
        gdn[j] = dict(in_t=wt, gates_t=gates_t, out=w["gdn_w_out"][j], conv=w["gdn_conv_w"][j].astype(F32),
                      alog=_row(w["gdn_a_log"][j]), dt=_row(w["gdn_dt_bias"][j]), nw=_row(w["gdn_norm_w"][j]))
    return gdn


def _prepare_dil(w, layers=range(DEPTH // 2)):
    d = D_MODEL
    dil = {}
    for j in layers:
        wt = w["dil_w_in"][j].reshape(3, len(DIL_GROUPS), DIL_HEADS, DIL_DH, d)
        wg_t = [jnp.pad(wt[:, g], ((0, 0), (0, 0), (0, LANES - DIL_DH), (0, 0))).reshape(DIL_SLAB, d)
                for g in range(len(DIL_GROUPS))]
        out_t = jnp.pad(w["dil_w_out"][j].reshape(d, DIL_HEADS, DIL_DH), ((0, 0), (0, 0), (0, LANES - DIL_DH)))
        dil[j] = dict(wg_t=wg_t, out_t=out_t.reshape(d, DIL_HEADS * LANES), wq=_row(w["dil_q_norm"][j]),
                      wk=_row(w["dil_k_norm"][j]))
    return dil


def _prepare_ffn(w, layers=range(DEPTH)):
    return {i: dict(in_t=w["ffn_w_in"][i], out=w["ffn_w_out"][i]) for i in layers}


def _gdn_layer_fwd(x, nrow, p):
    hn = _rmsnorm_fwd(x, nrow, name="rmsnorm_fwd")
    pm = _matmul(hn, p["in_t"], trans_b=True, b_rows=(0, GDN_MAIN), name="gdn_proj_main")
    ab = _matmul(hn, p["gates_t"], trans_b=True, name="gdn_proj_gates")
    qkv = _gdn_conv_fwd(pm, p["conv"], name="gdn_conv_fwd")
    gb, bb = _gdn_gates_fwd(ab, p["alog"], p["dt"], name="gdn_gates_fwd")
    prep = _gdn_prep_fwd(qkv, gb, bb, name="gdn_prep_fwd")
    o, states = _gdn_scan_fwd(prep, name="gdn_scan_fwd")
    og = _gdn_outgate_fwd(o, pm, p["nw"], name="gdn_outgate_fwd")
    y = _matmul(og, p["out"], add=x, name="gdn_proj_out")
    return y, (x, hn, pm, ab, qkv, gb, bb, prep, states, o, og)


def _gdn_layer_bwd(dx, dxb, nrow, p, saved):
    x, hn, pm, ab, qkv, gb, bb, prep, states, o, og = saved
    d_og = _matmul(dxb, p["out"], trans_b=True, name="gdn_dgate")
    g_out = _matmul(og, dxb, trans_a=True, name="gdn_gw_out")
    d_o, d_z, d_nw = _gdn_outgate_bwd(o, pm, p["nw"], d_og, name="gdn_outgate_bwd")
    cts = _gdn_scan_bwd(prep, states, d_o, name="gdn_scan_bwd")
    dqkv, dgb, dbb = _gdn_prep_bwd(qkv, gb, bb, cts, name="gdn_prep_bwd")
    d_ab, d_alog, d_dt = _gdn_gates_bwd(ab, p["alog"], p["dt"], dgb, dbb, name="gdn_gates_bwd")
    d_conv, g_conv = _gdn_conv_bwd(pm, p["conv"], dqkv, name="gdn_conv_bwd")
    d_hn = _matmul(d_conv, p["in_t"], b_rows=(0, GDN_QKV), name="gdn_dhn_qkv")
    d_hn = _matmul(d_z, p["in_t"], b_rows=(GDN_QKV, GDN_MAIN - GDN_QKV), add=d_hn, name="gdn_dhn_z")
    d_hn = _matmul(d_ab, p["gates_t"], add=d_hn, name="gdn_dhn_gates")
    g_in_t = jnp.concatenate([
        _matmul(d_conv, hn, trans_a=True, name="gdn_gw_qkv"),
        _matmul(d_z, hn, trans_a=True, name="gdn_gw_z"),
        _matmul(d_ab, hn, trans_a=True, name="gdn_gw_gates")[:2 * GDN_HEADS],
    ], axis=0)
    dx_new, dxb_new, g_norm = _rmsnorm_bwd(x, nrow, d_hn, dx, name="rmsnorm_bwd")
    grads = dict(w_in=g_in_t, conv=g_conv, a_log=d_alog[0, :GDN_HEADS], dt_bias=d_dt[0, :GDN_HEADS], norm_w=d_nw[0],
                 w_out=g_out, norm=g_norm[0])
    return dx_new, dxb_new, grads


def _dil_layer_fwd(x, nrow, p):
    hn = _rmsnorm_fwd(x, nrow, name="rmsnorm_fwd")
    slabs = [_matmul(hn, p["wg_t"][g], trans_b=True, name="dil_proj_in") for g in range(len(DIL_GROUPS))]
    oe = [(_dil_attn_fwd if DIL_GROUPS[g][1] == 1 else _dil_attn_fwd_strided)(
        slabs[g], p["wq"], p["wk"], group=g, name=f"dil_attn_fwd_g{g}") for g in range(len(DIL_GROUPS))]
    y, om = _dil_merge_fwd(oe, name="dil_merge_fwd")
    out = _matmul(y, p["out_t"], trans_b=True, add=x, name="dil_proj_out")
    return out, (x, hn, slabs, y, om)


def _dil_layer_bwd(dx, dxb, nrow, p, saved):
    x, hn, slabs, y, om = saved
    d_y = _matmul(dxb, p["out_t"], name="dil_dmerged")
    g_out_t = _matmul(dxb, y, trans_a=True, name="dil_gw_out")
    g_out_t = g_out_t.reshape(D_MODEL, DIL_HEADS, LANES)[..., :DIL_DH].reshape(D_MODEL, DIL_HEADS * DIL_DH)
    stat = _dil_merge_bwd(d_y, om, name="dil_merge_bwd")
    d_hn = None
    dwq = jnp.zeros((1, LANES), F32)
    dwk = jnp.zeros((1, LANES), F32)
    g_groups = []
    wide = DIL_HEADS * LANES
    for g in range(len(DIL_GROUPS)):
        if DIL_GROUPS[g][1] == 1:
            dslab, dwq, dwk = _dil_attn_bwd(slabs[g], stat, p["wq"], p["wk"], dwq, dwk, group=g, name=f"dil_attn_bwd_g{g}")
            d_hn = _matmul(dslab, p["wg_t"][g], add=d_hn, name="dil_dhn")
            g_w = _matmul(dslab, hn, trans_a=True, name="dil_gw_in")
        else:
            dparts, dwq, dwk = _dil_attn_bwd_strided(slabs[g], stat, p["wq"], p["wk"], dwq, dwk, group=g,
                                                     name=f"dil_attn_bwd_g{g}")
            for part in range(3):
                d_hn = _matmul(dparts, p["wg_t"][g], a_lead=part, b_rows=(part * wide, wide), add=d_hn, name="dil_dhn_part")
            g_w = jnp.stack([_matmul(dparts, hn, trans_a=True, a_lead=part, name="dil_gw_in_part") for part in range(3)])
        g_groups.append(g_w.reshape(3, DIL_HEADS, LANES, D_MODEL)[:, :, :DIL_DH])
    g_in_t = jnp.stack(g_groups, axis=1).reshape(3 * len(DIL_GROUPS) * DIL_HEADS * DIL_DH, D_MODEL)
    dx_new, dxb_new, g_norm = _rmsnorm_bwd(x, nrow, d_hn, dx, name="rmsnorm_bwd")
    grads = dict(w_in=g_in_t, q_norm=dwq[0, :DIL_DH], k_norm=dwk[0, :DIL_DH], w_out=g_out_t, norm=g_norm[0])
    return dx_new, dxb_new, grads


def _ffn_layer_fwd(x, nrow, p):
    hn = _rmsnorm_fwd(x, nrow, name="rmsnorm_fwd")
    gate, up, act = _ffn_in(hn, p["in_t"], name="ffn_proj_in")
    y = _matmul(act, p["out"], add=x, name="ffn_proj_out")
    return y, (x, hn, gate, up, act)


def _ffn_layer_bwd(dx, dxb, nrow, p, saved):
    x, hn, gate, up, act = saved
    g_out = _matmul(act, dxb, trans_a=True, name="ffn_gw_out")
    d_g, d_u = _ffn_dact(dxb, p["out"], gate, up, name="ffn_dact")
    d_hn = _matmul(d_g, p["in_t"], b_rows=(0, FFN_HIDDEN), name="ffn_dhn_gate")
    d_hn = _matmul(d_u, p["in_t"], b_rows=(FFN_HIDDEN, FFN_HIDDEN), add=d_hn, name="ffn_dhn_up")
    g_in_t = jnp.concatenate([_matmul(d_g, hn, trans_a=True, name="ffn_gw_gate"),
                              _matmul(d_u, hn, trans_a=True, name="ffn_gw_up")], axis=0)
    dx_new, dxb_new, g_norm = _rmsnorm_bwd(x, nrow, d_hn, dx, name="rmsnorm_bwd")
    return dx_new, dxb_new, dict(w_in=g_in_t, w_out=g_out, norm=g_norm[0])


def _mixer_fwd(i, x, mix_row, prepared):
    if i % 2 == 0:
        return _gdn_layer_fwd(x, mix_row, prepared["gdn"][i // 2])
    return _dil_layer_fwd(x, mix_row, prepared["dil"][i // 2])


def _mixer_bwd(i, dx, dxb, mix_row, prepared, saved):
    if i % 2 == 0:
        return _gdn_layer_bwd(dx, dxb, mix_row, prepared["gdn"][i // 2], saved)
    return _dil_layer_bwd(dx, dxb, mix_row, prepared["dil"][i // 2], saved)


def _local_step(x, target, prepared, norm_mix, norm_ffn):
    saved = []
    for i in range(DEPTH):
        x, s_mix = _mixer_fwd(i, x, norm_mix[i].reshape(1, D_MODEL), prepared)
        x, s_ffn = _ffn_layer_fwd(x, norm_ffn[i].reshape(1, D_MODEL), prepared["ffn"][i])
        saved.append((s_mix, s_ffn))
    dx, dxb, loss = _loss_head(x, target, name="loss_head")
    g_mix, g_ffn = [None] * DEPTH, [None] * DEPTH
    for i in reversed(range(DEPTH)):
        s_mix, s_ffn = saved[i]
        dx, dxb, g_ffn[i] = _ffn_layer_bwd(dx, dxb, norm_ffn[i].reshape(1, D_MODEL), prepared["ffn"][i], s_ffn)
        dx, dxb, g_mix[i] = _mixer_bwd(i, dx, dxb, norm_mix[i].reshape(1, D_MODEL), prepared, s_mix)
    return loss[0, 0], dx, _collect_grads(g_mix, g_ffn)


def _collect_grads(g_mix, g_ffn):
    gdn = [g_mix[i] for i in range(0, DEPTH, 2)]
    dil = [g_mix[i] for i in range(1, DEPTH, 2)]
    if any(g is None for g in g_mix + g_ffn):
        pick = lambda gs, key: [None if g is None else g[key] for g in gs]
        return dict(gdn_w_in=pick(gdn, "w_in"), gdn_w_out=pick(gdn, "w_out"), dil_w_in=pick(dil, "w_in"),
                    dil_w_out=pick(dil, "w_out"), ffn_w_in=pick(g_ffn, "w_in"), ffn_w_out=pick(g_ffn, "w_out"))
    grads = dict(
        norm_mix=jnp.stack([g["norm"] for g in g_mix]),
        norm_ffn=jnp.stack([g["norm"] for g in g_ffn]),
        gdn_w_in=[g["w_in"] for g in gdn],
        gdn_conv_w=jnp.stack([g["conv"] for g in gdn]),
        gdn_a_log=jnp.stack([g["a_log"] for g in gdn]),
        gdn_dt_bias=jnp.stack([g["dt_bias"] for g in gdn]),
        gdn_norm_w=jnp.stack([g["norm_w"] for g in gdn]),
        gdn_w_out=[g["w_out"] for g in gdn],
        dil_w_in=[g["w_in"] for g in dil],
        dil_q_norm=jnp.stack([g["q_norm"] for g in dil]),
        dil_k_norm=jnp.stack([g["k_norm"] for g in dil]),
        dil_w_out=[g["w_out"] for g in dil],
        ffn_w_in=[g["w_in"] for g in g_ffn],
        ffn_w_out=[g["w_out"] for g in g_ffn],
    )
    return grads


MESH_ID = pl.DeviceIdType.MESH
ANY_SPACE = pl.BlockSpec(memory_space=pl.ANY)


def _mesh_position():
    return lax.axis_index("x"), lax.axis_index("y"), lax.axis_index("c")


def _flip(pos, k):
    x, y, c = pos
    return (1 - x if k & 4 else x, 1 - y if k & 2 else y, 1 - c if k & 1 else c)


def _linear(pos):
    return 4 * pos[0] + 2 * pos[1] + pos[2]


def _comm_scratch():
    return [pltpu.SemaphoreType.DMA((N_DEV - 1,)), pltpu.SemaphoreType.DMA((N_DEV - 1,)), pltpu.SemaphoreType.DMA(())]


def _all_gather(shard, *, name):
    def body(x_ref, out_ref, send_sems, recv_sems, local_sem):
        me = _mesh_position()
        mine = out_ref.at[_linear(me)]
        local = pltpu.make_async_copy(x_ref, mine, local_sem)
        local.start()
        copies = []
        for k in range(1, N_DEV):
            cp = pltpu.make_async_remote_copy(src_ref=x_ref, dst_ref=mine, send_sem=send_sems.at[k - 1],
                                              recv_sem=recv_sems.at[k - 1], device_id=_flip(me, k), device_id_type=MESH_ID)
            cp.start()
            copies.append(cp)
        for cp in copies:
            cp.wait()
        local.wait()

    return pl.pallas_call(
        body,
        out_shape=jax.ShapeDtypeStruct((N_DEV,) + shard.shape, shard.dtype),
        in_specs=[ANY_SPACE],
        out_specs=ANY_SPACE,
        scratch_shapes=_comm_scratch(),
        name=name,
    )(shard)


def _exchange(parts, *, name):
    def body(p_ref, out_ref, send_sems, recv_sems, local_sem):
        me = _mesh_position()
        mine = out_ref.at[_linear(me)]
        local = pltpu.make_async_copy(p_ref.at[_linear(me)], mine, local_sem)
        local.start()
        copies = []
        for k in range(1, N_DEV):
            peer = _flip(me, k)
            cp = pltpu.make_async_remote_copy(src_ref=p_ref.at[_linear(peer)], dst_ref=mine, send_sem=send_sems.at[k - 1],
                                              recv_sem=recv_sems.at[k - 1], device_id=peer, device_id_type=MESH_ID)
            cp.start()
            copies.append(cp)
        for cp in copies:
            cp.wait()
        local.wait()

    return pl.pallas_call(
        body,
        out_shape=jax.ShapeDtypeStruct(parts.shape, parts.dtype),
        in_specs=[ANY_SPACE],
        out_specs=ANY_SPACE,
        scratch_shapes=_comm_scratch(),
        name=name,
    )(parts)


HBM_SPACE = pl.BlockSpec(memory_space=pltpu.HBM)
SEM_SPACE = pl.BlockSpec(memory_space=pltpu.SEMAPHORE)
DATAFLOW = pltpu.SideEffectType.DATAFLOW_SIDE_EFFECTING


def _split_copies(src_ref, land_ref, send_sems, recv_sems, per_peer):
    me = _mesh_position()
    mine = land_ref.at[_linear(me)]
    copies = []
    for k in range(1, N_DEV):
        peer = _flip(me, k)
        src = src_ref.at[_linear(peer)] if per_peer else src_ref
        copies.append(pltpu.make_async_remote_copy(src_ref=src, dst_ref=mine, send_sem=send_sems.at[k - 1],
                                                   recv_sem=recv_sems.at[k - 1], device_id=peer, device_id_type=MESH_ID))
    return copies


def _travel_start(src, after, *, per_peer, name):
    me = _linear(_mesh_position())
    own = src[me] if per_peer else src
    shape = own.shape
    landing = lax.dynamic_update_slice(lax.empty((N_DEV,) + shape, src.dtype), own[None], (me, 0, 0))

    def body(src_ref, land_ref, after_ref, send_sems, recv_sems, src_thru, land_thru, token):
        for cp in _split_copies(src_ref, land_ref, send_sems, recv_sems, per_peer):
            cp.start()
        token[...] = jnp.zeros_like(token)

    return pl.pallas_call(
        body,
        name=name,
        out_shape=(pltpu.SemaphoreType.DMA((N_DEV - 1,)), pltpu.SemaphoreType.DMA((N_DEV - 1,)),
                   pltpu.HBM(src.shape, src.dtype), pltpu.HBM(landing.shape, landing.dtype),
                   jax.ShapeDtypeStruct((8, LANES), F32)),
        in_specs=(HBM_SPACE, HBM_SPACE, ANY_SPACE),
        out_specs=(SEM_SPACE, SEM_SPACE, HBM_SPACE, HBM_SPACE, pl.BlockSpec(memory_space=pltpu.VMEM)),
        input_output_aliases={0: 2, 1: 3},
        compiler_params=pltpu.CompilerParams(has_side_effects=DATAFLOW),
    )(pltpu.with_memory_space_constraint(src, pltpu.HBM), pltpu.with_memory_space_constraint(landing, pltpu.HBM), after)


def _travel_wait(started, after, *, per_peer, name):
    send_sems, recv_sems, src_thru, land_thru, _ = started

    def body(src_ref, land_ref, send_sems, recv_sems, after_ref, src_dead, got_ref):
        for cp in _split_copies(src_ref, land_ref, send_sems, recv_sems, per_peer):
            cp.wait_send()
            cp.wait_recv()

    return pl.pallas_call(
        body,
        name=name,
        out_shape=(pltpu.HBM(src_thru.shape, src_thru.dtype), pltpu.HBM(land_thru.shape, land_thru.dtype)),
        in_specs=(HBM_SPACE, HBM_SPACE, SEM_SPACE, SEM_SPACE, ANY_SPACE),
        out_specs=(HBM_SPACE, HBM_SPACE),
        input_output_aliases={0: 0, 1: 1},
        compiler_params=pltpu.CompilerParams(has_side_effects=DATAFLOW),
    )(src_thru, land_thru, send_sems, recv_sems, after)[1]


def _adamw(parts, w, m, v, *, name):
    rows, n = w.shape
    tb = _pick(rows, (PACK_ROW_ALIGN, 16))
    c1 = 1.0 - ADAM_B1 ** ADAM_STEP
    c2 = 1.0 - ADAM_B2 ** ADAM_STEP

    def body(p_ref, w_ref, m_ref, v_ref, g_ref, d_ref, nm_ref, nv_ref):
        g = p_ref[0].astype(F32)
        for s in range(1, N_DEV):
            g = g + p_ref[s].astype(F32)
        m_new = ADAM_B1 * m_ref[...] + (1.0 - ADAM_B1) * g
        v_new = ADAM_B2 * v_ref[...] + (1.0 - ADAM_B2) * (g * g)
        m_hat = m_new / c1
        v_hat = v_new / c2
        g_ref[...] = g
        nm_ref[...] = m_new
        nv_ref[...] = v_new
        d_ref[...] = -ADAM_LR * (m_hat / (jnp.sqrt(v_hat) + ADAM_EPS) + ADAM_WD * w_ref[...])

    blk = pl.BlockSpec((tb, n), lambda i: (i, 0))
    return pl.pallas_call(
        body,
        grid=(rows // tb,),
        in_specs=[pl.BlockSpec((N_DEV, tb, n), lambda i: (0, i, 0)), blk, blk, blk],
        out_specs=[blk] * 4,
        out_shape=[jax.ShapeDtypeStruct((rows, n), F32)] * 4,
        compiler_params=_cparams(("parallel",)),
        name=name,
    )(parts, w, m, v)


PACK_WIDTH = 1024
SHARDED = {
    "gdn_w_in": ((2, D_MODEL, GDN_IN_WIDTH), 2),
    "gdn_conv_w": ((2, GDN_CONV, GDN_QKV), 2),
    "gdn_w_out": ((2, GDN_HEADS * GDN_DV, D_MODEL), 1),
    "dil_w_in": ((2, D_MODEL, 3 * len(DIL_GROUPS) * DIL_HEADS * DIL_DH), 2),
    "dil_w_out": ((2, DIL_HEADS * DIL_DH, D_MODEL), 2),
    "ffn_w_in": ((DEPTH, D_MODEL, 2 * FFN_HIDDEN), 2),
    "ffn_w_out": ((DEPTH, FFN_HIDDEN, D_MODEL), 1),
}
REPLICATED = {"norm_mix": (DEPTH, D_MODEL), "norm_ffn": (DEPTH, D_MODEL), "gdn_a_log": (2, GDN_HEADS),
              "gdn_dt_bias": (2, GDN_HEADS), "gdn_norm_w": (2, GDN_DV), "dil_q_norm": (2, DIL_DH), "dil_k_norm": (2, DIL_DH)}
WEIGHT_ORDER = ("norm_mix", "norm_ffn", "gdn_w_in", "gdn_conv_w", "gdn_a_log", "gdn_dt_bias", "gdn_norm_w", "gdn_w_out",
                "dil_w_in", "dil_q_norm", "dil_k_norm", "dil_w_out", "ffn_w_in", "ffn_w_out")
PACK_ROW_ALIGN = 128
PIECE_ALIGN = 16
SMALL_ROWS = 16


def _shard_shape(name):
    shape, axis = SHARDED[name]
    return tuple(s // N_DEV if i == axis else s for i, s in enumerate(shape))


def _shard_rows(name):
    return math.prod(_shard_shape(name)) // PACK_WIDTH


def _split_shards(full, name):
    shape, axis = SHARDED[name]
    split = full.reshape(shape[:axis] + (N_DEV, shape[axis] // N_DEV) + shape[axis + 1:])
    return jnp.moveaxis(split, axis, 0)


def _join_shards(stacked, name):
    shape, axis = SHARDED[name]
    return jnp.moveaxis(stacked, 0, axis).reshape(shape)


COLUMN_SHARDED = ("gdn_w_in", "dil_w_in", "dil_w_out", "ffn_w_in")


def _to_rows(shard, name):
    if name in COLUMN_SHARDED:
        shard = jnp.swapaxes(shard, 1, 2)
    return shard.reshape(-1, PACK_WIDTH)


def _layer_columns(name):
    _, r, c = _shard_shape(name)
    return r if name in COLUMN_SHARDED else c


def _piece_rows(piece, halves=1):
    name, layer = piece
    rows = _shard_rows(name) * halves
    return rows if layer is None else rows // SHARDED[name][0][0]


def _aligned(rows, to=PIECE_ALIGN):
    return -(-rows // to) * to


def _pack_pieces(arrays, total_align=PIECE_ALIGN):
    padded, total = [], 0
    for a in arrays:
        rows = a.shape[-2]
        extra = _aligned(rows) - rows
        if extra:
            a = jnp.pad(a, [(0, 0)] * (a.ndim - 2) + [(0, extra), (0, 0)])
        padded.append(a)
        total += rows + extra
    tail = _aligned(total, total_align) - total
    if tail:
        padded.append(jnp.zeros(padded[0].shape[:-2] + (tail, PACK_WIDTH), padded[0].dtype))
    return jnp.concatenate(padded, axis=-2)


def _piece_offsets(pieces, halves=None):
    out, at = [], 0
    for p in pieces:
        rows = _piece_rows(p, (halves or {}).get(p[0], 1))
        out.append((p, at, rows))
        at += _aligned(rows)
    return out


def _shard_piece_rows(src, piece):
    name, layer = piece
    part = src[name] if layer is None else src[name][layer:layer + 1]
    return _to_rows(part.astype(F32), name)


def _piece_from_rows(rows, piece):
    name, layer = piece
    layers, r, c = _shard_shape(name)
    n_l = layers if layer is None else 1
    if name in COLUMN_SHARDED:
        return jnp.swapaxes(rows.reshape(n_l, c, r), 1, 2)
    return rows.reshape(n_l, r, c)


SMALL_TAIL = tuple(n for n in REPLICATED if n not in ("norm_mix", "norm_ffn"))


def _pack_small(vals):
    tail, at = jnp.zeros((PACK_WIDTH,), F32), 0
    for n in SMALL_TAIL:
        vec = vals[n].astype(F32).reshape(-1)
        tail = tail + jnp.pad(vec, (at, PACK_WIDTH - at - vec.shape[0]))
        at += vec.shape[0]
    buf = jnp.pad(vals["norm_mix"].astype(F32), ((0, SMALL_ROWS - DEPTH), (0, 0)))
    buf = buf + jnp.pad(vals["norm_ffn"].astype(F32), ((8, SMALL_ROWS - 8 - DEPTH), (0, 0)))
    return buf + jnp.pad(tail.reshape(1, PACK_WIDTH), ((SMALL_ROWS - 1, 0), (0, 0)))


def _unpack_small(buf):
    out = {"norm_mix": buf[0:DEPTH], "norm_ffn": buf[8:8 + DEPTH]}
    at = 0
    for n in SMALL_TAIL:
        size = math.prod(REPLICATED[n])
        out[n] = buf[SMALL_ROWS - 1, at:at + size].reshape(REPLICATED[n])
        at += size
    return out


GATHER_FIRST = (("gdn_w_in", 0), ("gdn_conv_w", None), ("gdn_w_out", 0))
GATHER_NEXT = (("ffn_w_in", 0), ("ffn_w_out", 0), ("dil_w_in", 0), ("dil_w_out", 0))
GATHER_LAST = (("ffn_w_in", 1), ("ffn_w_out", 1), ("gdn_w_in", 1), ("gdn_w_out", 1), ("ffn_w_in", 2), ("ffn_w_out", 2),
               ("dil_w_in", 1), ("dil_w_out", 1), ("ffn_w_in", 3), ("ffn_w_out", 3))
EXCHANGE_GROUPS = (
    (("ffn_w_in", 3), ("ffn_w_out", 3), ("dil_w_in", 1), ("dil_w_out", 1),
     ("ffn_w_in", 2), ("ffn_w_out", 2), ("gdn_w_in", 1), ("gdn_w_out", 1)),
    (("ffn_w_in", 1), ("ffn_w_out", 1), ("dil_w_in", 0), ("dil_w_out", 0)),
    (("ffn_w_in", 0), ("ffn_w_out", 0)),
    (("gdn_w_in", 0), ("gdn_w_out", 0), ("gdn_conv_w", None)),
)
EXCHANGE_AFTER = {("mix", 2): 0, ("mix", 1): 1, ("ffn", 0): 2}


def _gather_operand(w, pieces):
    arrays = []
    for n, layer in pieces:
        if layer is None:
            arrays.append(lax.bitcast_convert_type(w[n], BF16).reshape(-1, PACK_WIDTH))
        else:
            arrays.append(_to_rows(w[n][layer:layer + 1].astype(BF16), n))
    return _pack_pieces(arrays)


def _gathered_weights(gathered, pieces, full):
    for (n, layer), at, rows in _piece_offsets(pieces, halves={"gdn_conv_w": 2}):
        block = gathered[:, at:at + rows]
        if layer is None:
            block = lax.bitcast_convert_type(block.reshape((N_DEV,) + _shard_shape(n) + (2,)), F32)
            full[n] = _join_shards(block, n)
        else:
            full.setdefault(n, {})[layer] = block.reshape(-1, _layer_columns(n))
    return full


def _exchange_operand(grads, pieces):
    arrays = []
    for n, layer in pieces:
        if layer is None:
            arrays.append(_split_shards(grads[n], n).astype(BF16).reshape(N_DEV, -1, PACK_WIDTH))
        else:
            arrays.append(grads[n][layer].astype(BF16).reshape(N_DEV, -1, PACK_WIDTH))
    return _pack_pieces(arrays, total_align=PACK_ROW_ALIGN)


def _update_group(received, pieces, w, m, v, *, name):
    packed = [_pack_pieces([_shard_piece_rows(src, p) for p in pieces], total_align=PACK_ROW_ALIGN) for src in (w, m, v)]
    outs = _adamw(received, *packed, name=name)
    return {p: tuple(_piece_from_rows(o[at:at + rows], p) for o in outs) for p, at, rows in _piece_offsets(pieces)}


def kernel(x, norm_mix, norm_ffn, gdn_w_in, gdn_conv_w, gdn_a_log, gdn_dt_bias, gdn_norm_w, gdn_w_out, dil_w_in, dil_q_norm, dil_k_norm, dil_w_out, ffn_w_in, ffn_w_out, loss_target, m_norm_mix, m_norm_ffn, m_gdn_w_in, m_gdn_conv_w, m_gdn_a_log, m_gdn_dt_bias, m_gdn_norm_w, m_gdn_w_out, m_dil_w_in, m_dil_q_norm, m_dil_k_norm, m_dil_w_out, m_ffn_w_in, m_ffn_w_out, v_norm_mix, v_norm_ffn, v_gdn_w_in, v_gdn_conv_w, v_gdn_a_log, v_gdn_dt_bias, v_gdn_norm_w, v_gdn_w_out, v_dil_w_in, v_dil_q_norm, v_dil_k_norm, v_dil_w_out, v_ffn_w_in, v_ffn_w_out):
    w = dict(norm_mix=norm_mix, norm_ffn=norm_ffn, gdn_w_in=gdn_w_in, gdn_conv_w=gdn_conv_w, gdn_a_log=gdn_a_log,
             gdn_dt_bias=gdn_dt_bias, gdn_norm_w=gdn_norm_w, gdn_w_out=gdn_w_out, dil_w_in=dil_w_in, dil_q_norm=dil_q_norm,
             dil_k_norm=dil_k_norm, dil_w_out=dil_w_out, ffn_w_in=ffn_w_in, ffn_w_out=ffn_w_out)
    m = dict(norm_mix=m_norm_mix, norm_ffn=m_norm_ffn, gdn_w_in=m_gdn_w_in, gdn_conv_w=m_gdn_conv_w, gdn_a_log=m_gdn_a_log,
             gdn_dt_bias=m_gdn_dt_bias, gdn_norm_w=m_gdn_norm_w, gdn_w_out=m_gdn_w_out, dil_w_in=m_dil_w_in,
             dil_q_norm=m_dil_q_norm, dil_k_norm=m_dil_k_norm, dil_w_out=m_dil_w_out, ffn_w_in=m_ffn_w_in, ffn_w_out=m_ffn_w_out)
    v = dict(norm_mix=v_norm_mix, norm_ffn=v_norm_ffn, gdn_w_in=v_gdn_w_in, gdn_conv_w=v_gdn_conv_w, gdn_a_log=v_gdn_a_log,
             gdn_dt_bias=v_gdn_dt_bias, gdn_norm_w=v_gdn_norm_w, gdn_w_out=v_gdn_w_out, dil_w_in=v_dil_w_in,
             dil_q_norm=v_dil_q_norm, dil_k_norm=v_dil_k_norm, dil_w_out=v_dil_w_out, ffn_w_in=v_ffn_w_in, ffn_w_out=v_ffn_w_out)
    def row(src, i):
        return src[i].reshape(1, D_MODEL)

    first = _all_gather(_gather_operand(w, GATHER_FIRST), name="weight_all_gather_first")
    next_started = _travel_start(_gather_operand(w, GATHER_NEXT), first, per_peer=False, name="weight_gather_start_next")
    last_started = _travel_start(_gather_operand(w, GATHER_LAST), next_started[4], per_peer=False,
                                 name="weight_gather_start_last")
    full = _gathered_weights(first, GATHER_FIRST, {n: w[n] for n in REPLICATED})
    prepared = dict(gdn=_prepare_gdn(full, layers=(0,)))
    h = x[0]
    saved = [None] * DEPTH
    h, s_mix = _mixer_fwd(0, h, row(norm_mix, 0) + last_started[4][0, 0], prepared)
    got = _travel_wait(next_started, h, per_peer=False, name="weight_gather_wait_next")
    full = _gathered_weights(got, GATHER_NEXT, full)
    prepared.update(dil=_prepare_dil(full, layers=(0,)), ffn=_prepare_ffn(full, layers=(0,)))
    for i in range(DEPTH):
        if i > 0:
            h, s_mix = _mixer_fwd(i, h, row(norm_mix, i), prepared)
        if i == 1:
            got = _travel_wait(last_started, h, per_peer=False, name="weight_gather_wait_last")
            full = _gathered_weights(got, GATHER_LAST, full)
            prepared["gdn"].update(_prepare_gdn(full, layers=(1,)))
            prepared["dil"].update(_prepare_dil(full, layers=(1,)))
            prepared["ffn"].update(_prepare_ffn(full, layers=(1, 2, 3)))
        h, s_ffn = _ffn_layer_fwd(h, row(norm_ffn, i), prepared["ffn"][i])
        saved[i] = (s_mix, s_ffn)
    dx, dxb, loss = _loss_head(h, loss_target[0], name="loss_head")

    g_mix, g_ffn = [None] * DEPTH, [None] * DEPTH
    started = {}

    def travel(group, dxb):
        operand = _exchange_operand(_collect_grads(g_mix, g_ffn), EXCHANGE_GROUPS[group])
        started[group] = _travel_start(operand, dx, per_peer=True, name=f"grad_exchange_start_{group}")
        return dxb + started[group][4][0, 0].astype(dxb.dtype)

    for i in reversed(range(DEPTH)):
        s_mix, s_ffn = saved[i]
        dx, dxb, g_ffn[i] = _ffn_layer_bwd(dx, dxb, row(norm_ffn, i), prepared["ffn"][i], s_ffn)
        if ("ffn", i) in EXCHANGE_AFTER:
            dxb = travel(EXCHANGE_AFTER[("ffn", i)], dxb)
        dx, dxb, g_mix[i] = _mixer_bwd(i, dx, dxb, row(norm_mix, i), prepared, s_mix)
        if ("mix", i) in EXCHANGE_AFTER:
            dxb = travel(EXCHANGE_AFTER[("mix", i)], dxb)
    grads = _collect_grads(g_mix, g_ffn)
    received = [_travel_wait(started[g], dx, per_peer=True, name=f"grad_exchange_wait_{g}") for g in sorted(started)]
    received.append(_exchange(_exchange_operand(grads, EXCHANGE_GROUPS[-1]), name="grad_exchange_last"))
    updated = {}
    for g, pieces in enumerate(EXCHANGE_GROUPS):
        updated.update(_update_group(received[g], pieces, w, m, v, name=f"adamw_sharded_{g}"))

    small_parts = _all_gather(_pack_small(grads), name="small_grad_all_gather")
    outs_small = [_unpack_small(o) for o in
                  _adamw(small_parts, _pack_small(w), _pack_small(m), _pack_small(v), name="adamw_replicated")]

    total_loss = lax.psum(loss[0, 0], ("x", "y", "c"))
    result = [total_loss, dx[None]]
    for k in range(4):
        for n in WEIGHT_ORDER:
            if n not in SHARDED:
                result.append(outs_small[k][n])
            elif (n, None) in updated:
                result.append(updated[(n, None)][k])
            else:
                result.append(jnp.concatenate([updated[(n, l)][k] for l in range(SHARDED[n][0][0])], axis=0))
    return tuple(result)
```

```python
import functools
import math

import jax
import jax.numpy as jnp
from jax import lax
from jax.experimental import pallas as pl
from jax.experimental.pallas import tpu as pltpu

F32 = jnp.float32
BF16 = jnp.bfloat16
MM_DTYPE = BF16

N_DEV = 8
D_MODEL = 1024
DEPTH = 4
RMS_EPS = 1e-6
L2_EPS = 1e-6

LANES = 128

GDN_HEADS = 8
GDN_DK = 128
GDN_DV = 128
GDN_CONV = 4
GDN_CHUNK = 128
GDN_QKV = 3 * GDN_HEADS * GDN_DK
GDN_MAIN = GDN_QKV + GDN_HEADS * GDN_DV
GDN_IN_WIDTH = GDN_MAIN + 2 * GDN_HEADS

DIL_GROUPS = ((128, 1), (512, 4), (2048, 16))
DIL_HEADS = 8
DIL_DH = 64
DIL_SPAN = 128
DIL_SLAB = 3 * DIL_HEADS * LANES
ALIBI_MAX_BIAS = 8.0

FFN_HIDDEN = 2816

ADAM_LR = 0.001
ADAM_B1 = 0.9
ADAM_B2 = 0.999
ADAM_EPS = 1e-08
ADAM_WD = 0.01
ADAM_STEP = 10

VMEM_LIMIT = 56 * 1024 * 1024
NEG = -1e30
HI = lax.Precision.HIGHEST


def _cparams(sem):
    return pltpu.CompilerParams(dimension_semantics=sem, vmem_limit_bytes=VMEM_LIMIT)


def _dot(a, b):
    return lax.dot_general(a, b, (((1,), (0,)), ((), ())), preferred_element_type=F32, precision=HI)


def _dot_nt(a, b):
    return lax.dot_general(a, b, (((1,), (1,)), ((), ())), preferred_element_type=F32, precision=HI)


def _dot_tn(a, b):
    return lax.dot_general(a, b, (((0,), (0,)), ((), ())), preferred_element_type=F32, precision=HI)


def _single_pass(a, b, a_dim, b_dim):
    lead = a.ndim - 2
    batch = ((0,), (0,)) if lead else ((), ())
    return lax.dot_general(a.astype(BF16), b.astype(BF16), (((lead + a_dim,), (lead + b_dim,)), batch),
                           preferred_element_type=F32)


def _bdot(a, b):
    return _single_pass(a, b, 1, 0)


def _bdot_nt(a, b):
    return _single_pass(a, b, 1, 1)


def _bdot_tn(a, b):
    return _single_pass(a, b, 0, 0)


def _pick(n, candidates):
    for c in candidates:
        if n % c == 0:
            return c
    raise ValueError(f"no tile for {n}")


def _matmul(a, b, *, name, trans_a=False, trans_b=False, b_rows=None, a_lead=None, add=None, out_dtype=F32):
    if trans_a:
        k_dim, m_dim = a.shape[-2:]
    else:
        m_dim, k_dim = a.shape[-2:]
    b_start, b_size = b_rows if b_rows is not None else (0, b.shape[0])
    if trans_b:
        n_dim, k2 = b_size, b.shape[1]
    else:
        k2, n_dim = b_size, b.shape[1]
    assert k_dim == k2, (a.shape, b.shape, b_rows)
    tn = _pick(n_dim, (1024, 512, 256, 128))
    tm = min(m_dim, 2048, max(512, (1024 * 1024) // tn))
    tm = _pick(m_dim, (tm, 1408, 1024, 512, 256, 128))
    tk = _pick(k_dim, (1024, 1408, 512, 256, 128))
    nk = k_dim // tk
    has_add = add is not None
    dn = (((0 if trans_a else 1,), (1 if trans_b else 0,)), ((), ()))
    b_tile = tn if trans_b else tk
    assert b_start % b_tile == 0, (b_rows, b_tile)
    b_off = b_start // b_tile

    def body(*refs):
        if has_add:
            a_ref, b_ref, add_ref, o_ref, acc_ref = refs
        else:
            a_ref, b_ref, o_ref, acc_ref = refs
        part = lax.dot_general(a_ref[...], b_ref[...], dn, preferred_element_type=F32)

        def finish(total):
            if has_add:
                total = total + add_ref[...]
            o_ref[...] = total.astype(out_dtype)

        if nk == 1:
            finish(part)
        else:
            k = pl.program_id(2)

            @pl.when(k == 0)
            def _():
                acc_ref[...] = part

            @pl.when(k > 0)
            def _():
                acc_ref[...] += part

            @pl.when(k == nk - 1)
            def _():
                finish(acc_ref[...])

    a_tile = (tk, tm) if trans_a else (tm, tk)
    a_at = (lambda i, j, k: (k, i)) if trans_a else (lambda i, j, k: (i, k))
    if a_lead is None:
        a_spec = pl.BlockSpec(a_tile, a_at)
    else:
        a_spec = pl.BlockSpec((None,) + a_tile, lambda i, j, k: (a_lead,) + a_at(i, j, k))
    if trans_b:
        b_spec = pl.BlockSpec((tn, tk), lambda i, j, k: (j + b_off, k))
    else:
        b_spec = pl.BlockSpec((tk, tn), lambda i, j, k: (k + b_off, j))
    in_specs = [a_spec, b_spec]
    args = [a, b]
    if has_add:
        in_specs.append(pl.BlockSpec((tm, tn), lambda i, j, k: (i, j)))
        args.append(add)
    return pl.pallas_call(
        body,
        grid=(m_dim // tm, n_dim // tn, nk),
        in_specs=in_specs,
        out_specs=pl.BlockSpec((tm, tn), lambda i, j, k: (i, j)),
        out_shape=jax.ShapeDtypeStruct((m_dim, n_dim), out_dtype),
        scratch_shapes=[pltpu.VMEM((tm, tn) if nk > 1 else (8, LANES), F32)],
        compiler_params=_cparams(("parallel", "parallel", "arbitrary")),
        name=name,
    )(*args)


def _rmsnorm_fwd(x, w_row, *, name):
    t, d = x.shape
    tb = min(t, 1024)

    def body(x_ref, w_ref, o_ref):
        xf = x_ref[...]
        r = lax.rsqrt(jnp.mean(xf * xf, axis=-1, keepdims=True) + RMS_EPS)
        o_ref[...] = (xf * r * w_ref[...]).astype(o_ref.dtype)

    return pl.pallas_call(
        body,
        grid=(t // tb,),
        in_specs=[pl.BlockSpec((tb, d), lambda i: (i, 0)), pl.BlockSpec((1, d), lambda i: (0, 0))],
        out_specs=pl.BlockSpec((tb, d), lambda i: (i, 0)),
        out_shape=jax.ShapeDtypeStruct((t, d), MM_DTYPE),
        compiler_params=_cparams(("parallel",)),
        name=name,
    )(x, w_row)


def _rmsnorm_bwd(x, w_row, dy, dskip, *, name):
    t, d = x.shape
    tb = min(t, 512)

    def body(x_ref, w_ref, dy_ref, ds_ref, dx_ref, dxb_ref, dw_ref):
        xf = x_ref[...]
        g = dy_ref[...]
        r = lax.rsqrt(jnp.mean(xf * xf, axis=-1, keepdims=True) + RMS_EPS)
        gw = g * w_ref[...]
        proj = jnp.mean(gw * xf, axis=-1, keepdims=True)
        dx = r * gw - xf * (r * r * r * proj) + ds_ref[...]
        dx_ref[...] = dx
        dxb_ref[...] = dx.astype(dxb_ref.dtype)
        part = jnp.sum(g * xf * r, axis=0, keepdims=True)

        @pl.when(pl.program_id(0) == 0)
        def _():
            dw_ref[...] = part

        @pl.when(pl.program_id(0) > 0)
        def _():
            dw_ref[...] += part

    row = pl.BlockSpec((tb, d), lambda i: (i, 0))
    one = pl.BlockSpec((1, d), lambda i: (0, 0))
    return pl.pallas_call(
        body,
        grid=(t // tb,),
        in_specs=[row, one, row, row],
        out_specs=[row, row, one],
        out_shape=[jax.ShapeDtypeStruct((t, d), F32), jax.ShapeDtypeStruct((t, d), MM_DTYPE),
                   jax.ShapeDtypeStruct((1, d), F32)],
        compiler_params=_cparams(("arbitrary",)),
        name=name,
    )(x, w_row, dy, dskip)


def _silu(z):
    return z / (1.0 + jnp.exp(-z))


FFN_TM, FFN_TN = 512, 1408


def _ffn_in(hn, in_t, *, name):
    t, d = hn.shape
    h = FFN_HIDDEN
    tm, tn = min(t, FFN_TM), FFN_TN
    nj = h // tn
    dn = (((1,), (1,)), ((), ()))

    def body(a_ref, bg_ref, bu_ref, g_ref, u_ref, act_ref):
        a = a_ref[...]
        g = lax.dot_general(a, bg_ref[...], dn, preferred_element_type=F32)
        u = lax.dot_general(a, bu_ref[...], dn, preferred_element_type=F32)
        g_ref[...] = g.astype(g_ref.dtype)
        u_ref[...] = u.astype(u_ref.dtype)
        act_ref[...] = (_silu(g) * u).astype(act_ref.dtype)

    out = pl.BlockSpec((tm, tn), lambda i, j: (i, j))
    return pl.pallas_call(
        body,
        grid=(t // tm, nj),
        in_specs=[pl.BlockSpec((tm, d), lambda i, j: (i, 0)), pl.BlockSpec((tn, d), lambda i, j: (j, 0)),
                  pl.BlockSpec((tn, d), lambda i, j: (j + nj, 0))],
        out_specs=[out, out, out],
        out_shape=[jax.ShapeDtypeStruct((t, h), MM_DTYPE)] * 3,
        compiler_params=_cparams(("parallel", "parallel")),
        name=name,
    )(hn, in_t, in_t)


def _ffn_dact(dy, out_w, g, u, *, name):
    t, d = dy.shape
    h = FFN_HIDDEN
    tm, tn = min(t, FFN_TM), FFN_TN

    def body(a_ref, b_ref, g_ref, u_ref, dg_ref, du_ref):
        da = lax.dot_general(a_ref[...], b_ref[...], (((1,), (1,)), ((), ())), preferred_element_type=F32)
        gate = g_ref[...].astype(F32)
        sig = 1.0 / (1.0 + jnp.exp(-gate))
        sg = gate * sig
        dg_ref[...] = (da * u_ref[...].astype(F32) * (sig + sg * (1.0 - sig))).astype(dg_ref.dtype)
        du_ref[...] = (da * sg).astype(du_ref.dtype)

    blk = pl.BlockSpec((tm, tn), lambda i, j: (i, j))
    return pl.pallas_call(
        body,
        grid=(t // tm, h // tn),
        in_specs=[pl.BlockSpec((tm, d), lambda i, j: (i, 0)), pl.BlockSpec((tn, d), lambda i, j: (j, 0)), blk, blk],
        out_specs=[blk, blk],
        out_shape=[jax.ShapeDtypeStruct((t, h), MM_DTYPE)] * 2,
        compiler_params=_cparams(("parallel", "parallel")),
        name=name,
    )(dy, out_w, g, u)


def _loss_head(y, target, *, name):
    t, d = y.shape
    tb = min(t, 1024)

    def body(y_ref, t_ref, dy_ref, dyb_ref, l_ref):
        err = y_ref[...] - t_ref[...]
        dy_ref[...] = err * (1.0 / d)
        dyb_ref[...] = (err * (1.0 / d)).astype(dyb_ref.dtype)
        part = jnp.sum(jnp.sum(err * err, axis=0, keepdims=True), axis=1, keepdims=True) * (0.5 / d)
        part = jnp.broadcast_to(part, l_ref.shape)

        @pl.when(pl.program_id(0) == 0)
        def _():
            l_ref[...] = part

        @pl.when(pl.program_id(0) > 0)
        def _():
            l_ref[...] += part

    row = pl.BlockSpec((tb, d), lambda i: (i, 0))
    return pl.pallas_call(
        body,
        grid=(t // tb,),
        in_specs=[row, row],
        out_specs=[row, row, pl.BlockSpec((8, LANES), lambda i: (0, 0))],
        out_shape=[jax.ShapeDtypeStruct((t, d), F32), jax.ShapeDtypeStruct((t, d), MM_DTYPE),
                   jax.ShapeDtypeStruct((8, LANES), F32)],
        compiler_params=_cparams(("arbitrary",)),
        name=name,
    )(y, target)


CONV_HALO = 8


def _conv_tile_scale(c):
    is_qk = c < 2 * GDN_HEADS
    scale = jnp.where(c < GDN_HEADS, GDN_DK ** -0.5, 1.0).astype(F32)
    return is_qk, scale


def _gdn_conv_fwd(pm, conv_w, *, name):
    t = pm.shape[0]
    tb = min(t, 1024)
    nt = t // tb
    hb = tb // CONV_HALO

    def body(x_ref, xp_ref, w_ref, o_ref, xe_ref):
        c = pl.program_id(0)
        ti = pl.program_id(1)
        xe_ref[0:CONV_HALO, :] = jnp.where(ti > 0, xp_ref[...], 0.0)
        xe_ref[CONV_HALO:CONV_HALO + tb, :] = x_ref[...]
        w = w_ref[...]
        y = jnp.zeros((tb, LANES), F32)
        for j in range(GDN_CONV):
            off = CONV_HALO - (GDN_CONV - 1) + j
            y = y + w[j:j + 1, :] * xe_ref[pl.ds(off, tb), :]
        s = _silu(y)
        is_qk, scale = _conv_tile_scale(c)
        r = lax.rsqrt(jnp.sum(s * s, axis=-1, keepdims=True) + L2_EPS) * scale
        o_ref[...] = s * jnp.where(is_qk, r, 1.0)

    return pl.pallas_call(
        body,
        grid=(GDN_QKV // LANES, nt),
        in_specs=[
            pl.BlockSpec((tb, LANES), lambda c, i: (i, c)),
            pl.BlockSpec((CONV_HALO, LANES), lambda c, i: (jnp.maximum(i * hb - 1, 0), c)),
            pl.BlockSpec((GDN_CONV, LANES), lambda c, i: (0, c)),
        ],
        out_specs=pl.BlockSpec((tb, LANES), lambda c, i: (i, c)),
        out_shape=jax.ShapeDtypeStruct((t, GDN_QKV), F32),
        scratch_shapes=[pltpu.VMEM((tb + CONV_HALO, LANES), F32)],
        compiler_params=_cparams(("parallel", "parallel")),
        name=name,
    )(pm, pm, conv_w)


def _gdn_conv_bwd(pm, conv_w, dout, *, name):
    t = pm.shape[0]
    tb = min(t, 1024)
    nt = t // tb
    hb = tb // CONV_HALO
    last_hb = t // CONV_HALO - 1
    ext = tb + CONV_HALO

    def body(x_ref, xp_ref, xn_ref, d_ref, dn_ref, w_ref, dx_ref, dw_ref, xe_ref, dy_ref):
        c = pl.program_id(0)
        ti = pl.program_id(1)
        has_next = ti < nt - 1
        xe_ref[0:CONV_HALO, :] = jnp.where(ti > 0, xp_ref[...], 0.0)
        xe_ref[CONV_HALO:CONV_HALO + tb, :] = x_ref[...]
        xe_ref[CONV_HALO + tb:2 * CONV_HALO + tb, :] = jnp.where(has_next, xn_ref[...], 0.0)
        de = jnp.concatenate([d_ref[...], jnp.where(has_next, dn_ref[...], 0.0)], axis=0)
        w = w_ref[...]
        y = jnp.zeros((ext, LANES), F32)
        for j in range(GDN_CONV):
            off = CONV_HALO - (GDN_CONV - 1) + j
            y = y + w[j:j + 1, :] * xe_ref[pl.ds(off, ext), :]
        sig = 1.0 / (1.0 + jnp.exp(-y))
        s = y * sig
        is_qk, scale = _conv_tile_scale(c)
        r = lax.rsqrt(jnp.sum(s * s, axis=-1, keepdims=True) + L2_EPS)
        n = s * r
        dnrm = de * scale
        ds_qk = r * (dnrm - n * jnp.sum(dnrm * n, axis=-1, keepdims=True))
        ds = jnp.where(is_qk, ds_qk, de)
        dy_ref[...] = ds * (sig + s * (1.0 - sig))
        dy = dy_ref[0:tb, :]
        dx = jnp.zeros((tb, LANES), F32)
        dw_rows = []
        for j in range(GDN_CONV):
            sh = GDN_CONV - 1 - j
            dx = dx + w[j:j + 1, :] * dy_ref[pl.ds(sh, tb), :]
            off = CONV_HALO - (GDN_CONV - 1) + j
            dw_rows.append(jnp.sum(dy * xe_ref[pl.ds(off, tb), :], axis=0, keepdims=True))
        dx_ref[...] = dx.astype(dx_ref.dtype)
        part = jnp.concatenate(dw_rows, axis=0)

        @pl.when(ti == 0)
        def _():
            dw_ref[...] = part

        @pl.when(ti > 0)
        def _():
            dw_ref[...] += part

    main = pl.BlockSpec((tb, LANES), lambda c, i: (i, c))
    prev = pl.BlockSpec((CONV_HALO, LANES), lambda c, i: (jnp.maximum(i * hb - 1, 0), c))
    nxt = pl.BlockSpec((CONV_HALO, LANES), lambda c, i: (jnp.minimum((i + 1) * hb, last_hb), c))
    return pl.pallas_call(
        body,
        grid=(GDN_QKV // LANES, nt),
        in_specs=[main, prev, nxt, main, nxt, pl.BlockSpec((GDN_CONV, LANES), lambda c, i: (0, c))],
        out_specs=[main, pl.BlockSpec((GDN_CONV, LANES), lambda c, i: (0, c))],
        out_shape=[jax.ShapeDtypeStruct((t, GDN_QKV), MM_DTYPE), jax.ShapeDtypeStruct((GDN_CONV, GDN_QKV), F32)],
        scratch_shapes=[pltpu.VMEM((tb + 2 * CONV_HALO, LANES), F32), pltpu.VMEM((ext, LANES), F32)],
        compiler_params=_cparams(("parallel", "arbitrary")),
        name=name,
    )(pm, pm, pm, dout, dout, conv_w)


def _head_selector(first_col):
    row = lax.broadcasted_iota(jnp.int32, (LANES, GDN_HEADS * LANES), 0)
    col = lax.broadcasted_iota(jnp.int32, (LANES, GDN_HEADS * LANES), 1)
    return (col // LANES + first_col == row).astype(F32)


def _softplus(x):
    return jnp.maximum(x, 0.0) + jnp.log(1.0 + jnp.exp(-jnp.abs(x)))


def _gdn_gates_fwd(ab, alog_row, dt_row, *, name):
    t = ab.shape[0]
    tb = min(t, 1024)
    wide = GDN_HEADS * LANES

    def body(ab_ref, al_ref, dt_ref, g_ref, b_ref):
        x = ab_ref[...]
        g_cols = -jnp.exp(al_ref[...]) * _softplus(x + dt_ref[...])
        b_cols = 1.0 / (1.0 + jnp.exp(-x))
        g_ref[...] = _dot(g_cols, _head_selector(0))
        b_ref[...] = _dot(b_cols, _head_selector(GDN_HEADS))

    row = pl.BlockSpec((tb, LANES), lambda i: (i, 0))
    one = pl.BlockSpec((1, LANES), lambda i: (0, 0))
    out = pl.BlockSpec((tb, wide), lambda i: (i, 0))
    return pl.pallas_call(
        body,
        grid=(t // tb,),
        in_specs=[row, one, one],
        out_specs=[out, out],
        out_shape=[jax.ShapeDtypeStruct((t, wide), F32)] * 2,
        compiler_params=_cparams(("parallel",)),
        name=name,
    )(ab, alog_row, dt_row)


def _gdn_gates_bwd(ab, alog_row, dt_row, dgb, dbb, *, name):
    t = ab.shape[0]
    tb = min(t, 1024)
    wide = GDN_HEADS * LANES

    def body(ab_ref, al_ref, dt_ref, dg_ref, db_ref, dab_ref, dal_ref, ddt_ref):
        x = ab_ref[...]
        lane = lax.broadcasted_iota(jnp.int32, (tb, LANES), 1)
        dg_cols = _dot_nt(dg_ref[...], _head_selector(0))
        db_cols = _dot_nt(db_ref[...], _head_selector(GDN_HEADS))
        ea = jnp.exp(al_ref[...])
        z = x + dt_ref[...]
        sp = _softplus(z)
        sg = 1.0 / (1.0 + jnp.exp(-z))
        beta = 1.0 / (1.0 + jnp.exp(-x))
        da = jnp.where(lane < GDN_HEADS, dg_cols * (-ea) * sg, 0.0)
        db = jnp.where((lane >= GDN_HEADS) & (lane < 2 * GDN_HEADS), db_cols * beta * (1.0 - beta), 0.0)
        dab_ref[...] = (da + db).astype(dab_ref.dtype)
        p_al = jnp.sum(jnp.where(lane < GDN_HEADS, dg_cols * (-ea) * sp, 0.0), axis=0, keepdims=True)
        p_dt = jnp.sum(da, axis=0, keepdims=True)

        @pl.when(pl.program_id(0) == 0)
        def _():
            dal_ref[...] = p_al
            ddt_ref[...] = p_dt

        @pl.when(pl.program_id(0) > 0)
        def _():
            dal_ref[...] += p_al
            ddt_ref[...] += p_dt

    row = pl.BlockSpec((tb, LANES), lambda i: (i, 0))
    one = pl.BlockSpec((1, LANES), lambda i: (0, 0))
    big = pl.BlockSpec((tb, wide), lambda i: (i, 0))
    return pl.pallas_call(
        body,
        grid=(t // tb,),
        in_specs=[row, one, one, big, big],
        out_specs=[row, one, one],
        out_shape=[jax.ShapeDtypeStruct((t, LANES), MM_DTYPE), jax.ShapeDtypeStruct((1, LANES), F32),
                   jax.ShapeDtypeStruct((1, LANES), F32)],
        compiler_params=_cparams(("arbitrary",)),
        name=name,
    )(ab, alog_row, dt_row, dgb, dbb)


@jax.custom_vjp
def _unit_lower_inverse_rest(n):
    c = n.shape[-1]
    ri = lax.broadcasted_iota(jnp.int32, (c, c), 0)
    ci = lax.broadcasted_iota(jnp.int32, (c, c), 1)
    rest = None
    size = 1
    while size < c:
        joins = ((ri // (2 * size)) == (ci // (2 * size))) & ((ri // size) != (ci // size))
        low = jnp.where(joins, n, 0.0)
        if rest is None:
            rest = -low
        else:
            left = low + _bdot(rest, low)
            rest = rest - (left + _bdot(left, rest))
        size *= 2
    return rest


def _unit_lower_inverse_rest_fwd(n):
    rest = _unit_lower_inverse_rest(n)
    return rest, rest


def _unit_lower_inverse_rest_bwd(rest, ct):
    left = ct + _bdot_tn(rest, ct)
    return (-(left + _bdot_nt(left, rest)),)


_unit_lower_inverse_rest.defvjp(_unit_lower_inverse_rest_fwd, _unit_lower_inverse_rest_bwd)


@jax.custom_vjp
def _known_inverse_rest(n, rest):
    return rest


def _known_inverse_rest_fwd(n, rest):
    return rest, rest


def _known_inverse_rest_bwd(rest, ct):
    return _unit_lower_inverse_rest_bwd(rest, ct) + (jnp.zeros_like(rest),)


_known_inverse_rest.defvjp(_known_inverse_rest_fwd, _known_inverse_rest_bwd)


def _bf16_pieces(x):
    hi = x.astype(BF16)
    r1 = x - hi.astype(F32)
    mid = r1.astype(BF16)
    lo = (r1 - mid.astype(F32)).astype(BF16)
    return hi, mid, lo


def _lower_ones(shape):
    c = shape[-1]
    ri = lax.broadcasted_iota(jnp.int32, (c, c), 0)
    ci = lax.broadcasted_iota(jnp.int32, (c, c), 1)
    return jnp.broadcast_to((ri >= ci).astype(BF16), shape)


@jax.custom_vjp
def _running_sum(x):
    tri = _lower_ones(x.shape)
    return sum(_bdot(tri, p) for p in _bf16_pieces(x))


def _running_sum_fwd(x):
    return _running_sum(x), None


def _running_sum_bwd(_, ct):
    tri = _lower_ones(ct.shape)
    return (sum(_bdot_tn(tri, p) for p in _bf16_pieces(ct)),)


_running_sum.defvjp(_running_sum_fwd, _running_sum_bwd)


def _gdn_prep_math(q, k, v, gb, bb, known_rest=None, with_rest=False):
    c = GDN_CHUNK
    ri = lax.broadcasted_iota(jnp.int32, (c, c), 0)
    ci = lax.broadcasted_iota(jnp.int32, (c, c), 1)
    causal = ri >= ci
    gc = _running_sum(gb)
    decay = jnp.exp(jnp.where(causal, gc - jnp.swapaxes(gc, -1, -2), NEG))
    n = jnp.where(ri > ci, _bdot_nt(k, k) * bb * decay, 0.0)
    rest = _unit_lower_inverse_rest(n) if known_rest is None else _known_inverse_rest(n, known_rest)
    eg = jnp.exp(gc)
    rhs_v = v * bb
    rhs_k = k * bb * eg
    u = rhs_v + _bdot(rest, rhs_v)
    w = rhs_k + _bdot(rest, rhs_k)
    qk = _bdot_nt(q, k) * decay
    qd = q * eg
    last = jnp.sum(jnp.where(ri == c - 1, gc, 0.0), axis=-2, keepdims=True)
    gl = jnp.broadcast_to(last, gc.shape)
    kt = k * jnp.exp(gl - gc)
    cd = jnp.exp(gl)
    return (u, w, qk, qd, kt, cd, rest) if with_rest else (u, w, qk, qd, kt, cd)


def _head_tiles(ref, h):
    return ref[:, h * LANES:(h + 1) * LANES]


def _stack_heads(ref, first=0, heads=GDN_HEADS):
    return jnp.stack([_head_tiles(ref, first + h) for h in range(heads)])


def _store_heads(ref, val, first=0):
    for h in range(val.shape[0]):
        ref[:, (first + h) * LANES:(first + h + 1) * LANES] = val[h].astype(ref.dtype)


def _gdn_prep_fwd(qkv, gb, bb, *, name):
    t = qkv.shape[0]
    c = GDN_CHUNK
    wide = GDN_HEADS * LANES

    def body(q_ref, k_ref, v_ref, g_ref, b_ref, *outs):
        res = _gdn_prep_math(*(_stack_heads(r) for r in (q_ref, k_ref, v_ref, g_ref, b_ref)), with_rest=True)
        for o_ref, val in zip(outs, res):
            _store_heads(o_ref, val)

    blk = lambda off: pl.BlockSpec((c, wide), lambda i: (i, off))
    outs = pl.pallas_call(
        body,
        grid=(t // c,),
        in_specs=[blk(0), blk(1), blk(2), blk(0), blk(0)],
        out_specs=[blk(0)] * 7,
        out_shape=[jax.ShapeDtypeStruct((t, wide), F32)] * 7,
        compiler_params=_cparams(("parallel",)),
        name=name,
    )(qkv, qkv, qkv, gb, bb)
    return tuple(outs[:6]), outs[6]


def _gdn_prep_bwd(qkv, gb, bb, rest, cts, *, name):
    t = qkv.shape[0]
    c = GDN_CHUNK
    wide = GDN_HEADS * LANES

    def body(q_ref, k_ref, v_ref, g_ref, b_ref, r_ref, c0, c1, c2, c3, c4, c5, dqkv_ref, dg_ref, db_ref):
        prim = tuple(_stack_heads(r) for r in (q_ref, k_ref, v_ref, g_ref, b_ref))
        _, pull = jax.vjp(functools.partial(_gdn_prep_math, known_rest=_stack_heads(r_ref)), *prim)
        dq, dk, dv, dg, db = pull(tuple(_stack_heads(r) for r in (c0, c1, c2, c3, c4, c5)))
        _store_heads(dqkv_ref, dq)
        _store_heads(dqkv_ref, dk, first=GDN_HEADS)
        _store_heads(dqkv_ref, dv, first=2 * GDN_HEADS)
        _store_heads(dg_ref, dg)
        _store_heads(db_ref, db)

    blk = lambda off: pl.BlockSpec((c, wide), lambda i: (i, off))
    return pl.pallas_call(
        body,
        grid=(t // c,),
        in_specs=[blk(0), blk(1), blk(2), blk(0), blk(0)] + [blk(0)] * 7,
        out_specs=[pl.BlockSpec((c, 3 * wide), lambda i: (i, 0)), blk(0), blk(0)],
        out_shape=[jax.ShapeDtypeStruct((t, 3 * wide), F32), jax.ShapeDtypeStruct((t, wide), F32),
                   jax.ShapeDtypeStruct((t, wide), F32)],
        compiler_params=_cparams(("parallel",)),
        name=name,
    )(qkv, qkv, qkv, gb, bb, rest, *cts)


def _gdn_scan_math(s, u, w, qk, qd, kt, cd):
    v_new = u - _bdot(w, s)
    o = _bdot(qd, s) + _bdot(qk, v_new)
    s_new = s * cd + _bdot_tn(kt, v_new)
    return o, s_new


def _gdn_scan_fwd(prep, *, name):
    t = prep[0].shape[0]
    c = GDN_CHUNK
    wide = GDN_HEADS * LANES

    def body(u_ref, w_ref, qk_ref, qd_ref, kt_ref, cd_ref, o_ref, st_ref, s_ref):
        @pl.when(pl.program_id(0) == 0)
        def _():
            s_ref[...] = jnp.zeros_like(s_ref)

        s = _stack_heads(s_ref)
        _store_heads(st_ref, s)
        o, s_new = _gdn_scan_math(s, *(_stack_heads(r) for r in (u_ref, w_ref, qk_ref, qd_ref, kt_ref, cd_ref)))
        _store_heads(o_ref, o)
        _store_heads(s_ref, s_new)

    blk = pl.BlockSpec((c, wide), lambda i: (i, 0))
    return pl.pallas_call(
        body,
        grid=(t // c,),
        in_specs=[blk] * 6,
        out_specs=[blk, blk],
        out_shape=[jax.ShapeDtypeStruct((t, wide), F32)] * 2,
        scratch_shapes=[pltpu.VMEM((GDN_DK, wide), F32)],
        compiler_params=_cparams(("arbitrary",)),
        name=name,
    )(*prep)


def _gdn_scan_bwd(prep, states, do, *, name):
    t = do.shape[0]
    c = GDN_CHUNK
    wide = GDN_HEADS * LANES
    nc = t // c

    def body(u_ref, w_ref, qk_ref, qd_ref, kt_ref, cd_ref, st_ref, do_ref, *rest):
        outs, ds_ref = rest[:6], rest[6]

        @pl.when(pl.program_id(0) == 0)
        def _():
            ds_ref[...] = jnp.zeros_like(ds_ref)

        prim = tuple(_stack_heads(r) for r in (st_ref, u_ref, w_ref, qk_ref, qd_ref, kt_ref, cd_ref))
        _, pull = jax.vjp(_gdn_scan_math, *prim)
        grads = pull((_stack_heads(do_ref), _stack_heads(ds_ref)))
        _store_heads(ds_ref, grads[0])
        for o_ref, val in zip(outs, grads[1:]):
            _store_heads(o_ref, val)

    blk = pl.BlockSpec((c, wide), lambda i: (nc - 1 - i, 0))
    return pl.pallas_call(
        body,
        grid=(nc,),
        in_specs=[blk] * 8,
        out_specs=[blk] * 6,
        out_shape=[jax.ShapeDtypeStruct((t, wide), F32)] * 6,
        scratch_shapes=[pltpu.VMEM((GDN_DK, wide), F32)],
        compiler_params=_cparams(("arbitrary",)),
        name=name,
    )(*prep, states, do)


def _gdn_outgate_math(o, z, nw):
    r = lax.rsqrt(jnp.mean(o * o, axis=-1, keepdims=True) + RMS_EPS)
    return o * r * nw * _silu(z)


def _gdn_outgate_fwd(o, pm, nw_row, *, name):
    t = o.shape[0]
    tb = min(t, 1024)
    z_off = GDN_QKV // LANES

    def body(o_ref, z_ref, nw_ref, y_ref):
        y_ref[...] = _gdn_outgate_math(o_ref[...], z_ref[...], nw_ref[...]).astype(y_ref.dtype)

    return pl.pallas_call(
        body,
        grid=(t // tb, GDN_HEADS),
        in_specs=[pl.BlockSpec((tb, LANES), lambda i, h: (i, h)), pl.BlockSpec((tb, LANES), lambda i, h: (i, h + z_off)),
                  pl.BlockSpec((1, LANES), lambda i, h: (0, 0))],
        out_specs=pl.BlockSpec((tb, LANES), lambda i, h: (i, h)),
        out_shape=jax.ShapeDtypeStruct((t, GDN_HEADS * LANES), MM_DTYPE),
        compiler_params=_cparams(("parallel", "parallel")),
        name=name,
    )(o, pm, nw_row)


def _gdn_outgate_bwd(o, pm, nw_row, dy, *, name):
    t = o.shape[0]
    tb = min(t, 1024)
    z_off = GDN_QKV // LANES

    def body(o_ref, z_ref, nw_ref, dy_ref, do_ref, dz_ref, dnw_ref):
        _, pull = jax.vjp(_gdn_outgate_math, o_ref[...], z_ref[...], nw_ref[...])
        d_o, d_z, d_nw = pull(dy_ref[...])
        do_ref[...] = d_o
        dz_ref[...] = d_z.astype(dz_ref.dtype)
        first = (pl.program_id(0) == 0) & (pl.program_id(1) == 0)

        @pl.when(first)
        def _():
            dnw_ref[...] = d_nw

        @pl.when(jnp.logical_not(first))
        def _():
            dnw_ref[...] += d_nw

    blk = pl.BlockSpec((tb, LANES), lambda i, h: (i, h))
    one = pl.BlockSpec((1, LANES), lambda i, h: (0, 0))
    return pl.pallas_call(
        body,
        grid=(t // tb, GDN_HEADS),
        in_specs=[blk, pl.BlockSpec((tb, LANES), lambda i, h: (i, h + z_off)), one, blk],
        out_specs=[blk, blk, one],
        out_shape=[jax.ShapeDtypeStruct((t, GDN_HEADS * LANES), F32),
                   jax.ShapeDtypeStruct((t, GDN_HEADS * LANES), MM_DTYPE), jax.ShapeDtypeStruct((1, LANES), F32)],
        compiler_params=_cparams(("arbitrary", "arbitrary")),
        name=name,
    )(o, pm, nw_row, dy)


def _rms64(x, w_row):
    return x * lax.rsqrt(jnp.sum(x * x, axis=-1, keepdims=True) * (1.0 / DIL_DH) + RMS_EPS) * w_row


def _alibi_slopes(group):
    head = lax.broadcasted_iota(jnp.int32, (DIL_HEADS, 8, LANES), 0).astype(F32)
    rate = -math.log(2.0) * ALIBI_MAX_BIAS / (len(DIL_GROUPS) * DIL_HEADS)
    slope = jnp.exp(rate * (head + float(group * DIL_HEADS + 1)))
    return jnp.broadcast_to(slope[:, 0:1, :], (DIL_HEADS, DIL_SPAN, LANES))


def _band_logits(qn, kp, kc, slope_d, has_prev):
    qi = lax.broadcasted_iota(jnp.int32, (DIL_SPAN, DIL_SPAN), 0)
    kj = lax.broadcasted_iota(jnp.int32, (DIL_SPAN, DIL_SPAN), 1)
    steps_c = (qi - kj).astype(F32)
    scale = DIL_DH ** -0.5
    sp = _bdot_nt(qn, kp) * scale - slope_d * (steps_c + float(DIL_SPAN))
    sc = _bdot_nt(qn, kc) * scale - slope_d * steps_c
    sp = jnp.where((kj >= qi) & has_prev, sp, NEG)
    sc = jnp.where(kj <= qi, sc, NEG)
    return sp, sc


def _dil_attn_fwd(slab, wq_row, wk_row, *, group, name):
    dilation = DIL_GROUPS[group][1]
    t = slab.shape[0]
    rows = t // dilation
    nlb = rows // DIL_SPAN
    wide = DIL_HEADS * LANES
    view = slab.reshape(rows, dilation * DIL_SLAB)

    def body(q_ref, kc_ref, vc_ref, kp_ref, vp_ref, wq_ref, wk_ref, o_ref):
        has_prev = pl.program_id(1) > 0
        lane = lax.broadcasted_iota(jnp.int32, (DIL_SPAN, LANES), 1)
        qn = _rms64(_stack_heads(q_ref), wq_ref[...])
        kc = _rms64(_stack_heads(kc_ref), wk_ref[...])
        kp = _rms64(_stack_heads(kp_ref), wk_ref[...])
        sp, sc = _band_logits(qn, kp, kc, _alibi_slopes(group) * float(dilation), has_prev)
        m = jnp.maximum(jnp.max(sp, axis=-1, keepdims=True), jnp.max(sc, axis=-1, keepdims=True))
        pp = jnp.exp(sp - m)
        pc = jnp.exp(sc - m)
        l = jnp.sum(pp, axis=-1, keepdims=True) + jnp.sum(pc, axis=-1, keepdims=True)
        o = (_bdot(pp, _stack_heads(vp_ref)) + _bdot(pc, _stack_heads(vc_ref))) / l
        _store_heads(o_ref, jnp.where(lane < DIL_DH, o, m + jnp.log(l)))

    cur = lambda part: pl.BlockSpec((DIL_SPAN, wide), lambda r, i: (i, 3 * r + part))
    prv = lambda part: pl.BlockSpec((DIL_SPAN, wide), lambda r, i: (jnp.maximum(i - 1, 0), 3 * r + part))
    one = pl.BlockSpec((1, LANES), lambda r, i: (0, 0))
    out = pl.pallas_call(
        body,
        grid=(dilation, nlb),
        in_specs=[cur(0), cur(1), cur(2), prv(1), prv(2), one, one],
        out_specs=pl.BlockSpec((DIL_SPAN, wide), lambda r, i: (i, r)),
        out_shape=jax.ShapeDtypeStruct((rows, dilation * wide), F32),
        compiler_params=_cparams(("parallel", "parallel")),
        name=name,
    )(view, view, view, view, view, wq_row, wk_row)
    return out.reshape(t, wide)


def _head_slope(group, head):
    idx = jnp.zeros((8, LANES), F32) + head.astype(F32)
    rate = -math.log(2.0) * ALIBI_MAX_BIAS / (len(DIL_GROUPS) * DIL_HEADS)
    slope = jnp.exp(rate * (idx + float(group * DIL_HEADS + 1)))
    return jnp.broadcast_to(slope[0:1, :], (DIL_SPAN, LANES))


def _take_residues(ref, d):
    return jnp.stack([ref[pl.ds(r, DIL_SPAN, stride=d), :] for r in range(d)])


def _put_residues(ref, val, d):
    for r in range(d):
        ref[pl.ds(r, DIL_SPAN, stride=d), :] = val[r]


def _dil_attn_fwd_strided(slab, wq_row, wk_row, *, group, name):
    d = DIL_GROUPS[group][1]
    t = slab.shape[0]
    span = DIL_SPAN * d
    nsb = t // span

    def body(q_ref, kc_ref, vc_ref, kp_ref, vp_ref, wq_ref, wk_ref, o_ref):
        has_prev = pl.program_id(0) > 0
        lane = lax.broadcasted_iota(jnp.int32, (DIL_SPAN, LANES), 1)
        qn = _rms64(_take_residues(q_ref, d), wq_ref[...])
        kc = _rms64(_take_residues(kc_ref, d), wk_ref[...])
        kp = _rms64(_take_residues(kp_ref, d), wk_ref[...])
        sp, sc = _band_logits(qn, kp, kc, _head_slope(group, pl.program_id(1)) * float(d), has_prev)
        m = jnp.maximum(jnp.max(sp, axis=-1, keepdims=True), jnp.max(sc, axis=-1, keepdims=True))
        pp = jnp.exp(sp - m)
        pc = jnp.exp(sc - m)
        l = jnp.sum(pp, axis=-1, keepdims=True) + jnp.sum(pc, axis=-1, keepdims=True)
        o = (_bdot(pp, _take_residues(vp_ref, d)) + _bdot(pc, _take_residues(vc_ref, d))) / l
        _put_residues(o_ref, jnp.where(lane < DIL_DH, o, m + jnp.log(l)), d)

    cur = lambda part: pl.BlockSpec((span, LANES), lambda i, h: (i, part * DIL_HEADS + h))
    prv = lambda part: pl.BlockSpec((span, LANES), lambda i, h: (jnp.maximum(i - 1, 0), part * DIL_HEADS + h))
    one = pl.BlockSpec((1, LANES), lambda i, h: (0, 0))
    return pl.pallas_call(
        body,
        grid=(nsb, DIL_HEADS),
        in_specs=[cur(0), cur(1), cur(2), prv(1), prv(2), one, one],
        out_specs=pl.BlockSpec((span, LANES), lambda i, h: (i, h)),
        out_shape=jax.ShapeDtypeStruct((t, DIL_HEADS * LANES), F32),
        compiler_params=_cparams(("parallel", "parallel")),
        name=name,
    )(slab, slab, slab, slab, slab, wq_row, wk_row)


def _dil_attn_bwd_strided(slab, stat, wq_row, wk_row, dwq_in, dwk_in, *, group, name):
    d = DIL_GROUPS[group][1]
    t = slab.shape[0]
    span = DIL_SPAN * d
    nsb = t // span

    def body(q_ref, kc_ref, vc_ref, kp_ref, vp_ref, st_ref, wq_ref, wk_ref, dwq_in_ref, dwk_in_ref,
             d_ref, dwq_ref, dwk_ref, dk_carry, dv_carry, spread):
        step = pl.program_id(1)
        has_prev = step < nsb - 1
        first = (pl.program_id(0) == 0) & (step == 0)

        @pl.when(step == 0)
        def _():
            dk_carry[...] = jnp.zeros_like(dk_carry)
            dv_carry[...] = jnp.zeros_like(dv_carry)

        @pl.when(first)
        def _():
            dwq_ref[...] = dwq_in_ref[...]
            dwk_ref[...] = dwk_in_ref[...]

        lane = lax.broadcasted_iota(jnp.int32, (DIL_SPAN, LANES), 1)
        scale = DIL_DH ** -0.5
        q_raw = _take_residues(q_ref, d)
        kc_raw = _take_residues(kc_ref, d)
        vc = _take_residues(vc_ref, d)
        kp_raw = _take_residues(kp_ref, d)
        vp = _take_residues(vp_ref, d)
        st = _take_residues(st_ref, d)
        d_o = jnp.where(lane < DIL_DH, st, 0.0)
        lse = jnp.sum(jnp.where(lane == DIL_DH, st, 0.0), axis=-1, keepdims=True)
        delta = jnp.sum(jnp.where(lane == DIL_DH + 1, st, 0.0), axis=-1, keepdims=True)
        qn = _rms64(q_raw, wq_ref[...])
        kc = _rms64(kc_raw, wk_ref[...])
        kp = _rms64(kp_raw, wk_ref[...])
        sp, sc = _band_logits(qn, kp, kc, _head_slope(group, pl.program_id(0)) * float(d), has_prev)
        pp = jnp.exp(sp - lse)
        pc = jnp.exp(sc - lse)
        dsp = pp * (_bdot_nt(d_o, vp) - delta) * scale
        dsc = pc * (_bdot_nt(d_o, vc) - delta) * scale
        dqn = _bdot(dsp, kp) + _bdot(dsc, kc)
        dkc_n = _bdot_tn(dsc, qn) + dk_carry[...]
        dvc = _bdot_tn(pc, d_o) + dv_carry[...]
        dk_carry[...] = _bdot_tn(dsp, qn)
        dv_carry[...] = _bdot_tn(pp, d_o)
        dq_raw, dwq_rows = _rms64_bwd(q_raw, wq_ref[...], dqn)
        dk_raw, dwk_rows = _rms64_bwd(kc_raw, wk_ref[...], dkc_n)
        for part, val in enumerate((dq_raw, dk_raw, dvc)):
            _put_residues(spread, val, d)
            d_ref[part] = spread[...].astype(d_ref.dtype)
        dwq_ref[...] += jnp.sum(jnp.sum(dwq_rows, axis=0), axis=0, keepdims=True)
        dwk_ref[...] += jnp.sum(jnp.sum(dwk_rows, axis=0), axis=0, keepdims=True)

    at = lambda i: nsb - 1 - i
    cur = lambda part: pl.BlockSpec((span, LANES), lambda h, i: (at(i), part * DIL_HEADS + h))
    prv = lambda part: pl.BlockSpec((span, LANES), lambda h, i: (jnp.maximum(at(i) - 1, 0), part * DIL_HEADS + h))
    one = pl.BlockSpec((1, LANES), lambda h, i: (0, 0))
    return pl.pallas_call(
        body,
        grid=(DIL_HEADS, nsb),
        in_specs=[cur(0), cur(1), cur(2), prv(1), prv(2), pl.BlockSpec((span, LANES), lambda h, i: (at(i), h)),
                  one, one, one, one],
        out_specs=[pl.BlockSpec((3, span, LANES), lambda h, i: (0, at(i), h)), one, one],
        out_shape=[jax.ShapeDtypeStruct((3, t, DIL_HEADS * LANES), MM_DTYPE), jax.ShapeDtypeStruct((1, LANES), F32),
                   jax.ShapeDtypeStruct((1, LANES), F32)],
        scratch_shapes=[pltpu.VMEM((d, DIL_SPAN, LANES), F32), pltpu.VMEM((d, DIL_SPAN, LANES), F32),
                        pltpu.VMEM((span, LANES), F32)],
        compiler_params=_cparams(("arbitrary", "arbitrary")),
        name=name,
    )(slab, slab, slab, slab, slab, stat, wq_row, wk_row, dwq_in, dwk_in)


def _dil_merge_fwd(oe, *, name):
    t = oe[0].shape[0]
    tb = min(t, 1024)

    def body(e0, e1, e2, y_ref, om_ref):
        lane = lax.broadcasted_iota(jnp.int32, (tb, LANES), 1)
        es = [e0[...], e1[...], e2[...]]
        lse = [jnp.sum(jnp.where(lane == DIL_DH, e, 0.0), axis=-1, keepdims=True) for e in es]
        top = jnp.maximum(jnp.maximum(lse[0], lse[1]), lse[2])
        joint = top + jnp.log(jnp.exp(lse[0] - top) + jnp.exp(lse[1] - top) + jnp.exp(lse[2] - top))
        o = sum(jnp.exp(l - joint) * e for l, e in zip(lse, es))
        y_ref[...] = jnp.where(lane < DIL_DH, o, 0.0).astype(y_ref.dtype)
        om_ref[...] = jnp.where(lane < DIL_DH, o, joint)

    blk = pl.BlockSpec((tb, LANES), lambda i, h: (i, h))
    return pl.pallas_call(
        body,
        grid=(t // tb, DIL_HEADS),
        in_specs=[blk] * 3,
        out_specs=[blk, blk],
        out_shape=[jax.ShapeDtypeStruct((t, DIL_HEADS * LANES), MM_DTYPE),
                   jax.ShapeDtypeStruct((t, DIL_HEADS * LANES), F32)],
        compiler_params=_cparams(("parallel", "parallel")),
        name=name,
    )(*oe)


def _dil_merge_bwd(dy, om, *, name):
    t = dy.shape[0]
    tb = min(t, 1024)

    def body(dy_ref, om_ref, st_ref):
        lane = lax.broadcasted_iota(jnp.int32, (tb, LANES), 1)
        d_o = jnp.where(lane < DIL_DH, dy_ref[...], 0.0)
        om_t = om_ref[...]
        delta = jnp.sum(d_o * om_t, axis=-1, keepdims=True)
        st_ref[...] = jnp.where(lane < DIL_DH, d_o, jnp.where(lane == DIL_DH, om_t, jnp.where(lane == DIL_DH + 1, delta, 0.0)))

    blk = pl.BlockSpec((tb, LANES), lambda i, h: (i, h))
    return pl.pallas_call(
        body,
        grid=(t // tb, DIL_HEADS),
        in_specs=[blk, blk],
        out_specs=blk,
        out_shape=jax.ShapeDtypeStruct((t, DIL_HEADS * LANES), F32),
        compiler_params=_cparams(("parallel", "parallel")),
        name=name,
    )(dy, om)


def _rms64_bwd(x, w_row, dy):
    r = lax.rsqrt(jnp.sum(x * x, axis=-1, keepdims=True) * (1.0 / DIL_DH) + RMS_EPS)
    gw = dy * w_row
    dx = r * gw - x * (r * r * r * jnp.sum(gw * x, axis=-1, keepdims=True) * (1.0 / DIL_DH))
    return dx, dy * x * r


def _dil_attn_bwd(slab, stat, wq_row, wk_row, dwq_in, dwk_in, *, group, name):
    dilation = DIL_GROUPS[group][1]
    t = slab.shape[0]
    rows = t // dilation
    nlb = rows // DIL_SPAN
    wide = DIL_HEADS * LANES
    view = slab.reshape(rows, dilation * DIL_SLAB)
    stat_view = stat.reshape(rows, dilation * wide)

    def body(cur_ref, kp_ref, vp_ref, st_ref, wq_ref, wk_ref, dwq_in_ref, dwk_in_ref, d_ref, dwq_ref, dwk_ref,
             dk_carry, dv_carry):
        step = pl.program_id(1)
        has_prev = step < nlb - 1
        first = (pl.program_id(0) == 0) & (step == 0)

        @pl.when(step == 0)
        def _():
            dk_carry[...] = jnp.zeros_like(dk_carry)
            dv_carry[...] = jnp.zeros_like(dv_carry)

        @pl.when(first)
        def _():
            dwq_ref[...] = dwq_in_ref[...]
            dwk_ref[...] = dwk_in_ref[...]

        lane = lax.broadcasted_iota(jnp.int32, (DIL_SPAN, LANES), 1)
        scale = DIL_DH ** -0.5
        q_raw = _stack_heads(cur_ref)
        kc_raw = _stack_heads(cur_ref, first=DIL_HEADS)
        vc = _stack_heads(cur_ref, first=2 * DIL_HEADS)
        kp_raw = _stack_heads(kp_ref)
        vp = _stack_heads(vp_ref)
        st = _stack_heads(st_ref)
        d_o = jnp.where(lane < DIL_DH, st, 0.0)
        lse = jnp.sum(jnp.where(lane == DIL_DH, st, 0.0), axis=-1, keepdims=True)
        delta = jnp.sum(jnp.where(lane == DIL_DH + 1, st, 0.0), axis=-1, keepdims=True)
        qn = _rms64(q_raw, wq_ref[...])
        kc = _rms64(kc_raw, wk_ref[...])
        kp = _rms64(kp_raw, wk_ref[...])
        sp, sc = _band_logits(qn, kp, kc, _alibi_slopes(group) * float(dilation), has_prev)
        pp = jnp.exp(sp - lse)
        pc = jnp.exp(sc - lse)
        dsp = pp * (_bdot_nt(d_o, vp) - delta) * scale
        dsc = pc * (_bdot_nt(d_o, vc) - delta) * scale
        dqn = _bdot(dsp, kp) + _bdot(dsc, kc)
        dkc_n = _bdot_tn(dsc, qn) + _stack_heads(dk_carry)
        dvc = _bdot_tn(pc, d_o) + _stack_heads(dv_carry)
        _store_heads(dk_carry, _bdot_tn(dsp, qn))
        _store_heads(dv_carry, _bdot_tn(pp, d_o))
        dq_raw, dwq_rows = _rms64_bwd(q_raw, wq_ref[...], dqn)
        dk_raw, dwk_rows = _rms64_bwd(kc_raw, wk_ref[...], dkc_n)
        _store_heads(d_ref, dq_raw)
        _store_heads(d_ref, dk_raw, first=DIL_HEADS)
        _store_heads(d_ref, dvc, first=2 * DIL_HEADS)
        dwq_ref[...] += jnp.sum(jnp.sum(dwq_rows, axis=0), axis=0, keepdims=True)
        dwk_ref[...] += jnp.sum(jnp.sum(dwk_rows, axis=0), axis=0, keepdims=True)

    blk_i = lambda i: nlb - 1 - i
    cur = pl.BlockSpec((DIL_SPAN, DIL_SLAB), lambda r, i: (blk_i(i), r))
    prv = lambda part: pl.BlockSpec((DIL_SPAN, wide), lambda r, i: (jnp.maximum(blk_i(i) - 1, 0), 3 * r + part))
    one = pl.BlockSpec((1, LANES), lambda r, i: (0, 0))
    dslab, dwq, dwk = pl.pallas_call(
        body,
        grid=(dilation, nlb),
        in_specs=[cur, prv(1), prv(2), pl.BlockSpec((DIL_SPAN, wide), lambda r, i: (blk_i(i), r)), one, one, one, one],
        out_specs=[cur, one, one],
        out_shape=[jax.ShapeDtypeStruct((rows, dilation * DIL_SLAB), MM_DTYPE), jax.ShapeDtypeStruct((1, LANES), F32),
                   jax.ShapeDtypeStruct((1, LANES), F32)],
        scratch_shapes=[pltpu.VMEM((DIL_SPAN, wide), F32), pltpu.VMEM((DIL_SPAN, wide), F32)],
        compiler_params=_cparams(("arbitrary", "arbitrary")),
        name=name,
    )(view, view, view, stat_view, wq_row, wk_row, dwq_in, dwk_in)
    return dslab.reshape(t, DIL_SLAB), dwq, dwk


def _row(v, width=LANES):
    v = v.astype(F32).reshape(-1)
    return jnp.pad(v, (0, width - v.shape[0])).reshape(1, width)


def _prepare_weights(w):
    return dict(gdn=_prepare_gdn(w), dil=_prepare_dil(w), ffn=_prepare_ffn(w))


def _prepare_gdn(w, layers=range(DEPTH // 2)):
    gdn = {}
    for j in layers:
        wt = w["gdn_w_in"][j]
        gates_t = jnp.pad(wt[GDN_MAIN:], ((0, LANES - 2 * GDN_HEADS), (0, 0)))
        gdn[j] = dict(in_t=wt, gates_t=gates_t, out=w["gdn_w_out"][j], conv=w["gdn_conv_w"][j].astype(F32),
                      alog=_row(w["gdn_a_log"][j]), dt=_row(w["gdn_dt_bias"][j]), nw=_row(w["gdn_norm_w"][j]))
    return gdn


def _prepare_dil(w, layers=range(DEPTH // 2)):
    d = D_MODEL
    dil = {}
    for j in layers:
        wt = w["dil_w_in"][j].reshape(3, len(DIL_GROUPS), DIL_HEADS, DIL_DH, d)
        wg_t = [jnp.pad(wt[:, g], ((0, 0), (0, 0), (0, LANES - DIL_DH), (0, 0))).reshape(DIL_SLAB, d)
                for g in range(len(DIL_GROUPS))]
        out_t = jnp.pad(w["dil_w_out"][j].reshape(d, DIL_HEADS, DIL_DH), ((0, 0), (0, 0), (0, LANES - DIL_DH)))
        dil[j] = dict(wg_t=wg_t, out_t=out_t.reshape(d, DIL_HEADS * LANES), wq=_row(w["dil_q_norm"][j]),
                      wk=_row(w["dil_k_norm"][j]))
    return dil


def _prepare_ffn(w, layers=range(DEPTH)):
    return {i: dict(in_t=w["ffn_w_in"][i], out=w["ffn_w_out"][i]) for i in layers}


def _gdn_layer_fwd(x, nrow, p):
    hn = _rmsnorm_fwd(x, nrow, name="rmsnorm_fwd")
    pm = _matmul(hn, p["in_t"], trans_b=True, b_rows=(0, GDN_MAIN), name="gdn_proj_main")
    ab = _matmul(hn, p["gates_t"], trans_b=True, name="gdn_proj_gates")
    qkv = _gdn_conv_fwd(pm, p["conv"], name="gdn_conv_fwd")
    gb, bb = _gdn_gates_fwd(ab, p["alog"], p["dt"], name="gdn_gates_fwd")
    prep, rest = _gdn_prep_fwd(qkv, gb, bb, name="gdn_prep_fwd")
    o, states = _gdn_scan_fwd(prep, name="gdn_scan_fwd")
    og = _gdn_outgate_fwd(o, pm, p["nw"], name="gdn_outgate_fwd")
    y = _matmul(og, p["out"], add=x, name="gdn_proj_out")
    return y, (x, hn, pm, ab, qkv, gb, bb, prep, rest, states, o, og)


def _gdn_layer_bwd(dx, dxb, nrow, p, saved):
    x, hn, pm, ab, qkv, gb, bb, prep, rest, states, o, og = saved
    d_og = _matmul(dxb, p["out"], trans_b=True, name="gdn_dgate")
    g_out = _matmul(og, dxb, trans_a=True, out_dtype=MM_DTYPE, name="gdn_gw_out")
    d_o, d_z, d_nw = _gdn_outgate_bwd(o, pm, p["nw"], d_og, name="gdn_outgate_bwd")
    cts = _gdn_scan_bwd(prep, states, d_o, name="gdn_scan_bwd")
    dqkv, dgb, dbb = _gdn_prep_bwd(qkv, gb, bb, rest, cts, name="gdn_prep_bwd")
    d_ab, d_alog, d_dt = _gdn_gates_bwd(ab, p["alog"], p["dt"], dgb, dbb, name="gdn_gates_bwd")
    d_conv, g_conv = _gdn_conv_bwd(pm, p["conv"], dqkv, name="gdn_conv_bwd")
    d_hn = _matmul(d_conv, p["in_t"], b_rows=(0, GDN_QKV), name="gdn_dhn_qkv")
    d_hn = _matmul(d_z, p["in_t"], b_rows=(GDN_QKV, GDN_MAIN - GDN_QKV), add=d_hn, name="gdn_dhn_z")
    d_hn = _matmul(d_ab, p["gates_t"], add=d_hn, name="gdn_dhn_gates")
    g_in_t = jnp.concatenate([
        _matmul(d_conv, hn, trans_a=True, out_dtype=MM_DTYPE, name="gdn_gw_qkv"),
        _matmul(d_z, hn, trans_a=True, out_dtype=MM_DTYPE, name="gdn_gw_z"),
        _matmul(d_ab, hn, trans_a=True, out_dtype=MM_DTYPE, name="gdn_gw_gates")[:2 * GDN_HEADS],
    ], axis=0)
    dx_new, dxb_new, g_norm = _rmsnorm_bwd(x, nrow, d_hn, dx, name="rmsnorm_bwd")
    grads = dict(w_in=g_in_t, conv=g_conv, a_log=d_alog[0, :GDN_HEADS], dt_bias=d_dt[0, :GDN_HEADS], norm_w=d_nw[0],
                 w_out=g_out, norm=g_norm[0])
    return dx_new, dxb_new, grads


def _dil_layer_fwd(x, nrow, p):
    hn = _rmsnorm_fwd(x, nrow, name="rmsnorm_fwd")
    slabs = [_matmul(hn, p["wg_t"][g], trans_b=True, name="dil_proj_in") for g in range(len(DIL_GROUPS))]
    oe = [(_dil_attn_fwd if DIL_GROUPS[g][1] == 1 else _dil_attn_fwd_strided)(
        slabs[g], p["wq"], p["wk"], group=g, name=f"dil_attn_fwd_g{g}") for g in range(len(DIL_GROUPS))]
    y, om = _dil_merge_fwd(oe, name="dil_merge_fwd")
    out = _matmul(y, p["out_t"], trans_b=True, add=x, name="dil_proj_out")
    return out, (x, hn, slabs, y, om)


def _dil_layer_bwd(dx, dxb, nrow, p, saved):
    x, hn, slabs, y, om = saved
    d_y = _matmul(dxb, p["out_t"], name="dil_dmerged")
    g_out_t = _matmul(dxb, y, trans_a=True, out_dtype=MM_DTYPE, name="dil_gw_out")
    g_out_t = g_out_t.reshape(D_MODEL, DIL_HEADS, LANES)[..., :DIL_DH].reshape(D_MODEL, DIL_HEADS * DIL_DH)
    stat = _dil_merge_bwd(d_y, om, name="dil_merge_bwd")
    d_hn = None
    dwq = jnp.zeros((1, LANES), F32)
    dwk = jnp.zeros((1, LANES), F32)
    g_groups = []
    wide = DIL_HEADS * LANES
    for g in range(len(DIL_GROUPS)):
        if DIL_GROUPS[g][1] == 1:
            dslab, dwq, dwk = _dil_attn_bwd(slabs[g], stat, p["wq"], p["wk"], dwq, dwk, group=g, name=f"dil_attn_bwd_g{g}")
            d_hn = _matmul(dslab, p["wg_t"][g], add=d_hn, name="dil_dhn")
            g_w = _matmul(dslab, hn, trans_a=True, out_dtype=MM_DTYPE, name="dil_gw_in")
        else:
            dparts, dwq, dwk = _dil_attn_bwd_strided(slabs[g], stat, p["wq"], p["wk"], dwq, dwk, group=g,
                                                     name=f"dil_attn_bwd_g{g}")
            for part in range(3):
                d_hn = _matmul(dparts, p["wg_t"][g], a_lead=part, b_rows=(part * wide, wide), add=d_hn, name="dil_dhn_part")
            g_w = jnp.stack([_matmul(dparts, hn, trans_a=True, a_lead=part, out_dtype=MM_DTYPE, name="dil_gw_in_part") for part in range(3)])
        g_groups.append(g_w.reshape(3, DIL_HEADS, LANES, D_MODEL)[:, :, :DIL_DH])
    g_in_t = jnp.stack(g_groups, axis=1).reshape(3 * len(DIL_GROUPS) * DIL_HEADS * DIL_DH, D_MODEL)
    dx_new, dxb_new, g_norm = _rmsnorm_bwd(x, nrow, d_hn, dx, name="rmsnorm_bwd")
    grads = dict(w_in=g_in_t, q_norm=dwq[0, :DIL_DH], k_norm=dwk[0, :DIL_DH], w_out=g_out_t, norm=g_norm[0])
    return dx_new, dxb_new, grads


def _ffn_layer_fwd(x, nrow, p):
    hn = _rmsnorm_fwd(x, nrow, name="rmsnorm_fwd")
    gate, up, act = _ffn_in(hn, p["in_t"], name="ffn_proj_in")
    y = _matmul(act, p["out"], add=x, name="ffn_proj_out")
    return y, (x, hn, gate, up, act)


def _ffn_layer_bwd(dx, dxb, nrow, p, saved):
    x, hn, gate, up, act = saved
    g_out = _matmul(act, dxb, trans_a=True, out_dtype=MM_DTYPE, name="ffn_gw_out")
    d_g, d_u = _ffn_dact(dxb, p["out"], gate, up, name="ffn_dact")
    d_hn = _matmul(d_g, p["in_t"], b_rows=(0, FFN_HIDDEN), name="ffn_dhn_gate")
    d_hn = _matmul(d_u, p["in_t"], b_rows=(FFN_HIDDEN, FFN_HIDDEN), add=d_hn, name="ffn_dhn_up")
    g_in_t = jnp.concatenate([_matmul(d_g, hn, trans_a=True, out_dtype=MM_DTYPE, name="ffn_gw_gate"),
                              _matmul(d_u, hn, trans_a=True, out_dtype=MM_DTYPE, name="ffn_gw_up")], axis=0)
    dx_new, dxb_new, g_norm = _rmsnorm_bwd(x, nrow, d_hn, dx, name="rmsnorm_bwd")
    return dx_new, dxb_new, dict(w_in=g_in_t, w_out=g_out, norm=g_norm[0])


def _mixer_fwd(i, x, mix_row, prepared):
    if i % 2 == 0:
        return _gdn_layer_fwd(x, mix_row, prepared["gdn"][i // 2])
    return _dil_layer_fwd(x, mix_row, prepared["dil"][i // 2])


def _mixer_bwd(i, dx, dxb, mix_row, prepared, saved):
    if i % 2 == 0:
        return _gdn_layer_bwd(dx, dxb, mix_row, prepared["gdn"][i // 2], saved)
    return _dil_layer_bwd(dx, dxb, mix_row, prepared["dil"][i // 2], saved)


def _local_step(x, target, prepared, norm_mix, norm_ffn):
    saved = []
    for i in range(DEPTH):
        x, s_mix = _mixer_fwd(i, x, norm_mix[i].reshape(1, D_MODEL), prepared)
        x, s_ffn = _ffn_layer_fwd(x, norm_ffn[i].reshape(1, D_MODEL), prepared["ffn"][i])
        saved.append((s_mix, s_ffn))
    dx, dxb, loss = _loss_head(x, target, name="loss_head")
    g_mix, g_ffn = [None] * DEPTH, [None] * DEPTH
    for i in reversed(range(DEPTH)):
        s_mix, s_ffn = saved[i]
        dx, dxb, g_ffn[i] = _ffn_layer_bwd(dx, dxb, norm_ffn[i].reshape(1, D_MODEL), prepared["ffn"][i], s_ffn)
        dx, dxb, g_mix[i] = _mixer_bwd(i, dx, dxb, norm_mix[i].reshape(1, D_MODEL), prepared, s_mix)
    return loss[0, 0], dx, _collect_grads(g_mix, g_ffn)


def _collect_grads(g_mix, g_ffn):
    gdn = [g_mix[i] for i in range(0, DEPTH, 2)]
    dil = [g_mix[i] for i in range(1, DEPTH, 2)]
    if any(g is None for g in g_mix + g_ffn):
        pick = lambda gs, key: [None if g is None else g[key] for g in gs]
        return dict(gdn_w_in=pick(gdn, "w_in"), gdn_w_out=pick(gdn, "w_out"), dil_w_in=pick(dil, "w_in"),
                    dil_w_out=pick(dil, "w_out"), ffn_w_in=pick(g_ffn, "w_in"), ffn_w_out=pick(g_ffn, "w_out"))
    grads = dict(
        norm_mix=jnp.stack([g["norm"] for g in g_mix]),
        norm_ffn=jnp.stack([g["norm"] for g in g_ffn]),
        gdn_w_in=[g["w_in"] for g in gdn],
        gdn_conv_w=jnp.stack([g["conv"] for g in gdn]),
        gdn_a_log=jnp.stack([g["a_log"] for g in gdn]),
        gdn_dt_bias=jnp.stack([g["dt_bias"] for g in gdn]),
        gdn_norm_w=jnp.stack([g["norm_w"] for g in gdn]),
        gdn_w_out=[g["w_out"] for g in gdn],
        dil_w_in=[g["w_in"] for g in dil],
        dil_q_norm=jnp.stack([g["q_norm"] for g in dil]),
        dil_k_norm=jnp.stack([g["k_norm"] for g in dil]),
        dil_w_out=[g["w_out"] for g in dil],
        ffn_w_in=[g["w_in"] for g in g_ffn],
        ffn_w_out=[g["w_out"] for g in g_ffn],
    )
    return grads


MESH_ID = pl.DeviceIdType.MESH
ANY_SPACE = pl.BlockSpec(memory_space=pl.ANY)


def _mesh_position():
    return lax.axis_index("x"), lax.axis_index("y"), lax.axis_index("c")


def _flip(pos, k):
    x, y, c = pos
    return (1 - x if k & 4 else x, 1 - y if k & 2 else y, 1 - c if k & 1 else c)


def _linear(pos):
    return 4 * pos[0] + 2 * pos[1] + pos[2]


def _comm_scratch():
    return [pltpu.SemaphoreType.DMA((N_DEV - 1,)), pltpu.SemaphoreType.DMA((N_DEV - 1,)), pltpu.SemaphoreType.DMA(())]


def _all_gather(shard, *, name):
    def body(x_ref, out_ref, send_sems, recv_sems, local_sem):
        me = _mesh_position()
        mine = out_ref.at[_linear(me)]
        local = pltpu.make_async_copy(x_ref, mine, local_sem)
        local.start()
        copies = []
        for k in range(1, N_DEV):
            cp = pltpu.make_async_remote_copy(src_ref=x_ref, dst_ref=mine, send_sem=send_sems.at[k - 1],
                                              recv_sem=recv_sems.at[k - 1], device_id=_flip(me, k), device_id_type=MESH_ID)
            cp.start()
            copies.append(cp)
        for cp in copies:
            cp.wait()
        local.wait()

    return pl.pallas_call(
        body,
        out_shape=jax.ShapeDtypeStruct((N_DEV,) + shard.shape, shard.dtype),
        in_specs=[ANY_SPACE],
        out_specs=ANY_SPACE,
        scratch_shapes=_comm_scratch(),
        name=name,
    )(shard)


def _exchange(parts, *, name):
    def body(p_ref, out_ref, send_sems, recv_sems, local_sem):
        me = _mesh_position()
        mine = out_ref.at[_linear(me)]
        local = pltpu.make_async_copy(p_ref.at[_linear(me)], mine, local_sem)
        local.start()
        copies = []
        for k in range(1, N_DEV):
            peer = _flip(me, k)
            cp = pltpu.make_async_remote_copy(src_ref=p_ref.at[_linear(peer)], dst_ref=mine, send_sem=send_sems.at[k - 1],
                                              recv_sem=recv_sems.at[k - 1], device_id=peer, device_id_type=MESH_ID)
            cp.start()
            copies.append(cp)
        for cp in copies:
            cp.wait()
        local.wait()

    return pl.pallas_call(
        body,
        out_shape=jax.ShapeDtypeStruct(parts.shape, parts.dtype),
        in_specs=[ANY_SPACE],
        out_specs=ANY_SPACE,
        scratch_shapes=_comm_scratch(),
        name=name,
    )(parts)


HBM_SPACE = pl.BlockSpec(memory_space=pltpu.HBM)
SEM_SPACE = pl.BlockSpec(memory_space=pltpu.SEMAPHORE)
DATAFLOW = pltpu.SideEffectType.DATAFLOW_SIDE_EFFECTING


def _split_copies(src_ref, land_ref, send_sems, recv_sems, per_peer):
    me = _mesh_position()
    mine = land_ref.at[_linear(me)]
    copies = []
    for k in range(1, N_DEV):
        peer = _flip(me, k)
        src = src_ref.at[_linear(peer)] if per_peer else src_ref
        copies.append(pltpu.make_async_remote_copy(src_ref=src, dst_ref=mine, send_sem=send_sems.at[k - 1],
                                                   recv_sem=recv_sems.at[k - 1], device_id=peer, device_id_type=MESH_ID))
    return copies


def _travel_start(src, after, *, per_peer, name):
    me = _linear(_mesh_position())
    own = src[me] if per_peer else src
    shape = own.shape
    landing = lax.dynamic_update_slice(lax.empty((N_DEV,) + shape, src.dtype), own[None], (me, 0, 0))

    def body(src_ref, land_ref, after_ref, send_sems, recv_sems, src_thru, land_thru, token):
        for cp in _split_copies(src_ref, land_ref, send_sems, recv_sems, per_peer):
            cp.start()
        token[...] = jnp.zeros_like(token)

    return pl.pallas_call(
        body,
        name=name,
        out_shape=(pltpu.SemaphoreType.DMA((N_DEV - 1,)), pltpu.SemaphoreType.DMA((N_DEV - 1,)),
                   pltpu.HBM(src.shape, src.dtype), pltpu.HBM(landing.shape, landing.dtype),
                   jax.ShapeDtypeStruct((8, LANES), F32)),
        in_specs=(HBM_SPACE, HBM_SPACE, ANY_SPACE),
        out_specs=(SEM_SPACE, SEM_SPACE, HBM_SPACE, HBM_SPACE, pl.BlockSpec(memory_space=pltpu.VMEM)),
        input_output_aliases={0: 2, 1: 3},
        compiler_params=pltpu.CompilerParams(has_side_effects=DATAFLOW),
    )(pltpu.with_memory_space_constraint(src, pltpu.HBM), pltpu.with_memory_space_constraint(landing, pltpu.HBM), after)


def _travel_wait(started, after, *, per_peer, name):
    send_sems, recv_sems, src_thru, land_thru, _ = started

    def body(src_ref, land_ref, send_sems, recv_sems, after_ref, src_dead, got_ref):
        for cp in _split_copies(src_ref, land_ref, send_sems, recv_sems, per_peer):
            cp.wait_send()
            cp.wait_recv()

    return pl.pallas_call(
        body,
        name=name,
        out_shape=(pltpu.HBM(src_thru.shape, src_thru.dtype), pltpu.HBM(land_thru.shape, land_thru.dtype)),
        in_specs=(HBM_SPACE, HBM_SPACE, SEM_SPACE, SEM_SPACE, ANY_SPACE),
        out_specs=(HBM_SPACE, HBM_SPACE),
        input_output_aliases={0: 0, 1: 1},
        compiler_params=pltpu.CompilerParams(has_side_effects=DATAFLOW),
    )(src_thru, land_thru, send_sems, recv_sems, after)[1]


def _adamw(parts, w, m, v, *, name):
    rows, n = w.shape
    tb = _pick(rows, (PACK_ROW_ALIGN, 16))
    c1 = 1.0 - ADAM_B1 ** ADAM_STEP
    c2 = 1.0 - ADAM_B2 ** ADAM_STEP

    def body(p_ref, w_ref, m_ref, v_ref, g_ref, d_ref, nm_ref, nv_ref):
        g = p_ref[0].astype(F32)
        for s in range(1, N_DEV):
            g = g + p_ref[s].astype(F32)
        m_new = ADAM_B1 * m_ref[...] + (1.0 - ADAM_B1) * g
        v_new = ADAM_B2 * v_ref[...] + (1.0 - ADAM_B2) * (g * g)
        m_hat = m_new / c1
        v_hat = v_new / c2
        g_ref[...] = g
        nm_ref[...] = m_new
        nv_ref[...] = v_new
        d_ref[...] = -ADAM_LR * (m_hat / (jnp.sqrt(v_hat) + ADAM_EPS) + ADAM_WD * w_ref[...])

    blk = pl.BlockSpec((tb, n), lambda i: (i, 0))
    return pl.pallas_call(
        body,
        grid=(rows // tb,),
        in_specs=[pl.BlockSpec((N_DEV, tb, n), lambda i: (0, i, 0)), blk, blk, blk],
        out_specs=[blk] * 4,
        out_shape=[jax.ShapeDtypeStruct((rows, n), F32)] * 4,
        compiler_params=_cparams(("parallel",)),
        name=name,
    )(parts, w, m, v)


PACK_WIDTH = 1024
SHARDED = {
    "gdn_w_in": ((2, D_MODEL, GDN_IN_WIDTH), 2),
    "gdn_conv_w": ((2, GDN_CONV, GDN_QKV), 2),
    "gdn_w_out": ((2, GDN_HEADS * GDN_DV, D_MODEL), 1),
    "dil_w_in": ((2, D_MODEL, 3 * len(DIL_GROUPS) * DIL_HEADS * DIL_DH), 2),
    "dil_w_out": ((2, DIL_HEADS * DIL_DH, D_MODEL), 2),
    "ffn_w_in": ((DEPTH, D_MODEL, 2 * FFN_HIDDEN), 2),
    "ffn_w_out": ((DEPTH, FFN_HIDDEN, D_MODEL), 1),
}
REPLICATED = {"norm_mix": (DEPTH, D_MODEL), "norm_ffn": (DEPTH, D_MODEL), "gdn_a_log": (2, GDN_HEADS),
              "gdn_dt_bias": (2, GDN_HEADS), "gdn_norm_w": (2, GDN_DV), "dil_q_norm": (2, DIL_DH), "dil_k_norm": (2, DIL_DH)}
WEIGHT_ORDER = ("norm_mix", "norm_ffn", "gdn_w_in", "gdn_conv_w", "gdn_a_log", "gdn_dt_bias", "gdn_norm_w", "gdn_w_out",
                "dil_w_in", "dil_q_norm", "dil_k_norm", "dil_w_out", "ffn_w_in", "ffn_w_out")
PACK_ROW_ALIGN = 128
PIECE_ALIGN = 16
SMALL_ROWS = 16


def _shard_shape(name):
    shape, axis = SHARDED[name]
    return tuple(s // N_DEV if i == axis else s for i, s in enumerate(shape))


def _shard_rows(name):
    return math.prod(_shard_shape(name)) // PACK_WIDTH


def _split_shards(full, name):
    shape, axis = SHARDED[name]
    split = full.reshape(shape[:axis] + (N_DEV, shape[axis] // N_DEV) + shape[axis + 1:])
    return jnp.moveaxis(split, axis, 0)


def _join_shards(stacked, name):
    shape, axis = SHARDED[name]
    return jnp.moveaxis(stacked, 0, axis).reshape(shape)


COLUMN_SHARDED = ("gdn_w_in", "dil_w_in", "dil_w_out", "ffn_w_in")


def _to_rows(shard, name):
    if name in COLUMN_SHARDED:
        shard = jnp.swapaxes(shard, 1, 2)
    return shard.reshape(-1, PACK_WIDTH)


def _layer_columns(name):
    _, r, c = _shard_shape(name)
    return r if name in COLUMN_SHARDED else c


def _piece_rows(piece, halves=1):
    name, layer = piece
    rows = _shard_rows(name) * halves
    return rows if layer is None else rows // SHARDED[name][0][0]


def _aligned(rows, to=PIECE_ALIGN):
    return -(-rows // to) * to


def _pack_pieces(arrays, total_align=PIECE_ALIGN):
    padded, total = [], 0
    for a in arrays:
        rows = a.shape[-2]
        extra = _aligned(rows) - rows
        if extra:
            a = jnp.pad(a, [(0, 0)] * (a.ndim - 2) + [(0, extra), (0, 0)])
        padded.append(a)
        total += rows + extra
    tail = _aligned(total, total_align) - total
    if tail:
        padded.append(jnp.zeros(padded[0].shape[:-2] + (tail, PACK_WIDTH), padded[0].dtype))
    return jnp.concatenate(padded, axis=-2)


def _piece_offsets(pieces, halves=None):
    out, at = [], 0
    for p in pieces:
        rows = _piece_rows(p, (halves or {}).get(p[0], 1))
        out.append((p, at, rows))
        at += _aligned(rows)
    return out


def _shard_piece_rows(src, piece):
    name, layer = piece
    part = src[name] if layer is None else src[name][layer:layer + 1]
    return _to_rows(part.astype(F32), name)


def _piece_from_rows(rows, piece):
    name, layer = piece
    layers, r, c = _shard_shape(name)
    n_l = layers if layer is None else 1
    if name in COLUMN_SHARDED:
        return jnp.swapaxes(rows.reshape(n_l, c, r), 1, 2)
    return rows.reshape(n_l, r, c)


SMALL_TAIL = tuple(n for n in REPLICATED if n not in ("norm_mix", "norm_ffn"))


def _pack_small(vals):
    tail, at = jnp.zeros((PACK_WIDTH,), F32), 0
    for n in SMALL_TAIL:
        vec = vals[n].astype(F32).reshape(-1)
        tail = tail + jnp.pad(vec, (at, PACK_WIDTH - at - vec.shape[0]))
        at += vec.shape[0]
    buf = jnp.pad(vals["norm_mix"].astype(F32), ((0, SMALL_ROWS - DEPTH), (0, 0)))
    buf = buf + jnp.pad(vals["norm_ffn"].astype(F32), ((8, SMALL_ROWS - 8 - DEPTH), (0, 0)))
    return buf + jnp.pad(tail.reshape(1, PACK_WIDTH), ((SMALL_ROWS - 1, 0), (0, 0)))


def _unpack_small(buf):
    out = {"norm_mix": buf[0:DEPTH], "norm_ffn": buf[8:8 + DEPTH]}
    at = 0
    for n in SMALL_TAIL:
        size = math.prod(REPLICATED[n])
        out[n] = buf[SMALL_ROWS - 1, at:at + size].reshape(REPLICATED[n])
        at += size
    return out


GATHER_FIRST = (("gdn_w_in", 0), ("gdn_conv_w", None), ("gdn_w_out", 0))
GATHER_NEXT = (("ffn_w_in", 0), ("ffn_w_out", 0), ("dil_w_in", 0), ("dil_w_out", 0))
GATHER_LAST = (("ffn_w_in", 1), ("ffn_w_out", 1), ("gdn_w_in", 1), ("gdn_w_out", 1), ("ffn_w_in", 2), ("ffn_w_out", 2),
               ("dil_w_in", 1), ("dil_w_out", 1), ("ffn_w_in", 3), ("ffn_w_out", 3))
EXCHANGE_GROUPS = (
    (("ffn_w_in", 3), ("ffn_w_out", 3), ("dil_w_in", 1), ("dil_w_out", 1),
     ("ffn_w_in", 2), ("ffn_w_out", 2), ("gdn_w_in", 1), ("gdn_w_out", 1)),
    (("ffn_w_in", 1), ("ffn_w_out", 1), ("dil_w_in", 0), ("dil_w_out", 0)),
    (("ffn_w_in", 0), ("ffn_w_out", 0)),
    (("gdn_w_in", 0), ("gdn_w_out", 0), ("gdn_conv_w", None)),
)
EXCHANGE_AFTER = {("mix", 2): 0, ("mix", 1): 1, ("ffn", 0): 2}


def _gather_operand(w, pieces):
    arrays = []
    for n, layer in pieces:
        if layer is None:
            arrays.append(lax.bitcast_convert_type(w[n], BF16).reshape(-1, PACK_WIDTH))
        else:
            arrays.append(_to_rows(w[n][layer:layer + 1].astype(BF16), n))
    return _pack_pieces(arrays)


def _gathered_weights(gathered, pieces, full):
    for (n, layer), at, rows in _piece_offsets(pieces, halves={"gdn_conv_w": 2}):
        block = gathered[:, at:at + rows]
        if layer is None:
            block = lax.bitcast_convert_type(block.reshape((N_DEV,) + _shard_shape(n) + (2,)), F32)
            full[n] = _join_shards(block, n)
        else:
            full.setdefault(n, {})[layer] = block.reshape(-1, _layer_columns(n))
    return full


def _exchange_operand(grads, pieces):
    arrays = []
    for n, layer in pieces:
        if layer is None:
            arrays.append(_split_shards(grads[n], n).astype(BF16).reshape(N_DEV, -1, PACK_WIDTH))
        else:
            arrays.append(grads[n][layer].astype(BF16).reshape(N_DEV, -1, PACK_WIDTH))
    return _pack_pieces(arrays, total_align=PACK_ROW_ALIGN)


def _update_group(received, pieces, w, m, v, *, name):
    packed = [_pack_pieces([_shard_piece_rows(src, p) for p in pieces], total_align=PACK_ROW_ALIGN) for src in (w, m, v)]
    outs = _adamw(received, *packed, name=name)
    return {p: tuple(_piece_from_rows(o[at:at + rows], p) for o in outs) for p, at, rows in _piece_offsets(pieces)}


def kernel(x, norm_mix, norm_ffn, gdn_w_in, gdn_conv_w, gdn_a_log, gdn_dt_bias, gdn_norm_w, gdn_w_out, dil_w_in, dil_q_norm, dil_k_norm, dil_w_out, ffn_w_in, ffn_w_out, loss_target, m_norm_mix, m_norm_ffn, m_gdn_w_in, m_gdn_conv_w, m_gdn_a_log, m_gdn_dt_bias, m_gdn_norm_w, m_gdn_w_out, m_dil_w_in, m_dil_q_norm, m_dil_k_norm, m_dil_w_out, m_ffn_w_in, m_ffn_w_out, v_norm_mix, v_norm_ffn, v_gdn_w_in, v_gdn_conv_w, v_gdn_a_log, v_gdn_dt_bias, v_gdn_norm_w, v_gdn_w_out, v_dil_w_in, v_dil_q_norm, v_dil_k_norm, v_dil_w_out, v_ffn_w_in, v_ffn_w_out):
    w = dict(norm_mix=norm_mix, norm_ffn=norm_ffn, gdn_w_in=gdn_w_in, gdn_conv_w=gdn_conv_w, gdn_a_log=gdn_a_log,
             gdn_dt_bias=gdn_dt_bias, gdn_norm_w=gdn_norm_w, gdn_w_out=gdn_w_out, dil_w_in=dil_w_in, dil_q_norm=dil_q_norm,
             dil_k_norm=dil_k_norm, dil_w_out=dil_w_out, ffn_w_in=ffn_w_in, ffn_w_out=ffn_w_out)
    m = dict(norm_mix=m_norm_mix, norm_ffn=m_norm_ffn, gdn_w_in=m_gdn_w_in, gdn_conv_w=m_gdn_conv_w, gdn_a_log=m_gdn_a_log,
             gdn_dt_bias=m_gdn_dt_bias, gdn_norm_w=m_gdn_norm_w, gdn_w_out=m_gdn_w_out, dil_w_in=m_dil_w_in,
             dil_q_norm=m_dil_q_norm, dil_k_norm=m_dil_k_norm, dil_w_out=m_dil_w_out, ffn_w_in=m_ffn_w_in, ffn_w_out=m_ffn_w_out)
    v = dict(norm_mix=v_norm_mix, norm_ffn=v_norm_ffn, gdn_w_in=v_gdn_w_in, gdn_conv_w=v_gdn_conv_w, gdn_a_log=v_gdn_a_log,
             gdn_dt_bias=v_gdn_dt_bias, gdn_norm_w=v_gdn_norm_w, gdn_w_out=v_gdn_w_out, dil_w_in=v_dil_w_in,
             dil_q_norm=v_dil_q_norm, dil_k_norm=v_dil_k_norm, dil_w_out=v_dil_w_out, ffn_w_in=v_ffn_w_in, ffn_w_out=v_ffn_w_out)
    def row(src, i):
        return src[i].reshape(1, D_MODEL)

    first = _all_gather(_gather_operand(w, GATHER_FIRST), name="weight_all_gather_first")
    next_started = _travel_start(_gather_operand(w, GATHER_NEXT), first, per_peer=False, name="weight_gather_start_next")
    last_started = _travel_start(_gather_operand(w, GATHER_LAST), next_started[4], per_peer=False,
                                 name="weight_gather_start_last")
    full = _gathered_weights(first, GATHER_FIRST, {n: w[n] for n in REPLICATED})
    prepared = dict(gdn=_prepare_gdn(full, layers=(0,)))
    h = x[0]
    saved = [None] * DEPTH
    h, s_mix = _mixer_fwd(0, h, row(norm_mix, 0) + last_started[4][0, 0], prepared)
    got = _travel_wait(next_started, h, per_peer=False, name="weight_gather_wait_next")
    full = _gathered_weights(got, GATHER_NEXT, full)
    prepared.update(dil=_prepare_dil(full, layers=(0,)), ffn=_prepare_ffn(full, layers=(0,)))
    for i in range(DEPTH):
        if i > 0:
            h, s_mix = _mixer_fwd(i, h, row(norm_mix, i), prepared)
        if i == 1:
            got = _travel_wait(last_started, h, per_peer=False, name="weight_gather_wait_last")
            full = _gathered_weights(got, GATHER_LAST, full)
            prepared["gdn"].update(_prepare_gdn(full, layers=(1,)))
            prepared["dil"].update(_prepare_dil(full, layers=(1,)))
            prepared["ffn"].update(_prepare_ffn(full, layers=(1, 2, 3)))
        h, s_ffn = _ffn_layer_fwd(h, row(norm_ffn, i), prepared["ffn"][i])
        saved[i] = (s_mix, s_ffn)
    dx, dxb, loss = _loss_head(h, loss_target[0], name="loss_head")

    g_mix, g_ffn = [None] * DEPTH, [None] * DEPTH
    started = {}

    def travel(group, dxb):
        operand = _exchange_operand(_collect_grads(g_mix, g_ffn), EXCHANGE_GROUPS[group])
        started[group] = _travel_start(operand, dx, per_peer=True, name=f"grad_exchange_start_{group}")
        return dxb + started[group][4][0, 0].astype(dxb.dtype)

    for i in reversed(range(DEPTH)):
        s_mix, s_ffn = saved[i]
        dx, dxb, g_ffn[i] = _ffn_layer_bwd(dx, dxb, row(norm_ffn, i), prepared["ffn"][i], s_ffn)
        if ("ffn", i) in EXCHANGE_AFTER:
            dxb = travel(EXCHANGE_AFTER[("ffn", i)], dxb)
        dx, dxb, g_mix[i] = _mixer_bwd(i, dx, dxb, row(norm_mix, i), prepared, s_mix)
        if ("mix", i) in EXCHANGE_AFTER:
            dxb = travel(EXCHANGE_AFTER[("mix", i)], dxb)
    grads = _collect_grads(g_mix, g_ffn)
    received = [_travel_wait(started[g], dx, per_peer=True, name=f"grad_exchange_wait_{g}") for g in sorted(started)]
    received.append(_exchange(_exchange_operand(grads, EXCHANGE_GROUPS[-1]), name="grad_exchange_last"))
    updated = {}
    for g, pieces in enumerate(EXCHANGE_GROUPS):
        updated.update(_update_group(received[g], pieces, w, m, v, name=f"adamw_sharded_{g}"))

    small_parts = _all_gather(_pack_small(grads), name="small_grad_all_gather")
    outs_small = [_unpack_small(o) for o in
                  _adamw(small_parts, _pack_small(w), _pack_small(m), _pack_small(v), name="adamw_replicated")]

    total_loss = lax.psum(loss[0, 0], ("x", "y", "c"))
    result = [total_loss, dx[None]]
    for k in range(4):
        for n in WEIGHT_ORDER:
            if n not in SHARDED:
                result.append(outs_small[k][n])
            elif (n, None) in updated:
                result.append(updated[(n, None)][k])
            else:
                result.append(jnp.concatenate([updated[(n, l)][k] for l in range(SHARDED[n][0][0])], axis=0))
    return tuple(result)
```

```python
import functools
import math

import jax
import jax.numpy as jnp
from jax import lax
from jax.experimental import pallas as pl
from jax.experimental.pallas import tpu as pltpu

F32 = jnp.float32
BF16 = jnp.bfloat16
MM_DTYPE = BF16

N_DEV = 8
D_MODEL = 1024
DEPTH = 4
RMS_EPS = 1e-6
L2_EPS = 1e-6

LANES = 128

GDN_HEADS = 8
GDN_DK = 128
GDN_DV = 128
GDN_CONV = 4
GDN_CHUNK = 128
GDN_QKV = 3 * GDN_HEADS * GDN_DK
GDN_MAIN = GDN_QKV + GDN_HEADS * GDN_DV
GDN_IN_WIDTH = GDN_MAIN + 2 * GDN_HEADS

DIL_GROUPS = ((128, 1), (512, 4), (2048, 16))
DIL_HEADS = 8
DIL_DH = 64
DIL_SPAN = 128
DIL_SLAB = 3 * DIL_HEADS * LANES
ALIBI_MAX_BIAS = 8.0

FFN_HIDDEN = 2816

ADAM_LR = 0.001
ADAM_B1 = 0.9
ADAM_B2 = 0.999
ADAM_EPS = 1e-08
ADAM_WD = 0.01
ADAM_STEP = 10

VMEM_LIMIT = 56 * 1024 * 1024
NEG = -1e30
HI = lax.Precision.HIGHEST


def _cparams(sem):
    return pltpu.CompilerParams(dimension_semantics=sem, vmem_limit_bytes=VMEM_LIMIT)


def _dot(a, b):
    return lax.dot_general(a, b, (((1,), (0,)), ((), ())), preferred_element_type=F32, precision=HI)


def _dot_nt(a, b):
    return lax.dot_general(a, b, (((1,), (1,)), ((), ())), preferred_element_type=F32, precision=HI)


def _dot_tn(a, b):
    return lax.dot_general(a, b, (((0,), (0,)), ((), ())), preferred_element_type=F32, precision=HI)


def _single_pass(a, b, a_dim, b_dim):
    lead = a.ndim - 2
    batch = ((0,), (0,)) if lead else ((), ())
    return lax.dot_general(a.astype(BF16), b.astype(BF16), (((lead + a_dim,), (lead + b_dim,)), batch),
                           preferred_element_type=F32)


def _bdot(a, b):
    return _single_pass(a, b, 1, 0)


def _bdot_nt(a, b):
    return _single_pass(a, b, 1, 1)


def _bdot_tn(a, b):
    return _single_pass(a, b, 0, 0)


def _pick(n, candidates):
    for c in candidates:
        if n % c == 0:
            return c
    raise ValueError(f"no tile for {n}")


HALF = LANES // 2


def _pack_head_pairs(x):
    x = x.astype(F32)
    tiles = [x[:, (2 * i) * LANES:(2 * i + 1) * LANES] + pltpu.roll(x[:, (2 * i + 1) * LANES:(2 * i + 2) * LANES], HALF, 1)
             for i in range(x.shape[1] // (2 * LANES))]
    return tiles[0] if len(tiles) == 1 else jnp.concatenate(tiles, axis=1)


def _spread_head_pairs(y):
    low = lax.broadcasted_iota(jnp.int32, (y.shape[0], LANES), 1) < HALF
    tiles = []
    for i in range(y.shape[1] // LANES):
        pair = y[:, i * LANES:(i + 1) * LANES]
        tiles += [jnp.where(low, pair, 0.0), jnp.where(low, pltpu.roll(pair, HALF, 1), 0.0)]
    return jnp.concatenate(tiles, axis=1)


def _matmul(a, b, *, name, trans_a=False, trans_b=False, b_rows=None, a_lead=None, add=None, out_dtype=F32,
            packed_a=False, spread_out=False):
    if trans_a:
        k_dim, m_dim = a.shape[-2:]
        m_dim = m_dim // 2 if packed_a else m_dim
    else:
        m_dim, k_dim = a.shape[-2:]
        k_dim = k_dim // 2 if packed_a else k_dim
    b_start, b_size = b_rows if b_rows is not None else (0, b.shape[0])
    if trans_b:
        n_dim, k2 = b_size, b.shape[1]
    else:
        k2, n_dim = b_size, b.shape[1]
    assert k_dim == k2, (a.shape, b.shape, b_rows)
    tn = _pick(n_dim, (1024, 512, 256, 128))
    tm = min(m_dim, 2048, max(512, (1024 * 1024) // tn))
    tm = _pick(m_dim, (tm, 1408, 1024, 512, 256, 128))
    tk = _pick(k_dim, (1024, 1408, 512, 256, 128))
    nk = k_dim // tk
    has_add = add is not None
    dn = (((0 if trans_a else 1,), (1 if trans_b else 0,)), ((), ()))
    b_tile = tn if trans_b else tk
    assert b_start % b_tile == 0, (b_rows, b_tile)
    b_off = b_start // b_tile

    def body(*refs):
        if has_add:
            a_ref, b_ref, add_ref, o_ref, acc_ref = refs
        else:
            a_ref, b_ref, o_ref, acc_ref = refs
        a_blk = _pack_head_pairs(a_ref[...]).astype(a_ref.dtype) if packed_a else a_ref[...]
        part = lax.dot_general(a_blk, b_ref[...], dn, preferred_element_type=F32)

        def finish(total):
            if has_add:
                total = total + add_ref[...]
            if spread_out:
                total = _spread_head_pairs(total)
            o_ref[...] = total.astype(out_dtype)

        if nk == 1:
            finish(part)
        else:
            k = pl.program_id(2)

            @pl.when(k == 0)
            def _():
                acc_ref[...] = part

            @pl.when(k > 0)
            def _():
                acc_ref[...] += part

            @pl.when(k == nk - 1)
            def _():
                finish(acc_ref[...])

    wide = 2 if packed_a else 1
    a_tile = (tk, wide * tm) if trans_a else (tm, wide * tk)
    a_at = (lambda i, j, k: (k, i)) if trans_a else (lambda i, j, k: (i, k))
    if a_lead is None:
        a_spec = pl.BlockSpec(a_tile, a_at)
    else:
        a_spec = pl.BlockSpec((None,) + a_tile, lambda i, j, k: (a_lead,) + a_at(i, j, k))
    if trans_b:
        b_spec = pl.BlockSpec((tn, tk), lambda i, j, k: (j + b_off, k))
    else:
        b_spec = pl.BlockSpec((tk, tn), lambda i, j, k: (k + b_off, j))
    in_specs = [a_spec, b_spec]
    args = [a, b]
    if has_add:
        in_specs.append(pl.BlockSpec((tm, tn), lambda i, j, k: (i, j)))
        args.append(add)
    return pl.pallas_call(
        body,
        grid=(m_dim // tm, n_dim // tn, nk),
        in_specs=in_specs,
        out_specs=pl.BlockSpec((tm, (2 if spread_out else 1) * tn), lambda i, j, k: (i, j)),
        out_shape=jax.ShapeDtypeStruct((m_dim, (2 if spread_out else 1) * n_dim), out_dtype),
        scratch_shapes=[pltpu.VMEM((tm, tn) if nk > 1 else (8, LANES), F32)],
        compiler_params=_cparams(("parallel", "parallel", "arbitrary")),
        name=name,
    )(*args)


def _rmsnorm_fwd(x, w_row, *, name):
    t, d = x.shape
    tb = min(t, 1024)

    def body(x_ref, w_ref, o_ref):
        xf = x_ref[...]
        r = lax.rsqrt(jnp.mean(xf * xf, axis=-1, keepdims=True) + RMS_EPS)
        o_ref[...] = (xf * r * w_ref[...]).astype(o_ref.dtype)

    return pl.pallas_call(
        body,
        grid=(t // tb,),
        in_specs=[pl.BlockSpec((tb, d), lambda i: (i, 0)), pl.BlockSpec((1, d), lambda i: (0, 0))],
        out_specs=pl.BlockSpec((tb, d), lambda i: (i, 0)),
        out_shape=jax.ShapeDtypeStruct((t, d), MM_DTYPE),
        compiler_params=_cparams(("parallel",)),
        name=name,
    )(x, w_row)


def _rmsnorm_bwd(x, w_row, dy, dskip, *, name):
    t, d = x.shape
    tb = min(t, 512)

    def body(x_ref, w_ref, dy_ref, ds_ref, dx_ref, dxb_ref, dw_ref):
        xf = x_ref[...]
        g = dy_ref[...]
        r = lax.rsqrt(jnp.mean(xf * xf, axis=-1, keepdims=True) + RMS_EPS)
        gw = g * w_ref[...]
        proj = jnp.mean(gw * xf, axis=-1, keepdims=True)
        dx = r * gw - xf * (r * r * r * proj) + ds_ref[...]
        dx_ref[...] = dx
        dxb_ref[...] = dx.astype(dxb_ref.dtype)
        part = jnp.sum(g * xf * r, axis=0, keepdims=True)

        @pl.when(pl.program_id(0) == 0)
        def _():
            dw_ref[...] = part

        @pl.when(pl.program_id(0) > 0)
        def _():
            dw_ref[...] += part

    row = pl.BlockSpec((tb, d), lambda i: (i, 0))
    one = pl.BlockSpec((1, d), lambda i: (0, 0))
    return pl.pallas_call(
        body,
        grid=(t // tb,),
        in_specs=[row, one, row, row],
        out_specs=[row, row, one],
        out_shape=[jax.ShapeDtypeStruct((t, d), F32), jax.ShapeDtypeStruct((t, d), MM_DTYPE),
                   jax.ShapeDtypeStruct((1, d), F32)],
        compiler_params=_cparams(("arbitrary",)),
        name=name,
    )(x, w_row, dy, dskip)


def _silu(z):
    return z / (1.0 + jnp.exp(-z))


FFN_TM, FFN_TN = 512, 1408


def _ffn_in(hn, in_t, *, name):
    t, d = hn.shape
    h = FFN_HIDDEN
    tm, tn = min(t, FFN_TM), FFN_TN
    nj = h // tn
    dn = (((1,), (1,)), ((), ()))

    def body(a_ref, bg_ref, bu_ref, g_ref, u_ref, act_ref):
        a = a_ref[...]
        g = lax.dot_general(a, bg_ref[...], dn, preferred_element_type=F32)
        u = lax.dot_general(a, bu_ref[...], dn, preferred_element_type=F32)
        g_ref[...] = g.astype(g_ref.dtype)
        u_ref[...] = u.astype(u_ref.dtype)
        act_ref[...] = (_silu(g) * u).astype(act_ref.dtype)

    out = pl.BlockSpec((tm, tn), lambda j, i: (i, j))
    return pl.pallas_call(
        body,
        grid=(nj, t // tm),
        in_specs=[pl.BlockSpec((tm, d), lambda j, i: (i, 0)), pl.BlockSpec((tn, d), lambda j, i: (j, 0)),
                  pl.BlockSpec((tn, d), lambda j, i: (j + nj, 0))],
        out_specs=[out, out, out],
        out_shape=[jax.ShapeDtypeStruct((t, h), MM_DTYPE)] * 3,
        compiler_params=_cparams(("parallel", "parallel")),
        name=name,
    )(hn, in_t, in_t)


def _ffn_dact(dy, out_w, g, u, *, name):
    t, d = dy.shape
    h = FFN_HIDDEN
    tm, tn = min(t, FFN_TM), FFN_TN

    def body(a_ref, b_ref, g_ref, u_ref, dg_ref, du_ref):
        da = lax.dot_general(a_ref[...], b_ref[...], (((1,), (1,)), ((), ())), preferred_element_type=F32)
        gate = g_ref[...].astype(F32)
        sig = 1.0 / (1.0 + jnp.exp(-gate))
        sg = gate * sig
        dg_ref[...] = (da * u_ref[...].astype(F32) * (sig + sg * (1.0 - sig))).astype(dg_ref.dtype)
        du_ref[...] = (da * sg).astype(du_ref.dtype)

    blk = pl.BlockSpec((tm, tn), lambda j, i: (i, j))
    return pl.pallas_call(
        body,
        grid=(h // tn, t // tm),
        in_specs=[pl.BlockSpec((tm, d), lambda j, i: (i, 0)), pl.BlockSpec((tn, d), lambda j, i: (j, 0)), blk, blk],
        out_specs=[blk, blk],
        out_shape=[jax.ShapeDtypeStruct((t, h), MM_DTYPE)] * 2,
        compiler_params=_cparams(("parallel", "parallel")),
        name=name,
    )(dy, out_w, g, u)


def _loss_head(y, target, *, name):
    t, d = y.shape
    tb = min(t, 1024)

    def body(y_ref, t_ref, dy_ref, dyb_ref, l_ref):
        err = y_ref[...] - t_ref[...]
        dy_ref[...] = err * (1.0 / d)
        dyb_ref[...] = (err * (1.0 / d)).astype(dyb_ref.dtype)
        part = jnp.sum(jnp.sum(err * err, axis=0, keepdims=True), axis=1, keepdims=True) * (0.5 / d)
        part = jnp.broadcast_to(part, l_ref.shape)

        @pl.when(pl.program_id(0) == 0)
        def _():
            l_ref[...] = part

        @pl.when(pl.program_id(0) > 0)
        def _():
            l_ref[...] += part

    row = pl.BlockSpec((tb, d), lambda i: (i, 0))
    return pl.pallas_call(
        body,
        grid=(t // tb,),
        in_specs=[row, row],
        out_specs=[row, row, pl.BlockSpec((8, LANES), lambda i: (0, 0))],
        out_shape=[jax.ShapeDtypeStruct((t, d), F32), jax.ShapeDtypeStruct((t, d), MM_DTYPE),
                   jax.ShapeDtypeStruct((8, LANES), F32)],
        compiler_params=_cparams(("arbitrary",)),
        name=name,
    )(y, target)


CONV_HALO = 8


def _conv_tile_scale(c):
    is_qk = c < 2 * GDN_HEADS
    scale = jnp.where(c < GDN_HEADS, GDN_DK ** -0.5, 1.0).astype(F32)
    return is_qk, scale


def _gdn_conv_fwd(pm, conv_w, *, name):
    t = pm.shape[0]
    tb = min(t, 1024)
    nt = t // tb
    hb = tb // CONV_HALO

    def body(x_ref, xp_ref, w_ref, o_ref, xe_ref):
        c = pl.program_id(0)
        ti = pl.program_id(1)
        xe_ref[0:CONV_HALO, :] = jnp.where(ti > 0, xp_ref[...], 0.0)
        xe_ref[CONV_HALO:CONV_HALO + tb, :] = x_ref[...]
        w = w_ref[...]
        y = jnp.zeros((tb, LANES), F32)
        for j in range(GDN_CONV):
            off = CONV_HALO - (GDN_CONV - 1) + j
            y = y + w[j:j + 1, :] * xe_ref[pl.ds(off, tb), :]
        s = _silu(y)
        is_qk, scale = _conv_tile_scale(c)
        r = lax.rsqrt(jnp.sum(s * s, axis=-1, keepdims=True) + L2_EPS) * scale
        o_ref[...] = s * jnp.where(is_qk, r, 1.0)

    return pl.pallas_call(
        body,
        grid=(GDN_QKV // LANES, nt),
        in_specs=[
            pl.BlockSpec((tb, LANES), lambda c, i: (i, c)),
            pl.BlockSpec((CONV_HALO, LANES), lambda c, i: (jnp.maximum(i * hb - 1, 0), c)),
            pl.BlockSpec((GDN_CONV, LANES), lambda c, i: (0, c)),
        ],
        out_specs=pl.BlockSpec((tb, LANES), lambda c, i: (i, c)),
        out_shape=jax.ShapeDtypeStruct((t, GDN_QKV), F32),
        scratch_shapes=[pltpu.VMEM((tb + CONV_HALO, LANES), F32)],
        compiler_params=_cparams(("parallel", "parallel")),
        name=name,
    )(pm, pm, conv_w)


def _gdn_conv_bwd(pm, conv_w, dout, *, name):
    t = pm.shape[0]
    tb = min(t, 1024)
    nt = t // tb
    hb = tb // CONV_HALO
    last_hb = t // CONV_HALO - 1
    ext = tb + CONV_HALO

    def body(x_ref, xp_ref, xn_ref, d_ref, dn_ref, w_ref, dx_ref, dw_ref, xe_ref, dy_ref):
        c = pl.program_id(0)
        ti = pl.program_id(1)
        has_next = ti < nt - 1
        xe_ref[0:CONV_HALO, :] = jnp.where(ti > 0, xp_ref[...], 0.0)
        xe_ref[CONV_HALO:CONV_HALO + tb, :] = x_ref[...]
        xe_ref[CONV_HALO + tb:2 * CONV_HALO + tb, :] = jnp.where(has_next, xn_ref[...], 0.0)
        de = jnp.concatenate([d_ref[...], jnp.where(has_next, dn_ref[...], 0.0)], axis=0)
        w = w_ref[...]
        y = jnp.zeros((ext, LANES), F32)
        for j in range(GDN_CONV):
            off = CONV_HALO - (GDN_CONV - 1) + j
            y = y + w[j:j + 1, :] * xe_ref[pl.ds(off, ext), :]
        sig = 1.0 / (1.0 + jnp.exp(-y))
        s = y * sig
        is_qk, scale = _conv_tile_scale(c)
        r = lax.rsqrt(jnp.sum(s * s, axis=-1, keepdims=True) + L2_EPS)
        n = s * r
        dnrm = de * scale
        ds_qk = r * (dnrm - n * jnp.sum(dnrm * n, axis=-1, keepdims=True))
        ds = jnp.where(is_qk, ds_qk, de)
        dy_ref[...] = ds * (sig + s * (1.0 - sig))
        dy = dy_ref[0:tb, :]
        dx = jnp.zeros((tb, LANES), F32)
        dw_rows = []
        for j in range(GDN_CONV):
            sh = GDN_CONV - 1 - j
            dx = dx + w[j:j + 1, :] * dy_ref[pl.ds(sh, tb), :]
            off = CONV_HALO - (GDN_CONV - 1) + j
            dw_rows.append(jnp.sum(dy * xe_ref[pl.ds(off, tb), :], axis=0, keepdims=True))
        dx_ref[...] = dx.astype(dx_ref.dtype)
        part = jnp.concatenate(dw_rows, axis=0)

        @pl.when(ti == 0)
        def _():
            dw_ref[...] = part

        @pl.when(ti > 0)
        def _():
            dw_ref[...] += part

    main = pl.BlockSpec((tb, LANES), lambda c, i: (i, c))
    prev = pl.BlockSpec((CONV_HALO, LANES), lambda c, i: (jnp.maximum(i * hb - 1, 0), c))
    nxt = pl.BlockSpec((CONV_HALO, LANES), lambda c, i: (jnp.minimum((i + 1) * hb, last_hb), c))
    return pl.pallas_call(
        body,
        grid=(GDN_QKV // LANES, nt),
        in_specs=[main, prev, nxt, main, nxt, pl.BlockSpec((GDN_CONV, LANES), lambda c, i: (0, c))],
        out_specs=[main, pl.BlockSpec((GDN_CONV, LANES), lambda c, i: (0, c))],
        out_shape=[jax.ShapeDtypeStruct((t, GDN_QKV), MM_DTYPE), jax.ShapeDtypeStruct((GDN_CONV, GDN_QKV), F32)],
        scratch_shapes=[pltpu.VMEM((tb + 2 * CONV_HALO, LANES), F32), pltpu.VMEM((ext, LANES), F32)],
        compiler_params=_cparams(("parallel", "arbitrary")),
        name=name,
    )(pm, pm, pm, dout, dout, conv_w)


def _head_selector(first_col):
    row = lax.broadcasted_iota(jnp.int32, (LANES, GDN_HEADS * LANES), 0)
    col = lax.broadcasted_iota(jnp.int32, (LANES, GDN_HEADS * LANES), 1)
    return (col // LANES + first_col == row).astype(F32)


def _softplus(x):
    return jnp.maximum(x, 0.0) + jnp.log(1.0 + jnp.exp(-jnp.abs(x)))


def _gdn_gates_fwd(ab, alog_row, dt_row, *, name):
    t = ab.shape[0]
    tb = min(t, 1024)
    wide = GDN_HEADS * LANES

    def body(ab_ref, al_ref, dt_ref, g_ref, b_ref):
        x = ab_ref[...]
        g_cols = -jnp.exp(al_ref[...]) * _softplus(x + dt_ref[...])
        b_cols = 1.0 / (1.0 + jnp.exp(-x))
        g_ref[...] = _dot(g_cols, _head_selector(0))
        b_ref[...] = _dot(b_cols, _head_selector(GDN_HEADS))

    row = pl.BlockSpec((tb, LANES), lambda i: (i, 0))
    one = pl.BlockSpec((1, LANES), lambda i: (0, 0))
    out = pl.BlockSpec((tb, wide), lambda i: (i, 0))
    return pl.pallas_call(
        body,
        grid=(t // tb,),
        in_specs=[row, one, one],
        out_specs=[out, out],
        out_shape=[jax.ShapeDtypeStruct((t, wide), F32)] * 2,
        compiler_params=_cparams(("parallel",)),
        name=name,
    )(ab, alog_row, dt_row)


def _gdn_gates_bwd(ab, alog_row, dt_row, dgb, dbb, *, name):
    t = ab.shape[0]
    tb = min(t, 1024)
    wide = GDN_HEADS * LANES

    def body(ab_ref, al_ref, dt_ref, dg_ref, db_ref, dab_ref, dal_ref, ddt_ref):
        x = ab_ref[...]
        lane = lax.broadcasted_iota(jnp.int32, (tb, LANES), 1)
        dg_cols = _dot_nt(dg_ref[...], _head_selector(0))
        db_cols = _dot_nt(db_ref[...], _head_selector(GDN_HEADS))
        ea = jnp.exp(al_ref[...])
        z = x + dt_ref[...]
        sp = _softplus(z)
        sg = 1.0 / (1.0 + jnp.exp(-z))
        beta = 1.0 / (1.0 + jnp.exp(-x))
        da = jnp.where(lane < GDN_HEADS, dg_cols * (-ea) * sg, 0.0)
        db = jnp.where((lane >= GDN_HEADS) & (lane < 2 * GDN_HEADS), db_cols * beta * (1.0 - beta), 0.0)
        dab_ref[...] = (da + db).astype(dab_ref.dtype)
        p_al = jnp.sum(jnp.where(lane < GDN_HEADS, dg_cols * (-ea) * sp, 0.0), axis=0, keepdims=True)
        p_dt = jnp.sum(da, axis=0, keepdims=True)

        @pl.when(pl.program_id(0) == 0)
        def _():
            dal_ref[...] = p_al
            ddt_ref[...] = p_dt

        @pl.when(pl.program_id(0) > 0)
        def _():
            dal_ref[...] += p_al
            ddt_ref[...] += p_dt

    row = pl.BlockSpec((tb, LANES), lambda i: (i, 0))
    one = pl.BlockSpec((1, LANES), lambda i: (0, 0))
    big = pl.BlockSpec((tb, wide), lambda i: (i, 0))
    return pl.pallas_call(
        body,
        grid=(t // tb,),
        in_specs=[row, one, one, big, big],
        out_specs=[row, one, one],
        out_shape=[jax.ShapeDtypeStruct((t, LANES), MM_DTYPE), jax.ShapeDtypeStruct((1, LANES), F32),
                   jax.ShapeDtypeStruct((1, LANES), F32)],
        compiler_params=_cparams(("arbitrary",)),
        name=name,
    )(ab, alog_row, dt_row, dgb, dbb)


@jax.custom_vjp
def _unit_lower_inverse_rest(n):
    c = n.shape[-1]
    ri = lax.broadcasted_iota(jnp.int32, (c, c), 0)
    ci = lax.broadcasted_iota(jnp.int32, (c, c), 1)
    rest = None
    size = 1
    while size < c:
        joins = ((ri // (2 * size)) == (ci // (2 * size))) & ((ri // size) != (ci // size))
        low = jnp.where(joins, n, 0.0)
        if rest is None:
            rest = -low
        else:
            left = low + _bdot(rest, low)
            rest = rest - (left + _bdot(left, rest))
        size *= 2
    return rest


def _unit_lower_inverse_rest_fwd(n):
    rest = _unit_lower_inverse_rest(n)
    return rest, rest


def _unit_lower_inverse_rest_bwd(rest, ct):
    left = ct + _bdot_tn(rest, ct)
    return (-(left + _bdot_nt(left, rest)),)


_unit_lower_inverse_rest.defvjp(_unit_lower_inverse_rest_fwd, _unit_lower_inverse_rest_bwd)


@jax.custom_vjp
def _known_inverse_rest(n, rest):
    return rest


def _known_inverse_rest_fwd(n, rest):
    return rest, rest


def _known_inverse_rest_bwd(rest, ct):
    return _unit_lower_inverse_rest_bwd(rest, ct) + (jnp.zeros_like(rest),)


_known_inverse_rest.defvjp(_known_inverse_rest_fwd, _known_inverse_rest_bwd)


def _bf16_pieces(x):
    hi = x.astype(BF16)
    r1 = x - hi.astype(F32)
    mid = r1.astype(BF16)
    lo = (r1 - mid.astype(F32)).astype(BF16)
    return hi, mid, lo


def _lower_ones(shape):
    c = shape[-1]
    ri = lax.broadcasted_iota(jnp.int32, (c, c), 0)
    ci = lax.broadcasted_iota(jnp.int32, (c, c), 1)
    return jnp.broadcast_to((ri >= ci).astype(BF16), shape)


@jax.custom_vjp
def _running_sum(x):
    tri = _lower_ones(x.shape)
    return sum(_bdot(tri, p) for p in _bf16_pieces(x))


def _running_sum_fwd(x):
    return _running_sum(x), None


def _running_sum_bwd(_, ct):
    tri = _lower_ones(ct.shape)
    return (sum(_bdot_tn(tri, p) for p in _bf16_pieces(ct)),)


_running_sum.defvjp(_running_sum_fwd, _running_sum_bwd)


def _gdn_prep_math(q, k, v, gb, bb, known_rest=None, with_rest=False):
    c = GDN_CHUNK
    ri = lax.broadcasted_iota(jnp.int32, (c, c), 0)
    ci = lax.broadcasted_iota(jnp.int32, (c, c), 1)
    causal = ri >= ci
    gc = _running_sum(gb)
    decay = jnp.exp(jnp.where(causal, gc - jnp.swapaxes(gc, -1, -2), NEG))
    n = jnp.where(ri > ci, _bdot_nt(k, k) * bb * decay, 0.0)
    rest = _unit_lower_inverse_rest(n) if known_rest is None else _known_inverse_rest(n, known_rest)
    eg = jnp.exp(gc)
    rhs_v = v * bb
    rhs_k = k * bb * eg
    u = rhs_v + _bdot(rest, rhs_v)
    w = rhs_k + _bdot(rest, rhs_k)
    qk = _bdot_nt(q, k) * decay
    qd = q * eg
    last = jnp.sum(jnp.where(ri == c - 1, gc, 0.0), axis=-2, keepdims=True)
    gl = jnp.broadcast_to(last, gc.shape)
    kt = k * jnp.exp(gl - gc)
    cd = jnp.exp(gl)
    return (u, w, qk, qd, kt, cd, rest) if with_rest else (u, w, qk, qd, kt, cd)


def _head_tiles(ref, h):
    return ref[:, h * LANES:(h + 1) * LANES]


def _stack_heads(ref, first=0, heads=GDN_HEADS):
    return jnp.stack([_head_tiles(ref, first + h) for h in range(heads)])


def _store_heads(ref, val, first=0):
    for h in range(val.shape[0]):
        ref[:, (first + h) * LANES:(first + h + 1) * LANES] = val[h].astype(ref.dtype)


def _gdn_prep_fwd(qkv, gb, bb, *, name):
    t = qkv.shape[0]
    c = GDN_CHUNK
    wide = GDN_HEADS * LANES

    def body(q_ref, k_ref, v_ref, g_ref, b_ref, *outs):
        res = _gdn_prep_math(*(_stack_heads(r) for r in (q_ref, k_ref, v_ref, g_ref, b_ref)), with_rest=True)
        for o_ref, val in zip(outs, res):
            _store_heads(o_ref, val)

    blk = lambda off: pl.BlockSpec((c, wide), lambda i: (i, off))
    outs = pl.pallas_call(
        body,
        grid=(t // c,),
        in_specs=[blk(0), blk(1), blk(2), blk(0), blk(0)],
        out_specs=[blk(0)] * 7,
        out_shape=[jax.ShapeDtypeStruct((t, wide), F32)] * 7,
        compiler_params=_cparams(("parallel",)),
        name=name,
    )(qkv, qkv, qkv, gb, bb)
    return tuple(outs[:6]), outs[6]


def _gdn_prep_bwd(qkv, gb, bb, rest, cts, *, name):
    t = qkv.shape[0]
    c = GDN_CHUNK
    wide = GDN_HEADS * LANES

    def body(q_ref, k_ref, v_ref, g_ref, b_ref, r_ref, c0, c1, c2, c3, c4, c5, dqkv_ref, dg_ref, db_ref):
        prim = tuple(_stack_heads(r) for r in (q_ref, k_ref, v_ref, g_ref, b_ref))
        _, pull = jax.vjp(functools.partial(_gdn_prep_math, known_rest=_stack_heads(r_ref)), *prim)
        dq, dk, dv, dg, db = pull(tuple(_stack_heads(r) for r in (c0, c1, c2, c3, c4, c5)))
        _store_heads(dqkv_ref, dq)
        _store_heads(dqkv_ref, dk, first=GDN_HEADS)
        _store_heads(dqkv_ref, dv, first=2 * GDN_HEADS)
        _store_heads(dg_ref, dg)
        _store_heads(db_ref, db)

    blk = lambda off: pl.BlockSpec((c, wide), lambda i: (i, off))
    return pl.pallas_call(
        body,
        grid=(t // c,),
        in_specs=[blk(0), blk(1), blk(2), blk(0), blk(0)] + [blk(0)] * 7,
        out_specs=[pl.BlockSpec((c, 3 * wide), lambda i: (i, 0)), blk(0), blk(0)],
        out_shape=[jax.ShapeDtypeStruct((t, 3 * wide), F32), jax.ShapeDtypeStruct((t, wide), F32),
                   jax.ShapeDtypeStruct((t, wide), F32)],
        compiler_params=_cparams(("parallel",)),
        name=name,
    )(qkv, qkv, qkv, gb, bb, rest, *cts)


def _gdn_scan_math(s, u, w, qk, qd, kt, cd):
    v_new = u - _bdot(w, s)
    o = _bdot(qd, s) + _bdot(qk, v_new)
    s_new = s * cd + _bdot_tn(kt, v_new)
    return o, s_new


def _gdn_scan_fwd(prep, *, name):
    t = prep[0].shape[0]
    c = GDN_CHUNK
    wide = GDN_HEADS * LANES

    def body(u_ref, w_ref, qk_ref, qd_ref, kt_ref, cd_ref, o_ref, st_ref, s_ref):
        @pl.when(pl.program_id(0) == 0)
        def _():
            s_ref[...] = jnp.zeros_like(s_ref)

        s = _stack_heads(s_ref)
        _store_heads(st_ref, s)
        o, s_new = _gdn_scan_math(s, *(_stack_heads(r) for r in (u_ref, w_ref, qk_ref, qd_ref, kt_ref, cd_ref)))
        _store_heads(o_ref, o)
        _store_heads(s_ref, s_new)

    blk = pl.BlockSpec((c, wide), lambda i: (i, 0))
    return pl.pallas_call(
        body,
        grid=(t // c,),
        in_specs=[blk] * 6,
        out_specs=[blk, blk],
        out_shape=[jax.ShapeDtypeStruct((t, wide), F32)] * 2,
        scratch_shapes=[pltpu.VMEM((GDN_DK, wide), F32)],
        compiler_params=_cparams(("arbitrary",)),
        name=name,
    )(*prep)


def _gdn_scan_bwd(prep, states, do, *, name):
    t = do.shape[0]
    c = GDN_CHUNK
    wide = GDN_HEADS * LANES
    nc = t // c

    def body(u_ref, w_ref, qk_ref, qd_ref, kt_ref, cd_ref, st_ref, do_ref, *rest):
        outs, ds_ref = rest[:6], rest[6]

        @pl.when(pl.program_id(0) == 0)
        def _():
            ds_ref[...] = jnp.zeros_like(ds_ref)

        prim = tuple(_stack_heads(r) for r in (st_ref, u_ref, w_ref, qk_ref, qd_ref, kt_ref, cd_ref))
        _, pull = jax.vjp(_gdn_scan_math, *prim)
        grads = pull((_stack_heads(do_ref), _stack_heads(ds_ref)))
        _store_heads(ds_ref, grads[0])
        for o_ref, val in zip(outs, grads[1:]):
            _store_heads(o_ref, val)

    blk = pl.BlockSpec((c, wide), lambda i: (nc - 1 - i, 0))
    return pl.pallas_call(
        body,
        grid=(nc,),
        in_specs=[blk] * 8,
        out_specs=[blk] * 6,
        out_shape=[jax.ShapeDtypeStruct((t, wide), F32)] * 6,
        scratch_shapes=[pltpu.VMEM((GDN_DK, wide), F32)],
        compiler_params=_cparams(("arbitrary",)),
        name=name,
    )(*prep, states, do)


def _gdn_outgate_math(o, z, nw):
    r = lax.rsqrt(jnp.mean(o * o, axis=-1, keepdims=True) + RMS_EPS)
    return o * r * nw * _silu(z)


def _gdn_outgate_fwd(o, pm, nw_row, *, name):
    t = o.shape[0]
    tb = min(t, 1024)
    z_off = GDN_QKV // LANES

    def body(o_ref, z_ref, nw_ref, y_ref):
        y_ref[...] = _gdn_outgate_math(o_ref[...], z_ref[...], nw_ref[...]).astype(y_ref.dtype)

    return pl.pallas_call(
        body,
        grid=(t // tb, GDN_HEADS),
        in_specs=[pl.BlockSpec((tb, LANES), lambda i, h: (i, h)), pl.BlockSpec((tb, LANES), lambda i, h: (i, h + z_off)),
                  pl.BlockSpec((1, LANES), lambda i, h: (0, 0))],
        out_specs=pl.BlockSpec((tb, LANES), lambda i, h: (i, h)),
        out_shape=jax.ShapeDtypeStruct((t, GDN_HEADS * LANES), MM_DTYPE),
        compiler_params=_cparams(("parallel", "parallel")),
        name=name,
    )(o, pm, nw_row)


def _gdn_outgate_bwd(o, pm, nw_row, dy, *, name):
    t = o.shape[0]
    tb = min(t, 1024)
    z_off = GDN_QKV // LANES

    def body(o_ref, z_ref, nw_ref, dy_ref, do_ref, dz_ref, dnw_ref):
        _, pull = jax.vjp(_gdn_outgate_math, o_ref[...], z_ref[...], nw_ref[...])
        d_o, d_z, d_nw = pull(dy_ref[...])
        do_ref[...] = d_o
        dz_ref[...] = d_z.astype(dz_ref.dtype)
        first = (pl.program_id(0) == 0) & (pl.program_id(1) == 0)

        @pl.when(first)
        def _():
            dnw_ref[...] = d_nw

        @pl.when(jnp.logical_not(first))
        def _():
            dnw_ref[...] += d_nw

    blk = pl.BlockSpec((tb, LANES), lambda i, h: (i, h))
    one = pl.BlockSpec((1, LANES), lambda i, h: (0, 0))
    return pl.pallas_call(
        body,
        grid=(t // tb, GDN_HEADS),
        in_specs=[blk, pl.BlockSpec((tb, LANES), lambda i, h: (i, h + z_off)), one, blk],
        out_specs=[blk, blk, one],
        out_shape=[jax.ShapeDtypeStruct((t, GDN_HEADS * LANES), F32),
                   jax.ShapeDtypeStruct((t, GDN_HEADS * LANES), MM_DTYPE), jax.ShapeDtypeStruct((1, LANES), F32)],
        compiler_params=_cparams(("arbitrary", "arbitrary")),
        name=name,
    )(o, pm, nw_row, dy)


def _rms64(x, w_row):
    return x * lax.rsqrt(jnp.sum(x * x, axis=-1, keepdims=True) * (1.0 / DIL_DH) + RMS_EPS) * w_row


def _alibi_slopes(group):
    head = lax.broadcasted_iota(jnp.int32, (DIL_HEADS, 8, LANES), 0).astype(F32)
    rate = -math.log(2.0) * ALIBI_MAX_BIAS / (len(DIL_GROUPS) * DIL_HEADS)
    slope = jnp.exp(rate * (head + float(group * DIL_HEADS + 1)))
    return jnp.broadcast_to(slope[:, 0:1, :], (DIL_HEADS, DIL_SPAN, LANES))


def _band_logits(qn, kp, kc, slope_d, has_prev):
    qi = lax.broadcasted_iota(jnp.int32, (DIL_SPAN, DIL_SPAN), 0)
    kj = lax.broadcasted_iota(jnp.int32, (DIL_SPAN, DIL_SPAN), 1)
    steps_c = (qi - kj).astype(F32)
    scale = DIL_DH ** -0.5
    sp = _bdot_nt(qn, kp) * scale - slope_d * (steps_c + float(DIL_SPAN))
    sc = _bdot_nt(qn, kc) * scale - slope_d * steps_c
    sp = jnp.where((kj >= qi) & has_prev, sp, NEG)
    sc = jnp.where(kj <= qi, sc, NEG)
    return sp, sc


def _dil_attn_fwd(slab, wq_row, wk_row, *, group, name):
    dilation = DIL_GROUPS[group][1]
    t = slab.shape[0]
    rows = t // dilation
    nlb = rows // DIL_SPAN
    wide = DIL_HEADS * LANES
    view = slab.reshape(rows, dilation * DIL_SLAB)

    def body(q_ref, kc_ref, vc_ref, kp_ref, vp_ref, wq_ref, wk_ref, o_ref):
        has_prev = pl.program_id(1) > 0
        lane = lax.broadcasted_iota(jnp.int32, (DIL_SPAN, LANES), 1)
        qn = _rms64(_stack_heads(q_ref), wq_ref[...])
        kc = _rms64(_stack_heads(kc_ref), wk_ref[...])
        kp = _rms64(_stack_heads(kp_ref), wk_ref[...])
        sp, sc = _band_logits(qn, kp, kc, _alibi_slopes(group) * float(dilation), has_prev)
        m = jnp.maximum(jnp.max(sp, axis=-1, keepdims=True), jnp.max(sc, axis=-1, keepdims=True))
        pp = jnp.exp(sp - m)
        pc = jnp.exp(sc - m)
        l = jnp.sum(pp, axis=-1, keepdims=True) + jnp.sum(pc, axis=-1, keepdims=True)
        o = (_bdot(pp, _stack_heads(vp_ref)) + _bdot(pc, _stack_heads(vc_ref))) / l
        _store_heads(o_ref, jnp.where(lane < DIL_DH, o, m + jnp.log(l)))

    cur = lambda part: pl.BlockSpec((DIL_SPAN, wide), lambda r, i: (i, 3 * r + part))
    prv = lambda part: pl.BlockSpec((DIL_SPAN, wide), lambda r, i: (jnp.maximum(i - 1, 0), 3 * r + part))
    one = pl.BlockSpec((1, LANES), lambda r, i: (0, 0))
    out = pl.pallas_call(
        body,
        grid=(dilation, nlb),
        in_specs=[cur(0), cur(1), cur(2), prv(1), prv(2), one, one],
        out_specs=pl.BlockSpec((DIL_SPAN, wide), lambda r, i: (i, r)),
        out_shape=jax.ShapeDtypeStruct((rows, dilation * wide), F32),
        compiler_params=_cparams(("parallel", "parallel")),
        name=name,
    )(view, view, view, view, view, wq_row, wk_row)
    return out.reshape(t, wide)


def _head_slope(group, head):
    idx = jnp.zeros((8, LANES), F32) + head.astype(F32)
    rate = -math.log(2.0) * ALIBI_MAX_BIAS / (len(DIL_GROUPS) * DIL_HEADS)
    slope = jnp.exp(rate * (idx + float(group * DIL_HEADS + 1)))
    return jnp.broadcast_to(slope[0:1, :], (DIL_SPAN, LANES))


def _take_residues(ref, d):
    return jnp.stack([ref[pl.ds(r, DIL_SPAN, stride=d), :] for r in range(d)])


def _put_residues(ref, val, d):
    for r in range(d):
        ref[pl.ds(r, DIL_SPAN, stride=d), :] = val[r]


def _dil_attn_fwd_strided(slab, wq_row, wk_row, *, group, name):
    d = DIL_GROUPS[group][1]
    t = slab.shape[0]
    span = DIL_SPAN * d
    nsb = t // span

    def body(q_ref, kc_ref, vc_ref, kp_ref, vp_ref, wq_ref, wk_ref, o_ref):
        has_prev = pl.program_id(0) > 0
        lane = lax.broadcasted_iota(jnp.int32, (DIL_SPAN, LANES), 1)
        qn = _rms64(_take_residues(q_ref, d), wq_ref[...])
        kc = _rms64(_take_residues(kc_ref, d), wk_ref[...])
        kp = _rms64(_take_residues(kp_ref, d), wk_ref[...])
        sp, sc = _band_logits(qn, kp, kc, _head_slope(group, pl.program_id(1)) * float(d), has_prev)
        m = jnp.maximum(jnp.max(sp, axis=-1, keepdims=True), jnp.max(sc, axis=-1, keepdims=True))
        pp = jnp.exp(sp - m)
        pc = jnp.exp(sc - m)
        l = jnp.sum(pp, axis=-1, keepdims=True) + jnp.sum(pc, axis=-1, keepdims=True)
        o = (_bdot(pp, _take_residues(vp_ref, d)) + _bdot(pc, _take_residues(vc_ref, d))) / l
        _put_residues(o_ref, jnp.where(lane < DIL_DH, o, m + jnp.log(l)), d)

    cur = lambda part: pl.BlockSpec((span, LANES), lambda i, h: (i, part * DIL_HEADS + h))
    prv = lambda part: pl.BlockSpec((span, LANES), lambda i, h: (jnp.maximum(i - 1, 0), part * DIL_HEADS + h))
    one = pl.BlockSpec((1, LANES), lambda i, h: (0, 0))
    return pl.pallas_call(
        body,
        grid=(nsb, DIL_HEADS),
        in_specs=[cur(0), cur(1), cur(2), prv(1), prv(2), one, one],
        out_specs=pl.BlockSpec((span, LANES), lambda i, h: (i, h)),
        out_shape=jax.ShapeDtypeStruct((t, DIL_HEADS * LANES), F32),
        compiler_params=_cparams(("parallel", "parallel")),
        name=name,
    )(slab, slab, slab, slab, slab, wq_row, wk_row)


def _dil_attn_bwd_strided(slab, stat, wq_row, wk_row, dwq_in, dwk_in, *, group, name):
    d = DIL_GROUPS[group][1]
    t = slab.shape[0]
    span = DIL_SPAN * d
    nsb = t // span

    def body(q_ref, kc_ref, vc_ref, kp_ref, vp_ref, st_ref, wq_ref, wk_ref, dwq_in_ref, dwk_in_ref,
             d_ref, dwq_ref, dwk_ref, dk_carry, dv_carry, spread):
        step = pl.program_id(1)
        has_prev = step < nsb - 1
        first = (pl.program_id(0) == 0) & (step == 0)

        @pl.when(step == 0)
        def _():
            dk_carry[...] = jnp.zeros_like(dk_carry)
            dv_carry[...] = jnp.zeros_like(dv_carry)

        @pl.when(first)
        def _():
            dwq_ref[...] = dwq_in_ref[...]
            dwk_ref[...] = dwk_in_ref[...]

        lane = lax.broadcasted_iota(jnp.int32, (DIL_SPAN, LANES), 1)
        scale = DIL_DH ** -0.5
        q_raw = _take_residues(q_ref, d)
        kc_raw = _take_residues(kc_ref, d)
        vc = _take_residues(vc_ref, d)
        kp_raw = _take_residues(kp_ref, d)
        vp = _take_residues(vp_ref, d)
        st = _take_residues(st_ref, d)
        d_o = jnp.where(lane < DIL_DH, st, 0.0)
        lse = jnp.sum(jnp.where(lane == DIL_DH, st, 0.0), axis=-1, keepdims=True)
        delta = jnp.sum(jnp.where(lane == DIL_DH + 1, st, 0.0), axis=-1, keepdims=True)
        qn = _rms64(q_raw, wq_ref[...])
        kc = _rms64(kc_raw, wk_ref[...])
        kp = _rms64(kp_raw, wk_ref[...])
        sp, sc = _band_logits(qn, kp, kc, _head_slope(group, pl.program_id(0)) * float(d), has_prev)
        pp = jnp.exp(sp - lse)
        pc = jnp.exp(sc - lse)
        dsp = pp * (_bdot_nt(d_o, vp) - delta) * scale
        dsc = pc * (_bdot_nt(d_o, vc) - delta) * scale
        dqn = _bdot(dsp, kp) + _bdot(dsc, kc)
        dkc_n = _bdot_tn(dsc, qn) + dk_carry[...]
        dvc = _bdot_tn(pc, d_o) + dv_carry[...]
        dk_carry[...] = _bdot_tn(dsp, qn)
        dv_carry[...] = _bdot_tn(pp, d_o)
        dq_raw, dwq_rows = _rms64_bwd(q_raw, wq_ref[...], dqn)
        dk_raw, dwk_rows = _rms64_bwd(kc_raw, wk_ref[...], dkc_n)
        for part, val in enumerate((dq_raw, dk_raw, dvc)):
            _put_residues(spread, val, d)
            d_ref[part] = spread[...].astype(d_ref.dtype)
        dwq_ref[...] += jnp.sum(jnp.sum(dwq_rows, axis=0), axis=0, keepdims=True)
        dwk_ref[...] += jnp.sum(jnp.sum(dwk_rows, axis=0), axis=0, keepdims=True)

    at = lambda i: nsb - 1 - i
    cur = lambda part: pl.BlockSpec((span, LANES), lambda h, i: (at(i), part * DIL_HEADS + h))
    prv = lambda part: pl.BlockSpec((span, LANES), lambda h, i: (jnp.maximum(at(i) - 1, 0), part * DIL_HEADS + h))
    one = pl.BlockSpec((1, LANES), lambda h, i: (0, 0))
    return pl.pallas_call(
        body,
        grid=(DIL_HEADS, nsb),
        in_specs=[cur(0), cur(1), cur(2), prv(1), prv(2), pl.BlockSpec((span, LANES), lambda h, i: (at(i), h)),
                  one, one, one, one],
        out_specs=[pl.BlockSpec((3, span, LANES), lambda h, i: (0, at(i), h)), one, one],
        out_shape=[jax.ShapeDtypeStruct((3, t, DIL_HEADS * LANES), MM_DTYPE), jax.ShapeDtypeStruct((1, LANES), F32),
                   jax.ShapeDtypeStruct((1, LANES), F32)],
        scratch_shapes=[pltpu.VMEM((d, DIL_SPAN, LANES), F32), pltpu.VMEM((d, DIL_SPAN, LANES), F32),
                        pltpu.VMEM((span, LANES), F32)],
        compiler_params=_cparams(("arbitrary", "arbitrary")),
        name=name,
    )(slab, slab, slab, slab, slab, stat, wq_row, wk_row, dwq_in, dwk_in)


def _dil_merge_fwd(oe, *, name):
    t = oe[0].shape[0]
    tb = min(t, 1024)

    def body(e0, e1, e2, y_ref, om_ref):
        lane = lax.broadcasted_iota(jnp.int32, (tb, LANES), 1)
        es = [e0[...], e1[...], e2[...]]
        lse = [jnp.sum(jnp.where(lane == DIL_DH, e, 0.0), axis=-1, keepdims=True) for e in es]
        top = jnp.maximum(jnp.maximum(lse[0], lse[1]), lse[2])
        joint = top + jnp.log(jnp.exp(lse[0] - top) + jnp.exp(lse[1] - top) + jnp.exp(lse[2] - top))
        o = sum(jnp.exp(l - joint) * e for l, e in zip(lse, es))
        y_ref[...] = jnp.where(lane < DIL_DH, o, 0.0).astype(y_ref.dtype)
        om_ref[...] = jnp.where(lane < DIL_DH, o, joint)

    blk = pl.BlockSpec((tb, LANES), lambda i, h: (i, h))
    return pl.pallas_call(
        body,
        grid=(t // tb, DIL_HEADS),
        in_specs=[blk] * 3,
        out_specs=[blk, blk],
        out_shape=[jax.ShapeDtypeStruct((t, DIL_HEADS * LANES), MM_DTYPE),
                   jax.ShapeDtypeStruct((t, DIL_HEADS * LANES), F32)],
        compiler_params=_cparams(("parallel", "parallel")),
        name=name,
    )(*oe)


def _dil_merge_bwd(dy, om, *, name):
    t = dy.shape[0]
    tb = min(t, 1024)

    def body(dy_ref, om_ref, st_ref):
        lane = lax.broadcasted_iota(jnp.int32, (tb, LANES), 1)
        d_o = jnp.where(lane < DIL_DH, dy_ref[...], 0.0)
        om_t = om_ref[...]
        delta = jnp.sum(d_o * om_t, axis=-1, keepdims=True)
        st_ref[...] = jnp.where(lane < DIL_DH, d_o, jnp.where(lane == DIL_DH, om_t, jnp.where(lane == DIL_DH + 1, delta, 0.0)))

    blk = pl.BlockSpec((tb, LANES), lambda i, h: (i, h))
    return pl.pallas_call(
        body,
        grid=(t // tb, DIL_HEADS),
        in_specs=[blk, blk],
        out_specs=blk,
        out_shape=jax.ShapeDtypeStruct((t, DIL_HEADS * LANES), F32),
        compiler_params=_cparams(("parallel", "parallel")),
        name=name,
    )(dy, om)


def _rms64_bwd(x, w_row, dy):
    r = lax.rsqrt(jnp.sum(x * x, axis=-1, keepdims=True) * (1.0 / DIL_DH) + RMS_EPS)
    gw = dy * w_row
    dx = r * gw - x * (r * r * r * jnp.sum(gw * x, axis=-1, keepdims=True) * (1.0 / DIL_DH))
    return dx, dy * x * r


def _dil_attn_bwd(slab, stat, wq_row, wk_row, dwq_in, dwk_in, *, group, name):
    dilation = DIL_GROUPS[group][1]
    t = slab.shape[0]
    rows = t // dilation
    nlb = rows // DIL_SPAN
    wide = DIL_HEADS * LANES
    view = slab.reshape(rows, dilation * DIL_SLAB)
    stat_view = stat.reshape(rows, dilation * wide)

    def body(cur_ref, kp_ref, vp_ref, st_ref, wq_ref, wk_ref, dwq_in_ref, dwk_in_ref, d_ref, dwq_ref, dwk_ref,
             dk_carry, dv_carry):
        step = pl.program_id(1)
        has_prev = step < nlb - 1
        first = (pl.program_id(0) == 0) & (step == 0)

        @pl.when(step == 0)
        def _():
            dk_carry[...] = jnp.zeros_like(dk_carry)
            dv_carry[...] = jnp.zeros_like(dv_carry)

        @pl.when(first)
        def _():
            dwq_ref[...] = dwq_in_ref[...]
            dwk_ref[...] = dwk_in_ref[...]

        lane = lax.broadcasted_iota(jnp.int32, (DIL_SPAN, LANES), 1)
        scale = DIL_DH ** -0.5
        q_raw = _stack_heads(cur_ref)
        kc_raw = _stack_heads(cur_ref, first=DIL_HEADS)
        vc = _stack_heads(cur_ref, first=2 * DIL_HEADS)
        kp_raw = _stack_heads(kp_ref)
        vp = _stack_heads(vp_ref)
        st = _stack_heads(st_ref)
        d_o = jnp.where(lane < DIL_DH, st, 0.0)
        lse = jnp.sum(jnp.where(lane == DIL_DH, st, 0.0), axis=-1, keepdims=True)
        delta = jnp.sum(jnp.where(lane == DIL_DH + 1, st, 0.0), axis=-1, keepdims=True)
        qn = _rms64(q_raw, wq_ref[...])
        kc = _rms64(kc_raw, wk_ref[...])
        kp = _rms64(kp_raw, wk_ref[...])
        sp, sc = _band_logits(qn, kp, kc, _alibi_slopes(group) * float(dilation), has_prev)
        pp = jnp.exp(sp - lse)
        pc = jnp.exp(sc - lse)
        dsp = pp * (_bdot_nt(d_o, vp) - delta) * scale
        dsc = pc * (_bdot_nt(d_o, vc) - delta) * scale
        dqn = _bdot(dsp, kp) + _bdot(dsc, kc)
        dkc_n = _bdot_tn(dsc, qn) + _stack_heads(dk_carry)
        dvc = _bdot_tn(pc, d_o) + _stack_heads(dv_carry)
        _store_heads(dk_carry, _bdot_tn(dsp, qn))
        _store_heads(dv_carry, _bdot_tn(pp, d_o))
        dq_raw, dwq_rows = _rms64_bwd(q_raw, wq_ref[...], dqn)
        dk_raw, dwk_rows = _rms64_bwd(kc_raw, wk_ref[...], dkc_n)
        _store_heads(d_ref, dq_raw)
        _store_heads(d_ref, dk_raw, first=DIL_HEADS)
        _store_heads(d_ref, dvc, first=2 * DIL_HEADS)
        dwq_ref[...] += jnp.sum(jnp.sum(dwq_rows, axis=0), axis=0, keepdims=True)
        dwk_ref[...] += jnp.sum(jnp.sum(dwk_rows, axis=0), axis=0, keepdims=True)

    blk_i = lambda i: nlb - 1 - i
    cur = pl.BlockSpec((DIL_SPAN, DIL_SLAB), lambda r, i: (blk_i(i), r))
    prv = lambda part: pl.BlockSpec((DIL_SPAN, wide), lambda r, i: (jnp.maximum(blk_i(i) - 1, 0), 3 * r + part))
    one = pl.BlockSpec((1, LANES), lambda r, i: (0, 0))
    dslab, dwq, dwk = pl.pallas_call(
        body,
        grid=(dilation, nlb),
        in_specs=[cur, prv(1), prv(2), pl.BlockSpec((DIL_SPAN, wide), lambda r, i: (blk_i(i), r)), one, one, one, one],
        out_specs=[cur, one, one],
        out_shape=[jax.ShapeDtypeStruct((rows, dilation * DIL_SLAB), MM_DTYPE), jax.ShapeDtypeStruct((1, LANES), F32),
                   jax.ShapeDtypeStruct((1, LANES), F32)],
        scratch_shapes=[pltpu.VMEM((DIL_SPAN, wide), F32), pltpu.VMEM((DIL_SPAN, wide), F32)],
        compiler_params=_cparams(("arbitrary", "arbitrary")),
        name=name,
    )(view, view, view, stat_view, wq_row, wk_row, dwq_in, dwk_in)
    return dslab.reshape(t, DIL_SLAB), dwq, dwk


def _row(v, width=LANES):
    v = v.astype(F32).reshape(-1)
    return jnp.pad(v, (0, width - v.shape[0])).reshape(1, width)


def _prepare_weights(w):
    return dict(gdn=_prepare_gdn(w), dil=_prepare_dil(w), ffn=_prepare_ffn(w))


def _prepare_gdn(w, layers=range(DEPTH // 2)):
    gdn = {}
    for j in layers:
        wt = w["gdn_w_in"][j]
        gates_t = jnp.pad(wt[GDN_MAIN:], ((0, LANES - 2 * GDN_HEADS), (0, 0)))
        gdn[j] = dict(in_t=wt, gates_t=gates_t, out=w["gdn_w_out"][j], conv=w["gdn_conv_w"][j].astype(F32),
                      alog=_row(w["gdn_a_log"][j]), dt=_row(w["gdn_dt_bias"][j]), nw=_row(w["gdn_norm_w"][j]))
    return gdn


def _prepare_dil(w, layers=range(DEPTH // 2)):
    d = D_MODEL
    dil = {}
    for j in layers:
        wt = w["dil_w_in"][j].reshape(3, len(DIL_GROUPS), DIL_HEADS, DIL_DH, d)
        wg_t = [wt[:, g].reshape(DIL_SLAB // 2, d) for g in range(len(DIL_GROUPS))]
        out_t = jnp.pad(w["dil_w_out"][j].reshape(d, DIL_HEADS, DIL_DH), ((0, 0), (0, 0), (0, LANES - DIL_DH)))
        dil[j] = dict(wg_t=wg_t, out_t=out_t.reshape(d, DIL_HEADS * LANES), wq=_row(w["dil_q_norm"][j]),
                      wk=_row(w["dil_k_norm"][j]))
    return dil


def _prepare_ffn(w, layers=range(DEPTH)):
    return {i: dict(in_t=w["ffn_w_in"][i], out=w["ffn_w_out"][i]) for i in layers}


def _gdn_layer_fwd(x, nrow, p):
    hn = _rmsnorm_fwd(x, nrow, name="rmsnorm_fwd")
    pm = _matmul(hn, p["in_t"], trans_b=True, b_rows=(0, GDN_MAIN), name="gdn_proj_main")
    ab = _matmul(hn, p["gates_t"], trans_b=True, name="gdn_proj_gates")
    qkv = _gdn_conv_fwd(pm, p["conv"], name="gdn_conv_fwd")
    gb, bb = _gdn_gates_fwd(ab, p["alog"], p["dt"], name="gdn_gates_fwd")
    prep, rest = _gdn_prep_fwd(qkv, gb, bb, name="gdn_prep_fwd")
    o, states = _gdn_scan_fwd(prep, name="gdn_scan_fwd")
    og = _gdn_outgate_fwd(o, pm, p["nw"], name="gdn_outgate_fwd")
    y = _matmul(og, p["out"], add=x, name="gdn_proj_out")
    return y, (x, hn, pm, ab, qkv, gb, bb, prep, rest, states, o, og)


def _gdn_layer_bwd(dx, dxb, nrow, p, saved):
    x, hn, pm, ab, qkv, gb, bb, prep, rest, states, o, og = saved
    d_og = _matmul(dxb, p["out"], trans_b=True, name="gdn_dgate")
    g_out = _matmul(og, dxb, trans_a=True, out_dtype=MM_DTYPE, name="gdn_gw_out")
    d_o, d_z, d_nw = _gdn_outgate_bwd(o, pm, p["nw"], d_og, name="gdn_outgate_bwd")
    cts = _gdn_scan_bwd(prep, states, d_o, name="gdn_scan_bwd")
    dqkv, dgb, dbb = _gdn_prep_bwd(qkv, gb, bb, rest, cts, name="gdn_prep_bwd")
    d_ab, d_alog, d_dt = _gdn_gates_bwd(ab, p["alog"], p["dt"], dgb, dbb, name="gdn_gates_bwd")
    d_conv, g_conv = _gdn_conv_bwd(pm, p["conv"], dqkv, name="gdn_conv_bwd")
    d_hn = _matmul(d_conv, p["in_t"], b_rows=(0, GDN_QKV), name="gdn_dhn_qkv")
    d_hn = _matmul(d_z, p["in_t"], b_rows=(GDN_QKV, GDN_MAIN - GDN_QKV), add=d_hn, name="gdn_dhn_z")
    d_hn = _matmul(d_ab, p["gates_t"], add=d_hn, name="gdn_dhn_gates")
    g_in_t = jnp.concatenate([
        _matmul(d_conv, hn, trans_a=True, out_dtype=MM_DTYPE, name="gdn_gw_qkv"),
        _matmul(d_z, hn, trans_a=True, out_dtype=MM_DTYPE, name="gdn_gw_z"),
        _matmul(d_ab, hn, trans_a=True, out_dtype=MM_DTYPE, name="gdn_gw_gates")[:2 * GDN_HEADS],
    ], axis=0)
    dx_new, dxb_new, g_norm = _rmsnorm_bwd(x, nrow, d_hn, dx, name="rmsnorm_bwd")
    grads = dict(w_in=g_in_t, conv=g_conv, a_log=d_alog[0, :GDN_HEADS], dt_bias=d_dt[0, :GDN_HEADS], norm_w=d_nw[0],
                 w_out=g_out, norm=g_norm[0])
    return dx_new, dxb_new, grads


def _dil_layer_fwd(x, nrow, p):
    hn = _rmsnorm_fwd(x, nrow, name="rmsnorm_fwd")
    slabs = [_matmul(hn, p["wg_t"][g], trans_b=True, spread_out=True, name="dil_proj_in") for g in range(len(DIL_GROUPS))]
    oe = [(_dil_attn_fwd if DIL_GROUPS[g][1] == 1 else _dil_attn_fwd_strided)(
        slabs[g], p["wq"], p["wk"], group=g, name=f"dil_attn_fwd_g{g}") for g in range(len(DIL_GROUPS))]
    y, om = _dil_merge_fwd(oe, name="dil_merge_fwd")
    out = _matmul(y, p["out_t"], trans_b=True, add=x, name="dil_proj_out")
    return out, (x, hn, slabs, y, om)


def _dil_layer_bwd(dx, dxb, nrow, p, saved):
    x, hn, slabs, y, om = saved
    d_y = _matmul(dxb, p["out_t"], name="dil_dmerged")
    g_out_t = _matmul(dxb, y, trans_a=True, out_dtype=MM_DTYPE, name="dil_gw_out")
    g_out_t = g_out_t.reshape(D_MODEL, DIL_HEADS, LANES)[..., :DIL_DH].reshape(D_MODEL, DIL_HEADS * DIL_DH)
    stat = _dil_merge_bwd(d_y, om, name="dil_merge_bwd")
    d_hn = None
    dwq = jnp.zeros((1, LANES), F32)
    dwk = jnp.zeros((1, LANES), F32)
    g_groups = []
    wide = DIL_HEADS * LANES
    for g in range(len(DIL_GROUPS)):
        if DIL_GROUPS[g][1] == 1:
            dslab, dwq, dwk = _dil_attn_bwd(slabs[g], stat, p["wq"], p["wk"], dwq, dwk, group=g, name=f"dil_attn_bwd_g{g}")
            d_hn = _matmul(dslab, p["wg_t"][g], packed_a=True, add=d_hn, name="dil_dhn")
            g_w = _matmul(dslab, hn, trans_a=True, packed_a=True, out_dtype=MM_DTYPE, name="dil_gw_in")
        else:
            dparts, dwq, dwk = _dil_attn_bwd_strided(slabs[g], stat, p["wq"], p["wk"], dwq, dwk, group=g,
                                                     name=f"dil_attn_bwd_g{g}")
            for part in range(3):
                d_hn = _matmul(dparts, p["wg_t"][g], a_lead=part, packed_a=True, b_rows=(part * wide // 2, wide // 2),
                               add=d_hn, name="dil_dhn_part")
            g_w = jnp.stack([_matmul(dparts, hn, trans_a=True, a_lead=part, packed_a=True, out_dtype=MM_DTYPE,
                                     name="dil_gw_in_part") for part in range(3)])
        g_groups.append(g_w.reshape(3, DIL_HEADS, DIL_DH, D_MODEL))
    g_in_t = jnp.stack(g_groups, axis=1).reshape(3 * len(DIL_GROUPS) * DIL_HEADS * DIL_DH, D_MODEL)
    dx_new, dxb_new, g_norm = _rmsnorm_bwd(x, nrow, d_hn, dx, name="rmsnorm_bwd")
    grads = dict(w_in=g_in_t, q_norm=dwq[0, :DIL_DH], k_norm=dwk[0, :DIL_DH], w_out=g_out_t, norm=g_norm[0])
    return dx_new, dxb_new, grads


def _ffn_layer_fwd(x, nrow, p):
    hn = _rmsnorm_fwd(x, nrow, name="rmsnorm_fwd")
    gate, up, act = _ffn_in(hn, p["in_t"], name="ffn_proj_in")
    y = _matmul(act, p["out"], add=x, name="ffn_proj_out")
    return y, (x, hn, gate, up, act)


def _ffn_layer_bwd(dx, dxb, nrow, p, saved):
    x, hn, gate, up, act = saved
    g_out = _matmul(act, dxb, trans_a=True, out_dtype=MM_DTYPE, name="ffn_gw_out")
    d_g, d_u = _ffn_dact(dxb, p["out"], gate, up, name="ffn_dact")
    d_hn = _matmul(d_g, p["in_t"], b_rows=(0, FFN_HIDDEN), name="ffn_dhn_gate")
    d_hn = _matmul(d_u, p["in_t"], b_rows=(FFN_HIDDEN, FFN_HIDDEN), add=d_hn, name="ffn_dhn_up")
    g_in_t = jnp.concatenate([_matmul(d_g, hn, trans_a=True, out_dtype=MM_DTYPE, name="ffn_gw_gate"),
                              _matmul(d_u, hn, trans_a=True, out_dtype=MM_DTYPE, name="ffn_gw_up")], axis=0)
    dx_new, dxb_new, g_norm = _rmsnorm_bwd(x, nrow, d_hn, dx, name="rmsnorm_bwd")
    return dx_new, dxb_new, dict(w_in=g_in_t, w_out=g_out, norm=g_norm[0])


def _mixer_fwd(i, x, mix_row, prepared):
    if i % 2 == 0:
        return _gdn_layer_fwd(x, mix_row, prepared["gdn"][i // 2])
    return _dil_layer_fwd(x, mix_row, prepared["dil"][i // 2])


def _mixer_bwd(i, dx, dxb, mix_row, prepared, saved):
    if i % 2 == 0:
        return _gdn_layer_bwd(dx, dxb, mix_row, prepared["gdn"][i // 2], saved)
    return _dil_layer_bwd(dx, dxb, mix_row, prepared["dil"][i // 2], saved)


def _local_step(x, target, prepared, norm_mix, norm_ffn):
    saved = []
    for i in range(DEPTH):
        x, s_mix = _mixer_fwd(i, x, norm_mix[i].reshape(1, D_MODEL), prepared)
        x, s_ffn = _ffn_layer_fwd(x, norm_ffn[i].reshape(1, D_MODEL), prepared["ffn"][i])
        saved.append((s_mix, s_ffn))
    dx, dxb, loss = _loss_head(x, target, name="loss_head")
    g_mix, g_ffn = [None] * DEPTH, [None] * DEPTH
    for i in reversed(range(DEPTH)):
        s_mix, s_ffn = saved[i]
        dx, dxb, g_ffn[i] = _ffn_layer_bwd(dx, dxb, norm_ffn[i].reshape(1, D_MODEL), prepared["ffn"][i], s_ffn)
        dx, dxb, g_mix[i] = _mixer_bwd(i, dx, dxb, norm_mix[i].reshape(1, D_MODEL), prepared, s_mix)
    return loss[0, 0], dx, _collect_grads(g_mix, g_ffn)


def _collect_grads(g_mix, g_ffn):
    gdn = [g_mix[i] for i in range(0, DEPTH, 2)]
    dil = [g_mix[i] for i in range(1, DEPTH, 2)]
    if any(g is None for g in g_mix + g_ffn):
        pick = lambda gs, key: [None if g is None else g[key] for g in gs]
        return dict(gdn_w_in=pick(gdn, "w_in"), gdn_w_out=pick(gdn, "w_out"), dil_w_in=pick(dil, "w_in"),
                    dil_w_out=pick(dil, "w_out"), ffn_w_in=pick(g_ffn, "w_in"), ffn_w_out=pick(g_ffn, "w_out"))
    grads = dict(
        norm_mix=jnp.stack([g["norm"] for g in g_mix]),
        norm_ffn=jnp.stack([g["norm"] for g in g_ffn]),
        gdn_w_in=[g["w_in"] for g in gdn],
        gdn_conv_w=jnp.stack([g["conv"] for g in gdn]),
        gdn_a_log=jnp.stack([g["a_log"] for g in gdn]),
        gdn_dt_bias=jnp.stack([g["dt_bias"] for g in gdn]),
        gdn_norm_w=jnp.stack([g["norm_w"] for g in gdn]),
        gdn_w_out=[g["w_out"] for g in gdn],
        dil_w_in=[g["w_in"] for g in dil],
        dil_q_norm=jnp.stack([g["q_norm"] for g in dil]),
        dil_k_norm=jnp.stack([g["k_norm"] for g in dil]),
        dil_w_out=[g["w_out"] for g in dil],
        ffn_w_in=[g["w_in"] for g in g_ffn],
        ffn_w_out=[g["w_out"] for g in g_ffn],
    )
    return grads


MESH_ID = pl.DeviceIdType.MESH
ANY_SPACE = pl.BlockSpec(memory_space=pl.ANY)


def _mesh_position():
    return lax.axis_index("x"), lax.axis_index("y"), lax.axis_index("c")


def _flip(pos, k):
    x, y, c = pos
    return (1 - x if k & 4 else x, 1 - y if k & 2 else y, 1 - c if k & 1 else c)


def _linear(pos):
    return 4 * pos[0] + 2 * pos[1] + pos[2]


def _comm_scratch():
    return [pltpu.SemaphoreType.DMA((N_DEV - 1,)), pltpu.SemaphoreType.DMA((N_DEV - 1,)), pltpu.SemaphoreType.DMA(())]


def _all_gather(shard, *, name):
    def body(x_ref, out_ref, send_sems, recv_sems, local_sem):
        me = _mesh_position()
        mine = out_ref.at[_linear(me)]
        local = pltpu.make_async_copy(x_ref, mine, local_sem)
        local.start()
        copies = []
        for k in range(1, N_DEV):
            cp = pltpu.make_async_remote_copy(src_ref=x_ref, dst_ref=mine, send_sem=send_sems.at[k - 1],
                                              recv_sem=recv_sems.at[k - 1], device_id=_flip(me, k), device_id_type=MESH_ID)
            cp.start()
            copies.append(cp)
        for cp in copies:
            cp.wait()
        local.wait()

    return pl.pallas_call(
        body,
        out_shape=jax.ShapeDtypeStruct((N_DEV,) + shard.shape, shard.dtype),
        in_specs=[ANY_SPACE],
        out_specs=ANY_SPACE,
        scratch_shapes=_comm_scratch(),
        name=name,
    )(shard)


def _exchange(parts, *, name):
    def body(p_ref, out_ref, send_sems, recv_sems, local_sem):
        me = _mesh_position()
        mine = out_ref.at[_linear(me)]
        local = pltpu.make_async_copy(p_ref.at[_linear(me)], mine, local_sem)
        local.start()
        copies = []
        for k in range(1, N_DEV):
            peer = _flip(me, k)
            cp = pltpu.make_async_remote_copy(src_ref=p_ref.at[_linear(peer)], dst_ref=mine, send_sem=send_sems.at[k - 1],
                                              recv_sem=recv_sems.at[k - 1], device_id=peer, device_id_type=MESH_ID)
            cp.start()
            copies.append(cp)
        for cp in copies:
            cp.wait()
        local.wait()

    return pl.pallas_call(
        body,
        out_shape=jax.ShapeDtypeStruct(parts.shape, parts.dtype),
        in_specs=[ANY_SPACE],
        out_specs=ANY_SPACE,
        scratch_shapes=_comm_scratch(),
        name=name,
    )(parts)


HBM_SPACE = pl.BlockSpec(memory_space=pltpu.HBM)
SEM_SPACE = pl.BlockSpec(memory_space=pltpu.SEMAPHORE)
DATAFLOW = pltpu.SideEffectType.DATAFLOW_SIDE_EFFECTING


def _split_copies(src_ref, land_ref, send_sems, recv_sems, per_peer):
    me = _mesh_position()
    mine = land_ref.at[_linear(me)]
    copies = []
    for k in range(1, N_DEV):
        peer = _flip(me, k)
        src = src_ref.at[_linear(peer)] if per_peer else src_ref
        copies.append(pltpu.make_async_remote_copy(src_ref=src, dst_ref=mine, send_sem=send_sems.at[k - 1],
                                                   recv_sem=recv_sems.at[k - 1], device_id=peer, device_id_type=MESH_ID))
    return copies


def _travel_start(src, after, *, per_peer, name):
    me = _linear(_mesh_position())
    own = src[me] if per_peer else src
    shape = own.shape
    landing = lax.dynamic_update_slice(lax.empty((N_DEV,) + shape, src.dtype), own[None], (me, 0, 0))

    def body(src_ref, land_ref, after_ref, send_sems, recv_sems, src_thru, land_thru, token):
        for cp in _split_copies(src_ref, land_ref, send_sems, recv_sems, per_peer):
            cp.start()
        token[...] = jnp.zeros_like(token)

    return pl.pallas_call(
        body,
        name=name,
        out_shape=(pltpu.SemaphoreType.DMA((N_DEV - 1,)), pltpu.SemaphoreType.DMA((N_DEV - 1,)),
                   pltpu.HBM(src.shape, src.dtype), pltpu.HBM(landing.shape, landing.dtype),
                   jax.ShapeDtypeStruct((8, LANES), F32)),
        in_specs=(HBM_SPACE, HBM_SPACE, ANY_SPACE),
        out_specs=(SEM_SPACE, SEM_SPACE, HBM_SPACE, HBM_SPACE, pl.BlockSpec(memory_space=pltpu.VMEM)),
        input_output_aliases={0: 2, 1: 3},
        compiler_params=pltpu.CompilerParams(has_side_effects=DATAFLOW),
    )(pltpu.with_memory_space_constraint(src, pltpu.HBM), pltpu.with_memory_space_constraint(landing, pltpu.HBM), after)


def _travel_wait(started, after, *, per_peer, name):
    send_sems, recv_sems, src_thru, land_thru, _ = started

    def body(src_ref, land_ref, send_sems, recv_sems, after_ref, src_dead, got_ref):
        for cp in _split_copies(src_ref, land_ref, send_sems, recv_sems, per_peer):
            cp.wait_send()
            cp.wait_recv()

    return pl.pallas_call(
        body,
        name=name,
        out_shape=(pltpu.HBM(src_thru.shape, src_thru.dtype), pltpu.HBM(land_thru.shape, land_thru.dtype)),
        in_specs=(HBM_SPACE, HBM_SPACE, SEM_SPACE, SEM_SPACE, ANY_SPACE),
        out_specs=(HBM_SPACE, HBM_SPACE),
        input_output_aliases={0: 0, 1: 1},
        compiler_params=pltpu.CompilerParams(has_side_effects=DATAFLOW),
    )(src_thru, land_thru, send_sems, recv_sems, after)[1]


def _adamw(parts, w, m, v, *, name):
    rows, n = w.shape
    tb = _pick(rows, (PACK_ROW_ALIGN, 16))
    c1 = 1.0 - ADAM_B1 ** ADAM_STEP
    c2 = 1.0 - ADAM_B2 ** ADAM_STEP

    def body(p_ref, w_ref, m_ref, v_ref, g_ref, d_ref, nm_ref, nv_ref):
        g = p_ref[0].astype(F32)
        for s in range(1, N_DEV):
            g = g + p_ref[s].astype(F32)
        m_new = ADAM_B1 * m_ref[...] + (1.0 - ADAM_B1) * g
        v_new = ADAM_B2 * v_ref[...] + (1.0 - ADAM_B2) * (g * g)
        m_hat = m_new / c1
        v_hat = v_new / c2
        g_ref[...] = g
        nm_ref[...] = m_new
        nv_ref[...] = v_new
        d_ref[...] = -ADAM_LR * (m_hat / (jnp.sqrt(v_hat) + ADAM_EPS) + ADAM_WD * w_ref[...])

    blk = pl.BlockSpec((tb, n), lambda i: (i, 0))
    return pl.pallas_call(
        body,
        grid=(rows // tb,),
        in_specs=[pl.BlockSpec((N_DEV, tb, n), lambda i: (0, i, 0)), blk, blk, blk],
        out_specs=[blk] * 4,
        out_shape=[jax.ShapeDtypeStruct((rows, n), F32)] * 4,
        compiler_params=_cparams(("parallel",)),
        name=name,
    )(parts, w, m, v)


PACK_WIDTH = 1024
SHARDED = {
    "gdn_w_in": ((2, D_MODEL, GDN_IN_WIDTH), 2),
    "gdn_conv_w": ((2, GDN_CONV, GDN_QKV), 2),
    "gdn_w_out": ((2, GDN_HEADS * GDN_DV, D_MODEL), 1),
    "dil_w_in": ((2, D_MODEL, 3 * len(DIL_GROUPS) * DIL_HEADS * DIL_DH), 2),
    "dil_w_out": ((2, DIL_HEADS * DIL_DH, D_MODEL), 2),
    "ffn_w_in": ((DEPTH, D_MODEL, 2 * FFN_HIDDEN), 2),
    "ffn_w_out": ((DEPTH, FFN_HIDDEN, D_MODEL), 1),
}
REPLICATED = {"norm_mix": (DEPTH, D_MODEL), "norm_ffn": (DEPTH, D_MODEL), "gdn_a_log": (2, GDN_HEADS),
              "gdn_dt_bias": (2, GDN_HEADS), "gdn_norm_w": (2, GDN_DV), "dil_q_norm": (2, DIL_DH), "dil_k_norm": (2, DIL_DH)}
WEIGHT_ORDER = ("norm_mix", "norm_ffn", "gdn_w_in", "gdn_conv_w", "gdn_a_log", "gdn_dt_bias", "gdn_norm_w", "gdn_w_out",
                "dil_w_in", "dil_q_norm", "dil_k_norm", "dil_w_out", "ffn_w_in", "ffn_w_out")
PACK_ROW_ALIGN = 128
PIECE_ALIGN = 16
SMALL_ROWS = 16


def _shard_shape(name):
    shape, axis = SHARDED[name]
    return tuple(s // N_DEV if i == axis else s for i, s in enumerate(shape))


def _shard_rows(name):
    return math.prod(_shard_shape(name)) // PACK_WIDTH


def _split_shards(full, name):
    shape, axis = SHARDED[name]
    split = full.reshape(shape[:axis] + (N_DEV, shape[axis] // N_DEV) + shape[axis + 1:])
    return jnp.moveaxis(split, axis, 0)


def _join_shards(stacked, name):
    shape, axis = SHARDED[name]
    return jnp.moveaxis(stacked, 0, axis).reshape(shape)


COLUMN_SHARDED = ("gdn_w_in", "dil_w_in", "dil_w_out", "ffn_w_in")


def _to_rows(shard, name):
    if name in COLUMN_SHARDED:
        shard = jnp.swapaxes(shard, 1, 2)
    return shard.reshape(-1, PACK_WIDTH)


def _layer_columns(name):
    _, r, c = _shard_shape(name)
    return r if name in COLUMN_SHARDED else c


def _piece_rows(piece, halves=1):
    name, layer = piece
    rows = _shard_rows(name) * halves
    return rows if layer is None else rows // SHARDED[name][0][0]


def _aligned(rows, to=PIECE_ALIGN):
    return -(-rows // to) * to


def _pack_pieces(arrays, total_align=PIECE_ALIGN):
    padded, total = [], 0
    for a in arrays:
        rows = a.shape[-2]
        extra = _aligned(rows) - rows
        if extra:
            a = jnp.pad(a, [(0, 0)] * (a.ndim - 2) + [(0, extra), (0, 0)])
        padded.append(a)
        total += rows + extra
    tail = _aligned(total, total_align) - total
    if tail:
        padded.append(jnp.zeros(padded[0].shape[:-2] + (tail, PACK_WIDTH), padded[0].dtype))
    return jnp.concatenate(padded, axis=-2)


def _piece_offsets(pieces, halves=None):
    out, at = [], 0
    for p in pieces:
        rows = _piece_rows(p, (halves or {}).get(p[0], 1))
        out.append((p, at, rows))
        at += _aligned(rows)
    return out


def _shard_piece_rows(src, piece):
    name, layer = piece
    part = src[name] if layer is None else src[name][layer:layer + 1]
    return _to_rows(part.astype(F32), name)


def _piece_from_rows(rows, piece):
    name, layer = piece
    layers, r, c = _shard_shape(name)
    n_l = layers if layer is None else 1
    if name in COLUMN_SHARDED:
        return jnp.swapaxes(rows.reshape(n_l, c, r), 1, 2)
    return rows.reshape(n_l, r, c)


SMALL_TAIL = tuple(n for n in REPLICATED if n not in ("norm_mix", "norm_ffn"))


def _pack_small(vals):
    tail, at = jnp.zeros((PACK_WIDTH,), F32), 0
    for n in SMALL_TAIL:
        vec = vals[n].astype(F32).reshape(-1)
        tail = tail + jnp.pad(vec, (at, PACK_WIDTH - at - vec.shape[0]))
        at += vec.shape[0]
    buf = jnp.pad(vals["norm_mix"].astype(F32), ((0, SMALL_ROWS - DEPTH), (0, 0)))
    buf = buf + jnp.pad(vals["norm_ffn"].astype(F32), ((8, SMALL_ROWS - 8 - DEPTH), (0, 0)))
    return buf + jnp.pad(tail.reshape(1, PACK_WIDTH), ((SMALL_ROWS - 1, 0), (0, 0)))


def _unpack_small(buf):
    out = {"norm_mix": buf[0:DEPTH], "norm_ffn": buf[8:8 + DEPTH]}
    at = 0
    for n in SMALL_TAIL:
        size = math.prod(REPLICATED[n])
        out[n] = buf[SMALL_ROWS - 1, at:at + size].reshape(REPLICATED[n])
        at += size
    return out


GATHER_FIRST = (("gdn_w_in", 0), ("gdn_conv_w", None), ("gdn_w_out", 0))
GATHER_NEXT = (("ffn_w_in", 0), ("ffn_w_out", 0), ("dil_w_in", 0), ("dil_w_out", 0))
GATHER_LAST = (("ffn_w_in", 1), ("ffn_w_out", 1), ("gdn_w_in", 1), ("gdn_w_out", 1), ("ffn_w_in", 2), ("ffn_w_out", 2),
               ("dil_w_in", 1), ("dil_w_out", 1), ("ffn_w_in", 3), ("ffn_w_out", 3))
EXCHANGE_GROUPS = (
    (("ffn_w_in", 3), ("ffn_w_out", 3), ("dil_w_in", 1), ("dil_w_out", 1),
     ("ffn_w_in", 2), ("ffn_w_out", 2), ("gdn_w_in", 1), ("gdn_w_out", 1)),
    (("ffn_w_in", 1), ("ffn_w_out", 1), ("dil_w_in", 0), ("dil_w_out", 0)),
    (("ffn_w_in", 0), ("ffn_w_out", 0)),
    (("gdn_w_in", 0), ("gdn_w_out", 0), ("gdn_conv_w", None)),
)
EXCHANGE_AFTER = {("mix", 2): 0, ("mix", 1): 1, ("ffn", 0): 2}


def _gather_operand(w, pieces):
    arrays = []
    for n, layer in pieces:
        if layer is None:
            arrays.append(lax.bitcast_convert_type(w[n], BF16).reshape(-1, PACK_WIDTH))
        else:
            arrays.append(_to_rows(w[n][layer:layer + 1].astype(BF16), n))
    return _pack_pieces(arrays)


def _gathered_weights(gathered, pieces, full):
    for (n, layer), at, rows in _piece_offsets(pieces, halves={"gdn_conv_w": 2}):
        block = gathered[:, at:at + rows]
        if layer is None:
            block = lax.bitcast_convert_type(block.reshape((N_DEV,) + _shard_shape(n) + (2,)), F32)
            full[n] = _join_shards(block, n)
        else:
            full.setdefault(n, {})[layer] = block.reshape(-1, _layer_columns(n))
    return full


def _exchange_operand(grads, pieces):
    arrays = []
    for n, layer in pieces:
        if layer is None:
            arrays.append(_split_shards(grads[n], n).astype(BF16).reshape(N_DEV, -1, PACK_WIDTH))
        else:
            arrays.append(grads[n][layer].astype(BF16).reshape(N_DEV, -1, PACK_WIDTH))
    return _pack_pieces(arrays, total_align=PACK_ROW_ALIGN)


def _update_group(received, pieces, w, m, v, *, name):
    packed = [_pack_pieces([_shard_piece_rows(src, p) for p in pieces], total_align=PACK_ROW_ALIGN) for src in (w, m, v)]
    outs = _adamw(received, *packed, name=name)
    return {p: tuple(_piece_from_rows(o[at:at + rows], p) for o in outs) for p, at, rows in _piece_offsets(pieces)}


def kernel(x, norm_mix, norm_ffn, gdn_w_in, gdn_conv_w, gdn_a_log, gdn_dt_bias, gdn_norm_w, gdn_w_out, dil_w_in, dil_q_norm, dil_k_norm, dil_w_out, ffn_w_in, ffn_w_out, loss_target, m_norm_mix, m_norm_ffn, m_gdn_w_in, m_gdn_conv_w, m_gdn_a_log, m_gdn_dt_bias, m_gdn_norm_w, m_gdn_w_out, m_dil_w_in, m_dil_q_norm, m_dil_k_norm, m_dil_w_out, m_ffn_w_in, m_ffn_w_out, v_norm_mix, v_norm_ffn, v_gdn_w_in, v_gdn_conv_w, v_gdn_a_log, v_gdn_dt_bias, v_gdn_norm_w, v_gdn_w_out, v_dil_w_in, v_dil_q_norm, v_dil_k_norm, v_dil_w_out, v_ffn_w_in, v_ffn_w_out):
    w = dict(norm_mix=norm_mix, norm_ffn=norm_ffn, gdn_w_in=gdn_w_in, gdn_conv_w=gdn_conv_w, gdn_a_log=gdn_a_log,
             gdn_dt_bias=gdn_dt_bias, gdn_norm_w=gdn_norm_w, gdn_w_out=gdn_w_out, dil_w_in=dil_w_in, dil_q_norm=dil_q_norm,
             dil_k_norm=dil_k_norm, dil_w_out=dil_w_out, ffn_w_in=ffn_w_in, ffn_w_out=ffn_w_out)
    m = dict(norm_mix=m_norm_mix, norm_ffn=m_norm_ffn, gdn_w_in=m_gdn_w_in, gdn_conv_w=m_gdn_conv_w, gdn_a_log=m_gdn_a_log,
             gdn_dt_bias=m_gdn_dt_bias, gdn_norm_w=m_gdn_norm_w, gdn_w_out=m_gdn_w_out, dil_w_in=m_dil_w_in,
             dil_q_norm=m_dil_q_norm, dil_k_norm=m_dil_k_norm, dil_w_out=m_dil_w_out, ffn_w_in=m_ffn_w_in, ffn_w_out=m_ffn_w_out)
    v = dict(norm_mix=v_norm_mix, norm_ffn=v_norm_ffn, gdn_w_in=v_gdn_w_in, gdn_conv_w=v_gdn_conv_w, gdn_a_log=v_gdn_a_log,
             gdn_dt_bias=v_gdn_dt_bias, gdn_norm_w=v_gdn_norm_w, gdn_w_out=v_gdn_w_out, dil_w_in=v_dil_w_in,
             dil_q_norm=v_dil_q_norm, dil_k_norm=v_dil_k_norm, dil_w_out=v_dil_w_out, ffn_w_in=v_ffn_w_in, ffn_w_out=v_ffn_w_out)
    def row(src, i):
        return src[i].reshape(1, D_MODEL)

    first = _all_gather(_gather_operand(w, GATHER_FIRST), name="weight_all_gather_first")
    next_started = _travel_start(_gather_operand(w, GATHER_NEXT), first, per_peer=False, name="weight_gather_start_next")
    last_started = _travel_start(_gather_operand(w, GATHER_LAST), next_started[4], per_peer=False,
                                 name="weight_gather_start_last")
    full = _gathered_weights(first, GATHER_FIRST, {n: w[n] for n in REPLICATED})
    prepared = dict(gdn=_prepare_gdn(full, layers=(0,)))
    h = x[0]
    saved = [None] * DEPTH
    h, s_mix = _mixer_fwd(0, h, row(norm_mix, 0) + last_started[4][0, 0], prepared)
    got = _travel_wait(next_started, h, per_peer=False, name="weight_gather_wait_next")
    full = _gathered_weights(got, GATHER_NEXT, full)
    prepared.update(dil=_prepare_dil(full, layers=(0,)), ffn=_prepare_ffn(full, layers=(0,)))
    for i in range(DEPTH):
        if i > 0:
            h, s_mix = _mixer_fwd(i, h, row(norm_mix, i), prepared)
        if i == 1:
            got = _travel_wait(last_started, h, per_peer=False, name="weight_gather_wait_last")
            full = _gathered_weights(got, GATHER_LAST, full)
            prepared["gdn"].update(_prepare_gdn(full, layers=(1,)))
            prepared["dil"].update(_prepare_dil(full, layers=(1,)))
            prepared["ffn"].update(_prepare_ffn(full, layers=(1, 2, 3)))
        h, s_ffn = _ffn_layer_fwd(h, row(norm_ffn, i), prepared["ffn"][i])
        saved[i] = (s_mix, s_ffn)
    dx, dxb, loss = _loss_head(h, loss_target[0], name="loss_head")

    g_mix, g_ffn = [None] * DEPTH, [None] * DEPTH
    started = {}

    def travel(group, dxb):
        operand = _exchange_operand(_collect_grads(g_mix, g_ffn), EXCHANGE_GROUPS[group])
        started[group] = _travel_start(operand, dx, per_peer=True, name=f"grad_exchange_start_{group}")
        return dxb + started[group][4][0, 0].astype(dxb.dtype)

    for i in reversed(range(DEPTH)):
        s_mix, s_ffn = saved[i]
        dx, dxb, g_ffn[i] = _ffn_layer_bwd(dx, dxb, row(norm_ffn, i), prepared["ffn"][i], s_ffn)
        if ("ffn", i) in EXCHANGE_AFTER:
            dxb = travel(EXCHANGE_AFTER[("ffn", i)], dxb)
        dx, dxb, g_mix[i] = _mixer_bwd(i, dx, dxb, row(norm_mix, i), prepared, s_mix)
        if ("mix", i) in EXCHANGE_AFTER:
            dxb = travel(EXCHANGE_AFTER[("mix", i)], dxb)
    grads = _collect_grads(g_mix, g_ffn)
    received = [_travel_wait(started[g], dx, per_peer=True, name=f"grad_exchange_wait_{g}") for g in sorted(started)]
    received.append(_exchange(_exchange_operand(grads, EXCHANGE_GROUPS[-1]), name="grad_exchange_last"))
    updated = {}
    for g, pieces in enumerate(EXCHANGE_GROUPS):
        updated.update(_update_group(received[g], pieces, w, m, v, name=f"adamw_sharded_{g}"))

    small_parts = _all_gather(_pack_small(grads), name="small_grad_all_gather")
    outs_small = [_unpack_small(o) for o in
                  _adamw(small_parts, _pack_small(w), _pack_small(m), _pack_small(v), name="adamw_replicated")]

    total_loss = lax.psum(loss[0, 0], ("x", "y", "c"))
    result = [total_loss, dx[None]]
    for k in range(4):
        for n in WEIGHT_ORDER:
            if n not in SHARDED:
                result.append(outs_small[k][n])
            elif (n, None) in updated:
                result.append(updated[(n, None)][k])
            else:
                result.append(jnp.concatenate([updated[(n, l)][k] for l in range(SHARDED[n][0][0])], axis=0))
    return tuple(result)
```

```python
import functools
import math

import jax
import jax.numpy as jnp
from jax import lax
from jax.experimental import pallas as pl
from jax.experimental.pallas import tpu as pltpu

F32 = jnp.float32
BF16 = jnp.bfloat16
MM_DTYPE = BF16

N_DEV = 8
D_MODEL = 1024
DEPTH = 4
RMS_EPS = 1e-6
L2_EPS = 1e-6

LANES = 128

GDN_HEADS = 8
GDN_DK = 128
GDN_DV = 128
GDN_CONV = 4
GDN_CHUNK = 128
GDN_QKV = 3 * GDN_HEADS * GDN_DK
GDN_MAIN = GDN_QKV + GDN_HEADS * GDN_DV
GDN_IN_WIDTH = GDN_MAIN + 2 * GDN_HEADS

DIL_GROUPS = ((128, 1), (512, 4), (2048, 16))
DIL_HEADS = 8
DIL_DH = 64
DIL_SPAN = 128
DIL_SLAB = 3 * DIL_HEADS * LANES
ALIBI_MAX_BIAS = 8.0

FFN_HIDDEN = 2816

ADAM_LR = 0.001
ADAM_B1 = 0.9
ADAM_B2 = 0.999
ADAM_EPS = 1e-08
ADAM_WD = 0.01
ADAM_STEP = 10

VMEM_LIMIT = 56 * 1024 * 1024
NEG = -1e30
HI = lax.Precision.HIGHEST


def _cparams(sem):
    return pltpu.CompilerParams(dimension_semantics=sem, vmem_limit_bytes=VMEM_LIMIT)


def _dot(a, b):
    return lax.dot_general(a, b, (((1,), (0,)), ((), ())), preferred_element_type=F32, precision=HI)


def _dot_nt(a, b):
    return lax.dot_general(a, b, (((1,), (1,)), ((), ())), preferred_element_type=F32, precision=HI)


def _dot_tn(a, b):
    return lax.dot_general(a, b, (((0,), (0,)), ((), ())), preferred_element_type=F32, precision=HI)


def _single_pass(a, b, a_dim, b_dim):
    lead = a.ndim - 2
    batch = ((0,), (0,)) if lead else ((), ())
    return lax.dot_general(a.astype(BF16), b.astype(BF16), (((lead + a_dim,), (lead + b_dim,)), batch),
                           preferred_element_type=F32)


def _bdot(a, b):
    return _single_pass(a, b, 1, 0)


def _bdot_nt(a, b):
    return _single_pass(a, b, 1, 1)


def _bdot_tn(a, b):
    return _single_pass(a, b, 0, 0)


def _pick(n, candidates):
    for c in candidates:
        if n % c == 0:
            return c
    raise ValueError(f"no tile for {n}")


HALF = LANES // 2


def _pack_head_pairs(x):
    x = x.astype(F32)
    tiles = [x[:, (2 * i) * LANES:(2 * i + 1) * LANES] + pltpu.roll(x[:, (2 * i + 1) * LANES:(2 * i + 2) * LANES], HALF, 1)
             for i in range(x.shape[1] // (2 * LANES))]
    return tiles[0] if len(tiles) == 1 else jnp.concatenate(tiles, axis=1)


def _spread_head_pairs(y):
    low = lax.broadcasted_iota(jnp.int32, (y.shape[0], LANES), 1) < HALF
    tiles = []
    for i in range(y.shape[1] // LANES):
        pair = y[:, i * LANES:(i + 1) * LANES]
        tiles += [jnp.where(low, pair, 0.0), jnp.where(low, pltpu.roll(pair, HALF, 1), 0.0)]
    return jnp.concatenate(tiles, axis=1)


def _matmul(a, b, *, name, trans_a=False, trans_b=False, b_rows=None, a_lead=None, add=None, out_dtype=F32,
            packed_a=False, spread_out=False):
    if trans_a:
        k_dim, m_dim = a.shape[-2:]
        m_dim = m_dim // 2 if packed_a else m_dim
    else:
        m_dim, k_dim = a.shape[-2:]
        k_dim = k_dim // 2 if packed_a else k_dim
    slab_m, slab_k = m_dim, k_dim
    if a_lead == "k":
        assert not trans_a
        k_dim *= a.shape[0]
    elif a_lead == "i":
        assert trans_a
        m_dim *= a.shape[0]
    b_start, b_size = b_rows if b_rows is not None else (0, b.shape[0])
    if trans_b:
        n_dim, k2 = b_size, b.shape[1]
    else:
        k2, n_dim = b_size, b.shape[1]
    assert k_dim == k2, (a.shape, b.shape, b_rows)
    tn = _pick(n_dim, (1024, 512, 256, 128))
    tm = min(slab_m, 2048, max(512, (1024 * 1024) // tn))
    tm = _pick(slab_m, (tm, 1408, 1024, 512, 256, 128))
    tk = _pick(slab_k, (1024, 1408, 512, 256, 128))
    nk = k_dim // tk
    has_add = add is not None
    dn = (((0 if trans_a else 1,), (1 if trans_b else 0,)), ((), ()))
    b_tile = tn if trans_b else tk
    assert b_start % b_tile == 0, (b_rows, b_tile)
    b_off = b_start // b_tile

    def body(*refs):
        if has_add:
            a_ref, b_ref, add_ref, o_ref, acc_ref = refs
        else:
            a_ref, b_ref, o_ref, acc_ref = refs
        a_blk = _pack_head_pairs(a_ref[...]).astype(a_ref.dtype) if packed_a else a_ref[...]
        part = lax.dot_general(a_blk, b_ref[...], dn, preferred_element_type=F32)

        def finish(total):
            if has_add:
                total = total + add_ref[...]
            if spread_out:
                total = _spread_head_pairs(total)
            o_ref[...] = total.astype(out_dtype)

        if nk == 1:
            finish(part)
        else:
            k = pl.program_id(2)

            @pl.when(k == 0)
            def _():
                acc_ref[...] = part

            @pl.when(k > 0)
            def _():
                acc_ref[...] += part

            @pl.when(k == nk - 1)
            def _():
                finish(acc_ref[...])

    wide = 2 if packed_a else 1
    a_tile = (tk, wide * tm) if trans_a else (tm, wide * tk)
    a_at = (lambda i, j, k: (k, i)) if trans_a else (lambda i, j, k: (i, k))
    if a_lead is None:
        a_spec = pl.BlockSpec(a_tile, a_at)
    elif a_lead == "k":
        per = slab_k // tk
        a_spec = pl.BlockSpec((None,) + a_tile, lambda i, j, k: (k // per, i, k % per))
    elif a_lead == "i":
        per = slab_m // tm
        a_spec = pl.BlockSpec((None,) + a_tile, lambda i, j, k: (i // per, k, i % per))
    else:
        a_spec = pl.BlockSpec((None,) + a_tile, lambda i, j, k: (a_lead,) + a_at(i, j, k))
    if trans_b:
        b_spec = pl.BlockSpec((tn, tk), lambda i, j, k: (j + b_off, k))
    else:
        b_spec = pl.BlockSpec((tk, tn), lambda i, j, k: (k + b_off, j))
    in_specs = [a_spec, b_spec]
    args = [a, b]
    if has_add:
        in_specs.append(pl.BlockSpec((tm, tn), lambda i, j, k: (i, j)))
        args.append(add)
    return pl.pallas_call(
        body,
        grid=(m_dim // tm, n_dim // tn, nk),
        in_specs=in_specs,
        out_specs=pl.BlockSpec((tm, (2 if spread_out else 1) * tn), lambda i, j, k: (i, j)),
        out_shape=jax.ShapeDtypeStruct((m_dim, (2 if spread_out else 1) * n_dim), out_dtype),
        scratch_shapes=[pltpu.VMEM((tm, tn) if nk > 1 else (8, LANES), F32)],
        compiler_params=_cparams(("parallel", "parallel", "arbitrary")),
        name=name,
    )(*args)


def _rmsnorm_fwd(x, w_row, *, name):
    t, d = x.shape
    tb = min(t, 1024)

    def body(x_ref, w_ref, o_ref):
        xf = x_ref[...]
        r = lax.rsqrt(jnp.mean(xf * xf, axis=-1, keepdims=True) + RMS_EPS)
        o_ref[...] = (xf * r * w_ref[...]).astype(o_ref.dtype)

    return pl.pallas_call(
        body,
        grid=(t // tb,),
        in_specs=[pl.BlockSpec((tb, d), lambda i: (i, 0)), pl.BlockSpec((1, d), lambda i: (0, 0))],
        out_specs=pl.BlockSpec((tb, d), lambda i: (i, 0)),
        out_shape=jax.ShapeDtypeStruct((t, d), MM_DTYPE),
        compiler_params=_cparams(("parallel",)),
        name=name,
    )(x, w_row)


def _rmsnorm_bwd(x, w_row, dy, dskip, *, name):
    t, d = x.shape
    tb = min(t, 512)

    def body(x_ref, w_ref, dy_ref, ds_ref, dx_ref, dxb_ref, dw_ref):
        xf = x_ref[...]
        g = dy_ref[...]
        r = lax.rsqrt(jnp.mean(xf * xf, axis=-1, keepdims=True) + RMS_EPS)
        gw = g * w_ref[...]
        proj = jnp.mean(gw * xf, axis=-1, keepdims=True)
        dx = r * gw - xf * (r * r * r * proj) + ds_ref[...]
        dx_ref[...] = dx
        dxb_ref[...] = dx.astype(dxb_ref.dtype)
        part = jnp.sum(g * xf * r, axis=0, keepdims=True)

        @pl.when(pl.program_id(0) == 0)
        def _():
            dw_ref[...] = part

        @pl.when(pl.program_id(0) > 0)
        def _():
            dw_ref[...] += part

    row = pl.BlockSpec((tb, d), lambda i: (i, 0))
    one = pl.BlockSpec((1, d), lambda i: (0, 0))
    return pl.pallas_call(
        body,
        grid=(t // tb,),
        in_specs=[row, one, row, row],
        out_specs=[row, row, one],
        out_shape=[jax.ShapeDtypeStruct((t, d), F32), jax.ShapeDtypeStruct((t, d), MM_DTYPE),
                   jax.ShapeDtypeStruct((1, d), F32)],
        compiler_params=_cparams(("arbitrary",)),
        name=name,
    )(x, w_row, dy, dskip)


def _silu(z):
    return z / (1.0 + jnp.exp(-z))


FFN_TM, FFN_TN = 512, 1408


def _ffn_in(hn, in_t, *, name):
    t, d = hn.shape
    h = FFN_HIDDEN
    tm, tn = min(t, FFN_TM), FFN_TN
    nj = h // tn
    dn = (((1,), (1,)), ((), ()))

    def body(a_ref, bg_ref, bu_ref, g_ref, u_ref, act_ref):
        a = a_ref[...]
        g = lax.dot_general(a, bg_ref[...], dn, preferred_element_type=F32)
        u = lax.dot_general(a, bu_ref[...], dn, preferred_element_type=F32)
        g_ref[...] = g.astype(g_ref.dtype)
        u_ref[...] = u.astype(u_ref.dtype)
        act_ref[...] = (_silu(g) * u).astype(act_ref.dtype)

    out = pl.BlockSpec((tm, tn), lambda j, i: (i, j))
    return pl.pallas_call(
        body,
        grid=(nj, t // tm),
        in_specs=[pl.BlockSpec((tm, d), lambda j, i: (i, 0)), pl.BlockSpec((tn, d), lambda j, i: (j, 0)),
                  pl.BlockSpec((tn, d), lambda j, i: (j + nj, 0))],
        out_specs=[out, out, out],
        out_shape=[jax.ShapeDtypeStruct((t, h), MM_DTYPE)] * 3,
        compiler_params=_cparams(("parallel", "parallel")),
        name=name,
    )(hn, in_t, in_t)


def _ffn_dact(dy, out_w, g, u, *, name):
    t, d = dy.shape
    h = FFN_HIDDEN
    tm, tn = min(t, FFN_TM), FFN_TN

    def body(a_ref, b_ref, g_ref, u_ref, d_ref):
        da = lax.dot_general(a_ref[...], b_ref[...], (((1,), (1,)), ((), ())), preferred_element_type=F32)
        gate = g_ref[...].astype(F32)
        sig = 1.0 / (1.0 + jnp.exp(-gate))
        sg = gate * sig
        d_ref[0] = (da * u_ref[...].astype(F32) * (sig + sg * (1.0 - sig))).astype(d_ref.dtype)
        d_ref[1] = (da * sg).astype(d_ref.dtype)

    blk = pl.BlockSpec((tm, tn), lambda j, i: (i, j))
    return pl.pallas_call(
        body,
        grid=(h // tn, t // tm),
        in_specs=[pl.BlockSpec((tm, d), lambda j, i: (i, 0)), pl.BlockSpec((tn, d), lambda j, i: (j, 0)), blk, blk],
        out_specs=pl.BlockSpec((2, tm, tn), lambda j, i: (0, i, j)),
        out_shape=jax.ShapeDtypeStruct((2, t, h), MM_DTYPE),
        compiler_params=_cparams(("parallel", "parallel")),
        name=name,
    )(dy, out_w, g, u)


def _loss_head(y, target, *, name):
    t, d = y.shape
    tb = min(t, 1024)

    def body(y_ref, t_ref, dy_ref, dyb_ref, l_ref):
        err = y_ref[...] - t_ref[...]
        dy_ref[...] = err * (1.0 / d)
        dyb_ref[...] = (err * (1.0 / d)).astype(dyb_ref.dtype)
        part = jnp.sum(jnp.sum(err * err, axis=0, keepdims=True), axis=1, keepdims=True) * (0.5 / d)
        part = jnp.broadcast_to(part, l_ref.shape)

        @pl.when(pl.program_id(0) == 0)
        def _():
            l_ref[...] = part

        @pl.when(pl.program_id(0) > 0)
        def _():
            l_ref[...] += part

    row = pl.BlockSpec((tb, d), lambda i: (i, 0))
    return pl.pallas_call(
        body,
        grid=(t // tb,),
        in_specs=[row, row],
        out_specs=[row, row, pl.BlockSpec((8, LANES), lambda i: (0, 0))],
        out_shape=[jax.ShapeDtypeStruct((t, d), F32), jax.ShapeDtypeStruct((t, d), MM_DTYPE),
                   jax.ShapeDtypeStruct((8, LANES), F32)],
        compiler_params=_cparams(("arbitrary",)),
        name=name,
    )(y, target)


CONV_HALO = 8
CONV_TIME_TILE = 2048


def _conv_tile_scale(c):
    is_qk = c < 2 * GDN_HEADS
    scale = jnp.where(c < GDN_HEADS, GDN_DK ** -0.5, 1.0).astype(F32)
    return is_qk, scale


def _gdn_conv_fwd(pm, conv_w, *, name):
    t = pm.shape[0]
    tb = min(t, CONV_TIME_TILE)
    nt = t // tb
    hb = tb // CONV_HALO

    def body(x_ref, xp_ref, w_ref, o_ref, xe_ref):
        c = pl.program_id(0)
        ti = pl.program_id(1)
        xe_ref[0:CONV_HALO, :] = jnp.where(ti > 0, xp_ref[...], 0.0)
        xe_ref[CONV_HALO:CONV_HALO + tb, :] = x_ref[...]
        w = w_ref[...]
        y = jnp.zeros((tb, LANES), F32)
        for j in range(GDN_CONV):
            off = CONV_HALO - (GDN_CONV - 1) + j
            y = y + w[j:j + 1, :] * xe_ref[pl.ds(off, tb), :]
        s = _silu(y)
        is_qk, scale = _conv_tile_scale(c)
        r = lax.rsqrt(jnp.sum(s * s, axis=-1, keepdims=True) + L2_EPS) * scale
        o_ref[...] = s * jnp.where(is_qk, r, 1.0)

    return pl.pallas_call(
        body,
        grid=(GDN_QKV // LANES, nt),
        in_specs=[
            pl.BlockSpec((tb, LANES), lambda c, i: (i, c)),
            pl.BlockSpec((CONV_HALO, LANES), lambda c, i: (jnp.maximum(i * hb - 1, 0), c)),
            pl.BlockSpec((GDN_CONV, LANES), lambda c, i: (0, c)),
        ],
        out_specs=pl.BlockSpec((tb, LANES), lambda c, i: (i, c)),
        out_shape=jax.ShapeDtypeStruct((t, GDN_QKV), F32),
        scratch_shapes=[pltpu.VMEM((tb + CONV_HALO, LANES), F32)],
        compiler_params=_cparams(("parallel", "parallel")),
        name=name,
    )(pm, pm, conv_w)


def _gdn_conv_bwd(pm, conv_w, dout, *, name):
    t = pm.shape[0]
    tb = min(t, CONV_TIME_TILE)
    nt = t // tb
    hb = tb // CONV_HALO
    last_hb = t // CONV_HALO - 1
    ext = tb + CONV_HALO

    def body(x_ref, xp_ref, xn_ref, d_ref, dn_ref, w_ref, dx_ref, dw_ref, xe_ref, dy_ref):
        c = pl.program_id(0)
        ti = pl.program_id(1)
        has_next = ti < nt - 1
        xe_ref[0:CONV_HALO, :] = jnp.where(ti > 0, xp_ref[...], 0.0)
        xe_ref[CONV_HALO:CONV_HALO + tb, :] = x_ref[...]
        xe_ref[CONV_HALO + tb:2 * CONV_HALO + tb, :] = jnp.where(has_next, xn_ref[...], 0.0)
        de = jnp.concatenate([d_ref[...], jnp.where(has_next, dn_ref[...], 0.0)], axis=0)
        w = w_ref[...]
        y = jnp.zeros((ext, LANES), F32)
        for j in range(GDN_CONV):
            off = CONV_HALO - (GDN_CONV - 1) + j
            y = y + w[j:j + 1, :] * xe_ref[pl.ds(off, ext), :]
        sig = 1.0 / (1.0 + jnp.exp(-y))
        s = y * sig
        is_qk, scale = _conv_tile_scale(c)
        r = lax.rsqrt(jnp.sum(s * s, axis=-1, keepdims=True) + L2_EPS)
        n = s * r
        dnrm = de * scale
        ds_qk = r * (dnrm - n * jnp.sum(dnrm * n, axis=-1, keepdims=True))
        ds = jnp.where(is_qk, ds_qk, de)
        dy_ref[...] = ds * (sig + s * (1.0 - sig))
        dy = dy_ref[0:tb, :]
        dx = jnp.zeros((tb, LANES), F32)
        dw_rows = []
        for j in range(GDN_CONV):
            sh = GDN_CONV - 1 - j
            dx = dx + w[j:j + 1, :] * dy_ref[pl.ds(sh, tb), :]
            off = CONV_HALO - (GDN_CONV - 1) + j
            dw_rows.append(jnp.sum(dy * xe_ref[pl.ds(off, tb), :], axis=0, keepdims=True))
        dx_ref[...] = dx.astype(dx_ref.dtype)
        part = jnp.concatenate(dw_rows, axis=0)

        @pl.when(ti == 0)
        def _():
            dw_ref[...] = part

        @pl.when(ti > 0)
        def _():
            dw_ref[...] += part

    main = pl.BlockSpec((tb, LANES), lambda c, i: (i, c))
    prev = pl.BlockSpec((CONV_HALO, LANES), lambda c, i: (jnp.maximum(i * hb - 1, 0), c))
    nxt = pl.BlockSpec((CONV_HALO, LANES), lambda c, i: (jnp.minimum((i + 1) * hb, last_hb), c))
    return pl.pallas_call(
        body,
        grid=(GDN_QKV // LANES, nt),
        in_specs=[main, prev, nxt, main, nxt, pl.BlockSpec((GDN_CONV, LANES), lambda c, i: (0, c))],
        out_specs=[main, pl.BlockSpec((GDN_CONV, LANES), lambda c, i: (0, c))],
        out_shape=[jax.ShapeDtypeStruct((t, GDN_QKV), MM_DTYPE), jax.ShapeDtypeStruct((GDN_CONV, GDN_QKV), F32)],
        scratch_shapes=[pltpu.VMEM((tb + 2 * CONV_HALO, LANES), F32), pltpu.VMEM((ext, LANES), F32)],
        compiler_params=_cparams(("parallel", "arbitrary")),
        name=name,
    )(pm, pm, pm, dout, dout, conv_w)


def _head_selector(first_col):
    row = lax.broadcasted_iota(jnp.int32, (LANES, GDN_HEADS * LANES), 0)
    col = lax.broadcasted_iota(jnp.int32, (LANES, GDN_HEADS * LANES), 1)
    return (col // LANES + first_col == row).astype(F32)


def _softplus(x):
    return jnp.maximum(x, 0.0) + jnp.log(1.0 + jnp.exp(-jnp.abs(x)))


def _gdn_gates_fwd(ab, alog_row, dt_row, *, name):
    t = ab.shape[0]
    tb = min(t, 1024)
    wide = GDN_HEADS * LANES

    def body(ab_ref, al_ref, dt_ref, g_ref, b_ref):
        x = ab_ref[...]
        g_cols = -jnp.exp(al_ref[...]) * _softplus(x + dt_ref[...])
        b_cols = 1.0 / (1.0 + jnp.exp(-x))
        g_ref[...] = _dot(g_cols, _head_selector(0))
        b_ref[...] = _dot(b_cols, _head_selector(GDN_HEADS))

    row = pl.BlockSpec((tb, LANES), lambda i: (i, 0))
    one = pl.BlockSpec((1, LANES), lambda i: (0, 0))
    out = pl.BlockSpec((tb, wide), lambda i: (i, 0))
    return pl.pallas_call(
        body,
        grid=(t // tb,),
        in_specs=[row, one, one],
        out_specs=[out, out],
        out_shape=[jax.ShapeDtypeStruct((t, wide), F32)] * 2,
        compiler_params=_cparams(("parallel",)),
        name=name,
    )(ab, alog_row, dt_row)


def _gdn_gates_bwd(ab, alog_row, dt_row, dgb, dbb, *, name):
    t = ab.shape[0]
    tb = min(t, 1024)
    wide = GDN_HEADS * LANES

    def body(ab_ref, al_ref, dt_ref, dg_ref, db_ref, dab_ref, dal_ref, ddt_ref):
        x = ab_ref[...]
        lane = lax.broadcasted_iota(jnp.int32, (tb, LANES), 1)
        dg_cols = _dot_nt(dg_ref[...], _head_selector(0))
        db_cols = _dot_nt(db_ref[...], _head_selector(GDN_HEADS))
        ea = jnp.exp(al_ref[...])
        z = x + dt_ref[...]
        sp = _softplus(z)
        sg = 1.0 / (1.0 + jnp.exp(-z))
        beta = 1.0 / (1.0 + jnp.exp(-x))
        da = jnp.where(lane < GDN_HEADS, dg_cols * (-ea) * sg, 0.0)
        db = jnp.where((lane >= GDN_HEADS) & (lane < 2 * GDN_HEADS), db_cols * beta * (1.0 - beta), 0.0)
        dab_ref[...] = (da + db).astype(dab_ref.dtype)
        p_al = jnp.sum(jnp.where(lane < GDN_HEADS, dg_cols * (-ea) * sp, 0.0), axis=0, keepdims=True)
        p_dt = jnp.sum(da, axis=0, keepdims=True)

        @pl.when(pl.program_id(0) == 0)
        def _():
            dal_ref[...] = p_al
            ddt_ref[...] = p_dt

        @pl.when(pl.program_id(0) > 0)
        def _():
            dal_ref[...] += p_al
            ddt_ref[...] += p_dt

    row = pl.BlockSpec((tb, LANES), lambda i: (i, 0))
    one = pl.BlockSpec((1, LANES), lambda i: (0, 0))
    big = pl.BlockSpec((tb, wide), lambda i: (i, 0))
    return pl.pallas_call(
        body,
        grid=(t // tb,),
        in_specs=[row, one, one, big, big],
        out_specs=[row, one, one],
        out_shape=[jax.ShapeDtypeStruct((t, LANES), MM_DTYPE), jax.ShapeDtypeStruct((1, LANES), F32),
                   jax.ShapeDtypeStruct((1, LANES), F32)],
        compiler_params=_cparams(("arbitrary",)),
        name=name,
    )(ab, alog_row, dt_row, dgb, dbb)


@jax.custom_vjp
def _unit_lower_inverse_rest(n):
    c = n.shape[-1]
    ri = lax.broadcasted_iota(jnp.int32, (c, c), 0)
    ci = lax.broadcasted_iota(jnp.int32, (c, c), 1)
    rest = None
    size = 1
    while size < c:
        joins = ((ri // (2 * size)) == (ci // (2 * size))) & ((ri // size) != (ci // size))
        low = jnp.where(joins, n, 0.0)
        if rest is None:
            rest = -low
        else:
            left = low + _bdot(rest, low)
            rest = rest - (left + _bdot(left, rest))
        size *= 2
    return rest


def _unit_lower_inverse_rest_fwd(n):
    rest = _unit_lower_inverse_rest(n)
    return rest, rest


def _unit_lower_inverse_rest_bwd(rest, ct):
    left = ct + _bdot_tn(rest, ct)
    return (-(left + _bdot_nt(left, rest)),)


_unit_lower_inverse_rest.defvjp(_unit_lower_inverse_rest_fwd, _unit_lower_inverse_rest_bwd)


@jax.custom_vjp
def _known_inverse_rest(n, rest):
    return rest


def _known_inverse_rest_fwd(n, rest):
    return rest, rest


def _known_inverse_rest_bwd(rest, ct):
    return _unit_lower_inverse_rest_bwd(rest, ct) + (jnp.zeros_like(rest),)


_known_inverse_rest.defvjp(_known_inverse_rest_fwd, _known_inverse_rest_bwd)


def _bf16_pieces(x):
    hi = x.astype(BF16)
    r1 = x - hi.astype(F32)
    mid = r1.astype(BF16)
    lo = (r1 - mid.astype(F32)).astype(BF16)
    return hi, mid, lo


def _lower_ones(shape):
    c = shape[-1]
    ri = lax.broadcasted_iota(jnp.int32, (c, c), 0)
    ci = lax.broadcasted_iota(jnp.int32, (c, c), 1)
    return jnp.broadcast_to((ri >= ci).astype(BF16), shape)


@jax.custom_vjp
def _running_sum(x):
    tri = _lower_ones(x.shape)
    return sum(_bdot(tri, p) for p in _bf16_pieces(x))


def _running_sum_fwd(x):
    return _running_sum(x), None


def _running_sum_bwd(_, ct):
    tri = _lower_ones(ct.shape)
    return (sum(_bdot_tn(tri, p) for p in _bf16_pieces(ct)),)


_running_sum.defvjp(_running_sum_fwd, _running_sum_bwd)


def _gdn_prep_math(q, k, v, gb, bb, known_rest=None, with_rest=False):
    c = GDN_CHUNK
    ri = lax.broadcasted_iota(jnp.int32, (c, c), 0)
    ci = lax.broadcasted_iota(jnp.int32, (c, c), 1)
    causal = ri >= ci
    gc = _running_sum(gb)
    decay = jnp.exp(jnp.where(causal, gc - jnp.swapaxes(gc, -1, -2), NEG))
    n = jnp.where(ri > ci, _bdot_nt(k, k) * bb * decay, 0.0)
    rest = _unit_lower_inverse_rest(n) if known_rest is None else _known_inverse_rest(n, known_rest)
    eg = jnp.exp(gc)
    rhs_v = v * bb
    rhs_k = k * bb * eg
    u = rhs_v + _bdot(rest, rhs_v)
    w = rhs_k + _bdot(rest, rhs_k)
    qk = _bdot_nt(q, k) * decay
    qd = q * eg
    last = jnp.sum(jnp.where(ri == c - 1, gc, 0.0), axis=-2, keepdims=True)
    gl = jnp.broadcast_to(last, gc.shape)
    kt = k * jnp.exp(gl - gc)
    cd = jnp.exp(gl)
    return (u, w, qk, qd, kt, cd, rest) if with_rest else (u, w, qk, qd, kt, cd)


def _head_tiles(ref, h):
    return ref[:, h * LANES:(h + 1) * LANES]


def _stack_heads(ref, first=0, heads=GDN_HEADS):
    return jnp.stack([_head_tiles(ref, first + h) for h in range(heads)])


def _store_heads(ref, val, first=0):
    for h in range(val.shape[0]):
        ref[:, (first + h) * LANES:(first + h + 1) * LANES] = val[h].astype(ref.dtype)


def _gdn_prep_fwd(qkv, gb, bb, *, name):
    t = qkv.shape[0]
    c = GDN_CHUNK
    wide = GDN_HEADS * LANES

    def body(q_ref, k_ref, v_ref, g_ref, b_ref, *outs):
        res = _gdn_prep_math(*(_stack_heads(r) for r in (q_ref, k_ref, v_ref, g_ref, b_ref)), with_rest=True)
        for o_ref, val in zip(outs, res):
            _store_heads(o_ref, val)

    blk = lambda off: pl.BlockSpec((c, wide), lambda i: (i, off))
    outs = pl.pallas_call(
        body,
        grid=(t // c,),
        in_specs=[blk(0), blk(1), blk(2), blk(0), blk(0)],
        out_specs=[blk(0)] * 7,
        out_shape=[jax.ShapeDtypeStruct((t, wide), dt) for dt in (F32, MM_DTYPE, MM_DTYPE, MM_DTYPE, MM_DTYPE, F32, F32)],
        compiler_params=_cparams(("parallel",)),
        name=name,
    )(qkv, qkv, qkv, gb, bb)
    return tuple(outs[:6]), outs[6]


def _gdn_prep_bwd(qkv, gb, bb, rest, cts, *, name):
    t = qkv.shape[0]
    c = GDN_CHUNK
    wide = GDN_HEADS * LANES

    def body(q_ref, k_ref, v_ref, g_ref, b_ref, r_ref, c0, c1, c2, c3, c4, c5, dqkv_ref, dg_ref, db_ref):
        prim = tuple(_stack_heads(r) for r in (q_ref, k_ref, v_ref, g_ref, b_ref))
        _, pull = jax.vjp(functools.partial(_gdn_prep_math, known_rest=_stack_heads(r_ref)), *prim)
        dq, dk, dv, dg, db = pull(tuple(_stack_heads(r) for r in (c0, c1, c2, c3, c4, c5)))
        _store_heads(dqkv_ref, dq)
        _store_heads(dqkv_ref, dk, first=GDN_HEADS)
        _store_heads(dqkv_ref, dv, first=2 * GDN_HEADS)
        _store_heads(dg_ref, dg)
        _store_heads(db_ref, db)

    blk = lambda off: pl.BlockSpec((c, wide), lambda i: (i, off))
    return pl.pallas_call(
        body,
        grid=(t // c,),
        in_specs=[blk(0), blk(1), blk(2), blk(0), blk(0)] + [blk(0)] * 7,
        out_specs=[pl.BlockSpec((c, 3 * wide), lambda i: (i, 0)), blk(0), blk(0)],
        out_shape=[jax.ShapeDtypeStruct((t, 3 * wide), F32), jax.ShapeDtypeStruct((t, wide), F32),
                   jax.ShapeDtypeStruct((t, wide), F32)],
        compiler_params=_cparams(("parallel",)),
        name=name,
    )(qkv, qkv, qkv, gb, bb, rest, *cts)


def _gdn_scan_math(s, u, w, qk, qd, kt, cd):
    v_new = u - _bdot(w, s)
    o = _bdot(qd, s) + _bdot(qk, v_new)
    s_new = s * cd + _bdot_tn(kt, v_new)
    return o, s_new


def _gdn_scan_fwd(prep, *, name):
    t = prep[0].shape[0]
    c = GDN_CHUNK
    wide = GDN_HEADS * LANES

    def body(u_ref, w_ref, qk_ref, qd_ref, kt_ref, cd_ref, o_ref, st_ref, s_ref):
        @pl.when(pl.program_id(0) == 0)
        def _():
            s_ref[...] = jnp.zeros_like(s_ref)

        s = _stack_heads(s_ref)
        _store_heads(st_ref, s)
        o, s_new = _gdn_scan_math(s, *(_stack_heads(r).astype(F32) for r in (u_ref, w_ref, qk_ref, qd_ref, kt_ref, cd_ref)))
        _store_heads(o_ref, o)
        _store_heads(s_ref, s_new)

    blk = pl.BlockSpec((c, wide), lambda i: (i, 0))
    return pl.pallas_call(
        body,
        grid=(t // c,),
        in_specs=[blk] * 6,
        out_specs=[blk, blk],
        out_shape=[jax.ShapeDtypeStruct((t, wide), F32)] * 2,
        scratch_shapes=[pltpu.VMEM((GDN_DK, wide), F32)],
        compiler_params=_cparams(("arbitrary",)),
        name=name,
    )(*prep)


def _gdn_scan_bwd(prep, states, do, *, name):
    t = do.shape[0]
    c = GDN_CHUNK
    wide = GDN_HEADS * LANES
    nc = t // c

    def body(u_ref, w_ref, qk_ref, qd_ref, kt_ref, cd_ref, st_ref, do_ref, *rest):
        outs, ds_ref = rest[:6], rest[6]

        @pl.when(pl.program_id(0) == 0)
        def _():
            ds_ref[...] = jnp.zeros_like(ds_ref)

        prim = tuple(_stack_heads(r).astype(F32) for r in (st_ref, u_ref, w_ref, qk_ref, qd_ref, kt_ref, cd_ref))
        _, pull = jax.vjp(_gdn_scan_math, *prim)
        grads = pull((_stack_heads(do_ref), _stack_heads(ds_ref)))
        _store_heads(ds_ref, grads[0])
        for o_ref, val in zip(outs, grads[1:]):
            _store_heads(o_ref, val)

    blk = pl.BlockSpec((c, wide), lambda i: (nc - 1 - i, 0))
    return pl.pallas_call(
        body,
        grid=(nc,),
        in_specs=[blk] * 8,
        out_specs=[blk] * 6,
        out_shape=[jax.ShapeDtypeStruct((t, wide), F32)] * 6,
        scratch_shapes=[pltpu.VMEM((GDN_DK, wide), F32)],
        compiler_params=_cparams(("arbitrary",)),
        name=name,
    )(*prep, states, do)


def _gdn_outgate_math(o, z, nw):
    r = lax.rsqrt(jnp.mean(o * o, axis=-1, keepdims=True) + RMS_EPS)
    return o * r * nw * _silu(z)


def _gdn_outgate_fwd(o, pm, nw_row, *, name):
    t = o.shape[0]
    tb = min(t, 1024)
    z_off = GDN_QKV // LANES

    def body(o_ref, z_ref, nw_ref, y_ref):
        y_ref[...] = _gdn_outgate_math(o_ref[...], z_ref[...], nw_ref[...]).astype(y_ref.dtype)

    return pl.pallas_call(
        body,
        grid=(t // tb, GDN_HEADS),
        in_specs=[pl.BlockSpec((tb, LANES), lambda i, h: (i, h)), pl.BlockSpec((tb, LANES), lambda i, h: (i, h + z_off)),
                  pl.BlockSpec((1, LANES), lambda i, h: (0, 0))],
        out_specs=pl.BlockSpec((tb, LANES), lambda i, h: (i, h)),
        out_shape=jax.ShapeDtypeStruct((t, GDN_HEADS * LANES), MM_DTYPE),
        compiler_params=_cparams(("parallel", "parallel")),
        name=name,
    )(o, pm, nw_row)


def _gdn_outgate_bwd(o, pm, nw_row, dy, *, name):
    t = o.shape[0]
    tb = min(t, 1024)
    z_off = GDN_QKV // LANES

    def body(o_ref, z_ref, nw_ref, dy_ref, do_ref, dz_ref, dnw_ref):
        _, pull = jax.vjp(_gdn_outgate_math, o_ref[...], z_ref[...], nw_ref[...])
        d_o, d_z, d_nw = pull(dy_ref[...])
        do_ref[...] = d_o
        dz_ref[...] = d_z.astype(dz_ref.dtype)
        first = (pl.program_id(0) == 0) & (pl.program_id(1) == 0)

        @pl.when(first)
        def _():
            dnw_ref[...] = d_nw

        @pl.when(jnp.logical_not(first))
        def _():
            dnw_ref[...] += d_nw

    blk = pl.BlockSpec((tb, LANES), lambda i, h: (i, h))
    one = pl.BlockSpec((1, LANES), lambda i, h: (0, 0))
    return pl.pallas_call(
        body,
        grid=(t // tb, GDN_HEADS),
        in_specs=[blk, pl.BlockSpec((tb, LANES), lambda i, h: (i, h + z_off)), one, blk],
        out_specs=[blk, blk, one],
        out_shape=[jax.ShapeDtypeStruct((t, GDN_HEADS * LANES), F32),
                   jax.ShapeDtypeStruct((t, GDN_HEADS * LANES), MM_DTYPE), jax.ShapeDtypeStruct((1, LANES), F32)],
        compiler_params=_cparams(("arbitrary", "arbitrary")),
        name=name,
    )(o, pm, nw_row, dy)


def _rms64(x, w_row):
    return x * lax.rsqrt(jnp.sum(x * x, axis=-1, keepdims=True) * (1.0 / DIL_DH) + RMS_EPS) * w_row


def _alibi_slopes(group):
    head = lax.broadcasted_iota(jnp.int32, (DIL_HEADS, 8, LANES), 0).astype(F32)
    rate = -math.log(2.0) * ALIBI_MAX_BIAS / (len(DIL_GROUPS) * DIL_HEADS)
    slope = jnp.exp(rate * (head + float(group * DIL_HEADS + 1)))
    return jnp.broadcast_to(slope[:, 0:1, :], (DIL_HEADS, DIL_SPAN, LANES))


def _band_logits(qn, kp, kc, slope_d, has_prev):
    qi = lax.broadcasted_iota(jnp.int32, (DIL_SPAN, DIL_SPAN), 0)
    kj = lax.broadcasted_iota(jnp.int32, (DIL_SPAN, DIL_SPAN), 1)
    steps_c = (qi - kj).astype(F32)
    scale = DIL_DH ** -0.5
    sp = _bdot_nt(qn, kp) * scale - slope_d * (steps_c + float(DIL_SPAN))
    sc = _bdot_nt(qn, kc) * scale - slope_d * steps_c
    sp = jnp.where((kj >= qi) & has_prev, sp, NEG)
    sc = jnp.where(kj <= qi, sc, NEG)
    return sp, sc


def _dil_attn_fwd(slab, wq_row, wk_row, *, group, name):
    dilation = DIL_GROUPS[group][1]
    t = slab.shape[0]
    rows = t // dilation
    nlb = rows // DIL_SPAN
    wide = DIL_HEADS * LANES
    view = slab.reshape(rows, dilation * DIL_SLAB)

    def body(q_ref, kc_ref, vc_ref, kp_ref, vp_ref, wq_ref, wk_ref, o_ref):
        has_prev = pl.program_id(1) > 0
        lane = lax.broadcasted_iota(jnp.int32, (DIL_SPAN, LANES), 1)
        qn = _rms64(_stack_heads(q_ref), wq_ref[...])
        kc = _rms64(_stack_heads(kc_ref), wk_ref[...])
        kp = _rms64(_stack_heads(kp_ref), wk_ref[...])
        sp, sc = _band_logits(qn, kp, kc, _alibi_slopes(group) * float(dilation), has_prev)
        m = jnp.maximum(jnp.max(sp, axis=-1, keepdims=True), jnp.max(sc, axis=-1, keepdims=True))
        pp = jnp.exp(sp - m)
        pc = jnp.exp(sc - m)
        l = jnp.sum(pp, axis=-1, keepdims=True) + jnp.sum(pc, axis=-1, keepdims=True)
        o = (_bdot(pp, _stack_heads(vp_ref)) + _bdot(pc, _stack_heads(vc_ref))) / l
        _store_heads(o_ref, jnp.where(lane < DIL_DH, o, m + jnp.log(l)))

    cur = lambda part: pl.BlockSpec((DIL_SPAN, wide), lambda r, i: (i, 3 * r + part))
    prv = lambda part: pl.BlockSpec((DIL_SPAN, wide), lambda r, i: (jnp.maximum(i - 1, 0), 3 * r + part))
    one = pl.BlockSpec((1, LANES), lambda r, i: (0, 0))
    out = pl.pallas_call(
        body,
        grid=(dilation, nlb),
        in_specs=[cur(0), cur(1), cur(2), prv(1), prv(2), one, one],
        out_specs=pl.BlockSpec((DIL_SPAN, wide), lambda r, i: (i, r)),
        out_shape=jax.ShapeDtypeStruct((rows, dilation * wide), F32),
        compiler_params=_cparams(("parallel", "parallel")),
        name=name,
    )(view, view, view, view, view, wq_row, wk_row)
    return out.reshape(t, wide)


def _head_slope(group, head):
    idx = jnp.zeros((8, LANES), F32) + head.astype(F32)
    rate = -math.log(2.0) * ALIBI_MAX_BIAS / (len(DIL_GROUPS) * DIL_HEADS)
    slope = jnp.exp(rate * (idx + float(group * DIL_HEADS + 1)))
    return jnp.broadcast_to(slope[0:1, :], (DIL_SPAN, LANES))


def _take_residues(ref, d):
    return jnp.stack([ref[pl.ds(r, DIL_SPAN, stride=d), :] for r in range(d)])


def _put_residues(ref, val, d):
    for r in range(d):
        ref[pl.ds(r, DIL_SPAN, stride=d), :] = val[r]


def _dil_attn_fwd_strided(slab, wq_row, wk_row, *, group, name):
    d = DIL_GROUPS[group][1]
    t = slab.shape[0]
    span = DIL_SPAN * d
    nsb = t // span

    def body(q_ref, kc_ref, vc_ref, kp_ref, vp_ref, wq_ref, wk_ref, o_ref):
        has_prev = pl.program_id(0) > 0
        lane = lax.broadcasted_iota(jnp.int32, (DIL_SPAN, LANES), 1)
        qn = _rms64(_take_residues(q_ref, d), wq_ref[...])
        kc = _rms64(_take_residues(kc_ref, d), wk_ref[...])
        kp = _rms64(_take_residues(kp_ref, d), wk_ref[...])
        sp, sc = _band_logits(qn, kp, kc, _head_slope(group, pl.program_id(1)) * float(d), has_prev)
        m = jnp.maximum(jnp.max(sp, axis=-1, keepdims=True), jnp.max(sc, axis=-1, keepdims=True))
        pp = jnp.exp(sp - m)
        pc = jnp.exp(sc - m)
        l = jnp.sum(pp, axis=-1, keepdims=True) + jnp.sum(pc, axis=-1, keepdims=True)
        o = (_bdot(pp, _take_residues(vp_ref, d)) + _bdot(pc, _take_residues(vc_ref, d))) / l
        _put_residues(o_ref, jnp.where(lane < DIL_DH, o, m + jnp.log(l)), d)

    cur = lambda part: pl.BlockSpec((span, LANES), lambda i, h: (i, part * DIL_HEADS + h))
    prv = lambda part: pl.BlockSpec((span, LANES), lambda i, h: (jnp.maximum(i - 1, 0), part * DIL_HEADS + h))
    one = pl.BlockSpec((1, LANES), lambda i, h: (0, 0))
    return pl.pallas_call(
        body,
        grid=(nsb, DIL_HEADS),
        in_specs=[cur(0), cur(1), cur(2), prv(1), prv(2), one, one],
        out_specs=pl.BlockSpec((span, LANES), lambda i, h: (i, h)),
        out_shape=jax.ShapeDtypeStruct((t, DIL_HEADS * LANES), F32),
        compiler_params=_cparams(("parallel", "parallel")),
        name=name,
    )(slab, slab, slab, slab, slab, wq_row, wk_row)


def _dil_attn_bwd_strided(slab, stat, wq_row, wk_row, dwq_in, dwk_in, *, group, name):
    d = DIL_GROUPS[group][1]
    t = slab.shape[0]
    span = DIL_SPAN * d
    nsb = t // span

    def body(q_ref, kc_ref, vc_ref, kp_ref, vp_ref, st_ref, wq_ref, wk_ref, dwq_in_ref, dwk_in_ref,
             d_ref, dwq_ref, dwk_ref, dk_carry, dv_carry, spread):
        step = pl.program_id(1)
        has_prev = step < nsb - 1
        first = (pl.program_id(0) == 0) & (step == 0)

        @pl.when(step == 0)
        def _():
            dk_carry[...] = jnp.zeros_like(dk_carry)
            dv_carry[...] = jnp.zeros_like(dv_carry)

        @pl.when(first)
        def _():
            dwq_ref[...] = dwq_in_ref[...]
            dwk_ref[...] = dwk_in_ref[...]

        lane = lax.broadcasted_iota(jnp.int32, (DIL_SPAN, LANES), 1)
        scale = DIL_DH ** -0.5
        q_raw = _take_residues(q_ref, d)
        kc_raw = _take_residues(kc_ref, d)
        vc = _take_residues(vc_ref, d)
        kp_raw = _take_residues(kp_ref, d)
        vp = _take_residues(vp_ref, d)
        st = _take_residues(st_ref, d)
        d_o = jnp.where(lane < DIL_DH, st, 0.0)
        lse = jnp.sum(jnp.where(lane == DIL_DH, st, 0.0), axis=-1, keepdims=True)
        delta = jnp.sum(jnp.where(lane == DIL_DH + 1, st, 0.0), axis=-1, keepdims=True)
        qn = _rms64(q_raw, wq_ref[...])
        kc = _rms64(kc_raw, wk_ref[...])
        kp = _rms64(kp_raw, wk_ref[...])
        sp, sc = _band_logits(qn, kp, kc, _head_slope(group, pl.program_id(0)) * float(d), has_prev)
        pp = jnp.exp(sp - lse)
        pc = jnp.exp(sc - lse)
        dsp = pp * (_bdot_nt(d_o, vp) - delta) * scale
        dsc = pc * (_bdot_nt(d_o, vc) - delta) * scale
        dqn = _bdot(dsp, kp) + _bdot(dsc, kc)
        dkc_n = _bdot_tn(dsc, qn) + dk_carry[...]
        dvc = _bdot_tn(pc, d_o) + dv_carry[...]
        dk_carry[...] = _bdot_tn(dsp, qn)
        dv_carry[...] = _bdot_tn(pp, d_o)
        dq_raw, dwq_rows = _rms64_bwd(q_raw, wq_ref[...], dqn)
        dk_raw, dwk_rows = _rms64_bwd(kc_raw, wk_ref[...], dkc_n)
        for part, val in enumerate((dq_raw, dk_raw, dvc)):
            _put_residues(spread, val, d)
            d_ref[part] = spread[...].astype(d_ref.dtype)
        dwq_ref[...] += jnp.sum(jnp.sum(dwq_rows, axis=0), axis=0, keepdims=True)
        dwk_ref[...] += jnp.sum(jnp.sum(dwk_rows, axis=0), axis=0, keepdims=True)

    at = lambda i: nsb - 1 - i
    cur = lambda part: pl.BlockSpec((span, LANES), lambda h, i: (at(i), part * DIL_HEADS + h))
    prv = lambda part: pl.BlockSpec((span, LANES), lambda h, i: (jnp.maximum(at(i) - 1, 0), part * DIL_HEADS + h))
    one = pl.BlockSpec((1, LANES), lambda h, i: (0, 0))
    return pl.pallas_call(
        body,
        grid=(DIL_HEADS, nsb),
        in_specs=[cur(0), cur(1), cur(2), prv(1), prv(2), pl.BlockSpec((span, LANES), lambda h, i: (at(i), h)),
                  one, one, one, one],
        out_specs=[pl.BlockSpec((3, span, LANES), lambda h, i: (0, at(i), h)), one, one],
        out_shape=[jax.ShapeDtypeStruct((3, t, DIL_HEADS * LANES), MM_DTYPE), jax.ShapeDtypeStruct((1, LANES), F32),
                   jax.ShapeDtypeStruct((1, LANES), F32)],
        scratch_shapes=[pltpu.VMEM((d, DIL_SPAN, LANES), F32), pltpu.VMEM((d, DIL_SPAN, LANES), F32),
                        pltpu.VMEM((span, LANES), F32)],
        compiler_params=_cparams(("arbitrary", "arbitrary")),
        name=name,
    )(slab, slab, slab, slab, slab, stat, wq_row, wk_row, dwq_in, dwk_in)


def _dil_merge_fwd(oe, *, name):
    t = oe[0].shape[0]
    tb = min(t, 1024)

    def body(e0, e1, e2, y_ref, om_ref):
        lane = lax.broadcasted_iota(jnp.int32, (tb, LANES), 1)
        es = [e0[...], e1[...], e2[...]]
        lse = [jnp.sum(jnp.where(lane == DIL_DH, e, 0.0), axis=-1, keepdims=True) for e in es]
        top = jnp.maximum(jnp.maximum(lse[0], lse[1]), lse[2])
        joint = top + jnp.log(jnp.exp(lse[0] - top) + jnp.exp(lse[1] - top) + jnp.exp(lse[2] - top))
        o = sum(jnp.exp(l - joint) * e for l, e in zip(lse, es))
        y_ref[...] = jnp.where(lane < DIL_DH, o, 0.0).astype(y_ref.dtype)
        om_ref[...] = jnp.where(lane < DIL_DH, o, joint)

    blk = pl.BlockSpec((tb, LANES), lambda i, h: (i, h))
    return pl.pallas_call(
        body,
        grid=(t // tb, DIL_HEADS),
        in_specs=[blk] * 3,
        out_specs=[blk, blk],
        out_shape=[jax.ShapeDtypeStruct((t, DIL_HEADS * LANES), MM_DTYPE),
                   jax.ShapeDtypeStruct((t, DIL_HEADS * LANES), F32)],
        compiler_params=_cparams(("parallel", "parallel")),
        name=name,
    )(*oe)


def _dil_merge_bwd(dy, om, *, name):
    t = dy.shape[0]
    tb = min(t, 1024)

    def body(dy_ref, om_ref, st_ref):
        lane = lax.broadcasted_iota(jnp.int32, (tb, LANES), 1)
        d_o = jnp.where(lane < DIL_DH, dy_ref[...], 0.0)
        om_t = om_ref[...]
        delta = jnp.sum(d_o * om_t, axis=-1, keepdims=True)
        st_ref[...] = jnp.where(lane < DIL_DH, d_o, jnp.where(lane == DIL_DH, om_t, jnp.where(lane == DIL_DH + 1, delta, 0.0)))

    blk = pl.BlockSpec((tb, LANES), lambda i, h: (i, h))
    return pl.pallas_call(
        body,
        grid=(t // tb, DIL_HEADS),
        in_specs=[blk, blk],
        out_specs=blk,
        out_shape=jax.ShapeDtypeStruct((t, DIL_HEADS * LANES), F32),
        compiler_params=_cparams(("parallel", "parallel")),
        name=name,
    )(dy, om)


def _rms64_bwd(x, w_row, dy):
    r = lax.rsqrt(jnp.sum(x * x, axis=-1, keepdims=True) * (1.0 / DIL_DH) + RMS_EPS)
    gw = dy * w_row
    dx = r * gw - x * (r * r * r * jnp.sum(gw * x, axis=-1, keepdims=True) * (1.0 / DIL_DH))
    return dx, dy * x * r


def _dil_attn_bwd(slab, stat, wq_row, wk_row, dwq_in, dwk_in, *, group, name):
    dilation = DIL_GROUPS[group][1]
    t = slab.shape[0]
    rows = t // dilation
    nlb = rows // DIL_SPAN
    wide = DIL_HEADS * LANES
    view = slab.reshape(rows, dilation * DIL_SLAB)
    stat_view = stat.reshape(rows, dilation * wide)

    def body(cur_ref, kp_ref, vp_ref, st_ref, wq_ref, wk_ref, dwq_in_ref, dwk_in_ref, d_ref, dwq_ref, dwk_ref,
             dk_carry, dv_carry):
        step = pl.program_id(1)
        has_prev = step < nlb - 1
        first = (pl.program_id(0) == 0) & (step == 0)

        @pl.when(step == 0)
        def _():
            dk_carry[...] = jnp.zeros_like(dk_carry)
            dv_carry[...] = jnp.zeros_like(dv_carry)

        @pl.when(first)
        def _():
            dwq_ref[...] = dwq_in_ref[...]
            dwk_ref[...] = dwk_in_ref[...]

        lane = lax.broadcasted_iota(jnp.int32, (DIL_SPAN, LANES), 1)
        scale = DIL_DH ** -0.5
        q_raw = _stack_heads(cur_ref)
        kc_raw = _stack_heads(cur_ref, first=DIL_HEADS)
        vc = _stack_heads(cur_ref, first=2 * DIL_HEADS)
        kp_raw = _stack_heads(kp_ref)
        vp = _stack_heads(vp_ref)
        st = _stack_heads(st_ref)
        d_o = jnp.where(lane < DIL_DH, st, 0.0)
        lse = jnp.sum(jnp.where(lane == DIL_DH, st, 0.0), axis=-1, keepdims=True)
        delta = jnp.sum(jnp.where(lane == DIL_DH + 1, st, 0.0), axis=-1, keepdims=True)
        qn = _rms64(q_raw, wq_ref[...])
        kc = _rms64(kc_raw, wk_ref[...])
        kp = _rms64(kp_raw, wk_ref[...])
        sp, sc = _band_logits(qn, kp, kc, _alibi_slopes(group) * float(dilation), has_prev)
        pp = jnp.exp(sp - lse)
        pc = jnp.exp(sc - lse)
        dsp = pp * (_bdot_nt(d_o, vp) - delta) * scale
        dsc = pc * (_bdot_nt(d_o, vc) - delta) * scale
        dqn = _bdot(dsp, kp) + _bdot(dsc, kc)
        dkc_n = _bdot_tn(dsc, qn) + _stack_heads(dk_carry)
        dvc = _bdot_tn(pc, d_o) + _stack_heads(dv_carry)
        _store_heads(dk_carry, _bdot_tn(dsp, qn))
        _store_heads(dv_carry, _bdot_tn(pp, d_o))
        dq_raw, dwq_rows = _rms64_bwd(q_raw, wq_ref[...], dqn)
        dk_raw, dwk_rows = _rms64_bwd(kc_raw, wk_ref[...], dkc_n)
        _store_heads(d_ref, dq_raw)
        _store_heads(d_ref, dk_raw, first=DIL_HEADS)
        _store_heads(d_ref, dvc, first=2 * DIL_HEADS)
        dwq_ref[...] += jnp.sum(jnp.sum(dwq_rows, axis=0), axis=0, keepdims=True)
        dwk_ref[...] += jnp.sum(jnp.sum(dwk_rows, axis=0), axis=0, keepdims=True)

    blk_i = lambda i: nlb - 1 - i
    cur = pl.BlockSpec((DIL_SPAN, DIL_SLAB), lambda r, i: (blk_i(i), r))
    prv = lambda part: pl.BlockSpec((DIL_SPAN, wide), lambda r, i: (jnp.maximum(blk_i(i) - 1, 0), 3 * r + part))
    one = pl.BlockSpec((1, LANES), lambda r, i: (0, 0))
    dslab, dwq, dwk = pl.pallas_call(
        body,
        grid=(dilation, nlb),
        in_specs=[cur, prv(1), prv(2), pl.BlockSpec((DIL_SPAN, wide), lambda r, i: (blk_i(i), r)), one, one, one, one],
        out_specs=[cur, one, one],
        out_shape=[jax.ShapeDtypeStruct((rows, dilation * DIL_SLAB), MM_DTYPE), jax.ShapeDtypeStruct((1, LANES), F32),
                   jax.ShapeDtypeStruct((1, LANES), F32)],
        scratch_shapes=[pltpu.VMEM((DIL_SPAN, wide), F32), pltpu.VMEM((DIL_SPAN, wide), F32)],
        compiler_params=_cparams(("arbitrary", "arbitrary")),
        name=name,
    )(view, view, view, stat_view, wq_row, wk_row, dwq_in, dwk_in)
    return dslab.reshape(t, DIL_SLAB), dwq, dwk


def _row(v, width=LANES):
    v = v.astype(F32).reshape(-1)
    return jnp.pad(v, (0, width - v.shape[0])).reshape(1, width)


def _prepare_weights(w):
    return dict(gdn=_prepare_gdn(w), dil=_prepare_dil(w), ffn=_prepare_ffn(w))


def _prepare_gdn(w, layers=range(DEPTH // 2)):
    gdn = {}
    for j in layers:
        wt = w["gdn_w_in"][j]
        gates_t = jnp.pad(wt[GDN_MAIN:], ((0, LANES - 2 * GDN_HEADS), (0, 0)))
        gdn[j] = dict(in_t=wt, gates_t=gates_t, out=w["gdn_w_out"][j], conv=w["gdn_conv_w"][j].astype(F32),
                      alog=_row(w["gdn_a_log"][j]), dt=_row(w["gdn_dt_bias"][j]), nw=_row(w["gdn_norm_w"][j]))
    return gdn


def _prepare_dil(w, layers=range(DEPTH // 2)):
    d = D_MODEL
    dil = {}
    for j in layers:
        wt = w["dil_w_in"][j].reshape(3, len(DIL_GROUPS), DIL_HEADS, DIL_DH, d)
        wg_t = [wt[:, g].reshape(DIL_SLAB // 2, d) for g in range(len(DIL_GROUPS))]
        out_t = jnp.pad(w["dil_w_out"][j].reshape(d, DIL_HEADS, DIL_DH), ((0, 0), (0, 0), (0, LANES - DIL_DH)))
        dil[j] = dict(wg_t=wg_t, out_t=out_t.reshape(d, DIL_HEADS * LANES), wq=_row(w["dil_q_norm"][j]),
                      wk=_row(w["dil_k_norm"][j]))
    return dil


def _prepare_ffn(w, layers=range(DEPTH)):
    return {i: dict(in_t=w["ffn_w_in"][i], out=w["ffn_w_out"][i]) for i in layers}


def _gdn_layer_fwd(x, nrow, p):
    hn = _rmsnorm_fwd(x, nrow, name="rmsnorm_fwd")
    pm = _matmul(hn, p["in_t"], trans_b=True, b_rows=(0, GDN_MAIN), name="gdn_proj_main")
    ab = _matmul(hn, p["gates_t"], trans_b=True, name="gdn_proj_gates")
    qkv = _gdn_conv_fwd(pm, p["conv"], name="gdn_conv_fwd")
    gb, bb = _gdn_gates_fwd(ab, p["alog"], p["dt"], name="gdn_gates_fwd")
    prep, rest = _gdn_prep_fwd(qkv, gb, bb, name="gdn_prep_fwd")
    o, states = _gdn_scan_fwd(prep, name="gdn_scan_fwd")
    og = _gdn_outgate_fwd(o, pm, p["nw"], name="gdn_outgate_fwd")
    y = _matmul(og, p["out"], add=x, name="gdn_proj_out")
    return y, (x, hn, pm, ab, qkv, gb, bb, prep, rest, states, o, og)


def _gdn_layer_bwd(dx, dxb, nrow, p, saved):
    x, hn, pm, ab, qkv, gb, bb, prep, rest, states, o, og = saved
    d_og = _matmul(dxb, p["out"], trans_b=True, name="gdn_dgate")
    g_out = _matmul(og, dxb, trans_a=True, out_dtype=MM_DTYPE, name="gdn_gw_out")
    d_o, d_z, d_nw = _gdn_outgate_bwd(o, pm, p["nw"], d_og, name="gdn_outgate_bwd")
    cts = _gdn_scan_bwd(prep, states, d_o, name="gdn_scan_bwd")
    dqkv, dgb, dbb = _gdn_prep_bwd(qkv, gb, bb, rest, cts, name="gdn_prep_bwd")
    d_ab, d_alog, d_dt = _gdn_gates_bwd(ab, p["alog"], p["dt"], dgb, dbb, name="gdn_gates_bwd")
    d_conv, g_conv = _gdn_conv_bwd(pm, p["conv"], dqkv, name="gdn_conv_bwd")
    d_hn = _matmul(d_conv, p["in_t"], b_rows=(0, GDN_QKV), name="gdn_dhn_qkv")
    d_hn = _matmul(d_z, p["in_t"], b_rows=(GDN_QKV, GDN_MAIN - GDN_QKV), add=d_hn, name="gdn_dhn_z")
    d_hn = _matmul(d_ab, p["gates_t"], add=d_hn, name="gdn_dhn_gates")
    g_in_t = jnp.concatenate([
        _matmul(d_conv, hn, trans_a=True, out_dtype=MM_DTYPE, name="gdn_gw_qkv"),
        _matmul(d_z, hn, trans_a=True, out_dtype=MM_DTYPE, name="gdn_gw_z"),
        _matmul(d_ab, hn, trans_a=True, out_dtype=MM_DTYPE, name="gdn_gw_gates")[:2 * GDN_HEADS],
    ], axis=0)
    dx_new, dxb_new, g_norm = _rmsnorm_bwd(x, nrow, d_hn, dx, name="rmsnorm_bwd")
    grads = dict(w_in=g_in_t, conv=g_conv, a_log=d_alog[0, :GDN_HEADS], dt_bias=d_dt[0, :GDN_HEADS], norm_w=d_nw[0],
                 w_out=g_out, norm=g_norm[0])
    return dx_new, dxb_new, grads


def _dil_layer_fwd(x, nrow, p):
    hn = _rmsnorm_fwd(x, nrow, name="rmsnorm_fwd")
    slabs = [_matmul(hn, p["wg_t"][g], trans_b=True, spread_out=True, name="dil_proj_in") for g in range(len(DIL_GROUPS))]
    oe = [(_dil_attn_fwd if DIL_GROUPS[g][1] == 1 else _dil_attn_fwd_strided)(
        slabs[g], p["wq"], p["wk"], group=g, name=f"dil_attn_fwd_g{g}") for g in range(len(DIL_GROUPS))]
    y, om = _dil_merge_fwd(oe, name="dil_merge_fwd")
    out = _matmul(y, p["out_t"], trans_b=True, add=x, name="dil_proj_out")
    return out, (x, hn, slabs, y, om)


def _dil_layer_bwd(dx, dxb, nrow, p, saved):
    x, hn, slabs, y, om = saved
    d_y = _matmul(dxb, p["out_t"], name="dil_dmerged")
    g_out_t = _matmul(dxb, y, trans_a=True, out_dtype=MM_DTYPE, name="dil_gw_out")
    g_out_t = g_out_t.reshape(D_MODEL, DIL_HEADS, LANES)[..., :DIL_DH].reshape(D_MODEL, DIL_HEADS * DIL_DH)
    stat = _dil_merge_bwd(d_y, om, name="dil_merge_bwd")
    d_hn = None
    dwq = jnp.zeros((1, LANES), F32)
    dwk = jnp.zeros((1, LANES), F32)
    g_groups = []
    wide = DIL_HEADS * LANES
    for g in range(len(DIL_GROUPS)):
        if DIL_GROUPS[g][1] == 1:
            dslab, dwq, dwk = _dil_attn_bwd(slabs[g], stat, p["wq"], p["wk"], dwq, dwk, group=g, name=f"dil_attn_bwd_g{g}")
            d_hn = _matmul(dslab, p["wg_t"][g], packed_a=True, add=d_hn, name="dil_dhn")
            g_w = _matmul(dslab, hn, trans_a=True, packed_a=True, out_dtype=MM_DTYPE, name="dil_gw_in")
        else:
            dparts, dwq, dwk = _dil_attn_bwd_strided(slabs[g], stat, p["wq"], p["wk"], dwq, dwk, group=g,
                                                     name=f"dil_attn_bwd_g{g}")
            d_hn = _matmul(dparts, p["wg_t"][g], a_lead="k", packed_a=True, add=d_hn, name="dil_dhn_parts")
            g_w = _matmul(dparts, hn, trans_a=True, a_lead="i", packed_a=True, out_dtype=MM_DTYPE, name="dil_gw_in_parts")
        g_groups.append(g_w.reshape(3, DIL_HEADS, DIL_DH, D_MODEL))
    g_in_t = jnp.stack(g_groups, axis=1).reshape(3 * len(DIL_GROUPS) * DIL_HEADS * DIL_DH, D_MODEL)
    dx_new, dxb_new, g_norm = _rmsnorm_bwd(x, nrow, d_hn, dx, name="rmsnorm_bwd")
    grads = dict(w_in=g_in_t, q_norm=dwq[0, :DIL_DH], k_norm=dwk[0, :DIL_DH], w_out=g_out_t, norm=g_norm[0])
    return dx_new, dxb_new, grads


def _ffn_layer_fwd(x, nrow, p):
    hn = _rmsnorm_fwd(x, nrow, name="rmsnorm_fwd")
    gate, up, act = _ffn_in(hn, p["in_t"], name="ffn_proj_in")
    y = _matmul(act, p["out"], add=x, name="ffn_proj_out")
    return y, (x, hn, gate, up, act)


def _ffn_layer_bwd(dx, dxb, nrow, p, saved):
    x, hn, gate, up, act = saved
    g_out = _matmul(act, dxb, trans_a=True, out_dtype=MM_DTYPE, name="ffn_gw_out")
    d_gu = _ffn_dact(dxb, p["out"], gate, up, name="ffn_dact")
    d_hn = _matmul(d_gu, p["in_t"], a_lead="k", name="ffn_dhn")
    g_in_t = _matmul(d_gu, hn, trans_a=True, a_lead="i", out_dtype=MM_DTYPE, name="ffn_gw_in")
    dx_new, dxb_new, g_norm = _rmsnorm_bwd(x, nrow, d_hn, dx, name="rmsnorm_bwd")
    return dx_new, dxb_new, dict(w_in=g_in_t, w_out=g_out, norm=g_norm[0])


def _mixer_fwd(i, x, mix_row, prepared):
    if i % 2 == 0:
        return _gdn_layer_fwd(x, mix_row, prepared["gdn"][i // 2])
    return _dil_layer_fwd(x, mix_row, prepared["dil"][i // 2])


def _mixer_bwd(i, dx, dxb, mix_row, prepared, saved):
    if i % 2 == 0:
        return _gdn_layer_bwd(dx, dxb, mix_row, prepared["gdn"][i // 2], saved)
    return _dil_layer_bwd(dx, dxb, mix_row, prepared["dil"][i // 2], saved)


def _local_step(x, target, prepared, norm_mix, norm_ffn):
    saved = []
    for i in range(DEPTH):
        x, s_mix = _mixer_fwd(i, x, norm_mix[i].reshape(1, D_MODEL), prepared)
        x, s_ffn = _ffn_layer_fwd(x, norm_ffn[i].reshape(1, D_MODEL), prepared["ffn"][i])
        saved.append((s_mix, s_ffn))
    dx, dxb, loss = _loss_head(x, target, name="loss_head")
    g_mix, g_ffn = [None] * DEPTH, [None] * DEPTH
    for i in reversed(range(DEPTH)):
        s_mix, s_ffn = saved[i]
        dx, dxb, g_ffn[i] = _ffn_layer_bwd(dx, dxb, norm_ffn[i].reshape(1, D_MODEL), prepared["ffn"][i], s_ffn)
        dx, dxb, g_mix[i] = _mixer_bwd(i, dx, dxb, norm_mix[i].reshape(1, D_MODEL), prepared, s_mix)
    return loss[0, 0], dx, _collect_grads(g_mix, g_ffn)


def _collect_grads(g_mix, g_ffn):
    gdn = [g_mix[i] for i in range(0, DEPTH, 2)]
    dil = [g_mix[i] for i in range(1, DEPTH, 2)]
    if any(g is None for g in g_mix + g_ffn):
        pick = lambda gs, key: [None if g is None else g[key] for g in gs]
        return dict(gdn_w_in=pick(gdn, "w_in"), gdn_w_out=pick(gdn, "w_out"), dil_w_in=pick(dil, "w_in"),
                    dil_w_out=pick(dil, "w_out"), ffn_w_in=pick(g_ffn, "w_in"), ffn_w_out=pick(g_ffn, "w_out"))
    grads = dict(
        norm_mix=jnp.stack([g["norm"] for g in g_mix]),
        norm_ffn=jnp.stack([g["norm"] for g in g_ffn]),
        gdn_w_in=[g["w_in"] for g in gdn],
        gdn_conv_w=jnp.stack([g["conv"] for g in gdn]),
        gdn_a_log=jnp.stack([g["a_log"] for g in gdn]),
        gdn_dt_bias=jnp.stack([g["dt_bias"] for g in gdn]),
        gdn_norm_w=jnp.stack([g["norm_w"] for g in gdn]),
        gdn_w_out=[g["w_out"] for g in gdn],
        dil_w_in=[g["w_in"] for g in dil],
        dil_q_norm=jnp.stack([g["q_norm"] for g in dil]),
        dil_k_norm=jnp.stack([g["k_norm"] for g in dil]),
        dil_w_out=[g["w_out"] for g in dil],
        ffn_w_in=[g["w_in"] for g in g_ffn],
        ffn_w_out=[g["w_out"] for g in g_ffn],
    )
    return grads


MESH_ID = pl.DeviceIdType.MESH
ANY_SPACE = pl.BlockSpec(memory_space=pl.ANY)


def _mesh_position():
    return lax.axis_index("x"), lax.axis_index("y"), lax.axis_index("c")


def _flip(pos, k):
    x, y, c = pos
    return (1 - x if k & 4 else x, 1 - y if k & 2 else y, 1 - c if k & 1 else c)


def _linear(pos):
    return 4 * pos[0] + 2 * pos[1] + pos[2]


def _comm_scratch():
    return [pltpu.SemaphoreType.DMA((N_DEV - 1,)), pltpu.SemaphoreType.DMA((N_DEV - 1,)), pltpu.SemaphoreType.DMA(())]


def _all_gather(shard, *, name):
    def body(x_ref, out_ref, send_sems, recv_sems, local_sem):
        me = _mesh_position()
        mine = out_ref.at[_linear(me)]
        local = pltpu.make_async_copy(x_ref, mine, local_sem)
        local.start()
        copies = []
        for k in range(1, N_DEV):
            cp = pltpu.make_async_remote_copy(src_ref=x_ref, dst_ref=mine, send_sem=send_sems.at[k - 1],
                                              recv_sem=recv_sems.at[k - 1], device_id=_flip(me, k), device_id_type=MESH_ID)
            cp.start()
            copies.append(cp)
        for cp in copies:
            cp.wait()
        local.wait()

    return pl.pallas_call(
        body,
        out_shape=jax.ShapeDtypeStruct((N_DEV,) + shard.shape, shard.dtype),
        in_specs=[ANY_SPACE],
        out_specs=ANY_SPACE,
        scratch_shapes=_comm_scratch(),
        name=name,
    )(shard)


def _exchange(parts, *, name):
    def body(p_ref, out_ref, send_sems, recv_sems, local_sem):
        me = _mesh_position()
        mine = out_ref.at[_linear(me)]
        local = pltpu.make_async_copy(p_ref.at[_linear(me)], mine, local_sem)
        local.start()
        copies = []
        for k in range(1, N_DEV):
            peer = _flip(me, k)
            cp = pltpu.make_async_remote_copy(src_ref=p_ref.at[_linear(peer)], dst_ref=mine, send_sem=send_sems.at[k - 1],
                                              recv_sem=recv_sems.at[k - 1], device_id=peer, device_id_type=MESH_ID)
            cp.start()
            copies.append(cp)
        for cp in copies:
            cp.wait()
        local.wait()

    return pl.pallas_call(
        body,
        out_shape=jax.ShapeDtypeStruct(parts.shape, parts.dtype),
        in_specs=[ANY_SPACE],
        out_specs=ANY_SPACE,
        scratch_shapes=_comm_scratch(),
        name=name,
    )(parts)


HBM_SPACE = pl.BlockSpec(memory_space=pltpu.HBM)
SEM_SPACE = pl.BlockSpec(memory_space=pltpu.SEMAPHORE)
DATAFLOW = pltpu.SideEffectType.DATAFLOW_SIDE_EFFECTING


def _split_copies(src_ref, land_ref, send_sems, recv_sems, per_peer):
    me = _mesh_position()
    mine = land_ref.at[_linear(me)]
    copies = []
    for k in range(1, N_DEV):
        peer = _flip(me, k)
        src = src_ref.at[_linear(peer)] if per_peer else src_ref
        copies.append(pltpu.make_async_remote_copy(src_ref=src, dst_ref=mine, send_sem=send_sems.at[k - 1],
                                                   recv_sem=recv_sems.at[k - 1], device_id=peer, device_id_type=MESH_ID))
    return copies


def _travel_start(src, after, *, per_peer, name):
    me = _linear(_mesh_position())
    own = src[me] if per_peer else src
    shape = own.shape
    landing = lax.dynamic_update_slice(lax.empty((N_DEV,) + shape, src.dtype), own[None], (me, 0, 0))

    def body(src_ref, land_ref, after_ref, send_sems, recv_sems, src_thru, land_thru, token):
        for cp in _split_copies(src_ref, land_ref, send_sems, recv_sems, per_peer):
            cp.start()
        token[...] = jnp.zeros_like(token)

    return pl.pallas_call(
        body,
        name=name,
        out_shape=(pltpu.SemaphoreType.DMA((N_DEV - 1,)), pltpu.SemaphoreType.DMA((N_DEV - 1,)),
                   pltpu.HBM(src.shape, src.dtype), pltpu.HBM(landing.shape, landing.dtype),
                   jax.ShapeDtypeStruct((8, LANES), F32)),
        in_specs=(HBM_SPACE, HBM_SPACE, ANY_SPACE),
        out_specs=(SEM_SPACE, SEM_SPACE, HBM_SPACE, HBM_SPACE, pl.BlockSpec(memory_space=pltpu.VMEM)),
        input_output_aliases={0: 2, 1: 3},
        compiler_params=pltpu.CompilerParams(has_side_effects=DATAFLOW),
    )(pltpu.with_memory_space_constraint(src, pltpu.HBM), pltpu.with_memory_space_constraint(landing, pltpu.HBM), after)


def _travel_wait(started, after, *, per_peer, name):
    send_sems, recv_sems, src_thru, land_thru, _ = started

    def body(src_ref, land_ref, send_sems, recv_sems, after_ref, src_dead, got_ref):
        for cp in _split_copies(src_ref, land_ref, send_sems, recv_sems, per_peer):
            cp.wait_send()
            cp.wait_recv()

    return pl.pallas_call(
        body,
        name=name,
        out_shape=(pltpu.HBM(src_thru.shape, src_thru.dtype), pltpu.HBM(land_thru.shape, land_thru.dtype)),
        in_specs=(HBM_SPACE, HBM_SPACE, SEM_SPACE, SEM_SPACE, ANY_SPACE),
        out_specs=(HBM_SPACE, HBM_SPACE),
        input_output_aliases={0: 0, 1: 1},
        compiler_params=pltpu.CompilerParams(has_side_effects=DATAFLOW),
    )(src_thru, land_thru, send_sems, recv_sems, after)[1]


def _adamw(parts, w, m, v, *, name):
    rows, n = w.shape
    tb = _pick(rows, (PACK_ROW_ALIGN, 16))
    c1 = 1.0 - ADAM_B1 ** ADAM_STEP
    c2 = 1.0 - ADAM_B2 ** ADAM_STEP

    def body(p_ref, w_ref, m_ref, v_ref, g_ref, d_ref, nm_ref, nv_ref):
        g = p_ref[0].astype(F32)
        for s in range(1, N_DEV):
            g = g + p_ref[s].astype(F32)
        m_new = ADAM_B1 * m_ref[...] + (1.0 - ADAM_B1) * g
        v_new = ADAM_B2 * v_ref[...] + (1.0 - ADAM_B2) * (g * g)
        m_hat = m_new / c1
        v_hat = v_new / c2
        g_ref[...] = g
        nm_ref[...] = m_new
        nv_ref[...] = v_new
        d_ref[...] = -ADAM_LR * (m_hat / (jnp.sqrt(v_hat) + ADAM_EPS) + ADAM_WD * w_ref[...])

    blk = pl.BlockSpec((tb, n), lambda i: (i, 0))
    return pl.pallas_call(
        body,
        grid=(rows // tb,),
        in_specs=[pl.BlockSpec((N_DEV, tb, n), lambda i: (0, i, 0)), blk, blk, blk],
        out_specs=[blk] * 4,
        out_shape=[jax.ShapeDtypeStruct((rows, n), F32)] * 4,
        compiler_params=_cparams(("parallel",)),
        name=name,
    )(parts, w, m, v)


PACK_WIDTH = 1024
SHARDED = {
    "gdn_w_in": ((2, D_MODEL, GDN_IN_WIDTH), 2),
    "gdn_conv_w": ((2, GDN_CONV, GDN_QKV), 2),
    "gdn_w_out": ((2, GDN_HEADS * GDN_DV, D_MODEL), 1),
    "dil_w_in": ((2, D_MODEL, 3 * len(DIL_GROUPS) * DIL_HEADS * DIL_DH), 2),
    "dil_w_out": ((2, DIL_HEADS * DIL_DH, D_MODEL), 2),
    "ffn_w_in": ((DEPTH, D_MODEL, 2 * FFN_HIDDEN), 2),
    "ffn_w_out": ((DEPTH, FFN_HIDDEN, D_MODEL), 1),
}
REPLICATED = {"norm_mix": (DEPTH, D_MODEL), "norm_ffn": (DEPTH, D_MODEL), "gdn_a_log": (2, GDN_HEADS),
              "gdn_dt_bias": (2, GDN_HEADS), "gdn_norm_w": (2, GDN_DV), "dil_q_norm": (2, DIL_DH), "dil_k_norm": (2, DIL_DH)}
WEIGHT_ORDER = ("norm_mix", "norm_ffn", "gdn_w_in", "gdn_conv_w", "gdn_a_log", "gdn_dt_bias", "gdn_norm_w", "gdn_w_out",
                "dil_w_in", "dil_q_norm", "dil_k_norm", "dil_w_out", "ffn_w_in", "ffn_w_out")
PACK_ROW_ALIGN = 128
PIECE_ALIGN = 16
SMALL_ROWS = 16


def _shard_shape(name):
    shape, axis = SHARDED[name]
    return tuple(s // N_DEV if i == axis else s for i, s in enumerate(shape))


def _shard_rows(name):
    return math.prod(_shard_shape(name)) // PACK_WIDTH


def _split_shards(full, name):
    shape, axis = SHARDED[name]
    split = full.reshape(shape[:axis] + (N_DEV, shape[axis] // N_DEV) + shape[axis + 1:])
    return jnp.moveaxis(split, axis, 0)


def _join_shards(stacked, name):
    shape, axis = SHARDED[name]
    return jnp.moveaxis(stacked, 0, axis).reshape(shape)


COLUMN_SHARDED = ("gdn_w_in", "dil_w_in", "dil_w_out", "ffn_w_in")


def _to_rows(shard, name):
    if name in COLUMN_SHARDED:
        shard = jnp.swapaxes(shard, 1, 2)
    return shard.reshape(-1, PACK_WIDTH)


def _layer_columns(name):
    _, r, c = _shard_shape(name)
    return r if name in COLUMN_SHARDED else c


def _piece_rows(piece, halves=1):
    name, layer = piece
    rows = _shard_rows(name) * halves
    return rows if layer is None else rows // SHARDED[name][0][0]


def _aligned(rows, to=PIECE_ALIGN):
    return -(-rows // to) * to


def _pack_pieces(arrays, total_align=PIECE_ALIGN):
    padded, total = [], 0
    for a in arrays:
        rows = a.shape[-2]
        extra = _aligned(rows) - rows
        if extra:
            a = jnp.pad(a, [(0, 0)] * (a.ndim - 2) + [(0, extra), (0, 0)])
        padded.append(a)
        total += rows + extra
    tail = _aligned(total, total_align) - total
    if tail:
        padded.append(jnp.zeros(padded[0].shape[:-2] + (tail, PACK_WIDTH), padded[0].dtype))
    return jnp.concatenate(padded, axis=-2)


def _piece_offsets(pieces, halves=None):
    out, at = [], 0
    for p in pieces:
        rows = _piece_rows(p, (halves or {}).get(p[0], 1))
        out.append((p, at, rows))
        at += _aligned(rows)
    return out


def _shard_piece_rows(src, piece):
    name, layer = piece
    part = src[name] if layer is None else src[name][layer:layer + 1]
    return _to_rows(part.astype(F32), name)


def _piece_from_rows(rows, piece):
    name, layer = piece
    layers, r, c = _shard_shape(name)
    n_l = layers if layer is None else 1
    if name in COLUMN_SHARDED:
        return jnp.swapaxes(rows.reshape(n_l, c, r), 1, 2)
    return rows.reshape(n_l, r, c)


SMALL_TAIL = tuple(n for n in REPLICATED if n not in ("norm_mix", "norm_ffn"))


def _pack_small(vals):
    tail, at = jnp.zeros((PACK_WIDTH,), F32), 0
    for n in SMALL_TAIL:
        vec = vals[n].astype(F32).reshape(-1)
        tail = tail + jnp.pad(vec, (at, PACK_WIDTH - at - vec.shape[0]))
        at += vec.shape[0]
    buf = jnp.pad(vals["norm_mix"].astype(F32), ((0, SMALL_ROWS - DEPTH), (0, 0)))
    buf = buf + jnp.pad(vals["norm_ffn"].astype(F32), ((8, SMALL_ROWS - 8 - DEPTH), (0, 0)))
    return buf + jnp.pad(tail.reshape(1, PACK_WIDTH), ((SMALL_ROWS - 1, 0), (0, 0)))


def _unpack_small(buf):
    out = {"norm_mix": buf[0:DEPTH], "norm_ffn": buf[8:8 + DEPTH]}
    at = 0
    for n in SMALL_TAIL:
        size = math.prod(REPLICATED[n])
        out[n] = buf[SMALL_ROWS - 1, at:at + size].reshape(REPLICATED[n])
        at += size
    return out


GATHER_FIRST = (("gdn_w_in", 0), ("gdn_conv_w", None), ("gdn_w_out", 0))
GATHER_NEXT = (("ffn_w_in", 0), ("ffn_w_out", 0), ("dil_w_in", 0), ("dil_w_out", 0))
GATHER_LAST = (("ffn_w_in", 1), ("ffn_w_out", 1), ("gdn_w_in", 1), ("gdn_w_out", 1), ("ffn_w_in", 2), ("ffn_w_out", 2),
               ("dil_w_in", 1), ("dil_w_out", 1), ("ffn_w_in", 3), ("ffn_w_out", 3))
EXCHANGE_GROUPS = (
    (("ffn_w_in", 3), ("ffn_w_out", 3), ("dil_w_in", 1), ("dil_w_out", 1),
     ("ffn_w_in", 2), ("ffn_w_out", 2), ("gdn_w_in", 1), ("gdn_w_out", 1)),
    (("ffn_w_in", 1), ("ffn_w_out", 1), ("dil_w_in", 0), ("dil_w_out", 0)),
    (("ffn_w_in", 0), ("ffn_w_out", 0)),
    (("gdn_w_in", 0), ("gdn_w_out", 0), ("gdn_conv_w", None)),
)
EXCHANGE_AFTER = {("mix", 2): 0, ("mix", 1): 1, ("ffn", 0): 2}


def _gather_operand(w, pieces):
    arrays = []
    for n, layer in pieces:
        if layer is None:
            arrays.append(lax.bitcast_convert_type(w[n], BF16).reshape(-1, PACK_WIDTH))
        else:
            arrays.append(_to_rows(w[n][layer:layer + 1].astype(BF16), n))
    return _pack_pieces(arrays)


def _gathered_weights(gathered, pieces, full):
    for (n, layer), at, rows in _piece_offsets(pieces, halves={"gdn_conv_w": 2}):
        block = gathered[:, at:at + rows]
        if layer is None:
            block = lax.bitcast_convert_type(block.reshape((N_DEV,) + _shard_shape(n) + (2,)), F32)
            full[n] = _join_shards(block, n)
        else:
            full.setdefault(n, {})[layer] = block.reshape(-1, _layer_columns(n))
    return full


def _exchange_operand(grads, pieces):
    arrays = []
    for n, layer in pieces:
        if layer is None:
            arrays.append(_split_shards(grads[n], n).astype(BF16).reshape(N_DEV, -1, PACK_WIDTH))
        else:
            arrays.append(grads[n][layer].astype(BF16).reshape(N_DEV, -1, PACK_WIDTH))
    return _pack_pieces(arrays, total_align=PACK_ROW_ALIGN)


def _update_group(received, pieces, w, m, v, *, name):
    packed = [_pack_pieces([_shard_piece_rows(src, p) for p in pieces], total_align=PACK_ROW_ALIGN) for src in (w, m, v)]
    outs = _adamw(received, *packed, name=name)
    return {p: tuple(_piece_from_rows(o[at:at + rows], p) for o in outs) for p, at, rows in _piece_offsets(pieces)}


def kernel(x, norm_mix, norm_ffn, gdn_w_in, gdn_conv_w, gdn_a_log, gdn_dt_bias, gdn_norm_w, gdn_w_out, dil_w_in, dil_q_norm, dil_k_norm, dil_w_out, ffn_w_in, ffn_w_out, loss_target, m_norm_mix, m_norm_ffn, m_gdn_w_in, m_gdn_conv_w, m_gdn_a_log, m_gdn_dt_bias, m_gdn_norm_w, m_gdn_w_out, m_dil_w_in, m_dil_q_norm, m_dil_k_norm, m_dil_w_out, m_ffn_w_in, m_ffn_w_out, v_norm_mix, v_norm_ffn, v_gdn_w_in, v_gdn_conv_w, v_gdn_a_log, v_gdn_dt_bias, v_gdn_norm_w, v_gdn_w_out, v_dil_w_in, v_dil_q_norm, v_dil_k_norm, v_dil_w_out, v_ffn_w_in, v_ffn_w_out):
    w = dict(norm_mix=norm_mix, norm_ffn=norm_ffn, gdn_w_in=gdn_w_in, gdn_conv_w=gdn_conv_w, gdn_a_log=gdn_a_log,
             gdn_dt_bias=gdn_dt_bias, gdn_norm_w=gdn_norm_w, gdn_w_out=gdn_w_out, dil_w_in=dil_w_in, dil_q_norm=dil_q_norm,
             dil_k_norm=dil_k_norm, dil_w_out=dil_w_out, ffn_w_in=ffn_w_in, ffn_w_out=ffn_w_out)
    m = dict(norm_mix=m_norm_mix, norm_ffn=m_norm_ffn, gdn_w_in=m_gdn_w_in, gdn_conv_w=m_gdn_conv_w, gdn_a_log=m_gdn_a_log,
             gdn_dt_bias=m_gdn_dt_bias, gdn_norm_w=m_gdn_norm_w, gdn_w_out=m_gdn_w_out, dil_w_in=m_dil_w_in,
             dil_q_norm=m_dil_q_norm, dil_k_norm=m_dil_k_norm, dil_w_out=m_dil_w_out, ffn_w_in=m_ffn_w_in, ffn_w_out=m_ffn_w_out)
    v = dict(norm_mix=v_norm_mix, norm_ffn=v_norm_ffn, gdn_w_in=v_gdn_w_in, gdn_conv_w=v_gdn_conv_w, gdn_a_log=v_gdn_a_log,
             gdn_dt_bias=v_gdn_dt_bias, gdn_norm_w=v_gdn_norm_w, gdn_w_out=v_gdn_w_out, dil_w_in=v_dil_w_in,
             dil_q_norm=v_dil_q_norm, dil_k_norm=v_dil_k_norm, dil_w_out=v_dil_w_out, ffn_w_in=v_ffn_w_in, ffn_w_out=v_ffn_w_out)
    def row(src, i):
        return src[i].reshape(1, D_MODEL)

    first = _all_gather(_gather_operand(w, GATHER_FIRST), name="weight_all_gather_first")
    next_started = _travel_start(_gather_operand(w, GATHER_NEXT), first, per_peer=False, name="weight_gather_start_next")
    last_started = _travel_start(_gather_operand(w, GATHER_LAST), next_started[4], per_peer=False,
                                 name="weight_gather_start_last")
    full = _gathered_weights(first, GATHER_FIRST, {n: w[n] for n in REPLICATED})
    prepared = dict(gdn=_prepare_gdn(full, layers=(0,)))
    h = x[0]
    saved = [None] * DEPTH
    h, s_mix = _mixer_fwd(0, h, row(norm_mix, 0) + last_started[4][0, 0], prepared)
    got = _travel_wait(next_started, h, per_peer=False, name="weight_gather_wait_next")
    full = _gathered_weights(got, GATHER_NEXT, full)
    prepared.update(dil=_prepare_dil(full, layers=(0,)), ffn=_prepare_ffn(full, layers=(0,)))
    for i in range(DEPTH):
        if i > 0:
            h, s_mix = _mixer_fwd(i, h, row(norm_mix, i), prepared)
        if i == 1:
            got = _travel_wait(last_started, h, per_peer=False, name="weight_gather_wait_last")
            full = _gathered_weights(got, GATHER_LAST, full)
            prepared["gdn"].update(_prepare_gdn(full, layers=(1,)))
            prepared["dil"].update(_prepare_dil(full, layers=(1,)))
            prepared["ffn"].update(_prepare_ffn(full, layers=(1, 2, 3)))
        h, s_ffn = _ffn_layer_fwd(h, row(norm_ffn, i), prepared["ffn"][i])
        saved[i] = (s_mix, s_ffn)
    dx, dxb, loss = _loss_head(h, loss_target[0], name="loss_head")

    g_mix, g_ffn = [None] * DEPTH, [None] * DEPTH
    started = {}

    def travel(group, dxb):
        operand = _exchange_operand(_collect_grads(g_mix, g_ffn), EXCHANGE_GROUPS[group])
        started[group] = _travel_start(operand, dx, per_peer=True, name=f"grad_exchange_start_{group}")
        return dxb + started[group][4][0, 0].astype(dxb.dtype)

    for i in reversed(range(DEPTH)):
        s_mix, s_ffn = saved[i]
        dx, dxb, g_ffn[i] = _ffn_layer_bwd(dx, dxb, row(norm_ffn, i), prepared["ffn"][i], s_ffn)
        if ("ffn", i) in EXCHANGE_AFTER:
            dxb = travel(EXCHANGE_AFTER[("ffn", i)], dxb)
        dx, dxb, g_mix[i] = _mixer_bwd(i, dx, dxb, row(norm_mix, i), prepared, s_mix)
        if ("mix", i) in EXCHANGE_AFTER:
            dxb = travel(EXCHANGE_AFTER[("mix", i)], dxb)
    grads = _collect_grads(g_mix, g_ffn)
    received = [_travel_wait(started[g], dx, per_peer=True, name=f"grad_exchange_wait_{g}") for g in sorted(started)]
    received.append(_exchange(_exchange_operand(grads, EXCHANGE_GROUPS[-1]), name="grad_exchange_last"))
    updated = {}
    for g, pieces in enumerate(EXCHANGE_GROUPS):
        updated.update(_update_group(received[g], pieces, w, m, v, name=f"adamw_sharded_{g}"))

    small_parts = _all_gather(_pack_small(grads), name="small_grad_all_gather")
    outs_small = [_unpack_small(o) for o in
                  _adamw(small_parts, _pack_small(w), _pack_small(m), _pack_small(v), name="adamw_replicated")]

    total_loss = lax.psum(loss[0, 0], ("x", "y", "c"))
    result = [total_loss, dx[None]]
    for k in range(4):
        for n in WEIGHT_ORDER:
            if n not in SHARDED:
                result.append(outs_small[k][n])
            elif (n, None) in updated:
                result.append(updated[(n, None)][k])
            else:
                result.append(jnp.concatenate([updated[(n, l)][k] for l in range(SHARDED[n][0][0])], axis=0))
    return tuple(result)
```

```python
import functools
import math

import jax
import jax.numpy as jnp
from jax import lax
from jax.experimental import pallas as pl
from jax.experimental.pallas import tpu as pltpu

F32 = jnp.float32
BF16 = jnp.bfloat16
MM_DTYPE = BF16

N_DEV = 8
D_MODEL = 1024
DEPTH = 4
RMS_EPS = 1e-6
L2_EPS = 1e-6

LANES = 128

GDN_HEADS = 8
GDN_DK = 128
GDN_DV = 128
GDN_CONV = 4
GDN_CHUNK = 128
GDN_QKV = 3 * GDN_HEADS * GDN_DK
GDN_MAIN = GDN_QKV + GDN_HEADS * GDN_DV
GDN_IN_WIDTH = GDN_MAIN + 2 * GDN_HEADS

DIL_GROUPS = ((128, 1), (512, 4), (2048, 16))
DIL_HEADS = 8
DIL_DH = 64
DIL_SPAN = 128
DIL_SLAB = 3 * DIL_HEADS * LANES
ALIBI_MAX_BIAS = 8.0

FFN_HIDDEN = 2816

ADAM_LR = 0.001
ADAM_B1 = 0.9
ADAM_B2 = 0.999
ADAM_EPS = 1e-08
ADAM_WD = 0.01
ADAM_STEP = 10

VMEM_LIMIT = 56 * 1024 * 1024
ROW_TILE = 512
NEG = -1e30
HI = lax.Precision.HIGHEST


def _cparams(sem):
    return pltpu.CompilerParams(dimension_semantics=sem, vmem_limit_bytes=VMEM_LIMIT)


def _dot(a, b):
    return lax.dot_general(a, b, (((1,), (0,)), ((), ())), preferred_element_type=F32, precision=HI)


def _dot_nt(a, b):
    return lax.dot_general(a, b, (((1,), (1,)), ((), ())), preferred_element_type=F32, precision=HI)


def _dot_tn(a, b):
    return lax.dot_general(a, b, (((0,), (0,)), ((), ())), preferred_element_type=F32, precision=HI)


def _single_pass(a, b, a_dim, b_dim):
    lead = a.ndim - 2
    batch = ((0,), (0,)) if lead else ((), ())
    return lax.dot_general(a.astype(BF16), b.astype(BF16), (((lead + a_dim,), (lead + b_dim,)), batch),
                           preferred_element_type=F32)


def _bdot(a, b):
    return _single_pass(a, b, 1, 0)


def _bdot_nt(a, b):
    return _single_pass(a, b, 1, 1)


def _bdot_tn(a, b):
    return _single_pass(a, b, 0, 0)


def _pick(n, candidates):
    for c in candidates:
        if n % c == 0:
            return c
    raise ValueError(f"no tile for {n}")


HALF = LANES // 2


def _pack_head_pairs(x):
    x = x.astype(F32)
    tiles = [x[:, (2 * i) * LANES:(2 * i + 1) * LANES] + pltpu.roll(x[:, (2 * i + 1) * LANES:(2 * i + 2) * LANES], HALF, 1)
             for i in range(x.shape[1] // (2 * LANES))]
    return tiles[0] if len(tiles) == 1 else jnp.concatenate(tiles, axis=1)


def _spread_head_pairs(y):
    low = lax.broadcasted_iota(jnp.int32, (y.shape[0], LANES), 1) < HALF
    tiles = []
    for i in range(y.shape[1] // LANES):
        pair = y[:, i * LANES:(i + 1) * LANES]
        tiles += [jnp.where(low, pair, 0.0), jnp.where(low, pltpu.roll(pair, HALF, 1), 0.0)]
    return jnp.concatenate(tiles, axis=1)


def _matmul(a, b, *, name, trans_a=False, trans_b=False, b_rows=None, a_lead=None, add=None, out_dtype=F32,
            packed_a=False, spread_out=False):
    if trans_a:
        k_dim, m_dim = a.shape[-2:]
        m_dim = m_dim // 2 if packed_a else m_dim
    else:
        m_dim, k_dim = a.shape[-2:]
        k_dim = k_dim // 2 if packed_a else k_dim
    slab_m, slab_k = m_dim, k_dim
    if a_lead == "k":
        assert not trans_a
        k_dim *= a.shape[0]
    elif a_lead == "i":
        assert trans_a
        m_dim *= a.shape[0]
    b_start, b_size = b_rows if b_rows is not None else (0, b.shape[0])
    if trans_b:
        n_dim, k2 = b_size, b.shape[1]
    else:
        k2, n_dim = b_size, b.shape[1]
    assert k_dim == k2, (a.shape, b.shape, b_rows)
    tn = _pick(n_dim, (1024, 512, 256, 128))
    tm = min(slab_m, 2048, max(512, (1024 * 1024) // tn))
    tm = _pick(slab_m, (tm, 1408, 1024, 512, 256, 128))
    tk = _pick(slab_k, (1024, 1408, 512, 256, 128))
    nk = k_dim // tk
    has_add = add is not None
    dn = (((0 if trans_a else 1,), (1 if trans_b else 0,)), ((), ()))
    b_tile = tn if trans_b else tk
    assert b_start % b_tile == 0, (b_rows, b_tile)
    b_off = b_start // b_tile

    def body(*refs):
        if has_add:
            a_ref, b_ref, add_ref, o_ref, acc_ref = refs
        else:
            a_ref, b_ref, o_ref, acc_ref = refs
        a_blk = _pack_head_pairs(a_ref[...]).astype(a_ref.dtype) if packed_a else a_ref[...]
        part = lax.dot_general(a_blk, b_ref[...], dn, preferred_element_type=F32)

        def finish(total):
            if has_add:
                total = total + add_ref[...]
            if spread_out:
                total = _spread_head_pairs(total)
            o_ref[...] = total.astype(out_dtype)

        if nk == 1:
            finish(part)
        else:
            k = pl.program_id(2)

            @pl.when(k == 0)
            def _():
                acc_ref[...] = part

            @pl.when(k > 0)
            def _():
                acc_ref[...] += part

            @pl.when(k == nk - 1)
            def _():
                finish(acc_ref[...])

    wide = 2 if packed_a else 1
    a_tile = (tk, wide * tm) if trans_a else (tm, wide * tk)
    a_at = (lambda i, j, k: (k, i)) if trans_a else (lambda i, j, k: (i, k))
    if a_lead is None:
        a_spec = pl.BlockSpec(a_tile, a_at)
    elif a_lead == "k":
        per = slab_k // tk
        a_spec = pl.BlockSpec((None,) + a_tile, lambda i, j, k: (k // per, i, k % per))
    elif a_lead == "i":
        per = slab_m // tm
        a_spec = pl.BlockSpec((None,) + a_tile, lambda i, j, k: (i // per, k, i % per))
    else:
        a_spec = pl.BlockSpec((None,) + a_tile, lambda i, j, k: (a_lead,) + a_at(i, j, k))
    if trans_b:
        b_spec = pl.BlockSpec((tn, tk), lambda i, j, k: (j + b_off, k))
    else:
        b_spec = pl.BlockSpec((tk, tn), lambda i, j, k: (k + b_off, j))
    in_specs = [a_spec, b_spec]
    args = [a, b]
    if has_add:
        in_specs.append(pl.BlockSpec((tm, tn), lambda i, j, k: (i, j)))
        args.append(add)
    return pl.pallas_call(
        body,
        grid=(m_dim // tm, n_dim // tn, nk),
        in_specs=in_specs,
        out_specs=pl.BlockSpec((tm, (2 if spread_out else 1) * tn), lambda i, j, k: (i, j)),
        out_shape=jax.ShapeDtypeStruct((m_dim, (2 if spread_out else 1) * n_dim), out_dtype),
        scratch_shapes=[pltpu.VMEM((tm, tn) if nk > 1 else (8, LANES), F32)],
        compiler_params=_cparams(("parallel", "parallel", "arbitrary")),
        name=name,
    )(*args)


def _rmsnorm_fwd(x, w_row, *, name):
    t, d = x.shape
    tb = min(t, 1024)

    def body(x_ref, w_ref, o_ref):
        xf = x_ref[...]
        r = lax.rsqrt(jnp.mean(xf * xf, axis=-1, keepdims=True) + RMS_EPS)
        o_ref[...] = (xf * r * w_ref[...]).astype(o_ref.dtype)

    return pl.pallas_call(
        body,
        grid=(t // tb,),
        in_specs=[pl.BlockSpec((tb, d), lambda i: (i, 0)), pl.BlockSpec((1, d), lambda i: (0, 0))],
        out_specs=pl.BlockSpec((tb, d), lambda i: (i, 0)),
        out_shape=jax.ShapeDtypeStruct((t, d), MM_DTYPE),
        compiler_params=_cparams(("parallel",)),
        name=name,
    )(x, w_row)


def _rmsnorm_bwd(x, w_row, dy, dskip, *, name):
    t, d = x.shape
    tb = min(t, 512)

    def body(x_ref, w_ref, dy_ref, ds_ref, dx_ref, dxb_ref, dw_ref):
        xf = x_ref[...]
        g = dy_ref[...]
        r = lax.rsqrt(jnp.mean(xf * xf, axis=-1, keepdims=True) + RMS_EPS)
        gw = g * w_ref[...]
        proj = jnp.mean(gw * xf, axis=-1, keepdims=True)
        dx = r * gw - xf * (r * r * r * proj) + ds_ref[...]
        dx_ref[...] = dx
        dxb_ref[...] = dx.astype(dxb_ref.dtype)
        part = jnp.sum(g * xf * r, axis=0, keepdims=True)

        @pl.when(pl.program_id(0) == 0)
        def _():
            dw_ref[...] = part

        @pl.when(pl.program_id(0) > 0)
        def _():
            dw_ref[...] += part

    row = pl.BlockSpec((tb, d), lambda i: (i, 0))
    one = pl.BlockSpec((1, d), lambda i: (0, 0))
    return pl.pallas_call(
        body,
        grid=(t // tb,),
        in_specs=[row, one, row, row],
        out_specs=[row, row, one],
        out_shape=[jax.ShapeDtypeStruct((t, d), F32), jax.ShapeDtypeStruct((t, d), MM_DTYPE),
                   jax.ShapeDtypeStruct((1, d), F32)],
        compiler_params=_cparams(("arbitrary",)),
        name=name,
    )(x, w_row, dy, dskip)


def _silu(z):
    return z / (1.0 + jnp.exp(-z))


FFN_TM, FFN_TN = 512, 1408


def _ffn_in(hn, in_t, *, name):
    t, d = hn.shape
    h = FFN_HIDDEN
    tm, tn = min(t, FFN_TM), FFN_TN
    nj = h // tn
    dn = (((1,), (1,)), ((), ()))

    def body(a_ref, bg_ref, bu_ref, g_ref, u_ref, act_ref):
        a = a_ref[...]
        g = lax.dot_general(a, bg_ref[...], dn, preferred_element_type=F32)
        u = lax.dot_general(a, bu_ref[...], dn, preferred_element_type=F32)
        g_ref[...] = g.astype(g_ref.dtype)
        u_ref[...] = u.astype(u_ref.dtype)
        act_ref[...] = (_silu(g) * u).astype(act_ref.dtype)

    out = pl.BlockSpec((tm, tn), lambda j, i: (i, j))
    return pl.pallas_call(
        body,
        grid=(nj, t // tm),
        in_specs=[pl.BlockSpec((tm, d), lambda j, i: (i, 0)), pl.BlockSpec((tn, d), lambda j, i: (j, 0)),
                  pl.BlockSpec((tn, d), lambda j, i: (j + nj, 0))],
        out_specs=[out, out, out],
        out_shape=[jax.ShapeDtypeStruct((t, h), MM_DTYPE)] * 3,
        compiler_params=_cparams(("parallel", "parallel")),
        name=name,
    )(hn, in_t, in_t)


def _ffn_dact(dy, out_w, g, u, *, name):
    t, d = dy.shape
    h = FFN_HIDDEN
    tm, tn = min(t, FFN_TM), FFN_TN

    def body(a_ref, b_ref, g_ref, u_ref, d_ref):
        da = lax.dot_general(a_ref[...], b_ref[...], (((1,), (1,)), ((), ())), preferred_element_type=F32)
        gate = g_ref[...].astype(F32)
        sig = 1.0 / (1.0 + jnp.exp(-gate))
        sg = gate * sig
        d_ref[0] = (da * u_ref[...].astype(F32) * (sig + sg * (1.0 - sig))).astype(d_ref.dtype)
        d_ref[1] = (da * sg).astype(d_ref.dtype)

    blk = pl.BlockSpec((tm, tn), lambda j, i: (i, j))
    return pl.pallas_call(
        body,
        grid=(h // tn, t // tm),
        in_specs=[pl.BlockSpec((tm, d), lambda j, i: (i, 0)), pl.BlockSpec((tn, d), lambda j, i: (j, 0)), blk, blk],
        out_specs=pl.BlockSpec((2, tm, tn), lambda j, i: (0, i, j)),
        out_shape=jax.ShapeDtypeStruct((2, t, h), MM_DTYPE),
        compiler_params=_cparams(("parallel", "parallel")),
        name=name,
    )(dy, out_w, g, u)


def _loss_head(y, target, *, name):
    t, d = y.shape
    tb = min(t, 1024)

    def body(y_ref, t_ref, dy_ref, dyb_ref, l_ref):
        err = y_ref[...] - t_ref[...]
        dy_ref[...] = err * (1.0 / d)
        dyb_ref[...] = (err * (1.0 / d)).astype(dyb_ref.dtype)
        part = jnp.sum(jnp.sum(err * err, axis=0, keepdims=True), axis=1, keepdims=True) * (0.5 / d)
        part = jnp.broadcast_to(part, l_ref.shape)

        @pl.when(pl.program_id(0) == 0)
        def _():
            l_ref[...] = part

        @pl.when(pl.program_id(0) > 0)
        def _():
            l_ref[...] += part

    row = pl.BlockSpec((tb, d), lambda i: (i, 0))
    return pl.pallas_call(
        body,
        grid=(t // tb,),
        in_specs=[row, row],
        out_specs=[row, row, pl.BlockSpec((8, LANES), lambda i: (0, 0))],
        out_shape=[jax.ShapeDtypeStruct((t, d), F32), jax.ShapeDtypeStruct((t, d), MM_DTYPE),
                   jax.ShapeDtypeStruct((8, LANES), F32)],
        compiler_params=_cparams(("arbitrary",)),
        name=name,
    )(y, target)


CONV_HALO = 8
CONV_TIME_TILE = 2048


def _conv_tile_scale(c):
    is_qk = c < 2 * GDN_HEADS
    scale = jnp.where(c < GDN_HEADS, GDN_DK ** -0.5, 1.0).astype(F32)
    return is_qk, scale


def _gdn_conv_fwd(pm, conv_w, *, name):
    t = pm.shape[0]
    tb = min(t, CONV_TIME_TILE)
    nt = t // tb
    hb = tb // CONV_HALO

    def body(x_ref, xp_ref, w_ref, o_ref, xe_ref):
        c = pl.program_id(0)
        ti = pl.program_id(1)
        xe_ref[0:CONV_HALO, :] = jnp.where(ti > 0, xp_ref[...], 0.0)
        xe_ref[CONV_HALO:CONV_HALO + tb, :] = x_ref[...]
        w = w_ref[...]
        y = jnp.zeros((tb, LANES), F32)
        for j in range(GDN_CONV):
            off = CONV_HALO - (GDN_CONV - 1) + j
            y = y + w[j:j + 1, :] * xe_ref[pl.ds(off, tb), :]
        s = _silu(y)
        is_qk, scale = _conv_tile_scale(c)
        r = lax.rsqrt(jnp.sum(s * s, axis=-1, keepdims=True) + L2_EPS) * scale
        o_ref[...] = s * jnp.where(is_qk, r, 1.0)

    return pl.pallas_call(
        body,
        grid=(GDN_QKV // LANES, nt),
        in_specs=[
            pl.BlockSpec((tb, LANES), lambda c, i: (i, c)),
            pl.BlockSpec((CONV_HALO, LANES), lambda c, i: (jnp.maximum(i * hb - 1, 0), c)),
            pl.BlockSpec((GDN_CONV, LANES), lambda c, i: (0, c)),
        ],
        out_specs=pl.BlockSpec((tb, LANES), lambda c, i: (i, c)),
        out_shape=jax.ShapeDtypeStruct((t, GDN_QKV), F32),
        scratch_shapes=[pltpu.VMEM((tb + CONV_HALO, LANES), F32)],
        compiler_params=_cparams(("parallel", "parallel")),
        name=name,
    )(pm, pm, conv_w)


def _gdn_conv_bwd(pm, conv_w, dout, *, name):
    t = pm.shape[0]
    tb = min(t, CONV_TIME_TILE)
    nt = t // tb
    hb = tb // CONV_HALO
    last_hb = t // CONV_HALO - 1
    ext = tb + CONV_HALO

    def body(x_ref, xp_ref, xn_ref, d_ref, dn_ref, w_ref, dx_ref, dw_ref, xe_ref, dy_ref):
        c = pl.program_id(0)
        ti = pl.program_id(1)
        has_next = ti < nt - 1
        xe_ref[0:CONV_HALO, :] = jnp.where(ti > 0, xp_ref[...], 0.0)
        xe_ref[CONV_HALO:CONV_HALO + tb, :] = x_ref[...]
        xe_ref[CONV_HALO + tb:2 * CONV_HALO + tb, :] = jnp.where(has_next, xn_ref[...], 0.0)
        de = jnp.concatenate([d_ref[...], jnp.where(has_next, dn_ref[...], 0.0)], axis=0)
        w = w_ref[...]
        y = jnp.zeros((ext, LANES), F32)
        for j in range(GDN_CONV):
            off = CONV_HALO - (GDN_CONV - 1) + j
            y = y + w[j:j + 1, :] * xe_ref[pl.ds(off, ext), :]
        sig = 1.0 / (1.0 + jnp.exp(-y))
        s = y * sig
        is_qk, scale = _conv_tile_scale(c)
        r = lax.rsqrt(jnp.sum(s * s, axis=-1, keepdims=True) + L2_EPS)
        n = s * r
        dnrm = de * scale
        ds_qk = r * (dnrm - n * jnp.sum(dnrm * n, axis=-1, keepdims=True))
        ds = jnp.where(is_qk, ds_qk, de)
        dy_ref[...] = ds * (sig + s * (1.0 - sig))
        dy = dy_ref[0:tb, :]
        dx = jnp.zeros((tb, LANES), F32)
        dw_rows = []
        for j in range(GDN_CONV):
            sh = GDN_CONV - 1 - j
            dx = dx + w[j:j + 1, :] * dy_ref[pl.ds(sh, tb), :]
            off = CONV_HALO - (GDN_CONV - 1) + j
            dw_rows.append(jnp.sum(dy * xe_ref[pl.ds(off, tb), :], axis=0, keepdims=True))
        dx_ref[...] = dx.astype(dx_ref.dtype)
        part = jnp.concatenate(dw_rows, axis=0)

        @pl.when(ti == 0)
        def _():
            dw_ref[...] = part

        @pl.when(ti > 0)
        def _():
            dw_ref[...] += part

    main = pl.BlockSpec((tb, LANES), lambda c, i: (i, c))
    prev = pl.BlockSpec((CONV_HALO, LANES), lambda c, i: (jnp.maximum(i * hb - 1, 0), c))
    nxt = pl.BlockSpec((CONV_HALO, LANES), lambda c, i: (jnp.minimum((i + 1) * hb, last_hb), c))
    return pl.pallas_call(
        body,
        grid=(GDN_QKV // LANES, nt),
        in_specs=[main, prev, nxt, main, nxt, pl.BlockSpec((GDN_CONV, LANES), lambda c, i: (0, c))],
        out_specs=[main, pl.BlockSpec((GDN_CONV, LANES), lambda c, i: (0, c))],
        out_shape=[jax.ShapeDtypeStruct((t, GDN_QKV), MM_DTYPE), jax.ShapeDtypeStruct((GDN_CONV, GDN_QKV), F32)],
        scratch_shapes=[pltpu.VMEM((tb + 2 * CONV_HALO, LANES), F32), pltpu.VMEM((ext, LANES), F32)],
        compiler_params=_cparams(("parallel", "arbitrary")),
        name=name,
    )(pm, pm, pm, dout, dout, conv_w)


def _head_selector(first_col):
    row = lax.broadcasted_iota(jnp.int32, (LANES, GDN_HEADS * LANES), 0)
    col = lax.broadcasted_iota(jnp.int32, (LANES, GDN_HEADS * LANES), 1)
    return (col // LANES + first_col == row).astype(F32)


def _softplus(x):
    return jnp.maximum(x, 0.0) + jnp.log(1.0 + jnp.exp(-jnp.abs(x)))


def _gdn_gates_fwd(ab, alog_row, dt_row, *, name):
    t = ab.shape[0]
    tb = min(t, 1024)
    wide = GDN_HEADS * LANES

    def body(ab_ref, al_ref, dt_ref, g_ref, b_ref):
        x = ab_ref[...]
        g_cols = -jnp.exp(al_ref[...]) * _softplus(x + dt_ref[...])
        b_cols = 1.0 / (1.0 + jnp.exp(-x))
        g_ref[...] = _dot(g_cols, _head_selector(0))
        b_ref[...] = _dot(b_cols, _head_selector(GDN_HEADS))

    row = pl.BlockSpec((tb, LANES), lambda i: (i, 0))
    one = pl.BlockSpec((1, LANES), lambda i: (0, 0))
    out = pl.BlockSpec((tb, wide), lambda i: (i, 0))
    return pl.pallas_call(
        body,
        grid=(t // tb,),
        in_specs=[row, one, one],
        out_specs=[out, out],
        out_shape=[jax.ShapeDtypeStruct((t, wide), F32)] * 2,
        compiler_params=_cparams(("parallel",)),
        name=name,
    )(ab, alog_row, dt_row)


def _gdn_gates_bwd(ab, alog_row, dt_row, dgb, dbb, *, name):
    t = ab.shape[0]
    tb = min(t, 1024)
    wide = GDN_HEADS * LANES

    def body(ab_ref, al_ref, dt_ref, dg_ref, db_ref, dab_ref, dal_ref, ddt_ref):
        x = ab_ref[...]
        lane = lax.broadcasted_iota(jnp.int32, (tb, LANES), 1)
        dg_cols = _dot_nt(dg_ref[...], _head_selector(0))
        db_cols = _dot_nt(db_ref[...], _head_selector(GDN_HEADS))
        ea = jnp.exp(al_ref[...])
        z = x + dt_ref[...]
        sp = _softplus(z)
        sg = 1.0 / (1.0 + jnp.exp(-z))
        beta = 1.0 / (1.0 + jnp.exp(-x))
        da = jnp.where(lane < GDN_HEADS, dg_cols * (-ea) * sg, 0.0)
        db = jnp.where((lane >= GDN_HEADS) & (lane < 2 * GDN_HEADS), db_cols * beta * (1.0 - beta), 0.0)
        dab_ref[...] = (da + db).astype(dab_ref.dtype)
        p_al = jnp.sum(jnp.where(lane < GDN_HEADS, dg_cols * (-ea) * sp, 0.0), axis=0, keepdims=True)
        p_dt = jnp.sum(da, axis=0, keepdims=True)

        @pl.when(pl.program_id(0) == 0)
        def _():
            dal_ref[...] = p_al
            ddt_ref[...] = p_dt

        @pl.when(pl.program_id(0) > 0)
        def _():
            dal_ref[...] += p_al
            ddt_ref[...] += p_dt

    row = pl.BlockSpec((tb, LANES), lambda i: (i, 0))
    one = pl.BlockSpec((1, LANES), lambda i: (0, 0))
    big = pl.BlockSpec((tb, wide), lambda i: (i, 0))
    return pl.pallas_call(
        body,
        grid=(t // tb,),
        in_specs=[row, one, one, big, big],
        out_specs=[row, one, one],
        out_shape=[jax.ShapeDtypeStruct((t, LANES), MM_DTYPE), jax.ShapeDtypeStruct((1, LANES), F32),
                   jax.ShapeDtypeStruct((1, LANES), F32)],
        compiler_params=_cparams(("arbitrary",)),
        name=name,
    )(ab, alog_row, dt_row, dgb, dbb)


@jax.custom_vjp
def _unit_lower_inverse_rest(n):
    c = n.shape[-1]
    ri = lax.broadcasted_iota(jnp.int32, (c, c), 0)
    ci = lax.broadcasted_iota(jnp.int32, (c, c), 1)
    rest = None
    size = 1
    while size < c:
        joins = ((ri // (2 * size)) == (ci // (2 * size))) & ((ri // size) != (ci // size))
        low = jnp.where(joins, n, 0.0)
        if rest is None:
            rest = -low
        else:
            left = low + _bdot(rest, low)
            rest = rest - (left + _bdot(left, rest))
        size *= 2
    return rest


def _unit_lower_inverse_rest_fwd(n):
    rest = _unit_lower_inverse_rest(n)
    return rest, rest


def _unit_lower_inverse_rest_bwd(rest, ct):
    left = ct + _bdot_tn(rest, ct)
    return (-(left + _bdot_nt(left, rest)),)


_unit_lower_inverse_rest.defvjp(_unit_lower_inverse_rest_fwd, _unit_lower_inverse_rest_bwd)


@jax.custom_vjp
def _known_inverse_rest(n, rest):
    return rest


def _known_inverse_rest_fwd(n, rest):
    return rest, rest


def _known_inverse_rest_bwd(rest, ct):
    return _unit_lower_inverse_rest_bwd(rest, ct) + (jnp.zeros_like(rest),)


_known_inverse_rest.defvjp(_known_inverse_rest_fwd, _known_inverse_rest_bwd)


def _bf16_pieces(x):
    hi = x.astype(BF16)
    r1 = x - hi.astype(F32)
    mid = r1.astype(BF16)
    lo = (r1 - mid.astype(F32)).astype(BF16)
    return hi, mid, lo


def _lower_ones(shape):
    c = shape[-1]
    ri = lax.broadcasted_iota(jnp.int32, (c, c), 0)
    ci = lax.broadcasted_iota(jnp.int32, (c, c), 1)
    return jnp.broadcast_to((ri >= ci).astype(BF16), shape)


@jax.custom_vjp
def _running_sum(x):
    tri = _lower_ones(x.shape)
    return sum(_bdot(tri, p) for p in _bf16_pieces(x))


def _running_sum_fwd(x):
    return _running_sum(x), None


def _running_sum_bwd(_, ct):
    tri = _lower_ones(ct.shape)
    return (sum(_bdot_tn(tri, p) for p in _bf16_pieces(ct)),)


_running_sum.defvjp(_running_sum_fwd, _running_sum_bwd)


def _gdn_prep_math(q, k, v, gb, bb, known_rest=None, with_rest=False):
    c = GDN_CHUNK
    ri = lax.broadcasted_iota(jnp.int32, (c, c), 0)
    ci = lax.broadcasted_iota(jnp.int32, (c, c), 1)
    causal = ri >= ci
    gc = _running_sum(gb)
    decay = jnp.exp(jnp.where(causal, gc - jnp.swapaxes(gc, -1, -2), NEG))
    n = jnp.where(ri > ci, _bdot_nt(k, k) * bb * decay, 0.0)
    rest = _unit_lower_inverse_rest(n) if known_rest is None else _known_inverse_rest(n, known_rest)
    eg = jnp.exp(gc)
    rhs_v = v * bb
    rhs_k = k * bb * eg
    u = rhs_v + _bdot(rest, rhs_v)
    w = rhs_k + _bdot(rest, rhs_k)
    qk = _bdot_nt(q, k) * decay
    qd = q * eg
    last = jnp.sum(jnp.where(ri == c - 1, gc, 0.0), axis=-2, keepdims=True)
    gl = jnp.broadcast_to(last, gc.shape)
    kt = k * jnp.exp(gl - gc)
    cd = jnp.exp(gl)
    return (u, w, qk, qd, kt, cd, rest) if with_rest else (u, w, qk, qd, kt, cd)


def _head_tiles(ref, h):
    return ref[:, h * LANES:(h + 1) * LANES]


def _stack_heads(ref, first=0, heads=GDN_HEADS):
    return jnp.stack([_head_tiles(ref, first + h) for h in range(heads)])


def _store_heads(ref, val, first=0):
    for h in range(val.shape[0]):
        ref[:, (first + h) * LANES:(first + h + 1) * LANES] = val[h].astype(ref.dtype)


def _gdn_prep_fwd(qkv, gb, bb, *, name):
    t = qkv.shape[0]
    c = GDN_CHUNK
    wide = GDN_HEADS * LANES

    def body(q_ref, k_ref, v_ref, g_ref, b_ref, *outs):
        res = _gdn_prep_math(*(_stack_heads(r) for r in (q_ref, k_ref, v_ref, g_ref, b_ref)), with_rest=True)
        for o_ref, val in zip(outs, res):
            _store_heads(o_ref, val)

    blk = lambda off: pl.BlockSpec((c, wide), lambda i: (i, off))
    outs = pl.pallas_call(
        body,
        grid=(t // c,),
        in_specs=[blk(0), blk(1), blk(2), blk(0), blk(0)],
        out_specs=[blk(0)] * 7,
        out_shape=[jax.ShapeDtypeStruct((t, wide), dt) for dt in (F32, MM_DTYPE, MM_DTYPE, MM_DTYPE, MM_DTYPE, F32, F32)],
        compiler_params=_cparams(("parallel",)),
        name=name,
    )(qkv, qkv, qkv, gb, bb)
    return tuple(outs[:6]), outs[6]


def _gdn_prep_bwd(qkv, gb, bb, rest, cts, *, name):
    t = qkv.shape[0]
    c = GDN_CHUNK
    wide = GDN_HEADS * LANES

    def body(q_ref, k_ref, v_ref, g_ref, b_ref, r_ref, c0, c1, c2, c3, c4, c5, dqkv_ref, dg_ref, db_ref):
        prim = tuple(_stack_heads(r) for r in (q_ref, k_ref, v_ref, g_ref, b_ref))
        _, pull = jax.vjp(functools.partial(_gdn_prep_math, known_rest=_stack_heads(r_ref)), *prim)
        dq, dk, dv, dg, db = pull(tuple(_stack_heads(r) for r in (c0, c1, c2, c3, c4, c5)))
        _store_heads(dqkv_ref, dq)
        _store_heads(dqkv_ref, dk, first=GDN_HEADS)
        _store_heads(dqkv_ref, dv, first=2 * GDN_HEADS)
        _store_heads(dg_ref, dg)
        _store_heads(db_ref, db)

    blk = lambda off: pl.BlockSpec((c, wide), lambda i: (i, off))
    return pl.pallas_call(
        body,
        grid=(t // c,),
        in_specs=[blk(0), blk(1), blk(2), blk(0), blk(0)] + [blk(0)] * 7,
        out_specs=[pl.BlockSpec((c, 3 * wide), lambda i: (i, 0)), blk(0), blk(0)],
        out_shape=[jax.ShapeDtypeStruct((t, 3 * wide), F32), jax.ShapeDtypeStruct((t, wide), F32),
                   jax.ShapeDtypeStruct((t, wide), F32)],
        compiler_params=_cparams(("parallel",)),
        name=name,
    )(qkv, qkv, qkv, gb, bb, rest, *cts)


def _gdn_scan_math(s, u, w, qk, qd, kt, cd):
    v_new = u - _bdot(w, s)
    o = _bdot(qd, s) + _bdot(qk, v_new)
    s_new = s * cd + _bdot_tn(kt, v_new)
    return o, s_new


def _gdn_scan_fwd(prep, *, name):
    t = prep[0].shape[0]
    c = GDN_CHUNK
    wide = GDN_HEADS * LANES

    def body(u_ref, w_ref, qk_ref, qd_ref, kt_ref, cd_ref, o_ref, st_ref, s_ref):
        @pl.when(pl.program_id(0) == 0)
        def _():
            s_ref[...] = jnp.zeros_like(s_ref)

        s = _stack_heads(s_ref)
        _store_heads(st_ref, s)
        o, s_new = _gdn_scan_math(s, *(_stack_heads(r).astype(F32) for r in (u_ref, w_ref, qk_ref, qd_ref, kt_ref, cd_ref)))
        _store_heads(o_ref, o)
        _store_heads(s_ref, s_new)

    blk = pl.BlockSpec((c, wide), lambda i: (i, 0))
    return pl.pallas_call(
        body,
        grid=(t // c,),
        in_specs=[blk] * 6,
        out_specs=[blk, blk],
        out_shape=[jax.ShapeDtypeStruct((t, wide), F32)] * 2,
        scratch_shapes=[pltpu.VMEM((GDN_DK, wide), F32)],
        compiler_params=_cparams(("arbitrary",)),
        name=name,
    )(*prep)


def _gdn_scan_bwd(prep, states, do, *, name):
    t = do.shape[0]
    c = GDN_CHUNK
    wide = GDN_HEADS * LANES
    nc = t // c

    def body(u_ref, w_ref, qk_ref, qd_ref, kt_ref, cd_ref, st_ref, do_ref, *rest):
        outs, ds_ref = rest[:6], rest[6]

        @pl.when(pl.program_id(0) == 0)
        def _():
            ds_ref[...] = jnp.zeros_like(ds_ref)

        prim = tuple(_stack_heads(r).astype(F32) for r in (st_ref, u_ref, w_ref, qk_ref, qd_ref, kt_ref, cd_ref))
        _, pull = jax.vjp(_gdn_scan_math, *prim)
        grads = pull((_stack_heads(do_ref), _stack_heads(ds_ref)))
        _store_heads(ds_ref, grads[0])
        for o_ref, val in zip(outs, grads[1:]):
            _store_heads(o_ref, val)

    blk = pl.BlockSpec((c, wide), lambda i: (nc - 1 - i, 0))
    return pl.pallas_call(
        body,
        grid=(nc,),
        in_specs=[blk] * 8,
        out_specs=[blk] * 6,
        out_shape=[jax.ShapeDtypeStruct((t, wide), F32)] * 6,
        scratch_shapes=[pltpu.VMEM((GDN_DK, wide), F32)],
        compiler_params=_cparams(("arbitrary",)),
        name=name,
    )(*prep, states, do)


def _gdn_outgate_math(o, z, nw):
    r = lax.rsqrt(jnp.mean(o * o, axis=-1, keepdims=True) + RMS_EPS)
    return o * r * nw * _silu(z)


def _gdn_outgate_fwd(o, pm, nw_row, *, name):
    t = o.shape[0]
    tb = min(t, ROW_TILE)
    wide = GDN_HEADS * LANES
    z_at = GDN_QKV // wide

    def body(o_ref, z_ref, nw_ref, y_ref):
        for h in range(GDN_HEADS):
            y = _gdn_outgate_math(_head_tiles(o_ref, h), _head_tiles(z_ref, h), nw_ref[...])
            y_ref[:, h * LANES:(h + 1) * LANES] = y.astype(y_ref.dtype)

    return pl.pallas_call(
        body,
        grid=(t // tb,),
        in_specs=[pl.BlockSpec((tb, wide), lambda i: (i, 0)), pl.BlockSpec((tb, wide), lambda i: (i, z_at)),
                  pl.BlockSpec((1, LANES), lambda i: (0, 0))],
        out_specs=pl.BlockSpec((tb, wide), lambda i: (i, 0)),
        out_shape=jax.ShapeDtypeStruct((t, wide), MM_DTYPE),
        compiler_params=_cparams(("parallel",)),
        name=name,
    )(o, pm, nw_row)


def _gdn_outgate_bwd(o, pm, nw_row, dy, *, name):
    t = o.shape[0]
    tb = min(t, ROW_TILE)
    wide = GDN_HEADS * LANES
    z_at = GDN_QKV // wide

    def body(o_ref, z_ref, nw_ref, dy_ref, do_ref, dz_ref, dnw_ref):
        total = jnp.zeros((1, LANES), F32)
        for h in range(GDN_HEADS):
            _, pull = jax.vjp(_gdn_outgate_math, _head_tiles(o_ref, h), _head_tiles(z_ref, h), nw_ref[...])
            d_o, d_z, d_nw = pull(_head_tiles(dy_ref, h))
            do_ref[:, h * LANES:(h + 1) * LANES] = d_o
            dz_ref[:, h * LANES:(h + 1) * LANES] = d_z.astype(dz_ref.dtype)
            total = total + d_nw

        @pl.when(pl.program_id(0) == 0)
        def _():
            dnw_ref[...] = total

        @pl.when(pl.program_id(0) > 0)
        def _():
            dnw_ref[...] += total

    blk = pl.BlockSpec((tb, wide), lambda i: (i, 0))
    one = pl.BlockSpec((1, LANES), lambda i: (0, 0))
    return pl.pallas_call(
        body,
        grid=(t // tb,),
        in_specs=[blk, pl.BlockSpec((tb, wide), lambda i: (i, z_at)), one, blk],
        out_specs=[blk, blk, one],
        out_shape=[jax.ShapeDtypeStruct((t, wide), F32), jax.ShapeDtypeStruct((t, wide), MM_DTYPE),
                   jax.ShapeDtypeStruct((1, LANES), F32)],
        compiler_params=_cparams(("arbitrary",)),
        name=name,
    )(o, pm, nw_row, dy)


def _rms64(x, w_row):
    return x * lax.rsqrt(jnp.sum(x * x, axis=-1, keepdims=True) * (1.0 / DIL_DH) + RMS_EPS) * w_row


def _alibi_slopes(group):
    head = lax.broadcasted_iota(jnp.int32, (DIL_HEADS, 8, LANES), 0).astype(F32)
    rate = -math.log(2.0) * ALIBI_MAX_BIAS / (len(DIL_GROUPS) * DIL_HEADS)
    slope = jnp.exp(rate * (head + float(group * DIL_HEADS + 1)))
    return jnp.broadcast_to(slope[:, 0:1, :], (DIL_HEADS, DIL_SPAN, LANES))


def _band_logits(qn, kp, kc, slope_d, has_prev):
    qi = lax.broadcasted_iota(jnp.int32, (DIL_SPAN, DIL_SPAN), 0)
    kj = lax.broadcasted_iota(jnp.int32, (DIL_SPAN, DIL_SPAN), 1)
    steps_c = (qi - kj).astype(F32)
    scale = DIL_DH ** -0.5
    sp = _bdot_nt(qn, kp) * scale - slope_d * (steps_c + float(DIL_SPAN))
    sc = _bdot_nt(qn, kc) * scale - slope_d * steps_c
    sp = jnp.where((kj >= qi) & has_prev, sp, NEG)
    sc = jnp.where(kj <= qi, sc, NEG)
    return sp, sc


def _dil_attn_fwd(slab, wq_row, wk_row, *, group, name):
    dilation = DIL_GROUPS[group][1]
    t = slab.shape[0]
    rows = t // dilation
    nlb = rows // DIL_SPAN
    wide = DIL_HEADS * LANES
    view = slab.reshape(rows, dilation * DIL_SLAB)

    def body(q_ref, kc_ref, vc_ref, kp_ref, vp_ref, wq_ref, wk_ref, o_ref):
        has_prev = pl.program_id(1) > 0
        lane = lax.broadcasted_iota(jnp.int32, (DIL_SPAN, LANES), 1)
        qn = _rms64(_stack_heads(q_ref), wq_ref[...])
        kc = _rms64(_stack_heads(kc_ref), wk_ref[...])
        kp = _rms64(_stack_heads(kp_ref), wk_ref[...])
        sp, sc = _band_logits(qn, kp, kc, _alibi_slopes(group) * float(dilation), has_prev)
        m = jnp.maximum(jnp.max(sp, axis=-1, keepdims=True), jnp.max(sc, axis=-1, keepdims=True))
        pp = jnp.exp(sp - m)
        pc = jnp.exp(sc - m)
        l = jnp.sum(pp, axis=-1, keepdims=True) + jnp.sum(pc, axis=-1, keepdims=True)
        o = (_bdot(pp, _stack_heads(vp_ref)) + _bdot(pc, _stack_heads(vc_ref))) / l
        _store_heads(o_ref, jnp.where(lane < DIL_DH, o, m + jnp.log(l)))

    cur = lambda part: pl.BlockSpec((DIL_SPAN, wide), lambda r, i: (i, 3 * r + part))
    prv = lambda part: pl.BlockSpec((DIL_SPAN, wide), lambda r, i: (jnp.maximum(i - 1, 0), 3 * r + part))
    one = pl.BlockSpec((1, LANES), lambda r, i: (0, 0))
    out = pl.pallas_call(
        body,
        grid=(dilation, nlb),
        in_specs=[cur(0), cur(1), cur(2), prv(1), prv(2), one, one],
        out_specs=pl.BlockSpec((DIL_SPAN, wide), lambda r, i: (i, r)),
        out_shape=jax.ShapeDtypeStruct((rows, dilation * wide), F32),
        compiler_params=_cparams(("parallel", "parallel")),
        name=name,
    )(view, view, view, view, view, wq_row, wk_row)
    return out.reshape(t, wide)


def _head_slope(group, head):
    idx = jnp.zeros((8, LANES), F32) + head.astype(F32)
    rate = -math.log(2.0) * ALIBI_MAX_BIAS / (len(DIL_GROUPS) * DIL_HEADS)
    slope = jnp.exp(rate * (idx + float(group * DIL_HEADS + 1)))
    return jnp.broadcast_to(slope[0:1, :], (DIL_SPAN, LANES))


def _take_residues(ref, d):
    return jnp.stack([ref[pl.ds(r, DIL_SPAN, stride=d), :] for r in range(d)])


def _put_residues(ref, val, d):
    for r in range(d):
        ref[pl.ds(r, DIL_SPAN, stride=d), :] = val[r]


def _dil_attn_fwd_strided(slab, wq_row, wk_row, *, group, name):
    d = DIL_GROUPS[group][1]
    t = slab.shape[0]
    span = DIL_SPAN * d
    nsb = t // span

    def body(q_ref, kc_ref, vc_ref, kp_ref, vp_ref, wq_ref, wk_ref, o_ref):
        has_prev = pl.program_id(0) > 0
        lane = lax.broadcasted_iota(jnp.int32, (DIL_SPAN, LANES), 1)
        qn = _rms64(_take_residues(q_ref, d), wq_ref[...])
        kc = _rms64(_take_residues(kc_ref, d), wk_ref[...])
        kp = _rms64(_take_residues(kp_ref, d), wk_ref[...])
        sp, sc = _band_logits(qn, kp, kc, _head_slope(group, pl.program_id(1)) * float(d), has_prev)
        m = jnp.maximum(jnp.max(sp, axis=-1, keepdims=True), jnp.max(sc, axis=-1, keepdims=True))
        pp = jnp.exp(sp - m)
        pc = jnp.exp(sc - m)
        l = jnp.sum(pp, axis=-1, keepdims=True) + jnp.sum(pc, axis=-1, keepdims=True)
        o = (_bdot(pp, _take_residues(vp_ref, d)) + _bdot(pc, _take_residues(vc_ref, d))) / l
        _put_residues(o_ref, jnp.where(lane < DIL_DH, o, m + jnp.log(l)), d)

    cur = lambda part: pl.BlockSpec((span, LANES), lambda i, h: (i, part * DIL_HEADS + h))
    prv = lambda part: pl.BlockSpec((span, LANES), lambda i, h: (jnp.maximum(i - 1, 0), part * DIL_HEADS + h))
    one = pl.BlockSpec((1, LANES), lambda i, h: (0, 0))
    return pl.pallas_call(
        body,
        grid=(nsb, DIL_HEADS),
        in_specs=[cur(0), cur(1), cur(2), prv(1), prv(2), one, one],
        out_specs=pl.BlockSpec((span, LANES), lambda i, h: (i, h)),
        out_shape=jax.ShapeDtypeStruct((t, DIL_HEADS * LANES), F32),
        compiler_params=_cparams(("parallel", "parallel")),
        name=name,
    )(slab, slab, slab, slab, slab, wq_row, wk_row)


def _dil_attn_bwd_strided(slab, stat, wq_row, wk_row, dwq_in, dwk_in, *, group, name):
    d = DIL_GROUPS[group][1]
    t = slab.shape[0]
    span = DIL_SPAN * d
    nsb = t // span

    def body(q_ref, kc_ref, vc_ref, kp_ref, vp_ref, st_ref, wq_ref, wk_ref, dwq_in_ref, dwk_in_ref,
             d_ref, dwq_ref, dwk_ref, dk_carry, dv_carry, spread):
        step = pl.program_id(1)
        has_prev = step < nsb - 1
        first = (pl.program_id(0) == 0) & (step == 0)

        @pl.when(step == 0)
        def _():
            dk_carry[...] = jnp.zeros_like(dk_carry)
            dv_carry[...] = jnp.zeros_like(dv_carry)

        @pl.when(first)
        def _():
            dwq_ref[...] = dwq_in_ref[...]
            dwk_ref[...] = dwk_in_ref[...]

        lane = lax.broadcasted_iota(jnp.int32, (DIL_SPAN, LANES), 1)
        scale = DIL_DH ** -0.5
        q_raw = _take_residues(q_ref, d)
        kc_raw = _take_residues(kc_ref, d)
        vc = _take_residues(vc_ref, d)
        kp_raw = _take_residues(kp_ref, d)
        vp = _take_residues(vp_ref, d)
        st = _take_residues(st_ref, d)
        d_o = jnp.where(lane < DIL_DH, st, 0.0)
        lse = jnp.sum(jnp.where(lane == DIL_DH, st, 0.0), axis=-1, keepdims=True)
        delta = jnp.sum(jnp.where(lane == DIL_DH + 1, st, 0.0), axis=-1, keepdims=True)
        qn = _rms64(q_raw, wq_ref[...])
        kc = _rms64(kc_raw, wk_ref[...])
        kp = _rms64(kp_raw, wk_ref[...])
        sp, sc = _band_logits(qn, kp, kc, _head_slope(group, pl.program_id(0)) * float(d), has_prev)
        pp = jnp.exp(sp - lse)
        pc = jnp.exp(sc - lse)
        dsp = pp * (_bdot_nt(d_o, vp) - delta) * scale
        dsc = pc * (_bdot_nt(d_o, vc) - delta) * scale
        dqn = _bdot(dsp, kp) + _bdot(dsc, kc)
        dkc_n = _bdot_tn(dsc, qn) + dk_carry[...]
        dvc = _bdot_tn(pc, d_o) + dv_carry[...]
        dk_carry[...] = _bdot_tn(dsp, qn)
        dv_carry[...] = _bdot_tn(pp, d_o)
        dq_raw, dwq_rows = _rms64_bwd(q_raw, wq_ref[...], dqn)
        dk_raw, dwk_rows = _rms64_bwd(kc_raw, wk_ref[...], dkc_n)
        for part, val in enumerate((dq_raw, dk_raw, dvc)):
            _put_residues(spread, val, d)
            d_ref[part] = spread[...].astype(d_ref.dtype)
        dwq_ref[...] += jnp.sum(jnp.sum(dwq_rows, axis=0), axis=0, keepdims=True)
        dwk_ref[...] += jnp.sum(jnp.sum(dwk_rows, axis=0), axis=0, keepdims=True)

    at = lambda i: nsb - 1 - i
    cur = lambda part: pl.BlockSpec((span, LANES), lambda h, i: (at(i), part * DIL_HEADS + h))
    prv = lambda part: pl.BlockSpec((span, LANES), lambda h, i: (jnp.maximum(at(i) - 1, 0), part * DIL_HEADS + h))
    one = pl.BlockSpec((1, LANES), lambda h, i: (0, 0))
    return pl.pallas_call(
        body,
        grid=(DIL_HEADS, nsb),
        in_specs=[cur(0), cur(1), cur(2), prv(1), prv(2), pl.BlockSpec((span, LANES), lambda h, i: (at(i), h)),
                  one, one, one, one],
        out_specs=[pl.BlockSpec((3, span, LANES), lambda h, i: (0, at(i), h)), one, one],
        out_shape=[jax.ShapeDtypeStruct((3, t, DIL_HEADS * LANES), MM_DTYPE), jax.ShapeDtypeStruct((1, LANES), F32),
                   jax.ShapeDtypeStruct((1, LANES), F32)],
        scratch_shapes=[pltpu.VMEM((d, DIL_SPAN, LANES), F32), pltpu.VMEM((d, DIL_SPAN, LANES), F32),
                        pltpu.VMEM((span, LANES), F32)],
        compiler_params=_cparams(("arbitrary", "arbitrary")),
        name=name,
    )(slab, slab, slab, slab, slab, stat, wq_row, wk_row, dwq_in, dwk_in)


def _dil_merge_fwd(oe, *, name):
    t = oe[0].shape[0]
    tb = min(t, ROW_TILE)
    wide = DIL_HEADS * LANES

    def body(e0, e1, e2, y_ref, om_ref):
        lane = lax.broadcasted_iota(jnp.int32, (tb, LANES), 1)
        for h in range(DIL_HEADS):
            es = [_head_tiles(e, h) for e in (e0, e1, e2)]
            lse = [jnp.sum(jnp.where(lane == DIL_DH, e, 0.0), axis=-1, keepdims=True) for e in es]
            top = jnp.maximum(jnp.maximum(lse[0], lse[1]), lse[2])
            joint = top + jnp.log(jnp.exp(lse[0] - top) + jnp.exp(lse[1] - top) + jnp.exp(lse[2] - top))
            o = sum(jnp.exp(l - joint) * e for l, e in zip(lse, es))
            y_ref[:, h * LANES:(h + 1) * LANES] = jnp.where(lane < DIL_DH, o, 0.0).astype(y_ref.dtype)
            om_ref[:, h * LANES:(h + 1) * LANES] = jnp.where(lane < DIL_DH, o, joint)

    blk = pl.BlockSpec((tb, wide), lambda i: (i, 0))
    return pl.pallas_call(
        body,
        grid=(t // tb,),
        in_specs=[blk] * 3,
        out_specs=[blk, blk],
        out_shape=[jax.ShapeDtypeStruct((t, wide), MM_DTYPE), jax.ShapeDtypeStruct((t, wide), F32)],
        compiler_params=_cparams(("parallel",)),
        name=name,
    )(*oe)


def _dil_merge_bwd(dy, om, *, name):
    t = dy.shape[0]
    tb = min(t, ROW_TILE)
    wide = DIL_HEADS * LANES

    def body(dy_ref, om_ref, st_ref):
        lane = lax.broadcasted_iota(jnp.int32, (tb, LANES), 1)
        for h in range(DIL_HEADS):
            d_o = jnp.where(lane < DIL_DH, _head_tiles(dy_ref, h), 0.0)
            om_t = _head_tiles(om_ref, h)
            delta = jnp.sum(d_o * om_t, axis=-1, keepdims=True)
            st_ref[:, h * LANES:(h + 1) * LANES] = jnp.where(
                lane < DIL_DH, d_o, jnp.where(lane == DIL_DH, om_t, jnp.where(lane == DIL_DH + 1, delta, 0.0)))

    blk = pl.BlockSpec((tb, wide), lambda i: (i, 0))
    return pl.pallas_call(
        body,
        grid=(t // tb,),
        in_specs=[blk, blk],
        out_specs=blk,
        out_shape=jax.ShapeDtypeStruct((t, wide), F32),
        compiler_params=_cparams(("parallel",)),
        name=name,
    )(dy, om)


def _rms64_bwd(x, w_row, dy):
    r = lax.rsqrt(jnp.sum(x * x, axis=-1, keepdims=True) * (1.0 / DIL_DH) + RMS_EPS)
    gw = dy * w_row
    dx = r * gw - x * (r * r * r * jnp.sum(gw * x, axis=-1, keepdims=True) * (1.0 / DIL_DH))
    return dx, dy * x * r


def _dil_attn_bwd(slab, stat, wq_row, wk_row, dwq_in, dwk_in, *, group, name):
    dilation = DIL_GROUPS[group][1]
    t = slab.shape[0]
    rows = t // dilation
    nlb = rows // DIL_SPAN
    wide = DIL_HEADS * LANES
    view = slab.reshape(rows, dilation * DIL_SLAB)
    stat_view = stat.reshape(rows, dilation * wide)

    def body(cur_ref, kp_ref, vp_ref, st_ref, wq_ref, wk_ref, dwq_in_ref, dwk_in_ref, d_ref, dwq_ref, dwk_ref,
             dk_carry, dv_carry):
        step = pl.program_id(1)
        has_prev = step < nlb - 1
        first = (pl.program_id(0) == 0) & (step == 0)

        @pl.when(step == 0)
        def _():
            dk_carry[...] = jnp.zeros_like(dk_carry)
            dv_carry[...] = jnp.zeros_like(dv_carry)

        @pl.when(first)
        def _():
            dwq_ref[...] = dwq_in_ref[...]
            dwk_ref[...] = dwk_in_ref[...]

        lane = lax.broadcasted_iota(jnp.int32, (DIL_SPAN, LANES), 1)
        scale = DIL_DH ** -0.5
        q_raw = _stack_heads(cur_ref)
        kc_raw = _stack_heads(cur_ref, first=DIL_HEADS)
        vc = _stack_heads(cur_ref, first=2 * DIL_HEADS)
        kp_raw = _stack_heads(kp_ref)
        vp = _stack_heads(vp_ref)
        st = _stack_heads(st_ref)
        d_o = jnp.where(lane < DIL_DH, st, 0.0)
        lse = jnp.sum(jnp.where(lane == DIL_DH, st, 0.0), axis=-1, keepdims=True)
        delta = jnp.sum(jnp.where(lane == DIL_DH + 1, st, 0.0), axis=-1, keepdims=True)
        qn = _rms64(q_raw, wq_ref[...])
        kc = _rms64(kc_raw, wk_ref[...])
        kp = _rms64(kp_raw, wk_ref[...])
        sp, sc = _band_logits(qn, kp, kc, _alibi_slopes(group) * float(dilation), has_prev)
        pp = jnp.exp(sp - lse)
        pc = jnp.exp(sc - lse)
        dsp = pp * (_bdot_nt(d_o, vp) - delta) * scale
        dsc = pc * (_bdot_nt(d_o, vc) - delta) * scale
        dqn = _bdot(dsp, kp) + _bdot(dsc, kc)
        dkc_n = _bdot_tn(dsc, qn) + _stack_heads(dk_carry)
        dvc = _bdot_tn(pc, d_o) + _stack_heads(dv_carry)
        _store_heads(dk_carry, _bdot_tn(dsp, qn))
        _store_heads(dv_carry, _bdot_tn(pp, d_o))
        dq_raw, dwq_rows = _rms64_bwd(q_raw, wq_ref[...], dqn)
        dk_raw, dwk_rows = _rms64_bwd(kc_raw, wk_ref[...], dkc_n)
        _store_heads(d_ref, dq_raw)
        _store_heads(d_ref, dk_raw, first=DIL_HEADS)
        _store_heads(d_ref, dvc, first=2 * DIL_HEADS)
        dwq_ref[...] += jnp.sum(jnp.sum(dwq_rows, axis=0), axis=0, keepdims=True)
        dwk_ref[...] += jnp.sum(jnp.sum(dwk_rows, axis=0), axis=0, keepdims=True)

    blk_i = lambda i: nlb - 1 - i
    cur = pl.BlockSpec((DIL_SPAN, DIL_SLAB), lambda r, i: (blk_i(i), r))
    prv = lambda part: pl.BlockSpec((DIL_SPAN, wide), lambda r, i: (jnp.maximum(blk_i(i) - 1, 0), 3 * r + part))
    one = pl.BlockSpec((1, LANES), lambda r, i: (0, 0))
    dslab, dwq, dwk = pl.pallas_call(
        body,
        grid=(dilation, nlb),
        in_specs=[cur, prv(1), prv(2), pl.BlockSpec((DIL_SPAN, wide), lambda r, i: (blk_i(i), r)), one, one, one, one],
        out_specs=[cur, one, one],
        out_shape=[jax.ShapeDtypeStruct((rows, dilation * DIL_SLAB), MM_DTYPE), jax.ShapeDtypeStruct((1, LANES), F32),
                   jax.ShapeDtypeStruct((1, LANES), F32)],
        scratch_shapes=[pltpu.VMEM((DIL_SPAN, wide), F32), pltpu.VMEM((DIL_SPAN, wide), F32)],
        compiler_params=_cparams(("arbitrary", "arbitrary")),
        name=name,
    )(view, view, view, stat_view, wq_row, wk_row, dwq_in, dwk_in)
    return dslab.reshape(t, DIL_SLAB), dwq, dwk


def _row(v, width=LANES):
    v = v.astype(F32).reshape(-1)
    return jnp.pad(v, (0, width - v.shape[0])).reshape(1, width)


def _prepare_weights(w):
    return dict(gdn=_prepare_gdn(w), dil=_prepare_dil(w), ffn=_prepare_ffn(w))


def _prepare_gdn(w, layers=range(DEPTH // 2)):
    gdn = {}
    for j in layers:
        wt = w["gdn_w_in"][j]
        gates_t = jnp.pad(wt[GDN_MAIN:], ((0, LANES - 2 * GDN_HEADS), (0, 0)))
        gdn[j] = dict(in_t=wt, gates_t=gates_t, out=w["gdn_w_out"][j], conv=w["gdn_conv_w"][j].astype(F32),
                      alog=_row(w["gdn_a_log"][j]), dt=_row(w["gdn_dt_bias"][j]), nw=_row(w["gdn_norm_w"][j]))
    return gdn


def _prepare_dil(w, layers=range(DEPTH // 2)):
    d = D_MODEL
    dil = {}
    for j in layers:
        wt = w["dil_w_in"][j].reshape(3, len(DIL_GROUPS), DIL_HEADS, DIL_DH, d)
        wg_t = [wt[:, g].reshape(DIL_SLAB // 2, d) for g in range(len(DIL_GROUPS))]
        out_t = jnp.pad(w["dil_w_out"][j].reshape(d, DIL_HEADS, DIL_DH), ((0, 0), (0, 0), (0, LANES - DIL_DH)))
        dil[j] = dict(wg_t=wg_t, out_t=out_t.reshape(d, DIL_HEADS * LANES), wq=_row(w["dil_q_norm"][j]),
                      wk=_row(w["dil_k_norm"][j]))
    return dil


def _prepare_ffn(w, layers=range(DEPTH)):
    return {i: dict(in_t=w["ffn_w_in"][i], out=w["ffn_w_out"][i]) for i in layers}


def _gdn_layer_fwd(x, nrow, p):
    hn = _rmsnorm_fwd(x, nrow, name="rmsnorm_fwd")
    pm = _matmul(hn, p["in_t"], trans_b=True, b_rows=(0, GDN_MAIN), name="gdn_proj_main")
    ab = _matmul(hn, p["gates_t"], trans_b=True, name="gdn_proj_gates")
    qkv = _gdn_conv_fwd(pm, p["conv"], name="gdn_conv_fwd")
    gb, bb = _gdn_gates_fwd(ab, p["alog"], p["dt"], name="gdn_gates_fwd")
    prep, rest = _gdn_prep_fwd(qkv, gb, bb, name="gdn_prep_fwd")
    o, states = _gdn_scan_fwd(prep, name="gdn_scan_fwd")
    og = _gdn_outgate_fwd(o, pm, p["nw"], name="gdn_outgate_fwd")
    y = _matmul(og, p["out"], add=x, name="gdn_proj_out")
    return y, (x, hn, pm, ab, qkv, gb, bb, prep, rest, states, o, og)


def _gdn_layer_bwd(dx, dxb, nrow, p, saved):
    x, hn, pm, ab, qkv, gb, bb, prep, rest, states, o, og = saved
    d_og = _matmul(dxb, p["out"], trans_b=True, name="gdn_dgate")
    g_out = _matmul(og, dxb, trans_a=True, out_dtype=MM_DTYPE, name="gdn_gw_out")
    d_o, d_z, d_nw = _gdn_outgate_bwd(o, pm, p["nw"], d_og, name="gdn_outgate_bwd")
    cts = _gdn_scan_bwd(prep, states, d_o, name="gdn_scan_bwd")
    dqkv, dgb, dbb = _gdn_prep_bwd(qkv, gb, bb, rest, cts, name="gdn_prep_bwd")
    d_ab, d_alog, d_dt = _gdn_gates_bwd(ab, p["alog"], p["dt"], dgb, dbb, name="gdn_gates_bwd")
    d_conv, g_conv = _gdn_conv_bwd(pm, p["conv"], dqkv, name="gdn_conv_bwd")
    d_hn = _matmul(d_conv, p["in_t"], b_rows=(0, GDN_QKV), name="gdn_dhn_qkv")
    d_hn = _matmul(d_z, p["in_t"], b_rows=(GDN_QKV, GDN_MAIN - GDN_QKV), add=d_hn, name="gdn_dhn_z")
    d_hn = _matmul(d_ab, p["gates_t"], add=d_hn, name="gdn_dhn_gates")
    g_in_t = jnp.concatenate([
        _matmul(d_conv, hn, trans_a=True, out_dtype=MM_DTYPE, name="gdn_gw_qkv"),
        _matmul(d_z, hn, trans_a=True, out_dtype=MM_DTYPE, name="gdn_gw_z"),
        _matmul(d_ab, hn, trans_a=True, out_dtype=MM_DTYPE, name="gdn_gw_gates")[:2 * GDN_HEADS],
    ], axis=0)
    dx_new, dxb_new, g_norm = _rmsnorm_bwd(x, nrow, d_hn, dx, name="rmsnorm_bwd")
    grads = dict(w_in=g_in_t, conv=g_conv, a_log=d_alog[0, :GDN_HEADS], dt_bias=d_dt[0, :GDN_HEADS], norm_w=d_nw[0],
                 w_out=g_out, norm=g_norm[0])
    return dx_new, dxb_new, grads


def _dil_layer_fwd(x, nrow, p):
    hn = _rmsnorm_fwd(x, nrow, name="rmsnorm_fwd")
    slabs = [_matmul(hn, p["wg_t"][g], trans_b=True, spread_out=True, name="dil_proj_in") for g in range(len(DIL_GROUPS))]
    oe = [(_dil_attn_fwd if DIL_GROUPS[g][1] == 1 else _dil_attn_fwd_strided)(
        slabs[g], p["wq"], p["wk"], group=g, name=f"dil_attn_fwd_g{g}") for g in range(len(DIL_GROUPS))]
    y, om = _dil_merge_fwd(oe, name="dil_merge_fwd")
    out = _matmul(y, p["out_t"], trans_b=True, add=x, name="dil_proj_out")
    return out, (x, hn, slabs, y, om)


def _dil_layer_bwd(dx, dxb, nrow, p, saved):
    x, hn, slabs, y, om = saved
    d_y = _matmul(dxb, p["out_t"], name="dil_dmerged")
    g_out_t = _matmul(dxb, y, trans_a=True, out_dtype=MM_DTYPE, name="dil_gw_out")
    g_out_t = g_out_t.reshape(D_MODEL, DIL_HEADS, LANES)[..., :DIL_DH].reshape(D_MODEL, DIL_HEADS * DIL_DH)
    stat = _dil_merge_bwd(d_y, om, name="dil_merge_bwd")
    d_hn = None
    dwq = jnp.zeros((1, LANES), F32)
    dwk = jnp.zeros((1, LANES), F32)
    g_groups = []
    wide = DIL_HEADS * LANES
    for g in range(len(DIL_GROUPS)):
        if DIL_GROUPS[g][1] == 1:
            dslab, dwq, dwk = _dil_attn_bwd(slabs[g], stat, p["wq"], p["wk"], dwq, dwk, group=g, name=f"dil_attn_bwd_g{g}")
            d_hn = _matmul(dslab, p["wg_t"][g], packed_a=True, add=d_hn, name="dil_dhn")
            g_w = _matmul(dslab, hn, trans_a=True, packed_a=True, out_dtype=MM_DTYPE, name="dil_gw_in")
        else:
            dparts, dwq, dwk = _dil_attn_bwd_strided(slabs[g], stat, p["wq"], p["wk"], dwq, dwk, group=g,
                                                     name=f"dil_attn_bwd_g{g}")
            d_hn = _matmul(dparts, p["wg_t"][g], a_lead="k", packed_a=True, add=d_hn, name="dil_dhn_parts")
            g_w = _matmul(dparts, hn, trans_a=True, a_lead="i", packed_a=True, out_dtype=MM_DTYPE, name="dil_gw_in_parts")
        g_groups.append(g_w.reshape(3, DIL_HEADS, DIL_DH, D_MODEL))
    g_in_t = jnp.stack(g_groups, axis=1).reshape(3 * len(DIL_GROUPS) * DIL_HEADS * DIL_DH, D_MODEL)
    dx_new, dxb_new, g_norm = _rmsnorm_bwd(x, nrow, d_hn, dx, name="rmsnorm_bwd")
    grads = dict(w_in=g_in_t, q_norm=dwq[0, :DIL_DH], k_norm=dwk[0, :DIL_DH], w_out=g_out_t, norm=g_norm[0])
    return dx_new, dxb_new, grads


def _ffn_layer_fwd(x, nrow, p):
    hn = _rmsnorm_fwd(x, nrow, name="rmsnorm_fwd")
    gate, up, act = _ffn_in(hn, p["in_t"], name="ffn_proj_in")
    y = _matmul(act, p["out"], add=x, name="ffn_proj_out")
    return y, (x, hn, gate, up, act)


def _ffn_layer_bwd(dx, dxb, nrow, p, saved):
    x, hn, gate, up, act = saved
    g_out = _matmul(act, dxb, trans_a=True, out_dtype=MM_DTYPE, name="ffn_gw_out")
    d_gu = _ffn_dact(dxb, p["out"], gate, up, name="ffn_dact")
    d_hn = _matmul(d_gu, p["in_t"], a_lead="k", name="ffn_dhn")
    g_in_t = _matmul(d_gu, hn, trans_a=True, a_lead="i", out_dtype=MM_DTYPE, name="ffn_gw_in")
    dx_new, dxb_new, g_norm = _rmsnorm_bwd(x, nrow, d_hn, dx, name="rmsnorm_bwd")
    return dx_new, dxb_new, dict(w_in=g_in_t, w_out=g_out, norm=g_norm[0])


def _mixer_fwd(i, x, mix_row, prepared):
    if i % 2 == 0:
        return _gdn_layer_fwd(x, mix_row, prepared["gdn"][i // 2])
    return _dil_layer_fwd(x, mix_row, prepared["dil"][i // 2])


def _mixer_bwd(i, dx, dxb, mix_row, prepared, saved):
    if i % 2 == 0:
        return _gdn_layer_bwd(dx, dxb, mix_row, prepared["gdn"][i // 2], saved)
    return _dil_layer_bwd(dx, dxb, mix_row, prepared["dil"][i // 2], saved)


def _local_step(x, target, prepared, norm_mix, norm_ffn):
    saved = []
    for i in range(DEPTH):
        x, s_mix = _mixer_fwd(i, x, norm_mix[i].reshape(1, D_MODEL), prepared)
        x, s_ffn = _ffn_layer_fwd(x, norm_ffn[i].reshape(1, D_MODEL), prepared["ffn"][i])
        saved.append((s_mix, s_ffn))
    dx, dxb, loss = _loss_head(x, target, name="loss_head")
    g_mix, g_ffn = [None] * DEPTH, [None] * DEPTH
    for i in reversed(range(DEPTH)):
        s_mix, s_ffn = saved[i]
        dx, dxb, g_ffn[i] = _ffn_layer_bwd(dx, dxb, norm_ffn[i].reshape(1, D_MODEL), prepared["ffn"][i], s_ffn)
        dx, dxb, g_mix[i] = _mixer_bwd(i, dx, dxb, norm_mix[i].reshape(1, D_MODEL), prepared, s_mix)
    return loss[0, 0], dx, _collect_grads(g_mix, g_ffn)


def _collect_grads(g_mix, g_ffn):
    gdn = [g_mix[i] for i in range(0, DEPTH, 2)]
    dil = [g_mix[i] for i in range(1, DEPTH, 2)]
    if any(g is None for g in g_mix + g_ffn):
        pick = lambda gs, key: [None if g is None else g[key] for g in gs]
        return dict(gdn_w_in=pick(gdn, "w_in"), gdn_w_out=pick(gdn, "w_out"), dil_w_in=pick(dil, "w_in"),
                    dil_w_out=pick(dil, "w_out"), ffn_w_in=pick(g_ffn, "w_in"), ffn_w_out=pick(g_ffn, "w_out"))
    grads = dict(
        norm_mix=jnp.stack([g["norm"] for g in g_mix]),
        norm_ffn=jnp.stack([g["norm"] for g in g_ffn]),
        gdn_w_in=[g["w_in"] for g in gdn],
        gdn_conv_w=jnp.stack([g["conv"] for g in gdn]),
        gdn_a_log=jnp.stack([g["a_log"] for g in gdn]),
        gdn_dt_bias=jnp.stack([g["dt_bias"] for g in gdn]),
        gdn_norm_w=jnp.stack([g["norm_w"] for g in gdn]),
        gdn_w_out=[g["w_out"] for g in gdn],
        dil_w_in=[g["w_in"] for g in dil],
        dil_q_norm=jnp.stack([g["q_norm"] for g in dil]),
        dil_k_norm=jnp.stack([g["k_norm"] for g in dil]),
        dil_w_out=[g["w_out"] for g in dil],
        ffn_w_in=[g["w_in"] for g in g_ffn],
        ffn_w_out=[g["w_out"] for g in g_ffn],
    )
    return grads


MESH_ID = pl.DeviceIdType.MESH
ANY_SPACE = pl.BlockSpec(memory_space=pl.ANY)


def _mesh_position():
    return lax.axis_index("x"), lax.axis_index("y"), lax.axis_index("c")


def _flip(pos, k):
    x, y, c = pos
    return (1 - x if k & 4 else x, 1 - y if k & 2 else y, 1 - c if k & 1 else c)


def _linear(pos):
    return 4 * pos[0] + 2 * pos[1] + pos[2]


def _comm_scratch():
    return [pltpu.SemaphoreType.DMA((N_DEV - 1,)), pltpu.SemaphoreType.DMA((N_DEV - 1,)), pltpu.SemaphoreType.DMA(())]


def _all_gather(shard, *, name):
    def body(x_ref, out_ref, send_sems, recv_sems, local_sem):
        me = _mesh_position()
        mine = out_ref.at[_linear(me)]
        local = pltpu.make_async_copy(x_ref, mine, local_sem)
        local.start()
        copies = []
        for k in range(1, N_DEV):
            cp = pltpu.make_async_remote_copy(src_ref=x_ref, dst_ref=mine, send_sem=send_sems.at[k - 1],
                                              recv_sem=recv_sems.at[k - 1], device_id=_flip(me, k), device_id_type=MESH_ID)
            cp.start()
            copies.append(cp)
        for cp in copies:
            cp.wait()
        local.wait()

    return pl.pallas_call(
        body,
        out_shape=jax.ShapeDtypeStruct((N_DEV,) + shard.shape, shard.dtype),
        in_specs=[ANY_SPACE],
        out_specs=ANY_SPACE,
        scratch_shapes=_comm_scratch(),
        name=name,
    )(shard)


def _exchange(parts, *, name):
    def body(p_ref, out_ref, send_sems, recv_sems, local_sem):
        me = _mesh_position()
        mine = out_ref.at[_linear(me)]
        local = pltpu.make_async_copy(p_ref.at[_linear(me)], mine, local_sem)
        local.start()
        copies = []
        for k in range(1, N_DEV):
            peer = _flip(me, k)
            cp = pltpu.make_async_remote_copy(src_ref=p_ref.at[_linear(peer)], dst_ref=mine, send_sem=send_sems.at[k - 1],
                                              recv_sem=recv_sems.at[k - 1], device_id=peer, device_id_type=MESH_ID)
            cp.start()
            copies.append(cp)
        for cp in copies:
            cp.wait()
        local.wait()

    return pl.pallas_call(
        body,
        out_shape=jax.ShapeDtypeStruct(parts.shape, parts.dtype),
        in_specs=[ANY_SPACE],
        out_specs=ANY_SPACE,
        scratch_shapes=_comm_scratch(),
        name=name,
    )(parts)


HBM_SPACE = pl.BlockSpec(memory_space=pltpu.HBM)
SEM_SPACE = pl.BlockSpec(memory_space=pltpu.SEMAPHORE)
DATAFLOW = pltpu.SideEffectType.DATAFLOW_SIDE_EFFECTING


def _split_copies(src_ref, land_ref, send_sems, recv_sems, per_peer):
    me = _mesh_position()
    mine = land_ref.at[_linear(me)]
    copies = []
    for k in range(1, N_DEV):
        peer = _flip(me, k)
        src = src_ref.at[_linear(peer)] if per_peer else src_ref
        copies.append(pltpu.make_async_remote_copy(src_ref=src, dst_ref=mine, send_sem=send_sems.at[k - 1],
                                                   recv_sem=recv_sems.at[k - 1], device_id=peer, device_id_type=MESH_ID))
    return copies


def _travel_start(src, after, *, per_peer, name):
    me = _linear(_mesh_position())
    own = src[me] if per_peer else src
    shape = own.shape
    landing = lax.dynamic_update_slice(lax.empty((N_DEV,) + shape, src.dtype), own[None], (me, 0, 0))

    def body(src_ref, land_ref, after_ref, send_sems, recv_sems, src_thru, land_thru, token):
        for cp in _split_copies(src_ref, land_ref, send_sems, recv_sems, per_peer):
            cp.start()
        token[...] = jnp.zeros_like(token)

    return pl.pallas_call(
        body,
        name=name,
        out_shape=(pltpu.SemaphoreType.DMA((N_DEV - 1,)), pltpu.SemaphoreType.DMA((N_DEV - 1,)),
                   pltpu.HBM(src.shape, src.dtype), pltpu.HBM(landing.shape, landing.dtype),
                   jax.ShapeDtypeStruct((8, LANES), F32)),
        in_specs=(HBM_SPACE, HBM_SPACE, ANY_SPACE),
        out_specs=(SEM_SPACE, SEM_SPACE, HBM_SPACE, HBM_SPACE, pl.BlockSpec(memory_space=pltpu.VMEM)),
        input_output_aliases={0: 2, 1: 3},
        compiler_params=pltpu.CompilerParams(has_side_effects=DATAFLOW),
    )(pltpu.with_memory_space_constraint(src, pltpu.HBM), pltpu.with_memory_space_constraint(landing, pltpu.HBM), after)


def _travel_wait(started, after, *, per_peer, name):
    send_sems, recv_sems, src_thru, land_thru, _ = started

    def body(src_ref, land_ref, send_sems, recv_sems, after_ref, src_dead, got_ref):
        for cp in _split_copies(src_ref, land_ref, send_sems, recv_sems, per_peer):
            cp.wait_send()
            cp.wait_recv()

    return pl.pallas_call(
        body,
        name=name,
        out_shape=(pltpu.HBM(src_thru.shape, src_thru.dtype), pltpu.HBM(land_thru.shape, land_thru.dtype)),
        in_specs=(HBM_SPACE, HBM_SPACE, SEM_SPACE, SEM_SPACE, ANY_SPACE),
        out_specs=(HBM_SPACE, HBM_SPACE),
        input_output_aliases={0: 0, 1: 1},
        compiler_params=pltpu.CompilerParams(has_side_effects=DATAFLOW),
    )(src_thru, land_thru, send_sems, recv_sems, after)[1]


def _adamw(parts, w, m, v, *, name):
    rows, n = w.shape
    tb = _pick(rows, (PACK_ROW_ALIGN, 16))
    c1 = 1.0 - ADAM_B1 ** ADAM_STEP
    c2 = 1.0 - ADAM_B2 ** ADAM_STEP

    def body(p_ref, w_ref, m_ref, v_ref, g_ref, d_ref, nm_ref, nv_ref):
        g = p_ref[0].astype(F32)
        for s in range(1, N_DEV):
            g = g + p_ref[s].astype(F32)
        m_new = ADAM_B1 * m_ref[...] + (1.0 - ADAM_B1) * g
        v_new = ADAM_B2 * v_ref[...] + (1.0 - ADAM_B2) * (g * g)
        m_hat = m_new / c1
        v_hat = v_new / c2
        g_ref[...] = g
        nm_ref[...] = m_new
        nv_ref[...] = v_new
        d_ref[...] = -ADAM_LR * (m_hat / (jnp.sqrt(v_hat) + ADAM_EPS) + ADAM_WD * w_ref[...])

    blk = pl.BlockSpec((tb, n), lambda i: (i, 0))
    return pl.pallas_call(
        body,
        grid=(rows // tb,),
        in_specs=[pl.BlockSpec((N_DEV, tb, n), lambda i: (0, i, 0)), blk, blk, blk],
        out_specs=[blk] * 4,
        out_shape=[jax.ShapeDtypeStruct((rows, n), F32)] * 4,
        compiler_params=_cparams(("parallel",)),
        name=name,
    )(parts, w, m, v)


PACK_WIDTH = 1024
SHARDED = {
    "gdn_w_in": ((2, D_MODEL, GDN_IN_WIDTH), 2),
    "gdn_conv_w": ((2, GDN_CONV, GDN_QKV), 2),
    "gdn_w_out": ((2, GDN_HEADS * GDN_DV, D_MODEL), 1),
    "dil_w_in": ((2, D_MODEL, 3 * len(DIL_GROUPS) * DIL_HEADS * DIL_DH), 2),
    "dil_w_out": ((2, DIL_HEADS * DIL_DH, D_MODEL), 2),
    "ffn_w_in": ((DEPTH, D_MODEL, 2 * FFN_HIDDEN), 2),
    "ffn_w_out": ((DEPTH, FFN_HIDDEN, D_MODEL), 1),
}
REPLICATED = {"norm_mix": (DEPTH, D_MODEL), "norm_ffn": (DEPTH, D_MODEL), "gdn_a_log": (2, GDN_HEADS),
              "gdn_dt_bias": (2, GDN_HEADS), "gdn_norm_w": (2, GDN_DV), "dil_q_norm": (2, DIL_DH), "dil_k_norm": (2, DIL_DH)}
WEIGHT_ORDER = ("norm_mix", "norm_ffn", "gdn_w_in", "gdn_conv_w", "gdn_a_log", "gdn_dt_bias", "gdn_norm_w", "gdn_w_out",
                "dil_w_in", "dil_q_norm", "dil_k_norm", "dil_w_out", "ffn_w_in", "ffn_w_out")
PACK_ROW_ALIGN = 128
PIECE_ALIGN = 16
SMALL_ROWS = 16


def _shard_shape(name):
    shape, axis = SHARDED[name]
    return tuple(s // N_DEV if i == axis else s for i, s in enumerate(shape))


def _shard_rows(name):
    return math.prod(_shard_shape(name)) // PACK_WIDTH


def _split_shards(full, name):
    shape, axis = SHARDED[name]
    split = full.reshape(shape[:axis] + (N_DEV, shape[axis] // N_DEV) + shape[axis + 1:])
    return jnp.moveaxis(split, axis, 0)


def _join_shards(stacked, name):
    shape, axis = SHARDED[name]
    return jnp.moveaxis(stacked, 0, axis).reshape(shape)


COLUMN_SHARDED = ("gdn_w_in", "dil_w_in", "dil_w_out", "ffn_w_in")


def _to_rows(shard, name):
    if name in COLUMN_SHARDED:
        shard = jnp.swapaxes(shard, 1, 2)
    return shard.reshape(-1, PACK_WIDTH)


def _layer_columns(name):
    _, r, c = _shard_shape(name)
    return r if name in COLUMN_SHARDED else c


def _piece_rows(piece, halves=1):
    name, layer = piece
    rows = _shard_rows(name) * halves
    return rows if layer is None else rows // SHARDED[name][0][0]


def _aligned(rows, to=PIECE_ALIGN):
    return -(-rows // to) * to


def _pack_pieces(arrays, total_align=PIECE_ALIGN):
    padded, total = [], 0
    for a in arrays:
        rows = a.shape[-2]
        extra = _aligned(rows) - rows
        if extra:
            a = jnp.pad(a, [(0, 0)] * (a.ndim - 2) + [(0, extra), (0, 0)])
        padded.append(a)
        total += rows + extra
    tail = _aligned(total, total_align) - total
    if tail:
        padded.append(jnp.zeros(padded[0].shape[:-2] + (tail, PACK_WIDTH), padded[0].dtype))
    return jnp.concatenate(padded, axis=-2)


def _piece_offsets(pieces, halves=None):
    out, at = [], 0
    for p in pieces:
        rows = _piece_rows(p, (halves or {}).get(p[0], 1))
        out.append((p, at, rows))
        at += _aligned(rows)
    return out


def _shard_piece_rows(src, piece):
    name, layer = piece
    part = src[name] if layer is None else src[name][layer:layer + 1]
    return _to_rows(part.astype(F32), name)


def _piece_from_rows(rows, piece):
    name, layer = piece
    layers, r, c = _shard_shape(name)
    n_l = layers if layer is None else 1
    if name in COLUMN_SHARDED:
        return jnp.swapaxes(rows.reshape(n_l, c, r), 1, 2)
    return rows.reshape(n_l, r, c)


SMALL_TAIL = tuple(n for n in REPLICATED if n not in ("norm_mix", "norm_ffn"))


def _pack_small(vals):
    tail, at = jnp.zeros((PACK_WIDTH,), F32), 0
    for n in SMALL_TAIL:
        vec = vals[n].astype(F32).reshape(-1)
        tail = tail + jnp.pad(vec, (at, PACK_WIDTH - at - vec.shape[0]))
        at += vec.shape[0]
    buf = jnp.pad(vals["norm_mix"].astype(F32), ((0, SMALL_ROWS - DEPTH), (0, 0)))
    buf = buf + jnp.pad(vals["norm_ffn"].astype(F32), ((8, SMALL_ROWS - 8 - DEPTH), (0, 0)))
    return buf + jnp.pad(tail.reshape(1, PACK_WIDTH), ((SMALL_ROWS - 1, 0), (0, 0)))


def _unpack_small(buf):
    out = {"norm_mix": buf[0:DEPTH], "norm_ffn": buf[8:8 + DEPTH]}
    at = 0
    for n in SMALL_TAIL:
        size = math.prod(REPLICATED[n])
        out[n] = buf[SMALL_ROWS - 1, at:at + size].reshape(REPLICATED[n])
        at += size
    return out


GATHER_FIRST = (("gdn_w_in", 0), ("gdn_conv_w", None), ("gdn_w_out", 0))
GATHER_NEXT = (("ffn_w_in", 0), ("ffn_w_out", 0), ("dil_w_in", 0), ("dil_w_out", 0))
GATHER_LAST = (("ffn_w_in", 1), ("ffn_w_out", 1), ("gdn_w_in", 1), ("gdn_w_out", 1), ("ffn_w_in", 2), ("ffn_w_out", 2),
               ("dil_w_in", 1), ("dil_w_out", 1), ("ffn_w_in", 3), ("ffn_w_out", 3))
EXCHANGE_GROUPS = (
    (("ffn_w_in", 3), ("ffn_w_out", 3), ("dil_w_in", 1), ("dil_w_out", 1),
     ("ffn_w_in", 2), ("ffn_w_out", 2), ("gdn_w_in", 1), ("gdn_w_out", 1)),
    (("ffn_w_in", 1), ("ffn_w_out", 1), ("dil_w_in", 0), ("dil_w_out", 0)),
    (("ffn_w_in", 0), ("ffn_w_out", 0)),
    (("gdn_w_in", 0), ("gdn_w_out", 0), ("gdn_conv_w", None)),
)
EXCHANGE_AFTER = {("mix", 2): 0, ("mix", 1): 1, ("ffn", 0): 2}


def _gather_operand(w, pieces):
    arrays = []
    for n, layer in pieces:
        if layer is None:
            arrays.append(lax.bitcast_convert_type(w[n], BF16).reshape(-1, PACK_WIDTH))
        else:
            arrays.append(_to_rows(w[n][layer:layer + 1].astype(BF16), n))
    return _pack_pieces(arrays)


def _gathered_weights(gathered, pieces, full):
    for (n, layer), at, rows in _piece_offsets(pieces, halves={"gdn_conv_w": 2}):
        block = gathered[:, at:at + rows]
        if layer is None:
            block = lax.bitcast_convert_type(block.reshape((N_DEV,) + _shard_shape(n) + (2,)), F32)
            full[n] = _join_shards(block, n)
        else:
            full.setdefault(n, {})[layer] = block.reshape(-1, _layer_columns(n))
    return full


def _exchange_operand(grads, pieces):
    arrays = []
    for n, layer in pieces:
        if layer is None:
            arrays.append(_split_shards(grads[n], n).astype(BF16).reshape(N_DEV, -1, PACK_WIDTH))
        else:
            arrays.append(grads[n][layer].astype(BF16).reshape(N_DEV, -1, PACK_WIDTH))
    return _pack_pieces(arrays, total_align=PACK_ROW_ALIGN)


def _update_group(received, pieces, w, m, v, *, name):
    packed = [_pack_pieces([_shard_piece_rows(src, p) for p in pieces], total_align=PACK_ROW_ALIGN) for src in (w, m, v)]
    outs = _adamw(received, *packed, name=name)
    return {p: tuple(_piece_from_rows(o[at:at + rows], p) for o in outs) for p, at, rows in _piece_offsets(pieces)}


def kernel(x, norm_mix, norm_ffn, gdn_w_in, gdn_conv_w, gdn_a_log, gdn_dt_bias, gdn_norm_w, gdn_w_out, dil_w_in, dil_q_norm, dil_k_norm, dil_w_out, ffn_w_in, ffn_w_out, loss_target, m_norm_mix, m_norm_ffn, m_gdn_w_in, m_gdn_conv_w, m_gdn_a_log, m_gdn_dt_bias, m_gdn_norm_w, m_gdn_w_out, m_dil_w_in, m_dil_q_norm, m_dil_k_norm, m_dil_w_out, m_ffn_w_in, m_ffn_w_out, v_norm_mix, v_norm_ffn, v_gdn_w_in, v_gdn_conv_w, v_gdn_a_log, v_gdn_dt_bias, v_gdn_norm_w, v_gdn_w_out, v_dil_w_in, v_dil_q_norm, v_dil_k_norm, v_dil_w_out, v_ffn_w_in, v_ffn_w_out):
    w = dict(norm_mix=norm_mix, norm_ffn=norm_ffn, gdn_w_in=gdn_w_in, gdn_conv_w=gdn_conv_w, gdn_a_log=gdn_a_log,
             gdn_dt_bias=gdn_dt_bias, gdn_norm_w=gdn_norm_w, gdn_w_out=gdn_w_out, dil_w_in=dil_w_in, dil_q_norm=dil_q_norm,
             dil_k_norm=dil_k_norm, dil_w_out=dil_w_out, ffn_w_in=ffn_w_in, ffn_w_out=ffn_w_out)
    m = dict(norm_mix=m_norm_mix, norm_ffn=m_norm_ffn, gdn_w_in=m_gdn_w_in, gdn_conv_w=m_gdn_conv_w, gdn_a_log=m_gdn_a_log,
             gdn_dt_bias=m_gdn_dt_bias, gdn_norm_w=m_gdn_norm_w, gdn_w_out=m_gdn_w_out, dil_w_in=m_dil_w_in,
             dil_q_norm=m_dil_q_norm, dil_k_norm=m_dil_k_norm, dil_w_out=m_dil_w_out, ffn_w_in=m_ffn_w_in, ffn_w_out=m_ffn_w_out)
    v = dict(norm_mix=v_norm_mix, norm_ffn=v_norm_ffn, gdn_w_in=v_gdn_w_in, gdn_conv_w=v_gdn_conv_w, gdn_a_log=v_gdn_a_log,
             gdn_dt_bias=v_gdn_dt_bias, gdn_norm_w=v_gdn_norm_w, gdn_w_out=v_gdn_w_out, dil_w_in=v_dil_w_in,
             dil_q_norm=v_dil_q_norm, dil_k_norm=v_dil_k_norm, dil_w_out=v_dil_w_out, ffn_w_in=v_ffn_w_in, ffn_w_out=v_ffn_w_out)
    def row(src, i):
        return src[i].reshape(1, D_MODEL)

    first = _all_gather(_gather_operand(w, GATHER_FIRST), name="weight_all_gather_first")
    next_started = _travel_start(_gather_operand(w, GATHER_NEXT), first, per_peer=False, name="weight_gather_start_next")
    last_started = _travel_start(_gather_operand(w, GATHER_LAST), next_started[4], per_peer=False,
                                 name="weight_gather_start_last")
    full = _gathered_weights(first, GATHER_FIRST, {n: w[n] for n in REPLICATED})
    prepared = dict(gdn=_prepare_gdn(full, layers=(0,)))
    h = x[0]
    saved = [None] * DEPTH
    h, s_mix = _mixer_fwd(0, h, row(norm_mix, 0) + last_started[4][0, 0], prepared)
    got = _travel_wait(next_started, h, per_peer=False, name="weight_gather_wait_next")
    full = _gathered_weights(got, GATHER_NEXT, full)
    prepared.update(dil=_prepare_dil(full, layers=(0,)), ffn=_prepare_ffn(full, layers=(0,)))
    for i in range(DEPTH):
        if i > 0:
            h, s_mix = _mixer_fwd(i, h, row(norm_mix, i), prepared)
        if i == 1:
            got = _travel_wait(last_started, h, per_peer=False, name="weight_gather_wait_last")
            full = _gathered_weights(got, GATHER_LAST, full)
            prepared["gdn"].update(_prepare_gdn(full, layers=(1,)))
            prepared["dil"].update(_prepare_dil(full, layers=(1,)))
            prepared["ffn"].update(_prepare_ffn(full, layers=(1, 2, 3)))
        h, s_ffn = _ffn_layer_fwd(h, row(norm_ffn, i), prepared["ffn"][i])
        saved[i] = (s_mix, s_ffn)
    dx, dxb, loss = _loss_head(h, loss_target[0], name="loss_head")

    g_mix, g_ffn = [None] * DEPTH, [None] * DEPTH
    started = {}

    def travel(group):
        operand = _exchange_operand(_collect_grads(g_mix, g_ffn), EXCHANGE_GROUPS[group])
        started[group] = _travel_start(operand, dx, per_peer=True, name=f"grad_exchange_start_{group}")
        return started[group][4][0, 0]

    zero = 0.0
    for i in reversed(range(DEPTH)):
        s_mix, s_ffn = saved[i]
        dx, dxb, g_ffn[i] = _ffn_layer_bwd(dx, dxb, row(norm_ffn, i) + zero, prepared["ffn"][i], s_ffn)
        zero = travel(EXCHANGE_AFTER[("ffn", i)]) if ("ffn", i) in EXCHANGE_AFTER else 0.0
        dx, dxb, g_mix[i] = _mixer_bwd(i, dx, dxb, row(norm_mix, i) + zero, prepared, s_mix)
        zero = travel(EXCHANGE_AFTER[("mix", i)]) if ("mix", i) in EXCHANGE_AFTER else 0.0
    grads = _collect_grads(g_mix, g_ffn)
    received = [_travel_wait(started[g], dx, per_peer=True, name=f"grad_exchange_wait_{g}") for g in sorted(started)]
    received.append(_exchange(_exchange_operand(grads, EXCHANGE_GROUPS[-1]), name="grad_exchange_last"))
    updated = {}
    for g, pieces in enumerate(EXCHANGE_GROUPS):
        updated.update(_update_group(received[g], pieces, w, m, v, name=f"adamw_sharded_{g}"))

    small_parts = _all_gather(_pack_small(grads), name="small_grad_all_gather")
    outs_small = [_unpack_small(o) for o in
                  _adamw(small_parts, _pack_small(w), _pack_small(m), _pack_small(v), name="adamw_replicated")]

    total_loss = lax.psum(loss[0, 0], ("x", "y", "c"))
    result = [total_loss, dx[None]]
    for k in range(4):
        for n in WEIGHT_ORDER:
            if n not in SHARDED:
                result.append(outs_small[k][n])
            elif (n, None) in updated:
                result.append(updated[(n, None)][k])
            else:
                result.append(jnp.concatenate([updated[(n, l)][k] for l in range(SHARDED[n][0][0])], axis=0))
    return tuple(result)
```

```python
import functools
import math

import jax
import jax.numpy as jnp
from jax import lax
from jax.experimental import pallas as pl
from jax.experimental.pallas import tpu as pltpu

F32 = jnp.float32
BF16 = jnp.bfloat16
MM_DTYPE = BF16

N_DEV = 8
D_MODEL = 1024
DEPTH = 4
RMS_EPS = 1e-6
L2_EPS = 1e-6

LANES = 128

GDN_HEADS = 8
GDN_DK = 128
GDN_DV = 128
GDN_CONV = 4
GDN_CHUNK = 128
GDN_QKV = 3 * GDN_HEADS * GDN_DK
GDN_MAIN = GDN_QKV + GDN_HEADS * GDN_DV
GDN_IN_WIDTH = GDN_MAIN + 2 * GDN_HEADS

DIL_GROUPS = ((128, 1), (512, 4), (2048, 16))
DIL_HEADS = 8
DIL_DH = 64
DIL_SPAN = 128
DIL_SLAB = 3 * DIL_HEADS * LANES
ALIBI_MAX_BIAS = 8.0

FFN_HIDDEN = 2816

ADAM_LR = 0.001
ADAM_B1 = 0.9
ADAM_B2 = 0.999
ADAM_EPS = 1e-08
ADAM_WD = 0.01
ADAM_STEP = 10

VMEM_LIMIT = 56 * 1024 * 1024
ROW_TILE = 512
NEG = -1e30
HI = lax.Precision.HIGHEST


def _cparams(sem):
    return pltpu.CompilerParams(dimension_semantics=sem, vmem_limit_bytes=VMEM_LIMIT)


def _dot(a, b):
    return lax.dot_general(a, b, (((1,), (0,)), ((), ())), preferred_element_type=F32, precision=HI)


def _dot_nt(a, b):
    return lax.dot_general(a, b, (((1,), (1,)), ((), ())), preferred_element_type=F32, precision=HI)


def _dot_tn(a, b):
    return lax.dot_general(a, b, (((0,), (0,)), ((), ())), preferred_element_type=F32, precision=HI)


def _single_pass(a, b, a_dim, b_dim):
    lead = a.ndim - 2
    batch = ((0,), (0,)) if lead else ((), ())
    return lax.dot_general(a.astype(BF16), b.astype(BF16), (((lead + a_dim,), (lead + b_dim,)), batch),
                           preferred_element_type=F32)


def _bdot(a, b):
    return _single_pass(a, b, 1, 0)


def _bdot_nt(a, b):
    return _single_pass(a, b, 1, 1)


def _bdot_tn(a, b):
    return _single_pass(a, b, 0, 0)


def _pick(n, candidates):
    for c in candidates:
        if n % c == 0:
            return c
    raise ValueError(f"no tile for {n}")


HALF = LANES // 2


def _pack_head_pairs(x):
    x = x.astype(F32)
    tiles = [x[:, (2 * i) * LANES:(2 * i + 1) * LANES] + pltpu.roll(x[:, (2 * i + 1) * LANES:(2 * i + 2) * LANES], HALF, 1)
             for i in range(x.shape[1] // (2 * LANES))]
    return tiles[0] if len(tiles) == 1 else jnp.concatenate(tiles, axis=1)


def _spread_head_pairs(y):
    low = lax.broadcasted_iota(jnp.int32, (y.shape[0], LANES), 1) < HALF
    tiles = []
    for i in range(y.shape[1] // LANES):
        pair = y[:, i * LANES:(i + 1) * LANES]
        tiles += [jnp.where(low, pair, 0.0), jnp.where(low, pltpu.roll(pair, HALF, 1), 0.0)]
    return jnp.concatenate(tiles, axis=1)


def _matmul(a, b, *, name, trans_a=False, trans_b=False, b_rows=None, a_lead=None, add=None, out_dtype=F32,
            packed_a=False, spread_out=False):
    if trans_a:
        k_dim, m_dim = a.shape[-2:]
        m_dim = m_dim // 2 if packed_a else m_dim
    else:
        m_dim, k_dim = a.shape[-2:]
        k_dim = k_dim // 2 if packed_a else k_dim
    slab_m, slab_k = m_dim, k_dim
    if a_lead == "k":
        assert not trans_a
        k_dim *= a.shape[0]
    elif a_lead == "i":
        assert trans_a
        m_dim *= a.shape[0]
    b_start, b_size = b_rows if b_rows is not None else (0, b.shape[0])
    if trans_b:
        n_dim, k2 = b_size, b.shape[1]
    else:
        k2, n_dim = b_size, b.shape[1]
    assert k_dim == k2, (a.shape, b.shape, b_rows)
    tn = _pick(n_dim, (1024, 512, 256, 128))
    tm = min(slab_m, 2048, max(512, (1024 * 1024) // tn))
    tm = _pick(slab_m, (tm, 1408, 1024, 512, 256, 128))
    tk = _pick(slab_k, (1024, 1408, 512, 256, 128))
    nk = k_dim // tk
    has_add = add is not None
    dn = (((0 if trans_a else 1,), (1 if trans_b else 0,)), ((), ()))
    b_tile = tn if trans_b else tk
    assert b_start % b_tile == 0, (b_rows, b_tile)
    b_off = b_start // b_tile

    def body(*refs):
        if has_add:
            a_ref, b_ref, add_ref, o_ref, acc_ref = refs
        else:
            a_ref, b_ref, o_ref, acc_ref = refs
        a_blk = _pack_head_pairs(a_ref[...]).astype(a_ref.dtype) if packed_a else a_ref[...]
        part = lax.dot_general(a_blk, b_ref[...], dn, preferred_element_type=F32)

        def finish(total):
            if has_add:
                total = total + add_ref[...]
            if spread_out:
                total = _spread_head_pairs(total)
            o_ref[...] = total.astype(out_dtype)

        if nk == 1:
            finish(part)
        else:
            k = pl.program_id(2)

            @pl.when(k == 0)
            def _():
                acc_ref[...] = part

            @pl.when(k > 0)
            def _():
                acc_ref[...] += part

            @pl.when(k == nk - 1)
            def _():
                finish(acc_ref[...])

    wide = 2 if packed_a else 1
    a_tile = (tk, wide * tm) if trans_a else (tm, wide * tk)
    a_at = (lambda i, j, k: (k, i)) if trans_a else (lambda i, j, k: (i, k))
    if a_lead is None:
        a_spec = pl.BlockSpec(a_tile, a_at)
    elif a_lead == "k":
        per = slab_k // tk
        a_spec = pl.BlockSpec((None,) + a_tile, lambda i, j, k: (k // per, i, k % per))
    elif a_lead == "i":
        per = slab_m // tm
        a_spec = pl.BlockSpec((None,) + a_tile, lambda i, j, k: (i // per, k, i % per))
    else:
        a_spec = pl.BlockSpec((None,) + a_tile, lambda i, j, k: (a_lead,) + a_at(i, j, k))
    if trans_b:
        b_spec = pl.BlockSpec((tn, tk), lambda i, j, k: (j + b_off, k))
    else:
        b_spec = pl.BlockSpec((tk, tn), lambda i, j, k: (k + b_off, j))
    in_specs = [a_spec, b_spec]
    args = [a, b]
    if has_add:
        in_specs.append(pl.BlockSpec((tm, tn), lambda i, j, k: (i, j)))
        args.append(add)
    return pl.pallas_call(
        body,
        grid=(m_dim // tm, n_dim // tn, nk),
        in_specs=in_specs,
        out_specs=pl.BlockSpec((tm, (2 if spread_out else 1) * tn), lambda i, j, k: (i, j)),
        out_shape=jax.ShapeDtypeStruct((m_dim, (2 if spread_out else 1) * n_dim), out_dtype),
        scratch_shapes=[pltpu.VMEM((tm, tn) if nk > 1 else (8, LANES), F32)],
        compiler_params=_cparams(("parallel", "parallel", "arbitrary")),
        name=name,
    )(*args)


def _rmsnorm_fwd(x, w_row, *, name):
    t, d = x.shape
    tb = min(t, 1024)

    def body(x_ref, w_ref, o_ref):
        xf = x_ref[...]
        r = lax.rsqrt(jnp.mean(xf * xf, axis=-1, keepdims=True) + RMS_EPS)
        o_ref[...] = (xf * r * w_ref[...]).astype(o_ref.dtype)

    return pl.pallas_call(
        body,
        grid=(t // tb,),
        in_specs=[pl.BlockSpec((tb, d), lambda i: (i, 0)), pl.BlockSpec((1, d), lambda i: (0, 0))],
        out_specs=pl.BlockSpec((tb, d), lambda i: (i, 0)),
        out_shape=jax.ShapeDtypeStruct((t, d), MM_DTYPE),
        compiler_params=_cparams(("parallel",)),
        name=name,
    )(x, w_row)


def _rmsnorm_bwd(x, w_row, dy, dskip, *, name):
    t, d = x.shape
    tb = min(t, 512)

    def body(x_ref, w_ref, dy_ref, ds_ref, dx_ref, dxb_ref, dw_ref):
        xf = x_ref[...]
        g = dy_ref[...]
        r = lax.rsqrt(jnp.mean(xf * xf, axis=-1, keepdims=True) + RMS_EPS)
        gw = g * w_ref[...]
        proj = jnp.mean(gw * xf, axis=-1, keepdims=True)
        dx = r * gw - xf * (r * r * r * proj) + ds_ref[...]
        dx_ref[...] = dx
        dxb_ref[...] = dx.astype(dxb_ref.dtype)
        part = jnp.sum(g * xf * r, axis=0, keepdims=True)

        @pl.when(pl.program_id(0) == 0)
        def _():
            dw_ref[...] = part

        @pl.when(pl.program_id(0) > 0)
        def _():
            dw_ref[...] += part

    row = pl.BlockSpec((tb, d), lambda i: (i, 0))
    one = pl.BlockSpec((1, d), lambda i: (0, 0))
    return pl.pallas_call(
        body,
        grid=(t // tb,),
        in_specs=[row, one, row, row],
        out_specs=[row, row, one],
        out_shape=[jax.ShapeDtypeStruct((t, d), F32), jax.ShapeDtypeStruct((t, d), MM_DTYPE),
                   jax.ShapeDtypeStruct((1, d), F32)],
        compiler_params=_cparams(("arbitrary",)),
        name=name,
    )(x, w_row, dy, dskip)


def _silu(z):
    return z / (1.0 + jnp.exp(-z))


FFN_TM, FFN_TN = 512, 1408


def _ffn_in(hn, in_t, *, name):
    t, d = hn.shape
    h = FFN_HIDDEN
    tm, tn = min(t, FFN_TM), FFN_TN
    nj = h // tn
    dn = (((1,), (1,)), ((), ()))

    def body(a_ref, bg_ref, bu_ref, g_ref, u_ref, act_ref):
        a = a_ref[...]
        g = lax.dot_general(a, bg_ref[...], dn, preferred_element_type=F32)
        u = lax.dot_general(a, bu_ref[...], dn, preferred_element_type=F32)
        g_ref[...] = g.astype(g_ref.dtype)
        u_ref[...] = u.astype(u_ref.dtype)
        act_ref[...] = (_silu(g) * u).astype(act_ref.dtype)

    out = pl.BlockSpec((tm, tn), lambda j, i: (i, j))
    return pl.pallas_call(
        body,
        grid=(nj, t // tm),
        in_specs=[pl.BlockSpec((tm, d), lambda j, i: (i, 0)), pl.BlockSpec((tn, d), lambda j, i: (j, 0)),
                  pl.BlockSpec((tn, d), lambda j, i: (j + nj, 0))],
        out_specs=[out, out, out],
        out_shape=[jax.ShapeDtypeStruct((t, h), MM_DTYPE)] * 3,
        compiler_params=_cparams(("parallel", "parallel")),
        name=name,
    )(hn, in_t, in_t)


def _ffn_dact(dy, out_w, g, u, *, name):
    t, d = dy.shape
    h = FFN_HIDDEN
    tm, tn = min(t, FFN_TM), FFN_TN

    def body(a_ref, b_ref, g_ref, u_ref, d_ref):
        da = lax.dot_general(a_ref[...], b_ref[...], (((1,), (1,)), ((), ())), preferred_element_type=F32)
        gate = g_ref[...].astype(F32)
        sig = 1.0 / (1.0 + jnp.exp(-gate))
        sg = gate * sig
        d_ref[0] = (da * u_ref[...].astype(F32) * (sig + sg * (1.0 - sig))).astype(d_ref.dtype)
        d_ref[1] = (da * sg).astype(d_ref.dtype)

    blk = pl.BlockSpec((tm, tn), lambda j, i: (i, j))
    return pl.pallas_call(
        body,
        grid=(h // tn, t // tm),
        in_specs=[pl.BlockSpec((tm, d), lambda j, i: (i, 0)), pl.BlockSpec((tn, d), lambda j, i: (j, 0)), blk, blk],
        out_specs=pl.BlockSpec((2, tm, tn), lambda j, i: (0, i, j)),
        out_shape=jax.ShapeDtypeStruct((2, t, h), MM_DTYPE),
        compiler_params=_cparams(("parallel", "parallel")),
        name=name,
    )(dy, out_w, g, u)


def _loss_head(y, target, *, name):
    t, d = y.shape
    tb = min(t, 1024)

    def body(y_ref, t_ref, dy_ref, dyb_ref, l_ref):
        err = y_ref[...] - t_ref[...]
        dy_ref[...] = err * (1.0 / d)
        dyb_ref[...] = (err * (1.0 / d)).astype(dyb_ref.dtype)
        part = jnp.sum(jnp.sum(err * err, axis=0, keepdims=True), axis=1, keepdims=True) * (0.5 / d)
        part = jnp.broadcast_to(part, l_ref.shape)

        @pl.when(pl.program_id(0) == 0)
        def _():
            l_ref[...] = part

        @pl.when(pl.program_id(0) > 0)
        def _():
            l_ref[...] += part

    row = pl.BlockSpec((tb, d), lambda i: (i, 0))
    return pl.pallas_call(
        body,
        grid=(t // tb,),
        in_specs=[row, row],
        out_specs=[row, row, pl.BlockSpec((8, LANES), lambda i: (0, 0))],
        out_shape=[jax.ShapeDtypeStruct((t, d), F32), jax.ShapeDtypeStruct((t, d), MM_DTYPE),
                   jax.ShapeDtypeStruct((8, LANES), F32)],
        compiler_params=_cparams(("arbitrary",)),
        name=name,
    )(y, target)


CONV_HALO = 8
CONV_TIME_TILE = 2048


def _conv_tile_scale(c):
    is_qk = c < 2 * GDN_HEADS
    scale = jnp.where(c < GDN_HEADS, GDN_DK ** -0.5, 1.0).astype(F32)
    return is_qk, scale


def _gdn_conv_fwd(pm, conv_w, *, name):
    t = pm.shape[0]
    tb = min(t, CONV_TIME_TILE)
    nt = t // tb
    hb = tb // CONV_HALO

    def body(x_ref, xp_ref, w_ref, o_ref, xe_ref):
        c = pl.program_id(0)
        ti = pl.program_id(1)
        xe_ref[0:CONV_HALO, :] = jnp.where(ti > 0, xp_ref[...], 0.0)
        xe_ref[CONV_HALO:CONV_HALO + tb, :] = x_ref[...]
        w = w_ref[...]
        y = jnp.zeros((tb, LANES), F32)
        for j in range(GDN_CONV):
            off = CONV_HALO - (GDN_CONV - 1) + j
            y = y + w[j:j + 1, :] * xe_ref[pl.ds(off, tb), :]
        s = _silu(y)
        is_qk, scale = _conv_tile_scale(c)
        r = lax.rsqrt(jnp.sum(s * s, axis=-1, keepdims=True) + L2_EPS) * scale
        o_ref[...] = s * jnp.where(is_qk, r, 1.0)

    return pl.pallas_call(
        body,
        grid=(GDN_QKV // LANES, nt),
        in_specs=[
            pl.BlockSpec((tb, LANES), lambda c, i: (i, c)),
            pl.BlockSpec((CONV_HALO, LANES), lambda c, i: (jnp.maximum(i * hb - 1, 0), c)),
            pl.BlockSpec((GDN_CONV, LANES), lambda c, i: (0, c)),
        ],
        out_specs=pl.BlockSpec((tb, LANES), lambda c, i: (i, c)),
        out_shape=jax.ShapeDtypeStruct((t, GDN_QKV), F32),
        scratch_shapes=[pltpu.VMEM((tb + CONV_HALO, LANES), F32)],
        compiler_params=_cparams(("parallel", "parallel")),
        name=name,
    )(pm, pm, conv_w)


def _gdn_conv_bwd(pm, conv_w, dout, *, name):
    t = pm.shape[0]
    tb = min(t, CONV_TIME_TILE)
    nt = t // tb
    hb = tb // CONV_HALO
    last_hb = t // CONV_HALO - 1
    ext = tb + CONV_HALO

    def body(x_ref, xp_ref, xn_ref, d_ref, dn_ref, w_ref, dx_ref, dw_ref, xe_ref, dy_ref):
        c = pl.program_id(0)
        ti = pl.program_id(1)
        has_next = ti < nt - 1
        xe_ref[0:CONV_HALO, :] = jnp.where(ti > 0, xp_ref[...], 0.0)
        xe_ref[CONV_HALO:CONV_HALO + tb, :] = x_ref[...]
        xe_ref[CONV_HALO + tb:2 * CONV_HALO + tb, :] = jnp.where(has_next, xn_ref[...], 0.0)
        de = jnp.concatenate([d_ref[...], jnp.where(has_next, dn_ref[...], 0.0)], axis=0)
        w = w_ref[...]
        y = jnp.zeros((ext, LANES), F32)
        for j in range(GDN_CONV):
            off = CONV_HALO - (GDN_CONV - 1) + j
            y = y + w[j:j + 1, :] * xe_ref[pl.ds(off, ext), :]
        sig = 1.0 / (1.0 + jnp.exp(-y))
        s = y * sig
        is_qk, scale = _conv_tile_scale(c)
        r = lax.rsqrt(jnp.sum(s * s, axis=-1, keepdims=True) + L2_EPS)
        n = s * r
        dnrm = de * scale
        ds_qk = r * (dnrm - n * jnp.sum(dnrm * n, axis=-1, keepdims=True))
        ds = jnp.where(is_qk, ds_qk, de)
        dy_ref[...] = ds * (sig + s * (1.0 - sig))
        dy = dy_ref[0:tb, :]
        dx = jnp.zeros((tb, LANES), F32)
        dw_rows = []
        for j in range(GDN_CONV):
            sh = GDN_CONV - 1 - j
            dx = dx + w[j:j + 1, :] * dy_ref[pl.ds(sh, tb), :]
            off = CONV_HALO - (GDN_CONV - 1) + j
            dw_rows.append(jnp.sum(dy * xe_ref[pl.ds(off, tb), :], axis=0, keepdims=True))
        dx_ref[...] = dx.astype(dx_ref.dtype)
        part = jnp.concatenate(dw_rows, axis=0)

        @pl.when(ti == 0)
        def _():
            dw_ref[...] = part

        @pl.when(ti > 0)
        def _():
            dw_ref[...] += part

    main = pl.BlockSpec((tb, LANES), lambda c, i: (i, c))
    prev = pl.BlockSpec((CONV_HALO, LANES), lambda c, i: (jnp.maximum(i * hb - 1, 0), c))
    nxt = pl.BlockSpec((CONV_HALO, LANES), lambda c, i: (jnp.minimum((i + 1) * hb, last_hb), c))
    return pl.pallas_call(
        body,
        grid=(GDN_QKV // LANES, nt),
        in_specs=[main, prev, nxt, main, nxt, pl.BlockSpec((GDN_CONV, LANES), lambda c, i: (0, c))],
        out_specs=[main, pl.BlockSpec((GDN_CONV, LANES), lambda c, i: (0, c))],
        out_shape=[jax.ShapeDtypeStruct((t, GDN_QKV), MM_DTYPE), jax.ShapeDtypeStruct((GDN_CONV, GDN_QKV), F32)],
        scratch_shapes=[pltpu.VMEM((tb + 2 * CONV_HALO, LANES), F32), pltpu.VMEM((ext, LANES), F32)],
        compiler_params=_cparams(("parallel", "arbitrary")),
        name=name,
    )(pm, pm, pm, dout, dout, conv_w)


def _head_selector(first_col):
    row = lax.broadcasted_iota(jnp.int32, (LANES, GDN_HEADS * LANES), 0)
    col = lax.broadcasted_iota(jnp.int32, (LANES, GDN_HEADS * LANES), 1)
    return (col // LANES + first_col == row).astype(BF16)


def _spread_columns(cols, first_col):
    sel = _head_selector(first_col)
    return sum(_bdot(p, sel) for p in _bf16_pieces(cols))


def _gather_columns(wide, first_col):
    sel = _head_selector(first_col)
    return sum(_bdot_nt(p, sel) for p in _bf16_pieces(wide))


def _softplus(x):
    return jnp.maximum(x, 0.0) + jnp.log(1.0 + jnp.exp(-jnp.abs(x)))


def _gdn_gates_fwd(ab, alog_row, dt_row, *, name):
    t = ab.shape[0]
    tb = min(t, 1024)
    wide = GDN_HEADS * LANES

    def body(ab_ref, al_ref, dt_ref, g_ref, b_ref):
        x = ab_ref[...]
        g_cols = -jnp.exp(al_ref[...]) * _softplus(x + dt_ref[...])
        b_cols = 1.0 / (1.0 + jnp.exp(-x))
        g_ref[...] = _spread_columns(g_cols, 0)
        b_ref[...] = _spread_columns(b_cols, GDN_HEADS)

    row = pl.BlockSpec((tb, LANES), lambda i: (i, 0))
    one = pl.BlockSpec((1, LANES), lambda i: (0, 0))
    out = pl.BlockSpec((tb, wide), lambda i: (i, 0))
    return pl.pallas_call(
        body,
        grid=(t // tb,),
        in_specs=[row, one, one],
        out_specs=[out, out],
        out_shape=[jax.ShapeDtypeStruct((t, wide), F32)] * 2,
        compiler_params=_cparams(("parallel",)),
        name=name,
    )(ab, alog_row, dt_row)


def _gdn_gates_bwd(ab, alog_row, dt_row, dgb, dbb, *, name):
    t = ab.shape[0]
    tb = min(t, 1024)
    wide = GDN_HEADS * LANES

    def body(ab_ref, al_ref, dt_ref, dg_ref, db_ref, dab_ref, dal_ref, ddt_ref):
        x = ab_ref[...]
        lane = lax.broadcasted_iota(jnp.int32, (tb, LANES), 1)
        dg_cols = _gather_columns(dg_ref[...], 0)
        db_cols = _gather_columns(db_ref[...], GDN_HEADS)
        ea = jnp.exp(al_ref[...])
        z = x + dt_ref[...]
        sp = _softplus(z)
        sg = 1.0 / (1.0 + jnp.exp(-z))
        beta = 1.0 / (1.0 + jnp.exp(-x))
        da = jnp.where(lane < GDN_HEADS, dg_cols * (-ea) * sg, 0.0)
        db = jnp.where((lane >= GDN_HEADS) & (lane < 2 * GDN_HEADS), db_cols * beta * (1.0 - beta), 0.0)
        dab_ref[...] = (da + db).astype(dab_ref.dtype)
        p_al = jnp.sum(jnp.where(lane < GDN_HEADS, dg_cols * (-ea) * sp, 0.0), axis=0, keepdims=True)
        p_dt = jnp.sum(da, axis=0, keepdims=True)

        @pl.when(pl.program_id(0) == 0)
        def _():
            dal_ref[...] = p_al
            ddt_ref[...] = p_dt

        @pl.when(pl.program_id(0) > 0)
        def _():
            dal_ref[...] += p_al
            ddt_ref[...] += p_dt

    row = pl.BlockSpec((tb, LANES), lambda i: (i, 0))
    one = pl.BlockSpec((1, LANES), lambda i: (0, 0))
    big = pl.BlockSpec((tb, wide), lambda i: (i, 0))
    return pl.pallas_call(
        body,
        grid=(t // tb,),
        in_specs=[row, one, one, big, big],
        out_specs=[row, one, one],
        out_shape=[jax.ShapeDtypeStruct((t, LANES), MM_DTYPE), jax.ShapeDtypeStruct((1, LANES), F32),
                   jax.ShapeDtypeStruct((1, LANES), F32)],
        compiler_params=_cparams(("arbitrary",)),
        name=name,
    )(ab, alog_row, dt_row, dgb, dbb)


@jax.custom_vjp
def _unit_lower_inverse_rest(n):
    c = n.shape[-1]
    ri = lax.broadcasted_iota(jnp.int32, (c, c), 0)
    ci = lax.broadcasted_iota(jnp.int32, (c, c), 1)
    rest = None
    size = 1
    while size < c:
        joins = ((ri // (2 * size)) == (ci // (2 * size))) & ((ri // size) != (ci // size))
        low = jnp.where(joins, n, 0.0)
        if rest is None:
            rest = -low
        else:
            left = low + _bdot(rest, low)
            rest = rest - (left + _bdot(left, rest))
        size *= 2
    return rest


def _unit_lower_inverse_rest_fwd(n):
    rest = _unit_lower_inverse_rest(n)
    return rest, rest


def _unit_lower_inverse_rest_bwd(rest, ct):
    left = ct + _bdot_tn(rest, ct)
    return (-(left + _bdot_nt(left, rest)),)


_unit_lower_inverse_rest.defvjp(_unit_lower_inverse_rest_fwd, _unit_lower_inverse_rest_bwd)


@jax.custom_vjp
def _known_inverse_rest(n, rest):
    return rest


def _known_inverse_rest_fwd(n, rest):
    return rest, rest


def _known_inverse_rest_bwd(rest, ct):
    return _unit_lower_inverse_rest_bwd(rest, ct) + (jnp.zeros_like(rest),)


_known_inverse_rest.defvjp(_known_inverse_rest_fwd, _known_inverse_rest_bwd)


def _bf16_pieces(x):
    hi = x.astype(BF16)
    r1 = x - hi.astype(F32)
    mid = r1.astype(BF16)
    lo = (r1 - mid.astype(F32)).astype(BF16)
    return hi, mid, lo


def _lower_ones(shape):
    c = shape[-1]
    ri = lax.broadcasted_iota(jnp.int32, (c, c), 0)
    ci = lax.broadcasted_iota(jnp.int32, (c, c), 1)
    return jnp.broadcast_to((ri >= ci).astype(BF16), shape)


@jax.custom_vjp
def _running_sum(x):
    tri = _lower_ones(x.shape)
    return sum(_bdot(tri, p) for p in _bf16_pieces(x))


def _running_sum_fwd(x):
    return _running_sum(x), None


def _running_sum_bwd(_, ct):
    tri = _lower_ones(ct.shape)
    return (sum(_bdot_tn(tri, p) for p in _bf16_pieces(ct)),)


_running_sum.defvjp(_running_sum_fwd, _running_sum_bwd)


def _gdn_prep_math(q, k, v, gb, bb, known_rest=None, with_rest=False):
    c = GDN_CHUNK
    ri = lax.broadcasted_iota(jnp.int32, (c, c), 0)
    ci = lax.broadcasted_iota(jnp.int32, (c, c), 1)
    causal = ri >= ci
    gc = _running_sum(gb)
    decay = jnp.exp(jnp.where(causal, gc - jnp.swapaxes(gc, -1, -2), NEG))
    n = jnp.where(ri > ci, _bdot_nt(k, k) * bb * decay, 0.0)
    rest = _unit_lower_inverse_rest(n) if known_rest is None else _known_inverse_rest(n, known_rest)
    eg = jnp.exp(gc)
    rhs_v = v * bb
    rhs_k = k * bb * eg
    u = rhs_v + _bdot(rest, rhs_v)
    w = rhs_k + _bdot(rest, rhs_k)
    qk = _bdot_nt(q, k) * decay
    qd = q * eg
    last = jnp.sum(jnp.where(ri == c - 1, gc, 0.0), axis=-2, keepdims=True)
    gl = jnp.broadcast_to(last, gc.shape)
    kt = k * jnp.exp(gl - gc)
    cd = jnp.exp(gl)
    return (u, w, qk, qd, kt, cd, rest) if with_rest else (u, w, qk, qd, kt, cd)


def _head_tiles(ref, h):
    return ref[:, h * LANES:(h + 1) * LANES]


def _stack_heads(ref, first=0, heads=GDN_HEADS):
    return jnp.stack([_head_tiles(ref, first + h) for h in range(heads)])


def _store_heads(ref, val, first=0):
    for h in range(val.shape[0]):
        ref[:, (first + h) * LANES:(first + h + 1) * LANES] = val[h].astype(ref.dtype)


def _gdn_prep_fwd(qkv, gb, bb, *, name):
    t = qkv.shape[0]
    c = GDN_CHUNK
    wide = GDN_HEADS * LANES

    def body(q_ref, k_ref, v_ref, g_ref, b_ref, *outs):
        res = _gdn_prep_math(*(_stack_heads(r) for r in (q_ref, k_ref, v_ref, g_ref, b_ref)), with_rest=True)
        for o_ref, val in zip(outs, res):
            _store_heads(o_ref, val)

    blk = lambda off: pl.BlockSpec((c, wide), lambda i: (i, off))
    outs = pl.pallas_call(
        body,
        grid=(t // c,),
        in_specs=[blk(0), blk(1), blk(2), blk(0), blk(0)],
        out_specs=[blk(0)] * 7,
        out_shape=[jax.ShapeDtypeStruct((t, wide), dt) for dt in (F32, MM_DTYPE, MM_DTYPE, MM_DTYPE, MM_DTYPE, F32, F32)],
        compiler_params=_cparams(("parallel",)),
        name=name,
    )(qkv, qkv, qkv, gb, bb)
    return tuple(outs[:6]), outs[6]


def _gdn_prep_bwd(qkv, gb, bb, rest, cts, *, name):
    t = qkv.shape[0]
    c = GDN_CHUNK
    wide = GDN_HEADS * LANES

    def body(q_ref, k_ref, v_ref, g_ref, b_ref, r_ref, c0, c1, c2, c3, c4, c5, dqkv_ref, dg_ref, db_ref):
        prim = tuple(_stack_heads(r) for r in (q_ref, k_ref, v_ref, g_ref, b_ref))
        _, pull = jax.vjp(functools.partial(_gdn_prep_math, known_rest=_stack_heads(r_ref)), *prim)
        dq, dk, dv, dg, db = pull(tuple(_stack_heads(r) for r in (c0, c1, c2, c3, c4, c5)))
        _store_heads(dqkv_ref, dq)
        _store_heads(dqkv_ref, dk, first=GDN_HEADS)
        _store_heads(dqkv_ref, dv, first=2 * GDN_HEADS)
        _store_heads(dg_ref, dg)
        _store_heads(db_ref, db)

    blk = lambda off: pl.BlockSpec((c, wide), lambda i: (i, off))
    return pl.pallas_call(
        body,
        grid=(t // c,),
        in_specs=[blk(0), blk(1), blk(2), blk(0), blk(0)] + [blk(0)] * 7,
        out_specs=[pl.BlockSpec((c, 3 * wide), lambda i: (i, 0)), blk(0), blk(0)],
        out_shape=[jax.ShapeDtypeStruct((t, 3 * wide), F32), jax.ShapeDtypeStruct((t, wide), F32),
                   jax.ShapeDtypeStruct((t, wide), F32)],
        compiler_params=_cparams(("parallel",)),
        name=name,
    )(qkv, qkv, qkv, gb, bb, rest, *cts)


def _gdn_scan_math(s, u, w, qk, qd, kt, cd):
    v_new = u - _bdot(w, s)
    o = _bdot(qd, s) + _bdot(qk, v_new)
    s_new = s * cd + _bdot_tn(kt, v_new)
    return o, s_new


def _gdn_scan_fwd(prep, *, name):
    t = prep[0].shape[0]
    c = GDN_CHUNK
    wide = GDN_HEADS * LANES

    def body(u_ref, w_ref, qk_ref, qd_ref, kt_ref, cd_ref, o_ref, st_ref, s_ref):
        @pl.when(pl.program_id(0) == 0)
        def _():
            s_ref[...] = jnp.zeros_like(s_ref)

        s = _stack_heads(s_ref)
        _store_heads(st_ref, s)
        o, s_new = _gdn_scan_math(s, *(_stack_heads(r).astype(F32) for r in (u_ref, w_ref, qk_ref, qd_ref, kt_ref, cd_ref)))
        _store_heads(o_ref, o)
        _store_heads(s_ref, s_new)

    blk = pl.BlockSpec((c, wide), lambda i: (i, 0))
    return pl.pallas_call(
        body,
        grid=(t // c,),
        in_specs=[blk] * 6,
        out_specs=[blk, blk],
        out_shape=[jax.ShapeDtypeStruct((t, wide), F32)] * 2,
        scratch_shapes=[pltpu.VMEM((GDN_DK, wide), F32)],
        compiler_params=_cparams(("arbitrary",)),
        name=name,
    )(*prep)


def _gdn_scan_bwd(prep, states, do, *, name):
    t = do.shape[0]
    c = GDN_CHUNK
    wide = GDN_HEADS * LANES
    nc = t // c

    def body(u_ref, w_ref, qk_ref, qd_ref, kt_ref, cd_ref, st_ref, do_ref, *rest):
        outs, ds_ref = rest[:6], rest[6]

        @pl.when(pl.program_id(0) == 0)
        def _():
            ds_ref[...] = jnp.zeros_like(ds_ref)

        prim = tuple(_stack_heads(r).astype(F32) for r in (st_ref, u_ref, w_ref, qk_ref, qd_ref, kt_ref, cd_ref))
        _, pull = jax.vjp(_gdn_scan_math, *prim)
        grads = pull((_stack_heads(do_ref), _stack_heads(ds_ref)))
        _store_heads(ds_ref, grads[0])
        for o_ref, val in zip(outs, grads[1:]):
            _store_heads(o_ref, val)

    blk = pl.BlockSpec((c, wide), lambda i: (nc - 1 - i, 0))
    return pl.pallas_call(
        body,
        grid=(nc,),
        in_specs=[blk] * 8,
        out_specs=[blk] * 6,
        out_shape=[jax.ShapeDtypeStruct((t, wide), F32)] * 6,
        scratch_shapes=[pltpu.VMEM((GDN_DK, wide), F32)],
        compiler_params=_cparams(("arbitrary",)),
        name=name,
    )(*prep, states, do)


def _gdn_outgate_math(o, z, nw):
    r = lax.rsqrt(jnp.mean(o * o, axis=-1, keepdims=True) + RMS_EPS)
    return o * r * nw * _silu(z)


def _gdn_outgate_fwd(o, pm, nw_row, *, name):
    t = o.shape[0]
    tb = min(t, ROW_TILE)
    wide = GDN_HEADS * LANES
    z_at = GDN_QKV // wide

    def body(o_ref, z_ref, nw_ref, y_ref):
        for h in range(GDN_HEADS):
            y = _gdn_outgate_math(_head_tiles(o_ref, h), _head_tiles(z_ref, h), nw_ref[...])
            y_ref[:, h * LANES:(h + 1) * LANES] = y.astype(y_ref.dtype)

    return pl.pallas_call(
        body,
        grid=(t // tb,),
        in_specs=[pl.BlockSpec((tb, wide), lambda i: (i, 0)), pl.BlockSpec((tb, wide), lambda i: (i, z_at)),
                  pl.BlockSpec((1, LANES), lambda i: (0, 0))],
        out_specs=pl.BlockSpec((tb, wide), lambda i: (i, 0)),
        out_shape=jax.ShapeDtypeStruct((t, wide), MM_DTYPE),
        compiler_params=_cparams(("parallel",)),
        name=name,
    )(o, pm, nw_row)


def _gdn_outgate_bwd(o, pm, nw_row, dy, *, name):
    t = o.shape[0]
    tb = min(t, ROW_TILE)
    wide = GDN_HEADS * LANES
    z_at = GDN_QKV // wide

    def body(o_ref, z_ref, nw_ref, dy_ref, do_ref, dz_ref, dnw_ref):
        total = jnp.zeros((1, LANES), F32)
        for h in range(GDN_HEADS):
            _, pull = jax.vjp(_gdn_outgate_math, _head_tiles(o_ref, h), _head_tiles(z_ref, h), nw_ref[...])
            d_o, d_z, d_nw = pull(_head_tiles(dy_ref, h))
            do_ref[:, h * LANES:(h + 1) * LANES] = d_o
            dz_ref[:, h * LANES:(h + 1) * LANES] = d_z.astype(dz_ref.dtype)
            total = total + d_nw

        @pl.when(pl.program_id(0) == 0)
        def _():
            dnw_ref[...] = total

        @pl.when(pl.program_id(0) > 0)
        def _():
            dnw_ref[...] += total

    blk = pl.BlockSpec((tb, wide), lambda i: (i, 0))
    one = pl.BlockSpec((1, LANES), lambda i: (0, 0))
    return pl.pallas_call(
        body,
        grid=(t // tb,),
        in_specs=[blk, pl.BlockSpec((tb, wide), lambda i: (i, z_at)), one, blk],
        out_specs=[blk, blk, one],
        out_shape=[jax.ShapeDtypeStruct((t, wide), F32), jax.ShapeDtypeStruct((t, wide), MM_DTYPE),
                   jax.ShapeDtypeStruct((1, LANES), F32)],
        compiler_params=_cparams(("arbitrary",)),
        name=name,
    )(o, pm, nw_row, dy)


def _rms64(x, w_row):
    return x * lax.rsqrt(jnp.sum(x * x, axis=-1, keepdims=True) * (1.0 / DIL_DH) + RMS_EPS) * w_row


def _alibi_slopes(group):
    head = lax.broadcasted_iota(jnp.int32, (DIL_HEADS, 8, LANES), 0).astype(F32)
    rate = -math.log(2.0) * ALIBI_MAX_BIAS / (len(DIL_GROUPS) * DIL_HEADS)
    slope = jnp.exp(rate * (head + float(group * DIL_HEADS + 1)))
    return jnp.broadcast_to(slope[:, 0:1, :], (DIL_HEADS, DIL_SPAN, LANES))


def _band_logits(qn, kp, kc, slope_d, has_prev):
    qi = lax.broadcasted_iota(jnp.int32, (DIL_SPAN, DIL_SPAN), 0)
    kj = lax.broadcasted_iota(jnp.int32, (DIL_SPAN, DIL_SPAN), 1)
    steps_c = (qi - kj).astype(F32)
    scale = DIL_DH ** -0.5
    sp = _bdot_nt(qn, kp) * scale - slope_d * (steps_c + float(DIL_SPAN))
    sc = _bdot_nt(qn, kc) * scale - slope_d * steps_c
    sp = jnp.where((kj >= qi) & has_prev, sp, NEG)
    sc = jnp.where(kj <= qi, sc, NEG)
    return sp, sc


def _dil_attn_fwd(slab, wq_row, wk_row, *, group, name):
    dilation = DIL_GROUPS[group][1]
    t = slab.shape[0]
    rows = t // dilation
    nlb = rows // DIL_SPAN
    wide = DIL_HEADS * LANES
    view = slab.reshape(rows, dilation * DIL_SLAB)

    def body(q_ref, kc_ref, vc_ref, kp_ref, vp_ref, wq_ref, wk_ref, o_ref):
        has_prev = pl.program_id(1) > 0
        lane = lax.broadcasted_iota(jnp.int32, (DIL_SPAN, LANES), 1)
        qn = _rms64(_stack_heads(q_ref), wq_ref[...])
        kc = _rms64(_stack_heads(kc_ref), wk_ref[...])
        kp = _rms64(_stack_heads(kp_ref), wk_ref[...])
        sp, sc = _band_logits(qn, kp, kc, _alibi_slopes(group) * float(dilation), has_prev)
        m = jnp.maximum(jnp.max(sp, axis=-1, keepdims=True), jnp.max(sc, axis=-1, keepdims=True))
        pp = jnp.exp(sp - m)
        pc = jnp.exp(sc - m)
        l = jnp.sum(pp, axis=-1, keepdims=True) + jnp.sum(pc, axis=-1, keepdims=True)
        o = (_bdot(pp, _stack_heads(vp_ref)) + _bdot(pc, _stack_heads(vc_ref))) / l
        _store_heads(o_ref, jnp.where(lane < DIL_DH, o, m + jnp.log(l)))

    cur = lambda part: pl.BlockSpec((DIL_SPAN, wide), lambda r, i: (i, 3 * r + part))
    prv = lambda part: pl.BlockSpec((DIL_SPAN, wide), lambda r, i: (jnp.maximum(i - 1, 0), 3 * r + part))
    one = pl.BlockSpec((1, LANES), lambda r, i: (0, 0))
    out = pl.pallas_call(
        body,
        grid=(dilation, nlb),
        in_specs=[cur(0), cur(1), cur(2), prv(1), prv(2), one, one],
        out_specs=pl.BlockSpec((DIL_SPAN, wide), lambda r, i: (i, r)),
        out_shape=jax.ShapeDtypeStruct((rows, dilation * wide), F32),
        compiler_params=_cparams(("parallel", "parallel")),
        name=name,
    )(view, view, view, view, view, wq_row, wk_row)
    return out.reshape(t, wide)


def _head_slope(group, head):
    idx = jnp.zeros((8, LANES), F32) + head.astype(F32)
    rate = -math.log(2.0) * ALIBI_MAX_BIAS / (len(DIL_GROUPS) * DIL_HEADS)
    slope = jnp.exp(rate * (idx + float(group * DIL_HEADS + 1)))
    return jnp.broadcast_to(slope[0:1, :], (DIL_SPAN, LANES))


def _take_residues(ref, d):
    return jnp.stack([ref[pl.ds(r, DIL_SPAN, stride=d), :] for r in range(d)])


def _put_residues(ref, val, d):
    for r in range(d):
        ref[pl.ds(r, DIL_SPAN, stride=d), :] = val[r]


def _dil_attn_fwd_strided(slab, wq_row, wk_row, *, group, name):
    d = DIL_GROUPS[group][1]
    t = slab.shape[0]
    span = DIL_SPAN * d
    nsb = t // span

    def body(q_ref, kc_ref, vc_ref, kp_ref, vp_ref, wq_ref, wk_ref, o_ref):
        has_prev = pl.program_id(0) > 0
        lane = lax.broadcasted_iota(jnp.int32, (DIL_SPAN, LANES), 1)
        qn = _rms64(_take_residues(q_ref, d), wq_ref[...])
        kc = _rms64(_take_residues(kc_ref, d), wk_ref[...])
        kp = _rms64(_take_residues(kp_ref, d), wk_ref[...])
        sp, sc = _band_logits(qn, kp, kc, _head_slope(group, pl.program_id(1)) * float(d), has_prev)
        m = jnp.maximum(jnp.max(sp, axis=-1, keepdims=True), jnp.max(sc, axis=-1, keepdims=True))
        pp = jnp.exp(sp - m)
        pc = jnp.exp(sc - m)
        l = jnp.sum(pp, axis=-1, keepdims=True) + jnp.sum(pc, axis=-1, keepdims=True)
        o = (_bdot(pp, _take_residues(vp_ref, d)) + _bdot(pc, _take_residues(vc_ref, d))) / l
        _put_residues(o_ref, jnp.where(lane < DIL_DH, o, m + jnp.log(l)), d)

    cur = lambda part: pl.BlockSpec((span, LANES), lambda i, h: (i, part * DIL_HEADS + h))
    prv = lambda part: pl.BlockSpec((span, LANES), lambda i, h: (jnp.maximum(i - 1, 0), part * DIL_HEADS + h))
    one = pl.BlockSpec((1, LANES), lambda i, h: (0, 0))
    return pl.pallas_call(
        body,
        grid=(nsb, DIL_HEADS),
        in_specs=[cur(0), cur(1), cur(2), prv(1), prv(2), one, one],
        out_specs=pl.BlockSpec((span, LANES), lambda i, h: (i, h)),
        out_shape=jax.ShapeDtypeStruct((t, DIL_HEADS * LANES), F32),
        compiler_params=_cparams(("parallel", "parallel")),
        name=name,
    )(slab, slab, slab, slab, slab, wq_row, wk_row)


def _dil_attn_bwd_strided(slab, stat, wq_row, wk_row, dwq_in, dwk_in, *, group, name):
    d = DIL_GROUPS[group][1]
    t = slab.shape[0]
    span = DIL_SPAN * d
    nsb = t // span

    def body(q_ref, kc_ref, vc_ref, kp_ref, vp_ref, st_ref, wq_ref, wk_ref, dwq_in_ref, dwk_in_ref,
             d_ref, dwq_ref, dwk_ref, dk_carry, dv_carry, spread):
        step = pl.program_id(1)
        has_prev = step < nsb - 1
        first = (pl.program_id(0) == 0) & (step == 0)

        @pl.when(step == 0)
        def _():
            dk_carry[...] = jnp.zeros_like(dk_carry)
            dv_carry[...] = jnp.zeros_like(dv_carry)

        @pl.when(first)
        def _():
            dwq_ref[...] = dwq_in_ref[...]
            dwk_ref[...] = dwk_in_ref[...]

        lane = lax.broadcasted_iota(jnp.int32, (DIL_SPAN, LANES), 1)
        scale = DIL_DH ** -0.5
        q_raw = _take_residues(q_ref, d)
        kc_raw = _take_residues(kc_ref, d)
        vc = _take_residues(vc_ref, d)
        kp_raw = _take_residues(kp_ref, d)
        vp = _take_residues(vp_ref, d)
        st = _take_residues(st_ref, d)
        d_o = jnp.where(lane < DIL_DH, st, 0.0)
        lse = jnp.sum(jnp.where(lane == DIL_DH, st, 0.0), axis=-1, keepdims=True)
        delta = jnp.sum(jnp.where(lane == DIL_DH + 1, st, 0.0), axis=-1, keepdims=True)
        qn = _rms64(q_raw, wq_ref[...])
        kc = _rms64(kc_raw, wk_ref[...])
        kp = _rms64(kp_raw, wk_ref[...])
        sp, sc = _band_logits(qn, kp, kc, _head_slope(group, pl.program_id(0)) * float(d), has_prev)
        pp = jnp.exp(sp - lse)
        pc = jnp.exp(sc - lse)
        dsp = pp * (_bdot_nt(d_o, vp) - delta) * scale
        dsc = pc * (_bdot_nt(d_o, vc) - delta) * scale
        dqn = _bdot(dsp, kp) + _bdot(dsc, kc)
        dkc_n = _bdot_tn(dsc, qn) + dk_carry[...]
        dvc = _bdot_tn(pc, d_o) + dv_carry[...]
        dk_carry[...] = _bdot_tn(dsp, qn)
        dv_carry[...] = _bdot_tn(pp, d_o)
        dq_raw, dwq_rows = _rms64_bwd(q_raw, wq_ref[...], dqn)
        dk_raw, dwk_rows = _rms64_bwd(kc_raw, wk_ref[...], dkc_n)
        for part, val in enumerate((dq_raw, dk_raw, dvc)):
            _put_residues(spread, val, d)
            d_ref[part] = spread[...].astype(d_ref.dtype)
        dwq_ref[...] += jnp.sum(jnp.sum(dwq_rows, axis=0), axis=0, keepdims=True)
        dwk_ref[...] += jnp.sum(jnp.sum(dwk_rows, axis=0), axis=0, keepdims=True)

    at = lambda i: nsb - 1 - i
    cur = lambda part: pl.BlockSpec((span, LANES), lambda h, i: (at(i), part * DIL_HEADS + h))
    prv = lambda part: pl.BlockSpec((span, LANES), lambda h, i: (jnp.maximum(at(i) - 1, 0), part * DIL_HEADS + h))
    one = pl.BlockSpec((1, LANES), lambda h, i: (0, 0))
    return pl.pallas_call(
        body,
        grid=(DIL_HEADS, nsb),
        in_specs=[cur(0), cur(1), cur(2), prv(1), prv(2), pl.BlockSpec((span, LANES), lambda h, i: (at(i), h)),
                  one, one, one, one],
        out_specs=[pl.BlockSpec((3, span, LANES), lambda h, i: (0, at(i), h)), one, one],
        out_shape=[jax.ShapeDtypeStruct((3, t, DIL_HEADS * LANES), MM_DTYPE), jax.ShapeDtypeStruct((1, LANES), F32),
                   jax.ShapeDtypeStruct((1, LANES), F32)],
        scratch_shapes=[pltpu.VMEM((d, DIL_SPAN, LANES), F32), pltpu.VMEM((d, DIL_SPAN, LANES), F32),
                        pltpu.VMEM((span, LANES), F32)],
        compiler_params=_cparams(("arbitrary", "arbitrary")),
        name=name,
    )(slab, slab, slab, slab, slab, stat, wq_row, wk_row, dwq_in, dwk_in)


def _dil_merge_fwd(oe, *, name):
    t = oe[0].shape[0]
    tb = min(t, ROW_TILE)
    wide = DIL_HEADS * LANES

    def body(e0, e1, e2, y_ref, om_ref):
        lane = lax.broadcasted_iota(jnp.int32, (tb, LANES), 1)
        for h in range(DIL_HEADS):
            es = [_head_tiles(e, h) for e in (e0, e1, e2)]
            lse = [jnp.sum(jnp.where(lane == DIL_DH, e, 0.0), axis=-1, keepdims=True) for e in es]
            top = jnp.maximum(jnp.maximum(lse[0], lse[1]), lse[2])
            joint = top + jnp.log(jnp.exp(lse[0] - top) + jnp.exp(lse[1] - top) + jnp.exp(lse[2] - top))
            o = sum(jnp.exp(l - joint) * e for l, e in zip(lse, es))
            y_ref[:, h * LANES:(h + 1) * LANES] = jnp.where(lane < DIL_DH, o, 0.0).astype(y_ref.dtype)
            om_ref[:, h * LANES:(h + 1) * LANES] = jnp.where(lane < DIL_DH, o, joint)

    blk = pl.BlockSpec((tb, wide), lambda i: (i, 0))
    return pl.pallas_call(
        body,
        grid=(t // tb,),
        in_specs=[blk] * 3,
        out_specs=[blk, blk],
        out_shape=[jax.ShapeDtypeStruct((t, wide), MM_DTYPE), jax.ShapeDtypeStruct((t, wide), F32)],
        compiler_params=_cparams(("parallel",)),
        name=name,
    )(*oe)


def _dil_merge_bwd(dy, om, *, name):
    t = dy.shape[0]
    tb = min(t, ROW_TILE)
    wide = DIL_HEADS * LANES

    def body(dy_ref, om_ref, st_ref):
        lane = lax.broadcasted_iota(jnp.int32, (tb, LANES), 1)
        for h in range(DIL_HEADS):
            d_o = jnp.where(lane < DIL_DH, _head_tiles(dy_ref, h), 0.0)
            om_t = _head_tiles(om_ref, h)
            delta = jnp.sum(d_o * om_t, axis=-1, keepdims=True)
            st_ref[:, h * LANES:(h + 1) * LANES] = jnp.where(
                lane < DIL_DH, d_o, jnp.where(lane == DIL_DH, om_t, jnp.where(lane == DIL_DH + 1, delta, 0.0)))

    blk = pl.BlockSpec((tb, wide), lambda i: (i, 0))
    return pl.pallas_call(
        body,
        grid=(t // tb,),
        in_specs=[blk, blk],
        out_specs=blk,
        out_shape=jax.ShapeDtypeStruct((t, wide), F32),
        compiler_params=_cparams(("parallel",)),
        name=name,
    )(dy, om)


def _rms64_bwd(x, w_row, dy):
    r = lax.rsqrt(jnp.sum(x * x, axis=-1, keepdims=True) * (1.0 / DIL_DH) + RMS_EPS)
    gw = dy * w_row
    dx = r * gw - x * (r * r * r * jnp.sum(gw * x, axis=-1, keepdims=True) * (1.0 / DIL_DH))
    return dx, dy * x * r


def _dil_attn_bwd(slab, stat, wq_row, wk_row, dwq_in, dwk_in, *, group, name):
    dilation = DIL_GROUPS[group][1]
    t = slab.shape[0]
    rows = t // dilation
    nlb = rows // DIL_SPAN
    wide = DIL_HEADS * LANES
    view = slab.reshape(rows, dilation * DIL_SLAB)
    stat_view = stat.reshape(rows, dilation * wide)

    def body(cur_ref, kp_ref, vp_ref, st_ref, wq_ref, wk_ref, dwq_in_ref, dwk_in_ref, d_ref, dwq_ref, dwk_ref,
             dk_carry, dv_carry):
        step = pl.program_id(1)
        has_prev = step < nlb - 1
        first = (pl.program_id(0) == 0) & (step == 0)

        @pl.when(step == 0)
        def _():
            dk_carry[...] = jnp.zeros_like(dk_carry)
            dv_carry[...] = jnp.zeros_like(dv_carry)

        @pl.when(first)
        def _():
            dwq_ref[...] = dwq_in_ref[...]
            dwk_ref[...] = dwk_in_ref[...]

        lane = lax.broadcasted_iota(jnp.int32, (DIL_SPAN, LANES), 1)
        scale = DIL_DH ** -0.5
        q_raw = _stack_heads(cur_ref)
        kc_raw = _stack_heads(cur_ref, first=DIL_HEADS)
        vc = _stack_heads(cur_ref, first=2 * DIL_HEADS)
        kp_raw = _stack_heads(kp_ref)
        vp = _stack_heads(vp_ref)
        st = _stack_heads(st_ref)
        d_o = jnp.where(lane < DIL_DH, st, 0.0)
        lse = jnp.sum(jnp.where(lane == DIL_DH, st, 0.0), axis=-1, keepdims=True)
        delta = jnp.sum(jnp.where(lane == DIL_DH + 1, st, 0.0), axis=-1, keepdims=True)
        qn = _rms64(q_raw, wq_ref[...])
        kc = _rms64(kc_raw, wk_ref[...])
        kp = _rms64(kp_raw, wk_ref[...])
        sp, sc = _band_logits(qn, kp, kc, _alibi_slopes(group) * float(dilation), has_prev)
        pp = jnp.exp(sp - lse)
        pc = jnp.exp(sc - lse)
        dsp = pp * (_bdot_nt(d_o, vp) - delta) * scale
        dsc = pc * (_bdot_nt(d_o, vc) - delta) * scale
        dqn = _bdot(dsp, kp) + _bdot(dsc, kc)
        dkc_n = _bdot_tn(dsc, qn) + _stack_heads(dk_carry)
        dvc = _bdot_tn(pc, d_o) + _stack_heads(dv_carry)
        _store_heads(dk_carry, _bdot_tn(dsp, qn))
        _store_heads(dv_carry, _bdot_tn(pp, d_o))
        dq_raw, dwq_rows = _rms64_bwd(q_raw, wq_ref[...], dqn)
        dk_raw, dwk_rows = _rms64_bwd(kc_raw, wk_ref[...], dkc_n)
        _store_heads(d_ref, dq_raw)
        _store_heads(d_ref, dk_raw, first=DIL_HEADS)
        _store_heads(d_ref, dvc, first=2 * DIL_HEADS)
        dwq_ref[...] += jnp.sum(jnp.sum(dwq_rows, axis=0), axis=0, keepdims=True)
        dwk_ref[...] += jnp.sum(jnp.sum(dwk_rows, axis=0), axis=0, keepdims=True)

    blk_i = lambda i: nlb - 1 - i
    cur = pl.BlockSpec((DIL_SPAN, DIL_SLAB), lambda r, i: (blk_i(i), r))
    prv = lambda part: pl.BlockSpec((DIL_SPAN, wide), lambda r, i: (jnp.maximum(blk_i(i) - 1, 0), 3 * r + part))
    one = pl.BlockSpec((1, LANES), lambda r, i: (0, 0))
    dslab, dwq, dwk = pl.pallas_call(
        body,
        grid=(dilation, nlb),
        in_specs=[cur, prv(1), prv(2), pl.BlockSpec((DIL_SPAN, wide), lambda r, i: (blk_i(i), r)), one, one, one, one],
        out_specs=[cur, one, one],
        out_shape=[jax.ShapeDtypeStruct((rows, dilation * DIL_SLAB), MM_DTYPE), jax.ShapeDtypeStruct((1, LANES), F32),
                   jax.ShapeDtypeStruct((1, LANES), F32)],
        scratch_shapes=[pltpu.VMEM((DIL_SPAN, wide), F32), pltpu.VMEM((DIL_SPAN, wide), F32)],
        compiler_params=_cparams(("arbitrary", "arbitrary")),
        name=name,
    )(view, view, view, stat_view, wq_row, wk_row, dwq_in, dwk_in)
    return dslab.reshape(t, DIL_SLAB), dwq, dwk


def _row(v, width=LANES):
    v = v.astype(F32).reshape(-1)
    return jnp.pad(v, (0, width - v.shape[0])).reshape(1, width)


def _prepare_weights(w):
    return dict(gdn=_prepare_gdn(w), dil=_prepare_dil(w), ffn=_prepare_ffn(w))


def _prepare_gdn(w, layers=range(DEPTH // 2)):
    gdn = {}
    for j in layers:
        wt = w["gdn_w_in"][j]
        gates_t = jnp.pad(wt[GDN_MAIN:], ((0, LANES - 2 * GDN_HEADS), (0, 0)))
        gdn[j] = dict(in_t=wt, gates_t=gates_t, out=w["gdn_w_out"][j], conv=w["gdn_conv_w"][j].astype(F32),
                      alog=_row(w["gdn_a_log"][j]), dt=_row(w["gdn_dt_bias"][j]), nw=_row(w["gdn_norm_w"][j]))
    return gdn


def _prepare_dil(w, layers=range(DEPTH // 2)):
    d = D_MODEL
    dil = {}
    for j in layers:
        wt = w["dil_w_in"][j].reshape(3, len(DIL_GROUPS), DIL_HEADS, DIL_DH, d)
        wg_t = [wt[:, g].reshape(DIL_SLAB // 2, d) for g in range(len(DIL_GROUPS))]
        out_t = jnp.pad(w["dil_w_out"][j].reshape(d, DIL_HEADS, DIL_DH), ((0, 0), (0, 0), (0, LANES - DIL_DH)))
        dil[j] = dict(wg_t=wg_t, out_t=out_t.reshape(d, DIL_HEADS * LANES), wq=_row(w["dil_q_norm"][j]),
                      wk=_row(w["dil_k_norm"][j]))
    return dil


def _prepare_ffn(w, layers=range(DEPTH)):
    return {i: dict(in_t=w["ffn_w_in"][i], out=w["ffn_w_out"][i]) for i in layers}


def _gdn_layer_fwd(x, nrow, p):
    hn = _rmsnorm_fwd(x, nrow, name="rmsnorm_fwd")
    pm = _matmul(hn, p["in_t"], trans_b=True, b_rows=(0, GDN_MAIN), name="gdn_proj_main")
    ab = _matmul(hn, p["gates_t"], trans_b=True, name="gdn_proj_gates")
    qkv = _gdn_conv_fwd(pm, p["conv"], name="gdn_conv_fwd")
    gb, bb = _gdn_gates_fwd(ab, p["alog"], p["dt"], name="gdn_gates_fwd")
    prep, rest = _gdn_prep_fwd(qkv, gb, bb, name="gdn_prep_fwd")
    o, states = _gdn_scan_fwd(prep, name="gdn_scan_fwd")
    og = _gdn_outgate_fwd(o, pm, p["nw"], name="gdn_outgate_fwd")
    y = _matmul(og, p["out"], add=x, name="gdn_proj_out")
    return y, (x, hn, pm, ab, qkv, gb, bb, prep, rest, states, o, og)


def _gdn_layer_bwd(dx, dxb, nrow, p, saved):
    x, hn, pm, ab, qkv, gb, bb, prep, rest, states, o, og = saved
    d_og = _matmul(dxb, p["out"], trans_b=True, name="gdn_dgate")
    g_out = _matmul(og, dxb, trans_a=True, out_dtype=MM_DTYPE, name="gdn_gw_out")
    d_o, d_z, d_nw = _gdn_outgate_bwd(o, pm, p["nw"], d_og, name="gdn_outgate_bwd")
    cts = _gdn_scan_bwd(prep, states, d_o, name="gdn_scan_bwd")
    dqkv, dgb, dbb = _gdn_prep_bwd(qkv, gb, bb, rest, cts, name="gdn_prep_bwd")
    d_ab, d_alog, d_dt = _gdn_gates_bwd(ab, p["alog"], p["dt"], dgb, dbb, name="gdn_gates_bwd")
    d_conv, g_conv = _gdn_conv_bwd(pm, p["conv"], dqkv, name="gdn_conv_bwd")
    d_hn = _matmul(d_conv, p["in_t"], b_rows=(0, GDN_QKV), name="gdn_dhn_qkv")
    d_hn = _matmul(d_z, p["in_t"], b_rows=(GDN_QKV, GDN_MAIN - GDN_QKV), add=d_hn, name="gdn_dhn_z")
    d_hn = _matmul(d_ab, p["gates_t"], add=d_hn, name="gdn_dhn_gates")
    g_in_t = jnp.concatenate([
        _matmul(d_conv, hn, trans_a=True, out_dtype=MM_DTYPE, name="gdn_gw_qkv"),
        _matmul(d_z, hn, trans_a=True, out_dtype=MM_DTYPE, name="gdn_gw_z"),
        _matmul(d_ab, hn, trans_a=True, out_dtype=MM_DTYPE, name="gdn_gw_gates")[:2 * GDN_HEADS],
    ], axis=0)
    dx_new, dxb_new, g_norm = _rmsnorm_bwd(x, nrow, d_hn, dx, name="rmsnorm_bwd")
    grads = dict(w_in=g_in_t, conv=g_conv, a_log=d_alog[0, :GDN_HEADS], dt_bias=d_dt[0, :GDN_HEADS], norm_w=d_nw[0],
                 w_out=g_out, norm=g_norm[0])
    return dx_new, dxb_new, grads


def _dil_layer_fwd(x, nrow, p):
    hn = _rmsnorm_fwd(x, nrow, name="rmsnorm_fwd")
    slabs = [_matmul(hn, p["wg_t"][g], trans_b=True, spread_out=True, name="dil_proj_in") for g in range(len(DIL_GROUPS))]
    oe = [(_dil_attn_fwd if DIL_GROUPS[g][1] == 1 else _dil_attn_fwd_strided)(
        slabs[g], p["wq"], p["wk"], group=g, name=f"dil_attn_fwd_g{g}") for g in range(len(DIL_GROUPS))]
    y, om = _dil_merge_fwd(oe, name="dil_merge_fwd")
    out = _matmul(y, p["out_t"], trans_b=True, add=x, name="dil_proj_out")
    return out, (x, hn, slabs, y, om)


def _dil_layer_bwd(dx, dxb, nrow, p, saved):
    x, hn, slabs, y, om = saved
    d_y = _matmul(dxb, p["out_t"], name="dil_dmerged")
    g_out_t = _matmul(dxb, y, trans_a=True, out_dtype=MM_DTYPE, name="dil_gw_out")
    g_out_t = g_out_t.reshape(D_MODEL, DIL_HEADS, LANES)[..., :DIL_DH].reshape(D_MODEL, DIL_HEADS * DIL_DH)
    stat = _dil_merge_bwd(d_y, om, name="dil_merge_bwd")
    d_hn = None
    dwq = jnp.zeros((1, LANES), F32)
    dwk = jnp.zeros((1, LANES), F32)
    g_groups = []
    wide = DIL_HEADS * LANES
    for g in range(len(DIL_GROUPS)):
        if DIL_GROUPS[g][1] == 1:
            dslab, dwq, dwk = _dil_attn_bwd(slabs[g], stat, p["wq"], p["wk"], dwq, dwk, group=g, name=f"dil_attn_bwd_g{g}")
            d_hn = _matmul(dslab, p["wg_t"][g], packed_a=True, add=d_hn, name="dil_dhn")
            g_w = _matmul(dslab, hn, trans_a=True, packed_a=True, out_dtype=MM_DTYPE, name="dil_gw_in")
        else:
            dparts, dwq, dwk = _dil_attn_bwd_strided(slabs[g], stat, p["wq"], p["wk"], dwq, dwk, group=g,
                                                     name=f"dil_attn_bwd_g{g}")
            d_hn = _matmul(dparts, p["wg_t"][g], a_lead="k", packed_a=True, add=d_hn, name="dil_dhn_parts")
            g_w = _matmul(dparts, hn, trans_a=True, a_lead="i", packed_a=True, out_dtype=MM_DTYPE, name="dil_gw_in_parts")
        g_groups.append(g_w.reshape(3, DIL_HEADS, DIL_DH, D_MODEL))
    g_in_t = jnp.stack(g_groups, axis=1).reshape(3 * len(DIL_GROUPS) * DIL_HEADS * DIL_DH, D_MODEL)
    dx_new, dxb_new, g_norm = _rmsnorm_bwd(x, nrow, d_hn, dx, name="rmsnorm_bwd")
    grads = dict(w_in=g_in_t, q_norm=dwq[0, :DIL_DH], k_norm=dwk[0, :DIL_DH], w_out=g_out_t, norm=g_norm[0])
    return dx_new, dxb_new, grads


def _ffn_layer_fwd(x, nrow, p):
    hn = _rmsnorm_fwd(x, nrow, name="rmsnorm_fwd")
    gate, up, act = _ffn_in(hn, p["in_t"], name="ffn_proj_in")
    y = _matmul(act, p["out"], add=x, name="ffn_proj_out")
    return y, (x, hn, gate, up, act)


def _ffn_layer_bwd(dx, dxb, nrow, p, saved):
    x, hn, gate, up, act = saved
    g_out = _matmul(act, dxb, trans_a=True, out_dtype=MM_DTYPE, name="ffn_gw_out")
    d_gu = _ffn_dact(dxb, p["out"], gate, up, name="ffn_dact")
    d_hn = _matmul(d_gu, p["in_t"], a_lead="k", name="ffn_dhn")
    g_in_t = _matmul(d_gu, hn, trans_a=True, a_lead="i", out_dtype=MM_DTYPE, name="ffn_gw_in")
    dx_new, dxb_new, g_norm = _rmsnorm_bwd(x, nrow, d_hn, dx, name="rmsnorm_bwd")
    return dx_new, dxb_new, dict(w_in=g_in_t, w_out=g_out, norm=g_norm[0])


def _mixer_fwd(i, x, mix_row, prepared):
    if i % 2 == 0:
        return _gdn_layer_fwd(x, mix_row, prepared["gdn"][i // 2])
    return _dil_layer_fwd(x, mix_row, prepared["dil"][i // 2])


def _mixer_bwd(i, dx, dxb, mix_row, prepared, saved, zero=0.0):
    if i % 2 == 0:
        p = prepared["gdn"][i // 2]
        return _gdn_layer_bwd(dx, dxb, mix_row, dict(p, nw=p["nw"] + zero), saved)
    p = prepared["dil"][i // 2]
    return _dil_layer_bwd(dx, dxb, mix_row, dict(p, wq=p["wq"] + zero), saved)


def _local_step(x, target, prepared, norm_mix, norm_ffn):
    saved = []
    for i in range(DEPTH):
        x, s_mix = _mixer_fwd(i, x, norm_mix[i].reshape(1, D_MODEL), prepared)
        x, s_ffn = _ffn_layer_fwd(x, norm_ffn[i].reshape(1, D_MODEL), prepared["ffn"][i])
        saved.append((s_mix, s_ffn))
    dx, dxb, loss = _loss_head(x, target, name="loss_head")
    g_mix, g_ffn = [None] * DEPTH, [None] * DEPTH
    for i in reversed(range(DEPTH)):
        s_mix, s_ffn = saved[i]
        dx, dxb, g_ffn[i] = _ffn_layer_bwd(dx, dxb, norm_ffn[i].reshape(1, D_MODEL), prepared["ffn"][i], s_ffn)
        dx, dxb, g_mix[i] = _mixer_bwd(i, dx, dxb, norm_mix[i].reshape(1, D_MODEL), prepared, s_mix)
    return loss[0, 0], dx, _collect_grads(g_mix, g_ffn)


def _collect_grads(g_mix, g_ffn):
    gdn = [g_mix[i] for i in range(0, DEPTH, 2)]
    dil = [g_mix[i] for i in range(1, DEPTH, 2)]
    if any(g is None for g in g_mix + g_ffn):
        pick = lambda gs, key: [None if g is None else g[key] for g in gs]
        return dict(gdn_w_in=pick(gdn, "w_in"), gdn_w_out=pick(gdn, "w_out"), dil_w_in=pick(dil, "w_in"),
                    dil_w_out=pick(dil, "w_out"), ffn_w_in=pick(g_ffn, "w_in"), ffn_w_out=pick(g_ffn, "w_out"))
    grads = dict(
        norm_mix=jnp.stack([g["norm"] for g in g_mix]),
        norm_ffn=jnp.stack([g["norm"] for g in g_ffn]),
        gdn_w_in=[g["w_in"] for g in gdn],
        gdn_conv_w=jnp.stack([g["conv"] for g in gdn]),
        gdn_a_log=jnp.stack([g["a_log"] for g in gdn]),
        gdn_dt_bias=jnp.stack([g["dt_bias"] for g in gdn]),
        gdn_norm_w=jnp.stack([g["norm_w"] for g in gdn]),
        gdn_w_out=[g["w_out"] for g in gdn],
        dil_w_in=[g["w_in"] for g in dil],
        dil_q_norm=jnp.stack([g["q_norm"] for g in dil]),
        dil_k_norm=jnp.stack([g["k_norm"] for g in dil]),
        dil_w_out=[g["w_out"] for g in dil],
        ffn_w_in=[g["w_in"] for g in g_ffn],
        ffn_w_out=[g["w_out"] for g in g_ffn],
    )
    return grads


MESH_ID = pl.DeviceIdType.MESH
ANY_SPACE = pl.BlockSpec(memory_space=pl.ANY)


def _mesh_position():
    return lax.axis_index("x"), lax.axis_index("y"), lax.axis_index("c")


def _flip(pos, k):
    x, y, c = pos
    return (1 - x if k & 4 else x, 1 - y if k & 2 else y, 1 - c if k & 1 else c)


def _linear(pos):
    return 4 * pos[0] + 2 * pos[1] + pos[2]


def _comm_scratch():
    return [pltpu.SemaphoreType.DMA((N_DEV - 1,)), pltpu.SemaphoreType.DMA((N_DEV - 1,)), pltpu.SemaphoreType.DMA(())]


def _all_gather(shard, *, name):
    def body(x_ref, out_ref, send_sems, recv_sems, local_sem):
        me = _mesh_position()
        mine = out_ref.at[_linear(me)]
        local = pltpu.make_async_copy(x_ref, mine, local_sem)
        local.start()
        copies = []
        for k in range(1, N_DEV):
            cp = pltpu.make_async_remote_copy(src_ref=x_ref, dst_ref=mine, send_sem=send_sems.at[k - 1],
                                              recv_sem=recv_sems.at[k - 1], device_id=_flip(me, k), device_id_type=MESH_ID)
            cp.start()
            copies.append(cp)
        for cp in copies:
            cp.wait()
        local.wait()

    return pl.pallas_call(
        body,
        out_shape=jax.ShapeDtypeStruct((N_DEV,) + shard.shape, shard.dtype),
        in_specs=[ANY_SPACE],
        out_specs=ANY_SPACE,
        scratch_shapes=_comm_scratch(),
        name=name,
    )(shard)


def _exchange(parts, *, name):
    def body(p_ref, out_ref, send_sems, recv_sems, local_sem):
        me = _mesh_position()
        mine = out_ref.at[_linear(me)]
        local = pltpu.make_async_copy(p_ref.at[_linear(me)], mine, local_sem)
        local.start()
        copies = []
        for k in range(1, N_DEV):
            peer = _flip(me, k)
            cp = pltpu.make_async_remote_copy(src_ref=p_ref.at[_linear(peer)], dst_ref=mine, send_sem=send_sems.at[k - 1],
                                              recv_sem=recv_sems.at[k - 1], device_id=peer, device_id_type=MESH_ID)
            cp.start()
            copies.append(cp)
        for cp in copies:
            cp.wait()
        local.wait()

    return pl.pallas_call(
        body,
        out_shape=jax.ShapeDtypeStruct(parts.shape, parts.dtype),
        in_specs=[ANY_SPACE],
        out_specs=ANY_SPACE,
        scratch_shapes=_comm_scratch(),
        name=name,
    )(parts)


HBM_SPACE = pl.BlockSpec(memory_space=pltpu.HBM)
SEM_SPACE = pl.BlockSpec(memory_space=pltpu.SEMAPHORE)
DATAFLOW = pltpu.SideEffectType.DATAFLOW_SIDE_EFFECTING


def _split_copies(src_ref, land_ref, send_sems, recv_sems, per_peer):
    me = _mesh_position()
    mine = land_ref.at[_linear(me)]
    copies = []
    for k in range(1, N_DEV):
        peer = _flip(me, k)
        src = src_ref.at[_linear(peer)] if per_peer else src_ref
        copies.append(pltpu.make_async_remote_copy(src_ref=src, dst_ref=mine, send_sem=send_sems.at[k - 1],
                                                   recv_sem=recv_sems.at[k - 1], device_id=peer, device_id_type=MESH_ID))
    return copies


def _travel_start(src, after, *, per_peer, name):
    me = _linear(_mesh_position())
    own = src[me] if per_peer else src
    shape = own.shape
    landing = lax.dynamic_update_slice(lax.empty((N_DEV,) + shape, src.dtype), own[None], (me, 0, 0))

    def body(src_ref, land_ref, after_ref, send_sems, recv_sems, src_thru, land_thru, token):
        for cp in _split_copies(src_ref, land_ref, send_sems, recv_sems, per_peer):
            cp.start()
        token[...] = jnp.zeros_like(token)

    return pl.pallas_call(
        body,
        name=name,
        out_shape=(pltpu.SemaphoreType.DMA((N_DEV - 1,)), pltpu.SemaphoreType.DMA((N_DEV - 1,)),
                   pltpu.HBM(src.shape, src.dtype), pltpu.HBM(landing.shape, landing.dtype),
                   jax.ShapeDtypeStruct((8, LANES), F32)),
        in_specs=(HBM_SPACE, HBM_SPACE, ANY_SPACE),
        out_specs=(SEM_SPACE, SEM_SPACE, HBM_SPACE, HBM_SPACE, pl.BlockSpec(memory_space=pltpu.VMEM)),
        input_output_aliases={0: 2, 1: 3},
        compiler_params=pltpu.CompilerParams(has_side_effects=DATAFLOW),
    )(pltpu.with_memory_space_constraint(src, pltpu.HBM), pltpu.with_memory_space_constraint(landing, pltpu.HBM), after)


def _travel_wait(started, after, *, per_peer, name):
    send_sems, recv_sems, src_thru, land_thru, _ = started

    def body(src_ref, land_ref, send_sems, recv_sems, after_ref, src_dead, got_ref):
        for cp in _split_copies(src_ref, land_ref, send_sems, recv_sems, per_peer):
            cp.wait_send()
            cp.wait_recv()

    return pl.pallas_call(
        body,
        name=name,
        out_shape=(pltpu.HBM(src_thru.shape, src_thru.dtype), pltpu.HBM(land_thru.shape, land_thru.dtype)),
        in_specs=(HBM_SPACE, HBM_SPACE, SEM_SPACE, SEM_SPACE, ANY_SPACE),
        out_specs=(HBM_SPACE, HBM_SPACE),
        input_output_aliases={0: 0, 1: 1},
        compiler_params=pltpu.CompilerParams(has_side_effects=DATAFLOW),
    )(src_thru, land_thru, send_sems, recv_sems, after)[1]


def _adamw(parts, w, m, v, *, name):
    rows, n = w.shape
    tb = _pick(rows, (PACK_ROW_ALIGN, 16))
    c1 = 1.0 - ADAM_B1 ** ADAM_STEP
    c2 = 1.0 - ADAM_B2 ** ADAM_STEP

    def body(p_ref, w_ref, m_ref, v_ref, g_ref, d_ref, nm_ref, nv_ref):
        g = p_ref[0].astype(F32)
        for s in range(1, N_DEV):
            g = g + p_ref[s].astype(F32)
        m_new = ADAM_B1 * m_ref[...] + (1.0 - ADAM_B1) * g
        v_new = ADAM_B2 * v_ref[...] + (1.0 - ADAM_B2) * (g * g)
        m_hat = m_new / c1
        v_hat = v_new / c2
        g_ref[...] = g
        nm_ref[...] = m_new
        nv_ref[...] = v_new
        d_ref[...] = -ADAM_LR * (m_hat / (jnp.sqrt(v_hat) + ADAM_EPS) + ADAM_WD * w_ref[...])

    blk = pl.BlockSpec((tb, n), lambda i: (i, 0))
    return pl.pallas_call(
        body,
        grid=(rows // tb,),
        in_specs=[pl.BlockSpec((N_DEV, tb, n), lambda i: (0, i, 0)), blk, blk, blk],
        out_specs=[blk] * 4,
        out_shape=[jax.ShapeDtypeStruct((rows, n), F32)] * 4,
        compiler_params=_cparams(("parallel",)),
        name=name,
    )(parts, w, m, v)


PACK_WIDTH = 1024
SHARDED = {
    "gdn_w_in": ((2, D_MODEL, GDN_IN_WIDTH), 2),
    "gdn_conv_w": ((2, GDN_CONV, GDN_QKV), 2),
    "gdn_w_out": ((2, GDN_HEADS * GDN_DV, D_MODEL), 1),
    "dil_w_in": ((2, D_MODEL, 3 * len(DIL_GROUPS) * DIL_HEADS * DIL_DH), 2),
    "dil_w_out": ((2, DIL_HEADS * DIL_DH, D_MODEL), 2),
    "ffn_w_in": ((DEPTH, D_MODEL, 2 * FFN_HIDDEN), 2),
    "ffn_w_out": ((DEPTH, FFN_HIDDEN, D_MODEL), 1),
}
REPLICATED = {"norm_mix": (DEPTH, D_MODEL), "norm_ffn": (DEPTH, D_MODEL), "gdn_a_log": (2, GDN_HEADS),
              "gdn_dt_bias": (2, GDN_HEADS), "gdn_norm_w": (2, GDN_DV), "dil_q_norm": (2, DIL_DH), "dil_k_norm": (2, DIL_DH)}
WEIGHT_ORDER = ("norm_mix", "norm_ffn", "gdn_w_in", "gdn_conv_w", "gdn_a_log", "gdn_dt_bias", "gdn_norm_w", "gdn_w_out",
                "dil_w_in", "dil_q_norm", "dil_k_norm", "dil_w_out", "ffn_w_in", "ffn_w_out")
PACK_ROW_ALIGN = 128
PIECE_ALIGN = 16
SMALL_ROWS = 16


def _shard_shape(name):
    shape, axis = SHARDED[name]
    return tuple(s // N_DEV if i == axis else s for i, s in enumerate(shape))


def _shard_rows(name):
    return math.prod(_shard_shape(name)) // PACK_WIDTH


def _split_shards(full, name):
    shape, axis = SHARDED[name]
    split = full.reshape(shape[:axis] + (N_DEV, shape[axis] // N_DEV) + shape[axis + 1:])
    return jnp.moveaxis(split, axis, 0)


def _join_shards(stacked, name):
    shape, axis = SHARDED[name]
    return jnp.moveaxis(stacked, 0, axis).reshape(shape)


COLUMN_SHARDED = ("gdn_w_in", "dil_w_in", "dil_w_out", "ffn_w_in")


def _to_rows(shard, name):
    if name in COLUMN_SHARDED:
        shard = jnp.swapaxes(shard, 1, 2)
    return shard.reshape(-1, PACK_WIDTH)


def _layer_columns(name):
    _, r, c = _shard_shape(name)
    return r if name in COLUMN_SHARDED else c


def _piece_rows(piece, halves=1):
    name, layer = piece
    rows = _shard_rows(name) * halves
    return rows if layer is None else rows // SHARDED[name][0][0]


def _aligned(rows, to=PIECE_ALIGN):
    return -(-rows // to) * to


def _pack_pieces(arrays, total_align=PIECE_ALIGN):
    padded, total = [], 0
    for a in arrays:
        rows = a.shape[-2]
        extra = _aligned(rows) - rows
        if extra:
            a = jnp.pad(a, [(0, 0)] * (a.ndim - 2) + [(0, extra), (0, 0)])
        padded.append(a)
        total += rows + extra
    tail = _aligned(total, total_align) - total
    if tail:
        padded.append(jnp.zeros(padded[0].shape[:-2] + (tail, PACK_WIDTH), padded[0].dtype))
    return jnp.concatenate(padded, axis=-2)


def _piece_offsets(pieces, halves=None):
    out, at = [], 0
    for p in pieces:
        rows = _piece_rows(p, (halves or {}).get(p[0], 1))
        out.append((p, at, rows))
        at += _aligned(rows)
    return out


def _shard_piece_rows(src, piece):
    name, layer = piece
    part = src[name] if layer is None else src[name][layer:layer + 1]
    return _to_rows(part.astype(F32), name)


def _piece_from_rows(rows, piece):
    name, layer = piece
    layers, r, c = _shard_shape(name)
    n_l = layers if layer is None else 1
    if name in COLUMN_SHARDED:
        return jnp.swapaxes(rows.reshape(n_l, c, r), 1, 2)
    return rows.reshape(n_l, r, c)


SMALL_TAIL = tuple(n for n in REPLICATED if n not in ("norm_mix", "norm_ffn"))


def _pack_small(vals):
    tail, at = jnp.zeros((PACK_WIDTH,), F32), 0
    for n in SMALL_TAIL:
        vec = vals[n].astype(F32).reshape(-1)
        tail = tail + jnp.pad(vec, (at, PACK_WIDTH - at - vec.shape[0]))
        at += vec.shape[0]
    buf = jnp.pad(vals["norm_mix"].astype(F32), ((0, SMALL_ROWS - DEPTH), (0, 0)))
    buf = buf + jnp.pad(vals["norm_ffn"].astype(F32), ((8, SMALL_ROWS - 8 - DEPTH), (0, 0)))
    return buf + jnp.pad(tail.reshape(1, PACK_WIDTH), ((SMALL_ROWS - 1, 0), (0, 0)))


def _unpack_small(buf):
    out = {"norm_mix": buf[0:DEPTH], "norm_ffn": buf[8:8 + DEPTH]}
    at = 0
    for n in SMALL_TAIL:
        size = math.prod(REPLICATED[n])
        out[n] = buf[SMALL_ROWS - 1, at:at + size].reshape(REPLICATED[n])
        at += size
    return out


GATHER_FIRST = (("gdn_w_in", 0), ("gdn_conv_w", None), ("gdn_w_out", 0))
GATHER_NEXT = (("ffn_w_in", 0), ("ffn_w_out", 0), ("dil_w_in", 0), ("dil_w_out", 0))
GATHER_LAST = (("ffn_w_in", 1), ("ffn_w_out", 1), ("gdn_w_in", 1), ("gdn_w_out", 1), ("ffn_w_in", 2), ("ffn_w_out", 2),
               ("dil_w_in", 1), ("dil_w_out", 1), ("ffn_w_in", 3), ("ffn_w_out", 3))
EXCHANGE_GROUPS = (
    (("ffn_w_in", 3), ("ffn_w_out", 3), ("dil_w_in", 1), ("dil_w_out", 1),
     ("ffn_w_in", 2), ("ffn_w_out", 2), ("gdn_w_in", 1), ("gdn_w_out", 1)),
    (("ffn_w_in", 1), ("ffn_w_out", 1), ("dil_w_in", 0), ("dil_w_out", 0)),
    (("ffn_w_in", 0), ("ffn_w_out", 0)),
    (("gdn_w_in", 0), ("gdn_w_out", 0), ("gdn_conv_w", None)),
)
EXCHANGE_AFTER = {("mix", 2): 0, ("mix", 1): 1, ("ffn", 0): 2}


def _gather_operand(w, pieces):
    arrays = []
    for n, layer in pieces:
        if layer is None:
            arrays.append(lax.bitcast_convert_type(w[n], BF16).reshape(-1, PACK_WIDTH))
        else:
            arrays.append(_to_rows(w[n][layer:layer + 1].astype(BF16), n))
    return _pack_pieces(arrays)


def _gathered_weights(gathered, pieces, full):
    for (n, layer), at, rows in _piece_offsets(pieces, halves={"gdn_conv_w": 2}):
        block = gathered[:, at:at + rows]
        if layer is None:
            block = lax.bitcast_convert_type(block.reshape((N_DEV,) + _shard_shape(n) + (2,)), F32)
            full[n] = _join_shards(block, n)
        else:
            full.setdefault(n, {})[layer] = block.reshape(-1, _layer_columns(n))
    return full


def _exchange_operand(grads, pieces):
    arrays = []
    for n, layer in pieces:
        if layer is None:
            arrays.append(_split_shards(grads[n], n).astype(BF16).reshape(N_DEV, -1, PACK_WIDTH))
        else:
            arrays.append(grads[n][layer].astype(BF16).reshape(N_DEV, -1, PACK_WIDTH))
    return _pack_pieces(arrays, total_align=PACK_ROW_ALIGN)


def _update_group(received, pieces, w, m, v, *, name):
    packed = [_pack_pieces([_shard_piece_rows(src, p) for p in pieces], total_align=PACK_ROW_ALIGN) for src in (w, m, v)]
    outs = _adamw(received, *packed, name=name)
    return {p: tuple(_piece_from_rows(o[at:at + rows], p) for o in outs) for p, at, rows in _piece_offsets(pieces)}


def kernel(x, norm_mix, norm_ffn, gdn_w_in, gdn_conv_w, gdn_a_log, gdn_dt_bias, gdn_norm_w, gdn_w_out, dil_w_in, dil_q_norm, dil_k_norm, dil_w_out, ffn_w_in, ffn_w_out, loss_target, m_norm_mix, m_norm_ffn, m_gdn_w_in, m_gdn_conv_w, m_gdn_a_log, m_gdn_dt_bias, m_gdn_norm_w, m_gdn_w_out, m_dil_w_in, m_dil_q_norm, m_dil_k_norm, m_dil_w_out, m_ffn_w_in, m_ffn_w_out, v_norm_mix, v_norm_ffn, v_gdn_w_in, v_gdn_conv_w, v_gdn_a_log, v_gdn_dt_bias, v_gdn_norm_w, v_gdn_w_out, v_dil_w_in, v_dil_q_norm, v_dil_k_norm, v_dil_w_out, v_ffn_w_in, v_ffn_w_out):
    w = dict(norm_mix=norm_mix, norm_ffn=norm_ffn, gdn_w_in=gdn_w_in, gdn_conv_w=gdn_conv_w, gdn_a_log=gdn_a_log,
             gdn_dt_bias=gdn_dt_bias, gdn_norm_w=gdn_norm_w, gdn_w_out=gdn_w_out, dil_w_in=dil_w_in, dil_q_norm=dil_q_norm,
             dil_k_norm=dil_k_norm, dil_w_out=dil_w_out, ffn_w_in=ffn_w_in, ffn_w_out=ffn_w_out)
    m = dict(norm_mix=m_norm_mix, norm_ffn=m_norm_ffn, gdn_w_in=m_gdn_w_in, gdn_conv_w=m_gdn_conv_w, gdn_a_log=m_gdn_a_log,
             gdn_dt_bias=m_gdn_dt_bias, gdn_norm_w=m_gdn_norm_w, gdn_w_out=m_gdn_w_out, dil_w_in=m_dil_w_in,
             dil_q_norm=m_dil_q_norm, dil_k_norm=m_dil_k_norm, dil_w_out=m_dil_w_out, ffn_w_in=m_ffn_w_in, ffn_w_out=m_ffn_w_out)
    v = dict(norm_mix=v_norm_mix, norm_ffn=v_norm_ffn, gdn_w_in=v_gdn_w_in, gdn_conv_w=v_gdn_conv_w, gdn_a_log=v_gdn_a_log,
             gdn_dt_bias=v_gdn_dt_bias, gdn_norm_w=v_gdn_norm_w, gdn_w_out=v_gdn_w_out, dil_w_in=v_dil_w_in,
             dil_q_norm=v_dil_q_norm, dil_k_norm=v_dil_k_norm, dil_w_out=v_dil_w_out, ffn_w_in=v_ffn_w_in, ffn_w_out=v_ffn_w_out)
    def row(src, i):
        return src[i].reshape(1, D_MODEL)

    first = _all_gather(_gather_operand(w, GATHER_FIRST), name="weight_all_gather_first")
    next_started = _travel_start(_gather_operand(w, GATHER_NEXT), first, per_peer=False, name="weight_gather_start_next")
    last_started = _travel_start(_gather_operand(w, GATHER_LAST), next_started[4], per_peer=False,
                                 name="weight_gather_start_last")
    full = _gathered_weights(first, GATHER_FIRST, {n: w[n] for n in REPLICATED})
    prepared = dict(gdn=_prepare_gdn(full, layers=(0,)))
    h = x[0]
    saved = [None] * DEPTH
    h, s_mix = _mixer_fwd(0, h, row(norm_mix, 0) + last_started[4][0, 0], prepared)
    got = _travel_wait(next_started, h, per_peer=False, name="weight_gather_wait_next")
    full = _gathered_weights(got, GATHER_NEXT, full)
    prepared.update(dil=_prepare_dil(full, layers=(0,)), ffn=_prepare_ffn(full, layers=(0,)))
    for i in range(DEPTH):
        if i > 0:
            h, s_mix = _mixer_fwd(i, h, row(norm_mix, i), prepared)
        if i == 1:
            got = _travel_wait(last_started, h, per_peer=False, name="weight_gather_wait_last")
            full = _gathered_weights(got, GATHER_LAST, full)
            prepared["gdn"].update(_prepare_gdn(full, layers=(1,)))
            prepared["dil"].update(_prepare_dil(full, layers=(1,)))
            prepared["ffn"].update(_prepare_ffn(full, layers=(1, 2, 3)))
        h, s_ffn = _ffn_layer_fwd(h, row(norm_ffn, i), prepared["ffn"][i])
        saved[i] = (s_mix, s_ffn)
    dx, dxb, loss = _loss_head(h, loss_target[0], name="loss_head")

    g_mix, g_ffn = [None] * DEPTH, [None] * DEPTH
    started = {}

    def travel(group):
        operand = _exchange_operand(_collect_grads(g_mix, g_ffn), EXCHANGE_GROUPS[group])
        started[group] = _travel_start(operand, dx, per_peer=True, name=f"grad_exchange_start_{group}")
        return started[group][4][0, 0]

    zero = 0.0
    for i in reversed(range(DEPTH)):
        s_mix, s_ffn = saved[i]
        dx, dxb, g_ffn[i] = _ffn_layer_bwd(dx, dxb, row(norm_ffn, i) + zero, prepared["ffn"][i], s_ffn)
        zero = travel(EXCHANGE_AFTER[("ffn", i)]) if ("ffn", i) in EXCHANGE_AFTER else 0.0
        dx, dxb, g_mix[i] = _mixer_bwd(i, dx, dxb, row(norm_mix, i), prepared, s_mix, zero)
        zero = travel(EXCHANGE_AFTER[("mix", i)]) if ("mix", i) in EXCHANGE_AFTER else 0.0
    grads = _collect_grads(g_mix, g_ffn)
    received = [_travel_wait(started[g], dx, per_peer=True, name=f"grad_exchange_wait_{g}") for g in sorted(started)]
    received.append(_exchange(_exchange_operand(grads, EXCHANGE_GROUPS[-1]), name="grad_exchange_last"))
    updated = {}
    for g, pieces in enumerate(EXCHANGE_GROUPS):
        updated.update(_update_group(received[g], pieces, w, m, v, name=f"adamw_sharded_{g}"))

    small_parts = _all_gather(_pack_small(grads), name="small_grad_all_gather")
    outs_small = [_unpack_small(o) for o in
                  _adamw(small_parts, _pack_small(w), _pack_small(m), _pack_small(v), name="adamw_replicated")]

    total_loss = lax.psum(loss[0, 0], ("x", "y", "c"))
    result = [total_loss, dx[None]]
    for k in range(4):
        for n in WEIGHT_ORDER:
            if n not in SHARDED:
                result.append(outs_small[k][n])
            elif (n, None) in updated:
                result.append(updated[(n, None)][k])
            else:
                result.append(jnp.concatenate([updated[(n, l)][k] for l in range(SHARDED[n][0][0])], axis=0))
    return tuple(result)
```

```python
import functools
import math

import jax
import jax.numpy as jnp
from jax import lax
from jax.experimental import pallas as pl
from jax.experimental.pallas import tpu as pltpu

F32 = jnp.float32
BF16 = jnp.bfloat16
MM_DTYPE = BF16

N_DEV = 8
D_MODEL = 1024
DEPTH = 4
RMS_EPS = 1e-6
L2_EPS = 1e-6

LANES = 128

GDN_HEADS = 8
GDN_DK = 128
GDN_DV = 128
GDN_CONV = 4
GDN_CHUNK = 128
GDN_QKV = 3 * GDN_HEADS * GDN_DK
GDN_MAIN = GDN_QKV + GDN_HEADS * GDN_DV
GDN_IN_WIDTH = GDN_MAIN + 2 * GDN_HEADS

DIL_GROUPS = ((128, 1), (512, 4), (2048, 16))
DIL_HEADS = 8
DIL_DH = 64
DIL_SPAN = 128
DIL_SLAB = 3 * DIL_HEADS * LANES
ALIBI_MAX_BIAS = 8.0

FFN_HIDDEN = 2816

ADAM_LR = 0.001
ADAM_B1 = 0.9
ADAM_B2 = 0.999
ADAM_EPS = 1e-08
ADAM_WD = 0.01
ADAM_STEP = 10

VMEM_LIMIT = 56 * 1024 * 1024
ROW_TILE = 512
MATMUL_VMEM_BUDGET = 40 * 1024 * 1024
NEG = -1e30
HI = lax.Precision.HIGHEST


def _cparams(sem):
    return pltpu.CompilerParams(dimension_semantics=sem, vmem_limit_bytes=VMEM_LIMIT)


def _dot(a, b):
    return lax.dot_general(a, b, (((1,), (0,)), ((), ())), preferred_element_type=F32, precision=HI)


def _dot_nt(a, b):
    return lax.dot_general(a, b, (((1,), (1,)), ((), ())), preferred_element_type=F32, precision=HI)


def _dot_tn(a, b):
    return lax.dot_general(a, b, (((0,), (0,)), ((), ())), preferred_element_type=F32, precision=HI)


def _single_pass(a, b, a_dim, b_dim):
    lead = a.ndim - 2
    batch = ((0,), (0,)) if lead else ((), ())
    return lax.dot_general(a.astype(BF16), b.astype(BF16), (((lead + a_dim,), (lead + b_dim,)), batch),
                           preferred_element_type=F32)


def _bdot(a, b):
    return _single_pass(a, b, 1, 0)


def _bdot_nt(a, b):
    return _single_pass(a, b, 1, 1)


def _bdot_tn(a, b):
    return _single_pass(a, b, 0, 0)


def _pick(n, candidates):
    for c in candidates:
        if n % c == 0:
            return c
    raise ValueError(f"no tile for {n}")


HALF = LANES // 2


def _pack_head_pairs(x):
    x = x.astype(F32)
    tiles = [x[:, (2 * i) * LANES:(2 * i + 1) * LANES] + pltpu.roll(x[:, (2 * i + 1) * LANES:(2 * i + 2) * LANES], HALF, 1)
             for i in range(x.shape[1] // (2 * LANES))]
    return tiles[0] if len(tiles) == 1 else jnp.concatenate(tiles, axis=1)


def _spread_head_pairs(y):
    low = lax.broadcasted_iota(jnp.int32, (y.shape[0], LANES), 1) < HALF
    tiles = []
    for i in range(y.shape[1] // LANES):
        pair = y[:, i * LANES:(i + 1) * LANES]
        tiles += [jnp.where(low, pair, 0.0), jnp.where(low, pltpu.roll(pair, HALF, 1), 0.0)]
    return jnp.concatenate(tiles, axis=1)


def _matmul(a, b, *, name, trans_a=False, trans_b=False, b_rows=None, a_lead=None, add=None, out_dtype=F32,
            packed_a=False, spread_out=False, tiles=None):
    if trans_a:
        k_dim, m_dim = a.shape[-2:]
        m_dim = m_dim // 2 if packed_a else m_dim
    else:
        m_dim, k_dim = a.shape[-2:]
        k_dim = k_dim // 2 if packed_a else k_dim
    slab_m, slab_k = m_dim, k_dim
    if a_lead == "k":
        assert not trans_a
        k_dim *= a.shape[0]
    elif a_lead == "i":
        assert trans_a
        m_dim *= a.shape[0]
    b_start, b_size = b_rows if b_rows is not None else (0, b.shape[0])
    if trans_b:
        n_dim, k2 = b_size, b.shape[1]
    else:
        k2, n_dim = b_size, b.shape[1]
    assert k_dim == k2, (a.shape, b.shape, b_rows)
    tn = _pick(n_dim, (1024, 512, 256, 128))
    tm = min(slab_m, 2048, max(512, (1024 * 1024) // tn))
    tm = _pick(slab_m, (tm, 1408, 1024, 512, 256, 128))
    out_bytes = jnp.dtype(out_dtype).itemsize * (2 if spread_out else 1)

    def deepest(rows):
        fixed = rows * tn * (2 * out_bytes + 4 + (8 if add is not None else 0))
        fits = lambda c: fixed + 2 * 2 * c * ((2 if packed_a else 1) * rows + tn) <= MATMUL_VMEM_BUDGET
        return _pick(slab_k, tuple(c for c in (3072, 2816, 2048, 1536, 1408, 1024, 512, 256) if fits(c)) + (128,))

    tk = deepest(tm)
    if tm % 1024 == 0 and deepest(tm // 2) > tk:
        tm, tk = tm // 2, deepest(tm // 2)
    if tiles is not None:
        tm, tn, tk = tiles
    nk = k_dim // tk
    has_add = add is not None
    dn = (((0 if trans_a else 1,), (1 if trans_b else 0,)), ((), ()))
    b_tile = tn if trans_b else tk
    assert b_start % b_tile == 0, (b_rows, b_tile)
    b_off = b_start // b_tile

    def body(*refs):
        if has_add:
            a_ref, b_ref, add_ref, o_ref, acc_ref = refs
        else:
            a_ref, b_ref, o_ref, acc_ref = refs
        a_blk = _pack_head_pairs(a_ref[...]).astype(a_ref.dtype) if packed_a else a_ref[...]
        part = lax.dot_general(a_blk, b_ref[...], dn, preferred_element_type=F32)

        def finish(total):
            if has_add:
                total = total + add_ref[...]
            if spread_out:
                total = _spread_head_pairs(total)
            o_ref[...] = total.astype(out_dtype)

        if nk == 1:
            finish(part)
        else:
            k = pl.program_id(2)

            @pl.when(k == 0)
            def _():
                acc_ref[...] = part

            @pl.when(k > 0)
            def _():
                acc_ref[...] += part

            @pl.when(k == nk - 1)
            def _():
                finish(acc_ref[...])

    wide = 2 if packed_a else 1
    a_tile = (tk, wide * tm) if trans_a else (tm, wide * tk)
    a_at = (lambda i, j, k: (k, i)) if trans_a else (lambda i, j, k: (i, k))
    if a_lead is None:
        a_spec = pl.BlockSpec(a_tile, a_at)
    elif a_lead == "k":
        per = slab_k // tk
        a_spec = pl.BlockSpec((None,) + a_tile, lambda i, j, k: (k // per, i, k % per))
    elif a_lead == "i":
        per = slab_m // tm
        a_spec = pl.BlockSpec((None,) + a_tile, lambda i, j, k: (i // per, k, i % per))
    else:
        a_spec = pl.BlockSpec((None,) + a_tile, lambda i, j, k: (a_lead,) + a_at(i, j, k))
    if trans_b:
        b_spec = pl.BlockSpec((tn, tk), lambda i, j, k: (j + b_off, k))
    else:
        b_spec = pl.BlockSpec((tk, tn), lambda i, j, k: (k + b_off, j))
    in_specs = [a_spec, b_spec]
    args = [a, b]
    if has_add:
        in_specs.append(pl.BlockSpec((tm, tn), lambda i, j, k: (i, j)))
        args.append(add)
    return pl.pallas_call(
        body,
        grid=(m_dim // tm, n_dim // tn, nk),
        in_specs=in_specs,
        out_specs=pl.BlockSpec((tm, (2 if spread_out else 1) * tn), lambda i, j, k: (i, j)),
        out_shape=jax.ShapeDtypeStruct((m_dim, (2 if spread_out else 1) * n_dim), out_dtype),
        scratch_shapes=[pltpu.VMEM((tm, tn) if nk > 1 else (8, LANES), F32)],
        compiler_params=_cparams(("parallel", "parallel", "arbitrary")),
        name=name,
    )(*args)


def _rmsnorm_fwd(x, w_row, *, name):
    t, d = x.shape
    tb = min(t, 1024)

    def body(x_ref, w_ref, o_ref):
        xf = x_ref[...]
        r = lax.rsqrt(jnp.mean(xf * xf, axis=-1, keepdims=True) + RMS_EPS)
        o_ref[...] = (xf * r * w_ref[...]).astype(o_ref.dtype)

    return pl.pallas_call(
        body,
        grid=(t // tb,),
        in_specs=[pl.BlockSpec((tb, d), lambda i: (i, 0)), pl.BlockSpec((1, d), lambda i: (0, 0))],
        out_specs=pl.BlockSpec((tb, d), lambda i: (i, 0)),
        out_shape=jax.ShapeDtypeStruct((t, d), MM_DTYPE),
        compiler_params=_cparams(("parallel",)),
        name=name,
    )(x, w_row)


def _rmsnorm_bwd(x, w_row, dy, dskip, *, name):
    t, d = x.shape
    tb = min(t, 512)

    def body(x_ref, w_ref, dy_ref, ds_ref, dx_ref, dxb_ref, dw_ref):
        xf = x_ref[...]
        g = dy_ref[...]
        r = lax.rsqrt(jnp.mean(xf * xf, axis=-1, keepdims=True) + RMS_EPS)
        gw = g * w_ref[...]
        proj = jnp.mean(gw * xf, axis=-1, keepdims=True)
        dx = r * gw - xf * (r * r * r * proj) + ds_ref[...]
        dx_ref[...] = dx
        dxb_ref[...] = dx.astype(dxb_ref.dtype)
        part = jnp.sum(g * xf * r, axis=0, keepdims=True)

        @pl.when(pl.program_id(0) == 0)
        def _():
            dw_ref[...] = part

        @pl.when(pl.program_id(0) > 0)
        def _():
            dw_ref[...] += part

    row = pl.BlockSpec((tb, d), lambda i: (i, 0))
    one = pl.BlockSpec((1, d), lambda i: (0, 0))
    return pl.pallas_call(
        body,
        grid=(t // tb,),
        in_specs=[row, one, row, row],
        out_specs=[row, row, one],
        out_shape=[jax.ShapeDtypeStruct((t, d), F32), jax.ShapeDtypeStruct((t, d), MM_DTYPE),
                   jax.ShapeDtypeStruct((1, d), F32)],
        compiler_params=_cparams(("arbitrary",)),
        name=name,
    )(x, w_row, dy, dskip)


def _silu(z):
    return z / (1.0 + jnp.exp(-z))


FFN_TM, FFN_TN = 512, 1408


def _ffn_in(hn, in_t, *, name):
    t, d = hn.shape
    h = FFN_HIDDEN
    tm, tn = min(t, FFN_TM), FFN_TN
    nj = h // tn
    dn = (((1,), (1,)), ((), ()))

    def body(a_ref, bg_ref, bu_ref, g_ref, u_ref, act_ref):
        a = a_ref[...]
        g = lax.dot_general(a, bg_ref[...], dn, preferred_element_type=F32)
        u = lax.dot_general(a, bu_ref[...], dn, preferred_element_type=F32)
        g_ref[...] = g.astype(g_ref.dtype)
        u_ref[...] = u.astype(u_ref.dtype)
        act_ref[...] = (_silu(g) * u).astype(act_ref.dtype)

    out = pl.BlockSpec((tm, tn), lambda j, i: (i, j))
    return pl.pallas_call(
        body,
        grid=(nj, t // tm),
        in_specs=[pl.BlockSpec((tm, d), lambda j, i: (i, 0)), pl.BlockSpec((tn, d), lambda j, i: (j, 0)),
                  pl.BlockSpec((tn, d), lambda j, i: (j + nj, 0))],
        out_specs=[out, out, out],
        out_shape=[jax.ShapeDtypeStruct((t, h), MM_DTYPE)] * 3,
        compiler_params=_cparams(("parallel", "parallel")),
        name=name,
    )(hn, in_t, in_t)


def _ffn_dact(dy, out_w, g, u, *, name):
    t, d = dy.shape
    h = FFN_HIDDEN
    tm, tn = min(t, FFN_TM), FFN_TN

    def body(a_ref, b_ref, g_ref, u_ref, d_ref):
        da = lax.dot_general(a_ref[...], b_ref[...], (((1,), (1,)), ((), ())), preferred_element_type=F32)
        gate = g_ref[...].astype(F32)
        sig = 1.0 / (1.0 + jnp.exp(-gate))
        sg = gate * sig
        d_ref[0] = (da * u_ref[...].astype(F32) * (sig + sg * (1.0 - sig))).astype(d_ref.dtype)
        d_ref[1] = (da * sg).astype(d_ref.dtype)

    blk = pl.BlockSpec((tm, tn), lambda j, i: (i, j))
    return pl.pallas_call(
        body,
        grid=(h // tn, t // tm),
        in_specs=[pl.BlockSpec((tm, d), lambda j, i: (i, 0)), pl.BlockSpec((tn, d), lambda j, i: (j, 0)), blk, blk],
        out_specs=pl.BlockSpec((2, tm, tn), lambda j, i: (0, i, j)),
        out_shape=jax.ShapeDtypeStruct((2, t, h), MM_DTYPE),
        compiler_params=_cparams(("parallel", "parallel")),
        name=name,
    )(dy, out_w, g, u)


def _loss_head(y, target, *, name):
    t, d = y.shape
    tb = min(t, 1024)

    def body(y_ref, t_ref, dy_ref, dyb_ref, l_ref):
        err = y_ref[...] - t_ref[...]
        dy_ref[...] = err * (1.0 / d)
        dyb_ref[...] = (err * (1.0 / d)).astype(dyb_ref.dtype)
        part = jnp.sum(jnp.sum(err * err, axis=0, keepdims=True), axis=1, keepdims=True) * (0.5 / d)
        part = jnp.broadcast_to(part, l_ref.shape)

        @pl.when(pl.program_id(0) == 0)
        def _():
            l_ref[...] = part

        @pl.when(pl.program_id(0) > 0)
        def _():
            l_ref[...] += part

    row = pl.BlockSpec((tb, d), lambda i: (i, 0))
    return pl.pallas_call(
        body,
        grid=(t // tb,),
        in_specs=[row, row],
        out_specs=[row, row, pl.BlockSpec((8, LANES), lambda i: (0, 0))],
        out_shape=[jax.ShapeDtypeStruct((t, d), F32), jax.ShapeDtypeStruct((t, d), MM_DTYPE),
                   jax.ShapeDtypeStruct((8, LANES), F32)],
        compiler_params=_cparams(("arbitrary",)),
        name=name,
    )(y, target)


CONV_HALO = 8
CONV_TIME_TILE = 2048


def _conv_tile_scale(c):
    is_qk = c < 2 * GDN_HEADS
    scale = jnp.where(c < GDN_HEADS, GDN_DK ** -0.5, 1.0).astype(F32)
    return is_qk, scale


def _gdn_conv_fwd(pm, conv_w, *, name):
    t = pm.shape[0]
    tb = min(t, CONV_TIME_TILE)
    nt = t // tb
    hb = tb // CONV_HALO

    def body(x_ref, xp_ref, w_ref, o_ref, xe_ref):
        c = pl.program_id(0)
        ti = pl.program_id(1)
        xe_ref[0:CONV_HALO, :] = jnp.where(ti > 0, xp_ref[...], 0.0)
        xe_ref[CONV_HALO:CONV_HALO + tb, :] = x_ref[...]
        w = w_ref[...]
        y = jnp.zeros((tb, LANES), F32)
        for j in range(GDN_CONV):
            off = CONV_HALO - (GDN_CONV - 1) + j
            y = y + w[j:j + 1, :] * xe_ref[pl.ds(off, tb), :]
        s = _silu(y)
        is_qk, scale = _conv_tile_scale(c)
        r = lax.rsqrt(jnp.sum(s * s, axis=-1, keepdims=True) + L2_EPS) * scale
        o_ref[...] = s * jnp.where(is_qk, r, 1.0)

    return pl.pallas_call(
        body,
        grid=(GDN_QKV // LANES, nt),
        in_specs=[
            pl.BlockSpec((tb, LANES), lambda c, i: (i, c)),
            pl.BlockSpec((CONV_HALO, LANES), lambda c, i: (jnp.maximum(i * hb - 1, 0), c)),
            pl.BlockSpec((GDN_CONV, LANES), lambda c, i: (0, c)),
        ],
        out_specs=pl.BlockSpec((tb, LANES), lambda c, i: (i, c)),
        out_shape=jax.ShapeDtypeStruct((t, GDN_QKV), F32),
        scratch_shapes=[pltpu.VMEM((tb + CONV_HALO, LANES), F32)],
        compiler_params=_cparams(("parallel", "parallel")),
        name=name,
    )(pm, pm, conv_w)


def _gdn_conv_bwd(pm, conv_w, dout, *, name):
    t = pm.shape[0]
    tb = min(t, CONV_TIME_TILE)
    nt = t // tb
    hb = tb // CONV_HALO
    last_hb = t // CONV_HALO - 1
    ext = tb + CONV_HALO

    def body(x_ref, xp_ref, xn_ref, d_ref, dn_ref, w_ref, dx_ref, dw_ref, xe_ref, dy_ref):
        c = pl.program_id(0)
        ti = pl.program_id(1)
        has_next = ti < nt - 1
        xe_ref[0:CONV_HALO, :] = jnp.where(ti > 0, xp_ref[...], 0.0)
        xe_ref[CONV_HALO:CONV_HALO + tb, :] = x_ref[...]
        xe_ref[CONV_HALO + tb:2 * CONV_HALO + tb, :] = jnp.where(has_next, xn_ref[...], 0.0)
        de = jnp.concatenate([d_ref[...], jnp.where(has_next, dn_ref[...], 0.0)], axis=0)
        w = w_ref[...]
        y = jnp.zeros((ext, LANES), F32)
        for j in range(GDN_CONV):
            off = CONV_HALO - (GDN_CONV - 1) + j
            y = y + w[j:j + 1, :] * xe_ref[pl.ds(off, ext), :]
        sig = 1.0 / (1.0 + jnp.exp(-y))
        s = y * sig
        is_qk, scale = _conv_tile_scale(c)
        r = lax.rsqrt(jnp.sum(s * s, axis=-1, keepdims=True) + L2_EPS)
        n = s * r
        dnrm = de * scale
        ds_qk = r * (dnrm - n * jnp.sum(dnrm * n, axis=-1, keepdims=True))
        ds = jnp.where(is_qk, ds_qk, de)
        dy_ref[...] = ds * (sig + s * (1.0 - sig))
        dy = dy_ref[0:tb, :]
        dx = jnp.zeros((tb, LANES), F32)
        dw_rows = []
        for j in range(GDN_CONV):
            sh = GDN_CONV - 1 - j
            dx = dx + w[j:j + 1, :] * dy_ref[pl.ds(sh, tb), :]
            off = CONV_HALO - (GDN_CONV - 1) + j
            dw_rows.append(jnp.sum(dy * xe_ref[pl.ds(off, tb), :], axis=0, keepdims=True))
        dx_ref[...] = dx.astype(dx_ref.dtype)
        part = jnp.concatenate(dw_rows, axis=0)

        @pl.when(ti == 0)
        def _():
            dw_ref[...] = part

        @pl.when(ti > 0)
        def _():
            dw_ref[...] += part

    main = pl.BlockSpec((tb, LANES), lambda c, i: (i, c))
    prev = pl.BlockSpec((CONV_HALO, LANES), lambda c, i: (jnp.maximum(i * hb - 1, 0), c))
    nxt = pl.BlockSpec((CONV_HALO, LANES), lambda c, i: (jnp.minimum((i + 1) * hb, last_hb), c))
    return pl.pallas_call(
        body,
        grid=(GDN_QKV // LANES, nt),
        in_specs=[main, prev, nxt, main, nxt, pl.BlockSpec((GDN_CONV, LANES), lambda c, i: (0, c))],
        out_specs=[main, pl.BlockSpec((GDN_CONV, LANES), lambda c, i: (0, c))],
        out_shape=[jax.ShapeDtypeStruct((t, GDN_QKV), MM_DTYPE), jax.ShapeDtypeStruct((GDN_CONV, GDN_QKV), F32)],
        scratch_shapes=[pltpu.VMEM((tb + 2 * CONV_HALO, LANES), F32), pltpu.VMEM((ext, LANES), F32)],
        compiler_params=_cparams(("parallel", "arbitrary")),
        name=name,
    )(pm, pm, pm, dout, dout, conv_w)


def _head_selector(first_col):
    row = lax.broadcasted_iota(jnp.int32, (LANES, GDN_HEADS * LANES), 0)
    col = lax.broadcasted_iota(jnp.int32, (LANES, GDN_HEADS * LANES), 1)
    return (col // LANES + first_col == row).astype(BF16)


def _spread_columns(cols, first_col):
    sel = _head_selector(first_col)
    return sum(_bdot(p, sel) for p in _bf16_pieces(cols))


def _gather_columns(wide, first_col):
    sel = _head_selector(first_col)
    return sum(_bdot_nt(p, sel) for p in _bf16_pieces(wide))


def _softplus(x):
    return jnp.maximum(x, 0.0) + jnp.log(1.0 + jnp.exp(-jnp.abs(x)))


def _gdn_gates_fwd(ab, alog_row, dt_row, *, name):
    t = ab.shape[0]
    tb = min(t, 1024)
    wide = GDN_HEADS * LANES

    def body(ab_ref, al_ref, dt_ref, g_ref, b_ref):
        x = ab_ref[...]
        g_cols = -jnp.exp(al_ref[...]) * _softplus(x + dt_ref[...])
        b_cols = 1.0 / (1.0 + jnp.exp(-x))
        g_ref[...] = _spread_columns(g_cols, 0)
        b_ref[...] = _spread_columns(b_cols, GDN_HEADS)

    row = pl.BlockSpec((tb, LANES), lambda i: (i, 0))
    one = pl.BlockSpec((1, LANES), lambda i: (0, 0))
    out = pl.BlockSpec((tb, wide), lambda i: (i, 0))
    return pl.pallas_call(
        body,
        grid=(t // tb,),
        in_specs=[row, one, one],
        out_specs=[out, out],
        out_shape=[jax.ShapeDtypeStruct((t, wide), F32)] * 2,
        compiler_params=_cparams(("parallel",)),
        name=name,
    )(ab, alog_row, dt_row)


def _gdn_gates_bwd(ab, alog_row, dt_row, dgb, dbb, *, name):
    t = ab.shape[0]
    tb = min(t, 1024)
    wide = GDN_HEADS * LANES

    def body(ab_ref, al_ref, dt_ref, dg_ref, db_ref, dab_ref, dal_ref, ddt_ref):
        x = ab_ref[...]
        lane = lax.broadcasted_iota(jnp.int32, (tb, LANES), 1)
        dg_cols = _gather_columns(dg_ref[...], 0)
        db_cols = _gather_columns(db_ref[...], GDN_HEADS)
        ea = jnp.exp(al_ref[...])
        z = x + dt_ref[...]
        sp = _softplus(z)
        sg = 1.0 / (1.0 + jnp.exp(-z))
        beta = 1.0 / (1.0 + jnp.exp(-x))
        da = jnp.where(lane < GDN_HEADS, dg_cols * (-ea) * sg, 0.0)
        db = jnp.where((lane >= GDN_HEADS) & (lane < 2 * GDN_HEADS), db_cols * beta * (1.0 - beta), 0.0)
        dab_ref[...] = (da + db).astype(dab_ref.dtype)
        p_al = jnp.sum(jnp.where(lane < GDN_HEADS, dg_cols * (-ea) * sp, 0.0), axis=0, keepdims=True)
        p_dt = jnp.sum(da, axis=0, keepdims=True)

        @pl.when(pl.program_id(0) == 0)
        def _():
            dal_ref[...] = p_al
            ddt_ref[...] = p_dt

        @pl.when(pl.program_id(0) > 0)
        def _():
            dal_ref[...] += p_al
            ddt_ref[...] += p_dt

    row = pl.BlockSpec((tb, LANES), lambda i: (i, 0))
    one = pl.BlockSpec((1, LANES), lambda i: (0, 0))
    big = pl.BlockSpec((tb, wide), lambda i: (i, 0))
    return pl.pallas_call(
        body,
        grid=(t // tb,),
        in_specs=[row, one, one, big, big],
        out_specs=[row, one, one],
        out_shape=[jax.ShapeDtypeStruct((t, LANES), MM_DTYPE), jax.ShapeDtypeStruct((1, LANES), F32),
                   jax.ShapeDtypeStruct((1, LANES), F32)],
        compiler_params=_cparams(("arbitrary",)),
        name=name,
    )(ab, alog_row, dt_row, dgb, dbb)


@jax.custom_vjp
def _unit_lower_inverse_rest(n):
    c = n.shape[-1]
    ri = lax.broadcasted_iota(jnp.int32, (c, c), 0)
    ci = lax.broadcasted_iota(jnp.int32, (c, c), 1)
    rest = None
    size = 1
    while size < c:
        joins = ((ri // (2 * size)) == (ci // (2 * size))) & ((ri // size) != (ci // size))
        low = jnp.where(joins, n, 0.0)
        if rest is None:
            rest = -low
        else:
            left = low + _bdot(rest, low)
            rest = rest - (left + _bdot(left, rest))
        size *= 2
    return rest


def _unit_lower_inverse_rest_fwd(n):
    rest = _unit_lower_inverse_rest(n)
    return rest, rest


def _unit_lower_inverse_rest_bwd(rest, ct):
    left = ct + _bdot_tn(rest, ct)
    return (-(left + _bdot_nt(left, rest)),)


_unit_lower_inverse_rest.defvjp(_unit_lower_inverse_rest_fwd, _unit_lower_inverse_rest_bwd)


@jax.custom_vjp
def _known_inverse_rest(n, rest):
    return rest


def _known_inverse_rest_fwd(n, rest):
    return rest, rest


def _known_inverse_rest_bwd(rest, ct):
    return _unit_lower_inverse_rest_bwd(rest, ct) + (jnp.zeros_like(rest),)


_known_inverse_rest.defvjp(_known_inverse_rest_fwd, _known_inverse_rest_bwd)


def _bf16_pieces(x):
    hi = x.astype(BF16)
    r1 = x - hi.astype(F32)
    mid = r1.astype(BF16)
    lo = (r1 - mid.astype(F32)).astype(BF16)
    return hi, mid, lo


def _lower_ones(shape):
    c = shape[-1]
    ri = lax.broadcasted_iota(jnp.int32, (c, c), 0)
    ci = lax.broadcasted_iota(jnp.int32, (c, c), 1)
    return jnp.broadcast_to((ri >= ci).astype(BF16), shape)


@jax.custom_vjp
def _running_sum(x):
    tri = _lower_ones(x.shape)
    return sum(_bdot(tri, p) for p in _bf16_pieces(x))


def _running_sum_fwd(x):
    return _running_sum(x), None


def _running_sum_bwd(_, ct):
    tri = _lower_ones(ct.shape)
    return (sum(_bdot_tn(tri, p) for p in _bf16_pieces(ct)),)


_running_sum.defvjp(_running_sum_fwd, _running_sum_bwd)


def _gdn_prep_math(q, k, v, gb, bb, known_rest=None, with_rest=False):
    c = GDN_CHUNK
    ri = lax.broadcasted_iota(jnp.int32, (c, c), 0)
    ci = lax.broadcasted_iota(jnp.int32, (c, c), 1)
    causal = ri >= ci
    gc = _running_sum(gb)
    decay = jnp.exp(jnp.where(causal, gc - jnp.swapaxes(gc, -1, -2), NEG))
    n = jnp.where(ri > ci, _bdot_nt(k, k) * bb * decay, 0.0)
    rest = _unit_lower_inverse_rest(n) if known_rest is None else _known_inverse_rest(n, known_rest)
    eg = jnp.exp(gc)
    rhs_v = v * bb
    rhs_k = k * bb * eg
    u = rhs_v + _bdot(rest, rhs_v)
    w = rhs_k + _bdot(rest, rhs_k)
    qk = _bdot_nt(q, k) * decay
    qd = q * eg
    last = jnp.sum(jnp.where(ri == c - 1, gc, 0.0), axis=-2, keepdims=True)
    gl = jnp.broadcast_to(last, gc.shape)
    kt = k * jnp.exp(gl - gc)
    cd = jnp.exp(gl)
    return (u, w, qk, qd, kt, cd, rest) if with_rest else (u, w, qk, qd, kt, cd)


def _head_tiles(ref, h):
    return ref[:, h * LANES:(h + 1) * LANES]


def _stack_heads(ref, first=0, heads=GDN_HEADS):
    return jnp.stack([_head_tiles(ref, first + h) for h in range(heads)])


def _store_heads(ref, val, first=0):
    for h in range(val.shape[0]):
        ref[:, (first + h) * LANES:(first + h + 1) * LANES] = val[h].astype(ref.dtype)


def _gdn_prep_fwd(qkv, gb, bb, *, name):
    t = qkv.shape[0]
    c = GDN_CHUNK
    wide = GDN_HEADS * LANES

    def body(q_ref, k_ref, v_ref, g_ref, b_ref, *outs):
        res = _gdn_prep_math(*(_stack_heads(r) for r in (q_ref, k_ref, v_ref, g_ref, b_ref)), with_rest=True)
        for o_ref, val in zip(outs, res):
            _store_heads(o_ref, val)

    blk = lambda off: pl.BlockSpec((c, wide), lambda i: (i, off))
    outs = pl.pallas_call(
        body,
        grid=(t // c,),
        in_specs=[blk(0), blk(1), blk(2), blk(0), blk(0)],
        out_specs=[blk(0)] * 7,
        out_shape=[jax.ShapeDtypeStruct((t, wide), dt) for dt in (F32, MM_DTYPE, MM_DTYPE, MM_DTYPE, MM_DTYPE, F32, F32)],
        compiler_params=_cparams(("parallel",)),
        name=name,
    )(qkv, qkv, qkv, gb, bb)
    return tuple(outs[:6]), outs[6]


def _gdn_prep_bwd(qkv, gb, bb, rest, cts, *, name):
    t = qkv.shape[0]
    c = GDN_CHUNK
    wide = GDN_HEADS * LANES

    def body(q_ref, k_ref, v_ref, g_ref, b_ref, r_ref, c0, c1, c2, c3, c4, c5, dqkv_ref, dg_ref, db_ref):
        prim = tuple(_stack_heads(r) for r in (q_ref, k_ref, v_ref, g_ref, b_ref))
        _, pull = jax.vjp(functools.partial(_gdn_prep_math, known_rest=_stack_heads(r_ref)), *prim)
        dq, dk, dv, dg, db = pull(tuple(_stack_heads(r) for r in (c0, c1, c2, c3, c4, c5)))
        _store_heads(dqkv_ref, dq)
        _store_heads(dqkv_ref, dk, first=GDN_HEADS)
        _store_heads(dqkv_ref, dv, first=2 * GDN_HEADS)
        _store_heads(dg_ref, dg)
        _store_heads(db_ref, db)

    blk = lambda off: pl.BlockSpec((c, wide), lambda i: (i, off))
    return pl.pallas_call(
        body,
        grid=(t // c,),
        in_specs=[blk(0), blk(1), blk(2), blk(0), blk(0)] + [blk(0)] * 7,
        out_specs=[pl.BlockSpec((c, 3 * wide), lambda i: (i, 0)), blk(0), blk(0)],
        out_shape=[jax.ShapeDtypeStruct((t, 3 * wide), F32), jax.ShapeDtypeStruct((t, wide), F32),
                   jax.ShapeDtypeStruct((t, wide), F32)],
        compiler_params=_cparams(("parallel",)),
        name=name,
    )(qkv, qkv, qkv, gb, bb, rest, *cts)


def _gdn_scan_math(s, u, w, qk, qd, kt, cd):
    v_new = u - _bdot(w, s)
    o = _bdot(qd, s) + _bdot(qk, v_new)
    s_new = s * cd + _bdot_tn(kt, v_new)
    return o, s_new


def _gdn_scan_fwd(prep, *, name):
    t = prep[0].shape[0]
    c = GDN_CHUNK
    wide = GDN_HEADS * LANES

    def body(u_ref, w_ref, qk_ref, qd_ref, kt_ref, cd_ref, o_ref, st_ref, s_ref):
        @pl.when(pl.program_id(0) == 0)
        def _():
            s_ref[...] = jnp.zeros_like(s_ref)

        s = _stack_heads(s_ref)
        _store_heads(st_ref, s)
        o, s_new = _gdn_scan_math(s, *(_stack_heads(r).astype(F32) for r in (u_ref, w_ref, qk_ref, qd_ref, kt_ref, cd_ref)))
        _store_heads(o_ref, o)
        _store_heads(s_ref, s_new)

    blk = pl.BlockSpec((c, wide), lambda i: (i, 0))
    return pl.pallas_call(
        body,
        grid=(t // c,),
        in_specs=[blk] * 6,
        out_specs=[blk, blk],
        out_shape=[jax.ShapeDtypeStruct((t, wide), F32)] * 2,
        scratch_shapes=[pltpu.VMEM((GDN_DK, wide), F32)],
        compiler_params=_cparams(("arbitrary",)),
        name=name,
    )(*prep)


def _gdn_scan_bwd(prep, states, do, *, name):
    t = do.shape[0]
    c = GDN_CHUNK
    wide = GDN_HEADS * LANES
    nc = t // c

    def body(u_ref, w_ref, qk_ref, qd_ref, kt_ref, cd_ref, st_ref, do_ref, *rest):
        outs, ds_ref = rest[:6], rest[6]

        @pl.when(pl.program_id(0) == 0)
        def _():
            ds_ref[...] = jnp.zeros_like(ds_ref)

        prim = tuple(_stack_heads(r).astype(F32) for r in (st_ref, u_ref, w_ref, qk_ref, qd_ref, kt_ref, cd_ref))
        _, pull = jax.vjp(_gdn_scan_math, *prim)
        grads = pull((_stack_heads(do_ref), _stack_heads(ds_ref)))
        _store_heads(ds_ref, grads[0])
        for o_ref, val in zip(outs, grads[1:]):
            _store_heads(o_ref, val)

    blk = pl.BlockSpec((c, wide), lambda i: (nc - 1 - i, 0))
    return pl.pallas_call(
        body,
        grid=(nc,),
        in_specs=[blk] * 8,
        out_specs=[blk] * 6,
        out_shape=[jax.ShapeDtypeStruct((t, wide), F32)] * 6,
        scratch_shapes=[pltpu.VMEM((GDN_DK, wide), F32)],
        compiler_params=_cparams(("arbitrary",)),
        name=name,
    )(*prep, states, do)


def _gdn_outgate_math(o, z, nw):
    r = lax.rsqrt(jnp.mean(o * o, axis=-1, keepdims=True) + RMS_EPS)
    return o * r * nw * _silu(z)


def _gdn_outgate_fwd(o, pm, nw_row, *, name):
    t = o.shape[0]
    tb = min(t, ROW_TILE)
    wide = GDN_HEADS * LANES
    z_at = GDN_QKV // wide

    def body(o_ref, z_ref, nw_ref, y_ref):
        for h in range(GDN_HEADS):
            y = _gdn_outgate_math(_head_tiles(o_ref, h), _head_tiles(z_ref, h), nw_ref[...])
            y_ref[:, h * LANES:(h + 1) * LANES] = y.astype(y_ref.dtype)

    return pl.pallas_call(
        body,
        grid=(t // tb,),
        in_specs=[pl.BlockSpec((tb, wide), lambda i: (i, 0)), pl.BlockSpec((tb, wide), lambda i: (i, z_at)),
                  pl.BlockSpec((1, LANES), lambda i: (0, 0))],
        out_specs=pl.BlockSpec((tb, wide), lambda i: (i, 0)),
        out_shape=jax.ShapeDtypeStruct((t, wide), MM_DTYPE),
        compiler_params=_cparams(("parallel",)),
        name=name,
    )(o, pm, nw_row)


def _gdn_outgate_bwd(o, pm, nw_row, dy, *, name):
    t = o.shape[0]
    tb = min(t, ROW_TILE)
    wide = GDN_HEADS * LANES
    z_at = GDN_QKV // wide

    def body(o_ref, z_ref, nw_ref, dy_ref, do_ref, dz_ref, dnw_ref):
        total = jnp.zeros((1, LANES), F32)
        for h in range(GDN_HEADS):
            _, pull = jax.vjp(_gdn_outgate_math, _head_tiles(o_ref, h), _head_tiles(z_ref, h), nw_ref[...])
            d_o, d_z, d_nw = pull(_head_tiles(dy_ref, h))
            do_ref[:, h * LANES:(h + 1) * LANES] = d_o
            dz_ref[:, h * LANES:(h + 1) * LANES] = d_z.astype(dz_ref.dtype)
            total = total + d_nw

        @pl.when(pl.program_id(0) == 0)
        def _():
            dnw_ref[...] = total

        @pl.when(pl.program_id(0) > 0)
        def _():
            dnw_ref[...] += total

    blk = pl.BlockSpec((tb, wide), lambda i: (i, 0))
    one = pl.BlockSpec((1, LANES), lambda i: (0, 0))
    return pl.pallas_call(
        body,
        grid=(t // tb,),
        in_specs=[blk, pl.BlockSpec((tb, wide), lambda i: (i, z_at)), one, blk],
        out_specs=[blk, blk, one],
        out_shape=[jax.ShapeDtypeStruct((t, wide), F32), jax.ShapeDtypeStruct((t, wide), MM_DTYPE),
                   jax.ShapeDtypeStruct((1, LANES), F32)],
        compiler_params=_cparams(("arbitrary",)),
        name=name,
    )(o, pm, nw_row, dy)


def _rms64(x, w_row):
    return x * lax.rsqrt(jnp.sum(x * x, axis=-1, keepdims=True) * (1.0 / DIL_DH) + RMS_EPS) * w_row


def _alibi_slopes(group):
    head = lax.broadcasted_iota(jnp.int32, (DIL_HEADS, 8, LANES), 0).astype(F32)
    rate = -math.log(2.0) * ALIBI_MAX_BIAS / (len(DIL_GROUPS) * DIL_HEADS)
    slope = jnp.exp(rate * (head + float(group * DIL_HEADS + 1)))
    return jnp.broadcast_to(slope[:, 0:1, :], (DIL_HEADS, DIL_SPAN, LANES))


def _band_logits(qn, kp, kc, slope_d, has_prev):
    qi = lax.broadcasted_iota(jnp.int32, (DIL_SPAN, DIL_SPAN), 0)
    kj = lax.broadcasted_iota(jnp.int32, (DIL_SPAN, DIL_SPAN), 1)
    steps_c = (qi - kj).astype(F32)
    scale = DIL_DH ** -0.5
    sp = _bdot_nt(qn, kp) * scale - slope_d * (steps_c + float(DIL_SPAN))
    sc = _bdot_nt(qn, kc) * scale - slope_d * steps_c
    sp = jnp.where((kj >= qi) & has_prev, sp, NEG)
    sc = jnp.where(kj <= qi, sc, NEG)
    return sp, sc


def _dil_attn_fwd(slab, wq_row, wk_row, *, group, name):
    dilation = DIL_GROUPS[group][1]
    t = slab.shape[0]
    rows = t // dilation
    nlb = rows // DIL_SPAN
    wide = DIL_HEADS * LANES
    view = slab.reshape(rows, dilation * DIL_SLAB)

    def body(q_ref, kc_ref, vc_ref, kp_ref, vp_ref, wq_ref, wk_ref, o_ref):
        has_prev = pl.program_id(1) > 0
        lane = lax.broadcasted_iota(jnp.int32, (DIL_SPAN, LANES), 1)
        qn = _rms64(_stack_heads(q_ref), wq_ref[...])
        kc = _rms64(_stack_heads(kc_ref), wk_ref[...])
        kp = _rms64(_stack_heads(kp_ref), wk_ref[...])
        sp, sc = _band_logits(qn, kp, kc, _alibi_slopes(group) * float(dilation), has_prev)
        m = jnp.maximum(jnp.max(sp, axis=-1, keepdims=True), jnp.max(sc, axis=-1, keepdims=True))
        pp = jnp.exp(sp - m)
        pc = jnp.exp(sc - m)
        l = jnp.sum(pp, axis=-1, keepdims=True) + jnp.sum(pc, axis=-1, keepdims=True)
        o = (_bdot(pp, _stack_heads(vp_ref)) + _bdot(pc, _stack_heads(vc_ref))) / l
        _store_heads(o_ref, jnp.where(lane < DIL_DH, o, m + jnp.log(l)))

    cur = lambda part: pl.BlockSpec((DIL_SPAN, wide), lambda r, i: (i, 3 * r + part))
    prv = lambda part: pl.BlockSpec((DIL_SPAN, wide), lambda r, i: (jnp.maximum(i - 1, 0), 3 * r + part))
    one = pl.BlockSpec((1, LANES), lambda r, i: (0, 0))
    out = pl.pallas_call(
        body,
        grid=(dilation, nlb),
        in_specs=[cur(0), cur(1), cur(2), prv(1), prv(2), one, one],
        out_specs=pl.BlockSpec((DIL_SPAN, wide), lambda r, i: (i, r)),
        out_shape=jax.ShapeDtypeStruct((rows, dilation * wide), F32),
        compiler_params=_cparams(("parallel", "parallel")),
        name=name,
    )(view, view, view, view, view, wq_row, wk_row)
    return out.reshape(t, wide)


def _head_slope(group, head):
    idx = jnp.zeros((8, LANES), F32) + head.astype(F32)
    rate = -math.log(2.0) * ALIBI_MAX_BIAS / (len(DIL_GROUPS) * DIL_HEADS)
    slope = jnp.exp(rate * (idx + float(group * DIL_HEADS + 1)))
    return jnp.broadcast_to(slope[0:1, :], (DIL_SPAN, LANES))


def _take_residues(ref, d):
    return jnp.stack([ref[pl.ds(r, DIL_SPAN, stride=d), :] for r in range(d)])


def _put_residues(ref, val, d):
    for r in range(d):
        ref[pl.ds(r, DIL_SPAN, stride=d), :] = val[r]


def _dil_attn_fwd_strided(slab, wq_row, wk_row, *, group, name):
    d = DIL_GROUPS[group][1]
    t = slab.shape[0]
    span = DIL_SPAN * d
    nsb = t // span

    def body(q_ref, kc_ref, vc_ref, kp_ref, vp_ref, wq_ref, wk_ref, o_ref):
        has_prev = pl.program_id(0) > 0
        lane = lax.broadcasted_iota(jnp.int32, (DIL_SPAN, LANES), 1)
        qn = _rms64(_take_residues(q_ref, d), wq_ref[...])
        kc = _rms64(_take_residues(kc_ref, d), wk_ref[...])
        kp = _rms64(_take_residues(kp_ref, d), wk_ref[...])
        sp, sc = _band_logits(qn, kp, kc, _head_slope(group, pl.program_id(1)) * float(d), has_prev)
        m = jnp.maximum(jnp.max(sp, axis=-1, keepdims=True), jnp.max(sc, axis=-1, keepdims=True))
        pp = jnp.exp(sp - m)
        pc = jnp.exp(sc - m)
        l = jnp.sum(pp, axis=-1, keepdims=True) + jnp.sum(pc, axis=-1, keepdims=True)
        o = (_bdot(pp, _take_residues(vp_ref, d)) + _bdot(pc, _take_residues(vc_ref, d))) / l
        _put_residues(o_ref, jnp.where(lane < DIL_DH, o, m + jnp.log(l)), d)

    cur = lambda part: pl.BlockSpec((span, LANES), lambda i, h: (i, part * DIL_HEADS + h))
    prv = lambda part: pl.BlockSpec((span, LANES), lambda i, h: (jnp.maximum(i - 1, 0), part * DIL_HEADS + h))
    one = pl.BlockSpec((1, LANES), lambda i, h: (0, 0))
    return pl.pallas_call(
        body,
        grid=(nsb, DIL_HEADS),
        in_specs=[cur(0), cur(1), cur(2), prv(1), prv(2), one, one],
        out_specs=pl.BlockSpec((span, LANES), lambda i, h: (i, h)),
        out_shape=jax.ShapeDtypeStruct((t, DIL_HEADS * LANES), F32),
        compiler_params=_cparams(("parallel", "parallel")),
        name=name,
    )(slab, slab, slab, slab, slab, wq_row, wk_row)


def _dil_attn_bwd_strided(slab, stat, wq_row, wk_row, dwq_in, dwk_in, *, group, name):
    d = DIL_GROUPS[group][1]
    t = slab.shape[0]
    span = DIL_SPAN * d
    nsb = t // span

    def body(q_ref, kc_ref, vc_ref, kp_ref, vp_ref, st_ref, wq_ref, wk_ref, dwq_in_ref, dwk_in_ref,
             d_ref, dwq_ref, dwk_ref, dk_carry, dv_carry, spread):
        step = pl.program_id(1)
        has_prev = step < nsb - 1
        first = (pl.program_id(0) == 0) & (step == 0)

        @pl.when(step == 0)
        def _():
            dk_carry[...] = jnp.zeros_like(dk_carry)
            dv_carry[...] = jnp.zeros_like(dv_carry)

        @pl.when(first)
        def _():
            dwq_ref[...] = dwq_in_ref[...]
            dwk_ref[...] = dwk_in_ref[...]

        lane = lax.broadcasted_iota(jnp.int32, (DIL_SPAN, LANES), 1)
        scale = DIL_DH ** -0.5
        q_raw = _take_residues(q_ref, d)
        kc_raw = _take_residues(kc_ref, d)
        vc = _take_residues(vc_ref, d)
        kp_raw = _take_residues(kp_ref, d)
        vp = _take_residues(vp_ref, d)
        st = _take_residues(st_ref, d)
        d_o = jnp.where(lane < DIL_DH, st, 0.0)
        lse = jnp.sum(jnp.where(lane == DIL_DH, st, 0.0), axis=-1, keepdims=True)
        delta = jnp.sum(jnp.where(lane == DIL_DH + 1, st, 0.0), axis=-1, keepdims=True)
        qn = _rms64(q_raw, wq_ref[...])
        kc = _rms64(kc_raw, wk_ref[...])
        kp = _rms64(kp_raw, wk_ref[...])
        sp, sc = _band_logits(qn, kp, kc, _head_slope(group, pl.program_id(0)) * float(d), has_prev)
        pp = jnp.exp(sp - lse)
        pc = jnp.exp(sc - lse)
        dsp = pp * (_bdot_nt(d_o, vp) - delta) * scale
        dsc = pc * (_bdot_nt(d_o, vc) - delta) * scale
        dqn = _bdot(dsp, kp) + _bdot(dsc, kc)
        dkc_n = _bdot_tn(dsc, qn) + dk_carry[...]
        dvc = _bdot_tn(pc, d_o) + dv_carry[...]
        dk_carry[...] = _bdot_tn(dsp, qn)
        dv_carry[...] = _bdot_tn(pp, d_o)
        dq_raw, dwq_rows = _rms64_bwd(q_raw, wq_ref[...], dqn)
        dk_raw, dwk_rows = _rms64_bwd(kc_raw, wk_ref[...], dkc_n)
        for part, val in enumerate((dq_raw, dk_raw, dvc)):
            _put_residues(spread, val, d)
            d_ref[part] = spread[...].astype(d_ref.dtype)
        dwq_ref[...] += jnp.sum(jnp.sum(dwq_rows, axis=0), axis=0, keepdims=True)
        dwk_ref[...] += jnp.sum(jnp.sum(dwk_rows, axis=0), axis=0, keepdims=True)

    at = lambda i: nsb - 1 - i
    cur = lambda part: pl.BlockSpec((span, LANES), lambda h, i: (at(i), part * DIL_HEADS + h))
    prv = lambda part: pl.BlockSpec((span, LANES), lambda h, i: (jnp.maximum(at(i) - 1, 0), part * DIL_HEADS + h))
    one = pl.BlockSpec((1, LANES), lambda h, i: (0, 0))
    return pl.pallas_call(
        body,
        grid=(DIL_HEADS, nsb),
        in_specs=[cur(0), cur(1), cur(2), prv(1), prv(2), pl.BlockSpec((span, LANES), lambda h, i: (at(i), h)),
                  one, one, one, one],
        out_specs=[pl.BlockSpec((3, span, LANES), lambda h, i: (0, at(i), h)), one, one],
        out_shape=[jax.ShapeDtypeStruct((3, t, DIL_HEADS * LANES), MM_DTYPE), jax.ShapeDtypeStruct((1, LANES), F32),
                   jax.ShapeDtypeStruct((1, LANES), F32)],
        scratch_shapes=[pltpu.VMEM((d, DIL_SPAN, LANES), F32), pltpu.VMEM((d, DIL_SPAN, LANES), F32),
                        pltpu.VMEM((span, LANES), F32)],
        compiler_params=_cparams(("arbitrary", "arbitrary")),
        name=name,
    )(slab, slab, slab, slab, slab, stat, wq_row, wk_row, dwq_in, dwk_in)


def _dil_merge_fwd(oe, *, name):
    t = oe[0].shape[0]
    tb = min(t, ROW_TILE)
    wide = DIL_HEADS * LANES

    def body(e0, e1, e2, y_ref, om_ref):
        lane = lax.broadcasted_iota(jnp.int32, (tb, LANES), 1)
        for h in range(DIL_HEADS):
            es = [_head_tiles(e, h) for e in (e0, e1, e2)]
            lse = [jnp.sum(jnp.where(lane == DIL_DH, e, 0.0), axis=-1, keepdims=True) for e in es]
            top = jnp.maximum(jnp.maximum(lse[0], lse[1]), lse[2])
            joint = top + jnp.log(jnp.exp(lse[0] - top) + jnp.exp(lse[1] - top) + jnp.exp(lse[2] - top))
            o = sum(jnp.exp(l - joint) * e for l, e in zip(lse, es))
            y_ref[:, h * LANES:(h + 1) * LANES] = jnp.where(lane < DIL_DH, o, 0.0).astype(y_ref.dtype)
            om_ref[:, h * LANES:(h + 1) * LANES] = jnp.where(lane < DIL_DH, o, joint)

    blk = pl.BlockSpec((tb, wide), lambda i: (i, 0))
    return pl.pallas_call(
        body,
        grid=(t // tb,),
        in_specs=[blk] * 3,
        out_specs=[blk, blk],
        out_shape=[jax.ShapeDtypeStruct((t, wide), MM_DTYPE), jax.ShapeDtypeStruct((t, wide), F32)],
        compiler_params=_cparams(("parallel",)),
        name=name,
    )(*oe)


def _dil_merge_bwd(dy, om, *, name):
    t = dy.shape[0]
    tb = min(t, ROW_TILE)
    wide = DIL_HEADS * LANES

    def body(dy_ref, om_ref, st_ref):
        lane = lax.broadcasted_iota(jnp.int32, (tb, LANES), 1)
        for h in range(DIL_HEADS):
            d_o = jnp.where(lane < DIL_DH, _head_tiles(dy_ref, h), 0.0)
            om_t = _head_tiles(om_ref, h)
            delta = jnp.sum(d_o * om_t, axis=-1, keepdims=True)
            st_ref[:, h * LANES:(h + 1) * LANES] = jnp.where(
                lane < DIL_DH, d_o, jnp.where(lane == DIL_DH, om_t, jnp.where(lane == DIL_DH + 1, delta, 0.0)))

    blk = pl.BlockSpec((tb, wide), lambda i: (i, 0))
    return pl.pallas_call(
        body,
        grid=(t // tb,),
        in_specs=[blk, blk],
        out_specs=blk,
        out_shape=jax.ShapeDtypeStruct((t, wide), F32),
        compiler_params=_cparams(("parallel",)),
        name=name,
    )(dy, om)


def _rms64_bwd(x, w_row, dy):
    r = lax.rsqrt(jnp.sum(x * x, axis=-1, keepdims=True) * (1.0 / DIL_DH) + RMS_EPS)
    gw = dy * w_row
    dx = r * gw - x * (r * r * r * jnp.sum(gw * x, axis=-1, keepdims=True) * (1.0 / DIL_DH))
    return dx, dy * x * r


def _dil_attn_bwd(slab, stat, wq_row, wk_row, dwq_in, dwk_in, *, group, name):
    dilation = DIL_GROUPS[group][1]
    t = slab.shape[0]
    rows = t // dilation
    nlb = rows // DIL_SPAN
    wide = DIL_HEADS * LANES
    view = slab.reshape(rows, dilation * DIL_SLAB)
    stat_view = stat.reshape(rows, dilation * wide)

    def body(cur_ref, kp_ref, vp_ref, st_ref, wq_ref, wk_ref, dwq_in_ref, dwk_in_ref, d_ref, dwq_ref, dwk_ref,
             dk_carry, dv_carry):
        step = pl.program_id(1)
        has_prev = step < nlb - 1
        first = (pl.program_id(0) == 0) & (step == 0)

        @pl.when(step == 0)
        def _():
            dk_carry[...] = jnp.zeros_like(dk_carry)
            dv_carry[...] = jnp.zeros_like(dv_carry)

        @pl.when(first)
        def _():
            dwq_ref[...] = dwq_in_ref[...]
            dwk_ref[...] = dwk_in_ref[...]

        lane = lax.broadcasted_iota(jnp.int32, (DIL_SPAN, LANES), 1)
        scale = DIL_DH ** -0.5
        q_raw = _stack_heads(cur_ref)
        kc_raw = _stack_heads(cur_ref, first=DIL_HEADS)
        vc = _stack_heads(cur_ref, first=2 * DIL_HEADS)
        kp_raw = _stack_heads(kp_ref)
        vp = _stack_heads(vp_ref)
        st = _stack_heads(st_ref)
        d_o = jnp.where(lane < DIL_DH, st, 0.0)
        lse = jnp.sum(jnp.where(lane == DIL_DH, st, 0.0), axis=-1, keepdims=True)
        delta = jnp.sum(jnp.where(lane == DIL_DH + 1, st, 0.0), axis=-1, keepdims=True)
        qn = _rms64(q_raw, wq_ref[...])
        kc = _rms64(kc_raw, wk_ref[...])
        kp = _rms64(kp_raw, wk_ref[...])
        sp, sc = _band_logits(qn, kp, kc, _alibi_slopes(group) * float(dilation), has_prev)
        pp = jnp.exp(sp - lse)
        pc = jnp.exp(sc - lse)
        dsp = pp * (_bdot_nt(d_o, vp) - delta) * scale
        dsc = pc * (_bdot_nt(d_o, vc) - delta) * scale
        dqn = _bdot(dsp, kp) + _bdot(dsc, kc)
        dkc_n = _bdot_tn(dsc, qn) + _stack_heads(dk_carry)
        dvc = _bdot_tn(pc, d_o) + _stack_heads(dv_carry)
        _store_heads(dk_carry, _bdot_tn(dsp, qn))
        _store_heads(dv_carry, _bdot_tn(pp, d_o))
        dq_raw, dwq_rows = _rms64_bwd(q_raw, wq_ref[...], dqn)
        dk_raw, dwk_rows = _rms64_bwd(kc_raw, wk_ref[...], dkc_n)
        _store_heads(d_ref, dq_raw)
        _store_heads(d_ref, dk_raw, first=DIL_HEADS)
        _store_heads(d_ref, dvc, first=2 * DIL_HEADS)
        dwq_ref[...] += jnp.sum(jnp.sum(dwq_rows, axis=0), axis=0, keepdims=True)
        dwk_ref[...] += jnp.sum(jnp.sum(dwk_rows, axis=0), axis=0, keepdims=True)

    blk_i = lambda i: nlb - 1 - i
    cur = pl.BlockSpec((DIL_SPAN, DIL_SLAB), lambda r, i: (blk_i(i), r))
    prv = lambda part: pl.BlockSpec((DIL_SPAN, wide), lambda r, i: (jnp.maximum(blk_i(i) - 1, 0), 3 * r + part))
    one = pl.BlockSpec((1, LANES), lambda r, i: (0, 0))
    dslab, dwq, dwk = pl.pallas_call(
        body,
        grid=(dilation, nlb),
        in_specs=[cur, prv(1), prv(2), pl.BlockSpec((DIL_SPAN, wide), lambda r, i: (blk_i(i), r)), one, one, one, one],
        out_specs=[cur, one, one],
        out_shape=[jax.ShapeDtypeStruct((rows, dilation * DIL_SLAB), MM_DTYPE), jax.ShapeDtypeStruct((1, LANES), F32),
                   jax.ShapeDtypeStruct((1, LANES), F32)],
        scratch_shapes=[pltpu.VMEM((DIL_SPAN, wide), F32), pltpu.VMEM((DIL_SPAN, wide), F32)],
        compiler_params=_cparams(("arbitrary", "arbitrary")),
        name=name,
    )(view, view, view, stat_view, wq_row, wk_row, dwq_in, dwk_in)
    return dslab.reshape(t, DIL_SLAB), dwq, dwk


def _row(v, width=LANES):
    v = v.astype(F32).reshape(-1)
    return jnp.pad(v, (0, width - v.shape[0])).reshape(1, width)


def _prepare_weights(w):
    return dict(gdn=_prepare_gdn(w), dil=_prepare_dil(w), ffn=_prepare_ffn(w))


def _prepare_gdn(w, layers=range(DEPTH // 2)):
    gdn = {}
    for j in layers:
        wt = w["gdn_w_in"][j]
        gates_t = jnp.pad(wt[GDN_MAIN:], ((0, LANES - 2 * GDN_HEADS), (0, 0)))
        gdn[j] = dict(in_t=wt, gates_t=gates_t, out=w["gdn_w_out"][j], conv=w["gdn_conv_w"][j].astype(F32),
                      alog=_row(w["gdn_a_log"][j]), dt=_row(w["gdn_dt_bias"][j]), nw=_row(w["gdn_norm_w"][j]))
    return gdn


def _prepare_dil(w, layers=range(DEPTH // 2)):
    d = D_MODEL
    dil = {}
    for j in layers:
        wt = w["dil_w_in"][j].reshape(3, len(DIL_GROUPS), DIL_HEADS, DIL_DH, d)
        wg_t = [wt[:, g].reshape(DIL_SLAB // 2, d) for g in range(len(DIL_GROUPS))]
        out_t = jnp.pad(w["dil_w_out"][j].reshape(d, DIL_HEADS, DIL_DH), ((0, 0), (0, 0), (0, LANES - DIL_DH)))
        dil[j] = dict(wg_t=wg_t, out_t=out_t.reshape(d, DIL_HEADS * LANES), wq=_row(w["dil_q_norm"][j]),
                      wk=_row(w["dil_k_norm"][j]))
    return dil


def _prepare_ffn(w, layers=range(DEPTH)):
    return {i: dict(in_t=w["ffn_w_in"][i], out=w["ffn_w_out"][i]) for i in layers}


def _gdn_layer_fwd(x, nrow, p):
    hn = _rmsnorm_fwd(x, nrow, name="rmsnorm_fwd")
    pm = _matmul(hn, p["in_t"], trans_b=True, b_rows=(0, GDN_MAIN), name="gdn_proj_main")
    ab = _matmul(hn, p["gates_t"], trans_b=True, name="gdn_proj_gates")
    qkv = _gdn_conv_fwd(pm, p["conv"], name="gdn_conv_fwd")
    gb, bb = _gdn_gates_fwd(ab, p["alog"], p["dt"], name="gdn_gates_fwd")
    prep, rest = _gdn_prep_fwd(qkv, gb, bb, name="gdn_prep_fwd")
    o, states = _gdn_scan_fwd(prep, name="gdn_scan_fwd")
    og = _gdn_outgate_fwd(o, pm, p["nw"], name="gdn_outgate_fwd")
    y = _matmul(og, p["out"], add=x, name="gdn_proj_out")
    return y, (x, hn, pm, ab, qkv, gb, bb, prep, rest, states, o, og)


def _gdn_layer_bwd(dx, dxb, nrow, p, saved):
    x, hn, pm, ab, qkv, gb, bb, prep, rest, states, o, og = saved
    d_og = _matmul(dxb, p["out"], trans_b=True, name="gdn_dgate")
    g_out = _matmul(og, dxb, trans_a=True, out_dtype=MM_DTYPE, name="gdn_gw_out")
    d_o, d_z, d_nw = _gdn_outgate_bwd(o, pm, p["nw"], d_og, name="gdn_outgate_bwd")
    cts = _gdn_scan_bwd(prep, states, d_o, name="gdn_scan_bwd")
    dqkv, dgb, dbb = _gdn_prep_bwd(qkv, gb, bb, rest, cts, name="gdn_prep_bwd")
    d_ab, d_alog, d_dt = _gdn_gates_bwd(ab, p["alog"], p["dt"], dgb, dbb, name="gdn_gates_bwd")
    d_conv, g_conv = _gdn_conv_bwd(pm, p["conv"], dqkv, name="gdn_conv_bwd")
    d_hn = _matmul(d_conv, p["in_t"], b_rows=(0, GDN_QKV), name="gdn_dhn_qkv")
    d_hn = _matmul(d_z, p["in_t"], b_rows=(GDN_QKV, GDN_MAIN - GDN_QKV), add=d_hn, name="gdn_dhn_z")
    d_hn = _matmul(d_ab, p["gates_t"], add=d_hn, name="gdn_dhn_gates")
    g_in_t = jnp.concatenate([
        _matmul(d_conv, hn, trans_a=True, out_dtype=MM_DTYPE, name="gdn_gw_qkv"),
        _matmul(d_z, hn, trans_a=True, out_dtype=MM_DTYPE, name="gdn_gw_z"),
        _matmul(d_ab, hn, trans_a=True, out_dtype=MM_DTYPE, name="gdn_gw_gates")[:2 * GDN_HEADS],
    ], axis=0)
    dx_new, dxb_new, g_norm = _rmsnorm_bwd(x, nrow, d_hn, dx, name="rmsnorm_bwd")
    grads = dict(w_in=g_in_t, conv=g_conv, a_log=d_alog[0, :GDN_HEADS], dt_bias=d_dt[0, :GDN_HEADS], norm_w=d_nw[0],
                 w_out=g_out, norm=g_norm[0])
    return dx_new, dxb_new, grads


def _dil_layer_fwd(x, nrow, p):
    hn = _rmsnorm_fwd(x, nrow, name="rmsnorm_fwd")
    slabs = [_matmul(hn, p["wg_t"][g], trans_b=True, spread_out=True, name="dil_proj_in") for g in range(len(DIL_GROUPS))]
    oe = [(_dil_attn_fwd if DIL_GROUPS[g][1] == 1 else _dil_attn_fwd_strided)(
        slabs[g], p["wq"], p["wk"], group=g, name=f"dil_attn_fwd_g{g}") for g in range(len(DIL_GROUPS))]
    y, om = _dil_merge_fwd(oe, name="dil_merge_fwd")
    out = _matmul(y, p["out_t"], trans_b=True, add=x, name="dil_proj_out")
    return out, (x, hn, slabs, y, om)


def _dil_layer_bwd(dx, dxb, nrow, p, saved):
    x, hn, slabs, y, om = saved
    d_y = _matmul(dxb, p["out_t"], name="dil_dmerged")
    g_out_t = _matmul(dxb, y, trans_a=True, out_dtype=MM_DTYPE, name="dil_gw_out")
    g_out_t = g_out_t.reshape(D_MODEL, DIL_HEADS, LANES)[..., :DIL_DH].reshape(D_MODEL, DIL_HEADS * DIL_DH)
    stat = _dil_merge_bwd(d_y, om, name="dil_merge_bwd")
    d_hn = None
    dwq = jnp.zeros((1, LANES), F32)
    dwk = jnp.zeros((1, LANES), F32)
    g_groups = []
    wide = DIL_HEADS * LANES
    for g in range(len(DIL_GROUPS)):
        if DIL_GROUPS[g][1] == 1:
            dslab, dwq, dwk = _dil_attn_bwd(slabs[g], stat, p["wq"], p["wk"], dwq, dwk, group=g, name=f"dil_attn_bwd_g{g}")
            d_hn = _matmul(dslab, p["wg_t"][g], packed_a=True, add=d_hn, name="dil_dhn")
            g_w = _matmul(dslab, hn, trans_a=True, packed_a=True, out_dtype=MM_DTYPE, name="dil_gw_in")
        else:
            dparts, dwq, dwk = _dil_attn_bwd_strided(slabs[g], stat, p["wq"], p["wk"], dwq, dwk, group=g,
                                                     name=f"dil_attn_bwd_g{g}")
            d_hn = _matmul(dparts, p["wg_t"][g], a_lead="k", packed_a=True, add=d_hn, name="dil_dhn_parts")
            g_w = _matmul(dparts, hn, trans_a=True, a_lead="i", packed_a=True, out_dtype=MM_DTYPE, name="dil_gw_in_parts")
        g_groups.append(g_w.reshape(3, DIL_HEADS, DIL_DH, D_MODEL))
    g_in_t = jnp.stack(g_groups, axis=1).reshape(3 * len(DIL_GROUPS) * DIL_HEADS * DIL_DH, D_MODEL)
    dx_new, dxb_new, g_norm = _rmsnorm_bwd(x, nrow, d_hn, dx, name="rmsnorm_bwd")
    grads = dict(w_in=g_in_t, q_norm=dwq[0, :DIL_DH], k_norm=dwk[0, :DIL_DH], w_out=g_out_t, norm=g_norm[0])
    return dx_new, dxb_new, grads


def _ffn_layer_fwd(x, nrow, p):
    hn = _rmsnorm_fwd(x, nrow, name="rmsnorm_fwd")
    gate, up, act = _ffn_in(hn, p["in_t"], name="ffn_proj_in")
    y = _matmul(act, p["out"], add=x, name="ffn_proj_out")
    return y, (x, hn, gate, up, act)


def _ffn_layer_bwd(dx, dxb, nrow, p, saved):
    x, hn, gate, up, act = saved
    g_out = _matmul(act, dxb, trans_a=True, out_dtype=MM_DTYPE, name="ffn_gw_out")
    d_gu = _ffn_dact(dxb, p["out"], gate, up, name="ffn_dact")
    d_hn = _matmul(d_gu, p["in_t"], a_lead="k", name="ffn_dhn")
    g_in_t = _matmul(d_gu, hn, trans_a=True, a_lead="i", out_dtype=MM_DTYPE, name="ffn_gw_in")
    dx_new, dxb_new, g_norm = _rmsnorm_bwd(x, nrow, d_hn, dx, name="rmsnorm_bwd")
    return dx_new, dxb_new, dict(w_in=g_in_t, w_out=g_out, norm=g_norm[0])


def _mixer_fwd(i, x, mix_row, prepared):
    if i % 2 == 0:
        return _gdn_layer_fwd(x, mix_row, prepared["gdn"][i // 2])
    return _dil_layer_fwd(x, mix_row, prepared["dil"][i // 2])


def _mixer_bwd(i, dx, dxb, mix_row, prepared, saved, zero=0.0):
    if i % 2 == 0:
        p = prepared["gdn"][i // 2]
        return _gdn_layer_bwd(dx, dxb, mix_row, dict(p, nw=p["nw"] + zero), saved)
    p = prepared["dil"][i // 2]
    return _dil_layer_bwd(dx, dxb, mix_row, dict(p, wq=p["wq"] + zero), saved)


def _local_step(x, target, prepared, norm_mix, norm_ffn):
    saved = []
    for i in range(DEPTH):
        x, s_mix = _mixer_fwd(i, x, norm_mix[i].reshape(1, D_MODEL), prepared)
        x, s_ffn = _ffn_layer_fwd(x, norm_ffn[i].reshape(1, D_MODEL), prepared["ffn"][i])
        saved.append((s_mix, s_ffn))
    dx, dxb, loss = _loss_head(x, target, name="loss_head")
    g_mix, g_ffn = [None] * DEPTH, [None] * DEPTH
    for i in reversed(range(DEPTH)):
        s_mix, s_ffn = saved[i]
        dx, dxb, g_ffn[i] = _ffn_layer_bwd(dx, dxb, norm_ffn[i].reshape(1, D_MODEL), prepared["ffn"][i], s_ffn)
        dx, dxb, g_mix[i] = _mixer_bwd(i, dx, dxb, norm_mix[i].reshape(1, D_MODEL), prepared, s_mix)
    return loss[0, 0], dx, _collect_grads(g_mix, g_ffn)


def _collect_grads(g_mix, g_ffn):
    gdn = [g_mix[i] for i in range(0, DEPTH, 2)]
    dil = [g_mix[i] for i in range(1, DEPTH, 2)]
    if any(g is None for g in g_mix + g_ffn):
        pick = lambda gs, key: [None if g is None else g[key] for g in gs]
        return dict(gdn_w_in=pick(gdn, "w_in"), gdn_w_out=pick(gdn, "w_out"), dil_w_in=pick(dil, "w_in"),
                    dil_w_out=pick(dil, "w_out"), ffn_w_in=pick(g_ffn, "w_in"), ffn_w_out=pick(g_ffn, "w_out"))
    grads = dict(
        norm_mix=jnp.stack([g["norm"] for g in g_mix]),
        norm_ffn=jnp.stack([g["norm"] for g in g_ffn]),
        gdn_w_in=[g["w_in"] for g in gdn],
        gdn_conv_w=jnp.stack([g["conv"] for g in gdn]),
        gdn_a_log=jnp.stack([g["a_log"] for g in gdn]),
        gdn_dt_bias=jnp.stack([g["dt_bias"] for g in gdn]),
        gdn_norm_w=jnp.stack([g["norm_w"] for g in gdn]),
        gdn_w_out=[g["w_out"] for g in gdn],
        dil_w_in=[g["w_in"] for g in dil],
        dil_q_norm=jnp.stack([g["q_norm"] for g in dil]),
        dil_k_norm=jnp.stack([g["k_norm"] for g in dil]),
        dil_w_out=[g["w_out"] for g in dil],
        ffn_w_in=[g["w_in"] for g in g_ffn],
        ffn_w_out=[g["w_out"] for g in g_ffn],
    )
    return grads


MESH_ID = pl.DeviceIdType.MESH
ANY_SPACE = pl.BlockSpec(memory_space=pl.ANY)


def _mesh_position():
    return lax.axis_index("x"), lax.axis_index("y"), lax.axis_index("c")


def _flip(pos, k):
    x, y, c = pos
    return (1 - x if k & 4 else x, 1 - y if k & 2 else y, 1 - c if k & 1 else c)


def _linear(pos):
    return 4 * pos[0] + 2 * pos[1] + pos[2]


def _comm_scratch():
    return [pltpu.SemaphoreType.DMA((N_DEV - 1,)), pltpu.SemaphoreType.DMA((N_DEV - 1,)), pltpu.SemaphoreType.DMA(())]


def _all_gather(shard, *, name):
    def body(x_ref, out_ref, send_sems, recv_sems, local_sem):
        me = _mesh_position()
        mine = out_ref.at[_linear(me)]
        local = pltpu.make_async_copy(x_ref, mine, local_sem)
        local.start()
        copies = []
        for k in range(1, N_DEV):
            cp = pltpu.make_async_remote_copy(src_ref=x_ref, dst_ref=mine, send_sem=send_sems.at[k - 1],
                                              recv_sem=recv_sems.at[k - 1], device_id=_flip(me, k), device_id_type=MESH_ID)
            cp.start()
            copies.append(cp)
        for cp in copies:
            cp.wait()
        local.wait()

    return pl.pallas_call(
        body,
        out_shape=jax.ShapeDtypeStruct((N_DEV,) + shard.shape, shard.dtype),
        in_specs=[ANY_SPACE],
        out_specs=ANY_SPACE,
        scratch_shapes=_comm_scratch(),
        name=name,
    )(shard)


def _exchange(parts, *, name):
    def body(p_ref, out_ref, send_sems, recv_sems, local_sem):
        me = _mesh_position()
        mine = out_ref.at[_linear(me)]
        local = pltpu.make_async_copy(p_ref.at[_linear(me)], mine, local_sem)
        local.start()
        copies = []
        for k in range(1, N_DEV):
            peer = _flip(me, k)
            cp = pltpu.make_async_remote_copy(src_ref=p_ref.at[_linear(peer)], dst_ref=mine, send_sem=send_sems.at[k - 1],
                                              recv_sem=recv_sems.at[k - 1], device_id=peer, device_id_type=MESH_ID)
            cp.start()
            copies.append(cp)
        for cp in copies:
            cp.wait()
        local.wait()

    return pl.pallas_call(
        body,
        out_shape=jax.ShapeDtypeStruct(parts.shape, parts.dtype),
        in_specs=[ANY_SPACE],
        out_specs=ANY_SPACE,
        scratch_shapes=_comm_scratch(),
        name=name,
    )(parts)


HBM_SPACE = pl.BlockSpec(memory_space=pltpu.HBM)
SEM_SPACE = pl.BlockSpec(memory_space=pltpu.SEMAPHORE)
DATAFLOW = pltpu.SideEffectType.DATAFLOW_SIDE_EFFECTING


def _split_copies(src_ref, land_ref, send_sems, recv_sems, per_peer):
    me = _mesh_position()
    mine = land_ref.at[_linear(me)]
    copies = []
    for k in range(1, N_DEV):
        peer = _flip(me, k)
        src = src_ref.at[_linear(peer)] if per_peer else src_ref
        copies.append(pltpu.make_async_remote_copy(src_ref=src, dst_ref=mine, send_sem=send_sems.at[k - 1],
                                                   recv_sem=recv_sems.at[k - 1], device_id=peer, device_id_type=MESH_ID))
    return copies


def _travel_start(src, after, *, per_peer, name):
    me = _linear(_mesh_position())
    own = src[me] if per_peer else src
    shape = own.shape
    landing = lax.dynamic_update_slice(lax.empty((N_DEV,) + shape, src.dtype), own[None], (me, 0, 0))

    def body(src_ref, land_ref, after_ref, send_sems, recv_sems, src_thru, land_thru, token):
        for cp in _split_copies(src_ref, land_ref, send_sems, recv_sems, per_peer):
            cp.start()
        token[...] = jnp.zeros_like(token)

    return pl.pallas_call(
        body,
        name=name,
        out_shape=(pltpu.SemaphoreType.DMA((N_DEV - 1,)), pltpu.SemaphoreType.DMA((N_DEV - 1,)),
                   pltpu.HBM(src.shape, src.dtype), pltpu.HBM(landing.shape, landing.dtype),
                   jax.ShapeDtypeStruct((8, LANES), F32)),
        in_specs=(HBM_SPACE, HBM_SPACE, ANY_SPACE),
        out_specs=(SEM_SPACE, SEM_SPACE, HBM_SPACE, HBM_SPACE, pl.BlockSpec(memory_space=pltpu.VMEM)),
        input_output_aliases={0: 2, 1: 3},
        compiler_params=pltpu.CompilerParams(has_side_effects=DATAFLOW),
    )(pltpu.with_memory_space_constraint(src, pltpu.HBM), pltpu.with_memory_space_constraint(landing, pltpu.HBM), after)


def _travel_wait(started, after, *, per_peer, name):
    send_sems, recv_sems, src_thru, land_thru, _ = started

    def body(src_ref, land_ref, send_sems, recv_sems, after_ref, src_dead, got_ref):
        for cp in _split_copies(src_ref, land_ref, send_sems, recv_sems, per_peer):
            cp.wait_send()
            cp.wait_recv()

    return pl.pallas_call(
        body,
        name=name,
        out_shape=(pltpu.HBM(src_thru.shape, src_thru.dtype), pltpu.HBM(land_thru.shape, land_thru.dtype)),
        in_specs=(HBM_SPACE, HBM_SPACE, SEM_SPACE, SEM_SPACE, ANY_SPACE),
        out_specs=(HBM_SPACE, HBM_SPACE),
        input_output_aliases={0: 0, 1: 1},
        compiler_params=pltpu.CompilerParams(has_side_effects=DATAFLOW),
    )(src_thru, land_thru, send_sems, recv_sems, after)[1]


def _adamw(parts, w, m, v, *, name):
    rows, n = w.shape
    tb = _pick(rows, (PACK_ROW_ALIGN, 16))
    c1 = 1.0 - ADAM_B1 ** ADAM_STEP
    c2 = 1.0 - ADAM_B2 ** ADAM_STEP

    def body(p_ref, w_ref, m_ref, v_ref, g_ref, d_ref, nm_ref, nv_ref):
        g = p_ref[0].astype(F32)
        for s in range(1, N_DEV):
            g = g + p_ref[s].astype(F32)
        m_new = ADAM_B1 * m_ref[...] + (1.0 - ADAM_B1) * g
        v_new = ADAM_B2 * v_ref[...] + (1.0 - ADAM_B2) * (g * g)
        m_hat = m_new / c1
        v_hat = v_new / c2
        g_ref[...] = g
        nm_ref[...] = m_new
        nv_ref[...] = v_new
        d_ref[...] = -ADAM_LR * (m_hat / (jnp.sqrt(v_hat) + ADAM_EPS) + ADAM_WD * w_ref[...])

    blk = pl.BlockSpec((tb, n), lambda i: (i, 0))
    return pl.pallas_call(
        body,
        grid=(rows // tb,),
        in_specs=[pl.BlockSpec((N_DEV, tb, n), lambda i: (0, i, 0)), blk, blk, blk],
        out_specs=[blk] * 4,
        out_shape=[jax.ShapeDtypeStruct((rows, n), F32)] * 4,
        compiler_params=_cparams(("parallel",)),
        name=name,
    )(parts, w, m, v)


PACK_WIDTH = 1024
SHARDED = {
    "gdn_w_in": ((2, D_MODEL, GDN_IN_WIDTH), 2),
    "gdn_conv_w": ((2, GDN_CONV, GDN_QKV), 2),
    "gdn_w_out": ((2, GDN_HEADS * GDN_DV, D_MODEL), 1),
    "dil_w_in": ((2, D_MODEL, 3 * len(DIL_GROUPS) * DIL_HEADS * DIL_DH), 2),
    "dil_w_out": ((2, DIL_HEADS * DIL_DH, D_MODEL), 2),
    "ffn_w_in": ((DEPTH, D_MODEL, 2 * FFN_HIDDEN), 2),
    "ffn_w_out": ((DEPTH, FFN_HIDDEN, D_MODEL), 1),
}
REPLICATED = {"norm_mix": (DEPTH, D_MODEL), "norm_ffn": (DEPTH, D_MODEL), "gdn_a_log": (2, GDN_HEADS),
              "gdn_dt_bias": (2, GDN_HEADS), "gdn_norm_w": (2, GDN_DV), "dil_q_norm": (2, DIL_DH), "dil_k_norm": (2, DIL_DH)}
WEIGHT_ORDER = ("norm_mix", "norm_ffn", "gdn_w_in", "gdn_conv_w", "gdn_a_log", "gdn_dt_bias", "gdn_norm_w", "gdn_w_out",
                "dil_w_in", "dil_q_norm", "dil_k_norm", "dil_w_out", "ffn_w_in", "ffn_w_out")
PACK_ROW_ALIGN = 128
PIECE_ALIGN = 16
SMALL_ROWS = 16


def _shard_shape(name):
    shape, axis = SHARDED[name]
    return tuple(s // N_DEV if i == axis else s for i, s in enumerate(shape))


def _shard_rows(name):
    return math.prod(_shard_shape(name)) // PACK_WIDTH


def _split_shards(full, name):
    shape, axis = SHARDED[name]
    split = full.reshape(shape[:axis] + (N_DEV, shape[axis] // N_DEV) + shape[axis + 1:])
    return jnp.moveaxis(split, axis, 0)


def _join_shards(stacked, name):
    shape, axis = SHARDED[name]
    return jnp.moveaxis(stacked, 0, axis).reshape(shape)


COLUMN_SHARDED = ("gdn_w_in", "dil_w_in", "dil_w_out", "ffn_w_in")


def _to_rows(shard, name):
    if name in COLUMN_SHARDED:
        shard = jnp.swapaxes(shard, 1, 2)
    return shard.reshape(-1, PACK_WIDTH)


def _layer_columns(name):
    _, r, c = _shard_shape(name)
    return r if name in COLUMN_SHARDED else c


def _piece_rows(piece, halves=1):
    name, layer = piece
    rows = _shard_rows(name) * halves
    return rows if layer is None else rows // SHARDED[name][0][0]


def _aligned(rows, to=PIECE_ALIGN):
    return -(-rows // to) * to


def _pack_pieces(arrays, total_align=PIECE_ALIGN):
    padded, total = [], 0
    for a in arrays:
        rows = a.shape[-2]
        extra = _aligned(rows) - rows
        if extra:
            a = jnp.pad(a, [(0, 0)] * (a.ndim - 2) + [(0, extra), (0, 0)])
        padded.append(a)
        total += rows + extra
    tail = _aligned(total, total_align) - total
    if tail:
        padded.append(jnp.zeros(padded[0].shape[:-2] + (tail, PACK_WIDTH), padded[0].dtype))
    return jnp.concatenate(padded, axis=-2)


def _piece_offsets(pieces, halves=None):
    out, at = [], 0
    for p in pieces:
        rows = _piece_rows(p, (halves or {}).get(p[0], 1))
        out.append((p, at, rows))
        at += _aligned(rows)
    return out


def _shard_piece_rows(src, piece):
    name, layer = piece
    part = src[name] if layer is None else src[name][layer:layer + 1]
    return _to_rows(part.astype(F32), name)


def _piece_from_rows(rows, piece):
    name, layer = piece
    layers, r, c = _shard_shape(name)
    n_l = layers if layer is None else 1
    if name in COLUMN_SHARDED:
        return jnp.swapaxes(rows.reshape(n_l, c, r), 1, 2)
    return rows.reshape(n_l, r, c)


SMALL_TAIL = tuple(n for n in REPLICATED if n not in ("norm_mix", "norm_ffn"))


def _pack_small(vals):
    tail, at = jnp.zeros((PACK_WIDTH,), F32), 0
    for n in SMALL_TAIL:
        vec = vals[n].astype(F32).reshape(-1)
        tail = tail + jnp.pad(vec, (at, PACK_WIDTH - at - vec.shape[0]))
        at += vec.shape[0]
    buf = jnp.pad(vals["norm_mix"].astype(F32), ((0, SMALL_ROWS - DEPTH), (0, 0)))
    buf = buf + jnp.pad(vals["norm_ffn"].astype(F32), ((8, SMALL_ROWS - 8 - DEPTH), (0, 0)))
    return buf + jnp.pad(tail.reshape(1, PACK_WIDTH), ((SMALL_ROWS - 1, 0), (0, 0)))


def _unpack_small(buf):
    out = {"norm_mix": buf[0:DEPTH], "norm_ffn": buf[8:8 + DEPTH]}
    at = 0
    for n in SMALL_TAIL:
        size = math.prod(REPLICATED[n])
        out[n] = buf[SMALL_ROWS - 1, at:at + size].reshape(REPLICATED[n])
        at += size
    return out


GATHER_FIRST = (("gdn_w_in", 0), ("gdn_conv_w", None), ("gdn_w_out", 0))
GATHER_NEXT = (("ffn_w_in", 0), ("ffn_w_out", 0), ("dil_w_in", 0), ("dil_w_out", 0))
GATHER_LAST = (("ffn_w_in", 1), ("ffn_w_out", 1), ("gdn_w_in", 1), ("gdn_w_out", 1), ("ffn_w_in", 2), ("ffn_w_out", 2),
               ("dil_w_in", 1), ("dil_w_out", 1), ("ffn_w_in", 3), ("ffn_w_out", 3))
EXCHANGE_GROUPS = (
    (("ffn_w_in", 3), ("ffn_w_out", 3), ("dil_w_in", 1), ("dil_w_out", 1),
     ("ffn_w_in", 2), ("ffn_w_out", 2), ("gdn_w_in", 1), ("gdn_w_out", 1)),
    (("ffn_w_in", 1), ("ffn_w_out", 1), ("dil_w_in", 0), ("dil_w_out", 0)),
    (("ffn_w_in", 0), ("ffn_w_out", 0)),
    (("gdn_w_in", 0), ("gdn_w_out", 0), ("gdn_conv_w", None)),
)
EXCHANGE_AFTER = {("mix", 2): 0, ("mix", 1): 1, ("ffn", 0): 2}


def _gather_operand(w, pieces):
    arrays = []
    for n, layer in pieces:
        if layer is None:
            arrays.append(lax.bitcast_convert_type(w[n], BF16).reshape(-1, PACK_WIDTH))
        else:
            arrays.append(_to_rows(w[n][layer:layer + 1].astype(BF16), n))
    return _pack_pieces(arrays)


def _gathered_weights(gathered, pieces, full):
    for (n, layer), at, rows in _piece_offsets(pieces, halves={"gdn_conv_w": 2}):
        block = gathered[:, at:at + rows]
        if layer is None:
            block = lax.bitcast_convert_type(block.reshape((N_DEV,) + _shard_shape(n) + (2,)), F32)
            full[n] = _join_shards(block, n)
        else:
            full.setdefault(n, {})[layer] = block.reshape(-1, _layer_columns(n))
    return full


def _exchange_operand(grads, pieces):
    arrays = []
    for n, layer in pieces:
        if layer is None:
            arrays.append(_split_shards(grads[n], n).astype(BF16).reshape(N_DEV, -1, PACK_WIDTH))
        else:
            arrays.append(grads[n][layer].astype(BF16).reshape(N_DEV, -1, PACK_WIDTH))
    return _pack_pieces(arrays, total_align=PACK_ROW_ALIGN)


def _update_group(received, pieces, w, m, v, *, name):
    packed = [_pack_pieces([_shard_piece_rows(src, p) for p in pieces], total_align=PACK_ROW_ALIGN) for src in (w, m, v)]
    outs = _adamw(received, *packed, name=name)
    return {p: tuple(_piece_from_rows(o[at:at + rows], p) for o in outs) for p, at, rows in _piece_offsets(pieces)}


def kernel(x, norm_mix, norm_ffn, gdn_w_in, gdn_conv_w, gdn_a_log, gdn_dt_bias, gdn_norm_w, gdn_w_out, dil_w_in, dil_q_norm, dil_k_norm, dil_w_out, ffn_w_in, ffn_w_out, loss_target, m_norm_mix, m_norm_ffn, m_gdn_w_in, m_gdn_conv_w, m_gdn_a_log, m_gdn_dt_bias, m_gdn_norm_w, m_gdn_w_out, m_dil_w_in, m_dil_q_norm, m_dil_k_norm, m_dil_w_out, m_ffn_w_in, m_ffn_w_out, v_norm_mix, v_norm_ffn, v_gdn_w_in, v_gdn_conv_w, v_gdn_a_log, v_gdn_dt_bias, v_gdn_norm_w, v_gdn_w_out, v_dil_w_in, v_dil_q_norm, v_dil_k_norm, v_dil_w_out, v_ffn_w_in, v_ffn_w_out):
    w = dict(norm_mix=norm_mix, norm_ffn=norm_ffn, gdn_w_in=gdn_w_in, gdn_conv_w=gdn_conv_w, gdn_a_log=gdn_a_log,
             gdn_dt_bias=gdn_dt_bias, gdn_norm_w=gdn_norm_w, gdn_w_out=gdn_w_out, dil_w_in=dil_w_in, dil_q_norm=dil_q_norm,
             dil_k_norm=dil_k_norm, dil_w_out=dil_w_out, ffn_w_in=ffn_w_in, ffn_w_out=ffn_w_out)
    m = dict(norm_mix=m_norm_mix, norm_ffn=m_norm_ffn, gdn_w_in=m_gdn_w_in, gdn_conv_w=m_gdn_conv_w, gdn_a_log=m_gdn_a_log,
             gdn_dt_bias=m_gdn_dt_bias, gdn_norm_w=m_gdn_norm_w, gdn_w_out=m_gdn_w_out, dil_w_in=m_dil_w_in,
             dil_q_norm=m_dil_q_norm, dil_k_norm=m_dil_k_norm, dil_w_out=m_dil_w_out, ffn_w_in=m_ffn_w_in, ffn_w_out=m_ffn_w_out)
    v = dict(norm_mix=v_norm_mix, norm_ffn=v_norm_ffn, gdn_w_in=v_gdn_w_in, gdn_conv_w=v_gdn_conv_w, gdn_a_log=v_gdn_a_log,
             gdn_dt_bias=v_gdn_dt_bias, gdn_norm_w=v_gdn_norm_w, gdn_w_out=v_gdn_w_out, dil_w_in=v_dil_w_in,
             dil_q_norm=v_dil_q_norm, dil_k_norm=v_dil_k_norm, dil_w_out=v_dil_w_out, ffn_w_in=v_ffn_w_in, ffn_w_out=v_ffn_w_out)
    def row(src, i):
        return src[i].reshape(1, D_MODEL)

    first = _all_gather(_gather_operand(w, GATHER_FIRST), name="weight_all_gather_first")
    next_started = _travel_start(_gather_operand(w, GATHER_NEXT), first, per_peer=False, name="weight_gather_start_next")
    last_started = _travel_start(_gather_operand(w, GATHER_LAST), next_started[4], per_peer=False,
                                 name="weight_gather_start_last")
    full = _gathered_weights(first, GATHER_FIRST, {n: w[n] for n in REPLICATED})
    prepared = dict(gdn=_prepare_gdn(full, layers=(0,)))
    h = x[0]
    saved = [None] * DEPTH
    h, s_mix = _mixer_fwd(0, h, row(norm_mix, 0) + last_started[4][0, 0], prepared)
    got = _travel_wait(next_started, h, per_peer=False, name="weight_gather_wait_next")
    full = _gathered_weights(got, GATHER_NEXT, full)
    prepared.update(dil=_prepare_dil(full, layers=(0,)), ffn=_prepare_ffn(full, layers=(0,)))
    for i in range(DEPTH):
        if i > 0:
            h, s_mix = _mixer_fwd(i, h, row(norm_mix, i), prepared)
        if i == 1:
            got = _travel_wait(last_started, h, per_peer=False, name="weight_gather_wait_last")
            full = _gathered_weights(got, GATHER_LAST, full)
            prepared["gdn"].update(_prepare_gdn(full, layers=(1,)))
            prepared["dil"].update(_prepare_dil(full, layers=(1,)))
            prepared["ffn"].update(_prepare_ffn(full, layers=(1, 2, 3)))
        h, s_ffn = _ffn_layer_fwd(h, row(norm_ffn, i), prepared["ffn"][i])
        saved[i] = (s_mix, s_ffn)
    dx, dxb, loss = _loss_head(h, loss_target[0], name="loss_head")

    g_mix, g_ffn = [None] * DEPTH, [None] * DEPTH
    started = {}

    def travel(group):
        operand = _exchange_operand(_collect_grads(g_mix, g_ffn), EXCHANGE_GROUPS[group])
        started[group] = _travel_start(operand, dx, per_peer=True, name=f"grad_exchange_start_{group}")
        return started[group][4][0, 0]

    zero = 0.0
    for i in reversed(range(DEPTH)):
        s_mix, s_ffn = saved[i]
        dx, dxb, g_ffn[i] = _ffn_layer_bwd(dx, dxb, row(norm_ffn, i) + zero, prepared["ffn"][i], s_ffn)
        zero = travel(EXCHANGE_AFTER[("ffn", i)]) if ("ffn", i) in EXCHANGE_AFTER else 0.0
        dx, dxb, g_mix[i] = _mixer_bwd(i, dx, dxb, row(norm_mix, i), prepared, s_mix, zero)
        zero = travel(EXCHANGE_AFTER[("mix", i)]) if ("mix", i) in EXCHANGE_AFTER else 0.0
    grads = _collect_grads(g_mix, g_ffn)
    received = [_travel_wait(started[g], dx, per_peer=True, name=f"grad_exchange_wait_{g}") for g in sorted(started)]
    received.append(_exchange(_exchange_operand(grads, EXCHANGE_GROUPS[-1]), name="grad_exchange_last"))
    updated = {}
    for g, pieces in enumerate(EXCHANGE_GROUPS):
        updated.update(_update_group(received[g], pieces, w, m, v, name=f"adamw_sharded_{g}"))

    small_parts = _all_gather(_pack_small(grads), name="small_grad_all_gather")
    outs_small = [_unpack_small(o) for o in
                  _adamw(small_parts, _pack_small(w), _pack_small(m), _pack_small(v), name="adamw_replicated")]

    total_loss = lax.psum(loss[0, 0], ("x", "y", "c"))
    result = [total_loss, dx[None]]
    for k in range(4):
        for n in WEIGHT_ORDER:
            if n not in SHARDED:
                result.append(outs_small[k][n])
            elif (n, None) in updated:
                result.append(updated[(n, None)][k])
            else:
                result.append(jnp.concatenate([updated[(n, l)][k] for l in range(SHARDED[n][0][0])], axis=0))
    return tuple(result)
```

```python
import functools
import math

import jax
import jax.numpy as jnp
from jax import lax
from jax.experimental import pallas as pl
from jax.experimental.pallas import tpu as pltpu

F32 = jnp.float32
BF16 = jnp.bfloat16
MM_DTYPE = BF16

N_DEV = 8
D_MODEL = 1024
DEPTH = 4
RMS_EPS = 1e-6
L2_EPS = 1e-6

LANES = 128

GDN_HEADS = 8
GDN_DK = 128
GDN_DV = 128
GDN_CONV = 4
GDN_CHUNK = 128
GDN_QKV = 3 * GDN_HEADS * GDN_DK
GDN_MAIN = GDN_QKV + GDN_HEADS * GDN_DV
GDN_IN_WIDTH = GDN_MAIN + 2 * GDN_HEADS

DIL_GROUPS = ((128, 1), (512, 4), (2048, 16))
DIL_HEADS = 8
DIL_DH = 64
DIL_SPAN = 128
DIL_SLAB = 3 * DIL_HEADS * LANES
ALIBI_MAX_BIAS = 8.0

FFN_HIDDEN = 2816

ADAM_LR = 0.001
ADAM_B1 = 0.9
ADAM_B2 = 0.999
ADAM_EPS = 1e-08
ADAM_WD = 0.01
ADAM_STEP = 10

VMEM_LIMIT = 56 * 1024 * 1024
ROW_TILE = 512
MATMUL_VMEM_BUDGET = 40 * 1024 * 1024
NEG = -1e30
HI = lax.Precision.HIGHEST


def _cparams(sem):
    return pltpu.CompilerParams(dimension_semantics=sem, vmem_limit_bytes=VMEM_LIMIT)


def _dot(a, b):
    return lax.dot_general(a, b, (((1,), (0,)), ((), ())), preferred_element_type=F32, precision=HI)


def _dot_nt(a, b):
    return lax.dot_general(a, b, (((1,), (1,)), ((), ())), preferred_element_type=F32, precision=HI)


def _dot_tn(a, b):
    return lax.dot_general(a, b, (((0,), (0,)), ((), ())), preferred_element_type=F32, precision=HI)


def _single_pass(a, b, a_dim, b_dim):
    lead = a.ndim - 2
    batch = ((0,), (0,)) if lead else ((), ())
    return lax.dot_general(a.astype(BF16), b.astype(BF16), (((lead + a_dim,), (lead + b_dim,)), batch),
                           preferred_element_type=F32)


def _bdot(a, b):
    return _single_pass(a, b, 1, 0)


def _bdot_nt(a, b):
    return _single_pass(a, b, 1, 1)


def _bdot_tn(a, b):
    return _single_pass(a, b, 0, 0)


def _pick(n, candidates):
    for c in candidates:
        if n % c == 0:
            return c
    raise ValueError(f"no tile for {n}")


HALF = LANES // 2


def _pack_head_pairs(x):
    x = x.astype(F32)
    tiles = [x[:, (2 * i) * LANES:(2 * i + 1) * LANES] + pltpu.roll(x[:, (2 * i + 1) * LANES:(2 * i + 2) * LANES], HALF, 1)
             for i in range(x.shape[1] // (2 * LANES))]
    return tiles[0] if len(tiles) == 1 else jnp.concatenate(tiles, axis=1)


def _spread_head_pairs(y):
    low = lax.broadcasted_iota(jnp.int32, (y.shape[0], LANES), 1) < HALF
    tiles = []
    for i in range(y.shape[1] // LANES):
        pair = y[:, i * LANES:(i + 1) * LANES]
        tiles += [jnp.where(low, pair, 0.0), jnp.where(low, pltpu.roll(pair, HALF, 1), 0.0)]
    return jnp.concatenate(tiles, axis=1)


def _matmul(a, b, *, name, trans_a=False, trans_b=False, b_rows=None, a_lead=None, add=None, out_dtype=F32,
            packed_a=False, spread_out=False, tiles=None):
    if trans_a:
        k_dim, m_dim = a.shape[-2:]
        m_dim = m_dim // 2 if packed_a else m_dim
    else:
        m_dim, k_dim = a.shape[-2:]
        k_dim = k_dim // 2 if packed_a else k_dim
    slab_m, slab_k = m_dim, k_dim
    if a_lead == "k":
        assert not trans_a
        k_dim *= a.shape[0]
    elif a_lead == "i":
        assert trans_a
        m_dim *= a.shape[0]
    b_start, b_size = b_rows if b_rows is not None else (0, b.shape[0])
    if trans_b:
        n_dim, k2 = b_size, b.shape[1]
    else:
        k2, n_dim = b_size, b.shape[1]
    assert k_dim == k2, (a.shape, b.shape, b_rows)
    tn = _pick(n_dim, (1024, 512, 256, 128))
    tm = min(slab_m, 2048, max(512, (1024 * 1024) // tn))
    tm = _pick(slab_m, (tm, 1408, 1024, 512, 256, 128))
    out_bytes = jnp.dtype(out_dtype).itemsize * (2 if spread_out else 1)

    def deepest(rows):
        fixed = rows * tn * (2 * out_bytes + 4 + (8 if add is not None else 0))
        fits = lambda c: fixed + 2 * 2 * c * ((2 if packed_a else 1) * rows + tn) <= MATMUL_VMEM_BUDGET
        return _pick(slab_k, tuple(c for c in (3072, 2816, 2048, 1536, 1408, 1024, 512, 256) if fits(c)) + (128,))

    tk = deepest(tm)
    if tm % 1024 == 0 and deepest(tm // 2) > tk:
        tm, tk = tm // 2, deepest(tm // 2)
    if tiles is not None:
        tm, tn, tk = tiles
    nk = k_dim // tk
    has_add = add is not None
    dn = (((0 if trans_a else 1,), (1 if trans_b else 0,)), ((), ()))
    b_tile = tn if trans_b else tk
    assert b_start % b_tile == 0, (b_rows, b_tile)
    b_off = b_start // b_tile

    def body(*refs):
        if has_add:
            a_ref, b_ref, add_ref, o_ref, acc_ref = refs
        else:
            a_ref, b_ref, o_ref, acc_ref = refs
        a_blk = _pack_head_pairs(a_ref[...]).astype(a_ref.dtype) if packed_a else a_ref[...]
        part = lax.dot_general(a_blk, b_ref[...], dn, preferred_element_type=F32)

        def finish(total):
            if has_add:
                total = total + add_ref[...]
            if spread_out:
                total = _spread_head_pairs(total)
            o_ref[...] = total.astype(out_dtype)

        if nk == 1:
            finish(part)
        else:
            k = pl.program_id(2)

            @pl.when(k == 0)
            def _():
                acc_ref[...] = part

            @pl.when(k > 0)
            def _():
                acc_ref[...] += part

            @pl.when(k == nk - 1)
            def _():
                finish(acc_ref[...])

    wide = 2 if packed_a else 1
    a_tile = (tk, wide * tm) if trans_a else (tm, wide * tk)
    a_at = (lambda i, j, k: (k, i)) if trans_a else (lambda i, j, k: (i, k))
    if a_lead is None:
        a_spec = pl.BlockSpec(a_tile, a_at)
    elif a_lead == "k":
        per = slab_k // tk
        a_spec = pl.BlockSpec((None,) + a_tile, lambda i, j, k: (k // per, i, k % per))
    elif a_lead == "i":
        per = slab_m // tm
        a_spec = pl.BlockSpec((None,) + a_tile, lambda i, j, k: (i // per, k, i % per))
    else:
        a_spec = pl.BlockSpec((None,) + a_tile, lambda i, j, k: (a_lead,) + a_at(i, j, k))
    if trans_b:
        b_spec = pl.BlockSpec((tn, tk), lambda i, j, k: (j + b_off, k))
    else:
        b_spec = pl.BlockSpec((tk, tn), lambda i, j, k: (k + b_off, j))
    in_specs = [a_spec, b_spec]
    args = [a, b]
    if has_add:
        in_specs.append(pl.BlockSpec((tm, tn), lambda i, j, k: (i, j)))
        args.append(add)
    return pl.pallas_call(
        body,
        grid=(m_dim // tm, n_dim // tn, nk),
        in_specs=in_specs,
        out_specs=pl.BlockSpec((tm, (2 if spread_out else 1) * tn), lambda i, j, k: (i, j)),
        out_shape=jax.ShapeDtypeStruct((m_dim, (2 if spread_out else 1) * n_dim), out_dtype),
        scratch_shapes=[pltpu.VMEM((tm, tn) if nk > 1 else (8, LANES), F32)],
        compiler_params=_cparams(("parallel", "parallel", "arbitrary")),
        name=name,
    )(*args)


def _rmsnorm_fwd(x, w_row, *, name):
    t, d = x.shape
    tb = min(t, 1024)

    def body(x_ref, w_ref, o_ref):
        xf = x_ref[...]
        r = lax.rsqrt(jnp.mean(xf * xf, axis=-1, keepdims=True) + RMS_EPS)
        o_ref[...] = (xf * r * w_ref[...]).astype(o_ref.dtype)

    return pl.pallas_call(
        body,
        grid=(t // tb,),
        in_specs=[pl.BlockSpec((tb, d), lambda i: (i, 0)), pl.BlockSpec((1, d), lambda i: (0, 0))],
        out_specs=pl.BlockSpec((tb, d), lambda i: (i, 0)),
        out_shape=jax.ShapeDtypeStruct((t, d), MM_DTYPE),
        compiler_params=_cparams(("parallel",)),
        name=name,
    )(x, w_row)


def _rmsnorm_bwd(x, w_row, dy, dskip, *, name):
    t, d = x.shape
    tb = min(t, 512)

    def body(x_ref, w_ref, dy_ref, ds_ref, dx_ref, dxb_ref, dw_ref):
        xf = x_ref[...]
        g = dy_ref[...]
        r = lax.rsqrt(jnp.mean(xf * xf, axis=-1, keepdims=True) + RMS_EPS)
        gw = g * w_ref[...]
        proj = jnp.mean(gw * xf, axis=-1, keepdims=True)
        dx = r * gw - xf * (r * r * r * proj) + ds_ref[...]
        dx_ref[...] = dx
        dxb_ref[...] = dx.astype(dxb_ref.dtype)
        part = jnp.sum(g * xf * r, axis=0, keepdims=True)

        @pl.when(pl.program_id(0) == 0)
        def _():
            dw_ref[...] = part

        @pl.when(pl.program_id(0) > 0)
        def _():
            dw_ref[...] += part

    row = pl.BlockSpec((tb, d), lambda i: (i, 0))
    one = pl.BlockSpec((1, d), lambda i: (0, 0))
    return pl.pallas_call(
        body,
        grid=(t // tb,),
        in_specs=[row, one, row, row],
        out_specs=[row, row, one],
        out_shape=[jax.ShapeDtypeStruct((t, d), F32), jax.ShapeDtypeStruct((t, d), MM_DTYPE),
                   jax.ShapeDtypeStruct((1, d), F32)],
        compiler_params=_cparams(("arbitrary",)),
        name=name,
    )(x, w_row, dy, dskip)


def _silu(z):
    return z / (1.0 + jnp.exp(-z))


FFN_TM, FFN_TN = 512, 1408


def _ffn_in(hn, in_t, *, name):
    t, d = hn.shape
    h = FFN_HIDDEN
    tm, tn = min(t, FFN_TM), FFN_TN
    nj = h // tn
    dn = (((1,), (1,)), ((), ()))

    def body(a_ref, bg_ref, bu_ref, g_ref, u_ref, act_ref):
        a = a_ref[...]
        g = lax.dot_general(a, bg_ref[...], dn, preferred_element_type=F32)
        u = lax.dot_general(a, bu_ref[...], dn, preferred_element_type=F32)
        g_ref[...] = g.astype(g_ref.dtype)
        u_ref[...] = u.astype(u_ref.dtype)
        act_ref[...] = (_silu(g) * u).astype(act_ref.dtype)

    out = pl.BlockSpec((tm, tn), lambda j, i: (i, j))
    return pl.pallas_call(
        body,
        grid=(nj, t // tm),
        in_specs=[pl.BlockSpec((tm, d), lambda j, i: (i, 0)), pl.BlockSpec((tn, d), lambda j, i: (j, 0)),
                  pl.BlockSpec((tn, d), lambda j, i: (j + nj, 0))],
        out_specs=[out, out, out],
        out_shape=[jax.ShapeDtypeStruct((t, h), MM_DTYPE)] * 3,
        compiler_params=_cparams(("parallel", "parallel")),
        name=name,
    )(hn, in_t, in_t)


def _ffn_dact(dy, out_w, g, u, *, name):
    t, d = dy.shape
    h = FFN_HIDDEN
    tm, tn = min(t, FFN_TM), FFN_TN

    def body(a_ref, b_ref, g_ref, u_ref, d_ref):
        da = lax.dot_general(a_ref[...], b_ref[...], (((1,), (1,)), ((), ())), preferred_element_type=F32)
        gate = g_ref[...].astype(F32)
        sig = 1.0 / (1.0 + jnp.exp(-gate))
        sg = gate * sig
        d_ref[0] = (da * u_ref[...].astype(F32) * (sig + sg * (1.0 - sig))).astype(d_ref.dtype)
        d_ref[1] = (da * sg).astype(d_ref.dtype)

    blk = pl.BlockSpec((tm, tn), lambda j, i: (i, j))
    return pl.pallas_call(
        body,
        grid=(h // tn, t // tm),
        in_specs=[pl.BlockSpec((tm, d), lambda j, i: (i, 0)), pl.BlockSpec((tn, d), lambda j, i: (j, 0)), blk, blk],
        out_specs=pl.BlockSpec((2, tm, tn), lambda j, i: (0, i, j)),
        out_shape=jax.ShapeDtypeStruct((2, t, h), MM_DTYPE),
        compiler_params=_cparams(("parallel", "parallel")),
        name=name,
    )(dy, out_w, g, u)


def _loss_head(y, target, *, name):
    t, d = y.shape
    tb = min(t, 1024)

    def body(y_ref, t_ref, dy_ref, dyb_ref, l_ref):
        err = y_ref[...] - t_ref[...]
        dy_ref[...] = err * (1.0 / d)
        dyb_ref[...] = (err * (1.0 / d)).astype(dyb_ref.dtype)
        part = jnp.sum(jnp.sum(err * err, axis=0, keepdims=True), axis=1, keepdims=True) * (0.5 / d)
        part = jnp.broadcast_to(part, l_ref.shape)

        @pl.when(pl.program_id(0) == 0)
        def _():
            l_ref[...] = part

        @pl.when(pl.program_id(0) > 0)
        def _():
            l_ref[...] += part

    row = pl.BlockSpec((tb, d), lambda i: (i, 0))
    return pl.pallas_call(
        body,
        grid=(t // tb,),
        in_specs=[row, row],
        out_specs=[row, row, pl.BlockSpec((8, LANES), lambda i: (0, 0))],
        out_shape=[jax.ShapeDtypeStruct((t, d), F32), jax.ShapeDtypeStruct((t, d), MM_DTYPE),
                   jax.ShapeDtypeStruct((8, LANES), F32)],
        compiler_params=_cparams(("arbitrary",)),
        name=name,
    )(y, target)


CONV_HALO = 8
CONV_TIME_TILE = 2048


def _conv_tile_scale(c):
    is_qk = c < 2 * GDN_HEADS
    scale = jnp.where(c < GDN_HEADS, GDN_DK ** -0.5, 1.0).astype(F32)
    return is_qk, scale


def _gdn_conv_fwd(pm, conv_w, *, name):
    t = pm.shape[0]
    tb = min(t, CONV_TIME_TILE)
    nt = t // tb
    hb = tb // CONV_HALO

    def body(x_ref, xp_ref, w_ref, o_ref, xe_ref):
        c = pl.program_id(0)
        ti = pl.program_id(1)
        xe_ref[0:CONV_HALO, :] = jnp.where(ti > 0, xp_ref[...], 0.0)
        xe_ref[CONV_HALO:CONV_HALO + tb, :] = x_ref[...]
        w = w_ref[...]
        y = jnp.zeros((tb, LANES), F32)
        for j in range(GDN_CONV):
            off = CONV_HALO - (GDN_CONV - 1) + j
            y = y + w[j:j + 1, :] * xe_ref[pl.ds(off, tb), :]
        s = _silu(y)
        is_qk, scale = _conv_tile_scale(c)
        r = lax.rsqrt(jnp.sum(s * s, axis=-1, keepdims=True) + L2_EPS) * scale
        o_ref[...] = s * jnp.where(is_qk, r, 1.0)

    return pl.pallas_call(
        body,
        grid=(GDN_QKV // LANES, nt),
        in_specs=[
            pl.BlockSpec((tb, LANES), lambda c, i: (i, c)),
            pl.BlockSpec((CONV_HALO, LANES), lambda c, i: (jnp.maximum(i * hb - 1, 0), c)),
            pl.BlockSpec((GDN_CONV, LANES), lambda c, i: (0, c)),
        ],
        out_specs=pl.BlockSpec((tb, LANES), lambda c, i: (i, c)),
        out_shape=jax.ShapeDtypeStruct((t, GDN_QKV), F32),
        scratch_shapes=[pltpu.VMEM((tb + CONV_HALO, LANES), F32)],
        compiler_params=_cparams(("parallel", "parallel")),
        name=name,
    )(pm, pm, conv_w)


def _gdn_conv_bwd(pm, conv_w, dout, *, name):
    t = pm.shape[0]
    tb = min(t, CONV_TIME_TILE)
    nt = t // tb
    hb = tb // CONV_HALO
    last_hb = t // CONV_HALO - 1
    ext = tb + CONV_HALO

    def body(x_ref, xp_ref, xn_ref, d_ref, dn_ref, w_ref, dx_ref, dw_ref, xe_ref, dy_ref):
        c = pl.program_id(0)
        ti = pl.program_id(1)
        has_next = ti < nt - 1
        xe_ref[0:CONV_HALO, :] = jnp.where(ti > 0, xp_ref[...], 0.0)
        xe_ref[CONV_HALO:CONV_HALO + tb, :] = x_ref[...]
        xe_ref[CONV_HALO + tb:2 * CONV_HALO + tb, :] = jnp.where(has_next, xn_ref[...], 0.0)
        de = jnp.concatenate([d_ref[...], jnp.where(has_next, dn_ref[...], 0.0)], axis=0)
        w = w_ref[...]
        y = jnp.zeros((ext, LANES), F32)
        for j in range(GDN_CONV):
            off = CONV_HALO - (GDN_CONV - 1) + j
            y = y + w[j:j + 1, :] * xe_ref[pl.ds(off, ext), :]
        sig = 1.0 / (1.0 + jnp.exp(-y))
        s = y * sig
        is_qk, scale = _conv_tile_scale(c)
        r = lax.rsqrt(jnp.sum(s * s, axis=-1, keepdims=True) + L2_EPS)
        n = s * r
        dnrm = de * scale
        ds_qk = r * (dnrm - n * jnp.sum(dnrm * n, axis=-1, keepdims=True))
        ds = jnp.where(is_qk, ds_qk, de)
        dy_ref[...] = ds * (sig + s * (1.0 - sig))
        dy = dy_ref[0:tb, :]
        dx = jnp.zeros((tb, LANES), F32)
        dw_rows = []
        for j in range(GDN_CONV):
            sh = GDN_CONV - 1 - j
            dx = dx + w[j:j + 1, :] * dy_ref[pl.ds(sh, tb), :]
            off = CONV_HALO - (GDN_CONV - 1) + j
            dw_rows.append(jnp.sum(dy * xe_ref[pl.ds(off, tb), :], axis=0, keepdims=True))
        dx_ref[...] = dx.astype(dx_ref.dtype)
        part = jnp.concatenate(dw_rows, axis=0)

        @pl.when(ti == 0)
        def _():
            dw_ref[...] = part

        @pl.when(ti > 0)
        def _():
            dw_ref[...] += part

    main = pl.BlockSpec((tb, LANES), lambda c, i: (i, c))
    prev = pl.BlockSpec((CONV_HALO, LANES), lambda c, i: (jnp.maximum(i * hb - 1, 0), c))
    nxt = pl.BlockSpec((CONV_HALO, LANES), lambda c, i: (jnp.minimum((i + 1) * hb, last_hb), c))
    return pl.pallas_call(
        body,
        grid=(GDN_QKV // LANES, nt),
        in_specs=[main, prev, nxt, main, nxt, pl.BlockSpec((GDN_CONV, LANES), lambda c, i: (0, c))],
        out_specs=[main, pl.BlockSpec((GDN_CONV, LANES), lambda c, i: (0, c))],
        out_shape=[jax.ShapeDtypeStruct((t, GDN_QKV), MM_DTYPE), jax.ShapeDtypeStruct((GDN_CONV, GDN_QKV), F32)],
        scratch_shapes=[pltpu.VMEM((tb + 2 * CONV_HALO, LANES), F32), pltpu.VMEM((ext, LANES), F32)],
        compiler_params=_cparams(("parallel", "arbitrary")),
        name=name,
    )(pm, pm, pm, dout, dout, conv_w)


def _head_selector(first_col):
    row = lax.broadcasted_iota(jnp.int32, (LANES, GDN_HEADS * LANES), 0)
    col = lax.broadcasted_iota(jnp.int32, (LANES, GDN_HEADS * LANES), 1)
    return (col // LANES + first_col == row).astype(BF16)


def _spread_columns(cols, first_col):
    sel = _head_selector(first_col)
    return sum(_bdot(p, sel) for p in _bf16_pieces(cols))


def _gather_columns(wide, first_col):
    sel = _head_selector(first_col)
    return sum(_bdot_nt(p, sel) for p in _bf16_pieces(wide))


def _softplus(x):
    return jnp.maximum(x, 0.0) + jnp.log(1.0 + jnp.exp(-jnp.abs(x)))


def _gdn_gates_fwd(ab, alog_row, dt_row, *, name):
    t = ab.shape[0]
    tb = min(t, 1024)
    wide = GDN_HEADS * LANES

    def body(ab_ref, al_ref, dt_ref, g_ref, b_ref):
        x = ab_ref[...]
        g_cols = -jnp.exp(al_ref[...]) * _softplus(x + dt_ref[...])
        b_cols = 1.0 / (1.0 + jnp.exp(-x))
        g_ref[...] = _spread_columns(g_cols, 0)
        b_ref[...] = _spread_columns(b_cols, GDN_HEADS)

    row = pl.BlockSpec((tb, LANES), lambda i: (i, 0))
    one = pl.BlockSpec((1, LANES), lambda i: (0, 0))
    out = pl.BlockSpec((tb, wide), lambda i: (i, 0))
    return pl.pallas_call(
        body,
        grid=(t // tb,),
        in_specs=[row, one, one],
        out_specs=[out, out],
        out_shape=[jax.ShapeDtypeStruct((t, wide), F32)] * 2,
        compiler_params=_cparams(("parallel",)),
        name=name,
    )(ab, alog_row, dt_row)


def _gdn_gates_bwd(ab, alog_row, dt_row, dgb, dbb, *, name):
    t = ab.shape[0]
    tb = min(t, 1024)
    wide = GDN_HEADS * LANES

    def body(ab_ref, al_ref, dt_ref, dg_ref, db_ref, dab_ref, dal_ref, ddt_ref):
        x = ab_ref[...]
        lane = lax.broadcasted_iota(jnp.int32, (tb, LANES), 1)
        dg_cols = _gather_columns(dg_ref[...], 0)
        db_cols = _gather_columns(db_ref[...], GDN_HEADS)
        ea = jnp.exp(al_ref[...])
        z = x + dt_ref[...]
        sp = _softplus(z)
        sg = 1.0 / (1.0 + jnp.exp(-z))
        beta = 1.0 / (1.0 + jnp.exp(-x))
        da = jnp.where(lane < GDN_HEADS, dg_cols * (-ea) * sg, 0.0)
        db = jnp.where((lane >= GDN_HEADS) & (lane < 2 * GDN_HEADS), db_cols * beta * (1.0 - beta), 0.0)
        dab_ref[...] = (da + db).astype(dab_ref.dtype)
        p_al = jnp.sum(jnp.where(lane < GDN_HEADS, dg_cols * (-ea) * sp, 0.0), axis=0, keepdims=True)
        p_dt = jnp.sum(da, axis=0, keepdims=True)

        @pl.when(pl.program_id(0) == 0)
        def _():
            dal_ref[...] = p_al
            ddt_ref[...] = p_dt

        @pl.when(pl.program_id(0) > 0)
        def _():
            dal_ref[...] += p_al
            ddt_ref[...] += p_dt

    row = pl.BlockSpec((tb, LANES), lambda i: (i, 0))
    one = pl.BlockSpec((1, LANES), lambda i: (0, 0))
    big = pl.BlockSpec((tb, wide), lambda i: (i, 0))
    return pl.pallas_call(
        body,
        grid=(t // tb,),
        in_specs=[row, one, one, big, big],
        out_specs=[row, one, one],
        out_shape=[jax.ShapeDtypeStruct((t, LANES), MM_DTYPE), jax.ShapeDtypeStruct((1, LANES), F32),
                   jax.ShapeDtypeStruct((1, LANES), F32)],
        compiler_params=_cparams(("arbitrary",)),
        name=name,
    )(ab, alog_row, dt_row, dgb, dbb)


@jax.custom_vjp
def _unit_lower_inverse_rest(n):
    c = n.shape[-1]
    ri = lax.broadcasted_iota(jnp.int32, (c, c), 0)
    ci = lax.broadcasted_iota(jnp.int32, (c, c), 1)
    rest = None
    size = 1
    while size < c:
        joins = ((ri // (2 * size)) == (ci // (2 * size))) & ((ri // size) != (ci // size))
        low = jnp.where(joins, n, 0.0)
        if rest is None:
            rest = -low
        else:
            left = low + _bdot(rest, low)
            rest = rest - (left + _bdot(left, rest))
        size *= 2
    return rest


def _unit_lower_inverse_rest_fwd(n):
    rest = _unit_lower_inverse_rest(n)
    return rest, rest


def _unit_lower_inverse_rest_bwd(rest, ct):
    left = ct + _bdot_tn(rest, ct)
    return (-(left + _bdot_nt(left, rest)),)


_unit_lower_inverse_rest.defvjp(_unit_lower_inverse_rest_fwd, _unit_lower_inverse_rest_bwd)


@jax.custom_vjp
def _known_inverse_rest(n, rest):
    return rest


def _known_inverse_rest_fwd(n, rest):
    return rest, rest


def _known_inverse_rest_bwd(rest, ct):
    return _unit_lower_inverse_rest_bwd(rest, ct) + (jnp.zeros_like(rest),)


_known_inverse_rest.defvjp(_known_inverse_rest_fwd, _known_inverse_rest_bwd)


def _bf16_pieces(x):
    hi = x.astype(BF16)
    r1 = x - hi.astype(F32)
    mid = r1.astype(BF16)
    lo = (r1 - mid.astype(F32)).astype(BF16)
    return hi, mid, lo


def _lower_ones(shape):
    c = shape[-1]
    ri = lax.broadcasted_iota(jnp.int32, (c, c), 0)
    ci = lax.broadcasted_iota(jnp.int32, (c, c), 1)
    return jnp.broadcast_to((ri >= ci).astype(BF16), shape)


@jax.custom_vjp
def _running_sum(x):
    tri = _lower_ones(x.shape)
    return sum(_bdot(tri, p) for p in _bf16_pieces(x))


def _running_sum_fwd(x):
    return _running_sum(x), None


def _running_sum_bwd(_, ct):
    tri = _lower_ones(ct.shape)
    return (sum(_bdot_tn(tri, p) for p in _bf16_pieces(ct)),)


_running_sum.defvjp(_running_sum_fwd, _running_sum_bwd)


def _gdn_prep_math(q, k, v, gb, bb, known_rest=None, with_rest=False):
    c = GDN_CHUNK
    ri = lax.broadcasted_iota(jnp.int32, (c, c), 0)
    ci = lax.broadcasted_iota(jnp.int32, (c, c), 1)
    causal = ri >= ci
    gc = _running_sum(gb)
    decay = jnp.exp(jnp.where(causal, gc - jnp.swapaxes(gc, -1, -2), NEG))
    n = jnp.where(ri > ci, _bdot_nt(k, k) * bb * decay, 0.0)
    rest = _unit_lower_inverse_rest(n) if known_rest is None else _known_inverse_rest(n, known_rest)
    eg = jnp.exp(gc)
    rhs_v = v * bb
    rhs_k = k * bb * eg
    u = rhs_v + _bdot(rest, rhs_v)
    w = rhs_k + _bdot(rest, rhs_k)
    qk = _bdot_nt(q, k) * decay
    qd = q * eg
    last = jnp.sum(jnp.where(ri == c - 1, gc, 0.0), axis=-2, keepdims=True)
    gl = jnp.broadcast_to(last, gc.shape)
    kt = k * jnp.exp(gl - gc)
    cd = jnp.exp(gl)
    return (u, w, qk, qd, kt, cd, rest) if with_rest else (u, w, qk, qd, kt, cd)


def _head_tiles(ref, h):
    return ref[:, h * LANES:(h + 1) * LANES]


def _stack_heads(ref, first=0, heads=GDN_HEADS):
    return jnp.stack([_head_tiles(ref, first + h) for h in range(heads)])


def _store_heads(ref, val, first=0):
    for h in range(val.shape[0]):
        ref[:, (first + h) * LANES:(first + h + 1) * LANES] = val[h].astype(ref.dtype)


def _gdn_prep_fwd(qkv, gb, bb, *, name):
    t = qkv.shape[0]
    c = GDN_CHUNK
    wide = GDN_HEADS * LANES

    def body(q_ref, k_ref, v_ref, g_ref, b_ref, *outs):
        res = _gdn_prep_math(*(_stack_heads(r) for r in (q_ref, k_ref, v_ref, g_ref, b_ref)), with_rest=True)
        for o_ref, val in zip(outs, res):
            _store_heads(o_ref, val)

    blk = lambda off: pl.BlockSpec((c, wide), lambda i: (i, off))
    outs = pl.pallas_call(
        body,
        grid=(t // c,),
        in_specs=[blk(0), blk(1), blk(2), blk(0), blk(0)],
        out_specs=[blk(0)] * 7,
        out_shape=[jax.ShapeDtypeStruct((t, wide), dt) for dt in (F32, MM_DTYPE, MM_DTYPE, MM_DTYPE, MM_DTYPE, F32, F32)],
        compiler_params=_cparams(("parallel",)),
        name=name,
    )(qkv, qkv, qkv, gb, bb)
    return tuple(outs[:6]), outs[6]


def _gdn_prep_bwd(qkv, gb, bb, rest, cts, *, name):
    t = qkv.shape[0]
    c = GDN_CHUNK
    wide = GDN_HEADS * LANES

    def body(q_ref, k_ref, v_ref, g_ref, b_ref, r_ref, c0, c1, c2, c3, c4, c5, dqkv_ref, dg_ref, db_ref):
        prim = tuple(_stack_heads(r) for r in (q_ref, k_ref, v_ref, g_ref, b_ref))
        _, pull = jax.vjp(functools.partial(_gdn_prep_math, known_rest=_stack_heads(r_ref)), *prim)
        dq, dk, dv, dg, db = pull(tuple(_stack_heads(r) for r in (c0, c1, c2, c3, c4, c5)))
        _store_heads(dqkv_ref, dq)
        _store_heads(dqkv_ref, dk, first=GDN_HEADS)
        _store_heads(dqkv_ref, dv, first=2 * GDN_HEADS)
        _store_heads(dg_ref, dg)
        _store_heads(db_ref, db)

    blk = lambda off: pl.BlockSpec((c, wide), lambda i: (i, off))
    return pl.pallas_call(
        body,
        grid=(t // c,),
        in_specs=[blk(0), blk(1), blk(2), blk(0), blk(0)] + [blk(0)] * 7,
        out_specs=[pl.BlockSpec((c, 3 * wide), lambda i: (i, 0)), blk(0), blk(0)],
        out_shape=[jax.ShapeDtypeStruct((t, 3 * wide), F32), jax.ShapeDtypeStruct((t, wide), F32),
                   jax.ShapeDtypeStruct((t, wide), F32)],
        compiler_params=_cparams(("parallel",)),
        name=name,
    )(qkv, qkv, qkv, gb, bb, rest, *cts)


def _gdn_scan_math(s, u, w, qk, qd, kt, cd):
    v_new = u - _bdot(w, s)
    o = _bdot(qd, s) + _bdot(qk, v_new)
    s_new = s * cd + _bdot_tn(kt, v_new)
    return o, s_new


def _gdn_scan_fwd(prep, *, name):
    t = prep[0].shape[0]
    c = GDN_CHUNK
    wide = GDN_HEADS * LANES

    def body(u_ref, w_ref, qk_ref, qd_ref, kt_ref, cd_ref, o_ref, st_ref, s_ref):
        @pl.when(pl.program_id(0) == 0)
        def _():
            s_ref[...] = jnp.zeros_like(s_ref)

        s = _stack_heads(s_ref)
        _store_heads(st_ref, s)
        o, s_new = _gdn_scan_math(s, *(_stack_heads(r).astype(F32) for r in (u_ref, w_ref, qk_ref, qd_ref, kt_ref, cd_ref)))
        _store_heads(o_ref, o)
        _store_heads(s_ref, s_new)

    blk = pl.BlockSpec((c, wide), lambda i: (i, 0))
    return pl.pallas_call(
        body,
        grid=(t // c,),
        in_specs=[blk] * 6,
        out_specs=[blk, blk],
        out_shape=[jax.ShapeDtypeStruct((t, wide), F32)] * 2,
        scratch_shapes=[pltpu.VMEM((GDN_DK, wide), F32)],
        compiler_params=_cparams(("arbitrary",)),
        name=name,
    )(*prep)


def _gdn_scan_bwd(prep, states, do, *, name):
    t = do.shape[0]
    c = GDN_CHUNK
    wide = GDN_HEADS * LANES
    nc = t // c

    def body(u_ref, w_ref, qk_ref, qd_ref, kt_ref, cd_ref, st_ref, do_ref, *rest):
        outs, ds_ref = rest[:6], rest[6]

        @pl.when(pl.program_id(0) == 0)
        def _():
            ds_ref[...] = jnp.zeros_like(ds_ref)

        prim = tuple(_stack_heads(r).astype(F32) for r in (st_ref, u_ref, w_ref, qk_ref, qd_ref, kt_ref, cd_ref))
        _, pull = jax.vjp(_gdn_scan_math, *prim)
        grads = pull((_stack_heads(do_ref), _stack_heads(ds_ref)))
        _store_heads(ds_ref, grads[0])
        for o_ref, val in zip(outs, grads[1:]):
            _store_heads(o_ref, val)

    blk = pl.BlockSpec((c, wide), lambda i: (nc - 1 - i, 0))
    return pl.pallas_call(
        body,
        grid=(nc,),
        in_specs=[blk] * 8,
        out_specs=[blk] * 6,
        out_shape=[jax.ShapeDtypeStruct((t, wide), F32)] * 6,
        scratch_shapes=[pltpu.VMEM((GDN_DK, wide), F32)],
        compiler_params=_cparams(("arbitrary",)),
        name=name,
    )(*prep, states, do)


def _gdn_outgate_math(o, z, nw):
    r = lax.rsqrt(jnp.mean(o * o, axis=-1, keepdims=True) + RMS_EPS)
    return o * r * nw * _silu(z)


def _gdn_outgate_fwd(o, pm, nw_row, *, name):
    t = o.shape[0]
    tb = min(t, ROW_TILE)
    wide = GDN_HEADS * LANES
    z_at = GDN_QKV // wide

    def body(o_ref, z_ref, nw_ref, y_ref):
        for h in range(GDN_HEADS):
            y = _gdn_outgate_math(_head_tiles(o_ref, h), _head_tiles(z_ref, h), nw_ref[...])
            y_ref[:, h * LANES:(h + 1) * LANES] = y.astype(y_ref.dtype)

    return pl.pallas_call(
        body,
        grid=(t // tb,),
        in_specs=[pl.BlockSpec((tb, wide), lambda i: (i, 0)), pl.BlockSpec((tb, wide), lambda i: (i, z_at)),
                  pl.BlockSpec((1, LANES), lambda i: (0, 0))],
        out_specs=pl.BlockSpec((tb, wide), lambda i: (i, 0)),
        out_shape=jax.ShapeDtypeStruct((t, wide), MM_DTYPE),
        compiler_params=_cparams(("parallel",)),
        name=name,
    )(o, pm, nw_row)


def _gdn_outgate_bwd(o, pm, nw_row, dy, *, name):
    t = o.shape[0]
    tb = min(t, ROW_TILE)
    wide = GDN_HEADS * LANES
    z_at = GDN_QKV // wide

    def body(o_ref, z_ref, nw_ref, dy_ref, do_ref, dz_ref, dnw_ref):
        total = jnp.zeros((1, LANES), F32)
        for h in range(GDN_HEADS):
            _, pull = jax.vjp(_gdn_outgate_math, _head_tiles(o_ref, h), _head_tiles(z_ref, h), nw_ref[...])
            d_o, d_z, d_nw = pull(_head_tiles(dy_ref, h))
            do_ref[:, h * LANES:(h + 1) * LANES] = d_o
            dz_ref[:, h * LANES:(h + 1) * LANES] = d_z.astype(dz_ref.dtype)
            total = total + d_nw

        @pl.when(pl.program_id(0) == 0)
        def _():
            dnw_ref[...] = total

        @pl.when(pl.program_id(0) > 0)
        def _():
            dnw_ref[...] += total

    blk = pl.BlockSpec((tb, wide), lambda i: (i, 0))
    one = pl.BlockSpec((1, LANES), lambda i: (0, 0))
    return pl.pallas_call(
        body,
        grid=(t // tb,),
        in_specs=[blk, pl.BlockSpec((tb, wide), lambda i: (i, z_at)), one, blk],
        out_specs=[blk, blk, one],
        out_shape=[jax.ShapeDtypeStruct((t, wide), F32), jax.ShapeDtypeStruct((t, wide), MM_DTYPE),
                   jax.ShapeDtypeStruct((1, LANES), F32)],
        compiler_params=_cparams(("arbitrary",)),
        name=name,
    )(o, pm, nw_row, dy)


def _rms64(x, w_row):
    return x * lax.rsqrt(jnp.sum(x * x, axis=-1, keepdims=True) * (1.0 / DIL_DH) + RMS_EPS) * w_row


def _alibi_slopes(group):
    head = lax.broadcasted_iota(jnp.int32, (DIL_HEADS, 8, LANES), 0).astype(F32)
    rate = -math.log(2.0) * ALIBI_MAX_BIAS / (len(DIL_GROUPS) * DIL_HEADS)
    slope = jnp.exp(rate * (head + float(group * DIL_HEADS + 1)))
    return jnp.broadcast_to(slope[:, 0:1, :], (DIL_HEADS, DIL_SPAN, LANES))


def _band_logits(qn, kp, kc, slope_d, has_prev):
    qi = lax.broadcasted_iota(jnp.int32, (DIL_SPAN, DIL_SPAN), 0)
    kj = lax.broadcasted_iota(jnp.int32, (DIL_SPAN, DIL_SPAN), 1)
    steps_c = (qi - kj).astype(F32)
    scale = DIL_DH ** -0.5
    sp = _bdot_nt(qn, kp) * scale - slope_d * (steps_c + float(DIL_SPAN))
    sc = _bdot_nt(qn, kc) * scale - slope_d * steps_c
    sp = jnp.where((kj >= qi) & has_prev, sp, NEG)
    sc = jnp.where(kj <= qi, sc, NEG)
    return sp, sc


def _dil_attn_fwd(slab, wq_row, wk_row, *, group, name):
    dilation = DIL_GROUPS[group][1]
    t = slab.shape[0]
    rows = t // dilation
    nlb = rows // DIL_SPAN
    wide = DIL_HEADS * LANES
    view = slab.reshape(rows, dilation * DIL_SLAB)

    def body(q_ref, kc_ref, vc_ref, kp_ref, vp_ref, wq_ref, wk_ref, o_ref):
        has_prev = pl.program_id(1) > 0
        lane = lax.broadcasted_iota(jnp.int32, (DIL_SPAN, LANES), 1)
        qn = _rms64(_stack_heads(q_ref), wq_ref[...])
        kc = _rms64(_stack_heads(kc_ref), wk_ref[...])
        kp = _rms64(_stack_heads(kp_ref), wk_ref[...])
        sp, sc = _band_logits(qn, kp, kc, _alibi_slopes(group) * float(dilation), has_prev)
        m = jnp.maximum(jnp.max(sp, axis=-1, keepdims=True), jnp.max(sc, axis=-1, keepdims=True))
        pp = jnp.exp(sp - m)
        pc = jnp.exp(sc - m)
        l = jnp.sum(pp, axis=-1, keepdims=True) + jnp.sum(pc, axis=-1, keepdims=True)
        o = (_bdot(pp, _stack_heads(vp_ref)) + _bdot(pc, _stack_heads(vc_ref))) / l
        _store_heads(o_ref, jnp.where(lane < DIL_DH, o, m + jnp.log(l)))

    cur = lambda part: pl.BlockSpec((DIL_SPAN, wide), lambda r, i: (i, 3 * r + part))
    prv = lambda part: pl.BlockSpec((DIL_SPAN, wide), lambda r, i: (jnp.maximum(i - 1, 0), 3 * r + part))
    one = pl.BlockSpec((1, LANES), lambda r, i: (0, 0))
    out = pl.pallas_call(
        body,
        grid=(dilation, nlb),
        in_specs=[cur(0), cur(1), cur(2), prv(1), prv(2), one, one],
        out_specs=pl.BlockSpec((DIL_SPAN, wide), lambda r, i: (i, r)),
        out_shape=jax.ShapeDtypeStruct((rows, dilation * wide), F32),
        compiler_params=_cparams(("parallel", "parallel")),
        name=name,
    )(view, view, view, view, view, wq_row, wk_row)
    return out.reshape(t, wide)


def _head_slope(group, head):
    idx = jnp.zeros((8, LANES), F32) + head.astype(F32)
    rate = -math.log(2.0) * ALIBI_MAX_BIAS / (len(DIL_GROUPS) * DIL_HEADS)
    slope = jnp.exp(rate * (idx + float(group * DIL_HEADS + 1)))
    return jnp.broadcast_to(slope[0:1, :], (DIL_SPAN, LANES))


RESIDUE_BATCH = 8


def _take_residues(ref, d, first=0, count=None):
    count = d if count is None else count
    return jnp.stack([ref[pl.ds(first + r, DIL_SPAN, stride=d), :] for r in range(count)])


def _put_residues(ref, val, d, first=0):
    for r in range(val.shape[0]):
        ref[pl.ds(first + r, DIL_SPAN, stride=d), :] = val[r]


def _dil_attn_fwd_strided(slab, wq_row, wk_row, *, group, name):
    d = DIL_GROUPS[group][1]
    t = slab.shape[0]
    span = DIL_SPAN * d
    nsb = t // span

    hs = max(1, RESIDUE_BATCH // d)

    def body(*refs):
        q, kc, vc, kp, vp = (refs[i * hs:(i + 1) * hs] for i in range(5))
        wq_ref, wk_ref, o_ref, spread = refs[5 * hs:]
        has_prev = pl.program_id(0) > 0
        lane = lax.broadcasted_iota(jnp.int32, (DIL_SPAN, LANES), 1)
        nb = min(d, RESIDUE_BATCH)
        for r0 in range(0, d, nb):
            take = lambda group_refs: jnp.concatenate([_take_residues(ref, d, r0, nb) for ref in group_refs])
            slope = jnp.concatenate([jnp.broadcast_to(_head_slope(group, pl.program_id(1) * hs + j) * float(d),
                                                      (nb, DIL_SPAN, LANES)) for j in range(hs)])
            qn = _rms64(take(q), wq_ref[...])
            kcn = _rms64(take(kc), wk_ref[...])
            kpn = _rms64(take(kp), wk_ref[...])
            sp, sc = _band_logits(qn, kpn, kcn, slope, has_prev)
            m = jnp.maximum(jnp.max(sp, axis=-1, keepdims=True), jnp.max(sc, axis=-1, keepdims=True))
            pp = jnp.exp(sp - m)
            pc = jnp.exp(sc - m)
            l = jnp.sum(pp, axis=-1, keepdims=True) + jnp.sum(pc, axis=-1, keepdims=True)
            o = (_bdot(pp, take(vp)) + _bdot(pc, take(vc))) / l
            res = jnp.where(lane < DIL_DH, o, m + jnp.log(l))
            for j in range(hs):
                _put_residues(spread, res[j * nb:(j + 1) * nb], d, r0)
                if r0 + nb == d:
                    o_ref[:, j * LANES:(j + 1) * LANES] = spread[...]

    cur = lambda part, j: pl.BlockSpec((span, LANES), lambda i, h: (i, part * DIL_HEADS + h * hs + j))
    prv = lambda part, j: pl.BlockSpec((span, LANES), lambda i, h: (jnp.maximum(i - 1, 0), part * DIL_HEADS + h * hs + j))
    one = pl.BlockSpec((1, LANES), lambda i, h: (0, 0))
    heads = range(hs)
    in_specs = ([cur(0, j) for j in heads] + [cur(1, j) for j in heads] + [cur(2, j) for j in heads]
                + [prv(1, j) for j in heads] + [prv(2, j) for j in heads] + [one, one])
    return pl.pallas_call(
        body,
        grid=(nsb, DIL_HEADS // hs),
        in_specs=in_specs,
        out_specs=pl.BlockSpec((span, hs * LANES), lambda i, h: (i, h)),
        out_shape=jax.ShapeDtypeStruct((t, DIL_HEADS * LANES), F32),
        scratch_shapes=[pltpu.VMEM((span, LANES), F32)],
        compiler_params=_cparams(("parallel", "parallel")),
        name=name,
    )(*([slab] * (5 * hs)), wq_row, wk_row)


def _dil_attn_bwd_strided(slab, stat, wq_row, wk_row, dwq_in, dwk_in, *, group, name):
    d = DIL_GROUPS[group][1]
    t = slab.shape[0]
    span = DIL_SPAN * d
    nsb = t // span

    hs = max(1, RESIDUE_BATCH // d)

    def body(*refs):
        q_refs, kc_refs, vc_refs, kp_refs, vp_refs, st_refs = (refs[i * hs:(i + 1) * hs] for i in range(6))
        wq_ref, wk_ref, dwq_in_ref, dwk_in_ref, d_ref, dwq_ref, dwk_ref, dk_carry, dv_carry, spread = refs[6 * hs:]
        take = lambda group_refs: jnp.concatenate([_take_residues(ref, d) for ref in group_refs])
        step = pl.program_id(1)
        has_prev = step < nsb - 1
        first = (pl.program_id(0) == 0) & (step == 0)

        @pl.when(step == 0)
        def _():
            dk_carry[...] = jnp.zeros_like(dk_carry)
            dv_carry[...] = jnp.zeros_like(dv_carry)

        @pl.when(first)
        def _():
            dwq_ref[...] = dwq_in_ref[...]
            dwk_ref[...] = dwk_in_ref[...]

        lane = lax.broadcasted_iota(jnp.int32, (DIL_SPAN, LANES), 1)
        scale = DIL_DH ** -0.5
        q_raw = take(q_refs)
        kc_raw = take(kc_refs)
        vc = take(vc_refs)
        kp_raw = take(kp_refs)
        vp = take(vp_refs)
        st = take(st_refs)
        slope = jnp.concatenate([jnp.broadcast_to(_head_slope(group, pl.program_id(0) * hs + j) * float(d),
                                                  (d, DIL_SPAN, LANES)) for j in range(hs)])
        d_o = jnp.where(lane < DIL_DH, st, 0.0)
        lse = jnp.sum(jnp.where(lane == DIL_DH, st, 0.0), axis=-1, keepdims=True)
        delta = jnp.sum(jnp.where(lane == DIL_DH + 1, st, 0.0), axis=-1, keepdims=True)
        qn = _rms64(q_raw, wq_ref[...])
        kc = _rms64(kc_raw, wk_ref[...])
        kp = _rms64(kp_raw, wk_ref[...])
        sp, sc = _band_logits(qn, kp, kc, slope, has_prev)
        pp = jnp.exp(sp - lse)
        pc = jnp.exp(sc - lse)
        dsp = pp * (_bdot_nt(d_o, vp) - delta) * scale
        dsc = pc * (_bdot_nt(d_o, vc) - delta) * scale
        dqn = _bdot(dsp, kp) + _bdot(dsc, kc)
        dkc_n = _bdot_tn(dsc, qn) + dk_carry[...]
        dvc = _bdot_tn(pc, d_o) + dv_carry[...]
        dk_carry[...] = _bdot_tn(dsp, qn)
        dv_carry[...] = _bdot_tn(pp, d_o)
        dq_raw, dwq_rows = _rms64_bwd(q_raw, wq_ref[...], dqn)
        dk_raw, dwk_rows = _rms64_bwd(kc_raw, wk_ref[...], dkc_n)
        for part, val in enumerate((dq_raw, dk_raw, dvc)):
            for j in range(hs):
                _put_residues(spread, val[j * d:(j + 1) * d], d)
                d_ref[part, :, j * LANES:(j + 1) * LANES] = spread[...].astype(d_ref.dtype)
        dwq_ref[...] += jnp.sum(jnp.sum(dwq_rows, axis=0), axis=0, keepdims=True)
        dwk_ref[...] += jnp.sum(jnp.sum(dwk_rows, axis=0), axis=0, keepdims=True)

    at = lambda i: nsb - 1 - i
    cur = lambda part, j: pl.BlockSpec((span, LANES), lambda h, i: (at(i), part * DIL_HEADS + h * hs + j))
    prv = lambda part, j: pl.BlockSpec((span, LANES), lambda h, i: (jnp.maximum(at(i) - 1, 0), part * DIL_HEADS + h * hs + j))
    one = pl.BlockSpec((1, LANES), lambda h, i: (0, 0))
    heads = range(hs)
    in_specs = ([cur(0, j) for j in heads] + [cur(1, j) for j in heads] + [cur(2, j) for j in heads]
                + [prv(1, j) for j in heads] + [prv(2, j) for j in heads] + [cur(0, j) for j in heads] + [one] * 4)
    return pl.pallas_call(
        body,
        grid=(DIL_HEADS // hs, nsb),
        in_specs=in_specs,
        out_specs=[pl.BlockSpec((3, span, hs * LANES), lambda h, i: (0, at(i), h)), one, one],
        out_shape=[jax.ShapeDtypeStruct((3, t, DIL_HEADS * LANES), MM_DTYPE), jax.ShapeDtypeStruct((1, LANES), F32),
                   jax.ShapeDtypeStruct((1, LANES), F32)],
        scratch_shapes=[pltpu.VMEM((hs * d, DIL_SPAN, LANES), F32), pltpu.VMEM((hs * d, DIL_SPAN, LANES), F32),
                        pltpu.VMEM((span, LANES), F32)],
        compiler_params=_cparams(("arbitrary", "arbitrary")),
        name=name,
    )(*([slab] * (5 * hs)), *([stat] * hs), wq_row, wk_row, dwq_in, dwk_in)


def _dil_merge_fwd(oe, *, name):
    t = oe[0].shape[0]
    tb = min(t, ROW_TILE)
    wide = DIL_HEADS * LANES

    def body(e0, e1, e2, y_ref, om_ref):
        lane = lax.broadcasted_iota(jnp.int32, (tb, LANES), 1)
        for h in range(DIL_HEADS):
            es = [_head_tiles(e, h) for e in (e0, e1, e2)]
            lse = [jnp.sum(jnp.where(lane == DIL_DH, e, 0.0), axis=-1, keepdims=True) for e in es]
            top = jnp.maximum(jnp.maximum(lse[0], lse[1]), lse[2])
            joint = top + jnp.log(jnp.exp(lse[0] - top) + jnp.exp(lse[1] - top) + jnp.exp(lse[2] - top))
            o = sum(jnp.exp(l - joint) * e for l, e in zip(lse, es))
            y_ref[:, h * LANES:(h + 1) * LANES] = jnp.where(lane < DIL_DH, o, 0.0).astype(y_ref.dtype)
            om_ref[:, h * LANES:(h + 1) * LANES] = jnp.where(lane < DIL_DH, o, joint)

    blk = pl.BlockSpec((tb, wide), lambda i: (i, 0))
    return pl.pallas_call(
        body,
        grid=(t // tb,),
        in_specs=[blk] * 3,
        out_specs=[blk, blk],
        out_shape=[jax.ShapeDtypeStruct((t, wide), MM_DTYPE), jax.ShapeDtypeStruct((t, wide), F32)],
        compiler_params=_cparams(("parallel",)),
        name=name,
    )(*oe)


def _dil_merge_bwd(dy, om, *, name):
    t = dy.shape[0]
    tb = min(t, ROW_TILE)
    wide = DIL_HEADS * LANES

    def body(dy_ref, om_ref, st_ref):
        lane = lax.broadcasted_iota(jnp.int32, (tb, LANES), 1)
        for h in range(DIL_HEADS):
            d_o = jnp.where(lane < DIL_DH, _head_tiles(dy_ref, h), 0.0)
            om_t = _head_tiles(om_ref, h)
            delta = jnp.sum(d_o * om_t, axis=-1, keepdims=True)
            st_ref[:, h * LANES:(h + 1) * LANES] = jnp.where(
                lane < DIL_DH, d_o, jnp.where(lane == DIL_DH, om_t, jnp.where(lane == DIL_DH + 1, delta, 0.0)))

    blk = pl.BlockSpec((tb, wide), lambda i: (i, 0))
    return pl.pallas_call(
        body,
        grid=(t // tb,),
        in_specs=[blk, blk],
        out_specs=blk,
        out_shape=jax.ShapeDtypeStruct((t, wide), F32),
        compiler_params=_cparams(("parallel",)),
        name=name,
    )(dy, om)


def _rms64_bwd(x, w_row, dy):
    r = lax.rsqrt(jnp.sum(x * x, axis=-1, keepdims=True) * (1.0 / DIL_DH) + RMS_EPS)
    gw = dy * w_row
    dx = r * gw - x * (r * r * r * jnp.sum(gw * x, axis=-1, keepdims=True) * (1.0 / DIL_DH))
    return dx, dy * x * r


def _dil_attn_bwd(slab, stat, wq_row, wk_row, dwq_in, dwk_in, *, group, name):
    dilation = DIL_GROUPS[group][1]
    t = slab.shape[0]
    rows = t // dilation
    nlb = rows // DIL_SPAN
    wide = DIL_HEADS * LANES
    view = slab.reshape(rows, dilation * DIL_SLAB)
    stat_view = stat.reshape(rows, dilation * wide)

    def body(cur_ref, kp_ref, vp_ref, st_ref, wq_ref, wk_ref, dwq_in_ref, dwk_in_ref, d_ref, dwq_ref, dwk_ref,
             dk_carry, dv_carry):
        step = pl.program_id(1)
        has_prev = step < nlb - 1
        first = (pl.program_id(0) == 0) & (step == 0)

        @pl.when(step == 0)
        def _():
            dk_carry[...] = jnp.zeros_like(dk_carry)
            dv_carry[...] = jnp.zeros_like(dv_carry)

        @pl.when(first)
        def _():
            dwq_ref[...] = dwq_in_ref[...]
            dwk_ref[...] = dwk_in_ref[...]

        lane = lax.broadcasted_iota(jnp.int32, (DIL_SPAN, LANES), 1)
        scale = DIL_DH ** -0.5
        q_raw = _stack_heads(cur_ref)
        kc_raw = _stack_heads(cur_ref, first=DIL_HEADS)
        vc = _stack_heads(cur_ref, first=2 * DIL_HEADS)
        kp_raw = _stack_heads(kp_ref)
        vp = _stack_heads(vp_ref)
        st = _stack_heads(st_ref)
        d_o = jnp.where(lane < DIL_DH, st, 0.0)
        lse = jnp.sum(jnp.where(lane == DIL_DH, st, 0.0), axis=-1, keepdims=True)
        delta = jnp.sum(jnp.where(lane == DIL_DH + 1, st, 0.0), axis=-1, keepdims=True)
        qn = _rms64(q_raw, wq_ref[...])
        kc = _rms64(kc_raw, wk_ref[...])
        kp = _rms64(kp_raw, wk_ref[...])
        sp, sc = _band_logits(qn, kp, kc, _alibi_slopes(group) * float(dilation), has_prev)
        pp = jnp.exp(sp - lse)
        pc = jnp.exp(sc - lse)
        dsp = pp * (_bdot_nt(d_o, vp) - delta) * scale
        dsc = pc * (_bdot_nt(d_o, vc) - delta) * scale
        dqn = _bdot(dsp, kp) + _bdot(dsc, kc)
        dkc_n = _bdot_tn(dsc, qn) + _stack_heads(dk_carry)
        dvc = _bdot_tn(pc, d_o) + _stack_heads(dv_carry)
        _store_heads(dk_carry, _bdot_tn(dsp, qn))
        _store_heads(dv_carry, _bdot_tn(pp, d_o))
        dq_raw, dwq_rows = _rms64_bwd(q_raw, wq_ref[...], dqn)
        dk_raw, dwk_rows = _rms64_bwd(kc_raw, wk_ref[...], dkc_n)
        _store_heads(d_ref, dq_raw)
        _store_heads(d_ref, dk_raw, first=DIL_HEADS)
        _store_heads(d_ref, dvc, first=2 * DIL_HEADS)
        dwq_ref[...] += jnp.sum(jnp.sum(dwq_rows, axis=0), axis=0, keepdims=True)
        dwk_ref[...] += jnp.sum(jnp.sum(dwk_rows, axis=0), axis=0, keepdims=True)

    blk_i = lambda i: nlb - 1 - i
    cur = pl.BlockSpec((DIL_SPAN, DIL_SLAB), lambda r, i: (blk_i(i), r))
    prv = lambda part: pl.BlockSpec((DIL_SPAN, wide), lambda r, i: (jnp.maximum(blk_i(i) - 1, 0), 3 * r + part))
    one = pl.BlockSpec((1, LANES), lambda r, i: (0, 0))
    dslab, dwq, dwk = pl.pallas_call(
        body,
        grid=(dilation, nlb),
        in_specs=[cur, prv(1), prv(2), pl.BlockSpec((DIL_SPAN, wide), lambda r, i: (blk_i(i), r)), one, one, one, one],
        out_specs=[cur, one, one],
        out_shape=[jax.ShapeDtypeStruct((rows, dilation * DIL_SLAB), MM_DTYPE), jax.ShapeDtypeStruct((1, LANES), F32),
                   jax.ShapeDtypeStruct((1, LANES), F32)],
        scratch_shapes=[pltpu.VMEM((DIL_SPAN, wide), F32), pltpu.VMEM((DIL_SPAN, wide), F32)],
        compiler_params=_cparams(("arbitrary", "arbitrary")),
        name=name,
    )(view, view, view, stat_view, wq_row, wk_row, dwq_in, dwk_in)
    return dslab.reshape(t, DIL_SLAB), dwq, dwk


def _row(v, width=LANES):
    v = v.astype(F32).reshape(-1)
    return jnp.pad(v, (0, width - v.shape[0])).reshape(1, width)


def _prepare_weights(w):
    return dict(gdn=_prepare_gdn(w), dil=_prepare_dil(w), ffn=_prepare_ffn(w))


def _prepare_gdn(w, layers=range(DEPTH // 2)):
    gdn = {}
    for j in layers:
        wt = w["gdn_w_in"][j]
        gates_t = jnp.pad(wt[GDN_MAIN:], ((0, LANES - 2 * GDN_HEADS), (0, 0)))
        gdn[j] = dict(in_t=wt, gates_t=gates_t, out=w["gdn_w_out"][j], conv=w["gdn_conv_w"][j].astype(F32),
                      alog=_row(w["gdn_a_log"][j]), dt=_row(w["gdn_dt_bias"][j]), nw=_row(w["gdn_norm_w"][j]))
    return gdn


def _prepare_dil(w, layers=range(DEPTH // 2)):
    d = D_MODEL
    dil = {}
    for j in layers:
        wt = w["dil_w_in"][j].reshape(3, len(DIL_GROUPS), DIL_HEADS, DIL_DH, d)
        wg_t = [wt[:, g].reshape(DIL_SLAB // 2, d) for g in range(len(DIL_GROUPS))]
        out_t = jnp.pad(w["dil_w_out"][j].reshape(d, DIL_HEADS, DIL_DH), ((0, 0), (0, 0), (0, LANES - DIL_DH)))
        dil[j] = dict(wg_t=wg_t, out_t=out_t.reshape(d, DIL_HEADS * LANES), wq=_row(w["dil_q_norm"][j]),
                      wk=_row(w["dil_k_norm"][j]))
    return dil


def _prepare_ffn(w, layers=range(DEPTH)):
    return {i: dict(in_t=w["ffn_w_in"][i], out=w["ffn_w_out"][i]) for i in layers}


def _gdn_layer_fwd(x, nrow, p):
    hn = _rmsnorm_fwd(x, nrow, name="rmsnorm_fwd")
    pm = _matmul(hn, p["in_t"], trans_b=True, b_rows=(0, GDN_MAIN), name="gdn_proj_main")
    ab = _matmul(hn, p["gates_t"], trans_b=True, name="gdn_proj_gates")
    qkv = _gdn_conv_fwd(pm, p["conv"], name="gdn_conv_fwd")
    gb, bb = _gdn_gates_fwd(ab, p["alog"], p["dt"], name="gdn_gates_fwd")
    prep, rest = _gdn_prep_fwd(qkv, gb, bb, name="gdn_prep_fwd")
    o, states = _gdn_scan_fwd(prep, name="gdn_scan_fwd")
    og = _gdn_outgate_fwd(o, pm, p["nw"], name="gdn_outgate_fwd")
    y = _matmul(og, p["out"], add=x, name="gdn_proj_out")
    return y, (x, hn, pm, ab, qkv, gb, bb, prep, rest, states, o, og)


def _gdn_layer_bwd(dx, dxb, nrow, p, saved):
    x, hn, pm, ab, qkv, gb, bb, prep, rest, states, o, og = saved
    d_og = _matmul(dxb, p["out"], trans_b=True, name="gdn_dgate")
    g_out = _matmul(og, dxb, trans_a=True, out_dtype=MM_DTYPE, name="gdn_gw_out")
    d_o, d_z, d_nw = _gdn_outgate_bwd(o, pm, p["nw"], d_og, name="gdn_outgate_bwd")
    cts = _gdn_scan_bwd(prep, states, d_o, name="gdn_scan_bwd")
    dqkv, dgb, dbb = _gdn_prep_bwd(qkv, gb, bb, rest, cts, name="gdn_prep_bwd")
    d_ab, d_alog, d_dt = _gdn_gates_bwd(ab, p["alog"], p["dt"], dgb, dbb, name="gdn_gates_bwd")
    d_conv, g_conv = _gdn_conv_bwd(pm, p["conv"], dqkv, name="gdn_conv_bwd")
    d_hn = _matmul(d_conv, p["in_t"], b_rows=(0, GDN_QKV), name="gdn_dhn_qkv")
    d_hn = _matmul(d_z, p["in_t"], b_rows=(GDN_QKV, GDN_MAIN - GDN_QKV), add=d_hn, name="gdn_dhn_z")
    d_hn = _matmul(d_ab, p["gates_t"], add=d_hn, name="gdn_dhn_gates")
    g_in_t = jnp.concatenate([
        _matmul(d_conv, hn, trans_a=True, out_dtype=MM_DTYPE, name="gdn_gw_qkv"),
        _matmul(d_z, hn, trans_a=True, out_dtype=MM_DTYPE, name="gdn_gw_z"),
        _matmul(d_ab, hn, trans_a=True, out_dtype=MM_DTYPE, name="gdn_gw_gates")[:2 * GDN_HEADS],
    ], axis=0)
    dx_new, dxb_new, g_norm = _rmsnorm_bwd(x, nrow, d_hn, dx, name="rmsnorm_bwd")
    grads = dict(w_in=g_in_t, conv=g_conv, a_log=d_alog[0, :GDN_HEADS], dt_bias=d_dt[0, :GDN_HEADS], norm_w=d_nw[0],
                 w_out=g_out, norm=g_norm[0])
    return dx_new, dxb_new, grads


def _dil_layer_fwd(x, nrow, p):
    hn = _rmsnorm_fwd(x, nrow, name="rmsnorm_fwd")
    slabs = [_matmul(hn, p["wg_t"][g], trans_b=True, spread_out=True, name="dil_proj_in") for g in range(len(DIL_GROUPS))]
    oe = [(_dil_attn_fwd if DIL_GROUPS[g][1] == 1 else _dil_attn_fwd_strided)(
        slabs[g], p["wq"], p["wk"], group=g, name=f"dil_attn_fwd_g{g}") for g in range(len(DIL_GROUPS))]
    y, om = _dil_merge_fwd(oe, name="dil_merge_fwd")
    out = _matmul(y, p["out_t"], trans_b=True, add=x, name="dil_proj_out")
    return out, (x, hn, slabs, y, om)


def _dil_layer_bwd(dx, dxb, nrow, p, saved):
    x, hn, slabs, y, om = saved
    d_y = _matmul(dxb, p["out_t"], name="dil_dmerged")
    g_out_t = _matmul(dxb, y, trans_a=True, out_dtype=MM_DTYPE, name="dil_gw_out")
    g_out_t = g_out_t.reshape(D_MODEL, DIL_HEADS, LANES)[..., :DIL_DH].reshape(D_MODEL, DIL_HEADS * DIL_DH)
    stat = _dil_merge_bwd(d_y, om, name="dil_merge_bwd")
    d_hn = None
    dwq = jnp.zeros((1, LANES), F32)
    dwk = jnp.zeros((1, LANES), F32)
    g_groups = []
    wide = DIL_HEADS * LANES
    for g in range(len(DIL_GROUPS)):
        if DIL_GROUPS[g][1] == 1:
            dslab, dwq, dwk = _dil_attn_bwd(slabs[g], stat, p["wq"], p["wk"], dwq, dwk, group=g, name=f"dil_attn_bwd_g{g}")
            d_hn = _matmul(dslab, p["wg_t"][g], packed_a=True, add=d_hn, name="dil_dhn")
            g_w = _matmul(dslab, hn, trans_a=True, packed_a=True, out_dtype=MM_DTYPE, name="dil_gw_in")
        else:
            dparts, dwq, dwk = _dil_attn_bwd_strided(slabs[g], stat, p["wq"], p["wk"], dwq, dwk, group=g,
                                                     name=f"dil_attn_bwd_g{g}")
            d_hn = _matmul(dparts, p["wg_t"][g], a_lead="k", packed_a=True, add=d_hn, name="dil_dhn_parts")
            g_w = _matmul(dparts, hn, trans_a=True, a_lead="i", packed_a=True, out_dtype=MM_DTYPE, name="dil_gw_in_parts")
        g_groups.append(g_w.reshape(3, DIL_HEADS, DIL_DH, D_MODEL))
    g_in_t = jnp.stack(g_groups, axis=1).reshape(3 * len(DIL_GROUPS) * DIL_HEADS * DIL_DH, D_MODEL)
    dx_new, dxb_new, g_norm = _rmsnorm_bwd(x, nrow, d_hn, dx, name="rmsnorm_bwd")
    grads = dict(w_in=g_in_t, q_norm=dwq[0, :DIL_DH], k_norm=dwk[0, :DIL_DH], w_out=g_out_t, norm=g_norm[0])
    return dx_new, dxb_new, grads


def _ffn_layer_fwd(x, nrow, p):
    hn = _rmsnorm_fwd(x, nrow, name="rmsnorm_fwd")
    gate, up, act = _ffn_in(hn, p["in_t"], name="ffn_proj_in")
    y = _matmul(act, p["out"], add=x, name="ffn_proj_out")
    return y, (x, hn, gate, up, act)


def _ffn_layer_bwd(dx, dxb, nrow, p, saved):
    x, hn, gate, up, act = saved
    g_out = _matmul(act, dxb, trans_a=True, out_dtype=MM_DTYPE, name="ffn_gw_out")
    d_gu = _ffn_dact(dxb, p["out"], gate, up, name="ffn_dact")
    d_hn = _matmul(d_gu, p["in_t"], a_lead="k", name="ffn_dhn")
    g_in_t = _matmul(d_gu, hn, trans_a=True, a_lead="i", out_dtype=MM_DTYPE, name="ffn_gw_in")
    dx_new, dxb_new, g_norm = _rmsnorm_bwd(x, nrow, d_hn, dx, name="rmsnorm_bwd")
    return dx_new, dxb_new, dict(w_in=g_in_t, w_out=g_out, norm=g_norm[0])


def _mixer_fwd(i, x, mix_row, prepared):
    if i % 2 == 0:
        return _gdn_layer_fwd(x, mix_row, prepared["gdn"][i // 2])
    return _dil_layer_fwd(x, mix_row, prepared["dil"][i // 2])


def _mixer_bwd(i, dx, dxb, mix_row, prepared, saved, zero=0.0):
    if i % 2 == 0:
        p = prepared["gdn"][i // 2]
        return _gdn_layer_bwd(dx, dxb, mix_row, dict(p, nw=p["nw"] + zero), saved)
    p = prepared["dil"][i // 2]
    return _dil_layer_bwd(dx, dxb, mix_row, dict(p, wq=p["wq"] + zero), saved)


def _local_step(x, target, prepared, norm_mix, norm_ffn):
    saved = []
    for i in range(DEPTH):
        x, s_mix = _mixer_fwd(i, x, norm_mix[i].reshape(1, D_MODEL), prepared)
        x, s_ffn = _ffn_layer_fwd(x, norm_ffn[i].reshape(1, D_MODEL), prepared["ffn"][i])
        saved.append((s_mix, s_ffn))
    dx, dxb, loss = _loss_head(x, target, name="loss_head")
    g_mix, g_ffn = [None] * DEPTH, [None] * DEPTH
    for i in reversed(range(DEPTH)):
        s_mix, s_ffn = saved[i]
        dx, dxb, g_ffn[i] = _ffn_layer_bwd(dx, dxb, norm_ffn[i].reshape(1, D_MODEL), prepared["ffn"][i], s_ffn)
        dx, dxb, g_mix[i] = _mixer_bwd(i, dx, dxb, norm_mix[i].reshape(1, D_MODEL), prepared, s_mix)
    return loss[0, 0], dx, _collect_grads(g_mix, g_ffn)


def _collect_grads(g_mix, g_ffn):
    gdn = [g_mix[i] for i in range(0, DEPTH, 2)]
    dil = [g_mix[i] for i in range(1, DEPTH, 2)]
    if any(g is None for g in g_mix + g_ffn):
        pick = lambda gs, key: [None if g is None else g[key] for g in gs]
        return dict(gdn_w_in=pick(gdn, "w_in"), gdn_w_out=pick(gdn, "w_out"), dil_w_in=pick(dil, "w_in"),
                    dil_w_out=pick(dil, "w_out"), ffn_w_in=pick(g_ffn, "w_in"), ffn_w_out=pick(g_ffn, "w_out"))
    grads = dict(
        norm_mix=jnp.stack([g["norm"] for g in g_mix]),
        norm_ffn=jnp.stack([g["norm"] for g in g_ffn]),
        gdn_w_in=[g["w_in"] for g in gdn],
        gdn_conv_w=jnp.stack([g["conv"] for g in gdn]),
        gdn_a_log=jnp.stack([g["a_log"] for g in gdn]),
        gdn_dt_bias=jnp.stack([g["dt_bias"] for g in gdn]),
        gdn_norm_w=jnp.stack([g["norm_w"] for g in gdn]),
        gdn_w_out=[g["w_out"] for g in gdn],
        dil_w_in=[g["w_in"] for g in dil],
        dil_q_norm=jnp.stack([g["q_norm"] for g in dil]),
        dil_k_norm=jnp.stack([g["k_norm"] for g in dil]),
        dil_w_out=[g["w_out"] for g in dil],
        ffn_w_in=[g["w_in"] for g in g_ffn],
        ffn_w_out=[g["w_out"] for g in g_ffn],
    )
    return grads


MESH_ID = pl.DeviceIdType.MESH
ANY_SPACE = pl.BlockSpec(memory_space=pl.ANY)


def _mesh_position():
    return lax.axis_index("x"), lax.axis_index("y"), lax.axis_index("c")


def _flip(pos, k):
    x, y, c = pos
    return (1 - x if k & 4 else x, 1 - y if k & 2 else y, 1 - c if k & 1 else c)


def _linear(pos):
    return 4 * pos[0] + 2 * pos[1] + pos[2]


def _comm_scratch():
    return [pltpu.SemaphoreType.DMA((N_DEV - 1,)), pltpu.SemaphoreType.DMA((N_DEV - 1,)), pltpu.SemaphoreType.DMA(())]


def _all_gather(shard, *, name):
    def body(x_ref, out_ref, send_sems, recv_sems, local_sem):
        me = _mesh_position()
        mine = out_ref.at[_linear(me)]
        local = pltpu.make_async_copy(x_ref, mine, local_sem)
        local.start()
        copies = []
        for k in range(1, N_DEV):
            cp = pltpu.make_async_remote_copy(src_ref=x_ref, dst_ref=mine, send_sem=send_sems.at[k - 1],
                                              recv_sem=recv_sems.at[k - 1], device_id=_flip(me, k), device_id_type=MESH_ID)
            cp.start()
            copies.append(cp)
        for cp in copies:
            cp.wait()
        local.wait()

    return pl.pallas_call(
        body,
        out_shape=jax.ShapeDtypeStruct((N_DEV,) + shard.shape, shard.dtype),
        in_specs=[ANY_SPACE],
        out_specs=ANY_SPACE,
        scratch_shapes=_comm_scratch(),
        name=name,
    )(shard)


def _exchange(parts, *, name):
    def body(p_ref, out_ref, send_sems, recv_sems, local_sem):
        me = _mesh_position()
        mine = out_ref.at[_linear(me)]
        local = pltpu.make_async_copy(p_ref.at[_linear(me)], mine, local_sem)
        local.start()
        copies = []
        for k in range(1, N_DEV):
            peer = _flip(me, k)
            cp = pltpu.make_async_remote_copy(src_ref=p_ref.at[_linear(peer)], dst_ref=mine, send_sem=send_sems.at[k - 1],
                                              recv_sem=recv_sems.at[k - 1], device_id=peer, device_id_type=MESH_ID)
            cp.start()
            copies.append(cp)
        for cp in copies:
            cp.wait()
        local.wait()

    return pl.pallas_call(
        body,
        out_shape=jax.ShapeDtypeStruct(parts.shape, parts.dtype),
        in_specs=[ANY_SPACE],
        out_specs=ANY_SPACE,
        scratch_shapes=_comm_scratch(),
        name=name,
    )(parts)


HBM_SPACE = pl.BlockSpec(memory_space=pltpu.HBM)
SEM_SPACE = pl.BlockSpec(memory_space=pltpu.SEMAPHORE)
DATAFLOW = pltpu.SideEffectType.DATAFLOW_SIDE_EFFECTING


def _split_copies(src_ref, land_ref, send_sems, recv_sems, per_peer):
    me = _mesh_position()
    mine = land_ref.at[_linear(me)]
    copies = []
    for k in range(1, N_DEV):
        peer = _flip(me, k)
        src = src_ref.at[_linear(peer)] if per_peer else src_ref
        copies.append(pltpu.make_async_remote_copy(src_ref=src, dst_ref=mine, send_sem=send_sems.at[k - 1],
                                                   recv_sem=recv_sems.at[k - 1], device_id=peer, device_id_type=MESH_ID))
    return copies


def _travel_start(src, after, *, per_peer, name):
    me = _linear(_mesh_position())
    own = src[me] if per_peer else src
    shape = own.shape
    landing = lax.dynamic_update_slice(lax.empty((N_DEV,) + shape, src.dtype), own[None], (me, 0, 0))

    def body(src_ref, land_ref, after_ref, send_sems, recv_sems, src_thru, land_thru, token):
        for cp in _split_copies(src_ref, land_ref, send_sems, recv_sems, per_peer):
            cp.start()
        token[...] = jnp.zeros_like(token)

    return pl.pallas_call(
        body,
        name=name,
        out_shape=(pltpu.SemaphoreType.DMA((N_DEV - 1,)), pltpu.SemaphoreType.DMA((N_DEV - 1,)),
                   pltpu.HBM(src.shape, src.dtype), pltpu.HBM(landing.shape, landing.dtype),
                   jax.ShapeDtypeStruct((8, LANES), F32)),
        in_specs=(HBM_SPACE, HBM_SPACE, ANY_SPACE),
        out_specs=(SEM_SPACE, SEM_SPACE, HBM_SPACE, HBM_SPACE, pl.BlockSpec(memory_space=pltpu.VMEM)),
        input_output_aliases={0: 2, 1: 3},
        compiler_params=pltpu.CompilerParams(has_side_effects=DATAFLOW),
    )(pltpu.with_memory_space_constraint(src, pltpu.HBM), pltpu.with_memory_space_constraint(landing, pltpu.HBM), after)


def _travel_wait(started, after, *, per_peer, name):
    send_sems, recv_sems, src_thru, land_thru, _ = started

    def body(src_ref, land_ref, send_sems, recv_sems, after_ref, src_dead, got_ref):
        for cp in _split_copies(src_ref, land_ref, send_sems, recv_sems, per_peer):
            cp.wait_send()
            cp.wait_recv()

    return pl.pallas_call(
        body,
        name=name,
        out_shape=(pltpu.HBM(src_thru.shape, src_thru.dtype), pltpu.HBM(land_thru.shape, land_thru.dtype)),
        in_specs=(HBM_SPACE, HBM_SPACE, SEM_SPACE, SEM_SPACE, ANY_SPACE),
        out_specs=(HBM_SPACE, HBM_SPACE),
        input_output_aliases={0: 0, 1: 1},
        compiler_params=pltpu.CompilerParams(has_side_effects=DATAFLOW),
    )(src_thru, land_thru, send_sems, recv_sems, after)[1]


def _adamw(parts, w, m, v, *, name):
    rows, n = w.shape
    tb = _pick(rows, (PACK_ROW_ALIGN, 16))
    c1 = 1.0 - ADAM_B1 ** ADAM_STEP
    c2 = 1.0 - ADAM_B2 ** ADAM_STEP

    def body(p_ref, w_ref, m_ref, v_ref, g_ref, d_ref, nm_ref, nv_ref):
        g = p_ref[0].astype(F32)
        for s in range(1, N_DEV):
            g = g + p_ref[s].astype(F32)
        m_new = ADAM_B1 * m_ref[...] + (1.0 - ADAM_B1) * g
        v_new = ADAM_B2 * v_ref[...] + (1.0 - ADAM_B2) * (g * g)
        m_hat = m_new / c1
        v_hat = v_new / c2
        g_ref[...] = g
        nm_ref[...] = m_new
        nv_ref[...] = v_new
        d_ref[...] = -ADAM_LR * (m_hat / (jnp.sqrt(v_hat) + ADAM_EPS) + ADAM_WD * w_ref[...])

    blk = pl.BlockSpec((tb, n), lambda i: (i, 0))
    return pl.pallas_call(
        body,
        grid=(rows // tb,),
        in_specs=[pl.BlockSpec((N_DEV, tb, n), lambda i: (0, i, 0)), blk, blk, blk],
        out_specs=[blk] * 4,
        out_shape=[jax.ShapeDtypeStruct((rows, n), F32)] * 4,
        compiler_params=_cparams(("parallel",)),
        name=name,
    )(parts, w, m, v)


PACK_WIDTH = 1024
SHARDED = {
    "gdn_w_in": ((2, D_MODEL, GDN_IN_WIDTH), 2),
    "gdn_conv_w": ((2, GDN_CONV, GDN_QKV), 2),
    "gdn_w_out": ((2, GDN_HEADS * GDN_DV, D_MODEL), 1),
    "dil_w_in": ((2, D_MODEL, 3 * len(DIL_GROUPS) * DIL_HEADS * DIL_DH), 2),
    "dil_w_out": ((2, DIL_HEADS * DIL_DH, D_MODEL), 2),
    "ffn_w_in": ((DEPTH, D_MODEL, 2 * FFN_HIDDEN), 2),
    "ffn_w_out": ((DEPTH, FFN_HIDDEN, D_MODEL), 1),
}
REPLICATED = {"norm_mix": (DEPTH, D_MODEL), "norm_ffn": (DEPTH, D_MODEL), "gdn_a_log": (2, GDN_HEADS),
              "gdn_dt_bias": (2, GDN_HEADS), "gdn_norm_w": (2, GDN_DV), "dil_q_norm": (2, DIL_DH), "dil_k_norm": (2, DIL_DH)}
WEIGHT_ORDER = ("norm_mix", "norm_ffn", "gdn_w_in", "gdn_conv_w", "gdn_a_log", "gdn_dt_bias", "gdn_norm_w", "gdn_w_out",
                "dil_w_in", "dil_q_norm", "dil_k_norm", "dil_w_out", "ffn_w_in", "ffn_w_out")
PACK_ROW_ALIGN = 128
PIECE_ALIGN = 16
SMALL_ROWS = 16


def _shard_shape(name):
    shape, axis = SHARDED[name]
    return tuple(s // N_DEV if i == axis else s for i, s in enumerate(shape))


def _shard_rows(name):
    return math.prod(_shard_shape(name)) // PACK_WIDTH


def _split_shards(full, name):
    shape, axis = SHARDED[name]
    split = full.reshape(shape[:axis] + (N_DEV, shape[axis] // N_DEV) + shape[axis + 1:])
    return jnp.moveaxis(split, axis, 0)


def _join_shards(stacked, name):
    shape, axis = SHARDED[name]
    return jnp.moveaxis(stacked, 0, axis).reshape(shape)


COLUMN_SHARDED = ("gdn_w_in", "dil_w_in", "dil_w_out", "ffn_w_in")


def _to_rows(shard, name):
    if name in COLUMN_SHARDED:
        shard = jnp.swapaxes(shard, 1, 2)
    return shard.reshape(-1, PACK_WIDTH)


def _layer_columns(name):
    _, r, c = _shard_shape(name)
    return r if name in COLUMN_SHARDED else c


def _piece_rows(piece, halves=1):
    name, layer = piece
    rows = _shard_rows(name) * halves
    return rows if layer is None else rows // SHARDED[name][0][0]


def _aligned(rows, to=PIECE_ALIGN):
    return -(-rows // to) * to


def _pack_pieces(arrays, total_align=PIECE_ALIGN):
    padded, total = [], 0
    for a in arrays:
        rows = a.shape[-2]
        extra = _aligned(rows) - rows
        if extra:
            a = jnp.pad(a, [(0, 0)] * (a.ndim - 2) + [(0, extra), (0, 0)])
        padded.append(a)
        total += rows + extra
    tail = _aligned(total, total_align) - total
    if tail:
        padded.append(jnp.zeros(padded[0].shape[:-2] + (tail, PACK_WIDTH), padded[0].dtype))
    return jnp.concatenate(padded, axis=-2)


def _piece_offsets(pieces, halves=None):
    out, at = [], 0
    for p in pieces:
        rows = _piece_rows(p, (halves or {}).get(p[0], 1))
        out.append((p, at, rows))
        at += _aligned(rows)
    return out


def _shard_piece_rows(src, piece):
    name, layer = piece
    part = src[name] if layer is None else src[name][layer:layer + 1]
    return _to_rows(part.astype(F32), name)


def _piece_from_rows(rows, piece):
    name, layer = piece
    layers, r, c = _shard_shape(name)
    n_l = layers if layer is None else 1
    if name in COLUMN_SHARDED:
        return jnp.swapaxes(rows.reshape(n_l, c, r), 1, 2)
    return rows.reshape(n_l, r, c)


SMALL_TAIL = tuple(n for n in REPLICATED if n not in ("norm_mix", "norm_ffn"))


def _pack_small(vals):
    tail, at = jnp.zeros((PACK_WIDTH,), F32), 0
    for n in SMALL_TAIL:
        vec = vals[n].astype(F32).reshape(-1)
        tail = tail + jnp.pad(vec, (at, PACK_WIDTH - at - vec.shape[0]))
        at += vec.shape[0]
    buf = jnp.pad(vals["norm_mix"].astype(F32), ((0, SMALL_ROWS - DEPTH), (0, 0)))
    buf = buf + jnp.pad(vals["norm_ffn"].astype(F32), ((8, SMALL_ROWS - 8 - DEPTH), (0, 0)))
    return buf + jnp.pad(tail.reshape(1, PACK_WIDTH), ((SMALL_ROWS - 1, 0), (0, 0)))


def _unpack_small(buf):
    out = {"norm_mix": buf[0:DEPTH], "norm_ffn": buf[8:8 + DEPTH]}
    at = 0
    for n in SMALL_TAIL:
        size = math.prod(REPLICATED[n])
        out[n] = buf[SMALL_ROWS - 1, at:at + size].reshape(REPLICATED[n])
        at += size
    return out


GATHER_FIRST = (("gdn_w_in", 0), ("gdn_conv_w", None), ("gdn_w_out", 0))
GATHER_NEXT = (("ffn_w_in", 0), ("ffn_w_out", 0), ("dil_w_in", 0), ("dil_w_out", 0))
GATHER_LAST = (("ffn_w_in", 1), ("ffn_w_out", 1), ("gdn_w_in", 1), ("gdn_w_out", 1), ("ffn_w_in", 2), ("ffn_w_out", 2),
               ("dil_w_in", 1), ("dil_w_out", 1), ("ffn_w_in", 3), ("ffn_w_out", 3))
EXCHANGE_GROUPS = (
    (("ffn_w_in", 3), ("ffn_w_out", 3), ("dil_w_in", 1), ("dil_w_out", 1),
     ("ffn_w_in", 2), ("ffn_w_out", 2), ("gdn_w_in", 1), ("gdn_w_out", 1)),
    (("ffn_w_in", 1), ("ffn_w_out", 1), ("dil_w_in", 0), ("dil_w_out", 0)),
    (("ffn_w_in", 0), ("ffn_w_out", 0)),
    (("gdn_w_in", 0), ("gdn_w_out", 0), ("gdn_conv_w", None)),
)
EXCHANGE_AFTER = {("mix", 2): 0, ("mix", 1): 1, ("ffn", 0): 2}


def _gather_operand(w, pieces):
    arrays = []
    for n, layer in pieces:
        if layer is None:
            arrays.append(lax.bitcast_convert_type(w[n], BF16).reshape(-1, PACK_WIDTH))
        else:
            arrays.append(_to_rows(w[n][layer:layer + 1].astype(BF16), n))
    return _pack_pieces(arrays)


def _gathered_weights(gathered, pieces, full):
    for (n, layer), at, rows in _piece_offsets(pieces, halves={"gdn_conv_w": 2}):
        block = gathered[:, at:at + rows]
        if layer is None:
            block = lax.bitcast_convert_type(block.reshape((N_DEV,) + _shard_shape(n) + (2,)), F32)
            full[n] = _join_shards(block, n)
        else:
            full.setdefault(n, {})[layer] = block.reshape(-1, _layer_columns(n))
    return full


def _exchange_operand(grads, pieces):
    arrays = []
    for n, layer in pieces:
        if layer is None:
            arrays.append(_split_shards(grads[n], n).astype(BF16).reshape(N_DEV, -1, PACK_WIDTH))
        else:
            arrays.append(grads[n][layer].astype(BF16).reshape(N_DEV, -1, PACK_WIDTH))
    return _pack_pieces(arrays, total_align=PACK_ROW_ALIGN)


def _update_group(received, pieces, w, m, v, *, name):
    packed = [_pack_pieces([_shard_piece_rows(src, p) for p in pieces], total_align=PACK_ROW_ALIGN) for src in (w, m, v)]
    outs = _adamw(received, *packed, name=name)
    return {p: tuple(_piece_from_rows(o[at:at + rows], p) for o in outs) for p, at, rows in _piece_offsets(pieces)}


def kernel(x, norm_mix, norm_ffn, gdn_w_in, gdn_conv_w, gdn_a_log, gdn_dt_bias, gdn_norm_w, gdn_w_out, dil_w_in, dil_q_norm, dil_k_norm, dil_w_out, ffn_w_in, ffn_w_out, loss_target, m_norm_mix, m_norm_ffn, m_gdn_w_in, m_gdn_conv_w, m_gdn_a_log, m_gdn_dt_bias, m_gdn_norm_w, m_gdn_w_out, m_dil_w_in, m_dil_q_norm, m_dil_k_norm, m_dil_w_out, m_ffn_w_in, m_ffn_w_out, v_norm_mix, v_norm_ffn, v_gdn_w_in, v_gdn_conv_w, v_gdn_a_log, v_gdn_dt_bias, v_gdn_norm_w, v_gdn_w_out, v_dil_w_in, v_dil_q_norm, v_dil_k_norm, v_dil_w_out, v_ffn_w_in, v_ffn_w_out):
    w = dict(norm_mix=norm_mix, norm_ffn=norm_ffn, gdn_w_in=gdn_w_in, gdn_conv_w=gdn_conv_w, gdn_a_log=gdn_a_log,
             gdn_dt_bias=gdn_dt_bias, gdn_norm_w=gdn_norm_w, gdn_w_out=gdn_w_out, dil_w_in=dil_w_in, dil_q_norm=dil_q_norm,
             dil_k_norm=dil_k_norm, dil_w_out=dil_w_out, ffn_w_in=ffn_w_in, ffn_w_out=ffn_w_out)
    m = dict(norm_mix=m_norm_mix, norm_ffn=m_norm_ffn, gdn_w_in=m_gdn_w_in, gdn_conv_w=m_gdn_conv_w, gdn_a_log=m_gdn_a_log,
             gdn_dt_bias=m_gdn_dt_bias, gdn_norm_w=m_gdn_norm_w, gdn_w_out=m_gdn_w_out, dil_w_in=m_dil_w_in,
             dil_q_norm=m_dil_q_norm, dil_k_norm=m_dil_k_norm, dil_w_out=m_dil_w_out, ffn_w_in=m_ffn_w_in, ffn_w_out=m_ffn_w_out)
    v = dict(norm_mix=v_norm_mix, norm_ffn=v_norm_ffn, gdn_w_in=v_gdn_w_in, gdn_conv_w=v_gdn_conv_w, gdn_a_log=v_gdn_a_log,
             gdn_dt_bias=v_gdn_dt_bias, gdn_norm_w=v_gdn_norm_w, gdn_w_out=v_gdn_w_out, dil_w_in=v_dil_w_in,
             dil_q_norm=v_dil_q_norm, dil_k_norm=v_dil_k_norm, dil_w_out=v_dil_w_out, ffn_w_in=v_ffn_w_in, ffn_w_out=v_ffn_w_out)
    def row(src, i):
        return src[i].reshape(1, D_MODEL)

    first = _all_gather(_gather_operand(w, GATHER_FIRST), name="weight_all_gather_first")
    next_started = _travel_start(_gather_operand(w, GATHER_NEXT), first, per_peer=False, name="weight_gather_start_next")
    last_started = _travel_start(_gather_operand(w, GATHER_LAST), next_started[4], per_peer=False,
                                 name="weight_gather_start_last")
    full = _gathered_weights(first, GATHER_FIRST, {n: w[n] for n in REPLICATED})
    prepared = dict(gdn=_prepare_gdn(full, layers=(0,)))
    h = x[0]
    saved = [None] * DEPTH
    h, s_mix = _mixer_fwd(0, h, row(norm_mix, 0) + last_started[4][0, 0], prepared)
    got = _travel_wait(next_started, h, per_peer=False, name="weight_gather_wait_next")
    full = _gathered_weights(got, GATHER_NEXT, full)
    prepared.update(dil=_prepare_dil(full, layers=(0,)), ffn=_prepare_ffn(full, layers=(0,)))
    for i in range(DEPTH):
        if i > 0:
            h, s_mix = _mixer_fwd(i, h, row(norm_mix, i), prepared)
        if i == 1:
            got = _travel_wait(last_started, h, per_peer=False, name="weight_gather_wait_last")
            full = _gathered_weights(got, GATHER_LAST, full)
            prepared["gdn"].update(_prepare_gdn(full, layers=(1,)))
            prepared["dil"].update(_prepare_dil(full, layers=(1,)))
            prepared["ffn"].update(_prepare_ffn(full, layers=(1, 2, 3)))
        h, s_ffn = _ffn_layer_fwd(h, row(norm_ffn, i), prepared["ffn"][i])
        saved[i] = (s_mix, s_ffn)
    dx, dxb, loss = _loss_head(h, loss_target[0], name="loss_head")

    g_mix, g_ffn = [None] * DEPTH, [None] * DEPTH
    started = {}

    def travel(group):
        operand = _exchange_operand(_collect_grads(g_mix, g_ffn), EXCHANGE_GROUPS[group])
        started[group] = _travel_start(operand, dx, per_peer=True, name=f"grad_exchange_start_{group}")
        return started[group][4][0, 0]

    zero = 0.0
    for i in reversed(range(DEPTH)):
        s_mix, s_ffn = saved[i]
        dx, dxb, g_ffn[i] = _ffn_layer_bwd(dx, dxb, row(norm_ffn, i) + zero, prepared["ffn"][i], s_ffn)
        zero = travel(EXCHANGE_AFTER[("ffn", i)]) if ("ffn", i) in EXCHANGE_AFTER else 0.0
        dx, dxb, g_mix[i] = _mixer_bwd(i, dx, dxb, row(norm_mix, i), prepared, s_mix, zero)
        zero = travel(EXCHANGE_AFTER[("mix", i)]) if ("mix", i) in EXCHANGE_AFTER else 0.0
    grads = _collect_grads(g_mix, g_ffn)
    received = [_travel_wait(started[g], dx, per_peer=True, name=f"grad_exchange_wait_{g}") for g in sorted(started)]
    received.append(_exchange(_exchange_operand(grads, EXCHANGE_GROUPS[-1]), name="grad_exchange_last"))
    updated = {}
    for g, pieces in enumerate(EXCHANGE_GROUPS):
        updated.update(_update_group(received[g], pieces, w, m, v, name=f"adamw_sharded_{g}"))

    small_parts = _all_gather(_pack_small(grads), name="small_grad_all_gather")
    outs_small = [_unpack_small(o) for o in
                  _adamw(small_parts, _pack_small(w), _pack_small(m), _pack_small(v), name="adamw_replicated")]

    total_loss = lax.psum(loss[0, 0], ("x", "y", "c"))
    result = [total_loss, dx[None]]
    for k in range(4):
        for n in WEIGHT_ORDER:
            if n not in SHARDED:
                result.append(outs_small[k][n])
            elif (n, None) in updated:
                result.append(updated[(n, None)][k])
            else:
                result.append(jnp.concatenate([updated[(n, l)][k] for l in range(SHARDED[n][0][0])], axis=0))
    return tuple(result)
```

```python
import functools
import math

import jax
import jax.numpy as jnp
from jax import lax
from jax.experimental import pallas as pl
from jax.experimental.pallas import tpu as pltpu

F32 = jnp.float32
BF16 = jnp.bfloat16
MM_DTYPE = BF16

N_DEV = 8
D_MODEL = 1024
DEPTH = 4
RMS_EPS = 1e-6
L2_EPS = 1e-6

LANES = 128

GDN_HEADS = 8
GDN_DK = 128
GDN_DV = 128
GDN_CONV = 4
GDN_CHUNK = 128
GDN_QKV = 3 * GDN_HEADS * GDN_DK
GDN_MAIN = GDN_QKV + GDN_HEADS * GDN_DV
GDN_IN_WIDTH = GDN_MAIN + 2 * GDN_HEADS

DIL_GROUPS = ((128, 1), (512, 4), (2048, 16))
DIL_HEADS = 8
DIL_DH = 64
DIL_SPAN = 128
DIL_SLAB = 3 * DIL_HEADS * LANES
ALIBI_MAX_BIAS = 8.0

FFN_HIDDEN = 2816

ADAM_LR = 0.001
ADAM_B1 = 0.9
ADAM_B2 = 0.999
ADAM_EPS = 1e-08
ADAM_WD = 0.01
ADAM_STEP = 10

VMEM_LIMIT = 56 * 1024 * 1024
ROW_TILE = 512
MATMUL_VMEM_BUDGET = 40 * 1024 * 1024
NEG = -1e30


def _cparams(sem):
    return pltpu.CompilerParams(dimension_semantics=sem, vmem_limit_bytes=VMEM_LIMIT)


def _single_pass(a, b, a_dim, b_dim):
    lead = a.ndim - 2
    batch = ((0,), (0,)) if lead else ((), ())
    return lax.dot_general(a.astype(BF16), b.astype(BF16), (((lead + a_dim,), (lead + b_dim,)), batch),
                           preferred_element_type=F32)


def _bdot(a, b):
    return _single_pass(a, b, 1, 0)


def _bdot_nt(a, b):
    return _single_pass(a, b, 1, 1)


def _bdot_tn(a, b):
    return _single_pass(a, b, 0, 0)


def _pick(n, candidates):
    for c in candidates:
        if n % c == 0:
            return c
    raise ValueError(f"no tile for {n}")


HALF = LANES // 2


def _pack_head_pairs(x):
    x = x.astype(F32)
    tiles = [x[:, (2 * i) * LANES:(2 * i + 1) * LANES] + pltpu.roll(x[:, (2 * i + 1) * LANES:(2 * i + 2) * LANES], HALF, 1)
             for i in range(x.shape[1] // (2 * LANES))]
    return tiles[0] if len(tiles) == 1 else jnp.concatenate(tiles, axis=1)


def _spread_head_pairs(y):
    low = lax.broadcasted_iota(jnp.int32, (y.shape[0], LANES), 1) < HALF
    tiles = []
    for i in range(y.shape[1] // LANES):
        pair = y[:, i * LANES:(i + 1) * LANES]
        tiles += [jnp.where(low, pair, 0.0), jnp.where(low, pltpu.roll(pair, HALF, 1), 0.0)]
    return jnp.concatenate(tiles, axis=1)


def _matmul(a, b, *, name, trans_a=False, trans_b=False, b_rows=None, a_lead=None, add=None, out_dtype=F32,
            packed_a=False, spread_out=False, norm_bwd=None):
    if trans_a:
        k_dim, m_dim = a.shape[-2:]
        m_dim = m_dim // 2 if packed_a else m_dim
    else:
        m_dim, k_dim = a.shape[-2:]
        k_dim = k_dim // 2 if packed_a else k_dim
    slab_m, slab_k = m_dim, k_dim
    if a_lead == "k":
        assert not trans_a
        k_dim *= a.shape[0]
    elif a_lead == "i":
        assert trans_a
        m_dim *= a.shape[0]
    b_start, b_size = b_rows if b_rows is not None else (0, b.shape[0])
    if trans_b:
        n_dim, k2 = b_size, b.shape[1]
    else:
        k2, n_dim = b_size, b.shape[1]
    assert k_dim == k2, (a.shape, b.shape, b_rows)
    tn = _pick(n_dim, (1024, 512, 256, 128))
    tm = min(slab_m, 2048, max(512, (1024 * 1024) // tn))
    tm = _pick(slab_m, (tm, 1408, 1024, 512, 256, 128))
    out_bytes = jnp.dtype(out_dtype).itemsize * (2 if spread_out else 1)
    if norm_bwd is not None:
        out_bytes = 4 + 4 + 4 + 2
        tm = min(tm, 512)

    def deepest(rows):
        fixed = rows * tn * (2 * out_bytes + 4 + (8 if add is not None else 0))
        fits = lambda c: fixed + 2 * 2 * c * ((2 if packed_a else 1) * rows + tn) <= MATMUL_VMEM_BUDGET
        return _pick(slab_k, tuple(c for c in (3072, 2816, 2048, 1536, 1408, 1024, 512, 256) if fits(c)) + (128,))

    tk = deepest(tm)
    if tm % 1024 == 0 and deepest(tm // 2) > tk:
        tm, tk = tm // 2, deepest(tm // 2)
    nk = k_dim // tk
    has_add = add is not None
    dn = (((0 if trans_a else 1,), (1 if trans_b else 0,)), ((), ()))
    b_tile = tn if trans_b else tk
    assert b_start % b_tile == 0, (b_rows, b_tile)
    b_off = b_start // b_tile

    has_norm = norm_bwd is not None
    if has_norm:
        assert n_dim == tn and not spread_out

    def body(*refs):
        refs = list(refs)
        a_ref, b_ref = refs[:2]
        add_ref = refs[2] if has_add else None
        rest = refs[2 + has_add:]
        if has_norm:
            x_ref, w_ref, skip_ref, dx_ref, dxb_ref, dw_ref, acc_ref = rest
        else:
            o_ref, acc_ref = rest
        a_blk = _pack_head_pairs(a_ref[...]).astype(a_ref.dtype) if packed_a else a_ref[...]
        part = lax.dot_general(a_blk, b_ref[...], dn, preferred_element_type=F32)
        first_rows = pl.program_id(0) == 0

        def finish(total):
            if has_add:
                total = total + add_ref[...]
            if has_norm:
                xf = x_ref[...]
                r = lax.rsqrt(jnp.mean(xf * xf, axis=-1, keepdims=True) + RMS_EPS)
                gw = total * w_ref[...]
                dx = r * gw - xf * (r * r * r * jnp.mean(gw * xf, axis=-1, keepdims=True)) + skip_ref[...]
                dx_ref[...] = dx
                dxb_ref[...] = dx.astype(dxb_ref.dtype)
                rows = jnp.sum(total * xf * r, axis=0, keepdims=True)

                @pl.when(first_rows)
                def _():
                    dw_ref[...] = rows

                @pl.when(jnp.logical_not(first_rows))
                def _():
                    dw_ref[...] += rows
                return
            if spread_out:
                total = _spread_head_pairs(total)
            o_ref[...] = total.astype(out_dtype)

        if nk == 1:
            finish(part)
        else:
            k = pl.program_id(2)

            @pl.when(k == 0)
            def _():
                acc_ref[...] = part

            @pl.when(k > 0)
            def _():
                acc_ref[...] += part

            @pl.when(k == nk - 1)
            def _():
                finish(acc_ref[...])

    wide = 2 if packed_a else 1
    a_tile = (tk, wide * tm) if trans_a else (tm, wide * tk)
    a_at = (lambda i, j, k: (k, i)) if trans_a else (lambda i, j, k: (i, k))
    if a_lead is None:
        a_spec = pl.BlockSpec(a_tile, a_at)
    elif a_lead == "k":
        per = slab_k // tk
        a_spec = pl.BlockSpec((None,) + a_tile, lambda i, j, k: (k // per, i, k % per))
    elif a_lead == "i":
        per = slab_m // tm
        a_spec = pl.BlockSpec((None,) + a_tile, lambda i, j, k: (i // per, k, i % per))
    else:
        a_spec = pl.BlockSpec((None,) + a_tile, lambda i, j, k: (a_lead,) + a_at(i, j, k))
    if trans_b:
        b_spec = pl.BlockSpec((tn, tk), lambda i, j, k: (j + b_off, k))
    else:
        b_spec = pl.BlockSpec((tk, tn), lambda i, j, k: (k + b_off, j))
    in_specs = [a_spec, b_spec]
    args = [a, b]
    tile = pl.BlockSpec((tm, tn), lambda i, j, k: (i, j))
    if has_add:
        in_specs.append(tile)
        args.append(add)
    scratch = [pltpu.VMEM((tm, tn) if nk > 1 else (8, LANES), F32)]
    if has_norm:
        x, w_row, dskip = norm_bwd
        one = pl.BlockSpec((1, tn), lambda i, j, k: (0, 0))
        return pl.pallas_call(
            body,
            grid=(m_dim // tm, 1, nk),
            in_specs=in_specs + [tile, one, tile],
            out_specs=[tile, tile, one],
            out_shape=[jax.ShapeDtypeStruct((m_dim, n_dim), F32), jax.ShapeDtypeStruct((m_dim, n_dim), MM_DTYPE),
                       jax.ShapeDtypeStruct((1, n_dim), F32)],
            scratch_shapes=scratch,
            compiler_params=_cparams(("arbitrary", "arbitrary", "arbitrary")),
            name=name,
        )(*args, x, w_row, dskip)
    return pl.pallas_call(
        body,
        grid=(m_dim // tm, n_dim // tn, nk),
        in_specs=in_specs,
        out_specs=pl.BlockSpec((tm, (2 if spread_out else 1) * tn), lambda i, j, k: (i, j)),
        out_shape=jax.ShapeDtypeStruct((m_dim, (2 if spread_out else 1) * n_dim), out_dtype),
        scratch_shapes=scratch,
        compiler_params=_cparams(("parallel", "parallel", "arbitrary")),
        name=name,
    )(*args)


def _rmsnorm_fwd(x, w_row, *, name):
    t, d = x.shape
    tb = min(t, 1024)

    def body(x_ref, w_ref, o_ref):
        xf = x_ref[...]
        r = lax.rsqrt(jnp.mean(xf * xf, axis=-1, keepdims=True) + RMS_EPS)
        o_ref[...] = (xf * r * w_ref[...]).astype(o_ref.dtype)

    return pl.pallas_call(
        body,
        grid=(t // tb,),
        in_specs=[pl.BlockSpec((tb, d), lambda i: (i, 0)), pl.BlockSpec((1, d), lambda i: (0, 0))],
        out_specs=pl.BlockSpec((tb, d), lambda i: (i, 0)),
        out_shape=jax.ShapeDtypeStruct((t, d), MM_DTYPE),
        compiler_params=_cparams(("parallel",)),
        name=name,
    )(x, w_row)


def _silu(z):
    return z / (1.0 + jnp.exp(-z))


FFN_TM, FFN_TN = 512, 1408


def _ffn_in(hn, in_t, *, name):
    t, d = hn.shape
    h = FFN_HIDDEN
    tm, tn = min(t, FFN_TM), FFN_TN
    nj = h // tn
    dn = (((1,), (1,)), ((), ()))

    def body(a_ref, bg_ref, bu_ref, g_ref, u_ref, act_ref):
        a = a_ref[...]
        g = lax.dot_general(a, bg_ref[...], dn, preferred_element_type=F32)
        u = lax.dot_general(a, bu_ref[...], dn, preferred_element_type=F32)
        g_ref[...] = g.astype(g_ref.dtype)
        u_ref[...] = u.astype(u_ref.dtype)
        act_ref[...] = (_silu(g) * u).astype(act_ref.dtype)

    out = pl.BlockSpec((tm, tn), lambda j, i: (i, j))
    return pl.pallas_call(
        body,
        grid=(nj, t // tm),
        in_specs=[pl.BlockSpec((tm, d), lambda j, i: (i, 0)), pl.BlockSpec((tn, d), lambda j, i: (j, 0)),
                  pl.BlockSpec((tn, d), lambda j, i: (j + nj, 0))],
        out_specs=[out, out, out],
        out_shape=[jax.ShapeDtypeStruct((t, h), MM_DTYPE)] * 3,
        compiler_params=_cparams(("parallel", "parallel")),
        name=name,
    )(hn, in_t, in_t)


def _ffn_dact(dy, out_w, g, u, *, name):
    t, d = dy.shape
    h = FFN_HIDDEN
    tm, tn = min(t, FFN_TM), FFN_TN

    def body(a_ref, b_ref, g_ref, u_ref, d_ref):
        da = lax.dot_general(a_ref[...], b_ref[...], (((1,), (1,)), ((), ())), preferred_element_type=F32)
        gate = g_ref[...].astype(F32)
        sig = 1.0 / (1.0 + jnp.exp(-gate))
        sg = gate * sig
        d_ref[0] = (da * u_ref[...].astype(F32) * (sig + sg * (1.0 - sig))).astype(d_ref.dtype)
        d_ref[1] = (da * sg).astype(d_ref.dtype)

    blk = pl.BlockSpec((tm, tn), lambda j, i: (i, j))
    return pl.pallas_call(
        body,
        grid=(h // tn, t // tm),
        in_specs=[pl.BlockSpec((tm, d), lambda j, i: (i, 0)), pl.BlockSpec((tn, d), lambda j, i: (j, 0)), blk, blk],
        out_specs=pl.BlockSpec((2, tm, tn), lambda j, i: (0, i, j)),
        out_shape=jax.ShapeDtypeStruct((2, t, h), MM_DTYPE),
        compiler_params=_cparams(("parallel", "parallel")),
        name=name,
    )(dy, out_w, g, u)


def _loss_head(y, target, *, name):
    t, d = y.shape
    tb = min(t, 1024)

    def body(y_ref, t_ref, dy_ref, dyb_ref, l_ref):
        err = y_ref[...] - t_ref[...]
        dy_ref[...] = err * (1.0 / d)
        dyb_ref[...] = (err * (1.0 / d)).astype(dyb_ref.dtype)
        part = jnp.sum(jnp.sum(err * err, axis=0, keepdims=True), axis=1, keepdims=True) * (0.5 / d)
        part = jnp.broadcast_to(part, l_ref.shape)

        @pl.when(pl.program_id(0) == 0)
        def _():
            l_ref[...] = part

        @pl.when(pl.program_id(0) > 0)
        def _():
            l_ref[...] += part

    row = pl.BlockSpec((tb, d), lambda i: (i, 0))
    return pl.pallas_call(
        body,
        grid=(t // tb,),
        in_specs=[row, row],
        out_specs=[row, row, pl.BlockSpec((8, LANES), lambda i: (0, 0))],
        out_shape=[jax.ShapeDtypeStruct((t, d), F32), jax.ShapeDtypeStruct((t, d), MM_DTYPE),
                   jax.ShapeDtypeStruct((8, LANES), F32)],
        compiler_params=_cparams(("arbitrary",)),
        name=name,
    )(y, target)


CONV_HALO = 8
CONV_TIME_TILE = 2048


def _conv_tile_scale(c):
    is_qk = c < 2 * GDN_HEADS
    scale = jnp.where(c < GDN_HEADS, GDN_DK ** -0.5, 1.0).astype(F32)
    return is_qk, scale


def _gdn_conv_fwd(pm, conv_w, *, name):
    t = pm.shape[0]
    tb = min(t, CONV_TIME_TILE)
    nt = t // tb
    hb = tb // CONV_HALO

    def body(x_ref, xp_ref, w_ref, o_ref, xe_ref):
        c = pl.program_id(0)
        ti = pl.program_id(1)
        xe_ref[0:CONV_HALO, :] = jnp.where(ti > 0, xp_ref[...], 0.0)
        xe_ref[CONV_HALO:CONV_HALO + tb, :] = x_ref[...]
        w = w_ref[...]
        y = jnp.zeros((tb, LANES), F32)
        for j in range(GDN_CONV):
            off = CONV_HALO - (GDN_CONV - 1) + j
            y = y + w[j:j + 1, :] * xe_ref[pl.ds(off, tb), :]
        s = _silu(y)
        is_qk, scale = _conv_tile_scale(c)
        r = lax.rsqrt(jnp.sum(s * s, axis=-1, keepdims=True) + L2_EPS) * scale
        o_ref[...] = s * jnp.where(is_qk, r, 1.0)

    return pl.pallas_call(
        body,
        grid=(GDN_QKV // LANES, nt),
        in_specs=[
            pl.BlockSpec((tb, LANES), lambda c, i: (i, c)),
            pl.BlockSpec((CONV_HALO, LANES), lambda c, i: (jnp.maximum(i * hb - 1, 0), c)),
            pl.BlockSpec((GDN_CONV, LANES), lambda c, i: (0, c)),
        ],
        out_specs=pl.BlockSpec((tb, LANES), lambda c, i: (i, c)),
        out_shape=jax.ShapeDtypeStruct((t, GDN_QKV), F32),
        scratch_shapes=[pltpu.VMEM((tb + CONV_HALO, LANES), F32)],
        compiler_params=_cparams(("parallel", "parallel")),
        name=name,
    )(pm, pm, conv_w)


def _gdn_conv_bwd(pm, conv_w, dout, *, name):
    t = pm.shape[0]
    tb = min(t, CONV_TIME_TILE)
    nt = t // tb
    hb = tb // CONV_HALO
    last_hb = t // CONV_HALO - 1
    ext = tb + CONV_HALO

    def body(x_ref, xp_ref, xn_ref, d_ref, dn_ref, w_ref, dx_ref, dw_ref, xe_ref, dy_ref):
        c = pl.program_id(0)
        ti = pl.program_id(1)
        has_next = ti < nt - 1
        xe_ref[0:CONV_HALO, :] = jnp.where(ti > 0, xp_ref[...], 0.0)
        xe_ref[CONV_HALO:CONV_HALO + tb, :] = x_ref[...]
        xe_ref[CONV_HALO + tb:2 * CONV_HALO + tb, :] = jnp.where(has_next, xn_ref[...], 0.0)
        de = jnp.concatenate([d_ref[...], jnp.where(has_next, dn_ref[...], 0.0)], axis=0)
        w = w_ref[...]
        y = jnp.zeros((ext, LANES), F32)
        for j in range(GDN_CONV):
            off = CONV_HALO - (GDN_CONV - 1) + j
            y = y + w[j:j + 1, :] * xe_ref[pl.ds(off, ext), :]
        sig = 1.0 / (1.0 + jnp.exp(-y))
        s = y * sig
        is_qk, scale = _conv_tile_scale(c)
        r = lax.rsqrt(jnp.sum(s * s, axis=-1, keepdims=True) + L2_EPS)
        n = s * r
        dnrm = de * scale
        ds_qk = r * (dnrm - n * jnp.sum(dnrm * n, axis=-1, keepdims=True))
        ds = jnp.where(is_qk, ds_qk, de)
        dy_ref[...] = ds * (sig + s * (1.0 - sig))
        dy = dy_ref[0:tb, :]
        dx = jnp.zeros((tb, LANES), F32)
        dw_rows = []
        for j in range(GDN_CONV):
            sh = GDN_CONV - 1 - j
            dx = dx + w[j:j + 1, :] * dy_ref[pl.ds(sh, tb), :]
            off = CONV_HALO - (GDN_CONV - 1) + j
            dw_rows.append(jnp.sum(dy * xe_ref[pl.ds(off, tb), :], axis=0, keepdims=True))
        dx_ref[...] = dx.astype(dx_ref.dtype)
        part = jnp.concatenate(dw_rows, axis=0)

        @pl.when(ti == 0)
        def _():
            dw_ref[...] = part

        @pl.when(ti > 0)
        def _():
            dw_ref[...] += part

    main = pl.BlockSpec((tb, LANES), lambda c, i: (i, c))
    prev = pl.BlockSpec((CONV_HALO, LANES), lambda c, i: (jnp.maximum(i * hb - 1, 0), c))
    nxt = pl.BlockSpec((CONV_HALO, LANES), lambda c, i: (jnp.minimum((i + 1) * hb, last_hb), c))
    return pl.pallas_call(
        body,
        grid=(GDN_QKV // LANES, nt),
        in_specs=[main, prev, nxt, main, nxt, pl.BlockSpec((GDN_CONV, LANES), lambda c, i: (0, c))],
        out_specs=[main, pl.BlockSpec((GDN_CONV, LANES), lambda c, i: (0, c))],
        out_shape=[jax.ShapeDtypeStruct((t, GDN_QKV), MM_DTYPE), jax.ShapeDtypeStruct((GDN_CONV, GDN_QKV), F32)],
        scratch_shapes=[pltpu.VMEM((tb + 2 * CONV_HALO, LANES), F32), pltpu.VMEM((ext, LANES), F32)],
        compiler_params=_cparams(("parallel", "arbitrary")),
        name=name,
    )(pm, pm, pm, dout, dout, conv_w)


def _head_selector(first_col):
    row = lax.broadcasted_iota(jnp.int32, (LANES, GDN_HEADS * LANES), 0)
    col = lax.broadcasted_iota(jnp.int32, (LANES, GDN_HEADS * LANES), 1)
    return (col // LANES + first_col == row).astype(BF16)


def _spread_columns(cols, first_col):
    sel = _head_selector(first_col)
    return sum(_bdot(p, sel) for p in _bf16_pieces(cols))


def _gather_columns(wide, first_col):
    sel = _head_selector(first_col)
    return sum(_bdot_nt(p, sel) for p in _bf16_pieces(wide))


def _softplus(x):
    return jnp.maximum(x, 0.0) + jnp.log(1.0 + jnp.exp(-jnp.abs(x)))


def _gdn_gates_fwd(ab, alog_row, dt_row, *, name):
    t = ab.shape[0]
    tb = min(t, 1024)
    wide = GDN_HEADS * LANES

    def body(ab_ref, al_ref, dt_ref, g_ref, b_ref):
        x = ab_ref[...]
        g_cols = -jnp.exp(al_ref[...]) * _softplus(x + dt_ref[...])
        b_cols = 1.0 / (1.0 + jnp.exp(-x))
        g_ref[...] = _spread_columns(g_cols, 0)
        b_ref[...] = _spread_columns(b_cols, GDN_HEADS)

    row = pl.BlockSpec((tb, LANES), lambda i: (i, 0))
    one = pl.BlockSpec((1, LANES), lambda i: (0, 0))
    out = pl.BlockSpec((tb, wide), lambda i: (i, 0))
    return pl.pallas_call(
        body,
        grid=(t // tb,),
        in_specs=[row, one, one],
        out_specs=[out, out],
        out_shape=[jax.ShapeDtypeStruct((t, wide), F32)] * 2,
        compiler_params=_cparams(("parallel",)),
        name=name,
    )(ab, alog_row, dt_row)


def _gdn_gates_bwd(ab, alog_row, dt_row, dgb, dbb, *, name):
    t = ab.shape[0]
    tb = min(t, 1024)
    wide = GDN_HEADS * LANES

    def body(ab_ref, al_ref, dt_ref, dg_ref, db_ref, dab_ref, dal_ref, ddt_ref):
        x = ab_ref[...]
        lane = lax.broadcasted_iota(jnp.int32, (tb, LANES), 1)
        dg_cols = _gather_columns(dg_ref[...], 0)
        db_cols = _gather_columns(db_ref[...], GDN_HEADS)
        ea = jnp.exp(al_ref[...])
        z = x + dt_ref[...]
        sp = _softplus(z)
        sg = 1.0 / (1.0 + jnp.exp(-z))
        beta = 1.0 / (1.0 + jnp.exp(-x))
        da = jnp.where(lane < GDN_HEADS, dg_cols * (-ea) * sg, 0.0)
        db = jnp.where((lane >= GDN_HEADS) & (lane < 2 * GDN_HEADS), db_cols * beta * (1.0 - beta), 0.0)
        dab_ref[...] = (da + db).astype(dab_ref.dtype)
        p_al = jnp.sum(jnp.where(lane < GDN_HEADS, dg_cols * (-ea) * sp, 0.0), axis=0, keepdims=True)
        p_dt = jnp.sum(da, axis=0, keepdims=True)

        @pl.when(pl.program_id(0) == 0)
        def _():
            dal_ref[...] = p_al
            ddt_ref[...] = p_dt

        @pl.when(pl.program_id(0) > 0)
        def _():
            dal_ref[...] += p_al
            ddt_ref[...] += p_dt

    row = pl.BlockSpec((tb, LANES), lambda i: (i, 0))
    one = pl.BlockSpec((1, LANES), lambda i: (0, 0))
    big = pl.BlockSpec((tb, wide), lambda i: (i, 0))
    return pl.pallas_call(
        body,
        grid=(t // tb,),
        in_specs=[row, one, one, big, big],
        out_specs=[row, one, one],
        out_shape=[jax.ShapeDtypeStruct((t, LANES), MM_DTYPE), jax.ShapeDtypeStruct((1, LANES), F32),
                   jax.ShapeDtypeStruct((1, LANES), F32)],
        compiler_params=_cparams(("arbitrary",)),
        name=name,
    )(ab, alog_row, dt_row, dgb, dbb)


@jax.custom_vjp
def _unit_lower_inverse_rest(n):
    c = n.shape[-1]
    ri = lax.broadcasted_iota(jnp.int32, (c, c), 0)
    ci = lax.broadcasted_iota(jnp.int32, (c, c), 1)
    rest = None
    size = 1
    while size < c:
        joins = ((ri // (2 * size)) == (ci // (2 * size))) & ((ri // size) != (ci // size))
        low = jnp.where(joins, n, 0.0)
        if rest is None:
            rest = -low
        else:
            left = low + _bdot(rest, low)
            rest = rest - (left + _bdot(left, rest))
        size *= 2
    return rest


def _unit_lower_inverse_rest_fwd(n):
    rest = _unit_lower_inverse_rest(n)
    return rest, rest


def _unit_lower_inverse_rest_bwd(rest, ct):
    left = ct + _bdot_tn(rest, ct)
    return (-(left + _bdot_nt(left, rest)),)


_unit_lower_inverse_rest.defvjp(_unit_lower_inverse_rest_fwd, _unit_lower_inverse_rest_bwd)


@jax.custom_vjp
def _known_inverse_rest(n, rest):
    return rest


def _known_inverse_rest_fwd(n, rest):
    return rest, rest


def _known_inverse_rest_bwd(rest, ct):
    return _unit_lower_inverse_rest_bwd(rest, ct) + (jnp.zeros_like(rest),)


_known_inverse_rest.defvjp(_known_inverse_rest_fwd, _known_inverse_rest_bwd)


def _bf16_pieces(x):
    hi = x.astype(BF16)
    r1 = x - hi.astype(F32)
    mid = r1.astype(BF16)
    lo = (r1 - mid.astype(F32)).astype(BF16)
    return hi, mid, lo


def _lower_ones(shape):
    c = shape[-1]
    ri = lax.broadcasted_iota(jnp.int32, (c, c), 0)
    ci = lax.broadcasted_iota(jnp.int32, (c, c), 1)
    return jnp.broadcast_to((ri >= ci).astype(BF16), shape)


@jax.custom_vjp
def _running_sum(x):
    tri = _lower_ones(x.shape)
    return sum(_bdot(tri, p) for p in _bf16_pieces(x))


def _running_sum_fwd(x):
    return _running_sum(x), None


def _running_sum_bwd(_, ct):
    tri = _lower_ones(ct.shape)
    return (sum(_bdot_tn(tri, p) for p in _bf16_pieces(ct)),)


_running_sum.defvjp(_running_sum_fwd, _running_sum_bwd)


def _gdn_prep_math(q, k, v, gb, bb, known_rest=None, with_rest=False):
    c = GDN_CHUNK
    ri = lax.broadcasted_iota(jnp.int32, (c, c), 0)
    ci = lax.broadcasted_iota(jnp.int32, (c, c), 1)
    causal = ri >= ci
    gc = _running_sum(gb)
    decay = jnp.exp(jnp.where(causal, gc - jnp.swapaxes(gc, -1, -2), NEG))
    n = jnp.where(ri > ci, _bdot_nt(k, k) * bb * decay, 0.0)
    rest = _unit_lower_inverse_rest(n) if known_rest is None else _known_inverse_rest(n, known_rest)
    eg = jnp.exp(gc)
    rhs_v = v * bb
    rhs_k = k * bb * eg
    u = rhs_v + _bdot(rest, rhs_v)
    w = rhs_k + _bdot(rest, rhs_k)
    qk = _bdot_nt(q, k) * decay
    qd = q * eg
    last = jnp.sum(jnp.where(ri == c - 1, gc, 0.0), axis=-2, keepdims=True)
    gl = jnp.broadcast_to(last, gc.shape)
    kt = k * jnp.exp(gl - gc)
    cd = jnp.exp(gl)
    return (u, w, qk, qd, kt, cd, rest) if with_rest else (u, w, qk, qd, kt, cd)


def _head_tiles(ref, h):
    return ref[:, h * LANES:(h + 1) * LANES]


def _stack_heads(ref, first=0, heads=GDN_HEADS):
    return jnp.stack([_head_tiles(ref, first + h) for h in range(heads)])


def _store_heads(ref, val, first=0):
    for h in range(val.shape[0]):
        ref[:, (first + h) * LANES:(first + h + 1) * LANES] = val[h].astype(ref.dtype)


def _gdn_prep_fwd(qkv, gb, bb, *, name):
    t = qkv.shape[0]
    c = GDN_CHUNK
    wide = GDN_HEADS * LANES

    def body(q_ref, k_ref, v_ref, g_ref, b_ref, *outs):
        res = _gdn_prep_math(*(_stack_heads(r) for r in (q_ref, k_ref, v_ref, g_ref, b_ref)), with_rest=True)
        for o_ref, val in zip(outs, res):
            _store_heads(o_ref, val)

    blk = lambda off: pl.BlockSpec((c, wide), lambda i: (i, off))
    outs = pl.pallas_call(
        body,
        grid=(t // c,),
        in_specs=[blk(0), blk(1), blk(2), blk(0), blk(0)],
        out_specs=[blk(0)] * 7,
        out_shape=[jax.ShapeDtypeStruct((t, wide), dt) for dt in (F32, MM_DTYPE, MM_DTYPE, MM_DTYPE, MM_DTYPE, F32, F32)],
        compiler_params=_cparams(("parallel",)),
        name=name,
    )(qkv, qkv, qkv, gb, bb)
    return tuple(outs[:6]), outs[6]


def _gdn_prep_bwd(qkv, gb, bb, rest, cts, *, name):
    t = qkv.shape[0]
    c = GDN_CHUNK
    wide = GDN_HEADS * LANES

    def body(q_ref, k_ref, v_ref, g_ref, b_ref, r_ref, c0, c1, c2, c3, c4, c5, dqkv_ref, dg_ref, db_ref):
        prim = tuple(_stack_heads(r) for r in (q_ref, k_ref, v_ref, g_ref, b_ref))
        _, pull = jax.vjp(functools.partial(_gdn_prep_math, known_rest=_stack_heads(r_ref)), *prim)
        dq, dk, dv, dg, db = pull(tuple(_stack_heads(r) for r in (c0, c1, c2, c3, c4, c5)))
        _store_heads(dqkv_ref, dq)
        _store_heads(dqkv_ref, dk, first=GDN_HEADS)
        _store_heads(dqkv_ref, dv, first=2 * GDN_HEADS)
        _store_heads(dg_ref, dg)
        _store_heads(db_ref, db)

    blk = lambda off: pl.BlockSpec((c, wide), lambda i: (i, off))
    return pl.pallas_call(
        body,
        grid=(t // c,),
        in_specs=[blk(0), blk(1), blk(2), blk(0), blk(0)] + [blk(0)] * 7,
        out_specs=[pl.BlockSpec((c, 3 * wide), lambda i: (i, 0)), blk(0), blk(0)],
        out_shape=[jax.ShapeDtypeStruct((t, 3 * wide), F32), jax.ShapeDtypeStruct((t, wide), F32),
                   jax.ShapeDtypeStruct((t, wide), F32)],
        compiler_params=_cparams(("parallel",)),
        name=name,
    )(qkv, qkv, qkv, gb, bb, rest, *cts)


def _gdn_scan_math(s, u, w, qk, qd, kt, cd):
    v_new = u - _bdot(w, s)
    o = _bdot(qd, s) + _bdot(qk, v_new)
    s_new = s * cd + _bdot_tn(kt, v_new)
    return o, s_new


def _gdn_scan_fwd(prep, *, name):
    t = prep[0].shape[0]
    c = GDN_CHUNK
    wide = GDN_HEADS * LANES

    def body(u_ref, w_ref, qk_ref, qd_ref, kt_ref, cd_ref, o_ref, st_ref, s_ref):
        @pl.when(pl.program_id(0) == 0)
        def _():
            s_ref[...] = jnp.zeros_like(s_ref)

        s = _stack_heads(s_ref)
        _store_heads(st_ref, s)
        o, s_new = _gdn_scan_math(s, *(_stack_heads(r).astype(F32) for r in (u_ref, w_ref, qk_ref, qd_ref, kt_ref, cd_ref)))
        _store_heads(o_ref, o)
        _store_heads(s_ref, s_new)

    blk = pl.BlockSpec((c, wide), lambda i: (i, 0))
    return pl.pallas_call(
        body,
        grid=(t // c,),
        in_specs=[blk] * 6,
        out_specs=[blk, blk],
        out_shape=[jax.ShapeDtypeStruct((t, wide), F32)] * 2,
        scratch_shapes=[pltpu.VMEM((GDN_DK, wide), F32)],
        compiler_params=_cparams(("arbitrary",)),
        name=name,
    )(*prep)


def _gdn_scan_bwd(prep, states, do, *, name):
    t = do.shape[0]
    c = GDN_CHUNK
    wide = GDN_HEADS * LANES
    nc = t // c

    def body(u_ref, w_ref, qk_ref, qd_ref, kt_ref, cd_ref, st_ref, do_ref, *rest):
        outs, ds_ref = rest[:6], rest[6]

        @pl.when(pl.program_id(0) == 0)
        def _():
            ds_ref[...] = jnp.zeros_like(ds_ref)

        prim = tuple(_stack_heads(r).astype(F32) for r in (st_ref, u_ref, w_ref, qk_ref, qd_ref, kt_ref, cd_ref))
        _, pull = jax.vjp(_gdn_scan_math, *prim)
        grads = pull((_stack_heads(do_ref), _stack_heads(ds_ref)))
        _store_heads(ds_ref, grads[0])
        for o_ref, val in zip(outs, grads[1:]):
            _store_heads(o_ref, val)

    blk = pl.BlockSpec((c, wide), lambda i: (nc - 1 - i, 0))
    return pl.pallas_call(
        body,
        grid=(nc,),
        in_specs=[blk] * 8,
        out_specs=[blk] * 6,
        out_shape=[jax.ShapeDtypeStruct((t, wide), F32)] * 6,
        scratch_shapes=[pltpu.VMEM((GDN_DK, wide), F32)],
        compiler_params=_cparams(("arbitrary",)),
        name=name,
    )(*prep, states, do)


def _gdn_outgate_math(o, z, nw):
    r = lax.rsqrt(jnp.mean(o * o, axis=-1, keepdims=True) + RMS_EPS)
    return o * r * nw * _silu(z)


def _gdn_outgate_fwd(o, pm, nw_row, *, name):
    t = o.shape[0]
    tb = min(t, ROW_TILE)
    wide = GDN_HEADS * LANES
    z_at = GDN_QKV // wide

    def body(o_ref, z_ref, nw_ref, y_ref):
        for h in range(GDN_HEADS):
            y = _gdn_outgate_math(_head_tiles(o_ref, h), _head_tiles(z_ref, h), nw_ref[...])
            y_ref[:, h * LANES:(h + 1) * LANES] = y.astype(y_ref.dtype)

    return pl.pallas_call(
        body,
        grid=(t // tb,),
        in_specs=[pl.BlockSpec((tb, wide), lambda i: (i, 0)), pl.BlockSpec((tb, wide), lambda i: (i, z_at)),
                  pl.BlockSpec((1, LANES), lambda i: (0, 0))],
        out_specs=pl.BlockSpec((tb, wide), lambda i: (i, 0)),
        out_shape=jax.ShapeDtypeStruct((t, wide), MM_DTYPE),
        compiler_params=_cparams(("parallel",)),
        name=name,
    )(o, pm, nw_row)


def _gdn_outgate_bwd(o, pm, nw_row, dy, *, name):
    t = o.shape[0]
    tb = min(t, ROW_TILE)
    wide = GDN_HEADS * LANES
    z_at = GDN_QKV // wide

    def body(o_ref, z_ref, nw_ref, dy_ref, do_ref, dz_ref, dnw_ref):
        total = jnp.zeros((1, LANES), F32)
        for h in range(GDN_HEADS):
            _, pull = jax.vjp(_gdn_outgate_math, _head_tiles(o_ref, h), _head_tiles(z_ref, h), nw_ref[...])
            d_o, d_z, d_nw = pull(_head_tiles(dy_ref, h))
            do_ref[:, h * LANES:(h + 1) * LANES] = d_o
            dz_ref[:, h * LANES:(h + 1) * LANES] = d_z.astype(dz_ref.dtype)
            total = total + d_nw

        @pl.when(pl.program_id(0) == 0)
        def _():
            dnw_ref[...] = total

        @pl.when(pl.program_id(0) > 0)
        def _():
            dnw_ref[...] += total

    blk = pl.BlockSpec((tb, wide), lambda i: (i, 0))
    one = pl.BlockSpec((1, LANES), lambda i: (0, 0))
    return pl.pallas_call(
        body,
        grid=(t // tb,),
        in_specs=[blk, pl.BlockSpec((tb, wide), lambda i: (i, z_at)), one, blk],
        out_specs=[blk, blk, one],
        out_shape=[jax.ShapeDtypeStruct((t, wide), F32), jax.ShapeDtypeStruct((t, wide), MM_DTYPE),
                   jax.ShapeDtypeStruct((1, LANES), F32)],
        compiler_params=_cparams(("arbitrary",)),
        name=name,
    )(o, pm, nw_row, dy)


def _rms64(x, w_row):
    return x * lax.rsqrt(jnp.sum(x * x, axis=-1, keepdims=True) * (1.0 / DIL_DH) + RMS_EPS) * w_row


def _alibi_slopes(group):
    head = lax.broadcasted_iota(jnp.int32, (DIL_HEADS, 8, LANES), 0).astype(F32)
    rate = -math.log(2.0) * ALIBI_MAX_BIAS / (len(DIL_GROUPS) * DIL_HEADS)
    slope = jnp.exp(rate * (head + float(group * DIL_HEADS + 1)))
    return jnp.broadcast_to(slope[:, 0:1, :], (DIL_HEADS, DIL_SPAN, LANES))


def _band_logits(qn, kp, kc, slope_d, has_prev):
    qi = lax.broadcasted_iota(jnp.int32, (DIL_SPAN, DIL_SPAN), 0)
    kj = lax.broadcasted_iota(jnp.int32, (DIL_SPAN, DIL_SPAN), 1)
    steps_c = (qi - kj).astype(F32)
    scale = DIL_DH ** -0.5
    sp = _bdot_nt(qn, kp) * scale - slope_d * (steps_c + float(DIL_SPAN))
    sc = _bdot_nt(qn, kc) * scale - slope_d * steps_c
    sp = jnp.where((kj >= qi) & has_prev, sp, NEG)
    sc = jnp.where(kj <= qi, sc, NEG)
    return sp, sc


def _dil_attn_fwd(slab, wq_row, wk_row, *, group, name):
    dilation = DIL_GROUPS[group][1]
    t = slab.shape[0]
    rows = t // dilation
    nlb = rows // DIL_SPAN
    wide = DIL_HEADS * LANES
    view = slab.reshape(rows, dilation * DIL_SLAB)

    def body(q_ref, kc_ref, vc_ref, kp_ref, vp_ref, wq_ref, wk_ref, o_ref):
        has_prev = pl.program_id(1) > 0
        lane = lax.broadcasted_iota(jnp.int32, (DIL_SPAN, LANES), 1)
        qn = _rms64(_stack_heads(q_ref), wq_ref[...])
        kc = _rms64(_stack_heads(kc_ref), wk_ref[...])
        kp = _rms64(_stack_heads(kp_ref), wk_ref[...])
        sp, sc = _band_logits(qn, kp, kc, _alibi_slopes(group) * float(dilation), has_prev)
        m = jnp.maximum(jnp.max(sp, axis=-1, keepdims=True), jnp.max(sc, axis=-1, keepdims=True))
        pp = jnp.exp(sp - m)
        pc = jnp.exp(sc - m)
        l = jnp.sum(pp, axis=-1, keepdims=True) + jnp.sum(pc, axis=-1, keepdims=True)
        o = (_bdot(pp, _stack_heads(vp_ref)) + _bdot(pc, _stack_heads(vc_ref))) / l
        _store_heads(o_ref, jnp.where(lane < DIL_DH, o, m + jnp.log(l)))

    cur = lambda part: pl.BlockSpec((DIL_SPAN, wide), lambda r, i: (i, 3 * r + part))
    prv = lambda part: pl.BlockSpec((DIL_SPAN, wide), lambda r, i: (jnp.maximum(i - 1, 0), 3 * r + part))
    one = pl.BlockSpec((1, LANES), lambda r, i: (0, 0))
    out = pl.pallas_call(
        body,
        grid=(dilation, nlb),
        in_specs=[cur(0), cur(1), cur(2), prv(1), prv(2), one, one],
        out_specs=pl.BlockSpec((DIL_SPAN, wide), lambda r, i: (i, r)),
        out_shape=jax.ShapeDtypeStruct((rows, dilation * wide), F32),
        compiler_params=_cparams(("parallel", "parallel")),
        name=name,
    )(view, view, view, view, view, wq_row, wk_row)
    return out.reshape(t, wide)


def _head_slope(group, head):
    idx = jnp.zeros((8, LANES), F32) + head.astype(F32)
    rate = -math.log(2.0) * ALIBI_MAX_BIAS / (len(DIL_GROUPS) * DIL_HEADS)
    slope = jnp.exp(rate * (idx + float(group * DIL_HEADS + 1)))
    return jnp.broadcast_to(slope[0:1, :], (DIL_SPAN, LANES))


RESIDUE_BATCH = 8


def _take_residues(ref, d, first=0, count=None):
    count = d if count is None else count
    return jnp.stack([ref[pl.ds(first + r, DIL_SPAN, stride=d), :] for r in range(count)])


def _put_residues(ref, val, d, first=0):
    for r in range(val.shape[0]):
        ref[pl.ds(first + r, DIL_SPAN, stride=d), :] = val[r]


def _dil_attn_fwd_strided(slab, wq_row, wk_row, *, group, name):
    d = DIL_GROUPS[group][1]
    t = slab.shape[0]
    span = DIL_SPAN * d
    nsb = t // span

    hs = max(1, RESIDUE_BATCH // d)

    def body(*refs):
        q, kc, vc, kp, vp = (refs[i * hs:(i + 1) * hs] for i in range(5))
        wq_ref, wk_ref, o_ref, spread = refs[5 * hs:]
        has_prev = pl.program_id(0) > 0
        lane = lax.broadcasted_iota(jnp.int32, (DIL_SPAN, LANES), 1)
        nb = min(d, RESIDUE_BATCH)
        for r0 in range(0, d, nb):
            take = lambda group_refs: jnp.concatenate([_take_residues(ref, d, r0, nb) for ref in group_refs])
            slope = jnp.concatenate([jnp.broadcast_to(_head_slope(group, pl.program_id(1) * hs + j) * float(d),
                                                      (nb, DIL_SPAN, LANES)) for j in range(hs)])
            qn = _rms64(take(q), wq_ref[...])
            kcn = _rms64(take(kc), wk_ref[...])
            kpn = _rms64(take(kp), wk_ref[...])
            sp, sc = _band_logits(qn, kpn, kcn, slope, has_prev)
            m = jnp.maximum(jnp.max(sp, axis=-1, keepdims=True), jnp.max(sc, axis=-1, keepdims=True))
            pp = jnp.exp(sp - m)
            pc = jnp.exp(sc - m)
            l = jnp.sum(pp, axis=-1, keepdims=True) + jnp.sum(pc, axis=-1, keepdims=True)
            o = (_bdot(pp, take(vp)) + _bdot(pc, take(vc))) / l
            res = jnp.where(lane < DIL_DH, o, m + jnp.log(l))
            for j in range(hs):
                _put_residues(spread, res[j * nb:(j + 1) * nb], d, r0)
                if r0 + nb == d:
                    o_ref[:, j * LANES:(j + 1) * LANES] = spread[...]

    cur = lambda part, j: pl.BlockSpec((span, LANES), lambda i, h: (i, part * DIL_HEADS + h * hs + j))
    prv = lambda part, j: pl.BlockSpec((span, LANES), lambda i, h: (jnp.maximum(i - 1, 0), part * DIL_HEADS + h * hs + j))
    one = pl.BlockSpec((1, LANES), lambda i, h: (0, 0))
    heads = range(hs)
    in_specs = ([cur(0, j) for j in heads] + [cur(1, j) for j in heads] + [cur(2, j) for j in heads]
                + [prv(1, j) for j in heads] + [prv(2, j) for j in heads] + [one, one])
    return pl.pallas_call(
        body,
        grid=(nsb, DIL_HEADS // hs),
        in_specs=in_specs,
        out_specs=pl.BlockSpec((span, hs * LANES), lambda i, h: (i, h)),
        out_shape=jax.ShapeDtypeStruct((t, DIL_HEADS * LANES), F32),
        scratch_shapes=[pltpu.VMEM((span, LANES), F32)],
        compiler_params=_cparams(("parallel", "parallel")),
        name=name,
    )(*([slab] * (5 * hs)), wq_row, wk_row)


def _dil_attn_bwd_strided(slab, stat, wq_row, wk_row, dwq_in, dwk_in, *, group, name):
    d = DIL_GROUPS[group][1]
    t = slab.shape[0]
    span = DIL_SPAN * d
    nsb = t // span

    hs = max(1, RESIDUE_BATCH // d)

    def body(*refs):
        q_refs, kc_refs, vc_refs, kp_refs, vp_refs, st_refs = (refs[i * hs:(i + 1) * hs] for i in range(6))
        wq_ref, wk_ref, dwq_in_ref, dwk_in_ref, d_ref, dwq_ref, dwk_ref, dk_carry, dv_carry, spread = refs[6 * hs:]
        take = lambda group_refs: jnp.concatenate([_take_residues(ref, d) for ref in group_refs])
        step = pl.program_id(1)
        has_prev = step < nsb - 1
        first = (pl.program_id(0) == 0) & (step == 0)

        @pl.when(step == 0)
        def _():
            dk_carry[...] = jnp.zeros_like(dk_carry)
            dv_carry[...] = jnp.zeros_like(dv_carry)

        @pl.when(first)
        def _():
            dwq_ref[...] = dwq_in_ref[...]
            dwk_ref[...] = dwk_in_ref[...]

        lane = lax.broadcasted_iota(jnp.int32, (DIL_SPAN, LANES), 1)
        scale = DIL_DH ** -0.5
        q_raw = take(q_refs)
        kc_raw = take(kc_refs)
        vc = take(vc_refs)
        kp_raw = take(kp_refs)
        vp = take(vp_refs)
        st = take(st_refs)
        slope = jnp.concatenate([jnp.broadcast_to(_head_slope(group, pl.program_id(0) * hs + j) * float(d),
                                                  (d, DIL_SPAN, LANES)) for j in range(hs)])
        d_o = jnp.where(lane < DIL_DH, st, 0.0)
        lse = jnp.sum(jnp.where(lane == DIL_DH, st, 0.0), axis=-1, keepdims=True)
        delta = jnp.sum(jnp.where(lane == DIL_DH + 1, st, 0.0), axis=-1, keepdims=True)
        qn = _rms64(q_raw, wq_ref[...])
        kc = _rms64(kc_raw, wk_ref[...])
        kp = _rms64(kp_raw, wk_ref[...])
        sp, sc = _band_logits(qn, kp, kc, slope, has_prev)
        pp = jnp.exp(sp - lse)
        pc = jnp.exp(sc - lse)
        dsp = pp * (_bdot_nt(d_o, vp) - delta) * scale
        dsc = pc * (_bdot_nt(d_o, vc) - delta) * scale
        dqn = _bdot(dsp, kp) + _bdot(dsc, kc)
        dkc_n = _bdot_tn(dsc, qn) + dk_carry[...]
        dvc = _bdot_tn(pc, d_o) + dv_carry[...]
        dk_carry[...] = _bdot_tn(dsp, qn)
        dv_carry[...] = _bdot_tn(pp, d_o)
        dq_raw, dwq_rows = _rms64_bwd(q_raw, wq_ref[...], dqn)
        dk_raw, dwk_rows = _rms64_bwd(kc_raw, wk_ref[...], dkc_n)
        for part, val in enumerate((dq_raw, dk_raw, dvc)):
            for j in range(hs):
                _put_residues(spread, val[j * d:(j + 1) * d], d)
                d_ref[part, :, j * LANES:(j + 1) * LANES] = spread[...].astype(d_ref.dtype)
        dwq_ref[...] += jnp.sum(jnp.sum(dwq_rows, axis=0), axis=0, keepdims=True)
        dwk_ref[...] += jnp.sum(jnp.sum(dwk_rows, axis=0), axis=0, keepdims=True)

    at = lambda i: nsb - 1 - i
    cur = lambda part, j: pl.BlockSpec((span, LANES), lambda h, i: (at(i), part * DIL_HEADS + h * hs + j))
    prv = lambda part, j: pl.BlockSpec((span, LANES), lambda h, i: (jnp.maximum(at(i) - 1, 0), part * DIL_HEADS + h * hs + j))
    one = pl.BlockSpec((1, LANES), lambda h, i: (0, 0))
    heads = range(hs)
    in_specs = ([cur(0, j) for j in heads] + [cur(1, j) for j in heads] + [cur(2, j) for j in heads]
                + [prv(1, j) for j in heads] + [prv(2, j) for j in heads] + [cur(0, j) for j in heads] + [one] * 4)
    return pl.pallas_call(
        body,
        grid=(DIL_HEADS // hs, nsb),
        in_specs=in_specs,
        out_specs=[pl.BlockSpec((3, span, hs * LANES), lambda h, i: (0, at(i), h)), one, one],
        out_shape=[jax.ShapeDtypeStruct((3, t, DIL_HEADS * LANES), MM_DTYPE), jax.ShapeDtypeStruct((1, LANES), F32),
                   jax.ShapeDtypeStruct((1, LANES), F32)],
        scratch_shapes=[pltpu.VMEM((hs * d, DIL_SPAN, LANES), F32), pltpu.VMEM((hs * d, DIL_SPAN, LANES), F32),
                        pltpu.VMEM((span, LANES), F32)],
        compiler_params=_cparams(("arbitrary", "arbitrary")),
        name=name,
    )(*([slab] * (5 * hs)), *([stat] * hs), wq_row, wk_row, dwq_in, dwk_in)


def _dil_merge_fwd(oe, *, name):
    t = oe[0].shape[0]
    tb = min(t, ROW_TILE)
    wide = DIL_HEADS * LANES

    def body(e0, e1, e2, y_ref, om_ref):
        lane = lax.broadcasted_iota(jnp.int32, (tb, LANES), 1)
        for h in range(DIL_HEADS):
            es = [_head_tiles(e, h) for e in (e0, e1, e2)]
            lse = [jnp.sum(jnp.where(lane == DIL_DH, e, 0.0), axis=-1, keepdims=True) for e in es]
            top = jnp.maximum(jnp.maximum(lse[0], lse[1]), lse[2])
            joint = top + jnp.log(jnp.exp(lse[0] - top) + jnp.exp(lse[1] - top) + jnp.exp(lse[2] - top))
            o = sum(jnp.exp(l - joint) * e for l, e in zip(lse, es))
            y_ref[:, h * LANES:(h + 1) * LANES] = jnp.where(lane < DIL_DH, o, 0.0).astype(y_ref.dtype)
            om_ref[:, h * LANES:(h + 1) * LANES] = jnp.where(lane < DIL_DH, o, joint)

    blk = pl.BlockSpec((tb, wide), lambda i: (i, 0))
    return pl.pallas_call(
        body,
        grid=(t // tb,),
        in_specs=[blk] * 3,
        out_specs=[blk, blk],
        out_shape=[jax.ShapeDtypeStruct((t, wide), MM_DTYPE), jax.ShapeDtypeStruct((t, wide), F32)],
        compiler_params=_cparams(("parallel",)),
        name=name,
    )(*oe)


def _dil_merge_bwd(dy, om, *, name):
    t = dy.shape[0]
    tb = min(t, ROW_TILE)
    wide = DIL_HEADS * LANES

    def body(dy_ref, om_ref, st_ref):
        lane = lax.broadcasted_iota(jnp.int32, (tb, LANES), 1)
        for h in range(DIL_HEADS):
            d_o = jnp.where(lane < DIL_DH, _head_tiles(dy_ref, h), 0.0)
            om_t = _head_tiles(om_ref, h)
            delta = jnp.sum(d_o * om_t, axis=-1, keepdims=True)
            st_ref[:, h * LANES:(h + 1) * LANES] = jnp.where(
                lane < DIL_DH, d_o, jnp.where(lane == DIL_DH, om_t, jnp.where(lane == DIL_DH + 1, delta, 0.0)))

    blk = pl.BlockSpec((tb, wide), lambda i: (i, 0))
    return pl.pallas_call(
        body,
        grid=(t // tb,),
        in_specs=[blk, blk],
        out_specs=blk,
        out_shape=jax.ShapeDtypeStruct((t, wide), F32),
        compiler_params=_cparams(("parallel",)),
        name=name,
    )(dy, om)


def _rms64_bwd(x, w_row, dy):
    r = lax.rsqrt(jnp.sum(x * x, axis=-1, keepdims=True) * (1.0 / DIL_DH) + RMS_EPS)
    gw = dy * w_row
    dx = r * gw - x * (r * r * r * jnp.sum(gw * x, axis=-1, keepdims=True) * (1.0 / DIL_DH))
    return dx, dy * x * r


def _dil_attn_bwd(slab, stat, wq_row, wk_row, dwq_in, dwk_in, *, group, name):
    dilation = DIL_GROUPS[group][1]
    t = slab.shape[0]
    rows = t // dilation
    nlb = rows // DIL_SPAN
    wide = DIL_HEADS * LANES
    view = slab.reshape(rows, dilation * DIL_SLAB)
    stat_view = stat.reshape(rows, dilation * wide)

    def body(cur_ref, kp_ref, vp_ref, st_ref, wq_ref, wk_ref, dwq_in_ref, dwk_in_ref, d_ref, dwq_ref, dwk_ref,
             dk_carry, dv_carry):
        step = pl.program_id(1)
        has_prev = step < nlb - 1
        first = (pl.program_id(0) == 0) & (step == 0)

        @pl.when(step == 0)
        def _():
            dk_carry[...] = jnp.zeros_like(dk_carry)
            dv_carry[...] = jnp.zeros_like(dv_carry)

        @pl.when(first)
        def _():
            dwq_ref[...] = dwq_in_ref[...]
            dwk_ref[...] = dwk_in_ref[...]

        lane = lax.broadcasted_iota(jnp.int32, (DIL_SPAN, LANES), 1)
        scale = DIL_DH ** -0.5
        q_raw = _stack_heads(cur_ref)
        kc_raw = _stack_heads(cur_ref, first=DIL_HEADS)
        vc = _stack_heads(cur_ref, first=2 * DIL_HEADS)
        kp_raw = _stack_heads(kp_ref)
        vp = _stack_heads(vp_ref)
        st = _stack_heads(st_ref)
        d_o = jnp.where(lane < DIL_DH, st, 0.0)
        lse = jnp.sum(jnp.where(lane == DIL_DH, st, 0.0), axis=-1, keepdims=True)
        delta = jnp.sum(jnp.where(lane == DIL_DH + 1, st, 0.0), axis=-1, keepdims=True)
        qn = _rms64(q_raw, wq_ref[...])
        kc = _rms64(kc_raw, wk_ref[...])
        kp = _rms64(kp_raw, wk_ref[...])
        sp, sc = _band_logits(qn, kp, kc, _alibi_slopes(group) * float(dilation), has_prev)
        pp = jnp.exp(sp - lse)
        pc = jnp.exp(sc - lse)
        dsp = pp * (_bdot_nt(d_o, vp) - delta) * scale
        dsc = pc * (_bdot_nt(d_o, vc) - delta) * scale
        dqn = _bdot(dsp, kp) + _bdot(dsc, kc)
        dkc_n = _bdot_tn(dsc, qn) + _stack_heads(dk_carry)
        dvc = _bdot_tn(pc, d_o) + _stack_heads(dv_carry)
        _store_heads(dk_carry, _bdot_tn(dsp, qn))
        _store_heads(dv_carry, _bdot_tn(pp, d_o))
        dq_raw, dwq_rows = _rms64_bwd(q_raw, wq_ref[...], dqn)
        dk_raw, dwk_rows = _rms64_bwd(kc_raw, wk_ref[...], dkc_n)
        _store_heads(d_ref, dq_raw)
        _store_heads(d_ref, dk_raw, first=DIL_HEADS)
        _store_heads(d_ref, dvc, first=2 * DIL_HEADS)
        dwq_ref[...] += jnp.sum(jnp.sum(dwq_rows, axis=0), axis=0, keepdims=True)
        dwk_ref[...] += jnp.sum(jnp.sum(dwk_rows, axis=0), axis=0, keepdims=True)

    blk_i = lambda i: nlb - 1 - i
    cur = pl.BlockSpec((DIL_SPAN, DIL_SLAB), lambda r, i: (blk_i(i), r))
    prv = lambda part: pl.BlockSpec((DIL_SPAN, wide), lambda r, i: (jnp.maximum(blk_i(i) - 1, 0), 3 * r + part))
    one = pl.BlockSpec((1, LANES), lambda r, i: (0, 0))
    dslab, dwq, dwk = pl.pallas_call(
        body,
        grid=(dilation, nlb),
        in_specs=[cur, prv(1), prv(2), pl.BlockSpec((DIL_SPAN, wide), lambda r, i: (blk_i(i), r)), one, one, one, one],
        out_specs=[cur, one, one],
        out_shape=[jax.ShapeDtypeStruct((rows, dilation * DIL_SLAB), MM_DTYPE), jax.ShapeDtypeStruct((1, LANES), F32),
                   jax.ShapeDtypeStruct((1, LANES), F32)],
        scratch_shapes=[pltpu.VMEM((DIL_SPAN, wide), F32), pltpu.VMEM((DIL_SPAN, wide), F32)],
        compiler_params=_cparams(("arbitrary", "arbitrary")),
        name=name,
    )(view, view, view, stat_view, wq_row, wk_row, dwq_in, dwk_in)
    return dslab.reshape(t, DIL_SLAB), dwq, dwk


def _row(v, width=LANES):
    v = v.astype(F32).reshape(-1)
    return jnp.pad(v, (0, width - v.shape[0])).reshape(1, width)


def _prepare_weights(w):
    return dict(gdn=_prepare_gdn(w), dil=_prepare_dil(w), ffn=_prepare_ffn(w))


def _prepare_gdn(w, layers=range(DEPTH // 2)):
    gdn = {}
    for j in layers:
        wt = w["gdn_w_in"][j]
        gates_t = jnp.pad(wt[GDN_MAIN:], ((0, LANES - 2 * GDN_HEADS), (0, 0)))
        gdn[j] = dict(in_t=wt, gates_t=gates_t, out=w["gdn_w_out"][j], conv=w["gdn_conv_w"][j].astype(F32),
                      alog=_row(w["gdn_a_log"][j]), dt=_row(w["gdn_dt_bias"][j]), nw=_row(w["gdn_norm_w"][j]))
    return gdn


def _prepare_dil(w, layers=range(DEPTH // 2)):
    d = D_MODEL
    dil = {}
    for j in layers:
        wt = w["dil_w_in"][j].reshape(3, len(DIL_GROUPS), DIL_HEADS, DIL_DH, d)
        wg_t = [wt[:, g].reshape(DIL_SLAB // 2, d) for g in range(len(DIL_GROUPS))]
        out_t = jnp.pad(w["dil_w_out"][j].reshape(d, DIL_HEADS, DIL_DH), ((0, 0), (0, 0), (0, LANES - DIL_DH)))
        dil[j] = dict(wg_t=wg_t, out_t=out_t.reshape(d, DIL_HEADS * LANES), wq=_row(w["dil_q_norm"][j]),
                      wk=_row(w["dil_k_norm"][j]))
    return dil


def _prepare_ffn(w, layers=range(DEPTH)):
    return {i: dict(in_t=w["ffn_w_in"][i], out=w["ffn_w_out"][i]) for i in layers}


def _gdn_layer_fwd(x, nrow, p):
    hn = _rmsnorm_fwd(x, nrow, name="rmsnorm_fwd")
    pm = _matmul(hn, p["in_t"], trans_b=True, b_rows=(0, GDN_MAIN), name="gdn_proj_main")
    ab = _matmul(hn, p["gates_t"], trans_b=True, name="gdn_proj_gates")
    qkv = _gdn_conv_fwd(pm, p["conv"], name="gdn_conv_fwd")
    gb, bb = _gdn_gates_fwd(ab, p["alog"], p["dt"], name="gdn_gates_fwd")
    prep, rest = _gdn_prep_fwd(qkv, gb, bb, name="gdn_prep_fwd")
    o, states = _gdn_scan_fwd(prep, name="gdn_scan_fwd")
    og = _gdn_outgate_fwd(o, pm, p["nw"], name="gdn_outgate_fwd")
    y = _matmul(og, p["out"], add=x, name="gdn_proj_out")
    return y, (x, hn, pm, ab, qkv, gb, bb, prep, rest, states, o, og)


def _gdn_layer_bwd(dx, dxb, nrow, p, saved):
    x, hn, pm, ab, qkv, gb, bb, prep, rest, states, o, og = saved
    d_og = _matmul(dxb, p["out"], trans_b=True, name="gdn_dgate")
    g_out = _matmul(og, dxb, trans_a=True, out_dtype=MM_DTYPE, name="gdn_gw_out")
    d_o, d_z, d_nw = _gdn_outgate_bwd(o, pm, p["nw"], d_og, name="gdn_outgate_bwd")
    cts = _gdn_scan_bwd(prep, states, d_o, name="gdn_scan_bwd")
    dqkv, dgb, dbb = _gdn_prep_bwd(qkv, gb, bb, rest, cts, name="gdn_prep_bwd")
    d_ab, d_alog, d_dt = _gdn_gates_bwd(ab, p["alog"], p["dt"], dgb, dbb, name="gdn_gates_bwd")
    d_conv, g_conv = _gdn_conv_bwd(pm, p["conv"], dqkv, name="gdn_conv_bwd")
    d_hn = _matmul(d_conv, p["in_t"], b_rows=(0, GDN_QKV), name="gdn_dhn_qkv")
    d_hn = _matmul(d_z, p["in_t"], b_rows=(GDN_QKV, GDN_MAIN - GDN_QKV), add=d_hn, name="gdn_dhn_z")
    dx_new, dxb_new, g_norm = _matmul(d_ab, p["gates_t"], add=d_hn, norm_bwd=(x, nrow, dx), name="gdn_dhn_gates_norm")
    g_in_t = jnp.concatenate([
        _matmul(d_conv, hn, trans_a=True, out_dtype=MM_DTYPE, name="gdn_gw_qkv"),
        _matmul(d_z, hn, trans_a=True, out_dtype=MM_DTYPE, name="gdn_gw_z"),
        _matmul(d_ab, hn, trans_a=True, out_dtype=MM_DTYPE, name="gdn_gw_gates")[:2 * GDN_HEADS],
    ], axis=0)
    grads = dict(w_in=g_in_t, conv=g_conv, a_log=d_alog[0, :GDN_HEADS], dt_bias=d_dt[0, :GDN_HEADS], norm_w=d_nw[0],
                 w_out=g_out, norm=g_norm[0])
    return dx_new, dxb_new, grads


def _dil_layer_fwd(x, nrow, p):
    hn = _rmsnorm_fwd(x, nrow, name="rmsnorm_fwd")
    slabs = [_matmul(hn, p["wg_t"][g], trans_b=True, spread_out=True, name="dil_proj_in") for g in range(len(DIL_GROUPS))]
    oe = [(_dil_attn_fwd if DIL_GROUPS[g][1] == 1 else _dil_attn_fwd_strided)(
        slabs[g], p["wq"], p["wk"], group=g, name=f"dil_attn_fwd_g{g}") for g in range(len(DIL_GROUPS))]
    y, om = _dil_merge_fwd(oe, name="dil_merge_fwd")
    out = _matmul(y, p["out_t"], trans_b=True, add=x, name="dil_proj_out")
    return out, (x, hn, slabs, y, om)


def _dil_layer_bwd(dx, dxb, nrow, p, saved):
    x, hn, slabs, y, om = saved
    d_y = _matmul(dxb, p["out_t"], name="dil_dmerged")
    g_out_t = _matmul(dxb, y, trans_a=True, out_dtype=MM_DTYPE, name="dil_gw_out")
    g_out_t = g_out_t.reshape(D_MODEL, DIL_HEADS, LANES)[..., :DIL_DH].reshape(D_MODEL, DIL_HEADS * DIL_DH)
    stat = _dil_merge_bwd(d_y, om, name="dil_merge_bwd")
    d_hn = None
    dwq = jnp.zeros((1, LANES), F32)
    dwk = jnp.zeros((1, LANES), F32)
    g_groups = []
    wide = DIL_HEADS * LANES
    for g in range(len(DIL_GROUPS)):
        last = dict(norm_bwd=(x, nrow, dx)) if g == len(DIL_GROUPS) - 1 else {}
        if DIL_GROUPS[g][1] == 1:
            dslab, dwq, dwk = _dil_attn_bwd(slabs[g], stat, p["wq"], p["wk"], dwq, dwk, group=g, name=f"dil_attn_bwd_g{g}")
            d_hn = _matmul(dslab, p["wg_t"][g], packed_a=True, add=d_hn, name="dil_dhn", **last)
            g_w = _matmul(dslab, hn, trans_a=True, packed_a=True, out_dtype=MM_DTYPE, name="dil_gw_in")
        else:
            dparts, dwq, dwk = _dil_attn_bwd_strided(slabs[g], stat, p["wq"], p["wk"], dwq, dwk, group=g,
                                                     name=f"dil_attn_bwd_g{g}")
            d_hn = _matmul(dparts, p["wg_t"][g], a_lead="k", packed_a=True, add=d_hn,
                           name="dil_dhn_parts_norm" if last else "dil_dhn_parts", **last)
            g_w = _matmul(dparts, hn, trans_a=True, a_lead="i", packed_a=True, out_dtype=MM_DTYPE, name="dil_gw_in_parts")
        g_groups.append(g_w.reshape(3, DIL_HEADS, DIL_DH, D_MODEL))
    g_in_t = jnp.stack(g_groups, axis=1).reshape(3 * len(DIL_GROUPS) * DIL_HEADS * DIL_DH, D_MODEL)
    dx_new, dxb_new, g_norm = d_hn
    grads = dict(w_in=g_in_t, q_norm=dwq[0, :DIL_DH], k_norm=dwk[0, :DIL_DH], w_out=g_out_t, norm=g_norm[0])
    return dx_new, dxb_new, grads


def _ffn_layer_fwd(x, nrow, p):
    hn = _rmsnorm_fwd(x, nrow, name="rmsnorm_fwd")
    gate, up, act = _ffn_in(hn, p["in_t"], name="ffn_proj_in")
    y = _matmul(act, p["out"], add=x, name="ffn_proj_out")
    return y, (x, hn, gate, up, act)


def _ffn_layer_bwd(dx, dxb, nrow, p, saved):
    x, hn, gate, up, act = saved
    g_out = _matmul(act, dxb, trans_a=True, out_dtype=MM_DTYPE, name="ffn_gw_out")
    d_gu = _ffn_dact(dxb, p["out"], gate, up, name="ffn_dact")
    dx_new, dxb_new, g_norm = _matmul(d_gu, p["in_t"], a_lead="k", norm_bwd=(x, nrow, dx), name="ffn_dhn_norm")
    g_in_t = _matmul(d_gu, hn, trans_a=True, a_lead="i", out_dtype=MM_DTYPE, name="ffn_gw_in")
    return dx_new, dxb_new, dict(w_in=g_in_t, w_out=g_out, norm=g_norm[0])


def _mixer_fwd(i, x, mix_row, prepared):
    if i % 2 == 0:
        return _gdn_layer_fwd(x, mix_row, prepared["gdn"][i // 2])
    return _dil_layer_fwd(x, mix_row, prepared["dil"][i // 2])


def _mixer_bwd(i, dx, dxb, mix_row, prepared, saved, zero=0.0):
    if i % 2 == 0:
        p = prepared["gdn"][i // 2]
        return _gdn_layer_bwd(dx, dxb, mix_row, dict(p, nw=p["nw"] + zero), saved)
    p = prepared["dil"][i // 2]
    return _dil_layer_bwd(dx, dxb, mix_row, dict(p, wq=p["wq"] + zero), saved)


def _local_step(x, target, prepared, norm_mix, norm_ffn):
    saved = []
    for i in range(DEPTH):
        x, s_mix = _mixer_fwd(i, x, norm_mix[i].reshape(1, D_MODEL), prepared)
        x, s_ffn = _ffn_layer_fwd(x, norm_ffn[i].reshape(1, D_MODEL), prepared["ffn"][i])
        saved.append((s_mix, s_ffn))
    dx, dxb, loss = _loss_head(x, target, name="loss_head")
    g_mix, g_ffn = [None] * DEPTH, [None] * DEPTH
    for i in reversed(range(DEPTH)):
        s_mix, s_ffn = saved[i]
        dx, dxb, g_ffn[i] = _ffn_layer_bwd(dx, dxb, norm_ffn[i].reshape(1, D_MODEL), prepared["ffn"][i], s_ffn)
        dx, dxb, g_mix[i] = _mixer_bwd(i, dx, dxb, norm_mix[i].reshape(1, D_MODEL), prepared, s_mix)
    return loss[0, 0], dx, _collect_grads(g_mix, g_ffn)


def _collect_grads(g_mix, g_ffn):
    gdn = [g_mix[i] for i in range(0, DEPTH, 2)]
    dil = [g_mix[i] for i in range(1, DEPTH, 2)]
    if any(g is None for g in g_mix + g_ffn):
        pick = lambda gs, key: [None if g is None else g[key] for g in gs]
        return dict(gdn_w_in=pick(gdn, "w_in"), gdn_w_out=pick(gdn, "w_out"), dil_w_in=pick(dil, "w_in"),
                    dil_w_out=pick(dil, "w_out"), ffn_w_in=pick(g_ffn, "w_in"), ffn_w_out=pick(g_ffn, "w_out"))
    grads = dict(
        norm_mix=jnp.stack([g["norm"] for g in g_mix]),
        norm_ffn=jnp.stack([g["norm"] for g in g_ffn]),
        gdn_w_in=[g["w_in"] for g in gdn],
        gdn_conv_w=jnp.stack([g["conv"] for g in gdn]),
        gdn_a_log=jnp.stack([g["a_log"] for g in gdn]),
        gdn_dt_bias=jnp.stack([g["dt_bias"] for g in gdn]),
        gdn_norm_w=jnp.stack([g["norm_w"] for g in gdn]),
        gdn_w_out=[g["w_out"] for g in gdn],
        dil_w_in=[g["w_in"] for g in dil],
        dil_q_norm=jnp.stack([g["q_norm"] for g in dil]),
        dil_k_norm=jnp.stack([g["k_norm"] for g in dil]),
        dil_w_out=[g["w_out"] for g in dil],
        ffn_w_in=[g["w_in"] for g in g_ffn],
        ffn_w_out=[g["w_out"] for g in g_ffn],
    )
    return grads


MESH_ID = pl.DeviceIdType.MESH
ANY_SPACE = pl.BlockSpec(memory_space=pl.ANY)


def _mesh_position():
    return lax.axis_index("x"), lax.axis_index("y"), lax.axis_index("c")


def _flip(pos, k):
    x, y, c = pos
    return (1 - x if k & 4 else x, 1 - y if k & 2 else y, 1 - c if k & 1 else c)


def _linear(pos):
    return 4 * pos[0] + 2 * pos[1] + pos[2]


def _comm_scratch():
    return [pltpu.SemaphoreType.DMA((N_DEV - 1,)), pltpu.SemaphoreType.DMA((N_DEV - 1,)), pltpu.SemaphoreType.DMA(())]


def _all_gather(shard, *, name):
    def body(x_ref, out_ref, send_sems, recv_sems, local_sem):
        me = _mesh_position()
        mine = out_ref.at[_linear(me)]
        local = pltpu.make_async_copy(x_ref, mine, local_sem)
        local.start()
        copies = []
        for k in range(1, N_DEV):
            cp = pltpu.make_async_remote_copy(src_ref=x_ref, dst_ref=mine, send_sem=send_sems.at[k - 1],
                                              recv_sem=recv_sems.at[k - 1], device_id=_flip(me, k), device_id_type=MESH_ID)
            cp.start()
            copies.append(cp)
        for cp in copies:
            cp.wait()
        local.wait()

    return pl.pallas_call(
        body,
        out_shape=jax.ShapeDtypeStruct((N_DEV,) + shard.shape, shard.dtype),
        in_specs=[ANY_SPACE],
        out_specs=ANY_SPACE,
        scratch_shapes=_comm_scratch(),
        name=name,
    )(shard)


def _exchange(parts, *, name):
    def body(p_ref, out_ref, send_sems, recv_sems, local_sem):
        me = _mesh_position()
        mine = out_ref.at[_linear(me)]
        local = pltpu.make_async_copy(p_ref.at[_linear(me)], mine, local_sem)
        local.start()
        copies = []
        for k in range(1, N_DEV):
            peer = _flip(me, k)
            cp = pltpu.make_async_remote_copy(src_ref=p_ref.at[_linear(peer)], dst_ref=mine, send_sem=send_sems.at[k - 1],
                                              recv_sem=recv_sems.at[k - 1], device_id=peer, device_id_type=MESH_ID)
            cp.start()
            copies.append(cp)
        for cp in copies:
            cp.wait()
        local.wait()

    return pl.pallas_call(
        body,
        out_shape=jax.ShapeDtypeStruct(parts.shape, parts.dtype),
        in_specs=[ANY_SPACE],
        out_specs=ANY_SPACE,
        scratch_shapes=_comm_scratch(),
        name=name,
    )(parts)


HBM_SPACE = pl.BlockSpec(memory_space=pltpu.HBM)
SEM_SPACE = pl.BlockSpec(memory_space=pltpu.SEMAPHORE)
DATAFLOW = pltpu.SideEffectType.DATAFLOW_SIDE_EFFECTING


def _split_copies(src_ref, land_ref, send_sems, recv_sems, per_peer):
    me = _mesh_position()
    mine = land_ref.at[_linear(me)]
    copies = []
    for k in range(1, N_DEV):
        peer = _flip(me, k)
        src = src_ref.at[_linear(peer)] if per_peer else src_ref
        copies.append(pltpu.make_async_remote_copy(src_ref=src, dst_ref=mine, send_sem=send_sems.at[k - 1],
                                                   recv_sem=recv_sems.at[k - 1], device_id=peer, device_id_type=MESH_ID))
    return copies


def _travel_start(src, after, *, per_peer, name):
    me = _linear(_mesh_position())
    own = src[me] if per_peer else src
    shape = own.shape
    landing = lax.dynamic_update_slice(lax.empty((N_DEV,) + shape, src.dtype), own[None], (me, 0, 0))

    def body(src_ref, land_ref, after_ref, send_sems, recv_sems, src_thru, land_thru, token):
        for cp in _split_copies(src_ref, land_ref, send_sems, recv_sems, per_peer):
            cp.start()
        token[...] = jnp.zeros_like(token)

    return pl.pallas_call(
        body,
        name=name,
        out_shape=(pltpu.SemaphoreType.DMA((N_DEV - 1,)), pltpu.SemaphoreType.DMA((N_DEV - 1,)),
                   pltpu.HBM(src.shape, src.dtype), pltpu.HBM(landing.shape, landing.dtype),
                   jax.ShapeDtypeStruct((8, LANES), F32)),
        in_specs=(HBM_SPACE, HBM_SPACE, ANY_SPACE),
        out_specs=(SEM_SPACE, SEM_SPACE, HBM_SPACE, HBM_SPACE, pl.BlockSpec(memory_space=pltpu.VMEM)),
        input_output_aliases={0: 2, 1: 3},
        compiler_params=pltpu.CompilerParams(has_side_effects=DATAFLOW),
    )(pltpu.with_memory_space_constraint(src, pltpu.HBM), pltpu.with_memory_space_constraint(landing, pltpu.HBM), after)


def _travel_wait(started, after, *, per_peer, name):
    send_sems, recv_sems, src_thru, land_thru, _ = started

    def body(src_ref, land_ref, send_sems, recv_sems, after_ref, src_dead, got_ref):
        for cp in _split_copies(src_ref, land_ref, send_sems, recv_sems, per_peer):
            cp.wait_send()
            cp.wait_recv()

    return pl.pallas_call(
        body,
        name=name,
        out_shape=(pltpu.HBM(src_thru.shape, src_thru.dtype), pltpu.HBM(land_thru.shape, land_thru.dtype)),
        in_specs=(HBM_SPACE, HBM_SPACE, SEM_SPACE, SEM_SPACE, ANY_SPACE),
        out_specs=(HBM_SPACE, HBM_SPACE),
        input_output_aliases={0: 0, 1: 1},
        compiler_params=pltpu.CompilerParams(has_side_effects=DATAFLOW),
    )(src_thru, land_thru, send_sems, recv_sems, after)[1]


def _adamw(parts, w, m, v, *, name):
    rows, n = w.shape
    tb = _pick(rows, (PACK_ROW_ALIGN, 16))
    c1 = 1.0 - ADAM_B1 ** ADAM_STEP
    c2 = 1.0 - ADAM_B2 ** ADAM_STEP

    def body(p_ref, w_ref, m_ref, v_ref, g_ref, d_ref, nm_ref, nv_ref):
        g = p_ref[0].astype(F32)
        for s in range(1, N_DEV):
            g = g + p_ref[s].astype(F32)
        m_new = ADAM_B1 * m_ref[...] + (1.0 - ADAM_B1) * g
        v_new = ADAM_B2 * v_ref[...] + (1.0 - ADAM_B2) * (g * g)
        m_hat = m_new / c1
        v_hat = v_new / c2
        g_ref[...] = g
        nm_ref[...] = m_new
        nv_ref[...] = v_new
        d_ref[...] = -ADAM_LR * (m_hat / (jnp.sqrt(v_hat) + ADAM_EPS) + ADAM_WD * w_ref[...])

    blk = pl.BlockSpec((tb, n), lambda i: (i, 0))
    return pl.pallas_call(
        body,
        grid=(rows // tb,),
        in_specs=[pl.BlockSpec((N_DEV, tb, n), lambda i: (0, i, 0)), blk, blk, blk],
        out_specs=[blk] * 4,
        out_shape=[jax.ShapeDtypeStruct((rows, n), F32)] * 4,
        compiler_params=_cparams(("parallel",)),
        name=name,
    )(parts, w, m, v)


PACK_WIDTH = 1024
SHARDED = {
    "gdn_w_in": ((2, D_MODEL, GDN_IN_WIDTH), 2),
    "gdn_conv_w": ((2, GDN_CONV, GDN_QKV), 2),
    "gdn_w_out": ((2, GDN_HEADS * GDN_DV, D_MODEL), 1),
    "dil_w_in": ((2, D_MODEL, 3 * len(DIL_GROUPS) * DIL_HEADS * DIL_DH), 2),
    "dil_w_out": ((2, DIL_HEADS * DIL_DH, D_MODEL), 2),
    "ffn_w_in": ((DEPTH, D_MODEL, 2 * FFN_HIDDEN), 2),
    "ffn_w_out": ((DEPTH, FFN_HIDDEN, D_MODEL), 1),
}
REPLICATED = {"norm_mix": (DEPTH, D_MODEL), "norm_ffn": (DEPTH, D_MODEL), "gdn_a_log": (2, GDN_HEADS),
              "gdn_dt_bias": (2, GDN_HEADS), "gdn_norm_w": (2, GDN_DV), "dil_q_norm": (2, DIL_DH), "dil_k_norm": (2, DIL_DH)}
WEIGHT_ORDER = ("norm_mix", "norm_ffn", "gdn_w_in", "gdn_conv_w", "gdn_a_log", "gdn_dt_bias", "gdn_norm_w", "gdn_w_out",
                "dil_w_in", "dil_q_norm", "dil_k_norm", "dil_w_out", "ffn_w_in", "ffn_w_out")
PACK_ROW_ALIGN = 128
PIECE_ALIGN = 16
SMALL_ROWS = 16


def _shard_shape(name):
    shape, axis = SHARDED[name]
    return tuple(s // N_DEV if i == axis else s for i, s in enumerate(shape))


def _shard_rows(name):
    return math.prod(_shard_shape(name)) // PACK_WIDTH


def _split_shards(full, name):
    shape, axis = SHARDED[name]
    split = full.reshape(shape[:axis] + (N_DEV, shape[axis] // N_DEV) + shape[axis + 1:])
    return jnp.moveaxis(split, axis, 0)


def _join_shards(stacked, name):
    shape, axis = SHARDED[name]
    return jnp.moveaxis(stacked, 0, axis).reshape(shape)


COLUMN_SHARDED = ("gdn_w_in", "dil_w_in", "dil_w_out", "ffn_w_in")


def _to_rows(shard, name):
    if name in COLUMN_SHARDED:
        shard = jnp.swapaxes(shard, 1, 2)
    return shard.reshape(-1, PACK_WIDTH)


def _layer_columns(name):
    _, r, c = _shard_shape(name)
    return r if name in COLUMN_SHARDED else c


def _piece_rows(piece, halves=1):
    name, layer = piece
    rows = _shard_rows(name) * halves
    return rows if layer is None else rows // SHARDED[name][0][0]


def _aligned(rows, to=PIECE_ALIGN):
    return -(-rows // to) * to


def _pack_pieces(arrays, total_align=PIECE_ALIGN):
    padded, total = [], 0
    for a in arrays:
        rows = a.shape[-2]
        extra = _aligned(rows) - rows
        if extra:
            a = jnp.pad(a, [(0, 0)] * (a.ndim - 2) + [(0, extra), (0, 0)])
        padded.append(a)
        total += rows + extra
    tail = _aligned(total, total_align) - total
    if tail:
        padded.append(jnp.zeros(padded[0].shape[:-2] + (tail, PACK_WIDTH), padded[0].dtype))
    return jnp.concatenate(padded, axis=-2)


def _piece_offsets(pieces, halves=None):
    out, at = [], 0
    for p in pieces:
        rows = _piece_rows(p, (halves or {}).get(p[0], 1))
        out.append((p, at, rows))
        at += _aligned(rows)
    return out


def _shard_piece_rows(src, piece):
    name, layer = piece
    part = src[name] if layer is None else src[name][layer:layer + 1]
    return _to_rows(part.astype(F32), name)


def _piece_from_rows(rows, piece):
    name, layer = piece
    layers, r, c = _shard_shape(name)
    n_l = layers if layer is None else 1
    if name in COLUMN_SHARDED:
        return jnp.swapaxes(rows.reshape(n_l, c, r), 1, 2)
    return rows.reshape(n_l, r, c)


SMALL_TAIL = tuple(n for n in REPLICATED if n not in ("norm_mix", "norm_ffn"))


def _pack_small(vals):
    tail, at = jnp.zeros((PACK_WIDTH,), F32), 0
    for n in SMALL_TAIL:
        vec = vals[n].astype(F32).reshape(-1)
        tail = tail + jnp.pad(vec, (at, PACK_WIDTH - at - vec.shape[0]))
        at += vec.shape[0]
    buf = jnp.pad(vals["norm_mix"].astype(F32), ((0, SMALL_ROWS - DEPTH), (0, 0)))
    buf = buf + jnp.pad(vals["norm_ffn"].astype(F32), ((8, SMALL_ROWS - 8 - DEPTH), (0, 0)))
    return buf + jnp.pad(tail.reshape(1, PACK_WIDTH), ((SMALL_ROWS - 1, 0), (0, 0)))


def _unpack_small(buf):
    out = {"norm_mix": buf[0:DEPTH], "norm_ffn": buf[8:8 + DEPTH]}
    at = 0
    for n in SMALL_TAIL:
        size = math.prod(REPLICATED[n])
        out[n] = buf[SMALL_ROWS - 1, at:at + size].reshape(REPLICATED[n])
        at += size
    return out


GATHER_FIRST = (("gdn_w_in", 0), ("gdn_conv_w", None), ("gdn_w_out", 0))
GATHER_NEXT = (("ffn_w_in", 0), ("ffn_w_out", 0), ("dil_w_in", 0), ("dil_w_out", 0))
GATHER_LAST = (("ffn_w_in", 1), ("ffn_w_out", 1), ("gdn_w_in", 1), ("gdn_w_out", 1), ("ffn_w_in", 2), ("ffn_w_out", 2),
               ("dil_w_in", 1), ("dil_w_out", 1), ("ffn_w_in", 3), ("ffn_w_out", 3))
EXCHANGE_GROUPS = (
    (("ffn_w_in", 3), ("ffn_w_out", 3), ("dil_w_in", 1), ("dil_w_out", 1),
     ("ffn_w_in", 2), ("ffn_w_out", 2), ("gdn_w_in", 1), ("gdn_w_out", 1)),
    (("ffn_w_in", 1), ("ffn_w_out", 1), ("dil_w_in", 0), ("dil_w_out", 0)),
    (("ffn_w_in", 0), ("ffn_w_out", 0)),
    (("gdn_w_in", 0), ("gdn_w_out", 0), ("gdn_conv_w", None)),
)
EXCHANGE_AFTER = {("mix", 2): 0, ("mix", 1): 1, ("ffn", 0): 2}


def _gather_operand(w, pieces):
    arrays = []
    for n, layer in pieces:
        if layer is None:
            arrays.append(lax.bitcast_convert_type(w[n], BF16).reshape(-1, PACK_WIDTH))
        else:
            arrays.append(_to_rows(w[n][layer:layer + 1].astype(BF16), n))
    return _pack_pieces(arrays)


def _gathered_weights(gathered, pieces, full):
    for (n, layer), at, rows in _piece_offsets(pieces, halves={"gdn_conv_w": 2}):
        block = gathered[:, at:at + rows]
        if layer is None:
            block = lax.bitcast_convert_type(block.reshape((N_DEV,) + _shard_shape(n) + (2,)), F32)
            full[n] = _join_shards(block, n)
        else:
            full.setdefault(n, {})[layer] = block.reshape(-1, _layer_columns(n))
    return full


def _exchange_operand(grads, pieces):
    arrays = []
    for n, layer in pieces:
        if layer is None:
            arrays.append(_split_shards(grads[n], n).astype(BF16).reshape(N_DEV, -1, PACK_WIDTH))
        else:
            arrays.append(grads[n][layer].astype(BF16).reshape(N_DEV, -1, PACK_WIDTH))
    return _pack_pieces(arrays, total_align=PACK_ROW_ALIGN)


def _update_group(received, pieces, w, m, v, *, name):
    packed = [_pack_pieces([_shard_piece_rows(src, p) for p in pieces], total_align=PACK_ROW_ALIGN) for src in (w, m, v)]
    outs = _adamw(received, *packed, name=name)
    return {p: tuple(_piece_from_rows(o[at:at + rows], p) for o in outs) for p, at, rows in _piece_offsets(pieces)}


def kernel(x, norm_mix, norm_ffn, gdn_w_in, gdn_conv_w, gdn_a_log, gdn_dt_bias, gdn_norm_w, gdn_w_out, dil_w_in, dil_q_norm, dil_k_norm, dil_w_out, ffn_w_in, ffn_w_out, loss_target, m_norm_mix, m_norm_ffn, m_gdn_w_in, m_gdn_conv_w, m_gdn_a_log, m_gdn_dt_bias, m_gdn_norm_w, m_gdn_w_out, m_dil_w_in, m_dil_q_norm, m_dil_k_norm, m_dil_w_out, m_ffn_w_in, m_ffn_w_out, v_norm_mix, v_norm_ffn, v_gdn_w_in, v_gdn_conv_w, v_gdn_a_log, v_gdn_dt_bias, v_gdn_norm_w, v_gdn_w_out, v_dil_w_in, v_dil_q_norm, v_dil_k_norm, v_dil_w_out, v_ffn_w_in, v_ffn_w_out):
    w = dict(norm_mix=norm_mix, norm_ffn=norm_ffn, gdn_w_in=gdn_w_in, gdn_conv_w=gdn_conv_w, gdn_a_log=gdn_a_log,
             gdn_dt_bias=gdn_dt_bias, gdn_norm_w=gdn_norm_w, gdn_w_out=gdn_w_out, dil_w_in=dil_w_in, dil_q_norm=dil_q_norm,
             dil_k_norm=dil_k_norm, dil_w_out=dil_w_out, ffn_w_in=ffn_w_in, ffn_w_out=ffn_w_out)
    m = dict(norm_mix=m_norm_mix, norm_ffn=m_norm_ffn, gdn_w_in=m_gdn_w_in, gdn_conv_w=m_gdn_conv_w, gdn_a_log=m_gdn_a_log,
             gdn_dt_bias=m_gdn_dt_bias, gdn_norm_w=m_gdn_norm_w, gdn_w_out=m_gdn_w_out, dil_w_in=m_dil_w_in,
             dil_q_norm=m_dil_q_norm, dil_k_norm=m_dil_k_norm, dil_w_out=m_dil_w_out, ffn_w_in=m_ffn_w_in, ffn_w_out=m_ffn_w_out)
    v = dict(norm_mix=v_norm_mix, norm_ffn=v_norm_ffn, gdn_w_in=v_gdn_w_in, gdn_conv_w=v_gdn_conv_w, gdn_a_log=v_gdn_a_log,
             gdn_dt_bias=v_gdn_dt_bias, gdn_norm_w=v_gdn_norm_w, gdn_w_out=v_gdn_w_out, dil_w_in=v_dil_w_in,
             dil_q_norm=v_dil_q_norm, dil_k_norm=v_dil_k_norm, dil_w_out=v_dil_w_out, ffn_w_in=v_ffn_w_in, ffn_w_out=v_ffn_w_out)
    def row(src, i):
        return src[i].reshape(1, D_MODEL)

    first = _all_gather(_gather_operand(w, GATHER_FIRST), name="weight_all_gather_first")
    next_started = _travel_start(_gather_operand(w, GATHER_NEXT), first, per_peer=False, name="weight_gather_start_next")
    last_started = _travel_start(_gather_operand(w, GATHER_LAST), next_started[4], per_peer=False,
                                 name="weight_gather_start_last")
    full = _gathered_weights(first, GATHER_FIRST, {n: w[n] for n in REPLICATED})
    prepared = dict(gdn=_prepare_gdn(full, layers=(0,)))
    h = x[0]
    saved = [None] * DEPTH
    h, s_mix = _mixer_fwd(0, h, row(norm_mix, 0) + last_started[4][0, 0], prepared)
    got = _travel_wait(next_started, h, per_peer=False, name="weight_gather_wait_next")
    full = _gathered_weights(got, GATHER_NEXT, full)
    prepared.update(dil=_prepare_dil(full, layers=(0,)), ffn=_prepare_ffn(full, layers=(0,)))
    for i in range(DEPTH):
        if i > 0:
            h, s_mix = _mixer_fwd(i, h, row(norm_mix, i), prepared)
        if i == 1:
            got = _travel_wait(last_started, h, per_peer=False, name="weight_gather_wait_last")
            full = _gathered_weights(got, GATHER_LAST, full)
            prepared["gdn"].update(_prepare_gdn(full, layers=(1,)))
            prepared["dil"].update(_prepare_dil(full, layers=(1,)))
            prepared["ffn"].update(_prepare_ffn(full, layers=(1, 2, 3)))
        h, s_ffn = _ffn_layer_fwd(h, row(norm_ffn, i), prepared["ffn"][i])
        saved[i] = (s_mix, s_ffn)
    dx, dxb, loss = _loss_head(h, loss_target[0], name="loss_head")

    g_mix, g_ffn = [None] * DEPTH, [None] * DEPTH
    started = {}

    def travel(group):
        operand = _exchange_operand(_collect_grads(g_mix, g_ffn), EXCHANGE_GROUPS[group])
        started[group] = _travel_start(operand, dx, per_peer=True, name=f"grad_exchange_start_{group}")
        return started[group][4][0, 0]

    zero = 0.0
    for i in reversed(range(DEPTH)):
        s_mix, s_ffn = saved[i]
        dx, dxb, g_ffn[i] = _ffn_layer_bwd(dx, dxb, row(norm_ffn, i) + zero, prepared["ffn"][i], s_ffn)
        zero = travel(EXCHANGE_AFTER[("ffn", i)]) if ("ffn", i) in EXCHANGE_AFTER else 0.0
        dx, dxb, g_mix[i] = _mixer_bwd(i, dx, dxb, row(norm_mix, i), prepared, s_mix, zero)
        zero = travel(EXCHANGE_AFTER[("mix", i)]) if ("mix", i) in EXCHANGE_AFTER else 0.0
    grads = _collect_grads(g_mix, g_ffn)
    received = [_travel_wait(started[g], dx, per_peer=True, name=f"grad_exchange_wait_{g}") for g in sorted(started)]
    received.append(_exchange(_exchange_operand(grads, EXCHANGE_GROUPS[-1]), name="grad_exchange_last"))
    updated = {}
    for g, pieces in enumerate(EXCHANGE_GROUPS):
        updated.update(_update_group(received[g], pieces, w, m, v, name=f"adamw_sharded_{g}"))

    small_parts = _all_gather(_pack_small(grads), name="small_grad_all_gather")
    outs_small = [_unpack_small(o) for o in
                  _adamw(small_parts, _pack_small(w), _pack_small(m), _pack_small(v), name="adamw_replicated")]

    total_loss = lax.psum(loss[0, 0], ("x", "y", "c"))
    result = [total_loss, dx[None]]
    for k in range(4):
        for n in WEIGHT_ORDER:
            if n not in SHARDED:
                result.append(outs_small[k][n])
            elif (n, None) in updated:
                result.append(updated[(n, None)][k])
            else:
                result.append(jnp.concatenate([updated[(n, l)][k] for l in range(SHARDED[n][0][0])], axis=0))
    return tuple(result)
```

```python
import functools
import math

import jax
import jax.numpy as jnp
from jax import lax
from jax.experimental import pallas as pl
from jax.experimental.pallas import tpu as pltpu

F32 = jnp.float32
BF16 = jnp.bfloat16
MM_DTYPE = BF16

N_DEV = 8
D_MODEL = 1024
DEPTH = 4
RMS_EPS = 1e-6
L2_EPS = 1e-6

LANES = 128

GDN_HEADS = 8
GDN_DK = 128
GDN_DV = 128
GDN_CONV = 4
GDN_CHUNK = 128
GDN_QKV = 3 * GDN_HEADS * GDN_DK
GDN_MAIN = GDN_QKV + GDN_HEADS * GDN_DV
GDN_IN_WIDTH = GDN_MAIN + 2 * GDN_HEADS

DIL_GROUPS = ((128, 1), (512, 4), (2048, 16))
DIL_HEADS = 8
DIL_DH = 64
DIL_SPAN = 128
DIL_SLAB = 3 * DIL_HEADS * LANES
ALIBI_MAX_BIAS = 8.0

FFN_HIDDEN = 2816

ADAM_LR = 0.001
ADAM_B1 = 0.9
ADAM_B2 = 0.999
ADAM_EPS = 1e-08
ADAM_WD = 0.01
ADAM_STEP = 10

VMEM_LIMIT = 56 * 1024 * 1024
ROW_TILE = 512
MATMUL_VMEM_BUDGET = 40 * 1024 * 1024
NEG = -1e30


def _cparams(sem):
    return pltpu.CompilerParams(dimension_semantics=sem, vmem_limit_bytes=VMEM_LIMIT)


def _single_pass(a, b, a_dim, b_dim):
    lead = a.ndim - 2
    batch = ((0,), (0,)) if lead else ((), ())
    return lax.dot_general(a.astype(BF16), b.astype(BF16), (((lead + a_dim,), (lead + b_dim,)), batch),
                           preferred_element_type=F32)


def _bdot(a, b):
    return _single_pass(a, b, 1, 0)


def _bdot_nt(a, b):
    return _single_pass(a, b, 1, 1)


def _bdot_tn(a, b):
    return _single_pass(a, b, 0, 0)


def _pick(n, candidates):
    for c in candidates:
        if n % c == 0:
            return c
    raise ValueError(f"no tile for {n}")


HALF = LANES // 2


def _pack_head_pairs(x):
    x = x.astype(F32)
    tiles = [x[:, (2 * i) * LANES:(2 * i + 1) * LANES] + pltpu.roll(x[:, (2 * i + 1) * LANES:(2 * i + 2) * LANES], HALF, 1)
             for i in range(x.shape[1] // (2 * LANES))]
    return tiles[0] if len(tiles) == 1 else jnp.concatenate(tiles, axis=1)


def _spread_head_pairs(y):
    low = lax.broadcasted_iota(jnp.int32, (y.shape[0], LANES), 1) < HALF
    tiles = []
    for i in range(y.shape[1] // LANES):
        pair = y[:, i * LANES:(i + 1) * LANES]
        tiles += [jnp.where(low, pair, 0.0), jnp.where(low, pltpu.roll(pair, HALF, 1), 0.0)]
    return jnp.concatenate(tiles, axis=1)


def _matmul(a, b, *, name, trans_a=False, trans_b=False, b_rows=None, a_lead=None, add=None, out_dtype=F32,
            packed_a=False, spread_out=False, norm_bwd=None, norm_fwd=None):
    if trans_a:
        k_dim, m_dim = a.shape[-2:]
        m_dim = m_dim // 2 if packed_a else m_dim
    else:
        m_dim, k_dim = a.shape[-2:]
        k_dim = k_dim // 2 if packed_a else k_dim
    slab_m, slab_k = m_dim, k_dim
    if a_lead == "k":
        assert not trans_a
        k_dim *= a.shape[0]
    elif a_lead == "i":
        assert trans_a
        m_dim *= a.shape[0]
    b_start, b_size = b_rows if b_rows is not None else (0, b.shape[0])
    if trans_b:
        n_dim, k2 = b_size, b.shape[1]
    else:
        k2, n_dim = b_size, b.shape[1]
    assert k_dim == k2, (a.shape, b.shape, b_rows)
    tn = _pick(n_dim, (1024, 512, 256, 128))
    tm = min(slab_m, 2048, max(512, (1024 * 1024) // tn))
    tm = _pick(slab_m, (tm, 1408, 1024, 512, 256, 128))
    out_bytes = jnp.dtype(out_dtype).itemsize * (2 if spread_out else 1)
    if norm_bwd is not None:
        out_bytes = 4 + 4 + 4 + 2
        tm = min(tm, 512)
    if norm_fwd is not None:
        out_bytes += 2

    def deepest(rows):
        fixed = rows * tn * (2 * out_bytes + 4 + (8 if add is not None else 0))
        fits = lambda c: fixed + 2 * 2 * c * ((2 if packed_a else 1) * rows + tn) <= MATMUL_VMEM_BUDGET
        return _pick(slab_k, tuple(c for c in (3072, 2816, 2048, 1536, 1408, 1024, 512, 256) if fits(c)) + (128,))

    tk = deepest(tm)
    if tm % 1024 == 0 and deepest(tm // 2) > tk:
        tm, tk = tm // 2, deepest(tm // 2)
    nk = k_dim // tk
    has_add = add is not None
    dn = (((0 if trans_a else 1,), (1 if trans_b else 0,)), ((), ()))
    b_tile = tn if trans_b else tk
    assert b_start % b_tile == 0, (b_rows, b_tile)
    b_off = b_start // b_tile

    has_norm = norm_bwd is not None
    also_norm = norm_fwd is not None
    if has_norm or also_norm:
        assert n_dim == tn and not spread_out and not (has_norm and also_norm)

    def body(*refs):
        refs = list(refs)
        a_ref, b_ref = refs[:2]
        add_ref = refs[2] if has_add else None
        rest = refs[2 + has_add:]
        if has_norm:
            x_ref, w_ref, skip_ref, dx_ref, dxb_ref, dw_ref, acc_ref = rest
        elif also_norm:
            w_ref, o_ref, hn_ref, acc_ref = rest
        else:
            o_ref, acc_ref = rest
        a_blk = _pack_head_pairs(a_ref[...]).astype(a_ref.dtype) if packed_a else a_ref[...]
        part = lax.dot_general(a_blk, b_ref[...], dn, preferred_element_type=F32)
        first_rows = pl.program_id(0) == 0

        def finish(total):
            if has_add:
                total = total + add_ref[...]
            if has_norm:
                xf = x_ref[...]
                r = lax.rsqrt(jnp.mean(xf * xf, axis=-1, keepdims=True) + RMS_EPS)
                gw = total * w_ref[...]
                dx = r * gw - xf * (r * r * r * jnp.mean(gw * xf, axis=-1, keepdims=True)) + skip_ref[...]
                dx_ref[...] = dx
                dxb_ref[...] = dx.astype(dxb_ref.dtype)
                rows = jnp.sum(total * xf * r, axis=0, keepdims=True)

                @pl.when(first_rows)
                def _():
                    dw_ref[...] = rows

                @pl.when(jnp.logical_not(first_rows))
                def _():
                    dw_ref[...] += rows
                return
            if spread_out:
                total = _spread_head_pairs(total)
            o_ref[...] = total.astype(out_dtype)
            if also_norm:
                r = lax.rsqrt(jnp.mean(total * total, axis=-1, keepdims=True) + RMS_EPS)
                hn_ref[...] = (total * r * w_ref[...]).astype(hn_ref.dtype)

        if nk == 1:
            finish(part)
        else:
            k = pl.program_id(2)

            @pl.when(k == 0)
            def _():
                acc_ref[...] = part

            @pl.when(k > 0)
            def _():
                acc_ref[...] += part

            @pl.when(k == nk - 1)
            def _():
                finish(acc_ref[...])

    wide = 2 if packed_a else 1
    a_tile = (tk, wide * tm) if trans_a else (tm, wide * tk)
    a_at = (lambda i, j, k: (k, i)) if trans_a else (lambda i, j, k: (i, k))
    if a_lead is None:
        a_spec = pl.BlockSpec(a_tile, a_at)
    elif a_lead == "k":
        per = slab_k // tk
        a_spec = pl.BlockSpec((None,) + a_tile, lambda i, j, k: (k // per, i, k % per))
    elif a_lead == "i":
        per = slab_m // tm
        a_spec = pl.BlockSpec((None,) + a_tile, lambda i, j, k: (i // per, k, i % per))
    else:
        a_spec = pl.BlockSpec((None,) + a_tile, lambda i, j, k: (a_lead,) + a_at(i, j, k))
    if trans_b:
        b_spec = pl.BlockSpec((tn, tk), lambda i, j, k: (j + b_off, k))
    else:
        b_spec = pl.BlockSpec((tk, tn), lambda i, j, k: (k + b_off, j))
    in_specs = [a_spec, b_spec]
    args = [a, b]
    tile = pl.BlockSpec((tm, tn), lambda i, j, k: (i, j))
    if has_add:
        in_specs.append(tile)
        args.append(add)
    scratch = [pltpu.VMEM((tm, tn) if nk > 1 else (8, LANES), F32)]
    if has_norm:
        x, w_row, dskip = norm_bwd
        one = pl.BlockSpec((1, tn), lambda i, j, k: (0, 0))
        return pl.pallas_call(
            body,
            grid=(m_dim // tm, 1, nk),
            in_specs=in_specs + [tile, one, tile],
            out_specs=[tile, tile, one],
            out_shape=[jax.ShapeDtypeStruct((m_dim, n_dim), F32), jax.ShapeDtypeStruct((m_dim, n_dim), MM_DTYPE),
                       jax.ShapeDtypeStruct((1, n_dim), F32)],
            scratch_shapes=scratch,
            compiler_params=_cparams(("arbitrary", "arbitrary", "arbitrary")),
            name=name,
        )(*args, x, w_row, dskip)
    if also_norm:
        return pl.pallas_call(
            body,
            grid=(m_dim // tm, 1, nk),
            in_specs=in_specs + [pl.BlockSpec((1, tn), lambda i, j, k: (0, 0))],
            out_specs=[tile, tile],
            out_shape=[jax.ShapeDtypeStruct((m_dim, n_dim), out_dtype), jax.ShapeDtypeStruct((m_dim, n_dim), MM_DTYPE)],
            scratch_shapes=scratch,
            compiler_params=_cparams(("parallel", "parallel", "arbitrary")),
            name=name,
        )(*args, norm_fwd)
    return pl.pallas_call(
        body,
        grid=(m_dim // tm, n_dim // tn, nk),
        in_specs=in_specs,
        out_specs=pl.BlockSpec((tm, (2 if spread_out else 1) * tn), lambda i, j, k: (i, j)),
        out_shape=jax.ShapeDtypeStruct((m_dim, (2 if spread_out else 1) * n_dim), out_dtype),
        scratch_shapes=scratch,
        compiler_params=_cparams(("parallel", "parallel", "arbitrary")),
        name=name,
    )(*args)


def _rmsnorm_fwd(x, w_row, *, name):
    t, d = x.shape
    tb = min(t, 1024)

    def body(x_ref, w_ref, o_ref):
        xf = x_ref[...]
        r = lax.rsqrt(jnp.mean(xf * xf, axis=-1, keepdims=True) + RMS_EPS)
        o_ref[...] = (xf * r * w_ref[...]).astype(o_ref.dtype)

    return pl.pallas_call(
        body,
        grid=(t // tb,),
        in_specs=[pl.BlockSpec((tb, d), lambda i: (i, 0)), pl.BlockSpec((1, d), lambda i: (0, 0))],
        out_specs=pl.BlockSpec((tb, d), lambda i: (i, 0)),
        out_shape=jax.ShapeDtypeStruct((t, d), MM_DTYPE),
        compiler_params=_cparams(("parallel",)),
        name=name,
    )(x, w_row)


def _silu(z):
    return z / (1.0 + jnp.exp(-z))


FFN_TM, FFN_TN = 512, 1408


def _ffn_in(hn, in_t, *, name):
    t, d = hn.shape
    h = FFN_HIDDEN
    tm, tn = min(t, FFN_TM), FFN_TN
    nj = h // tn
    dn = (((1,), (1,)), ((), ()))

    def body(a_ref, bg_ref, bu_ref, g_ref, u_ref, act_ref):
        a = a_ref[...]
        g = lax.dot_general(a, bg_ref[...], dn, preferred_element_type=F32)
        u = lax.dot_general(a, bu_ref[...], dn, preferred_element_type=F32)
        g_ref[...] = g.astype(g_ref.dtype)
        u_ref[...] = u.astype(u_ref.dtype)
        act_ref[...] = (_silu(g) * u).astype(act_ref.dtype)

    out = pl.BlockSpec((tm, tn), lambda j, i: (i, j))
    return pl.pallas_call(
        body,
        grid=(nj, t // tm),
        in_specs=[pl.BlockSpec((tm, d), lambda j, i: (i, 0)), pl.BlockSpec((tn, d), lambda j, i: (j, 0)),
                  pl.BlockSpec((tn, d), lambda j, i: (j + nj, 0))],
        out_specs=[out, out, out],
        out_shape=[jax.ShapeDtypeStruct((t, h), MM_DTYPE)] * 3,
        compiler_params=_cparams(("parallel", "parallel")),
        name=name,
    )(hn, in_t, in_t)


def _ffn_dact(dy, out_w, g, u, *, name):
    t, d = dy.shape
    h = FFN_HIDDEN
    tm, tn = min(t, FFN_TM), FFN_TN

    def body(a_ref, b_ref, g_ref, u_ref, d_ref):
        da = lax.dot_general(a_ref[...], b_ref[...], (((1,), (1,)), ((), ())), preferred_element_type=F32)
        gate = g_ref[...].astype(F32)
        sig = 1.0 / (1.0 + jnp.exp(-gate))
        sg = gate * sig
        d_ref[0] = (da * u_ref[...].astype(F32) * (sig + sg * (1.0 - sig))).astype(d_ref.dtype)
        d_ref[1] = (da * sg).astype(d_ref.dtype)

    blk = pl.BlockSpec((tm, tn), lambda j, i: (i, j))
    return pl.pallas_call(
        body,
        grid=(h // tn, t // tm),
        in_specs=[pl.BlockSpec((tm, d), lambda j, i: (i, 0)), pl.BlockSpec((tn, d), lambda j, i: (j, 0)), blk, blk],
        out_specs=pl.BlockSpec((2, tm, tn), lambda j, i: (0, i, j)),
        out_shape=jax.ShapeDtypeStruct((2, t, h), MM_DTYPE),
        compiler_params=_cparams(("parallel", "parallel")),
        name=name,
    )(dy, out_w, g, u)


def _loss_head(y, target, *, name):
    t, d = y.shape
    tb = min(t, 1024)

    def body(y_ref, t_ref, dy_ref, dyb_ref, l_ref):
        err = y_ref[...] - t_ref[...]
        dy_ref[...] = err * (1.0 / d)
        dyb_ref[...] = (err * (1.0 / d)).astype(dyb_ref.dtype)
        part = jnp.sum(jnp.sum(err * err, axis=0, keepdims=True), axis=1, keepdims=True) * (0.5 / d)
        part = jnp.broadcast_to(part, l_ref.shape)

        @pl.when(pl.program_id(0) == 0)
        def _():
            l_ref[...] = part

        @pl.when(pl.program_id(0) > 0)
        def _():
            l_ref[...] += part

    row = pl.BlockSpec((tb, d), lambda i: (i, 0))
    return pl.pallas_call(
        body,
        grid=(t // tb,),
        in_specs=[row, row],
        out_specs=[row, row, pl.BlockSpec((8, LANES), lambda i: (0, 0))],
        out_shape=[jax.ShapeDtypeStruct((t, d), F32), jax.ShapeDtypeStruct((t, d), MM_DTYPE),
                   jax.ShapeDtypeStruct((8, LANES), F32)],
        compiler_params=_cparams(("arbitrary",)),
        name=name,
    )(y, target)


CONV_HALO = 8
CONV_TIME_TILE = 2048


def _conv_tile_scale(c):
    is_qk = c < 2 * GDN_HEADS
    scale = jnp.where(c < GDN_HEADS, GDN_DK ** -0.5, 1.0).astype(F32)
    return is_qk, scale


def _gdn_conv_fwd(pm, conv_w, *, name):
    t = pm.shape[0]
    tb = min(t, CONV_TIME_TILE)
    nt = t // tb
    hb = tb // CONV_HALO

    def body(x_ref, xp_ref, w_ref, o_ref, xe_ref):
        c = pl.program_id(0)
        ti = pl.program_id(1)
        xe_ref[0:CONV_HALO, :] = jnp.where(ti > 0, xp_ref[...], 0.0)
        xe_ref[CONV_HALO:CONV_HALO + tb, :] = x_ref[...]
        w = w_ref[...]
        y = jnp.zeros((tb, LANES), F32)
        for j in range(GDN_CONV):
            off = CONV_HALO - (GDN_CONV - 1) + j
            y = y + w[j:j + 1, :] * xe_ref[pl.ds(off, tb), :]
        s = _silu(y)
        is_qk, scale = _conv_tile_scale(c)
        r = lax.rsqrt(jnp.sum(s * s, axis=-1, keepdims=True) + L2_EPS) * scale
        o_ref[...] = s * jnp.where(is_qk, r, 1.0)

    return pl.pallas_call(
        body,
        grid=(GDN_QKV // LANES, nt),
        in_specs=[
            pl.BlockSpec((tb, LANES), lambda c, i: (i, c)),
            pl.BlockSpec((CONV_HALO, LANES), lambda c, i: (jnp.maximum(i * hb - 1, 0), c)),
            pl.BlockSpec((GDN_CONV, LANES), lambda c, i: (0, c)),
        ],
        out_specs=pl.BlockSpec((tb, LANES), lambda c, i: (i, c)),
        out_shape=jax.ShapeDtypeStruct((t, GDN_QKV), F32),
        scratch_shapes=[pltpu.VMEM((tb + CONV_HALO, LANES), F32)],
        compiler_params=_cparams(("parallel", "parallel")),
        name=name,
    )(pm, pm, conv_w)


def _gdn_conv_bwd(pm, conv_w, dout, *, name):
    t = pm.shape[0]
    tb = min(t, CONV_TIME_TILE)
    nt = t // tb
    hb = tb // CONV_HALO
    last_hb = t // CONV_HALO - 1
    ext = tb + CONV_HALO

    def body(x_ref, xp_ref, xn_ref, d_ref, dn_ref, w_ref, dx_ref, dw_ref, xe_ref, dy_ref):
        c = pl.program_id(0)
        ti = pl.program_id(1)
        has_next = ti < nt - 1
        xe_ref[0:CONV_HALO, :] = jnp.where(ti > 0, xp_ref[...], 0.0)
        xe_ref[CONV_HALO:CONV_HALO + tb, :] = x_ref[...]
        xe_ref[CONV_HALO + tb:2 * CONV_HALO + tb, :] = jnp.where(has_next, xn_ref[...], 0.0)
        de = jnp.concatenate([d_ref[...], jnp.where(has_next, dn_ref[...], 0.0)], axis=0)
        w = w_ref[...]
        y = jnp.zeros((ext, LANES), F32)
        for j in range(GDN_CONV):
            off = CONV_HALO - (GDN_CONV - 1) + j
            y = y + w[j:j + 1, :] * xe_ref[pl.ds(off, ext), :]
        sig = 1.0 / (1.0 + jnp.exp(-y))
        s = y * sig
        is_qk, scale = _conv_tile_scale(c)
        r = lax.rsqrt(jnp.sum(s * s, axis=-1, keepdims=True) + L2_EPS)
        n = s * r
        dnrm = de * scale
        ds_qk = r * (dnrm - n * jnp.sum(dnrm * n, axis=-1, keepdims=True))
        ds = jnp.where(is_qk, ds_qk, de)
        dy_ref[...] = ds * (sig + s * (1.0 - sig))
        dy = dy_ref[0:tb, :]
        dx = jnp.zeros((tb, LANES), F32)
        dw_rows = []
        for j in range(GDN_CONV):
            sh = GDN_CONV - 1 - j
            dx = dx + w[j:j + 1, :] * dy_ref[pl.ds(sh, tb), :]
            off = CONV_HALO - (GDN_CONV - 1) + j
            dw_rows.append(jnp.sum(dy * xe_ref[pl.ds(off, tb), :], axis=0, keepdims=True))
        dx_ref[...] = dx.astype(dx_ref.dtype)
        part = jnp.concatenate(dw_rows, axis=0)

        @pl.when(ti == 0)
        def _():
            dw_ref[...] = part

        @pl.when(ti > 0)
        def _():
            dw_ref[...] += part

    main = pl.BlockSpec((tb, LANES), lambda c, i: (i, c))
    prev = pl.BlockSpec((CONV_HALO, LANES), lambda c, i: (jnp.maximum(i * hb - 1, 0), c))
    nxt = pl.BlockSpec((CONV_HALO, LANES), lambda c, i: (jnp.minimum((i + 1) * hb, last_hb), c))
    return pl.pallas_call(
        body,
        grid=(GDN_QKV // LANES, nt),
        in_specs=[main, prev, nxt, main, nxt, pl.BlockSpec((GDN_CONV, LANES), lambda c, i: (0, c))],
        out_specs=[main, pl.BlockSpec((GDN_CONV, LANES), lambda c, i: (0, c))],
        out_shape=[jax.ShapeDtypeStruct((t, GDN_QKV), MM_DTYPE), jax.ShapeDtypeStruct((GDN_CONV, GDN_QKV), F32)],
        scratch_shapes=[pltpu.VMEM((tb + 2 * CONV_HALO, LANES), F32), pltpu.VMEM((ext, LANES), F32)],
        compiler_params=_cparams(("parallel", "arbitrary")),
        name=name,
    )(pm, pm, pm, dout, dout, conv_w)


def _head_selector(first_col):
    row = lax.broadcasted_iota(jnp.int32, (LANES, GDN_HEADS * LANES), 0)
    col = lax.broadcasted_iota(jnp.int32, (LANES, GDN_HEADS * LANES), 1)
    return (col // LANES + first_col == row).astype(BF16)


def _spread_columns(cols, first_col):
    sel = _head_selector(first_col)
    return sum(_bdot(p, sel) for p in _bf16_pieces(cols))


def _gather_columns(wide, first_col):
    sel = _head_selector(first_col)
    return sum(_bdot_nt(p, sel) for p in _bf16_pieces(wide))


def _softplus(x):
    return jnp.maximum(x, 0.0) + jnp.log(1.0 + jnp.exp(-jnp.abs(x)))


def _gdn_gates_fwd(ab, alog_row, dt_row, *, name):
    t = ab.shape[0]
    tb = min(t, 1024)
    wide = GDN_HEADS * LANES

    def body(ab_ref, al_ref, dt_ref, g_ref, b_ref):
        x = ab_ref[...]
        g_cols = -jnp.exp(al_ref[...]) * _softplus(x + dt_ref[...])
        b_cols = 1.0 / (1.0 + jnp.exp(-x))
        g_ref[...] = _spread_columns(g_cols, 0)
        b_ref[...] = _spread_columns(b_cols, GDN_HEADS)

    row = pl.BlockSpec((tb, LANES), lambda i: (i, 0))
    one = pl.BlockSpec((1, LANES), lambda i: (0, 0))
    out = pl.BlockSpec((tb, wide), lambda i: (i, 0))
    return pl.pallas_call(
        body,
        grid=(t // tb,),
        in_specs=[row, one, one],
        out_specs=[out, out],
        out_shape=[jax.ShapeDtypeStruct((t, wide), F32)] * 2,
        compiler_params=_cparams(("parallel",)),
        name=name,
    )(ab, alog_row, dt_row)


def _gdn_gates_bwd(ab, alog_row, dt_row, dgb, dbb, *, name):
    t = ab.shape[0]
    tb = min(t, 1024)
    wide = GDN_HEADS * LANES

    def body(ab_ref, al_ref, dt_ref, dg_ref, db_ref, dab_ref, dal_ref, ddt_ref):
        x = ab_ref[...]
        lane = lax.broadcasted_iota(jnp.int32, (tb, LANES), 1)
        dg_cols = _gather_columns(dg_ref[...], 0)
        db_cols = _gather_columns(db_ref[...], GDN_HEADS)
        ea = jnp.exp(al_ref[...])
        z = x + dt_ref[...]
        sp = _softplus(z)
        sg = 1.0 / (1.0 + jnp.exp(-z))
        beta = 1.0 / (1.0 + jnp.exp(-x))
        da = jnp.where(lane < GDN_HEADS, dg_cols * (-ea) * sg, 0.0)
        db = jnp.where((lane >= GDN_HEADS) & (lane < 2 * GDN_HEADS), db_cols * beta * (1.0 - beta), 0.0)
        dab_ref[...] = (da + db).astype(dab_ref.dtype)
        p_al = jnp.sum(jnp.where(lane < GDN_HEADS, dg_cols * (-ea) * sp, 0.0), axis=0, keepdims=True)
        p_dt = jnp.sum(da, axis=0, keepdims=True)

        @pl.when(pl.program_id(0) == 0)
        def _():
            dal_ref[...] = p_al
            ddt_ref[...] = p_dt

        @pl.when(pl.program_id(0) > 0)
        def _():
            dal_ref[...] += p_al
            ddt_ref[...] += p_dt

    row = pl.BlockSpec((tb, LANES), lambda i: (i, 0))
    one = pl.BlockSpec((1, LANES), lambda i: (0, 0))
    big = pl.BlockSpec((tb, wide), lambda i: (i, 0))
    return pl.pallas_call(
        body,
        grid=(t // tb,),
        in_specs=[row, one, one, big, big],
        out_specs=[row, one, one],
        out_shape=[jax.ShapeDtypeStruct((t, LANES), MM_DTYPE), jax.ShapeDtypeStruct((1, LANES), F32),
                   jax.ShapeDtypeStruct((1, LANES), F32)],
        compiler_params=_cparams(("arbitrary",)),
        name=name,
    )(ab, alog_row, dt_row, dgb, dbb)


@jax.custom_vjp
def _unit_lower_inverse_rest(n):
    c = n.shape[-1]
    ri = lax.broadcasted_iota(jnp.int32, (c, c), 0)
    ci = lax.broadcasted_iota(jnp.int32, (c, c), 1)
    rest = None
    size = 1
    while size < c:
        joins = ((ri // (2 * size)) == (ci // (2 * size))) & ((ri // size) != (ci // size))
        low = jnp.where(joins, n, 0.0)
        if rest is None:
            rest = -low
        else:
            left = low + _bdot(rest, low)
            rest = rest - (left + _bdot(left, rest))
        size *= 2
    return rest


def _unit_lower_inverse_rest_fwd(n):
    rest = _unit_lower_inverse_rest(n)
    return rest, rest


def _unit_lower_inverse_rest_bwd(rest, ct):
    left = ct + _bdot_tn(rest, ct)
    return (-(left + _bdot_nt(left, rest)),)


_unit_lower_inverse_rest.defvjp(_unit_lower_inverse_rest_fwd, _unit_lower_inverse_rest_bwd)


@jax.custom_vjp
def _known_inverse_rest(n, rest):
    return rest


def _known_inverse_rest_fwd(n, rest):
    return rest, rest


def _known_inverse_rest_bwd(rest, ct):
    return _unit_lower_inverse_rest_bwd(rest, ct) + (jnp.zeros_like(rest),)


_known_inverse_rest.defvjp(_known_inverse_rest_fwd, _known_inverse_rest_bwd)


def _bf16_pieces(x):
    hi = x.astype(BF16)
    r1 = x - hi.astype(F32)
    mid = r1.astype(BF16)
    lo = (r1 - mid.astype(F32)).astype(BF16)
    return hi, mid, lo


def _lower_ones(shape):
    c = shape[-1]
    ri = lax.broadcasted_iota(jnp.int32, (c, c), 0)
    ci = lax.broadcasted_iota(jnp.int32, (c, c), 1)
    return jnp.broadcast_to((ri >= ci).astype(BF16), shape)


@jax.custom_vjp
def _running_sum(x):
    tri = _lower_ones(x.shape)
    return sum(_bdot(tri, p) for p in _bf16_pieces(x))


def _running_sum_fwd(x):
    return _running_sum(x), None


def _running_sum_bwd(_, ct):
    tri = _lower_ones(ct.shape)
    return (sum(_bdot_tn(tri, p) for p in _bf16_pieces(ct)),)


_running_sum.defvjp(_running_sum_fwd, _running_sum_bwd)


def _gdn_prep_math(q, k, v, gb, bb, known_rest=None, with_rest=False):
    c = GDN_CHUNK
    ri = lax.broadcasted_iota(jnp.int32, (c, c), 0)
    ci = lax.broadcasted_iota(jnp.int32, (c, c), 1)
    causal = ri >= ci
    gc = _running_sum(gb)
    decay = jnp.exp(jnp.where(causal, gc - jnp.swapaxes(gc, -1, -2), NEG))
    n = jnp.where(ri > ci, _bdot_nt(k, k) * bb * decay, 0.0)
    rest = _unit_lower_inverse_rest(n) if known_rest is None else _known_inverse_rest(n, known_rest)
    eg = jnp.exp(gc)
    rhs_v = v * bb
    rhs_k = k * bb * eg
    u = rhs_v + _bdot(rest, rhs_v)
    w = rhs_k + _bdot(rest, rhs_k)
    qk = _bdot_nt(q, k) * decay
    qd = q * eg
    last = jnp.sum(jnp.where(ri == c - 1, gc, 0.0), axis=-2, keepdims=True)
    gl = jnp.broadcast_to(last, gc.shape)
    kt = k * jnp.exp(gl - gc)
    cd = jnp.exp(gl)
    return (u, w, qk, qd, kt, cd, rest) if with_rest else (u, w, qk, qd, kt, cd)


def _head_tiles(ref, h):
    return ref[:, h * LANES:(h + 1) * LANES]


def _stack_heads(ref, first=0, heads=GDN_HEADS):
    return jnp.stack([_head_tiles(ref, first + h) for h in range(heads)])


def _store_heads(ref, val, first=0):
    for h in range(val.shape[0]):
        ref[:, (first + h) * LANES:(first + h + 1) * LANES] = val[h].astype(ref.dtype)


def _gdn_prep_fwd(qkv, gb, bb, *, name):
    t = qkv.shape[0]
    c = GDN_CHUNK
    wide = GDN_HEADS * LANES

    def body(q_ref, k_ref, v_ref, g_ref, b_ref, *outs):
        res = _gdn_prep_math(*(_stack_heads(r) for r in (q_ref, k_ref, v_ref, g_ref, b_ref)), with_rest=True)
        for o_ref, val in zip(outs, res):
            _store_heads(o_ref, val)

    blk = lambda off: pl.BlockSpec((c, wide), lambda i: (i, off))
    outs = pl.pallas_call(
        body,
        grid=(t // c,),
        in_specs=[blk(0), blk(1), blk(2), blk(0), blk(0)],
        out_specs=[blk(0)] * 7,
        out_shape=[jax.ShapeDtypeStruct((t, wide), dt) for dt in (F32, MM_DTYPE, MM_DTYPE, MM_DTYPE, MM_DTYPE, F32, F32)],
        compiler_params=_cparams(("parallel",)),
        name=name,
    )(qkv, qkv, qkv, gb, bb)
    return tuple(outs[:6]), outs[6]


def _gdn_prep_bwd(qkv, gb, bb, rest, cts, *, name):
    t = qkv.shape[0]
    c = GDN_CHUNK
    wide = GDN_HEADS * LANES

    def body(q_ref, k_ref, v_ref, g_ref, b_ref, r_ref, c0, c1, c2, c3, c4, c5, dqkv_ref, dg_ref, db_ref):
        prim = tuple(_stack_heads(r) for r in (q_ref, k_ref, v_ref, g_ref, b_ref))
        _, pull = jax.vjp(functools.partial(_gdn_prep_math, known_rest=_stack_heads(r_ref)), *prim)
        dq, dk, dv, dg, db = pull(tuple(_stack_heads(r) for r in (c0, c1, c2, c3, c4, c5)))
        _store_heads(dqkv_ref, dq)
        _store_heads(dqkv_ref, dk, first=GDN_HEADS)
        _store_heads(dqkv_ref, dv, first=2 * GDN_HEADS)
        _store_heads(dg_ref, dg)
        _store_heads(db_ref, db)

    blk = lambda off: pl.BlockSpec((c, wide), lambda i: (i, off))
    return pl.pallas_call(
        body,
        grid=(t // c,),
        in_specs=[blk(0), blk(1), blk(2), blk(0), blk(0)] + [blk(0)] * 7,
        out_specs=[pl.BlockSpec((c, 3 * wide), lambda i: (i, 0)), blk(0), blk(0)],
        out_shape=[jax.ShapeDtypeStruct((t, 3 * wide), F32), jax.ShapeDtypeStruct((t, wide), F32),
                   jax.ShapeDtypeStruct((t, wide), F32)],
        compiler_params=_cparams(("parallel",)),
        name=name,
    )(qkv, qkv, qkv, gb, bb, rest, *cts)


def _gdn_scan_math(s, u, w, qk, qd, kt, cd):
    v_new = u - _bdot(w, s)
    o = _bdot(qd, s) + _bdot(qk, v_new)
    s_new = s * cd + _bdot_tn(kt, v_new)
    return o, s_new


def _gdn_scan_fwd(prep, *, name):
    t = prep[0].shape[0]
    c = GDN_CHUNK
    wide = GDN_HEADS * LANES

    def body(u_ref, w_ref, qk_ref, qd_ref, kt_ref, cd_ref, o_ref, st_ref, s_ref):
        @pl.when(pl.program_id(0) == 0)
        def _():
            s_ref[...] = jnp.zeros_like(s_ref)

        s = _stack_heads(s_ref)
        _store_heads(st_ref, s)
        o, s_new = _gdn_scan_math(s, *(_stack_heads(r).astype(F32) for r in (u_ref, w_ref, qk_ref, qd_ref, kt_ref, cd_ref)))
        _store_heads(o_ref, o)
        _store_heads(s_ref, s_new)

    blk = pl.BlockSpec((c, wide), lambda i: (i, 0))
    return pl.pallas_call(
        body,
        grid=(t // c,),
        in_specs=[blk] * 6,
        out_specs=[blk, blk],
        out_shape=[jax.ShapeDtypeStruct((t, wide), F32)] * 2,
        scratch_shapes=[pltpu.VMEM((GDN_DK, wide), F32)],
        compiler_params=_cparams(("arbitrary",)),
        name=name,
    )(*prep)


def _gdn_scan_bwd(prep, states, do, *, name):
    t = do.shape[0]
    c = GDN_CHUNK
    wide = GDN_HEADS * LANES
    nc = t // c

    def body(u_ref, w_ref, qk_ref, qd_ref, kt_ref, cd_ref, st_ref, do_ref, *rest):
        outs, ds_ref = rest[:6], rest[6]

        @pl.when(pl.program_id(0) == 0)
        def _():
            ds_ref[...] = jnp.zeros_like(ds_ref)

        prim = tuple(_stack_heads(r).astype(F32) for r in (st_ref, u_ref, w_ref, qk_ref, qd_ref, kt_ref, cd_ref))
        _, pull = jax.vjp(_gdn_scan_math, *prim)
        grads = pull((_stack_heads(do_ref), _stack_heads(ds_ref)))
        _store_heads(ds_ref, grads[0])
        for o_ref, val in zip(outs, grads[1:]):
            _store_heads(o_ref, val)

    blk = pl.BlockSpec((c, wide), lambda i: (nc - 1 - i, 0))
    return pl.pallas_call(
        body,
        grid=(nc,),
        in_specs=[blk] * 8,
        out_specs=[blk] * 6,
        out_shape=[jax.ShapeDtypeStruct((t, wide), F32)] * 6,
        scratch_shapes=[pltpu.VMEM((GDN_DK, wide), F32)],
        compiler_params=_cparams(("arbitrary",)),
        name=name,
    )(*prep, states, do)


def _gdn_outgate_math(o, z, nw):
    r = lax.rsqrt(jnp.mean(o * o, axis=-1, keepdims=True) + RMS_EPS)
    return o * r * nw * _silu(z)


def _gdn_outgate_fwd(o, pm, nw_row, *, name):
    t = o.shape[0]
    tb = min(t, ROW_TILE)
    wide = GDN_HEADS * LANES
    z_at = GDN_QKV // wide

    def body(o_ref, z_ref, nw_ref, y_ref):
        for h in range(GDN_HEADS):
            y = _gdn_outgate_math(_head_tiles(o_ref, h), _head_tiles(z_ref, h), nw_ref[...])
            y_ref[:, h * LANES:(h + 1) * LANES] = y.astype(y_ref.dtype)

    return pl.pallas_call(
        body,
        grid=(t // tb,),
        in_specs=[pl.BlockSpec((tb, wide), lambda i: (i, 0)), pl.BlockSpec((tb, wide), lambda i: (i, z_at)),
                  pl.BlockSpec((1, LANES), lambda i: (0, 0))],
        out_specs=pl.BlockSpec((tb, wide), lambda i: (i, 0)),
        out_shape=jax.ShapeDtypeStruct((t, wide), MM_DTYPE),
        compiler_params=_cparams(("parallel",)),
        name=name,
    )(o, pm, nw_row)


def _gdn_outgate_bwd(o, pm, nw_row, dy, *, name):
    t = o.shape[0]
    tb = min(t, ROW_TILE)
    wide = GDN_HEADS * LANES
    z_at = GDN_QKV // wide

    def body(o_ref, z_ref, nw_ref, dy_ref, do_ref, dz_ref, dnw_ref):
        total = jnp.zeros((1, LANES), F32)
        for h in range(GDN_HEADS):
            _, pull = jax.vjp(_gdn_outgate_math, _head_tiles(o_ref, h), _head_tiles(z_ref, h), nw_ref[...])
            d_o, d_z, d_nw = pull(_head_tiles(dy_ref, h))
            do_ref[:, h * LANES:(h + 1) * LANES] = d_o
            dz_ref[:, h * LANES:(h + 1) * LANES] = d_z.astype(dz_ref.dtype)
            total = total + d_nw

        @pl.when(pl.program_id(0) == 0)
        def _():
            dnw_ref[...] = total

        @pl.when(pl.program_id(0) > 0)
        def _():
            dnw_ref[...] += total

    blk = pl.BlockSpec((tb, wide), lambda i: (i, 0))
    one = pl.BlockSpec((1, LANES), lambda i: (0, 0))
    return pl.pallas_call(
        body,
        grid=(t // tb,),
        in_specs=[blk, pl.BlockSpec((tb, wide), lambda i: (i, z_at)), one, blk],
        out_specs=[blk, blk, one],
        out_shape=[jax.ShapeDtypeStruct((t, wide), F32), jax.ShapeDtypeStruct((t, wide), MM_DTYPE),
                   jax.ShapeDtypeStruct((1, LANES), F32)],
        compiler_params=_cparams(("arbitrary",)),
        name=name,
    )(o, pm, nw_row, dy)


def _rms64(x, w_row):
    return x * lax.rsqrt(jnp.sum(x * x, axis=-1, keepdims=True) * (1.0 / DIL_DH) + RMS_EPS) * w_row


def _alibi_slopes(group):
    head = lax.broadcasted_iota(jnp.int32, (DIL_HEADS, 8, LANES), 0).astype(F32)
    rate = -math.log(2.0) * ALIBI_MAX_BIAS / (len(DIL_GROUPS) * DIL_HEADS)
    slope = jnp.exp(rate * (head + float(group * DIL_HEADS + 1)))
    return jnp.broadcast_to(slope[:, 0:1, :], (DIL_HEADS, DIL_SPAN, LANES))


def _band_logits(qn, kp, kc, slope_d, has_prev):
    qi = lax.broadcasted_iota(jnp.int32, (DIL_SPAN, DIL_SPAN), 0)
    kj = lax.broadcasted_iota(jnp.int32, (DIL_SPAN, DIL_SPAN), 1)
    steps_c = (qi - kj).astype(F32)
    scale = DIL_DH ** -0.5
    sp = _bdot_nt(qn, kp) * scale - slope_d * (steps_c + float(DIL_SPAN))
    sc = _bdot_nt(qn, kc) * scale - slope_d * steps_c
    sp = jnp.where((kj >= qi) & has_prev, sp, NEG)
    sc = jnp.where(kj <= qi, sc, NEG)
    return sp, sc


def _dil_attn_fwd(slab, wq_row, wk_row, *, group, name):
    dilation = DIL_GROUPS[group][1]
    t = slab.shape[0]
    rows = t // dilation
    nlb = rows // DIL_SPAN
    wide = DIL_HEADS * LANES
    view = slab.reshape(rows, dilation * DIL_SLAB)

    def body(q_ref, kc_ref, vc_ref, kp_ref, vp_ref, wq_ref, wk_ref, o_ref):
        has_prev = pl.program_id(1) > 0
        lane = lax.broadcasted_iota(jnp.int32, (DIL_SPAN, LANES), 1)
        qn = _rms64(_stack_heads(q_ref), wq_ref[...])
        kc = _rms64(_stack_heads(kc_ref), wk_ref[...])
        kp = _rms64(_stack_heads(kp_ref), wk_ref[...])
        sp, sc = _band_logits(qn, kp, kc, _alibi_slopes(group) * float(dilation), has_prev)
        m = jnp.maximum(jnp.max(sp, axis=-1, keepdims=True), jnp.max(sc, axis=-1, keepdims=True))
        pp = jnp.exp(sp - m)
        pc = jnp.exp(sc - m)
        l = jnp.sum(pp, axis=-1, keepdims=True) + jnp.sum(pc, axis=-1, keepdims=True)
        o = (_bdot(pp, _stack_heads(vp_ref)) + _bdot(pc, _stack_heads(vc_ref))) / l
        _store_heads(o_ref, jnp.where(lane < DIL_DH, o, m + jnp.log(l)))

    cur = lambda part: pl.BlockSpec((DIL_SPAN, wide), lambda r, i: (i, 3 * r + part))
    prv = lambda part: pl.BlockSpec((DIL_SPAN, wide), lambda r, i: (jnp.maximum(i - 1, 0), 3 * r + part))
    one = pl.BlockSpec((1, LANES), lambda r, i: (0, 0))
    out = pl.pallas_call(
        body,
        grid=(dilation, nlb),
        in_specs=[cur(0), cur(1), cur(2), prv(1), prv(2), one, one],
        out_specs=pl.BlockSpec((DIL_SPAN, wide), lambda r, i: (i, r)),
        out_shape=jax.ShapeDtypeStruct((rows, dilation * wide), F32),
        compiler_params=_cparams(("parallel", "parallel")),
        name=name,
    )(view, view, view, view, view, wq_row, wk_row)
    return out.reshape(t, wide)


def _head_slope(group, head):
    idx = jnp.zeros((8, LANES), F32) + head.astype(F32)
    rate = -math.log(2.0) * ALIBI_MAX_BIAS / (len(DIL_GROUPS) * DIL_HEADS)
    slope = jnp.exp(rate * (idx + float(group * DIL_HEADS + 1)))
    return jnp.broadcast_to(slope[0:1, :], (DIL_SPAN, LANES))


RESIDUE_BATCH = 8


def _take_residues(ref, d, first=0, count=None):
    count = d if count is None else count
    return jnp.stack([ref[pl.ds(first + r, DIL_SPAN, stride=d), :] for r in range(count)])


def _put_residues(ref, val, d, first=0):
    for r in range(val.shape[0]):
        ref[pl.ds(first + r, DIL_SPAN, stride=d), :] = val[r]


def _dil_attn_fwd_strided(slab, wq_row, wk_row, *, group, name):
    d = DIL_GROUPS[group][1]
    t = slab.shape[0]
    span = DIL_SPAN * d
    nsb = t // span

    hs = max(1, RESIDUE_BATCH // d)

    def body(*refs):
        q, kc, vc, kp, vp = (refs[i * hs:(i + 1) * hs] for i in range(5))
        wq_ref, wk_ref, o_ref, spread = refs[5 * hs:]
        has_prev = pl.program_id(0) > 0
        lane = lax.broadcasted_iota(jnp.int32, (DIL_SPAN, LANES), 1)
        nb = min(d, RESIDUE_BATCH)
        for r0 in range(0, d, nb):
            take = lambda group_refs: jnp.concatenate([_take_residues(ref, d, r0, nb) for ref in group_refs])
            slope = jnp.concatenate([jnp.broadcast_to(_head_slope(group, pl.program_id(1) * hs + j) * float(d),
                                                      (nb, DIL_SPAN, LANES)) for j in range(hs)])
            qn = _rms64(take(q), wq_ref[...])
            kcn = _rms64(take(kc), wk_ref[...])
            kpn = _rms64(take(kp), wk_ref[...])
            sp, sc = _band_logits(qn, kpn, kcn, slope, has_prev)
            m = jnp.maximum(jnp.max(sp, axis=-1, keepdims=True), jnp.max(sc, axis=-1, keepdims=True))
            pp = jnp.exp(sp - m)
            pc = jnp.exp(sc - m)
            l = jnp.sum(pp, axis=-1, keepdims=True) + jnp.sum(pc, axis=-1, keepdims=True)
            o = (_bdot(pp, take(vp)) + _bdot(pc, take(vc))) / l
            res = jnp.where(lane < DIL_DH, o, m + jnp.log(l))
            for j in range(hs):
                _put_residues(spread, res[j * nb:(j + 1) * nb], d, r0)
                if r0 + nb == d:
                    o_ref[:, j * LANES:(j + 1) * LANES] = spread[...]

    cur = lambda part, j: pl.BlockSpec((span, LANES), lambda i, h: (i, part * DIL_HEADS + h * hs + j))
    prv = lambda part, j: pl.BlockSpec((span, LANES), lambda i, h: (jnp.maximum(i - 1, 0), part * DIL_HEADS + h * hs + j))
    one = pl.BlockSpec((1, LANES), lambda i, h: (0, 0))
    heads = range(hs)
    in_specs = ([cur(0, j) for j in heads] + [cur(1, j) for j in heads] + [cur(2, j) for j in heads]
                + [prv(1, j) for j in heads] + [prv(2, j) for j in heads] + [one, one])
    return pl.pallas_call(
        body,
        grid=(nsb, DIL_HEADS // hs),
        in_specs=in_specs,
        out_specs=pl.BlockSpec((span, hs * LANES), lambda i, h: (i, h)),
        out_shape=jax.ShapeDtypeStruct((t, DIL_HEADS * LANES), F32),
        scratch_shapes=[pltpu.VMEM((span, LANES), F32)],
        compiler_params=_cparams(("parallel", "parallel")),
        name=name,
    )(*([slab] * (5 * hs)), wq_row, wk_row)


def _dil_attn_bwd_strided(slab, stat, wq_row, wk_row, dwq_in, dwk_in, *, group, name):
    d = DIL_GROUPS[group][1]
    t = slab.shape[0]
    span = DIL_SPAN * d
    nsb = t // span

    hs = max(1, RESIDUE_BATCH // d)

    def body(*refs):
        q_refs, kc_refs, vc_refs, kp_refs, vp_refs, st_refs = (refs[i * hs:(i + 1) * hs] for i in range(6))
        wq_ref, wk_ref, dwq_in_ref, dwk_in_ref, d_ref, dwq_ref, dwk_ref, dk_carry, dv_carry, spread = refs[6 * hs:]
        take = lambda group_refs: jnp.concatenate([_take_residues(ref, d) for ref in group_refs])
        step = pl.program_id(1)
        has_prev = step < nsb - 1
        first = (pl.program_id(0) == 0) & (step == 0)

        @pl.when(step == 0)
        def _():
            dk_carry[...] = jnp.zeros_like(dk_carry)
            dv_carry[...] = jnp.zeros_like(dv_carry)

        @pl.when(first)
        def _():
            dwq_ref[...] = dwq_in_ref[...]
            dwk_ref[...] = dwk_in_ref[...]

        lane = lax.broadcasted_iota(jnp.int32, (DIL_SPAN, LANES), 1)
        scale = DIL_DH ** -0.5
        q_raw = take(q_refs)
        kc_raw = take(kc_refs)
        vc = take(vc_refs)
        kp_raw = take(kp_refs)
        vp = take(vp_refs)
        st = take(st_refs)
        slope = jnp.concatenate([jnp.broadcast_to(_head_slope(group, pl.program_id(0) * hs + j) * float(d),
                                                  (d, DIL_SPAN, LANES)) for j in range(hs)])
        d_o = jnp.where(lane < DIL_DH, st, 0.0)
        lse = jnp.sum(jnp.where(lane == DIL_DH, st, 0.0), axis=-1, keepdims=True)
        delta = jnp.sum(jnp.where(lane == DIL_DH + 1, st, 0.0), axis=-1, keepdims=True)
        qn = _rms64(q_raw, wq_ref[...])
        kc = _rms64(kc_raw, wk_ref[...])
        kp = _rms64(kp_raw, wk_ref[...])
        sp, sc = _band_logits(qn, kp, kc, slope, has_prev)
        pp = jnp.exp(sp - lse)
        pc = jnp.exp(sc - lse)
        dsp = pp * (_bdot_nt(d_o, vp) - delta) * scale
        dsc = pc * (_bdot_nt(d_o, vc) - delta) * scale
        dqn = _bdot(dsp, kp) + _bdot(dsc, kc)
        dkc_n = _bdot_tn(dsc, qn) + dk_carry[...]
        dvc = _bdot_tn(pc, d_o) + dv_carry[...]
        dk_carry[...] = _bdot_tn(dsp, qn)
        dv_carry[...] = _bdot_tn(pp, d_o)
        dq_raw, dwq_rows = _rms64_bwd(q_raw, wq_ref[...], dqn)
        dk_raw, dwk_rows = _rms64_bwd(kc_raw, wk_ref[...], dkc_n)
        for part, val in enumerate((dq_raw, dk_raw, dvc)):
            for j in range(hs):
                _put_residues(spread, val[j * d:(j + 1) * d], d)
                d_ref[part, :, j * LANES:(j + 1) * LANES] = spread[...].astype(d_ref.dtype)
        dwq_ref[...] += jnp.sum(jnp.sum(dwq_rows, axis=0), axis=0, keepdims=True)
        dwk_ref[...] += jnp.sum(jnp.sum(dwk_rows, axis=0), axis=0, keepdims=True)

    at = lambda i: nsb - 1 - i
    cur = lambda part, j: pl.BlockSpec((span, LANES), lambda h, i: (at(i), part * DIL_HEADS + h * hs + j))
    prv = lambda part, j: pl.BlockSpec((span, LANES), lambda h, i: (jnp.maximum(at(i) - 1, 0), part * DIL_HEADS + h * hs + j))
    one = pl.BlockSpec((1, LANES), lambda h, i: (0, 0))
    heads = range(hs)
    in_specs = ([cur(0, j) for j in heads] + [cur(1, j) for j in heads] + [cur(2, j) for j in heads]
                + [prv(1, j) for j in heads] + [prv(2, j) for j in heads] + [cur(0, j) for j in heads] + [one] * 4)
    return pl.pallas_call(
        body,
        grid=(DIL_HEADS // hs, nsb),
        in_specs=in_specs,
        out_specs=[pl.BlockSpec((3, span, hs * LANES), lambda h, i: (0, at(i), h)), one, one],
        out_shape=[jax.ShapeDtypeStruct((3, t, DIL_HEADS * LANES), MM_DTYPE), jax.ShapeDtypeStruct((1, LANES), F32),
                   jax.ShapeDtypeStruct((1, LANES), F32)],
        scratch_shapes=[pltpu.VMEM((hs * d, DIL_SPAN, LANES), F32), pltpu.VMEM((hs * d, DIL_SPAN, LANES), F32),
                        pltpu.VMEM((span, LANES), F32)],
        compiler_params=_cparams(("arbitrary", "arbitrary")),
        name=name,
    )(*([slab] * (5 * hs)), *([stat] * hs), wq_row, wk_row, dwq_in, dwk_in)


def _dil_merge_fwd(oe, *, name):
    t = oe[0].shape[0]
    tb = min(t, ROW_TILE)
    wide = DIL_HEADS * LANES

    def body(e0, e1, e2, y_ref, om_ref):
        lane = lax.broadcasted_iota(jnp.int32, (tb, LANES), 1)
        for h in range(DIL_HEADS):
            es = [_head_tiles(e, h) for e in (e0, e1, e2)]
            lse = [jnp.sum(jnp.where(lane == DIL_DH, e, 0.0), axis=-1, keepdims=True) for e in es]
            top = jnp.maximum(jnp.maximum(lse[0], lse[1]), lse[2])
            joint = top + jnp.log(jnp.exp(lse[0] - top) + jnp.exp(lse[1] - top) + jnp.exp(lse[2] - top))
            o = sum(jnp.exp(l - joint) * e for l, e in zip(lse, es))
            y_ref[:, h * LANES:(h + 1) * LANES] = jnp.where(lane < DIL_DH, o, 0.0).astype(y_ref.dtype)
            om_ref[:, h * LANES:(h + 1) * LANES] = jnp.where(lane < DIL_DH, o, joint)

    blk = pl.BlockSpec((tb, wide), lambda i: (i, 0))
    return pl.pallas_call(
        body,
        grid=(t // tb,),
        in_specs=[blk] * 3,
        out_specs=[blk, blk],
        out_shape=[jax.ShapeDtypeStruct((t, wide), MM_DTYPE), jax.ShapeDtypeStruct((t, wide), F32)],
        compiler_params=_cparams(("parallel",)),
        name=name,
    )(*oe)


def _dil_merge_bwd(dy, om, *, name):
    t = dy.shape[0]
    tb = min(t, ROW_TILE)
    wide = DIL_HEADS * LANES

    def body(dy_ref, om_ref, st_ref):
        lane = lax.broadcasted_iota(jnp.int32, (tb, LANES), 1)
        for h in range(DIL_HEADS):
            d_o = jnp.where(lane < DIL_DH, _head_tiles(dy_ref, h), 0.0)
            om_t = _head_tiles(om_ref, h)
            delta = jnp.sum(d_o * om_t, axis=-1, keepdims=True)
            st_ref[:, h * LANES:(h + 1) * LANES] = jnp.where(
                lane < DIL_DH, d_o, jnp.where(lane == DIL_DH, om_t, jnp.where(lane == DIL_DH + 1, delta, 0.0)))

    blk = pl.BlockSpec((tb, wide), lambda i: (i, 0))
    return pl.pallas_call(
        body,
        grid=(t // tb,),
        in_specs=[blk, blk],
        out_specs=blk,
        out_shape=jax.ShapeDtypeStruct((t, wide), F32),
        compiler_params=_cparams(("parallel",)),
        name=name,
    )(dy, om)


def _rms64_bwd(x, w_row, dy):
    r = lax.rsqrt(jnp.sum(x * x, axis=-1, keepdims=True) * (1.0 / DIL_DH) + RMS_EPS)
    gw = dy * w_row
    dx = r * gw - x * (r * r * r * jnp.sum(gw * x, axis=-1, keepdims=True) * (1.0 / DIL_DH))
    return dx, dy * x * r


def _dil_attn_bwd(slab, stat, wq_row, wk_row, dwq_in, dwk_in, *, group, name):
    dilation = DIL_GROUPS[group][1]
    t = slab.shape[0]
    rows = t // dilation
    nlb = rows // DIL_SPAN
    wide = DIL_HEADS * LANES
    view = slab.reshape(rows, dilation * DIL_SLAB)
    stat_view = stat.reshape(rows, dilation * wide)

    def body(cur_ref, kp_ref, vp_ref, st_ref, wq_ref, wk_ref, dwq_in_ref, dwk_in_ref, d_ref, dwq_ref, dwk_ref,
             dk_carry, dv_carry):
        step = pl.program_id(1)
        has_prev = step < nlb - 1
        first = (pl.program_id(0) == 0) & (step == 0)

        @pl.when(step == 0)
        def _():
            dk_carry[...] = jnp.zeros_like(dk_carry)
            dv_carry[...] = jnp.zeros_like(dv_carry)

        @pl.when(first)
        def _():
            dwq_ref[...] = dwq_in_ref[...]
            dwk_ref[...] = dwk_in_ref[...]

        lane = lax.broadcasted_iota(jnp.int32, (DIL_SPAN, LANES), 1)
        scale = DIL_DH ** -0.5
        q_raw = _stack_heads(cur_ref)
        kc_raw = _stack_heads(cur_ref, first=DIL_HEADS)
        vc = _stack_heads(cur_ref, first=2 * DIL_HEADS)
        kp_raw = _stack_heads(kp_ref)
        vp = _stack_heads(vp_ref)
        st = _stack_heads(st_ref)
        d_o = jnp.where(lane < DIL_DH, st, 0.0)
        lse = jnp.sum(jnp.where(lane == DIL_DH, st, 0.0), axis=-1, keepdims=True)
        delta = jnp.sum(jnp.where(lane == DIL_DH + 1, st, 0.0), axis=-1, keepdims=True)
        qn = _rms64(q_raw, wq_ref[...])
        kc = _rms64(kc_raw, wk_ref[...])
        kp = _rms64(kp_raw, wk_ref[...])
        sp, sc = _band_logits(qn, kp, kc, _alibi_slopes(group) * float(dilation), has_prev)
        pp = jnp.exp(sp - lse)
        pc = jnp.exp(sc - lse)
        dsp = pp * (_bdot_nt(d_o, vp) - delta) * scale
        dsc = pc * (_bdot_nt(d_o, vc) - delta) * scale
        dqn = _bdot(dsp, kp) + _bdot(dsc, kc)
        dkc_n = _bdot_tn(dsc, qn) + _stack_heads(dk_carry)
        dvc = _bdot_tn(pc, d_o) + _stack_heads(dv_carry)
        _store_heads(dk_carry, _bdot_tn(dsp, qn))
        _store_heads(dv_carry, _bdot_tn(pp, d_o))
        dq_raw, dwq_rows = _rms64_bwd(q_raw, wq_ref[...], dqn)
        dk_raw, dwk_rows = _rms64_bwd(kc_raw, wk_ref[...], dkc_n)
        _store_heads(d_ref, dq_raw)
        _store_heads(d_ref, dk_raw, first=DIL_HEADS)
        _store_heads(d_ref, dvc, first=2 * DIL_HEADS)
        dwq_ref[...] += jnp.sum(jnp.sum(dwq_rows, axis=0), axis=0, keepdims=True)
        dwk_ref[...] += jnp.sum(jnp.sum(dwk_rows, axis=0), axis=0, keepdims=True)

    blk_i = lambda i: nlb - 1 - i
    cur = pl.BlockSpec((DIL_SPAN, DIL_SLAB), lambda r, i: (blk_i(i), r))
    prv = lambda part: pl.BlockSpec((DIL_SPAN, wide), lambda r, i: (jnp.maximum(blk_i(i) - 1, 0), 3 * r + part))
    one = pl.BlockSpec((1, LANES), lambda r, i: (0, 0))
    dslab, dwq, dwk = pl.pallas_call(
        body,
        grid=(dilation, nlb),
        in_specs=[cur, prv(1), prv(2), pl.BlockSpec((DIL_SPAN, wide), lambda r, i: (blk_i(i), r)), one, one, one, one],
        out_specs=[cur, one, one],
        out_shape=[jax.ShapeDtypeStruct((rows, dilation * DIL_SLAB), MM_DTYPE), jax.ShapeDtypeStruct((1, LANES), F32),
                   jax.ShapeDtypeStruct((1, LANES), F32)],
        scratch_shapes=[pltpu.VMEM((DIL_SPAN, wide), F32), pltpu.VMEM((DIL_SPAN, wide), F32)],
        compiler_params=_cparams(("arbitrary", "arbitrary")),
        name=name,
    )(view, view, view, stat_view, wq_row, wk_row, dwq_in, dwk_in)
    return dslab.reshape(t, DIL_SLAB), dwq, dwk


def _row(v, width=LANES):
    v = v.astype(F32).reshape(-1)
    return jnp.pad(v, (0, width - v.shape[0])).reshape(1, width)


def _prepare_weights(w):
    return dict(gdn=_prepare_gdn(w), dil=_prepare_dil(w), ffn=_prepare_ffn(w))


def _prepare_gdn(w, layers=range(DEPTH // 2)):
    gdn = {}
    for j in layers:
        wt = w["gdn_w_in"][j]
        gates_t = jnp.pad(wt[GDN_MAIN:], ((0, LANES - 2 * GDN_HEADS), (0, 0)))
        gdn[j] = dict(in_t=wt, gates_t=gates_t, out=w["gdn_w_out"][j], conv=w["gdn_conv_w"][j].astype(F32),
                      alog=_row(w["gdn_a_log"][j]), dt=_row(w["gdn_dt_bias"][j]), nw=_row(w["gdn_norm_w"][j]))
    return gdn


def _prepare_dil(w, layers=range(DEPTH // 2)):
    d = D_MODEL
    dil = {}
    for j in layers:
        wt = w["dil_w_in"][j].reshape(3, len(DIL_GROUPS), DIL_HEADS, DIL_DH, d)
        wg_t = [wt[:, g].reshape(DIL_SLAB // 2, d) for g in range(len(DIL_GROUPS))]
        out_t = jnp.pad(w["dil_w_out"][j].reshape(d, DIL_HEADS, DIL_DH), ((0, 0), (0, 0), (0, LANES - DIL_DH)))
        dil[j] = dict(wg_t=wg_t, out_t=out_t.reshape(d, DIL_HEADS * LANES), wq=_row(w["dil_q_norm"][j]),
                      wk=_row(w["dil_k_norm"][j]))
    return dil


def _prepare_ffn(w, layers=range(DEPTH)):
    return {i: dict(in_t=w["ffn_w_in"][i], out=w["ffn_w_out"][i]) for i in layers}


def _residual_out(a, w, x, next_row, *, name, **kw):
    if next_row is None:
        return _matmul(a, w, add=x, name=name, **kw), None
    return _matmul(a, w, add=x, norm_fwd=next_row, name=name + "_norm", **kw)


def _gdn_layer_fwd(x, nrow, p, hn=None, next_row=None):
    if hn is None:
        hn = _rmsnorm_fwd(x, nrow, name="rmsnorm_fwd")
    pm = _matmul(hn, p["in_t"], trans_b=True, b_rows=(0, GDN_MAIN), name="gdn_proj_main")
    ab = _matmul(hn, p["gates_t"], trans_b=True, name="gdn_proj_gates")
    qkv = _gdn_conv_fwd(pm, p["conv"], name="gdn_conv_fwd")
    gb, bb = _gdn_gates_fwd(ab, p["alog"], p["dt"], name="gdn_gates_fwd")
    prep, rest = _gdn_prep_fwd(qkv, gb, bb, name="gdn_prep_fwd")
    o, states = _gdn_scan_fwd(prep, name="gdn_scan_fwd")
    og = _gdn_outgate_fwd(o, pm, p["nw"], name="gdn_outgate_fwd")
    y, hn_next = _residual_out(og, p["out"], x, next_row, name="gdn_proj_out")
    return y, (x, hn, pm, ab, qkv, gb, bb, prep, rest, states, o, og), hn_next


def _gdn_layer_bwd(dx, dxb, nrow, p, saved):
    x, hn, pm, ab, qkv, gb, bb, prep, rest, states, o, og = saved
    d_og = _matmul(dxb, p["out"], trans_b=True, name="gdn_dgate")
    g_out = _matmul(og, dxb, trans_a=True, out_dtype=MM_DTYPE, name="gdn_gw_out")
    d_o, d_z, d_nw = _gdn_outgate_bwd(o, pm, p["nw"], d_og, name="gdn_outgate_bwd")
    cts = _gdn_scan_bwd(prep, states, d_o, name="gdn_scan_bwd")
    dqkv, dgb, dbb = _gdn_prep_bwd(qkv, gb, bb, rest, cts, name="gdn_prep_bwd")
    d_ab, d_alog, d_dt = _gdn_gates_bwd(ab, p["alog"], p["dt"], dgb, dbb, name="gdn_gates_bwd")
    d_conv, g_conv = _gdn_conv_bwd(pm, p["conv"], dqkv, name="gdn_conv_bwd")
    d_hn = _matmul(d_conv, p["in_t"], b_rows=(0, GDN_QKV), name="gdn_dhn_qkv")
    d_hn = _matmul(d_z, p["in_t"], b_rows=(GDN_QKV, GDN_MAIN - GDN_QKV), add=d_hn, name="gdn_dhn_z")
    dx_new, dxb_new, g_norm = _matmul(d_ab, p["gates_t"], add=d_hn, norm_bwd=(x, nrow, dx), name="gdn_dhn_gates_norm")
    g_in_t = jnp.concatenate([
        _matmul(d_conv, hn, trans_a=True, out_dtype=MM_DTYPE, name="gdn_gw_qkv"),
        _matmul(d_z, hn, trans_a=True, out_dtype=MM_DTYPE, name="gdn_gw_z"),
        _matmul(d_ab, hn, trans_a=True, out_dtype=MM_DTYPE, name="gdn_gw_gates")[:2 * GDN_HEADS],
    ], axis=0)
    grads = dict(w_in=g_in_t, conv=g_conv, a_log=d_alog[0, :GDN_HEADS], dt_bias=d_dt[0, :GDN_HEADS], norm_w=d_nw[0],
                 w_out=g_out, norm=g_norm[0])
    return dx_new, dxb_new, grads


def _dil_layer_fwd(x, nrow, p, hn=None, next_row=None):
    if hn is None:
        hn = _rmsnorm_fwd(x, nrow, name="rmsnorm_fwd")
    slabs = [_matmul(hn, p["wg_t"][g], trans_b=True, spread_out=True, name="dil_proj_in") for g in range(len(DIL_GROUPS))]
    oe = [(_dil_attn_fwd if DIL_GROUPS[g][1] == 1 else _dil_attn_fwd_strided)(
        slabs[g], p["wq"], p["wk"], group=g, name=f"dil_attn_fwd_g{g}") for g in range(len(DIL_GROUPS))]
    y, om = _dil_merge_fwd(oe, name="dil_merge_fwd")
    out, hn_next = _residual_out(y, p["out_t"], x, next_row, trans_b=True, name="dil_proj_out")
    return out, (x, hn, slabs, y, om), hn_next


def _dil_layer_bwd(dx, dxb, nrow, p, saved):
    x, hn, slabs, y, om = saved
    d_y = _matmul(dxb, p["out_t"], name="dil_dmerged")
    g_out_t = _matmul(dxb, y, trans_a=True, out_dtype=MM_DTYPE, name="dil_gw_out")
    g_out_t = g_out_t.reshape(D_MODEL, DIL_HEADS, LANES)[..., :DIL_DH].reshape(D_MODEL, DIL_HEADS * DIL_DH)
    stat = _dil_merge_bwd(d_y, om, name="dil_merge_bwd")
    d_hn = None
    dwq = jnp.zeros((1, LANES), F32)
    dwk = jnp.zeros((1, LANES), F32)
    g_groups = []
    wide = DIL_HEADS * LANES
    for g in range(len(DIL_GROUPS)):
        last = dict(norm_bwd=(x, nrow, dx)) if g == len(DIL_GROUPS) - 1 else {}
        if DIL_GROUPS[g][1] == 1:
            dslab, dwq, dwk = _dil_attn_bwd(slabs[g], stat, p["wq"], p["wk"], dwq, dwk, group=g, name=f"dil_attn_bwd_g{g}")
            d_hn = _matmul(dslab, p["wg_t"][g], packed_a=True, add=d_hn, name="dil_dhn", **last)
            g_w = _matmul(dslab, hn, trans_a=True, packed_a=True, out_dtype=MM_DTYPE, name="dil_gw_in")
        else:
            dparts, dwq, dwk = _dil_attn_bwd_strided(slabs[g], stat, p["wq"], p["wk"], dwq, dwk, group=g,
                                                     name=f"dil_attn_bwd_g{g}")
            d_hn = _matmul(dparts, p["wg_t"][g], a_lead="k", packed_a=True, add=d_hn,
                           name="dil_dhn_parts_norm" if last else "dil_dhn_parts", **last)
            g_w = _matmul(dparts, hn, trans_a=True, a_lead="i", packed_a=True, out_dtype=MM_DTYPE, name="dil_gw_in_parts")
        g_groups.append(g_w.reshape(3, DIL_HEADS, DIL_DH, D_MODEL))
    g_in_t = jnp.stack(g_groups, axis=1).reshape(3 * len(DIL_GROUPS) * DIL_HEADS * DIL_DH, D_MODEL)
    dx_new, dxb_new, g_norm = d_hn
    grads = dict(w_in=g_in_t, q_norm=dwq[0, :DIL_DH], k_norm=dwk[0, :DIL_DH], w_out=g_out_t, norm=g_norm[0])
    return dx_new, dxb_new, grads


def _ffn_layer_fwd(x, nrow, p, hn=None, next_row=None):
    if hn is None:
        hn = _rmsnorm_fwd(x, nrow, name="rmsnorm_fwd")
    gate, up, act = _ffn_in(hn, p["in_t"], name="ffn_proj_in")
    y, hn_next = _residual_out(act, p["out"], x, next_row, name="ffn_proj_out")
    return y, (x, hn, gate, up, act), hn_next


def _ffn_layer_bwd(dx, dxb, nrow, p, saved):
    x, hn, gate, up, act = saved
    g_out = _matmul(act, dxb, trans_a=True, out_dtype=MM_DTYPE, name="ffn_gw_out")
    d_gu = _ffn_dact(dxb, p["out"], gate, up, name="ffn_dact")
    dx_new, dxb_new, g_norm = _matmul(d_gu, p["in_t"], a_lead="k", norm_bwd=(x, nrow, dx), name="ffn_dhn_norm")
    g_in_t = _matmul(d_gu, hn, trans_a=True, a_lead="i", out_dtype=MM_DTYPE, name="ffn_gw_in")
    return dx_new, dxb_new, dict(w_in=g_in_t, w_out=g_out, norm=g_norm[0])


def _mixer_fwd(i, x, mix_row, prepared, hn=None, next_row=None):
    if i % 2 == 0:
        return _gdn_layer_fwd(x, mix_row, prepared["gdn"][i // 2], hn, next_row)
    return _dil_layer_fwd(x, mix_row, prepared["dil"][i // 2], hn, next_row)


def _mixer_bwd(i, dx, dxb, mix_row, prepared, saved, zero=0.0):
    if i % 2 == 0:
        p = prepared["gdn"][i // 2]
        return _gdn_layer_bwd(dx, dxb, mix_row, dict(p, nw=p["nw"] + zero), saved)
    p = prepared["dil"][i // 2]
    return _dil_layer_bwd(dx, dxb, mix_row, dict(p, wq=p["wq"] + zero), saved)


def _local_step(x, target, prepared, norm_mix, norm_ffn):
    saved = []
    hn = None
    for i in range(DEPTH):
        after = norm_mix[i + 1].reshape(1, D_MODEL) if i + 1 < DEPTH else None
        x, s_mix, hn = _mixer_fwd(i, x, norm_mix[i].reshape(1, D_MODEL), prepared, hn, norm_ffn[i].reshape(1, D_MODEL))
        x, s_ffn, hn = _ffn_layer_fwd(x, norm_ffn[i].reshape(1, D_MODEL), prepared["ffn"][i], hn, after)
        saved.append((s_mix, s_ffn))
    dx, dxb, loss = _loss_head(x, target, name="loss_head")
    g_mix, g_ffn = [None] * DEPTH, [None] * DEPTH
    for i in reversed(range(DEPTH)):
        s_mix, s_ffn = saved[i]
        dx, dxb, g_ffn[i] = _ffn_layer_bwd(dx, dxb, norm_ffn[i].reshape(1, D_MODEL), prepared["ffn"][i], s_ffn)
        dx, dxb, g_mix[i] = _mixer_bwd(i, dx, dxb, norm_mix[i].reshape(1, D_MODEL), prepared, s_mix)
    return loss[0, 0], dx, _collect_grads(g_mix, g_ffn)


def _collect_grads(g_mix, g_ffn):
    gdn = [g_mix[i] for i in range(0, DEPTH, 2)]
    dil = [g_mix[i] for i in range(1, DEPTH, 2)]
    if any(g is None for g in g_mix + g_ffn):
        pick = lambda gs, key: [None if g is None else g[key] for g in gs]
        return dict(gdn_w_in=pick(gdn, "w_in"), gdn_w_out=pick(gdn, "w_out"), dil_w_in=pick(dil, "w_in"),
                    dil_w_out=pick(dil, "w_out"), ffn_w_in=pick(g_ffn, "w_in"), ffn_w_out=pick(g_ffn, "w_out"))
    grads = dict(
        norm_mix=jnp.stack([g["norm"] for g in g_mix]),
        norm_ffn=jnp.stack([g["norm"] for g in g_ffn]),
        gdn_w_in=[g["w_in"] for g in gdn],
        gdn_conv_w=jnp.stack([g["conv"] for g in gdn]),
        gdn_a_log=jnp.stack([g["a_log"] for g in gdn]),
        gdn_dt_bias=jnp.stack([g["dt_bias"] for g in gdn]),
        gdn_norm_w=jnp.stack([g["norm_w"] for g in gdn]),
        gdn_w_out=[g["w_out"] for g in gdn],
        dil_w_in=[g["w_in"] for g in dil],
        dil_q_norm=jnp.stack([g["q_norm"] for g in dil]),
        dil_k_norm=jnp.stack([g["k_norm"] for g in dil]),
        dil_w_out=[g["w_out"] for g in dil],
        ffn_w_in=[g["w_in"] for g in g_ffn],
        ffn_w_out=[g["w_out"] for g in g_ffn],
    )
    return grads


MESH_ID = pl.DeviceIdType.MESH
ANY_SPACE = pl.BlockSpec(memory_space=pl.ANY)


def _mesh_position():
    return lax.axis_index("x"), lax.axis_index("y"), lax.axis_index("c")


def _flip(pos, k):
    x, y, c = pos
    return (1 - x if k & 4 else x, 1 - y if k & 2 else y, 1 - c if k & 1 else c)


def _linear(pos):
    return 4 * pos[0] + 2 * pos[1] + pos[2]


def _comm_scratch():
    return [pltpu.SemaphoreType.DMA((N_DEV - 1,)), pltpu.SemaphoreType.DMA((N_DEV - 1,)), pltpu.SemaphoreType.DMA(())]


def _all_gather(shard, *, name):
    def body(x_ref, out_ref, send_sems, recv_sems, local_sem):
        me = _mesh_position()
        mine = out_ref.at[_linear(me)]
        local = pltpu.make_async_copy(x_ref, mine, local_sem)
        local.start()
        copies = []
        for k in range(1, N_DEV):
            cp = pltpu.make_async_remote_copy(src_ref=x_ref, dst_ref=mine, send_sem=send_sems.at[k - 1],
                                              recv_sem=recv_sems.at[k - 1], device_id=_flip(me, k), device_id_type=MESH_ID)
            cp.start()
            copies.append(cp)
        for cp in copies:
            cp.wait()
        local.wait()

    return pl.pallas_call(
        body,
        out_shape=jax.ShapeDtypeStruct((N_DEV,) + shard.shape, shard.dtype),
        in_specs=[ANY_SPACE],
        out_specs=ANY_SPACE,
        scratch_shapes=_comm_scratch(),
        name=name,
    )(shard)


def _exchange(parts, *, name):
    def body(p_ref, out_ref, send_sems, recv_sems, local_sem):
        me = _mesh_position()
        mine = out_ref.at[_linear(me)]
        local = pltpu.make_async_copy(p_ref.at[_linear(me)], mine, local_sem)
        local.start()
        copies = []
        for k in range(1, N_DEV):
            peer = _flip(me, k)
            cp = pltpu.make_async_remote_copy(src_ref=p_ref.at[_linear(peer)], dst_ref=mine, send_sem=send_sems.at[k - 1],
                                              recv_sem=recv_sems.at[k - 1], device_id=peer, device_id_type=MESH_ID)
            cp.start()
            copies.append(cp)
        for cp in copies:
            cp.wait()
        local.wait()

    return pl.pallas_call(
        body,
        out_shape=jax.ShapeDtypeStruct(parts.shape, parts.dtype),
        in_specs=[ANY_SPACE],
        out_specs=ANY_SPACE,
        scratch_shapes=_comm_scratch(),
        name=name,
    )(parts)


HBM_SPACE = pl.BlockSpec(memory_space=pltpu.HBM)
SEM_SPACE = pl.BlockSpec(memory_space=pltpu.SEMAPHORE)
DATAFLOW = pltpu.SideEffectType.DATAFLOW_SIDE_EFFECTING


def _split_copies(src_ref, land_ref, send_sems, recv_sems, per_peer):
    me = _mesh_position()
    mine = land_ref.at[_linear(me)]
    copies = []
    for k in range(1, N_DEV):
        peer = _flip(me, k)
        src = src_ref.at[_linear(peer)] if per_peer else src_ref
        copies.append(pltpu.make_async_remote_copy(src_ref=src, dst_ref=mine, send_sem=send_sems.at[k - 1],
                                                   recv_sem=recv_sems.at[k - 1], device_id=peer, device_id_type=MESH_ID))
    return copies


def _travel_start(src, after, *, per_peer, name):
    me = _linear(_mesh_position())
    own = src[me] if per_peer else src
    shape = own.shape
    landing = lax.dynamic_update_slice(lax.empty((N_DEV,) + shape, src.dtype), own[None], (me, 0, 0))

    def body(src_ref, land_ref, after_ref, send_sems, recv_sems, src_thru, land_thru, token):
        for cp in _split_copies(src_ref, land_ref, send_sems, recv_sems, per_peer):
            cp.start()
        token[...] = jnp.zeros_like(token)

    return pl.pallas_call(
        body,
        name=name,
        out_shape=(pltpu.SemaphoreType.DMA((N_DEV - 1,)), pltpu.SemaphoreType.DMA((N_DEV - 1,)),
                   pltpu.HBM(src.shape, src.dtype), pltpu.HBM(landing.shape, landing.dtype),
                   jax.ShapeDtypeStruct((8, LANES), F32)),
        in_specs=(HBM_SPACE, HBM_SPACE, ANY_SPACE),
        out_specs=(SEM_SPACE, SEM_SPACE, HBM_SPACE, HBM_SPACE, pl.BlockSpec(memory_space=pltpu.VMEM)),
        input_output_aliases={0: 2, 1: 3},
        compiler_params=pltpu.CompilerParams(has_side_effects=DATAFLOW),
    )(pltpu.with_memory_space_constraint(src, pltpu.HBM), pltpu.with_memory_space_constraint(landing, pltpu.HBM), after)


def _travel_wait(started, after, *, per_peer, name):
    send_sems, recv_sems, src_thru, land_thru, _ = started

    def body(src_ref, land_ref, send_sems, recv_sems, after_ref, src_dead, got_ref):
        for cp in _split_copies(src_ref, land_ref, send_sems, recv_sems, per_peer):
            cp.wait_send()
            cp.wait_recv()

    return pl.pallas_call(
        body,
        name=name,
        out_shape=(pltpu.HBM(src_thru.shape, src_thru.dtype), pltpu.HBM(land_thru.shape, land_thru.dtype)),
        in_specs=(HBM_SPACE, HBM_SPACE, SEM_SPACE, SEM_SPACE, ANY_SPACE),
        out_specs=(HBM_SPACE, HBM_SPACE),
        input_output_aliases={0: 0, 1: 1},
        compiler_params=pltpu.CompilerParams(has_side_effects=DATAFLOW),
    )(src_thru, land_thru, send_sems, recv_sems, after)[1]


def _adamw(parts, w, m, v, *, name):
    rows, n = w.shape
    tb = _pick(rows, (PACK_ROW_ALIGN, 16))
    c1 = 1.0 - ADAM_B1 ** ADAM_STEP
    c2 = 1.0 - ADAM_B2 ** ADAM_STEP

    def body(p_ref, w_ref, m_ref, v_ref, g_ref, d_ref, nm_ref, nv_ref):
        g = p_ref[0].astype(F32)
        for s in range(1, N_DEV):
            g = g + p_ref[s].astype(F32)
        m_new = ADAM_B1 * m_ref[...] + (1.0 - ADAM_B1) * g
        v_new = ADAM_B2 * v_ref[...] + (1.0 - ADAM_B2) * (g * g)
        m_hat = m_new / c1
        v_hat = v_new / c2
        g_ref[...] = g
        nm_ref[...] = m_new
        nv_ref[...] = v_new
        d_ref[...] = -ADAM_LR * (m_hat / (jnp.sqrt(v_hat) + ADAM_EPS) + ADAM_WD * w_ref[...])

    blk = pl.BlockSpec((tb, n), lambda i: (i, 0))
    return pl.pallas_call(
        body,
        grid=(rows // tb,),
        in_specs=[pl.BlockSpec((N_DEV, tb, n), lambda i: (0, i, 0)), blk, blk, blk],
        out_specs=[blk] * 4,
        out_shape=[jax.ShapeDtypeStruct((rows, n), F32)] * 4,
        compiler_params=_cparams(("parallel",)),
        name=name,
    )(parts, w, m, v)


PACK_WIDTH = 1024
SHARDED = {
    "gdn_w_in": ((2, D_MODEL, GDN_IN_WIDTH), 2),
    "gdn_conv_w": ((2, GDN_CONV, GDN_QKV), 2),
    "gdn_w_out": ((2, GDN_HEADS * GDN_DV, D_MODEL), 1),
    "dil_w_in": ((2, D_MODEL, 3 * len(DIL_GROUPS) * DIL_HEADS * DIL_DH), 2),
    "dil_w_out": ((2, DIL_HEADS * DIL_DH, D_MODEL), 2),
    "ffn_w_in": ((DEPTH, D_MODEL, 2 * FFN_HIDDEN), 2),
    "ffn_w_out": ((DEPTH, FFN_HIDDEN, D_MODEL), 1),
}
REPLICATED = {"norm_mix": (DEPTH, D_MODEL), "norm_ffn": (DEPTH, D_MODEL), "gdn_a_log": (2, GDN_HEADS),
              "gdn_dt_bias": (2, GDN_HEADS), "gdn_norm_w": (2, GDN_DV), "dil_q_norm": (2, DIL_DH), "dil_k_norm": (2, DIL_DH)}
WEIGHT_ORDER = ("norm_mix", "norm_ffn", "gdn_w_in", "gdn_conv_w", "gdn_a_log", "gdn_dt_bias", "gdn_norm_w", "gdn_w_out",
                "dil_w_in", "dil_q_norm", "dil_k_norm", "dil_w_out", "ffn_w_in", "ffn_w_out")
PACK_ROW_ALIGN = 128
PIECE_ALIGN = 16
SMALL_ROWS = 16


def _shard_shape(name):
    shape, axis = SHARDED[name]
    return tuple(s // N_DEV if i == axis else s for i, s in enumerate(shape))


def _shard_rows(name):
    return math.prod(_shard_shape(name)) // PACK_WIDTH


def _split_shards(full, name):
    shape, axis = SHARDED[name]
    split = full.reshape(shape[:axis] + (N_DEV, shape[axis] // N_DEV) + shape[axis + 1:])
    return jnp.moveaxis(split, axis, 0)


def _join_shards(stacked, name):
    shape, axis = SHARDED[name]
    return jnp.moveaxis(stacked, 0, axis).reshape(shape)


COLUMN_SHARDED = ("gdn_w_in", "dil_w_in", "dil_w_out", "ffn_w_in")


def _to_rows(shard, name):
    if name in COLUMN_SHARDED:
        shard = jnp.swapaxes(shard, 1, 2)
    return shard.reshape(-1, PACK_WIDTH)


def _layer_columns(name):
    _, r, c = _shard_shape(name)
    return r if name in COLUMN_SHARDED else c


def _piece_rows(piece, halves=1):
    name, layer = piece
    rows = _shard_rows(name) * halves
    return rows if layer is None else rows // SHARDED[name][0][0]


def _aligned(rows, to=PIECE_ALIGN):
    return -(-rows // to) * to


def _pack_pieces(arrays, total_align=PIECE_ALIGN):
    padded, total = [], 0
    for a in arrays:
        rows = a.shape[-2]
        extra = _aligned(rows) - rows
        if extra:
            a = jnp.pad(a, [(0, 0)] * (a.ndim - 2) + [(0, extra), (0, 0)])
        padded.append(a)
        total += rows + extra
    tail = _aligned(total, total_align) - total
    if tail:
        padded.append(jnp.zeros(padded[0].shape[:-2] + (tail, PACK_WIDTH), padded[0].dtype))
    return jnp.concatenate(padded, axis=-2)


def _piece_offsets(pieces, halves=None):
    out, at = [], 0
    for p in pieces:
        rows = _piece_rows(p, (halves or {}).get(p[0], 1))
        out.append((p, at, rows))
        at += _aligned(rows)
    return out


def _shard_piece_rows(src, piece):
    name, layer = piece
    part = src[name] if layer is None else src[name][layer:layer + 1]
    return _to_rows(part.astype(F32), name)


def _piece_from_rows(rows, piece):
    name, layer = piece
    layers, r, c = _shard_shape(name)
    n_l = layers if layer is None else 1
    if name in COLUMN_SHARDED:
        return jnp.swapaxes(rows.reshape(n_l, c, r), 1, 2)
    return rows.reshape(n_l, r, c)


SMALL_TAIL = tuple(n for n in REPLICATED if n not in ("norm_mix", "norm_ffn"))


def _pack_small(vals):
    tail, at = jnp.zeros((PACK_WIDTH,), F32), 0
    for n in SMALL_TAIL:
        vec = vals[n].astype(F32).reshape(-1)
        tail = tail + jnp.pad(vec, (at, PACK_WIDTH - at - vec.shape[0]))
        at += vec.shape[0]
    buf = jnp.pad(vals["norm_mix"].astype(F32), ((0, SMALL_ROWS - DEPTH), (0, 0)))
    buf = buf + jnp.pad(vals["norm_ffn"].astype(F32), ((8, SMALL_ROWS - 8 - DEPTH), (0, 0)))
    return buf + jnp.pad(tail.reshape(1, PACK_WIDTH), ((SMALL_ROWS - 1, 0), (0, 0)))


def _unpack_small(buf):
    out = {"norm_mix": buf[0:DEPTH], "norm_ffn": buf[8:8 + DEPTH]}
    at = 0
    for n in SMALL_TAIL:
        size = math.prod(REPLICATED[n])
        out[n] = buf[SMALL_ROWS - 1, at:at + size].reshape(REPLICATED[n])
        at += size
    return out


GATHER_FIRST = (("gdn_w_in", 0), ("gdn_conv_w", None), ("gdn_w_out", 0))
GATHER_NEXT = (("ffn_w_in", 0), ("ffn_w_out", 0), ("dil_w_in", 0), ("dil_w_out", 0))
GATHER_LAST = (("ffn_w_in", 1), ("ffn_w_out", 1), ("gdn_w_in", 1), ("gdn_w_out", 1), ("ffn_w_in", 2), ("ffn_w_out", 2),
               ("dil_w_in", 1), ("dil_w_out", 1), ("ffn_w_in", 3), ("ffn_w_out", 3))
EXCHANGE_GROUPS = (
    (("ffn_w_in", 3), ("ffn_w_out", 3), ("dil_w_in", 1), ("dil_w_out", 1),
     ("ffn_w_in", 2), ("ffn_w_out", 2), ("gdn_w_in", 1), ("gdn_w_out", 1)),
    (("ffn_w_in", 1), ("ffn_w_out", 1), ("dil_w_in", 0), ("dil_w_out", 0)),
    (("ffn_w_in", 0), ("ffn_w_out", 0)),
    (("gdn_w_in", 0), ("gdn_w_out", 0), ("gdn_conv_w", None)),
)
EXCHANGE_AFTER = {("mix", 2): 0, ("mix", 1): 1, ("ffn", 0): 2}


def _gather_operand(w, pieces):
    arrays = []
    for n, layer in pieces:
        if layer is None:
            arrays.append(lax.bitcast_convert_type(w[n], BF16).reshape(-1, PACK_WIDTH))
        else:
            arrays.append(_to_rows(w[n][layer:layer + 1].astype(BF16), n))
    return _pack_pieces(arrays)


def _gathered_weights(gathered, pieces, full):
    for (n, layer), at, rows in _piece_offsets(pieces, halves={"gdn_conv_w": 2}):
        block = gathered[:, at:at + rows]
        if layer is None:
            block = lax.bitcast_convert_type(block.reshape((N_DEV,) + _shard_shape(n) + (2,)), F32)
            full[n] = _join_shards(block, n)
        else:
            full.setdefault(n, {})[layer] = block.reshape(-1, _layer_columns(n))
    return full


def _exchange_operand(grads, pieces):
    arrays = []
    for n, layer in pieces:
        if layer is None:
            arrays.append(_split_shards(grads[n], n).astype(BF16).reshape(N_DEV, -1, PACK_WIDTH))
        else:
            arrays.append(grads[n][layer].astype(BF16).reshape(N_DEV, -1, PACK_WIDTH))
    return _pack_pieces(arrays, total_align=PACK_ROW_ALIGN)


def _update_group(received, pieces, w, m, v, *, name):
    packed = [_pack_pieces([_shard_piece_rows(src, p) for p in pieces], total_align=PACK_ROW_ALIGN) for src in (w, m, v)]
    outs = _adamw(received, *packed, name=name)
    return {p: tuple(_piece_from_rows(o[at:at + rows], p) for o in outs) for p, at, rows in _piece_offsets(pieces)}


def kernel(x, norm_mix, norm_ffn, gdn_w_in, gdn_conv_w, gdn_a_log, gdn_dt_bias, gdn_norm_w, gdn_w_out, dil_w_in, dil_q_norm, dil_k_norm, dil_w_out, ffn_w_in, ffn_w_out, loss_target, m_norm_mix, m_norm_ffn, m_gdn_w_in, m_gdn_conv_w, m_gdn_a_log, m_gdn_dt_bias, m_gdn_norm_w, m_gdn_w_out, m_dil_w_in, m_dil_q_norm, m_dil_k_norm, m_dil_w_out, m_ffn_w_in, m_ffn_w_out, v_norm_mix, v_norm_ffn, v_gdn_w_in, v_gdn_conv_w, v_gdn_a_log, v_gdn_dt_bias, v_gdn_norm_w, v_gdn_w_out, v_dil_w_in, v_dil_q_norm, v_dil_k_norm, v_dil_w_out, v_ffn_w_in, v_ffn_w_out):
    w = dict(norm_mix=norm_mix, norm_ffn=norm_ffn, gdn_w_in=gdn_w_in, gdn_conv_w=gdn_conv_w, gdn_a_log=gdn_a_log,
             gdn_dt_bias=gdn_dt_bias, gdn_norm_w=gdn_norm_w, gdn_w_out=gdn_w_out, dil_w_in=dil_w_in, dil_q_norm=dil_q_norm,
             dil_k_norm=dil_k_norm, dil_w_out=dil_w_out, ffn_w_in=ffn_w_in, ffn_w_out=ffn_w_out)
    m = dict(norm_mix=m_norm_mix, norm_ffn=m_norm_ffn, gdn_w_in=m_gdn_w_in, gdn_conv_w=m_gdn_conv_w, gdn_a_log=m_gdn_a_log,
             gdn_dt_bias=m_gdn_dt_bias, gdn_norm_w=m_gdn_norm_w, gdn_w_out=m_gdn_w_out, dil_w_in=m_dil_w_in,
             dil_q_norm=m_dil_q_norm, dil_k_norm=m_dil_k_norm, dil_w_out=m_dil_w_out, ffn_w_in=m_ffn_w_in, ffn_w_out=m_ffn_w_out)
    v = dict(norm_mix=v_norm_mix, norm_ffn=v_norm_ffn, gdn_w_in=v_gdn_w_in, gdn_conv_w=v_gdn_conv_w, gdn_a_log=v_gdn_a_log,
             gdn_dt_bias=v_gdn_dt_bias, gdn_norm_w=v_gdn_norm_w, gdn_w_out=v_gdn_w_out, dil_w_in=v_dil_w_in,
             dil_q_norm=v_dil_q_norm, dil_k_norm=v_dil_k_norm, dil_w_out=v_dil_w_out, ffn_w_in=v_ffn_w_in, ffn_w_out=v_ffn_w_out)
    def row(src, i):
        return src[i].reshape(1, D_MODEL)

    first = _all_gather(_gather_operand(w, GATHER_FIRST), name="weight_all_gather_first")
    next_started = _travel_start(_gather_operand(w, GATHER_NEXT), first, per_peer=False, name="weight_gather_start_next")
    last_started = _travel_start(_gather_operand(w, GATHER_LAST), next_started[4], per_peer=False,
                                 name="weight_gather_start_last")
    full = _gathered_weights(first, GATHER_FIRST, {n: w[n] for n in REPLICATED})
    prepared = dict(gdn=_prepare_gdn(full, layers=(0,)))
    h = x[0]
    saved = [None] * DEPTH
    h, s_mix, hn = _mixer_fwd(0, h, row(norm_mix, 0) + last_started[4][0, 0], prepared, None, row(norm_ffn, 0))
    got = _travel_wait(next_started, h, per_peer=False, name="weight_gather_wait_next")
    full = _gathered_weights(got, GATHER_NEXT, full)
    prepared.update(dil=_prepare_dil(full, layers=(0,)), ffn=_prepare_ffn(full, layers=(0,)))
    for i in range(DEPTH):
        if i > 0:
            h, s_mix, hn = _mixer_fwd(i, h, row(norm_mix, i), prepared, hn, row(norm_ffn, i))
        if i == 1:
            got = _travel_wait(last_started, h, per_peer=False, name="weight_gather_wait_last")
            full = _gathered_weights(got, GATHER_LAST, full)
            prepared["gdn"].update(_prepare_gdn(full, layers=(1,)))
            prepared["dil"].update(_prepare_dil(full, layers=(1,)))
            prepared["ffn"].update(_prepare_ffn(full, layers=(1, 2, 3)))
        h, s_ffn, hn = _ffn_layer_fwd(h, row(norm_ffn, i), prepared["ffn"][i], hn,
                                      row(norm_mix, i + 1) if i + 1 < DEPTH else None)
        saved[i] = (s_mix, s_ffn)
    dx, dxb, loss = _loss_head(h, loss_target[0], name="loss_head")

    g_mix, g_ffn = [None] * DEPTH, [None] * DEPTH
    started = {}

    def travel(group):
        operand = _exchange_operand(_collect_grads(g_mix, g_ffn), EXCHANGE_GROUPS[group])
        started[group] = _travel_start(operand, dx, per_peer=True, name=f"grad_exchange_start_{group}")
        return started[group][4][0, 0]

    zero = 0.0
    for i in reversed(range(DEPTH)):
        s_mix, s_ffn = saved[i]
        dx, dxb, g_ffn[i] = _ffn_layer_bwd(dx, dxb, row(norm_ffn, i) + zero, prepared["ffn"][i], s_ffn)
        zero = travel(EXCHANGE_AFTER[("ffn", i)]) if ("ffn", i) in EXCHANGE_AFTER else 0.0
        dx, dxb, g_mix[i] = _mixer_bwd(i, dx, dxb, row(norm_mix, i), prepared, s_mix, zero)
        zero = travel(EXCHANGE_AFTER[("mix", i)]) if ("mix", i) in EXCHANGE_AFTER else 0.0
    grads = _collect_grads(g_mix, g_ffn)
    received = [_travel_wait(started[g], dx, per_peer=True, name=f"grad_exchange_wait_{g}") for g in sorted(started)]
    received.append(_exchange(_exchange_operand(grads, EXCHANGE_GROUPS[-1]), name="grad_exchange_last"))
    updated = {}
    for g, pieces in enumerate(EXCHANGE_GROUPS):
        updated.update(_update_group(received[g], pieces, w, m, v, name=f"adamw_sharded_{g}"))

    small_parts = _all_gather(_pack_small(grads), name="small_grad_all_gather")
    outs_small = [_unpack_small(o) for o in
                  _adamw(small_parts, _pack_small(w), _pack_small(m), _pack_small(v), name="adamw_replicated")]

    total_loss = lax.psum(loss[0, 0], ("x", "y", "c"))
    result = [total_loss, dx[None]]
    for k in range(4):
        for n in WEIGHT_ORDER:
            if n not in SHARDED:
                result.append(outs_small[k][n])
            elif (n, None) in updated:
                result.append(updated[(n, None)][k])
            else:
                result.append(jnp.concatenate([updated[(n, l)][k] for l in range(SHARDED[n][0][0])], axis=0))
    return tuple(result)
```

```python
import functools
import math

import jax
import jax.numpy as jnp
from jax import lax
from jax.experimental import pallas as pl
from jax.experimental.pallas import tpu as pltpu

F32 = jnp.float32
BF16 = jnp.bfloat16
MM_DTYPE = BF16

N_DEV = 8
D_MODEL = 1024
DEPTH = 4
RMS_EPS = 1e-6
L2_EPS = 1e-6

LANES = 128

GDN_HEADS = 8
GDN_DK = 128
GDN_DV = 128
GDN_CONV = 4
GDN_CHUNK = 128
GDN_QKV = 3 * GDN_HEADS * GDN_DK
GDN_MAIN = GDN_QKV + GDN_HEADS * GDN_DV
GDN_IN_WIDTH = GDN_MAIN + 2 * GDN_HEADS

DIL_GROUPS = ((128, 1), (512, 4), (2048, 16))
DIL_HEADS = 8
DIL_DH = 64
DIL_SPAN = 128
DIL_SLAB = 3 * DIL_HEADS * LANES
ALIBI_MAX_BIAS = 8.0

FFN_HIDDEN = 2816

ADAM_LR = 0.001
ADAM_B1 = 0.9
ADAM_B2 = 0.999
ADAM_EPS = 1e-08
ADAM_WD = 0.01
ADAM_STEP = 10

VMEM_LIMIT = 56 * 1024 * 1024
ROW_TILE = 512
MATMUL_VMEM_BUDGET = 40 * 1024 * 1024
NEG = -1e30


def _cparams(sem):
    return pltpu.CompilerParams(dimension_semantics=sem, vmem_limit_bytes=VMEM_LIMIT)


def _single_pass(a, b, a_dim, b_dim):
    lead = a.ndim - 2
    batch = ((0,), (0,)) if lead else ((), ())
    return lax.dot_general(a.astype(BF16), b.astype(BF16), (((lead + a_dim,), (lead + b_dim,)), batch),
                           preferred_element_type=F32)


def _bdot(a, b):
    return _single_pass(a, b, 1, 0)


def _bdot_nt(a, b):
    return _single_pass(a, b, 1, 1)


def _bdot_tn(a, b):
    return _single_pass(a, b, 0, 0)


def _pick(n, candidates):
    for c in candidates:
        if n % c == 0:
            return c
    raise ValueError(f"no tile for {n}")


HALF = LANES // 2


def _pack_head_pairs(x):
    x = x.astype(F32)
    tiles = [x[:, (2 * i) * LANES:(2 * i + 1) * LANES] + pltpu.roll(x[:, (2 * i + 1) * LANES:(2 * i + 2) * LANES], HALF, 1)
             for i in range(x.shape[1] // (2 * LANES))]
    return tiles[0] if len(tiles) == 1 else jnp.concatenate(tiles, axis=1)


def _spread_head_pairs(y):
    low = lax.broadcasted_iota(jnp.int32, (y.shape[0], LANES), 1) < HALF
    tiles = []
    for i in range(y.shape[1] // LANES):
        pair = y[:, i * LANES:(i + 1) * LANES]
        tiles += [jnp.where(low, pair, 0.0), jnp.where(low, pltpu.roll(pair, HALF, 1), 0.0)]
    return jnp.concatenate(tiles, axis=1)


def _matmul(a, b, *, name, trans_a=False, trans_b=False, b_rows=None, a_lead=None, add=None, out_dtype=F32,
            packed_a=False, spread_out=False, norm_bwd=None, norm_fwd=None):
    if trans_a:
        k_dim, m_dim = a.shape[-2:]
        m_dim = m_dim // 2 if packed_a else m_dim
    else:
        m_dim, k_dim = a.shape[-2:]
        k_dim = k_dim // 2 if packed_a else k_dim
    slab_m, slab_k = m_dim, k_dim
    if a_lead == "k":
        assert not trans_a
        k_dim *= a.shape[0]
    elif a_lead == "i":
        assert trans_a
        m_dim *= a.shape[0]
    b_start, b_size = b_rows if b_rows is not None else (0, b.shape[0])
    if trans_b:
        n_dim, k2 = b_size, b.shape[1]
    else:
        k2, n_dim = b_size, b.shape[1]
    assert k_dim == k2, (a.shape, b.shape, b_rows)
    tn = _pick(n_dim, (1024, 512, 256, 128))
    tm = min(slab_m, 2048, max(512, (1024 * 1024) // tn))
    tm = _pick(slab_m, (tm, 1408, 1024, 512, 256, 128))
    out_bytes = jnp.dtype(out_dtype).itemsize * (2 if spread_out else 1)
    if norm_bwd is not None:
        out_bytes = 4 + 4 + 4 + 2
        tm = min(tm, 512)
    if norm_fwd is not None:
        out_bytes += 2

    def deepest(rows):
        fixed = rows * tn * (2 * out_bytes + 4 + (8 if add is not None else 0))
        fits = lambda c: fixed + 2 * 2 * c * ((2 if packed_a else 1) * rows + tn) <= MATMUL_VMEM_BUDGET
        return _pick(slab_k, tuple(c for c in (3072, 2816, 2048, 1536, 1408, 1024, 512, 256) if fits(c)) + (128,))

    tk = deepest(tm)
    if tm % 1024 == 0 and deepest(tm // 2) > tk:
        tm, tk = tm // 2, deepest(tm // 2)
    nk = k_dim // tk
    has_add = add is not None
    dn = (((0 if trans_a else 1,), (1 if trans_b else 0,)), ((), ()))
    b_tile = tn if trans_b else tk
    assert b_start % b_tile == 0, (b_rows, b_tile)
    b_off = b_start // b_tile

    has_norm = norm_bwd is not None
    also_norm = norm_fwd is not None
    if has_norm or also_norm:
        assert n_dim == tn and not spread_out and not (has_norm and also_norm)

    def body(*refs):
        refs = list(refs)
        a_ref, b_ref = refs[:2]
        add_ref = refs[2] if has_add else None
        rest = refs[2 + has_add:]
        if has_norm:
            x_ref, w_ref, skip_ref, dx_ref, dxb_ref, dw_ref, acc_ref = rest
        elif also_norm:
            w_ref, o_ref, hn_ref, acc_ref = rest
        else:
            o_ref, acc_ref = rest
        a_blk = _pack_head_pairs(a_ref[...]).astype(a_ref.dtype) if packed_a else a_ref[...]
        part = lax.dot_general(a_blk, b_ref[...], dn, preferred_element_type=F32)
        first_rows = pl.program_id(0) == 0

        def finish(total):
            if has_add:
                total = total + add_ref[...]
            if has_norm:
                xf = x_ref[...]
                r = lax.rsqrt(jnp.mean(xf * xf, axis=-1, keepdims=True) + RMS_EPS)
                gw = total * w_ref[...]
                dx = r * gw - xf * (r * r * r * jnp.mean(gw * xf, axis=-1, keepdims=True)) + skip_ref[...]
                dx_ref[...] = dx
                dxb_ref[...] = dx.astype(dxb_ref.dtype)
                rows = jnp.sum(total * xf * r, axis=0, keepdims=True)

                @pl.when(first_rows)
                def _():
                    dw_ref[...] = rows

                @pl.when(jnp.logical_not(first_rows))
                def _():
                    dw_ref[...] += rows
                return
            if spread_out:
                total = _spread_head_pairs(total)
            o_ref[...] = total.astype(out_dtype)
            if also_norm:
                r = lax.rsqrt(jnp.mean(total * total, axis=-1, keepdims=True) + RMS_EPS)
                hn_ref[...] = (total * r * w_ref[...]).astype(hn_ref.dtype)

        if nk == 1:
            finish(part)
        else:
            k = pl.program_id(2)

            @pl.when(k == 0)
            def _():
                acc_ref[...] = part

            @pl.when(k > 0)
            def _():
                acc_ref[...] += part

            @pl.when(k == nk - 1)
            def _():
                finish(acc_ref[...])

    wide = 2 if packed_a else 1
    a_tile = (tk, wide * tm) if trans_a else (tm, wide * tk)
    a_at = (lambda i, j, k: (k, i)) if trans_a else (lambda i, j, k: (i, k))
    if a_lead is None:
        a_spec = pl.BlockSpec(a_tile, a_at)
    elif a_lead == "k":
        per = slab_k // tk
        a_spec = pl.BlockSpec((None,) + a_tile, lambda i, j, k: (k // per, i, k % per))
    elif a_lead == "i":
        per = slab_m // tm
        a_spec = pl.BlockSpec((None,) + a_tile, lambda i, j, k: (i // per, k, i % per))
    else:
        a_spec = pl.BlockSpec((None,) + a_tile, lambda i, j, k: (a_lead,) + a_at(i, j, k))
    if trans_b:
        b_spec = pl.BlockSpec((tn, tk), lambda i, j, k: (j + b_off, k))
    else:
        b_spec = pl.BlockSpec((tk, tn), lambda i, j, k: (k + b_off, j))
    in_specs = [a_spec, b_spec]
    args = [a, b]
    tile = pl.BlockSpec((tm, tn), lambda i, j, k: (i, j))
    if has_add:
        in_specs.append(tile)
        args.append(add)
    scratch = [pltpu.VMEM((tm, tn) if nk > 1 else (8, LANES), F32)]
    if has_norm:
        x, w_row, dskip = norm_bwd
        one = pl.BlockSpec((1, tn), lambda i, j, k: (0, 0))
        return pl.pallas_call(
            body,
            grid=(m_dim // tm, 1, nk),
            in_specs=in_specs + [tile, one, tile],
            out_specs=[tile, tile, one],
            out_shape=[jax.ShapeDtypeStruct((m_dim, n_dim), F32), jax.ShapeDtypeStruct((m_dim, n_dim), MM_DTYPE),
                       jax.ShapeDtypeStruct((1, n_dim), F32)],
            scratch_shapes=scratch,
            compiler_params=_cparams(("arbitrary", "arbitrary", "arbitrary")),
            name=name,
        )(*args, x, w_row, dskip)
    if also_norm:
        return pl.pallas_call(
            body,
            grid=(m_dim // tm, 1, nk),
            in_specs=in_specs + [pl.BlockSpec((1, tn), lambda i, j, k: (0, 0))],
            out_specs=[tile, tile],
            out_shape=[jax.ShapeDtypeStruct((m_dim, n_dim), out_dtype), jax.ShapeDtypeStruct((m_dim, n_dim), MM_DTYPE)],
            scratch_shapes=scratch,
            compiler_params=_cparams(("parallel", "parallel", "arbitrary")),
            name=name,
        )(*args, norm_fwd)
    return pl.pallas_call(
        body,
        grid=(m_dim // tm, n_dim // tn, nk),
        in_specs=in_specs,
        out_specs=pl.BlockSpec((tm, (2 if spread_out else 1) * tn), lambda i, j, k: (i, j)),
        out_shape=jax.ShapeDtypeStruct((m_dim, (2 if spread_out else 1) * n_dim), out_dtype),
        scratch_shapes=scratch,
        compiler_params=_cparams(("parallel", "parallel", "arbitrary")),
        name=name,
    )(*args)


def _rmsnorm_fwd(x, w_row, *, name):
    t, d = x.shape
    tb = min(t, 1024)

    def body(x_ref, w_ref, o_ref):
        xf = x_ref[...]
        r = lax.rsqrt(jnp.mean(xf * xf, axis=-1, keepdims=True) + RMS_EPS)
        o_ref[...] = (xf * r * w_ref[...]).astype(o_ref.dtype)

    return pl.pallas_call(
        body,
        grid=(t // tb,),
        in_specs=[pl.BlockSpec((tb, d), lambda i: (i, 0)), pl.BlockSpec((1, d), lambda i: (0, 0))],
        out_specs=pl.BlockSpec((tb, d), lambda i: (i, 0)),
        out_shape=jax.ShapeDtypeStruct((t, d), MM_DTYPE),
        compiler_params=_cparams(("parallel",)),
        name=name,
    )(x, w_row)


def _silu(z):
    return z / (1.0 + jnp.exp(-z))


FFN_TM, FFN_TN = 512, 1408


def _ffn_in(hn, in_t, *, name):
    t, d = hn.shape
    h = FFN_HIDDEN
    tm, tn = min(t, FFN_TM), FFN_TN
    nj = h // tn
    dn = (((1,), (1,)), ((), ()))

    def body(a_ref, bg_ref, bu_ref, g_ref, u_ref, act_ref):
        a = a_ref[...]
        g = lax.dot_general(a, bg_ref[...], dn, preferred_element_type=F32)
        u = lax.dot_general(a, bu_ref[...], dn, preferred_element_type=F32)
        g_ref[...] = g.astype(g_ref.dtype)
        u_ref[...] = u.astype(u_ref.dtype)
        act_ref[...] = (_silu(g) * u).astype(act_ref.dtype)

    out = pl.BlockSpec((tm, tn), lambda j, i: (i, j))
    return pl.pallas_call(
        body,
        grid=(nj, t // tm),
        in_specs=[pl.BlockSpec((tm, d), lambda j, i: (i, 0)), pl.BlockSpec((tn, d), lambda j, i: (j, 0)),
                  pl.BlockSpec((tn, d), lambda j, i: (j + nj, 0))],
        out_specs=[out, out, out],
        out_shape=[jax.ShapeDtypeStruct((t, h), MM_DTYPE)] * 3,
        compiler_params=_cparams(("parallel", "parallel")),
        name=name,
    )(hn, in_t, in_t)


def _ffn_dact(dy, out_w, g, u, *, name):
    t, d = dy.shape
    h = FFN_HIDDEN
    tm, tn = min(t, FFN_TM), FFN_TN

    def body(a_ref, b_ref, g_ref, u_ref, d_ref):
        da = lax.dot_general(a_ref[...], b_ref[...], (((1,), (1,)), ((), ())), preferred_element_type=F32)
        gate = g_ref[...].astype(F32)
        sig = 1.0 / (1.0 + jnp.exp(-gate))
        sg = gate * sig
        d_ref[0] = (da * u_ref[...].astype(F32) * (sig + sg * (1.0 - sig))).astype(d_ref.dtype)
        d_ref[1] = (da * sg).astype(d_ref.dtype)

    blk = pl.BlockSpec((tm, tn), lambda j, i: (i, j))
    return pl.pallas_call(
        body,
        grid=(h // tn, t // tm),
        in_specs=[pl.BlockSpec((tm, d), lambda j, i: (i, 0)), pl.BlockSpec((tn, d), lambda j, i: (j, 0)), blk, blk],
        out_specs=pl.BlockSpec((2, tm, tn), lambda j, i: (0, i, j)),
        out_shape=jax.ShapeDtypeStruct((2, t, h), MM_DTYPE),
        compiler_params=_cparams(("parallel", "parallel")),
        name=name,
    )(dy, out_w, g, u)


def _loss_head(y, target, *, name):
    t, d = y.shape
    tb = min(t, 1024)

    def body(y_ref, t_ref, dy_ref, dyb_ref, l_ref):
        err = y_ref[...] - t_ref[...]
        dy_ref[...] = err * (1.0 / d)
        dyb_ref[...] = (err * (1.0 / d)).astype(dyb_ref.dtype)
        part = jnp.sum(jnp.sum(err * err, axis=0, keepdims=True), axis=1, keepdims=True) * (0.5 / d)
        part = jnp.broadcast_to(part, l_ref.shape)

        @pl.when(pl.program_id(0) == 0)
        def _():
            l_ref[...] = part

        @pl.when(pl.program_id(0) > 0)
        def _():
            l_ref[...] += part

    row = pl.BlockSpec((tb, d), lambda i: (i, 0))
    return pl.pallas_call(
        body,
        grid=(t // tb,),
        in_specs=[row, row],
        out_specs=[row, row, pl.BlockSpec((8, LANES), lambda i: (0, 0))],
        out_shape=[jax.ShapeDtypeStruct((t, d), F32), jax.ShapeDtypeStruct((t, d), MM_DTYPE),
                   jax.ShapeDtypeStruct((8, LANES), F32)],
        compiler_params=_cparams(("arbitrary",)),
        name=name,
    )(y, target)


CONV_HALO = 8
CONV_TIME_TILE = 2048


def _conv_tile_scale(c):
    is_qk = c < 2 * GDN_HEADS
    scale = jnp.where(c < GDN_HEADS, GDN_DK ** -0.5, 1.0).astype(F32)
    return is_qk, scale


def _gdn_conv_fwd(pm, conv_w, *, name):
    t = pm.shape[0]
    tb = min(t, CONV_TIME_TILE)
    nt = t // tb
    hb = tb // CONV_HALO

    def body(x_ref, xp_ref, w_ref, o_ref, xe_ref):
        c = pl.program_id(0)
        ti = pl.program_id(1)
        xe_ref[0:CONV_HALO, :] = jnp.where(ti > 0, xp_ref[...], 0.0)
        xe_ref[CONV_HALO:CONV_HALO + tb, :] = x_ref[...]
        w = w_ref[...]
        y = jnp.zeros((tb, LANES), F32)
        for j in range(GDN_CONV):
            off = CONV_HALO - (GDN_CONV - 1) + j
            y = y + w[j:j + 1, :] * xe_ref[pl.ds(off, tb), :]
        s = _silu(y)
        is_qk, scale = _conv_tile_scale(c)
        r = lax.rsqrt(jnp.sum(s * s, axis=-1, keepdims=True) + L2_EPS) * scale
        o_ref[...] = s * jnp.where(is_qk, r, 1.0)

    return pl.pallas_call(
        body,
        grid=(GDN_QKV // LANES, nt),
        in_specs=[
            pl.BlockSpec((tb, LANES), lambda c, i: (i, c)),
            pl.BlockSpec((CONV_HALO, LANES), lambda c, i: (jnp.maximum(i * hb - 1, 0), c)),
            pl.BlockSpec((GDN_CONV, LANES), lambda c, i: (0, c)),
        ],
        out_specs=pl.BlockSpec((tb, LANES), lambda c, i: (i, c)),
        out_shape=jax.ShapeDtypeStruct((t, GDN_QKV), F32),
        scratch_shapes=[pltpu.VMEM((tb + CONV_HALO, LANES), F32)],
        compiler_params=_cparams(("parallel", "parallel")),
        name=name,
    )(pm, pm, conv_w)


def _gdn_conv_bwd(pm, conv_w, dout, *, name):
    t = pm.shape[0]
    tb = min(t, CONV_TIME_TILE)
    nt = t // tb
    hb = tb // CONV_HALO
    last_hb = t // CONV_HALO - 1
    ext = tb + CONV_HALO

    def body(x_ref, xp_ref, xn_ref, d_ref, dn_ref, w_ref, dx_ref, dw_ref, xe_ref, dy_ref):
        c = pl.program_id(0)
        ti = pl.program_id(1)
        has_next = ti < nt - 1
        xe_ref[0:CONV_HALO, :] = jnp.where(ti > 0, xp_ref[...], 0.0)
        xe_ref[CONV_HALO:CONV_HALO + tb, :] = x_ref[...]
        xe_ref[CONV_HALO + tb:2 * CONV_HALO + tb, :] = jnp.where(has_next, xn_ref[...], 0.0)
        de = jnp.concatenate([d_ref[...], jnp.where(has_next, dn_ref[...], 0.0)], axis=0)
        w = w_ref[...]
        y = jnp.zeros((ext, LANES), F32)
        for j in range(GDN_CONV):
            off = CONV_HALO - (GDN_CONV - 1) + j
            y = y + w[j:j + 1, :] * xe_ref[pl.ds(off, ext), :]
        sig = 1.0 / (1.0 + jnp.exp(-y))
        s = y * sig
        is_qk, scale = _conv_tile_scale(c)
        r = lax.rsqrt(jnp.sum(s * s, axis=-1, keepdims=True) + L2_EPS)
        n = s * r
        dnrm = de * scale
        ds_qk = r * (dnrm - n * jnp.sum(dnrm * n, axis=-1, keepdims=True))
        ds = jnp.where(is_qk, ds_qk, de)
        dy_ref[...] = ds * (sig + s * (1.0 - sig))
        dy = dy_ref[0:tb, :]
        dx = jnp.zeros((tb, LANES), F32)
        dw_rows = []
        for j in range(GDN_CONV):
            sh = GDN_CONV - 1 - j
            dx = dx + w[j:j + 1, :] * dy_ref[pl.ds(sh, tb), :]
            off = CONV_HALO - (GDN_CONV - 1) + j
            dw_rows.append(jnp.sum(dy * xe_ref[pl.ds(off, tb), :], axis=0, keepdims=True))
        dx_ref[...] = dx.astype(dx_ref.dtype)
        part = jnp.concatenate(dw_rows, axis=0)

        @pl.when(ti == 0)
        def _():
            dw_ref[...] = part

        @pl.when(ti > 0)
        def _():
            dw_ref[...] += part

    main = pl.BlockSpec((tb, LANES), lambda c, i: (i, c))
    prev = pl.BlockSpec((CONV_HALO, LANES), lambda c, i: (jnp.maximum(i * hb - 1, 0), c))
    nxt = pl.BlockSpec((CONV_HALO, LANES), lambda c, i: (jnp.minimum((i + 1) * hb, last_hb), c))
    return pl.pallas_call(
        body,
        grid=(GDN_QKV // LANES, nt),
        in_specs=[main, prev, nxt, main, nxt, pl.BlockSpec((GDN_CONV, LANES), lambda c, i: (0, c))],
        out_specs=[main, pl.BlockSpec((GDN_CONV, LANES), lambda c, i: (0, c))],
        out_shape=[jax.ShapeDtypeStruct((t, GDN_QKV), MM_DTYPE), jax.ShapeDtypeStruct((GDN_CONV, GDN_QKV), F32)],
        scratch_shapes=[pltpu.VMEM((tb + 2 * CONV_HALO, LANES), F32), pltpu.VMEM((ext, LANES), F32)],
        compiler_params=_cparams(("parallel", "arbitrary")),
        name=name,
    )(pm, pm, pm, dout, dout, conv_w)


def _head_selector(first_col):
    row = lax.broadcasted_iota(jnp.int32, (LANES, GDN_HEADS * LANES), 0)
    col = lax.broadcasted_iota(jnp.int32, (LANES, GDN_HEADS * LANES), 1)
    return (col // LANES + first_col == row).astype(BF16)


def _spread_columns(cols, first_col):
    sel = _head_selector(first_col)
    return sum(_bdot(p, sel) for p in _bf16_pieces(cols))


def _gather_columns(wide, first_col):
    sel = _head_selector(first_col)
    return sum(_bdot_nt(p, sel) for p in _bf16_pieces(wide))


def _softplus(x):
    return jnp.maximum(x, 0.0) + jnp.log(1.0 + jnp.exp(-jnp.abs(x)))


def _gdn_gates_fwd(ab, alog_row, dt_row, *, name):
    t = ab.shape[0]
    tb = min(t, 1024)
    wide = GDN_HEADS * LANES

    def body(ab_ref, al_ref, dt_ref, g_ref, b_ref):
        x = ab_ref[...]
        g_cols = -jnp.exp(al_ref[...]) * _softplus(x + dt_ref[...])
        b_cols = 1.0 / (1.0 + jnp.exp(-x))
        g_ref[...] = _spread_columns(g_cols, 0)
        b_ref[...] = _spread_columns(b_cols, GDN_HEADS)

    row = pl.BlockSpec((tb, LANES), lambda i: (i, 0))
    one = pl.BlockSpec((1, LANES), lambda i: (0, 0))
    out = pl.BlockSpec((tb, wide), lambda i: (i, 0))
    return pl.pallas_call(
        body,
        grid=(t // tb,),
        in_specs=[row, one, one],
        out_specs=[out, out],
        out_shape=[jax.ShapeDtypeStruct((t, wide), F32)] * 2,
        compiler_params=_cparams(("parallel",)),
        name=name,
    )(ab, alog_row, dt_row)


def _gdn_gates_bwd(ab, alog_row, dt_row, dgb, dbb, *, name):
    t = ab.shape[0]
    tb = min(t, 1024)
    wide = GDN_HEADS * LANES

    def body(ab_ref, al_ref, dt_ref, dg_ref, db_ref, dab_ref, dal_ref, ddt_ref):
        x = ab_ref[...]
        lane = lax.broadcasted_iota(jnp.int32, (tb, LANES), 1)
        dg_cols = _gather_columns(dg_ref[...], 0)
        db_cols = _gather_columns(db_ref[...], GDN_HEADS)
        ea = jnp.exp(al_ref[...])
        z = x + dt_ref[...]
        sp = _softplus(z)
        sg = 1.0 / (1.0 + jnp.exp(-z))
        beta = 1.0 / (1.0 + jnp.exp(-x))
        da = jnp.where(lane < GDN_HEADS, dg_cols * (-ea) * sg, 0.0)
        db = jnp.where((lane >= GDN_HEADS) & (lane < 2 * GDN_HEADS), db_cols * beta * (1.0 - beta), 0.0)
        dab_ref[...] = (da + db).astype(dab_ref.dtype)
        p_al = jnp.sum(jnp.where(lane < GDN_HEADS, dg_cols * (-ea) * sp, 0.0), axis=0, keepdims=True)
        p_dt = jnp.sum(da, axis=0, keepdims=True)

        @pl.when(pl.program_id(0) == 0)
        def _():
            dal_ref[...] = p_al
            ddt_ref[...] = p_dt

        @pl.when(pl.program_id(0) > 0)
        def _():
            dal_ref[...] += p_al
            ddt_ref[...] += p_dt

    row = pl.BlockSpec((tb, LANES), lambda i: (i, 0))
    one = pl.BlockSpec((1, LANES), lambda i: (0, 0))
    big = pl.BlockSpec((tb, wide), lambda i: (i, 0))
    return pl.pallas_call(
        body,
        grid=(t // tb,),
        in_specs=[row, one, one, big, big],
        out_specs=[row, one, one],
        out_shape=[jax.ShapeDtypeStruct((t, LANES), MM_DTYPE), jax.ShapeDtypeStruct((1, LANES), F32),
                   jax.ShapeDtypeStruct((1, LANES), F32)],
        compiler_params=_cparams(("arbitrary",)),
        name=name,
    )(ab, alog_row, dt_row, dgb, dbb)


@jax.custom_vjp
def _unit_lower_inverse_rest(n):
    c = n.shape[-1]
    ri = lax.broadcasted_iota(jnp.int32, (c, c), 0)
    ci = lax.broadcasted_iota(jnp.int32, (c, c), 1)
    rest = None
    size = 1
    while size < c:
        joins = ((ri // (2 * size)) == (ci // (2 * size))) & ((ri // size) != (ci // size))
        low = jnp.where(joins, n, 0.0)
        if rest is None:
            rest = -low
        else:
            left = low + _bdot(rest, low)
            rest = rest - (left + _bdot(left, rest))
        size *= 2
    return rest


def _unit_lower_inverse_rest_fwd(n):
    rest = _unit_lower_inverse_rest(n)
    return rest, rest


def _unit_lower_inverse_rest_bwd(rest, ct):
    left = ct + _bdot_tn(rest, ct)
    return (-(left + _bdot_nt(left, rest)),)


_unit_lower_inverse_rest.defvjp(_unit_lower_inverse_rest_fwd, _unit_lower_inverse_rest_bwd)


@jax.custom_vjp
def _known_inverse_rest(n, rest):
    return rest


def _known_inverse_rest_fwd(n, rest):
    return rest, rest


def _known_inverse_rest_bwd(rest, ct):
    return _unit_lower_inverse_rest_bwd(rest, ct) + (jnp.zeros_like(rest),)


_known_inverse_rest.defvjp(_known_inverse_rest_fwd, _known_inverse_rest_bwd)


def _bf16_pieces(x):
    hi = x.astype(BF16)
    r1 = x - hi.astype(F32)
    mid = r1.astype(BF16)
    lo = (r1 - mid.astype(F32)).astype(BF16)
    return hi, mid, lo


def _lower_ones(shape):
    c = shape[-1]
    ri = lax.broadcasted_iota(jnp.int32, (c, c), 0)
    ci = lax.broadcasted_iota(jnp.int32, (c, c), 1)
    return jnp.broadcast_to((ri >= ci).astype(BF16), shape)


@jax.custom_vjp
def _running_sum(x):
    tri = _lower_ones(x.shape)
    return sum(_bdot(tri, p) for p in _bf16_pieces(x))


def _running_sum_fwd(x):
    return _running_sum(x), None


def _running_sum_bwd(_, ct):
    tri = _lower_ones(ct.shape)
    return (sum(_bdot_tn(tri, p) for p in _bf16_pieces(ct)),)


_running_sum.defvjp(_running_sum_fwd, _running_sum_bwd)


def _gdn_prep_math(q, k, v, gb, bb, known_rest=None, with_rest=False):
    c = GDN_CHUNK
    ri = lax.broadcasted_iota(jnp.int32, (c, c), 0)
    ci = lax.broadcasted_iota(jnp.int32, (c, c), 1)
    causal = ri >= ci
    gc = _running_sum(gb)
    decay = jnp.exp(jnp.where(causal, gc - jnp.swapaxes(gc, -1, -2), NEG))
    n = jnp.where(ri > ci, _bdot_nt(k, k) * bb * decay, 0.0)
    rest = _unit_lower_inverse_rest(n) if known_rest is None else _known_inverse_rest(n, known_rest)
    eg = jnp.exp(gc)
    rhs_v = v * bb
    rhs_k = k * bb * eg
    u = rhs_v + _bdot(rest, rhs_v)
    w = rhs_k + _bdot(rest, rhs_k)
    qk = _bdot_nt(q, k) * decay
    qd = q * eg
    last = jnp.sum(jnp.where(ri == c - 1, gc, 0.0), axis=-2, keepdims=True)
    gl = jnp.broadcast_to(last, gc.shape)
    kt = k * jnp.exp(gl - gc)
    cd = jnp.exp(gl)
    return (u, w, qk, qd, kt, cd, rest) if with_rest else (u, w, qk, qd, kt, cd)


def _head_tiles(ref, h):
    return ref[:, h * LANES:(h + 1) * LANES]


def _stack_heads(ref, first=0, heads=GDN_HEADS):
    return jnp.stack([_head_tiles(ref, first + h) for h in range(heads)])


def _store_heads(ref, val, first=0):
    for h in range(val.shape[0]):
        ref[:, (first + h) * LANES:(first + h + 1) * LANES] = val[h].astype(ref.dtype)


def _gdn_prep_fwd(qkv, gb, bb, *, name):
    t = qkv.shape[0]
    c = GDN_CHUNK
    wide = GDN_HEADS * LANES

    def body(q_ref, k_ref, v_ref, g_ref, b_ref, *outs):
        res = _gdn_prep_math(*(_stack_heads(r) for r in (q_ref, k_ref, v_ref, g_ref, b_ref)), with_rest=True)
        for o_ref, val in zip(outs, res):
            _store_heads(o_ref, val)

    blk = lambda off: pl.BlockSpec((c, wide), lambda i: (i, off))
    outs = pl.pallas_call(
        body,
        grid=(t // c,),
        in_specs=[blk(0), blk(1), blk(2), blk(0), blk(0)],
        out_specs=[blk(0)] * 7,
        out_shape=[jax.ShapeDtypeStruct((t, wide), dt) for dt in (F32, MM_DTYPE, MM_DTYPE, MM_DTYPE, MM_DTYPE, F32, F32)],
        compiler_params=_cparams(("parallel",)),
        name=name,
    )(qkv, qkv, qkv, gb, bb)
    return tuple(outs[:6]), outs[6]


def _gdn_prep_bwd(qkv, gb, bb, rest, cts, *, name):
    t = qkv.shape[0]
    c = GDN_CHUNK
    wide = GDN_HEADS * LANES

    def body(q_ref, k_ref, v_ref, g_ref, b_ref, r_ref, c0, c1, c2, c3, c4, c5, dqkv_ref, dg_ref, db_ref):
        prim = tuple(_stack_heads(r) for r in (q_ref, k_ref, v_ref, g_ref, b_ref))
        _, pull = jax.vjp(functools.partial(_gdn_prep_math, known_rest=_stack_heads(r_ref)), *prim)
        dq, dk, dv, dg, db = pull(tuple(_stack_heads(r).astype(F32) for r in (c0, c1, c2, c3, c4, c5)))
        _store_heads(dqkv_ref, dq)
        _store_heads(dqkv_ref, dk, first=GDN_HEADS)
        _store_heads(dqkv_ref, dv, first=2 * GDN_HEADS)
        _store_heads(dg_ref, dg)
        _store_heads(db_ref, db)

    blk = lambda off: pl.BlockSpec((c, wide), lambda i: (i, off))
    return pl.pallas_call(
        body,
        grid=(t // c,),
        in_specs=[blk(0), blk(1), blk(2), blk(0), blk(0)] + [blk(0)] * 7,
        out_specs=[pl.BlockSpec((c, 3 * wide), lambda i: (i, 0)), blk(0), blk(0)],
        out_shape=[jax.ShapeDtypeStruct((t, 3 * wide), F32), jax.ShapeDtypeStruct((t, wide), F32),
                   jax.ShapeDtypeStruct((t, wide), F32)],
        compiler_params=_cparams(("parallel",)),
        name=name,
    )(qkv, qkv, qkv, gb, bb, rest, *cts)


def _gdn_scan_math(s, u, w, qk, qd, kt, cd):
    v_new = u - _bdot(w, s)
    o = _bdot(qd, s) + _bdot(qk, v_new)
    s_new = s * cd + _bdot_tn(kt, v_new)
    return o, s_new


def _gdn_scan_fwd(prep, *, name):
    t = prep[0].shape[0]
    c = GDN_CHUNK
    wide = GDN_HEADS * LANES

    def body(u_ref, w_ref, qk_ref, qd_ref, kt_ref, cd_ref, o_ref, st_ref, s_ref):
        @pl.when(pl.program_id(0) == 0)
        def _():
            s_ref[...] = jnp.zeros_like(s_ref)

        s = _stack_heads(s_ref)
        _store_heads(st_ref, s)
        o, s_new = _gdn_scan_math(s, *(_stack_heads(r).astype(F32) for r in (u_ref, w_ref, qk_ref, qd_ref, kt_ref, cd_ref)))
        _store_heads(o_ref, o)
        _store_heads(s_ref, s_new)

    blk = pl.BlockSpec((c, wide), lambda i: (i, 0))
    return pl.pallas_call(
        body,
        grid=(t // c,),
        in_specs=[blk] * 6,
        out_specs=[blk, blk],
        out_shape=[jax.ShapeDtypeStruct((t, wide), F32)] * 2,
        scratch_shapes=[pltpu.VMEM((GDN_DK, wide), F32)],
        compiler_params=_cparams(("arbitrary",)),
        name=name,
    )(*prep)


def _gdn_scan_bwd(prep, states, do, *, name):
    t = do.shape[0]
    c = GDN_CHUNK
    wide = GDN_HEADS * LANES
    nc = t // c

    def body(u_ref, w_ref, qk_ref, qd_ref, kt_ref, cd_ref, st_ref, do_ref, *rest):
        outs, ds_ref = rest[:6], rest[6]

        @pl.when(pl.program_id(0) == 0)
        def _():
            ds_ref[...] = jnp.zeros_like(ds_ref)

        prim = tuple(_stack_heads(r).astype(F32) for r in (st_ref, u_ref, w_ref, qk_ref, qd_ref, kt_ref, cd_ref))
        _, pull = jax.vjp(_gdn_scan_math, *prim)
        grads = pull((_stack_heads(do_ref), _stack_heads(ds_ref)))
        _store_heads(ds_ref, grads[0])
        for o_ref, val in zip(outs, grads[1:]):
            _store_heads(o_ref, val)

    blk = pl.BlockSpec((c, wide), lambda i: (nc - 1 - i, 0))
    return pl.pallas_call(
        body,
        grid=(nc,),
        in_specs=[blk] * 8,
        out_specs=[blk] * 6,
        out_shape=[jax.ShapeDtypeStruct((t, wide), dt) for dt in (F32, MM_DTYPE, MM_DTYPE, MM_DTYPE, MM_DTYPE, F32)],
        scratch_shapes=[pltpu.VMEM((GDN_DK, wide), F32)],
        compiler_params=_cparams(("arbitrary",)),
        name=name,
    )(*prep, states, do)


def _gdn_outgate_math(o, z, nw):
    r = lax.rsqrt(jnp.mean(o * o, axis=-1, keepdims=True) + RMS_EPS)
    return o * r * nw * _silu(z)


def _gdn_outgate_fwd(o, pm, nw_row, *, name):
    t = o.shape[0]
    tb = min(t, ROW_TILE)
    wide = GDN_HEADS * LANES
    z_at = GDN_QKV // wide

    def body(o_ref, z_ref, nw_ref, y_ref):
        for h in range(GDN_HEADS):
            y = _gdn_outgate_math(_head_tiles(o_ref, h), _head_tiles(z_ref, h), nw_ref[...])
            y_ref[:, h * LANES:(h + 1) * LANES] = y.astype(y_ref.dtype)

    return pl.pallas_call(
        body,
        grid=(t // tb,),
        in_specs=[pl.BlockSpec((tb, wide), lambda i: (i, 0)), pl.BlockSpec((tb, wide), lambda i: (i, z_at)),
                  pl.BlockSpec((1, LANES), lambda i: (0, 0))],
        out_specs=pl.BlockSpec((tb, wide), lambda i: (i, 0)),
        out_shape=jax.ShapeDtypeStruct((t, wide), MM_DTYPE),
        compiler_params=_cparams(("parallel",)),
        name=name,
    )(o, pm, nw_row)


def _gdn_outgate_bwd(o, pm, nw_row, dy, *, name):
    t = o.shape[0]
    tb = min(t, ROW_TILE)
    wide = GDN_HEADS * LANES
    z_at = GDN_QKV // wide

    def body(o_ref, z_ref, nw_ref, dy_ref, do_ref, dz_ref, dnw_ref):
        total = jnp.zeros((1, LANES), F32)
        for h in range(GDN_HEADS):
            _, pull = jax.vjp(_gdn_outgate_math, _head_tiles(o_ref, h), _head_tiles(z_ref, h), nw_ref[...])
            d_o, d_z, d_nw = pull(_head_tiles(dy_ref, h))
            do_ref[:, h * LANES:(h + 1) * LANES] = d_o
            dz_ref[:, h * LANES:(h + 1) * LANES] = d_z.astype(dz_ref.dtype)
            total = total + d_nw

        @pl.when(pl.program_id(0) == 0)
        def _():
            dnw_ref[...] = total

        @pl.when(pl.program_id(0) > 0)
        def _():
            dnw_ref[...] += total

    blk = pl.BlockSpec((tb, wide), lambda i: (i, 0))
    one = pl.BlockSpec((1, LANES), lambda i: (0, 0))
    return pl.pallas_call(
        body,
        grid=(t // tb,),
        in_specs=[blk, pl.BlockSpec((tb, wide), lambda i: (i, z_at)), one, blk],
        out_specs=[blk, blk, one],
        out_shape=[jax.ShapeDtypeStruct((t, wide), F32), jax.ShapeDtypeStruct((t, wide), MM_DTYPE),
                   jax.ShapeDtypeStruct((1, LANES), F32)],
        compiler_params=_cparams(("arbitrary",)),
        name=name,
    )(o, pm, nw_row, dy)


def _rms64(x, w_row):
    return x * lax.rsqrt(jnp.sum(x * x, axis=-1, keepdims=True) * (1.0 / DIL_DH) + RMS_EPS) * w_row


def _alibi_slopes(group):
    head = lax.broadcasted_iota(jnp.int32, (DIL_HEADS, 8, LANES), 0).astype(F32)
    rate = -math.log(2.0) * ALIBI_MAX_BIAS / (len(DIL_GROUPS) * DIL_HEADS)
    slope = jnp.exp(rate * (head + float(group * DIL_HEADS + 1)))
    return jnp.broadcast_to(slope[:, 0:1, :], (DIL_HEADS, DIL_SPAN, LANES))


def _band_logits(qn, kp, kc, slope_d, has_prev):
    qi = lax.broadcasted_iota(jnp.int32, (DIL_SPAN, DIL_SPAN), 0)
    kj = lax.broadcasted_iota(jnp.int32, (DIL_SPAN, DIL_SPAN), 1)
    steps_c = (qi - kj).astype(F32)
    scale = DIL_DH ** -0.5
    sp = _bdot_nt(qn, kp) * scale - slope_d * (steps_c + float(DIL_SPAN))
    sc = _bdot_nt(qn, kc) * scale - slope_d * steps_c
    sp = jnp.where((kj >= qi) & has_prev, sp, NEG)
    sc = jnp.where(kj <= qi, sc, NEG)
    return sp, sc


def _dil_attn_fwd(slab, wq_row, wk_row, *, group, name):
    dilation = DIL_GROUPS[group][1]
    t = slab.shape[0]
    rows = t // dilation
    nlb = rows // DIL_SPAN
    wide = DIL_HEADS * LANES
    view = slab.reshape(rows, dilation * DIL_SLAB)

    def body(q_ref, kc_ref, vc_ref, kp_ref, vp_ref, wq_ref, wk_ref, o_ref):
        has_prev = pl.program_id(1) > 0
        lane = lax.broadcasted_iota(jnp.int32, (DIL_SPAN, LANES), 1)
        qn = _rms64(_stack_heads(q_ref), wq_ref[...])
        kc = _rms64(_stack_heads(kc_ref), wk_ref[...])
        kp = _rms64(_stack_heads(kp_ref), wk_ref[...])
        sp, sc = _band_logits(qn, kp, kc, _alibi_slopes(group) * float(dilation), has_prev)
        m = jnp.maximum(jnp.max(sp, axis=-1, keepdims=True), jnp.max(sc, axis=-1, keepdims=True))
        pp = jnp.exp(sp - m)
        pc = jnp.exp(sc - m)
        l = jnp.sum(pp, axis=-1, keepdims=True) + jnp.sum(pc, axis=-1, keepdims=True)
        o = (_bdot(pp, _stack_heads(vp_ref)) + _bdot(pc, _stack_heads(vc_ref))) / l
        _store_heads(o_ref, jnp.where(lane < DIL_DH, o, m + jnp.log(l)))

    cur = lambda part: pl.BlockSpec((DIL_SPAN, wide), lambda r, i: (i, 3 * r + part))
    prv = lambda part: pl.BlockSpec((DIL_SPAN, wide), lambda r, i: (jnp.maximum(i - 1, 0), 3 * r + part))
    one = pl.BlockSpec((1, LANES), lambda r, i: (0, 0))
    out = pl.pallas_call(
        body,
        grid=(dilation, nlb),
        in_specs=[cur(0), cur(1), cur(2), prv(1), prv(2), one, one],
        out_specs=pl.BlockSpec((DIL_SPAN, wide), lambda r, i: (i, r)),
        out_shape=jax.ShapeDtypeStruct((rows, dilation * wide), F32),
        compiler_params=_cparams(("parallel", "parallel")),
        name=name,
    )(view, view, view, view, view, wq_row, wk_row)
    return out.reshape(t, wide)


def _head_slope(group, head):
    idx = jnp.zeros((8, LANES), F32) + head.astype(F32)
    rate = -math.log(2.0) * ALIBI_MAX_BIAS / (len(DIL_GROUPS) * DIL_HEADS)
    slope = jnp.exp(rate * (idx + float(group * DIL_HEADS + 1)))
    return jnp.broadcast_to(slope[0:1, :], (DIL_SPAN, LANES))


RESIDUE_BATCH = 8


def _take_residues(ref, d, first=0, count=None):
    count = d if count is None else count
    return jnp.stack([ref[pl.ds(first + r, DIL_SPAN, stride=d), :] for r in range(count)])


def _put_residues(ref, val, d, first=0):
    for r in range(val.shape[0]):
        ref[pl.ds(first + r, DIL_SPAN, stride=d), :] = val[r]


def _dil_attn_fwd_strided(slab, wq_row, wk_row, *, group, name):
    d = DIL_GROUPS[group][1]
    t = slab.shape[0]
    span = DIL_SPAN * d
    nsb = t // span

    hs = max(1, RESIDUE_BATCH // d)

    def body(*refs):
        q, kc, vc, kp, vp = (refs[i * hs:(i + 1) * hs] for i in range(5))
        wq_ref, wk_ref, o_ref, spread = refs[5 * hs:]
        has_prev = pl.program_id(0) > 0
        lane = lax.broadcasted_iota(jnp.int32, (DIL_SPAN, LANES), 1)
        nb = min(d, RESIDUE_BATCH)
        for r0 in range(0, d, nb):
            take = lambda group_refs: jnp.concatenate([_take_residues(ref, d, r0, nb) for ref in group_refs])
            slope = jnp.concatenate([jnp.broadcast_to(_head_slope(group, pl.program_id(1) * hs + j) * float(d),
                                                      (nb, DIL_SPAN, LANES)) for j in range(hs)])
            qn = _rms64(take(q), wq_ref[...])
            kcn = _rms64(take(kc), wk_ref[...])
            kpn = _rms64(take(kp), wk_ref[...])
            sp, sc = _band_logits(qn, kpn, kcn, slope, has_prev)
            m = jnp.maximum(jnp.max(sp, axis=-1, keepdims=True), jnp.max(sc, axis=-1, keepdims=True))
            pp = jnp.exp(sp - m)
            pc = jnp.exp(sc - m)
            l = jnp.sum(pp, axis=-1, keepdims=True) + jnp.sum(pc, axis=-1, keepdims=True)
            o = (_bdot(pp, take(vp)) + _bdot(pc, take(vc))) / l
            res = jnp.where(lane < DIL_DH, o, m + jnp.log(l))
            for j in range(hs):
                _put_residues(spread, res[j * nb:(j + 1) * nb], d, r0)
                if r0 + nb == d:
                    o_ref[:, j * LANES:(j + 1) * LANES] = spread[...]

    cur = lambda part, j: pl.BlockSpec((span, LANES), lambda i, h: (i, part * DIL_HEADS + h * hs + j))
    prv = lambda part, j: pl.BlockSpec((span, LANES), lambda i, h: (jnp.maximum(i - 1, 0), part * DIL_HEADS + h * hs + j))
    one = pl.BlockSpec((1, LANES), lambda i, h: (0, 0))
    heads = range(hs)
    in_specs = ([cur(0, j) for j in heads] + [cur(1, j) for j in heads] + [cur(2, j) for j in heads]
                + [prv(1, j) for j in heads] + [prv(2, j) for j in heads] + [one, one])
    return pl.pallas_call(
        body,
        grid=(nsb, DIL_HEADS // hs),
        in_specs=in_specs,
        out_specs=pl.BlockSpec((span, hs * LANES), lambda i, h: (i, h)),
        out_shape=jax.ShapeDtypeStruct((t, DIL_HEADS * LANES), F32),
        scratch_shapes=[pltpu.VMEM((span, LANES), F32)],
        compiler_params=_cparams(("parallel", "parallel")),
        name=name,
    )(*([slab] * (5 * hs)), wq_row, wk_row)


def _dil_attn_bwd_strided(slab, stat, wq_row, wk_row, dwq_in, dwk_in, *, group, name):
    d = DIL_GROUPS[group][1]
    t = slab.shape[0]
    span = DIL_SPAN * d
    nsb = t // span

    hs = max(1, RESIDUE_BATCH // d)

    def body(*refs):
        q_refs, kc_refs, vc_refs, kp_refs, vp_refs, st_refs = (refs[i * hs:(i + 1) * hs] for i in range(6))
        wq_ref, wk_ref, dwq_in_ref, dwk_in_ref, d_ref, dwq_ref, dwk_ref, dk_carry, dv_carry, spread = refs[6 * hs:]
        take = lambda group_refs: jnp.concatenate([_take_residues(ref, d) for ref in group_refs])
        step = pl.program_id(1)
        has_prev = step < nsb - 1
        first = (pl.program_id(0) == 0) & (step == 0)

        @pl.when(step == 0)
        def _():
            dk_carry[...] = jnp.zeros_like(dk_carry)
            dv_carry[...] = jnp.zeros_like(dv_carry)

        @pl.when(first)
        def _():
            dwq_ref[...] = dwq_in_ref[...]
            dwk_ref[...] = dwk_in_ref[...]

        lane = lax.broadcasted_iota(jnp.int32, (DIL_SPAN, LANES), 1)
        scale = DIL_DH ** -0.5
        q_raw = take(q_refs)
        kc_raw = take(kc_refs)
        vc = take(vc_refs)
        kp_raw = take(kp_refs)
        vp = take(vp_refs)
        st = take(st_refs)
        slope = jnp.concatenate([jnp.broadcast_to(_head_slope(group, pl.program_id(0) * hs + j) * float(d),
                                                  (d, DIL_SPAN, LANES)) for j in range(hs)])
        d_o = jnp.where(lane < DIL_DH, st, 0.0)
        lse = jnp.sum(jnp.where(lane == DIL_DH, st, 0.0), axis=-1, keepdims=True)
        delta = jnp.sum(jnp.where(lane == DIL_DH + 1, st, 0.0), axis=-1, keepdims=True)
        qn = _rms64(q_raw, wq_ref[...])
        kc = _rms64(kc_raw, wk_ref[...])
        kp = _rms64(kp_raw, wk_ref[...])
        sp, sc = _band_logits(qn, kp, kc, slope, has_prev)
        pp = jnp.exp(sp - lse)
        pc = jnp.exp(sc - lse)
        dsp = pp * (_bdot_nt(d_o, vp) - delta) * scale
        dsc = pc * (_bdot_nt(d_o, vc) - delta) * scale
        dqn = _bdot(dsp, kp) + _bdot(dsc, kc)
        dkc_n = _bdot_tn(dsc, qn) + dk_carry[...]
        dvc = _bdot_tn(pc, d_o) + dv_carry[...]
        dk_carry[...] = _bdot_tn(dsp, qn)
        dv_carry[...] = _bdot_tn(pp, d_o)
        dq_raw, dwq_rows = _rms64_bwd(q_raw, wq_ref[...], dqn)
        dk_raw, dwk_rows = _rms64_bwd(kc_raw, wk_ref[...], dkc_n)
        for part, val in enumerate((dq_raw, dk_raw, dvc)):
            for j in range(hs):
                _put_residues(spread, val[j * d:(j + 1) * d], d)
                d_ref[part, :, j * LANES:(j + 1) * LANES] = spread[...].astype(d_ref.dtype)
        dwq_ref[...] += jnp.sum(jnp.sum(dwq_rows, axis=0), axis=0, keepdims=True)
        dwk_ref[...] += jnp.sum(jnp.sum(dwk_rows, axis=0), axis=0, keepdims=True)

    at = lambda i: nsb - 1 - i
    cur = lambda part, j: pl.BlockSpec((span, LANES), lambda h, i: (at(i), part * DIL_HEADS + h * hs + j))
    prv = lambda part, j: pl.BlockSpec((span, LANES), lambda h, i: (jnp.maximum(at(i) - 1, 0), part * DIL_HEADS + h * hs + j))
    one = pl.BlockSpec((1, LANES), lambda h, i: (0, 0))
    heads = range(hs)
    in_specs = ([cur(0, j) for j in heads] + [cur(1, j) for j in heads] + [cur(2, j) for j in heads]
                + [prv(1, j) for j in heads] + [prv(2, j) for j in heads] + [cur(0, j) for j in heads] + [one] * 4)
    return pl.pallas_call(
        body,
        grid=(DIL_HEADS // hs, nsb),
        in_specs=in_specs,
        out_specs=[pl.BlockSpec((3, span, hs * LANES), lambda h, i: (0, at(i), h)), one, one],
        out_shape=[jax.ShapeDtypeStruct((3, t, DIL_HEADS * LANES), MM_DTYPE), jax.ShapeDtypeStruct((1, LANES), F32),
                   jax.ShapeDtypeStruct((1, LANES), F32)],
        scratch_shapes=[pltpu.VMEM((hs * d, DIL_SPAN, LANES), F32), pltpu.VMEM((hs * d, DIL_SPAN, LANES), F32),
                        pltpu.VMEM((span, LANES), F32)],
        compiler_params=_cparams(("arbitrary", "arbitrary")),
        name=name,
    )(*([slab] * (5 * hs)), *([stat] * hs), wq_row, wk_row, dwq_in, dwk_in)


def _dil_merge_fwd(oe, *, name):
    t = oe[0].shape[0]
    tb = min(t, ROW_TILE)
    wide = DIL_HEADS * LANES

    def body(e0, e1, e2, y_ref, om_ref):
        lane = lax.broadcasted_iota(jnp.int32, (tb, LANES), 1)
        for h in range(DIL_HEADS):
            es = [_head_tiles(e, h) for e in (e0, e1, e2)]
            lse = [jnp.sum(jnp.where(lane == DIL_DH, e, 0.0), axis=-1, keepdims=True) for e in es]
            top = jnp.maximum(jnp.maximum(lse[0], lse[1]), lse[2])
            joint = top + jnp.log(jnp.exp(lse[0] - top) + jnp.exp(lse[1] - top) + jnp.exp(lse[2] - top))
            o = sum(jnp.exp(l - joint) * e for l, e in zip(lse, es))
            y_ref[:, h * LANES:(h + 1) * LANES] = jnp.where(lane < DIL_DH, o, 0.0).astype(y_ref.dtype)
            om_ref[:, h * LANES:(h + 1) * LANES] = jnp.where(lane < DIL_DH, o, joint)

    blk = pl.BlockSpec((tb, wide), lambda i: (i, 0))
    return pl.pallas_call(
        body,
        grid=(t // tb,),
        in_specs=[blk] * 3,
        out_specs=[blk, blk],
        out_shape=[jax.ShapeDtypeStruct((t, wide), MM_DTYPE), jax.ShapeDtypeStruct((t, wide), F32)],
        compiler_params=_cparams(("parallel",)),
        name=name,
    )(*oe)


def _dil_merge_bwd(dy, om, *, name):
    t = dy.shape[0]
    tb = min(t, ROW_TILE)
    wide = DIL_HEADS * LANES

    def body(dy_ref, om_ref, st_ref):
        lane = lax.broadcasted_iota(jnp.int32, (tb, LANES), 1)
        for h in range(DIL_HEADS):
            d_o = jnp.where(lane < DIL_DH, _head_tiles(dy_ref, h), 0.0)
            om_t = _head_tiles(om_ref, h)
            delta = jnp.sum(d_o * om_t, axis=-1, keepdims=True)
            st_ref[:, h * LANES:(h + 1) * LANES] = jnp.where(
                lane < DIL_DH, d_o, jnp.where(lane == DIL_DH, om_t, jnp.where(lane == DIL_DH + 1, delta, 0.0)))

    blk = pl.BlockSpec((tb, wide), lambda i: (i, 0))
    return pl.pallas_call(
        body,
        grid=(t // tb,),
        in_specs=[blk, blk],
        out_specs=blk,
        out_shape=jax.ShapeDtypeStruct((t, wide), F32),
        compiler_params=_cparams(("parallel",)),
        name=name,
    )(dy, om)


def _rms64_bwd(x, w_row, dy):
    r = lax.rsqrt(jnp.sum(x * x, axis=-1, keepdims=True) * (1.0 / DIL_DH) + RMS_EPS)
    gw = dy * w_row
    dx = r * gw - x * (r * r * r * jnp.sum(gw * x, axis=-1, keepdims=True) * (1.0 / DIL_DH))
    return dx, dy * x * r


def _dil_attn_bwd(slab, stat, wq_row, wk_row, dwq_in, dwk_in, *, group, name):
    dilation = DIL_GROUPS[group][1]
    t = slab.shape[0]
    rows = t // dilation
    nlb = rows // DIL_SPAN
    wide = DIL_HEADS * LANES
    view = slab.reshape(rows, dilation * DIL_SLAB)
    stat_view = stat.reshape(rows, dilation * wide)

    def body(cur_ref, kp_ref, vp_ref, st_ref, wq_ref, wk_ref, dwq_in_ref, dwk_in_ref, d_ref, dwq_ref, dwk_ref,
             dk_carry, dv_carry):
        step = pl.program_id(1)
        has_prev = step < nlb - 1
        first = (pl.program_id(0) == 0) & (step == 0)

        @pl.when(step == 0)
        def _():
            dk_carry[...] = jnp.zeros_like(dk_carry)
            dv_carry[...] = jnp.zeros_like(dv_carry)

        @pl.when(first)
        def _():
            dwq_ref[...] = dwq_in_ref[...]
            dwk_ref[...] = dwk_in_ref[...]

        lane = lax.broadcasted_iota(jnp.int32, (DIL_SPAN, LANES), 1)
        scale = DIL_DH ** -0.5
        q_raw = _stack_heads(cur_ref)
        kc_raw = _stack_heads(cur_ref, first=DIL_HEADS)
        vc = _stack_heads(cur_ref, first=2 * DIL_HEADS)
        kp_raw = _stack_heads(kp_ref)
        vp = _stack_heads(vp_ref)
        st = _stack_heads(st_ref)
        d_o = jnp.where(lane < DIL_DH, st, 0.0)
        lse = jnp.sum(jnp.where(lane == DIL_DH, st, 0.0), axis=-1, keepdims=True)
        delta = jnp.sum(jnp.where(lane == DIL_DH + 1, st, 0.0), axis=-1, keepdims=True)
        qn = _rms64(q_raw, wq_ref[...])
        kc = _rms64(kc_raw, wk_ref[...])
        kp = _rms64(kp_raw, wk_ref[...])
        sp, sc = _band_logits(qn, kp, kc, _alibi_slopes(group) * float(dilation), has_prev)
        pp = jnp.exp(sp - lse)
        pc = jnp.exp(sc - lse)
        dsp = pp * (_bdot_nt(d_o, vp) - delta) * scale
        dsc = pc * (_bdot_nt(d_o, vc) - delta) * scale
        dqn = _bdot(dsp, kp) + _bdot(dsc, kc)
        dkc_n = _bdot_tn(dsc, qn) + _stack_heads(dk_carry)
        dvc = _bdot_tn(pc, d_o) + _stack_heads(dv_carry)
        _store_heads(dk_carry, _bdot_tn(dsp, qn))
        _store_heads(dv_carry, _bdot_tn(pp, d_o))
        dq_raw, dwq_rows = _rms64_bwd(q_raw, wq_ref[...], dqn)
        dk_raw, dwk_rows = _rms64_bwd(kc_raw, wk_ref[...], dkc_n)
        _store_heads(d_ref, dq_raw)
        _store_heads(d_ref, dk_raw, first=DIL_HEADS)
        _store_heads(d_ref, dvc, first=2 * DIL_HEADS)
        dwq_ref[...] += jnp.sum(jnp.sum(dwq_rows, axis=0), axis=0, keepdims=True)
        dwk_ref[...] += jnp.sum(jnp.sum(dwk_rows, axis=0), axis=0, keepdims=True)

    blk_i = lambda i: nlb - 1 - i
    cur = pl.BlockSpec((DIL_SPAN, DIL_SLAB), lambda r, i: (blk_i(i), r))
    prv = lambda part: pl.BlockSpec((DIL_SPAN, wide), lambda r, i: (jnp.maximum(blk_i(i) - 1, 0), 3 * r + part))
    one = pl.BlockSpec((1, LANES), lambda r, i: (0, 0))
    dslab, dwq, dwk = pl.pallas_call(
        body,
        grid=(dilation, nlb),
        in_specs=[cur, prv(1), prv(2), pl.BlockSpec((DIL_SPAN, wide), lambda r, i: (blk_i(i), r)), one, one, one, one],
        out_specs=[cur, one, one],
        out_shape=[jax.ShapeDtypeStruct((rows, dilation * DIL_SLAB), MM_DTYPE), jax.ShapeDtypeStruct((1, LANES), F32),
                   jax.ShapeDtypeStruct((1, LANES), F32)],
        scratch_shapes=[pltpu.VMEM((DIL_SPAN, wide), F32), pltpu.VMEM((DIL_SPAN, wide), F32)],
        compiler_params=_cparams(("arbitrary", "arbitrary")),
        name=name,
    )(view, view, view, stat_view, wq_row, wk_row, dwq_in, dwk_in)
    return dslab.reshape(t, DIL_SLAB), dwq, dwk


def _row(v, width=LANES):
    v = v.astype(F32).reshape(-1)
    return jnp.pad(v, (0, width - v.shape[0])).reshape(1, width)


def _prepare_weights(w):
    return dict(gdn=_prepare_gdn(w), dil=_prepare_dil(w), ffn=_prepare_ffn(w))


def _prepare_gdn(w, layers=range(DEPTH // 2)):
    gdn = {}
    for j in layers:
        wt = w["gdn_w_in"][j]
        gates_t = jnp.pad(wt[GDN_MAIN:], ((0, LANES - 2 * GDN_HEADS), (0, 0)))
        gdn[j] = dict(in_t=wt, gates_t=gates_t, out=w["gdn_w_out"][j], conv=w["gdn_conv_w"][j].astype(F32),
                      alog=_row(w["gdn_a_log"][j]), dt=_row(w["gdn_dt_bias"][j]), nw=_row(w["gdn_norm_w"][j]))
    return gdn


def _prepare_dil(w, layers=range(DEPTH // 2)):
    d = D_MODEL
    dil = {}
    for j in layers:
        wt = w["dil_w_in"][j].reshape(3, len(DIL_GROUPS), DIL_HEADS, DIL_DH, d)
        wg_t = [wt[:, g].reshape(DIL_SLAB // 2, d) for g in range(len(DIL_GROUPS))]
        out_t = jnp.pad(w["dil_w_out"][j].reshape(d, DIL_HEADS, DIL_DH), ((0, 0), (0, 0), (0, LANES - DIL_DH)))
        dil[j] = dict(wg_t=wg_t, out_t=out_t.reshape(d, DIL_HEADS * LANES), wq=_row(w["dil_q_norm"][j]),
                      wk=_row(w["dil_k_norm"][j]))
    return dil


def _prepare_ffn(w, layers=range(DEPTH)):
    return {i: dict(in_t=w["ffn_w_in"][i], out=w["ffn_w_out"][i]) for i in layers}


def _residual_out(a, w, x, next_row, *, name, **kw):
    if next_row is None:
        return _matmul(a, w, add=x, name=name, **kw), None
    return _matmul(a, w, add=x, norm_fwd=next_row, name=name + "_norm", **kw)


def _gdn_layer_fwd(x, nrow, p, hn=None, next_row=None):
    if hn is None:
        hn = _rmsnorm_fwd(x, nrow, name="rmsnorm_fwd")
    pm = _matmul(hn, p["in_t"], trans_b=True, b_rows=(0, GDN_MAIN), name="gdn_proj_main")
    ab = _matmul(hn, p["gates_t"], trans_b=True, name="gdn_proj_gates")
    qkv = _gdn_conv_fwd(pm, p["conv"], name="gdn_conv_fwd")
    gb, bb = _gdn_gates_fwd(ab, p["alog"], p["dt"], name="gdn_gates_fwd")
    prep, rest = _gdn_prep_fwd(qkv, gb, bb, name="gdn_prep_fwd")
    o, states = _gdn_scan_fwd(prep, name="gdn_scan_fwd")
    og = _gdn_outgate_fwd(o, pm, p["nw"], name="gdn_outgate_fwd")
    y, hn_next = _residual_out(og, p["out"], x, next_row, name="gdn_proj_out")
    return y, (x, hn, pm, ab, qkv, gb, bb, prep, rest, states, o, og), hn_next


def _gdn_layer_bwd(dx, dxb, nrow, p, saved):
    x, hn, pm, ab, qkv, gb, bb, prep, rest, states, o, og = saved
    d_og = _matmul(dxb, p["out"], trans_b=True, name="gdn_dgate")
    g_out = _matmul(og, dxb, trans_a=True, out_dtype=MM_DTYPE, name="gdn_gw_out")
    d_o, d_z, d_nw = _gdn_outgate_bwd(o, pm, p["nw"], d_og, name="gdn_outgate_bwd")
    cts = _gdn_scan_bwd(prep, states, d_o, name="gdn_scan_bwd")
    dqkv, dgb, dbb = _gdn_prep_bwd(qkv, gb, bb, rest, cts, name="gdn_prep_bwd")
    d_ab, d_alog, d_dt = _gdn_gates_bwd(ab, p["alog"], p["dt"], dgb, dbb, name="gdn_gates_bwd")
    d_conv, g_conv = _gdn_conv_bwd(pm, p["conv"], dqkv, name="gdn_conv_bwd")
    d_hn = _matmul(d_conv, p["in_t"], b_rows=(0, GDN_QKV), name="gdn_dhn_qkv")
    d_hn = _matmul(d_z, p["in_t"], b_rows=(GDN_QKV, GDN_MAIN - GDN_QKV), add=d_hn, name="gdn_dhn_z")
    dx_new, dxb_new, g_norm = _matmul(d_ab, p["gates_t"], add=d_hn, norm_bwd=(x, nrow, dx), name="gdn_dhn_gates_norm")
    g_in_t = jnp.concatenate([
        _matmul(d_conv, hn, trans_a=True, out_dtype=MM_DTYPE, name="gdn_gw_qkv"),
        _matmul(d_z, hn, trans_a=True, out_dtype=MM_DTYPE, name="gdn_gw_z"),
        _matmul(d_ab, hn, trans_a=True, out_dtype=MM_DTYPE, name="gdn_gw_gates")[:2 * GDN_HEADS],
    ], axis=0)
    grads = dict(w_in=g_in_t, conv=g_conv, a_log=d_alog[0, :GDN_HEADS], dt_bias=d_dt[0, :GDN_HEADS], norm_w=d_nw[0],
                 w_out=g_out, norm=g_norm[0])
    return dx_new, dxb_new, grads


def _dil_layer_fwd(x, nrow, p, hn=None, next_row=None):
    if hn is None:
        hn = _rmsnorm_fwd(x, nrow, name="rmsnorm_fwd")
    slabs = [_matmul(hn, p["wg_t"][g], trans_b=True, spread_out=True, name="dil_proj_in") for g in range(len(DIL_GROUPS))]
    oe = [(_dil_attn_fwd if DIL_GROUPS[g][1] == 1 else _dil_attn_fwd_strided)(
        slabs[g], p["wq"], p["wk"], group=g, name=f"dil_attn_fwd_g{g}") for g in range(len(DIL_GROUPS))]
    y, om = _dil_merge_fwd(oe, name="dil_merge_fwd")
    out, hn_next = _residual_out(y, p["out_t"], x, next_row, trans_b=True, name="dil_proj_out")
    return out, (x, hn, slabs, y, om), hn_next


def _dil_layer_bwd(dx, dxb, nrow, p, saved):
    x, hn, slabs, y, om = saved
    d_y = _matmul(dxb, p["out_t"], name="dil_dmerged")
    g_out_t = _matmul(dxb, y, trans_a=True, out_dtype=MM_DTYPE, name="dil_gw_out")
    g_out_t = g_out_t.reshape(D_MODEL, DIL_HEADS, LANES)[..., :DIL_DH].reshape(D_MODEL, DIL_HEADS * DIL_DH)
    stat = _dil_merge_bwd(d_y, om, name="dil_merge_bwd")
    d_hn = None
    dwq = jnp.zeros((1, LANES), F32)
    dwk = jnp.zeros((1, LANES), F32)
    g_groups = []
    wide = DIL_HEADS * LANES
    for g in range(len(DIL_GROUPS)):
        last = dict(norm_bwd=(x, nrow, dx)) if g == len(DIL_GROUPS) - 1 else {}
        if DIL_GROUPS[g][1] == 1:
            dslab, dwq, dwk = _dil_attn_bwd(slabs[g], stat, p["wq"], p["wk"], dwq, dwk, group=g, name=f"dil_attn_bwd_g{g}")
            d_hn = _matmul(dslab, p["wg_t"][g], packed_a=True, add=d_hn, name="dil_dhn", **last)
            g_w = _matmul(dslab, hn, trans_a=True, packed_a=True, out_dtype=MM_DTYPE, name="dil_gw_in")
        else:
            dparts, dwq, dwk = _dil_attn_bwd_strided(slabs[g], stat, p["wq"], p["wk"], dwq, dwk, group=g,
                                                     name=f"dil_attn_bwd_g{g}")
            d_hn = _matmul(dparts, p["wg_t"][g], a_lead="k", packed_a=True, add=d_hn,
                           name="dil_dhn_parts_norm" if last else "dil_dhn_parts", **last)
            g_w = _matmul(dparts, hn, trans_a=True, a_lead="i", packed_a=True, out_dtype=MM_DTYPE, name="dil_gw_in_parts")
        g_groups.append(g_w.reshape(3, DIL_HEADS, DIL_DH, D_MODEL))
    g_in_t = jnp.stack(g_groups, axis=1).reshape(3 * len(DIL_GROUPS) * DIL_HEADS * DIL_DH, D_MODEL)
    dx_new, dxb_new, g_norm = d_hn
    grads = dict(w_in=g_in_t, q_norm=dwq[0, :DIL_DH], k_norm=dwk[0, :DIL_DH], w_out=g_out_t, norm=g_norm[0])
    return dx_new, dxb_new, grads


def _ffn_layer_fwd(x, nrow, p, hn=None, next_row=None):
    if hn is None:
        hn = _rmsnorm_fwd(x, nrow, name="rmsnorm_fwd")
    gate, up, act = _ffn_in(hn, p["in_t"], name="ffn_proj_in")
    y, hn_next = _residual_out(act, p["out"], x, next_row, name="ffn_proj_out")
    return y, (x, hn, gate, up, act), hn_next


def _ffn_layer_bwd(dx, dxb, nrow, p, saved):
    x, hn, gate, up, act = saved
    g_out = _matmul(act, dxb, trans_a=True, out_dtype=MM_DTYPE, name="ffn_gw_out")
    d_gu = _ffn_dact(dxb, p["out"], gate, up, name="ffn_dact")
    dx_new, dxb_new, g_norm = _matmul(d_gu, p["in_t"], a_lead="k", norm_bwd=(x, nrow, dx), name="ffn_dhn_norm")
    g_in_t = _matmul(d_gu, hn, trans_a=True, a_lead="i", out_dtype=MM_DTYPE, name="ffn_gw_in")
    return dx_new, dxb_new, dict(w_in=g_in_t, w_out=g_out, norm=g_norm[0])


def _mixer_fwd(i, x, mix_row, prepared, hn=None, next_row=None):
    if i % 2 == 0:
        return _gdn_layer_fwd(x, mix_row, prepared["gdn"][i // 2], hn, next_row)
    return _dil_layer_fwd(x, mix_row, prepared["dil"][i // 2], hn, next_row)


def _mixer_bwd(i, dx, dxb, mix_row, prepared, saved, zero=0.0):
    if i % 2 == 0:
        p = prepared["gdn"][i // 2]
        return _gdn_layer_bwd(dx, dxb, mix_row, dict(p, nw=p["nw"] + zero), saved)
    p = prepared["dil"][i // 2]
    return _dil_layer_bwd(dx, dxb, mix_row, dict(p, wq=p["wq"] + zero), saved)


def _local_step(x, target, prepared, norm_mix, norm_ffn):
    saved = []
    hn = None
    for i in range(DEPTH):
        after = norm_mix[i + 1].reshape(1, D_MODEL) if i + 1 < DEPTH else None
        x, s_mix, hn = _mixer_fwd(i, x, norm_mix[i].reshape(1, D_MODEL), prepared, hn, norm_ffn[i].reshape(1, D_MODEL))
        x, s_ffn, hn = _ffn_layer_fwd(x, norm_ffn[i].reshape(1, D_MODEL), prepared["ffn"][i], hn, after)
        saved.append((s_mix, s_ffn))
    dx, dxb, loss = _loss_head(x, target, name="loss_head")
    g_mix, g_ffn = [None] * DEPTH, [None] * DEPTH
    for i in reversed(range(DEPTH)):
        s_mix, s_ffn = saved[i]
        dx, dxb, g_ffn[i] = _ffn_layer_bwd(dx, dxb, norm_ffn[i].reshape(1, D_MODEL), prepared["ffn"][i], s_ffn)
        dx, dxb, g_mix[i] = _mixer_bwd(i, dx, dxb, norm_mix[i].reshape(1, D_MODEL), prepared, s_mix)
    return loss[0, 0], dx, _collect_grads(g_mix, g_ffn)


def _collect_grads(g_mix, g_ffn):
    gdn = [g_mix[i] for i in range(0, DEPTH, 2)]
    dil = [g_mix[i] for i in range(1, DEPTH, 2)]
    if any(g is None for g in g_mix + g_ffn):
        pick = lambda gs, key: [None if g is None else g[key] for g in gs]
        return dict(gdn_w_in=pick(gdn, "w_in"), gdn_w_out=pick(gdn, "w_out"), dil_w_in=pick(dil, "w_in"),
                    dil_w_out=pick(dil, "w_out"), ffn_w_in=pick(g_ffn, "w_in"), ffn_w_out=pick(g_ffn, "w_out"))
    grads = dict(
        norm_mix=jnp.stack([g["norm"] for g in g_mix]),
        norm_ffn=jnp.stack([g["norm"] for g in g_ffn]),
        gdn_w_in=[g["w_in"] for g in gdn],
        gdn_conv_w=jnp.stack([g["conv"] for g in gdn]),
        gdn_a_log=jnp.stack([g["a_log"] for g in gdn]),
        gdn_dt_bias=jnp.stack([g["dt_bias"] for g in gdn]),
        gdn_norm_w=jnp.stack([g["norm_w"] for g in gdn]),
        gdn_w_out=[g["w_out"] for g in gdn],
        dil_w_in=[g["w_in"] for g in dil],
        dil_q_norm=jnp.stack([g["q_norm"] for g in dil]),
        dil_k_norm=jnp.stack([g["k_norm"] for g in dil]),
        dil_w_out=[g["w_out"] for g in dil],
        ffn_w_in=[g["w_in"] for g in g_ffn],
        ffn_w_out=[g["w_out"] for g in g_ffn],
    )
    return grads


MESH_ID = pl.DeviceIdType.MESH
ANY_SPACE = pl.BlockSpec(memory_space=pl.ANY)


def _mesh_position():
    return lax.axis_index("x"), lax.axis_index("y"), lax.axis_index("c")


def _flip(pos, k):
    x, y, c = pos
    return (1 - x if k & 4 else x, 1 - y if k & 2 else y, 1 - c if k & 1 else c)


def _linear(pos):
    return 4 * pos[0] + 2 * pos[1] + pos[2]


def _comm_scratch():
    return [pltpu.SemaphoreType.DMA((N_DEV - 1,)), pltpu.SemaphoreType.DMA((N_DEV - 1,)), pltpu.SemaphoreType.DMA(())]


def _all_gather(shard, *, name):
    def body(x_ref, out_ref, send_sems, recv_sems, local_sem):
        me = _mesh_position()
        mine = out_ref.at[_linear(me)]
        local = pltpu.make_async_copy(x_ref, mine, local_sem)
        local.start()
        copies = []
        for k in range(1, N_DEV):
            cp = pltpu.make_async_remote_copy(src_ref=x_ref, dst_ref=mine, send_sem=send_sems.at[k - 1],
                                              recv_sem=recv_sems.at[k - 1], device_id=_flip(me, k), device_id_type=MESH_ID)
            cp.start()
            copies.append(cp)
        for cp in copies:
            cp.wait()
        local.wait()

    return pl.pallas_call(
        body,
        out_shape=jax.ShapeDtypeStruct((N_DEV,) + shard.shape, shard.dtype),
        in_specs=[ANY_SPACE],
        out_specs=ANY_SPACE,
        scratch_shapes=_comm_scratch(),
        name=name,
    )(shard)


def _exchange(parts, *, name):
    def body(p_ref, out_ref, send_sems, recv_sems, local_sem):
        me = _mesh_position()
        mine = out_ref.at[_linear(me)]
        local = pltpu.make_async_copy(p_ref.at[_linear(me)], mine, local_sem)
        local.start()
        copies = []
        for k in range(1, N_DEV):
            peer = _flip(me, k)
            cp = pltpu.make_async_remote_copy(src_ref=p_ref.at[_linear(peer)], dst_ref=mine, send_sem=send_sems.at[k - 1],
                                              recv_sem=recv_sems.at[k - 1], device_id=peer, device_id_type=MESH_ID)
            cp.start()
            copies.append(cp)
        for cp in copies:
            cp.wait()
        local.wait()

    return pl.pallas_call(
        body,
        out_shape=jax.ShapeDtypeStruct(parts.shape, parts.dtype),
        in_specs=[ANY_SPACE],
        out_specs=ANY_SPACE,
        scratch_shapes=_comm_scratch(),
        name=name,
    )(parts)


HBM_SPACE = pl.BlockSpec(memory_space=pltpu.HBM)
SEM_SPACE = pl.BlockSpec(memory_space=pltpu.SEMAPHORE)
DATAFLOW = pltpu.SideEffectType.DATAFLOW_SIDE_EFFECTING


def _split_copies(src_ref, land_ref, send_sems, recv_sems, per_peer):
    me = _mesh_position()
    mine = land_ref.at[_linear(me)]
    copies = []
    for k in range(1, N_DEV):
        peer = _flip(me, k)
        src = src_ref.at[_linear(peer)] if per_peer else src_ref
        copies.append(pltpu.make_async_remote_copy(src_ref=src, dst_ref=mine, send_sem=send_sems.at[k - 1],
                                                   recv_sem=recv_sems.at[k - 1], device_id=peer, device_id_type=MESH_ID))
    return copies


def _travel_start(src, after, *, per_peer, name):
    me = _linear(_mesh_position())
    own = src[me] if per_peer else src
    shape = own.shape
    landing = lax.dynamic_update_slice(lax.empty((N_DEV,) + shape, src.dtype), own[None], (me, 0, 0))

    def body(src_ref, land_ref, after_ref, send_sems, recv_sems, src_thru, land_thru, token):
        for cp in _split_copies(src_ref, land_ref, send_sems, recv_sems, per_peer):
            cp.start()
        token[...] = jnp.zeros_like(token)

    return pl.pallas_call(
        body,
        name=name,
        out_shape=(pltpu.SemaphoreType.DMA((N_DEV - 1,)), pltpu.SemaphoreType.DMA((N_DEV - 1,)),
                   pltpu.HBM(src.shape, src.dtype), pltpu.HBM(landing.shape, landing.dtype),
                   jax.ShapeDtypeStruct((8, LANES), F32)),
        in_specs=(HBM_SPACE, HBM_SPACE, ANY_SPACE),
        out_specs=(SEM_SPACE, SEM_SPACE, HBM_SPACE, HBM_SPACE, pl.BlockSpec(memory_space=pltpu.VMEM)),
        input_output_aliases={0: 2, 1: 3},
        compiler_params=pltpu.CompilerParams(has_side_effects=DATAFLOW),
    )(pltpu.with_memory_space_constraint(src, pltpu.HBM), pltpu.with_memory_space_constraint(landing, pltpu.HBM), after)


def _travel_wait(started, after, *, per_peer, name):
    send_sems, recv_sems, src_thru, land_thru, _ = started

    def body(src_ref, land_ref, send_sems, recv_sems, after_ref, src_dead, got_ref):
        for cp in _split_copies(src_ref, land_ref, send_sems, recv_sems, per_peer):
            cp.wait_send()
            cp.wait_recv()

    return pl.pallas_call(
        body,
        name=name,
        out_shape=(pltpu.HBM(src_thru.shape, src_thru.dtype), pltpu.HBM(land_thru.shape, land_thru.dtype)),
        in_specs=(HBM_SPACE, HBM_SPACE, SEM_SPACE, SEM_SPACE, ANY_SPACE),
        out_specs=(HBM_SPACE, HBM_SPACE),
        input_output_aliases={0: 0, 1: 1},
        compiler_params=pltpu.CompilerParams(has_side_effects=DATAFLOW),
    )(src_thru, land_thru, send_sems, recv_sems, after)[1]


def _adamw(parts, w, m, v, *, name):
    rows, n = w.shape
    tb = _pick(rows, (PACK_ROW_ALIGN, 16))
    c1 = 1.0 - ADAM_B1 ** ADAM_STEP
    c2 = 1.0 - ADAM_B2 ** ADAM_STEP

    def body(p_ref, w_ref, m_ref, v_ref, g_ref, d_ref, nm_ref, nv_ref):
        g = p_ref[0].astype(F32)
        for s in range(1, N_DEV):
            g = g + p_ref[s].astype(F32)
        m_new = ADAM_B1 * m_ref[...] + (1.0 - ADAM_B1) * g
        v_new = ADAM_B2 * v_ref[...] + (1.0 - ADAM_B2) * (g * g)
        m_hat = m_new / c1
        v_hat = v_new / c2
        g_ref[...] = g
        nm_ref[...] = m_new
        nv_ref[...] = v_new
        d_ref[...] = -ADAM_LR * (m_hat / (jnp.sqrt(v_hat) + ADAM_EPS) + ADAM_WD * w_ref[...])

    blk = pl.BlockSpec((tb, n), lambda i: (i, 0))
    return pl.pallas_call(
        body,
        grid=(rows // tb,),
        in_specs=[pl.BlockSpec((N_DEV, tb, n), lambda i: (0, i, 0)), blk, blk, blk],
        out_specs=[blk] * 4,
        out_shape=[jax.ShapeDtypeStruct((rows, n), F32)] * 4,
        compiler_params=_cparams(("parallel",)),
        name=name,
    )(parts, w, m, v)


PACK_WIDTH = 1024
SHARDED = {
    "gdn_w_in": ((2, D_MODEL, GDN_IN_WIDTH), 2),
    "gdn_conv_w": ((2, GDN_CONV, GDN_QKV), 2),
    "gdn_w_out": ((2, GDN_HEADS * GDN_DV, D_MODEL), 1),
    "dil_w_in": ((2, D_MODEL, 3 * len(DIL_GROUPS) * DIL_HEADS * DIL_DH), 2),
    "dil_w_out": ((2, DIL_HEADS * DIL_DH, D_MODEL), 2),
    "ffn_w_in": ((DEPTH, D_MODEL, 2 * FFN_HIDDEN), 2),
    "ffn_w_out": ((DEPTH, FFN_HIDDEN, D_MODEL), 1),
}
REPLICATED = {"norm_mix": (DEPTH, D_MODEL), "norm_ffn": (DEPTH, D_MODEL), "gdn_a_log": (2, GDN_HEADS),
              "gdn_dt_bias": (2, GDN_HEADS), "gdn_norm_w": (2, GDN_DV), "dil_q_norm": (2, DIL_DH), "dil_k_norm": (2, DIL_DH)}
WEIGHT_ORDER = ("norm_mix", "norm_ffn", "gdn_w_in", "gdn_conv_w", "gdn_a_log", "gdn_dt_bias", "gdn_norm_w", "gdn_w_out",
                "dil_w_in", "dil_q_norm", "dil_k_norm", "dil_w_out", "ffn_w_in", "ffn_w_out")
PACK_ROW_ALIGN = 128
PIECE_ALIGN = 16
SMALL_ROWS = 16


def _shard_shape(name):
    shape, axis = SHARDED[name]
    return tuple(s // N_DEV if i == axis else s for i, s in enumerate(shape))


def _shard_rows(name):
    return math.prod(_shard_shape(name)) // PACK_WIDTH


def _split_shards(full, name):
    shape, axis = SHARDED[name]
    split = full.reshape(shape[:axis] + (N_DEV, shape[axis] // N_DEV) + shape[axis + 1:])
    return jnp.moveaxis(split, axis, 0)


def _join_shards(stacked, name):
    shape, axis = SHARDED[name]
    return jnp.moveaxis(stacked, 0, axis).reshape(shape)


COLUMN_SHARDED = ("gdn_w_in", "dil_w_in", "dil_w_out", "ffn_w_in")


def _to_rows(shard, name):
    if name in COLUMN_SHARDED:
        shard = jnp.swapaxes(shard, 1, 2)
    return shard.reshape(-1, PACK_WIDTH)


def _layer_columns(name):
    _, r, c = _shard_shape(name)
    return r if name in COLUMN_SHARDED else c


def _piece_rows(piece, halves=1):
    name, layer = piece
    rows = _shard_rows(name) * halves
    return rows if layer is None else rows // SHARDED[name][0][0]


def _aligned(rows, to=PIECE_ALIGN):
    return -(-rows // to) * to


def _pack_pieces(arrays, total_align=PIECE_ALIGN):
    padded, total = [], 0
    for a in arrays:
        rows = a.shape[-2]
        extra = _aligned(rows) - rows
        if extra:
            a = jnp.pad(a, [(0, 0)] * (a.ndim - 2) + [(0, extra), (0, 0)])
        padded.append(a)
        total += rows + extra
    tail = _aligned(total, total_align) - total
    if tail:
        padded.append(jnp.zeros(padded[0].shape[:-2] + (tail, PACK_WIDTH), padded[0].dtype))
    return jnp.concatenate(padded, axis=-2)


def _piece_offsets(pieces, halves=None):
    out, at = [], 0
    for p in pieces:
        rows = _piece_rows(p, (halves or {}).get(p[0], 1))
        out.append((p, at, rows))
        at += _aligned(rows)
    return out


def _shard_piece_rows(src, piece):
    name, layer = piece
    part = src[name] if layer is None else src[name][layer:layer + 1]
    return _to_rows(part.astype(F32), name)


def _piece_from_rows(rows, piece):
    name, layer = piece
    layers, r, c = _shard_shape(name)
    n_l = layers if layer is None else 1
    if name in COLUMN_SHARDED:
        return jnp.swapaxes(rows.reshape(n_l, c, r), 1, 2)
    return rows.reshape(n_l, r, c)


SMALL_TAIL = tuple(n for n in REPLICATED if n not in ("norm_mix", "norm_ffn"))


def _pack_small(vals):
    tail, at = jnp.zeros((PACK_WIDTH,), F32), 0
    for n in SMALL_TAIL:
        vec = vals[n].astype(F32).reshape(-1)
        tail = tail + jnp.pad(vec, (at, PACK_WIDTH - at - vec.shape[0]))
        at += vec.shape[0]
    buf = jnp.pad(vals["norm_mix"].astype(F32), ((0, SMALL_ROWS - DEPTH), (0, 0)))
    buf = buf + jnp.pad(vals["norm_ffn"].astype(F32), ((8, SMALL_ROWS - 8 - DEPTH), (0, 0)))
    return buf + jnp.pad(tail.reshape(1, PACK_WIDTH), ((SMALL_ROWS - 1, 0), (0, 0)))


def _unpack_small(buf):
    out = {"norm_mix": buf[0:DEPTH], "norm_ffn": buf[8:8 + DEPTH]}
    at = 0
    for n in SMALL_TAIL:
        size = math.prod(REPLICATED[n])
        out[n] = buf[SMALL_ROWS - 1, at:at + size].reshape(REPLICATED[n])
        at += size
    return out


GATHER_FIRST = (("gdn_w_in", 0), ("gdn_conv_w", None), ("gdn_w_out", 0))
GATHER_NEXT = (("ffn_w_in", 0), ("ffn_w_out", 0), ("dil_w_in", 0), ("dil_w_out", 0))
GATHER_LAST = (("ffn_w_in", 1), ("ffn_w_out", 1), ("gdn_w_in", 1), ("gdn_w_out", 1), ("ffn_w_in", 2), ("ffn_w_out", 2),
               ("dil_w_in", 1), ("dil_w_out", 1), ("ffn_w_in", 3), ("ffn_w_out", 3))
EXCHANGE_GROUPS = (
    (("ffn_w_in", 3), ("ffn_w_out", 3), ("dil_w_in", 1), ("dil_w_out", 1),
     ("ffn_w_in", 2), ("ffn_w_out", 2), ("gdn_w_in", 1), ("gdn_w_out", 1)),
    (("ffn_w_in", 1), ("ffn_w_out", 1), ("dil_w_in", 0), ("dil_w_out", 0)),
    (("ffn_w_in", 0), ("ffn_w_out", 0)),
    (("gdn_w_in", 0), ("gdn_w_out", 0), ("gdn_conv_w", None)),
)
EXCHANGE_AFTER = {("mix", 2): 0, ("mix", 1): 1, ("ffn", 0): 2}


def _gather_operand(w, pieces):
    arrays = []
    for n, layer in pieces:
        if layer is None:
            arrays.append(lax.bitcast_convert_type(w[n], BF16).reshape(-1, PACK_WIDTH))
        else:
            arrays.append(_to_rows(w[n][layer:layer + 1].astype(BF16), n))
    return _pack_pieces(arrays)


def _gathered_weights(gathered, pieces, full):
    for (n, layer), at, rows in _piece_offsets(pieces, halves={"gdn_conv_w": 2}):
        block = gathered[:, at:at + rows]
        if layer is None:
            block = lax.bitcast_convert_type(block.reshape((N_DEV,) + _shard_shape(n) + (2,)), F32)
            full[n] = _join_shards(block, n)
        else:
            full.setdefault(n, {})[layer] = block.reshape(-1, _layer_columns(n))
    return full


def _exchange_operand(grads, pieces):
    arrays = []
    for n, layer in pieces:
        if layer is None:
            arrays.append(_split_shards(grads[n], n).astype(BF16).reshape(N_DEV, -1, PACK_WIDTH))
        else:
            arrays.append(grads[n][layer].astype(BF16).reshape(N_DEV, -1, PACK_WIDTH))
    return _pack_pieces(arrays, total_align=PACK_ROW_ALIGN)


def _update_group(received, pieces, w, m, v, *, name):
    packed = [_pack_pieces([_shard_piece_rows(src, p) for p in pieces], total_align=PACK_ROW_ALIGN) for src in (w, m, v)]
    outs = _adamw(received, *packed, name=name)
    return {p: tuple(_piece_from_rows(o[at:at + rows], p) for o in outs) for p, at, rows in _piece_offsets(pieces)}


def kernel(x, norm_mix, norm_ffn, gdn_w_in, gdn_conv_w, gdn_a_log, gdn_dt_bias, gdn_norm_w, gdn_w_out, dil_w_in, dil_q_norm, dil_k_norm, dil_w_out, ffn_w_in, ffn_w_out, loss_target, m_norm_mix, m_norm_ffn, m_gdn_w_in, m_gdn_conv_w, m_gdn_a_log, m_gdn_dt_bias, m_gdn_norm_w, m_gdn_w_out, m_dil_w_in, m_dil_q_norm, m_dil_k_norm, m_dil_w_out, m_ffn_w_in, m_ffn_w_out, v_norm_mix, v_norm_ffn, v_gdn_w_in, v_gdn_conv_w, v_gdn_a_log, v_gdn_dt_bias, v_gdn_norm_w, v_gdn_w_out, v_dil_w_in, v_dil_q_norm, v_dil_k_norm, v_dil_w_out, v_ffn_w_in, v_ffn_w_out):
    w = dict(norm_mix=norm_mix, norm_ffn=norm_ffn, gdn_w_in=gdn_w_in, gdn_conv_w=gdn_conv_w, gdn_a_log=gdn_a_log,
             gdn_dt_bias=gdn_dt_bias, gdn_norm_w=gdn_norm_w, gdn_w_out=gdn_w_out, dil_w_in=dil_w_in, dil_q_norm=dil_q_norm,
             dil_k_norm=dil_k_norm, dil_w_out=dil_w_out, ffn_w_in=ffn_w_in, ffn_w_out=ffn_w_out)
    m = dict(norm_mix=m_norm_mix, norm_ffn=m_norm_ffn, gdn_w_in=m_gdn_w_in, gdn_conv_w=m_gdn_conv_w, gdn_a_log=m_gdn_a_log,
             gdn_dt_bias=m_gdn_dt_bias, gdn_norm_w=m_gdn_norm_w, gdn_w_out=m_gdn_w_out, dil_w_in=m_dil_w_in,
             dil_q_norm=m_dil_q_norm, dil_k_norm=m_dil_k_norm, dil_w_out=m_dil_w_out, ffn_w_in=m_ffn_w_in, ffn_w_out=m_ffn_w_out)
    v = dict(norm_mix=v_norm_mix, norm_ffn=v_norm_ffn, gdn_w_in=v_gdn_w_in, gdn_conv_w=v_gdn_conv_w, gdn_a_log=v_gdn_a_log,
             gdn_dt_bias=v_gdn_dt_bias, gdn_norm_w=v_gdn_norm_w, gdn_w_out=v_gdn_w_out, dil_w_in=v_dil_w_in,
             dil_q_norm=v_dil_q_norm, dil_k_norm=v_dil_k_norm, dil_w_out=v_dil_w_out, ffn_w_in=v_ffn_w_in, ffn_w_out=v_ffn_w_out)
    def row(src, i):
        return src[i].reshape(1, D_MODEL)

    first = _all_gather(_gather_operand(w, GATHER_FIRST), name="weight_all_gather_first")
    next_started = _travel_start(_gather_operand(w, GATHER_NEXT), first, per_peer=False, name="weight_gather_start_next")
    last_started = _travel_start(_gather_operand(w, GATHER_LAST), next_started[4], per_peer=False,
                                 name="weight_gather_start_last")
    full = _gathered_weights(first, GATHER_FIRST, {n: w[n] for n in REPLICATED})
    prepared = dict(gdn=_prepare_gdn(full, layers=(0,)))
    h = x[0]
    saved = [None] * DEPTH
    h, s_mix, hn = _mixer_fwd(0, h, row(norm_mix, 0) + last_started[4][0, 0], prepared, None, row(norm_ffn, 0))
    got = _travel_wait(next_started, h, per_peer=False, name="weight_gather_wait_next")
    full = _gathered_weights(got, GATHER_NEXT, full)
    prepared.update(dil=_prepare_dil(full, layers=(0,)), ffn=_prepare_ffn(full, layers=(0,)))
    for i in range(DEPTH):
        if i > 0:
            h, s_mix, hn = _mixer_fwd(i, h, row(norm_mix, i), prepared, hn, row(norm_ffn, i))
        if i == 1:
            got = _travel_wait(last_started, h, per_peer=False, name="weight_gather_wait_last")
            full = _gathered_weights(got, GATHER_LAST, full)
            prepared["gdn"].update(_prepare_gdn(full, layers=(1,)))
            prepared["dil"].update(_prepare_dil(full, layers=(1,)))
            prepared["ffn"].update(_prepare_ffn(full, layers=(1, 2, 3)))
        h, s_ffn, hn = _ffn_layer_fwd(h, row(norm_ffn, i), prepared["ffn"][i], hn,
                                      row(norm_mix, i + 1) if i + 1 < DEPTH else None)
        saved[i] = (s_mix, s_ffn)
    dx, dxb, loss = _loss_head(h, loss_target[0], name="loss_head")

    g_mix, g_ffn = [None] * DEPTH, [None] * DEPTH
    started = {}

    def travel(group):
        operand = _exchange_operand(_collect_grads(g_mix, g_ffn), EXCHANGE_GROUPS[group])
        started[group] = _travel_start(operand, dx, per_peer=True, name=f"grad_exchange_start_{group}")
        return started[group][4][0, 0]

    zero = 0.0
    for i in reversed(range(DEPTH)):
        s_mix, s_ffn = saved[i]
        dx, dxb, g_ffn[i] = _ffn_layer_bwd(dx, dxb, row(norm_ffn, i) + zero, prepared["ffn"][i], s_ffn)
        zero = travel(EXCHANGE_AFTER[("ffn", i)]) if ("ffn", i) in EXCHANGE_AFTER else 0.0
        dx, dxb, g_mix[i] = _mixer_bwd(i, dx, dxb, row(norm_mix, i), prepared, s_mix, zero)
        zero = travel(EXCHANGE_AFTER[("mix", i)]) if ("mix", i) in EXCHANGE_AFTER else 0.0
    grads = _collect_grads(g_mix, g_ffn)
    received = [_travel_wait(started[g], dx, per_peer=True, name=f"grad_exchange_wait_{g}") for g in sorted(started)]
    received.append(_exchange(_exchange_operand(grads, EXCHANGE_GROUPS[-1]), name="grad_exchange_last"))
    updated = {}
    for g, pieces in enumerate(EXCHANGE_GROUPS):
        updated.update(_update_group(received[g], pieces, w, m, v, name=f"adamw_sharded_{g}"))

    small_parts = _all_gather(_pack_small(grads), name="small_grad_all_gather")
    outs_small = [_unpack_small(o) for o in
                  _adamw(small_parts, _pack_small(w), _pack_small(m), _pack_small(v), name="adamw_replicated")]

    total_loss = lax.psum(loss[0, 0], ("x", "y", "c"))
    result = [total_loss, dx[None]]
    for k in range(4):
        for n in WEIGHT_ORDER:
            if n not in SHARDED:
                result.append(outs_small[k][n])
            elif (n, None) in updated:
                result.append(updated[(n, None)][k])
            else:
                result.append(jnp.concatenate([updated[(n, l)][k] for l in range(SHARDED[n][0][0])], axis=0))
    return tuple(result)
```

```python
import functools
import math

import jax
import jax.numpy as jnp
from jax import lax
from jax.experimental import pallas as pl
from jax.experimental.pallas import tpu as pltpu

F32 = jnp.float32
BF16 = jnp.bfloat16
MM_DTYPE = BF16

N_DEV = 8
D_MODEL = 1024
DEPTH = 4
RMS_EPS = 1e-6
L2_EPS = 1e-6

LANES = 128

GDN_HEADS = 8
GDN_DK = 128
GDN_DV = 128
GDN_CONV = 4
GDN_CHUNK = 128
GDN_QKV = 3 * GDN_HEADS * GDN_DK
GDN_MAIN = GDN_QKV + GDN_HEADS * GDN_DV
GDN_IN_WIDTH = GDN_MAIN + 2 * GDN_HEADS

DIL_GROUPS = ((128, 1), (512, 4), (2048, 16))
DIL_HEADS = 8
DIL_DH = 64
DIL_SPAN = 128
DIL_SLAB = 3 * DIL_HEADS * LANES
ALIBI_MAX_BIAS = 8.0

FFN_HIDDEN = 2816

ADAM_LR = 0.001
ADAM_B1 = 0.9
ADAM_B2 = 0.999
ADAM_EPS = 1e-08
ADAM_WD = 0.01
ADAM_STEP = 10

VMEM_LIMIT = 56 * 1024 * 1024
ROW_TILE = 512
MATMUL_VMEM_BUDGET = 40 * 1024 * 1024
NEG = -1e30


def _cparams(sem):
    return pltpu.CompilerParams(dimension_semantics=sem, vmem_limit_bytes=VMEM_LIMIT)


def _single_pass(a, b, a_dim, b_dim):
    lead = a.ndim - 2
    batch = ((0,), (0,)) if lead else ((), ())
    return lax.dot_general(a.astype(BF16), b.astype(BF16), (((lead + a_dim,), (lead + b_dim,)), batch),
                           preferred_element_type=F32)


def _bdot(a, b):
    return _single_pass(a, b, 1, 0)


def _bdot_nt(a, b):
    return _single_pass(a, b, 1, 1)


def _bdot_tn(a, b):
    return _single_pass(a, b, 0, 0)


def _pick(n, candidates):
    for c in candidates:
        if n % c == 0:
            return c
    raise ValueError(f"no tile for {n}")


HALF = LANES // 2


def _pack_head_pairs(x):
    x = x.astype(F32)
    tiles = [x[:, (2 * i) * LANES:(2 * i + 1) * LANES] + pltpu.roll(x[:, (2 * i + 1) * LANES:(2 * i + 2) * LANES], HALF, 1)
             for i in range(x.shape[1] // (2 * LANES))]
    return tiles[0] if len(tiles) == 1 else jnp.concatenate(tiles, axis=1)


def _spread_head_pairs(y):
    low = lax.broadcasted_iota(jnp.int32, (y.shape[0], LANES), 1) < HALF
    tiles = []
    for i in range(y.shape[1] // LANES):
        pair = y[:, i * LANES:(i + 1) * LANES]
        tiles += [jnp.where(low, pair, 0.0), jnp.where(low, pltpu.roll(pair, HALF, 1), 0.0)]
    return jnp.concatenate(tiles, axis=1)


def _matmul(a, b, *, name, trans_a=False, trans_b=False, b_rows=None, a_lead=None, add=None, out_dtype=F32,
            packed_a=False, spread_out=False, norm_bwd=None, norm_fwd=None):
    if trans_a:
        k_dim, m_dim = a.shape[-2:]
        m_dim = m_dim // 2 if packed_a else m_dim
    else:
        m_dim, k_dim = a.shape[-2:]
        k_dim = k_dim // 2 if packed_a else k_dim
    slab_m, slab_k = m_dim, k_dim
    if a_lead == "k":
        assert not trans_a
        k_dim *= a.shape[0]
    elif a_lead == "i":
        assert trans_a
        m_dim *= a.shape[0]
    b_start, b_size = b_rows if b_rows is not None else (0, b.shape[0])
    if trans_b:
        n_dim, k2 = b_size, b.shape[1]
    else:
        k2, n_dim = b_size, b.shape[1]
    assert k_dim == k2, (a.shape, b.shape, b_rows)
    tn = _pick(n_dim, (1024, 512, 256, 128))
    tm = min(slab_m, 2048, max(512, (1024 * 1024) // tn))
    tm = _pick(slab_m, (tm, 1408, 1024, 512, 256, 128))
    out_bytes = jnp.dtype(out_dtype).itemsize * (2 if spread_out else 1)
    if norm_bwd is not None:
        out_bytes = 4 + 4 + 4 + 2
        tm = min(tm, 512)
    if norm_fwd is not None:
        out_bytes += 2

    def deepest(rows):
        fixed = rows * tn * (2 * out_bytes + 4 + (8 if add is not None else 0))
        fits = lambda c: fixed + 2 * 2 * c * ((2 if packed_a else 1) * rows + tn) <= MATMUL_VMEM_BUDGET
        return _pick(slab_k, tuple(c for c in (3072, 2816, 2048, 1536, 1408, 1024, 512, 256) if fits(c)) + (128,))

    tk = deepest(tm)
    if tm % 1024 == 0 and deepest(tm // 2) > tk:
        tm, tk = tm // 2, deepest(tm // 2)
    nk = k_dim // tk
    has_add = add is not None
    dn = (((0 if trans_a else 1,), (1 if trans_b else 0,)), ((), ()))
    b_tile = tn if trans_b else tk
    assert b_start % b_tile == 0, (b_rows, b_tile)
    b_off = b_start // b_tile

    has_norm = norm_bwd is not None
    also_norm = norm_fwd is not None
    if has_norm or also_norm:
        assert n_dim == tn and not spread_out and not (has_norm and also_norm)

    def body(*refs):
        refs = list(refs)
        a_ref, b_ref = refs[:2]
        add_ref = refs[2] if has_add else None
        rest = refs[2 + has_add:]
        if has_norm:
            x_ref, w_ref, skip_ref, dx_ref, dxb_ref, dw_ref, acc_ref = rest
        elif also_norm:
            w_ref, o_ref, hn_ref, acc_ref = rest
        else:
            o_ref, acc_ref = rest
        a_blk = _pack_head_pairs(a_ref[...]).astype(a_ref.dtype) if packed_a else a_ref[...]
        part = lax.dot_general(a_blk, b_ref[...], dn, preferred_element_type=F32)
        first_rows = pl.program_id(0) == 0

        def finish(total):
            if has_add:
                total = total + add_ref[...]
            if has_norm:
                xf = x_ref[...]
                r = lax.rsqrt(jnp.mean(xf * xf, axis=-1, keepdims=True) + RMS_EPS)
                gw = total * w_ref[...]
                dx = r * gw - xf * (r * r * r * jnp.mean(gw * xf, axis=-1, keepdims=True)) + skip_ref[...]
                dx_ref[...] = dx
                dxb_ref[...] = dx.astype(dxb_ref.dtype)
                rows = jnp.sum(total * xf * r, axis=0, keepdims=True)

                @pl.when(first_rows)
                def _():
                    dw_ref[...] = rows

                @pl.when(jnp.logical_not(first_rows))
                def _():
                    dw_ref[...] += rows
                return
            if spread_out:
                total = _spread_head_pairs(total)
            o_ref[...] = total.astype(out_dtype)
            if also_norm:
                r = lax.rsqrt(jnp.mean(total * total, axis=-1, keepdims=True) + RMS_EPS)
                hn_ref[...] = (total * r * w_ref[...]).astype(hn_ref.dtype)

        if nk == 1:
            finish(part)
        else:
            k = pl.program_id(2)

            @pl.when(k == 0)
            def _():
                acc_ref[...] = part

            @pl.when(k > 0)
            def _():
                acc_ref[...] += part

            @pl.when(k == nk - 1)
            def _():
                finish(acc_ref[...])

    wide = 2 if packed_a else 1
    a_tile = (tk, wide * tm) if trans_a else (tm, wide * tk)
    a_at = (lambda i, j, k: (k, i)) if trans_a else (lambda i, j, k: (i, k))
    if a_lead is None:
        a_spec = pl.BlockSpec(a_tile, a_at)
    elif a_lead == "k":
        per = slab_k // tk
        a_spec = pl.BlockSpec((None,) + a_tile, lambda i, j, k: (k // per, i, k % per))
    elif a_lead == "i":
        per = slab_m // tm
        a_spec = pl.BlockSpec((None,) + a_tile, lambda i, j, k: (i // per, k, i % per))
    else:
        a_spec = pl.BlockSpec((None,) + a_tile, lambda i, j, k: (a_lead,) + a_at(i, j, k))
    if trans_b:
        b_spec = pl.BlockSpec((tn, tk), lambda i, j, k: (j + b_off, k))
    else:
        b_spec = pl.BlockSpec((tk, tn), lambda i, j, k: (k + b_off, j))
    in_specs = [a_spec, b_spec]
    args = [a, b]
    tile = pl.BlockSpec((tm, tn), lambda i, j, k: (i, j))
    if has_add:
        in_specs.append(tile)
        args.append(add)
    scratch = [pltpu.VMEM((tm, tn) if nk > 1 else (8, LANES), F32)]
    if has_norm:
        x, w_row, dskip = norm_bwd
        one = pl.BlockSpec((1, tn), lambda i, j, k: (0, 0))
        return pl.pallas_call(
            body,
            grid=(m_dim // tm, 1, nk),
            in_specs=in_specs + [tile, one, tile],
            out_specs=[tile, tile, one],
            out_shape=[jax.ShapeDtypeStruct((m_dim, n_dim), F32), jax.ShapeDtypeStruct((m_dim, n_dim), MM_DTYPE),
                       jax.ShapeDtypeStruct((1, n_dim), F32)],
            scratch_shapes=scratch,
            compiler_params=_cparams(("arbitrary", "arbitrary", "arbitrary")),
            name=name,
        )(*args, x, w_row, dskip)
    if also_norm:
        return pl.pallas_call(
            body,
            grid=(m_dim // tm, 1, nk),
            in_specs=in_specs + [pl.BlockSpec((1, tn), lambda i, j, k: (0, 0))],
            out_specs=[tile, tile],
            out_shape=[jax.ShapeDtypeStruct((m_dim, n_dim), out_dtype), jax.ShapeDtypeStruct((m_dim, n_dim), MM_DTYPE)],
            scratch_shapes=scratch,
            compiler_params=_cparams(("parallel", "parallel", "arbitrary")),
            name=name,
        )(*args, norm_fwd)
    return pl.pallas_call(
        body,
        grid=(m_dim // tm, n_dim // tn, nk),
        in_specs=in_specs,
        out_specs=pl.BlockSpec((tm, (2 if spread_out else 1) * tn), lambda i, j, k: (i, j)),
        out_shape=jax.ShapeDtypeStruct((m_dim, (2 if spread_out else 1) * n_dim), out_dtype),
        scratch_shapes=scratch,
        compiler_params=_cparams(("parallel", "parallel", "arbitrary")),
        name=name,
    )(*args)


def _rmsnorm_fwd(x, w_row, *, name):
    t, d = x.shape
    tb = min(t, 1024)

    def body(x_ref, w_ref, o_ref):
        xf = x_ref[...]
        r = lax.rsqrt(jnp.mean(xf * xf, axis=-1, keepdims=True) + RMS_EPS)
        o_ref[...] = (xf * r * w_ref[...]).astype(o_ref.dtype)

    return pl.pallas_call(
        body,
        grid=(t // tb,),
        in_specs=[pl.BlockSpec((tb, d), lambda i: (i, 0)), pl.BlockSpec((1, d), lambda i: (0, 0))],
        out_specs=pl.BlockSpec((tb, d), lambda i: (i, 0)),
        out_shape=jax.ShapeDtypeStruct((t, d), MM_DTYPE),
        compiler_params=_cparams(("parallel",)),
        name=name,
    )(x, w_row)


def _silu(z):
    return z / (1.0 + jnp.exp(-z))


FFN_TM, FFN_TN = 512, 1408


def _ffn_in(hn, in_t, *, name):
    t, d = hn.shape
    h = FFN_HIDDEN
    tm, tn = min(t, FFN_TM), FFN_TN
    nj = h // tn
    dn = (((1,), (1,)), ((), ()))

    def body(a_ref, bg_ref, bu_ref, g_ref, u_ref, act_ref):
        a = a_ref[...]
        g = lax.dot_general(a, bg_ref[...], dn, preferred_element_type=F32)
        u = lax.dot_general(a, bu_ref[...], dn, preferred_element_type=F32)
        g_ref[...] = g.astype(g_ref.dtype)
        u_ref[...] = u.astype(u_ref.dtype)
        act_ref[...] = (_silu(g) * u).astype(act_ref.dtype)

    out = pl.BlockSpec((tm, tn), lambda j, i: (i, j))
    return pl.pallas_call(
        body,
        grid=(nj, t // tm),
        in_specs=[pl.BlockSpec((tm, d), lambda j, i: (i, 0)), pl.BlockSpec((tn, d), lambda j, i: (j, 0)),
                  pl.BlockSpec((tn, d), lambda j, i: (j + nj, 0))],
        out_specs=[out, out, out],
        out_shape=[jax.ShapeDtypeStruct((t, h), MM_DTYPE)] * 3,
        compiler_params=_cparams(("parallel", "parallel")),
        name=name,
    )(hn, in_t, in_t)


def _ffn_dact(dy, out_w, g, u, *, name):
    t, d = dy.shape
    h = FFN_HIDDEN
    tm, tn = min(t, FFN_TM), FFN_TN

    def body(a_ref, b_ref, g_ref, u_ref, d_ref):
        da = lax.dot_general(a_ref[...], b_ref[...], (((1,), (1,)), ((), ())), preferred_element_type=F32)
        gate = g_ref[...].astype(F32)
        sig = 1.0 / (1.0 + jnp.exp(-gate))
        sg = gate * sig
        d_ref[0] = (da * u_ref[...].astype(F32) * (sig + sg * (1.0 - sig))).astype(d_ref.dtype)
        d_ref[1] = (da * sg).astype(d_ref.dtype)

    blk = pl.BlockSpec((tm, tn), lambda j, i: (i, j))
    return pl.pallas_call(
        body,
        grid=(h // tn, t // tm),
        in_specs=[pl.BlockSpec((tm, d), lambda j, i: (i, 0)), pl.BlockSpec((tn, d), lambda j, i: (j, 0)), blk, blk],
        out_specs=pl.BlockSpec((2, tm, tn), lambda j, i: (0, i, j)),
        out_shape=jax.ShapeDtypeStruct((2, t, h), MM_DTYPE),
        compiler_params=_cparams(("parallel", "parallel")),
        name=name,
    )(dy, out_w, g, u)


def _loss_head(y, target, *, name):
    t, d = y.shape
    tb = min(t, 1024)

    def body(y_ref, t_ref, dy_ref, dyb_ref, l_ref):
        err = y_ref[...] - t_ref[...]
        dy_ref[...] = err * (1.0 / d)
        dyb_ref[...] = (err * (1.0 / d)).astype(dyb_ref.dtype)
        part = jnp.sum(jnp.sum(err * err, axis=0, keepdims=True), axis=1, keepdims=True) * (0.5 / d)
        part = jnp.broadcast_to(part, l_ref.shape)

        @pl.when(pl.program_id(0) == 0)
        def _():
            l_ref[...] = part

        @pl.when(pl.program_id(0) > 0)
        def _():
            l_ref[...] += part

    row = pl.BlockSpec((tb, d), lambda i: (i, 0))
    return pl.pallas_call(
        body,
        grid=(t // tb,),
        in_specs=[row, row],
        out_specs=[row, row, pl.BlockSpec((8, LANES), lambda i: (0, 0))],
        out_shape=[jax.ShapeDtypeStruct((t, d), F32), jax.ShapeDtypeStruct((t, d), MM_DTYPE),
                   jax.ShapeDtypeStruct((8, LANES), F32)],
        compiler_params=_cparams(("arbitrary",)),
        name=name,
    )(y, target)


CONV_HALO = 8
CONV_TIME_TILE = 2048


def _conv_tile_scale(c):
    is_qk = c < 2 * GDN_HEADS
    scale = jnp.where(c < GDN_HEADS, GDN_DK ** -0.5, 1.0).astype(F32)
    return is_qk, scale


def _gdn_conv_fwd(pm, conv_w, *, name):
    t = pm.shape[0]
    tb = min(t, CONV_TIME_TILE)
    nt = t // tb
    hb = tb // CONV_HALO

    def body(x_ref, xp_ref, w_ref, o_ref, xe_ref):
        c = pl.program_id(0)
        ti = pl.program_id(1)
        xe_ref[0:CONV_HALO, :] = jnp.where(ti > 0, xp_ref[...], 0.0)
        xe_ref[CONV_HALO:CONV_HALO + tb, :] = x_ref[...]
        w = w_ref[...]
        y = jnp.zeros((tb, LANES), F32)
        for j in range(GDN_CONV):
            off = CONV_HALO - (GDN_CONV - 1) + j
            y = y + w[j:j + 1, :] * xe_ref[pl.ds(off, tb), :]
        s = _silu(y)
        is_qk, scale = _conv_tile_scale(c)
        r = lax.rsqrt(jnp.sum(s * s, axis=-1, keepdims=True) + L2_EPS) * scale
        o_ref[...] = s * jnp.where(is_qk, r, 1.0)

    return pl.pallas_call(
        body,
        grid=(GDN_QKV // LANES, nt),
        in_specs=[
            pl.BlockSpec((tb, LANES), lambda c, i: (i, c)),
            pl.BlockSpec((CONV_HALO, LANES), lambda c, i: (jnp.maximum(i * hb - 1, 0), c)),
            pl.BlockSpec((GDN_CONV, LANES), lambda c, i: (0, c)),
        ],
        out_specs=pl.BlockSpec((tb, LANES), lambda c, i: (i, c)),
        out_shape=jax.ShapeDtypeStruct((t, GDN_QKV), F32),
        scratch_shapes=[pltpu.VMEM((tb + CONV_HALO, LANES), F32)],
        compiler_params=_cparams(("parallel", "parallel")),
        name=name,
    )(pm, pm, conv_w)


def _gdn_conv_bwd(pm, conv_w, dout, *, name):
    t = pm.shape[0]
    tb = min(t, CONV_TIME_TILE)
    nt = t // tb
    hb = tb // CONV_HALO
    last_hb = t // CONV_HALO - 1
    ext = tb + CONV_HALO

    def body(x_ref, xp_ref, xn_ref, d_ref, dn_ref, w_ref, dx_ref, dw_ref, xe_ref, dy_ref):
        c = pl.program_id(0)
        ti = pl.program_id(1)
        has_next = ti < nt - 1
        xe_ref[0:CONV_HALO, :] = jnp.where(ti > 0, xp_ref[...], 0.0)
        xe_ref[CONV_HALO:CONV_HALO + tb, :] = x_ref[...]
        xe_ref[CONV_HALO + tb:2 * CONV_HALO + tb, :] = jnp.where(has_next, xn_ref[...], 0.0)
        de = jnp.concatenate([d_ref[...], jnp.where(has_next, dn_ref[...], 0.0)], axis=0)
        w = w_ref[...]
        y = jnp.zeros((ext, LANES), F32)
        for j in range(GDN_CONV):
            off = CONV_HALO - (GDN_CONV - 1) + j
            y = y + w[j:j + 1, :] * xe_ref[pl.ds(off, ext), :]
        sig = 1.0 / (1.0 + jnp.exp(-y))
        s = y * sig
        is_qk, scale = _conv_tile_scale(c)
        r = lax.rsqrt(jnp.sum(s * s, axis=-1, keepdims=True) + L2_EPS)
        n = s * r
        dnrm = de * scale
        ds_qk = r * (dnrm - n * jnp.sum(dnrm * n, axis=-1, keepdims=True))
        ds = jnp.where(is_qk, ds_qk, de)
        dy_ref[...] = ds * (sig + s * (1.0 - sig))
        dy = dy_ref[0:tb, :]
        dx = jnp.zeros((tb, LANES), F32)
        dw_rows = []
        for j in range(GDN_CONV):
            sh = GDN_CONV - 1 - j
            dx = dx + w[j:j + 1, :] * dy_ref[pl.ds(sh, tb), :]
            off = CONV_HALO - (GDN_CONV - 1) + j
            dw_rows.append(jnp.sum(dy * xe_ref[pl.ds(off, tb), :], axis=0, keepdims=True))
        dx_ref[...] = dx.astype(dx_ref.dtype)
        part = jnp.concatenate(dw_rows, axis=0)

        @pl.when(ti == 0)
        def _():
            dw_ref[...] = part

        @pl.when(ti > 0)
        def _():
            dw_ref[...] += part

    main = pl.BlockSpec((tb, LANES), lambda c, i: (i, c))
    prev = pl.BlockSpec((CONV_HALO, LANES), lambda c, i: (jnp.maximum(i * hb - 1, 0), c))
    nxt = pl.BlockSpec((CONV_HALO, LANES), lambda c, i: (jnp.minimum((i + 1) * hb, last_hb), c))
    return pl.pallas_call(
        body,
        grid=(GDN_QKV // LANES, nt),
        in_specs=[main, prev, nxt, main, nxt, pl.BlockSpec((GDN_CONV, LANES), lambda c, i: (0, c))],
        out_specs=[main, pl.BlockSpec((GDN_CONV, LANES), lambda c, i: (0, c))],
        out_shape=[jax.ShapeDtypeStruct((t, GDN_QKV), MM_DTYPE), jax.ShapeDtypeStruct((GDN_CONV, GDN_QKV), F32)],
        scratch_shapes=[pltpu.VMEM((tb + 2 * CONV_HALO, LANES), F32), pltpu.VMEM((ext, LANES), F32)],
        compiler_params=_cparams(("parallel", "arbitrary")),
        name=name,
    )(pm, pm, pm, dout, dout, conv_w)


def _head_selector(first_col):
    row = lax.broadcasted_iota(jnp.int32, (LANES, GDN_HEADS * LANES), 0)
    col = lax.broadcasted_iota(jnp.int32, (LANES, GDN_HEADS * LANES), 1)
    return (col // LANES + first_col == row).astype(BF16)


def _spread_columns(cols, first_col):
    sel = _head_selector(first_col)
    return sum(_bdot(p, sel) for p in _bf16_pieces(cols))


def _gather_columns(wide, first_col):
    sel = _head_selector(first_col)
    return sum(_bdot_nt(p, sel) for p in _bf16_pieces(wide))


def _softplus(x):
    return jnp.maximum(x, 0.0) + jnp.log(1.0 + jnp.exp(-jnp.abs(x)))


def _gdn_gates_fwd(ab, alog_row, dt_row, *, name):
    t = ab.shape[0]
    tb = min(t, 1024)
    wide = GDN_HEADS * LANES

    def body(ab_ref, al_ref, dt_ref, g_ref, b_ref):
        x = ab_ref[...]
        g_cols = -jnp.exp(al_ref[...]) * _softplus(x + dt_ref[...])
        b_cols = 1.0 / (1.0 + jnp.exp(-x))
        g_ref[...] = _spread_columns(g_cols, 0)
        b_ref[...] = _spread_columns(b_cols, GDN_HEADS)

    row = pl.BlockSpec((tb, LANES), lambda i: (i, 0))
    one = pl.BlockSpec((1, LANES), lambda i: (0, 0))
    out = pl.BlockSpec((tb, wide), lambda i: (i, 0))
    return pl.pallas_call(
        body,
        grid=(t // tb,),
        in_specs=[row, one, one],
        out_specs=[out, out],
        out_shape=[jax.ShapeDtypeStruct((t, wide), F32)] * 2,
        compiler_params=_cparams(("parallel",)),
        name=name,
    )(ab, alog_row, dt_row)


def _gdn_gates_bwd(ab, alog_row, dt_row, dgb, dbb, *, name):
    t = ab.shape[0]
    tb = min(t, 1024)
    wide = GDN_HEADS * LANES

    def body(ab_ref, al_ref, dt_ref, dg_ref, db_ref, dab_ref, dal_ref, ddt_ref):
        x = ab_ref[...]
        lane = lax.broadcasted_iota(jnp.int32, (tb, LANES), 1)
        dg_cols = _gather_columns(dg_ref[...], 0)
        db_cols = _gather_columns(db_ref[...], GDN_HEADS)
        ea = jnp.exp(al_ref[...])
        z = x + dt_ref[...]
        sp = _softplus(z)
        sg = 1.0 / (1.0 + jnp.exp(-z))
        beta = 1.0 / (1.0 + jnp.exp(-x))
        da = jnp.where(lane < GDN_HEADS, dg_cols * (-ea) * sg, 0.0)
        db = jnp.where((lane >= GDN_HEADS) & (lane < 2 * GDN_HEADS), db_cols * beta * (1.0 - beta), 0.0)
        dab_ref[...] = (da + db).astype(dab_ref.dtype)
        p_al = jnp.sum(jnp.where(lane < GDN_HEADS, dg_cols * (-ea) * sp, 0.0), axis=0, keepdims=True)
        p_dt = jnp.sum(da, axis=0, keepdims=True)

        @pl.when(pl.program_id(0) == 0)
        def _():
            dal_ref[...] = p_al
            ddt_ref[...] = p_dt

        @pl.when(pl.program_id(0) > 0)
        def _():
            dal_ref[...] += p_al
            ddt_ref[...] += p_dt

    row = pl.BlockSpec((tb, LANES), lambda i: (i, 0))
    one = pl.BlockSpec((1, LANES), lambda i: (0, 0))
    big = pl.BlockSpec((tb, wide), lambda i: (i, 0))
    return pl.pallas_call(
        body,
        grid=(t // tb,),
        in_specs=[row, one, one, big, big],
        out_specs=[row, one, one],
        out_shape=[jax.ShapeDtypeStruct((t, LANES), MM_DTYPE), jax.ShapeDtypeStruct((1, LANES), F32),
                   jax.ShapeDtypeStruct((1, LANES), F32)],
        compiler_params=_cparams(("arbitrary",)),
        name=name,
    )(ab, alog_row, dt_row, dgb, dbb)


@jax.custom_vjp
def _unit_lower_inverse_rest(n):
    c = n.shape[-1]
    ri = lax.broadcasted_iota(jnp.int32, (c, c), 0)
    ci = lax.broadcasted_iota(jnp.int32, (c, c), 1)
    rest = None
    size = 1
    while size < c:
        joins = ((ri // (2 * size)) == (ci // (2 * size))) & ((ri // size) != (ci // size))
        low = jnp.where(joins, n, 0.0)
        if rest is None:
            rest = -low
        else:
            left = low + _bdot(rest, low)
            rest = rest - (left + _bdot(left, rest))
        size *= 2
    return rest


def _unit_lower_inverse_rest_fwd(n):
    rest = _unit_lower_inverse_rest(n)
    return rest, rest


def _unit_lower_inverse_rest_bwd(rest, ct):
    left = ct + _bdot_tn(rest, ct)
    return (-(left + _bdot_nt(left, rest)),)


_unit_lower_inverse_rest.defvjp(_unit_lower_inverse_rest_fwd, _unit_lower_inverse_rest_bwd)


@jax.custom_vjp
def _known_inverse_rest(n, rest):
    return rest


def _known_inverse_rest_fwd(n, rest):
    return rest, rest


def _known_inverse_rest_bwd(rest, ct):
    return _unit_lower_inverse_rest_bwd(rest, ct) + (jnp.zeros_like(rest),)


_known_inverse_rest.defvjp(_known_inverse_rest_fwd, _known_inverse_rest_bwd)


def _bf16_pieces(x):
    hi = x.astype(BF16)
    r1 = x - hi.astype(F32)
    mid = r1.astype(BF16)
    lo = (r1 - mid.astype(F32)).astype(BF16)
    return hi, mid, lo


def _lower_ones(shape):
    c = shape[-1]
    ri = lax.broadcasted_iota(jnp.int32, (c, c), 0)
    ci = lax.broadcasted_iota(jnp.int32, (c, c), 1)
    return jnp.broadcast_to((ri >= ci).astype(BF16), shape)


@jax.custom_vjp
def _running_sum(x):
    tri = _lower_ones(x.shape)
    return sum(_bdot(tri, p) for p in _bf16_pieces(x))


def _running_sum_fwd(x):
    return _running_sum(x), None


def _running_sum_bwd(_, ct):
    tri = _lower_ones(ct.shape)
    return (sum(_bdot_tn(tri, p) for p in _bf16_pieces(ct)),)


_running_sum.defvjp(_running_sum_fwd, _running_sum_bwd)


def _gdn_prep_math(q, k, v, gb, bb, known_rest=None, with_rest=False):
    c = GDN_CHUNK
    ri = lax.broadcasted_iota(jnp.int32, (c, c), 0)
    ci = lax.broadcasted_iota(jnp.int32, (c, c), 1)
    causal = ri >= ci
    gc = _running_sum(gb)
    decay = jnp.exp(jnp.where(causal, gc - jnp.swapaxes(gc, -1, -2), NEG))
    n = jnp.where(ri > ci, _bdot_nt(k, k) * bb * decay, 0.0)
    rest = _unit_lower_inverse_rest(n) if known_rest is None else _known_inverse_rest(n, known_rest)
    eg = jnp.exp(gc)
    rhs_v = v * bb
    rhs_k = k * bb * eg
    u = rhs_v + _bdot(rest, rhs_v)
    w = rhs_k + _bdot(rest, rhs_k)
    qk = _bdot_nt(q, k) * decay
    qd = q * eg
    last = jnp.sum(jnp.where(ri == c - 1, gc, 0.0), axis=-2, keepdims=True)
    gl = jnp.broadcast_to(last, gc.shape)
    kt = k * jnp.exp(gl - gc)
    cd = jnp.exp(gl)
    return (u, w, qk, qd, kt, cd, rest) if with_rest else (u, w, qk, qd, kt, cd)


def _head_tiles(ref, h):
    return ref[:, h * LANES:(h + 1) * LANES]


def _stack_heads(ref, first=0, heads=GDN_HEADS):
    return jnp.stack([_head_tiles(ref, first + h) for h in range(heads)])


def _store_heads(ref, val, first=0):
    for h in range(val.shape[0]):
        ref[:, (first + h) * LANES:(first + h + 1) * LANES] = val[h].astype(ref.dtype)


def _gdn_prep_fwd(qkv, gb, bb, *, name):
    t = qkv.shape[0]
    c = GDN_CHUNK
    wide = GDN_HEADS * LANES

    def body(q_ref, k_ref, v_ref, g_ref, b_ref, *outs):
        res = _gdn_prep_math(*(_stack_heads(r) for r in (q_ref, k_ref, v_ref, g_ref, b_ref)), with_rest=True)
        for o_ref, val in zip(outs, res):
            _store_heads(o_ref, val)

    blk = lambda off: pl.BlockSpec((c, wide), lambda i: (i, off))
    outs = pl.pallas_call(
        body,
        grid=(t // c,),
        in_specs=[blk(0), blk(1), blk(2), blk(0), blk(0)],
        out_specs=[blk(0)] * 7,
        out_shape=[jax.ShapeDtypeStruct((t, wide), dt)
                   for dt in (F32, MM_DTYPE, MM_DTYPE, MM_DTYPE, MM_DTYPE, F32, MM_DTYPE)],
        compiler_params=_cparams(("parallel",)),
        name=name,
    )(qkv, qkv, qkv, gb, bb)
    return tuple(outs[:6]), outs[6]


def _gdn_prep_bwd(qkv, gb, bb, rest, cts, *, name):
    t = qkv.shape[0]
    c = GDN_CHUNK
    wide = GDN_HEADS * LANES

    def body(q_ref, k_ref, v_ref, g_ref, b_ref, r_ref, c0, c1, c2, c3, c4, c5, dqkv_ref, dg_ref, db_ref):
        prim = tuple(_stack_heads(r) for r in (q_ref, k_ref, v_ref, g_ref, b_ref))
        _, pull = jax.vjp(functools.partial(_gdn_prep_math, known_rest=_stack_heads(r_ref).astype(F32)), *prim)
        dq, dk, dv, dg, db = pull(tuple(_stack_heads(r).astype(F32) for r in (c0, c1, c2, c3, c4, c5)))
        _store_heads(dqkv_ref, dq)
        _store_heads(dqkv_ref, dk, first=GDN_HEADS)
        _store_heads(dqkv_ref, dv, first=2 * GDN_HEADS)
        _store_heads(dg_ref, dg)
        _store_heads(db_ref, db)

    blk = lambda off: pl.BlockSpec((c, wide), lambda i: (i, off))
    return pl.pallas_call(
        body,
        grid=(t // c,),
        in_specs=[blk(0), blk(1), blk(2), blk(0), blk(0)] + [blk(0)] * 7,
        out_specs=[pl.BlockSpec((c, 3 * wide), lambda i: (i, 0)), blk(0), blk(0)],
        out_shape=[jax.ShapeDtypeStruct((t, 3 * wide), F32), jax.ShapeDtypeStruct((t, wide), F32),
                   jax.ShapeDtypeStruct((t, wide), F32)],
        compiler_params=_cparams(("parallel",)),
        name=name,
    )(qkv, qkv, qkv, gb, bb, rest, *cts)


def _gdn_scan_math(s, u, w, qk, qd, kt, cd):
    v_new = u - _bdot(w, s)
    o = _bdot(qd, s) + _bdot(qk, v_new)
    s_new = s * cd + _bdot_tn(kt, v_new)
    return o, s_new


def _gdn_scan_fwd(prep, *, name):
    t = prep[0].shape[0]
    c = GDN_CHUNK
    wide = GDN_HEADS * LANES

    def body(u_ref, w_ref, qk_ref, qd_ref, kt_ref, cd_ref, o_ref, st_ref, s_ref):
        @pl.when(pl.program_id(0) == 0)
        def _():
            s_ref[...] = jnp.zeros_like(s_ref)

        s = _stack_heads(s_ref)
        _store_heads(st_ref, s)
        o, s_new = _gdn_scan_math(s, *(_stack_heads(r).astype(F32) for r in (u_ref, w_ref, qk_ref, qd_ref, kt_ref, cd_ref)))
        _store_heads(o_ref, o)
        _store_heads(s_ref, s_new)

    blk = pl.BlockSpec((c, wide), lambda i: (i, 0))
    return pl.pallas_call(
        body,
        grid=(t // c,),
        in_specs=[blk] * 6,
        out_specs=[blk, blk],
        out_shape=[jax.ShapeDtypeStruct((t, wide), F32)] * 2,
        scratch_shapes=[pltpu.VMEM((GDN_DK, wide), F32)],
        compiler_params=_cparams(("arbitrary",)),
        name=name,
    )(*prep)


def _gdn_scan_bwd(prep, states, do, *, name):
    t = do.shape[0]
    c = GDN_CHUNK
    wide = GDN_HEADS * LANES
    nc = t // c

    def body(u_ref, w_ref, qk_ref, qd_ref, kt_ref, cd_ref, st_ref, do_ref, *rest):
        outs, ds_ref = rest[:6], rest[6]

        @pl.when(pl.program_id(0) == 0)
        def _():
            ds_ref[...] = jnp.zeros_like(ds_ref)

        prim = tuple(_stack_heads(r).astype(F32) for r in (st_ref, u_ref, w_ref, qk_ref, qd_ref, kt_ref, cd_ref))
        _, pull = jax.vjp(_gdn_scan_math, *prim)
        grads = pull((_stack_heads(do_ref), _stack_heads(ds_ref)))
        _store_heads(ds_ref, grads[0])
        for o_ref, val in zip(outs, grads[1:]):
            _store_heads(o_ref, val)

    blk = pl.BlockSpec((c, wide), lambda i: (nc - 1 - i, 0))
    return pl.pallas_call(
        body,
        grid=(nc,),
        in_specs=[blk] * 8,
        out_specs=[blk] * 6,
        out_shape=[jax.ShapeDtypeStruct((t, wide), dt) for dt in (F32, MM_DTYPE, MM_DTYPE, MM_DTYPE, MM_DTYPE, F32)],
        scratch_shapes=[pltpu.VMEM((GDN_DK, wide), F32)],
        compiler_params=_cparams(("arbitrary",)),
        name=name,
    )(*prep, states, do)


def _gdn_outgate_math(o, z, nw):
    r = lax.rsqrt(jnp.mean(o * o, axis=-1, keepdims=True) + RMS_EPS)
    return o * r * nw * _silu(z)


def _gdn_outgate_fwd(o, pm, nw_row, *, name):
    t = o.shape[0]
    tb = min(t, ROW_TILE)
    wide = GDN_HEADS * LANES
    z_at = GDN_QKV // wide

    def body(o_ref, z_ref, nw_ref, y_ref):
        for h in range(GDN_HEADS):
            y = _gdn_outgate_math(_head_tiles(o_ref, h), _head_tiles(z_ref, h), nw_ref[...])
            y_ref[:, h * LANES:(h + 1) * LANES] = y.astype(y_ref.dtype)

    return pl.pallas_call(
        body,
        grid=(t // tb,),
        in_specs=[pl.BlockSpec((tb, wide), lambda i: (i, 0)), pl.BlockSpec((tb, wide), lambda i: (i, z_at)),
                  pl.BlockSpec((1, LANES), lambda i: (0, 0))],
        out_specs=pl.BlockSpec((tb, wide), lambda i: (i, 0)),
        out_shape=jax.ShapeDtypeStruct((t, wide), MM_DTYPE),
        compiler_params=_cparams(("parallel",)),
        name=name,
    )(o, pm, nw_row)


def _gdn_outgate_bwd(o, pm, nw_row, dy, *, name):
    t = o.shape[0]
    tb = min(t, ROW_TILE)
    wide = GDN_HEADS * LANES
    z_at = GDN_QKV // wide

    def body(o_ref, z_ref, nw_ref, dy_ref, do_ref, dz_ref, dnw_ref):
        total = jnp.zeros((1, LANES), F32)
        for h in range(GDN_HEADS):
            _, pull = jax.vjp(_gdn_outgate_math, _head_tiles(o_ref, h), _head_tiles(z_ref, h), nw_ref[...])
            d_o, d_z, d_nw = pull(_head_tiles(dy_ref, h))
            do_ref[:, h * LANES:(h + 1) * LANES] = d_o
            dz_ref[:, h * LANES:(h + 1) * LANES] = d_z.astype(dz_ref.dtype)
            total = total + d_nw

        @pl.when(pl.program_id(0) == 0)
        def _():
            dnw_ref[...] = total

        @pl.when(pl.program_id(0) > 0)
        def _():
            dnw_ref[...] += total

    blk = pl.BlockSpec((tb, wide), lambda i: (i, 0))
    one = pl.BlockSpec((1, LANES), lambda i: (0, 0))
    return pl.pallas_call(
        body,
        grid=(t // tb,),
        in_specs=[blk, pl.BlockSpec((tb, wide), lambda i: (i, z_at)), one, blk],
        out_specs=[blk, blk, one],
        out_shape=[jax.ShapeDtypeStruct((t, wide), F32), jax.ShapeDtypeStruct((t, wide), MM_DTYPE),
                   jax.ShapeDtypeStruct((1, LANES), F32)],
        compiler_params=_cparams(("arbitrary",)),
        name=name,
    )(o, pm, nw_row, dy)


def _rms64(x, w_row):
    return x * lax.rsqrt(jnp.sum(x * x, axis=-1, keepdims=True) * (1.0 / DIL_DH) + RMS_EPS) * w_row


def _alibi_slopes(group):
    head = lax.broadcasted_iota(jnp.int32, (DIL_HEADS, 8, LANES), 0).astype(F32)
    rate = -math.log(2.0) * ALIBI_MAX_BIAS / (len(DIL_GROUPS) * DIL_HEADS)
    slope = jnp.exp(rate * (head + float(group * DIL_HEADS + 1)))
    return jnp.broadcast_to(slope[:, 0:1, :], (DIL_HEADS, DIL_SPAN, LANES))


def _band_logits(qn, kp, kc, slope_d, has_prev):
    qi = lax.broadcasted_iota(jnp.int32, (DIL_SPAN, DIL_SPAN), 0)
    kj = lax.broadcasted_iota(jnp.int32, (DIL_SPAN, DIL_SPAN), 1)
    steps_c = (qi - kj).astype(F32)
    scale = DIL_DH ** -0.5
    sp = _bdot_nt(qn, kp) * scale - slope_d * (steps_c + float(DIL_SPAN))
    sc = _bdot_nt(qn, kc) * scale - slope_d * steps_c
    sp = jnp.where((kj >= qi) & has_prev, sp, NEG)
    sc = jnp.where(kj <= qi, sc, NEG)
    return sp, sc


def _dil_attn_fwd(slab, wq_row, wk_row, *, group, name):
    dilation = DIL_GROUPS[group][1]
    t = slab.shape[0]
    rows = t // dilation
    nlb = rows // DIL_SPAN
    wide = DIL_HEADS * LANES
    view = slab.reshape(rows, dilation * DIL_SLAB)

    def body(q_ref, kc_ref, vc_ref, kp_ref, vp_ref, wq_ref, wk_ref, o_ref):
        has_prev = pl.program_id(1) > 0
        lane = lax.broadcasted_iota(jnp.int32, (DIL_SPAN, LANES), 1)
        qn = _rms64(_stack_heads(q_ref), wq_ref[...])
        kc = _rms64(_stack_heads(kc_ref), wk_ref[...])
        kp = _rms64(_stack_heads(kp_ref), wk_ref[...])
        sp, sc = _band_logits(qn, kp, kc, _alibi_slopes(group) * float(dilation), has_prev)
        m = jnp.maximum(jnp.max(sp, axis=-1, keepdims=True), jnp.max(sc, axis=-1, keepdims=True))
        pp = jnp.exp(sp - m)
        pc = jnp.exp(sc - m)
        l = jnp.sum(pp, axis=-1, keepdims=True) + jnp.sum(pc, axis=-1, keepdims=True)
        o = (_bdot(pp, _stack_heads(vp_ref)) + _bdot(pc, _stack_heads(vc_ref))) / l
        _store_heads(o_ref, jnp.where(lane < DIL_DH, o, m + jnp.log(l)))

    cur = lambda part: pl.BlockSpec((DIL_SPAN, wide), lambda r, i: (i, 3 * r + part))
    prv = lambda part: pl.BlockSpec((DIL_SPAN, wide), lambda r, i: (jnp.maximum(i - 1, 0), 3 * r + part))
    one = pl.BlockSpec((1, LANES), lambda r, i: (0, 0))
    out = pl.pallas_call(
        body,
        grid=(dilation, nlb),
        in_specs=[cur(0), cur(1), cur(2), prv(1), prv(2), one, one],
        out_specs=pl.BlockSpec((DIL_SPAN, wide), lambda r, i: (i, r)),
        out_shape=jax.ShapeDtypeStruct((rows, dilation * wide), F32),
        compiler_params=_cparams(("parallel", "parallel")),
        name=name,
    )(view, view, view, view, view, wq_row, wk_row)
    return out.reshape(t, wide)


def _head_slope(group, head):
    idx = jnp.zeros((8, LANES), F32) + head.astype(F32)
    rate = -math.log(2.0) * ALIBI_MAX_BIAS / (len(DIL_GROUPS) * DIL_HEADS)
    slope = jnp.exp(rate * (idx + float(group * DIL_HEADS + 1)))
    return jnp.broadcast_to(slope[0:1, :], (DIL_SPAN, LANES))


RESIDUE_BATCH = 8


def _take_residues(ref, d, first=0, count=None):
    count = d if count is None else count
    return jnp.stack([ref[pl.ds(first + r, DIL_SPAN, stride=d), :] for r in range(count)])


def _put_residues(ref, val, d, first=0):
    for r in range(val.shape[0]):
        ref[pl.ds(first + r, DIL_SPAN, stride=d), :] = val[r]


def _dil_attn_fwd_strided(slab, wq_row, wk_row, *, group, name):
    d = DIL_GROUPS[group][1]
    t = slab.shape[0]
    span = DIL_SPAN * d
    nsb = t // span

    hs = max(1, RESIDUE_BATCH // d)

    def body(*refs):
        q, kc, vc, kp, vp = (refs[i * hs:(i + 1) * hs] for i in range(5))
        wq_ref, wk_ref, o_ref, spread = refs[5 * hs:]
        has_prev = pl.program_id(0) > 0
        lane = lax.broadcasted_iota(jnp.int32, (DIL_SPAN, LANES), 1)
        nb = min(d, RESIDUE_BATCH)
        for r0 in range(0, d, nb):
            take = lambda group_refs: jnp.concatenate([_take_residues(ref, d, r0, nb) for ref in group_refs])
            slope = jnp.concatenate([jnp.broadcast_to(_head_slope(group, pl.program_id(1) * hs + j) * float(d),
                                                      (nb, DIL_SPAN, LANES)) for j in range(hs)])
            qn = _rms64(take(q), wq_ref[...])
            kcn = _rms64(take(kc), wk_ref[...])
            kpn = _rms64(take(kp), wk_ref[...])
            sp, sc = _band_logits(qn, kpn, kcn, slope, has_prev)
            m = jnp.maximum(jnp.max(sp, axis=-1, keepdims=True), jnp.max(sc, axis=-1, keepdims=True))
            pp = jnp.exp(sp - m)
            pc = jnp.exp(sc - m)
            l = jnp.sum(pp, axis=-1, keepdims=True) + jnp.sum(pc, axis=-1, keepdims=True)
            o = (_bdot(pp, take(vp)) + _bdot(pc, take(vc))) / l
            res = jnp.where(lane < DIL_DH, o, m + jnp.log(l))
            for j in range(hs):
                _put_residues(spread, res[j * nb:(j + 1) * nb], d, r0)
                if r0 + nb == d:
                    o_ref[:, j * LANES:(j + 1) * LANES] = spread[...]

    cur = lambda part, j: pl.BlockSpec((span, LANES), lambda i, h: (i, part * DIL_HEADS + h * hs + j))
    prv = lambda part, j: pl.BlockSpec((span, LANES), lambda i, h: (jnp.maximum(i - 1, 0), part * DIL_HEADS + h * hs + j))
    one = pl.BlockSpec((1, LANES), lambda i, h: (0, 0))
    heads = range(hs)
    in_specs = ([cur(0, j) for j in heads] + [cur(1, j) for j in heads] + [cur(2, j) for j in heads]
                + [prv(1, j) for j in heads] + [prv(2, j) for j in heads] + [one, one])
    return pl.pallas_call(
        body,
        grid=(nsb, DIL_HEADS // hs),
        in_specs=in_specs,
        out_specs=pl.BlockSpec((span, hs * LANES), lambda i, h: (i, h)),
        out_shape=jax.ShapeDtypeStruct((t, DIL_HEADS * LANES), F32),
        scratch_shapes=[pltpu.VMEM((span, LANES), F32)],
        compiler_params=_cparams(("parallel", "parallel")),
        name=name,
    )(*([slab] * (5 * hs)), wq_row, wk_row)


def _dil_attn_bwd_strided(slab, stat, wq_row, wk_row, dwq_in, dwk_in, *, group, name):
    d = DIL_GROUPS[group][1]
    t = slab.shape[0]
    span = DIL_SPAN * d
    nsb = t // span

    hs = max(1, RESIDUE_BATCH // d)

    def body(*refs):
        q_refs, kc_refs, vc_refs, kp_refs, vp_refs, st_refs = (refs[i * hs:(i + 1) * hs] for i in range(6))
        wq_ref, wk_ref, dwq_in_ref, dwk_in_ref, d_ref, dwq_ref, dwk_ref, dk_carry, dv_carry, spread = refs[6 * hs:]
        take = lambda group_refs: jnp.concatenate([_take_residues(ref, d) for ref in group_refs])
        step = pl.program_id(1)
        has_prev = step < nsb - 1
        first = (pl.program_id(0) == 0) & (step == 0)

        @pl.when(step == 0)
        def _():
            dk_carry[...] = jnp.zeros_like(dk_carry)
            dv_carry[...] = jnp.zeros_like(dv_carry)

        @pl.when(first)
        def _():
            dwq_ref[...] = dwq_in_ref[...]
            dwk_ref[...] = dwk_in_ref[...]

        lane = lax.broadcasted_iota(jnp.int32, (DIL_SPAN, LANES), 1)
        scale = DIL_DH ** -0.5
        q_raw = take(q_refs)
        kc_raw = take(kc_refs)
        vc = take(vc_refs)
        kp_raw = take(kp_refs)
        vp = take(vp_refs)
        st = take(st_refs)
        slope = jnp.concatenate([jnp.broadcast_to(_head_slope(group, pl.program_id(0) * hs + j) * float(d),
                                                  (d, DIL_SPAN, LANES)) for j in range(hs)])
        d_o = jnp.where(lane < DIL_DH, st, 0.0)
        lse = jnp.sum(jnp.where(lane == DIL_DH, st, 0.0), axis=-1, keepdims=True)
        delta = jnp.sum(jnp.where(lane == DIL_DH + 1, st, 0.0), axis=-1, keepdims=True)
        qn = _rms64(q_raw, wq_ref[...])
        kc = _rms64(kc_raw, wk_ref[...])
        kp = _rms64(kp_raw, wk_ref[...])
        sp, sc = _band_logits(qn, kp, kc, slope, has_prev)
        pp = jnp.exp(sp - lse)
        pc = jnp.exp(sc - lse)
        dsp = pp * (_bdot_nt(d_o, vp) - delta) * scale
        dsc = pc * (_bdot_nt(d_o, vc) - delta) * scale
        dqn = _bdot(dsp, kp) + _bdot(dsc, kc)
        dkc_n = _bdot_tn(dsc, qn) + dk_carry[...]
        dvc = _bdot_tn(pc, d_o) + dv_carry[...]
        dk_carry[...] = _bdot_tn(dsp, qn)
        dv_carry[...] = _bdot_tn(pp, d_o)
        dq_raw, dwq_rows = _rms64_bwd(q_raw, wq_ref[...], dqn)
        dk_raw, dwk_rows = _rms64_bwd(kc_raw, wk_ref[...], dkc_n)
        for part, val in enumerate((dq_raw, dk_raw, dvc)):
            for j in range(hs):
                _put_residues(spread, val[j * d:(j + 1) * d], d)
                d_ref[part, :, j * LANES:(j + 1) * LANES] = spread[...].astype(d_ref.dtype)
        dwq_ref[...] += jnp.sum(jnp.sum(dwq_rows, axis=0), axis=0, keepdims=True)
        dwk_ref[...] += jnp.sum(jnp.sum(dwk_rows, axis=0), axis=0, keepdims=True)

    at = lambda i: nsb - 1 - i
    cur = lambda part, j: pl.BlockSpec((span, LANES), lambda h, i: (at(i), part * DIL_HEADS + h * hs + j))
    prv = lambda part, j: pl.BlockSpec((span, LANES), lambda h, i: (jnp.maximum(at(i) - 1, 0), part * DIL_HEADS + h * hs + j))
    one = pl.BlockSpec((1, LANES), lambda h, i: (0, 0))
    heads = range(hs)
    in_specs = ([cur(0, j) for j in heads] + [cur(1, j) for j in heads] + [cur(2, j) for j in heads]
                + [prv(1, j) for j in heads] + [prv(2, j) for j in heads] + [cur(0, j) for j in heads] + [one] * 4)
    return pl.pallas_call(
        body,
        grid=(DIL_HEADS // hs, nsb),
        in_specs=in_specs,
        out_specs=[pl.BlockSpec((3, span, hs * LANES), lambda h, i: (0, at(i), h)), one, one],
        out_shape=[jax.ShapeDtypeStruct((3, t, DIL_HEADS * LANES), MM_DTYPE), jax.ShapeDtypeStruct((1, LANES), F32),
                   jax.ShapeDtypeStruct((1, LANES), F32)],
        scratch_shapes=[pltpu.VMEM((hs * d, DIL_SPAN, LANES), F32), pltpu.VMEM((hs * d, DIL_SPAN, LANES), F32),
                        pltpu.VMEM((span, LANES), F32)],
        compiler_params=_cparams(("arbitrary", "arbitrary")),
        name=name,
    )(*([slab] * (5 * hs)), *([stat] * hs), wq_row, wk_row, dwq_in, dwk_in)


def _dil_merge_fwd(oe, *, name):
    t = oe[0].shape[0]
    tb = min(t, ROW_TILE)
    wide = DIL_HEADS * LANES

    def body(e0, e1, e2, y_ref, om_ref):
        lane = lax.broadcasted_iota(jnp.int32, (tb, LANES), 1)
        for h in range(DIL_HEADS):
            es = [_head_tiles(e, h) for e in (e0, e1, e2)]
            lse = [jnp.sum(jnp.where(lane == DIL_DH, e, 0.0), axis=-1, keepdims=True) for e in es]
            top = jnp.maximum(jnp.maximum(lse[0], lse[1]), lse[2])
            joint = top + jnp.log(jnp.exp(lse[0] - top) + jnp.exp(lse[1] - top) + jnp.exp(lse[2] - top))
            o = sum(jnp.exp(l - joint) * e for l, e in zip(lse, es))
            y_ref[:, h * LANES:(h + 1) * LANES] = jnp.where(lane < DIL_DH, o, 0.0).astype(y_ref.dtype)
            om_ref[:, h * LANES:(h + 1) * LANES] = jnp.where(lane < DIL_DH, o, joint)

    blk = pl.BlockSpec((tb, wide), lambda i: (i, 0))
    return pl.pallas_call(
        body,
        grid=(t // tb,),
        in_specs=[blk] * 3,
        out_specs=[blk, blk],
        out_shape=[jax.ShapeDtypeStruct((t, wide), MM_DTYPE), jax.ShapeDtypeStruct((t, wide), F32)],
        compiler_params=_cparams(("parallel",)),
        name=name,
    )(*oe)


def _dil_merge_bwd(dy, om, *, name):
    t = dy.shape[0]
    tb = min(t, ROW_TILE)
    wide = DIL_HEADS * LANES

    def body(dy_ref, om_ref, st_ref):
        lane = lax.broadcasted_iota(jnp.int32, (tb, LANES), 1)
        for h in range(DIL_HEADS):
            d_o = jnp.where(lane < DIL_DH, _head_tiles(dy_ref, h), 0.0)
            om_t = _head_tiles(om_ref, h)
            delta = jnp.sum(d_o * om_t, axis=-1, keepdims=True)
            st_ref[:, h * LANES:(h + 1) * LANES] = jnp.where(
                lane < DIL_DH, d_o, jnp.where(lane == DIL_DH, om_t, jnp.where(lane == DIL_DH + 1, delta, 0.0)))

    blk = pl.BlockSpec((tb, wide), lambda i: (i, 0))
    return pl.pallas_call(
        body,
        grid=(t // tb,),
        in_specs=[blk, blk],
        out_specs=blk,
        out_shape=jax.ShapeDtypeStruct((t, wide), F32),
        compiler_params=_cparams(("parallel",)),
        name=name,
    )(dy, om)


def _rms64_bwd(x, w_row, dy):
    r = lax.rsqrt(jnp.sum(x * x, axis=-1, keepdims=True) * (1.0 / DIL_DH) + RMS_EPS)
    gw = dy * w_row
    dx = r * gw - x * (r * r * r * jnp.sum(gw * x, axis=-1, keepdims=True) * (1.0 / DIL_DH))
    return dx, dy * x * r


def _dil_attn_bwd(slab, stat, wq_row, wk_row, dwq_in, dwk_in, *, group, name):
    dilation = DIL_GROUPS[group][1]
    t = slab.shape[0]
    rows = t // dilation
    nlb = rows // DIL_SPAN
    wide = DIL_HEADS * LANES
    view = slab.reshape(rows, dilation * DIL_SLAB)
    stat_view = stat.reshape(rows, dilation * wide)

    def body(cur_ref, kp_ref, vp_ref, st_ref, wq_ref, wk_ref, dwq_in_ref, dwk_in_ref, d_ref, dwq_ref, dwk_ref,
             dk_carry, dv_carry):
        step = pl.program_id(1)
        has_prev = step < nlb - 1
        first = (pl.program_id(0) == 0) & (step == 0)

        @pl.when(step == 0)
        def _():
            dk_carry[...] = jnp.zeros_like(dk_carry)
            dv_carry[...] = jnp.zeros_like(dv_carry)

        @pl.when(first)
        def _():
            dwq_ref[...] = dwq_in_ref[...]
            dwk_ref[...] = dwk_in_ref[...]

        lane = lax.broadcasted_iota(jnp.int32, (DIL_SPAN, LANES), 1)
        scale = DIL_DH ** -0.5
        q_raw = _stack_heads(cur_ref)
        kc_raw = _stack_heads(cur_ref, first=DIL_HEADS)
        vc = _stack_heads(cur_ref, first=2 * DIL_HEADS)
        kp_raw = _stack_heads(kp_ref)
        vp = _stack_heads(vp_ref)
        st = _stack_heads(st_ref)
        d_o = jnp.where(lane < DIL_DH, st, 0.0)
        lse = jnp.sum(jnp.where(lane == DIL_DH, st, 0.0), axis=-1, keepdims=True)
        delta = jnp.sum(jnp.where(lane == DIL_DH + 1, st, 0.0), axis=-1, keepdims=True)
        qn = _rms64(q_raw, wq_ref[...])
        kc = _rms64(kc_raw, wk_ref[...])
        kp = _rms64(kp_raw, wk_ref[...])
        sp, sc = _band_logits(qn, kp, kc, _alibi_slopes(group) * float(dilation), has_prev)
        pp = jnp.exp(sp - lse)
        pc = jnp.exp(sc - lse)
        dsp = pp * (_bdot_nt(d_o, vp) - delta) * scale
        dsc = pc * (_bdot_nt(d_o, vc) - delta) * scale
        dqn = _bdot(dsp, kp) + _bdot(dsc, kc)
        dkc_n = _bdot_tn(dsc, qn) + _stack_heads(dk_carry)
        dvc = _bdot_tn(pc, d_o) + _stack_heads(dv_carry)
        _store_heads(dk_carry, _bdot_tn(dsp, qn))
        _store_heads(dv_carry, _bdot_tn(pp, d_o))
        dq_raw, dwq_rows = _rms64_bwd(q_raw, wq_ref[...], dqn)
        dk_raw, dwk_rows = _rms64_bwd(kc_raw, wk_ref[...], dkc_n)
        _store_heads(d_ref, dq_raw)
        _store_heads(d_ref, dk_raw, first=DIL_HEADS)
        _store_heads(d_ref, dvc, first=2 * DIL_HEADS)
        dwq_ref[...] += jnp.sum(jnp.sum(dwq_rows, axis=0), axis=0, keepdims=True)
        dwk_ref[...] += jnp.sum(jnp.sum(dwk_rows, axis=0), axis=0, keepdims=True)

    blk_i = lambda i: nlb - 1 - i
    cur = pl.BlockSpec((DIL_SPAN, DIL_SLAB), lambda r, i: (blk_i(i), r))
    prv = lambda part: pl.BlockSpec((DIL_SPAN, wide), lambda r, i: (jnp.maximum(blk_i(i) - 1, 0), 3 * r + part))
    one = pl.BlockSpec((1, LANES), lambda r, i: (0, 0))
    dslab, dwq, dwk = pl.pallas_call(
        body,
        grid=(dilation, nlb),
        in_specs=[cur, prv(1), prv(2), pl.BlockSpec((DIL_SPAN, wide), lambda r, i: (blk_i(i), r)), one, one, one, one],
        out_specs=[cur, one, one],
        out_shape=[jax.ShapeDtypeStruct((rows, dilation * DIL_SLAB), MM_DTYPE), jax.ShapeDtypeStruct((1, LANES), F32),
                   jax.ShapeDtypeStruct((1, LANES), F32)],
        scratch_shapes=[pltpu.VMEM((DIL_SPAN, wide), F32), pltpu.VMEM((DIL_SPAN, wide), F32)],
        compiler_params=_cparams(("arbitrary", "arbitrary")),
        name=name,
    )(view, view, view, stat_view, wq_row, wk_row, dwq_in, dwk_in)
    return dslab.reshape(t, DIL_SLAB), dwq, dwk


def _row(v, width=LANES):
    v = v.astype(F32).reshape(-1)
    return jnp.pad(v, (0, width - v.shape[0])).reshape(1, width)


def _prepare_weights(w):
    return dict(gdn=_prepare_gdn(w), dil=_prepare_dil(w), ffn=_prepare_ffn(w))


def _prepare_gdn(w, layers=range(DEPTH // 2)):
    gdn = {}
    for j in layers:
        wt = w["gdn_w_in"][j]
        gates_t = jnp.pad(wt[GDN_MAIN:], ((0, LANES - 2 * GDN_HEADS), (0, 0)))
        gdn[j] = dict(in_t=wt, gates_t=gates_t, out=w["gdn_w_out"][j], conv=w["gdn_conv_w"][j].astype(F32),
                      alog=_row(w["gdn_a_log"][j]), dt=_row(w["gdn_dt_bias"][j]), nw=_row(w["gdn_norm_w"][j]))
    return gdn


def _prepare_dil(w, layers=range(DEPTH // 2)):
    d = D_MODEL
    dil = {}
    for j in layers:
        wt = w["dil_w_in"][j].reshape(3, len(DIL_GROUPS), DIL_HEADS, DIL_DH, d)
        wg_t = [wt[:, g].reshape(DIL_SLAB // 2, d) for g in range(len(DIL_GROUPS))]
        out_t = jnp.pad(w["dil_w_out"][j].reshape(d, DIL_HEADS, DIL_DH), ((0, 0), (0, 0), (0, LANES - DIL_DH)))
        dil[j] = dict(wg_t=wg_t, out_t=out_t.reshape(d, DIL_HEADS * LANES), wq=_row(w["dil_q_norm"][j]),
                      wk=_row(w["dil_k_norm"][j]))
    return dil


def _prepare_ffn(w, layers=range(DEPTH)):
    return {i: dict(in_t=w["ffn_w_in"][i], out=w["ffn_w_out"][i]) for i in layers}


def _residual_out(a, w, x, next_row, *, name, **kw):
    if next_row is None:
        return _matmul(a, w, add=x, name=name, **kw), None
    return _matmul(a, w, add=x, norm_fwd=next_row, name=name + "_norm", **kw)


def _gdn_layer_fwd(x, nrow, p, hn=None, next_row=None):
    if hn is None:
        hn = _rmsnorm_fwd(x, nrow, name="rmsnorm_fwd")
    pm = _matmul(hn, p["in_t"], trans_b=True, b_rows=(0, GDN_MAIN), name="gdn_proj_main")
    ab = _matmul(hn, p["gates_t"], trans_b=True, name="gdn_proj_gates")
    qkv = _gdn_conv_fwd(pm, p["conv"], name="gdn_conv_fwd")
    gb, bb = _gdn_gates_fwd(ab, p["alog"], p["dt"], name="gdn_gates_fwd")
    prep, rest = _gdn_prep_fwd(qkv, gb, bb, name="gdn_prep_fwd")
    o, states = _gdn_scan_fwd(prep, name="gdn_scan_fwd")
    og = _gdn_outgate_fwd(o, pm, p["nw"], name="gdn_outgate_fwd")
    y, hn_next = _residual_out(og, p["out"], x, next_row, name="gdn_proj_out")
    return y, (x, hn, pm, ab, qkv, gb, bb, prep, rest, states, o, og), hn_next


def _gdn_layer_bwd(dx, dxb, nrow, p, saved):
    x, hn, pm, ab, qkv, gb, bb, prep, rest, states, o, og = saved
    d_og = _matmul(dxb, p["out"], trans_b=True, name="gdn_dgate")
    g_out = _matmul(og, dxb, trans_a=True, out_dtype=MM_DTYPE, name="gdn_gw_out")
    d_o, d_z, d_nw = _gdn_outgate_bwd(o, pm, p["nw"], d_og, name="gdn_outgate_bwd")
    cts = _gdn_scan_bwd(prep, states, d_o, name="gdn_scan_bwd")
    dqkv, dgb, dbb = _gdn_prep_bwd(qkv, gb, bb, rest, cts, name="gdn_prep_bwd")
    d_ab, d_alog, d_dt = _gdn_gates_bwd(ab, p["alog"], p["dt"], dgb, dbb, name="gdn_gates_bwd")
    d_conv, g_conv = _gdn_conv_bwd(pm, p["conv"], dqkv, name="gdn_conv_bwd")
    d_hn = _matmul(d_conv, p["in_t"], b_rows=(0, GDN_QKV), name="gdn_dhn_qkv")
    d_hn = _matmul(d_z, p["in_t"], b_rows=(GDN_QKV, GDN_MAIN - GDN_QKV), add=d_hn, name="gdn_dhn_z")
    dx_new, dxb_new, g_norm = _matmul(d_ab, p["gates_t"], add=d_hn, norm_bwd=(x, nrow, dx), name="gdn_dhn_gates_norm")
    g_in_t = jnp.concatenate([
        _matmul(d_conv, hn, trans_a=True, out_dtype=MM_DTYPE, name="gdn_gw_qkv"),
        _matmul(d_z, hn, trans_a=True, out_dtype=MM_DTYPE, name="gdn_gw_z"),
        _matmul(d_ab, hn, trans_a=True, out_dtype=MM_DTYPE, name="gdn_gw_gates")[:2 * GDN_HEADS],
    ], axis=0)
    grads = dict(w_in=g_in_t, conv=g_conv, a_log=d_alog[0, :GDN_HEADS], dt_bias=d_dt[0, :GDN_HEADS], norm_w=d_nw[0],
                 w_out=g_out, norm=g_norm[0])
    return dx_new, dxb_new, grads


def _dil_layer_fwd(x, nrow, p, hn=None, next_row=None):
    if hn is None:
        hn = _rmsnorm_fwd(x, nrow, name="rmsnorm_fwd")
    slabs = [_matmul(hn, p["wg_t"][g], trans_b=True, spread_out=True, name="dil_proj_in") for g in range(len(DIL_GROUPS))]
    oe = [(_dil_attn_fwd if DIL_GROUPS[g][1] == 1 else _dil_attn_fwd_strided)(
        slabs[g], p["wq"], p["wk"], group=g, name=f"dil_attn_fwd_g{g}") for g in range(len(DIL_GROUPS))]
    y, om = _dil_merge_fwd(oe, name="dil_merge_fwd")
    out, hn_next = _residual_out(y, p["out_t"], x, next_row, trans_b=True, name="dil_proj_out")
    return out, (x, hn, slabs, y, om), hn_next


def _dil_layer_bwd(dx, dxb, nrow, p, saved):
    x, hn, slabs, y, om = saved
    d_y = _matmul(dxb, p["out_t"], name="dil_dmerged")
    g_out_t = _matmul(dxb, y, trans_a=True, out_dtype=MM_DTYPE, name="dil_gw_out")
    g_out_t = g_out_t.reshape(D_MODEL, DIL_HEADS, LANES)[..., :DIL_DH].reshape(D_MODEL, DIL_HEADS * DIL_DH)
    stat = _dil_merge_bwd(d_y, om, name="dil_merge_bwd")
    d_hn = None
    dwq = jnp.zeros((1, LANES), F32)
    dwk = jnp.zeros((1, LANES), F32)
    g_groups = []
    wide = DIL_HEADS * LANES
    for g in range(len(DIL_GROUPS)):
        last = dict(norm_bwd=(x, nrow, dx)) if g == len(DIL_GROUPS) - 1 else {}
        if DIL_GROUPS[g][1] == 1:
            dslab, dwq, dwk = _dil_attn_bwd(slabs[g], stat, p["wq"], p["wk"], dwq, dwk, group=g, name=f"dil_attn_bwd_g{g}")
            d_hn = _matmul(dslab, p["wg_t"][g], packed_a=True, add=d_hn, name="dil_dhn", **last)
            g_w = _matmul(dslab, hn, trans_a=True, packed_a=True, out_dtype=MM_DTYPE, name="dil_gw_in")
        else:
            dparts, dwq, dwk = _dil_attn_bwd_strided(slabs[g], stat, p["wq"], p["wk"], dwq, dwk, group=g,
                                                     name=f"dil_attn_bwd_g{g}")
            d_hn = _matmul(dparts, p["wg_t"][g], a_lead="k", packed_a=True, add=d_hn,
                           name="dil_dhn_parts_norm" if last else "dil_dhn_parts", **last)
            g_w = _matmul(dparts, hn, trans_a=True, a_lead="i", packed_a=True, out_dtype=MM_DTYPE, name="dil_gw_in_parts")
        g_groups.append(g_w.reshape(3, DIL_HEADS, DIL_DH, D_MODEL))
    g_in_t = jnp.stack(g_groups, axis=1).reshape(3 * len(DIL_GROUPS) * DIL_HEADS * DIL_DH, D_MODEL)
    dx_new, dxb_new, g_norm = d_hn
    grads = dict(w_in=g_in_t, q_norm=dwq[0, :DIL_DH], k_norm=dwk[0, :DIL_DH], w_out=g_out_t, norm=g_norm[0])
    return dx_new, dxb_new, grads


def _ffn_layer_fwd(x, nrow, p, hn=None, next_row=None):
    if hn is None:
        hn = _rmsnorm_fwd(x, nrow, name="rmsnorm_fwd")
    gate, up, act = _ffn_in(hn, p["in_t"], name="ffn_proj_in")
    y, hn_next = _residual_out(act, p["out"], x, next_row, name="ffn_proj_out")
    return y, (x, hn, gate, up, act), hn_next


def _ffn_layer_bwd(dx, dxb, nrow, p, saved):
    x, hn, gate, up, act = saved
    g_out = _matmul(act, dxb, trans_a=True, out_dtype=MM_DTYPE, name="ffn_gw_out")
    d_gu = _ffn_dact(dxb, p["out"], gate, up, name="ffn_dact")
    dx_new, dxb_new, g_norm = _matmul(d_gu, p["in_t"], a_lead="k", norm_bwd=(x, nrow, dx), name="ffn_dhn_norm")
    g_in_t = _matmul(d_gu, hn, trans_a=True, a_lead="i", out_dtype=MM_DTYPE, name="ffn_gw_in")
    return dx_new, dxb_new, dict(w_in=g_in_t, w_out=g_out, norm=g_norm[0])


def _mixer_fwd(i, x, mix_row, prepared, hn=None, next_row=None):
    if i % 2 == 0:
        return _gdn_layer_fwd(x, mix_row, prepared["gdn"][i // 2], hn, next_row)
    return _dil_layer_fwd(x, mix_row, prepared["dil"][i // 2], hn, next_row)


def _mixer_bwd(i, dx, dxb, mix_row, prepared, saved, zero=0.0):
    if i % 2 == 0:
        p = prepared["gdn"][i // 2]
        return _gdn_layer_bwd(dx, dxb, mix_row, dict(p, nw=p["nw"] + zero), saved)
    p = prepared["dil"][i // 2]
    return _dil_layer_bwd(dx, dxb, mix_row, dict(p, wq=p["wq"] + zero), saved)


def _local_step(x, target, prepared, norm_mix, norm_ffn):
    saved = []
    hn = None
    for i in range(DEPTH):
        after = norm_mix[i + 1].reshape(1, D_MODEL) if i + 1 < DEPTH else None
        x, s_mix, hn = _mixer_fwd(i, x, norm_mix[i].reshape(1, D_MODEL), prepared, hn, norm_ffn[i].reshape(1, D_MODEL))
        x, s_ffn, hn = _ffn_layer_fwd(x, norm_ffn[i].reshape(1, D_MODEL), prepared["ffn"][i], hn, after)
        saved.append((s_mix, s_ffn))
    dx, dxb, loss = _loss_head(x, target, name="loss_head")
    g_mix, g_ffn = [None] * DEPTH, [None] * DEPTH
    for i in reversed(range(DEPTH)):
        s_mix, s_ffn = saved[i]
        dx, dxb, g_ffn[i] = _ffn_layer_bwd(dx, dxb, norm_ffn[i].reshape(1, D_MODEL), prepared["ffn"][i], s_ffn)
        dx, dxb, g_mix[i] = _mixer_bwd(i, dx, dxb, norm_mix[i].reshape(1, D_MODEL), prepared, s_mix)
    return loss[0, 0], dx, _collect_grads(g_mix, g_ffn)


def _collect_grads(g_mix, g_ffn):
    gdn = [g_mix[i] for i in range(0, DEPTH, 2)]
    dil = [g_mix[i] for i in range(1, DEPTH, 2)]
    if any(g is None for g in g_mix + g_ffn):
        pick = lambda gs, key: [None if g is None else g[key] for g in gs]
        return dict(gdn_w_in=pick(gdn, "w_in"), gdn_w_out=pick(gdn, "w_out"), dil_w_in=pick(dil, "w_in"),
                    dil_w_out=pick(dil, "w_out"), ffn_w_in=pick(g_ffn, "w_in"), ffn_w_out=pick(g_ffn, "w_out"))
    grads = dict(
        norm_mix=jnp.stack([g["norm"] for g in g_mix]),
        norm_ffn=jnp.stack([g["norm"] for g in g_ffn]),
        gdn_w_in=[g["w_in"] for g in gdn],
        gdn_conv_w=jnp.stack([g["conv"] for g in gdn]),
        gdn_a_log=jnp.stack([g["a_log"] for g in gdn]),
        gdn_dt_bias=jnp.stack([g["dt_bias"] for g in gdn]),
        gdn_norm_w=jnp.stack([g["norm_w"] for g in gdn]),
        gdn_w_out=[g["w_out"] for g in gdn],
        dil_w_in=[g["w_in"] for g in dil],
        dil_q_norm=jnp.stack([g["q_norm"] for g in dil]),
        dil_k_norm=jnp.stack([g["k_norm"] for g in dil]),
        dil_w_out=[g["w_out"] for g in dil],
        ffn_w_in=[g["w_in"] for g in g_ffn],
        ffn_w_out=[g["w_out"] for g in g_ffn],
    )
    return grads


MESH_ID = pl.DeviceIdType.MESH
ANY_SPACE = pl.BlockSpec(memory_space=pl.ANY)


def _mesh_position():
    return lax.axis_index("x"), lax.axis_index("y"), lax.axis_index("c")


def _flip(pos, k):
    x, y, c = pos
    return (1 - x if k & 4 else x, 1 - y if k & 2 else y, 1 - c if k & 1 else c)


def _linear(pos):
    return 4 * pos[0] + 2 * pos[1] + pos[2]


def _comm_scratch():
    return [pltpu.SemaphoreType.DMA((N_DEV - 1,)), pltpu.SemaphoreType.DMA((N_DEV - 1,)), pltpu.SemaphoreType.DMA(())]


def _all_gather(shard, *, name):
    def body(x_ref, out_ref, send_sems, recv_sems, local_sem):
        me = _mesh_position()
        mine = out_ref.at[_linear(me)]
        local = pltpu.make_async_copy(x_ref, mine, local_sem)
        local.start()
        copies = []
        for k in range(1, N_DEV):
            cp = pltpu.make_async_remote_copy(src_ref=x_ref, dst_ref=mine, send_sem=send_sems.at[k - 1],
                                              recv_sem=recv_sems.at[k - 1], device_id=_flip(me, k), device_id_type=MESH_ID)
            cp.start()
            copies.append(cp)
        for cp in copies:
            cp.wait()
        local.wait()

    return pl.pallas_call(
        body,
        out_shape=jax.ShapeDtypeStruct((N_DEV,) + shard.shape, shard.dtype),
        in_specs=[ANY_SPACE],
        out_specs=ANY_SPACE,
        scratch_shapes=_comm_scratch(),
        name=name,
    )(shard)


def _exchange(parts, *, name):
    def body(p_ref, out_ref, send_sems, recv_sems, local_sem):
        me = _mesh_position()
        mine = out_ref.at[_linear(me)]
        local = pltpu.make_async_copy(p_ref.at[_linear(me)], mine, local_sem)
        local.start()
        copies = []
        for k in range(1, N_DEV):
            peer = _flip(me, k)
            cp = pltpu.make_async_remote_copy(src_ref=p_ref.at[_linear(peer)], dst_ref=mine, send_sem=send_sems.at[k - 1],
                                              recv_sem=recv_sems.at[k - 1], device_id=peer, device_id_type=MESH_ID)
            cp.start()
            copies.append(cp)
        for cp in copies:
            cp.wait()
        local.wait()

    return pl.pallas_call(
        body,
        out_shape=jax.ShapeDtypeStruct(parts.shape, parts.dtype),
        in_specs=[ANY_SPACE],
        out_specs=ANY_SPACE,
        scratch_shapes=_comm_scratch(),
        name=name,
    )(parts)


HBM_SPACE = pl.BlockSpec(memory_space=pltpu.HBM)
SEM_SPACE = pl.BlockSpec(memory_space=pltpu.SEMAPHORE)
DATAFLOW = pltpu.SideEffectType.DATAFLOW_SIDE_EFFECTING


def _split_copies(src_ref, land_ref, send_sems, recv_sems, per_peer):
    me = _mesh_position()
    mine = land_ref.at[_linear(me)]
    copies = []
    for k in range(1, N_DEV):
        peer = _flip(me, k)
        src = src_ref.at[_linear(peer)] if per_peer else src_ref
        copies.append(pltpu.make_async_remote_copy(src_ref=src, dst_ref=mine, send_sem=send_sems.at[k - 1],
                                                   recv_sem=recv_sems.at[k - 1], device_id=peer, device_id_type=MESH_ID))
    return copies


def _travel_start(src, after, *, per_peer, name):
    me = _linear(_mesh_position())
    own = src[me] if per_peer else src
    shape = own.shape
    landing = lax.dynamic_update_slice(lax.empty((N_DEV,) + shape, src.dtype), own[None], (me, 0, 0))

    def body(src_ref, land_ref, after_ref, send_sems, recv_sems, src_thru, land_thru, token):
        for cp in _split_copies(src_ref, land_ref, send_sems, recv_sems, per_peer):
            cp.start()
        token[...] = jnp.zeros_like(token)

    return pl.pallas_call(
        body,
        name=name,
        out_shape=(pltpu.SemaphoreType.DMA((N_DEV - 1,)), pltpu.SemaphoreType.DMA((N_DEV - 1,)),
                   pltpu.HBM(src.shape, src.dtype), pltpu.HBM(landing.shape, landing.dtype),
                   jax.ShapeDtypeStruct((8, LANES), F32)),
        in_specs=(HBM_SPACE, HBM_SPACE, ANY_SPACE),
        out_specs=(SEM_SPACE, SEM_SPACE, HBM_SPACE, HBM_SPACE, pl.BlockSpec(memory_space=pltpu.VMEM)),
        input_output_aliases={0: 2, 1: 3},
        compiler_params=pltpu.CompilerParams(has_side_effects=DATAFLOW),
    )(pltpu.with_memory_space_constraint(src, pltpu.HBM), pltpu.with_memory_space_constraint(landing, pltpu.HBM), after)


def _travel_wait(started, after, *, per_peer, name):
    send_sems, recv_sems, src_thru, land_thru, _ = started

    def body(src_ref, land_ref, send_sems, recv_sems, after_ref, src_dead, got_ref):
        for cp in _split_copies(src_ref, land_ref, send_sems, recv_sems, per_peer):
            cp.wait_send()
            cp.wait_recv()

    return pl.pallas_call(
        body,
        name=name,
        out_shape=(pltpu.HBM(src_thru.shape, src_thru.dtype), pltpu.HBM(land_thru.shape, land_thru.dtype)),
        in_specs=(HBM_SPACE, HBM_SPACE, SEM_SPACE, SEM_SPACE, ANY_SPACE),
        out_specs=(HBM_SPACE, HBM_SPACE),
        input_output_aliases={0: 0, 1: 1},
        compiler_params=pltpu.CompilerParams(has_side_effects=DATAFLOW),
    )(src_thru, land_thru, send_sems, recv_sems, after)[1]


def _adamw(parts, w, m, v, *, name):
    rows, n = w.shape
    tb = _pick(rows, (PACK_ROW_ALIGN, 16))
    c1 = 1.0 - ADAM_B1 ** ADAM_STEP
    c2 = 1.0 - ADAM_B2 ** ADAM_STEP

    def body(p_ref, w_ref, m_ref, v_ref, g_ref, d_ref, nm_ref, nv_ref):
        g = p_ref[0].astype(F32)
        for s in range(1, N_DEV):
            g = g + p_ref[s].astype(F32)
        m_new = ADAM_B1 * m_ref[...] + (1.0 - ADAM_B1) * g
        v_new = ADAM_B2 * v_ref[...] + (1.0 - ADAM_B2) * (g * g)
        m_hat = m_new / c1
        v_hat = v_new / c2
        g_ref[...] = g
        nm_ref[...] = m_new
        nv_ref[...] = v_new
        d_ref[...] = -ADAM_LR * (m_hat / (jnp.sqrt(v_hat) + ADAM_EPS) + ADAM_WD * w_ref[...])

    blk = pl.BlockSpec((tb, n), lambda i: (i, 0))
    return pl.pallas_call(
        body,
        grid=(rows // tb,),
        in_specs=[pl.BlockSpec((N_DEV, tb, n), lambda i: (0, i, 0)), blk, blk, blk],
        out_specs=[blk] * 4,
        out_shape=[jax.ShapeDtypeStruct((rows, n), F32)] * 4,
        compiler_params=_cparams(("parallel",)),
        name=name,
    )(parts, w, m, v)


PACK_WIDTH = 1024
SHARDED = {
    "gdn_w_in": ((2, D_MODEL, GDN_IN_WIDTH), 2),
    "gdn_conv_w": ((2, GDN_CONV, GDN_QKV), 2),
    "gdn_w_out": ((2, GDN_HEADS * GDN_DV, D_MODEL), 1),
    "dil_w_in": ((2, D_MODEL, 3 * len(DIL_GROUPS) * DIL_HEADS * DIL_DH), 2),
    "dil_w_out": ((2, DIL_HEADS * DIL_DH, D_MODEL), 2),
    "ffn_w_in": ((DEPTH, D_MODEL, 2 * FFN_HIDDEN), 2),
    "ffn_w_out": ((DEPTH, FFN_HIDDEN, D_MODEL), 1),
}
REPLICATED = {"norm_mix": (DEPTH, D_MODEL), "norm_ffn": (DEPTH, D_MODEL), "gdn_a_log": (2, GDN_HEADS),
              "gdn_dt_bias": (2, GDN_HEADS), "gdn_norm_w": (2, GDN_DV), "dil_q_norm": (2, DIL_DH), "dil_k_norm": (2, DIL_DH)}
WEIGHT_ORDER = ("norm_mix", "norm_ffn", "gdn_w_in", "gdn_conv_w", "gdn_a_log", "gdn_dt_bias", "gdn_norm_w", "gdn_w_out",
                "dil_w_in", "dil_q_norm", "dil_k_norm", "dil_w_out", "ffn_w_in", "ffn_w_out")
PACK_ROW_ALIGN = 128
PIECE_ALIGN = 16
SMALL_ROWS = 16


def _shard_shape(name):
    shape, axis = SHARDED[name]
    return tuple(s // N_DEV if i == axis else s for i, s in enumerate(shape))


def _shard_rows(name):
    return math.prod(_shard_shape(name)) // PACK_WIDTH


def _split_shards(full, name):
    shape, axis = SHARDED[name]
    split = full.reshape(shape[:axis] + (N_DEV, shape[axis] // N_DEV) + shape[axis + 1:])
    return jnp.moveaxis(split, axis, 0)


def _join_shards(stacked, name):
    shape, axis = SHARDED[name]
    return jnp.moveaxis(stacked, 0, axis).reshape(shape)


COLUMN_SHARDED = ("gdn_w_in", "dil_w_in", "dil_w_out", "ffn_w_in")


def _to_rows(shard, name):
    if name in COLUMN_SHARDED:
        shard = jnp.swapaxes(shard, 1, 2)
    return shard.reshape(-1, PACK_WIDTH)


def _layer_columns(name):
    _, r, c = _shard_shape(name)
    return r if name in COLUMN_SHARDED else c


def _piece_rows(piece, halves=1):
    name, layer = piece
    rows = _shard_rows(name) * halves
    return rows if layer is None else rows // SHARDED[name][0][0]


def _aligned(rows, to=PIECE_ALIGN):
    return -(-rows // to) * to


def _pack_pieces(arrays, total_align=PIECE_ALIGN):
    padded, total = [], 0
    for a in arrays:
        rows = a.shape[-2]
        extra = _aligned(rows) - rows
        if extra:
            a = jnp.pad(a, [(0, 0)] * (a.ndim - 2) + [(0, extra), (0, 0)])
        padded.append(a)
        total += rows + extra
    tail = _aligned(total, total_align) - total
    if tail:
        padded.append(jnp.zeros(padded[0].shape[:-2] + (tail, PACK_WIDTH), padded[0].dtype))
    return jnp.concatenate(padded, axis=-2)


def _piece_offsets(pieces, halves=None):
    out, at = [], 0
    for p in pieces:
        rows = _piece_rows(p, (halves or {}).get(p[0], 1))
        out.append((p, at, rows))
        at += _aligned(rows)
    return out


def _shard_piece_rows(src, piece):
    name, layer = piece
    part = src[name] if layer is None else src[name][layer:layer + 1]
    return _to_rows(part.astype(F32), name)


def _piece_from_rows(rows, piece):
    name, layer = piece
    layers, r, c = _shard_shape(name)
    n_l = layers if layer is None else 1
    if name in COLUMN_SHARDED:
        return jnp.swapaxes(rows.reshape(n_l, c, r), 1, 2)
    return rows.reshape(n_l, r, c)


SMALL_TAIL = tuple(n for n in REPLICATED if n not in ("norm_mix", "norm_ffn"))


def _pack_small(vals):
    tail, at = jnp.zeros((PACK_WIDTH,), F32), 0
    for n in SMALL_TAIL:
        vec = vals[n].astype(F32).reshape(-1)
        tail = tail + jnp.pad(vec, (at, PACK_WIDTH - at - vec.shape[0]))
        at += vec.shape[0]
    buf = jnp.pad(vals["norm_mix"].astype(F32), ((0, SMALL_ROWS - DEPTH), (0, 0)))
    buf = buf + jnp.pad(vals["norm_ffn"].astype(F32), ((8, SMALL_ROWS - 8 - DEPTH), (0, 0)))
    return buf + jnp.pad(tail.reshape(1, PACK_WIDTH), ((SMALL_ROWS - 1, 0), (0, 0)))


def _unpack_small(buf):
    out = {"norm_mix": buf[0:DEPTH], "norm_ffn": buf[8:8 + DEPTH]}
    at = 0
    for n in SMALL_TAIL:
        size = math.prod(REPLICATED[n])
        out[n] = buf[SMALL_ROWS - 1, at:at + size].reshape(REPLICATED[n])
        at += size
    return out


GATHER_FIRST = (("gdn_w_in", 0), ("gdn_conv_w", None), ("gdn_w_out", 0))
GATHER_NEXT = (("ffn_w_in", 0), ("ffn_w_out", 0), ("dil_w_in", 0), ("dil_w_out", 0))
GATHER_LAST = (("ffn_w_in", 1), ("ffn_w_out", 1), ("gdn_w_in", 1), ("gdn_w_out", 1), ("ffn_w_in", 2), ("ffn_w_out", 2),
               ("dil_w_in", 1), ("dil_w_out", 1), ("ffn_w_in", 3), ("ffn_w_out", 3))
EXCHANGE_GROUPS = (
    (("ffn_w_in", 3), ("ffn_w_out", 3), ("dil_w_in", 1), ("dil_w_out", 1),
     ("ffn_w_in", 2), ("ffn_w_out", 2), ("gdn_w_in", 1), ("gdn_w_out", 1)),
    (("ffn_w_in", 1), ("ffn_w_out", 1), ("dil_w_in", 0), ("dil_w_out", 0)),
    (("ffn_w_in", 0), ("ffn_w_out", 0)),
    (("gdn_w_in", 0), ("gdn_w_out", 0), ("gdn_conv_w", None)),
)
EXCHANGE_AFTER = {("mix", 2): 0, ("mix", 1): 1, ("ffn", 0): 2}


def _gather_operand(w, pieces):
    arrays = []
    for n, layer in pieces:
        if layer is None:
            arrays.append(lax.bitcast_convert_type(w[n], BF16).reshape(-1, PACK_WIDTH))
        else:
            arrays.append(_to_rows(w[n][layer:layer + 1].astype(BF16), n))
    return _pack_pieces(arrays)


def _gathered_weights(gathered, pieces, full):
    for (n, layer), at, rows in _piece_offsets(pieces, halves={"gdn_conv_w": 2}):
        block = gathered[:, at:at + rows]
        if layer is None:
            block = lax.bitcast_convert_type(block.reshape((N_DEV,) + _shard_shape(n) + (2,)), F32)
            full[n] = _join_shards(block, n)
        else:
            full.setdefault(n, {})[layer] = block.reshape(-1, _layer_columns(n))
    return full


def _exchange_operand(grads, pieces):
    arrays = []
    for n, layer in pieces:
        if layer is None:
            arrays.append(_split_shards(grads[n], n).astype(BF16).reshape(N_DEV, -1, PACK_WIDTH))
        else:
            arrays.append(grads[n][layer].astype(BF16).reshape(N_DEV, -1, PACK_WIDTH))
    return _pack_pieces(arrays, total_align=PACK_ROW_ALIGN)


def _update_group(received, pieces, w, m, v, *, name):
    packed = [_pack_pieces([_shard_piece_rows(src, p) for p in pieces], total_align=PACK_ROW_ALIGN) for src in (w, m, v)]
    outs = _adamw(received, *packed, name=name)
    return {p: tuple(_piece_from_rows(o[at:at + rows], p) for o in outs) for p, at, rows in _piece_offsets(pieces)}


def kernel(x, norm_mix, norm_ffn, gdn_w_in, gdn_conv_w, gdn_a_log, gdn_dt_bias, gdn_norm_w, gdn_w_out, dil_w_in, dil_q_norm, dil_k_norm, dil_w_out, ffn_w_in, ffn_w_out, loss_target, m_norm_mix, m_norm_ffn, m_gdn_w_in, m_gdn_conv_w, m_gdn_a_log, m_gdn_dt_bias, m_gdn_norm_w, m_gdn_w_out, m_dil_w_in, m_dil_q_norm, m_dil_k_norm, m_dil_w_out, m_ffn_w_in, m_ffn_w_out, v_norm_mix, v_norm_ffn, v_gdn_w_in, v_gdn_conv_w, v_gdn_a_log, v_gdn_dt_bias, v_gdn_norm_w, v_gdn_w_out, v_dil_w_in, v_dil_q_norm, v_dil_k_norm, v_dil_w_out, v_ffn_w_in, v_ffn_w_out):
    w = dict(norm_mix=norm_mix, norm_ffn=norm_ffn, gdn_w_in=gdn_w_in, gdn_conv_w=gdn_conv_w, gdn_a_log=gdn_a_log,
             gdn_dt_bias=gdn_dt_bias, gdn_norm_w=gdn_norm_w, gdn_w_out=gdn_w_out, dil_w_in=dil_w_in, dil_q_norm=dil_q_norm,
             dil_k_norm=dil_k_norm, dil_w_out=dil_w_out, ffn_w_in=ffn_w_in, ffn_w_out=ffn_w_out)
    m = dict(norm_mix=m_norm_mix, norm_ffn=m_norm_ffn, gdn_w_in=m_gdn_w_in, gdn_conv_w=m_gdn_conv_w, gdn_a_log=m_gdn_a_log,
             gdn_dt_bias=m_gdn_dt_bias, gdn_norm_w=m_gdn_norm_w, gdn_w_out=m_gdn_w_out, dil_w_in=m_dil_w_in,
             dil_q_norm=m_dil_q_norm, dil_k_norm=m_dil_k_norm, dil_w_out=m_dil_w_out, ffn_w_in=m_ffn_w_in, ffn_w_out=m_ffn_w_out)
    v = dict(norm_mix=v_norm_mix, norm_ffn=v_norm_ffn, gdn_w_in=v_gdn_w_in, gdn_conv_w=v_gdn_conv_w, gdn_a_log=v_gdn_a_log,
             gdn_dt_bias=v_gdn_dt_bias, gdn_norm_w=v_gdn_norm_w, gdn_w_out=v_gdn_w_out, dil_w_in=v_dil_w_in,
             dil_q_norm=v_dil_q_norm, dil_k_norm=v_dil_k_norm, dil_w_out=v_dil_w_out, ffn_w_in=v_ffn_w_in, ffn_w_out=v_ffn_w_out)
    def row(src, i):
        return src[i].reshape(1, D_MODEL)

    first = _all_gather(_gather_operand(w, GATHER_FIRST), name="weight_all_gather_first")
    next_started = _travel_start(_gather_operand(w, GATHER_NEXT), first, per_peer=False, name="weight_gather_start_next")
    last_started = _travel_start(_gather_operand(w, GATHER_LAST), next_started[4], per_peer=False,
                                 name="weight_gather_start_last")
    full = _gathered_weights(first, GATHER_FIRST, {n: w[n] for n in REPLICATED})
    prepared = dict(gdn=_prepare_gdn(full, layers=(0,)))
    h = x[0]
    saved = [None] * DEPTH
    h, s_mix, hn = _mixer_fwd(0, h, row(norm_mix, 0) + last_started[4][0, 0], prepared, None, row(norm_ffn, 0))
    got = _travel_wait(next_started, h, per_peer=False, name="weight_gather_wait_next")
    full = _gathered_weights(got, GATHER_NEXT, full)
    prepared.update(dil=_prepare_dil(full, layers=(0,)), ffn=_prepare_ffn(full, layers=(0,)))
    for i in range(DEPTH):
        if i > 0:
            h, s_mix, hn = _mixer_fwd(i, h, row(norm_mix, i), prepared, hn, row(norm_ffn, i))
        if i == 1:
            got = _travel_wait(last_started, h, per_peer=False, name="weight_gather_wait_last")
            full = _gathered_weights(got, GATHER_LAST, full)
            prepared["gdn"].update(_prepare_gdn(full, layers=(1,)))
            prepared["dil"].update(_prepare_dil(full, layers=(1,)))
            prepared["ffn"].update(_prepare_ffn(full, layers=(1, 2, 3)))
        h, s_ffn, hn = _ffn_layer_fwd(h, row(norm_ffn, i), prepared["ffn"][i], hn,
                                      row(norm_mix, i + 1) if i + 1 < DEPTH else None)
        saved[i] = (s_mix, s_ffn)
    dx, dxb, loss = _loss_head(h, loss_target[0], name="loss_head")

    g_mix, g_ffn = [None] * DEPTH, [None] * DEPTH
    started = {}

    def travel(group):
        operand = _exchange_operand(_collect_grads(g_mix, g_ffn), EXCHANGE_GROUPS[group])
        started[group] = _travel_start(operand, dx, per_peer=True, name=f"grad_exchange_start_{group}")
        return started[group][4][0, 0]

    zero = 0.0
    for i in reversed(range(DEPTH)):
        s_mix, s_ffn = saved[i]
        dx, dxb, g_ffn[i] = _ffn_layer_bwd(dx, dxb, row(norm_ffn, i) + zero, prepared["ffn"][i], s_ffn)
        zero = travel(EXCHANGE_AFTER[("ffn", i)]) if ("ffn", i) in EXCHANGE_AFTER else 0.0
        dx, dxb, g_mix[i] = _mixer_bwd(i, dx, dxb, row(norm_mix, i), prepared, s_mix, zero)
        zero = travel(EXCHANGE_AFTER[("mix", i)]) if ("mix", i) in EXCHANGE_AFTER else 0.0
    grads = _collect_grads(g_mix, g_ffn)
    received = [_travel_wait(started[g], dx, per_peer=True, name=f"grad_exchange_wait_{g}") for g in sorted(started)]
    received.append(_exchange(_exchange_operand(grads, EXCHANGE_GROUPS[-1]), name="grad_exchange_last"))
    updated = {}
    for g, pieces in enumerate(EXCHANGE_GROUPS):
        updated.update(_update_group(received[g], pieces, w, m, v, name=f"adamw_sharded_{g}"))

    small_parts = _all_gather(_pack_small(grads), name="small_grad_all_gather")
    outs_small = [_unpack_small(o) for o in
                  _adamw(small_parts, _pack_small(w), _pack_small(m), _pack_small(v), name="adamw_replicated")]

    total_loss = lax.psum(loss[0, 0], ("x", "y", "c"))
    result = [total_loss, dx[None]]
    for k in range(4):
        for n in WEIGHT_ORDER:
            if n not in SHARDED:
                result.append(outs_small[k][n])
            elif (n, None) in updated:
                result.append(updated[(n, None)][k])
            else:
                result.append(jnp.concatenate([updated[(n, l)][k] for l in range(SHARDED[n][0][0])], axis=0))
    return tuple(result)
```

```python
import functools
import math

import jax
import jax.numpy as jnp
from jax import lax
from jax.experimental import pallas as pl
from jax.experimental.pallas import tpu as pltpu

F32 = jnp.float32
BF16 = jnp.bfloat16
MM_DTYPE = BF16

N_DEV = 8
D_MODEL = 1024
DEPTH = 4
RMS_EPS = 1e-6
L2_EPS = 1e-6

LANES = 128

GDN_HEADS = 8
GDN_DK = 128
GDN_DV = 128
GDN_CONV = 4
GDN_CHUNK = 128
GDN_QKV = 3 * GDN_HEADS * GDN_DK
GDN_MAIN = GDN_QKV + GDN_HEADS * GDN_DV
GDN_IN_WIDTH = GDN_MAIN + 2 * GDN_HEADS

DIL_GROUPS = ((128, 1), (512, 4), (2048, 16))
DIL_HEADS = 8
DIL_DH = 64
DIL_SPAN = 128
DIL_SLAB = 3 * DIL_HEADS * LANES
ALIBI_MAX_BIAS = 8.0

FFN_HIDDEN = 2816

ADAM_LR = 0.001
ADAM_B1 = 0.9
ADAM_B2 = 0.999
ADAM_EPS = 1e-08
ADAM_WD = 0.01
ADAM_STEP = 10

VMEM_LIMIT = 56 * 1024 * 1024
ROW_TILE = 512
MATMUL_VMEM_BUDGET = 40 * 1024 * 1024
NEG = -1e30


def _cparams(sem):
    return pltpu.CompilerParams(dimension_semantics=sem, vmem_limit_bytes=VMEM_LIMIT)


def _single_pass(a, b, a_dim, b_dim):
    lead = a.ndim - 2
    batch = ((0,), (0,)) if lead else ((), ())
    return lax.dot_general(a.astype(BF16), b.astype(BF16), (((lead + a_dim,), (lead + b_dim,)), batch),
                           preferred_element_type=F32)


def _bdot(a, b):
    return _single_pass(a, b, 1, 0)


def _bdot_nt(a, b):
    return _single_pass(a, b, 1, 1)


def _bdot_tn(a, b):
    return _single_pass(a, b, 0, 0)


def _pick(n, candidates):
    for c in candidates:
        if n % c == 0:
            return c
    raise ValueError(f"no tile for {n}")


HALF = LANES // 2


def _pack_head_pairs(x):
    x = x.astype(F32)
    tiles = [x[:, (2 * i) * LANES:(2 * i + 1) * LANES] + pltpu.roll(x[:, (2 * i + 1) * LANES:(2 * i + 2) * LANES], HALF, 1)
             for i in range(x.shape[1] // (2 * LANES))]
    return tiles[0] if len(tiles) == 1 else jnp.concatenate(tiles, axis=1)


def _spread_head_pairs(y):
    low = lax.broadcasted_iota(jnp.int32, (y.shape[0], LANES), 1) < HALF
    tiles = []
    for i in range(y.shape[1] // LANES):
        pair = y[:, i * LANES:(i + 1) * LANES]
        tiles += [jnp.where(low, pair, 0.0), jnp.where(low, pltpu.roll(pair, HALF, 1), 0.0)]
    return jnp.concatenate(tiles, axis=1)


def _matmul(a, b, *, name, trans_a=False, trans_b=False, b_rows=None, a_lead=None, add=None, out_dtype=F32,
            packed_a=False, spread_out=False, norm_bwd=None, norm_fwd=None):
    if trans_a:
        k_dim, m_dim = a.shape[-2:]
        m_dim = m_dim // 2 if packed_a else m_dim
    else:
        m_dim, k_dim = a.shape[-2:]
        k_dim = k_dim // 2 if packed_a else k_dim
    slab_m, slab_k = m_dim, k_dim
    if a_lead == "k":
        assert not trans_a
        k_dim *= a.shape[0]
    elif a_lead == "i":
        assert trans_a
        m_dim *= a.shape[0]
    b_start, b_size = b_rows if b_rows is not None else (0, b.shape[0])
    if trans_b:
        n_dim, k2 = b_size, b.shape[1]
    else:
        k2, n_dim = b_size, b.shape[1]
    assert k_dim == k2, (a.shape, b.shape, b_rows)
    tn = _pick(n_dim, (1024, 512, 256, 128))
    tm = min(slab_m, 2048, max(512, (1024 * 1024) // tn))
    tm = _pick(slab_m, (tm, 1408, 1024, 512, 256, 128))
    out_bytes = jnp.dtype(out_dtype).itemsize * (2 if spread_out else 1)
    if norm_bwd is not None:
        out_bytes = 4 + 4 + 4 + 2
        tm = min(tm, 512)
    if norm_fwd is not None:
        out_bytes += 2

    def deepest(rows):
        fixed = rows * tn * (2 * out_bytes + 4 + (8 if add is not None else 0))
        fits = lambda c: fixed + 2 * 2 * c * ((2 if packed_a else 1) * rows + tn) <= MATMUL_VMEM_BUDGET
        return _pick(slab_k, tuple(c for c in (3072, 2816, 2048, 1536, 1408, 1024, 512, 256) if fits(c)) + (128,))

    tk = deepest(tm)
    if tm % 1024 == 0 and deepest(tm // 2) > tk:
        tm, tk = tm // 2, deepest(tm // 2)
    nk = k_dim // tk
    has_add = add is not None
    dn = (((0 if trans_a else 1,), (1 if trans_b else 0,)), ((), ()))
    b_tile = tn if trans_b else tk
    assert b_start % b_tile == 0, (b_rows, b_tile)
    b_off = b_start // b_tile

    has_norm = norm_bwd is not None
    also_norm = norm_fwd is not None
    if has_norm or also_norm:
        assert n_dim == tn and not spread_out and not (has_norm and also_norm)

    def body(*refs):
        refs = list(refs)
        a_ref, b_ref = refs[:2]
        add_ref = refs[2] if has_add else None
        rest = refs[2 + has_add:]
        if has_norm:
            x_ref, w_ref, skip_ref, dx_ref, dxb_ref, dw_ref, acc_ref = rest
        elif also_norm:
            w_ref, o_ref, hn_ref, acc_ref = rest
        else:
            o_ref, acc_ref = rest
        a_blk = _pack_head_pairs(a_ref[...]).astype(a_ref.dtype) if packed_a else a_ref[...]
        part = lax.dot_general(a_blk, b_ref[...], dn, preferred_element_type=F32)
        first_rows = pl.program_id(0) == 0

        def finish(total):
            if has_add:
                total = total + add_ref[...]
            if has_norm:
                xf = x_ref[...]
                r = lax.rsqrt(jnp.mean(xf * xf, axis=-1, keepdims=True) + RMS_EPS)
                gw = total * w_ref[...]
                dx = r * gw - xf * (r * r * r * jnp.mean(gw * xf, axis=-1, keepdims=True)) + skip_ref[...]
                dx_ref[...] = dx
                dxb_ref[...] = dx.astype(dxb_ref.dtype)
                rows = jnp.sum(total * xf * r, axis=0, keepdims=True)

                @pl.when(first_rows)
                def _():
                    dw_ref[...] = rows

                @pl.when(jnp.logical_not(first_rows))
                def _():
                    dw_ref[...] += rows
                return
            if spread_out:
                total = _spread_head_pairs(total)
            o_ref[...] = total.astype(out_dtype)
            if also_norm:
                r = lax.rsqrt(jnp.mean(total * total, axis=-1, keepdims=True) + RMS_EPS)
                hn_ref[...] = (total * r * w_ref[...]).astype(hn_ref.dtype)

        if nk == 1:
            finish(part)
        else:
            k = pl.program_id(2)

            @pl.when(k == 0)
            def _():
                acc_ref[...] = part

            @pl.when(k > 0)
            def _():
                acc_ref[...] += part

            @pl.when(k == nk - 1)
            def _():
                finish(acc_ref[...])

    wide = 2 if packed_a else 1
    a_tile = (tk, wide * tm) if trans_a else (tm, wide * tk)
    a_at = (lambda i, j, k: (k, i)) if trans_a else (lambda i, j, k: (i, k))
    if a_lead is None:
        a_spec = pl.BlockSpec(a_tile, a_at)
    elif a_lead == "k":
        per = slab_k // tk
        a_spec = pl.BlockSpec((None,) + a_tile, lambda i, j, k: (k // per, i, k % per))
    elif a_lead == "i":
        per = slab_m // tm
        a_spec = pl.BlockSpec((None,) + a_tile, lambda i, j, k: (i // per, k, i % per))
    else:
        a_spec = pl.BlockSpec((None,) + a_tile, lambda i, j, k: (a_lead,) + a_at(i, j, k))
    if trans_b:
        b_spec = pl.BlockSpec((tn, tk), lambda i, j, k: (j + b_off, k))
    else:
        b_spec = pl.BlockSpec((tk, tn), lambda i, j, k: (k + b_off, j))
    in_specs = [a_spec, b_spec]
    args = [a, b]
    tile = pl.BlockSpec((tm, tn), lambda i, j, k: (i, j))
    if has_add:
        in_specs.append(tile)
        args.append(add)
    scratch = [pltpu.VMEM((tm, tn) if nk > 1 else (8, LANES), F32)]
    if has_norm:
        x, w_row, dskip = norm_bwd
        one = pl.BlockSpec((1, tn), lambda i, j, k: (0, 0))
        return pl.pallas_call(
            body,
            grid=(m_dim // tm, 1, nk),
            in_specs=in_specs + [tile, one, tile],
            out_specs=[tile, tile, one],
            out_shape=[jax.ShapeDtypeStruct((m_dim, n_dim), F32), jax.ShapeDtypeStruct((m_dim, n_dim), MM_DTYPE),
                       jax.ShapeDtypeStruct((1, n_dim), F32)],
            scratch_shapes=scratch,
            compiler_params=_cparams(("arbitrary", "arbitrary", "arbitrary")),
            name=name,
        )(*args, x, w_row, dskip)
    if also_norm:
        return pl.pallas_call(
            body,
            grid=(m_dim // tm, 1, nk),
            in_specs=in_specs + [pl.BlockSpec((1, tn), lambda i, j, k: (0, 0))],
            out_specs=[tile, tile],
            out_shape=[jax.ShapeDtypeStruct((m_dim, n_dim), out_dtype), jax.ShapeDtypeStruct((m_dim, n_dim), MM_DTYPE)],
            scratch_shapes=scratch,
            compiler_params=_cparams(("parallel", "parallel", "arbitrary")),
            name=name,
        )(*args, norm_fwd)
    return pl.pallas_call(
        body,
        grid=(m_dim // tm, n_dim // tn, nk),
        in_specs=in_specs,
        out_specs=pl.BlockSpec((tm, (2 if spread_out else 1) * tn), lambda i, j, k: (i, j)),
        out_shape=jax.ShapeDtypeStruct((m_dim, (2 if spread_out else 1) * n_dim), out_dtype),
        scratch_shapes=scratch,
        compiler_params=_cparams(("parallel", "parallel", "arbitrary")),
        name=name,
    )(*args)


def _rmsnorm_fwd(x, w_row, *, name):
    t, d = x.shape
    tb = min(t, 1024)

    def body(x_ref, w_ref, o_ref):
        xf = x_ref[...]
        r = lax.rsqrt(jnp.mean(xf * xf, axis=-1, keepdims=True) + RMS_EPS)
        o_ref[...] = (xf * r * w_ref[...]).astype(o_ref.dtype)

    return pl.pallas_call(
        body,
        grid=(t // tb,),
        in_specs=[pl.BlockSpec((tb, d), lambda i: (i, 0)), pl.BlockSpec((1, d), lambda i: (0, 0))],
        out_specs=pl.BlockSpec((tb, d), lambda i: (i, 0)),
        out_shape=jax.ShapeDtypeStruct((t, d), MM_DTYPE),
        compiler_params=_cparams(("parallel",)),
        name=name,
    )(x, w_row)


def _silu(z):
    return z / (1.0 + jnp.exp(-z))


FFN_TM, FFN_TN = 512, 1408


def _ffn_in(hn, in_t, *, name):
    t, d = hn.shape
    h = FFN_HIDDEN
    tm, tn = min(t, FFN_TM), FFN_TN
    nj = h // tn
    dn = (((1,), (1,)), ((), ()))

    def body(a_ref, bg_ref, bu_ref, g_ref, u_ref, act_ref):
        a = a_ref[...]
        g = lax.dot_general(a, bg_ref[...], dn, preferred_element_type=F32)
        u = lax.dot_general(a, bu_ref[...], dn, preferred_element_type=F32)
        g_ref[...] = g.astype(g_ref.dtype)
        u_ref[...] = u.astype(u_ref.dtype)
        act_ref[...] = (_silu(g) * u).astype(act_ref.dtype)

    out = pl.BlockSpec((tm, tn), lambda j, i: (i, j))
    return pl.pallas_call(
        body,
        grid=(nj, t // tm),
        in_specs=[pl.BlockSpec((tm, d), lambda j, i: (i, 0)), pl.BlockSpec((tn, d), lambda j, i: (j, 0)),
                  pl.BlockSpec((tn, d), lambda j, i: (j + nj, 0))],
        out_specs=[out, out, out],
        out_shape=[jax.ShapeDtypeStruct((t, h), MM_DTYPE)] * 3,
        compiler_params=_cparams(("parallel", "parallel")),
        name=name,
    )(hn, in_t, in_t)


def _ffn_dact(dy, out_w, g, u, *, name):
    t, d = dy.shape
    h = FFN_HIDDEN
    tm, tn = min(t, FFN_TM), FFN_TN

    def body(a_ref, b_ref, g_ref, u_ref, d_ref):
        da = lax.dot_general(a_ref[...], b_ref[...], (((1,), (1,)), ((), ())), preferred_element_type=F32)
        gate = g_ref[...].astype(F32)
        sig = 1.0 / (1.0 + jnp.exp(-gate))
        sg = gate * sig
        d_ref[0] = (da * u_ref[...].astype(F32) * (sig + sg * (1.0 - sig))).astype(d_ref.dtype)
        d_ref[1] = (da * sg).astype(d_ref.dtype)

    blk = pl.BlockSpec((tm, tn), lambda j, i: (i, j))
    return pl.pallas_call(
        body,
        grid=(h // tn, t // tm),
        in_specs=[pl.BlockSpec((tm, d), lambda j, i: (i, 0)), pl.BlockSpec((tn, d), lambda j, i: (j, 0)), blk, blk],
        out_specs=pl.BlockSpec((2, tm, tn), lambda j, i: (0, i, j)),
        out_shape=jax.ShapeDtypeStruct((2, t, h), MM_DTYPE),
        compiler_params=_cparams(("parallel", "parallel")),
        name=name,
    )(dy, out_w, g, u)


def _loss_head(y, target, *, name):
    t, d = y.shape
    tb = min(t, 1024)

    def body(y_ref, t_ref, dy_ref, dyb_ref, l_ref):
        err = y_ref[...] - t_ref[...]
        dy_ref[...] = err * (1.0 / d)
        dyb_ref[...] = (err * (1.0 / d)).astype(dyb_ref.dtype)
        part = jnp.sum(jnp.sum(err * err, axis=0, keepdims=True), axis=1, keepdims=True) * (0.5 / d)
        part = jnp.broadcast_to(part, l_ref.shape)

        @pl.when(pl.program_id(0) == 0)
        def _():
            l_ref[...] = part

        @pl.when(pl.program_id(0) > 0)
        def _():
            l_ref[...] += part

    row = pl.BlockSpec((tb, d), lambda i: (i, 0))
    return pl.pallas_call(
        body,
        grid=(t // tb,),
        in_specs=[row, row],
        out_specs=[row, row, pl.BlockSpec((8, LANES), lambda i: (0, 0))],
        out_shape=[jax.ShapeDtypeStruct((t, d), F32), jax.ShapeDtypeStruct((t, d), MM_DTYPE),
                   jax.ShapeDtypeStruct((8, LANES), F32)],
        compiler_params=_cparams(("arbitrary",)),
        name=name,
    )(y, target)


CONV_HALO = 8
CONV_TIME_TILE = 4096


def _conv_tile_scale(c):
    is_qk = c < 2 * GDN_HEADS
    scale = jnp.where(c < GDN_HEADS, GDN_DK ** -0.5, 1.0).astype(F32)
    return is_qk, scale


def _gdn_conv_fwd(pm, conv_w, *, name):
    t = pm.shape[0]
    tb = min(t, CONV_TIME_TILE)
    nt = t // tb
    hb = tb // CONV_HALO

    def body(x_ref, xp_ref, w_ref, o_ref, xe_ref):
        c = pl.program_id(0)
        ti = pl.program_id(1)
        xe_ref[0:CONV_HALO, :] = jnp.where(ti > 0, xp_ref[...], 0.0)
        xe_ref[CONV_HALO:CONV_HALO + tb, :] = x_ref[...]
        w = w_ref[...]
        y = jnp.zeros((tb, LANES), F32)
        for j in range(GDN_CONV):
            off = CONV_HALO - (GDN_CONV - 1) + j
            y = y + w[j:j + 1, :] * xe_ref[pl.ds(off, tb), :]
        s = _silu(y)
        is_qk, scale = _conv_tile_scale(c)
        r = lax.rsqrt(jnp.sum(s * s, axis=-1, keepdims=True) + L2_EPS) * scale
        o_ref[...] = s * jnp.where(is_qk, r, 1.0)

    return pl.pallas_call(
        body,
        grid=(GDN_QKV // LANES, nt),
        in_specs=[
            pl.BlockSpec((tb, LANES), lambda c, i: (i, c)),
            pl.BlockSpec((CONV_HALO, LANES), lambda c, i: (jnp.maximum(i * hb - 1, 0), c)),
            pl.BlockSpec((GDN_CONV, LANES), lambda c, i: (0, c)),
        ],
        out_specs=pl.BlockSpec((tb, LANES), lambda c, i: (i, c)),
        out_shape=jax.ShapeDtypeStruct((t, GDN_QKV), F32),
        scratch_shapes=[pltpu.VMEM((tb + CONV_HALO, LANES), F32)],
        compiler_params=_cparams(("parallel", "parallel")),
        name=name,
    )(pm, pm, conv_w)


def _gdn_conv_bwd(pm, conv_w, dout, *, name):
    t = pm.shape[0]
    tb = min(t, CONV_TIME_TILE)
    nt = t // tb
    hb = tb // CONV_HALO
    last_hb = t // CONV_HALO - 1
    ext = tb + CONV_HALO

    def body(x_ref, xp_ref, xn_ref, d_ref, dn_ref, w_ref, dx_ref, dw_ref, xe_ref, dy_ref):
        c = pl.program_id(0)
        ti = pl.program_id(1)
        has_next = ti < nt - 1
        xe_ref[0:CONV_HALO, :] = jnp.where(ti > 0, xp_ref[...], 0.0)
        xe_ref[CONV_HALO:CONV_HALO + tb, :] = x_ref[...]
        xe_ref[CONV_HALO + tb:2 * CONV_HALO + tb, :] = jnp.where(has_next, xn_ref[...], 0.0)
        de = jnp.concatenate([d_ref[...], jnp.where(has_next, dn_ref[...], 0.0)], axis=0)
        w = w_ref[...]
        y = jnp.zeros((ext, LANES), F32)
        for j in range(GDN_CONV):
            off = CONV_HALO - (GDN_CONV - 1) + j
            y = y + w[j:j + 1, :] * xe_ref[pl.ds(off, ext), :]
        sig = 1.0 / (1.0 + jnp.exp(-y))
        s = y * sig
        is_qk, scale = _conv_tile_scale(c)
        r = lax.rsqrt(jnp.sum(s * s, axis=-1, keepdims=True) + L2_EPS)
        n = s * r
        dnrm = de * scale
        ds_qk = r * (dnrm - n * jnp.sum(dnrm * n, axis=-1, keepdims=True))
        ds = jnp.where(is_qk, ds_qk, de)
        dy_ref[...] = ds * (sig + s * (1.0 - sig))
        dy = dy_ref[0:tb, :]
        dx = jnp.zeros((tb, LANES), F32)
        dw_rows = []
        for j in range(GDN_CONV):
            sh = GDN_CONV - 1 - j
            dx = dx + w[j:j + 1, :] * dy_ref[pl.ds(sh, tb), :]
            off = CONV_HALO - (GDN_CONV - 1) + j
            dw_rows.append(jnp.sum(dy * xe_ref[pl.ds(off, tb), :], axis=0, keepdims=True))
        dx_ref[...] = dx.astype(dx_ref.dtype)
        part = jnp.concatenate(dw_rows, axis=0)

        @pl.when(ti == 0)
        def _():
            dw_ref[...] = part

        @pl.when(ti > 0)
        def _():
            dw_ref[...] += part

    main = pl.BlockSpec((tb, LANES), lambda c, i: (i, c))
    prev = pl.BlockSpec((CONV_HALO, LANES), lambda c, i: (jnp.maximum(i * hb - 1, 0), c))
    nxt = pl.BlockSpec((CONV_HALO, LANES), lambda c, i: (jnp.minimum((i + 1) * hb, last_hb), c))
    return pl.pallas_call(
        body,
        grid=(GDN_QKV // LANES, nt),
        in_specs=[main, prev, nxt, main, nxt, pl.BlockSpec((GDN_CONV, LANES), lambda c, i: (0, c))],
        out_specs=[main, pl.BlockSpec((GDN_CONV, LANES), lambda c, i: (0, c))],
        out_shape=[jax.ShapeDtypeStruct((t, GDN_QKV), MM_DTYPE), jax.ShapeDtypeStruct((GDN_CONV, GDN_QKV), F32)],
        scratch_shapes=[pltpu.VMEM((tb + 2 * CONV_HALO, LANES), F32), pltpu.VMEM((ext, LANES), F32)],
        compiler_params=_cparams(("parallel", "arbitrary")),
        name=name,
    )(pm, pm, pm, dout, dout, conv_w)


def _head_selector(first_col):
    row = lax.broadcasted_iota(jnp.int32, (LANES, GDN_HEADS * LANES), 0)
    col = lax.broadcasted_iota(jnp.int32, (LANES, GDN_HEADS * LANES), 1)
    return (col // LANES + first_col == row).astype(BF16)


def _spread_columns(cols, first_col):
    sel = _head_selector(first_col)
    return sum(_bdot(p, sel) for p in _bf16_pieces(cols))


def _gather_columns(wide, first_col):
    sel = _head_selector(first_col)
    return sum(_bdot_nt(p, sel) for p in _bf16_pieces(wide))


def _softplus(x):
    return jnp.maximum(x, 0.0) + jnp.log(1.0 + jnp.exp(-jnp.abs(x)))


def _gdn_gates_fwd(ab, alog_row, dt_row, *, name):
    t = ab.shape[0]
    tb = min(t, 1024)
    wide = GDN_HEADS * LANES

    def body(ab_ref, al_ref, dt_ref, g_ref, b_ref):
        x = ab_ref[...]
        g_cols = -jnp.exp(al_ref[...]) * _softplus(x + dt_ref[...])
        b_cols = 1.0 / (1.0 + jnp.exp(-x))
        g_ref[...] = _spread_columns(g_cols, 0)
        b_ref[...] = _spread_columns(b_cols, GDN_HEADS)

    row = pl.BlockSpec((tb, LANES), lambda i: (i, 0))
    one = pl.BlockSpec((1, LANES), lambda i: (0, 0))
    out = pl.BlockSpec((tb, wide), lambda i: (i, 0))
    return pl.pallas_call(
        body,
        grid=(t // tb,),
        in_specs=[row, one, one],
        out_specs=[out, out],
        out_shape=[jax.ShapeDtypeStruct((t, wide), F32)] * 2,
        compiler_params=_cparams(("parallel",)),
        name=name,
    )(ab, alog_row, dt_row)


def _gdn_gates_bwd(ab, alog_row, dt_row, dgb, dbb, *, name):
    t = ab.shape[0]
    tb = min(t, 1024)
    wide = GDN_HEADS * LANES

    def body(ab_ref, al_ref, dt_ref, dg_ref, db_ref, dab_ref, dal_ref, ddt_ref):
        x = ab_ref[...]
        lane = lax.broadcasted_iota(jnp.int32, (tb, LANES), 1)
        dg_cols = _gather_columns(dg_ref[...], 0)
        db_cols = _gather_columns(db_ref[...], GDN_HEADS)
        ea = jnp.exp(al_ref[...])
        z = x + dt_ref[...]
        sp = _softplus(z)
        sg = 1.0 / (1.0 + jnp.exp(-z))
        beta = 1.0 / (1.0 + jnp.exp(-x))
        da = jnp.where(lane < GDN_HEADS, dg_cols * (-ea) * sg, 0.0)
        db = jnp.where((lane >= GDN_HEADS) & (lane < 2 * GDN_HEADS), db_cols * beta * (1.0 - beta), 0.0)
        dab_ref[...] = (da + db).astype(dab_ref.dtype)
        p_al = jnp.sum(jnp.where(lane < GDN_HEADS, dg_cols * (-ea) * sp, 0.0), axis=0, keepdims=True)
        p_dt = jnp.sum(da, axis=0, keepdims=True)

        @pl.when(pl.program_id(0) == 0)
        def _():
            dal_ref[...] = p_al
            ddt_ref[...] = p_dt

        @pl.when(pl.program_id(0) > 0)
        def _():
            dal_ref[...] += p_al
            ddt_ref[...] += p_dt

    row = pl.BlockSpec((tb, LANES), lambda i: (i, 0))
    one = pl.BlockSpec((1, LANES), lambda i: (0, 0))
    big = pl.BlockSpec((tb, wide), lambda i: (i, 0))
    return pl.pallas_call(
        body,
        grid=(t // tb,),
        in_specs=[row, one, one, big, big],
        out_specs=[row, one, one],
        out_shape=[jax.ShapeDtypeStruct((t, LANES), MM_DTYPE), jax.ShapeDtypeStruct((1, LANES), F32),
                   jax.ShapeDtypeStruct((1, LANES), F32)],
        compiler_params=_cparams(("arbitrary",)),
        name=name,
    )(ab, alog_row, dt_row, dgb, dbb)


@jax.custom_vjp
def _unit_lower_inverse_rest(n):
    c = n.shape[-1]
    ri = lax.broadcasted_iota(jnp.int32, (c, c), 0)
    ci = lax.broadcasted_iota(jnp.int32, (c, c), 1)
    rest = None
    size = 1
    while size < c:
        joins = ((ri // (2 * size)) == (ci // (2 * size))) & ((ri // size) != (ci // size))
        low = jnp.where(joins, n, 0.0)
        if rest is None:
            rest = -low
        else:
            left = low + _bdot(rest, low)
            rest = rest - (left + _bdot(left, rest))
        size *= 2
    return rest


def _unit_lower_inverse_rest_fwd(n):
    rest = _unit_lower_inverse_rest(n)
    return rest, rest


def _unit_lower_inverse_rest_bwd(rest, ct):
    left = ct + _bdot_tn(rest, ct)
    return (-(left + _bdot_nt(left, rest)),)


_unit_lower_inverse_rest.defvjp(_unit_lower_inverse_rest_fwd, _unit_lower_inverse_rest_bwd)


@jax.custom_vjp
def _known_inverse_rest(n, rest):
    return rest


def _known_inverse_rest_fwd(n, rest):
    return rest, rest


def _known_inverse_rest_bwd(rest, ct):
    return _unit_lower_inverse_rest_bwd(rest, ct) + (jnp.zeros_like(rest),)


_known_inverse_rest.defvjp(_known_inverse_rest_fwd, _known_inverse_rest_bwd)


def _bf16_pieces(x):
    hi = x.astype(BF16)
    r1 = x - hi.astype(F32)
    mid = r1.astype(BF16)
    lo = (r1 - mid.astype(F32)).astype(BF16)
    return hi, mid, lo


def _lower_ones(shape):
    c = shape[-1]
    ri = lax.broadcasted_iota(jnp.int32, (c, c), 0)
    ci = lax.broadcasted_iota(jnp.int32, (c, c), 1)
    return jnp.broadcast_to((ri >= ci).astype(BF16), shape)


@jax.custom_vjp
def _running_sum(x):
    tri = _lower_ones(x.shape)
    return sum(_bdot(tri, p) for p in _bf16_pieces(x))


def _running_sum_fwd(x):
    return _running_sum(x), None


def _running_sum_bwd(_, ct):
    tri = _lower_ones(ct.shape)
    return (sum(_bdot_tn(tri, p) for p in _bf16_pieces(ct)),)


_running_sum.defvjp(_running_sum_fwd, _running_sum_bwd)


def _gdn_prep_math(q, k, v, gb, bb, known_rest=None, with_rest=False):
    c = GDN_CHUNK
    ri = lax.broadcasted_iota(jnp.int32, (c, c), 0)
    ci = lax.broadcasted_iota(jnp.int32, (c, c), 1)
    causal = ri >= ci
    gc = _running_sum(gb)
    decay = jnp.exp(jnp.where(causal, gc - jnp.swapaxes(gc, -1, -2), NEG))
    n = jnp.where(ri > ci, _bdot_nt(k, k) * bb * decay, 0.0)
    rest = _unit_lower_inverse_rest(n) if known_rest is None else _known_inverse_rest(n, known_rest)
    eg = jnp.exp(gc)
    rhs_v = v * bb
    rhs_k = k * bb * eg
    u = rhs_v + _bdot(rest, rhs_v)
    w = rhs_k + _bdot(rest, rhs_k)
    qk = _bdot_nt(q, k) * decay
    qd = q * eg
    last = jnp.sum(jnp.where(ri == c - 1, gc, 0.0), axis=-2, keepdims=True)
    gl = jnp.broadcast_to(last, gc.shape)
    kt = k * jnp.exp(gl - gc)
    cd = jnp.exp(gl)
    return (u, w, qk, qd, kt, cd, rest) if with_rest else (u, w, qk, qd, kt, cd)


def _head_tiles(ref, h):
    return ref[:, h * LANES:(h + 1) * LANES]


def _stack_heads(ref, first=0, heads=GDN_HEADS):
    return jnp.stack([_head_tiles(ref, first + h) for h in range(heads)])


def _store_heads(ref, val, first=0):
    for h in range(val.shape[0]):
        ref[:, (first + h) * LANES:(first + h + 1) * LANES] = val[h].astype(ref.dtype)


def _gdn_prep_fwd(qkv, gb, bb, *, name):
    t = qkv.shape[0]
    c = GDN_CHUNK
    wide = GDN_HEADS * LANES

    def body(q_ref, k_ref, v_ref, g_ref, b_ref, *outs):
        res = _gdn_prep_math(*(_stack_heads(r) for r in (q_ref, k_ref, v_ref, g_ref, b_ref)), with_rest=True)
        for o_ref, val in zip(outs, res):
            _store_heads(o_ref, val)

    blk = lambda off: pl.BlockSpec((c, wide), lambda i: (i, off))
    outs = pl.pallas_call(
        body,
        grid=(t // c,),
        in_specs=[blk(0), blk(1), blk(2), blk(0), blk(0)],
        out_specs=[blk(0)] * 7,
        out_shape=[jax.ShapeDtypeStruct((t, wide), dt)
                   for dt in (F32, MM_DTYPE, MM_DTYPE, MM_DTYPE, MM_DTYPE, F32, MM_DTYPE)],
        compiler_params=_cparams(("parallel",)),
        name=name,
    )(qkv, qkv, qkv, gb, bb)
    return tuple(outs[:6]), outs[6]


def _gdn_prep_bwd(qkv, gb, bb, rest, cts, *, name):
    t = qkv.shape[0]
    c = GDN_CHUNK
    wide = GDN_HEADS * LANES

    def body(q_ref, k_ref, v_ref, g_ref, b_ref, r_ref, c0, c1, c2, c3, c4, c5, dqkv_ref, dg_ref, db_ref):
        prim = tuple(_stack_heads(r) for r in (q_ref, k_ref, v_ref, g_ref, b_ref))
        _, pull = jax.vjp(functools.partial(_gdn_prep_math, known_rest=_stack_heads(r_ref).astype(F32)), *prim)
        dq, dk, dv, dg, db = pull(tuple(_stack_heads(r).astype(F32) for r in (c0, c1, c2, c3, c4, c5)))
        _store_heads(dqkv_ref, dq)
        _store_heads(dqkv_ref, dk, first=GDN_HEADS)
        _store_heads(dqkv_ref, dv, first=2 * GDN_HEADS)
        _store_heads(dg_ref, dg)
        _store_heads(db_ref, db)

    blk = lambda off: pl.BlockSpec((c, wide), lambda i: (i, off))
    return pl.pallas_call(
        body,
        grid=(t // c,),
        in_specs=[blk(0), blk(1), blk(2), blk(0), blk(0)] + [blk(0)] * 7,
        out_specs=[pl.BlockSpec((c, 3 * wide), lambda i: (i, 0)), blk(0), blk(0)],
        out_shape=[jax.ShapeDtypeStruct((t, 3 * wide), F32), jax.ShapeDtypeStruct((t, wide), F32),
                   jax.ShapeDtypeStruct((t, wide), F32)],
        compiler_params=_cparams(("parallel",)),
        name=name,
    )(qkv, qkv, qkv, gb, bb, rest, *cts)


def _gdn_scan_math(s, u, w, qk, qd, kt, cd):
    v_new = u - _bdot(w, s)
    o = _bdot(qd, s) + _bdot(qk, v_new)
    s_new = s * cd + _bdot_tn(kt, v_new)
    return o, s_new


def _gdn_scan_fwd(prep, *, name):
    t = prep[0].shape[0]
    c = GDN_CHUNK
    wide = GDN_HEADS * LANES

    def body(u_ref, w_ref, qk_ref, qd_ref, kt_ref, cd_ref, o_ref, st_ref, s_ref):
        @pl.when(pl.program_id(0) == 0)
        def _():
            s_ref[...] = jnp.zeros_like(s_ref)

        s = _stack_heads(s_ref)
        _store_heads(st_ref, s)
        o, s_new = _gdn_scan_math(s, *(_stack_heads(r).astype(F32) for r in (u_ref, w_ref, qk_ref, qd_ref, kt_ref, cd_ref)))
        _store_heads(o_ref, o)
        _store_heads(s_ref, s_new)

    blk = pl.BlockSpec((c, wide), lambda i: (i, 0))
    return pl.pallas_call(
        body,
        grid=(t // c,),
        in_specs=[blk] * 6,
        out_specs=[blk, blk],
        out_shape=[jax.ShapeDtypeStruct((t, wide), F32)] * 2,
        scratch_shapes=[pltpu.VMEM((GDN_DK, wide), F32)],
        compiler_params=_cparams(("arbitrary",)),
        name=name,
    )(*prep)


def _gdn_scan_bwd(prep, states, do, *, name):
    t = do.shape[0]
    c = GDN_CHUNK
    wide = GDN_HEADS * LANES
    nc = t // c

    def body(u_ref, w_ref, qk_ref, qd_ref, kt_ref, cd_ref, st_ref, do_ref, *rest):
        outs, ds_ref = rest[:6], rest[6]

        @pl.when(pl.program_id(0) == 0)
        def _():
            ds_ref[...] = jnp.zeros_like(ds_ref)

        prim = tuple(_stack_heads(r).astype(F32) for r in (st_ref, u_ref, w_ref, qk_ref, qd_ref, kt_ref, cd_ref))
        _, pull = jax.vjp(_gdn_scan_math, *prim)
        grads = pull((_stack_heads(do_ref), _stack_heads(ds_ref)))
        _store_heads(ds_ref, grads[0])
        for o_ref, val in zip(outs, grads[1:]):
            _store_heads(o_ref, val)

    blk = pl.BlockSpec((c, wide), lambda i: (nc - 1 - i, 0))
    return pl.pallas_call(
        body,
        grid=(nc,),
        in_specs=[blk] * 8,
        out_specs=[blk] * 6,
        out_shape=[jax.ShapeDtypeStruct((t, wide), dt) for dt in (F32, MM_DTYPE, MM_DTYPE, MM_DTYPE, MM_DTYPE, F32)],
        scratch_shapes=[pltpu.VMEM((GDN_DK, wide), F32)],
        compiler_params=_cparams(("arbitrary",)),
        name=name,
    )(*prep, states, do)


def _gdn_outgate_math(o, z, nw):
    r = lax.rsqrt(jnp.mean(o * o, axis=-1, keepdims=True) + RMS_EPS)
    return o * r * nw * _silu(z)


def _gdn_outgate_fwd(o, pm, nw_row, *, name):
    t = o.shape[0]
    tb = min(t, ROW_TILE)
    wide = GDN_HEADS * LANES
    z_at = GDN_QKV // wide

    def body(o_ref, z_ref, nw_ref, y_ref):
        for h in range(GDN_HEADS):
            y = _gdn_outgate_math(_head_tiles(o_ref, h), _head_tiles(z_ref, h), nw_ref[...])
            y_ref[:, h * LANES:(h + 1) * LANES] = y.astype(y_ref.dtype)

    return pl.pallas_call(
        body,
        grid=(t // tb,),
        in_specs=[pl.BlockSpec((tb, wide), lambda i: (i, 0)), pl.BlockSpec((tb, wide), lambda i: (i, z_at)),
                  pl.BlockSpec((1, LANES), lambda i: (0, 0))],
        out_specs=pl.BlockSpec((tb, wide), lambda i: (i, 0)),
        out_shape=jax.ShapeDtypeStruct((t, wide), MM_DTYPE),
        compiler_params=_cparams(("parallel",)),
        name=name,
    )(o, pm, nw_row)


def _gdn_outgate_bwd(o, pm, nw_row, dy, *, name):
    t = o.shape[0]
    tb = min(t, ROW_TILE)
    wide = GDN_HEADS * LANES
    z_at = GDN_QKV // wide

    def body(o_ref, z_ref, nw_ref, dy_ref, do_ref, dz_ref, dnw_ref):
        total = jnp.zeros((1, LANES), F32)
        for h in range(GDN_HEADS):
            _, pull = jax.vjp(_gdn_outgate_math, _head_tiles(o_ref, h), _head_tiles(z_ref, h), nw_ref[...])
            d_o, d_z, d_nw = pull(_head_tiles(dy_ref, h))
            do_ref[:, h * LANES:(h + 1) * LANES] = d_o
            dz_ref[:, h * LANES:(h + 1) * LANES] = d_z.astype(dz_ref.dtype)
            total = total + d_nw

        @pl.when(pl.program_id(0) == 0)
        def _():
            dnw_ref[...] = total

        @pl.when(pl.program_id(0) > 0)
        def _():
            dnw_ref[...] += total

    blk = pl.BlockSpec((tb, wide), lambda i: (i, 0))
    one = pl.BlockSpec((1, LANES), lambda i: (0, 0))
    return pl.pallas_call(
        body,
        grid=(t // tb,),
        in_specs=[blk, pl.BlockSpec((tb, wide), lambda i: (i, z_at)), one, blk],
        out_specs=[blk, blk, one],
        out_shape=[jax.ShapeDtypeStruct((t, wide), F32), jax.ShapeDtypeStruct((t, wide), MM_DTYPE),
                   jax.ShapeDtypeStruct((1, LANES), F32)],
        compiler_params=_cparams(("arbitrary",)),
        name=name,
    )(o, pm, nw_row, dy)


def _rms64(x, w_row):
    return x * lax.rsqrt(jnp.sum(x * x, axis=-1, keepdims=True) * (1.0 / DIL_DH) + RMS_EPS) * w_row


def _alibi_slopes(group):
    head = lax.broadcasted_iota(jnp.int32, (DIL_HEADS, 8, LANES), 0).astype(F32)
    rate = -math.log(2.0) * ALIBI_MAX_BIAS / (len(DIL_GROUPS) * DIL_HEADS)
    slope = jnp.exp(rate * (head + float(group * DIL_HEADS + 1)))
    return jnp.broadcast_to(slope[:, 0:1, :], (DIL_HEADS, DIL_SPAN, LANES))


def _band_logits(qn, kp, kc, slope_d, has_prev):
    qi = lax.broadcasted_iota(jnp.int32, (DIL_SPAN, DIL_SPAN), 0)
    kj = lax.broadcasted_iota(jnp.int32, (DIL_SPAN, DIL_SPAN), 1)
    steps_c = (qi - kj).astype(F32)
    scale = DIL_DH ** -0.5
    sp = _bdot_nt(qn, kp) * scale - slope_d * (steps_c + float(DIL_SPAN))
    sc = _bdot_nt(qn, kc) * scale - slope_d * steps_c
    sp = jnp.where((kj >= qi) & has_prev, sp, NEG)
    sc = jnp.where(kj <= qi, sc, NEG)
    return sp, sc


def _dil_attn_fwd(slab, wq_row, wk_row, *, group, name):
    dilation = DIL_GROUPS[group][1]
    t = slab.shape[0]
    rows = t // dilation
    nlb = rows // DIL_SPAN
    wide = DIL_HEADS * LANES
    view = slab.reshape(rows, dilation * DIL_SLAB)

    def body(q_ref, kc_ref, vc_ref, kp_ref, vp_ref, wq_ref, wk_ref, o_ref):
        has_prev = pl.program_id(1) > 0
        lane = lax.broadcasted_iota(jnp.int32, (DIL_SPAN, LANES), 1)
        qn = _rms64(_stack_heads(q_ref), wq_ref[...])
        kc = _rms64(_stack_heads(kc_ref), wk_ref[...])
        kp = _rms64(_stack_heads(kp_ref), wk_ref[...])
        sp, sc = _band_logits(qn, kp, kc, _alibi_slopes(group) * float(dilation), has_prev)
        m = jnp.maximum(jnp.max(sp, axis=-1, keepdims=True), jnp.max(sc, axis=-1, keepdims=True))
        pp = jnp.exp(sp - m)
        pc = jnp.exp(sc - m)
        l = jnp.sum(pp, axis=-1, keepdims=True) + jnp.sum(pc, axis=-1, keepdims=True)
        o = (_bdot(pp, _stack_heads(vp_ref)) + _bdot(pc, _stack_heads(vc_ref))) / l
        _store_heads(o_ref, jnp.where(lane < DIL_DH, o, m + jnp.log(l)))

    cur = lambda part: pl.BlockSpec((DIL_SPAN, wide), lambda r, i: (i, 3 * r + part))
    prv = lambda part: pl.BlockSpec((DIL_SPAN, wide), lambda r, i: (jnp.maximum(i - 1, 0), 3 * r + part))
    one = pl.BlockSpec((1, LANES), lambda r, i: (0, 0))
    out = pl.pallas_call(
        body,
        grid=(dilation, nlb),
        in_specs=[cur(0), cur(1), cur(2), prv(1), prv(2), one, one],
        out_specs=pl.BlockSpec((DIL_SPAN, wide), lambda r, i: (i, r)),
        out_shape=jax.ShapeDtypeStruct((rows, dilation * wide), F32),
        compiler_params=_cparams(("parallel", "parallel")),
        name=name,
    )(view, view, view, view, view, wq_row, wk_row)
    return out.reshape(t, wide)


def _head_slope(group, head):
    idx = jnp.zeros((8, LANES), F32) + head.astype(F32)
    rate = -math.log(2.0) * ALIBI_MAX_BIAS / (len(DIL_GROUPS) * DIL_HEADS)
    slope = jnp.exp(rate * (idx + float(group * DIL_HEADS + 1)))
    return jnp.broadcast_to(slope[0:1, :], (DIL_SPAN, LANES))


RESIDUE_BATCH = 8


def _take_residues(ref, d, first=0, count=None):
    count = d if count is None else count
    return jnp.stack([ref[pl.ds(first + r, DIL_SPAN, stride=d), :] for r in range(count)])


def _put_residues(ref, val, d, first=0):
    for r in range(val.shape[0]):
        ref[pl.ds(first + r, DIL_SPAN, stride=d), :] = val[r]


def _dil_attn_fwd_strided(slab, wq_row, wk_row, *, group, name):
    d = DIL_GROUPS[group][1]
    t = slab.shape[0]
    span = DIL_SPAN * d
    nsb = t // span

    hs = max(1, RESIDUE_BATCH // d)

    def body(*refs):
        q, kc, vc, kp, vp = (refs[i * hs:(i + 1) * hs] for i in range(5))
        wq_ref, wk_ref, o_ref, spread = refs[5 * hs:]
        has_prev = pl.program_id(0) > 0
        lane = lax.broadcasted_iota(jnp.int32, (DIL_SPAN, LANES), 1)
        nb = min(d, RESIDUE_BATCH)
        for r0 in range(0, d, nb):
            take = lambda group_refs: jnp.concatenate([_take_residues(ref, d, r0, nb) for ref in group_refs])
            slope = jnp.concatenate([jnp.broadcast_to(_head_slope(group, pl.program_id(1) * hs + j) * float(d),
                                                      (nb, DIL_SPAN, LANES)) for j in range(hs)])
            qn = _rms64(take(q), wq_ref[...])
            kcn = _rms64(take(kc), wk_ref[...])
            kpn = _rms64(take(kp), wk_ref[...])
            sp, sc = _band_logits(qn, kpn, kcn, slope, has_prev)
            m = jnp.maximum(jnp.max(sp, axis=-1, keepdims=True), jnp.max(sc, axis=-1, keepdims=True))
            pp = jnp.exp(sp - m)
            pc = jnp.exp(sc - m)
            l = jnp.sum(pp, axis=-1, keepdims=True) + jnp.sum(pc, axis=-1, keepdims=True)
            o = (_bdot(pp, take(vp)) + _bdot(pc, take(vc))) / l
            res = jnp.where(lane < DIL_DH, o, m + jnp.log(l))
            for j in range(hs):
                _put_residues(spread, res[j * nb:(j + 1) * nb], d, r0)
                if r0 + nb == d:
                    o_ref[:, j * LANES:(j + 1) * LANES] = spread[...]

    cur = lambda part, j: pl.BlockSpec((span, LANES), lambda i, h: (i, part * DIL_HEADS + h * hs + j))
    prv = lambda part, j: pl.BlockSpec((span, LANES), lambda i, h: (jnp.maximum(i - 1, 0), part * DIL_HEADS + h * hs + j))
    one = pl.BlockSpec((1, LANES), lambda i, h: (0, 0))
    heads = range(hs)
    in_specs = ([cur(0, j) for j in heads] + [cur(1, j) for j in heads] + [cur(2, j) for j in heads]
                + [prv(1, j) for j in heads] + [prv(2, j) for j in heads] + [one, one])
    return pl.pallas_call(
        body,
        grid=(nsb, DIL_HEADS // hs),
        in_specs=in_specs,
        out_specs=pl.BlockSpec((span, hs * LANES), lambda i, h: (i, h)),
        out_shape=jax.ShapeDtypeStruct((t, DIL_HEADS * LANES), F32),
        scratch_shapes=[pltpu.VMEM((span, LANES), F32)],
        compiler_params=_cparams(("parallel", "parallel")),
        name=name,
    )(*([slab] * (5 * hs)), wq_row, wk_row)


def _dil_attn_bwd_strided(slab, stat, wq_row, wk_row, dwq_in, dwk_in, *, group, name):
    d = DIL_GROUPS[group][1]
    t = slab.shape[0]
    span = DIL_SPAN * d
    nsb = t // span

    hs = max(1, RESIDUE_BATCH // d)

    def body(*refs):
        q_refs, kc_refs, vc_refs, kp_refs, vp_refs, st_refs = (refs[i * hs:(i + 1) * hs] for i in range(6))
        wq_ref, wk_ref, dwq_in_ref, dwk_in_ref, d_ref, dwq_ref, dwk_ref, dk_carry, dv_carry, spread = refs[6 * hs:]
        take = lambda group_refs: jnp.concatenate([_take_residues(ref, d) for ref in group_refs])
        step = pl.program_id(1)
        has_prev = step < nsb - 1
        first = (pl.program_id(0) == 0) & (step == 0)

        @pl.when(step == 0)
        def _():
            dk_carry[...] = jnp.zeros_like(dk_carry)
            dv_carry[...] = jnp.zeros_like(dv_carry)

        @pl.when(first)
        def _():
            dwq_ref[...] = dwq_in_ref[...]
            dwk_ref[...] = dwk_in_ref[...]

        lane = lax.broadcasted_iota(jnp.int32, (DIL_SPAN, LANES), 1)
        scale = DIL_DH ** -0.5
        q_raw = take(q_refs)
        kc_raw = take(kc_refs)
        vc = take(vc_refs)
        kp_raw = take(kp_refs)
        vp = take(vp_refs)
        st = take(st_refs)
        slope = jnp.concatenate([jnp.broadcast_to(_head_slope(group, pl.program_id(0) * hs + j) * float(d),
                                                  (d, DIL_SPAN, LANES)) for j in range(hs)])
        d_o = jnp.where(lane < DIL_DH, st, 0.0)
        lse = jnp.sum(jnp.where(lane == DIL_DH, st, 0.0), axis=-1, keepdims=True)
        delta = jnp.sum(jnp.where(lane == DIL_DH + 1, st, 0.0), axis=-1, keepdims=True)
        qn = _rms64(q_raw, wq_ref[...])
        kc = _rms64(kc_raw, wk_ref[...])
        kp = _rms64(kp_raw, wk_ref[...])
        sp, sc = _band_logits(qn, kp, kc, slope, has_prev)
        pp = jnp.exp(sp - lse)
        pc = jnp.exp(sc - lse)
        dsp = pp * (_bdot_nt(d_o, vp) - delta) * scale
        dsc = pc * (_bdot_nt(d_o, vc) - delta) * scale
        dqn = _bdot(dsp, kp) + _bdot(dsc, kc)
        dkc_n = _bdot_tn(dsc, qn) + dk_carry[...]
        dvc = _bdot_tn(pc, d_o) + dv_carry[...]
        dk_carry[...] = _bdot_tn(dsp, qn)
        dv_carry[...] = _bdot_tn(pp, d_o)
        dq_raw, dwq_rows = _rms64_bwd(q_raw, wq_ref[...], dqn)
        dk_raw, dwk_rows = _rms64_bwd(kc_raw, wk_ref[...], dkc_n)
        for part, val in enumerate((dq_raw, dk_raw, dvc)):
            for j in range(hs):
                _put_residues(spread, val[j * d:(j + 1) * d], d)
                d_ref[part, :, j * LANES:(j + 1) * LANES] = spread[...].astype(d_ref.dtype)
        dwq_ref[...] += jnp.sum(jnp.sum(dwq_rows, axis=0), axis=0, keepdims=True)
        dwk_ref[...] += jnp.sum(jnp.sum(dwk_rows, axis=0), axis=0, keepdims=True)

    at = lambda i: nsb - 1 - i
    cur = lambda part, j: pl.BlockSpec((span, LANES), lambda h, i: (at(i), part * DIL_HEADS + h * hs + j))
    prv = lambda part, j: pl.BlockSpec((span, LANES), lambda h, i: (jnp.maximum(at(i) - 1, 0), part * DIL_HEADS + h * hs + j))
    one = pl.BlockSpec((1, LANES), lambda h, i: (0, 0))
    heads = range(hs)
    in_specs = ([cur(0, j) for j in heads] + [cur(1, j) for j in heads] + [cur(2, j) for j in heads]
                + [prv(1, j) for j in heads] + [prv(2, j) for j in heads] + [cur(0, j) for j in heads] + [one] * 4)
    return pl.pallas_call(
        body,
        grid=(DIL_HEADS // hs, nsb),
        in_specs=in_specs,
        out_specs=[pl.BlockSpec((3, span, hs * LANES), lambda h, i: (0, at(i), h)), one, one],
        out_shape=[jax.ShapeDtypeStruct((3, t, DIL_HEADS * LANES), MM_DTYPE), jax.ShapeDtypeStruct((1, LANES), F32),
                   jax.ShapeDtypeStruct((1, LANES), F32)],
        scratch_shapes=[pltpu.VMEM((hs * d, DIL_SPAN, LANES), F32), pltpu.VMEM((hs * d, DIL_SPAN, LANES), F32),
                        pltpu.VMEM((span, LANES), F32)],
        compiler_params=_cparams(("arbitrary", "arbitrary")),
        name=name,
    )(*([slab] * (5 * hs)), *([stat] * hs), wq_row, wk_row, dwq_in, dwk_in)


def _dil_merge_fwd(oe, *, name):
    t = oe[0].shape[0]
    tb = min(t, ROW_TILE)
    wide = DIL_HEADS * LANES

    def body(e0, e1, e2, y_ref, om_ref):
        lane = lax.broadcasted_iota(jnp.int32, (tb, LANES), 1)
        for h in range(DIL_HEADS):
            es = [_head_tiles(e, h) for e in (e0, e1, e2)]
            lse = [jnp.sum(jnp.where(lane == DIL_DH, e, 0.0), axis=-1, keepdims=True) for e in es]
            top = jnp.maximum(jnp.maximum(lse[0], lse[1]), lse[2])
            joint = top + jnp.log(jnp.exp(lse[0] - top) + jnp.exp(lse[1] - top) + jnp.exp(lse[2] - top))
            o = sum(jnp.exp(l - joint) * e for l, e in zip(lse, es))
            y_ref[:, h * LANES:(h + 1) * LANES] = jnp.where(lane < DIL_DH, o, 0.0).astype(y_ref.dtype)
            om_ref[:, h * LANES:(h + 1) * LANES] = jnp.where(lane < DIL_DH, o, joint)

    blk = pl.BlockSpec((tb, wide), lambda i: (i, 0))
    return pl.pallas_call(
        body,
        grid=(t // tb,),
        in_specs=[blk] * 3,
        out_specs=[blk, blk],
        out_shape=[jax.ShapeDtypeStruct((t, wide), MM_DTYPE), jax.ShapeDtypeStruct((t, wide), F32)],
        compiler_params=_cparams(("parallel",)),
        name=name,
    )(*oe)


def _dil_merge_bwd(dy, om, *, name):
    t = dy.shape[0]
    tb = min(t, ROW_TILE)
    wide = DIL_HEADS * LANES

    def body(dy_ref, om_ref, st_ref):
        lane = lax.broadcasted_iota(jnp.int32, (tb, LANES), 1)
        for h in range(DIL_HEADS):
            d_o = jnp.where(lane < DIL_DH, _head_tiles(dy_ref, h), 0.0)
            om_t = _head_tiles(om_ref, h)
            delta = jnp.sum(d_o * om_t, axis=-1, keepdims=True)
            st_ref[:, h * LANES:(h + 1) * LANES] = jnp.where(
                lane < DIL_DH, d_o, jnp.where(lane == DIL_DH, om_t, jnp.where(lane == DIL_DH + 1, delta, 0.0)))

    blk = pl.BlockSpec((tb, wide), lambda i: (i, 0))
    return pl.pallas_call(
        body,
        grid=(t // tb,),
        in_specs=[blk, blk],
        out_specs=blk,
        out_shape=jax.ShapeDtypeStruct((t, wide), F32),
        compiler_params=_cparams(("parallel",)),
        name=name,
    )(dy, om)


def _rms64_bwd(x, w_row, dy):
    r = lax.rsqrt(jnp.sum(x * x, axis=-1, keepdims=True) * (1.0 / DIL_DH) + RMS_EPS)
    gw = dy * w_row
    dx = r * gw - x * (r * r * r * jnp.sum(gw * x, axis=-1, keepdims=True) * (1.0 / DIL_DH))
    return dx, dy * x * r


def _dil_attn_bwd(slab, stat, wq_row, wk_row, dwq_in, dwk_in, *, group, name):
    dilation = DIL_GROUPS[group][1]
    t = slab.shape[0]
    rows = t // dilation
    nlb = rows // DIL_SPAN
    wide = DIL_HEADS * LANES
    view = slab.reshape(rows, dilation * DIL_SLAB)
    stat_view = stat.reshape(rows, dilation * wide)

    def body(cur_ref, kp_ref, vp_ref, st_ref, wq_ref, wk_ref, dwq_in_ref, dwk_in_ref, d_ref, dwq_ref, dwk_ref,
             dk_carry, dv_carry):
        step = pl.program_id(1)
        has_prev = step < nlb - 1
        first = (pl.program_id(0) == 0) & (step == 0)

        @pl.when(step == 0)
        def _():
            dk_carry[...] = jnp.zeros_like(dk_carry)
            dv_carry[...] = jnp.zeros_like(dv_carry)

        @pl.when(first)
        def _():
            dwq_ref[...] = dwq_in_ref[...]
            dwk_ref[...] = dwk_in_ref[...]

        lane = lax.broadcasted_iota(jnp.int32, (DIL_SPAN, LANES), 1)
        scale = DIL_DH ** -0.5
        q_raw = _stack_heads(cur_ref)
        kc_raw = _stack_heads(cur_ref, first=DIL_HEADS)
        vc = _stack_heads(cur_ref, first=2 * DIL_HEADS)
        kp_raw = _stack_heads(kp_ref)
        vp = _stack_heads(vp_ref)
        st = _stack_heads(st_ref)
        d_o = jnp.where(lane < DIL_DH, st, 0.0)
        lse = jnp.sum(jnp.where(lane == DIL_DH, st, 0.0), axis=-1, keepdims=True)
        delta = jnp.sum(jnp.where(lane == DIL_DH + 1, st, 0.0), axis=-1, keepdims=True)
        qn = _rms64(q_raw, wq_ref[...])
        kc = _rms64(kc_raw, wk_ref[...])
        kp = _rms64(kp_raw, wk_ref[...])
        sp, sc = _band_logits(qn, kp, kc, _alibi_slopes(group) * float(dilation), has_prev)
        pp = jnp.exp(sp - lse)
        pc = jnp.exp(sc - lse)
        dsp = pp * (_bdot_nt(d_o, vp) - delta) * scale
        dsc = pc * (_bdot_nt(d_o, vc) - delta) * scale
        dqn = _bdot(dsp, kp) + _bdot(dsc, kc)
        dkc_n = _bdot_tn(dsc, qn) + _stack_heads(dk_carry)
        dvc = _bdot_tn(pc, d_o) + _stack_heads(dv_carry)
        _store_heads(dk_carry, _bdot_tn(dsp, qn))
        _store_heads(dv_carry, _bdot_tn(pp, d_o))
        dq_raw, dwq_rows = _rms64_bwd(q_raw, wq_ref[...], dqn)
        dk_raw, dwk_rows = _rms64_bwd(kc_raw, wk_ref[...], dkc_n)
        _store_heads(d_ref, dq_raw)
        _store_heads(d_ref, dk_raw, first=DIL_HEADS)
        _store_heads(d_ref, dvc, first=2 * DIL_HEADS)
        dwq_ref[...] += jnp.sum(jnp.sum(dwq_rows, axis=0), axis=0, keepdims=True)
        dwk_ref[...] += jnp.sum(jnp.sum(dwk_rows, axis=0), axis=0, keepdims=True)

    blk_i = lambda i: nlb - 1 - i
    cur = pl.BlockSpec((DIL_SPAN, DIL_SLAB), lambda r, i: (blk_i(i), r))
    prv = lambda part: pl.BlockSpec((DIL_SPAN, wide), lambda r, i: (jnp.maximum(blk_i(i) - 1, 0), 3 * r + part))
    one = pl.BlockSpec((1, LANES), lambda r, i: (0, 0))
    dslab, dwq, dwk = pl.pallas_call(
        body,
        grid=(dilation, nlb),
        in_specs=[cur, prv(1), prv(2), pl.BlockSpec((DIL_SPAN, wide), lambda r, i: (blk_i(i), r)), one, one, one, one],
        out_specs=[cur, one, one],
        out_shape=[jax.ShapeDtypeStruct((rows, dilation * DIL_SLAB), MM_DTYPE), jax.ShapeDtypeStruct((1, LANES), F32),
                   jax.ShapeDtypeStruct((1, LANES), F32)],
        scratch_shapes=[pltpu.VMEM((DIL_SPAN, wide), F32), pltpu.VMEM((DIL_SPAN, wide), F32)],
        compiler_params=_cparams(("arbitrary", "arbitrary")),
        name=name,
    )(view, view, view, stat_view, wq_row, wk_row, dwq_in, dwk_in)
    return dslab.reshape(t, DIL_SLAB), dwq, dwk


def _row(v, width=LANES):
    v = v.astype(F32).reshape(-1)
    return jnp.pad(v, (0, width - v.shape[0])).reshape(1, width)


def _prepare_weights(w):
    return dict(gdn=_prepare_gdn(w), dil=_prepare_dil(w), ffn=_prepare_ffn(w))


def _prepare_gdn(w, layers=range(DEPTH // 2)):
    gdn = {}
    for j in layers:
        wt = w["gdn_w_in"][j]
        gates_t = jnp.pad(wt[GDN_MAIN:], ((0, LANES - 2 * GDN_HEADS), (0, 0)))
        gdn[j] = dict(in_t=wt, gates_t=gates_t, out=w["gdn_w_out"][j], conv=w["gdn_conv_w"][j].astype(F32),
                      alog=_row(w["gdn_a_log"][j]), dt=_row(w["gdn_dt_bias"][j]), nw=_row(w["gdn_norm_w"][j]))
    return gdn


def _prepare_dil(w, layers=range(DEPTH // 2)):
    d = D_MODEL
    dil = {}
    for j in layers:
        wt = w["dil_w_in"][j].reshape(3, len(DIL_GROUPS), DIL_HEADS, DIL_DH, d)
        wg_t = [wt[:, g].reshape(DIL_SLAB // 2, d) for g in range(len(DIL_GROUPS))]
        out_t = jnp.pad(w["dil_w_out"][j].reshape(d, DIL_HEADS, DIL_DH), ((0, 0), (0, 0), (0, LANES - DIL_DH)))
        dil[j] = dict(wg_t=wg_t, out_t=out_t.reshape(d, DIL_HEADS * LANES), wq=_row(w["dil_q_norm"][j]),
                      wk=_row(w["dil_k_norm"][j]))
    return dil


def _prepare_ffn(w, layers=range(DEPTH)):
    return {i: dict(in_t=w["ffn_w_in"][i], out=w["ffn_w_out"][i]) for i in layers}


def _residual_out(a, w, x, next_row, *, name, **kw):
    if next_row is None:
        return _matmul(a, w, add=x, name=name, **kw), None
    return _matmul(a, w, add=x, norm_fwd=next_row, name=name + "_norm", **kw)


def _gdn_layer_fwd(x, nrow, p, hn=None, next_row=None):
    if hn is None:
        hn = _rmsnorm_fwd(x, nrow, name="rmsnorm_fwd")
    pm = _matmul(hn, p["in_t"], trans_b=True, b_rows=(0, GDN_MAIN), name="gdn_proj_main")
    ab = _matmul(hn, p["gates_t"], trans_b=True, name="gdn_proj_gates")
    qkv = _gdn_conv_fwd(pm, p["conv"], name="gdn_conv_fwd")
    gb, bb = _gdn_gates_fwd(ab, p["alog"], p["dt"], name="gdn_gates_fwd")
    prep, rest = _gdn_prep_fwd(qkv, gb, bb, name="gdn_prep_fwd")
    o, states = _gdn_scan_fwd(prep, name="gdn_scan_fwd")
    og = _gdn_outgate_fwd(o, pm, p["nw"], name="gdn_outgate_fwd")
    y, hn_next = _residual_out(og, p["out"], x, next_row, name="gdn_proj_out")
    return y, (x, hn, pm, ab, qkv, gb, bb, prep, rest, states, o, og), hn_next


def _gdn_layer_bwd(dx, dxb, nrow, p, saved):
    x, hn, pm, ab, qkv, gb, bb, prep, rest, states, o, og = saved
    d_og = _matmul(dxb, p["out"], trans_b=True, name="gdn_dgate")
    g_out = _matmul(og, dxb, trans_a=True, out_dtype=MM_DTYPE, name="gdn_gw_out")
    d_o, d_z, d_nw = _gdn_outgate_bwd(o, pm, p["nw"], d_og, name="gdn_outgate_bwd")
    cts = _gdn_scan_bwd(prep, states, d_o, name="gdn_scan_bwd")
    dqkv, dgb, dbb = _gdn_prep_bwd(qkv, gb, bb, rest, cts, name="gdn_prep_bwd")
    d_ab, d_alog, d_dt = _gdn_gates_bwd(ab, p["alog"], p["dt"], dgb, dbb, name="gdn_gates_bwd")
    d_conv, g_conv = _gdn_conv_bwd(pm, p["conv"], dqkv, name="gdn_conv_bwd")
    d_hn = _matmul(d_conv, p["in_t"], b_rows=(0, GDN_QKV), name="gdn_dhn_qkv")
    d_hn = _matmul(d_z, p["in_t"], b_rows=(GDN_QKV, GDN_MAIN - GDN_QKV), add=d_hn, name="gdn_dhn_z")
    dx_new, dxb_new, g_norm = _matmul(d_ab, p["gates_t"], add=d_hn, norm_bwd=(x, nrow, dx), name="gdn_dhn_gates_norm")
    g_in_t = jnp.concatenate([
        _matmul(d_conv, hn, trans_a=True, out_dtype=MM_DTYPE, name="gdn_gw_qkv"),
        _matmul(d_z, hn, trans_a=True, out_dtype=MM_DTYPE, name="gdn_gw_z"),
        _matmul(d_ab, hn, trans_a=True, out_dtype=MM_DTYPE, name="gdn_gw_gates")[:2 * GDN_HEADS],
    ], axis=0)
    grads = dict(w_in=g_in_t, conv=g_conv, a_log=d_alog[0, :GDN_HEADS], dt_bias=d_dt[0, :GDN_HEADS], norm_w=d_nw[0],
                 w_out=g_out, norm=g_norm[0])
    return dx_new, dxb_new, grads


def _dil_layer_fwd(x, nrow, p, hn=None, next_row=None):
    if hn is None:
        hn = _rmsnorm_fwd(x, nrow, name="rmsnorm_fwd")
    slabs = [_matmul(hn, p["wg_t"][g], trans_b=True, spread_out=True, name="dil_proj_in") for g in range(len(DIL_GROUPS))]
    oe = [(_dil_attn_fwd if DIL_GROUPS[g][1] == 1 else _dil_attn_fwd_strided)(
        slabs[g], p["wq"], p["wk"], group=g, name=f"dil_attn_fwd_g{g}") for g in range(len(DIL_GROUPS))]
    y, om = _dil_merge_fwd(oe, name="dil_merge_fwd")
    out, hn_next = _residual_out(y, p["out_t"], x, next_row, trans_b=True, name="dil_proj_out")
    return out, (x, hn, slabs, y, om), hn_next


def _dil_layer_bwd(dx, dxb, nrow, p, saved):
    x, hn, slabs, y, om = saved
    d_y = _matmul(dxb, p["out_t"], name="dil_dmerged")
    g_out_t = _matmul(dxb, y, trans_a=True, out_dtype=MM_DTYPE, name="dil_gw_out")
    g_out_t = g_out_t.reshape(D_MODEL, DIL_HEADS, LANES)[..., :DIL_DH].reshape(D_MODEL, DIL_HEADS * DIL_DH)
    stat = _dil_merge_bwd(d_y, om, name="dil_merge_bwd")
    d_hn = None
    dwq = jnp.zeros((1, LANES), F32)
    dwk = jnp.zeros((1, LANES), F32)
    g_groups = []
    wide = DIL_HEADS * LANES
    for g in range(len(DIL_GROUPS)):
        last = dict(norm_bwd=(x, nrow, dx)) if g == len(DIL_GROUPS) - 1 else {}
        if DIL_GROUPS[g][1] == 1:
            dslab, dwq, dwk = _dil_attn_bwd(slabs[g], stat, p["wq"], p["wk"], dwq, dwk, group=g, name=f"dil_attn_bwd_g{g}")
            d_hn = _matmul(dslab, p["wg_t"][g], packed_a=True, add=d_hn, name="dil_dhn", **last)
            g_w = _matmul(dslab, hn, trans_a=True, packed_a=True, out_dtype=MM_DTYPE, name="dil_gw_in")
        else:
            dparts, dwq, dwk = _dil_attn_bwd_strided(slabs[g], stat, p["wq"], p["wk"], dwq, dwk, group=g,
                                                     name=f"dil_attn_bwd_g{g}")
            d_hn = _matmul(dparts, p["wg_t"][g], a_lead="k", packed_a=True, add=d_hn,
                           name="dil_dhn_parts_norm" if last else "dil_dhn_parts", **last)
            g_w = _matmul(dparts, hn, trans_a=True, a_lead="i", packed_a=True, out_dtype=MM_DTYPE, name="dil_gw_in_parts")
        g_groups.append(g_w.reshape(3, DIL_HEADS, DIL_DH, D_MODEL))
    g_in_t = jnp.stack(g_groups, axis=1).reshape(3 * len(DIL_GROUPS) * DIL_HEADS * DIL_DH, D_MODEL)
    dx_new, dxb_new, g_norm = d_hn
    grads = dict(w_in=g_in_t, q_norm=dwq[0, :DIL_DH], k_norm=dwk[0, :DIL_DH], w_out=g_out_t, norm=g_norm[0])
    return dx_new, dxb_new, grads


def _ffn_layer_fwd(x, nrow, p, hn=None, next_row=None):
    if hn is None:
        hn = _rmsnorm_fwd(x, nrow, name="rmsnorm_fwd")
    gate, up, act = _ffn_in(hn, p["in_t"], name="ffn_proj_in")
    y, hn_next = _residual_out(act, p["out"], x, next_row, name="ffn_proj_out")
    return y, (x, hn, gate, up, act), hn_next


def _ffn_layer_bwd(dx, dxb, nrow, p, saved):
    x, hn, gate, up, act = saved
    g_out = _matmul(act, dxb, trans_a=True, out_dtype=MM_DTYPE, name="ffn_gw_out")
    d_gu = _ffn_dact(dxb, p["out"], gate, up, name="ffn_dact")
    dx_new, dxb_new, g_norm = _matmul(d_gu, p["in_t"], a_lead="k", norm_bwd=(x, nrow, dx), name="ffn_dhn_norm")
    g_in_t = _matmul(d_gu, hn, trans_a=True, a_lead="i", out_dtype=MM_DTYPE, name="ffn_gw_in")
    return dx_new, dxb_new, dict(w_in=g_in_t, w_out=g_out, norm=g_norm[0])


def _mixer_fwd(i, x, mix_row, prepared, hn=None, next_row=None):
    if i % 2 == 0:
        return _gdn_layer_fwd(x, mix_row, prepared["gdn"][i // 2], hn, next_row)
    return _dil_layer_fwd(x, mix_row, prepared["dil"][i // 2], hn, next_row)


def _mixer_bwd(i, dx, dxb, mix_row, prepared, saved, zero=0.0):
    if i % 2 == 0:
        p = prepared["gdn"][i // 2]
        return _gdn_layer_bwd(dx, dxb, mix_row, dict(p, nw=p["nw"] + zero), saved)
    p = prepared["dil"][i // 2]
    return _dil_layer_bwd(dx, dxb, mix_row, dict(p, wq=p["wq"] + zero), saved)


def _local_step(x, target, prepared, norm_mix, norm_ffn):
    saved = []
    hn = None
    for i in range(DEPTH):
        after = norm_mix[i + 1].reshape(1, D_MODEL) if i + 1 < DEPTH else None
        x, s_mix, hn = _mixer_fwd(i, x, norm_mix[i].reshape(1, D_MODEL), prepared, hn, norm_ffn[i].reshape(1, D_MODEL))
        x, s_ffn, hn = _ffn_layer_fwd(x, norm_ffn[i].reshape(1, D_MODEL), prepared["ffn"][i], hn, after)
        saved.append((s_mix, s_ffn))
    dx, dxb, loss = _loss_head(x, target, name="loss_head")
    g_mix, g_ffn = [None] * DEPTH, [None] * DEPTH
    for i in reversed(range(DEPTH)):
        s_mix, s_ffn = saved[i]
        dx, dxb, g_ffn[i] = _ffn_layer_bwd(dx, dxb, norm_ffn[i].reshape(1, D_MODEL), prepared["ffn"][i], s_ffn)
        dx, dxb, g_mix[i] = _mixer_bwd(i, dx, dxb, norm_mix[i].reshape(1, D_MODEL), prepared, s_mix)
    return loss[0, 0], dx, _collect_grads(g_mix, g_ffn)


def _collect_grads(g_mix, g_ffn):
    gdn = [g_mix[i] for i in range(0, DEPTH, 2)]
    dil = [g_mix[i] for i in range(1, DEPTH, 2)]
    if any(g is None for g in g_mix + g_ffn):
        pick = lambda gs, key: [None if g is None else g[key] for g in gs]
        return dict(gdn_w_in=pick(gdn, "w_in"), gdn_w_out=pick(gdn, "w_out"), dil_w_in=pick(dil, "w_in"),
                    dil_w_out=pick(dil, "w_out"), ffn_w_in=pick(g_ffn, "w_in"), ffn_w_out=pick(g_ffn, "w_out"))
    grads = dict(
        norm_mix=jnp.stack([g["norm"] for g in g_mix]),
        norm_ffn=jnp.stack([g["norm"] for g in g_ffn]),
        gdn_w_in=[g["w_in"] for g in gdn],
        gdn_conv_w=jnp.stack([g["conv"] for g in gdn]),
        gdn_a_log=jnp.stack([g["a_log"] for g in gdn]),
        gdn_dt_bias=jnp.stack([g["dt_bias"] for g in gdn]),
        gdn_norm_w=jnp.stack([g["norm_w"] for g in gdn]),
        gdn_w_out=[g["w_out"] for g in gdn],
        dil_w_in=[g["w_in"] for g in dil],
        dil_q_norm=jnp.stack([g["q_norm"] for g in dil]),
        dil_k_norm=jnp.stack([g["k_norm"] for g in dil]),
        dil_w_out=[g["w_out"] for g in dil],
        ffn_w_in=[g["w_in"] for g in g_ffn],
        ffn_w_out=[g["w_out"] for g in g_ffn],
    )
    return grads


MESH_ID = pl.DeviceIdType.MESH
ANY_SPACE = pl.BlockSpec(memory_space=pl.ANY)


def _mesh_position():
    return lax.axis_index("x"), lax.axis_index("y"), lax.axis_index("c")


def _flip(pos, k):
    x, y, c = pos
    return (1 - x if k & 4 else x, 1 - y if k & 2 else y, 1 - c if k & 1 else c)


def _linear(pos):
    return 4 * pos[0] + 2 * pos[1] + pos[2]


def _comm_scratch():
    return [pltpu.SemaphoreType.DMA((N_DEV - 1,)), pltpu.SemaphoreType.DMA((N_DEV - 1,)), pltpu.SemaphoreType.DMA(())]


def _all_gather(shard, *, name):
    def body(x_ref, out_ref, send_sems, recv_sems, local_sem):
        me = _mesh_position()
        mine = out_ref.at[_linear(me)]
        local = pltpu.make_async_copy(x_ref, mine, local_sem)
        local.start()
        copies = []
        for k in range(1, N_DEV):
            cp = pltpu.make_async_remote_copy(src_ref=x_ref, dst_ref=mine, send_sem=send_sems.at[k - 1],
                                              recv_sem=recv_sems.at[k - 1], device_id=_flip(me, k), device_id_type=MESH_ID)
            cp.start()
            copies.append(cp)
        for cp in copies:
            cp.wait()
        local.wait()

    return pl.pallas_call(
        body,
        out_shape=jax.ShapeDtypeStruct((N_DEV,) + shard.shape, shard.dtype),
        in_specs=[ANY_SPACE],
        out_specs=ANY_SPACE,
        scratch_shapes=_comm_scratch(),
        name=name,
    )(shard)


def _exchange(parts, *, name):
    def body(p_ref, out_ref, send_sems, recv_sems, local_sem):
        me = _mesh_position()
        mine = out_ref.at[_linear(me)]
        local = pltpu.make_async_copy(p_ref.at[_linear(me)], mine, local_sem)
        local.start()
        copies = []
        for k in range(1, N_DEV):
            peer = _flip(me, k)
            cp = pltpu.make_async_remote_copy(src_ref=p_ref.at[_linear(peer)], dst_ref=mine, send_sem=send_sems.at[k - 1],
                                              recv_sem=recv_sems.at[k - 1], device_id=peer, device_id_type=MESH_ID)
            cp.start()
            copies.append(cp)
        for cp in copies:
            cp.wait()
        local.wait()

    return pl.pallas_call(
        body,
        out_shape=jax.ShapeDtypeStruct(parts.shape, parts.dtype),
        in_specs=[ANY_SPACE],
        out_specs=ANY_SPACE,
        scratch_shapes=_comm_scratch(),
        name=name,
    )(parts)


HBM_SPACE = pl.BlockSpec(memory_space=pltpu.HBM)
SEM_SPACE = pl.BlockSpec(memory_space=pltpu.SEMAPHORE)
DATAFLOW = pltpu.SideEffectType.DATAFLOW_SIDE_EFFECTING


def _split_copies(src_ref, land_ref, send_sems, recv_sems, per_peer):
    me = _mesh_position()
    mine = land_ref.at[_linear(me)]
    copies = []
    for k in range(1, N_DEV):
        peer = _flip(me, k)
        src = src_ref.at[_linear(peer)] if per_peer else src_ref
        copies.append(pltpu.make_async_remote_copy(src_ref=src, dst_ref=mine, send_sem=send_sems.at[k - 1],
                                                   recv_sem=recv_sems.at[k - 1], device_id=peer, device_id_type=MESH_ID))
    return copies


def _travel_start(src, after, *, per_peer, name):
    me = _linear(_mesh_position())
    own = src[me] if per_peer else src
    shape = own.shape
    landing = lax.dynamic_update_slice(lax.empty((N_DEV,) + shape, src.dtype), own[None], (me, 0, 0))

    def body(src_ref, land_ref, after_ref, send_sems, recv_sems, src_thru, land_thru, token):
        for cp in _split_copies(src_ref, land_ref, send_sems, recv_sems, per_peer):
            cp.start()
        token[...] = jnp.zeros_like(token)

    return pl.pallas_call(
        body,
        name=name,
        out_shape=(pltpu.SemaphoreType.DMA((N_DEV - 1,)), pltpu.SemaphoreType.DMA((N_DEV - 1,)),
                   pltpu.HBM(src.shape, src.dtype), pltpu.HBM(landing.shape, landing.dtype),
                   jax.ShapeDtypeStruct((8, LANES), F32)),
        in_specs=(HBM_SPACE, HBM_SPACE, ANY_SPACE),
        out_specs=(SEM_SPACE, SEM_SPACE, HBM_SPACE, HBM_SPACE, pl.BlockSpec(memory_space=pltpu.VMEM)),
        input_output_aliases={0: 2, 1: 3},
        compiler_params=pltpu.CompilerParams(has_side_effects=DATAFLOW),
    )(pltpu.with_memory_space_constraint(src, pltpu.HBM), pltpu.with_memory_space_constraint(landing, pltpu.HBM), after)


def _travel_wait(started, after, *, per_peer, name):
    send_sems, recv_sems, src_thru, land_thru, _ = started

    def body(src_ref, land_ref, send_sems, recv_sems, after_ref, src_dead, got_ref):
        for cp in _split_copies(src_ref, land_ref, send_sems, recv_sems, per_peer):
            cp.wait_send()
            cp.wait_recv()

    return pl.pallas_call(
        body,
        name=name,
        out_shape=(pltpu.HBM(src_thru.shape, src_thru.dtype), pltpu.HBM(land_thru.shape, land_thru.dtype)),
        in_specs=(HBM_SPACE, HBM_SPACE, SEM_SPACE, SEM_SPACE, ANY_SPACE),
        out_specs=(HBM_SPACE, HBM_SPACE),
        input_output_aliases={0: 0, 1: 1},
        compiler_params=pltpu.CompilerParams(has_side_effects=DATAFLOW),
    )(src_thru, land_thru, send_sems, recv_sems, after)[1]


def _adamw(parts, w, m, v, *, name):
    rows, n = w.shape
    tb = _pick(rows, (PACK_ROW_ALIGN, 16))
    c1 = 1.0 - ADAM_B1 ** ADAM_STEP
    c2 = 1.0 - ADAM_B2 ** ADAM_STEP

    def body(p_ref, w_ref, m_ref, v_ref, g_ref, d_ref, nm_ref, nv_ref):
        g = p_ref[0].astype(F32)
        for s in range(1, N_DEV):
            g = g + p_ref[s].astype(F32)
        m_new = ADAM_B1 * m_ref[...] + (1.0 - ADAM_B1) * g
        v_new = ADAM_B2 * v_ref[...] + (1.0 - ADAM_B2) * (g * g)
        m_hat = m_new / c1
        v_hat = v_new / c2
        g_ref[...] = g
        nm_ref[...] = m_new
        nv_ref[...] = v_new
        d_ref[...] = -ADAM_LR * (m_hat / (jnp.sqrt(v_hat) + ADAM_EPS) + ADAM_WD * w_ref[...])

    blk = pl.BlockSpec((tb, n), lambda i: (i, 0))
    return pl.pallas_call(
        body,
        grid=(rows // tb,),
        in_specs=[pl.BlockSpec((N_DEV, tb, n), lambda i: (0, i, 0)), blk, blk, blk],
        out_specs=[blk] * 4,
        out_shape=[jax.ShapeDtypeStruct((rows, n), F32)] * 4,
        compiler_params=_cparams(("parallel",)),
        name=name,
    )(parts, w, m, v)


PACK_WIDTH = 1024
SHARDED = {
    "gdn_w_in": ((2, D_MODEL, GDN_IN_WIDTH), 2),
    "gdn_conv_w": ((2, GDN_CONV, GDN_QKV), 2),
    "gdn_w_out": ((2, GDN_HEADS * GDN_DV, D_MODEL), 1),
    "dil_w_in": ((2, D_MODEL, 3 * len(DIL_GROUPS) * DIL_HEADS * DIL_DH), 2),
    "dil_w_out": ((2, DIL_HEADS * DIL_DH, D_MODEL), 2),
    "ffn_w_in": ((DEPTH, D_MODEL, 2 * FFN_HIDDEN), 2),
    "ffn_w_out": ((DEPTH, FFN_HIDDEN, D_MODEL), 1),
}
REPLICATED = {"norm_mix": (DEPTH, D_MODEL), "norm_ffn": (DEPTH, D_MODEL), "gdn_a_log": (2, GDN_HEADS),
              "gdn_dt_bias": (2, GDN_HEADS), "gdn_norm_w": (2, GDN_DV), "dil_q_norm": (2, DIL_DH), "dil_k_norm": (2, DIL_DH)}
WEIGHT_ORDER = ("norm_mix", "norm_ffn", "gdn_w_in", "gdn_conv_w", "gdn_a_log", "gdn_dt_bias", "gdn_norm_w", "gdn_w_out",
                "dil_w_in", "dil_q_norm", "dil_k_norm", "dil_w_out", "ffn_w_in", "ffn_w_out")
PACK_ROW_ALIGN = 128
PIECE_ALIGN = 16
SMALL_ROWS = 16


def _shard_shape(name):
    shape, axis = SHARDED[name]
    return tuple(s // N_DEV if i == axis else s for i, s in enumerate(shape))


def _shard_rows(name):
    return math.prod(_shard_shape(name)) // PACK_WIDTH


def _split_shards(full, name):
    shape, axis = SHARDED[name]
    split = full.reshape(shape[:axis] + (N_DEV, shape[axis] // N_DEV) + shape[axis + 1:])
    return jnp.moveaxis(split, axis, 0)


def _join_shards(stacked, name):
    shape, axis = SHARDED[name]
    return jnp.moveaxis(stacked, 0, axis).reshape(shape)


COLUMN_SHARDED = ("gdn_w_in", "dil_w_in", "dil_w_out", "ffn_w_in")


def _to_rows(shard, name):
    if name in COLUMN_SHARDED:
        shard = jnp.swapaxes(shard, 1, 2)
    return shard.reshape(-1, PACK_WIDTH)


def _layer_columns(name):
    _, r, c = _shard_shape(name)
    return r if name in COLUMN_SHARDED else c


def _piece_rows(piece, halves=1):
    name, layer = piece
    rows = _shard_rows(name) * halves
    return rows if layer is None else rows // SHARDED[name][0][0]


def _aligned(rows, to=PIECE_ALIGN):
    return -(-rows // to) * to


def _pack_pieces(arrays, total_align=PIECE_ALIGN):
    padded, total = [], 0
    for a in arrays:
        rows = a.shape[-2]
        extra = _aligned(rows) - rows
        if extra:
            a = jnp.pad(a, [(0, 0)] * (a.ndim - 2) + [(0, extra), (0, 0)])
        padded.append(a)
        total += rows + extra
    tail = _aligned(total, total_align) - total
    if tail:
        padded.append(jnp.zeros(padded[0].shape[:-2] + (tail, PACK_WIDTH), padded[0].dtype))
    return jnp.concatenate(padded, axis=-2)


def _piece_offsets(pieces, halves=None):
    out, at = [], 0
    for p in pieces:
        rows = _piece_rows(p, (halves or {}).get(p[0], 1))
        out.append((p, at, rows))
        at += _aligned(rows)
    return out


def _shard_piece_rows(src, piece):
    name, layer = piece
    part = src[name] if layer is None else src[name][layer:layer + 1]
    return _to_rows(part.astype(F32), name)


def _piece_from_rows(rows, piece):
    name, layer = piece
    layers, r, c = _shard_shape(name)
    n_l = layers if layer is None else 1
    if name in COLUMN_SHARDED:
        return jnp.swapaxes(rows.reshape(n_l, c, r), 1, 2)
    return rows.reshape(n_l, r, c)


SMALL_TAIL = tuple(n for n in REPLICATED if n not in ("norm_mix", "norm_ffn"))


def _pack_small(vals):
    tail, at = jnp.zeros((PACK_WIDTH,), F32), 0
    for n in SMALL_TAIL:
        vec = vals[n].astype(F32).reshape(-1)
        tail = tail + jnp.pad(vec, (at, PACK_WIDTH - at - vec.shape[0]))
        at += vec.shape[0]
    buf = jnp.pad(vals["norm_mix"].astype(F32), ((0, SMALL_ROWS - DEPTH), (0, 0)))
    buf = buf + jnp.pad(vals["norm_ffn"].astype(F32), ((8, SMALL_ROWS - 8 - DEPTH), (0, 0)))
    return buf + jnp.pad(tail.reshape(1, PACK_WIDTH), ((SMALL_ROWS - 1, 0), (0, 0)))


def _unpack_small(buf):
    out = {"norm_mix": buf[0:DEPTH], "norm_ffn": buf[8:8 + DEPTH]}
    at = 0
    for n in SMALL_TAIL:
        size = math.prod(REPLICATED[n])
        out[n] = buf[SMALL_ROWS - 1, at:at + size].reshape(REPLICATED[n])
        at += size
    return out


GATHER_FIRST = (("gdn_w_in", 0), ("gdn_conv_w", None), ("gdn_w_out", 0))
GATHER_NEXT = (("ffn_w_in", 0), ("ffn_w_out", 0), ("dil_w_in", 0), ("dil_w_out", 0))
GATHER_LAST = (("ffn_w_in", 1), ("ffn_w_out", 1), ("gdn_w_in", 1), ("gdn_w_out", 1), ("ffn_w_in", 2), ("ffn_w_out", 2),
               ("dil_w_in", 1), ("dil_w_out", 1), ("ffn_w_in", 3), ("ffn_w_out", 3))
EXCHANGE_GROUPS = (
    (("ffn_w_in", 3), ("ffn_w_out", 3), ("dil_w_in", 1), ("dil_w_out", 1),
     ("ffn_w_in", 2), ("ffn_w_out", 2), ("gdn_w_in", 1), ("gdn_w_out", 1)),
    (("ffn_w_in", 1), ("ffn_w_out", 1), ("dil_w_in", 0), ("dil_w_out", 0)),
    (("ffn_w_in", 0), ("ffn_w_out", 0)),
    (("gdn_w_in", 0), ("gdn_w_out", 0), ("gdn_conv_w", None)),
)
EXCHANGE_AFTER = {("mix", 2): 0, ("mix", 1): 1, ("ffn", 0): 2}


def _gather_operand(w, pieces):
    arrays = []
    for n, layer in pieces:
        if layer is None:
            arrays.append(lax.bitcast_convert_type(w[n], BF16).reshape(-1, PACK_WIDTH))
        else:
            arrays.append(_to_rows(w[n][layer:layer + 1].astype(BF16), n))
    return _pack_pieces(arrays)


def _gathered_weights(gathered, pieces, full):
    for (n, layer), at, rows in _piece_offsets(pieces, halves={"gdn_conv_w": 2}):
        block = gathered[:, at:at + rows]
        if layer is None:
            block = lax.bitcast_convert_type(block.reshape((N_DEV,) + _shard_shape(n) + (2,)), F32)
            full[n] = _join_shards(block, n)
        else:
            full.setdefault(n, {})[layer] = block.reshape(-1, _layer_columns(n))
    return full


def _exchange_operand(grads, pieces):
    arrays = []
    for n, layer in pieces:
        if layer is None:
            arrays.append(_split_shards(grads[n], n).astype(BF16).reshape(N_DEV, -1, PACK_WIDTH))
        else:
            arrays.append(grads[n][layer].astype(BF16).reshape(N_DEV, -1, PACK_WIDTH))
    return _pack_pieces(arrays, total_align=PACK_ROW_ALIGN)


def _update_group(received, pieces, w, m, v, *, name):
    packed = [_pack_pieces([_shard_piece_rows(src, p) for p in pieces], total_align=PACK_ROW_ALIGN) for src in (w, m, v)]
    outs = _adamw(received, *packed, name=name)
    return {p: tuple(_piece_from_rows(o[at:at + rows], p) for o in outs) for p, at, rows in _piece_offsets(pieces)}


def kernel(x, norm_mix, norm_ffn, gdn_w_in, gdn_conv_w, gdn_a_log, gdn_dt_bias, gdn_norm_w, gdn_w_out, dil_w_in, dil_q_norm, dil_k_norm, dil_w_out, ffn_w_in, ffn_w_out, loss_target, m_norm_mix, m_norm_ffn, m_gdn_w_in, m_gdn_conv_w, m_gdn_a_log, m_gdn_dt_bias, m_gdn_norm_w, m_gdn_w_out, m_dil_w_in, m_dil_q_norm, m_dil_k_norm, m_dil_w_out, m_ffn_w_in, m_ffn_w_out, v_norm_mix, v_norm_ffn, v_gdn_w_in, v_gdn_conv_w, v_gdn_a_log, v_gdn_dt_bias, v_gdn_norm_w, v_gdn_w_out, v_dil_w_in, v_dil_q_norm, v_dil_k_norm, v_dil_w_out, v_ffn_w_in, v_ffn_w_out):
    w = dict(norm_mix=norm_mix, norm_ffn=norm_ffn, gdn_w_in=gdn_w_in, gdn_conv_w=gdn_conv_w, gdn_a_log=gdn_a_log,
             gdn_dt_bias=gdn_dt_bias, gdn_norm_w=gdn_norm_w, gdn_w_out=gdn_w_out, dil_w_in=dil_w_in, dil_q_norm=dil_q_norm,
             dil_k_norm=dil_k_norm, dil_w_out=dil_w_out, ffn_w_in=ffn_w_in, ffn_w_out=ffn_w_out)
    m = dict(norm_mix=m_norm_mix, norm_ffn=m_norm_ffn, gdn_w_in=m_gdn_w_in, gdn_conv_w=m_gdn_conv_w, gdn_a_log=m_gdn_a_log,
             gdn_dt_bias=m_gdn_dt_bias, gdn_norm_w=m_gdn_norm_w, gdn_w_out=m_gdn_w_out, dil_w_in=m_dil_w_in,
             dil_q_norm=m_dil_q_norm, dil_k_norm=m_dil_k_norm, dil_w_out=m_dil_w_out, ffn_w_in=m_ffn_w_in, ffn_w_out=m_ffn_w_out)
    v = dict(norm_mix=v_norm_mix, norm_ffn=v_norm_ffn, gdn_w_in=v_gdn_w_in, gdn_conv_w=v_gdn_conv_w, gdn_a_log=v_gdn_a_log,
             gdn_dt_bias=v_gdn_dt_bias, gdn_norm_w=v_gdn_norm_w, gdn_w_out=v_gdn_w_out, dil_w_in=v_dil_w_in,
             dil_q_norm=v_dil_q_norm, dil_k_norm=v_dil_k_norm, dil_w_out=v_dil_w_out, ffn_w_in=v_ffn_w_in, ffn_w_out=v_ffn_w_out)
    def row(src, i):
        return src[i].reshape(1, D_MODEL)

    first = _all_gather(_gather_operand(w, GATHER_FIRST), name="weight_all_gather_first")
    next_started = _travel_start(_gather_operand(w, GATHER_NEXT), first, per_peer=False, name="weight_gather_start_next")
    last_started = _travel_start(_gather_operand(w, GATHER_LAST), next_started[4], per_peer=False,
                                 name="weight_gather_start_last")
    full = _gathered_weights(first, GATHER_FIRST, {n: w[n] for n in REPLICATED})
    prepared = dict(gdn=_prepare_gdn(full, layers=(0,)))
    h = x[0]
    saved = [None] * DEPTH
    h, s_mix, hn = _mixer_fwd(0, h, row(norm_mix, 0) + last_started[4][0, 0], prepared, None, row(norm_ffn, 0))
    got = _travel_wait(next_started, h, per_peer=False, name="weight_gather_wait_next")
    full = _gathered_weights(got, GATHER_NEXT, full)
    prepared.update(dil=_prepare_dil(full, layers=(0,)), ffn=_prepare_ffn(full, layers=(0,)))
    for i in range(DEPTH):
        if i > 0:
            h, s_mix, hn = _mixer_fwd(i, h, row(norm_mix, i), prepared, hn, row(norm_ffn, i))
        if i == 1:
            got = _travel_wait(last_started, h, per_peer=False, name="weight_gather_wait_last")
            full = _gathered_weights(got, GATHER_LAST, full)
            prepared["gdn"].update(_prepare_gdn(full, layers=(1,)))
            prepared["dil"].update(_prepare_dil(full, layers=(1,)))
            prepared["ffn"].update(_prepare_ffn(full, layers=(1, 2, 3)))
        h, s_ffn, hn = _ffn_layer_fwd(h, row(norm_ffn, i), prepared["ffn"][i], hn,
                                      row(norm_mix, i + 1) if i + 1 < DEPTH else None)
        saved[i] = (s_mix, s_ffn)
    dx, dxb, loss = _loss_head(h, loss_target[0], name="loss_head")

    g_mix, g_ffn = [None] * DEPTH, [None] * DEPTH
    started = {}

    def travel(group):
        operand = _exchange_operand(_collect_grads(g_mix, g_ffn), EXCHANGE_GROUPS[group])
        started[group] = _travel_start(operand, dx, per_peer=True, name=f"grad_exchange_start_{group}")
        return started[group][4][0, 0]

    zero = 0.0
    for i in reversed(range(DEPTH)):
        s_mix, s_ffn = saved[i]
        dx, dxb, g_ffn[i] = _ffn_layer_bwd(dx, dxb, row(norm_ffn, i) + zero, prepared["ffn"][i], s_ffn)
        zero = travel(EXCHANGE_AFTER[("ffn", i)]) if ("ffn", i) in EXCHANGE_AFTER else 0.0
        dx, dxb, g_mix[i] = _mixer_bwd(i, dx, dxb, row(norm_mix, i), prepared, s_mix, zero)
        zero = travel(EXCHANGE_AFTER[("mix", i)]) if ("mix", i) in EXCHANGE_AFTER else 0.0
    grads = _collect_grads(g_mix, g_ffn)
    received = [_travel_wait(started[g], dx, per_peer=True, name=f"grad_exchange_wait_{g}") for g in sorted(started)]
    received.append(_exchange(_exchange_operand(grads, EXCHANGE_GROUPS[-1]), name="grad_exchange_last"))
    updated = {}
    for g, pieces in enumerate(EXCHANGE_GROUPS):
        updated.update(_update_group(received[g], pieces, w, m, v, name=f"adamw_sharded_{g}"))

    small_parts = _all_gather(_pack_small(grads), name="small_grad_all_gather")
    outs_small = [_unpack_small(o) for o in
                  _adamw(small_parts, _pack_small(w), _pack_small(m), _pack_small(v), name="adamw_replicated")]

    total_loss = lax.psum(loss[0, 0], ("x", "y", "c"))
    result = [total_loss, dx[None]]
    for k in range(4):
        for n in WEIGHT_ORDER:
            if n not in SHARDED:
                result.append(outs_small[k][n])
            elif (n, None) in updated:
                result.append(updated[(n, None)][k])
            else:
                result.append(jnp.concatenate([updated[(n, l)][k] for l in range(SHARDED[n][0][0])], axis=0))
    return tuple(result)
```

```python
import functools
import math

import jax
import jax.numpy as jnp
from jax import lax
from jax.experimental import pallas as pl
from jax.experimental.pallas import tpu as pltpu

F32 = jnp.float32
BF16 = jnp.bfloat16
MM_DTYPE = BF16

N_DEV = 8
D_MODEL = 1024
DEPTH = 4
RMS_EPS = 1e-6
L2_EPS = 1e-6

LANES = 128

GDN_HEADS = 8
GDN_DK = 128
GDN_DV = 128
GDN_CONV = 4
GDN_CHUNK = 128
GDN_QKV = 3 * GDN_HEADS * GDN_DK
GDN_MAIN = GDN_QKV + GDN_HEADS * GDN_DV
GDN_IN_WIDTH = GDN_MAIN + 2 * GDN_HEADS

DIL_GROUPS = ((128, 1), (512, 4), (2048, 16))
DIL_HEADS = 8
DIL_DH = 64
DIL_SPAN = 128
DIL_SLAB = 3 * DIL_HEADS * LANES
ALIBI_MAX_BIAS = 8.0

FFN_HIDDEN = 2816

ADAM_LR = 0.001
ADAM_B1 = 0.9
ADAM_B2 = 0.999
ADAM_EPS = 1e-08
ADAM_WD = 0.01
ADAM_STEP = 10

VMEM_LIMIT = 56 * 1024 * 1024
ROW_TILE = 512
MATMUL_VMEM_BUDGET = 40 * 1024 * 1024
NEG = -1e30


def _cparams(sem):
    return pltpu.CompilerParams(dimension_semantics=sem, vmem_limit_bytes=VMEM_LIMIT)


def _single_pass(a, b, a_dim, b_dim):
    lead = a.ndim - 2
    batch = ((0,), (0,)) if lead else ((), ())
    return lax.dot_general(a.astype(BF16), b.astype(BF16), (((lead + a_dim,), (lead + b_dim,)), batch),
                           preferred_element_type=F32)


def _bdot(a, b):
    return _single_pass(a, b, 1, 0)


def _bdot_nt(a, b):
    return _single_pass(a, b, 1, 1)


def _bdot_tn(a, b):
    return _single_pass(a, b, 0, 0)


def _pick(n, candidates):
    for c in candidates:
        if n % c == 0:
            return c
    raise ValueError(f"no tile for {n}")


HALF = LANES // 2


def _pack_head_pairs(x):
    x = x.astype(F32)
    tiles = [x[:, (2 * i) * LANES:(2 * i + 1) * LANES] + pltpu.roll(x[:, (2 * i + 1) * LANES:(2 * i + 2) * LANES], HALF, 1)
             for i in range(x.shape[1] // (2 * LANES))]
    return tiles[0] if len(tiles) == 1 else jnp.concatenate(tiles, axis=1)


def _spread_head_pairs(y):
    low = lax.broadcasted_iota(jnp.int32, (y.shape[0], LANES), 1) < HALF
    tiles = []
    for i in range(y.shape[1] // LANES):
        pair = y[:, i * LANES:(i + 1) * LANES]
        tiles += [jnp.where(low, pair, 0.0), jnp.where(low, pltpu.roll(pair, HALF, 1), 0.0)]
    return jnp.concatenate(tiles, axis=1)


def _matmul(a, b, *, name, trans_a=False, trans_b=False, b_rows=None, a_lead=None, add=None, out_dtype=F32,
            packed_a=False, spread_out=False, norm_bwd=None, norm_fwd=None):
    if trans_a:
        k_dim, m_dim = a.shape[-2:]
        m_dim = m_dim // 2 if packed_a else m_dim
    else:
        m_dim, k_dim = a.shape[-2:]
        k_dim = k_dim // 2 if packed_a else k_dim
    slab_m, slab_k = m_dim, k_dim
    if a_lead == "k":
        assert not trans_a
        k_dim *= a.shape[0]
    elif a_lead == "i":
        assert trans_a
        m_dim *= a.shape[0]
    b_start, b_size = b_rows if b_rows is not None else (0, b.shape[0])
    if trans_b:
        n_dim, k2 = b_size, b.shape[1]
    else:
        k2, n_dim = b_size, b.shape[1]
    assert k_dim == k2, (a.shape, b.shape, b_rows)
    tn = _pick(n_dim, (1024, 512, 256, 128))
    tm = min(slab_m, 2048, max(512, (1024 * 1024) // tn))
    tm = _pick(slab_m, (tm, 1408, 1024, 512, 256, 128))
    out_bytes = jnp.dtype(out_dtype).itemsize * (2 if spread_out else 1)
    if norm_bwd is not None:
        out_bytes = 4 + 4 + 4 + 2
        tm = min(tm, 512)
    if norm_fwd is not None:
        out_bytes += 2

    def deepest(rows):
        fixed = rows * tn * (2 * out_bytes + 4 + (8 if add is not None else 0))
        fits = lambda c: fixed + 2 * 2 * c * ((2 if packed_a else 1) * rows + tn) <= MATMUL_VMEM_BUDGET
        return _pick(slab_k, tuple(c for c in (3072, 2816, 2048, 1536, 1408, 1024, 512, 256) if fits(c)) + (128,))

    tk = deepest(tm)
    if tm % 1024 == 0 and deepest(tm // 2) > tk:
        tm, tk = tm // 2, deepest(tm // 2)
    nk = k_dim // tk
    has_add = add is not None
    dn = (((0 if trans_a else 1,), (1 if trans_b else 0,)), ((), ()))
    b_tile = tn if trans_b else tk
    assert b_start % b_tile == 0, (b_rows, b_tile)
    b_off = b_start // b_tile

    has_norm = norm_bwd is not None
    also_norm = norm_fwd is not None
    if has_norm or also_norm:
        assert n_dim == tn and not spread_out and not (has_norm and also_norm)

    def body(*refs):
        refs = list(refs)
        a_ref, b_ref = refs[:2]
        add_ref = refs[2] if has_add else None
        rest = refs[2 + has_add:]
        if has_norm:
            x_ref, w_ref, skip_ref, dx_ref, dxb_ref, dw_ref, acc_ref = rest
        elif also_norm:
            w_ref, o_ref, hn_ref, acc_ref = rest
        else:
            o_ref, acc_ref = rest
        a_blk = _pack_head_pairs(a_ref[...]).astype(a_ref.dtype) if packed_a else a_ref[...]
        part = lax.dot_general(a_blk, b_ref[...], dn, preferred_element_type=F32)
        first_rows = pl.program_id(0) == 0

        def finish(total):
            if has_add:
                total = total + add_ref[...]
            if has_norm:
                xf = x_ref[...]
                r = lax.rsqrt(jnp.mean(xf * xf, axis=-1, keepdims=True) + RMS_EPS)
                gw = total * w_ref[...]
                dx = r * gw - xf * (r * r * r * jnp.mean(gw * xf, axis=-1, keepdims=True)) + skip_ref[...]
                dx_ref[...] = dx
                dxb_ref[...] = dx.astype(dxb_ref.dtype)
                rows = jnp.sum(total * xf * r, axis=0, keepdims=True)

                @pl.when(first_rows)
                def _():
                    dw_ref[...] = rows

                @pl.when(jnp.logical_not(first_rows))
                def _():
                    dw_ref[...] += rows
                return
            if spread_out:
                total = _spread_head_pairs(total)
            o_ref[...] = total.astype(out_dtype)
            if also_norm:
                r = lax.rsqrt(jnp.mean(total * total, axis=-1, keepdims=True) + RMS_EPS)
                hn_ref[...] = (total * r * w_ref[...]).astype(hn_ref.dtype)

        if nk == 1:
            finish(part)
        else:
            k = pl.program_id(2)

            @pl.when(k == 0)
            def _():
                acc_ref[...] = part

            @pl.when(k > 0)
            def _():
                acc_ref[...] += part

            @pl.when(k == nk - 1)
            def _():
                finish(acc_ref[...])

    wide = 2 if packed_a else 1
    a_tile = (tk, wide * tm) if trans_a else (tm, wide * tk)
    a_at = (lambda i, j, k: (k, i)) if trans_a else (lambda i, j, k: (i, k))
    if a_lead is None:
        a_spec = pl.BlockSpec(a_tile, a_at)
    elif a_lead == "k":
        per = slab_k // tk
        a_spec = pl.BlockSpec((None,) + a_tile, lambda i, j, k: (k // per, i, k % per))
    elif a_lead == "i":
        per = slab_m // tm
        a_spec = pl.BlockSpec((None,) + a_tile, lambda i, j, k: (i // per, k, i % per))
    else:
        a_spec = pl.BlockSpec((None,) + a_tile, lambda i, j, k: (a_lead,) + a_at(i, j, k))
    if trans_b:
        b_spec = pl.BlockSpec((tn, tk), lambda i, j, k: (j + b_off, k))
    else:
        b_spec = pl.BlockSpec((tk, tn), lambda i, j, k: (k + b_off, j))
    in_specs = [a_spec, b_spec]
    args = [a, b]
    tile = pl.BlockSpec((tm, tn), lambda i, j, k: (i, j))
    if has_add:
        in_specs.append(tile)
        args.append(add)
    scratch = [pltpu.VMEM((tm, tn) if nk > 1 else (8, LANES), F32)]
    if has_norm:
        x, w_row, dskip = norm_bwd
        one = pl.BlockSpec((1, tn), lambda i, j, k: (0, 0))
        return pl.pallas_call(
            body,
            grid=(m_dim // tm, 1, nk),
            in_specs=in_specs + [tile, one, tile],
            out_specs=[tile, tile, one],
            out_shape=[jax.ShapeDtypeStruct((m_dim, n_dim), F32), jax.ShapeDtypeStruct((m_dim, n_dim), MM_DTYPE),
                       jax.ShapeDtypeStruct((1, n_dim), F32)],
            scratch_shapes=scratch,
            compiler_params=_cparams(("arbitrary", "arbitrary", "arbitrary")),
            name=name,
        )(*args, x, w_row, dskip)
    if also_norm:
        return pl.pallas_call(
            body,
            grid=(m_dim // tm, 1, nk),
            in_specs=in_specs + [pl.BlockSpec((1, tn), lambda i, j, k: (0, 0))],
            out_specs=[tile, tile],
            out_shape=[jax.ShapeDtypeStruct((m_dim, n_dim), out_dtype), jax.ShapeDtypeStruct((m_dim, n_dim), MM_DTYPE)],
            scratch_shapes=scratch,
            compiler_params=_cparams(("parallel", "parallel", "arbitrary")),
            name=name,
        )(*args, norm_fwd)
    return pl.pallas_call(
        body,
        grid=(m_dim // tm, n_dim // tn, nk),
        in_specs=in_specs,
        out_specs=pl.BlockSpec((tm, (2 if spread_out else 1) * tn), lambda i, j, k: (i, j)),
        out_shape=jax.ShapeDtypeStruct((m_dim, (2 if spread_out else 1) * n_dim), out_dtype),
        scratch_shapes=scratch,
        compiler_params=_cparams(("parallel", "parallel", "arbitrary")),
        name=name,
    )(*args)


def _rmsnorm_fwd(x, w_row, *, name):
    t, d = x.shape
    tb = min(t, 1024)

    def body(x_ref, w_ref, o_ref):
        xf = x_ref[...]
        r = lax.rsqrt(jnp.mean(xf * xf, axis=-1, keepdims=True) + RMS_EPS)
        o_ref[...] = (xf * r * w_ref[...]).astype(o_ref.dtype)

    return pl.pallas_call(
        body,
        grid=(t // tb,),
        in_specs=[pl.BlockSpec((tb, d), lambda i: (i, 0)), pl.BlockSpec((1, d), lambda i: (0, 0))],
        out_specs=pl.BlockSpec((tb, d), lambda i: (i, 0)),
        out_shape=jax.ShapeDtypeStruct((t, d), MM_DTYPE),
        compiler_params=_cparams(("parallel",)),
        name=name,
    )(x, w_row)


def _silu(z):
    return z / (1.0 + jnp.exp(-z))


FFN_TM, FFN_TN = 1024, 1408


def _ffn_in(hn, in_t, *, name):
    t, d = hn.shape
    h = FFN_HIDDEN
    tm, tn = min(t, FFN_TM), FFN_TN
    nj = h // tn
    dn = (((1,), (1,)), ((), ()))

    def body(a_ref, bg_ref, bu_ref, g_ref, u_ref, act_ref):
        a = a_ref[...]
        g = lax.dot_general(a, bg_ref[...], dn, preferred_element_type=F32)
        u = lax.dot_general(a, bu_ref[...], dn, preferred_element_type=F32)
        g_ref[...] = g.astype(g_ref.dtype)
        u_ref[...] = u.astype(u_ref.dtype)
        act_ref[...] = (_silu(g) * u).astype(act_ref.dtype)

    out = pl.BlockSpec((tm, tn), lambda j, i: (i, j))
    return pl.pallas_call(
        body,
        grid=(nj, t // tm),
        in_specs=[pl.BlockSpec((tm, d), lambda j, i: (i, 0)), pl.BlockSpec((tn, d), lambda j, i: (j, 0)),
                  pl.BlockSpec((tn, d), lambda j, i: (j + nj, 0))],
        out_specs=[out, out, out],
        out_shape=[jax.ShapeDtypeStruct((t, h), MM_DTYPE)] * 3,
        compiler_params=_cparams(("parallel", "parallel")),
        name=name,
    )(hn, in_t, in_t)


def _ffn_dact(dy, out_w, g, u, *, name):
    t, d = dy.shape
    h = FFN_HIDDEN
    tm, tn = min(t, FFN_TM), FFN_TN

    def body(a_ref, b_ref, g_ref, u_ref, d_ref):
        da = lax.dot_general(a_ref[...], b_ref[...], (((1,), (1,)), ((), ())), preferred_element_type=F32)
        gate = g_ref[...].astype(F32)
        sig = 1.0 / (1.0 + jnp.exp(-gate))
        sg = gate * sig
        d_ref[0] = (da * u_ref[...].astype(F32) * (sig + sg * (1.0 - sig))).astype(d_ref.dtype)
        d_ref[1] = (da * sg).astype(d_ref.dtype)

    blk = pl.BlockSpec((tm, tn), lambda j, i: (i, j))
    return pl.pallas_call(
        body,
        grid=(h // tn, t // tm),
        in_specs=[pl.BlockSpec((tm, d), lambda j, i: (i, 0)), pl.BlockSpec((tn, d), lambda j, i: (j, 0)), blk, blk],
        out_specs=pl.BlockSpec((2, tm, tn), lambda j, i: (0, i, j)),
        out_shape=jax.ShapeDtypeStruct((2, t, h), MM_DTYPE),
        compiler_params=_cparams(("parallel", "parallel")),
        name=name,
    )(dy, out_w, g, u)


def _loss_head(y, target, *, name):
    t, d = y.shape
    tb = min(t, 1024)

    def body(y_ref, t_ref, dy_ref, dyb_ref, l_ref):
        err = y_ref[...] - t_ref[...]
        dy_ref[...] = err * (1.0 / d)
        dyb_ref[...] = (err * (1.0 / d)).astype(dyb_ref.dtype)
        part = jnp.sum(jnp.sum(err * err, axis=0, keepdims=True), axis=1, keepdims=True) * (0.5 / d)
        part = jnp.broadcast_to(part, l_ref.shape)

        @pl.when(pl.program_id(0) == 0)
        def _():
            l_ref[...] = part

        @pl.when(pl.program_id(0) > 0)
        def _():
            l_ref[...] += part

    row = pl.BlockSpec((tb, d), lambda i: (i, 0))
    return pl.pallas_call(
        body,
        grid=(t // tb,),
        in_specs=[row, row],
        out_specs=[row, row, pl.BlockSpec((8, LANES), lambda i: (0, 0))],
        out_shape=[jax.ShapeDtypeStruct((t, d), F32), jax.ShapeDtypeStruct((t, d), MM_DTYPE),
                   jax.ShapeDtypeStruct((8, LANES), F32)],
        compiler_params=_cparams(("arbitrary",)),
        name=name,
    )(y, target)


CONV_HALO = 8
CONV_TIME_TILE = 4096


def _conv_tile_scale(c):
    is_qk = c < 2 * GDN_HEADS
    scale = jnp.where(c < GDN_HEADS, GDN_DK ** -0.5, 1.0).astype(F32)
    return is_qk, scale


def _gdn_conv_fwd(pm, conv_w, *, name):
    t = pm.shape[0]
    tb = min(t, CONV_TIME_TILE)
    nt = t // tb
    hb = tb // CONV_HALO

    def body(x_ref, xp_ref, w_ref, o_ref, xe_ref):
        c = pl.program_id(0)
        ti = pl.program_id(1)
        xe_ref[0:CONV_HALO, :] = jnp.where(ti > 0, xp_ref[...], 0.0)
        xe_ref[CONV_HALO:CONV_HALO + tb, :] = x_ref[...]
        w = w_ref[...]
        y = jnp.zeros((tb, LANES), F32)
        for j in range(GDN_CONV):
            off = CONV_HALO - (GDN_CONV - 1) + j
            y = y + w[j:j + 1, :] * xe_ref[pl.ds(off, tb), :]
        s = _silu(y)
        is_qk, scale = _conv_tile_scale(c)
        r = lax.rsqrt(jnp.sum(s * s, axis=-1, keepdims=True) + L2_EPS) * scale
        o_ref[...] = s * jnp.where(is_qk, r, 1.0)

    return pl.pallas_call(
        body,
        grid=(GDN_QKV // LANES, nt),
        in_specs=[
            pl.BlockSpec((tb, LANES), lambda c, i: (i, c)),
            pl.BlockSpec((CONV_HALO, LANES), lambda c, i: (jnp.maximum(i * hb - 1, 0), c)),
            pl.BlockSpec((GDN_CONV, LANES), lambda c, i: (0, c)),
        ],
        out_specs=pl.BlockSpec((tb, LANES), lambda c, i: (i, c)),
        out_shape=jax.ShapeDtypeStruct((t, GDN_QKV), F32),
        scratch_shapes=[pltpu.VMEM((tb + CONV_HALO, LANES), F32)],
        compiler_params=_cparams(("parallel", "parallel")),
        name=name,
    )(pm, pm, conv_w)


def _gdn_conv_bwd(pm, conv_w, dout, *, name):
    t = pm.shape[0]
    tb = min(t, CONV_TIME_TILE)
    nt = t // tb
    hb = tb // CONV_HALO
    last_hb = t // CONV_HALO - 1
    ext = tb + CONV_HALO

    def body(x_ref, xp_ref, xn_ref, d_ref, dn_ref, w_ref, dx_ref, dw_ref, xe_ref, dy_ref):
        c = pl.program_id(0)
        ti = pl.program_id(1)
        has_next = ti < nt - 1
        xe_ref[0:CONV_HALO, :] = jnp.where(ti > 0, xp_ref[...], 0.0)
        xe_ref[CONV_HALO:CONV_HALO + tb, :] = x_ref[...]
        xe_ref[CONV_HALO + tb:2 * CONV_HALO + tb, :] = jnp.where(has_next, xn_ref[...], 0.0)
        de = jnp.concatenate([d_ref[...], jnp.where(has_next, dn_ref[...], 0.0)], axis=0)
        w = w_ref[...]
        y = jnp.zeros((ext, LANES), F32)
        for j in range(GDN_CONV):
            off = CONV_HALO - (GDN_CONV - 1) + j
            y = y + w[j:j + 1, :] * xe_ref[pl.ds(off, ext), :]
        sig = 1.0 / (1.0 + jnp.exp(-y))
        s = y * sig
        is_qk, scale = _conv_tile_scale(c)
        r = lax.rsqrt(jnp.sum(s * s, axis=-1, keepdims=True) + L2_EPS)
        n = s * r
        dnrm = de * scale
        ds_qk = r * (dnrm - n * jnp.sum(dnrm * n, axis=-1, keepdims=True))
        ds = jnp.where(is_qk, ds_qk, de)
        dy_ref[...] = ds * (sig + s * (1.0 - sig))
        dy = dy_ref[0:tb, :]
        dx = jnp.zeros((tb, LANES), F32)
        dw_rows = []
        for j in range(GDN_CONV):
            sh = GDN_CONV - 1 - j
            dx = dx + w[j:j + 1, :] * dy_ref[pl.ds(sh, tb), :]
            off = CONV_HALO - (GDN_CONV - 1) + j
            dw_rows.append(jnp.sum(dy * xe_ref[pl.ds(off, tb), :], axis=0, keepdims=True))
        dx_ref[...] = dx.astype(dx_ref.dtype)
        part = jnp.concatenate(dw_rows, axis=0)

        @pl.when(ti == 0)
        def _():
            dw_ref[...] = part

        @pl.when(ti > 0)
        def _():
            dw_ref[...] += part

    main = pl.BlockSpec((tb, LANES), lambda c, i: (i, c))
    prev = pl.BlockSpec((CONV_HALO, LANES), lambda c, i: (jnp.maximum(i * hb - 1, 0), c))
    nxt = pl.BlockSpec((CONV_HALO, LANES), lambda c, i: (jnp.minimum((i + 1) * hb, last_hb), c))
    return pl.pallas_call(
        body,
        grid=(GDN_QKV // LANES, nt),
        in_specs=[main, prev, nxt, main, nxt, pl.BlockSpec((GDN_CONV, LANES), lambda c, i: (0, c))],
        out_specs=[main, pl.BlockSpec((GDN_CONV, LANES), lambda c, i: (0, c))],
        out_shape=[jax.ShapeDtypeStruct((t, GDN_QKV), MM_DTYPE), jax.ShapeDtypeStruct((GDN_CONV, GDN_QKV), F32)],
        scratch_shapes=[pltpu.VMEM((tb + 2 * CONV_HALO, LANES), F32), pltpu.VMEM((ext, LANES), F32)],
        compiler_params=_cparams(("parallel", "arbitrary")),
        name=name,
    )(pm, pm, pm, dout, dout, conv_w)


def _head_selector(first_col):
    row = lax.broadcasted_iota(jnp.int32, (LANES, GDN_HEADS * LANES), 0)
    col = lax.broadcasted_iota(jnp.int32, (LANES, GDN_HEADS * LANES), 1)
    return (col // LANES + first_col == row).astype(BF16)


def _spread_columns(cols, first_col):
    sel = _head_selector(first_col)
    return sum(_bdot(p, sel) for p in _bf16_pieces(cols))


def _gather_columns(wide, first_col):
    sel = _head_selector(first_col)
    return sum(_bdot_nt(p, sel) for p in _bf16_pieces(wide))


def _softplus(x):
    return jnp.maximum(x, 0.0) + jnp.log(1.0 + jnp.exp(-jnp.abs(x)))


def _gdn_gates_fwd(ab, alog_row, dt_row, *, name):
    t = ab.shape[0]
    tb = min(t, 1024)
    wide = GDN_HEADS * LANES

    def body(ab_ref, al_ref, dt_ref, g_ref, b_ref):
        x = ab_ref[...]
        g_cols = -jnp.exp(al_ref[...]) * _softplus(x + dt_ref[...])
        b_cols = 1.0 / (1.0 + jnp.exp(-x))
        g_ref[...] = _spread_columns(g_cols, 0)
        b_ref[...] = _spread_columns(b_cols, GDN_HEADS)

    row = pl.BlockSpec((tb, LANES), lambda i: (i, 0))
    one = pl.BlockSpec((1, LANES), lambda i: (0, 0))
    out = pl.BlockSpec((tb, wide), lambda i: (i, 0))
    return pl.pallas_call(
        body,
        grid=(t // tb,),
        in_specs=[row, one, one],
        out_specs=[out, out],
        out_shape=[jax.ShapeDtypeStruct((t, wide), F32)] * 2,
        compiler_params=_cparams(("parallel",)),
        name=name,
    )(ab, alog_row, dt_row)


def _gdn_gates_bwd(ab, alog_row, dt_row, dgb, dbb, *, name):
    t = ab.shape[0]
    tb = min(t, 1024)
    wide = GDN_HEADS * LANES

    def body(ab_ref, al_ref, dt_ref, dg_ref, db_ref, dab_ref, dal_ref, ddt_ref):
        x = ab_ref[...]
        lane = lax.broadcasted_iota(jnp.int32, (tb, LANES), 1)
        dg_cols = _gather_columns(dg_ref[...], 0)
        db_cols = _gather_columns(db_ref[...], GDN_HEADS)
        ea = jnp.exp(al_ref[...])
        z = x + dt_ref[...]
        sp = _softplus(z)
        sg = 1.0 / (1.0 + jnp.exp(-z))
        beta = 1.0 / (1.0 + jnp.exp(-x))
        da = jnp.where(lane < GDN_HEADS, dg_cols * (-ea) * sg, 0.0)
        db = jnp.where((lane >= GDN_HEADS) & (lane < 2 * GDN_HEADS), db_cols * beta * (1.0 - beta), 0.0)
        dab_ref[...] = (da + db).astype(dab_ref.dtype)
        p_al = jnp.sum(jnp.where(lane < GDN_HEADS, dg_cols * (-ea) * sp, 0.0), axis=0, keepdims=True)
        p_dt = jnp.sum(da, axis=0, keepdims=True)

        @pl.when(pl.program_id(0) == 0)
        def _():
            dal_ref[...] = p_al
            ddt_ref[...] = p_dt

        @pl.when(pl.program_id(0) > 0)
        def _():
            dal_ref[...] += p_al
            ddt_ref[...] += p_dt

    row = pl.BlockSpec((tb, LANES), lambda i: (i, 0))
    one = pl.BlockSpec((1, LANES), lambda i: (0, 0))
    big = pl.BlockSpec((tb, wide), lambda i: (i, 0))
    return pl.pallas_call(
        body,
        grid=(t // tb,),
        in_specs=[row, one, one, big, big],
        out_specs=[row, one, one],
        out_shape=[jax.ShapeDtypeStruct((t, LANES), MM_DTYPE), jax.ShapeDtypeStruct((1, LANES), F32),
                   jax.ShapeDtypeStruct((1, LANES), F32)],
        compiler_params=_cparams(("arbitrary",)),
        name=name,
    )(ab, alog_row, dt_row, dgb, dbb)


@jax.custom_vjp
def _unit_lower_inverse_rest(n):
    c = n.shape[-1]
    ri = lax.broadcasted_iota(jnp.int32, (c, c), 0)
    ci = lax.broadcasted_iota(jnp.int32, (c, c), 1)
    rest = None
    size = 1
    while size < c:
        joins = ((ri // (2 * size)) == (ci // (2 * size))) & ((ri // size) != (ci // size))
        low = jnp.where(joins, n, 0.0)
        if rest is None:
            rest = -low
        else:
            left = low + _bdot(rest, low)
            rest = rest - (left + _bdot(left, rest))
        size *= 2
    return rest


def _unit_lower_inverse_rest_fwd(n):
    rest = _unit_lower_inverse_rest(n)
    return rest, rest


def _unit_lower_inverse_rest_bwd(rest, ct):
    left = ct + _bdot_tn(rest, ct)
    return (-(left + _bdot_nt(left, rest)),)


_unit_lower_inverse_rest.defvjp(_unit_lower_inverse_rest_fwd, _unit_lower_inverse_rest_bwd)


@jax.custom_vjp
def _known_inverse_rest(n, rest):
    return rest


def _known_inverse_rest_fwd(n, rest):
    return rest, rest


def _known_inverse_rest_bwd(rest, ct):
    return _unit_lower_inverse_rest_bwd(rest, ct) + (jnp.zeros_like(rest),)


_known_inverse_rest.defvjp(_known_inverse_rest_fwd, _known_inverse_rest_bwd)


def _bf16_pieces(x):
    hi = x.astype(BF16)
    r1 = x - hi.astype(F32)
    mid = r1.astype(BF16)
    lo = (r1 - mid.astype(F32)).astype(BF16)
    return hi, mid, lo


def _lower_ones(shape):
    c = shape[-1]
    ri = lax.broadcasted_iota(jnp.int32, (c, c), 0)
    ci = lax.broadcasted_iota(jnp.int32, (c, c), 1)
    return jnp.broadcast_to((ri >= ci).astype(BF16), shape)


@jax.custom_vjp
def _running_sum(x):
    tri = _lower_ones(x.shape)
    return sum(_bdot(tri, p) for p in _bf16_pieces(x))


def _running_sum_fwd(x):
    return _running_sum(x), None


def _running_sum_bwd(_, ct):
    tri = _lower_ones(ct.shape)
    return (sum(_bdot_tn(tri, p) for p in _bf16_pieces(ct)),)


_running_sum.defvjp(_running_sum_fwd, _running_sum_bwd)


def _gdn_prep_math(q, k, v, gb, bb, known_rest=None, with_rest=False):
    c = GDN_CHUNK
    ri = lax.broadcasted_iota(jnp.int32, (c, c), 0)
    ci = lax.broadcasted_iota(jnp.int32, (c, c), 1)
    causal = ri >= ci
    gc = _running_sum(gb)
    decay = jnp.exp(jnp.where(causal, gc - jnp.swapaxes(gc, -1, -2), NEG))
    n = jnp.where(ri > ci, _bdot_nt(k, k) * bb * decay, 0.0)
    rest = _unit_lower_inverse_rest(n) if known_rest is None else _known_inverse_rest(n, known_rest)
    eg = jnp.exp(gc)
    rhs_v = v * bb
    rhs_k = k * bb * eg
    u = rhs_v + _bdot(rest, rhs_v)
    w = rhs_k + _bdot(rest, rhs_k)
    qk = _bdot_nt(q, k) * decay
    qd = q * eg
    last = jnp.sum(jnp.where(ri == c - 1, gc, 0.0), axis=-2, keepdims=True)
    gl = jnp.broadcast_to(last, gc.shape)
    kt = k * jnp.exp(gl - gc)
    cd = jnp.exp(gl)
    return (u, w, qk, qd, kt, cd, rest) if with_rest else (u, w, qk, qd, kt, cd)


def _head_tiles(ref, h):
    return ref[:, h * LANES:(h + 1) * LANES]


def _stack_heads(ref, first=0, heads=GDN_HEADS):
    return jnp.stack([_head_tiles(ref, first + h) for h in range(heads)])


def _store_heads(ref, val, first=0):
    for h in range(val.shape[0]):
        ref[:, (first + h) * LANES:(first + h + 1) * LANES] = val[h].astype(ref.dtype)


def _gdn_prep_fwd(qkv, gb, bb, *, name):
    t = qkv.shape[0]
    c = GDN_CHUNK
    wide = GDN_HEADS * LANES

    def body(q_ref, k_ref, v_ref, g_ref, b_ref, *outs):
        res = _gdn_prep_math(*(_stack_heads(r) for r in (q_ref, k_ref, v_ref, g_ref, b_ref)), with_rest=True)
        for o_ref, val in zip(outs, res):
            _store_heads(o_ref, val)

    blk = lambda off: pl.BlockSpec((c, wide), lambda i: (i, off))
    outs = pl.pallas_call(
        body,
        grid=(t // c,),
        in_specs=[blk(0), blk(1), blk(2), blk(0), blk(0)],
        out_specs=[blk(0)] * 7,
        out_shape=[jax.ShapeDtypeStruct((t, wide), dt)
                   for dt in (F32, MM_DTYPE, MM_DTYPE, MM_DTYPE, MM_DTYPE, F32, MM_DTYPE)],
        compiler_params=_cparams(("parallel",)),
        name=name,
    )(qkv, qkv, qkv, gb, bb)
    return tuple(outs[:6]), outs[6]


def _gdn_prep_bwd(qkv, gb, bb, rest, cts, *, name):
    t = qkv.shape[0]
    c = GDN_CHUNK
    wide = GDN_HEADS * LANES

    def body(q_ref, k_ref, v_ref, g_ref, b_ref, r_ref, c0, c1, c2, c3, c4, c5, dqkv_ref, dg_ref, db_ref):
        prim = tuple(_stack_heads(r) for r in (q_ref, k_ref, v_ref, g_ref, b_ref))
        _, pull = jax.vjp(functools.partial(_gdn_prep_math, known_rest=_stack_heads(r_ref).astype(F32)), *prim)
        dq, dk, dv, dg, db = pull(tuple(_stack_heads(r).astype(F32) for r in (c0, c1, c2, c3, c4, c5)))
        _store_heads(dqkv_ref, dq)
        _store_heads(dqkv_ref, dk, first=GDN_HEADS)
        _store_heads(dqkv_ref, dv, first=2 * GDN_HEADS)
        _store_heads(dg_ref, dg)
        _store_heads(db_ref, db)

    blk = lambda off: pl.BlockSpec((c, wide), lambda i: (i, off))
    return pl.pallas_call(
        body,
        grid=(t // c,),
        in_specs=[blk(0), blk(1), blk(2), blk(0), blk(0)] + [blk(0)] * 7,
        out_specs=[pl.BlockSpec((c, 3 * wide), lambda i: (i, 0)), blk(0), blk(0)],
        out_shape=[jax.ShapeDtypeStruct((t, 3 * wide), F32), jax.ShapeDtypeStruct((t, wide), F32),
                   jax.ShapeDtypeStruct((t, wide), F32)],
        compiler_params=_cparams(("parallel",)),
        name=name,
    )(qkv, qkv, qkv, gb, bb, rest, *cts)


def _gdn_scan_math(s, u, w, qk, qd, kt, cd):
    v_new = u - _bdot(w, s)
    o = _bdot(qd, s) + _bdot(qk, v_new)
    s_new = s * cd + _bdot_tn(kt, v_new)
    return o, s_new


def _gdn_scan_fwd(prep, *, name):
    t = prep[0].shape[0]
    c = GDN_CHUNK
    wide = GDN_HEADS * LANES

    def body(u_ref, w_ref, qk_ref, qd_ref, kt_ref, cd_ref, o_ref, st_ref, s_ref):
        @pl.when(pl.program_id(0) == 0)
        def _():
            s_ref[...] = jnp.zeros_like(s_ref)

        s = _stack_heads(s_ref)
        _store_heads(st_ref, s)
        o, s_new = _gdn_scan_math(s, *(_stack_heads(r).astype(F32) for r in (u_ref, w_ref, qk_ref, qd_ref, kt_ref, cd_ref)))
        _store_heads(o_ref, o)
        _store_heads(s_ref, s_new)

    blk = pl.BlockSpec((c, wide), lambda i: (i, 0))
    return pl.pallas_call(
        body,
        grid=(t // c,),
        in_specs=[blk] * 6,
        out_specs=[blk, blk],
        out_shape=[jax.ShapeDtypeStruct((t, wide), F32)] * 2,
        scratch_shapes=[pltpu.VMEM((GDN_DK, wide), F32)],
        compiler_params=_cparams(("arbitrary",)),
        name=name,
    )(*prep)


def _gdn_scan_bwd(prep, states, do, *, name):
    t = do.shape[0]
    c = GDN_CHUNK
    wide = GDN_HEADS * LANES
    nc = t // c

    def body(u_ref, w_ref, qk_ref, qd_ref, kt_ref, cd_ref, st_ref, do_ref, *rest):
        outs, ds_ref = rest[:6], rest[6]

        @pl.when(pl.program_id(0) == 0)
        def _():
            ds_ref[...] = jnp.zeros_like(ds_ref)

        prim = tuple(_stack_heads(r).astype(F32) for r in (st_ref, u_ref, w_ref, qk_ref, qd_ref, kt_ref, cd_ref))
        _, pull = jax.vjp(_gdn_scan_math, *prim)
        grads = pull((_stack_heads(do_ref), _stack_heads(ds_ref)))
        _store_heads(ds_ref, grads[0])
        for o_ref, val in zip(outs, grads[1:]):
            _store_heads(o_ref, val)

    blk = pl.BlockSpec((c, wide), lambda i: (nc - 1 - i, 0))
    return pl.pallas_call(
        body,
        grid=(nc,),
        in_specs=[blk] * 8,
        out_specs=[blk] * 6,
        out_shape=[jax.ShapeDtypeStruct((t, wide), dt) for dt in (F32, MM_DTYPE, MM_DTYPE, MM_DTYPE, MM_DTYPE, F32)],
        scratch_shapes=[pltpu.VMEM((GDN_DK, wide), F32)],
        compiler_params=_cparams(("arbitrary",)),
        name=name,
    )(*prep, states, do)


def _gdn_outgate_math(o, z, nw):
    r = lax.rsqrt(jnp.mean(o * o, axis=-1, keepdims=True) + RMS_EPS)
    return o * r * nw * _silu(z)


def _gdn_outgate_fwd(o, pm, nw_row, *, name):
    t = o.shape[0]
    tb = min(t, ROW_TILE)
    wide = GDN_HEADS * LANES
    z_at = GDN_QKV // wide

    def body(o_ref, z_ref, nw_ref, y_ref):
        for h in range(GDN_HEADS):
            y = _gdn_outgate_math(_head_tiles(o_ref, h), _head_tiles(z_ref, h), nw_ref[...])
            y_ref[:, h * LANES:(h + 1) * LANES] = y.astype(y_ref.dtype)

    return pl.pallas_call(
        body,
        grid=(t // tb,),
        in_specs=[pl.BlockSpec((tb, wide), lambda i: (i, 0)), pl.BlockSpec((tb, wide), lambda i: (i, z_at)),
                  pl.BlockSpec((1, LANES), lambda i: (0, 0))],
        out_specs=pl.BlockSpec((tb, wide), lambda i: (i, 0)),
        out_shape=jax.ShapeDtypeStruct((t, wide), MM_DTYPE),
        compiler_params=_cparams(("parallel",)),
        name=name,
    )(o, pm, nw_row)


def _gdn_outgate_bwd(o, pm, nw_row, dy, *, name):
    t = o.shape[0]
    tb = min(t, ROW_TILE)
    wide = GDN_HEADS * LANES
    z_at = GDN_QKV // wide

    def body(o_ref, z_ref, nw_ref, dy_ref, do_ref, dz_ref, dnw_ref):
        total = jnp.zeros((1, LANES), F32)
        for h in range(GDN_HEADS):
            _, pull = jax.vjp(_gdn_outgate_math, _head_tiles(o_ref, h), _head_tiles(z_ref, h), nw_ref[...])
            d_o, d_z, d_nw = pull(_head_tiles(dy_ref, h))
            do_ref[:, h * LANES:(h + 1) * LANES] = d_o
            dz_ref[:, h * LANES:(h + 1) * LANES] = d_z.astype(dz_ref.dtype)
            total = total + d_nw

        @pl.when(pl.program_id(0) == 0)
        def _():
            dnw_ref[...] = total

        @pl.when(pl.program_id(0) > 0)
        def _():
            dnw_ref[...] += total

    blk = pl.BlockSpec((tb, wide), lambda i: (i, 0))
    one = pl.BlockSpec((1, LANES), lambda i: (0, 0))
    return pl.pallas_call(
        body,
        grid=(t // tb,),
        in_specs=[blk, pl.BlockSpec((tb, wide), lambda i: (i, z_at)), one, blk],
        out_specs=[blk, blk, one],
        out_shape=[jax.ShapeDtypeStruct((t, wide), F32), jax.ShapeDtypeStruct((t, wide), MM_DTYPE),
                   jax.ShapeDtypeStruct((1, LANES), F32)],
        compiler_params=_cparams(("arbitrary",)),
        name=name,
    )(o, pm, nw_row, dy)


def _rms64(x, w_row):
    return x * lax.rsqrt(jnp.sum(x * x, axis=-1, keepdims=True) * (1.0 / DIL_DH) + RMS_EPS) * w_row


def _alibi_slopes(group):
    head = lax.broadcasted_iota(jnp.int32, (DIL_HEADS, 8, LANES), 0).astype(F32)
    rate = -math.log(2.0) * ALIBI_MAX_BIAS / (len(DIL_GROUPS) * DIL_HEADS)
    slope = jnp.exp(rate * (head + float(group * DIL_HEADS + 1)))
    return jnp.broadcast_to(slope[:, 0:1, :], (DIL_HEADS, DIL_SPAN, LANES))


def _band_logits(qn, kp, kc, slope_d, has_prev):
    qi = lax.broadcasted_iota(jnp.int32, (DIL_SPAN, DIL_SPAN), 0)
    kj = lax.broadcasted_iota(jnp.int32, (DIL_SPAN, DIL_SPAN), 1)
    steps_c = (qi - kj).astype(F32)
    scale = DIL_DH ** -0.5
    sp = _bdot_nt(qn, kp) * scale - slope_d * (steps_c + float(DIL_SPAN))
    sc = _bdot_nt(qn, kc) * scale - slope_d * steps_c
    sp = jnp.where((kj >= qi) & has_prev, sp, NEG)
    sc = jnp.where(kj <= qi, sc, NEG)
    return sp, sc


def _dil_attn_fwd(slab, wq_row, wk_row, *, group, name):
    dilation = DIL_GROUPS[group][1]
    t = slab.shape[0]
    rows = t // dilation
    nlb = rows // DIL_SPAN
    wide = DIL_HEADS * LANES
    view = slab.reshape(rows, dilation * DIL_SLAB)

    def body(q_ref, kc_ref, vc_ref, kp_ref, vp_ref, wq_ref, wk_ref, o_ref):
        has_prev = pl.program_id(1) > 0
        lane = lax.broadcasted_iota(jnp.int32, (DIL_SPAN, LANES), 1)
        qn = _rms64(_stack_heads(q_ref), wq_ref[...])
        kc = _rms64(_stack_heads(kc_ref), wk_ref[...])
        kp = _rms64(_stack_heads(kp_ref), wk_ref[...])
        sp, sc = _band_logits(qn, kp, kc, _alibi_slopes(group) * float(dilation), has_prev)
        m = jnp.maximum(jnp.max(sp, axis=-1, keepdims=True), jnp.max(sc, axis=-1, keepdims=True))
        pp = jnp.exp(sp - m)
        pc = jnp.exp(sc - m)
        l = jnp.sum(pp, axis=-1, keepdims=True) + jnp.sum(pc, axis=-1, keepdims=True)
        o = (_bdot(pp, _stack_heads(vp_ref)) + _bdot(pc, _stack_heads(vc_ref))) / l
        _store_heads(o_ref, jnp.where(lane < DIL_DH, o, m + jnp.log(l)))

    cur = lambda part: pl.BlockSpec((DIL_SPAN, wide), lambda r, i: (i, 3 * r + part))
    prv = lambda part: pl.BlockSpec((DIL_SPAN, wide), lambda r, i: (jnp.maximum(i - 1, 0), 3 * r + part))
    one = pl.BlockSpec((1, LANES), lambda r, i: (0, 0))
    out = pl.pallas_call(
        body,
        grid=(dilation, nlb),
        in_specs=[cur(0), cur(1), cur(2), prv(1), prv(2), one, one],
        out_specs=pl.BlockSpec((DIL_SPAN, wide), lambda r, i: (i, r)),
        out_shape=jax.ShapeDtypeStruct((rows, dilation * wide), F32),
        compiler_params=_cparams(("parallel", "parallel")),
        name=name,
    )(view, view, view, view, view, wq_row, wk_row)
    return out.reshape(t, wide)


def _head_slope(group, head):
    idx = jnp.zeros((8, LANES), F32) + head.astype(F32)
    rate = -math.log(2.0) * ALIBI_MAX_BIAS / (len(DIL_GROUPS) * DIL_HEADS)
    slope = jnp.exp(rate * (idx + float(group * DIL_HEADS + 1)))
    return jnp.broadcast_to(slope[0:1, :], (DIL_SPAN, LANES))


RESIDUE_BATCH = 8


def _take_residues(ref, d, first=0, count=None):
    count = d if count is None else count
    return jnp.stack([ref[pl.ds(first + r, DIL_SPAN, stride=d), :] for r in range(count)])


def _put_residues(ref, val, d, first=0):
    for r in range(val.shape[0]):
        ref[pl.ds(first + r, DIL_SPAN, stride=d), :] = val[r]


def _dil_attn_fwd_strided(slab, wq_row, wk_row, *, group, name):
    d = DIL_GROUPS[group][1]
    t = slab.shape[0]
    span = DIL_SPAN * d
    nsb = t // span

    hs = max(1, RESIDUE_BATCH // d)

    def body(*refs):
        q, kc, vc, kp, vp = (refs[i * hs:(i + 1) * hs] for i in range(5))
        wq_ref, wk_ref, o_ref, spread = refs[5 * hs:]
        has_prev = pl.program_id(0) > 0
        lane = lax.broadcasted_iota(jnp.int32, (DIL_SPAN, LANES), 1)
        nb = min(d, RESIDUE_BATCH)
        for r0 in range(0, d, nb):
            take = lambda group_refs: jnp.concatenate([_take_residues(ref, d, r0, nb) for ref in group_refs])
            slope = jnp.concatenate([jnp.broadcast_to(_head_slope(group, pl.program_id(1) * hs + j) * float(d),
                                                      (nb, DIL_SPAN, LANES)) for j in range(hs)])
            qn = _rms64(take(q), wq_ref[...])
            kcn = _rms64(take(kc), wk_ref[...])
            kpn = _rms64(take(kp), wk_ref[...])
            sp, sc = _band_logits(qn, kpn, kcn, slope, has_prev)
            m = jnp.maximum(jnp.max(sp, axis=-1, keepdims=True), jnp.max(sc, axis=-1, keepdims=True))
            pp = jnp.exp(sp - m)
            pc = jnp.exp(sc - m)
            l = jnp.sum(pp, axis=-1, keepdims=True) + jnp.sum(pc, axis=-1, keepdims=True)
            o = (_bdot(pp, take(vp)) + _bdot(pc, take(vc))) / l
            res = jnp.where(lane < DIL_DH, o, m + jnp.log(l))
            for j in range(hs):
                _put_residues(spread, res[j * nb:(j + 1) * nb], d, r0)
                if r0 + nb == d:
                    o_ref[:, j * LANES:(j + 1) * LANES] = spread[...]

    cur = lambda part, j: pl.BlockSpec((span, LANES), lambda i, h: (i, part * DIL_HEADS + h * hs + j))
    prv = lambda part, j: pl.BlockSpec((span, LANES), lambda i, h: (jnp.maximum(i - 1, 0), part * DIL_HEADS + h * hs + j))
    one = pl.BlockSpec((1, LANES), lambda i, h: (0, 0))
    heads = range(hs)
    in_specs = ([cur(0, j) for j in heads] + [cur(1, j) for j in heads] + [cur(2, j) for j in heads]
                + [prv(1, j) for j in heads] + [prv(2, j) for j in heads] + [one, one])
    return pl.pallas_call(
        body,
        grid=(nsb, DIL_HEADS // hs),
        in_specs=in_specs,
        out_specs=pl.BlockSpec((span, hs * LANES), lambda i, h: (i, h)),
        out_shape=jax.ShapeDtypeStruct((t, DIL_HEADS * LANES), F32),
        scratch_shapes=[pltpu.VMEM((span, LANES), F32)],
        compiler_params=_cparams(("parallel", "parallel")),
        name=name,
    )(*([slab] * (5 * hs)), wq_row, wk_row)


def _dil_attn_bwd_strided(slab, stat, wq_row, wk_row, dwq_in, dwk_in, *, group, name):
    d = DIL_GROUPS[group][1]
    t = slab.shape[0]
    span = DIL_SPAN * d
    nsb = t // span

    hs = max(1, RESIDUE_BATCH // d)

    def body(*refs):
        q_refs, kc_refs, vc_refs, kp_refs, vp_refs, st_refs = (refs[i * hs:(i + 1) * hs] for i in range(6))
        wq_ref, wk_ref, dwq_in_ref, dwk_in_ref, d_ref, dwq_ref, dwk_ref, dk_carry, dv_carry, spread = refs[6 * hs:]
        take = lambda group_refs: jnp.concatenate([_take_residues(ref, d) for ref in group_refs])
        step = pl.program_id(1)
        has_prev = step < nsb - 1
        first = (pl.program_id(0) == 0) & (step == 0)

        @pl.when(step == 0)
        def _():
            dk_carry[...] = jnp.zeros_like(dk_carry)
            dv_carry[...] = jnp.zeros_like(dv_carry)

        @pl.when(first)
        def _():
            dwq_ref[...] = dwq_in_ref[...]
            dwk_ref[...] = dwk_in_ref[...]

        lane = lax.broadcasted_iota(jnp.int32, (DIL_SPAN, LANES), 1)
        scale = DIL_DH ** -0.5
        q_raw = take(q_refs)
        kc_raw = take(kc_refs)
        vc = take(vc_refs)
        kp_raw = take(kp_refs)
        vp = take(vp_refs)
        st = take(st_refs)
        slope = jnp.concatenate([jnp.broadcast_to(_head_slope(group, pl.program_id(0) * hs + j) * float(d),
                                                  (d, DIL_SPAN, LANES)) for j in range(hs)])
        d_o = jnp.where(lane < DIL_DH, st, 0.0)
        lse = jnp.sum(jnp.where(lane == DIL_DH, st, 0.0), axis=-1, keepdims=True)
        delta = jnp.sum(jnp.where(lane == DIL_DH + 1, st, 0.0), axis=-1, keepdims=True)
        qn = _rms64(q_raw, wq_ref[...])
        kc = _rms64(kc_raw, wk_ref[...])
        kp = _rms64(kp_raw, wk_ref[...])
        sp, sc = _band_logits(qn, kp, kc, slope, has_prev)
        pp = jnp.exp(sp - lse)
        pc = jnp.exp(sc - lse)
        dsp = pp * (_bdot_nt(d_o, vp) - delta) * scale
        dsc = pc * (_bdot_nt(d_o, vc) - delta) * scale
        dqn = _bdot(dsp, kp) + _bdot(dsc, kc)
        dkc_n = _bdot_tn(dsc, qn) + dk_carry[...]
        dvc = _bdot_tn(pc, d_o) + dv_carry[...]
        dk_carry[...] = _bdot_tn(dsp, qn)
        dv_carry[...] = _bdot_tn(pp, d_o)
        dq_raw, dwq_rows = _rms64_bwd(q_raw, wq_ref[...], dqn)
        dk_raw, dwk_rows = _rms64_bwd(kc_raw, wk_ref[...], dkc_n)
        for part, val in enumerate((dq_raw, dk_raw, dvc)):
            for j in range(hs):
                _put_residues(spread, val[j * d:(j + 1) * d], d)
                d_ref[part, :, j * LANES:(j + 1) * LANES] = spread[...].astype(d_ref.dtype)
        dwq_ref[...] += jnp.sum(jnp.sum(dwq_rows, axis=0), axis=0, keepdims=True)
        dwk_ref[...] += jnp.sum(jnp.sum(dwk_rows, axis=0), axis=0, keepdims=True)

    at = lambda i: nsb - 1 - i
    cur = lambda part, j: pl.BlockSpec((span, LANES), lambda h, i: (at(i), part * DIL_HEADS + h * hs + j))
    prv = lambda part, j: pl.BlockSpec((span, LANES), lambda h, i: (jnp.maximum(at(i) - 1, 0), part * DIL_HEADS + h * hs + j))
    one = pl.BlockSpec((1, LANES), lambda h, i: (0, 0))
    heads = range(hs)
    in_specs = ([cur(0, j) for j in heads] + [cur(1, j) for j in heads] + [cur(2, j) for j in heads]
                + [prv(1, j) for j in heads] + [prv(2, j) for j in heads] + [cur(0, j) for j in heads] + [one] * 4)
    return pl.pallas_call(
        body,
        grid=(DIL_HEADS // hs, nsb),
        in_specs=in_specs,
        out_specs=[pl.BlockSpec((3, span, hs * LANES), lambda h, i: (0, at(i), h)), one, one],
        out_shape=[jax.ShapeDtypeStruct((3, t, DIL_HEADS * LANES), MM_DTYPE), jax.ShapeDtypeStruct((1, LANES), F32),
                   jax.ShapeDtypeStruct((1, LANES), F32)],
        scratch_shapes=[pltpu.VMEM((hs * d, DIL_SPAN, LANES), F32), pltpu.VMEM((hs * d, DIL_SPAN, LANES), F32),
                        pltpu.VMEM((span, LANES), F32)],
        compiler_params=_cparams(("arbitrary", "arbitrary")),
        name=name,
    )(*([slab] * (5 * hs)), *([stat] * hs), wq_row, wk_row, dwq_in, dwk_in)


def _dil_merge_fwd(oe, *, name):
    t = oe[0].shape[0]
    tb = min(t, ROW_TILE)
    wide = DIL_HEADS * LANES

    def body(e0, e1, e2, y_ref, om_ref):
        lane = lax.broadcasted_iota(jnp.int32, (tb, LANES), 1)
        for h in range(DIL_HEADS):
            es = [_head_tiles(e, h) for e in (e0, e1, e2)]
            lse = [jnp.sum(jnp.where(lane == DIL_DH, e, 0.0), axis=-1, keepdims=True) for e in es]
            top = jnp.maximum(jnp.maximum(lse[0], lse[1]), lse[2])
            joint = top + jnp.log(jnp.exp(lse[0] - top) + jnp.exp(lse[1] - top) + jnp.exp(lse[2] - top))
            o = sum(jnp.exp(l - joint) * e for l, e in zip(lse, es))
            y_ref[:, h * LANES:(h + 1) * LANES] = jnp.where(lane < DIL_DH, o, 0.0).astype(y_ref.dtype)
            om_ref[:, h * LANES:(h + 1) * LANES] = jnp.where(lane < DIL_DH, o, joint)

    blk = pl.BlockSpec((tb, wide), lambda i: (i, 0))
    return pl.pallas_call(
        body,
        grid=(t // tb,),
        in_specs=[blk] * 3,
        out_specs=[blk, blk],
        out_shape=[jax.ShapeDtypeStruct((t, wide), MM_DTYPE), jax.ShapeDtypeStruct((t, wide), F32)],
        compiler_params=_cparams(("parallel",)),
        name=name,
    )(*oe)


def _dil_merge_bwd(dy, om, *, name):
    t = dy.shape[0]
    tb = min(t, ROW_TILE)
    wide = DIL_HEADS * LANES

    def body(dy_ref, om_ref, st_ref):
        lane = lax.broadcasted_iota(jnp.int32, (tb, LANES), 1)
        for h in range(DIL_HEADS):
            d_o = jnp.where(lane < DIL_DH, _head_tiles(dy_ref, h), 0.0)
            om_t = _head_tiles(om_ref, h)
            delta = jnp.sum(d_o * om_t, axis=-1, keepdims=True)
            st_ref[:, h * LANES:(h + 1) * LANES] = jnp.where(
                lane < DIL_DH, d_o, jnp.where(lane == DIL_DH, om_t, jnp.where(lane == DIL_DH + 1, delta, 0.0)))

    blk = pl.BlockSpec((tb, wide), lambda i: (i, 0))
    return pl.pallas_call(
        body,
        grid=(t // tb,),
        in_specs=[blk, blk],
        out_specs=blk,
        out_shape=jax.ShapeDtypeStruct((t, wide), F32),
        compiler_params=_cparams(("parallel",)),
        name=name,
    )(dy, om)


def _rms64_bwd(x, w_row, dy):
    r = lax.rsqrt(jnp.sum(x * x, axis=-1, keepdims=True) * (1.0 / DIL_DH) + RMS_EPS)
    gw = dy * w_row
    dx = r * gw - x * (r * r * r * jnp.sum(gw * x, axis=-1, keepdims=True) * (1.0 / DIL_DH))
    return dx, dy * x * r


def _dil_attn_bwd(slab, stat, wq_row, wk_row, dwq_in, dwk_in, *, group, name):
    dilation = DIL_GROUPS[group][1]
    t = slab.shape[0]
    rows = t // dilation
    nlb = rows // DIL_SPAN
    wide = DIL_HEADS * LANES
    view = slab.reshape(rows, dilation * DIL_SLAB)
    stat_view = stat.reshape(rows, dilation * wide)

    def body(cur_ref, kp_ref, vp_ref, st_ref, wq_ref, wk_ref, dwq_in_ref, dwk_in_ref, d_ref, dwq_ref, dwk_ref,
             dk_carry, dv_carry):
        step = pl.program_id(1)
        has_prev = step < nlb - 1
        first = (pl.program_id(0) == 0) & (step == 0)

        @pl.when(step == 0)
        def _():
            dk_carry[...] = jnp.zeros_like(dk_carry)
            dv_carry[...] = jnp.zeros_like(dv_carry)

        @pl.when(first)
        def _():
            dwq_ref[...] = dwq_in_ref[...]
            dwk_ref[...] = dwk_in_ref[...]

        lane = lax.broadcasted_iota(jnp.int32, (DIL_SPAN, LANES), 1)
        scale = DIL_DH ** -0.5
        q_raw = _stack_heads(cur_ref)
        kc_raw = _stack_heads(cur_ref, first=DIL_HEADS)
        vc = _stack_heads(cur_ref, first=2 * DIL_HEADS)
        kp_raw = _stack_heads(kp_ref)
        vp = _stack_heads(vp_ref)
        st = _stack_heads(st_ref)
        d_o = jnp.where(lane < DIL_DH, st, 0.0)
        lse = jnp.sum(jnp.where(lane == DIL_DH, st, 0.0), axis=-1, keepdims=True)
        delta = jnp.sum(jnp.where(lane == DIL_DH + 1, st, 0.0), axis=-1, keepdims=True)
        qn = _rms64(q_raw, wq_ref[...])
        kc = _rms64(kc_raw, wk_ref[...])
        kp = _rms64(kp_raw, wk_ref[...])
        sp, sc = _band_logits(qn, kp, kc, _alibi_slopes(group) * float(dilation), has_prev)
        pp = jnp.exp(sp - lse)
        pc = jnp.exp(sc - lse)
        dsp = pp * (_bdot_nt(d_o, vp) - delta) * scale
        dsc = pc * (_bdot_nt(d_o, vc) - delta) * scale
        dqn = _bdot(dsp, kp) + _bdot(dsc, kc)
        dkc_n = _bdot_tn(dsc, qn) + _stack_heads(dk_carry)
        dvc = _bdot_tn(pc, d_o) + _stack_heads(dv_carry)
        _store_heads(dk_carry, _bdot_tn(dsp, qn))
        _store_heads(dv_carry, _bdot_tn(pp, d_o))
        dq_raw, dwq_rows = _rms64_bwd(q_raw, wq_ref[...], dqn)
        dk_raw, dwk_rows = _rms64_bwd(kc_raw, wk_ref[...], dkc_n)
        _store_heads(d_ref, dq_raw)
        _store_heads(d_ref, dk_raw, first=DIL_HEADS)
        _store_heads(d_ref, dvc, first=2 * DIL_HEADS)
        dwq_ref[...] += jnp.sum(jnp.sum(dwq_rows, axis=0), axis=0, keepdims=True)
        dwk_ref[...] += jnp.sum(jnp.sum(dwk_rows, axis=0), axis=0, keepdims=True)

    blk_i = lambda i: nlb - 1 - i
    cur = pl.BlockSpec((DIL_SPAN, DIL_SLAB), lambda r, i: (blk_i(i), r))
    prv = lambda part: pl.BlockSpec((DIL_SPAN, wide), lambda r, i: (jnp.maximum(blk_i(i) - 1, 0), 3 * r + part))
    one = pl.BlockSpec((1, LANES), lambda r, i: (0, 0))
    dslab, dwq, dwk = pl.pallas_call(
        body,
        grid=(dilation, nlb),
        in_specs=[cur, prv(1), prv(2), pl.BlockSpec((DIL_SPAN, wide), lambda r, i: (blk_i(i), r)), one, one, one, one],
        out_specs=[cur, one, one],
        out_shape=[jax.ShapeDtypeStruct((rows, dilation * DIL_SLAB), MM_DTYPE), jax.ShapeDtypeStruct((1, LANES), F32),
                   jax.ShapeDtypeStruct((1, LANES), F32)],
        scratch_shapes=[pltpu.VMEM((DIL_SPAN, wide), F32), pltpu.VMEM((DIL_SPAN, wide), F32)],
        compiler_params=_cparams(("arbitrary", "arbitrary")),
        name=name,
    )(view, view, view, stat_view, wq_row, wk_row, dwq_in, dwk_in)
    return dslab.reshape(t, DIL_SLAB), dwq, dwk


def _row(v, width=LANES):
    v = v.astype(F32).reshape(-1)
    return jnp.pad(v, (0, width - v.shape[0])).reshape(1, width)


def _prepare_weights(w):
    return dict(gdn=_prepare_gdn(w), dil=_prepare_dil(w), ffn=_prepare_ffn(w))


def _prepare_gdn(w, layers=range(DEPTH // 2)):
    gdn = {}
    for j in layers:
        wt = w["gdn_w_in"][j]
        gates_t = jnp.pad(wt[GDN_MAIN:], ((0, LANES - 2 * GDN_HEADS), (0, 0)))
        gdn[j] = dict(in_t=wt, gates_t=gates_t, out=w["gdn_w_out"][j], conv=w["gdn_conv_w"][j].astype(F32),
                      alog=_row(w["gdn_a_log"][j]), dt=_row(w["gdn_dt_bias"][j]), nw=_row(w["gdn_norm_w"][j]))
    return gdn


def _prepare_dil(w, layers=range(DEPTH // 2)):
    d = D_MODEL
    dil = {}
    for j in layers:
        wt = w["dil_w_in"][j].reshape(3, len(DIL_GROUPS), DIL_HEADS, DIL_DH, d)
        wg_t = [wt[:, g].reshape(DIL_SLAB // 2, d) for g in range(len(DIL_GROUPS))]
        out_t = jnp.pad(w["dil_w_out"][j].reshape(d, DIL_HEADS, DIL_DH), ((0, 0), (0, 0), (0, LANES - DIL_DH)))
        dil[j] = dict(wg_t=wg_t, out_t=out_t.reshape(d, DIL_HEADS * LANES), wq=_row(w["dil_q_norm"][j]),
                      wk=_row(w["dil_k_norm"][j]))
    return dil


def _prepare_ffn(w, layers=range(DEPTH)):
    return {i: dict(in_t=w["ffn_w_in"][i], out=w["ffn_w_out"][i]) for i in layers}


def _residual_out(a, w, x, next_row, *, name, **kw):
    if next_row is None:
        return _matmul(a, w, add=x, name=name, **kw), None
    return _matmul(a, w, add=x, norm_fwd=next_row, name=name + "_norm", **kw)


def _gdn_layer_fwd(x, nrow, p, hn=None, next_row=None):
    if hn is None:
        hn = _rmsnorm_fwd(x, nrow, name="rmsnorm_fwd")
    pm = _matmul(hn, p["in_t"], trans_b=True, b_rows=(0, GDN_MAIN), name="gdn_proj_main")
    ab = _matmul(hn, p["gates_t"], trans_b=True, name="gdn_proj_gates")
    qkv = _gdn_conv_fwd(pm, p["conv"], name="gdn_conv_fwd")
    gb, bb = _gdn_gates_fwd(ab, p["alog"], p["dt"], name="gdn_gates_fwd")
    prep, rest = _gdn_prep_fwd(qkv, gb, bb, name="gdn_prep_fwd")
    o, states = _gdn_scan_fwd(prep, name="gdn_scan_fwd")
    og = _gdn_outgate_fwd(o, pm, p["nw"], name="gdn_outgate_fwd")
    y, hn_next = _residual_out(og, p["out"], x, next_row, name="gdn_proj_out")
    return y, (x, hn, pm, ab, qkv, gb, bb, prep, rest, states, o, og), hn_next


def _gdn_layer_bwd(dx, dxb, nrow, p, saved):
    x, hn, pm, ab, qkv, gb, bb, prep, rest, states, o, og = saved
    d_og = _matmul(dxb, p["out"], trans_b=True, name="gdn_dgate")
    g_out = _matmul(og, dxb, trans_a=True, out_dtype=MM_DTYPE, name="gdn_gw_out")
    d_o, d_z, d_nw = _gdn_outgate_bwd(o, pm, p["nw"], d_og, name="gdn_outgate_bwd")
    cts = _gdn_scan_bwd(prep, states, d_o, name="gdn_scan_bwd")
    dqkv, dgb, dbb = _gdn_prep_bwd(qkv, gb, bb, rest, cts, name="gdn_prep_bwd")
    d_ab, d_alog, d_dt = _gdn_gates_bwd(ab, p["alog"], p["dt"], dgb, dbb, name="gdn_gates_bwd")
    d_conv, g_conv = _gdn_conv_bwd(pm, p["conv"], dqkv, name="gdn_conv_bwd")
    d_hn = _matmul(d_conv, p["in_t"], b_rows=(0, GDN_QKV), name="gdn_dhn_qkv")
    d_hn = _matmul(d_z, p["in_t"], b_rows=(GDN_QKV, GDN_MAIN - GDN_QKV), add=d_hn, name="gdn_dhn_z")
    dx_new, dxb_new, g_norm = _matmul(d_ab, p["gates_t"], add=d_hn, norm_bwd=(x, nrow, dx), name="gdn_dhn_gates_norm")
    g_in_t = jnp.concatenate([
        _matmul(d_conv, hn, trans_a=True, out_dtype=MM_DTYPE, name="gdn_gw_qkv"),
        _matmul(d_z, hn, trans_a=True, out_dtype=MM_DTYPE, name="gdn_gw_z"),
        _matmul(d_ab, hn, trans_a=True, out_dtype=MM_DTYPE, name="gdn_gw_gates")[:2 * GDN_HEADS],
    ], axis=0)
    grads = dict(w_in=g_in_t, conv=g_conv, a_log=d_alog[0, :GDN_HEADS], dt_bias=d_dt[0, :GDN_HEADS], norm_w=d_nw[0],
                 w_out=g_out, norm=g_norm[0])
    return dx_new, dxb_new, grads


def _dil_layer_fwd(x, nrow, p, hn=None, next_row=None):
    if hn is None:
        hn = _rmsnorm_fwd(x, nrow, name="rmsnorm_fwd")
    slabs = [_matmul(hn, p["wg_t"][g], trans_b=True, spread_out=True, name="dil_proj_in") for g in range(len(DIL_GROUPS))]
    oe = [(_dil_attn_fwd if DIL_GROUPS[g][1] == 1 else _dil_attn_fwd_strided)(
        slabs[g], p["wq"], p["wk"], group=g, name=f"dil_attn_fwd_g{g}") for g in range(len(DIL_GROUPS))]
    y, om = _dil_merge_fwd(oe, name="dil_merge_fwd")
    out, hn_next = _residual_out(y, p["out_t"], x, next_row, trans_b=True, name="dil_proj_out")
    return out, (x, hn, slabs, y, om), hn_next


def _dil_layer_bwd(dx, dxb, nrow, p, saved):
    x, hn, slabs, y, om = saved
    d_y = _matmul(dxb, p["out_t"], name="dil_dmerged")
    g_out_t = _matmul(dxb, y, trans_a=True, out_dtype=MM_DTYPE, name="dil_gw_out")
    g_out_t = g_out_t.reshape(D_MODEL, DIL_HEADS, LANES)[..., :DIL_DH].reshape(D_MODEL, DIL_HEADS * DIL_DH)
    stat = _dil_merge_bwd(d_y, om, name="dil_merge_bwd")
    d_hn = None
    dwq = jnp.zeros((1, LANES), F32)
    dwk = jnp.zeros((1, LANES), F32)
    g_groups = []
    wide = DIL_HEADS * LANES
    for g in range(len(DIL_GROUPS)):
        last = dict(norm_bwd=(x, nrow, dx)) if g == len(DIL_GROUPS) - 1 else {}
        if DIL_GROUPS[g][1] == 1:
            dslab, dwq, dwk = _dil_attn_bwd(slabs[g], stat, p["wq"], p["wk"], dwq, dwk, group=g, name=f"dil_attn_bwd_g{g}")
            d_hn = _matmul(dslab, p["wg_t"][g], packed_a=True, add=d_hn, name="dil_dhn", **last)
            g_w = _matmul(dslab, hn, trans_a=True, packed_a=True, out_dtype=MM_DTYPE, name="dil_gw_in")
        else:
            dparts, dwq, dwk = _dil_attn_bwd_strided(slabs[g], stat, p["wq"], p["wk"], dwq, dwk, group=g,
                                                     name=f"dil_attn_bwd_g{g}")
            d_hn = _matmul(dparts, p["wg_t"][g], a_lead="k", packed_a=True, add=d_hn,
                           name="dil_dhn_parts_norm" if last else "dil_dhn_parts", **last)
            g_w = _matmul(dparts, hn, trans_a=True, a_lead="i", packed_a=True, out_dtype=MM_DTYPE, name="dil_gw_in_parts")
        g_groups.append(g_w.reshape(3, DIL_HEADS, DIL_DH, D_MODEL))
    g_in_t = jnp.stack(g_groups, axis=1).reshape(3 * len(DIL_GROUPS) * DIL_HEADS * DIL_DH, D_MODEL)
    dx_new, dxb_new, g_norm = d_hn
    grads = dict(w_in=g_in_t, q_norm=dwq[0, :DIL_DH], k_norm=dwk[0, :DIL_DH], w_out=g_out_t, norm=g_norm[0])
    return dx_new, dxb_new, grads


def _ffn_layer_fwd(x, nrow, p, hn=None, next_row=None):
    if hn is None:
        hn = _rmsnorm_fwd(x, nrow, name="rmsnorm_fwd")
    gate, up, act = _ffn_in(hn, p["in_t"], name="ffn_proj_in")
    y, hn_next = _residual_out(act, p["out"], x, next_row, name="ffn_proj_out")
    return y, (x, hn, gate, up, act), hn_next


def _ffn_layer_bwd(dx, dxb, nrow, p, saved):
    x, hn, gate, up, act = saved
    g_out = _matmul(act, dxb, trans_a=True, out_dtype=MM_DTYPE, name="ffn_gw_out")
    d_gu = _ffn_dact(dxb, p["out"], gate, up, name="ffn_dact")
    dx_new, dxb_new, g_norm = _matmul(d_gu, p["in_t"], a_lead="k", norm_bwd=(x, nrow, dx), name="ffn_dhn_norm")
    g_in_t = _matmul(d_gu, hn, trans_a=True, a_lead="i", out_dtype=MM_DTYPE, name="ffn_gw_in")
    return dx_new, dxb_new, dict(w_in=g_in_t, w_out=g_out, norm=g_norm[0])


def _mixer_fwd(i, x, mix_row, prepared, hn=None, next_row=None):
    if i % 2 == 0:
        return _gdn_layer_fwd(x, mix_row, prepared["gdn"][i // 2], hn, next_row)
    return _dil_layer_fwd(x, mix_row, prepared["dil"][i // 2], hn, next_row)


def _mixer_bwd(i, dx, dxb, mix_row, prepared, saved, zero=0.0):
    if i % 2 == 0:
        p = prepared["gdn"][i // 2]
        return _gdn_layer_bwd(dx, dxb, mix_row, dict(p, nw=p["nw"] + zero), saved)
    p = prepared["dil"][i // 2]
    return _dil_layer_bwd(dx, dxb, mix_row, dict(p, wq=p["wq"] + zero), saved)


def _local_step(x, target, prepared, norm_mix, norm_ffn):
    saved = []
    hn = None
    for i in range(DEPTH):
        after = norm_mix[i + 1].reshape(1, D_MODEL) if i + 1 < DEPTH else None
        x, s_mix, hn = _mixer_fwd(i, x, norm_mix[i].reshape(1, D_MODEL), prepared, hn, norm_ffn[i].reshape(1, D_MODEL))
        x, s_ffn, hn = _ffn_layer_fwd(x, norm_ffn[i].reshape(1, D_MODEL), prepared["ffn"][i], hn, after)
        saved.append((s_mix, s_ffn))
    dx, dxb, loss = _loss_head(x, target, name="loss_head")
    g_mix, g_ffn = [None] * DEPTH, [None] * DEPTH
    for i in reversed(range(DEPTH)):
        s_mix, s_ffn = saved[i]
        dx, dxb, g_ffn[i] = _ffn_layer_bwd(dx, dxb, norm_ffn[i].reshape(1, D_MODEL), prepared["ffn"][i], s_ffn)
        dx, dxb, g_mix[i] = _mixer_bwd(i, dx, dxb, norm_mix[i].reshape(1, D_MODEL), prepared, s_mix)
    return loss[0, 0], dx, _collect_grads(g_mix, g_ffn)


def _collect_grads(g_mix, g_ffn):
    gdn = [g_mix[i] for i in range(0, DEPTH, 2)]
    dil = [g_mix[i] for i in range(1, DEPTH, 2)]
    if any(g is None for g in g_mix + g_ffn):
        pick = lambda gs, key: [None if g is None else g[key] for g in gs]
        return dict(gdn_w_in=pick(gdn, "w_in"), gdn_w_out=pick(gdn, "w_out"), dil_w_in=pick(dil, "w_in"),
                    dil_w_out=pick(dil, "w_out"), ffn_w_in=pick(g_ffn, "w_in"), ffn_w_out=pick(g_ffn, "w_out"))
    grads = dict(
        norm_mix=jnp.stack([g["norm"] for g in g_mix]),
        norm_ffn=jnp.stack([g["norm"] for g in g_ffn]),
        gdn_w_in=[g["w_in"] for g in gdn],
        gdn_conv_w=jnp.stack([g["conv"] for g in gdn]),
        gdn_a_log=jnp.stack([g["a_log"] for g in gdn]),
        gdn_dt_bias=jnp.stack([g["dt_bias"] for g in gdn]),
        gdn_norm_w=jnp.stack([g["norm_w"] for g in gdn]),
        gdn_w_out=[g["w_out"] for g in gdn],
        dil_w_in=[g["w_in"] for g in dil],
        dil_q_norm=jnp.stack([g["q_norm"] for g in dil]),
        dil_k_norm=jnp.stack([g["k_norm"] for g in dil]),
        dil_w_out=[g["w_out"] for g in dil],
        ffn_w_in=[g["w_in"] for g in g_ffn],
        ffn_w_out=[g["w_out"] for g in g_ffn],
    )
    return grads


MESH_ID = pl.DeviceIdType.MESH
ANY_SPACE = pl.BlockSpec(memory_space=pl.ANY)


def _mesh_position():
    return lax.axis_index("x"), lax.axis_index("y"), lax.axis_index("c")


def _flip(pos, k):
    x, y, c = pos
    return (1 - x if k & 4 else x, 1 - y if k & 2 else y, 1 - c if k & 1 else c)


def _linear(pos):
    return 4 * pos[0] + 2 * pos[1] + pos[2]


def _comm_scratch():
    return [pltpu.SemaphoreType.DMA((N_DEV - 1,)), pltpu.SemaphoreType.DMA((N_DEV - 1,)), pltpu.SemaphoreType.DMA(())]


def _all_gather(shard, *, name):
    def body(x_ref, out_ref, send_sems, recv_sems, local_sem):
        me = _mesh_position()
        mine = out_ref.at[_linear(me)]
        local = pltpu.make_async_copy(x_ref, mine, local_sem)
        local.start()
        copies = []
        for k in range(1, N_DEV):
            cp = pltpu.make_async_remote_copy(src_ref=x_ref, dst_ref=mine, send_sem=send_sems.at[k - 1],
                                              recv_sem=recv_sems.at[k - 1], device_id=_flip(me, k), device_id_type=MESH_ID)
            cp.start()
            copies.append(cp)
        for cp in copies:
            cp.wait()
        local.wait()

    return pl.pallas_call(
        body,
        out_shape=jax.ShapeDtypeStruct((N_DEV,) + shard.shape, shard.dtype),
        in_specs=[ANY_SPACE],
        out_specs=ANY_SPACE,
        scratch_shapes=_comm_scratch(),
        name=name,
    )(shard)


def _exchange(parts, *, name):
    def body(p_ref, out_ref, send_sems, recv_sems, local_sem):
        me = _mesh_position()
        mine = out_ref.at[_linear(me)]
        local = pltpu.make_async_copy(p_ref.at[_linear(me)], mine, local_sem)
        local.start()
        copies = []
        for k in range(1, N_DEV):
            peer = _flip(me, k)
            cp = pltpu.make_async_remote_copy(src_ref=p_ref.at[_linear(peer)], dst_ref=mine, send_sem=send_sems.at[k - 1],
                                              recv_sem=recv_sems.at[k - 1], device_id=peer, device_id_type=MESH_ID)
            cp.start()
            copies.append(cp)
        for cp in copies:
            cp.wait()
        local.wait()

    return pl.pallas_call(
        body,
        out_shape=jax.ShapeDtypeStruct(parts.shape, parts.dtype),
        in_specs=[ANY_SPACE],
        out_specs=ANY_SPACE,
        scratch_shapes=_comm_scratch(),
        name=name,
    )(parts)


HBM_SPACE = pl.BlockSpec(memory_space=pltpu.HBM)
SEM_SPACE = pl.BlockSpec(memory_space=pltpu.SEMAPHORE)
DATAFLOW = pltpu.SideEffectType.DATAFLOW_SIDE_EFFECTING


def _split_copies(src_ref, land_ref, send_sems, recv_sems, per_peer):
    me = _mesh_position()
    mine = land_ref.at[_linear(me)]
    copies = []
    for k in range(1, N_DEV):
        peer = _flip(me, k)
        src = src_ref.at[_linear(peer)] if per_peer else src_ref
        copies.append(pltpu.make_async_remote_copy(src_ref=src, dst_ref=mine, send_sem=send_sems.at[k - 1],
                                                   recv_sem=recv_sems.at[k - 1], device_id=peer, device_id_type=MESH_ID))
    return copies


def _travel_start(src, after, *, per_peer, name):
    me = _linear(_mesh_position())
    own = src[me] if per_peer else src
    shape = own.shape
    landing = lax.dynamic_update_slice(lax.empty((N_DEV,) + shape, src.dtype), own[None], (me, 0, 0))

    def body(src_ref, land_ref, after_ref, send_sems, recv_sems, src_thru, land_thru, token):
        for cp in _split_copies(src_ref, land_ref, send_sems, recv_sems, per_peer):
            cp.start()
        token[...] = jnp.zeros_like(token)

    return pl.pallas_call(
        body,
        name=name,
        out_shape=(pltpu.SemaphoreType.DMA((N_DEV - 1,)), pltpu.SemaphoreType.DMA((N_DEV - 1,)),
                   pltpu.HBM(src.shape, src.dtype), pltpu.HBM(landing.shape, landing.dtype),
                   jax.ShapeDtypeStruct((8, LANES), F32)),
        in_specs=(HBM_SPACE, HBM_SPACE, ANY_SPACE),
        out_specs=(SEM_SPACE, SEM_SPACE, HBM_SPACE, HBM_SPACE, pl.BlockSpec(memory_space=pltpu.VMEM)),
        input_output_aliases={0: 2, 1: 3},
        compiler_params=pltpu.CompilerParams(has_side_effects=DATAFLOW),
    )(pltpu.with_memory_space_constraint(src, pltpu.HBM), pltpu.with_memory_space_constraint(landing, pltpu.HBM), after)


def _travel_wait(started, after, *, per_peer, name):
    send_sems, recv_sems, src_thru, land_thru, _ = started

    def body(src_ref, land_ref, send_sems, recv_sems, after_ref, src_dead, got_ref):
        for cp in _split_copies(src_ref, land_ref, send_sems, recv_sems, per_peer):
            cp.wait_send()
            cp.wait_recv()

    return pl.pallas_call(
        body,
        name=name,
        out_shape=(pltpu.HBM(src_thru.shape, src_thru.dtype), pltpu.HBM(land_thru.shape, land_thru.dtype)),
        in_specs=(HBM_SPACE, HBM_SPACE, SEM_SPACE, SEM_SPACE, ANY_SPACE),
        out_specs=(HBM_SPACE, HBM_SPACE),
        input_output_aliases={0: 0, 1: 1},
        compiler_params=pltpu.CompilerParams(has_side_effects=DATAFLOW),
    )(src_thru, land_thru, send_sems, recv_sems, after)[1]


def _adamw(parts, w, m, v, *, name):
    rows, n = w.shape
    tb = _pick(rows, (PACK_ROW_ALIGN, 16))
    c1 = 1.0 - ADAM_B1 ** ADAM_STEP
    c2 = 1.0 - ADAM_B2 ** ADAM_STEP

    def body(p_ref, w_ref, m_ref, v_ref, g_ref, d_ref, nm_ref, nv_ref):
        g = p_ref[0].astype(F32)
        for s in range(1, N_DEV):
            g = g + p_ref[s].astype(F32)
        m_new = ADAM_B1 * m_ref[...] + (1.0 - ADAM_B1) * g
        v_new = ADAM_B2 * v_ref[...] + (1.0 - ADAM_B2) * (g * g)
        m_hat = m_new / c1
        v_hat = v_new / c2
        g_ref[...] = g
        nm_ref[...] = m_new
        nv_ref[...] = v_new
        d_ref[...] = -ADAM_LR * (m_hat / (jnp.sqrt(v_hat) + ADAM_EPS) + ADAM_WD * w_ref[...])

    blk = pl.BlockSpec((tb, n), lambda i: (i, 0))
    return pl.pallas_call(
        body,
        grid=(rows // tb,),
        in_specs=[pl.BlockSpec((N_DEV, tb, n), lambda i: (0, i, 0)), blk, blk, blk],
        out_specs=[blk] * 4,
        out_shape=[jax.ShapeDtypeStruct((rows, n), F32)] * 4,
        compiler_params=_cparams(("parallel",)),
        name=name,
    )(parts, w, m, v)


PACK_WIDTH = 1024
SHARDED = {
    "gdn_w_in": ((2, D_MODEL, GDN_IN_WIDTH), 2),
    "gdn_conv_w": ((2, GDN_CONV, GDN_QKV), 2),
    "gdn_w_out": ((2, GDN_HEADS * GDN_DV, D_MODEL), 1),
    "dil_w_in": ((2, D_MODEL, 3 * len(DIL_GROUPS) * DIL_HEADS * DIL_DH), 2),
    "dil_w_out": ((2, DIL_HEADS * DIL_DH, D_MODEL), 2),
    "ffn_w_in": ((DEPTH, D_MODEL, 2 * FFN_HIDDEN), 2),
    "ffn_w_out": ((DEPTH, FFN_HIDDEN, D_MODEL), 1),
}
REPLICATED = {"norm_mix": (DEPTH, D_MODEL), "norm_ffn": (DEPTH, D_MODEL), "gdn_a_log": (2, GDN_HEADS),
              "gdn_dt_bias": (2, GDN_HEADS), "gdn_norm_w": (2, GDN_DV), "dil_q_norm": (2, DIL_DH), "dil_k_norm": (2, DIL_DH)}
WEIGHT_ORDER = ("norm_mix", "norm_ffn", "gdn_w_in", "gdn_conv_w", "gdn_a_log", "gdn_dt_bias", "gdn_norm_w", "gdn_w_out",
                "dil_w_in", "dil_q_norm", "dil_k_norm", "dil_w_out", "ffn_w_in", "ffn_w_out")
PACK_ROW_ALIGN = 128
PIECE_ALIGN = 16
SMALL_ROWS = 16


def _shard_shape(name):
    shape, axis = SHARDED[name]
    return tuple(s // N_DEV if i == axis else s for i, s in enumerate(shape))


def _shard_rows(name):
    return math.prod(_shard_shape(name)) // PACK_WIDTH


def _split_shards(full, name):
    shape, axis = SHARDED[name]
    split = full.reshape(shape[:axis] + (N_DEV, shape[axis] // N_DEV) + shape[axis + 1:])
    return jnp.moveaxis(split, axis, 0)


def _join_shards(stacked, name):
    shape, axis = SHARDED[name]
    return jnp.moveaxis(stacked, 0, axis).reshape(shape)


COLUMN_SHARDED = ("gdn_w_in", "dil_w_in", "dil_w_out", "ffn_w_in")


def _to_rows(shard, name):
    if name in COLUMN_SHARDED:
        shard = jnp.swapaxes(shard, 1, 2)
    return shard.reshape(-1, PACK_WIDTH)


def _layer_columns(name):
    _, r, c = _shard_shape(name)
    return r if name in COLUMN_SHARDED else c


def _piece_rows(piece, halves=1):
    name, layer = piece
    rows = _shard_rows(name) * halves
    return rows if layer is None else rows // SHARDED[name][0][0]


def _aligned(rows, to=PIECE_ALIGN):
    return -(-rows // to) * to


def _pack_pieces(arrays, total_align=PIECE_ALIGN):
    padded, total = [], 0
    for a in arrays:
        rows = a.shape[-2]
        extra = _aligned(rows) - rows
        if extra:
            a = jnp.pad(a, [(0, 0)] * (a.ndim - 2) + [(0, extra), (0, 0)])
        padded.append(a)
        total += rows + extra
    tail = _aligned(total, total_align) - total
    if tail:
        padded.append(jnp.zeros(padded[0].shape[:-2] + (tail, PACK_WIDTH), padded[0].dtype))
    return jnp.concatenate(padded, axis=-2)


def _piece_offsets(pieces, halves=None):
    out, at = [], 0
    for p in pieces:
        rows = _piece_rows(p, (halves or {}).get(p[0], 1))
        out.append((p, at, rows))
        at += _aligned(rows)
    return out


def _shard_piece_rows(src, piece):
    name, layer = piece
    part = src[name] if layer is None else src[name][layer:layer + 1]
    return _to_rows(part.astype(F32), name)


def _piece_from_rows(rows, piece):
    name, layer = piece
    layers, r, c = _shard_shape(name)
    n_l = layers if layer is None else 1
    if name in COLUMN_SHARDED:
        return jnp.swapaxes(rows.reshape(n_l, c, r), 1, 2)
    return rows.reshape(n_l, r, c)


SMALL_TAIL = tuple(n for n in REPLICATED if n not in ("norm_mix", "norm_ffn"))


def _pack_small(vals):
    tail, at = jnp.zeros((PACK_WIDTH,), F32), 0
    for n in SMALL_TAIL:
        vec = vals[n].astype(F32).reshape(-1)
        tail = tail + jnp.pad(vec, (at, PACK_WIDTH - at - vec.shape[0]))
        at += vec.shape[0]
    buf = jnp.pad(vals["norm_mix"].astype(F32), ((0, SMALL_ROWS - DEPTH), (0, 0)))
    buf = buf + jnp.pad(vals["norm_ffn"].astype(F32), ((8, SMALL_ROWS - 8 - DEPTH), (0, 0)))
    return buf + jnp.pad(tail.reshape(1, PACK_WIDTH), ((SMALL_ROWS - 1, 0), (0, 0)))


def _unpack_small(buf):
    out = {"norm_mix": buf[0:DEPTH], "norm_ffn": buf[8:8 + DEPTH]}
    at = 0
    for n in SMALL_TAIL:
        size = math.prod(REPLICATED[n])
        out[n] = buf[SMALL_ROWS - 1, at:at + size].reshape(REPLICATED[n])
        at += size
    return out


GATHER_FIRST = (("gdn_w_in", 0), ("gdn_conv_w", None), ("gdn_w_out", 0))
GATHER_NEXT = (("ffn_w_in", 0), ("ffn_w_out", 0), ("dil_w_in", 0), ("dil_w_out", 0))
GATHER_LAST = (("ffn_w_in", 1), ("ffn_w_out", 1), ("gdn_w_in", 1), ("gdn_w_out", 1), ("ffn_w_in", 2), ("ffn_w_out", 2),
               ("dil_w_in", 1), ("dil_w_out", 1), ("ffn_w_in", 3), ("ffn_w_out", 3))
EXCHANGE_GROUPS = (
    (("ffn_w_in", 3), ("ffn_w_out", 3), ("dil_w_in", 1), ("dil_w_out", 1),
     ("ffn_w_in", 2), ("ffn_w_out", 2), ("gdn_w_in", 1), ("gdn_w_out", 1)),
    (("ffn_w_in", 1), ("ffn_w_out", 1), ("dil_w_in", 0), ("dil_w_out", 0)),
    (("ffn_w_in", 0), ("ffn_w_out", 0)),
    (("gdn_w_in", 0), ("gdn_w_out", 0), ("gdn_conv_w", None)),
)
EXCHANGE_AFTER = {("mix", 2): 0, ("mix", 1): 1, ("ffn", 0): 2}


def _gather_operand(w, pieces):
    arrays = []
    for n, layer in pieces:
        if layer is None:
            arrays.append(lax.bitcast_convert_type(w[n], BF16).reshape(-1, PACK_WIDTH))
        else:
            arrays.append(_to_rows(w[n][layer:layer + 1].astype(BF16), n))
    return _pack_pieces(arrays)


def _gathered_weights(gathered, pieces, full):
    for (n, layer), at, rows in _piece_offsets(pieces, halves={"gdn_conv_w": 2}):
        block = gathered[:, at:at + rows]
        if layer is None:
            block = lax.bitcast_convert_type(block.reshape((N_DEV,) + _shard_shape(n) + (2,)), F32)
            full[n] = _join_shards(block, n)
        else:
            full.setdefault(n, {})[layer] = block.reshape(-1, _layer_columns(n))
    return full


def _exchange_operand(grads, pieces):
    arrays = []
    for n, layer in pieces:
        if layer is None:
            arrays.append(_split_shards(grads[n], n).astype(BF16).reshape(N_DEV, -1, PACK_WIDTH))
        else:
            arrays.append(grads[n][layer].astype(BF16).reshape(N_DEV, -1, PACK_WIDTH))
    return _pack_pieces(arrays, total_align=PACK_ROW_ALIGN)


def _update_group(received, pieces, w, m, v, *, name):
    packed = [_pack_pieces([_shard_piece_rows(src, p) for p in pieces], total_align=PACK_ROW_ALIGN) for src in (w, m, v)]
    outs = _adamw(received, *packed, name=name)
    return {p: tuple(_piece_from_rows(o[at:at + rows], p) for o in outs) for p, at, rows in _piece_offsets(pieces)}


def kernel(x, norm_mix, norm_ffn, gdn_w_in, gdn_conv_w, gdn_a_log, gdn_dt_bias, gdn_norm_w, gdn_w_out, dil_w_in, dil_q_norm, dil_k_norm, dil_w_out, ffn_w_in, ffn_w_out, loss_target, m_norm_mix, m_norm_ffn, m_gdn_w_in, m_gdn_conv_w, m_gdn_a_log, m_gdn_dt_bias, m_gdn_norm_w, m_gdn_w_out, m_dil_w_in, m_dil_q_norm, m_dil_k_norm, m_dil_w_out, m_ffn_w_in, m_ffn_w_out, v_norm_mix, v_norm_ffn, v_gdn_w_in, v_gdn_conv_w, v_gdn_a_log, v_gdn_dt_bias, v_gdn_norm_w, v_gdn_w_out, v_dil_w_in, v_dil_q_norm, v_dil_k_norm, v_dil_w_out, v_ffn_w_in, v_ffn_w_out):
    w = dict(norm_mix=norm_mix, norm_ffn=norm_ffn, gdn_w_in=gdn_w_in, gdn_conv_w=gdn_conv_w, gdn_a_log=gdn_a_log,
             gdn_dt_bias=gdn_dt_bias, gdn_norm_w=gdn_norm_w, gdn_w_out=gdn_w_out, dil_w_in=dil_w_in, dil_q_norm=dil_q_norm,
             dil_k_norm=dil_k_norm, dil_w_out=dil_w_out, ffn_w_in=ffn_w_in, ffn_w_out=ffn_w_out)
    m = dict(norm_mix=m_norm_mix, norm_ffn=m_norm_ffn, gdn_w_in=m_gdn_w_in, gdn_conv_w=m_gdn_conv_w, gdn_a_log=m_gdn_a_log,
             gdn_dt_bias=m_gdn_dt_bias, gdn_norm_w=m_gdn_norm_w, gdn_w_out=m_gdn_w_out, dil_w_in=m_dil_w_in,
             dil_q_norm=m_dil_q_norm, dil_k_norm=m_dil_k_norm, dil_w_out=m_dil_w_out, ffn_w_in=m_ffn_w_in, ffn_w_out=m_ffn_w_out)
    v = dict(norm_mix=v_norm_mix, norm_ffn=v_norm_ffn, gdn_w_in=v_gdn_w_in, gdn_conv_w=v_gdn_conv_w, gdn_a_log=v_gdn_a_log,
             gdn_dt_bias=v_gdn_dt_bias, gdn_norm_w=v_gdn_norm_w, gdn_w_out=v_gdn_w_out, dil_w_in=v_dil_w_in,
             dil_q_norm=v_dil_q_norm, dil_k_norm=v_dil_k_norm, dil_w_out=v_dil_w_out, ffn_w_in=v_ffn_w_in, ffn_w_out=v_ffn_w_out)
    def row(src, i):
        return src[i].reshape(1, D_MODEL)

    first = _all_gather(_gather_operand(w, GATHER_FIRST), name="weight_all_gather_first")
    next_started = _travel_start(_gather_operand(w, GATHER_NEXT), first, per_peer=False, name="weight_gather_start_next")
    last_started = _travel_start(_gather_operand(w, GATHER_LAST), next_started[4], per_peer=False,
                                 name="weight_gather_start_last")
    full = _gathered_weights(first, GATHER_FIRST, {n: w[n] for n in REPLICATED})
    prepared = dict(gdn=_prepare_gdn(full, layers=(0,)))
    h = x[0]
    saved = [None] * DEPTH
    h, s_mix, hn = _mixer_fwd(0, h, row(norm_mix, 0) + last_started[4][0, 0], prepared, None, row(norm_ffn, 0))
    got = _travel_wait(next_started, h, per_peer=False, name="weight_gather_wait_next")
    full = _gathered_weights(got, GATHER_NEXT, full)
    prepared.update(dil=_prepare_dil(full, layers=(0,)), ffn=_prepare_ffn(full, layers=(0,)))
    for i in range(DEPTH):
        if i > 0:
            h, s_mix, hn = _mixer_fwd(i, h, row(norm_mix, i), prepared, hn, row(norm_ffn, i))
        if i == 1:
            got = _travel_wait(last_started, h, per_peer=False, name="weight_gather_wait_last")
            full = _gathered_weights(got, GATHER_LAST, full)
            prepared["gdn"].update(_prepare_gdn(full, layers=(1,)))
            prepared["dil"].update(_prepare_dil(full, layers=(1,)))
            prepared["ffn"].update(_prepare_ffn(full, layers=(1, 2, 3)))
        h, s_ffn, hn = _ffn_layer_fwd(h, row(norm_ffn, i), prepared["ffn"][i], hn,
                                      row(norm_mix, i + 1) if i + 1 < DEPTH else None)
        saved[i] = (s_mix, s_ffn)
    dx, dxb, loss = _loss_head(h, loss_target[0], name="loss_head")

    g_mix, g_ffn = [None] * DEPTH, [None] * DEPTH
    started = {}

    def travel(group):
        operand = _exchange_operand(_collect_grads(g_mix, g_ffn), EXCHANGE_GROUPS[group])
        started[group] = _travel_start(operand, dx, per_peer=True, name=f"grad_exchange_start_{group}")
        return started[group][4][0, 0]

    zero = 0.0
    for i in reversed(range(DEPTH)):
        s_mix, s_ffn = saved[i]
        dx, dxb, g_ffn[i] = _ffn_layer_bwd(dx, dxb, row(norm_ffn, i) + zero, prepared["ffn"][i], s_ffn)
        zero = travel(EXCHANGE_AFTER[("ffn", i)]) if ("ffn", i) in EXCHANGE_AFTER else 0.0
        dx, dxb, g_mix[i] = _mixer_bwd(i, dx, dxb, row(norm_mix, i), prepared, s_mix, zero)
        zero = travel(EXCHANGE_AFTER[("mix", i)]) if ("mix", i) in EXCHANGE_AFTER else 0.0
    grads = _collect_grads(g_mix, g_ffn)
    received = [_travel_wait(started[g], dx, per_peer=True, name=f"grad_exchange_wait_{g}") for g in sorted(started)]
    received.append(_exchange(_exchange_operand(grads, EXCHANGE_GROUPS[-1]), name="grad_exchange_last"))
    updated = {}
    for g, pieces in enumerate(EXCHANGE_GROUPS):
        updated.update(_update_group(received[g], pieces, w, m, v, name=f"adamw_sharded_{g}"))

    small_parts = _all_gather(_pack_small(grads), name="small_grad_all_gather")
    outs_small = [_unpack_small(o) for o in
                  _adamw(small_parts, _pack_small(w), _pack_small(m), _pack_small(v), name="adamw_replicated")]

    total_loss = lax.psum(loss[0, 0], ("x", "y", "c"))
    result = [total_loss, dx[None]]
    for k in range(4):
        for n in WEIGHT_ORDER:
            if n not in SHARDED:
                result.append(outs_small[k][n])
            elif (n, None) in updated:
                result.append(updated[(n, None)][k])
            else:
                result.append(jnp.concatenate([updated[(n, l)][k] for l in range(SHARDED[n][0][0])], axis=0))
    return tuple(result)
```

```python
import functools
import math

import jax
import jax.numpy as jnp
from jax import lax
from jax.experimental import pallas as pl
from jax.experimental.pallas import tpu as pltpu

F32 = jnp.float32
BF16 = jnp.bfloat16
MM_DTYPE = BF16

N_DEV = 8
D_MODEL = 1024
DEPTH = 4
RMS_EPS = 1e-6
L2_EPS = 1e-6

LANES = 128

GDN_HEADS = 8
GDN_DK = 128
GDN_DV = 128
GDN_CONV = 4
GDN_CHUNK = 128
GDN_QKV = 3 * GDN_HEADS * GDN_DK
GDN_MAIN = GDN_QKV + GDN_HEADS * GDN_DV
GDN_IN_WIDTH = GDN_MAIN + 2 * GDN_HEADS

DIL_GROUPS = ((128, 1), (512, 4), (2048, 16))
DIL_HEADS = 8
DIL_DH = 64
DIL_SPAN = 128
DIL_SLAB = 3 * DIL_HEADS * LANES
ALIBI_MAX_BIAS = 8.0

FFN_HIDDEN = 2816

ADAM_LR = 0.001
ADAM_B1 = 0.9
ADAM_B2 = 0.999
ADAM_EPS = 1e-08
ADAM_WD = 0.01
ADAM_STEP = 10

VMEM_LIMIT = 56 * 1024 * 1024
ROW_TILE = 1024
MATMUL_VMEM_BUDGET = 40 * 1024 * 1024
NEG = -1e30


def _cparams(sem):
    return pltpu.CompilerParams(dimension_semantics=sem, vmem_limit_bytes=VMEM_LIMIT)


def _single_pass(a, b, a_dim, b_dim):
    lead = a.ndim - 2
    batch = ((0,), (0,)) if lead else ((), ())
    return lax.dot_general(a.astype(BF16), b.astype(BF16), (((lead + a_dim,), (lead + b_dim,)), batch),
                           preferred_element_type=F32)


def _bdot(a, b):
    return _single_pass(a, b, 1, 0)


def _bdot_nt(a, b):
    return _single_pass(a, b, 1, 1)


def _bdot_tn(a, b):
    return _single_pass(a, b, 0, 0)


def _pick(n, candidates):
    for c in candidates:
        if n % c == 0:
            return c
    raise ValueError(f"no tile for {n}")


HALF = LANES // 2


def _pack_head_pairs(x):
    x = x.astype(F32)
    tiles = [x[:, (2 * i) * LANES:(2 * i + 1) * LANES] + pltpu.roll(x[:, (2 * i + 1) * LANES:(2 * i + 2) * LANES], HALF, 1)
             for i in range(x.shape[1] // (2 * LANES))]
    return tiles[0] if len(tiles) == 1 else jnp.concatenate(tiles, axis=1)


def _spread_head_pairs(y):
    low = lax.broadcasted_iota(jnp.int32, (y.shape[0], LANES), 1) < HALF
    tiles = []
    for i in range(y.shape[1] // LANES):
        pair = y[:, i * LANES:(i + 1) * LANES]
        tiles += [jnp.where(low, pair, 0.0), jnp.where(low, pltpu.roll(pair, HALF, 1), 0.0)]
    return jnp.concatenate(tiles, axis=1)


def _matmul(a, b, *, name, trans_a=False, trans_b=False, b_rows=None, a_lead=None, add=None, out_dtype=F32,
            packed_a=False, spread_out=False, norm_bwd=None, norm_fwd=None):
    if trans_a:
        k_dim, m_dim = a.shape[-2:]
        m_dim = m_dim // 2 if packed_a else m_dim
    else:
        m_dim, k_dim = a.shape[-2:]
        k_dim = k_dim // 2 if packed_a else k_dim
    slab_m, slab_k = m_dim, k_dim
    if a_lead == "k":
        assert not trans_a
        k_dim *= a.shape[0]
    elif a_lead == "i":
        assert trans_a
        m_dim *= a.shape[0]
    b_start, b_size = b_rows if b_rows is not None else (0, b.shape[0])
    if trans_b:
        n_dim, k2 = b_size, b.shape[1]
    else:
        k2, n_dim = b_size, b.shape[1]
    assert k_dim == k2, (a.shape, b.shape, b_rows)
    tn = _pick(n_dim, (1024, 512, 256, 128))
    tm = min(slab_m, 2048, max(512, (1024 * 1024) // tn))
    tm = _pick(slab_m, (tm, 1408, 1024, 512, 256, 128))
    out_bytes = jnp.dtype(out_dtype).itemsize * (2 if spread_out else 1)
    if norm_bwd is not None:
        out_bytes = 4 + 4 + 4 + 2
        tm = min(tm, 512)
    if norm_fwd is not None:
        out_bytes += 2

    def deepest(rows):
        fixed = rows * tn * (2 * out_bytes + 4 + (8 if add is not None else 0))
        fits = lambda c: fixed + 2 * 2 * c * ((2 if packed_a else 1) * rows + tn) <= MATMUL_VMEM_BUDGET
        return _pick(slab_k, tuple(c for c in (3072, 2816, 2048, 1536, 1408, 1024, 512, 256) if fits(c)) + (128,))

    tk = deepest(tm)
    if tm % 1024 == 0 and deepest(tm // 2) > tk:
        tm, tk = tm // 2, deepest(tm // 2)
    nk = k_dim // tk
    has_add = add is not None
    dn = (((0 if trans_a else 1,), (1 if trans_b else 0,)), ((), ()))
    b_tile = tn if trans_b else tk
    assert b_start % b_tile == 0, (b_rows, b_tile)
    b_off = b_start // b_tile

    has_norm = norm_bwd is not None
    also_norm = norm_fwd is not None
    if has_norm or also_norm:
        assert n_dim == tn and not spread_out and not (has_norm and also_norm)

    def body(*refs):
        refs = list(refs)
        a_ref, b_ref = refs[:2]
        add_ref = refs[2] if has_add else None
        rest = refs[2 + has_add:]
        if has_norm:
            x_ref, w_ref, skip_ref, dx_ref, dxb_ref, dw_ref, acc_ref = rest
        elif also_norm:
            w_ref, o_ref, hn_ref, acc_ref = rest
        else:
            o_ref, acc_ref = rest
        a_blk = _pack_head_pairs(a_ref[...]).astype(a_ref.dtype) if packed_a else a_ref[...]
        part = lax.dot_general(a_blk, b_ref[...], dn, preferred_element_type=F32)
        first_rows = pl.program_id(0) == 0

        def finish(total):
            if has_add:
                total = total + add_ref[...]
            if has_norm:
                xf = x_ref[...]
                r = lax.rsqrt(jnp.mean(xf * xf, axis=-1, keepdims=True) + RMS_EPS)
                gw = total * w_ref[...]
                dx = r * gw - xf * (r * r * r * jnp.mean(gw * xf, axis=-1, keepdims=True)) + skip_ref[...]
                dx_ref[...] = dx
                dxb_ref[...] = dx.astype(dxb_ref.dtype)
                rows = jnp.sum(total * xf * r, axis=0, keepdims=True)

                @pl.when(first_rows)
                def _():
                    dw_ref[...] = rows

                @pl.when(jnp.logical_not(first_rows))
                def _():
                    dw_ref[...] += rows
                return
            if spread_out:
                total = _spread_head_pairs(total)
            o_ref[...] = total.astype(out_dtype)
            if also_norm:
                r = lax.rsqrt(jnp.mean(total * total, axis=-1, keepdims=True) + RMS_EPS)
                hn_ref[...] = (total * r * w_ref[...]).astype(hn_ref.dtype)

        if nk == 1:
            finish(part)
        else:
            k = pl.program_id(2)

            @pl.when(k == 0)
            def _():
                acc_ref[...] = part

            @pl.when(k > 0)
            def _():
                acc_ref[...] += part

            @pl.when(k == nk - 1)
            def _():
                finish(acc_ref[...])

    wide = 2 if packed_a else 1
    a_tile = (tk, wide * tm) if trans_a else (tm, wide * tk)
    a_at = (lambda i, j, k: (k, i)) if trans_a else (lambda i, j, k: (i, k))
    if a_lead is None:
        a_spec = pl.BlockSpec(a_tile, a_at)
    elif a_lead == "k":
        per = slab_k // tk
        a_spec = pl.BlockSpec((None,) + a_tile, lambda i, j, k: (k // per, i, k % per))
    elif a_lead == "i":
        per = slab_m // tm
        a_spec = pl.BlockSpec((None,) + a_tile, lambda i, j, k: (i // per, k, i % per))
    else:
        a_spec = pl.BlockSpec((None,) + a_tile, lambda i, j, k: (a_lead,) + a_at(i, j, k))
    if trans_b:
        b_spec = pl.BlockSpec((tn, tk), lambda i, j, k: (j + b_off, k))
    else:
        b_spec = pl.BlockSpec((tk, tn), lambda i, j, k: (k + b_off, j))
    in_specs = [a_spec, b_spec]
    args = [a, b]
    tile = pl.BlockSpec((tm, tn), lambda i, j, k: (i, j))
    if has_add:
        in_specs.append(tile)
        args.append(add)
    scratch = [pltpu.VMEM((tm, tn) if nk > 1 else (8, LANES), F32)]
    if has_norm:
        x, w_row, dskip = norm_bwd
        one = pl.BlockSpec((1, tn), lambda i, j, k: (0, 0))
        return pl.pallas_call(
            body,
            grid=(m_dim // tm, 1, nk),
            in_specs=in_specs + [tile, one, tile],
            out_specs=[tile, tile, one],
            out_shape=[jax.ShapeDtypeStruct((m_dim, n_dim), F32), jax.ShapeDtypeStruct((m_dim, n_dim), MM_DTYPE),
                       jax.ShapeDtypeStruct((1, n_dim), F32)],
            scratch_shapes=scratch,
            compiler_params=_cparams(("arbitrary", "arbitrary", "arbitrary")),
            name=name,
        )(*args, x, w_row, dskip)
    if also_norm:
        return pl.pallas_call(
            body,
            grid=(m_dim // tm, 1, nk),
            in_specs=in_specs + [pl.BlockSpec((1, tn), lambda i, j, k: (0, 0))],
            out_specs=[tile, tile],
            out_shape=[jax.ShapeDtypeStruct((m_dim, n_dim), out_dtype), jax.ShapeDtypeStruct((m_dim, n_dim), MM_DTYPE)],
            scratch_shapes=scratch,
            compiler_params=_cparams(("parallel", "parallel", "arbitrary")),
            name=name,
        )(*args, norm_fwd)
    return pl.pallas_call(
        body,
        grid=(m_dim // tm, n_dim // tn, nk),
        in_specs=in_specs,
        out_specs=pl.BlockSpec((tm, (2 if spread_out else 1) * tn), lambda i, j, k: (i, j)),
        out_shape=jax.ShapeDtypeStruct((m_dim, (2 if spread_out else 1) * n_dim), out_dtype),
        scratch_shapes=scratch,
        compiler_params=_cparams(("parallel", "parallel", "arbitrary")),
        name=name,
    )(*args)


def _rmsnorm_fwd(x, w_row, *, name):
    t, d = x.shape
    tb = min(t, 1024)

    def body(x_ref, w_ref, o_ref):
        xf = x_ref[...]
        r = lax.rsqrt(jnp.mean(xf * xf, axis=-1, keepdims=True) + RMS_EPS)
        o_ref[...] = (xf * r * w_ref[...]).astype(o_ref.dtype)

    return pl.pallas_call(
        body,
        grid=(t // tb,),
        in_specs=[pl.BlockSpec((tb, d), lambda i: (i, 0)), pl.BlockSpec((1, d), lambda i: (0, 0))],
        out_specs=pl.BlockSpec((tb, d), lambda i: (i, 0)),
        out_shape=jax.ShapeDtypeStruct((t, d), MM_DTYPE),
        compiler_params=_cparams(("parallel",)),
        name=name,
    )(x, w_row)


def _silu(z):
    return z / (1.0 + jnp.exp(-z))


FFN_TM, FFN_TN = 1024, 1408


def _ffn_in(hn, in_t, *, name):
    t, d = hn.shape
    h = FFN_HIDDEN
    tm, tn = min(t, FFN_TM), FFN_TN
    nj = h // tn
    dn = (((1,), (1,)), ((), ()))

    def body(a_ref, bg_ref, bu_ref, g_ref, u_ref, act_ref):
        a = a_ref[...]
        g = lax.dot_general(a, bg_ref[...], dn, preferred_element_type=F32)
        u = lax.dot_general(a, bu_ref[...], dn, preferred_element_type=F32)
        g_ref[...] = g.astype(g_ref.dtype)
        u_ref[...] = u.astype(u_ref.dtype)
        act_ref[...] = (_silu(g) * u).astype(act_ref.dtype)

    out = pl.BlockSpec((tm, tn), lambda j, i: (i, j))
    return pl.pallas_call(
        body,
        grid=(nj, t // tm),
        in_specs=[pl.BlockSpec((tm, d), lambda j, i: (i, 0)), pl.BlockSpec((tn, d), lambda j, i: (j, 0)),
                  pl.BlockSpec((tn, d), lambda j, i: (j + nj, 0))],
        out_specs=[out, out, out],
        out_shape=[jax.ShapeDtypeStruct((t, h), MM_DTYPE)] * 3,
        compiler_params=_cparams(("parallel", "parallel")),
        name=name,
    )(hn, in_t, in_t)


def _ffn_dact(dy, out_w, g, u, *, name):
    t, d = dy.shape
    h = FFN_HIDDEN
    tm, tn = min(t, FFN_TM), FFN_TN

    def body(a_ref, b_ref, g_ref, u_ref, d_ref):
        da = lax.dot_general(a_ref[...], b_ref[...], (((1,), (1,)), ((), ())), preferred_element_type=F32)
        gate = g_ref[...].astype(F32)
        sig = 1.0 / (1.0 + jnp.exp(-gate))
        sg = gate * sig
        d_ref[0] = (da * u_ref[...].astype(F32) * (sig + sg * (1.0 - sig))).astype(d_ref.dtype)
        d_ref[1] = (da * sg).astype(d_ref.dtype)

    blk = pl.BlockSpec((tm, tn), lambda j, i: (i, j))
    return pl.pallas_call(
        body,
        grid=(h // tn, t // tm),
        in_specs=[pl.BlockSpec((tm, d), lambda j, i: (i, 0)), pl.BlockSpec((tn, d), lambda j, i: (j, 0)), blk, blk],
        out_specs=pl.BlockSpec((2, tm, tn), lambda j, i: (0, i, j)),
        out_shape=jax.ShapeDtypeStruct((2, t, h), MM_DTYPE),
        compiler_params=_cparams(("parallel", "parallel")),
        name=name,
    )(dy, out_w, g, u)


def _loss_head(y, target, *, name):
    t, d = y.shape
    tb = min(t, 1024)

    def body(y_ref, t_ref, dy_ref, dyb_ref, l_ref):
        err = y_ref[...] - t_ref[...]
        dy_ref[...] = err * (1.0 / d)
        dyb_ref[...] = (err * (1.0 / d)).astype(dyb_ref.dtype)
        part = jnp.sum(jnp.sum(err * err, axis=0, keepdims=True), axis=1, keepdims=True) * (0.5 / d)
        part = jnp.broadcast_to(part, l_ref.shape)

        @pl.when(pl.program_id(0) == 0)
        def _():
            l_ref[...] = part

        @pl.when(pl.program_id(0) > 0)
        def _():
            l_ref[...] += part

    row = pl.BlockSpec((tb, d), lambda i: (i, 0))
    return pl.pallas_call(
        body,
        grid=(t // tb,),
        in_specs=[row, row],
        out_specs=[row, row, pl.BlockSpec((8, LANES), lambda i: (0, 0))],
        out_shape=[jax.ShapeDtypeStruct((t, d), F32), jax.ShapeDtypeStruct((t, d), MM_DTYPE),
                   jax.ShapeDtypeStruct((8, LANES), F32)],
        compiler_params=_cparams(("arbitrary",)),
        name=name,
    )(y, target)


CONV_HALO = 8
CONV_TIME_TILE = 4096


def _conv_tile_scale(c):
    is_qk = c < 2 * GDN_HEADS
    scale = jnp.where(c < GDN_HEADS, GDN_DK ** -0.5, 1.0).astype(F32)
    return is_qk, scale


def _gdn_conv_fwd(pm, conv_w, *, name):
    t = pm.shape[0]
    tb = min(t, CONV_TIME_TILE)
    nt = t // tb
    hb = tb // CONV_HALO

    def body(x_ref, xp_ref, w_ref, o_ref, xe_ref):
        c = pl.program_id(0)
        ti = pl.program_id(1)
        xe_ref[0:CONV_HALO, :] = jnp.where(ti > 0, xp_ref[...], 0.0)
        xe_ref[CONV_HALO:CONV_HALO + tb, :] = x_ref[...]
        w = w_ref[...]
        y = jnp.zeros((tb, LANES), F32)
        for j in range(GDN_CONV):
            off = CONV_HALO - (GDN_CONV - 1) + j
            y = y + w[j:j + 1, :] * xe_ref[pl.ds(off, tb), :]
        s = _silu(y)
        is_qk, scale = _conv_tile_scale(c)
        r = lax.rsqrt(jnp.sum(s * s, axis=-1, keepdims=True) + L2_EPS) * scale
        o_ref[...] = s * jnp.where(is_qk, r, 1.0)

    return pl.pallas_call(
        body,
        grid=(GDN_QKV // LANES, nt),
        in_specs=[
            pl.BlockSpec((tb, LANES), lambda c, i: (i, c)),
            pl.BlockSpec((CONV_HALO, LANES), lambda c, i: (jnp.maximum(i * hb - 1, 0), c)),
            pl.BlockSpec((GDN_CONV, LANES), lambda c, i: (0, c)),
        ],
        out_specs=pl.BlockSpec((tb, LANES), lambda c, i: (i, c)),
        out_shape=jax.ShapeDtypeStruct((t, GDN_QKV), F32),
        scratch_shapes=[pltpu.VMEM((tb + CONV_HALO, LANES), F32)],
        compiler_params=_cparams(("parallel", "parallel")),
        name=name,
    )(pm, pm, conv_w)


def _gdn_conv_bwd(pm, conv_w, dout, *, name):
    t = pm.shape[0]
    tb = min(t, CONV_TIME_TILE)
    nt = t // tb
    hb = tb // CONV_HALO
    last_hb = t // CONV_HALO - 1
    ext = tb + CONV_HALO

    def body(x_ref, xp_ref, xn_ref, d_ref, dn_ref, w_ref, dx_ref, dw_ref, xe_ref, dy_ref):
        c = pl.program_id(0)
        ti = pl.program_id(1)
        has_next = ti < nt - 1
        xe_ref[0:CONV_HALO, :] = jnp.where(ti > 0, xp_ref[...], 0.0)
        xe_ref[CONV_HALO:CONV_HALO + tb, :] = x_ref[...]
        xe_ref[CONV_HALO + tb:2 * CONV_HALO + tb, :] = jnp.where(has_next, xn_ref[...], 0.0)
        de = jnp.concatenate([d_ref[...], jnp.where(has_next, dn_ref[...], 0.0)], axis=0)
        w = w_ref[...]
        y = jnp.zeros((ext, LANES), F32)
        for j in range(GDN_CONV):
            off = CONV_HALO - (GDN_CONV - 1) + j
            y = y + w[j:j + 1, :] * xe_ref[pl.ds(off, ext), :]
        sig = 1.0 / (1.0 + jnp.exp(-y))
        s = y * sig
        is_qk, scale = _conv_tile_scale(c)
        r = lax.rsqrt(jnp.sum(s * s, axis=-1, keepdims=True) + L2_EPS)
        n = s * r
        dnrm = de * scale
        ds_qk = r * (dnrm - n * jnp.sum(dnrm * n, axis=-1, keepdims=True))
        ds = jnp.where(is_qk, ds_qk, de)
        dy_ref[...] = ds * (sig + s * (1.0 - sig))
        dy = dy_ref[0:tb, :]
        dx = jnp.zeros((tb, LANES), F32)
        dw_rows = []
        for j in range(GDN_CONV):
            sh = GDN_CONV - 1 - j
            dx = dx + w[j:j + 1, :] * dy_ref[pl.ds(sh, tb), :]
            off = CONV_HALO - (GDN_CONV - 1) + j
            dw_rows.append(jnp.sum(dy * xe_ref[pl.ds(off, tb), :], axis=0, keepdims=True))
        dx_ref[...] = dx.astype(dx_ref.dtype)
        part = jnp.concatenate(dw_rows, axis=0)

        @pl.when(ti == 0)
        def _():
            dw_ref[...] = part

        @pl.when(ti > 0)
        def _():
            dw_ref[...] += part

    main = pl.BlockSpec((tb, LANES), lambda c, i: (i, c))
    prev = pl.BlockSpec((CONV_HALO, LANES), lambda c, i: (jnp.maximum(i * hb - 1, 0), c))
    nxt = pl.BlockSpec((CONV_HALO, LANES), lambda c, i: (jnp.minimum((i + 1) * hb, last_hb), c))
    return pl.pallas_call(
        body,
        grid=(GDN_QKV // LANES, nt),
        in_specs=[main, prev, nxt, main, nxt, pl.BlockSpec((GDN_CONV, LANES), lambda c, i: (0, c))],
        out_specs=[main, pl.BlockSpec((GDN_CONV, LANES), lambda c, i: (0, c))],
        out_shape=[jax.ShapeDtypeStruct((t, GDN_QKV), MM_DTYPE), jax.ShapeDtypeStruct((GDN_CONV, GDN_QKV), F32)],
        scratch_shapes=[pltpu.VMEM((tb + 2 * CONV_HALO, LANES), F32), pltpu.VMEM((ext, LANES), F32)],
        compiler_params=_cparams(("parallel", "arbitrary")),
        name=name,
    )(pm, pm, pm, dout, dout, conv_w)


def _head_selector(first_col):
    row = lax.broadcasted_iota(jnp.int32, (LANES, GDN_HEADS * LANES), 0)
    col = lax.broadcasted_iota(jnp.int32, (LANES, GDN_HEADS * LANES), 1)
    return (col // LANES + first_col == row).astype(BF16)


def _spread_columns(cols, first_col):
    sel = _head_selector(first_col)
    return sum(_bdot(p, sel) for p in _bf16_pieces(cols))


def _gather_columns(wide, first_col):
    sel = _head_selector(first_col)
    return sum(_bdot_nt(p, sel) for p in _bf16_pieces(wide))


def _softplus(x):
    return jnp.maximum(x, 0.0) + jnp.log(1.0 + jnp.exp(-jnp.abs(x)))


def _gdn_gates_fwd(ab, alog_row, dt_row, *, name):
    t = ab.shape[0]
    tb = min(t, 1024)
    wide = GDN_HEADS * LANES

    def body(ab_ref, al_ref, dt_ref, g_ref, b_ref):
        x = ab_ref[...]
        g_cols = -jnp.exp(al_ref[...]) * _softplus(x + dt_ref[...])
        b_cols = 1.0 / (1.0 + jnp.exp(-x))
        g_ref[...] = _spread_columns(g_cols, 0)
        b_ref[...] = _spread_columns(b_cols, GDN_HEADS)

    row = pl.BlockSpec((tb, LANES), lambda i: (i, 0))
    one = pl.BlockSpec((1, LANES), lambda i: (0, 0))
    out = pl.BlockSpec((tb, wide), lambda i: (i, 0))
    return pl.pallas_call(
        body,
        grid=(t // tb,),
        in_specs=[row, one, one],
        out_specs=[out, out],
        out_shape=[jax.ShapeDtypeStruct((t, wide), F32)] * 2,
        compiler_params=_cparams(("parallel",)),
        name=name,
    )(ab, alog_row, dt_row)


def _gdn_gates_bwd(ab, alog_row, dt_row, dgb, dbb, *, name):
    t = ab.shape[0]
    tb = min(t, 1024)
    wide = GDN_HEADS * LANES

    def body(ab_ref, al_ref, dt_ref, dg_ref, db_ref, dab_ref, dal_ref, ddt_ref):
        x = ab_ref[...]
        lane = lax.broadcasted_iota(jnp.int32, (tb, LANES), 1)
        dg_cols = _gather_columns(dg_ref[...], 0)
        db_cols = _gather_columns(db_ref[...], GDN_HEADS)
        ea = jnp.exp(al_ref[...])
        z = x + dt_ref[...]
        sp = _softplus(z)
        sg = 1.0 / (1.0 + jnp.exp(-z))
        beta = 1.0 / (1.0 + jnp.exp(-x))
        da = jnp.where(lane < GDN_HEADS, dg_cols * (-ea) * sg, 0.0)
        db = jnp.where((lane >= GDN_HEADS) & (lane < 2 * GDN_HEADS), db_cols * beta * (1.0 - beta), 0.0)
        dab_ref[...] = (da + db).astype(dab_ref.dtype)
        p_al = jnp.sum(jnp.where(lane < GDN_HEADS, dg_cols * (-ea) * sp, 0.0), axis=0, keepdims=True)
        p_dt = jnp.sum(da, axis=0, keepdims=True)

        @pl.when(pl.program_id(0) == 0)
        def _():
            dal_ref[...] = p_al
            ddt_ref[...] = p_dt

        @pl.when(pl.program_id(0) > 0)
        def _():
            dal_ref[...] += p_al
            ddt_ref[...] += p_dt

    row = pl.BlockSpec((tb, LANES), lambda i: (i, 0))
    one = pl.BlockSpec((1, LANES), lambda i: (0, 0))
    big = pl.BlockSpec((tb, wide), lambda i: (i, 0))
    return pl.pallas_call(
        body,
        grid=(t // tb,),
        in_specs=[row, one, one, big, big],
        out_specs=[row, one, one],
        out_shape=[jax.ShapeDtypeStruct((t, LANES), MM_DTYPE), jax.ShapeDtypeStruct((1, LANES), F32),
                   jax.ShapeDtypeStruct((1, LANES), F32)],
        compiler_params=_cparams(("arbitrary",)),
        name=name,
    )(ab, alog_row, dt_row, dgb, dbb)


@jax.custom_vjp
def _unit_lower_inverse_rest(n):
    c = n.shape[-1]
    ri = lax.broadcasted_iota(jnp.int32, (c, c), 0)
    ci = lax.broadcasted_iota(jnp.int32, (c, c), 1)
    rest = None
    size = 1
    while size < c:
        joins = ((ri // (2 * size)) == (ci // (2 * size))) & ((ri // size) != (ci // size))
        low = jnp.where(joins, n, 0.0)
        if rest is None:
            rest = -low
        else:
            left = low + _bdot(rest, low)
            rest = rest - (left + _bdot(left, rest))
        size *= 2
    return rest


def _unit_lower_inverse_rest_fwd(n):
    rest = _unit_lower_inverse_rest(n)
    return rest, rest


def _unit_lower_inverse_rest_bwd(rest, ct):
    left = ct + _bdot_tn(rest, ct)
    return (-(left + _bdot_nt(left, rest)),)


_unit_lower_inverse_rest.defvjp(_unit_lower_inverse_rest_fwd, _unit_lower_inverse_rest_bwd)


@jax.custom_vjp
def _known_inverse_rest(n, rest):
    return rest


def _known_inverse_rest_fwd(n, rest):
    return rest, rest


def _known_inverse_rest_bwd(rest, ct):
    return _unit_lower_inverse_rest_bwd(rest, ct) + (jnp.zeros_like(rest),)


_known_inverse_rest.defvjp(_known_inverse_rest_fwd, _known_inverse_rest_bwd)


def _bf16_pieces(x):
    hi = x.astype(BF16)
    r1 = x - hi.astype(F32)
    mid = r1.astype(BF16)
    lo = (r1 - mid.astype(F32)).astype(BF16)
    return hi, mid, lo


def _lower_ones(shape):
    c = shape[-1]
    ri = lax.broadcasted_iota(jnp.int32, (c, c), 0)
    ci = lax.broadcasted_iota(jnp.int32, (c, c), 1)
    return jnp.broadcast_to((ri >= ci).astype(BF16), shape)


@jax.custom_vjp
def _running_sum(x):
    tri = _lower_ones(x.shape)
    return sum(_bdot(tri, p) for p in _bf16_pieces(x))


def _running_sum_fwd(x):
    return _running_sum(x), None


def _running_sum_bwd(_, ct):
    tri = _lower_ones(ct.shape)
    return (sum(_bdot_tn(tri, p) for p in _bf16_pieces(ct)),)


_running_sum.defvjp(_running_sum_fwd, _running_sum_bwd)


def _gdn_prep_math(q, k, v, gb, bb, known_rest=None, with_rest=False):
    c = GDN_CHUNK
    ri = lax.broadcasted_iota(jnp.int32, (c, c), 0)
    ci = lax.broadcasted_iota(jnp.int32, (c, c), 1)
    causal = ri >= ci
    gc = _running_sum(gb)
    decay = jnp.exp(jnp.where(causal, gc - jnp.swapaxes(gc, -1, -2), NEG))
    n = jnp.where(ri > ci, _bdot_nt(k, k) * bb * decay, 0.0)
    rest = _unit_lower_inverse_rest(n) if known_rest is None else _known_inverse_rest(n, known_rest)
    eg = jnp.exp(gc)
    rhs_v = v * bb
    rhs_k = k * bb * eg
    u = rhs_v + _bdot(rest, rhs_v)
    w = rhs_k + _bdot(rest, rhs_k)
    qk = _bdot_nt(q, k) * decay
    qd = q * eg
    last = jnp.sum(jnp.where(ri == c - 1, gc, 0.0), axis=-2, keepdims=True)
    gl = jnp.broadcast_to(last, gc.shape)
    kt = k * jnp.exp(gl - gc)
    cd = jnp.exp(gl)
    return (u, w, qk, qd, kt, cd, rest) if with_rest else (u, w, qk, qd, kt, cd)


def _head_tiles(ref, h):
    return ref[:, h * LANES:(h + 1) * LANES]


def _stack_heads(ref, first=0, heads=GDN_HEADS):
    return jnp.stack([_head_tiles(ref, first + h) for h in range(heads)])


def _store_heads(ref, val, first=0):
    for h in range(val.shape[0]):
        ref[:, (first + h) * LANES:(first + h + 1) * LANES] = val[h].astype(ref.dtype)


def _gdn_prep_fwd(qkv, gb, bb, *, name):
    t = qkv.shape[0]
    c = GDN_CHUNK
    wide = GDN_HEADS * LANES

    def body(q_ref, k_ref, v_ref, g_ref, b_ref, *outs):
        res = _gdn_prep_math(*(_stack_heads(r) for r in (q_ref, k_ref, v_ref, g_ref, b_ref)), with_rest=True)
        for o_ref, val in zip(outs, res):
            _store_heads(o_ref, val)

    blk = lambda off: pl.BlockSpec((c, wide), lambda i: (i, off))
    outs = pl.pallas_call(
        body,
        grid=(t // c,),
        in_specs=[blk(0), blk(1), blk(2), blk(0), blk(0)],
        out_specs=[blk(0)] * 7,
        out_shape=[jax.ShapeDtypeStruct((t, wide), dt)
                   for dt in (F32, MM_DTYPE, MM_DTYPE, MM_DTYPE, MM_DTYPE, F32, MM_DTYPE)],
        compiler_params=_cparams(("parallel",)),
        name=name,
    )(qkv, qkv, qkv, gb, bb)
    return tuple(outs[:6]), outs[6]


def _gdn_prep_bwd(qkv, gb, bb, rest, cts, *, name):
    t = qkv.shape[0]
    c = GDN_CHUNK
    wide = GDN_HEADS * LANES

    def body(q_ref, k_ref, v_ref, g_ref, b_ref, r_ref, c0, c1, c2, c3, c4, c5, dqkv_ref, dg_ref, db_ref):
        prim = tuple(_stack_heads(r) for r in (q_ref, k_ref, v_ref, g_ref, b_ref))
        _, pull = jax.vjp(functools.partial(_gdn_prep_math, known_rest=_stack_heads(r_ref).astype(F32)), *prim)
        dq, dk, dv, dg, db = pull(tuple(_stack_heads(r).astype(F32) for r in (c0, c1, c2, c3, c4, c5)))
        _store_heads(dqkv_ref, dq)
        _store_heads(dqkv_ref, dk, first=GDN_HEADS)
        _store_heads(dqkv_ref, dv, first=2 * GDN_HEADS)
        _store_heads(dg_ref, dg)
        _store_heads(db_ref, db)

    blk = lambda off: pl.BlockSpec((c, wide), lambda i: (i, off))
    return pl.pallas_call(
        body,
        grid=(t // c,),
        in_specs=[blk(0), blk(1), blk(2), blk(0), blk(0)] + [blk(0)] * 7,
        out_specs=[pl.BlockSpec((c, 3 * wide), lambda i: (i, 0)), blk(0), blk(0)],
        out_shape=[jax.ShapeDtypeStruct((t, 3 * wide), F32), jax.ShapeDtypeStruct((t, wide), F32),
                   jax.ShapeDtypeStruct((t, wide), F32)],
        compiler_params=_cparams(("parallel",)),
        name=name,
    )(qkv, qkv, qkv, gb, bb, rest, *cts)


def _gdn_scan_math(s, u, w, qk, qd, kt, cd):
    v_new = u - _bdot(w, s)
    o = _bdot(qd, s) + _bdot(qk, v_new)
    s_new = s * cd + _bdot_tn(kt, v_new)
    return o, s_new


def _gdn_scan_fwd(prep, *, name):
    t = prep[0].shape[0]
    c = GDN_CHUNK
    wide = GDN_HEADS * LANES

    def body(u_ref, w_ref, qk_ref, qd_ref, kt_ref, cd_ref, o_ref, st_ref, s_ref):
        @pl.when(pl.program_id(0) == 0)
        def _():
            s_ref[...] = jnp.zeros_like(s_ref)

        s = _stack_heads(s_ref)
        _store_heads(st_ref, s)
        o, s_new = _gdn_scan_math(s, *(_stack_heads(r).astype(F32) for r in (u_ref, w_ref, qk_ref, qd_ref, kt_ref, cd_ref)))
        _store_heads(o_ref, o)
        _store_heads(s_ref, s_new)

    blk = pl.BlockSpec((c, wide), lambda i: (i, 0))
    return pl.pallas_call(
        body,
        grid=(t // c,),
        in_specs=[blk] * 6,
        out_specs=[blk, blk],
        out_shape=[jax.ShapeDtypeStruct((t, wide), F32)] * 2,
        scratch_shapes=[pltpu.VMEM((GDN_DK, wide), F32)],
        compiler_params=_cparams(("arbitrary",)),
        name=name,
    )(*prep)


def _gdn_scan_bwd(prep, states, do, *, name):
    t = do.shape[0]
    c = GDN_CHUNK
    wide = GDN_HEADS * LANES
    nc = t // c

    def body(u_ref, w_ref, qk_ref, qd_ref, kt_ref, cd_ref, st_ref, do_ref, *rest):
        outs, ds_ref = rest[:6], rest[6]

        @pl.when(pl.program_id(0) == 0)
        def _():
            ds_ref[...] = jnp.zeros_like(ds_ref)

        prim = tuple(_stack_heads(r).astype(F32) for r in (st_ref, u_ref, w_ref, qk_ref, qd_ref, kt_ref, cd_ref))
        _, pull = jax.vjp(_gdn_scan_math, *prim)
        grads = pull((_stack_heads(do_ref), _stack_heads(ds_ref)))
        _store_heads(ds_ref, grads[0])
        for o_ref, val in zip(outs, grads[1:]):
            _store_heads(o_ref, val)

    blk = pl.BlockSpec((c, wide), lambda i: (nc - 1 - i, 0))
    return pl.pallas_call(
        body,
        grid=(nc,),
        in_specs=[blk] * 8,
        out_specs=[blk] * 6,
        out_shape=[jax.ShapeDtypeStruct((t, wide), dt) for dt in (F32, MM_DTYPE, MM_DTYPE, MM_DTYPE, MM_DTYPE, F32)],
        scratch_shapes=[pltpu.VMEM((GDN_DK, wide), F32)],
        compiler_params=_cparams(("arbitrary",)),
        name=name,
    )(*prep, states, do)


def _gdn_outgate_math(o, z, nw):
    r = lax.rsqrt(jnp.mean(o * o, axis=-1, keepdims=True) + RMS_EPS)
    return o * r * nw * _silu(z)


def _gdn_outgate_fwd(o, pm, nw_row, *, name):
    t = o.shape[0]
    tb = min(t, ROW_TILE)
    wide = GDN_HEADS * LANES
    z_at = GDN_QKV // wide

    def body(o_ref, z_ref, nw_ref, y_ref):
        for h in range(GDN_HEADS):
            y = _gdn_outgate_math(_head_tiles(o_ref, h), _head_tiles(z_ref, h), nw_ref[...])
            y_ref[:, h * LANES:(h + 1) * LANES] = y.astype(y_ref.dtype)

    return pl.pallas_call(
        body,
        grid=(t // tb,),
        in_specs=[pl.BlockSpec((tb, wide), lambda i: (i, 0)), pl.BlockSpec((tb, wide), lambda i: (i, z_at)),
                  pl.BlockSpec((1, LANES), lambda i: (0, 0))],
        out_specs=pl.BlockSpec((tb, wide), lambda i: (i, 0)),
        out_shape=jax.ShapeDtypeStruct((t, wide), MM_DTYPE),
        compiler_params=_cparams(("parallel",)),
        name=name,
    )(o, pm, nw_row)


def _gdn_outgate_bwd(o, pm, nw_row, dy, *, name):
    t = o.shape[0]
    tb = min(t, ROW_TILE)
    wide = GDN_HEADS * LANES
    z_at = GDN_QKV // wide

    def body(o_ref, z_ref, nw_ref, dy_ref, do_ref, dz_ref, dnw_ref):
        total = jnp.zeros((1, LANES), F32)
        for h in range(GDN_HEADS):
            _, pull = jax.vjp(_gdn_outgate_math, _head_tiles(o_ref, h), _head_tiles(z_ref, h), nw_ref[...])
            d_o, d_z, d_nw = pull(_head_tiles(dy_ref, h))
            do_ref[:, h * LANES:(h + 1) * LANES] = d_o
            dz_ref[:, h * LANES:(h + 1) * LANES] = d_z.astype(dz_ref.dtype)
            total = total + d_nw

        @pl.when(pl.program_id(0) == 0)
        def _():
            dnw_ref[...] = total

        @pl.when(pl.program_id(0) > 0)
        def _():
            dnw_ref[...] += total

    blk = pl.BlockSpec((tb, wide), lambda i: (i, 0))
    one = pl.BlockSpec((1, LANES), lambda i: (0, 0))
    return pl.pallas_call(
        body,
        grid=(t // tb,),
        in_specs=[blk, pl.BlockSpec((tb, wide), lambda i: (i, z_at)), one, blk],
        out_specs=[blk, blk, one],
        out_shape=[jax.ShapeDtypeStruct((t, wide), F32), jax.ShapeDtypeStruct((t, wide), MM_DTYPE),
                   jax.ShapeDtypeStruct((1, LANES), F32)],
        compiler_params=_cparams(("arbitrary",)),
        name=name,
    )(o, pm, nw_row, dy)


def _rms64(x, w_row):
    return x * lax.rsqrt(jnp.sum(x * x, axis=-1, keepdims=True) * (1.0 / DIL_DH) + RMS_EPS) * w_row


def _alibi_slopes(group):
    head = lax.broadcasted_iota(jnp.int32, (DIL_HEADS, 8, LANES), 0).astype(F32)
    rate = -math.log(2.0) * ALIBI_MAX_BIAS / (len(DIL_GROUPS) * DIL_HEADS)
    slope = jnp.exp(rate * (head + float(group * DIL_HEADS + 1)))
    return jnp.broadcast_to(slope[:, 0:1, :], (DIL_HEADS, DIL_SPAN, LANES))


def _band_logits(qn, kp, kc, slope_d, has_prev):
    qi = lax.broadcasted_iota(jnp.int32, (DIL_SPAN, DIL_SPAN), 0)
    kj = lax.broadcasted_iota(jnp.int32, (DIL_SPAN, DIL_SPAN), 1)
    steps_c = (qi - kj).astype(F32)
    scale = DIL_DH ** -0.5
    sp = _bdot_nt(qn, kp) * scale - slope_d * (steps_c + float(DIL_SPAN))
    sc = _bdot_nt(qn, kc) * scale - slope_d * steps_c
    sp = jnp.where((kj >= qi) & has_prev, sp, NEG)
    sc = jnp.where(kj <= qi, sc, NEG)
    return sp, sc


def _dil_attn_fwd(slab, wq_row, wk_row, *, group, name):
    dilation = DIL_GROUPS[group][1]
    t = slab.shape[0]
    rows = t // dilation
    nlb = rows // DIL_SPAN
    wide = DIL_HEADS * LANES
    view = slab.reshape(rows, dilation * DIL_SLAB)

    def body(q_ref, kc_ref, vc_ref, kp_ref, vp_ref, wq_ref, wk_ref, o_ref):
        has_prev = pl.program_id(1) > 0
        lane = lax.broadcasted_iota(jnp.int32, (DIL_SPAN, LANES), 1)
        qn = _rms64(_stack_heads(q_ref), wq_ref[...])
        kc = _rms64(_stack_heads(kc_ref), wk_ref[...])
        kp = _rms64(_stack_heads(kp_ref), wk_ref[...])
        sp, sc = _band_logits(qn, kp, kc, _alibi_slopes(group) * float(dilation), has_prev)
        m = jnp.maximum(jnp.max(sp, axis=-1, keepdims=True), jnp.max(sc, axis=-1, keepdims=True))
        pp = jnp.exp(sp - m)
        pc = jnp.exp(sc - m)
        l = jnp.sum(pp, axis=-1, keepdims=True) + jnp.sum(pc, axis=-1, keepdims=True)
        o = (_bdot(pp, _stack_heads(vp_ref)) + _bdot(pc, _stack_heads(vc_ref))) / l
        _store_heads(o_ref, jnp.where(lane < DIL_DH, o, m + jnp.log(l)))

    cur = lambda part: pl.BlockSpec((DIL_SPAN, wide), lambda r, i: (i, 3 * r + part))
    prv = lambda part: pl.BlockSpec((DIL_SPAN, wide), lambda r, i: (jnp.maximum(i - 1, 0), 3 * r + part))
    one = pl.BlockSpec((1, LANES), lambda r, i: (0, 0))
    out = pl.pallas_call(
        body,
        grid=(dilation, nlb),
        in_specs=[cur(0), cur(1), cur(2), prv(1), prv(2), one, one],
        out_specs=pl.BlockSpec((DIL_SPAN, wide), lambda r, i: (i, r)),
        out_shape=jax.ShapeDtypeStruct((rows, dilation * wide), F32),
        compiler_params=_cparams(("parallel", "parallel")),
        name=name,
    )(view, view, view, view, view, wq_row, wk_row)
    return out.reshape(t, wide)


def _head_slope(group, head):
    idx = jnp.zeros((8, LANES), F32) + head.astype(F32)
    rate = -math.log(2.0) * ALIBI_MAX_BIAS / (len(DIL_GROUPS) * DIL_HEADS)
    slope = jnp.exp(rate * (idx + float(group * DIL_HEADS + 1)))
    return jnp.broadcast_to(slope[0:1, :], (DIL_SPAN, LANES))


RESIDUE_BATCH = 8


def _take_residues(ref, d, first=0, count=None):
    count = d if count is None else count
    return jnp.stack([ref[pl.ds(first + r, DIL_SPAN, stride=d), :] for r in range(count)])


def _put_residues(ref, val, d, first=0):
    for r in range(val.shape[0]):
        ref[pl.ds(first + r, DIL_SPAN, stride=d), :] = val[r]


def _dil_attn_fwd_strided(slab, wq_row, wk_row, *, group, name):
    d = DIL_GROUPS[group][1]
    t = slab.shape[0]
    span = DIL_SPAN * d
    nsb = t // span

    hs = max(1, RESIDUE_BATCH // d)

    def body(*refs):
        q, kc, vc, kp, vp = (refs[i * hs:(i + 1) * hs] for i in range(5))
        wq_ref, wk_ref, o_ref, spread = refs[5 * hs:]
        has_prev = pl.program_id(0) > 0
        lane = lax.broadcasted_iota(jnp.int32, (DIL_SPAN, LANES), 1)
        nb = min(d, RESIDUE_BATCH)
        for r0 in range(0, d, nb):
            take = lambda group_refs: jnp.concatenate([_take_residues(ref, d, r0, nb) for ref in group_refs])
            slope = jnp.concatenate([jnp.broadcast_to(_head_slope(group, pl.program_id(1) * hs + j) * float(d),
                                                      (nb, DIL_SPAN, LANES)) for j in range(hs)])
            qn = _rms64(take(q), wq_ref[...])
            kcn = _rms64(take(kc), wk_ref[...])
            kpn = _rms64(take(kp), wk_ref[...])
            sp, sc = _band_logits(qn, kpn, kcn, slope, has_prev)
            m = jnp.maximum(jnp.max(sp, axis=-1, keepdims=True), jnp.max(sc, axis=-1, keepdims=True))
            pp = jnp.exp(sp - m)
            pc = jnp.exp(sc - m)
            l = jnp.sum(pp, axis=-1, keepdims=True) + jnp.sum(pc, axis=-1, keepdims=True)
            o = (_bdot(pp, take(vp)) + _bdot(pc, take(vc))) / l
            res = jnp.where(lane < DIL_DH, o, m + jnp.log(l))
            for j in range(hs):
                _put_residues(spread, res[j * nb:(j + 1) * nb], d, r0)
                if r0 + nb == d:
                    o_ref[:, j * LANES:(j + 1) * LANES] = spread[...]

    cur = lambda part, j: pl.BlockSpec((span, LANES), lambda i, h: (i, part * DIL_HEADS + h * hs + j))
    prv = lambda part, j: pl.BlockSpec((span, LANES), lambda i, h: (jnp.maximum(i - 1, 0), part * DIL_HEADS + h * hs + j))
    one = pl.BlockSpec((1, LANES), lambda i, h: (0, 0))
    heads = range(hs)
    in_specs = ([cur(0, j) for j in heads] + [cur(1, j) for j in heads] + [cur(2, j) for j in heads]
                + [prv(1, j) for j in heads] + [prv(2, j) for j in heads] + [one, one])
    return pl.pallas_call(
        body,
        grid=(nsb, DIL_HEADS // hs),
        in_specs=in_specs,
        out_specs=pl.BlockSpec((span, hs * LANES), lambda i, h: (i, h)),
        out_shape=jax.ShapeDtypeStruct((t, DIL_HEADS * LANES), F32),
        scratch_shapes=[pltpu.VMEM((span, LANES), F32)],
        compiler_params=_cparams(("parallel", "parallel")),
        name=name,
    )(*([slab] * (5 * hs)), wq_row, wk_row)


def _dil_attn_bwd_strided(slab, stat, wq_row, wk_row, dwq_in, dwk_in, *, group, name):
    d = DIL_GROUPS[group][1]
    t = slab.shape[0]
    span = DIL_SPAN * d
    nsb = t // span

    hs = max(1, RESIDUE_BATCH // d)

    def body(*refs):
        q_refs, kc_refs, vc_refs, kp_refs, vp_refs, st_refs = (refs[i * hs:(i + 1) * hs] for i in range(6))
        wq_ref, wk_ref, dwq_in_ref, dwk_in_ref, d_ref, dwq_ref, dwk_ref, dk_carry, dv_carry, spread = refs[6 * hs:]
        take = lambda group_refs: jnp.concatenate([_take_residues(ref, d) for ref in group_refs])
        step = pl.program_id(1)
        has_prev = step < nsb - 1
        first = (pl.program_id(0) == 0) & (step == 0)

        @pl.when(step == 0)
        def _():
            dk_carry[...] = jnp.zeros_like(dk_carry)
            dv_carry[...] = jnp.zeros_like(dv_carry)

        @pl.when(first)
        def _():
            dwq_ref[...] = dwq_in_ref[...]
            dwk_ref[...] = dwk_in_ref[...]

        lane = lax.broadcasted_iota(jnp.int32, (DIL_SPAN, LANES), 1)
        scale = DIL_DH ** -0.5
        q_raw = take(q_refs)
        kc_raw = take(kc_refs)
        vc = take(vc_refs)
        kp_raw = take(kp_refs)
        vp = take(vp_refs)
        st = take(st_refs)
        slope = jnp.concatenate([jnp.broadcast_to(_head_slope(group, pl.program_id(0) * hs + j) * float(d),
                                                  (d, DIL_SPAN, LANES)) for j in range(hs)])
        d_o = jnp.where(lane < DIL_DH, st, 0.0)
        lse = jnp.sum(jnp.where(lane == DIL_DH, st, 0.0), axis=-1, keepdims=True)
        delta = jnp.sum(jnp.where(lane == DIL_DH + 1, st, 0.0), axis=-1, keepdims=True)
        qn = _rms64(q_raw, wq_ref[...])
        kc = _rms64(kc_raw, wk_ref[...])
        kp = _rms64(kp_raw, wk_ref[...])
        sp, sc = _band_logits(qn, kp, kc, slope, has_prev)
        pp = jnp.exp(sp - lse)
        pc = jnp.exp(sc - lse)
        dsp = pp * (_bdot_nt(d_o, vp) - delta) * scale
        dsc = pc * (_bdot_nt(d_o, vc) - delta) * scale
        dqn = _bdot(dsp, kp) + _bdot(dsc, kc)
        dkc_n = _bdot_tn(dsc, qn) + dk_carry[...]
        dvc = _bdot_tn(pc, d_o) + dv_carry[...]
        dk_carry[...] = _bdot_tn(dsp, qn)
        dv_carry[...] = _bdot_tn(pp, d_o)
        dq_raw, dwq_rows = _rms64_bwd(q_raw, wq_ref[...], dqn)
        dk_raw, dwk_rows = _rms64_bwd(kc_raw, wk_ref[...], dkc_n)
        for part, val in enumerate((dq_raw, dk_raw, dvc)):
            for j in range(hs):
                _put_residues(spread, val[j * d:(j + 1) * d], d)
                d_ref[part, :, j * LANES:(j + 1) * LANES] = spread[...].astype(d_ref.dtype)
        dwq_ref[...] += jnp.sum(jnp.sum(dwq_rows, axis=0), axis=0, keepdims=True)
        dwk_ref[...] += jnp.sum(jnp.sum(dwk_rows, axis=0), axis=0, keepdims=True)

    at = lambda i: nsb - 1 - i
    cur = lambda part, j: pl.BlockSpec((span, LANES), lambda h, i: (at(i), part * DIL_HEADS + h * hs + j))
    prv = lambda part, j: pl.BlockSpec((span, LANES), lambda h, i: (jnp.maximum(at(i) - 1, 0), part * DIL_HEADS + h * hs + j))
    one = pl.BlockSpec((1, LANES), lambda h, i: (0, 0))
    heads = range(hs)
    in_specs = ([cur(0, j) for j in heads] + [cur(1, j) for j in heads] + [cur(2, j) for j in heads]
                + [prv(1, j) for j in heads] + [prv(2, j) for j in heads] + [cur(0, j) for j in heads] + [one] * 4)
    return pl.pallas_call(
        body,
        grid=(DIL_HEADS // hs, nsb),
        in_specs=in_specs,
        out_specs=[pl.BlockSpec((3, span, hs * LANES), lambda h, i: (0, at(i), h)), one, one],
        out_shape=[jax.ShapeDtypeStruct((3, t, DIL_HEADS * LANES), MM_DTYPE), jax.ShapeDtypeStruct((1, LANES), F32),
                   jax.ShapeDtypeStruct((1, LANES), F32)],
        scratch_shapes=[pltpu.VMEM((hs * d, DIL_SPAN, LANES), F32), pltpu.VMEM((hs * d, DIL_SPAN, LANES), F32),
                        pltpu.VMEM((span, LANES), F32)],
        compiler_params=_cparams(("arbitrary", "arbitrary")),
        name=name,
    )(*([slab] * (5 * hs)), *([stat] * hs), wq_row, wk_row, dwq_in, dwk_in)


def _dil_merge_fwd(oe, *, name):
    t = oe[0].shape[0]
    tb = min(t, ROW_TILE)
    wide = DIL_HEADS * LANES

    def body(e0, e1, e2, y_ref, om_ref):
        lane = lax.broadcasted_iota(jnp.int32, (tb, LANES), 1)
        for h in range(DIL_HEADS):
            es = [_head_tiles(e, h) for e in (e0, e1, e2)]
            lse = [jnp.sum(jnp.where(lane == DIL_DH, e, 0.0), axis=-1, keepdims=True) for e in es]
            top = jnp.maximum(jnp.maximum(lse[0], lse[1]), lse[2])
            joint = top + jnp.log(jnp.exp(lse[0] - top) + jnp.exp(lse[1] - top) + jnp.exp(lse[2] - top))
            o = sum(jnp.exp(l - joint) * e for l, e in zip(lse, es))
            y_ref[:, h * LANES:(h + 1) * LANES] = jnp.where(lane < DIL_DH, o, 0.0).astype(y_ref.dtype)
            om_ref[:, h * LANES:(h + 1) * LANES] = jnp.where(lane < DIL_DH, o, joint)

    blk = pl.BlockSpec((tb, wide), lambda i: (i, 0))
    return pl.pallas_call(
        body,
        grid=(t // tb,),
        in_specs=[blk] * 3,
        out_specs=[blk, blk],
        out_shape=[jax.ShapeDtypeStruct((t, wide), MM_DTYPE), jax.ShapeDtypeStruct((t, wide), F32)],
        compiler_params=_cparams(("parallel",)),
        name=name,
    )(*oe)


def _dil_merge_bwd(dy, om, *, name):
    t = dy.shape[0]
    tb = min(t, ROW_TILE)
    wide = DIL_HEADS * LANES

    def body(dy_ref, om_ref, st_ref):
        lane = lax.broadcasted_iota(jnp.int32, (tb, LANES), 1)
        for h in range(DIL_HEADS):
            d_o = jnp.where(lane < DIL_DH, _head_tiles(dy_ref, h), 0.0)
            om_t = _head_tiles(om_ref, h)
            delta = jnp.sum(d_o * om_t, axis=-1, keepdims=True)
            st_ref[:, h * LANES:(h + 1) * LANES] = jnp.where(
                lane < DIL_DH, d_o, jnp.where(lane == DIL_DH, om_t, jnp.where(lane == DIL_DH + 1, delta, 0.0)))

    blk = pl.BlockSpec((tb, wide), lambda i: (i, 0))
    return pl.pallas_call(
        body,
        grid=(t // tb,),
        in_specs=[blk, blk],
        out_specs=blk,
        out_shape=jax.ShapeDtypeStruct((t, wide), F32),
        compiler_params=_cparams(("parallel",)),
        name=name,
    )(dy, om)


def _rms64_bwd(x, w_row, dy):
    r = lax.rsqrt(jnp.sum(x * x, axis=-1, keepdims=True) * (1.0 / DIL_DH) + RMS_EPS)
    gw = dy * w_row
    dx = r * gw - x * (r * r * r * jnp.sum(gw * x, axis=-1, keepdims=True) * (1.0 / DIL_DH))
    return dx, dy * x * r


def _dil_attn_bwd(slab, stat, wq_row, wk_row, dwq_in, dwk_in, *, group, name):
    dilation = DIL_GROUPS[group][1]
    t = slab.shape[0]
    rows = t // dilation
    nlb = rows // DIL_SPAN
    wide = DIL_HEADS * LANES
    view = slab.reshape(rows, dilation * DIL_SLAB)
    stat_view = stat.reshape(rows, dilation * wide)

    def body(cur_ref, kp_ref, vp_ref, st_ref, wq_ref, wk_ref, dwq_in_ref, dwk_in_ref, d_ref, dwq_ref, dwk_ref,
             dk_carry, dv_carry):
        step = pl.program_id(1)
        has_prev = step < nlb - 1
        first = (pl.program_id(0) == 0) & (step == 0)

        @pl.when(step == 0)
        def _():
            dk_carry[...] = jnp.zeros_like(dk_carry)
            dv_carry[...] = jnp.zeros_like(dv_carry)

        @pl.when(first)
        def _():
            dwq_ref[...] = dwq_in_ref[...]
            dwk_ref[...] = dwk_in_ref[...]

        lane = lax.broadcasted_iota(jnp.int32, (DIL_SPAN, LANES), 1)
        scale = DIL_DH ** -0.5
        q_raw = _stack_heads(cur_ref)
        kc_raw = _stack_heads(cur_ref, first=DIL_HEADS)
        vc = _stack_heads(cur_ref, first=2 * DIL_HEADS)
        kp_raw = _stack_heads(kp_ref)
        vp = _stack_heads(vp_ref)
        st = _stack_heads(st_ref)
        d_o = jnp.where(lane < DIL_DH, st, 0.0)
        lse = jnp.sum(jnp.where(lane == DIL_DH, st, 0.0), axis=-1, keepdims=True)
        delta = jnp.sum(jnp.where(lane == DIL_DH + 1, st, 0.0), axis=-1, keepdims=True)
        qn = _rms64(q_raw, wq_ref[...])
        kc = _rms64(kc_raw, wk_ref[...])
        kp = _rms64(kp_raw, wk_ref[...])
        sp, sc = _band_logits(qn, kp, kc, _alibi_slopes(group) * float(dilation), has_prev)
        pp = jnp.exp(sp - lse)
        pc = jnp.exp(sc - lse)
        dsp = pp * (_bdot_nt(d_o, vp) - delta) * scale
        dsc = pc * (_bdot_nt(d_o, vc) - delta) * scale
        dqn = _bdot(dsp, kp) + _bdot(dsc, kc)
        dkc_n = _bdot_tn(dsc, qn) + _stack_heads(dk_carry)
        dvc = _bdot_tn(pc, d_o) + _stack_heads(dv_carry)
        _store_heads(dk_carry, _bdot_tn(dsp, qn))
        _store_heads(dv_carry, _bdot_tn(pp, d_o))
        dq_raw, dwq_rows = _rms64_bwd(q_raw, wq_ref[...], dqn)
        dk_raw, dwk_rows = _rms64_bwd(kc_raw, wk_ref[...], dkc_n)
        _store_heads(d_ref, dq_raw)
        _store_heads(d_ref, dk_raw, first=DIL_HEADS)
        _store_heads(d_ref, dvc, first=2 * DIL_HEADS)
        dwq_ref[...] += jnp.sum(jnp.sum(dwq_rows, axis=0), axis=0, keepdims=True)
        dwk_ref[...] += jnp.sum(jnp.sum(dwk_rows, axis=0), axis=0, keepdims=True)

    blk_i = lambda i: nlb - 1 - i
    cur = pl.BlockSpec((DIL_SPAN, DIL_SLAB), lambda r, i: (blk_i(i), r))
    prv = lambda part: pl.BlockSpec((DIL_SPAN, wide), lambda r, i: (jnp.maximum(blk_i(i) - 1, 0), 3 * r + part))
    one = pl.BlockSpec((1, LANES), lambda r, i: (0, 0))
    dslab, dwq, dwk = pl.pallas_call(
        body,
        grid=(dilation, nlb),
        in_specs=[cur, prv(1), prv(2), pl.BlockSpec((DIL_SPAN, wide), lambda r, i: (blk_i(i), r)), one, one, one, one],
        out_specs=[cur, one, one],
        out_shape=[jax.ShapeDtypeStruct((rows, dilation * DIL_SLAB), MM_DTYPE), jax.ShapeDtypeStruct((1, LANES), F32),
                   jax.ShapeDtypeStruct((1, LANES), F32)],
        scratch_shapes=[pltpu.VMEM((DIL_SPAN, wide), F32), pltpu.VMEM((DIL_SPAN, wide), F32)],
        compiler_params=_cparams(("arbitrary", "arbitrary")),
        name=name,
    )(view, view, view, stat_view, wq_row, wk_row, dwq_in, dwk_in)
    return dslab.reshape(t, DIL_SLAB), dwq, dwk


def _row(v, width=LANES):
    v = v.astype(F32).reshape(-1)
    return jnp.pad(v, (0, width - v.shape[0])).reshape(1, width)


def _prepare_weights(w):
    return dict(gdn=_prepare_gdn(w), dil=_prepare_dil(w), ffn=_prepare_ffn(w))


def _prepare_gdn(w, layers=range(DEPTH // 2)):
    gdn = {}
    for j in layers:
        wt = w["gdn_w_in"][j]
        gates_t = jnp.pad(wt[GDN_MAIN:], ((0, LANES - 2 * GDN_HEADS), (0, 0)))
        gdn[j] = dict(in_t=wt, gates_t=gates_t, out=w["gdn_w_out"][j], conv=w["gdn_conv_w"][j].astype(F32),
                      alog=_row(w["gdn_a_log"][j]), dt=_row(w["gdn_dt_bias"][j]), nw=_row(w["gdn_norm_w"][j]))
    return gdn


def _prepare_dil(w, layers=range(DEPTH // 2)):
    d = D_MODEL
    dil = {}
    for j in layers:
        wt = w["dil_w_in"][j].reshape(3, len(DIL_GROUPS), DIL_HEADS, DIL_DH, d)
        wg_t = [wt[:, g].reshape(DIL_SLAB // 2, d) for g in range(len(DIL_GROUPS))]
        out_t = jnp.pad(w["dil_w_out"][j].reshape(d, DIL_HEADS, DIL_DH), ((0, 0), (0, 0), (0, LANES - DIL_DH)))
        dil[j] = dict(wg_t=wg_t, out_t=out_t.reshape(d, DIL_HEADS * LANES), wq=_row(w["dil_q_norm"][j]),
                      wk=_row(w["dil_k_norm"][j]))
    return dil


def _prepare_ffn(w, layers=range(DEPTH)):
    return {i: dict(in_t=w["ffn_w_in"][i], out=w["ffn_w_out"][i]) for i in layers}


def _residual_out(a, w, x, next_row, *, name, **kw):
    if next_row is None:
        return _matmul(a, w, add=x, name=name, **kw), None
    return _matmul(a, w, add=x, norm_fwd=next_row, name=name + "_norm", **kw)


def _gdn_layer_fwd(x, nrow, p, hn=None, next_row=None):
    if hn is None:
        hn = _rmsnorm_fwd(x, nrow, name="rmsnorm_fwd")
    pm = _matmul(hn, p["in_t"], trans_b=True, b_rows=(0, GDN_MAIN), name="gdn_proj_main")
    ab = _matmul(hn, p["gates_t"], trans_b=True, name="gdn_proj_gates")
    qkv = _gdn_conv_fwd(pm, p["conv"], name="gdn_conv_fwd")
    gb, bb = _gdn_gates_fwd(ab, p["alog"], p["dt"], name="gdn_gates_fwd")
    prep, rest = _gdn_prep_fwd(qkv, gb, bb, name="gdn_prep_fwd")
    o, states = _gdn_scan_fwd(prep, name="gdn_scan_fwd")
    og = _gdn_outgate_fwd(o, pm, p["nw"], name="gdn_outgate_fwd")
    y, hn_next = _residual_out(og, p["out"], x, next_row, name="gdn_proj_out")
    return y, (x, hn, pm, ab, qkv, gb, bb, prep, rest, states, o, og), hn_next


def _gdn_layer_bwd(dx, dxb, nrow, p, saved):
    x, hn, pm, ab, qkv, gb, bb, prep, rest, states, o, og = saved
    d_og = _matmul(dxb, p["out"], trans_b=True, name="gdn_dgate")
    g_out = _matmul(og, dxb, trans_a=True, out_dtype=MM_DTYPE, name="gdn_gw_out")
    d_o, d_z, d_nw = _gdn_outgate_bwd(o, pm, p["nw"], d_og, name="gdn_outgate_bwd")
    cts = _gdn_scan_bwd(prep, states, d_o, name="gdn_scan_bwd")
    dqkv, dgb, dbb = _gdn_prep_bwd(qkv, gb, bb, rest, cts, name="gdn_prep_bwd")
    d_ab, d_alog, d_dt = _gdn_gates_bwd(ab, p["alog"], p["dt"], dgb, dbb, name="gdn_gates_bwd")
    d_conv, g_conv = _gdn_conv_bwd(pm, p["conv"], dqkv, name="gdn_conv_bwd")
    d_hn = _matmul(d_conv, p["in_t"], b_rows=(0, GDN_QKV), name="gdn_dhn_qkv")
    d_hn = _matmul(d_z, p["in_t"], b_rows=(GDN_QKV, GDN_MAIN - GDN_QKV), add=d_hn, name="gdn_dhn_z")
    dx_new, dxb_new, g_norm = _matmul(d_ab, p["gates_t"], add=d_hn, norm_bwd=(x, nrow, dx), name="gdn_dhn_gates_norm")
    g_in_t = jnp.concatenate([
        _matmul(d_conv, hn, trans_a=True, out_dtype=MM_DTYPE, name="gdn_gw_qkv"),
        _matmul(d_z, hn, trans_a=True, out_dtype=MM_DTYPE, name="gdn_gw_z"),
        _matmul(d_ab, hn, trans_a=True, out_dtype=MM_DTYPE, name="gdn_gw_gates")[:2 * GDN_HEADS],
    ], axis=0)
    grads = dict(w_in=g_in_t, conv=g_conv, a_log=d_alog[0, :GDN_HEADS], dt_bias=d_dt[0, :GDN_HEADS], norm_w=d_nw[0],
                 w_out=g_out, norm=g_norm[0])
    return dx_new, dxb_new, grads


def _dil_layer_fwd(x, nrow, p, hn=None, next_row=None):
    if hn is None:
        hn = _rmsnorm_fwd(x, nrow, name="rmsnorm_fwd")
    slabs = [_matmul(hn, p["wg_t"][g], trans_b=True, spread_out=True, name="dil_proj_in") for g in range(len(DIL_GROUPS))]
    oe = [(_dil_attn_fwd if DIL_GROUPS[g][1] == 1 else _dil_attn_fwd_strided)(
        slabs[g], p["wq"], p["wk"], group=g, name=f"dil_attn_fwd_g{g}") for g in range(len(DIL_GROUPS))]
    y, om = _dil_merge_fwd(oe, name="dil_merge_fwd")
    out, hn_next = _residual_out(y, p["out_t"], x, next_row, trans_b=True, name="dil_proj_out")
    return out, (x, hn, slabs, y, om), hn_next


def _dil_layer_bwd(dx, dxb, nrow, p, saved):
    x, hn, slabs, y, om = saved
    d_y = _matmul(dxb, p["out_t"], name="dil_dmerged")
    g_out_t = _matmul(dxb, y, trans_a=True, out_dtype=MM_DTYPE, name="dil_gw_out")
    g_out_t = g_out_t.reshape(D_MODEL, DIL_HEADS, LANES)[..., :DIL_DH].reshape(D_MODEL, DIL_HEADS * DIL_DH)
    stat = _dil_merge_bwd(d_y, om, name="dil_merge_bwd")
    d_hn = None
    dwq = jnp.zeros((1, LANES), F32)
    dwk = jnp.zeros((1, LANES), F32)
    g_groups = []
    wide = DIL_HEADS * LANES
    for g in range(len(DIL_GROUPS)):
        last = dict(norm_bwd=(x, nrow, dx)) if g == len(DIL_GROUPS) - 1 else {}
        if DIL_GROUPS[g][1] == 1:
            dslab, dwq, dwk = _dil_attn_bwd(slabs[g], stat, p["wq"], p["wk"], dwq, dwk, group=g, name=f"dil_attn_bwd_g{g}")
            d_hn = _matmul(dslab, p["wg_t"][g], packed_a=True, add=d_hn, name="dil_dhn", **last)
            g_w = _matmul(dslab, hn, trans_a=True, packed_a=True, out_dtype=MM_DTYPE, name="dil_gw_in")
        else:
            dparts, dwq, dwk = _dil_attn_bwd_strided(slabs[g], stat, p["wq"], p["wk"], dwq, dwk, group=g,
                                                     name=f"dil_attn_bwd_g{g}")
            d_hn = _matmul(dparts, p["wg_t"][g], a_lead="k", packed_a=True, add=d_hn,
                           name="dil_dhn_parts_norm" if last else "dil_dhn_parts", **last)
            g_w = _matmul(dparts, hn, trans_a=True, a_lead="i", packed_a=True, out_dtype=MM_DTYPE, name="dil_gw_in_parts")
        g_groups.append(g_w.reshape(3, DIL_HEADS, DIL_DH, D_MODEL))
    g_in_t = jnp.stack(g_groups, axis=1).reshape(3 * len(DIL_GROUPS) * DIL_HEADS * DIL_DH, D_MODEL)
    dx_new, dxb_new, g_norm = d_hn
    grads = dict(w_in=g_in_t, q_norm=dwq[0, :DIL_DH], k_norm=dwk[0, :DIL_DH], w_out=g_out_t, norm=g_norm[0])
    return dx_new, dxb_new, grads


def _ffn_layer_fwd(x, nrow, p, hn=None, next_row=None):
    if hn is None:
        hn = _rmsnorm_fwd(x, nrow, name="rmsnorm_fwd")
    gate, up, act = _ffn_in(hn, p["in_t"], name="ffn_proj_in")
    y, hn_next = _residual_out(act, p["out"], x, next_row, name="ffn_proj_out")
    return y, (x, hn, gate, up, act), hn_next


def _ffn_layer_bwd(dx, dxb, nrow, p, saved):
    x, hn, gate, up, act = saved
    g_out = _matmul(act, dxb, trans_a=True, out_dtype=MM_DTYPE, name="ffn_gw_out")
    d_gu = _ffn_dact(dxb, p["out"], gate, up, name="ffn_dact")
    dx_new, dxb_new, g_norm = _matmul(d_gu, p["in_t"], a_lead="k", norm_bwd=(x, nrow, dx), name="ffn_dhn_norm")
    g_in_t = _matmul(d_gu, hn, trans_a=True, a_lead="i", out_dtype=MM_DTYPE, name="ffn_gw_in")
    return dx_new, dxb_new, dict(w_in=g_in_t, w_out=g_out, norm=g_norm[0])


def _mixer_fwd(i, x, mix_row, prepared, hn=None, next_row=None):
    if i % 2 == 0:
        return _gdn_layer_fwd(x, mix_row, prepared["gdn"][i // 2], hn, next_row)
    return _dil_layer_fwd(x, mix_row, prepared["dil"][i // 2], hn, next_row)


def _mixer_bwd(i, dx, dxb, mix_row, prepared, saved, zero=0.0):
    if i % 2 == 0:
        p = prepared["gdn"][i // 2]
        return _gdn_layer_bwd(dx, dxb, mix_row, dict(p, nw=p["nw"] + zero), saved)
    p = prepared["dil"][i // 2]
    return _dil_layer_bwd(dx, dxb, mix_row, dict(p, wq=p["wq"] + zero), saved)


def _local_step(x, target, prepared, norm_mix, norm_ffn):
    saved = []
    hn = None
    for i in range(DEPTH):
        after = norm_mix[i + 1].reshape(1, D_MODEL) if i + 1 < DEPTH else None
        x, s_mix, hn = _mixer_fwd(i, x, norm_mix[i].reshape(1, D_MODEL), prepared, hn, norm_ffn[i].reshape(1, D_MODEL))
        x, s_ffn, hn = _ffn_layer_fwd(x, norm_ffn[i].reshape(1, D_MODEL), prepared["ffn"][i], hn, after)
        saved.append((s_mix, s_ffn))
    dx, dxb, loss = _loss_head(x, target, name="loss_head")
    g_mix, g_ffn = [None] * DEPTH, [None] * DEPTH
    for i in reversed(range(DEPTH)):
        s_mix, s_ffn = saved[i]
        dx, dxb, g_ffn[i] = _ffn_layer_bwd(dx, dxb, norm_ffn[i].reshape(1, D_MODEL), prepared["ffn"][i], s_ffn)
        dx, dxb, g_mix[i] = _mixer_bwd(i, dx, dxb, norm_mix[i].reshape(1, D_MODEL), prepared, s_mix)
    return loss[0, 0], dx, _collect_grads(g_mix, g_ffn)


def _collect_grads(g_mix, g_ffn):
    gdn = [g_mix[i] for i in range(0, DEPTH, 2)]
    dil = [g_mix[i] for i in range(1, DEPTH, 2)]
    if any(g is None for g in g_mix + g_ffn):
        pick = lambda gs, key: [None if g is None else g[key] for g in gs]
        return dict(gdn_w_in=pick(gdn, "w_in"), gdn_w_out=pick(gdn, "w_out"), dil_w_in=pick(dil, "w_in"),
                    dil_w_out=pick(dil, "w_out"), ffn_w_in=pick(g_ffn, "w_in"), ffn_w_out=pick(g_ffn, "w_out"))
    grads = dict(
        norm_mix=jnp.stack([g["norm"] for g in g_mix]),
        norm_ffn=jnp.stack([g["norm"] for g in g_ffn]),
        gdn_w_in=[g["w_in"] for g in gdn],
        gdn_conv_w=jnp.stack([g["conv"] for g in gdn]),
        gdn_a_log=jnp.stack([g["a_log"] for g in gdn]),
        gdn_dt_bias=jnp.stack([g["dt_bias"] for g in gdn]),
        gdn_norm_w=jnp.stack([g["norm_w"] for g in gdn]),
        gdn_w_out=[g["w_out"] for g in gdn],
        dil_w_in=[g["w_in"] for g in dil],
        dil_q_norm=jnp.stack([g["q_norm"] for g in dil]),
        dil_k_norm=jnp.stack([g["k_norm"] for g in dil]),
        dil_w_out=[g["w_out"] for g in dil],
        ffn_w_in=[g["w_in"] for g in g_ffn],
        ffn_w_out=[g["w_out"] for g in g_ffn],
    )
    return grads


MESH_ID = pl.DeviceIdType.MESH
ANY_SPACE = pl.BlockSpec(memory_space=pl.ANY)


def _mesh_position():
    return lax.axis_index("x"), lax.axis_index("y"), lax.axis_index("c")


def _flip(pos, k):
    x, y, c = pos
    return (1 - x if k & 4 else x, 1 - y if k & 2 else y, 1 - c if k & 1 else c)


def _linear(pos):
    return 4 * pos[0] + 2 * pos[1] + pos[2]


def _comm_scratch():
    return [pltpu.SemaphoreType.DMA((N_DEV - 1,)), pltpu.SemaphoreType.DMA((N_DEV - 1,)), pltpu.SemaphoreType.DMA(())]


def _all_gather(shard, *, name):
    def body(x_ref, out_ref, send_sems, recv_sems, local_sem):
        me = _mesh_position()
        mine = out_ref.at[_linear(me)]
        local = pltpu.make_async_copy(x_ref, mine, local_sem)
        local.start()
        copies = []
        for k in range(1, N_DEV):
            cp = pltpu.make_async_remote_copy(src_ref=x_ref, dst_ref=mine, send_sem=send_sems.at[k - 1],
                                              recv_sem=recv_sems.at[k - 1], device_id=_flip(me, k), device_id_type=MESH_ID)
            cp.start()
            copies.append(cp)
        for cp in copies:
            cp.wait()
        local.wait()

    return pl.pallas_call(
        body,
        out_shape=jax.ShapeDtypeStruct((N_DEV,) + shard.shape, shard.dtype),
        in_specs=[ANY_SPACE],
        out_specs=ANY_SPACE,
        scratch_shapes=_comm_scratch(),
        name=name,
    )(shard)


def _exchange(parts, *, name):
    def body(p_ref, out_ref, send_sems, recv_sems, local_sem):
        me = _mesh_position()
        mine = out_ref.at[_linear(me)]
        local = pltpu.make_async_copy(p_ref.at[_linear(me)], mine, local_sem)
        local.start()
        copies = []
        for k in range(1, N_DEV):
            peer = _flip(me, k)
            cp = pltpu.make_async_remote_copy(src_ref=p_ref.at[_linear(peer)], dst_ref=mine, send_sem=send_sems.at[k - 1],
                                              recv_sem=recv_sems.at[k - 1], device_id=peer, device_id_type=MESH_ID)
            cp.start()
            copies.append(cp)
        for cp in copies:
            cp.wait()
        local.wait()

    return pl.pallas_call(
        body,
        out_shape=jax.ShapeDtypeStruct(parts.shape, parts.dtype),
        in_specs=[ANY_SPACE],
        out_specs=ANY_SPACE,
        scratch_shapes=_comm_scratch(),
        name=name,
    )(parts)


HBM_SPACE = pl.BlockSpec(memory_space=pltpu.HBM)
SEM_SPACE = pl.BlockSpec(memory_space=pltpu.SEMAPHORE)
DATAFLOW = pltpu.SideEffectType.DATAFLOW_SIDE_EFFECTING


def _split_copies(src_ref, land_ref, send_sems, recv_sems, per_peer):
    me = _mesh_position()
    mine = land_ref.at[_linear(me)]
    copies = []
    for k in range(1, N_DEV):
        peer = _flip(me, k)
        src = src_ref.at[_linear(peer)] if per_peer else src_ref
        copies.append(pltpu.make_async_remote_copy(src_ref=src, dst_ref=mine, send_sem=send_sems.at[k - 1],
                                                   recv_sem=recv_sems.at[k - 1], device_id=peer, device_id_type=MESH_ID))
    return copies


def _travel_start(src, after, *, per_peer, name):
    me = _linear(_mesh_position())
    own = src[me] if per_peer else src
    shape = own.shape
    landing = lax.dynamic_update_slice(lax.empty((N_DEV,) + shape, src.dtype), own[None], (me, 0, 0))

    def body(src_ref, land_ref, after_ref, send_sems, recv_sems, src_thru, land_thru, token):
        for cp in _split_copies(src_ref, land_ref, send_sems, recv_sems, per_peer):
            cp.start()
        token[...] = jnp.zeros_like(token)

    return pl.pallas_call(
        body,
        name=name,
        out_shape=(pltpu.SemaphoreType.DMA((N_DEV - 1,)), pltpu.SemaphoreType.DMA((N_DEV - 1,)),
                   pltpu.HBM(src.shape, src.dtype), pltpu.HBM(landing.shape, landing.dtype),
                   jax.ShapeDtypeStruct((8, LANES), F32)),
        in_specs=(HBM_SPACE, HBM_SPACE, ANY_SPACE),
        out_specs=(SEM_SPACE, SEM_SPACE, HBM_SPACE, HBM_SPACE, pl.BlockSpec(memory_space=pltpu.VMEM)),
        input_output_aliases={0: 2, 1: 3},
        compiler_params=pltpu.CompilerParams(has_side_effects=DATAFLOW),
    )(pltpu.with_memory_space_constraint(src, pltpu.HBM), pltpu.with_memory_space_constraint(landing, pltpu.HBM), after)


def _travel_wait(started, after, *, per_peer, name):
    send_sems, recv_sems, src_thru, land_thru, _ = started

    def body(src_ref, land_ref, send_sems, recv_sems, after_ref, src_dead, got_ref):
        for cp in _split_copies(src_ref, land_ref, send_sems, recv_sems, per_peer):
            cp.wait_send()
            cp.wait_recv()

    return pl.pallas_call(
        body,
        name=name,
        out_shape=(pltpu.HBM(src_thru.shape, src_thru.dtype), pltpu.HBM(land_thru.shape, land_thru.dtype)),
        in_specs=(HBM_SPACE, HBM_SPACE, SEM_SPACE, SEM_SPACE, ANY_SPACE),
        out_specs=(HBM_SPACE, HBM_SPACE),
        input_output_aliases={0: 0, 1: 1},
        compiler_params=pltpu.CompilerParams(has_side_effects=DATAFLOW),
    )(src_thru, land_thru, send_sems, recv_sems, after)[1]


def _adamw(parts, w, m, v, *, name):
    rows, n = w.shape
    tb = _pick(rows, (PACK_ROW_ALIGN, 16))
    c1 = 1.0 - ADAM_B1 ** ADAM_STEP
    c2 = 1.0 - ADAM_B2 ** ADAM_STEP

    def body(p_ref, w_ref, m_ref, v_ref, g_ref, d_ref, nm_ref, nv_ref):
        g = p_ref[0].astype(F32)
        for s in range(1, N_DEV):
            g = g + p_ref[s].astype(F32)
        m_new = ADAM_B1 * m_ref[...] + (1.0 - ADAM_B1) * g
        v_new = ADAM_B2 * v_ref[...] + (1.0 - ADAM_B2) * (g * g)
        m_hat = m_new / c1
        v_hat = v_new / c2
        g_ref[...] = g
        nm_ref[...] = m_new
        nv_ref[...] = v_new
        d_ref[...] = -ADAM_LR * (m_hat / (jnp.sqrt(v_hat) + ADAM_EPS) + ADAM_WD * w_ref[...])

    blk = pl.BlockSpec((tb, n), lambda i: (i, 0))
    return pl.pallas_call(
        body,
        grid=(rows // tb,),
        in_specs=[pl.BlockSpec((N_DEV, tb, n), lambda i: (0, i, 0)), blk, blk, blk],
        out_specs=[blk] * 4,
        out_shape=[jax.ShapeDtypeStruct((rows, n), F32)] * 4,
        compiler_params=_cparams(("parallel",)),
        name=name,
    )(parts, w, m, v)


PACK_WIDTH = 1024
SHARDED = {
    "gdn_w_in": ((2, D_MODEL, GDN_IN_WIDTH), 2),
    "gdn_conv_w": ((2, GDN_CONV, GDN_QKV), 2),
    "gdn_w_out": ((2, GDN_HEADS * GDN_DV, D_MODEL), 1),
    "dil_w_in": ((2, D_MODEL, 3 * len(DIL_GROUPS) * DIL_HEADS * DIL_DH), 2),
    "dil_w_out": ((2, DIL_HEADS * DIL_DH, D_MODEL), 2),
    "ffn_w_in": ((DEPTH, D_MODEL, 2 * FFN_HIDDEN), 2),
    "ffn_w_out": ((DEPTH, FFN_HIDDEN, D_MODEL), 1),
}
REPLICATED = {"norm_mix": (DEPTH, D_MODEL), "norm_ffn": (DEPTH, D_MODEL), "gdn_a_log": (2, GDN_HEADS),
              "gdn_dt_bias": (2, GDN_HEADS), "gdn_norm_w": (2, GDN_DV), "dil_q_norm": (2, DIL_DH), "dil_k_norm": (2, DIL_DH)}
WEIGHT_ORDER = ("norm_mix", "norm_ffn", "gdn_w_in", "gdn_conv_w", "gdn_a_log", "gdn_dt_bias", "gdn_norm_w", "gdn_w_out",
                "dil_w_in", "dil_q_norm", "dil_k_norm", "dil_w_out", "ffn_w_in", "ffn_w_out")
PACK_ROW_ALIGN = 128
PIECE_ALIGN = 16
SMALL_ROWS = 16


def _shard_shape(name):
    shape, axis = SHARDED[name]
    return tuple(s // N_DEV if i == axis else s for i, s in enumerate(shape))


def _shard_rows(name):
    return math.prod(_shard_shape(name)) // PACK_WIDTH


def _split_shards(full, name):
    shape, axis = SHARDED[name]
    split = full.reshape(shape[:axis] + (N_DEV, shape[axis] // N_DEV) + shape[axis + 1:])
    return jnp.moveaxis(split, axis, 0)


def _join_shards(stacked, name):
    shape, axis = SHARDED[name]
    return jnp.moveaxis(stacked, 0, axis).reshape(shape)


COLUMN_SHARDED = ("gdn_w_in", "dil_w_in", "dil_w_out", "ffn_w_in")


def _to_rows(shard, name):
    if name in COLUMN_SHARDED:
        shard = jnp.swapaxes(shard, 1, 2)
    return shard.reshape(-1, PACK_WIDTH)


def _layer_columns(name):
    _, r, c = _shard_shape(name)
    return r if name in COLUMN_SHARDED else c


def _piece_rows(piece, halves=1):
    name, layer = piece
    rows = _shard_rows(name) * halves
    return rows if layer is None else rows // SHARDED[name][0][0]


def _aligned(rows, to=PIECE_ALIGN):
    return -(-rows // to) * to


def _pack_pieces(arrays, total_align=PIECE_ALIGN):
    padded, total = [], 0
    for a in arrays:
        rows = a.shape[-2]
        extra = _aligned(rows) - rows
        if extra:
            a = jnp.pad(a, [(0, 0)] * (a.ndim - 2) + [(0, extra), (0, 0)])
        padded.append(a)
        total += rows + extra
    tail = _aligned(total, total_align) - total
    if tail:
        padded.append(jnp.zeros(padded[0].shape[:-2] + (tail, PACK_WIDTH), padded[0].dtype))
    return jnp.concatenate(padded, axis=-2)


def _piece_offsets(pieces, halves=None):
    out, at = [], 0
    for p in pieces:
        rows = _piece_rows(p, (halves or {}).get(p[0], 1))
        out.append((p, at, rows))
        at += _aligned(rows)
    return out


def _shard_piece_rows(src, piece):
    name, layer = piece
    part = src[name] if layer is None else src[name][layer:layer + 1]
    return _to_rows(part.astype(F32), name)


def _piece_from_rows(rows, piece):
    name, layer = piece
    layers, r, c = _shard_shape(name)
    n_l = layers if layer is None else 1
    if name in COLUMN_SHARDED:
        return jnp.swapaxes(rows.reshape(n_l, c, r), 1, 2)
    return rows.reshape(n_l, r, c)


SMALL_TAIL = tuple(n for n in REPLICATED if n not in ("norm_mix", "norm_ffn"))


def _pack_small(vals):
    tail, at = jnp.zeros((PACK_WIDTH,), F32), 0
    for n in SMALL_TAIL:
        vec = vals[n].astype(F32).reshape(-1)
        tail = tail + jnp.pad(vec, (at, PACK_WIDTH - at - vec.shape[0]))
        at += vec.shape[0]
    buf = jnp.pad(vals["norm_mix"].astype(F32), ((0, SMALL_ROWS - DEPTH), (0, 0)))
    buf = buf + jnp.pad(vals["norm_ffn"].astype(F32), ((8, SMALL_ROWS - 8 - DEPTH), (0, 0)))
    return buf + jnp.pad(tail.reshape(1, PACK_WIDTH), ((SMALL_ROWS - 1, 0), (0, 0)))


def _unpack_small(buf):
    out = {"norm_mix": buf[0:DEPTH], "norm_ffn": buf[8:8 + DEPTH]}
    at = 0
    for n in SMALL_TAIL:
        size = math.prod(REPLICATED[n])
        out[n] = buf[SMALL_ROWS - 1, at:at + size].reshape(REPLICATED[n])
        at += size
    return out


GATHER_FIRST = (("gdn_w_in", 0), ("gdn_conv_w", None), ("gdn_w_out", 0))
GATHER_NEXT = (("ffn_w_in", 0), ("ffn_w_out", 0), ("dil_w_in", 0), ("dil_w_out", 0))
GATHER_LAST = (("ffn_w_in", 1), ("ffn_w_out", 1), ("gdn_w_in", 1), ("gdn_w_out", 1), ("ffn_w_in", 2), ("ffn_w_out", 2),
               ("dil_w_in", 1), ("dil_w_out", 1), ("ffn_w_in", 3), ("ffn_w_out", 3))
EXCHANGE_GROUPS = (
    (("ffn_w_in", 3), ("ffn_w_out", 3), ("dil_w_in", 1), ("dil_w_out", 1),
     ("ffn_w_in", 2), ("ffn_w_out", 2), ("gdn_w_in", 1), ("gdn_w_out", 1)),
    (("ffn_w_in", 1), ("ffn_w_out", 1), ("dil_w_in", 0), ("dil_w_out", 0)),
    (("ffn_w_in", 0), ("ffn_w_out", 0)),
    (("gdn_w_in", 0), ("gdn_w_out", 0), ("gdn_conv_w", None)),
)
EXCHANGE_AFTER = {("mix", 2): 0, ("mix", 1): 1, ("ffn", 0): 2}


def _gather_operand(w, pieces):
    arrays = []
    for n, layer in pieces:
        if layer is None:
            arrays.append(lax.bitcast_convert_type(w[n], BF16).reshape(-1, PACK_WIDTH))
        else:
            arrays.append(_to_rows(w[n][layer:layer + 1].astype(BF16), n))
    return _pack_pieces(arrays)


def _gathered_weights(gathered, pieces, full):
    for (n, layer), at, rows in _piece_offsets(pieces, halves={"gdn_conv_w": 2}):
        block = gathered[:, at:at + rows]
        if layer is None:
            block = lax.bitcast_convert_type(block.reshape((N_DEV,) + _shard_shape(n) + (2,)), F32)
            full[n] = _join_shards(block, n)
        else:
            full.setdefault(n, {})[layer] = block.reshape(-1, _layer_columns(n))
    return full


def _exchange_operand(grads, pieces):
    arrays = []
    for n, layer in pieces:
        if layer is None:
            arrays.append(_split_shards(grads[n], n).astype(BF16).reshape(N_DEV, -1, PACK_WIDTH))
        else:
            arrays.append(grads[n][layer].astype(BF16).reshape(N_DEV, -1, PACK_WIDTH))
    return _pack_pieces(arrays, total_align=PACK_ROW_ALIGN)


def _update_group(received, pieces, w, m, v, *, name):
    packed = [_pack_pieces([_shard_piece_rows(src, p) for p in pieces], total_align=PACK_ROW_ALIGN) for src in (w, m, v)]
    outs = _adamw(received, *packed, name=name)
    return {p: tuple(_piece_from_rows(o[at:at + rows], p) for o in outs) for p, at, rows in _piece_offsets(pieces)}


def kernel(x, norm_mix, norm_ffn, gdn_w_in, gdn_conv_w, gdn_a_log, gdn_dt_bias, gdn_norm_w, gdn_w_out, dil_w_in, dil_q_norm, dil_k_norm, dil_w_out, ffn_w_in, ffn_w_out, loss_target, m_norm_mix, m_norm_ffn, m_gdn_w_in, m_gdn_conv_w, m_gdn_a_log, m_gdn_dt_bias, m_gdn_norm_w, m_gdn_w_out, m_dil_w_in, m_dil_q_norm, m_dil_k_norm, m_dil_w_out, m_ffn_w_in, m_ffn_w_out, v_norm_mix, v_norm_ffn, v_gdn_w_in, v_gdn_conv_w, v_gdn_a_log, v_gdn_dt_bias, v_gdn_norm_w, v_gdn_w_out, v_dil_w_in, v_dil_q_norm, v_dil_k_norm, v_dil_w_out, v_ffn_w_in, v_ffn_w_out):
    w = dict(norm_mix=norm_mix, norm_ffn=norm_ffn, gdn_w_in=gdn_w_in, gdn_conv_w=gdn_conv_w, gdn_a_log=gdn_a_log,
             gdn_dt_bias=gdn_dt_bias, gdn_norm_w=gdn_norm_w, gdn_w_out=gdn_w_out, dil_w_in=dil_w_in, dil_q_norm=dil_q_norm,
             dil_k_norm=dil_k_norm, dil_w_out=dil_w_out, ffn_w_in=ffn_w_in, ffn_w_out=ffn_w_out)
    m = dict(norm_mix=m_norm_mix, norm_ffn=m_norm_ffn, gdn_w_in=m_gdn_w_in, gdn_conv_w=m_gdn_conv_w, gdn_a_log=m_gdn_a_log,
             gdn_dt_bias=m_gdn_dt_bias, gdn_norm_w=m_gdn_norm_w, gdn_w_out=m_gdn_w_out, dil_w_in=m_dil_w_in,
             dil_q_norm=m_dil_q_norm, dil_k_norm=m_dil_k_norm, dil_w_out=m_dil_w_out, ffn_w_in=m_ffn_w_in, ffn_w_out=m_ffn_w_out)
    v = dict(norm_mix=v_norm_mix, norm_ffn=v_norm_ffn, gdn_w_in=v_gdn_w_in, gdn_conv_w=v_gdn_conv_w, gdn_a_log=v_gdn_a_log,
             gdn_dt_bias=v_gdn_dt_bias, gdn_norm_w=v_gdn_norm_w, gdn_w_out=v_gdn_w_out, dil_w_in=v_dil_w_in,
             dil_q_norm=v_dil_q_norm, dil_k_norm=v_dil_k_norm, dil_w_out=v_dil_w_out, ffn_w_in=v_ffn_w_in, ffn_w_out=v_ffn_w_out)
    def row(src, i):
        return src[i].reshape(1, D_MODEL)

    first = _all_gather(_gather_operand(w, GATHER_FIRST), name="weight_all_gather_first")
    next_started = _travel_start(_gather_operand(w, GATHER_NEXT), first, per_peer=False, name="weight_gather_start_next")
    last_started = _travel_start(_gather_operand(w, GATHER_LAST), next_started[4], per_peer=False,
                                 name="weight_gather_start_last")
    full = _gathered_weights(first, GATHER_FIRST, {n: w[n] for n in REPLICATED})
    prepared = dict(gdn=_prepare_gdn(full, layers=(0,)))
    h = x[0]
    saved = [None] * DEPTH
    h, s_mix, hn = _mixer_fwd(0, h, row(norm_mix, 0) + last_started[4][0, 0], prepared, None, row(norm_ffn, 0))
    got = _travel_wait(next_started, h, per_peer=False, name="weight_gather_wait_next")
    full = _gathered_weights(got, GATHER_NEXT, full)
    prepared.update(dil=_prepare_dil(full, layers=(0,)), ffn=_prepare_ffn(full, layers=(0,)))
    for i in range(DEPTH):
        if i > 0:
            h, s_mix, hn = _mixer_fwd(i, h, row(norm_mix, i), prepared, hn, row(norm_ffn, i))
        if i == 1:
            got = _travel_wait(last_started, h, per_peer=False, name="weight_gather_wait_last")
            full = _gathered_weights(got, GATHER_LAST, full)
            prepared["gdn"].update(_prepare_gdn(full, layers=(1,)))
            prepared["dil"].update(_prepare_dil(full, layers=(1,)))
            prepared["ffn"].update(_prepare_ffn(full, layers=(1, 2, 3)))
        h, s_ffn, hn = _ffn_layer_fwd(h, row(norm_ffn, i), prepared["ffn"][i], hn,
                                      row(norm_mix, i + 1) if i + 1 < DEPTH else None)
        saved[i] = (s_mix, s_ffn)
    dx, dxb, loss = _loss_head(h, loss_target[0], name="loss_head")

    g_mix, g_ffn = [None] * DEPTH, [None] * DEPTH
    started = {}

    def travel(group):
        operand = _exchange_operand(_collect_grads(g_mix, g_ffn), EXCHANGE_GROUPS[group])
        started[group] = _travel_start(operand, dx, per_peer=True, name=f"grad_exchange_start_{group}")
        return started[group][4][0, 0]

    zero = 0.0
    for i in reversed(range(DEPTH)):
        s_mix, s_ffn = saved[i]
        dx, dxb, g_ffn[i] = _ffn_layer_bwd(dx, dxb, row(norm_ffn, i) + zero, prepared["ffn"][i], s_ffn)
        zero = travel(EXCHANGE_AFTER[("ffn", i)]) if ("ffn", i) in EXCHANGE_AFTER else 0.0
        dx, dxb, g_mix[i] = _mixer_bwd(i, dx, dxb, row(norm_mix, i), prepared, s_mix, zero)
        zero = travel(EXCHANGE_AFTER[("mix", i)]) if ("mix", i) in EXCHANGE_AFTER else 0.0
    grads = _collect_grads(g_mix, g_ffn)
    received = [_travel_wait(started[g], dx, per_peer=True, name=f"grad_exchange_wait_{g}") for g in sorted(started)]
    received.append(_exchange(_exchange_operand(grads, EXCHANGE_GROUPS[-1]), name="grad_exchange_last"))
    updated = {}
    for g, pieces in enumerate(EXCHANGE_GROUPS):
        updated.update(_update_group(received[g], pieces, w, m, v, name=f"adamw_sharded_{g}"))

    small_parts = _all_gather(_pack_small(grads), name="small_grad_all_gather")
    outs_small = [_unpack_small(o) for o in
                  _adamw(small_parts, _pack_small(w), _pack_small(m), _pack_small(v), name="adamw_replicated")]

    total_loss = lax.psum(loss[0, 0], ("x", "y", "c"))
    result = [total_loss, dx[None]]
    for k in range(4):
        for n in WEIGHT_ORDER:
            if n not in SHARDED:
                result.append(outs_small[k][n])
            elif (n, None) in updated:
                result.append(updated[(n, None)][k])
            else:
                result.append(jnp.concatenate([updated[(n, l)][k] for l in range(SHARDED[n][0][0])], axis=0))
    return tuple(result)
```
